```python
import jax, jax.numpy as jnp
from jax import lax
import numpy as np

D_MODEL = 1024
BATCH = 8
SEQ = 2048
DEPTH = 4

CTX_LEN = 256
GRID_W = 64
N_MIXERS = 3
WIDTH = D_MODEL
POOL_WINDOWS = (2, 4, 8, 16)
N_POOL_GROUPS = 4
POOL_GROUP = WIDTH // N_POOL_GROUPS
HEAD_DIM = 64
N_HEADS = WIDTH // HEAD_DIM
WIN_ROWS_MAX = 8
WIN_COLS = 16
CONV_WIDTH = 3
EPS = 1e-6
N_POOL_LAYERS = (DEPTH + 2) // 3
N_NA_LAYERS = (DEPTH + 1) // 3
N_CONV_LAYERS = DEPTH // 3

kernel_name = "hybrid_pool_natten_shortconv_dit"


def _rmsnorm(x, g):
    xf = x.astype(jnp.float32)
    y = xf * lax.rsqrt(jnp.mean(xf * xf, axis=-1, keepdims=True) + EPS)
    return (y * g.astype(jnp.float32)).astype(x.dtype)


def _modulation(cond, w, b):
    m = jax.nn.silu(cond) @ w + b
    return jnp.split(m, 3, axis=-1)


def _centred_mean(u, w):
    b_, l_, c_ = u.shape
    cs = jnp.concatenate([jnp.zeros((b_, 1, c_), jnp.float32),
                          jnp.cumsum(u.astype(jnp.float32), axis=1)], axis=1)
    t = jnp.arange(l_)
    lo = jnp.clip(t - w // 2, 0, l_)
    hi = jnp.clip(t + w // 2, 0, l_)
    cnt = (hi - lo).astype(jnp.float32)
    return ((cs[:, hi] - cs[:, lo]) / cnt[None, :, None]).astype(u.dtype)


def _pool_mixer(h, w_in, w_grp, scale, w_out):
    u, g = jnp.split(h @ w_in, 2, axis=-1)
    b_, l_, _ = u.shape
    ug = u.reshape(b_, l_, N_POOL_GROUPS, POOL_GROUP)
    pooled = jnp.stack([_centred_mean(ug[:, :, i], w) for i, w in enumerate(POOL_WINDOWS)], axis=2)
    mixed = jnp.einsum('blgc,gcd->blgd', pooled - ug, w_grp).reshape(b_, l_, WIDTH)
    return (mixed * scale * jax.nn.silu(g)) @ w_out


def _dwconv3(y, w, b):
    out = lax.conv_general_dilated(y, w[:, None, :], window_strides=(1,), padding=((1, 1),),
                                   dimension_numbers=('NWC', 'WIO', 'NWC'),
                                   feature_group_count=WIDTH)
    return out + b


def _conv_mixer(h, w_in, conv_w, conv_b, w_out):
    bg, cg, v, g = jnp.split(h @ w_in, 4, axis=-1)
    y = bg * _dwconv3(cg * v, conv_w, conv_b)
    return (y * jax.nn.silu(g)) @ w_out


def _na_mixer(h, hc, w_in, rpb, w_out, need_ctx_out):
    b_, l_, _ = h.shape
    rows = l_ // GRID_W
    wr = min(WIN_ROWS_MAX, rows)
    q, k, v, g = jnp.split(h @ w_in, 4, axis=-1)
    q = q.reshape(b_, rows, GRID_W, N_HEADS, HEAD_DIM) * HEAD_DIM ** -0.5
    k = k.reshape(b_, rows, GRID_W, N_HEADS, HEAD_DIM)
    v = v.reshape(b_, rows, GRID_W, N_HEADS, HEAD_DIM)
    if need_ctx_out:
        qc, kc, vc, gc = jnp.split(hc @ w_in, 4, axis=-1)
    else:
        kc, vc = jnp.split(hc @ w_in[:, WIDTH:3 * WIDTH], 2, axis=-1)
    n_ctx = hc.shape[1]
    kc = kc.reshape(b_, n_ctx, N_HEADS, HEAD_DIM)
    vc = vc.reshape(b_, n_ctx, N_HEADS, HEAD_DIM)

    r = jnp.arange(rows)
    row_idx = jnp.clip(r - wr // 2, 0, rows - wr)[:, None] + jnp.arange(wr)[None, :]
    col = jnp.arange(GRID_W)
    col_idx = jnp.clip(col - WIN_COLS // 2, 0, GRID_W - WIN_COLS)[:, None] + jnp.arange(WIN_COLS)[None, :]
    sel = jax.nn.one_hot(col_idx, GRID_W, dtype=h.dtype)

    kb = k[:, row_idx]
    vb = v[:, row_idx]
    s_blk = jnp.einsum('brqhd,brikhd->bhrqik', q, kb)
    s_loc = jnp.einsum('bhrqik,qjk->bhrqij', s_blk, sel).astype(jnp.float32)
    dr_idx = row_idx - r[:, None] + WIN_ROWS_MAX - 1
    dc_idx = col_idx - col[:, None] + WIN_COLS - 1
    bias = rpb[:, dr_idx[:, None, :, None], dc_idx[None, :, None, :]].astype(jnp.float32)
    s_loc = s_loc + bias[None]
    s_ctx = jnp.einsum('brqhd,bchd->bhrqc', q, kc).astype(jnp.float32)
    n_loc = wr * WIN_COLS
    logits = jnp.concatenate([s_loc.reshape(b_, N_HEADS, rows, GRID_W, n_loc), s_ctx], axis=-1)
    p = jax.nn.softmax(logits, axis=-1).astype(v.dtype)
    p_loc = p[..., :n_loc].reshape(b_, N_HEADS, rows, GRID_W, wr, WIN_COLS)
    p_ctx = p[..., n_loc:]
    p_blk = jnp.einsum('bhrqij,qjk->bhrqik', p_loc, sel)
    o = jnp.einsum('bhrqik,brikhd->brqhd', p_blk, vb) + jnp.einsum('bhrqc,bchd->brqhd', p_ctx, vc)
    y = (o.reshape(b_, l_, WIDTH) * jax.nn.silu(g)) @ w_out

    yc = None
    if need_ctx_out:
        qc = qc.reshape(b_, n_ctx, N_HEADS, HEAD_DIM) * HEAD_DIM ** -0.5
        sc = jnp.einsum('bqhd,bkhd->bhqk', qc, kc).astype(jnp.float32)
        pc = jax.nn.softmax(sc, axis=-1).astype(vc.dtype)
        oc = jnp.einsum('bhqk,bkhd->bqhd', pc, vc).reshape(b_, n_ctx, WIDTH)
        yc = (oc * jax.nn.silu(gc)) @ w_out
    return y, yc


def _fwd_setup_inputs(seed: int = 0) -> dict:
    key = jax.random.key(seed)
    ks = jax.random.split(key, 20)
    nrm = jax.random.normal
    d, w = D_MODEL, WIDTH
    return {
        "x": nrm(ks[0], (BATCH, SEQ, d), jnp.float32),
        "c": nrm(ks[1], (BATCH, d), jnp.float32),
        "ctx": nrm(ks[2], (BATCH, CTX_LEN, d), jnp.float32),
        "c_ctx": nrm(ks[3], (d,), jnp.float32),
        "norm_g": 1.0 + 0.02 * nrm(ks[4], (DEPTH, d), jnp.float32),
        "ada_w": 0.5 * d ** -0.5 * nrm(ks[5], (DEPTH, d, 3 * d), jnp.float32),
        "ada_b": 0.01 * nrm(ks[6], (DEPTH, 3 * d), jnp.float32),
        "pool_w_in": d ** -0.5 * nrm(ks[7], (N_POOL_LAYERS, d, 2 * w), jnp.float32),
        "pool_w_grp": POOL_GROUP ** -0.5 * nrm(ks[8], (N_POOL_LAYERS, N_POOL_GROUPS, POOL_GROUP, POOL_GROUP), jnp.float32),
        "pool_scale": 1.0 + 0.1 * nrm(ks[9], (N_POOL_LAYERS, w), jnp.float32),
        "pool_w_out": w ** -0.5 * nrm(ks[10], (N_POOL_LAYERS, w, d), jnp.float32),
        "na_w_in": d ** -0.5 * nrm(ks[11], (N_NA_LAYERS, d, 4 * w), jnp.float32),
        "na_rpb": 0.1 * nrm(ks[12], (N_NA_LAYERS, N_HEADS, 2 * WIN_ROWS_MAX - 1, 2 * WIN_COLS - 1), jnp.float32),
        "na_w_out": w ** -0.5 * nrm(ks[13], (N_NA_LAYERS, w, d), jnp.float32),
        "conv_w_in": d ** -0.5 * nrm(ks[14], (N_CONV_LAYERS, d, 4 * w), jnp.float32),
        "conv_dw": CONV_WIDTH ** -0.5 * nrm(ks[15], (N_CONV_LAYERS, CONV_WIDTH, w), jnp.float32),
        "conv_db": 0.01 * nrm(ks[16], (N_CONV_LAYERS, w), jnp.float32),
        "conv_w_out": w ** -0.5 * nrm(ks[17], (N_CONV_LAYERS, w, d), jnp.float32),
        "final_g": 1.0 + 0.02 * nrm(ks[18], (d,), jnp.float32),
    }


def _fwd_reference(x, c, ctx, c_ctx, norm_g, ada_w, ada_b, pool_w_in, pool_w_grp, pool_scale, pool_w_out,
              na_w_in, na_rpb, na_w_out, conv_w_in, conv_dw, conv_db, conv_w_out, final_g):
    last_ctx_reader = max([i for i in range(DEPTH) if i % N_MIXERS == 1], default=-1)
    for i in range(DEPTH):
        kind, j = i % N_MIXERS, i // N_MIXERS
        update_ctx = i < last_ctx_reader
        shift, scale, gate = _modulation(c, ada_w[i], ada_b[i])
        hx = _rmsnorm(x, norm_g[i]) * (1.0 + scale[:, None]) + shift[:, None]
        if kind == 1 or update_ctx:
            cshift, cscale, cgate = _modulation(c_ctx, ada_w[i], ada_b[i])
            hc = _rmsnorm(ctx, norm_g[i]) * (1.0 + cscale) + cshift
        if kind == 0:
            yx = _pool_mixer(hx, pool_w_in[j], pool_w_grp[j], pool_scale[j], pool_w_out[j])
            if update_ctx:
                yc = _pool_mixer(hc, pool_w_in[j], pool_w_grp[j], pool_scale[j], pool_w_out[j])
        elif kind == 1:
            yx, yc = _na_mixer(hx, hc, na_w_in[j], na_rpb[j], na_w_out[j], update_ctx)
        else:
            yx = _conv_mixer(hx, conv_w_in[j], conv_dw[j], conv_db[j], conv_w_out[j])
            if update_ctx:
                yc = _conv_mixer(hc, conv_w_in[j], conv_dw[j], conv_db[j], conv_w_out[j])
        x = x + gate[:, None] * yx
        if update_ctx:
            ctx = ctx + cgate * yc
    return _rmsnorm(x, final_g)


import jax as _jax
import jax.numpy as _jnp

TWIN_FORMAT = 'train_step'
FWD_PARAMS = ['x', 'c', 'ctx', 'c_ctx', 'norm_g', 'ada_w', 'ada_b', 'pool_w_in', 'pool_w_grp', 'pool_scale', 'pool_w_out', 'na_w_in', 'na_rpb', 'na_w_out', 'conv_w_in', 'conv_dw', 'conv_db', 'conv_w_out', 'final_g']
TWIN_WEIGHTS = ['c_ctx', 'norm_g', 'ada_w', 'ada_b', 'pool_w_in', 'pool_w_grp', 'pool_scale', 'pool_w_out', 'na_w_in', 'na_rpb', 'na_w_out', 'conv_w_in', 'conv_dw', 'conv_db', 'conv_w_out', 'final_g']
TWIN_DIFF_INPUT = 'x'
TWIN_INPUTS = ['x', 'c', 'ctx', 'c_ctx', 'norm_g', 'ada_w', 'ada_b', 'pool_w_in', 'pool_w_grp', 'pool_scale', 'pool_w_out', 'na_w_in', 'na_rpb', 'na_w_out', 'conv_w_in', 'conv_dw', 'conv_db', 'conv_w_out', 'final_g', 'loss_target', 'm_c_ctx', 'm_norm_g', 'm_ada_w', 'm_ada_b', 'm_pool_w_in', 'm_pool_w_grp', 'm_pool_scale', 'm_pool_w_out', 'm_na_w_in', 'm_na_rpb', 'm_na_w_out', 'm_conv_w_in', 'm_conv_dw', 'm_conv_db', 'm_conv_w_out', 'm_final_g', 'v_c_ctx', 'v_norm_g', 'v_ada_w', 'v_ada_b', 'v_pool_w_in', 'v_pool_w_grp', 'v_pool_scale', 'v_pool_w_out', 'v_na_w_in', 'v_na_rpb', 'v_na_w_out', 'v_conv_w_in', 'v_conv_dw', 'v_conv_db', 'v_conv_w_out', 'v_final_g']
TWIN_OUTPUTS = ['loss', 'grad_x', 'grad_c_ctx', 'grad_norm_g', 'grad_ada_w', 'grad_ada_b', 'grad_pool_w_in', 'grad_pool_w_grp', 'grad_pool_scale', 'grad_pool_w_out', 'grad_na_w_in', 'grad_na_rpb', 'grad_na_w_out', 'grad_conv_w_in', 'grad_conv_dw', 'grad_conv_db', 'grad_conv_w_out', 'grad_final_g', 'delta_c_ctx', 'delta_norm_g', 'delta_ada_w', 'delta_ada_b', 'delta_pool_w_in', 'delta_pool_w_grp', 'delta_pool_scale', 'delta_pool_w_out', 'delta_na_w_in', 'delta_na_rpb', 'delta_na_w_out', 'delta_conv_w_in', 'delta_conv_dw', 'delta_conv_db', 'delta_conv_w_out', 'delta_final_g', 'new_m_c_ctx', 'new_m_norm_g', 'new_m_ada_w', 'new_m_ada_b', 'new_m_pool_w_in', 'new_m_pool_w_grp', 'new_m_pool_scale', 'new_m_pool_w_out', 'new_m_na_w_in', 'new_m_na_rpb', 'new_m_na_w_out', 'new_m_conv_w_in', 'new_m_conv_dw', 'new_m_conv_db', 'new_m_conv_w_out', 'new_m_final_g', 'new_v_c_ctx', 'new_v_norm_g', 'new_v_ada_w', 'new_v_ada_b', 'new_v_pool_w_in', 'new_v_pool_w_grp', 'new_v_pool_scale', 'new_v_pool_w_out', 'new_v_na_w_in', 'new_v_na_rpb', 'new_v_na_w_out', 'new_v_conv_w_in', 'new_v_conv_dw', 'new_v_conv_db', 'new_v_conv_w_out', 'new_v_final_g']
TWIN_LEAF_KINDS = {'loss': 'loss', 'grad_x': 'grad_x', 'grad_c_ctx': 'grad_w', 'grad_norm_g': 'grad_w', 'grad_ada_w': 'grad_w', 'grad_ada_b': 'grad_w', 'grad_pool_w_in': 'grad_w', 'grad_pool_w_grp': 'grad_w', 'grad_pool_scale': 'grad_w', 'grad_pool_w_out': 'grad_w', 'grad_na_w_in': 'grad_w', 'grad_na_rpb': 'grad_w', 'grad_na_w_out': 'grad_w', 'grad_conv_w_in': 'grad_w', 'grad_conv_dw': 'grad_w', 'grad_conv_db': 'grad_w', 'grad_conv_w_out': 'grad_w', 'grad_final_g': 'grad_w', 'delta_c_ctx': 'delta_w', 'delta_norm_g': 'delta_w', 'delta_ada_w': 'delta_w', 'delta_ada_b': 'delta_w', 'delta_pool_w_in': 'delta_w', 'delta_pool_w_grp': 'delta_w', 'delta_pool_scale': 'delta_w', 'delta_pool_w_out': 'delta_w', 'delta_na_w_in': 'delta_w', 'delta_na_rpb': 'delta_w', 'delta_na_w_out': 'delta_w', 'delta_conv_w_in': 'delta_w', 'delta_conv_dw': 'delta_w', 'delta_conv_db': 'delta_w', 'delta_conv_w_out': 'delta_w', 'delta_final_g': 'delta_w', 'new_m_c_ctx': 'new_m', 'new_m_norm_g': 'new_m', 'new_m_ada_w': 'new_m', 'new_m_ada_b': 'new_m', 'new_m_pool_w_in': 'new_m', 'new_m_pool_w_grp': 'new_m', 'new_m_pool_scale': 'new_m', 'new_m_pool_w_out': 'new_m', 'new_m_na_w_in': 'new_m', 'new_m_na_rpb': 'new_m', 'new_m_na_w_out': 'new_m', 'new_m_conv_w_in': 'new_m', 'new_m_conv_dw': 'new_m', 'new_m_conv_db': 'new_m', 'new_m_conv_w_out': 'new_m', 'new_m_final_g': 'new_m', 'new_v_c_ctx': 'new_v', 'new_v_norm_g': 'new_v', 'new_v_ada_w': 'new_v', 'new_v_ada_b': 'new_v', 'new_v_pool_w_in': 'new_v', 'new_v_pool_w_grp': 'new_v', 'new_v_pool_scale': 'new_v', 'new_v_pool_w_out': 'new_v', 'new_v_na_w_in': 'new_v', 'new_v_na_rpb': 'new_v', 'new_v_na_w_out': 'new_v', 'new_v_conv_w_in': 'new_v', 'new_v_conv_dw': 'new_v', 'new_v_conv_db': 'new_v', 'new_v_conv_w_out': 'new_v', 'new_v_final_g': 'new_v'}


def _forward(args):
    return _fwd_reference(*[args[k] for k in FWD_PARAMS])


def _output_shape():
    out = _jax.eval_shape(lambda: _forward(_fwd_setup_inputs(0)))
    return out.shape, out.dtype

N_MICROBATCH = 1
ADAM_LR = 0.001
ADAM_B1 = 0.9
ADAM_B2 = 0.999
ADAM_EPS = 1e-08
ADAM_WD = 0.01
ADAM_STEP = 10
PER_EXAMPLE_BATCH_AXIS = {'x': 0, 'c': 0, 'ctx': 0, 'loss_target': 0}
SHARED_INPUTS = []
_WEIGHT_DTYPES = {'c_ctx': _jnp.float32, 'norm_g': _jnp.float32, 'ada_w': _jnp.float32, 'ada_b': _jnp.float32, 'pool_w_in': _jnp.float32, 'pool_w_grp': _jnp.float32, 'pool_scale': _jnp.float32, 'pool_w_out': _jnp.float32, 'na_w_in': _jnp.float32, 'na_rpb': _jnp.float32, 'na_w_out': _jnp.float32, 'conv_w_in': _jnp.float32, 'conv_dw': _jnp.float32, 'conv_db': _jnp.float32, 'conv_w_out': _jnp.float32, 'final_g': _jnp.float32}
MOMENT_SCALE = {'c_ctx': 7.024127e-03, 'norm_g': 3.914135e-02, 'ada_w': 3.327138e-02, 'ada_b': 5.403601e-02, 'pool_w_in': 2.387291e-02, 'pool_w_grp': 2.349443e-02, 'pool_scale': 2.384654e-02, 'pool_w_out': 2.350001e-02, 'na_w_in': 6.168462e-03, 'na_rpb': 6.835083e-04, 'na_w_out': 7.789122e-03, 'conv_w_in': 3.206374e-02, 'conv_dw': 3.118468e-02, 'conv_db': 2.775000e-02, 'conv_w_out': 3.177722e-02, 'final_g': 1.603069e+01}


def _to_microbatches(a, axis):
    t = _jnp.moveaxis(a, axis, 0)
    t = t.reshape((N_MICROBATCH, t.shape[0] // N_MICROBATCH) + t.shape[1:])
    return _jnp.moveaxis(t, 1, axis + 1)


def setup_inputs(seed: int = 0) -> dict:
    inp = _fwd_setup_inputs(seed)
    key = _jax.random.fold_in(_jax.random.key(seed), 7919)
    shape, _ = _output_shape()
    out = dict(inp)
    out["loss_target"] = _jax.random.normal(_jax.random.fold_in(key, 0), shape, _jnp.float32)
    for i, name in enumerate(TWIN_WEIGHTS):
        w = inp[name].astype(_jnp.float32)
        if MOMENT_SCALE is None:
            s = _jnp.sqrt(_jnp.mean(_jnp.square(w)) + 1e-30)
        else:
            s = MOMENT_SCALE[name]
        km, kv = _jax.random.split(_jax.random.fold_in(key, i + 1))
        out[name] = w
        out["m_" + name] = s * _jax.random.normal(km, w.shape, _jnp.float32)
        out["v_" + name] = (s * s) * _jax.random.uniform(kv, w.shape, _jnp.float32, 0.5, 1.5)
    if N_MICROBATCH > 1:
        for name, axis in PER_EXAMPLE_BATCH_AXIS.items():
            out[name] = _to_microbatches(out[name], axis)
    return {'x': out['x'], 'c': out['c'], 'ctx': out['ctx'], 'c_ctx': out['c_ctx'], 'norm_g': out['norm_g'], 'ada_w': out['ada_w'], 'ada_b': out['ada_b'], 'pool_w_in': out['pool_w_in'], 'pool_w_grp': out['pool_w_grp'], 'pool_scale': out['pool_scale'], 'pool_w_out': out['pool_w_out'], 'na_w_in': out['na_w_in'], 'na_rpb': out['na_rpb'], 'na_w_out': out['na_w_out'], 'conv_w_in': out['conv_w_in'], 'conv_dw': out['conv_dw'], 'conv_db': out['conv_db'], 'conv_w_out': out['conv_w_out'], 'final_g': out['final_g'], 'loss_target': out['loss_target'], 'm_c_ctx': out['m_c_ctx'], 'm_norm_g': out['m_norm_g'], 'm_ada_w': out['m_ada_w'], 'm_ada_b': out['m_ada_b'], 'm_pool_w_in': out['m_pool_w_in'], 'm_pool_w_grp': out['m_pool_w_grp'], 'm_pool_scale': out['m_pool_scale'], 'm_pool_w_out': out['m_pool_w_out'], 'm_na_w_in': out['m_na_w_in'], 'm_na_rpb': out['m_na_rpb'], 'm_na_w_out': out['m_na_w_out'], 'm_conv_w_in': out['m_conv_w_in'], 'm_conv_dw': out['m_conv_dw'], 'm_conv_db': out['m_conv_db'], 'm_conv_w_out': out['m_conv_w_out'], 'm_final_g': out['m_final_g'], 'v_c_ctx': out['v_c_ctx'], 'v_norm_g': out['v_norm_g'], 'v_ada_w': out['v_ada_w'], 'v_ada_b': out['v_ada_b'], 'v_pool_w_in': out['v_pool_w_in'], 'v_pool_w_grp': out['v_pool_w_grp'], 'v_pool_scale': out['v_pool_scale'], 'v_pool_w_out': out['v_pool_w_out'], 'v_na_w_in': out['v_na_w_in'], 'v_na_rpb': out['v_na_rpb'], 'v_na_w_out': out['v_na_w_out'], 'v_conv_w_in': out['v_conv_w_in'], 'v_conv_dw': out['v_conv_dw'], 'v_conv_db': out['v_conv_db'], 'v_conv_w_out': out['v_conv_w_out'], 'v_final_g': out['v_final_g']}


def _loss(weights, diff, rest, loss_target):
    with _jax.named_scope("forward"):
        args = {**rest, TWIN_DIFF_INPUT: diff, **{k: w.astype(_WEIGHT_DTYPES[k]) for k, w in weights.items()}}
        y = _forward(args)
    with _jax.named_scope("loss_head"):
        err = _jnp.square(y.astype(_jnp.float32) - loss_target)
        return 0.5 * _jnp.sum(_jnp.mean(err, axis=-1)) if err.ndim else 0.5 * err


def _adamw(w, g, m, v):
    m = ADAM_B1 * m + (1.0 - ADAM_B1) * g
    v = ADAM_B2 * v + (1.0 - ADAM_B2) * _jnp.square(g)
    m_hat = m / (1.0 - ADAM_B1 ** ADAM_STEP)
    v_hat = v / (1.0 - ADAM_B2 ** ADAM_STEP)
    delta = -ADAM_LR * (m_hat / (_jnp.sqrt(v_hat) + ADAM_EPS) + ADAM_WD * w)
    return delta, m, v


def reference(x, c, ctx, c_ctx, norm_g, ada_w, ada_b, pool_w_in, pool_w_grp, pool_scale, pool_w_out, na_w_in, na_rpb, na_w_out, conv_w_in, conv_dw, conv_db, conv_w_out, final_g, loss_target, m_c_ctx, m_norm_g, m_ada_w, m_ada_b, m_pool_w_in, m_pool_w_grp, m_pool_scale, m_pool_w_out, m_na_w_in, m_na_rpb, m_na_w_out, m_conv_w_in, m_conv_dw, m_conv_db, m_conv_w_out, m_final_g, v_c_ctx, v_norm_g, v_ada_w, v_ada_b, v_pool_w_in, v_pool_w_grp, v_pool_scale, v_pool_w_out, v_na_w_in, v_na_rpb, v_na_w_out, v_conv_w_in, v_conv_dw, v_conv_db, v_conv_w_out, v_final_g):
    given = dict(x=x, c=c, ctx=ctx, c_ctx=c_ctx, norm_g=norm_g, ada_w=ada_w, ada_b=ada_b, pool_w_in=pool_w_in, pool_w_grp=pool_w_grp, pool_scale=pool_scale, pool_w_out=pool_w_out, na_w_in=na_w_in, na_rpb=na_rpb, na_w_out=na_w_out, conv_w_in=conv_w_in, conv_dw=conv_dw, conv_db=conv_db, conv_w_out=conv_w_out, final_g=final_g, loss_target=loss_target, m_c_ctx=m_c_ctx, m_norm_g=m_norm_g, m_ada_w=m_ada_w, m_ada_b=m_ada_b, m_pool_w_in=m_pool_w_in, m_pool_w_grp=m_pool_w_grp, m_pool_scale=m_pool_scale, m_pool_w_out=m_pool_w_out, m_na_w_in=m_na_w_in, m_na_rpb=m_na_rpb, m_na_w_out=m_na_w_out, m_conv_w_in=m_conv_w_in, m_conv_dw=m_conv_dw, m_conv_db=m_conv_db, m_conv_w_out=m_conv_w_out, m_final_g=m_final_g, v_c_ctx=v_c_ctx, v_norm_g=v_norm_g, v_ada_w=v_ada_w, v_ada_b=v_ada_b, v_pool_w_in=v_pool_w_in, v_pool_w_grp=v_pool_w_grp, v_pool_scale=v_pool_scale, v_pool_w_out=v_pool_w_out, v_na_w_in=v_na_w_in, v_na_rpb=v_na_rpb, v_na_w_out=v_na_w_out, v_conv_w_in=v_conv_w_in, v_conv_dw=v_conv_dw, v_conv_db=v_conv_db, v_conv_w_out=v_conv_w_out, v_final_g=v_final_g)
    weights = {n: given[n] for n in TWIN_WEIGHTS}
    shared = {n: given[n] for n in SHARED_INPUTS}
    per_example = {n: given[n] for n in ['x', 'c', 'ctx']}
    grad_fn = _jax.value_and_grad(_loss, argnums=(0, 1))

    def one_microbatch(ex, loss_target):
        ex = dict(ex)
        diff = ex.pop(TWIN_DIFF_INPUT)
        return grad_fn(weights, diff, {**shared, **ex}, loss_target)

    if N_MICROBATCH == 1:
        loss, (grad_w, grad_x) = one_microbatch(per_example, given["loss_target"])
    else:
        def body(carry, xs):
            loss_sum, grad_sum = carry
            l_k, (gw_k, gx_k) = one_microbatch(xs[0], xs[1])
            with _jax.named_scope("update"):
                return (loss_sum + l_k, _jax.tree.map(_jnp.add, grad_sum, gw_k)), gx_k

        init = (_jnp.zeros((), _jnp.float32), _jax.tree.map(_jnp.zeros_like, weights))
        (loss, grad_w), grad_x = _jax.lax.scan(body, init, (per_example, given["loss_target"]))
    with _jax.named_scope("update"):
        delta_w, new_m, new_v = {}, {}, {}
        for n in TWIN_WEIGHTS:
            delta_w[n], new_m[n], new_v[n] = _adamw(weights[n], grad_w[n], given["m_" + n], given["v_" + n])
    return (loss, grad_x, *[grad_w[n] for n in TWIN_WEIGHTS], *[delta_w[n] for n in TWIN_WEIGHTS],
            *[new_m[n] for n in TWIN_WEIGHTS], *[new_v[n] for n in TWIN_WEIGHTS])
```

```python
import functools
import math

import numpy as np
import jax
import jax.numpy as jnp
from jax import lax
from jax.experimental import pallas as pl
from jax.experimental.pallas import tpu as pltpu

F32 = jnp.float32
BF16 = jnp.bfloat16
N_DEV = 8
LANES = 128
RMS_EPS = 1e-6
GRID_W = 64
WIN_ROWS = 8
WIN_COLS = 16
HEAD_DIM = 64
POOL_WINDOWS = (2, 4, 8, 16)
HALO = 8
CHUNK = 128
MASKED = -1e30
ADAM_LR = 0.001
ADAM_B1 = 0.9
ADAM_B2 = 0.999
ADAM_EPS = 1e-08
ADAM_WD = 0.01
ADAM_STEP = 10
VMEM_LIMIT = 56 * 1024 * 1024
MESH = pl.DeviceIdType.MESH
HIGHEST = lax.Precision.HIGHEST
ANY = pl.BlockSpec(memory_space=pl.ANY)


def _pc(body, *, name, **kw):
    return pl.pallas_call(body, name=name, **kw)


def _params(*sem):
    return pltpu.CompilerParams(dimension_semantics=sem if sem else None, vmem_limit_bytes=VMEM_LIMIT)


def _dot(a, b, ca=1, cb=0, precision=None):
    return lax.dot_general(a, b, (((ca,), (cb,)), ((), ())), preferred_element_type=F32, precision=precision)


def _tile(n, pref, unit=LANES):
    best = None
    for t in range(unit, min(n, pref) + 1, unit):
        if n % t == 0:
            best = t
    return best if best is not None else n


def _sigmoid(x):
    return 1.0 / (1.0 + jnp.exp(-x))


def _silu(x):
    return x * _sigmoid(x)


def _dsilu(x):
    s = _sigmoid(x)
    return s * (1.0 + x * (1.0 - s))


def _my_place():
    return lax.axis_index("x"), lax.axis_index("y"), lax.axis_index("c")


def _flip(v, f):
    return 1 - v if f else v


def _gather_small(block, name):
    rows, cols = block.shape

    def body(x_ref, out_ref, send_sems, recv_sems):
        x, y, c = _my_place()
        me = 4 * x + 2 * y + c
        out_ref[me] = x_ref[...]
        copies = []
        for k in range(1, N_DEV):
            peer = (_flip(x, k & 4), _flip(y, k & 2), _flip(c, k & 1))
            cp = pltpu.make_async_remote_copy(
                src_ref=x_ref, dst_ref=out_ref.at[me], send_sem=send_sems.at[k - 1], recv_sem=recv_sems.at[k - 1],
                device_id=peer, device_id_type=MESH)
            cp.start()
            copies.append(cp)
        for cp in copies:
            cp.wait()

    return _pc(
        body, name=name,
        out_shape=jax.ShapeDtypeStruct((N_DEV, rows, cols), block.dtype),
        in_specs=[pl.BlockSpec(memory_space=pltpu.VMEM)],
        out_specs=pl.BlockSpec(memory_space=pltpu.VMEM),
        scratch_shapes=[pltpu.SemaphoreType.DMA((N_DEV - 1,)), pltpu.SemaphoreType.DMA((N_DEV - 1,))],
    )(block)


def _allgather_big(tensors, name):
    nt = len(tensors)

    def body(*refs):
        ins, outs = refs[:nt], refs[nt:2 * nt]
        send_sems, recv_sems, local_sems = refs[2 * nt:]
        x, y, c = _my_place()
        sibling = (x, y, 1 - c)
        chips = [(1 - x, y), (x, 1 - y), (1 - x, 1 - y)]

        def idx(px, py, pc):
            return 4 * px + 2 * py + pc

        def copy(t, k, block, to, src=None):
            dst = outs[t].at[idx(*block)]
            return pltpu.make_async_remote_copy(
                src_ref=dst if src is None else src, dst_ref=dst,
                send_sem=send_sems.at[t, k], recv_sem=recv_sems.at[t, k], device_id=to, device_id_type=MESH)

        me = (x, y, c)
        mine, first, passed = [], [], []
        for t in range(nt):
            m = pltpu.make_async_copy(ins[t], outs[t].at[idx(*me)], local_sems.at[t])
            m.start()
            mine.append(m)
            f = [copy(t, 0, me, sibling, src=ins[t])]
            f += [copy(t, 1 + j, me, (*chip, c), src=ins[t]) for j, chip in enumerate(chips)]
            for cp in f:
                cp.start()
            first += f
        for t in range(nt):
            for j, chip in enumerate(chips):
                copy(t, 1 + j, (*chip, c), me).wait_recv()
                p = copy(t, 4 + j, (*chip, c), sibling)
                p.start()
                passed.append(p)
        for t in range(nt):
            copy(t, 0, sibling, me).wait_recv()
            for j, chip in enumerate(chips):
                copy(t, 4 + j, (*chip, 1 - c), me).wait_recv()
        for cp in first + passed:
            cp.wait_send()
        for m in mine:
            m.wait()

    return _pc(
        body, name=name,
        out_shape=[jax.ShapeDtypeStruct((N_DEV,) + t.shape, t.dtype) for t in tensors],
        in_specs=[ANY] * nt, out_specs=[ANY] * nt,
        scratch_shapes=[pltpu.SemaphoreType.DMA((nt, 7)), pltpu.SemaphoreType.DMA((nt, 7)),
                        pltpu.SemaphoreType.DMA((nt,))],
    )(*tensors)


def _rs_between_cores(grads, name):
    nt = len(grads)

    def body(*refs):
        ins, outs = refs[:nt], refs[nt:2 * nt]
        send_sems, recv_sems = refs[2 * nt:]
        x, y, c = _my_place()
        sibling = (x, y, 1 - c)
        for t in range(nt):
            for p in range(4):
                pltpu.make_async_remote_copy(
                    src_ref=ins[t].at[2 * p + 1 - c], dst_ref=outs[t].at[p],
                    send_sem=send_sems.at[t], recv_sem=recv_sems.at[t], device_id=sibling, device_id_type=MESH).start()
        for t in range(nt):
            pltpu.make_async_remote_copy(
                src_ref=outs[t], dst_ref=outs[t], send_sem=send_sems.at[t], recv_sem=recv_sems.at[t],
                device_id=sibling, device_id_type=MESH).wait()

    return _pc(
        body, name=name,
        out_shape=[jax.ShapeDtypeStruct((4,) + g.shape[1:], g.dtype) for g in grads],
        in_specs=[ANY] * nt, out_specs=[ANY] * nt,
        scratch_shapes=[pltpu.SemaphoreType.DMA((nt,)), pltpu.SemaphoreType.DMA((nt,))],
    )(*grads)


def _rs_chip_sum(g, r1, slots, name):
    _, rows, cols = g.shape
    tr = _tile(rows, max(8, 131072 // cols), 8)

    def body(slot_ref, *refs):
        gs, rs = refs[:4], refs[4:8]
        own_ref, send_ref = refs[8:]
        own_ref[...] = gs[0][...] + rs[0][...]
        for k in range(1, 4):
            send_ref[k - 1] = (gs[k][...] + rs[k][...]).astype(BF16)

    def g_spec(k):
        return pl.BlockSpec((None, tr, cols), lambda i, s: (2 * s[k] + s[4], i, 0))

    def r_spec(k):
        return pl.BlockSpec((None, tr, cols), lambda i, s: (s[k], i, 0))

    grid_spec = pltpu.PrefetchScalarGridSpec(
        num_scalar_prefetch=1, grid=(rows // tr,),
        in_specs=[g_spec(k) for k in range(4)] + [r_spec(k) for k in range(4)],
        out_specs=[pl.BlockSpec((tr, cols), lambda i, s: (i, 0)), pl.BlockSpec((3, tr, cols), lambda i, s: (0, i, 0))])
    return _pc(
        body, name=name, grid_spec=grid_spec,
        out_shape=[jax.ShapeDtypeStruct((rows, cols), F32), jax.ShapeDtypeStruct((3, rows, cols), BF16)],
        compiler_params=_params("parallel"),
    )(slots, g, g, g, g, r1, r1, r1, r1)


def _rs_between_chips(sends, name):
    nt = len(sends)

    def body(*refs):
        ins, outs = refs[:nt], refs[nt:2 * nt]
        send_sems, recv_sems = refs[2 * nt:]
        x, y, c = _my_place()
        peers = [(1 - x, y, c), (x, 1 - y, c), (1 - x, 1 - y, c)]
        copies = []
        for t in range(nt):
            for k, peer in enumerate(peers):
                cp = pltpu.make_async_remote_copy(
                    src_ref=ins[t].at[k], dst_ref=outs[t].at[k], send_sem=send_sems.at[t, k], recv_sem=recv_sems.at[t, k],
                    device_id=peer, device_id_type=MESH)
                cp.start()
                copies.append(cp)
        for cp in copies:
            cp.wait()

    return _pc(
        body, name=name,
        out_shape=[jax.ShapeDtypeStruct(s.shape, s.dtype) for s in sends],
        in_specs=[ANY] * nt, out_specs=[ANY] * nt,
        scratch_shapes=[pltpu.SemaphoreType.DMA((nt, 3)), pltpu.SemaphoreType.DMA((nt, 3))],
    )(*sends)


def _mod_fwd(cond, ada_w, bias):
    depth, d, nb = ada_w.shape

    def body(c_ref, w_ref, b_ref, o_ref):
        s = _silu(c_ref[...]).astype(BF16)
        o_ref[...] = _dot(s, w_ref[...].astype(BF16)) + b_ref[...]

    return _pc(
        body, name="mod_fwd", grid=(depth,),
        in_specs=[pl.BlockSpec((16, d), lambda i: (0, 0)), pl.BlockSpec((None, d, nb), lambda i: (i, 0, 0)),
                  pl.BlockSpec((None, 1, nb), lambda i: (i, 0, 0))],
        out_specs=pl.BlockSpec((None, 16, nb), lambda i: (i, 0, 0)),
        out_shape=jax.ShapeDtypeStruct((depth, 16, nb), F32),
        compiler_params=_params("parallel"),
    )(cond, ada_w, bias.reshape(depth, 1, nb))


def _mod_bwd(cond, ada_w, dm_all, dm_mine):
    depth, d, nb = ada_w.shape
    d3 = dm_all.shape[-1]

    def body(c_ref, w_ref, all_ref, call_ref, mine_ref, cmine_ref, gw_ref, gb_ref, part_ref, ds_ref):
        i = pl.program_id(0)
        cond_v = c_ref[...]
        s = _silu(cond_v).astype(BF16)
        has_ctx = jnp.where(i < 2, 1.0, 0.0)
        tot_all = jnp.sum(call_ref[...], axis=0, keepdims=True) * has_ctx
        tot_mine = jnp.broadcast_to(jnp.sum(cmine_ref[...], axis=0, keepdims=True) * has_ctx, (8, nb)).astype(BF16)
        gb_ref[...] = jnp.sum(all_ref[...], axis=0, keepdims=True) + tot_all
        gw_ref[...] = _dot(s[0:8], mine_ref[...].astype(BF16), 0, 0) + _dot(s[8:16], tot_mine, 0, 0)
        part = _dot(tot_mine, w_ref[...].astype(BF16), 1, 1)

        @pl.when(i == 0)
        def _():
            part_ref[...] = jnp.zeros_like(part_ref)
            ds_ref[...] = _dsilu(cond_v)

        part_ref[...] += part

    def rows(width, which):
        return pl.BlockSpec((None, N_DEV, width), which)

    layer = lambda i: (i, 0, 0)
    ctx_layer = lambda i: (jnp.minimum(i, 1) + 4, 0, 0)
    return _pc(
        body, name="mod_bwd", grid=(depth,),
        in_specs=[pl.BlockSpec((16, d), lambda i: (0, 0)), pl.BlockSpec((None, d, nb), layer),
                  rows(d3, layer), rows(d3, ctx_layer), rows(nb, layer), rows(nb, ctx_layer)],
        out_specs=[pl.BlockSpec((None, d, nb), layer), pl.BlockSpec((None, 1, d3), layer),
                   pl.BlockSpec((8, d), lambda i: (0, 0)), pl.BlockSpec((16, d), lambda i: (0, 0))],
        out_shape=[jax.ShapeDtypeStruct((depth, d, nb), F32), jax.ShapeDtypeStruct((depth, 1, d3), F32),
                   jax.ShapeDtypeStruct((8, d), F32), jax.ShapeDtypeStruct((16, d), F32)],
        compiler_params=_params("arbitrary"),
    )(cond, ada_w, dm_all, dm_all, dm_mine, dm_mine)


def _norm_fwd(xs, g, mod, tr, seg_tiles, name):
    t, d = xs.shape

    def body(x_ref, g_ref, mod_ref, h_ref):
        x = x_ref[...]
        r = lax.rsqrt(jnp.mean(x * x, axis=-1, keepdims=True) + RMS_EPS)
        y = (x * r) * g_ref[...]
        h_ref[...] = (y * (1.0 + mod_ref[1:2, :]) + mod_ref[0:1, :]).astype(BF16)

    return _pc(
        body, name=name, grid=(t // tr,),
        in_specs=[pl.BlockSpec((tr, d), lambda i: (i, 0)), pl.BlockSpec((1, d), lambda i: (0, 0)),
                  pl.BlockSpec((None, 8, d), lambda i: (i // seg_tiles, 0, 0))],
        out_specs=pl.BlockSpec((tr, d), lambda i: (i, 0)),
        out_shape=jax.ShapeDtypeStruct((t, d), BF16),
        compiler_params=_params("parallel"),
    )(xs, g, mod)


def _norm_bwd(xs, dh, dres, g, mod, tr, seg_tiles, name):
    t, d = xs.shape
    nseg = mod.shape[0]

    def body(x_ref, dh_ref, dres_ref, g_ref, mod_ref, dx_ref, sum_ref):
        i = pl.program_id(0)
        x = x_ref[...]
        r = lax.rsqrt(jnp.mean(x * x, axis=-1, keepdims=True) + RMS_EPS)
        xn = x * r
        dhv = dh_ref[...]
        gain = g_ref[...]
        one_scale = 1.0 + mod_ref[1:2, :]
        dxn = dhv * (gain * one_scale)
        dx_ref[...] = dres_ref[...] + r * (dxn - xn * jnp.mean(dxn * xn, axis=-1, keepdims=True))

        @pl.when(i % seg_tiles == 0)
        def _():
            sum_ref[...] = jnp.zeros_like(sum_ref)

        sum_ref[0:1, :] += jnp.sum(dhv, axis=0, keepdims=True)
        sum_ref[1:2, :] += jnp.sum(dhv * (xn * gain), axis=0, keepdims=True)
        sum_ref[2:3, :] += jnp.sum(dhv * one_scale * xn, axis=0, keepdims=True)

    row = pl.BlockSpec((tr, d), lambda i: (i, 0))
    seg = pl.BlockSpec((None, 8, d), lambda i: (i // seg_tiles, 0, 0))
    return _pc(
        body, name=name, grid=(t // tr,),
        in_specs=[row, row, row, pl.BlockSpec((1, d), lambda i: (0, 0)), seg],
        out_specs=[row, seg],
        out_shape=[jax.ShapeDtypeStruct((t, d), F32), jax.ShapeDtypeStruct((nseg, 8, d), F32)],
        compiler_params=_params("arbitrary"),
    )(xs, dh, dres, g, mod)


def _resid_bwd(dx, yx, mod, tr, seg_tiles, name):
    t, d = yx.shape
    nseg = mod.shape[0]

    def body(dx_ref, yx_ref, mod_ref, dyx_ref, sum_ref):
        i = pl.program_id(0)
        dxv = dx_ref[...]
        dyx_ref[...] = (dxv * mod_ref[2:3, :]).astype(BF16)

        @pl.when(i % seg_tiles == 0)
        def _():
            sum_ref[...] = jnp.zeros_like(sum_ref)

        sum_ref[0:1, :] += jnp.sum(dxv * yx_ref[...], axis=0, keepdims=True)

    row = pl.BlockSpec((tr, d), lambda i: (i, 0))
    seg = pl.BlockSpec((None, 8, d), lambda i: (i // seg_tiles, 0, 0))
    return _pc(
        body, name=name, grid=(t // tr,),
        in_specs=[row, row, seg], out_specs=[row, seg],
        out_shape=[jax.ShapeDtypeStruct((t, d), BF16), jax.ShapeDtypeStruct((nseg, 8, d), F32)],
        compiler_params=_params("arbitrary"),
    )(dx, yx, mod)


def _loss_head(xs, target, g, tr):
    t, d = xs.shape

    def body(x_ref, t_ref, g_ref, loss_ref, dx_ref, dg_ref):
        i = pl.program_id(0)
        x = x_ref[...]
        r = lax.rsqrt(jnp.mean(x * x, axis=-1, keepdims=True) + RMS_EPS)
        xn = x * r
        gain = g_ref[...]
        err = xn * gain - t_ref[...]
        dy = err * (1.0 / d)
        dxn = dy * gain
        dx_ref[...] = r * (dxn - xn * jnp.mean(dxn * xn, axis=-1, keepdims=True))

        @pl.when(i == 0)
        def _():
            loss_ref[...] = jnp.zeros_like(loss_ref)
            dg_ref[...] = jnp.zeros_like(dg_ref)

        loss_ref[...] += 0.5 * jnp.sum(jnp.mean(err * err, axis=-1, keepdims=True))
        dg_ref[0:1, :] += jnp.sum(dy * xn, axis=0, keepdims=True)

    row = pl.BlockSpec((tr, d), lambda i: (i, 0))
    return _pc(
        body, name="loss_head", grid=(t // tr,),
        in_specs=[row, row, pl.BlockSpec((1, d), lambda i: (0, 0))],
        out_specs=[pl.BlockSpec((8, LANES), lambda i: (0, 0)), row, pl.BlockSpec((8, d), lambda i: (0, 0))],
        out_shape=[jax.ShapeDtypeStruct((8, LANES), F32), jax.ShapeDtypeStruct((t, d), F32),
                   jax.ShapeDtypeStruct((8, d), F32)],
        compiler_params=_params("arbitrary"),
    )(xs, target, g)


def _proj_in(h, w, layer, parts, name):
    t, d = h.shape
    n8 = w.shape[-1]
    width = N_DEV * n8 // parts
    per_part = width // n8
    tm = _tile(t, 1152)

    def body(a_ref, b_ref, o_ref):
        o_ref[...] = _dot(a_ref[...], b_ref[...])

    return _pc(
        body, name=name, grid=(t // tm, N_DEV),
        in_specs=[pl.BlockSpec((tm, d), lambda i, j: (i, 0)),
                  pl.BlockSpec((None, None, d, n8), lambda i, j: (j, layer, 0, 0))],
        out_specs=pl.BlockSpec((None, tm, n8), lambda i, j: (j // per_part, i, j % per_part)),
        out_shape=jax.ShapeDtypeStruct((parts, t, width), F32),
        compiler_params=_params("parallel", "parallel"),
    )(h, w)


def _proj_out(z, w, res, mod, tm, seg_tiles, name):
    t, k = z.shape
    d = w.shape[1]
    tn = _tile(d, 512)

    def body(z_ref, w_ref, res_ref, mod_ref, yx_ref, x_ref):
        yx = _dot(z_ref[...], w_ref[...])
        yx_ref[...] = yx
        x_ref[...] = res_ref[...] + mod_ref[2:3, :] * yx

    tile = pl.BlockSpec((tm, tn), lambda i, j: (i, j))
    return _pc(
        body, name=name, grid=(t // tm, d // tn),
        in_specs=[pl.BlockSpec((tm, k), lambda i, j: (i, 0)), pl.BlockSpec((k, tn), lambda i, j: (0, j)), tile,
                  pl.BlockSpec((None, 8, tn), lambda i, j: (i // seg_tiles, 0, j))],
        out_specs=[tile, tile],
        out_shape=[jax.ShapeDtypeStruct((t, d), F32), jax.ShapeDtypeStruct((t, d), F32)],
        compiler_params=_params("parallel", "parallel"),
    )(z, w, res, mod)


def _proj_out_dz(dyx, w, name):
    t, d = dyx.shape
    width = w.shape[0]
    tm, tn = _tile(t, 1024), _tile(width, 512)

    def body(a_ref, w_ref, o_ref):
        o_ref[...] = _dot(a_ref[...], w_ref[...], 1, 1)

    return _pc(
        body, name=name, grid=(t // tm, width // tn),
        in_specs=[pl.BlockSpec((tm, d), lambda i, j: (i, 0)), pl.BlockSpec((tn, d), lambda i, j: (j, 0))],
        out_specs=pl.BlockSpec((tm, tn), lambda i, j: (i, j)),
        out_shape=jax.ShapeDtypeStruct((t, width), F32),
        compiler_params=_params("parallel", "parallel"),
    )(dyx, w)


def _proj_in_dh(dpre, w, layer, name):
    parts, t, width = dpre.shape
    d, n8 = w.shape[-2:]
    per_part = width // n8
    tm, tn = _tile(t, 1152), _tile(d, 512)

    def body(a_ref, w_ref, o_ref, acc_ref):
        k = pl.program_id(2)

        @pl.when(k == 0)
        def _():
            acc_ref[...] = jnp.zeros_like(acc_ref)

        acc_ref[...] += _dot(a_ref[...], w_ref[...], 1, 1)

        @pl.when(k == N_DEV - 1)
        def _():
            o_ref[...] = acc_ref[...]

    return _pc(
        body, name=name, grid=(t // tm, d // tn, N_DEV),
        in_specs=[pl.BlockSpec((None, tm, n8), lambda i, j, k: (k // per_part, i, k % per_part)),
                  pl.BlockSpec((None, None, tn, n8), lambda i, j, k: (k, layer, j, 0))],
        out_specs=pl.BlockSpec((tm, tn), lambda i, j, k: (i, j)),
        out_shape=jax.ShapeDtypeStruct((t, d), F32),
        scratch_shapes=[pltpu.VMEM((tm, tn), F32)],
        compiler_params=_params("parallel", "parallel", "arbitrary"),
    )(dpre, w)


def _grad_w_in(h, dpre, n8, name):
    t, d = h.shape
    parts, _, width = dpre.shape
    per_part = width // n8
    tm, tk = _tile(d, 512), _tile(t, 1152)
    nk = t // tk

    def body(a_ref, b_ref, o_ref, acc_ref):
        k = pl.program_id(2)

        @pl.when(k == 0)
        def _():
            acc_ref[...] = jnp.zeros_like(acc_ref)

        acc_ref[...] += _dot(a_ref[...], b_ref[...], 0, 0)

        @pl.when(k == nk - 1)
        def _():
            o_ref[...] = acc_ref[...]

    return _pc(
        body, name=name, grid=(d // tm, N_DEV, nk),
        in_specs=[pl.BlockSpec((tk, tm), lambda i, j, k: (k, i)),
                  pl.BlockSpec((None, tk, n8), lambda i, j, k: (j // per_part, k, j % per_part))],
        out_specs=pl.BlockSpec((None, tm, n8), lambda i, j, k: (j, i, 0)),
        out_shape=jax.ShapeDtypeStruct((N_DEV, d, n8), F32),
        scratch_shapes=[pltpu.VMEM((tm, n8), F32)],
        compiler_params=_params("parallel", "parallel", "arbitrary"),
    )(h, dpre)


def _grad_w_out(z, dyx, name):
    width = z.shape[1]
    t, d = dyx.shape
    tm, tn, tk = _tile(width, 512), _tile(d, 512), _tile(t, 1152)
    nk = t // tk

    def body(a_ref, b_ref, o_ref, acc_ref):
        k = pl.program_id(2)

        @pl.when(k == 0)
        def _():
            acc_ref[...] = jnp.zeros_like(acc_ref)

        acc_ref[...] += _dot(a_ref[...], b_ref[...], 0, 0)

        @pl.when(k == nk - 1)
        def _():
            o_ref[...] = acc_ref[...]

    return _pc(
        body, name=name, grid=(width // tm, d // tn, nk),
        in_specs=[pl.BlockSpec((tk, tm), lambda i, j, k: (k, i)), pl.BlockSpec((tk, tn), lambda i, j, k: (k, j))],
        out_specs=pl.BlockSpec((tm, tn), lambda i, j, k: (i, j)),
        out_shape=jax.ShapeDtypeStruct((width, d), F32),
        scratch_shapes=[pltpu.VMEM((tm, tn), F32)],
        compiler_params=_params("parallel", "parallel", "arbitrary"),
    )(z, dyx)


def _shift(v, k):
    n = v.shape[0]
    return pltpu.roll(v, k % n, 0)


def _window_sum(v, win):
    s = v + _shift(v, 1)
    step = 1
    while 2 * step < win:
        s = _shift(s, step) + _shift(s, -step)
        step *= 2
    return s


def _window_count(base, seg_len, win, shape):
    t = base + lax.broadcasted_iota(jnp.int32, shape, 0)
    hi = jnp.minimum(t + win // 2, seg_len)
    lo = jnp.maximum(t - win // 2, 0)
    return (hi - lo).astype(F32)


def _pad_offsets(segs):
    return [HALO * (s + 1) + st for s, (st, _) in enumerate(segs)]


def _for_chunks(segs, fn):
    offs = _pad_offsets(segs)
    for s, (st, ln) in enumerate(segs):
        def step(ci, carry, s=s, st=st, ln=ln):
            fn(s, st, ln, offs[s], pl.multiple_of(ci * CHUNK, CHUNK))
            return carry
        lax.fori_loop(0, ln // CHUNK, step, 0)


def _pool_fwd(pre, w_grp, scale, segs, name):
    _, t, width = pre.shape
    grp = width // len(POOL_WINDOWS)
    padded = t + HALO * (len(segs) + 1)

    def group(win, pre_ref, w_ref, sc_ref, z_ref, diff_ref, pad_ref):
        pad_ref[...] = jnp.zeros_like(pad_ref)

        def fill(s, st, ln, off, b):
            pad_ref[pl.ds(off + b, CHUNK), :] = pre_ref[0, pl.ds(st + b, CHUNK), :]

        _for_chunks(segs, fill)

        def mix(s, st, ln, off, b):
            ext = pad_ref[pl.ds(off - HALO + b, CHUNK + 2 * HALO), :]
            total = _window_sum(ext, win)[HALO:HALO + CHUNK]
            u = pre_ref[0, pl.ds(st + b, CHUNK), :]
            diff = (total / _window_count(b, ln, win, u.shape) - u).astype(BF16)
            mixed = _dot(diff, w_ref[...])
            gate = _silu(pre_ref[1, pl.ds(st + b, CHUNK), :])
            z_ref[pl.ds(st + b, CHUNK), :] = (mixed * sc_ref[...] * gate).astype(BF16)
            diff_ref[pl.ds(st + b, CHUNK), :] = diff

        _for_chunks(segs, mix)

    def body(pre_ref, w_ref, sc_ref, z_ref, diff_ref, pad_ref):
        gi = pl.program_id(0)
        for widx, win in enumerate(POOL_WINDOWS):
            @pl.when(gi == widx)
            def _(win=win):
                group(win, pre_ref, w_ref, sc_ref, z_ref, diff_ref, pad_ref)

    col = pl.BlockSpec((t, grp), lambda g: (0, g))
    return _pc(
        body, name=name, grid=(len(POOL_WINDOWS),),
        in_specs=[pl.BlockSpec((2, t, grp), lambda g: (0, 0, g)), pl.BlockSpec((None, grp, grp), lambda g: (g, 0, 0)),
                  pl.BlockSpec((1, grp), lambda g: (0, g))],
        out_specs=[col, col],
        out_shape=[jax.ShapeDtypeStruct((t, width), BF16), jax.ShapeDtypeStruct((t, width), BF16)],
        scratch_shapes=[pltpu.VMEM((padded, grp), F32)],
        compiler_params=_params("parallel"),
    )(pre, w_grp, scale)


def _pool_bwd(dz, diff, pre, w_grp, scale, segs, name):
    _, t, width = pre.shape
    grp = width // len(POOL_WINDOWS)
    padded = t + HALO * (len(segs) + 1)

    def group(win, dz_ref, diff_ref, pre_ref, w_ref, sc_ref, dpre_ref, dw_ref, dsc_ref, pad_ref, dd_ref):
        pad_ref[...] = jnp.zeros_like(pad_ref)
        dw_ref[...] = jnp.zeros_like(dw_ref)
        dsc_ref[...] = jnp.zeros_like(dsc_ref)

        def first(s, st, ln, off, b):
            rows = pl.ds(st + b, CHUNK)
            diff_v = diff_ref[rows, :]
            mixed = _dot(diff_v, w_ref[...])
            g = pre_ref[1, rows, :]
            sg = _silu(g)
            dzv = dz_ref[rows, :]
            dmixed = (dzv * sc_ref[...] * sg).astype(BF16)
            dsc_ref[...] += jnp.sum(dzv * mixed * sg, axis=0, keepdims=True)
            dpre_ref[1, rows, :] = (dzv * mixed * sc_ref[...] * _dsilu(g)).astype(BF16)
            ddiff = _dot(dmixed, w_ref[...], 1, 1)
            dw_ref[...] += _dot(diff_v, dmixed, 0, 0)
            dd_ref[rows, :] = ddiff
            pad_ref[pl.ds(off + b, CHUNK), :] = ddiff / _window_count(b, ln, win, ddiff.shape)

        _for_chunks(segs, first)

        def second(s, st, ln, off, b):
            rows = pl.ds(st + b, CHUNK)
            ext = pad_ref[pl.ds(off - HALO + b, CHUNK + 2 * HALO), :]
            total = _shift(_window_sum(ext, win), -1)[HALO:HALO + CHUNK]
            dpre_ref[0, rows, :] = (total - dd_ref[rows, :]).astype(BF16)

        _for_chunks(segs, second)

    def body(dz_ref, diff_ref, pre_ref, w_ref, sc_ref, dpre_ref, dw_ref, dsc_ref, pad_ref, dd_ref):
        gi = pl.program_id(0)
        for widx, win in enumerate(POOL_WINDOWS):
            @pl.when(gi == widx)
            def _(win=win):
                group(win, dz_ref, diff_ref, pre_ref, w_ref, sc_ref, dpre_ref, dw_ref, dsc_ref, pad_ref, dd_ref)

    col = pl.BlockSpec((t, grp), lambda g: (0, g))
    both = pl.BlockSpec((2, t, grp), lambda g: (0, 0, g))
    wspec = pl.BlockSpec((None, grp, grp), lambda g: (g, 0, 0))
    sspec = pl.BlockSpec((1, grp), lambda g: (0, g))
    return _pc(
        body, name=name, grid=(len(POOL_WINDOWS),),
        in_specs=[col, col, both, wspec, sspec],
        out_specs=[both, wspec, sspec],
        out_shape=[jax.ShapeDtypeStruct((2, t, width), BF16), jax.ShapeDtypeStruct((len(POOL_WINDOWS), grp, grp), F32),
                   jax.ShapeDtypeStruct((1, width), F32)],
        scratch_shapes=[pltpu.VMEM((padded, grp), F32), pltpu.VMEM((t, grp), F32)],
        compiler_params=_params("parallel"),
    )(dz, diff, pre, w_grp, scale)


def _conv_fwd(pre, dw, db, name):
    _, t, width = pre.shape
    cb = LANES
    segs = [(0, t)]

    def body(pre_ref, dw_ref, db_ref, z_ref, pad_ref):
        pad_ref[...] = jnp.zeros_like(pad_ref)

        def fill(s, st, ln, off, b):
            rows = pl.ds(b, CHUNK)
            pad_ref[pl.ds(off + b, CHUNK), :] = pre_ref[1, rows, :] * pre_ref[2, rows, :]

        _for_chunks(segs, fill)

        def mix(s, st, ln, off, b):
            rows = pl.ds(b, CHUNK)
            ext = pad_ref[pl.ds(off - HALO + b, CHUNK + 2 * HALO), :]
            conv = (dw_ref[0:1, :] * _shift(ext, 1) + dw_ref[1:2, :] * ext + dw_ref[2:3, :] * _shift(ext, -1))
            conv = conv[HALO:HALO + CHUNK] + db_ref[...]
            y = pre_ref[0, rows, :] * conv
            z_ref[rows, :] = (y * _silu(pre_ref[3, rows, :])).astype(BF16)

        _for_chunks(segs, mix)

    return _pc(
        body, name=name, grid=(width // cb,),
        in_specs=[pl.BlockSpec((4, t, cb), lambda j: (0, 0, j)), pl.BlockSpec((8, cb), lambda j: (0, j)),
                  pl.BlockSpec((1, cb), lambda j: (0, j))],
        out_specs=pl.BlockSpec((t, cb), lambda j: (0, j)),
        out_shape=jax.ShapeDtypeStruct((t, width), BF16),
        scratch_shapes=[pltpu.VMEM((t + 2 * HALO, cb), F32)],
        compiler_params=_params("parallel"),
    )(pre, dw, db)


def _conv_bwd(dz, pre, dw, db, name):
    _, t, width = pre.shape
    cb = LANES
    segs = [(0, t)]

    def body(dz_ref, pre_ref, dw_ref, db_ref, dpre_ref, ddw_ref, ddb_ref, pad_a, pad_c):
        pad_a[...] = jnp.zeros_like(pad_a)
        pad_c[...] = jnp.zeros_like(pad_c)
        ddw_ref[...] = jnp.zeros_like(ddw_ref)
        ddb_ref[...] = jnp.zeros_like(ddb_ref)

        def fill(s, st, ln, off, b):
            rows = pl.ds(b, CHUNK)
            pad_a[pl.ds(off + b, CHUNK), :] = pre_ref[1, rows, :] * pre_ref[2, rows, :]

        _for_chunks(segs, fill)

        def first(s, st, ln, off, b):
            rows = pl.ds(b, CHUNK)
            ext = pad_a[pl.ds(off - HALO + b, CHUNK + 2 * HALO), :]
            prev, nxt = _shift(ext, 1)[HALO:HALO + CHUNK], _shift(ext, -1)[HALO:HALO + CHUNK]
            here = ext[HALO:HALO + CHUNK]
            conv = dw_ref[0:1, :] * prev + dw_ref[1:2, :] * here + dw_ref[2:3, :] * nxt + db_ref[...]
            bg, g = pre_ref[0, rows, :], pre_ref[3, rows, :]
            dzv = dz_ref[rows, :]
            dy = dzv * _silu(g)
            dpre_ref[3, rows, :] = (dzv * (bg * conv) * _dsilu(g)).astype(BF16)
            dpre_ref[0, rows, :] = (dy * conv).astype(BF16)
            dconv = dy * bg
            pad_c[pl.ds(off + b, CHUNK), :] = dconv
            ddw_ref[0:1, :] += jnp.sum(dconv * prev, axis=0, keepdims=True)
            ddw_ref[1:2, :] += jnp.sum(dconv * here, axis=0, keepdims=True)
            ddw_ref[2:3, :] += jnp.sum(dconv * nxt, axis=0, keepdims=True)
            ddb_ref[0:1, :] += jnp.sum(dconv, axis=0, keepdims=True)

        _for_chunks(segs, first)

        def second(s, st, ln, off, b):
            rows = pl.ds(b, CHUNK)
            ext = pad_c[pl.ds(off - HALO + b, CHUNK + 2 * HALO), :]
            da = (dw_ref[0:1, :] * _shift(ext, -1) + dw_ref[1:2, :] * ext + dw_ref[2:3, :] * _shift(ext, 1))
            da = da[HALO:HALO + CHUNK]
            dpre_ref[1, rows, :] = (da * pre_ref[2, rows, :]).astype(BF16)
            dpre_ref[2, rows, :] = (da * pre_ref[1, rows, :]).astype(BF16)

        _for_chunks(segs, second)

    quad = pl.BlockSpec((4, t, cb), lambda j: (0, 0, j))
    rows8 = pl.BlockSpec((8, cb), lambda j: (0, j))
    return _pc(
        body, name=name, grid=(width // cb,),
        in_specs=[pl.BlockSpec((t, cb), lambda j: (0, j)), quad, rows8, pl.BlockSpec((1, cb), lambda j: (0, j))],
        out_specs=[quad, rows8, rows8],
        out_shape=[jax.ShapeDtypeStruct((4, t, width), BF16), jax.ShapeDtypeStruct((8, width), F32),
                   jax.ShapeDtypeStruct((8, width), F32)],
        scratch_shapes=[pltpu.VMEM((t + 2 * HALO, cb), F32), pltpu.VMEM((t + 2 * HALO, cb), F32)],
        compiler_params=_params("parallel"),
    )(dz, pre, dw, db)


def _bias_maps():
    q = np.arange(GRID_W)[:, None]
    kc = np.arange(GRID_W)[None, :]
    start = np.clip(q - WIN_COLS // 2, 0, GRID_W - WIN_COLS)
    inside = (kc >= start) & (kc < start + WIN_COLS)
    dc = kc - q + WIN_COLS - 1
    onehot = np.zeros((GRID_W * GRID_W, LANES), np.float32)
    flat = np.arange(GRID_W * GRID_W).reshape(GRID_W, GRID_W)
    onehot[flat[inside], dc[inside]] = 1.0
    mask = np.where(inside, 0.0, MASKED).astype(np.float32).reshape(1, -1)
    rows = np.zeros((16, WIN_ROWS * WIN_ROWS), np.float32)
    for j in range(WIN_ROWS):
        for i in range(WIN_ROWS):
            rows[j + i, j * WIN_ROWS + i] = 1.0
    return onehot, mask, rows


def _bias_table(rpb):
    heads = rpb.shape[0]
    onehot, mask, rows = _bias_maps()
    padded = jnp.zeros((heads, 16, LANES), F32).at[:, :rpb.shape[1], :rpb.shape[2]].set(rpb)

    def body(r_ref, rows_ref, oh_ref, mask_ref, o_ref):
        by_row = _dot(rows_ref[...], r_ref[...], precision=HIGHEST)
        o_ref[...] = _dot(by_row, oh_ref[...], precision=HIGHEST) + mask_ref[...]

    n = GRID_W * GRID_W
    return _pc(
        body, name="bias_table", grid=(heads,),
        in_specs=[pl.BlockSpec((None, 16, LANES), lambda h: (h, 0, 0)), pl.BlockSpec((64, 16), lambda h: (0, 0)),
                  pl.BlockSpec((LANES, n), lambda h: (0, 0)), pl.BlockSpec((1, n), lambda h: (0, 0))],
        out_specs=pl.BlockSpec((None, 64, n), lambda h: (h, 0, 0)),
        out_shape=jax.ShapeDtypeStruct((heads, 64, n), F32),
        compiler_params=_params("parallel"),
    )(padded, jnp.asarray(rows.T.copy()), jnp.asarray(onehot.T.copy()), jnp.asarray(mask))


def _bias_grad(dtab):
    heads = dtab.shape[0]
    onehot, _, rows = _bias_maps()
    n = GRID_W * GRID_W

    def body(d_ref, rows_ref, oh_ref, o_ref):
        by_dc = _dot(d_ref[...], oh_ref[...], precision=HIGHEST)
        o_ref[...] = _dot(rows_ref[...], by_dc, precision=HIGHEST)

    return _pc(
        body, name="bias_grad", grid=(heads,),
        in_specs=[pl.BlockSpec((None, 64, n), lambda h: (h, 0, 0)), pl.BlockSpec((16, 64), lambda h: (0, 0)),
                  pl.BlockSpec((n, LANES), lambda h: (0, 0))],
        out_specs=pl.BlockSpec((None, 16, LANES), lambda h: (h, 0, 0)),
        out_shape=jax.ShapeDtypeStruct((heads, 16, LANES), F32),
        compiler_params=_params("parallel"),
    )(dtab, jnp.asarray(rows), jnp.asarray(onehot))


def _attn_rows(r, n_rows):
    first = jnp.clip(r - WIN_ROWS // 2, 0, n_rows - WIN_ROWS)
    return first, first - r + WIN_ROWS - 1


def _attn_probs(q, k_blk, k_ctx, bias):
    s_loc = _dot(q, k_blk, 1, 1) + bias
    s_ctx = _dot(q, k_ctx, 1, 1)
    m = jnp.maximum(jnp.max(s_loc, axis=-1, keepdims=True), jnp.max(s_ctx, axis=-1, keepdims=True))
    e_loc, e_ctx = jnp.exp(s_loc - m), jnp.exp(s_ctx - m)
    inv = 1.0 / (jnp.sum(e_loc, axis=-1, keepdims=True) + jnp.sum(e_ctx, axis=-1, keepdims=True))
    return e_loc * inv, e_ctx * inv


def _attn_fwd(qkv, bias, seq):
    _, heads, t, hd = qkv.shape
    n_rows = seq // GRID_W
    n_ctx = t - seq
    blk = WIN_ROWS * GRID_W

    def body(q_ref, k_ref, v_ref, b_ref, o_ref, kb_ref, vb_ref):
        kb_ref[...] = k_ref[...].astype(BF16)
        vb_ref[...] = v_ref[...].astype(BF16)

        def step(r, carry):
            first, j = _attn_rows(r, n_rows)
            rows = pl.ds(pl.multiple_of(r * GRID_W, GRID_W), GRID_W)
            keys = pl.ds(pl.multiple_of(first * GRID_W, GRID_W), blk)
            q = (q_ref[rows, :] * HEAD_DIM ** -0.5).astype(BF16)
            p_loc, p_ctx = _attn_probs(q, kb_ref[keys, :], kb_ref[pl.ds(seq, n_ctx), :], b_ref[j])
            o_ref[rows, :] = (_dot(p_loc.astype(BF16), vb_ref[keys, :])
                              + _dot(p_ctx.astype(BF16), vb_ref[pl.ds(seq, n_ctx), :]))
            return carry

        lax.fori_loop(0, n_rows, step, 0)

    def head(part):
        return pl.BlockSpec((None, None, t, hd), lambda h: (part, h, 0, 0))

    return _pc(
        body, name="attn_fwd", grid=(heads,),
        in_specs=[head(0), head(1), head(2), pl.BlockSpec((None, WIN_ROWS, GRID_W, blk), lambda h: (h, 0, 0, 0))],
        out_specs=pl.BlockSpec((None, seq, hd), lambda h: (h, 0, 0)),
        out_shape=jax.ShapeDtypeStruct((heads, seq, hd), F32),
        scratch_shapes=[pltpu.VMEM((t, hd), BF16), pltpu.VMEM((t, hd), BF16)],
        compiler_params=_params("parallel"),
    )(qkv, qkv, qkv, bias)


def _attn_bwd(qkv, d_o, bias, seq):
    _, heads, t, hd = qkv.shape
    n_rows = seq // GRID_W
    n_ctx = t - seq
    blk = WIN_ROWS * GRID_W

    def body(q_ref, k_ref, v_ref, do_ref, b_ref, dq_ref, dk_ref, dv_ref, db_ref, kb_ref, vb_ref):
        kb_ref[...] = k_ref[...].astype(BF16)
        vb_ref[...] = v_ref[...].astype(BF16)
        dk_ref[...] = jnp.zeros_like(dk_ref)
        dv_ref[...] = jnp.zeros_like(dv_ref)
        db_ref[...] = jnp.zeros_like(db_ref)
        ctx = pl.ds(seq, n_ctx)

        def step(r, carry):
            first, j = _attn_rows(r, n_rows)
            rows = pl.ds(pl.multiple_of(r * GRID_W, GRID_W), GRID_W)
            keys = pl.ds(pl.multiple_of(first * GRID_W, GRID_W), blk)
            q = (q_ref[rows, :] * HEAD_DIM ** -0.5).astype(BF16)
            k_blk, k_ctx = kb_ref[keys, :], kb_ref[ctx, :]
            p_loc, p_ctx = _attn_probs(q, k_blk, k_ctx, b_ref[j])
            dov = do_ref[rows, :].astype(BF16)
            dp_loc = _dot(dov, vb_ref[keys, :], 1, 1)
            dp_ctx = _dot(dov, vb_ref[ctx, :], 1, 1)
            delta = (jnp.sum(p_loc * dp_loc, axis=-1, keepdims=True) + jnp.sum(p_ctx * dp_ctx, axis=-1, keepdims=True))
            ds_loc = p_loc * (dp_loc - delta)
            ds_ctx = p_ctx * (dp_ctx - delta)
            ds_loc_b, ds_ctx_b = ds_loc.astype(BF16), ds_ctx.astype(BF16)
            dq_ref[rows, :] = (_dot(ds_loc_b, k_blk) + _dot(ds_ctx_b, k_ctx)) * HEAD_DIM ** -0.5
            dk_ref[keys, :] += _dot(ds_loc_b, q, 0, 0)
            dk_ref[ctx, :] += _dot(ds_ctx_b, q, 0, 0)
            dv_ref[keys, :] += _dot(p_loc.astype(BF16), dov, 0, 0)
            dv_ref[ctx, :] += _dot(p_ctx.astype(BF16), dov, 0, 0)
            db_ref[j] += ds_loc
            return carry

        lax.fori_loop(0, n_rows, step, 0)

    def head(part):
        return pl.BlockSpec((None, None, t, hd), lambda h: (part, h, 0, 0))

    lat = pl.BlockSpec((None, seq, hd), lambda h: (h, 0, 0))
    full = pl.BlockSpec((None, t, hd), lambda h: (h, 0, 0))
    bspec = pl.BlockSpec((None, WIN_ROWS, GRID_W, blk), lambda h: (h, 0, 0, 0))
    return _pc(
        body, name="attn_bwd", grid=(heads,),
        in_specs=[head(0), head(1), head(2), lat, bspec],
        out_specs=[lat, full, full, bspec],
        out_shape=[jax.ShapeDtypeStruct((heads, seq, hd), F32), jax.ShapeDtypeStruct((heads, t, hd), F32),
                   jax.ShapeDtypeStruct((heads, t, hd), F32), jax.ShapeDtypeStruct(bias.shape, F32)],
        scratch_shapes=[pltpu.VMEM((t, hd), BF16), pltpu.VMEM((t, hd), BF16)],
        compiler_params=_params("parallel"),
    )(qkv, qkv, qkv, d_o, bias)


def _gate_fwd(o, pre, tr):
    seq, width = o.shape

    def body(o_ref, g_ref, z_ref):
        z_ref[...] = (o_ref[...] * _silu(g_ref[...])).astype(BF16)

    row = pl.BlockSpec((tr, width), lambda i: (i, 0))
    return _pc(
        body, name="gate_fwd", grid=(seq // tr,),
        in_specs=[row, pl.BlockSpec((None, tr, width), lambda i: (3, i, 0))], out_specs=row,
        out_shape=jax.ShapeDtypeStruct((seq, width), BF16),
        compiler_params=_params("parallel"),
    )(o, pre)


def _gate_bwd(dz, o, pre, tr):
    seq, width = o.shape

    def body(dz_ref, o_ref, g_ref, do_ref, dg_ref):
        g = g_ref[...]
        dzv = dz_ref[...]
        do_ref[...] = dzv * _silu(g)
        dg_ref[...] = (dzv * o_ref[...] * _dsilu(g)).astype(BF16)

    row = pl.BlockSpec((tr, width), lambda i: (i, 0))
    return _pc(
        body, name="gate_bwd", grid=(seq // tr,),
        in_specs=[row, row, pl.BlockSpec((None, tr, width), lambda i: (3, i, 0))], out_specs=[row, row],
        out_shape=[jax.ShapeDtypeStruct((seq, width), F32), jax.ShapeDtypeStruct((seq, width), BF16)],
        compiler_params=_params("parallel"),
    )(dz, o, pre)


def _adamw(w, m, v, parts, name, mult=None):
    rows, cols = w.shape
    tr = _tile(rows, max(8, 131072 // cols), 8)
    n_parts = len(parts)
    c1 = 1.0 - ADAM_B1 ** ADAM_STEP
    c2 = 1.0 - ADAM_B2 ** ADAM_STEP

    def body(*refs):
        w_ref, m_ref, v_ref = refs[:3]
        part_refs = refs[3:3 + n_parts]
        rest = refs[3 + n_parts:]
        g = part_refs[0][...].astype(F32)
        for p in part_refs[1:]:
            g = g + p[...].astype(F32)
        if mult is not None:
            g = g * rest[0][...]
            rest = rest[1:]
        g_ref, d_ref, nm_ref, nv_ref = rest
        m2 = ADAM_B1 * m_ref[...] + (1.0 - ADAM_B1) * g
        v2 = ADAM_B2 * v_ref[...] + (1.0 - ADAM_B2) * (g * g)
        m_hat = m2 / c1
        v_hat = v2 / c2
        g_ref[...] = g
        d_ref[...] = -ADAM_LR * (m_hat / (jnp.sqrt(v_hat) + ADAM_EPS) + ADAM_WD * w_ref[...])
        nm_ref[...] = m2
        nv_ref[...] = v2

    tile = pl.BlockSpec((tr, cols), lambda i: (i, 0))
    in_specs, args = [tile, tile, tile], [w, m, v]
    for p in parts:
        if isinstance(p, tuple):
            arr, k = p
            in_specs.append(pl.BlockSpec((None, tr, cols), lambda i, k=k: (k, i, 0)))
            args.append(arr)
        else:
            in_specs.append(tile)
            args.append(p)
    if mult is not None:
        in_specs.append(tile)
        args.append(mult)
    shape = jax.ShapeDtypeStruct((rows, cols), F32)
    return _pc(
        body, name=name, grid=(rows // tr,), in_specs=in_specs, out_specs=[tile] * 4, out_shape=[shape] * 4,
        compiler_params=_params("parallel"),
    )(*args)


def _rows128(a):
    flat = a.reshape(-1)
    pad = (-flat.shape[0]) % LANES
    if pad:
        flat = jnp.concatenate([flat, jnp.zeros((pad,), flat.dtype)])
    return flat.reshape(-1, LANES)


def _pad_rows(a, mult=8):
    pad = (-a.shape[0]) % mult
    if pad:
        a = jnp.concatenate([a, jnp.zeros((pad,) + a.shape[1:], a.dtype)], axis=0)
    return a


def kernel(x, c, ctx, c_ctx, norm_g, ada_w, ada_b, pool_w_in, pool_w_grp, pool_scale, pool_w_out, na_w_in, na_rpb, na_w_out, conv_w_in, conv_dw, conv_db, conv_w_out, final_g, loss_target, m_c_ctx, m_norm_g, m_ada_w, m_ada_b, m_pool_w_in, m_pool_w_grp, m_pool_scale, m_pool_w_out, m_na_w_in, m_na_rpb, m_na_w_out, m_conv_w_in, m_conv_dw, m_conv_db, m_conv_w_out, m_final_g, v_c_ctx, v_norm_g, v_ada_w, v_ada_b, v_pool_w_in, v_pool_w_grp, v_pool_scale, v_pool_w_out, v_na_w_in, v_na_rpb, v_na_w_out, v_conv_w_in, v_conv_dw, v_conv_db, v_conv_w_out, v_final_g):
    xi, yi, ci = _my_place()
    me = 4 * xi + 2 * yi + ci
    seq, d = x.shape[1], x.shape[2]
    n_ctx = ctx.shape[1]
    t_all = seq + n_ctx
    width = d
    heads = width // HEAD_DIM
    depth = norm_g.shape[0]
    nb = ada_w.shape[2]
    shard = width // N_DEV
    d_rows = d // LANES
    tr = math.gcd(math.gcd(seq, n_ctx), 256)
    x_tiles = seq // tr

    small_in = _pad_rows(jnp.concatenate([_rows128(c), pool_scale, conv_dw[0], conv_db], axis=0))
    got = _gather_small(small_in, "gather_inputs")
    r0 = d_rows
    c_all = got[:, :r0].reshape(N_DEV, d)
    n_pool = pool_scale.shape[0]
    scale_full = got[:, r0:r0 + n_pool].transpose(1, 0, 2).reshape(n_pool, width)
    r1 = r0 + n_pool
    taps_full = _pad_rows(got[:, r1:r1 + 3].transpose(1, 0, 2).reshape(3, width))
    bias_full = got[:, r1 + 3:r1 + 4].transpose(1, 0, 2).reshape(1, width)

    cond = jnp.concatenate([c_all, c_ctx[None], jnp.zeros((7, d), F32)], axis=0)
    bias_mine = lax.dynamic_slice(ada_b, (0, me * nb), (depth, nb))
    mod_mine = _mod_fwd(cond, ada_w, bias_mine)
    mod_all = _gather_small(mod_mine.reshape(-1, LANES), "gather_mod")
    mod_all = mod_all.reshape(N_DEV, depth, 16, nb).transpose(1, 2, 0, 3).reshape(depth, 16, 3 * d)
    mod_x = lax.dynamic_index_in_dim(mod_all, me, 1, keepdims=False).reshape(depth, 3, d)
    mod_c = mod_all[:, 8].reshape(depth, 3, d)
    pad5 = jnp.zeros((depth, 5, d), F32)
    mod_x = jnp.concatenate([mod_x, pad5], axis=1)
    mod_c = jnp.concatenate([mod_c, pad5], axis=1)
    mods = [jnp.stack([mod_x[i], mod_c[i]]) if i < 2 else mod_x[i][None] for i in range(depth)]

    gathered = _allgather_big(
        [w.astype(BF16) for w in (pool_w_in, pool_w_grp, pool_w_out, na_w_in, na_w_out, conv_w_in, conv_w_out)],
        "gather_weights")
    pool_in_w, pool_grp_w, pool_out_w, na_in_w, na_out_w, conv_in_w, conv_out_w = gathered
    n_grp = pool_w_grp.shape[1]
    grp = width // n_grp
    pool_grp_w = pool_grp_w.transpose(1, 2, 0, 3, 4).reshape(n_pool, n_grp, grp, grp)
    pool_out_w = pool_out_w.transpose(1, 0, 2, 3).reshape(n_pool, width, d)
    na_out_w = na_out_w.reshape(width, d)
    conv_out_w = conv_out_w.reshape(width, d)

    both = [(0, seq), (seq, n_ctx)]
    latent = [(0, seq)]

    xs0 = jnp.concatenate([x[0], ctx[0]], axis=0)
    h0 = _norm_fwd(xs0, norm_g[0:1], mods[0], tr, x_tiles, "norm_fwd0")
    pre0 = _proj_in(h0, pool_in_w, 0, 2, "proj_in0")
    z0, diff0 = _pool_fwd(pre0, pool_grp_w[0], scale_full[0:1], both, "pool_fwd0")
    yx0, xs1 = _proj_out(z0, pool_out_w[0], xs0, mods[0], tr, x_tiles, "proj_out0")

    h1 = _norm_fwd(xs1, norm_g[1:2], mods[1], tr, x_tiles, "norm_fwd1")
    pre1 = _proj_in(h1, na_in_w, 0, 4, "proj_in1")
    qkv = pre1[:3].reshape(3, t_all, heads, HEAD_DIM).transpose(0, 2, 1, 3)
    table = _bias_table(na_rpb[0])
    table = table.reshape(heads, WIN_ROWS, WIN_ROWS, GRID_W, GRID_W).transpose(0, 1, 3, 2, 4)
    table = table.reshape(heads, WIN_ROWS, GRID_W, WIN_ROWS * GRID_W)
    o_heads = _attn_fwd(qkv, table, seq)
    o1 = o_heads.transpose(1, 0, 2).reshape(seq, width)
    z1 = _gate_fwd(o1, pre1, tr)
    yx1, x2 = _proj_out(z1, na_out_w, xs1, mods[1], tr, x_tiles, "proj_out1")

    h2 = _norm_fwd(x2, norm_g[2:3], mods[2], tr, x_tiles, "norm_fwd2")
    pre2 = _proj_in(h2, conv_in_w, 0, 4, "proj_in2")
    z2 = _conv_fwd(pre2, taps_full, bias_full, "conv_fwd")
    yx2, x3 = _proj_out(z2, conv_out_w, x2, mods[2], tr, x_tiles, "proj_out2")

    h3 = _norm_fwd(x3, norm_g[3:4], mods[3], tr, x_tiles, "norm_fwd3")
    pre3 = _proj_in(h3, pool_in_w, 1, 2, "proj_in3")
    z3, diff3 = _pool_fwd(pre3, pool_grp_w[1], scale_full[1:2], latent, "pool_fwd3")
    yx3, x4 = _proj_out(z3, pool_out_w[1], x3, mods[3], tr, x_tiles, "proj_out3")

    loss_part, dx4, d_final = _loss_head(x4, loss_target[0], final_g[None], tr)
    loss = lax.psum(loss_part[0, 0], ("x", "y", "c"))

    dyx3, gate3 = _resid_bwd(dx4, yx3, mods[3], tr, x_tiles, "resid_bwd3")
    dz3 = _proj_out_dz(dyx3, pool_out_w[1], "proj_out_dz3")
    g_pool_out1 = _grad_w_out(z3, dyx3, "grad_w_out3")
    dpre3, g_grp1, g_scale1 = _pool_bwd(dz3, diff3, pre3, pool_grp_w[1], scale_full[1:2], latent, "pool_bwd3")
    dh3 = _proj_in_dh(dpre3, pool_in_w, 1, "proj_in_dh3")
    g_pool_in1 = _grad_w_in(h3, dpre3, pool_w_in.shape[2], "grad_w_in3")
    dx3, norm3 = _norm_bwd(x3, dh3, dx4, norm_g[3:4], mods[3], tr, x_tiles, "norm_bwd3")

    dyx2, gate2 = _resid_bwd(dx3, yx2, mods[2], tr, x_tiles, "resid_bwd2")
    dz2 = _proj_out_dz(dyx2, conv_out_w, "proj_out_dz2")
    g_conv_out = _grad_w_out(z2, dyx2, "grad_w_out2")
    dpre2, g_taps, g_cbias = _conv_bwd(dz2, pre2, taps_full, bias_full, "conv_bwd")
    dh2 = _proj_in_dh(dpre2, conv_in_w, 0, "proj_in_dh2")
    g_conv_in = _grad_w_in(h2, dpre2, conv_w_in.shape[2], "grad_w_in2")
    dx2, norm2 = _norm_bwd(x2, dh2, dx3, norm_g[2:3], mods[2], tr, x_tiles, "norm_bwd2")

    dyx1, gate1 = _resid_bwd(dx2, yx1, mods[1][:1], tr, x_tiles, "resid_bwd1")
    dz1 = _proj_out_dz(dyx1, na_out_w, "proj_out_dz1")
    g_na_out = _grad_w_out(z1, dyx1, "grad_w_out1")
    do1, dg1 = _gate_bwd(dz1, o1, pre1, tr)
    do_heads = do1.reshape(seq, heads, HEAD_DIM).transpose(1, 0, 2)
    dq, dk, dv, dtable = _attn_bwd(qkv, do_heads, table, seq)

    def unhead(a):
        return a.transpose(1, 0, 2).reshape(a.shape[1], width).astype(BF16)

    zero_ctx = jnp.zeros((n_ctx, width), BF16)
    dpre1 = jnp.stack([jnp.concatenate([unhead(dq), zero_ctx], axis=0), unhead(dk), unhead(dv),
                       jnp.concatenate([dg1, zero_ctx], axis=0)])
    dtable = dtable.reshape(heads, WIN_ROWS, GRID_W, WIN_ROWS, GRID_W).transpose(0, 1, 3, 2, 4)
    g_rpb = _bias_grad(dtable.reshape(heads, WIN_ROWS * WIN_ROWS, GRID_W * GRID_W))
    g_rpb = g_rpb[:, :na_rpb.shape[2], :na_rpb.shape[3]]
    dh1 = _proj_in_dh(dpre1, na_in_w, 0, "proj_in_dh1")
    g_na_in = _grad_w_in(h1, dpre1, na_w_in.shape[2], "grad_w_in1")
    dres1 = jnp.concatenate([dx2, jnp.zeros((n_ctx, d), F32)], axis=0)
    dxs1, norm1 = _norm_bwd(xs1, dh1, dres1, norm_g[1:2], mods[1], tr, x_tiles, "norm_bwd1")

    dyx0, gate0 = _resid_bwd(dxs1, yx0, mods[0], tr, x_tiles, "resid_bwd0")
    dz0 = _proj_out_dz(dyx0, pool_out_w[0], "proj_out_dz0")
    g_pool_out0 = _grad_w_out(z0, dyx0, "grad_w_out0")
    dpre0, g_grp0, g_scale0 = _pool_bwd(dz0, diff0, pre0, pool_grp_w[0], scale_full[0:1], both, "pool_bwd0")
    dh0 = _proj_in_dh(dpre0, pool_in_w, 0, "proj_in_dh0")
    g_pool_in0 = _grad_w_in(h0, dpre0, pool_w_in.shape[2], "grad_w_in0")
    dxs0, norm0 = _norm_bwd(xs0, dh0, dxs1, norm_g[0:1], mods[0], tr, x_tiles, "norm_bwd0")
    grad_x = dxs0[:seq][None]

    norms, gates = [norm0, norm1, norm2, norm3], [gate0, gate1, gate2, gate3]
    zero_d = jnp.zeros((d,), F32)
    dm_rows = [jnp.concatenate([norms[i][0, 0], norms[i][0, 1], gates[i][0, 0]]) for i in range(depth)]
    dm_rows.append(jnp.concatenate([norm0[1, 0], norm0[1, 1], gate0[1, 0]]))
    dm_rows.append(jnp.concatenate([norm1[1, 0], norm1[1, 1], zero_d]))
    dm_local = jnp.stack(dm_rows + [jnp.zeros((3 * d,), F32)] * 2)
    g_norm_part = jnp.stack([norm0[0, 2] + norm0[1, 2], norm1[0, 2] + norm1[1, 2], norm2[0, 2], norm3[0, 2]])
    g_scale_part = jnp.concatenate([g_scale0, g_scale1], axis=0)
    pieces = [_rows128(dm_local), _rows128(g_norm_part), _rows128(d_final[0]), _pad_rows(_rows128(g_rpb)),
              _rows128(g_scale_part), _rows128(g_taps[:3]), _rows128(g_cbias[0])]
    small_out = _gather_small(jnp.concatenate(pieces, axis=0), "gather_small_grads")
    marks = np.cumsum([0] + [p.shape[0] for p in pieces])

    def piece(k):
        return small_out[:, marks[k]:marks[k + 1]]

    dm_all = piece(0).reshape(N_DEV, 8, 3 * d).transpose(1, 0, 2)
    dm_mine = lax.dynamic_slice(dm_all, (0, 0, me * nb), (8, N_DEV, nb))
    g_ada_w, g_ada_b, cctx_part, dsilu_cond = _mod_bwd(cond, ada_w, dm_all, dm_mine)
    cctx_all = _gather_small(_rows128(cctx_part[0]), "gather_cctx")

    def my_shard(a, n):
        a = a.reshape(N_DEV, n, N_DEV, shard)
        return lax.dynamic_index_in_dim(a, me, 2, keepdims=False)

    rpb_rows = marks[4] - marks[3]
    zeros7 = lambda r: jnp.zeros((N_DEV - 1, r, LANES), F32)
    ada_b_rows = _rows128(g_ada_b)
    small_parts = jnp.concatenate([
        cctx_all, piece(1), jnp.concatenate([ada_b_rows[None], zeros7(ada_b_rows.shape[0])], axis=0), piece(2), piece(3),
        my_shard(piece(4), n_pool), my_shard(piece(5), 3), my_shard(piece(6), 1)], axis=1)
    n_small = small_parts.shape[1]
    small_parts = jnp.concatenate([small_parts, jnp.zeros((N_DEV, (-n_small) % 8, LANES), F32)], axis=1)

    def pack(c_ctx_, norm_g_, ada_b_, final_g_, rpb_, scale_, taps_, cbias_):
        rows = [_rows128(c_ctx_), _rows128(norm_g_), _rows128(ada_b_), _rows128(final_g_), _pad_rows(_rows128(rpb_)),
                scale_, taps_[0], cbias_]
        return _pad_rows(jnp.concatenate(rows, axis=0)), np.cumsum([0] + [r.shape[0] for r in rows])

    w_small, smarks = pack(c_ctx, norm_g, ada_b, final_g, na_rpb, pool_scale, conv_dw, conv_db)
    m_small, _ = pack(m_c_ctx, m_norm_g, m_ada_b, m_final_g, m_na_rpb, m_pool_scale, m_conv_dw, m_conv_db)
    v_small, _ = pack(v_c_ctx, v_norm_g, v_ada_b, v_final_g, v_na_rpb, v_pool_scale, v_conv_dw, v_conv_db)
    mult = jnp.concatenate([_rows128(dsilu_cond[8]), jnp.ones((w_small.shape[0] - d_rows, LANES), F32)], axis=0)
    small_res = _adamw(w_small, m_small, v_small, [(small_parts, k) for k in range(N_DEV)], "adamw_small", mult=mult)

    def unpack(k, like):
        out = []
        for r in small_res:
            flat = r[smarks[k]:smarks[k + 1]].reshape(-1)
            out.append(flat[:like.size].reshape(like.shape))
        return out

    res = {"c_ctx": unpack(0, c_ctx), "norm_g": unpack(1, norm_g), "ada_b": unpack(2, ada_b),
           "final_g": unpack(3, final_g), "na_rpb": unpack(4, na_rpb), "pool_scale": unpack(5, pool_scale),
           "conv_dw": unpack(6, conv_dw), "conv_db": unpack(7, conv_db)}

    res["ada_w"] = [r.reshape(ada_w.shape) for r in _adamw(
        ada_w.reshape(-1, nb), m_ada_w.reshape(-1, nb), v_ada_w.reshape(-1, nb), [g_ada_w.reshape(-1, nb)], "adamw_ada_w")]

    def grp_slots(g):
        return g.reshape(n_grp, N_DEV, grp // N_DEV, grp).transpose(1, 0, 2, 3).reshape(N_DEV, -1, grp)

    full = [g_pool_in0, g_pool_in1, grp_slots(g_grp0), grp_slots(g_grp1),
            g_pool_out0.reshape(N_DEV, shard, d), g_pool_out1.reshape(N_DEV, shard, d),
            g_na_in, g_na_out.reshape(N_DEV, shard, d), g_conv_in, g_conv_out.reshape(N_DEV, shard, d)]
    from_sibling = _rs_between_cores(full, "rs_cores")
    chip = 2 * xi + yi
    slots = jnp.stack([chip, chip ^ 2, chip ^ 1, chip ^ 3, ci]).astype(jnp.int32)
    own, sends = [], []
    for k, (g, r) in enumerate(zip(full, from_sibling)):
        o, s = _rs_chip_sum(g, r, slots, f"rs_chip_sum{k}")
        own.append(o)
        sends.append(s)
    landed = _rs_between_chips(sends, "rs_chips")

    def big(k, w, m, v, name):
        shape = w.shape
        view = (-1, shape[-1])
        parts = [own[k]] + [(landed[k], j) for j in range(3)]
        return [r.reshape(shape) for r in _adamw(w.reshape(view), m.reshape(view), v.reshape(view), parts, name)]

    def two_layers(k, w, m, v, name):
        a = big(k, w[0], m[0], v[0], name + "0")
        b = big(k + 1, w[1], m[1], v[1], name + "1")
        return [jnp.stack([p, q]) for p, q in zip(a, b)]

    res["pool_w_in"] = two_layers(0, pool_w_in, m_pool_w_in, v_pool_w_in, "adamw_pool_in")
    res["pool_w_grp"] = two_layers(2, pool_w_grp, m_pool_w_grp, v_pool_w_grp, "adamw_pool_grp")
    res["pool_w_out"] = two_layers(4, pool_w_out, m_pool_w_out, v_pool_w_out, "adamw_pool_out")
    res["na_w_in"] = [r[None] for r in big(6, na_w_in[0], m_na_w_in[0], v_na_w_in[0], "adamw_na_in")]
    res["na_w_out"] = [r[None] for r in big(7, na_w_out[0], m_na_w_out[0], v_na_w_out[0], "adamw_na_out")]
    res["conv_w_in"] = [r[None] for r in big(8, conv_w_in[0], m_conv_w_in[0], v_conv_w_in[0], "adamw_conv_in")]
    res["conv_w_out"] = [r[None] for r in big(9, conv_w_out[0], m_conv_w_out[0], v_conv_w_out[0], "adamw_conv_out")]

    order = ["c_ctx", "norm_g", "ada_w", "ada_b", "pool_w_in", "pool_w_grp", "pool_scale", "pool_w_out", "na_w_in",
             "na_rpb", "na_w_out", "conv_w_in", "conv_dw", "conv_db", "conv_w_out", "final_g"]
    outs = [loss, grad_x]
    for j in range(4):
        outs += [res[n][j] for n in order]
    return tuple(outs)
```

```python
import functools
import math

import numpy as np
import jax
import jax.numpy as jnp
from jax import lax
from jax.experimental import pallas as pl
from jax.experimental.pallas import tpu as pltpu

F32 = jnp.float32
BF16 = jnp.bfloat16
N_DEV = 8
LANES = 128
RMS_EPS = 1e-6
GRID_W = 64
WIN_ROWS = 8
WIN_COLS = 16
HEAD_DIM = 64
POOL_WINDOWS = (2, 4, 8, 16)
HALO = 8
CHUNK = 128
MASKED = -1e30
ADAM_LR = 0.001
ADAM_B1 = 0.9
ADAM_B2 = 0.999
ADAM_EPS = 1e-08
ADAM_WD = 0.01
ADAM_STEP = 10
VMEM_LIMIT = 56 * 1024 * 1024
MESH = pl.DeviceIdType.MESH
HIGHEST = lax.Precision.HIGHEST
ANY = pl.BlockSpec(memory_space=pl.ANY)


def _pc(body, *, name, **kw):
    return pl.pallas_call(body, name=name, **kw)


def _params(*sem):
    return pltpu.CompilerParams(dimension_semantics=sem if sem else None, vmem_limit_bytes=VMEM_LIMIT)


def _dot(a, b, ca=1, cb=0, precision=None):
    return lax.dot_general(a, b, (((ca,), (cb,)), ((), ())), preferred_element_type=F32, precision=precision)


def _tile(n, pref, unit=LANES):
    best = None
    for t in range(unit, min(n, pref) + 1, unit):
        if n % t == 0:
            best = t
    return best if best is not None else n


def _sigmoid(x):
    return 1.0 / (1.0 + jnp.exp(-x))


def _silu(x):
    return x * _sigmoid(x)


def _dsilu(x):
    s = _sigmoid(x)
    return s * (1.0 + x * (1.0 - s))


def _my_place():
    return lax.axis_index("x"), lax.axis_index("y"), lax.axis_index("c")


def _flip(v, f):
    return 1 - v if f else v


def _gather_small(block, name):
    rows, cols = block.shape

    def body(x_ref, out_ref, send_sems, recv_sems):
        x, y, c = _my_place()
        me = 4 * x + 2 * y + c
        out_ref[me] = x_ref[...]
        copies = []
        for k in range(1, N_DEV):
            peer = (_flip(x, k & 4), _flip(y, k & 2), _flip(c, k & 1))
            cp = pltpu.make_async_remote_copy(
                src_ref=x_ref, dst_ref=out_ref.at[me], send_sem=send_sems.at[k - 1], recv_sem=recv_sems.at[k - 1],
                device_id=peer, device_id_type=MESH)
            cp.start()
            copies.append(cp)
        for cp in copies:
            cp.wait()

    return _pc(
        body, name=name,
        out_shape=jax.ShapeDtypeStruct((N_DEV, rows, cols), block.dtype),
        in_specs=[pl.BlockSpec(memory_space=pltpu.VMEM)],
        out_specs=pl.BlockSpec(memory_space=pltpu.VMEM),
        scratch_shapes=[pltpu.SemaphoreType.DMA((N_DEV - 1,)), pltpu.SemaphoreType.DMA((N_DEV - 1,))],
    )(block)


def _allgather_big(tensors, name):
    nt = len(tensors)

    def body(*refs):
        ins, outs = refs[:nt], refs[nt:2 * nt]
        send_sems, recv_sems, local_sems = refs[2 * nt:]
        x, y, c = _my_place()
        sibling = (x, y, 1 - c)
        chips = [(1 - x, y), (x, 1 - y), (1 - x, 1 - y)]

        def idx(px, py, pc):
            return 4 * px + 2 * py + pc

        def copy(t, k, block, to, src=None):
            dst = outs[t].at[idx(*block)]
            return pltpu.make_async_remote_copy(
                src_ref=dst if src is None else src, dst_ref=dst,
                send_sem=send_sems.at[t, k], recv_sem=recv_sems.at[t, k], device_id=to, device_id_type=MESH)

        me = (x, y, c)
        mine, first, passed = [], [], []
        for t in range(nt):
            m = pltpu.make_async_copy(ins[t], outs[t].at[idx(*me)], local_sems.at[t])
            m.start()
            mine.append(m)
            f = [copy(t, 0, me, sibling, src=ins[t])]
            f += [copy(t, 1 + j, me, (*chip, c), src=ins[t]) for j, chip in enumerate(chips)]
            for cp in f:
                cp.start()
            first += f
        for t in range(nt):
            for j, chip in enumerate(chips):
                copy(t, 1 + j, (*chip, c), me).wait_recv()
                p = copy(t, 4 + j, (*chip, c), sibling)
                p.start()
                passed.append(p)
        for t in range(nt):
            copy(t, 0, sibling, me).wait_recv()
            for j, chip in enumerate(chips):
                copy(t, 4 + j, (*chip, 1 - c), me).wait_recv()
        for cp in first + passed:
            cp.wait_send()
        for m in mine:
            m.wait()

    return _pc(
        body, name=name,
        out_shape=[jax.ShapeDtypeStruct((N_DEV,) + t.shape, t.dtype) for t in tensors],
        in_specs=[ANY] * nt, out_specs=[ANY] * nt,
        scratch_shapes=[pltpu.SemaphoreType.DMA((nt, 7)), pltpu.SemaphoreType.DMA((nt, 7)),
                        pltpu.SemaphoreType.DMA((nt,))],
    )(*tensors)


def _rs_between_cores(grads, name):
    nt = len(grads)

    def body(*refs):
        ins, outs = refs[:nt], refs[nt:2 * nt]
        send_sems, recv_sems = refs[2 * nt:]
        x, y, c = _my_place()
        sibling = (x, y, 1 - c)
        for t in range(nt):
            for p in range(4):
                pltpu.make_async_remote_copy(
                    src_ref=ins[t].at[2 * p + 1 - c], dst_ref=outs[t].at[p],
                    send_sem=send_sems.at[t], recv_sem=recv_sems.at[t], device_id=sibling, device_id_type=MESH).start()
        for t in range(nt):
            pltpu.make_async_remote_copy(
                src_ref=outs[t], dst_ref=outs[t], send_sem=send_sems.at[t], recv_sem=recv_sems.at[t],
                device_id=sibling, device_id_type=MESH).wait()

    return _pc(
        body, name=name,
        out_shape=[jax.ShapeDtypeStruct((4,) + g.shape[1:], g.dtype) for g in grads],
        in_specs=[ANY] * nt, out_specs=[ANY] * nt,
        scratch_shapes=[pltpu.SemaphoreType.DMA((nt,)), pltpu.SemaphoreType.DMA((nt,))],
    )(*grads)


def _rs_chip_sum(g, r1, slots, name):
    _, rows, cols = g.shape
    tr = _tile(rows, max(8, 131072 // cols), 8)

    def body(slot_ref, *refs):
        gs, rs = refs[:4], refs[4:8]
        own_ref, send_ref = refs[8:]
        own_ref[...] = gs[0][...] + rs[0][...]
        for k in range(1, 4):
            send_ref[k - 1] = (gs[k][...] + rs[k][...]).astype(BF16)

    def g_spec(k):
        return pl.BlockSpec((None, tr, cols), lambda i, s: (2 * s[k] + s[4], i, 0))

    def r_spec(k):
        return pl.BlockSpec((None, tr, cols), lambda i, s: (s[k], i, 0))

    grid_spec = pltpu.PrefetchScalarGridSpec(
        num_scalar_prefetch=1, grid=(rows // tr,),
        in_specs=[g_spec(k) for k in range(4)] + [r_spec(k) for k in range(4)],
        out_specs=[pl.BlockSpec((tr, cols), lambda i, s: (i, 0)), pl.BlockSpec((3, tr, cols), lambda i, s: (0, i, 0))])
    return _pc(
        body, name=name, grid_spec=grid_spec,
        out_shape=[jax.ShapeDtypeStruct((rows, cols), F32), jax.ShapeDtypeStruct((3, rows, cols), BF16)],
        compiler_params=_params("parallel"),
    )(slots, g, g, g, g, r1, r1, r1, r1)


def _rs_between_chips(sends, name):
    nt = len(sends)

    def body(*refs):
        ins, outs = refs[:nt], refs[nt:2 * nt]
        send_sems, recv_sems = refs[2 * nt:]
        x, y, c = _my_place()
        peers = [(1 - x, y, c), (x, 1 - y, c), (1 - x, 1 - y, c)]
        copies = []
        for t in range(nt):
            for k, peer in enumerate(peers):
                cp = pltpu.make_async_remote_copy(
                    src_ref=ins[t].at[k], dst_ref=outs[t].at[k], send_sem=send_sems.at[t, k], recv_sem=recv_sems.at[t, k],
                    device_id=peer, device_id_type=MESH)
                cp.start()
                copies.append(cp)
        for cp in copies:
            cp.wait()

    return _pc(
        body, name=name,
        out_shape=[jax.ShapeDtypeStruct(s.shape, s.dtype) for s in sends],
        in_specs=[ANY] * nt, out_specs=[ANY] * nt,
        scratch_shapes=[pltpu.SemaphoreType.DMA((nt, 3)), pltpu.SemaphoreType.DMA((nt, 3))],
    )(*sends)


def _mod_fwd(cond, ada_w, bias):
    depth, d, nb = ada_w.shape

    def body(c_ref, w_ref, b_ref, o_ref):
        s = _silu(c_ref[...]).astype(BF16)
        o_ref[...] = _dot(s, w_ref[...].astype(BF16)) + b_ref[...]

    return _pc(
        body, name="mod_fwd", grid=(depth,),
        in_specs=[pl.BlockSpec((16, d), lambda i: (0, 0)), pl.BlockSpec((None, d, nb), lambda i: (i, 0, 0)),
                  pl.BlockSpec((None, 1, nb), lambda i: (i, 0, 0))],
        out_specs=pl.BlockSpec((None, 16, nb), lambda i: (i, 0, 0)),
        out_shape=jax.ShapeDtypeStruct((depth, 16, nb), F32),
        compiler_params=_params("parallel"),
    )(cond, ada_w, bias.reshape(depth, 1, nb))


def _mod_bwd(cond, ada_w, dm_all, dm_mine):
    depth, d, nb = ada_w.shape
    d3 = dm_all.shape[-1]

    def body(c_ref, w_ref, all_ref, call_ref, mine_ref, cmine_ref, gw_ref, gb_ref, part_ref, ds_ref):
        i = pl.program_id(0)
        cond_v = c_ref[...]
        s = _silu(cond_v).astype(BF16)
        has_ctx = jnp.where(i < 2, 1.0, 0.0)
        tot_all = jnp.sum(call_ref[...], axis=0, keepdims=True) * has_ctx
        tot_mine = jnp.broadcast_to(jnp.sum(cmine_ref[...], axis=0, keepdims=True) * has_ctx, (8, nb)).astype(BF16)
        gb_ref[...] = jnp.sum(all_ref[...], axis=0, keepdims=True) + tot_all
        gw_ref[...] = _dot(s[0:8], mine_ref[...].astype(BF16), 0, 0) + _dot(s[8:16], tot_mine, 0, 0)
        part = _dot(tot_mine, w_ref[...].astype(BF16), 1, 1)

        @pl.when(i == 0)
        def _():
            part_ref[...] = jnp.zeros_like(part_ref)
            ds_ref[...] = _dsilu(cond_v)

        part_ref[...] += part

    def rows(width, which):
        return pl.BlockSpec((None, N_DEV, width), which)

    layer = lambda i: (i, 0, 0)
    ctx_layer = lambda i: (jnp.minimum(i, 1) + 4, 0, 0)
    return _pc(
        body, name="mod_bwd", grid=(depth,),
        in_specs=[pl.BlockSpec((16, d), lambda i: (0, 0)), pl.BlockSpec((None, d, nb), layer),
                  rows(d3, layer), rows(d3, ctx_layer), rows(nb, layer), rows(nb, ctx_layer)],
        out_specs=[pl.BlockSpec((None, d, nb), layer), pl.BlockSpec((None, 1, d3), layer),
                   pl.BlockSpec((8, d), lambda i: (0, 0)), pl.BlockSpec((16, d), lambda i: (0, 0))],
        out_shape=[jax.ShapeDtypeStruct((depth, d, nb), F32), jax.ShapeDtypeStruct((depth, 1, d3), F32),
                   jax.ShapeDtypeStruct((8, d), F32), jax.ShapeDtypeStruct((16, d), F32)],
        compiler_params=_params("arbitrary"),
    )(cond, ada_w, dm_all, dm_all, dm_mine, dm_mine)


def _norm_fwd(xs, g, mod, tr, seg_tiles, name):
    t, d = xs.shape

    def body(x_ref, g_ref, mod_ref, h_ref):
        x = x_ref[...]
        r = lax.rsqrt(jnp.mean(x * x, axis=-1, keepdims=True) + RMS_EPS)
        y = (x * r) * g_ref[...]
        h_ref[...] = (y * (1.0 + mod_ref[1:2, :]) + mod_ref[0:1, :]).astype(BF16)

    return _pc(
        body, name=name, grid=(t // tr,),
        in_specs=[pl.BlockSpec((tr, d), lambda i: (i, 0)), pl.BlockSpec((1, d), lambda i: (0, 0)),
                  pl.BlockSpec((None, 8, d), lambda i: (i // seg_tiles, 0, 0))],
        out_specs=pl.BlockSpec((tr, d), lambda i: (i, 0)),
        out_shape=jax.ShapeDtypeStruct((t, d), BF16),
        compiler_params=_params("parallel"),
    )(xs, g, mod)


def _norm_bwd(xs, dh, dres, g, mod, tr, seg_tiles, name):
    t, d = xs.shape
    nseg = mod.shape[0]

    def body(x_ref, dh_ref, dres_ref, g_ref, mod_ref, dx_ref, sum_ref):
        i = pl.program_id(0)
        x = x_ref[...]
        r = lax.rsqrt(jnp.mean(x * x, axis=-1, keepdims=True) + RMS_EPS)
        xn = x * r
        dhv = dh_ref[...]
        gain = g_ref[...]
        one_scale = 1.0 + mod_ref[1:2, :]
        dxn = dhv * (gain * one_scale)
        dx_ref[...] = dres_ref[...] + r * (dxn - xn * jnp.mean(dxn * xn, axis=-1, keepdims=True))

        @pl.when(i % seg_tiles == 0)
        def _():
            sum_ref[...] = jnp.zeros_like(sum_ref)

        sum_ref[0:1, :] += jnp.sum(dhv, axis=0, keepdims=True)
        sum_ref[1:2, :] += jnp.sum(dhv * (xn * gain), axis=0, keepdims=True)
        sum_ref[2:3, :] += jnp.sum(dhv * one_scale * xn, axis=0, keepdims=True)

    row = pl.BlockSpec((tr, d), lambda i: (i, 0))
    seg = pl.BlockSpec((None, 8, d), lambda i: (i // seg_tiles, 0, 0))
    return _pc(
        body, name=name, grid=(t // tr,),
        in_specs=[row, row, row, pl.BlockSpec((1, d), lambda i: (0, 0)), seg],
        out_specs=[row, seg],
        out_shape=[jax.ShapeDtypeStruct((t, d), F32), jax.ShapeDtypeStruct((nseg, 8, d), F32)],
        compiler_params=_params("arbitrary"),
    )(xs, dh, dres, g, mod)


def _resid_bwd(dx, yx, mod, tr, seg_tiles, name):
    t, d = yx.shape
    nseg = mod.shape[0]

    def body(dx_ref, yx_ref, mod_ref, dyx_ref, sum_ref):
        i = pl.program_id(0)
        dxv = dx_ref[...]
        dyx_ref[...] = (dxv * mod_ref[2:3, :]).astype(BF16)

        @pl.when(i % seg_tiles == 0)
        def _():
            sum_ref[...] = jnp.zeros_like(sum_ref)

        sum_ref[0:1, :] += jnp.sum(dxv * yx_ref[...], axis=0, keepdims=True)

    row = pl.BlockSpec((tr, d), lambda i: (i, 0))
    seg = pl.BlockSpec((None, 8, d), lambda i: (i // seg_tiles, 0, 0))
    return _pc(
        body, name=name, grid=(t // tr,),
        in_specs=[row, row, seg], out_specs=[row, seg],
        out_shape=[jax.ShapeDtypeStruct((t, d), BF16), jax.ShapeDtypeStruct((nseg, 8, d), F32)],
        compiler_params=_params("arbitrary"),
    )(dx, yx, mod)


def _loss_head(xs, target, g, tr):
    t, d = xs.shape

    def body(x_ref, t_ref, g_ref, loss_ref, dx_ref, dg_ref):
        i = pl.program_id(0)
        x = x_ref[...]
        r = lax.rsqrt(jnp.mean(x * x, axis=-1, keepdims=True) + RMS_EPS)
        xn = x * r
        gain = g_ref[...]
        err = xn * gain - t_ref[...]
        dy = err * (1.0 / d)
        dxn = dy * gain
        dx_ref[...] = r * (dxn - xn * jnp.mean(dxn * xn, axis=-1, keepdims=True))

        @pl.when(i == 0)
        def _():
            loss_ref[...] = jnp.zeros_like(loss_ref)
            dg_ref[...] = jnp.zeros_like(dg_ref)

        loss_ref[...] += 0.5 * jnp.sum(jnp.mean(err * err, axis=-1, keepdims=True))
        dg_ref[0:1, :] += jnp.sum(dy * xn, axis=0, keepdims=True)

    row = pl.BlockSpec((tr, d), lambda i: (i, 0))
    return _pc(
        body, name="loss_head", grid=(t // tr,),
        in_specs=[row, row, pl.BlockSpec((1, d), lambda i: (0, 0))],
        out_specs=[pl.BlockSpec((8, LANES), lambda i: (0, 0)), row, pl.BlockSpec((8, d), lambda i: (0, 0))],
        out_shape=[jax.ShapeDtypeStruct((8, LANES), F32), jax.ShapeDtypeStruct((t, d), F32),
                   jax.ShapeDtypeStruct((8, d), F32)],
        compiler_params=_params("arbitrary"),
    )(xs, target, g)


def _proj_in(h, w, layer, width, name, blocks=None, dtype=F32):
    t, d = h.shape
    n8 = w.shape[-1]
    first, count = blocks if blocks is not None else (0, N_DEV)
    per_part = width // n8
    tm = _tile(t, 1152)

    def body(a_ref, b_ref, o_ref):
        o_ref[...] = _dot(a_ref[...], b_ref[...]).astype(dtype)

    return _pc(
        body, name=name, grid=(t // tm, count),
        in_specs=[pl.BlockSpec((tm, d), lambda i, j: (i, 0)),
                  pl.BlockSpec((None, None, d, n8), lambda i, j: (first + j, layer, 0, 0))],
        out_specs=pl.BlockSpec((None, tm, n8), lambda i, j: (j // per_part, i, j % per_part)),
        out_shape=jax.ShapeDtypeStruct((count // per_part, t, width), dtype),
        compiler_params=_params("parallel", "parallel"),
    )(h, w)


def _proj_out(z, w, res, mod, tm, seg_tiles, name):
    t, k = z.shape
    d = w.shape[1]

    def body(z_ref, w_ref, res_ref, mod_ref, yx_ref, x_ref):
        yx = _dot(z_ref[...], w_ref[...])
        yx_ref[...] = yx
        x_ref[...] = res_ref[...] + mod_ref[2:3, :] * yx

    tile = pl.BlockSpec((tm, d), lambda i: (i, 0))
    return _pc(
        body, name=name, grid=(t // tm,),
        in_specs=[pl.BlockSpec((tm, k), lambda i: (i, 0)), pl.BlockSpec((k, d), lambda i: (0, 0)), tile,
                  pl.BlockSpec((None, 8, d), lambda i: (i // seg_tiles, 0, 0))],
        out_specs=[tile, tile],
        out_shape=[jax.ShapeDtypeStruct((t, d), F32), jax.ShapeDtypeStruct((t, d), F32)],
        compiler_params=_params("parallel"),
    )(z, w, res, mod)


def _proj_out_dz(dyx, w, name):
    t, d = dyx.shape
    width = w.shape[0]
    tm, tn = _tile(t, 1024), _tile(width, 512)

    def body(a_ref, w_ref, o_ref):
        o_ref[...] = _dot(a_ref[...], w_ref[...], 1, 1)

    return _pc(
        body, name=name, grid=(t // tm, width // tn),
        in_specs=[pl.BlockSpec((tm, d), lambda i, j: (i, 0)), pl.BlockSpec((tn, d), lambda i, j: (j, 0))],
        out_specs=pl.BlockSpec((tm, tn), lambda i, j: (i, j)),
        out_shape=jax.ShapeDtypeStruct((t, width), F32),
        compiler_params=_params("parallel", "parallel"),
    )(dyx, w)


def _proj_in_dh(dpre, w, layer, name):
    parts, t, width = dpre.shape
    d, n8 = w.shape[-2:]
    per_part = width // n8
    tm, tn = _tile(t, 1152), _tile(d, 512)

    def body(a_ref, w_ref, o_ref, acc_ref):
        k = pl.program_id(2)

        @pl.when(k == 0)
        def _():
            acc_ref[...] = jnp.zeros_like(acc_ref)

        acc_ref[...] += _dot(a_ref[...], w_ref[...], 1, 1)

        @pl.when(k == N_DEV - 1)
        def _():
            o_ref[...] = acc_ref[...]

    return _pc(
        body, name=name, grid=(t // tm, d // tn, N_DEV),
        in_specs=[pl.BlockSpec((None, tm, n8), lambda i, j, k: (k // per_part, i, k % per_part)),
                  pl.BlockSpec((None, None, tn, n8), lambda i, j, k: (k, layer, j, 0))],
        out_specs=pl.BlockSpec((tm, tn), lambda i, j, k: (i, j)),
        out_shape=jax.ShapeDtypeStruct((t, d), F32),
        scratch_shapes=[pltpu.VMEM((tm, tn), F32)],
        compiler_params=_params("parallel", "parallel", "arbitrary"),
    )(dpre, w)


def _transposed(a_ref):
    return a_ref[...].astype(F32).T.astype(BF16)


def _grad_w_in(h, dpre, n8, name):
    t, d = h.shape
    parts, _, width = dpre.shape
    per_part = width // n8
    tm, tk = _tile(d, 512), _tile(t, 384)

    def body(a_ref, b_ref, o_ref):
        @pl.when(pl.program_id(1) == 0)
        def _():
            o_ref[...] = jnp.zeros_like(o_ref)

        at = _transposed(a_ref)
        for p in range(parts):
            r = _dot(at, b_ref[p])
            for s in range(per_part):
                o_ref[p * per_part + s] += r[:, s * n8:(s + 1) * n8]

    return _pc(
        body, name=name, grid=(d // tm, t // tk),
        in_specs=[pl.BlockSpec((tk, tm), lambda i, k: (k, i)), pl.BlockSpec((parts, tk, width), lambda i, k: (0, k, 0))],
        out_specs=pl.BlockSpec((parts * per_part, tm, n8), lambda i, k: (0, i, 0)),
        out_shape=jax.ShapeDtypeStruct((parts * per_part, d, n8), F32),
        compiler_params=_params("parallel", "arbitrary"),
    )(h, dpre)


def _grad_w_out(z, dyx, name):
    width = z.shape[1]
    t, d = dyx.shape
    tm, tk = _tile(width, 512), _tile(t, 384)

    def body(a_ref, b_ref, o_ref):
        @pl.when(pl.program_id(1) == 0)
        def _():
            o_ref[...] = jnp.zeros_like(o_ref)

        o_ref[...] += _dot(_transposed(a_ref), b_ref[...])

    return _pc(
        body, name=name, grid=(width // tm, t // tk),
        in_specs=[pl.BlockSpec((tk, tm), lambda i, k: (k, i)), pl.BlockSpec((tk, d), lambda i, k: (k, 0))],
        out_specs=pl.BlockSpec((tm, d), lambda i, k: (i, 0)),
        out_shape=jax.ShapeDtypeStruct((width, d), F32),
        compiler_params=_params("parallel", "arbitrary"),
    )(z, dyx)


def _shift(v, k):
    n = v.shape[0]
    return pltpu.roll(v, k % n, 0)


def _window_sum(v, win):
    s = v + _shift(v, 1)
    step = 1
    while 2 * step < win:
        s = _shift(s, step) + _shift(s, -step)
        step *= 2
    return s


def _window_count(base, seg_len, win, shape):
    t = base + lax.broadcasted_iota(jnp.int32, shape, 0)
    hi = jnp.minimum(t + win // 2, seg_len)
    lo = jnp.maximum(t - win // 2, 0)
    return (hi - lo).astype(F32)


def _pad_offsets(segs):
    return [HALO * (s + 1) + st for s, (st, _) in enumerate(segs)]


def _for_chunks(segs, fn):
    offs = _pad_offsets(segs)
    for s, (st, ln) in enumerate(segs):
        def step(ci, carry, s=s, st=st, ln=ln):
            fn(s, st, ln, offs[s], pl.multiple_of(ci * CHUNK, CHUNK))
            return carry
        lax.fori_loop(0, ln // CHUNK, step, 0)


def _pool_fwd(pre, w_grp, scale, segs, name):
    _, t, width = pre.shape
    grp = width // len(POOL_WINDOWS)
    padded = t + HALO * (len(segs) + 1)

    def group(win, pre_ref, w_ref, sc_ref, z_ref, diff_ref, pad_ref):
        pad_ref[...] = jnp.zeros_like(pad_ref)

        def fill(s, st, ln, off, b):
            pad_ref[pl.ds(off + b, CHUNK), :] = pre_ref[0, pl.ds(st + b, CHUNK), :]

        _for_chunks(segs, fill)

        def mix(s, st, ln, off, b):
            ext = pad_ref[pl.ds(off - HALO + b, CHUNK + 2 * HALO), :]
            total = _window_sum(ext, win)[HALO:HALO + CHUNK]
            u = pre_ref[0, pl.ds(st + b, CHUNK), :]
            diff = (total / _window_count(b, ln, win, u.shape) - u).astype(BF16)
            mixed = _dot(diff, w_ref[...])
            gate = _silu(pre_ref[1, pl.ds(st + b, CHUNK), :])
            z_ref[pl.ds(st + b, CHUNK), :] = (mixed * sc_ref[...] * gate).astype(BF16)
            diff_ref[pl.ds(st + b, CHUNK), :] = diff

        _for_chunks(segs, mix)

    def body(pre_ref, w_ref, sc_ref, z_ref, diff_ref, pad_ref):
        gi = pl.program_id(0)
        for widx, win in enumerate(POOL_WINDOWS):
            @pl.when(gi == widx)
            def _(win=win):
                group(win, pre_ref, w_ref, sc_ref, z_ref, diff_ref, pad_ref)

    col = pl.BlockSpec((t, grp), lambda g: (0, g))
    return _pc(
        body, name=name, grid=(len(POOL_WINDOWS),),
        in_specs=[pl.BlockSpec((2, t, grp), lambda g: (0, 0, g)), pl.BlockSpec((None, grp, grp), lambda g: (g, 0, 0)),
                  pl.BlockSpec((1, grp), lambda g: (0, g))],
        out_specs=[col, col],
        out_shape=[jax.ShapeDtypeStruct((t, width), BF16), jax.ShapeDtypeStruct((t, width), BF16)],
        scratch_shapes=[pltpu.VMEM((padded, grp), F32)],
        compiler_params=_params("parallel"),
    )(pre, w_grp, scale)


def _pool_bwd(dz, diff, pre, w_grp, scale, segs, name):
    _, t, width = pre.shape
    grp = width // len(POOL_WINDOWS)
    padded = t + HALO * (len(segs) + 1)

    def group(win, dz_ref, diff_ref, pre_ref, w_ref, sc_ref, dpre_ref, dw_ref, dsc_ref, pad_ref, dd_ref):
        pad_ref[...] = jnp.zeros_like(pad_ref)
        dw_ref[...] = jnp.zeros_like(dw_ref)
        dsc_ref[...] = jnp.zeros_like(dsc_ref)

        def first(s, st, ln, off, b):
            rows = pl.ds(st + b, CHUNK)
            diff_v = diff_ref[rows, :]
            mixed = _dot(diff_v, w_ref[...])
            g = pre_ref[1, rows, :]
            sg = _silu(g)
            dzv = dz_ref[rows, :]
            dmixed = (dzv * sc_ref[...] * sg).astype(BF16)
            dsc_ref[...] += jnp.sum(dzv * mixed * sg, axis=0, keepdims=True)
            dpre_ref[1, rows, :] = (dzv * mixed * sc_ref[...] * _dsilu(g)).astype(BF16)
            ddiff = _dot(dmixed, w_ref[...], 1, 1)
            dw_ref[...] += _dot(diff_v, dmixed, 0, 0)
            dd_ref[rows, :] = ddiff
            pad_ref[pl.ds(off + b, CHUNK), :] = ddiff / _window_count(b, ln, win, ddiff.shape)

        _for_chunks(segs, first)

        def second(s, st, ln, off, b):
            rows = pl.ds(st + b, CHUNK)
            ext = pad_ref[pl.ds(off - HALO + b, CHUNK + 2 * HALO), :]
            total = _shift(_window_sum(ext, win), -1)[HALO:HALO + CHUNK]
            dpre_ref[0, rows, :] = (total - dd_ref[rows, :]).astype(BF16)

        _for_chunks(segs, second)

    def body(dz_ref, diff_ref, pre_ref, w_ref, sc_ref, dpre_ref, dw_ref, dsc_ref, pad_ref, dd_ref):
        gi = pl.program_id(0)
        for widx, win in enumerate(POOL_WINDOWS):
            @pl.when(gi == widx)
            def _(win=win):
                group(win, dz_ref, diff_ref, pre_ref, w_ref, sc_ref, dpre_ref, dw_ref, dsc_ref, pad_ref, dd_ref)

    col = pl.BlockSpec((t, grp), lambda g: (0, g))
    both = pl.BlockSpec((2, t, grp), lambda g: (0, 0, g))
    wspec = pl.BlockSpec((None, grp, grp), lambda g: (g, 0, 0))
    sspec = pl.BlockSpec((1, grp), lambda g: (0, g))
    return _pc(
        body, name=name, grid=(len(POOL_WINDOWS),),
        in_specs=[col, col, both, wspec, sspec],
        out_specs=[both, wspec, sspec],
        out_shape=[jax.ShapeDtypeStruct((2, t, width), BF16), jax.ShapeDtypeStruct((len(POOL_WINDOWS), grp, grp), F32),
                   jax.ShapeDtypeStruct((1, width), F32)],
        scratch_shapes=[pltpu.VMEM((padded, grp), F32), pltpu.VMEM((t, grp), F32)],
        compiler_params=_params("parallel"),
    )(dz, diff, pre, w_grp, scale)


def _conv_fwd(pre, dw, db, name):
    _, t, width = pre.shape
    cb = LANES
    segs = [(0, t)]

    def body(pre_ref, dw_ref, db_ref, z_ref, pad_ref):
        pad_ref[...] = jnp.zeros_like(pad_ref)

        def fill(s, st, ln, off, b):
            rows = pl.ds(b, CHUNK)
            pad_ref[pl.ds(off + b, CHUNK), :] = pre_ref[1, rows, :] * pre_ref[2, rows, :]

        _for_chunks(segs, fill)

        def mix(s, st, ln, off, b):
            rows = pl.ds(b, CHUNK)
            ext = pad_ref[pl.ds(off - HALO + b, CHUNK + 2 * HALO), :]
            conv = (dw_ref[0:1, :] * _shift(ext, 1) + dw_ref[1:2, :] * ext + dw_ref[2:3, :] * _shift(ext, -1))
            conv = conv[HALO:HALO + CHUNK] + db_ref[...]
            y = pre_ref[0, rows, :] * conv
            z_ref[rows, :] = (y * _silu(pre_ref[3, rows, :])).astype(BF16)

        _for_chunks(segs, mix)

    return _pc(
        body, name=name, grid=(width // cb,),
        in_specs=[pl.BlockSpec((4, t, cb), lambda j: (0, 0, j)), pl.BlockSpec((8, cb), lambda j: (0, j)),
                  pl.BlockSpec((1, cb), lambda j: (0, j))],
        out_specs=pl.BlockSpec((t, cb), lambda j: (0, j)),
        out_shape=jax.ShapeDtypeStruct((t, width), BF16),
        scratch_shapes=[pltpu.VMEM((t + 2 * HALO, cb), F32)],
        compiler_params=_params("parallel"),
    )(pre, dw, db)


def _conv_bwd(dz, pre, dw, db, name):
    _, t, width = pre.shape
    cb = LANES
    segs = [(0, t)]

    def body(dz_ref, pre_ref, dw_ref, db_ref, dpre_ref, ddw_ref, ddb_ref, pad_a, pad_c):
        pad_a[...] = jnp.zeros_like(pad_a)
        pad_c[...] = jnp.zeros_like(pad_c)
        ddw_ref[...] = jnp.zeros_like(ddw_ref)
        ddb_ref[...] = jnp.zeros_like(ddb_ref)

        def fill(s, st, ln, off, b):
            rows = pl.ds(b, CHUNK)
            pad_a[pl.ds(off + b, CHUNK), :] = pre_ref[1, rows, :] * pre_ref[2, rows, :]

        _for_chunks(segs, fill)

        def first(s, st, ln, off, b):
            rows = pl.ds(b, CHUNK)
            ext = pad_a[pl.ds(off - HALO + b, CHUNK + 2 * HALO), :]
            prev, nxt = _shift(ext, 1)[HALO:HALO + CHUNK], _shift(ext, -1)[HALO:HALO + CHUNK]
            here = ext[HALO:HALO + CHUNK]
            conv = dw_ref[0:1, :] * prev + dw_ref[1:2, :] * here + dw_ref[2:3, :] * nxt + db_ref[...]
            bg, g = pre_ref[0, rows, :], pre_ref[3, rows, :]
            dzv = dz_ref[rows, :]
            dy = dzv * _silu(g)
            dpre_ref[3, rows, :] = (dzv * (bg * conv) * _dsilu(g)).astype(BF16)
            dpre_ref[0, rows, :] = (dy * conv).astype(BF16)
            dconv = dy * bg
            pad_c[pl.ds(off + b, CHUNK), :] = dconv
            ddw_ref[0:1, :] += jnp.sum(dconv * prev, axis=0, keepdims=True)
            ddw_ref[1:2, :] += jnp.sum(dconv * here, axis=0, keepdims=True)
            ddw_ref[2:3, :] += jnp.sum(dconv * nxt, axis=0, keepdims=True)
            ddb_ref[0:1, :] += jnp.sum(dconv, axis=0, keepdims=True)

        _for_chunks(segs, first)

        def second(s, st, ln, off, b):
            rows = pl.ds(b, CHUNK)
            ext = pad_c[pl.ds(off - HALO + b, CHUNK + 2 * HALO), :]
            da = (dw_ref[0:1, :] * _shift(ext, -1) + dw_ref[1:2, :] * ext + dw_ref[2:3, :] * _shift(ext, 1))
            da = da[HALO:HALO + CHUNK]
            dpre_ref[1, rows, :] = (da * pre_ref[2, rows, :]).astype(BF16)
            dpre_ref[2, rows, :] = (da * pre_ref[1, rows, :]).astype(BF16)

        _for_chunks(segs, second)

    quad = pl.BlockSpec((4, t, cb), lambda j: (0, 0, j))
    rows8 = pl.BlockSpec((8, cb), lambda j: (0, j))
    return _pc(
        body, name=name, grid=(width // cb,),
        in_specs=[pl.BlockSpec((t, cb), lambda j: (0, j)), quad, rows8, pl.BlockSpec((1, cb), lambda j: (0, j))],
        out_specs=[quad, rows8, rows8],
        out_shape=[jax.ShapeDtypeStruct((4, t, width), BF16), jax.ShapeDtypeStruct((8, width), F32),
                   jax.ShapeDtypeStruct((8, width), F32)],
        scratch_shapes=[pltpu.VMEM((t + 2 * HALO, cb), F32), pltpu.VMEM((t + 2 * HALO, cb), F32)],
        compiler_params=_params("parallel"),
    )(dz, pre, dw, db)


PAIR_TILES = 2 * WIN_ROWS - 2


def _pair_geometry():
    lane = lax.broadcasted_iota(jnp.int32, (GRID_W, LANES), 1)
    qcol = lax.broadcasted_iota(jnp.int32, (GRID_W, LANES), 0)
    low = lane < GRID_W
    kcol = jnp.where(low, lane, lane - GRID_W)
    start = jnp.clip(qcol - WIN_COLS // 2, 0, GRID_W - WIN_COLS)
    inside = (kcol >= start) & (kcol < start + WIN_COLS)
    return low, qcol, inside


def _bias_tiles(rpb_ref, tiles_ref, low, inside):
    for h in range(2):
        for t in range(PAIR_TILES):
            a = jnp.broadcast_to(rpb_ref[h, t:t + 1, :], (GRID_W, LANES))
            b = jnp.broadcast_to(rpb_ref[h, t + 1:t + 2, :], (GRID_W, LANES))
            a = pltpu.roll(pltpu.roll(a, LANES - (WIN_COLS - 1), 1), 0, 1, stride=1, stride_axis=0)
            b = pltpu.roll(pltpu.roll(b, GRID_W - (WIN_COLS - 1), 1), 0, 1, stride=1, stride_axis=0)
            tiles_ref[h, t] = jnp.where(inside, jnp.where(low, a, b), MASKED)


def _bias_tiles_grad(dtiles_ref, drpb_ref, qcol):
    for h in range(2):
        sums = []
        for t in range(PAIR_TILES):
            v = dtiles_ref[h, t]
            v = pltpu.roll(v, WIN_COLS - 1, 1)
            for bit in range(6):
                v = jnp.where((qcol >> bit) & 1 == 1, pltpu.roll(v, LANES - (1 << bit), 1), v)
            sums.append(jnp.sum(v, axis=0, keepdims=True))
        zero = jnp.zeros((1, LANES), F32)
        lane = lax.broadcasted_iota(jnp.int32, (1, LANES), 1)
        for r in range(2 * WIN_ROWS):
            here = sums[r] if r < PAIR_TILES else zero
            prev = pltpu.roll(sums[r - 1], GRID_W, 1) if 1 <= r <= PAIR_TILES else zero
            drpb_ref[h, r:r + 1, :] = jnp.where(lane < 2 * WIN_COLS - 1, here + prev, 0.0)


def _attn_rows(r, n_rows):
    first = jnp.clip(r - WIN_ROWS // 2, 0, n_rows - WIN_ROWS)
    return first, first - r + WIN_ROWS - 1


def _attn_probs(q, k_blk, k_ctx, bias):
    s_loc = _dot(q, k_blk, 1, 1) + bias
    s_ctx = _dot(q, k_ctx, 1, 1)
    m = jnp.maximum(jnp.max(s_loc, axis=-1, keepdims=True), jnp.max(s_ctx, axis=-1, keepdims=True))
    e_loc, e_ctx = jnp.exp(s_loc - m), jnp.exp(s_ctx - m)
    inv = 1.0 / (jnp.sum(e_loc, axis=-1, keepdims=True) + jnp.sum(e_ctx, axis=-1, keepdims=True))
    return e_loc * inv, e_ctx * inv


def _pair_bias(tiles_ref, h, j):
    return jnp.concatenate([tiles_ref[h, j + 2 * m] for m in range(WIN_ROWS // 2)], axis=1)


def _attn_fwd(qkv, gate, rpb, seq):
    _, t, width = qkv.shape
    n_rows = seq // GRID_W
    n_ctx = t - seq
    blk = WIN_ROWS * GRID_W

    def body(q_ref, k_ref, v_ref, g_ref, rpb_ref, z_ref, o_ref, tiles_ref):
        low, _, inside = _pair_geometry()
        _bias_tiles(rpb_ref, tiles_ref, low, inside)
        ctx = pl.ds(seq, n_ctx)

        def step(r, carry):
            first, j = _attn_rows(r, n_rows)
            rows = pl.ds(pl.multiple_of(r * GRID_W, GRID_W), GRID_W)
            keys = pl.ds(pl.multiple_of(first * GRID_W, GRID_W), blk)
            q = (q_ref[rows, :].astype(F32) * HEAD_DIM ** -0.5).astype(BF16)
            zero = jnp.zeros_like(q)
            outs = []
            for h, qh in enumerate((jnp.where(low, q, zero), jnp.where(low, zero, q))):
                p_loc, p_ctx = _attn_probs(qh, k_ref[keys, :], k_ref[ctx, :], _pair_bias(tiles_ref, h, j))
                outs.append(_dot(p_loc.astype(BF16), v_ref[keys, :]) + _dot(p_ctx.astype(BF16), v_ref[ctx, :]))
            o = jnp.where(low, outs[0], outs[1])
            o_ref[rows, :] = o
            z_ref[rows, :] = (o * _silu(g_ref[rows, :])).astype(BF16)
            return carry

        lax.fori_loop(0, n_rows, step, 0)

    def part(p):
        return pl.BlockSpec((None, t, LANES), lambda h: (p, 0, h))

    out = pl.BlockSpec((seq, LANES), lambda h: (0, h))
    return _pc(
        body, name="attn_fwd", grid=(width // LANES,),
        in_specs=[part(0), part(1), part(2), part(0), pl.BlockSpec((2, 2 * WIN_ROWS, LANES), lambda h: (h, 0, 0))],
        out_specs=[out, out],
        out_shape=[jax.ShapeDtypeStruct((seq, width), BF16), jax.ShapeDtypeStruct((seq, width), F32)],
        scratch_shapes=[pltpu.VMEM((2, PAIR_TILES, GRID_W, LANES), F32)],
        compiler_params=_params("parallel"),
    )(qkv, qkv, qkv, gate, rpb)


def _attn_bwd(qkv, gate, o, dz, rpb, seq):
    _, t, width = qkv.shape
    n_rows = seq // GRID_W
    n_ctx = t - seq
    blk = WIN_ROWS * GRID_W
    heads = 2 * width // LANES

    def body(q_ref, k_ref, v_ref, g_ref, o_ref, dz_ref, rpb_ref, dpre_ref, drpb_ref, tiles_ref, dtiles_ref, dk_ref, dv_ref):
        low, qcol, inside = _pair_geometry()
        _bias_tiles(rpb_ref, tiles_ref, low, inside)
        dtiles_ref[...] = jnp.zeros_like(dtiles_ref)
        dk_ref[...] = jnp.zeros_like(dk_ref)
        dv_ref[...] = jnp.zeros_like(dv_ref)
        ctx = pl.ds(seq, n_ctx)

        def step(r, carry):
            first, j = _attn_rows(r, n_rows)
            rows = pl.ds(pl.multiple_of(r * GRID_W, GRID_W), GRID_W)
            keys = pl.ds(pl.multiple_of(first * GRID_W, GRID_W), blk)
            g = g_ref[rows, :]
            dzv = dz_ref[rows, :]
            dpre_ref[3, rows, :] = (dzv * o_ref[rows, :] * _dsilu(g)).astype(BF16)
            d_o = (dzv * _silu(g)).astype(BF16)
            q = (q_ref[rows, :].astype(F32) * HEAD_DIM ** -0.5).astype(BF16)
            zero = jnp.zeros_like(q)
            k_blk, k_ctx, v_blk, v_ctx = k_ref[keys, :], k_ref[ctx, :], v_ref[keys, :], v_ref[ctx, :]
            dqs = []
            for h in range(2):
                qh = jnp.where(low, q, zero) if h == 0 else jnp.where(low, zero, q)
                doh = jnp.where(low, d_o, zero) if h == 0 else jnp.where(low, zero, d_o)
                p_loc, p_ctx = _attn_probs(qh, k_blk, k_ctx, _pair_bias(tiles_ref, h, j))
                dp_loc = _dot(doh, v_blk, 1, 1)
                dp_ctx = _dot(doh, v_ctx, 1, 1)
                delta = (jnp.sum(p_loc * dp_loc, axis=-1, keepdims=True)
                         + jnp.sum(p_ctx * dp_ctx, axis=-1, keepdims=True))
                ds_loc = p_loc * (dp_loc - delta)
                ds_ctx = p_ctx * (dp_ctx - delta)
                ds_loc_b, ds_ctx_b = ds_loc.astype(BF16), ds_ctx.astype(BF16)
                dqs.append(_dot(ds_loc_b, k_blk) + _dot(ds_ctx_b, k_ctx))
                dk_ref[keys, :] += _dot(ds_loc_b, qh, 0, 0)
                dk_ref[ctx, :] += _dot(ds_ctx_b, qh, 0, 0)
                dv_ref[keys, :] += _dot(p_loc.astype(BF16), doh, 0, 0)
                dv_ref[ctx, :] += _dot(p_ctx.astype(BF16), doh, 0, 0)
                for m in range(WIN_ROWS // 2):
                    dtiles_ref[h, j + 2 * m] += ds_loc[:, m * LANES:(m + 1) * LANES]
            dpre_ref[0, rows, :] = (jnp.where(low, dqs[0], dqs[1]) * HEAD_DIM ** -0.5).astype(BF16)
            return carry

        lax.fori_loop(0, n_rows, step, 0)
        dpre_ref[1] = dk_ref[...].astype(BF16)
        dpre_ref[2] = dv_ref[...].astype(BF16)
        dpre_ref[0, ctx, :] = jnp.zeros((n_ctx, LANES), BF16)
        dpre_ref[3, ctx, :] = jnp.zeros((n_ctx, LANES), BF16)
        _bias_tiles_grad(dtiles_ref, drpb_ref, qcol)

    def part(p):
        return pl.BlockSpec((None, t, LANES), lambda h: (p, 0, h))

    lat = pl.BlockSpec((seq, LANES), lambda h: (0, h))
    rspec = pl.BlockSpec((2, 2 * WIN_ROWS, LANES), lambda h: (h, 0, 0))
    tiles = pltpu.VMEM((2, PAIR_TILES, GRID_W, LANES), F32)
    return _pc(
        body, name="attn_bwd", grid=(width // LANES,),
        in_specs=[part(0), part(1), part(2), part(0), lat, lat, rspec],
        out_specs=[pl.BlockSpec((4, t, LANES), lambda h: (0, 0, h)), rspec],
        out_shape=[jax.ShapeDtypeStruct((4, t, width), BF16), jax.ShapeDtypeStruct((heads, 2 * WIN_ROWS, LANES), F32)],
        scratch_shapes=[tiles, tiles, pltpu.VMEM((t, LANES), F32), pltpu.VMEM((t, LANES), F32)],
        compiler_params=_params("parallel"),
    )(qkv, qkv, qkv, gate, o, dz, rpb)


def _adamw(w, m, v, parts, name, mult=None):
    rows, cols = w.shape
    tr = _tile(rows, max(8, 131072 // cols), 8)
    n_parts = len(parts)
    c1 = 1.0 - ADAM_B1 ** ADAM_STEP
    c2 = 1.0 - ADAM_B2 ** ADAM_STEP

    def body(*refs):
        w_ref, m_ref, v_ref = refs[:3]
        part_refs = refs[3:3 + n_parts]
        rest = refs[3 + n_parts:]
        g = part_refs[0][...].astype(F32)
        for p in part_refs[1:]:
            g = g + p[...].astype(F32)
        if mult is not None:
            g = g * rest[0][...]
            rest = rest[1:]
        g_ref, d_ref, nm_ref, nv_ref = rest
        m2 = ADAM_B1 * m_ref[...] + (1.0 - ADAM_B1) * g
        v2 = ADAM_B2 * v_ref[...] + (1.0 - ADAM_B2) * (g * g)
        m_hat = m2 / c1
        v_hat = v2 / c2
        g_ref[...] = g
        d_ref[...] = -ADAM_LR * (m_hat / (jnp.sqrt(v_hat) + ADAM_EPS) + ADAM_WD * w_ref[...])
        nm_ref[...] = m2
        nv_ref[...] = v2

    tile = pl.BlockSpec((tr, cols), lambda i: (i, 0))
    in_specs, args = [tile, tile, tile], [w, m, v]
    for p in parts:
        if isinstance(p, tuple):
            arr, k = p
            in_specs.append(pl.BlockSpec((None, tr, cols), lambda i, k=k: (k, i, 0)))
            args.append(arr)
        else:
            in_specs.append(tile)
            args.append(p)
    if mult is not None:
        in_specs.append(tile)
        args.append(mult)
    shape = jax.ShapeDtypeStruct((rows, cols), F32)
    return _pc(
        body, name=name, grid=(rows // tr,), in_specs=in_specs, out_specs=[tile] * 4, out_shape=[shape] * 4,
        compiler_params=_params("parallel"),
    )(*args)


def _rows128(a):
    flat = a.reshape(-1)
    pad = (-flat.shape[0]) % LANES
    if pad:
        flat = jnp.concatenate([flat, jnp.zeros((pad,), flat.dtype)])
    return flat.reshape(-1, LANES)


def _pad_rows(a, mult=8):
    pad = (-a.shape[0]) % mult
    if pad:
        a = jnp.concatenate([a, jnp.zeros((pad,) + a.shape[1:], a.dtype)], axis=0)
    return a


def kernel(x, c, ctx, c_ctx, norm_g, ada_w, ada_b, pool_w_in, pool_w_grp, pool_scale, pool_w_out, na_w_in, na_rpb, na_w_out, conv_w_in, conv_dw, conv_db, conv_w_out, final_g, loss_target, m_c_ctx, m_norm_g, m_ada_w, m_ada_b, m_pool_w_in, m_pool_w_grp, m_pool_scale, m_pool_w_out, m_na_w_in, m_na_rpb, m_na_w_out, m_conv_w_in, m_conv_dw, m_conv_db, m_conv_w_out, m_final_g, v_c_ctx, v_norm_g, v_ada_w, v_ada_b, v_pool_w_in, v_pool_w_grp, v_pool_scale, v_pool_w_out, v_na_w_in, v_na_rpb, v_na_w_out, v_conv_w_in, v_conv_dw, v_conv_db, v_conv_w_out, v_final_g):
    xi, yi, ci = _my_place()
    me = 4 * xi + 2 * yi + ci
    seq, d = x.shape[1], x.shape[2]
    n_ctx = ctx.shape[1]
    t_all = seq + n_ctx
    width = d
    depth = norm_g.shape[0]
    nb = ada_w.shape[2]
    shard = width // N_DEV
    d_rows = d // LANES
    tr = math.gcd(math.gcd(seq, n_ctx), 256)
    x_tiles = seq // tr

    small_in = _pad_rows(jnp.concatenate([_rows128(c), pool_scale, conv_dw[0], conv_db], axis=0))
    got = _gather_small(small_in, "gather_inputs")
    r0 = d_rows
    c_all = got[:, :r0].reshape(N_DEV, d)
    n_pool = pool_scale.shape[0]
    scale_full = got[:, r0:r0 + n_pool].transpose(1, 0, 2).reshape(n_pool, width)
    r1 = r0 + n_pool
    taps_full = _pad_rows(got[:, r1:r1 + 3].transpose(1, 0, 2).reshape(3, width))
    bias_full = got[:, r1 + 3:r1 + 4].transpose(1, 0, 2).reshape(1, width)

    cond = jnp.concatenate([c_all, c_ctx[None], jnp.zeros((7, d), F32)], axis=0)
    bias_mine = lax.dynamic_slice(ada_b, (0, me * nb), (depth, nb))
    mod_mine = _mod_fwd(cond, ada_w, bias_mine)
    mod_all = _gather_small(mod_mine.reshape(-1, LANES), "gather_mod")
    mod_all = mod_all.reshape(N_DEV, depth, 16, nb).transpose(1, 2, 0, 3).reshape(depth, 16, 3 * d)
    mod_x = lax.dynamic_index_in_dim(mod_all, me, 1, keepdims=False).reshape(depth, 3, d)
    mod_c = mod_all[:, 8].reshape(depth, 3, d)
    pad5 = jnp.zeros((depth, 5, d), F32)
    mod_x = jnp.concatenate([mod_x, pad5], axis=1)
    mod_c = jnp.concatenate([mod_c, pad5], axis=1)
    mods = [jnp.stack([mod_x[i], mod_c[i]]) if i < 2 else mod_x[i][None] for i in range(depth)]

    gathered = _allgather_big(
        [w.astype(BF16) for w in (pool_w_in, pool_w_grp, pool_w_out, na_w_in, na_w_out, conv_w_in, conv_w_out)],
        "gather_weights")
    pool_in_w, pool_grp_w, pool_out_w, na_in_w, na_out_w, conv_in_w, conv_out_w = gathered
    n_grp = pool_w_grp.shape[1]
    grp = width // n_grp
    pool_grp_w = pool_grp_w.transpose(1, 2, 0, 3, 4).reshape(n_pool, n_grp, grp, grp)
    pool_out_w = pool_out_w.transpose(1, 0, 2, 3).reshape(n_pool, width, d)
    na_out_w = na_out_w.reshape(width, d)
    conv_out_w = conv_out_w.reshape(width, d)

    both = [(0, seq), (seq, n_ctx)]
    latent = [(0, seq)]

    xs0 = jnp.concatenate([x[0], ctx[0]], axis=0)
    h0 = _norm_fwd(xs0, norm_g[0:1], mods[0], tr, x_tiles, "norm_fwd0")
    pre0 = _proj_in(h0, pool_in_w, 0, width, "proj_in0")
    z0, diff0 = _pool_fwd(pre0, pool_grp_w[0], scale_full[0:1], both, "pool_fwd0")
    yx0, xs1 = _proj_out(z0, pool_out_w[0], xs0, mods[0], tr, x_tiles, "proj_out0")

    h1 = _norm_fwd(xs1, norm_g[1:2], mods[1], tr, x_tiles, "norm_fwd1")
    per_part = width // na_w_in.shape[2]
    qkv1 = _proj_in(h1, na_in_w, 0, width, "proj_in1_qkv", blocks=(0, 3 * per_part), dtype=BF16)
    gpre1 = _proj_in(h1, na_in_w, 0, width, "proj_in1_gate", blocks=(3 * per_part, per_part))
    rpb_rows = jnp.pad(na_rpb[0], ((0, 0), (0, 2 * WIN_ROWS - na_rpb.shape[2]), (0, LANES - na_rpb.shape[3])))
    z1, o1 = _attn_fwd(qkv1, gpre1, rpb_rows, seq)
    yx1, x2 = _proj_out(z1, na_out_w, xs1, mods[1], tr, x_tiles, "proj_out1")

    h2 = _norm_fwd(x2, norm_g[2:3], mods[2], tr, x_tiles, "norm_fwd2")
    pre2 = _proj_in(h2, conv_in_w, 0, width, "proj_in2")
    z2 = _conv_fwd(pre2, taps_full, bias_full, "conv_fwd")
    yx2, x3 = _proj_out(z2, conv_out_w, x2, mods[2], tr, x_tiles, "proj_out2")

    h3 = _norm_fwd(x3, norm_g[3:4], mods[3], tr, x_tiles, "norm_fwd3")
    pre3 = _proj_in(h3, pool_in_w, 1, width, "proj_in3")
    z3, diff3 = _pool_fwd(pre3, pool_grp_w[1], scale_full[1:2], latent, "pool_fwd3")
    yx3, x4 = _proj_out(z3, pool_out_w[1], x3, mods[3], tr, x_tiles, "proj_out3")

    loss_part, dx4, d_final = _loss_head(x4, loss_target[0], final_g[None], tr)
    loss = lax.psum(loss_part[0, 0], ("x", "y", "c"))

    dyx3, gate3 = _resid_bwd(dx4, yx3, mods[3], tr, x_tiles, "resid_bwd3")
    dz3 = _proj_out_dz(dyx3, pool_out_w[1], "proj_out_dz3")
    g_pool_out1 = _grad_w_out(z3, dyx3, "grad_w_out3")
    dpre3, g_grp1, g_scale1 = _pool_bwd(dz3, diff3, pre3, pool_grp_w[1], scale_full[1:2], latent, "pool_bwd3")
    dh3 = _proj_in_dh(dpre3, pool_in_w, 1, "proj_in_dh3")
    g_pool_in1 = _grad_w_in(h3, dpre3, pool_w_in.shape[2], "grad_w_in3")
    dx3, norm3 = _norm_bwd(x3, dh3, dx4, norm_g[3:4], mods[3], tr, x_tiles, "norm_bwd3")

    dyx2, gate2 = _resid_bwd(dx3, yx2, mods[2], tr, x_tiles, "resid_bwd2")
    dz2 = _proj_out_dz(dyx2, conv_out_w, "proj_out_dz2")
    g_conv_out = _grad_w_out(z2, dyx2, "grad_w_out2")
    dpre2, g_taps, g_cbias = _conv_bwd(dz2, pre2, taps_full, bias_full, "conv_bwd")
    dh2 = _proj_in_dh(dpre2, conv_in_w, 0, "proj_in_dh2")
    g_conv_in = _grad_w_in(h2, dpre2, conv_w_in.shape[2], "grad_w_in2")
    dx2, norm2 = _norm_bwd(x2, dh2, dx3, norm_g[2:3], mods[2], tr, x_tiles, "norm_bwd2")

    dyx1, gate1 = _resid_bwd(dx2, yx1, mods[1][:1], tr, x_tiles, "resid_bwd1")
    dz1 = _proj_out_dz(dyx1, na_out_w, "proj_out_dz1")
    g_na_out = _grad_w_out(z1, dyx1, "grad_w_out1")
    dpre1, g_rpb = _attn_bwd(qkv1, gpre1, o1, dz1, rpb_rows, seq)
    g_rpb = g_rpb[:, :na_rpb.shape[2], :na_rpb.shape[3]]
    dh1 = _proj_in_dh(dpre1, na_in_w, 0, "proj_in_dh1")
    g_na_in = _grad_w_in(h1, dpre1, na_w_in.shape[2], "grad_w_in1")
    dres1 = jnp.concatenate([dx2, jnp.zeros((n_ctx, d), F32)], axis=0)
    dxs1, norm1 = _norm_bwd(xs1, dh1, dres1, norm_g[1:2], mods[1], tr, x_tiles, "norm_bwd1")

    dyx0, gate0 = _resid_bwd(dxs1, yx0, mods[0], tr, x_tiles, "resid_bwd0")
    dz0 = _proj_out_dz(dyx0, pool_out_w[0], "proj_out_dz0")
    g_pool_out0 = _grad_w_out(z0, dyx0, "grad_w_out0")
    dpre0, g_grp0, g_scale0 = _pool_bwd(dz0, diff0, pre0, pool_grp_w[0], scale_full[0:1], both, "pool_bwd0")
    dh0 = _proj_in_dh(dpre0, pool_in_w, 0, "proj_in_dh0")
    g_pool_in0 = _grad_w_in(h0, dpre0, pool_w_in.shape[2], "grad_w_in0")
    dxs0, norm0 = _norm_bwd(xs0, dh0, dxs1, norm_g[0:1], mods[0], tr, x_tiles, "norm_bwd0")
    grad_x = dxs0[:seq][None]

    norms, gates = [norm0, norm1, norm2, norm3], [gate0, gate1, gate2, gate3]
    zero_d = jnp.zeros((d,), F32)
    dm_rows = [jnp.concatenate([norms[i][0, 0], norms[i][0, 1], gates[i][0, 0]]) for i in range(depth)]
    dm_rows.append(jnp.concatenate([norm0[1, 0], norm0[1, 1], gate0[1, 0]]))
    dm_rows.append(jnp.concatenate([norm1[1, 0], norm1[1, 1], zero_d]))
    dm_local = jnp.stack(dm_rows + [jnp.zeros((3 * d,), F32)] * 2)
    g_norm_part = jnp.stack([norm0[0, 2] + norm0[1, 2], norm1[0, 2] + norm1[1, 2], norm2[0, 2], norm3[0, 2]])
    g_scale_part = jnp.concatenate([g_scale0, g_scale1], axis=0)
    pieces = [_rows128(dm_local), _rows128(g_norm_part), _rows128(d_final[0]), _pad_rows(_rows128(g_rpb)),
              _rows128(g_scale_part), _rows128(g_taps[:3]), _rows128(g_cbias[0])]
    small_out = _gather_small(jnp.concatenate(pieces, axis=0), "gather_small_grads")
    marks = np.cumsum([0] + [p.shape[0] for p in pieces])

    def piece(k):
        return small_out[:, marks[k]:marks[k + 1]]

    dm_all = piece(0).reshape(N_DEV, 8, 3 * d).transpose(1, 0, 2)
    dm_mine = lax.dynamic_slice(dm_all, (0, 0, me * nb), (8, N_DEV, nb))
    g_ada_w, g_ada_b, cctx_part, dsilu_cond = _mod_bwd(cond, ada_w, dm_all, dm_mine)
    cctx_all = _gather_small(_rows128(cctx_part[0]), "gather_cctx")

    def my_shard(a, n):
        a = a.reshape(N_DEV, n, N_DEV, shard)
        return lax.dynamic_index_in_dim(a, me, 2, keepdims=False)

    rpb_rows = marks[4] - marks[3]
    zeros7 = lambda r: jnp.zeros((N_DEV - 1, r, LANES), F32)
    ada_b_rows = _rows128(g_ada_b)
    small_parts = jnp.concatenate([
        cctx_all, piece(1), jnp.concatenate([ada_b_rows[None], zeros7(ada_b_rows.shape[0])], axis=0), piece(2), piece(3),
        my_shard(piece(4), n_pool), my_shard(piece(5), 3), my_shard(piece(6), 1)], axis=1)
    n_small = small_parts.shape[1]
    small_parts = jnp.concatenate([small_parts, jnp.zeros((N_DEV, (-n_small) % 8, LANES), F32)], axis=1)

    def pack(c_ctx_, norm_g_, ada_b_, final_g_, rpb_, scale_, taps_, cbias_):
        rows = [_rows128(c_ctx_), _rows128(norm_g_), _rows128(ada_b_), _rows128(final_g_), _pad_rows(_rows128(rpb_)),
                scale_, taps_[0], cbias_]
        return _pad_rows(jnp.concatenate(rows, axis=0)), np.cumsum([0] + [r.shape[0] for r in rows])

    w_small, smarks = pack(c_ctx, norm_g, ada_b, final_g, na_rpb, pool_scale, conv_dw, conv_db)
    m_small, _ = pack(m_c_ctx, m_norm_g, m_ada_b, m_final_g, m_na_rpb, m_pool_scale, m_conv_dw, m_conv_db)
    v_small, _ = pack(v_c_ctx, v_norm_g, v_ada_b, v_final_g, v_na_rpb, v_pool_scale, v_conv_dw, v_conv_db)
    mult = jnp.concatenate([_rows128(dsilu_cond[8]), jnp.ones((w_small.shape[0] - d_rows, LANES), F32)], axis=0)
    small_res = _adamw(w_small, m_small, v_small, [(small_parts, k) for k in range(N_DEV)], "adamw_small", mult=mult)

    def unpack(k, like):
        out = []
        for r in small_res:
            flat = r[smarks[k]:smarks[k + 1]].reshape(-1)
            out.append(flat[:like.size].reshape(like.shape))
        return out

    res = {"c_ctx": unpack(0, c_ctx), "norm_g": unpack(1, norm_g), "ada_b": unpack(2, ada_b),
           "final_g": unpack(3, final_g), "na_rpb": unpack(4, na_rpb), "pool_scale": unpack(5, pool_scale),
           "conv_dw": unpack(6, conv_dw), "conv_db": unpack(7, conv_db)}

    res["ada_w"] = [r.reshape(ada_w.shape) for r in _adamw(
        ada_w.reshape(-1, nb), m_ada_w.reshape(-1, nb), v_ada_w.reshape(-1, nb), [g_ada_w.reshape(-1, nb)], "adamw_ada_w")]

    def grp_slots(g):
        return g.reshape(n_grp, N_DEV, grp // N_DEV, grp).transpose(1, 0, 2, 3).reshape(N_DEV, -1, grp)

    full = [g_pool_in0, g_pool_in1, grp_slots(g_grp0), grp_slots(g_grp1),
            g_pool_out0.reshape(N_DEV, shard, d), g_pool_out1.reshape(N_DEV, shard, d),
            g_na_in, g_na_out.reshape(N_DEV, shard, d), g_conv_in, g_conv_out.reshape(N_DEV, shard, d)]
    from_sibling = _rs_between_cores(full, "rs_cores")
    chip = 2 * xi + yi
    slots = jnp.stack([chip, chip ^ 2, chip ^ 1, chip ^ 3, ci]).astype(jnp.int32)
    own, sends = [], []
    for k, (g, r) in enumerate(zip(full, from_sibling)):
        o, s = _rs_chip_sum(g, r, slots, f"rs_chip_sum{k}")
        own.append(o)
        sends.append(s)
    landed = _rs_between_chips(sends, "rs_chips")

    def big(k, w, m, v, name):
        shape = w.shape
        view = (-1, shape[-1])
        parts = [own[k]] + [(landed[k], j) for j in range(3)]
        return [r.reshape(shape) for r in _adamw(w.reshape(view), m.reshape(view), v.reshape(view), parts, name)]

    def two_layers(k, w, m, v, name):
        a = big(k, w[0], m[0], v[0], name + "0")
        b = big(k + 1, w[1], m[1], v[1], name + "1")
        return [jnp.stack([p, q]) for p, q in zip(a, b)]

    res["pool_w_in"] = two_layers(0, pool_w_in, m_pool_w_in, v_pool_w_in, "adamw_pool_in")
    res["pool_w_grp"] = two_layers(2, pool_w_grp, m_pool_w_grp, v_pool_w_grp, "adamw_pool_grp")
    res["pool_w_out"] = two_layers(4, pool_w_out, m_pool_w_out, v_pool_w_out, "adamw_pool_out")
    res["na_w_in"] = [r[None] for r in big(6, na_w_in[0], m_na_w_in[0], v_na_w_in[0], "adamw_na_in")]
    res["na_w_out"] = [r[None] for r in big(7, na_w_out[0], m_na_w_out[0], v_na_w_out[0], "adamw_na_out")]
    res["conv_w_in"] = [r[None] for r in big(8, conv_w_in[0], m_conv_w_in[0], v_conv_w_in[0], "adamw_conv_in")]
    res["conv_w_out"] = [r[None] for r in big(9, conv_w_out[0], m_conv_w_out[0], v_conv_w_out[0], "adamw_conv_out")]

    order = ["c_ctx", "norm_g", "ada_w", "ada_b", "pool_w_in", "pool_w_grp", "pool_scale", "pool_w_out", "na_w_in",
             "na_rpb", "na_w_out", "conv_w_in", "conv_dw", "conv_db", "conv_w_out", "final_g"]
    outs = [loss, grad_x]
    for j in range(4):
        outs += [res[n][j] for n in order]
    return tuple(outs)
```

```python
import functools
import math

import numpy as np
import jax
import jax.numpy as jnp
from jax import lax
from jax.experimental import pallas as pl
from jax.experimental.pallas import tpu as pltpu

F32 = jnp.float32
BF16 = jnp.bfloat16
N_DEV = 8
LANES = 128
RMS_EPS = 1e-6
GRID_W = 64
WIN_ROWS = 8
WIN_COLS = 16
HEAD_DIM = 64
POOL_WINDOWS = (2, 4, 8, 16)
HALO = 8
CHUNK = 128
MASKED = -1e30
ADAM_LR = 0.001
ADAM_B1 = 0.9
ADAM_B2 = 0.999
ADAM_EPS = 1e-08
ADAM_WD = 0.01
ADAM_STEP = 10
VMEM_LIMIT = 56 * 1024 * 1024
MESH = pl.DeviceIdType.MESH
HIGHEST = lax.Precision.HIGHEST
ANY = pl.BlockSpec(memory_space=pl.ANY)


def _pc(body, *, name, **kw):
    return pl.pallas_call(body, name=name, **kw)


def _params(*sem):
    return pltpu.CompilerParams(dimension_semantics=sem if sem else None, vmem_limit_bytes=VMEM_LIMIT)


def _dot(a, b, ca=1, cb=0, precision=None):
    return lax.dot_general(a, b, (((ca,), (cb,)), ((), ())), preferred_element_type=F32, precision=precision)


def _tile(n, pref, unit=LANES):
    best = None
    for t in range(unit, min(n, pref) + 1, unit):
        if n % t == 0:
            best = t
    return best if best is not None else n


def _sigmoid(x):
    return 1.0 / (1.0 + jnp.exp(-x))


def _silu(x):
    return x * _sigmoid(x)


def _dsilu(x):
    s = _sigmoid(x)
    return s * (1.0 + x * (1.0 - s))


def _my_place():
    return lax.axis_index("x"), lax.axis_index("y"), lax.axis_index("c")


def _flip(v, f):
    return 1 - v if f else v


def _gather_small(block, name):
    rows, cols = block.shape

    def body(x_ref, out_ref, send_sems, recv_sems):
        x, y, c = _my_place()
        me = 4 * x + 2 * y + c
        out_ref[me] = x_ref[...]
        copies = []
        for k in range(1, N_DEV):
            peer = (_flip(x, k & 4), _flip(y, k & 2), _flip(c, k & 1))
            cp = pltpu.make_async_remote_copy(
                src_ref=x_ref, dst_ref=out_ref.at[me], send_sem=send_sems.at[k - 1], recv_sem=recv_sems.at[k - 1],
                device_id=peer, device_id_type=MESH)
            cp.start()
            copies.append(cp)
        for cp in copies:
            cp.wait()

    return _pc(
        body, name=name,
        out_shape=jax.ShapeDtypeStruct((N_DEV, rows, cols), block.dtype),
        in_specs=[pl.BlockSpec(memory_space=pltpu.VMEM)],
        out_specs=pl.BlockSpec(memory_space=pltpu.VMEM),
        scratch_shapes=[pltpu.SemaphoreType.DMA((N_DEV - 1,)), pltpu.SemaphoreType.DMA((N_DEV - 1,))],
    )(block)


def _allgather_big(tensors, name):
    nt = len(tensors)

    def body(*refs):
        ins, outs = refs[:nt], refs[nt:2 * nt]
        send_sems, recv_sems, local_sems = refs[2 * nt:]
        x, y, c = _my_place()
        sibling = (x, y, 1 - c)
        chips = [(1 - x, y), (x, 1 - y), (1 - x, 1 - y)]

        def idx(px, py, pc):
            return 4 * px + 2 * py + pc

        def copy(t, k, block, to, src=None):
            dst = outs[t].at[idx(*block)]
            return pltpu.make_async_remote_copy(
                src_ref=dst if src is None else src, dst_ref=dst,
                send_sem=send_sems.at[t, k], recv_sem=recv_sems.at[t, k], device_id=to, device_id_type=MESH)

        me = (x, y, c)
        mine, first, passed = [], [], []
        for t in range(nt):
            m = pltpu.make_async_copy(ins[t], outs[t].at[idx(*me)], local_sems.at[t])
            m.start()
            mine.append(m)
            f = [copy(t, 0, me, sibling, src=ins[t])]
            f += [copy(t, 1 + j, me, (*chip, c), src=ins[t]) for j, chip in enumerate(chips)]
            for cp in f:
                cp.start()
            first += f
        for t in range(nt):
            for j, chip in enumerate(chips):
                copy(t, 1 + j, (*chip, c), me).wait_recv()
                p = copy(t, 4 + j, (*chip, c), sibling)
                p.start()
                passed.append(p)
        for t in range(nt):
            copy(t, 0, sibling, me).wait_recv()
            for j, chip in enumerate(chips):
                copy(t, 4 + j, (*chip, 1 - c), me).wait_recv()
        for cp in first + passed:
            cp.wait_send()
        for m in mine:
            m.wait()

    return _pc(
        body, name=name,
        out_shape=[jax.ShapeDtypeStruct((N_DEV,) + t.shape, t.dtype) for t in tensors],
        in_specs=[ANY] * nt, out_specs=[ANY] * nt,
        scratch_shapes=[pltpu.SemaphoreType.DMA((nt, 7)), pltpu.SemaphoreType.DMA((nt, 7)),
                        pltpu.SemaphoreType.DMA((nt,))],
    )(*tensors)


def _rs_between_cores(grads, name):
    nt = len(grads)

    def body(*refs):
        ins, outs = refs[:nt], refs[nt:2 * nt]
        send_sems, recv_sems = refs[2 * nt:]
        x, y, c = _my_place()
        sibling = (x, y, 1 - c)
        for t in range(nt):
            for p in range(4):
                pltpu.make_async_remote_copy(
                    src_ref=ins[t].at[2 * p + 1 - c], dst_ref=outs[t].at[p],
                    send_sem=send_sems.at[t], recv_sem=recv_sems.at[t], device_id=sibling, device_id_type=MESH).start()
        for t in range(nt):
            pltpu.make_async_remote_copy(
                src_ref=outs[t], dst_ref=outs[t], send_sem=send_sems.at[t], recv_sem=recv_sems.at[t],
                device_id=sibling, device_id_type=MESH).wait()

    return _pc(
        body, name=name,
        out_shape=[jax.ShapeDtypeStruct((4,) + g.shape[1:], g.dtype) for g in grads],
        in_specs=[ANY] * nt, out_specs=[ANY] * nt,
        scratch_shapes=[pltpu.SemaphoreType.DMA((nt,)), pltpu.SemaphoreType.DMA((nt,))],
    )(*grads)


def _rs_chip_sum(g, r1, slots, name):
    _, rows, cols = g.shape
    tr = _tile(rows, max(8, 131072 // cols), 8)

    def body(slot_ref, *refs):
        gs, rs = refs[:4], refs[4:8]
        own_ref, send_ref = refs[8:]
        own_ref[...] = gs[0][...] + rs[0][...]
        for k in range(1, 4):
            send_ref[k - 1] = (gs[k][...] + rs[k][...]).astype(BF16)

    def g_spec(k):
        return pl.BlockSpec((None, tr, cols), lambda i, s: (2 * s[k] + s[4], i, 0))

    def r_spec(k):
        return pl.BlockSpec((None, tr, cols), lambda i, s: (s[k], i, 0))

    grid_spec = pltpu.PrefetchScalarGridSpec(
        num_scalar_prefetch=1, grid=(rows // tr,),
        in_specs=[g_spec(k) for k in range(4)] + [r_spec(k) for k in range(4)],
        out_specs=[pl.BlockSpec((tr, cols), lambda i, s: (i, 0)), pl.BlockSpec((3, tr, cols), lambda i, s: (0, i, 0))])
    return _pc(
        body, name=name, grid_spec=grid_spec,
        out_shape=[jax.ShapeDtypeStruct((rows, cols), F32), jax.ShapeDtypeStruct((3, rows, cols), BF16)],
        compiler_params=_params("parallel"),
    )(slots, g, g, g, g, r1, r1, r1, r1)


def _rs_between_chips(sends, name):
    nt = len(sends)

    def body(*refs):
        ins, outs = refs[:nt], refs[nt:2 * nt]
        send_sems, recv_sems = refs[2 * nt:]
        x, y, c = _my_place()
        peers = [(1 - x, y, c), (x, 1 - y, c), (1 - x, 1 - y, c)]
        copies = []
        for t in range(nt):
            for k, peer in enumerate(peers):
                cp = pltpu.make_async_remote_copy(
                    src_ref=ins[t].at[k], dst_ref=outs[t].at[k], send_sem=send_sems.at[t, k], recv_sem=recv_sems.at[t, k],
                    device_id=peer, device_id_type=MESH)
                cp.start()
                copies.append(cp)
        for cp in copies:
            cp.wait()

    return _pc(
        body, name=name,
        out_shape=[jax.ShapeDtypeStruct(s.shape, s.dtype) for s in sends],
        in_specs=[ANY] * nt, out_specs=[ANY] * nt,
        scratch_shapes=[pltpu.SemaphoreType.DMA((nt, 3)), pltpu.SemaphoreType.DMA((nt, 3))],
    )(*sends)


def _mod_fwd(cond, ada_w, bias):
    depth, d, nb = ada_w.shape

    def body(c_ref, w_ref, b_ref, o_ref):
        s = _silu(c_ref[...]).astype(BF16)
        o_ref[...] = _dot(s, w_ref[...].astype(BF16)) + b_ref[...]

    return _pc(
        body, name="mod_fwd", grid=(depth,),
        in_specs=[pl.BlockSpec((16, d), lambda i: (0, 0)), pl.BlockSpec((None, d, nb), lambda i: (i, 0, 0)),
                  pl.BlockSpec((None, 1, nb), lambda i: (i, 0, 0))],
        out_specs=pl.BlockSpec((None, 16, nb), lambda i: (i, 0, 0)),
        out_shape=jax.ShapeDtypeStruct((depth, 16, nb), F32),
        compiler_params=_params("parallel"),
    )(cond, ada_w, bias.reshape(depth, 1, nb))


def _mod_bwd(cond, ada_w, dm_all, dm_mine):
    depth, d, nb = ada_w.shape
    d3 = dm_all.shape[-1]

    def body(c_ref, w_ref, all_ref, call_ref, mine_ref, cmine_ref, gw_ref, gb_ref, part_ref, ds_ref):
        i = pl.program_id(0)
        cond_v = c_ref[...]
        s = _silu(cond_v).astype(BF16)
        has_ctx = jnp.where(i < 2, 1.0, 0.0)
        tot_all = jnp.sum(call_ref[...], axis=0, keepdims=True) * has_ctx
        tot_mine = jnp.broadcast_to(jnp.sum(cmine_ref[...], axis=0, keepdims=True) * has_ctx, (8, nb)).astype(BF16)
        gb_ref[...] = jnp.sum(all_ref[...], axis=0, keepdims=True) + tot_all
        gw_ref[...] = _dot(s[0:8], mine_ref[...].astype(BF16), 0, 0) + _dot(s[8:16], tot_mine, 0, 0)
        part = _dot(tot_mine, w_ref[...].astype(BF16), 1, 1)

        @pl.when(i == 0)
        def _():
            part_ref[...] = jnp.zeros_like(part_ref)
            ds_ref[...] = _dsilu(cond_v)

        part_ref[...] += part

    def rows(width, which):
        return pl.BlockSpec((None, N_DEV, width), which)

    layer = lambda i: (i, 0, 0)
    ctx_layer = lambda i: (jnp.minimum(i, 1) + 4, 0, 0)
    return _pc(
        body, name="mod_bwd", grid=(depth,),
        in_specs=[pl.BlockSpec((16, d), lambda i: (0, 0)), pl.BlockSpec((None, d, nb), layer),
                  rows(d3, layer), rows(d3, ctx_layer), rows(nb, layer), rows(nb, ctx_layer)],
        out_specs=[pl.BlockSpec((None, d, nb), layer), pl.BlockSpec((None, 1, d3), layer),
                   pl.BlockSpec((8, d), lambda i: (0, 0)), pl.BlockSpec((16, d), lambda i: (0, 0))],
        out_shape=[jax.ShapeDtypeStruct((depth, d, nb), F32), jax.ShapeDtypeStruct((depth, 1, d3), F32),
                   jax.ShapeDtypeStruct((8, d), F32), jax.ShapeDtypeStruct((16, d), F32)],
        compiler_params=_params("arbitrary"),
    )(cond, ada_w, dm_all, dm_all, dm_mine, dm_mine)


def _norm_fwd(xs, g, mod, tr, seg_tiles, name):
    t, d = xs.shape

    def body(x_ref, g_ref, mod_ref, h_ref):
        x = x_ref[...]
        r = lax.rsqrt(jnp.mean(x * x, axis=-1, keepdims=True) + RMS_EPS)
        y = (x * r) * g_ref[...]
        h_ref[...] = (y * (1.0 + mod_ref[1:2, :]) + mod_ref[0:1, :]).astype(BF16)

    return _pc(
        body, name=name, grid=(t // tr,),
        in_specs=[pl.BlockSpec((tr, d), lambda i: (i, 0)), pl.BlockSpec((1, d), lambda i: (0, 0)),
                  pl.BlockSpec((None, 8, d), lambda i: (i // seg_tiles, 0, 0))],
        out_specs=pl.BlockSpec((tr, d), lambda i: (i, 0)),
        out_shape=jax.ShapeDtypeStruct((t, d), BF16),
        compiler_params=_params("parallel"),
    )(xs, g, mod)


def _norm_bwd(xs, dh, dres, g, mod, tr, seg_tiles, name):
    t, d = xs.shape
    nseg = mod.shape[0]

    def body(x_ref, dh_ref, dres_ref, g_ref, mod_ref, dx_ref, sum_ref):
        i = pl.program_id(0)
        x = x_ref[...]
        r = lax.rsqrt(jnp.mean(x * x, axis=-1, keepdims=True) + RMS_EPS)
        xn = x * r
        dhv = dh_ref[...]
        gain = g_ref[...]
        one_scale = 1.0 + mod_ref[1:2, :]
        dxn = dhv * (gain * one_scale)
        dx_ref[...] = dres_ref[...] + r * (dxn - xn * jnp.mean(dxn * xn, axis=-1, keepdims=True))

        @pl.when(i % seg_tiles == 0)
        def _():
            sum_ref[...] = jnp.zeros_like(sum_ref)

        sum_ref[0:1, :] += jnp.sum(dhv, axis=0, keepdims=True)
        sum_ref[1:2, :] += jnp.sum(dhv * (xn * gain), axis=0, keepdims=True)
        sum_ref[2:3, :] += jnp.sum(dhv * one_scale * xn, axis=0, keepdims=True)

    row = pl.BlockSpec((tr, d), lambda i: (i, 0))
    seg = pl.BlockSpec((None, 8, d), lambda i: (i // seg_tiles, 0, 0))
    return _pc(
        body, name=name, grid=(t // tr,),
        in_specs=[row, row, row, pl.BlockSpec((1, d), lambda i: (0, 0)), seg],
        out_specs=[row, seg],
        out_shape=[jax.ShapeDtypeStruct((t, d), F32), jax.ShapeDtypeStruct((nseg, 8, d), F32)],
        compiler_params=_params("arbitrary"),
    )(xs, dh, dres, g, mod)


def _resid_bwd(dx, yx, mod, tr, seg_tiles, name):
    t, d = yx.shape
    nseg = mod.shape[0]

    def body(dx_ref, yx_ref, mod_ref, dyx_ref, sum_ref):
        i = pl.program_id(0)
        dxv = dx_ref[...]
        dyx_ref[...] = (dxv * mod_ref[2:3, :]).astype(BF16)

        @pl.when(i % seg_tiles == 0)
        def _():
            sum_ref[...] = jnp.zeros_like(sum_ref)

        sum_ref[0:1, :] += jnp.sum(dxv * yx_ref[...], axis=0, keepdims=True)

    row = pl.BlockSpec((tr, d), lambda i: (i, 0))
    seg = pl.BlockSpec((None, 8, d), lambda i: (i // seg_tiles, 0, 0))
    return _pc(
        body, name=name, grid=(t // tr,),
        in_specs=[row, row, seg], out_specs=[row, seg],
        out_shape=[jax.ShapeDtypeStruct((t, d), BF16), jax.ShapeDtypeStruct((nseg, 8, d), F32)],
        compiler_params=_params("arbitrary"),
    )(dx, yx, mod)


def _loss_head(xs, target, g, tr):
    t, d = xs.shape

    def body(x_ref, t_ref, g_ref, loss_ref, dx_ref, dg_ref):
        i = pl.program_id(0)
        x = x_ref[...]
        r = lax.rsqrt(jnp.mean(x * x, axis=-1, keepdims=True) + RMS_EPS)
        xn = x * r
        gain = g_ref[...]
        err = xn * gain - t_ref[...]
        dy = err * (1.0 / d)
        dxn = dy * gain
        dx_ref[...] = r * (dxn - xn * jnp.mean(dxn * xn, axis=-1, keepdims=True))

        @pl.when(i == 0)
        def _():
            loss_ref[...] = jnp.zeros_like(loss_ref)
            dg_ref[...] = jnp.zeros_like(dg_ref)

        loss_ref[...] += 0.5 * jnp.sum(jnp.mean(err * err, axis=-1, keepdims=True))
        dg_ref[0:1, :] += jnp.sum(dy * xn, axis=0, keepdims=True)

    row = pl.BlockSpec((tr, d), lambda i: (i, 0))
    return _pc(
        body, name="loss_head", grid=(t // tr,),
        in_specs=[row, row, pl.BlockSpec((1, d), lambda i: (0, 0))],
        out_specs=[pl.BlockSpec((8, LANES), lambda i: (0, 0)), row, pl.BlockSpec((8, d), lambda i: (0, 0))],
        out_shape=[jax.ShapeDtypeStruct((8, LANES), F32), jax.ShapeDtypeStruct((t, d), F32),
                   jax.ShapeDtypeStruct((8, d), F32)],
        compiler_params=_params("arbitrary"),
    )(xs, target, g)


def _proj_in(h, w, layer, width, name, blocks=None, dtype=F32):
    t, d = h.shape
    n8 = w.shape[-1]
    first, count = blocks if blocks is not None else (0, N_DEV)
    per_part = width // n8
    tm = _tile(t, 1152)

    def body(a_ref, b_ref, o_ref):
        o_ref[...] = _dot(a_ref[...], b_ref[...]).astype(dtype)

    return _pc(
        body, name=name, grid=(t // tm, count),
        in_specs=[pl.BlockSpec((tm, d), lambda i, j: (i, 0)),
                  pl.BlockSpec((None, None, d, n8), lambda i, j: (first + j, layer, 0, 0))],
        out_specs=pl.BlockSpec((None, tm, n8), lambda i, j: (j // per_part, i, j % per_part)),
        out_shape=jax.ShapeDtypeStruct((count // per_part, t, width), dtype),
        compiler_params=_params("parallel", "parallel"),
    )(h, w)


def _proj_out(z, w, res, mod, tm, seg_tiles, name):
    t, k = z.shape
    d = w.shape[1]

    def body(z_ref, w_ref, res_ref, mod_ref, yx_ref, x_ref):
        yx = _dot(z_ref[...], w_ref[...])
        yx_ref[...] = yx
        x_ref[...] = res_ref[...] + mod_ref[2:3, :] * yx

    tile = pl.BlockSpec((tm, d), lambda i: (i, 0))
    return _pc(
        body, name=name, grid=(t // tm,),
        in_specs=[pl.BlockSpec((tm, k), lambda i: (i, 0)), pl.BlockSpec((k, d), lambda i: (0, 0)), tile,
                  pl.BlockSpec((None, 8, d), lambda i: (i // seg_tiles, 0, 0))],
        out_specs=[tile, tile],
        out_shape=[jax.ShapeDtypeStruct((t, d), F32), jax.ShapeDtypeStruct((t, d), F32)],
        compiler_params=_params("parallel"),
    )(z, w, res, mod)


def _proj_out_dz(dyx, w, name):
    t, d = dyx.shape
    width = w.shape[0]
    tm, tn = _tile(t, 1024), _tile(width, 512)

    def body(a_ref, w_ref, o_ref):
        o_ref[...] = _dot(a_ref[...], w_ref[...], 1, 1)

    return _pc(
        body, name=name, grid=(t // tm, width // tn),
        in_specs=[pl.BlockSpec((tm, d), lambda i, j: (i, 0)), pl.BlockSpec((tn, d), lambda i, j: (j, 0))],
        out_specs=pl.BlockSpec((tm, tn), lambda i, j: (i, j)),
        out_shape=jax.ShapeDtypeStruct((t, width), F32),
        compiler_params=_params("parallel", "parallel"),
    )(dyx, w)


def _proj_in_dh(dpre, w, layer, name):
    parts, t, width = dpre.shape
    d, n8 = w.shape[-2:]
    per_part = width // n8
    tm, tn = _tile(t, 1152), _tile(d, 512)

    def body(a_ref, w_ref, o_ref, acc_ref):
        k = pl.program_id(2)

        @pl.when(k == 0)
        def _():
            acc_ref[...] = jnp.zeros_like(acc_ref)

        acc_ref[...] += _dot(a_ref[...], w_ref[...], 1, 1)

        @pl.when(k == N_DEV - 1)
        def _():
            o_ref[...] = acc_ref[...]

    return _pc(
        body, name=name, grid=(t // tm, d // tn, N_DEV),
        in_specs=[pl.BlockSpec((None, tm, n8), lambda i, j, k: (k // per_part, i, k % per_part)),
                  pl.BlockSpec((None, None, tn, n8), lambda i, j, k: (k, layer, j, 0))],
        out_specs=pl.BlockSpec((tm, tn), lambda i, j, k: (i, j)),
        out_shape=jax.ShapeDtypeStruct((t, d), F32),
        scratch_shapes=[pltpu.VMEM((tm, tn), F32)],
        compiler_params=_params("parallel", "parallel", "arbitrary"),
    )(dpre, w)


def _transposed(a_ref):
    return a_ref[...].astype(F32).T.astype(BF16)


def _grad_w_in(h, dpre, n8, name):
    t, d = h.shape
    parts, _, width = dpre.shape
    per_part = width // n8
    tm, tk = _tile(d, 512), _tile(t, 384)

    def body(a_ref, b_ref, o_ref):
        @pl.when(pl.program_id(1) == 0)
        def _():
            o_ref[...] = jnp.zeros_like(o_ref)

        at = _transposed(a_ref)
        for p in range(parts):
            r = _dot(at, b_ref[p])
            for s in range(per_part):
                o_ref[p * per_part + s] += r[:, s * n8:(s + 1) * n8]

    return _pc(
        body, name=name, grid=(d // tm, t // tk),
        in_specs=[pl.BlockSpec((tk, tm), lambda i, k: (k, i)), pl.BlockSpec((parts, tk, width), lambda i, k: (0, k, 0))],
        out_specs=pl.BlockSpec((parts * per_part, tm, n8), lambda i, k: (0, i, 0)),
        out_shape=jax.ShapeDtypeStruct((parts * per_part, d, n8), F32),
        compiler_params=_params("parallel", "arbitrary"),
    )(h, dpre)


def _grad_w_out(z, dyx, name):
    width = z.shape[1]
    t, d = dyx.shape
    tm, tk = _tile(width, 512), _tile(t, 384)

    def body(a_ref, b_ref, o_ref):
        @pl.when(pl.program_id(1) == 0)
        def _():
            o_ref[...] = jnp.zeros_like(o_ref)

        o_ref[...] += _dot(_transposed(a_ref), b_ref[...])

    return _pc(
        body, name=name, grid=(width // tm, t // tk),
        in_specs=[pl.BlockSpec((tk, tm), lambda i, k: (k, i)), pl.BlockSpec((tk, d), lambda i, k: (k, 0))],
        out_specs=pl.BlockSpec((tm, d), lambda i, k: (i, 0)),
        out_shape=jax.ShapeDtypeStruct((width, d), F32),
        compiler_params=_params("parallel", "arbitrary"),
    )(z, dyx)


def _shift(v, k):
    n = v.shape[0]
    return pltpu.roll(v, k % n, 0)


def _window_sum(v, win):
    s = v + _shift(v, 1)
    step = 1
    while 2 * step < win:
        s = _shift(s, step) + _shift(s, -step)
        step *= 2
    return s


def _window_count(base, seg_len, win, shape):
    t = base + lax.broadcasted_iota(jnp.int32, shape, 0)
    hi = jnp.minimum(t + win // 2, seg_len)
    lo = jnp.maximum(t - win // 2, 0)
    return (hi - lo).astype(F32)


def _pad_offsets(segs):
    return [HALO * (s + 1) + st for s, (st, _) in enumerate(segs)]


def _for_chunks(segs, fn):
    offs = _pad_offsets(segs)
    for s, (st, ln) in enumerate(segs):
        def step(ci, carry, s=s, st=st, ln=ln):
            fn(s, st, ln, offs[s], pl.multiple_of(ci * CHUNK, CHUNK))
            return carry
        lax.fori_loop(0, ln // CHUNK, step, 0)


def _pool_fwd(pre, w_grp, scale, segs, name):
    _, t, width = pre.shape
    grp = width // len(POOL_WINDOWS)
    padded = t + HALO * (len(segs) + 1)

    def group(win, pre_ref, w_ref, sc_ref, z_ref, diff_ref, pad_ref):
        pad_ref[...] = jnp.zeros_like(pad_ref)

        def fill(s, st, ln, off, b):
            pad_ref[pl.ds(off + b, CHUNK), :] = pre_ref[0, pl.ds(st + b, CHUNK), :]

        _for_chunks(segs, fill)

        def mix(s, st, ln, off, b):
            ext = pad_ref[pl.ds(off - HALO + b, CHUNK + 2 * HALO), :]
            total = _window_sum(ext, win)[HALO:HALO + CHUNK]
            u = pre_ref[0, pl.ds(st + b, CHUNK), :]
            diff = (total / _window_count(b, ln, win, u.shape) - u).astype(BF16)
            mixed = _dot(diff, w_ref[...])
            gate = _silu(pre_ref[1, pl.ds(st + b, CHUNK), :])
            z_ref[pl.ds(st + b, CHUNK), :] = (mixed * sc_ref[...] * gate).astype(BF16)
            diff_ref[pl.ds(st + b, CHUNK), :] = diff

        _for_chunks(segs, mix)

    def body(pre_ref, w_ref, sc_ref, z_ref, diff_ref, pad_ref):
        gi = pl.program_id(0)
        for widx, win in enumerate(POOL_WINDOWS):
            @pl.when(gi == widx)
            def _(win=win):
                group(win, pre_ref, w_ref, sc_ref, z_ref, diff_ref, pad_ref)

    col = pl.BlockSpec((t, grp), lambda g: (0, g))
    return _pc(
        body, name=name, grid=(len(POOL_WINDOWS),),
        in_specs=[pl.BlockSpec((2, t, grp), lambda g: (0, 0, g)), pl.BlockSpec((None, grp, grp), lambda g: (g, 0, 0)),
                  pl.BlockSpec((1, grp), lambda g: (0, g))],
        out_specs=[col, col],
        out_shape=[jax.ShapeDtypeStruct((t, width), BF16), jax.ShapeDtypeStruct((t, width), BF16)],
        scratch_shapes=[pltpu.VMEM((padded, grp), F32)],
        compiler_params=_params("parallel"),
    )(pre, w_grp, scale)


def _pool_bwd(dz, diff, pre, w_grp, scale, segs, name):
    _, t, width = pre.shape
    grp = width // len(POOL_WINDOWS)
    padded = t + HALO * (len(segs) + 1)

    def group(win, dz_ref, diff_ref, pre_ref, w_ref, sc_ref, dpre_ref, dw_ref, dsc_ref, pad_ref, dd_ref):
        pad_ref[...] = jnp.zeros_like(pad_ref)
        dw_ref[...] = jnp.zeros_like(dw_ref)
        dsc_ref[...] = jnp.zeros_like(dsc_ref)

        def first(s, st, ln, off, b):
            rows = pl.ds(st + b, CHUNK)
            diff_v = diff_ref[rows, :]
            mixed = _dot(diff_v, w_ref[...])
            g = pre_ref[1, rows, :]
            sg = _silu(g)
            dzv = dz_ref[rows, :]
            dmixed = (dzv * sc_ref[...] * sg).astype(BF16)
            dsc_ref[...] += jnp.sum(dzv * mixed * sg, axis=0, keepdims=True)
            dpre_ref[1, rows, :] = (dzv * mixed * sc_ref[...] * _dsilu(g)).astype(BF16)
            ddiff = _dot(dmixed, w_ref[...], 1, 1)
            dw_ref[...] += _dot(diff_v, dmixed, 0, 0)
            dd_ref[rows, :] = ddiff
            pad_ref[pl.ds(off + b, CHUNK), :] = ddiff / _window_count(b, ln, win, ddiff.shape)

        _for_chunks(segs, first)

        def second(s, st, ln, off, b):
            rows = pl.ds(st + b, CHUNK)
            ext = pad_ref[pl.ds(off - HALO + b, CHUNK + 2 * HALO), :]
            total = _shift(_window_sum(ext, win), -1)[HALO:HALO + CHUNK]
            dpre_ref[0, rows, :] = (total - dd_ref[rows, :]).astype(BF16)

        _for_chunks(segs, second)

    def body(dz_ref, diff_ref, pre_ref, w_ref, sc_ref, dpre_ref, dw_ref, dsc_ref, pad_ref, dd_ref):
        gi = pl.program_id(0)
        for widx, win in enumerate(POOL_WINDOWS):
            @pl.when(gi == widx)
            def _(win=win):
                group(win, dz_ref, diff_ref, pre_ref, w_ref, sc_ref, dpre_ref, dw_ref, dsc_ref, pad_ref, dd_ref)

    col = pl.BlockSpec((t, grp), lambda g: (0, g))
    both = pl.BlockSpec((2, t, grp), lambda g: (0, 0, g))
    wspec = pl.BlockSpec((None, grp, grp), lambda g: (g, 0, 0))
    sspec = pl.BlockSpec((1, grp), lambda g: (0, g))
    return _pc(
        body, name=name, grid=(len(POOL_WINDOWS),),
        in_specs=[col, col, both, wspec, sspec],
        out_specs=[both, wspec, sspec],
        out_shape=[jax.ShapeDtypeStruct((2, t, width), BF16), jax.ShapeDtypeStruct((len(POOL_WINDOWS), grp, grp), F32),
                   jax.ShapeDtypeStruct((1, width), F32)],
        scratch_shapes=[pltpu.VMEM((padded, grp), F32), pltpu.VMEM((t, grp), F32)],
        compiler_params=_params("parallel"),
    )(dz, diff, pre, w_grp, scale)


def _conv_fwd(pre, dw, db, name):
    _, t, width = pre.shape
    cb = LANES
    segs = [(0, t)]

    def body(pre_ref, dw_ref, db_ref, z_ref, pad_ref):
        pad_ref[...] = jnp.zeros_like(pad_ref)

        def fill(s, st, ln, off, b):
            rows = pl.ds(b, CHUNK)
            pad_ref[pl.ds(off + b, CHUNK), :] = pre_ref[1, rows, :] * pre_ref[2, rows, :]

        _for_chunks(segs, fill)

        def mix(s, st, ln, off, b):
            rows = pl.ds(b, CHUNK)
            ext = pad_ref[pl.ds(off - HALO + b, CHUNK + 2 * HALO), :]
            conv = (dw_ref[0:1, :] * _shift(ext, 1) + dw_ref[1:2, :] * ext + dw_ref[2:3, :] * _shift(ext, -1))
            conv = conv[HALO:HALO + CHUNK] + db_ref[...]
            y = pre_ref[0, rows, :] * conv
            z_ref[rows, :] = (y * _silu(pre_ref[3, rows, :])).astype(BF16)

        _for_chunks(segs, mix)

    return _pc(
        body, name=name, grid=(width // cb,),
        in_specs=[pl.BlockSpec((4, t, cb), lambda j: (0, 0, j)), pl.BlockSpec((8, cb), lambda j: (0, j)),
                  pl.BlockSpec((1, cb), lambda j: (0, j))],
        out_specs=pl.BlockSpec((t, cb), lambda j: (0, j)),
        out_shape=jax.ShapeDtypeStruct((t, width), BF16),
        scratch_shapes=[pltpu.VMEM((t + 2 * HALO, cb), F32)],
        compiler_params=_params("parallel"),
    )(pre, dw, db)


def _conv_bwd(dz, pre, dw, db, name):
    _, t, width = pre.shape
    cb = LANES
    segs = [(0, t)]

    def body(dz_ref, pre_ref, dw_ref, db_ref, dpre_ref, ddw_ref, ddb_ref, pad_a, pad_c):
        pad_a[...] = jnp.zeros_like(pad_a)
        pad_c[...] = jnp.zeros_like(pad_c)
        ddw_ref[...] = jnp.zeros_like(ddw_ref)
        ddb_ref[...] = jnp.zeros_like(ddb_ref)

        def fill(s, st, ln, off, b):
            rows = pl.ds(b, CHUNK)
            pad_a[pl.ds(off + b, CHUNK), :] = pre_ref[1, rows, :] * pre_ref[2, rows, :]

        _for_chunks(segs, fill)

        def first(s, st, ln, off, b):
            rows = pl.ds(b, CHUNK)
            ext = pad_a[pl.ds(off - HALO + b, CHUNK + 2 * HALO), :]
            prev, nxt = _shift(ext, 1)[HALO:HALO + CHUNK], _shift(ext, -1)[HALO:HALO + CHUNK]
            here = ext[HALO:HALO + CHUNK]
            conv = dw_ref[0:1, :] * prev + dw_ref[1:2, :] * here + dw_ref[2:3, :] * nxt + db_ref[...]
            bg, g = pre_ref[0, rows, :], pre_ref[3, rows, :]
            dzv = dz_ref[rows, :]
            dy = dzv * _silu(g)
            dpre_ref[3, rows, :] = (dzv * (bg * conv) * _dsilu(g)).astype(BF16)
            dpre_ref[0, rows, :] = (dy * conv).astype(BF16)
            dconv = dy * bg
            pad_c[pl.ds(off + b, CHUNK), :] = dconv
            ddw_ref[0:1, :] += jnp.sum(dconv * prev, axis=0, keepdims=True)
            ddw_ref[1:2, :] += jnp.sum(dconv * here, axis=0, keepdims=True)
            ddw_ref[2:3, :] += jnp.sum(dconv * nxt, axis=0, keepdims=True)
            ddb_ref[0:1, :] += jnp.sum(dconv, axis=0, keepdims=True)

        _for_chunks(segs, first)

        def second(s, st, ln, off, b):
            rows = pl.ds(b, CHUNK)
            ext = pad_c[pl.ds(off - HALO + b, CHUNK + 2 * HALO), :]
            da = (dw_ref[0:1, :] * _shift(ext, -1) + dw_ref[1:2, :] * ext + dw_ref[2:3, :] * _shift(ext, 1))
            da = da[HALO:HALO + CHUNK]
            dpre_ref[1, rows, :] = (da * pre_ref[2, rows, :]).astype(BF16)
            dpre_ref[2, rows, :] = (da * pre_ref[1, rows, :]).astype(BF16)

        _for_chunks(segs, second)

    quad = pl.BlockSpec((4, t, cb), lambda j: (0, 0, j))
    rows8 = pl.BlockSpec((8, cb), lambda j: (0, j))
    return _pc(
        body, name=name, grid=(width // cb,),
        in_specs=[pl.BlockSpec((t, cb), lambda j: (0, j)), quad, rows8, pl.BlockSpec((1, cb), lambda j: (0, j))],
        out_specs=[quad, rows8, rows8],
        out_shape=[jax.ShapeDtypeStruct((4, t, width), BF16), jax.ShapeDtypeStruct((8, width), F32),
                   jax.ShapeDtypeStruct((8, width), F32)],
        scratch_shapes=[pltpu.VMEM((t + 2 * HALO, cb), F32), pltpu.VMEM((t + 2 * HALO, cb), F32)],
        compiler_params=_params("parallel"),
    )(dz, pre, dw, db)


PAIR_TILES = 2 * WIN_ROWS - 2


def _pair_geometry():
    lane = lax.broadcasted_iota(jnp.int32, (GRID_W, LANES), 1)
    qcol = lax.broadcasted_iota(jnp.int32, (GRID_W, LANES), 0)
    low = lane < GRID_W
    kcol = jnp.where(low, lane, lane - GRID_W)
    start = jnp.clip(qcol - WIN_COLS // 2, 0, GRID_W - WIN_COLS)
    inside = (kcol >= start) & (kcol < start + WIN_COLS)
    return low, inside


def _bias_tiles(rpb_ref, rows_ref, tiles_ref, inside):
    for h in range(2):
        rows = rpb_ref[h]
        rows_ref[h] = (pltpu.roll(rows, LANES - (WIN_COLS - 1), 1)
                       + pltpu.roll(pltpu.roll(rows, GRID_W - (WIN_COLS - 1), 1), 2 * WIN_ROWS - 1, 0))
        for t in range(PAIR_TILES):
            both = jnp.broadcast_to(rows_ref[h, t:t + 1, :], (GRID_W, LANES))
            tiles_ref[h, t] = jnp.where(inside, pltpu.roll(both, 0, 1, stride=1, stride_axis=0), MASKED)


def _bias_tiles_grad(dtiles_ref, drpb_ref):
    n = PAIR_TILES * GRID_W
    qcol = lax.broadcasted_iota(jnp.int32, (n, LANES), 0) & (GRID_W - 1)
    lane = lax.broadcasted_iota(jnp.int32, (1, LANES), 1)
    zero = jnp.zeros((1, LANES), F32)
    for h in range(2):
        v = pltpu.roll(dtiles_ref[h].reshape(n, LANES), WIN_COLS - 1, 1)
        for bit in range(6):
            v = jnp.where((qcol >> bit) & 1 == 1, pltpu.roll(v, LANES - (1 << bit), 1), v)
        sums = [jnp.sum(v[t * GRID_W:(t + 1) * GRID_W], axis=0, keepdims=True) for t in range(PAIR_TILES)]
        for r in range(2 * WIN_ROWS):
            here = sums[r] if r < PAIR_TILES else zero
            prev = pltpu.roll(sums[r - 1], GRID_W, 1) if 1 <= r <= PAIR_TILES else zero
            drpb_ref[h, r:r + 1, :] = jnp.where(lane < 2 * WIN_COLS - 1, here + prev, 0.0)


def _attn_rows(r, n_rows):
    first = jnp.clip(r - WIN_ROWS // 2, 0, n_rows - WIN_ROWS)
    return first, first - r + WIN_ROWS - 1


def _softmax(s_loc, s_ctx):
    m = jnp.maximum(jnp.max(s_loc, axis=-1, keepdims=True), jnp.max(s_ctx, axis=-1, keepdims=True))
    e_loc, e_ctx = jnp.exp(s_loc - m), jnp.exp(s_ctx - m)
    inv = 1.0 / (jnp.sum(e_loc, axis=-1, keepdims=True) + jnp.sum(e_ctx, axis=-1, keepdims=True))
    return e_loc * inv, e_ctx * inv


def _pair_bias(tiles_ref, h, j):
    return jnp.concatenate([tiles_ref[h, j + 2 * m] for m in range(WIN_ROWS // 2)], axis=1)


ROWS_PER_STEP = 2


def _attn_items(step, n_rows, q_ref, low):
    items = []
    for u in range(ROWS_PER_STEP):
        r = step * ROWS_PER_STEP + u
        first, j = _attn_rows(r, n_rows)
        rows = pl.ds(pl.multiple_of(r * GRID_W, GRID_W), GRID_W)
        keys = pl.ds(pl.multiple_of(first * GRID_W, GRID_W), WIN_ROWS * GRID_W)
        q = (q_ref[rows, :].astype(F32) * HEAD_DIM ** -0.5).astype(BF16)
        zero = jnp.zeros_like(q)
        items.append((rows, keys, j, 0, jnp.where(low, q, zero)))
        items.append((rows, keys, j, 1, jnp.where(low, zero, q)))
    return items


def _attn_fwd(qkv, gate, rpb, seq):
    _, t, width = qkv.shape
    n_rows = seq // GRID_W
    n_ctx = t - seq
    blk = WIN_ROWS * GRID_W

    def body(q_ref, k_ref, v_ref, g_ref, rpb_ref, z_ref, o_ref, rows_ref, tiles_ref):
        low, inside = _pair_geometry()
        _bias_tiles(rpb_ref, rows_ref, tiles_ref, inside)
        ctx = pl.ds(seq, n_ctx)

        def step(i, carry):
            items = _attn_items(i, n_rows, q_ref, low)
            k_ctx, v_ctx = k_ref[ctx, :], v_ref[ctx, :]
            scores = [(_dot(q, k_ref[keys, :], 1, 1) + _pair_bias(tiles_ref, h, j), _dot(q, k_ctx, 1, 1))
                      for _, keys, j, h, q in items]
            probs = [_softmax(s_loc, s_ctx) for s_loc, s_ctx in scores]
            outs = [_dot(p_loc.astype(BF16), v_ref[keys, :]) + _dot(p_ctx.astype(BF16), v_ctx)
                    for (_, keys, _, _, _), (p_loc, p_ctx) in zip(items, probs)]
            for u in range(ROWS_PER_STEP):
                rows = items[2 * u][0]
                o = jnp.where(low, outs[2 * u], outs[2 * u + 1])
                o_ref[rows, :] = o
                z_ref[rows, :] = (o * _silu(g_ref[rows, :])).astype(BF16)
            return carry

        lax.fori_loop(0, n_rows // ROWS_PER_STEP, step, 0)

    def part(p):
        return pl.BlockSpec((None, t, LANES), lambda h: (p, 0, h))

    out = pl.BlockSpec((seq, LANES), lambda h: (0, h))
    return _pc(
        body, name="attn_fwd", grid=(width // LANES,),
        in_specs=[part(0), part(1), part(2), part(0), pl.BlockSpec((2, 2 * WIN_ROWS, LANES), lambda h: (h, 0, 0))],
        out_specs=[out, out],
        out_shape=[jax.ShapeDtypeStruct((seq, width), BF16), jax.ShapeDtypeStruct((seq, width), F32)],
        scratch_shapes=[pltpu.VMEM((2, 2 * WIN_ROWS, LANES), F32), pltpu.VMEM((2, PAIR_TILES, GRID_W, LANES), F32)],
        compiler_params=_params("parallel"),
    )(qkv, qkv, qkv, gate, rpb)


def _attn_bwd(qkv, gate, o, dz, rpb, seq):
    _, t, width = qkv.shape
    n_rows = seq // GRID_W
    n_ctx = t - seq
    blk = WIN_ROWS * GRID_W
    heads = 2 * width // LANES

    def body(q_ref, k_ref, v_ref, g_ref, o_ref, dz_ref, rpb_ref, dpre_ref, drpb_ref,
             rows_ref, tiles_ref, dtiles_ref, dk_ref, dv_ref):
        low, inside = _pair_geometry()
        _bias_tiles(rpb_ref, rows_ref, tiles_ref, inside)
        dtiles_ref[...] = jnp.zeros_like(dtiles_ref)
        dk_ref[...] = jnp.zeros_like(dk_ref)
        dv_ref[...] = jnp.zeros_like(dv_ref)
        ctx = pl.ds(seq, n_ctx)

        def step(i, carry):
            items = _attn_items(i, n_rows, q_ref, low)
            k_ctx, v_ctx = k_ref[ctx, :], v_ref[ctx, :]
            d_outs = []
            for u in range(ROWS_PER_STEP):
                rows = items[2 * u][0]
                g = g_ref[rows, :]
                dzv = dz_ref[rows, :]
                dpre_ref[3, rows, :] = (dzv * o_ref[rows, :] * _dsilu(g)).astype(BF16)
                d_o = (dzv * _silu(g)).astype(BF16)
                zero = jnp.zeros_like(d_o)
                d_outs += [jnp.where(low, d_o, zero), jnp.where(low, zero, d_o)]
            scores = [(_dot(q, k_ref[keys, :], 1, 1) + _pair_bias(tiles_ref, h, j), _dot(q, k_ctx, 1, 1))
                      for _, keys, j, h, q in items]
            dprobs = [(_dot(doh, v_ref[keys, :], 1, 1), _dot(doh, v_ctx, 1, 1))
                      for (_, keys, _, _, _), doh in zip(items, d_outs)]
            probs = [_softmax(s_loc, s_ctx) for s_loc, s_ctx in scores]
            dscores = []
            for (p_loc, p_ctx), (dp_loc, dp_ctx) in zip(probs, dprobs):
                delta = (jnp.sum(p_loc * dp_loc, axis=-1, keepdims=True)
                         + jnp.sum(p_ctx * dp_ctx, axis=-1, keepdims=True))
                dscores.append((p_loc * (dp_loc - delta), p_ctx * (dp_ctx - delta)))
            dqs = [_dot(ds_loc.astype(BF16), k_ref[keys, :]) + _dot(ds_ctx.astype(BF16), k_ctx)
                   for (_, keys, _, _, _), (ds_loc, ds_ctx) in zip(items, dscores)]
            for u in range(ROWS_PER_STEP):
                rows = items[2 * u][0]
                dpre_ref[0, rows, :] = (jnp.where(low, dqs[2 * u], dqs[2 * u + 1]) * HEAD_DIM ** -0.5).astype(BF16)
            for (_, keys, j, h, q), doh, (p_loc, p_ctx), (ds_loc, ds_ctx) in zip(items, d_outs, probs, dscores):
                dk_ref[keys, :] += _dot(ds_loc.astype(BF16), q, 0, 0)
                dk_ref[ctx, :] += _dot(ds_ctx.astype(BF16), q, 0, 0)
                dv_ref[keys, :] += _dot(p_loc.astype(BF16), doh, 0, 0)
                dv_ref[ctx, :] += _dot(p_ctx.astype(BF16), doh, 0, 0)
                for m in range(WIN_ROWS // 2):
                    dtiles_ref[h, j + 2 * m] += ds_loc[:, m * LANES:(m + 1) * LANES]
            return carry

        lax.fori_loop(0, n_rows // ROWS_PER_STEP, step, 0)
        dpre_ref[1] = dk_ref[...].astype(BF16)
        dpre_ref[2] = dv_ref[...].astype(BF16)
        dpre_ref[0, ctx, :] = jnp.zeros((n_ctx, LANES), BF16)
        dpre_ref[3, ctx, :] = jnp.zeros((n_ctx, LANES), BF16)
        _bias_tiles_grad(dtiles_ref, drpb_ref)

    def part(p):
        return pl.BlockSpec((None, t, LANES), lambda h: (p, 0, h))

    lat = pl.BlockSpec((seq, LANES), lambda h: (0, h))
    rspec = pl.BlockSpec((2, 2 * WIN_ROWS, LANES), lambda h: (h, 0, 0))
    tiles = pltpu.VMEM((2, PAIR_TILES, GRID_W, LANES), F32)
    return _pc(
        body, name="attn_bwd", grid=(width // LANES,),
        in_specs=[part(0), part(1), part(2), part(0), lat, lat, rspec],
        out_specs=[pl.BlockSpec((4, t, LANES), lambda h: (0, 0, h)), rspec],
        out_shape=[jax.ShapeDtypeStruct((4, t, width), BF16), jax.ShapeDtypeStruct((heads, 2 * WIN_ROWS, LANES), F32)],
        scratch_shapes=[pltpu.VMEM((2, 2 * WIN_ROWS, LANES), F32), tiles, tiles,
                        pltpu.VMEM((t, LANES), F32), pltpu.VMEM((t, LANES), F32)],
        compiler_params=_params("parallel"),
    )(qkv, qkv, qkv, gate, o, dz, rpb)


def _adamw(w, m, v, parts, name, mult=None):
    rows, cols = w.shape
    tr = _tile(rows, max(8, 131072 // cols), 8)
    n_parts = len(parts)
    c1 = 1.0 - ADAM_B1 ** ADAM_STEP
    c2 = 1.0 - ADAM_B2 ** ADAM_STEP

    def body(*refs):
        w_ref, m_ref, v_ref = refs[:3]
        part_refs = refs[3:3 + n_parts]
        rest = refs[3 + n_parts:]
        g = part_refs[0][...].astype(F32)
        for p in part_refs[1:]:
            g = g + p[...].astype(F32)
        if mult is not None:
            g = g * rest[0][...]
            rest = rest[1:]
        g_ref, d_ref, nm_ref, nv_ref = rest
        m2 = ADAM_B1 * m_ref[...] + (1.0 - ADAM_B1) * g
        v2 = ADAM_B2 * v_ref[...] + (1.0 - ADAM_B2) * (g * g)
        m_hat = m2 / c1
        v_hat = v2 / c2
        g_ref[...] = g
        d_ref[...] = -ADAM_LR * (m_hat / (jnp.sqrt(v_hat) + ADAM_EPS) + ADAM_WD * w_ref[...])
        nm_ref[...] = m2
        nv_ref[...] = v2

    tile = pl.BlockSpec((tr, cols), lambda i: (i, 0))
    in_specs, args = [tile, tile, tile], [w, m, v]
    for p in parts:
        if isinstance(p, tuple):
            arr, k = p
            in_specs.append(pl.BlockSpec((None, tr, cols), lambda i, k=k: (k, i, 0)))
            args.append(arr)
        else:
            in_specs.append(tile)
            args.append(p)
    if mult is not None:
        in_specs.append(tile)
        args.append(mult)
    shape = jax.ShapeDtypeStruct((rows, cols), F32)
    return _pc(
        body, name=name, grid=(rows // tr,), in_specs=in_specs, out_specs=[tile] * 4, out_shape=[shape] * 4,
        compiler_params=_params("parallel"),
    )(*args)


def _rows128(a):
    flat = a.reshape(-1)
    pad = (-flat.shape[0]) % LANES
    if pad:
        flat = jnp.concatenate([flat, jnp.zeros((pad,), flat.dtype)])
    return flat.reshape(-1, LANES)


def _pad_rows(a, mult=8):
    pad = (-a.shape[0]) % mult
    if pad:
        a = jnp.concatenate([a, jnp.zeros((pad,) + a.shape[1:], a.dtype)], axis=0)
    return a


def kernel(x, c, ctx, c_ctx, norm_g, ada_w, ada_b, pool_w_in, pool_w_grp, pool_scale, pool_w_out, na_w_in, na_rpb, na_w_out, conv_w_in, conv_dw, conv_db, conv_w_out, final_g, loss_target, m_c_ctx, m_norm_g, m_ada_w, m_ada_b, m_pool_w_in, m_pool_w_grp, m_pool_scale, m_pool_w_out, m_na_w_in, m_na_rpb, m_na_w_out, m_conv_w_in, m_conv_dw, m_conv_db, m_conv_w_out, m_final_g, v_c_ctx, v_norm_g, v_ada_w, v_ada_b, v_pool_w_in, v_pool_w_grp, v_pool_scale, v_pool_w_out, v_na_w_in, v_na_rpb, v_na_w_out, v_conv_w_in, v_conv_dw, v_conv_db, v_conv_w_out, v_final_g):
    xi, yi, ci = _my_place()
    me = 4 * xi + 2 * yi + ci
    seq, d = x.shape[1], x.shape[2]
    n_ctx = ctx.shape[1]
    t_all = seq + n_ctx
    width = d
    depth = norm_g.shape[0]
    nb = ada_w.shape[2]
    shard = width // N_DEV
    d_rows = d // LANES
    tr = math.gcd(math.gcd(seq, n_ctx), 256)
    x_tiles = seq // tr

    small_in = _pad_rows(jnp.concatenate([_rows128(c), pool_scale, conv_dw[0], conv_db], axis=0))
    got = _gather_small(small_in, "gather_inputs")
    r0 = d_rows
    c_all = got[:, :r0].reshape(N_DEV, d)
    n_pool = pool_scale.shape[0]
    scale_full = got[:, r0:r0 + n_pool].transpose(1, 0, 2).reshape(n_pool, width)
    r1 = r0 + n_pool
    taps_full = _pad_rows(got[:, r1:r1 + 3].transpose(1, 0, 2).reshape(3, width))
    bias_full = got[:, r1 + 3:r1 + 4].transpose(1, 0, 2).reshape(1, width)

    cond = jnp.concatenate([c_all, c_ctx[None], jnp.zeros((7, d), F32)], axis=0)
    bias_mine = lax.dynamic_slice(ada_b, (0, me * nb), (depth, nb))
    mod_mine = _mod_fwd(cond, ada_w, bias_mine)
    mod_all = _gather_small(mod_mine.reshape(-1, LANES), "gather_mod")
    mod_all = mod_all.reshape(N_DEV, depth, 16, nb).transpose(1, 2, 0, 3).reshape(depth, 16, 3 * d)
    mod_x = lax.dynamic_index_in_dim(mod_all, me, 1, keepdims=False).reshape(depth, 3, d)
    mod_c = mod_all[:, 8].reshape(depth, 3, d)
    pad5 = jnp.zeros((depth, 5, d), F32)
    mod_x = jnp.concatenate([mod_x, pad5], axis=1)
    mod_c = jnp.concatenate([mod_c, pad5], axis=1)
    mods = [jnp.stack([mod_x[i], mod_c[i]]) if i < 2 else mod_x[i][None] for i in range(depth)]

    gathered = _allgather_big(
        [w.astype(BF16) for w in (pool_w_in, pool_w_grp, pool_w_out, na_w_in, na_w_out, conv_w_in, conv_w_out)],
        "gather_weights")
    pool_in_w, pool_grp_w, pool_out_w, na_in_w, na_out_w, conv_in_w, conv_out_w = gathered
    n_grp = pool_w_grp.shape[1]
    grp = width // n_grp
    pool_grp_w = pool_grp_w.transpose(1, 2, 0, 3, 4).reshape(n_pool, n_grp, grp, grp)
    pool_out_w = pool_out_w.transpose(1, 0, 2, 3).reshape(n_pool, width, d)
    na_out_w = na_out_w.reshape(width, d)
    conv_out_w = conv_out_w.reshape(width, d)

    both = [(0, seq), (seq, n_ctx)]
    latent = [(0, seq)]

    xs0 = jnp.concatenate([x[0], ctx[0]], axis=0)
    h0 = _norm_fwd(xs0, norm_g[0:1], mods[0], tr, x_tiles, "norm_fwd0")
    pre0 = _proj_in(h0, pool_in_w, 0, width, "proj_in0")
    z0, diff0 = _pool_fwd(pre0, pool_grp_w[0], scale_full[0:1], both, "pool_fwd0")
    yx0, xs1 = _proj_out(z0, pool_out_w[0], xs0, mods[0], tr, x_tiles, "proj_out0")

    h1 = _norm_fwd(xs1, norm_g[1:2], mods[1], tr, x_tiles, "norm_fwd1")
    per_part = width // na_w_in.shape[2]
    qkv1 = _proj_in(h1, na_in_w, 0, width, "proj_in1_qkv", blocks=(0, 3 * per_part), dtype=BF16)
    gpre1 = _proj_in(h1, na_in_w, 0, width, "proj_in1_gate", blocks=(3 * per_part, per_part))
    rpb_rows = jnp.pad(na_rpb[0], ((0, 0), (0, 2 * WIN_ROWS - na_rpb.shape[2]), (0, LANES - na_rpb.shape[3])))
    z1, o1 = _attn_fwd(qkv1, gpre1, rpb_rows, seq)
    yx1, x2 = _proj_out(z1, na_out_w, xs1, mods[1], tr, x_tiles, "proj_out1")

    h2 = _norm_fwd(x2, norm_g[2:3], mods[2], tr, x_tiles, "norm_fwd2")
    pre2 = _proj_in(h2, conv_in_w, 0, width, "proj_in2")
    z2 = _conv_fwd(pre2, taps_full, bias_full, "conv_fwd")
    yx2, x3 = _proj_out(z2, conv_out_w, x2, mods[2], tr, x_tiles, "proj_out2")

    h3 = _norm_fwd(x3, norm_g[3:4], mods[3], tr, x_tiles, "norm_fwd3")
    pre3 = _proj_in(h3, pool_in_w, 1, width, "proj_in3")
    z3, diff3 = _pool_fwd(pre3, pool_grp_w[1], scale_full[1:2], latent, "pool_fwd3")
    yx3, x4 = _proj_out(z3, pool_out_w[1], x3, mods[3], tr, x_tiles, "proj_out3")

    loss_part, dx4, d_final = _loss_head(x4, loss_target[0], final_g[None], tr)
    loss = lax.psum(loss_part[0, 0], ("x", "y", "c"))

    dyx3, gate3 = _resid_bwd(dx4, yx3, mods[3], tr, x_tiles, "resid_bwd3")
    dz3 = _proj_out_dz(dyx3, pool_out_w[1], "proj_out_dz3")
    g_pool_out1 = _grad_w_out(z3, dyx3, "grad_w_out3")
    dpre3, g_grp1, g_scale1 = _pool_bwd(dz3, diff3, pre3, pool_grp_w[1], scale_full[1:2], latent, "pool_bwd3")
    dh3 = _proj_in_dh(dpre3, pool_in_w, 1, "proj_in_dh3")
    g_pool_in1 = _grad_w_in(h3, dpre3, pool_w_in.shape[2], "grad_w_in3")
    dx3, norm3 = _norm_bwd(x3, dh3, dx4, norm_g[3:4], mods[3], tr, x_tiles, "norm_bwd3")

    dyx2, gate2 = _resid_bwd(dx3, yx2, mods[2], tr, x_tiles, "resid_bwd2")
    dz2 = _proj_out_dz(dyx2, conv_out_w, "proj_out_dz2")
    g_conv_out = _grad_w_out(z2, dyx2, "grad_w_out2")
    dpre2, g_taps, g_cbias = _conv_bwd(dz2, pre2, taps_full, bias_full, "conv_bwd")
    dh2 = _proj_in_dh(dpre2, conv_in_w, 0, "proj_in_dh2")
    g_conv_in = _grad_w_in(h2, dpre2, conv_w_in.shape[2], "grad_w_in2")
    dx2, norm2 = _norm_bwd(x2, dh2, dx3, norm_g[2:3], mods[2], tr, x_tiles, "norm_bwd2")

    dyx1, gate1 = _resid_bwd(dx2, yx1, mods[1][:1], tr, x_tiles, "resid_bwd1")
    dz1 = _proj_out_dz(dyx1, na_out_w, "proj_out_dz1")
    g_na_out = _grad_w_out(z1, dyx1, "grad_w_out1")
    dpre1, g_rpb = _attn_bwd(qkv1, gpre1, o1, dz1, rpb_rows, seq)
    g_rpb = g_rpb[:, :na_rpb.shape[2], :na_rpb.shape[3]]
    dh1 = _proj_in_dh(dpre1, na_in_w, 0, "proj_in_dh1")
    g_na_in = _grad_w_in(h1, dpre1, na_w_in.shape[2], "grad_w_in1")
    dres1 = jnp.concatenate([dx2, jnp.zeros((n_ctx, d), F32)], axis=0)
    dxs1, norm1 = _norm_bwd(xs1, dh1, dres1, norm_g[1:2], mods[1], tr, x_tiles, "norm_bwd1")

    dyx0, gate0 = _resid_bwd(dxs1, yx0, mods[0], tr, x_tiles, "resid_bwd0")
    dz0 = _proj_out_dz(dyx0, pool_out_w[0], "proj_out_dz0")
    g_pool_out0 = _grad_w_out(z0, dyx0, "grad_w_out0")
    dpre0, g_grp0, g_scale0 = _pool_bwd(dz0, diff0, pre0, pool_grp_w[0], scale_full[0:1], both, "pool_bwd0")
    dh0 = _proj_in_dh(dpre0, pool_in_w, 0, "proj_in_dh0")
    g_pool_in0 = _grad_w_in(h0, dpre0, pool_w_in.shape[2], "grad_w_in0")
    dxs0, norm0 = _norm_bwd(xs0, dh0, dxs1, norm_g[0:1], mods[0], tr, x_tiles, "norm_bwd0")
    grad_x = dxs0[:seq][None]

    norms, gates = [norm0, norm1, norm2, norm3], [gate0, gate1, gate2, gate3]
    zero_d = jnp.zeros((d,), F32)
    dm_rows = [jnp.concatenate([norms[i][0, 0], norms[i][0, 1], gates[i][0, 0]]) for i in range(depth)]
    dm_rows.append(jnp.concatenate([norm0[1, 0], norm0[1, 1], gate0[1, 0]]))
    dm_rows.append(jnp.concatenate([norm1[1, 0], norm1[1, 1], zero_d]))
    dm_local = jnp.stack(dm_rows + [jnp.zeros((3 * d,), F32)] * 2)
    g_norm_part = jnp.stack([norm0[0, 2] + norm0[1, 2], norm1[0, 2] + norm1[1, 2], norm2[0, 2], norm3[0, 2]])
    g_scale_part = jnp.concatenate([g_scale0, g_scale1], axis=0)
    pieces = [_rows128(dm_local), _rows128(g_norm_part), _rows128(d_final[0]), _pad_rows(_rows128(g_rpb)),
              _rows128(g_scale_part), _rows128(g_taps[:3]), _rows128(g_cbias[0])]
    small_out = _gather_small(jnp.concatenate(pieces, axis=0), "gather_small_grads")
    marks = np.cumsum([0] + [p.shape[0] for p in pieces])

    def piece(k):
        return small_out[:, marks[k]:marks[k + 1]]

    dm_all = piece(0).reshape(N_DEV, 8, 3 * d).transpose(1, 0, 2)
    dm_mine = lax.dynamic_slice(dm_all, (0, 0, me * nb), (8, N_DEV, nb))
    g_ada_w, g_ada_b, cctx_part, dsilu_cond = _mod_bwd(cond, ada_w, dm_all, dm_mine)
    cctx_all = _gather_small(_rows128(cctx_part[0]), "gather_cctx")

    def my_shard(a, n):
        a = a.reshape(N_DEV, n, N_DEV, shard)
        return lax.dynamic_index_in_dim(a, me, 2, keepdims=False)

    rpb_rows = marks[4] - marks[3]
    zeros7 = lambda r: jnp.zeros((N_DEV - 1, r, LANES), F32)
    ada_b_rows = _rows128(g_ada_b)
    small_parts = jnp.concatenate([
        cctx_all, piece(1), jnp.concatenate([ada_b_rows[None], zeros7(ada_b_rows.shape[0])], axis=0), piece(2), piece(3),
        my_shard(piece(4), n_pool), my_shard(piece(5), 3), my_shard(piece(6), 1)], axis=1)
    n_small = small_parts.shape[1]
    small_parts = jnp.concatenate([small_parts, jnp.zeros((N_DEV, (-n_small) % 8, LANES), F32)], axis=1)

    def pack(c_ctx_, norm_g_, ada_b_, final_g_, rpb_, scale_, taps_, cbias_):
        rows = [_rows128(c_ctx_), _rows128(norm_g_), _rows128(ada_b_), _rows128(final_g_), _pad_rows(_rows128(rpb_)),
                scale_, taps_[0], cbias_]
        return _pad_rows(jnp.concatenate(rows, axis=0)), np.cumsum([0] + [r.shape[0] for r in rows])

    w_small, smarks = pack(c_ctx, norm_g, ada_b, final_g, na_rpb, pool_scale, conv_dw, conv_db)
    m_small, _ = pack(m_c_ctx, m_norm_g, m_ada_b, m_final_g, m_na_rpb, m_pool_scale, m_conv_dw, m_conv_db)
    v_small, _ = pack(v_c_ctx, v_norm_g, v_ada_b, v_final_g, v_na_rpb, v_pool_scale, v_conv_dw, v_conv_db)
    mult = jnp.concatenate([_rows128(dsilu_cond[8]), jnp.ones((w_small.shape[0] - d_rows, LANES), F32)], axis=0)
    small_res = _adamw(w_small, m_small, v_small, [(small_parts, k) for k in range(N_DEV)], "adamw_small", mult=mult)

    def unpack(k, like):
        out = []
        for r in small_res:
            flat = r[smarks[k]:smarks[k + 1]].reshape(-1)
            out.append(flat[:like.size].reshape(like.shape))
        return out

    res = {"c_ctx": unpack(0, c_ctx), "norm_g": unpack(1, norm_g), "ada_b": unpack(2, ada_b),
           "final_g": unpack(3, final_g), "na_rpb": unpack(4, na_rpb), "pool_scale": unpack(5, pool_scale),
           "conv_dw": unpack(6, conv_dw), "conv_db": unpack(7, conv_db)}

    res["ada_w"] = [r.reshape(ada_w.shape) for r in _adamw(
        ada_w.reshape(-1, nb), m_ada_w.reshape(-1, nb), v_ada_w.reshape(-1, nb), [g_ada_w.reshape(-1, nb)], "adamw_ada_w")]

    def grp_slots(g):
        return g.reshape(n_grp, N_DEV, grp // N_DEV, grp).transpose(1, 0, 2, 3).reshape(N_DEV, -1, grp)

    full = [g_pool_in0, g_pool_in1, grp_slots(g_grp0), grp_slots(g_grp1),
            g_pool_out0.reshape(N_DEV, shard, d), g_pool_out1.reshape(N_DEV, shard, d),
            g_na_in, g_na_out.reshape(N_DEV, shard, d), g_conv_in, g_conv_out.reshape(N_DEV, shard, d)]
    from_sibling = _rs_between_cores(full, "rs_cores")
    chip = 2 * xi + yi
    slots = jnp.stack([chip, chip ^ 2, chip ^ 1, chip ^ 3, ci]).astype(jnp.int32)
    own, sends = [], []
    for k, (g, r) in enumerate(zip(full, from_sibling)):
        o, s = _rs_chip_sum(g, r, slots, f"rs_chip_sum{k}")
        own.append(o)
        sends.append(s)
    landed = _rs_between_chips(sends, "rs_chips")

    def big(k, w, m, v, name):
        shape = w.shape
        view = (-1, shape[-1])
        parts = [own[k]] + [(landed[k], j) for j in range(3)]
        return [r.reshape(shape) for r in _adamw(w.reshape(view), m.reshape(view), v.reshape(view), parts, name)]

    def two_layers(k, w, m, v, name):
        a = big(k, w[0], m[0], v[0], name + "0")
        b = big(k + 1, w[1], m[1], v[1], name + "1")
        return [jnp.stack([p, q]) for p, q in zip(a, b)]

    res["pool_w_in"] = two_layers(0, pool_w_in, m_pool_w_in, v_pool_w_in, "adamw_pool_in")
    res["pool_w_grp"] = two_layers(2, pool_w_grp, m_pool_w_grp, v_pool_w_grp, "adamw_pool_grp")
    res["pool_w_out"] = two_layers(4, pool_w_out, m_pool_w_out, v_pool_w_out, "adamw_pool_out")
    res["na_w_in"] = [r[None] for r in big(6, na_w_in[0], m_na_w_in[0], v_na_w_in[0], "adamw_na_in")]
    res["na_w_out"] = [r[None] for r in big(7, na_w_out[0], m_na_w_out[0], v_na_w_out[0], "adamw_na_out")]
    res["conv_w_in"] = [r[None] for r in big(8, conv_w_in[0], m_conv_w_in[0], v_conv_w_in[0], "adamw_conv_in")]
    res["conv_w_out"] = [r[None] for r in big(9, conv_w_out[0], m_conv_w_out[0], v_conv_w_out[0], "adamw_conv_out")]

    order = ["c_ctx", "norm_g", "ada_w", "ada_b", "pool_w_in", "pool_w_grp", "pool_scale", "pool_w_out", "na_w_in",
             "na_rpb", "na_w_out", "conv_w_in", "conv_dw", "conv_db", "conv_w_out", "final_g"]
    outs = [loss, grad_x]
    for j in range(4):
        outs += [res[n][j] for n in order]
    return tuple(outs)
```

```python
import functools
import math

import numpy as np
import jax
import jax.numpy as jnp
from jax import lax
from jax.experimental import pallas as pl
from jax.experimental.pallas import tpu as pltpu

F32 = jnp.float32
BF16 = jnp.bfloat16
N_DEV = 8
LANES = 128
RMS_EPS = 1e-6
GRID_W = 64
WIN_ROWS = 8
WIN_COLS = 16
HEAD_DIM = 64
POOL_WINDOWS = (2, 4, 8, 16)
HALO = 8
CHUNK = 128
MASKED = -1e30
ADAM_LR = 0.001
ADAM_B1 = 0.9
ADAM_B2 = 0.999
ADAM_EPS = 1e-08
ADAM_WD = 0.01
ADAM_STEP = 10
VMEM_LIMIT = 56 * 1024 * 1024
MESH = pl.DeviceIdType.MESH
ANY = pl.BlockSpec(memory_space=pl.ANY)
HBM = pl.BlockSpec(memory_space=pltpu.HBM)
SEM = pl.BlockSpec(memory_space=pltpu.SEMAPHORE)
EFFECT = pltpu.SideEffectType.DATAFLOW_SIDE_EFFECTING


def _pc(body, *, name, **kw):
    return pl.pallas_call(body, name=name, **kw)


def _params(*sem):
    return pltpu.CompilerParams(dimension_semantics=sem if sem else None, vmem_limit_bytes=VMEM_LIMIT)


def _dot(a, b, ca=1, cb=0, precision=None):
    return lax.dot_general(a, b, (((ca,), (cb,)), ((), ())), preferred_element_type=F32, precision=precision)


def _tile(n, pref, unit=LANES):
    best = None
    for t in range(unit, min(n, pref) + 1, unit):
        if n % t == 0:
            best = t
    return best if best is not None else n


def _sigmoid(x):
    return 1.0 / (1.0 + jnp.exp(-x))


def _silu(x):
    return x * _sigmoid(x)


def _dsilu(x):
    s = _sigmoid(x)
    return s * (1.0 + x * (1.0 - s))


def _my_place():
    return lax.axis_index("x"), lax.axis_index("y"), lax.axis_index("c")


def _flip(v, f):
    return 1 - v if f else v


def _gather_small(block, name):
    rows, cols = block.shape

    def body(x_ref, out_ref, send_sems, recv_sems):
        x, y, c = _my_place()
        me = 4 * x + 2 * y + c
        out_ref[me] = x_ref[...]
        copies = []
        for k in range(1, N_DEV):
            peer = (_flip(x, k & 4), _flip(y, k & 2), _flip(c, k & 1))
            cp = pltpu.make_async_remote_copy(
                src_ref=x_ref, dst_ref=out_ref.at[me], send_sem=send_sems.at[k - 1], recv_sem=recv_sems.at[k - 1],
                device_id=peer, device_id_type=MESH)
            cp.start()
            copies.append(cp)
        for cp in copies:
            cp.wait()

    return _pc(
        body, name=name,
        out_shape=jax.ShapeDtypeStruct((N_DEV, rows, cols), block.dtype),
        in_specs=[pl.BlockSpec(memory_space=pltpu.VMEM)],
        out_specs=pl.BlockSpec(memory_space=pltpu.VMEM),
        scratch_shapes=[pltpu.SemaphoreType.DMA((N_DEV - 1,)), pltpu.SemaphoreType.DMA((N_DEV - 1,))],
    )(block)


def _exchange_copies(srcs, lands, send_sems, recv_sems, per_dest):
    x, y, c = _my_place()
    me = 4 * x + 2 * y + c
    copies = []
    for t, (src, land) in enumerate(zip(srcs, lands)):
        for k in range(1, N_DEV):
            peer = (_flip(x, k & 4), _flip(y, k & 2), _flip(c, k & 1))
            dest = 4 * peer[0] + 2 * peer[1] + peer[2]
            s = t * (N_DEV - 1) + k - 1
            copies.append(pltpu.make_async_remote_copy(
                src_ref=src.at[dest] if per_dest else src, dst_ref=land.at[me],
                send_sem=send_sems[s], recv_sem=recv_sems[s], device_id=peer, device_id_type=MESH))
    return copies


def _exchange_start(srcs, per_dest, after, name):
    nt = len(srcs)
    ns = nt * (N_DEV - 1)
    lands = [lax.empty((N_DEV,) + (s.shape[1:] if per_dest else s.shape), s.dtype) for s in srcs]

    def body(*refs):
        ins, outs = refs[:2 * nt + 1], refs[2 * nt + 1:]
        for cp in _exchange_copies(ins[:nt], ins[nt:2 * nt], outs[:ns], outs[ns:2 * ns], per_dest):
            cp.start()
        outs[-1][...] = jnp.zeros_like(outs[-1])

    hbm = [pltpu.with_memory_space_constraint(a, pltpu.HBM) for a in list(srcs) + lands]
    res = _pc(
        body, name=name,
        out_shape=(*[pltpu.SemaphoreType.DMA(())] * (2 * ns), *[pltpu.HBM(a.shape, a.dtype) for a in hbm],
                   jax.ShapeDtypeStruct((8, LANES), F32)),
        in_specs=[HBM] * (2 * nt) + [ANY],
        out_specs=(*[SEM] * (2 * ns), *[HBM] * (2 * nt), pl.BlockSpec(memory_space=pltpu.VMEM)),
        input_output_aliases={i: 2 * ns + i for i in range(2 * nt)},
        compiler_params=pltpu.CompilerParams(has_side_effects=EFFECT),
    )(*hbm, after)
    sems, rest = res[:2 * ns], res[2 * ns:]
    return list(sems[:ns]), list(sems[ns:]), list(rest[:nt]), list(rest[nt:2 * nt]), rest[-1]


def _exchange_wait(state, per_dest, after, name):
    send_sems, recv_sems, srcs, lands, _ = state
    nt = len(srcs)
    ns = len(send_sems)

    def body(*refs):
        sems = refs[2 * nt:2 * nt + 2 * ns]
        for cp in _exchange_copies(refs[:nt], refs[nt:2 * nt], sems[:ns], sems[ns:], per_dest):
            cp.wait_send()
            cp.wait_recv()

    thru = list(srcs) + list(lands)
    res = _pc(
        body, name=name,
        out_shape=tuple(pltpu.HBM(a.shape, a.dtype) for a in thru),
        in_specs=[HBM] * (2 * nt) + [SEM] * (2 * ns) + [ANY],
        out_specs=tuple([HBM] * (2 * nt)),
        input_output_aliases={i: i for i in range(2 * nt)},
        compiler_params=pltpu.CompilerParams(has_side_effects=EFFECT),
    )(*thru, *send_sems, *recv_sems, after)
    me = 4 * lax.axis_index("x") + 2 * lax.axis_index("y") + lax.axis_index("c")
    out = []
    for src, land in zip(res[:nt], res[nt:]):
        own = lax.dynamic_index_in_dim(src, me, 0, keepdims=True) if per_dest else src[None]
        out.append(lax.dynamic_update_slice_in_dim(land, own, me, 0))
    return out


def _mod_fwd(cond, ada_w, bias):
    depth, d, nb = ada_w.shape

    def body(c_ref, w_ref, b_ref, o_ref):
        s = _silu(c_ref[...]).astype(BF16)
        o_ref[...] = _dot(s, w_ref[...].astype(BF16)) + b_ref[...]

    return _pc(
        body, name="mod_fwd", grid=(depth,),
        in_specs=[pl.BlockSpec((16, d), lambda i: (0, 0)), pl.BlockSpec((None, d, nb), lambda i: (i, 0, 0)),
                  pl.BlockSpec((None, 1, nb), lambda i: (i, 0, 0))],
        out_specs=pl.BlockSpec((None, 16, nb), lambda i: (i, 0, 0)),
        out_shape=jax.ShapeDtypeStruct((depth, 16, nb), F32),
        compiler_params=_params("parallel"),
    )(cond, ada_w, bias.reshape(depth, 1, nb))


def _mod_bwd(cond, ada_w, dm_all, dm_mine):
    depth, d, nb = ada_w.shape
    d3 = dm_all.shape[-1]

    def body(c_ref, w_ref, all_ref, call_ref, mine_ref, cmine_ref, gw_ref, gb_ref, part_ref, ds_ref):
        i = pl.program_id(0)
        cond_v = c_ref[...]
        s = _silu(cond_v).astype(BF16)
        has_ctx = jnp.where(i < 2, 1.0, 0.0)
        tot_all = jnp.sum(call_ref[...], axis=0, keepdims=True) * has_ctx
        tot_mine = jnp.broadcast_to(jnp.sum(cmine_ref[...], axis=0, keepdims=True) * has_ctx, (8, nb)).astype(BF16)
        gb_ref[...] = jnp.sum(all_ref[...], axis=0, keepdims=True) + tot_all
        gw_ref[...] = _dot(s[0:8], mine_ref[...].astype(BF16), 0, 0) + _dot(s[8:16], tot_mine, 0, 0)
        part = _dot(tot_mine, w_ref[...].astype(BF16), 1, 1)

        @pl.when(i == 0)
        def _():
            part_ref[...] = jnp.zeros_like(part_ref)
            ds_ref[...] = _dsilu(cond_v)

        part_ref[...] += part

    def rows(width, which):
        return pl.BlockSpec((None, N_DEV, width), which)

    layer = lambda i: (i, 0, 0)
    ctx_layer = lambda i: (jnp.minimum(i, 1) + 4, 0, 0)
    return _pc(
        body, name="mod_bwd", grid=(depth,),
        in_specs=[pl.BlockSpec((16, d), lambda i: (0, 0)), pl.BlockSpec((None, d, nb), layer),
                  rows(d3, layer), rows(d3, ctx_layer), rows(nb, layer), rows(nb, ctx_layer)],
        out_specs=[pl.BlockSpec((None, d, nb), layer), pl.BlockSpec((None, 1, d3), layer),
                   pl.BlockSpec((8, d), lambda i: (0, 0)), pl.BlockSpec((16, d), lambda i: (0, 0))],
        out_shape=[jax.ShapeDtypeStruct((depth, d, nb), F32), jax.ShapeDtypeStruct((depth, 1, d3), F32),
                   jax.ShapeDtypeStruct((8, d), F32), jax.ShapeDtypeStruct((16, d), F32)],
        compiler_params=_params("arbitrary"),
    )(cond, ada_w, dm_all, dm_all, dm_mine, dm_mine)


def _norm_fwd(xs, g, mod, tr, seg_tiles, name):
    t, d = xs.shape

    def body(x_ref, g_ref, mod_ref, h_ref):
        x = x_ref[...]
        r = lax.rsqrt(jnp.mean(x * x, axis=-1, keepdims=True) + RMS_EPS)
        y = (x * r) * g_ref[...]
        h_ref[...] = (y * (1.0 + mod_ref[1:2, :]) + mod_ref[0:1, :]).astype(BF16)

    return _pc(
        body, name=name, grid=(t // tr,),
        in_specs=[pl.BlockSpec((tr, d), lambda i: (i, 0)), pl.BlockSpec((1, d), lambda i: (0, 0)),
                  pl.BlockSpec((None, 8, d), lambda i: (i // seg_tiles, 0, 0))],
        out_specs=pl.BlockSpec((tr, d), lambda i: (i, 0)),
        out_shape=jax.ShapeDtypeStruct((t, d), BF16),
        compiler_params=_params("parallel"),
    )(xs, g, mod)


def _norm_bwd(xs, dh, dres, g, mod, tr, seg_tiles, name):
    t, d = xs.shape
    nseg = mod.shape[0]

    def body(x_ref, dh_ref, dres_ref, g_ref, mod_ref, dx_ref, sum_ref):
        i = pl.program_id(0)
        x = x_ref[...]
        r = lax.rsqrt(jnp.mean(x * x, axis=-1, keepdims=True) + RMS_EPS)
        xn = x * r
        dhv = dh_ref[...]
        gain = g_ref[...]
        one_scale = 1.0 + mod_ref[1:2, :]
        dxn = dhv * (gain * one_scale)
        dx_ref[...] = dres_ref[...] + r * (dxn - xn * jnp.mean(dxn * xn, axis=-1, keepdims=True))

        @pl.when(i % seg_tiles == 0)
        def _():
            sum_ref[...] = jnp.zeros_like(sum_ref)

        sum_ref[0:1, :] += jnp.sum(dhv, axis=0, keepdims=True)
        sum_ref[1:2, :] += jnp.sum(dhv * (xn * gain), axis=0, keepdims=True)
        sum_ref[2:3, :] += jnp.sum(dhv * one_scale * xn, axis=0, keepdims=True)

    row = pl.BlockSpec((tr, d), lambda i: (i, 0))
    seg = pl.BlockSpec((None, 8, d), lambda i: (i // seg_tiles, 0, 0))
    return _pc(
        body, name=name, grid=(t // tr,),
        in_specs=[row, row, row, pl.BlockSpec((1, d), lambda i: (0, 0)), seg],
        out_specs=[row, seg],
        out_shape=[jax.ShapeDtypeStruct((t, d), F32), jax.ShapeDtypeStruct((nseg, 8, d), F32)],
        compiler_params=_params("arbitrary"),
    )(xs, dh, dres, g, mod)


def _resid_bwd(dx, yx, mod, tr, seg_tiles, name):
    t, d = yx.shape
    nseg = mod.shape[0]

    def body(dx_ref, yx_ref, mod_ref, dyx_ref, sum_ref):
        i = pl.program_id(0)
        dxv = dx_ref[...]
        dyx_ref[...] = (dxv * mod_ref[2:3, :]).astype(BF16)

        @pl.when(i % seg_tiles == 0)
        def _():
            sum_ref[...] = jnp.zeros_like(sum_ref)

        sum_ref[0:1, :] += jnp.sum(dxv * yx_ref[...], axis=0, keepdims=True)

    row = pl.BlockSpec((tr, d), lambda i: (i, 0))
    seg = pl.BlockSpec((None, 8, d), lambda i: (i // seg_tiles, 0, 0))
    return _pc(
        body, name=name, grid=(t // tr,),
        in_specs=[row, row, seg], out_specs=[row, seg],
        out_shape=[jax.ShapeDtypeStruct((t, d), BF16), jax.ShapeDtypeStruct((nseg, 8, d), F32)],
        compiler_params=_params("arbitrary"),
    )(dx, yx, mod)


def _loss_head(xs, target, g, tr):
    t, d = xs.shape

    def body(x_ref, t_ref, g_ref, loss_ref, dx_ref, dg_ref):
        i = pl.program_id(0)
        x = x_ref[...]
        r = lax.rsqrt(jnp.mean(x * x, axis=-1, keepdims=True) + RMS_EPS)
        xn = x * r
        gain = g_ref[...]
        err = xn * gain - t_ref[...]
        dy = err * (1.0 / d)
        dxn = dy * gain
        dx_ref[...] = r * (dxn - xn * jnp.mean(dxn * xn, axis=-1, keepdims=True))

        @pl.when(i == 0)
        def _():
            loss_ref[...] = jnp.zeros_like(loss_ref)
            dg_ref[...] = jnp.zeros_like(dg_ref)

        loss_ref[...] += 0.5 * jnp.sum(jnp.mean(err * err, axis=-1, keepdims=True))
        dg_ref[0:1, :] += jnp.sum(dy * xn, axis=0, keepdims=True)

    row = pl.BlockSpec((tr, d), lambda i: (i, 0))
    return _pc(
        body, name="loss_head", grid=(t // tr,),
        in_specs=[row, row, pl.BlockSpec((1, d), lambda i: (0, 0))],
        out_specs=[pl.BlockSpec((8, LANES), lambda i: (0, 0)), row, pl.BlockSpec((8, d), lambda i: (0, 0))],
        out_shape=[jax.ShapeDtypeStruct((8, LANES), F32), jax.ShapeDtypeStruct((t, d), F32),
                   jax.ShapeDtypeStruct((8, d), F32)],
        compiler_params=_params("arbitrary"),
    )(xs, target, g)


def _proj_in(h, w, layer, width, name, blocks=None, dtype=F32):
    t, d = h.shape
    n8 = w.shape[-1]
    first, count = blocks if blocks is not None else (0, N_DEV)
    per_part = width // n8
    tm = _tile(t, 1152)

    def body(a_ref, b_ref, o_ref):
        o_ref[...] = _dot(a_ref[...], b_ref[...]).astype(dtype)

    return _pc(
        body, name=name, grid=(t // tm, count),
        in_specs=[pl.BlockSpec((tm, d), lambda i, j: (i, 0)),
                  pl.BlockSpec((None, None, d, n8), lambda i, j: (first + j, layer, 0, 0))],
        out_specs=pl.BlockSpec((None, tm, n8), lambda i, j: (j // per_part, i, j % per_part)),
        out_shape=jax.ShapeDtypeStruct((count // per_part, t, width), dtype),
        compiler_params=_params("parallel", "parallel"),
    )(h, w)


def _proj_out(z, w, res, mod, tm, seg_tiles, name):
    t, k = z.shape
    d = w.shape[1]

    def body(z_ref, w_ref, res_ref, mod_ref, yx_ref, x_ref):
        yx = _dot(z_ref[...], w_ref[...])
        yx_ref[...] = yx
        x_ref[...] = res_ref[...] + mod_ref[2:3, :] * yx

    tile = pl.BlockSpec((tm, d), lambda i: (i, 0))
    return _pc(
        body, name=name, grid=(t // tm,),
        in_specs=[pl.BlockSpec((tm, k), lambda i: (i, 0)), pl.BlockSpec((k, d), lambda i: (0, 0)), tile,
                  pl.BlockSpec((None, 8, d), lambda i: (i // seg_tiles, 0, 0))],
        out_specs=[tile, tile],
        out_shape=[jax.ShapeDtypeStruct((t, d), F32), jax.ShapeDtypeStruct((t, d), F32)],
        compiler_params=_params("parallel"),
    )(z, w, res, mod)


def _proj_out_dz(dyx, w, name):
    t, d = dyx.shape
    width = w.shape[0]
    tm, tn = _tile(t, 1024), _tile(width, 512)

    def body(a_ref, w_ref, o_ref):
        o_ref[...] = _dot(a_ref[...], w_ref[...], 1, 1)

    return _pc(
        body, name=name, grid=(t // tm, width // tn),
        in_specs=[pl.BlockSpec((tm, d), lambda i, j: (i, 0)), pl.BlockSpec((tn, d), lambda i, j: (j, 0))],
        out_specs=pl.BlockSpec((tm, tn), lambda i, j: (i, j)),
        out_shape=jax.ShapeDtypeStruct((t, width), F32),
        compiler_params=_params("parallel", "parallel"),
    )(dyx, w)


def _proj_in_dh(dpre, w, layer, name):
    parts, t, width = dpre.shape
    d, n8 = w.shape[-2:]
    per_part = width // n8
    tm, tn = _tile(t, 1152), _tile(d, 512)

    def body(a_ref, w_ref, o_ref, acc_ref):
        k = pl.program_id(2)

        @pl.when(k == 0)
        def _():
            acc_ref[...] = jnp.zeros_like(acc_ref)

        acc_ref[...] += _dot(a_ref[...], w_ref[...], 1, 1)

        @pl.when(k == N_DEV - 1)
        def _():
            o_ref[...] = acc_ref[...]

    return _pc(
        body, name=name, grid=(t // tm, d // tn, N_DEV),
        in_specs=[pl.BlockSpec((None, tm, n8), lambda i, j, k: (k // per_part, i, k % per_part)),
                  pl.BlockSpec((None, None, tn, n8), lambda i, j, k: (k, layer, j, 0))],
        out_specs=pl.BlockSpec((tm, tn), lambda i, j, k: (i, j)),
        out_shape=jax.ShapeDtypeStruct((t, d), F32),
        scratch_shapes=[pltpu.VMEM((tm, tn), F32)],
        compiler_params=_params("parallel", "parallel", "arbitrary"),
    )(dpre, w)


def _transposed(a_ref):
    return a_ref[...].astype(F32).T.astype(BF16)


def _grad_w_in(h, dpre, n8, name):
    t, d = h.shape
    parts, _, width = dpre.shape
    per_part = width // n8
    tm, tk = _tile(d, 512), _tile(t, 384)
    nk = t // tk

    def body(a_ref, b_ref, o_ref, acc_ref):
        k = pl.program_id(1)

        @pl.when(k == 0)
        def _():
            acc_ref[...] = jnp.zeros_like(acc_ref)

        at = _transposed(a_ref)
        for p in range(parts):
            r = _dot(at, b_ref[p])
            for s in range(per_part):
                acc_ref[p * per_part + s] += r[:, s * n8:(s + 1) * n8]

        @pl.when(k == nk - 1)
        def _():
            o_ref[...] = acc_ref[...].astype(BF16)

    return _pc(
        body, name=name, grid=(d // tm, nk),
        in_specs=[pl.BlockSpec((tk, tm), lambda i, k: (k, i)), pl.BlockSpec((parts, tk, width), lambda i, k: (0, k, 0))],
        out_specs=pl.BlockSpec((parts * per_part, tm, n8), lambda i, k: (0, i, 0)),
        out_shape=jax.ShapeDtypeStruct((parts * per_part, d, n8), BF16),
        scratch_shapes=[pltpu.VMEM((parts * per_part, tm, n8), F32)],
        compiler_params=_params("parallel", "arbitrary"),
    )(h, dpre)


def _grad_w_out(z, dyx, name):
    width = z.shape[1]
    t, d = dyx.shape
    tm, tk = _tile(width, 512), _tile(t, 384)
    nk = t // tk

    def body(a_ref, b_ref, o_ref, acc_ref):
        k = pl.program_id(1)

        @pl.when(k == 0)
        def _():
            acc_ref[...] = jnp.zeros_like(acc_ref)

        acc_ref[...] += _dot(_transposed(a_ref), b_ref[...])

        @pl.when(k == nk - 1)
        def _():
            o_ref[...] = acc_ref[...].astype(BF16)

    return _pc(
        body, name=name, grid=(width // tm, nk),
        in_specs=[pl.BlockSpec((tk, tm), lambda i, k: (k, i)), pl.BlockSpec((tk, d), lambda i, k: (k, 0))],
        out_specs=pl.BlockSpec((tm, d), lambda i, k: (i, 0)),
        out_shape=jax.ShapeDtypeStruct((width, d), BF16),
        scratch_shapes=[pltpu.VMEM((tm, d), F32)],
        compiler_params=_params("parallel", "arbitrary"),
    )(z, dyx)


def _shift(v, k):
    n = v.shape[0]
    return pltpu.roll(v, k % n, 0)


def _window_sum(v, win):
    s = v + _shift(v, 1)
    step = 1
    while 2 * step < win:
        s = _shift(s, step) + _shift(s, -step)
        step *= 2
    return s


def _window_count(base, seg_len, win, shape):
    t = base + lax.broadcasted_iota(jnp.int32, shape, 0)
    hi = jnp.minimum(t + win // 2, seg_len)
    lo = jnp.maximum(t - win // 2, 0)
    return (hi - lo).astype(F32)


def _pad_offsets(segs):
    return [HALO * (s + 1) + st for s, (st, _) in enumerate(segs)]


def _for_chunks(segs, fn):
    offs = _pad_offsets(segs)
    for s, (st, ln) in enumerate(segs):
        def step(ci, carry, s=s, st=st, ln=ln):
            fn(s, st, ln, offs[s], pl.multiple_of(ci * CHUNK, CHUNK))
            return carry
        lax.fori_loop(0, ln // CHUNK, step, 0)


def _pool_fwd(pre, w_grp, scale, segs, name):
    _, t, width = pre.shape
    grp = width // len(POOL_WINDOWS)
    padded = t + HALO * (len(segs) + 1)

    def group(win, pre_ref, w_ref, sc_ref, z_ref, diff_ref, pad_ref):
        pad_ref[...] = jnp.zeros_like(pad_ref)

        def fill(s, st, ln, off, b):
            pad_ref[pl.ds(off + b, CHUNK), :] = pre_ref[0, pl.ds(st + b, CHUNK), :]

        _for_chunks(segs, fill)

        def mix(s, st, ln, off, b):
            ext = pad_ref[pl.ds(off - HALO + b, CHUNK + 2 * HALO), :]
            total = _window_sum(ext, win)[HALO:HALO + CHUNK]
            u = pre_ref[0, pl.ds(st + b, CHUNK), :]
            diff = (total / _window_count(b, ln, win, u.shape) - u).astype(BF16)
            mixed = _dot(diff, w_ref[...])
            gate = _silu(pre_ref[1, pl.ds(st + b, CHUNK), :])
            z_ref[pl.ds(st + b, CHUNK), :] = (mixed * sc_ref[...] * gate).astype(BF16)
            diff_ref[pl.ds(st + b, CHUNK), :] = diff

        _for_chunks(segs, mix)

    def body(pre_ref, w_ref, sc_ref, z_ref, diff_ref, pad_ref):
        gi = pl.program_id(0)
        for widx, win in enumerate(POOL_WINDOWS):
            @pl.when(gi == widx)
            def _(win=win):
                group(win, pre_ref, w_ref, sc_ref, z_ref, diff_ref, pad_ref)

    col = pl.BlockSpec((t, grp), lambda g: (0, g))
    return _pc(
        body, name=name, grid=(len(POOL_WINDOWS),),
        in_specs=[pl.BlockSpec((2, t, grp), lambda g: (0, 0, g)), pl.BlockSpec((None, grp, grp), lambda g: (g, 0, 0)),
                  pl.BlockSpec((1, grp), lambda g: (0, g))],
        out_specs=[col, col],
        out_shape=[jax.ShapeDtypeStruct((t, width), BF16), jax.ShapeDtypeStruct((t, width), BF16)],
        scratch_shapes=[pltpu.VMEM((padded, grp), F32)],
        compiler_params=_params("parallel"),
    )(pre, w_grp, scale)


def _pool_bwd(dz, diff, pre, w_grp, scale, segs, name):
    _, t, width = pre.shape
    grp = width // len(POOL_WINDOWS)
    padded = t + HALO * (len(segs) + 1)

    def group(win, dz_ref, diff_ref, pre_ref, w_ref, sc_ref, dpre_ref, dw_ref, dsc_ref, pad_ref, dd_ref):
        pad_ref[...] = jnp.zeros_like(pad_ref)
        dw_ref[...] = jnp.zeros_like(dw_ref)
        dsc_ref[...] = jnp.zeros_like(dsc_ref)

        def first(s, st, ln, off, b):
            rows = pl.ds(st + b, CHUNK)
            diff_v = diff_ref[rows, :]
            mixed = _dot(diff_v, w_ref[...])
            g = pre_ref[1, rows, :]
            sg = _silu(g)
            dzv = dz_ref[rows, :]
            dmixed = (dzv * sc_ref[...] * sg).astype(BF16)
            dsc_ref[...] += jnp.sum(dzv * mixed * sg, axis=0, keepdims=True)
            dpre_ref[1, rows, :] = (dzv * mixed * sc_ref[...] * _dsilu(g)).astype(BF16)
            ddiff = _dot(dmixed, w_ref[...], 1, 1)
            dw_ref[...] += _dot(diff_v, dmixed, 0, 0)
            dd_ref[rows, :] = ddiff
            pad_ref[pl.ds(off + b, CHUNK), :] = ddiff / _window_count(b, ln, win, ddiff.shape)

        _for_chunks(segs, first)

        def second(s, st, ln, off, b):
            rows = pl.ds(st + b, CHUNK)
            ext = pad_ref[pl.ds(off - HALO + b, CHUNK + 2 * HALO), :]
            total = _shift(_window_sum(ext, win), -1)[HALO:HALO + CHUNK]
            dpre_ref[0, rows, :] = (total - dd_ref[rows, :]).astype(BF16)

        _for_chunks(segs, second)

    def body(dz_ref, diff_ref, pre_ref, w_ref, sc_ref, dpre_ref, dw_ref, dsc_ref, pad_ref, dd_ref):
        gi = pl.program_id(0)
        for widx, win in enumerate(POOL_WINDOWS):
            @pl.when(gi == widx)
            def _(win=win):
                group(win, dz_ref, diff_ref, pre_ref, w_ref, sc_ref, dpre_ref, dw_ref, dsc_ref, pad_ref, dd_ref)

    col = pl.BlockSpec((t, grp), lambda g: (0, g))
    both = pl.BlockSpec((2, t, grp), lambda g: (0, 0, g))
    wspec = pl.BlockSpec((None, grp, grp), lambda g: (g, 0, 0))
    sspec = pl.BlockSpec((1, grp), lambda g: (0, g))
    return _pc(
        body, name=name, grid=(len(POOL_WINDOWS),),
        in_specs=[col, col, both, wspec, sspec],
        out_specs=[both, wspec, sspec],
        out_shape=[jax.ShapeDtypeStruct((2, t, width), BF16), jax.ShapeDtypeStruct((len(POOL_WINDOWS), grp, grp), F32),
                   jax.ShapeDtypeStruct((1, width), F32)],
        scratch_shapes=[pltpu.VMEM((padded, grp), F32), pltpu.VMEM((t, grp), F32)],
        compiler_params=_params("parallel"),
    )(dz, diff, pre, w_grp, scale)


def _conv_fwd(pre, dw, db, name):
    _, t, width = pre.shape
    cb = LANES
    segs = [(0, t)]

    def body(pre_ref, dw_ref, db_ref, z_ref, pad_ref):
        pad_ref[...] = jnp.zeros_like(pad_ref)

        def fill(s, st, ln, off, b):
            rows = pl.ds(b, CHUNK)
            pad_ref[pl.ds(off + b, CHUNK), :] = pre_ref[1, rows, :] * pre_ref[2, rows, :]

        _for_chunks(segs, fill)

        def mix(s, st, ln, off, b):
            rows = pl.ds(b, CHUNK)
            ext = pad_ref[pl.ds(off - HALO + b, CHUNK + 2 * HALO), :]
            conv = (dw_ref[0:1, :] * _shift(ext, 1) + dw_ref[1:2, :] * ext + dw_ref[2:3, :] * _shift(ext, -1))
            conv = conv[HALO:HALO + CHUNK] + db_ref[...]
            y = pre_ref[0, rows, :] * conv
            z_ref[rows, :] = (y * _silu(pre_ref[3, rows, :])).astype(BF16)

        _for_chunks(segs, mix)

    return _pc(
        body, name=name, grid=(width // cb,),
        in_specs=[pl.BlockSpec((4, t, cb), lambda j: (0, 0, j)), pl.BlockSpec((8, cb), lambda j: (0, j)),
                  pl.BlockSpec((1, cb), lambda j: (0, j))],
        out_specs=pl.BlockSpec((t, cb), lambda j: (0, j)),
        out_shape=jax.ShapeDtypeStruct((t, width), BF16),
        scratch_shapes=[pltpu.VMEM((t + 2 * HALO, cb), F32)],
        compiler_params=_params("parallel"),
    )(pre, dw, db)


def _conv_bwd(dz, pre, dw, db, name):
    _, t, width = pre.shape
    cb = LANES
    segs = [(0, t)]

    def body(dz_ref, pre_ref, dw_ref, db_ref, dpre_ref, ddw_ref, ddb_ref, pad_a, pad_c):
        pad_a[...] = jnp.zeros_like(pad_a)
        pad_c[...] = jnp.zeros_like(pad_c)
        ddw_ref[...] = jnp.zeros_like(ddw_ref)
        ddb_ref[...] = jnp.zeros_like(ddb_ref)

        def fill(s, st, ln, off, b):
            rows = pl.ds(b, CHUNK)
            pad_a[pl.ds(off + b, CHUNK), :] = pre_ref[1, rows, :] * pre_ref[2, rows, :]

        _for_chunks(segs, fill)

        def first(s, st, ln, off, b):
            rows = pl.ds(b, CHUNK)
            ext = pad_a[pl.ds(off - HALO + b, CHUNK + 2 * HALO), :]
            prev, nxt = _shift(ext, 1)[HALO:HALO + CHUNK], _shift(ext, -1)[HALO:HALO + CHUNK]
            here = ext[HALO:HALO + CHUNK]
            conv = dw_ref[0:1, :] * prev + dw_ref[1:2, :] * here + dw_ref[2:3, :] * nxt + db_ref[...]
            bg, g = pre_ref[0, rows, :], pre_ref[3, rows, :]
            dzv = dz_ref[rows, :]
            dy = dzv * _silu(g)
            dpre_ref[3, rows, :] = (dzv * (bg * conv) * _dsilu(g)).astype(BF16)
            dpre_ref[0, rows, :] = (dy * conv).astype(BF16)
            dconv = dy * bg
            pad_c[pl.ds(off + b, CHUNK), :] = dconv
            ddw_ref[0:1, :] += jnp.sum(dconv * prev, axis=0, keepdims=True)
            ddw_ref[1:2, :] += jnp.sum(dconv * here, axis=0, keepdims=True)
            ddw_ref[2:3, :] += jnp.sum(dconv * nxt, axis=0, keepdims=True)
            ddb_ref[0:1, :] += jnp.sum(dconv, axis=0, keepdims=True)

        _for_chunks(segs, first)

        def second(s, st, ln, off, b):
            rows = pl.ds(b, CHUNK)
            ext = pad_c[pl.ds(off - HALO + b, CHUNK + 2 * HALO), :]
            da = (dw_ref[0:1, :] * _shift(ext, -1) + dw_ref[1:2, :] * ext + dw_ref[2:3, :] * _shift(ext, 1))
            da = da[HALO:HALO + CHUNK]
            dpre_ref[1, rows, :] = (da * pre_ref[2, rows, :]).astype(BF16)
            dpre_ref[2, rows, :] = (da * pre_ref[1, rows, :]).astype(BF16)

        _for_chunks(segs, second)

    quad = pl.BlockSpec((4, t, cb), lambda j: (0, 0, j))
    rows8 = pl.BlockSpec((8, cb), lambda j: (0, j))
    return _pc(
        body, name=name, grid=(width // cb,),
        in_specs=[pl.BlockSpec((t, cb), lambda j: (0, j)), quad, rows8, pl.BlockSpec((1, cb), lambda j: (0, j))],
        out_specs=[quad, rows8, rows8],
        out_shape=[jax.ShapeDtypeStruct((4, t, width), BF16), jax.ShapeDtypeStruct((8, width), F32),
                   jax.ShapeDtypeStruct((8, width), F32)],
        scratch_shapes=[pltpu.VMEM((t + 2 * HALO, cb), F32), pltpu.VMEM((t + 2 * HALO, cb), F32)],
        compiler_params=_params("parallel"),
    )(dz, pre, dw, db)


PAIR_TILES = 2 * WIN_ROWS - 2


def _pair_geometry():
    lane = lax.broadcasted_iota(jnp.int32, (GRID_W, LANES), 1)
    qcol = lax.broadcasted_iota(jnp.int32, (GRID_W, LANES), 0)
    low = lane < GRID_W
    kcol = jnp.where(low, lane, lane - GRID_W)
    start = jnp.clip(qcol - WIN_COLS // 2, 0, GRID_W - WIN_COLS)
    inside = (kcol >= start) & (kcol < start + WIN_COLS)
    return low, inside


def _bias_tiles(rpb_ref, rows_ref, tiles_ref, inside):
    for h in range(2):
        rows = rpb_ref[h]
        rows_ref[h] = (pltpu.roll(rows, LANES - (WIN_COLS - 1), 1)
                       + pltpu.roll(pltpu.roll(rows, GRID_W - (WIN_COLS - 1), 1), 2 * WIN_ROWS - 1, 0))
        for t in range(PAIR_TILES):
            both = jnp.broadcast_to(rows_ref[h, t:t + 1, :], (GRID_W, LANES))
            tiles_ref[h, t] = jnp.where(inside, pltpu.roll(both, 0, 1, stride=1, stride_axis=0), MASKED)


def _bias_tiles_grad(dtiles_ref, drpb_ref):
    n = PAIR_TILES * GRID_W
    qcol = lax.broadcasted_iota(jnp.int32, (n, LANES), 0) & (GRID_W - 1)
    lane = lax.broadcasted_iota(jnp.int32, (1, LANES), 1)
    zero = jnp.zeros((1, LANES), F32)
    for h in range(2):
        v = pltpu.roll(dtiles_ref[h].reshape(n, LANES), WIN_COLS - 1, 1)
        for bit in range(6):
            v = jnp.where((qcol >> bit) & 1 == 1, pltpu.roll(v, LANES - (1 << bit), 1), v)
        sums = [jnp.sum(v[t * GRID_W:(t + 1) * GRID_W], axis=0, keepdims=True) for t in range(PAIR_TILES)]
        for r in range(2 * WIN_ROWS):
            here = sums[r] if r < PAIR_TILES else zero
            prev = pltpu.roll(sums[r - 1], GRID_W, 1) if 1 <= r <= PAIR_TILES else zero
            drpb_ref[h, r:r + 1, :] = jnp.where(lane < 2 * WIN_COLS - 1, here + prev, 0.0)


def _attn_rows(r, n_rows):
    first = jnp.clip(r - WIN_ROWS // 2, 0, n_rows - WIN_ROWS)
    return first, first - r + WIN_ROWS - 1


def _softmax(s_loc, s_ctx):
    m = jnp.maximum(jnp.max(s_loc, axis=-1, keepdims=True), jnp.max(s_ctx, axis=-1, keepdims=True))
    e_loc, e_ctx = jnp.exp(s_loc - m), jnp.exp(s_ctx - m)
    inv = 1.0 / (jnp.sum(e_loc, axis=-1, keepdims=True) + jnp.sum(e_ctx, axis=-1, keepdims=True))
    return e_loc * inv, e_ctx * inv


def _pair_bias(tiles_ref, h, j):
    return jnp.concatenate([tiles_ref[h, j + 2 * m] for m in range(WIN_ROWS // 2)], axis=1)


ROWS_PER_STEP = 2


def _attn_items(step, n_rows, q_ref, low):
    items = []
    for u in range(ROWS_PER_STEP):
        r = step * ROWS_PER_STEP + u
        first, j = _attn_rows(r, n_rows)
        rows = pl.ds(pl.multiple_of(r * GRID_W, GRID_W), GRID_W)
        keys = pl.ds(pl.multiple_of(first * GRID_W, GRID_W), WIN_ROWS * GRID_W)
        q = (q_ref[rows, :].astype(F32) * HEAD_DIM ** -0.5).astype(BF16)
        zero = jnp.zeros_like(q)
        items.append((rows, keys, j, 0, jnp.where(low, q, zero)))
        items.append((rows, keys, j, 1, jnp.where(low, zero, q)))
    return items


def _attn_fwd(qkv, gate, rpb, seq):
    _, t, width = qkv.shape
    n_rows = seq // GRID_W
    n_ctx = t - seq
    blk = WIN_ROWS * GRID_W

    def body(q_ref, k_ref, v_ref, g_ref, rpb_ref, z_ref, o_ref, rows_ref, tiles_ref):
        low, inside = _pair_geometry()
        _bias_tiles(rpb_ref, rows_ref, tiles_ref, inside)
        ctx = pl.ds(seq, n_ctx)

        def step(i, carry):
            items = _attn_items(i, n_rows, q_ref, low)
            k_ctx, v_ctx = k_ref[ctx, :], v_ref[ctx, :]
            scores = [(_dot(q, k_ref[keys, :], 1, 1) + _pair_bias(tiles_ref, h, j), _dot(q, k_ctx, 1, 1))
                      for _, keys, j, h, q in items]
            probs = [_softmax(s_loc, s_ctx) for s_loc, s_ctx in scores]
            outs = [_dot(p_loc.astype(BF16), v_ref[keys, :]) + _dot(p_ctx.astype(BF16), v_ctx)
                    for (_, keys, _, _, _), (p_loc, p_ctx) in zip(items, probs)]
            for u in range(ROWS_PER_STEP):
                rows = items[2 * u][0]
                o = jnp.where(low, outs[2 * u], outs[2 * u + 1])
                o_ref[rows, :] = o
                z_ref[rows, :] = (o * _silu(g_ref[rows, :])).astype(BF16)
            return carry

        lax.fori_loop(0, n_rows // ROWS_PER_STEP, step, 0)

    def part(p):
        return pl.BlockSpec((None, t, LANES), lambda h: (p, 0, h))

    out = pl.BlockSpec((seq, LANES), lambda h: (0, h))
    return _pc(
        body, name="attn_fwd", grid=(width // LANES,),
        in_specs=[part(0), part(1), part(2), part(0), pl.BlockSpec((2, 2 * WIN_ROWS, LANES), lambda h: (h, 0, 0))],
        out_specs=[out, out],
        out_shape=[jax.ShapeDtypeStruct((seq, width), BF16), jax.ShapeDtypeStruct((seq, width), F32)],
        scratch_shapes=[pltpu.VMEM((2, 2 * WIN_ROWS, LANES), F32), pltpu.VMEM((2, PAIR_TILES, GRID_W, LANES), F32)],
        compiler_params=_params("parallel"),
    )(qkv, qkv, qkv, gate, rpb)


def _attn_bwd(qkv, gate, o, dz, rpb, seq):
    _, t, width = qkv.shape
    n_rows = seq // GRID_W
    n_ctx = t - seq
    blk = WIN_ROWS * GRID_W
    heads = 2 * width // LANES

    def body(q_ref, k_ref, v_ref, g_ref, o_ref, dz_ref, rpb_ref, dpre_ref, drpb_ref,
             rows_ref, tiles_ref, dtiles_ref, dk_ref, dv_ref):
        low, inside = _pair_geometry()
        _bias_tiles(rpb_ref, rows_ref, tiles_ref, inside)
        dtiles_ref[...] = jnp.zeros_like(dtiles_ref)
        dk_ref[...] = jnp.zeros_like(dk_ref)
        dv_ref[...] = jnp.zeros_like(dv_ref)
        ctx = pl.ds(seq, n_ctx)

        def step(i, carry):
            items = _attn_items(i, n_rows, q_ref, low)
            k_ctx, v_ctx = k_ref[ctx, :], v_ref[ctx, :]
            d_outs = []
            for u in range(ROWS_PER_STEP):
                rows = items[2 * u][0]
                g = g_ref[rows, :]
                dzv = dz_ref[rows, :]
                dpre_ref[3, rows, :] = (dzv * o_ref[rows, :] * _dsilu(g)).astype(BF16)
                d_o = (dzv * _silu(g)).astype(BF16)
                zero = jnp.zeros_like(d_o)
                d_outs += [jnp.where(low, d_o, zero), jnp.where(low, zero, d_o)]
            scores = [(_dot(q, k_ref[keys, :], 1, 1) + _pair_bias(tiles_ref, h, j), _dot(q, k_ctx, 1, 1))
                      for _, keys, j, h, q in items]
            dprobs = [(_dot(doh, v_ref[keys, :], 1, 1), _dot(doh, v_ctx, 1, 1))
                      for (_, keys, _, _, _), doh in zip(items, d_outs)]
            probs = [_softmax(s_loc, s_ctx) for s_loc, s_ctx in scores]
            dscores = []
            for (p_loc, p_ctx), (dp_loc, dp_ctx) in zip(probs, dprobs):
                delta = (jnp.sum(p_loc * dp_loc, axis=-1, keepdims=True)
                         + jnp.sum(p_ctx * dp_ctx, axis=-1, keepdims=True))
                dscores.append((p_loc * (dp_loc - delta), p_ctx * (dp_ctx - delta)))
            dqs = [_dot(ds_loc.astype(BF16), k_ref[keys, :]) + _dot(ds_ctx.astype(BF16), k_ctx)
                   for (_, keys, _, _, _), (ds_loc, ds_ctx) in zip(items, dscores)]
            for u in range(ROWS_PER_STEP):
                rows = items[2 * u][0]
                dpre_ref[0, rows, :] = (jnp.where(low, dqs[2 * u], dqs[2 * u + 1]) * HEAD_DIM ** -0.5).astype(BF16)
            for (_, keys, j, h, q), doh, (p_loc, p_ctx), (ds_loc, ds_ctx) in zip(items, d_outs, probs, dscores):
                dk_ref[keys, :] += _dot(ds_loc.astype(BF16), q, 0, 0)
                dk_ref[ctx, :] += _dot(ds_ctx.astype(BF16), q, 0, 0)
                dv_ref[keys, :] += _dot(p_loc.astype(BF16), doh, 0, 0)
                dv_ref[ctx, :] += _dot(p_ctx.astype(BF16), doh, 0, 0)
                for m in range(WIN_ROWS // 2):
                    dtiles_ref[h, j + 2 * m] += ds_loc[:, m * LANES:(m + 1) * LANES]
            return carry

        lax.fori_loop(0, n_rows // ROWS_PER_STEP, step, 0)
        dpre_ref[1] = dk_ref[...].astype(BF16)
        dpre_ref[2] = dv_ref[...].astype(BF16)
        dpre_ref[0, ctx, :] = jnp.zeros((n_ctx, LANES), BF16)
        dpre_ref[3, ctx, :] = jnp.zeros((n_ctx, LANES), BF16)
        _bias_tiles_grad(dtiles_ref, drpb_ref)

    def part(p):
        return pl.BlockSpec((None, t, LANES), lambda h: (p, 0, h))

    lat = pl.BlockSpec((seq, LANES), lambda h: (0, h))
    rspec = pl.BlockSpec((2, 2 * WIN_ROWS, LANES), lambda h: (h, 0, 0))
    tiles = pltpu.VMEM((2, PAIR_TILES, GRID_W, LANES), F32)
    return _pc(
        body, name="attn_bwd", grid=(width // LANES,),
        in_specs=[part(0), part(1), part(2), part(0), lat, lat, rspec],
        out_specs=[pl.BlockSpec((4, t, LANES), lambda h: (0, 0, h)), rspec],
        out_shape=[jax.ShapeDtypeStruct((4, t, width), BF16), jax.ShapeDtypeStruct((heads, 2 * WIN_ROWS, LANES), F32)],
        scratch_shapes=[pltpu.VMEM((2, 2 * WIN_ROWS, LANES), F32), tiles, tiles,
                        pltpu.VMEM((t, LANES), F32), pltpu.VMEM((t, LANES), F32)],
        compiler_params=_params("parallel"),
    )(qkv, qkv, qkv, gate, o, dz, rpb)


def _adamw(w, m, v, parts, name, mult=None):
    rows, cols = w.shape
    tr = _tile(rows, max(8, 131072 // cols), 8)
    n_parts = len(parts)
    c1 = 1.0 - ADAM_B1 ** ADAM_STEP
    c2 = 1.0 - ADAM_B2 ** ADAM_STEP

    def body(*refs):
        w_ref, m_ref, v_ref = refs[:3]
        part_refs = refs[3:3 + n_parts]
        rest = refs[3 + n_parts:]
        g = part_refs[0][...].astype(F32)
        for p in part_refs[1:]:
            g = g + p[...].astype(F32)
        if mult is not None:
            g = g * rest[0][...]
            rest = rest[1:]
        g_ref, d_ref, nm_ref, nv_ref = rest
        m2 = ADAM_B1 * m_ref[...] + (1.0 - ADAM_B1) * g
        v2 = ADAM_B2 * v_ref[...] + (1.0 - ADAM_B2) * (g * g)
        m_hat = m2 / c1
        v_hat = v2 / c2
        g_ref[...] = g
        d_ref[...] = -ADAM_LR * (m_hat / (jnp.sqrt(v_hat) + ADAM_EPS) + ADAM_WD * w_ref[...])
        nm_ref[...] = m2
        nv_ref[...] = v2

    tile = pl.BlockSpec((tr, cols), lambda i: (i, 0))
    in_specs, args = [tile, tile, tile], [w, m, v]
    for p in parts:
        if isinstance(p, tuple):
            arr, k = p
            in_specs.append(pl.BlockSpec((None, tr, cols), lambda i, k=k: (k, i, 0)))
            args.append(arr)
        else:
            in_specs.append(tile)
            args.append(p)
    if mult is not None:
        in_specs.append(tile)
        args.append(mult)
    shape = jax.ShapeDtypeStruct((rows, cols), F32)
    return _pc(
        body, name=name, grid=(rows // tr,), in_specs=in_specs, out_specs=[tile] * 4, out_shape=[shape] * 4,
        compiler_params=_params("parallel"),
    )(*args)


def _rows128(a):
    flat = a.reshape(-1)
    pad = (-flat.shape[0]) % LANES
    if pad:
        flat = jnp.concatenate([flat, jnp.zeros((pad,), flat.dtype)])
    return flat.reshape(-1, LANES)


def _pad_rows(a, mult=8):
    pad = (-a.shape[0]) % mult
    if pad:
        a = jnp.concatenate([a, jnp.zeros((pad,) + a.shape[1:], a.dtype)], axis=0)
    return a


def kernel(x, c, ctx, c_ctx, norm_g, ada_w, ada_b, pool_w_in, pool_w_grp, pool_scale, pool_w_out, na_w_in, na_rpb, na_w_out, conv_w_in, conv_dw, conv_db, conv_w_out, final_g, loss_target, m_c_ctx, m_norm_g, m_ada_w, m_ada_b, m_pool_w_in, m_pool_w_grp, m_pool_scale, m_pool_w_out, m_na_w_in, m_na_rpb, m_na_w_out, m_conv_w_in, m_conv_dw, m_conv_db, m_conv_w_out, m_final_g, v_c_ctx, v_norm_g, v_ada_w, v_ada_b, v_pool_w_in, v_pool_w_grp, v_pool_scale, v_pool_w_out, v_na_w_in, v_na_rpb, v_na_w_out, v_conv_w_in, v_conv_dw, v_conv_db, v_conv_w_out, v_final_g):
    xi, yi, ci = _my_place()
    me = 4 * xi + 2 * yi + ci
    seq, d = x.shape[1], x.shape[2]
    n_ctx = ctx.shape[1]
    t_all = seq + n_ctx
    width = d
    depth = norm_g.shape[0]
    nb = ada_w.shape[2]
    shard = width // N_DEV
    d_rows = d // LANES
    tr = math.gcd(math.gcd(seq, n_ctx), 256)
    x_tiles = seq // tr

    n_pool = pool_scale.shape[0]
    n_grp = pool_w_grp.shape[1]
    grp = width // n_grp
    layer_weights = [[pool_w_in[0], pool_w_grp[0], pool_w_out[0]], [na_w_in[0], na_w_out[0]],
                     [conv_w_in[0], conv_w_out[0]], [pool_w_in[1], pool_w_grp[1], pool_w_out[1]]]
    in_flight, token = [], jnp.zeros((8, LANES), F32)
    for i, ws in enumerate(layer_weights):
        state = _exchange_start([w.astype(BF16) for w in ws], False, token, f"weights_start{i}")
        token = state[-1]
        in_flight.append(state)

    def pool_weights(i, after):
        w_in, w_grp, w_out = _exchange_wait(in_flight[i], False, after, f"weights_wait{i}")
        return w_in[:, None], w_grp.transpose(1, 0, 2, 3).reshape(n_grp, grp, grp), w_out.reshape(width, d)

    def pair_weights(i, after):
        w_in, w_out = _exchange_wait(in_flight[i], False, after, f"weights_wait{i}")
        return w_in[:, None], w_out.reshape(width, d)

    small_in =_pad_rows(jnp.concatenate([_rows128(c), pool_scale, conv_dw[0], conv_db], axis=0))
    got = _gather_small(small_in, "gather_inputs")
    r0 = d_rows
    c_all = got[:, :r0].reshape(N_DEV, d)
    n_pool = pool_scale.shape[0]
    scale_full = got[:, r0:r0 + n_pool].transpose(1, 0, 2).reshape(n_pool, width)
    r1 = r0 + n_pool
    taps_full = _pad_rows(got[:, r1:r1 + 3].transpose(1, 0, 2).reshape(3, width))
    bias_full = got[:, r1 + 3:r1 + 4].transpose(1, 0, 2).reshape(1, width)

    cond = jnp.concatenate([c_all, c_ctx[None], jnp.zeros((7, d), F32)], axis=0)
    bias_mine = lax.dynamic_slice(ada_b, (0, me * nb), (depth, nb))
    mod_mine = _mod_fwd(cond, ada_w, bias_mine)
    mod_all = _gather_small(mod_mine.reshape(-1, LANES), "gather_mod")
    mod_all = mod_all.reshape(N_DEV, depth, 16, nb).transpose(1, 2, 0, 3).reshape(depth, 16, 3 * d)
    mod_x = lax.dynamic_index_in_dim(mod_all, me, 1, keepdims=False).reshape(depth, 3, d)
    mod_c = mod_all[:, 8].reshape(depth, 3, d)
    pad5 = jnp.zeros((depth, 5, d), F32)
    mod_x = jnp.concatenate([mod_x, pad5], axis=1)
    mod_c = jnp.concatenate([mod_c, pad5], axis=1)
    mods = [jnp.stack([mod_x[i], mod_c[i]]) if i < 2 else mod_x[i][None] for i in range(depth)]

    both = [(0, seq), (seq, n_ctx)]
    latent = [(0, seq)]

    def grp_slots(g):
        return g.reshape(n_grp, N_DEV, grp // N_DEV, grp).transpose(1, 0, 2, 3).reshape(N_DEV, -1, grp).astype(BF16)

    def send_grads(i, grads):
        return _exchange_start(grads, True, jnp.zeros((8, LANES), F32), f"grads_start{i}")

    xs0 = jnp.concatenate([x[0], ctx[0]], axis=0)
    h0 = _norm_fwd(xs0, norm_g[0:1], mods[0], tr, x_tiles, "norm_fwd0")
    pool_in_w0, pool_grp_w0, pool_out_w0 = pool_weights(0, h0)
    pre0 = _proj_in(h0, pool_in_w0, 0, width, "proj_in0")
    z0, diff0 = _pool_fwd(pre0, pool_grp_w0, scale_full[0:1], both, "pool_fwd0")
    yx0, xs1 = _proj_out(z0, pool_out_w0, xs0, mods[0], tr, x_tiles, "proj_out0")

    h1 = _norm_fwd(xs1, norm_g[1:2], mods[1], tr, x_tiles, "norm_fwd1")
    na_in_w, na_out_w = pair_weights(1, h1)
    per_part = width // na_w_in.shape[2]
    qkv1 = _proj_in(h1, na_in_w, 0, width, "proj_in1_qkv", blocks=(0, 3 * per_part), dtype=BF16)
    gpre1 = _proj_in(h1, na_in_w, 0, width, "proj_in1_gate", blocks=(3 * per_part, per_part))
    rpb_rows = jnp.pad(na_rpb[0], ((0, 0), (0, 2 * WIN_ROWS - na_rpb.shape[2]), (0, LANES - na_rpb.shape[3])))
    z1, o1 = _attn_fwd(qkv1, gpre1, rpb_rows, seq)
    yx1, x2 = _proj_out(z1, na_out_w, xs1, mods[1], tr, x_tiles, "proj_out1")

    h2 = _norm_fwd(x2, norm_g[2:3], mods[2], tr, x_tiles, "norm_fwd2")
    conv_in_w, conv_out_w = pair_weights(2, h2)
    pre2 = _proj_in(h2, conv_in_w, 0, width, "proj_in2")
    z2 = _conv_fwd(pre2, taps_full, bias_full, "conv_fwd")
    yx2, x3 = _proj_out(z2, conv_out_w, x2, mods[2], tr, x_tiles, "proj_out2")

    h3 = _norm_fwd(x3, norm_g[3:4], mods[3], tr, x_tiles, "norm_fwd3")
    pool_in_w3, pool_grp_w3, pool_out_w3 = pool_weights(3, h3)
    pre3 = _proj_in(h3, pool_in_w3, 0, width, "proj_in3")
    z3, diff3 = _pool_fwd(pre3, pool_grp_w3, scale_full[1:2], latent, "pool_fwd3")
    yx3, x4 = _proj_out(z3, pool_out_w3, x3, mods[3], tr, x_tiles, "proj_out3")

    loss_part, dx4, d_final = _loss_head(x4, loss_target[0], final_g[None], tr)
    loss = lax.psum(loss_part[0, 0], ("x", "y", "c"))

    dyx3, gate3 = _resid_bwd(dx4, yx3, mods[3], tr, x_tiles, "resid_bwd3")
    dz3 = _proj_out_dz(dyx3, pool_out_w3, "proj_out_dz3")
    g_pool_out1 = _grad_w_out(z3, dyx3, "grad_w_out3")
    dpre3, g_grp1, g_scale1 = _pool_bwd(dz3, diff3, pre3, pool_grp_w3, scale_full[1:2], latent, "pool_bwd3")
    dh3 = _proj_in_dh(dpre3, pool_in_w3, 0, "proj_in_dh3")
    g_pool_in1 = _grad_w_in(h3, dpre3, pool_w_in.shape[2], "grad_w_in3")
    sent3 = send_grads(3, [g_pool_in1, grp_slots(g_grp1), g_pool_out1.reshape(N_DEV, shard, d)])
    dx3, norm3 = _norm_bwd(x3, dh3, dx4, norm_g[3:4], mods[3], tr, x_tiles, "norm_bwd3")

    dyx2, gate2 = _resid_bwd(dx3, yx2, mods[2], tr, x_tiles, "resid_bwd2")
    dz2 = _proj_out_dz(dyx2, conv_out_w, "proj_out_dz2")
    g_conv_out = _grad_w_out(z2, dyx2, "grad_w_out2")
    dpre2, g_taps, g_cbias = _conv_bwd(dz2, pre2, taps_full, bias_full, "conv_bwd")
    dh2 = _proj_in_dh(dpre2, conv_in_w, 0, "proj_in_dh2")
    g_conv_in = _grad_w_in(h2, dpre2, conv_w_in.shape[2], "grad_w_in2")
    sent2 = send_grads(2, [g_conv_in, g_conv_out.reshape(N_DEV, shard, d)])
    dx2, norm2 = _norm_bwd(x2, dh2, dx3, norm_g[2:3], mods[2], tr, x_tiles, "norm_bwd2")

    dyx1, gate1 = _resid_bwd(dx2, yx1, mods[1][:1], tr, x_tiles, "resid_bwd1")
    dz1 = _proj_out_dz(dyx1, na_out_w, "proj_out_dz1")
    g_na_out = _grad_w_out(z1, dyx1, "grad_w_out1")
    dpre1, g_rpb = _attn_bwd(qkv1, gpre1, o1, dz1, rpb_rows, seq)
    g_rpb = g_rpb[:, :na_rpb.shape[2], :na_rpb.shape[3]]
    dh1 = _proj_in_dh(dpre1, na_in_w, 0, "proj_in_dh1")
    g_na_in = _grad_w_in(h1, dpre1, na_w_in.shape[2], "grad_w_in1")
    sent1 = send_grads(1, [g_na_in, g_na_out.reshape(N_DEV, shard, d)])
    dres1 = jnp.concatenate([dx2, jnp.zeros((n_ctx, d), F32)], axis=0)
    dxs1, norm1 = _norm_bwd(xs1, dh1, dres1, norm_g[1:2], mods[1], tr, x_tiles, "norm_bwd1")

    dyx0, gate0 = _resid_bwd(dxs1, yx0, mods[0], tr, x_tiles, "resid_bwd0")
    dz0 = _proj_out_dz(dyx0, pool_out_w0, "proj_out_dz0")
    g_pool_out0 = _grad_w_out(z0, dyx0, "grad_w_out0")
    dpre0, g_grp0, g_scale0 = _pool_bwd(dz0, diff0, pre0, pool_grp_w0, scale_full[0:1], both, "pool_bwd0")
    dh0 = _proj_in_dh(dpre0, pool_in_w0, 0, "proj_in_dh0")
    g_pool_in0 = _grad_w_in(h0, dpre0, pool_w_in.shape[2], "grad_w_in0")
    sent0 = send_grads(0, [g_pool_in0, grp_slots(g_grp0), g_pool_out0.reshape(N_DEV, shard, d)])
    dxs0, norm0 = _norm_bwd(xs0, dh0, dxs1, norm_g[0:1], mods[0], tr, x_tiles, "norm_bwd0")
    grad_x = dxs0[:seq][None]

    norms, gates = [norm0, norm1, norm2, norm3], [gate0, gate1, gate2, gate3]
    zero_d = jnp.zeros((d,), F32)
    dm_rows = [jnp.concatenate([norms[i][0, 0], norms[i][0, 1], gates[i][0, 0]]) for i in range(depth)]
    dm_rows.append(jnp.concatenate([norm0[1, 0], norm0[1, 1], gate0[1, 0]]))
    dm_rows.append(jnp.concatenate([norm1[1, 0], norm1[1, 1], zero_d]))
    dm_local = jnp.stack(dm_rows + [jnp.zeros((3 * d,), F32)] * 2)
    g_norm_part = jnp.stack([norm0[0, 2] + norm0[1, 2], norm1[0, 2] + norm1[1, 2], norm2[0, 2], norm3[0, 2]])
    g_scale_part = jnp.concatenate([g_scale0, g_scale1], axis=0)
    pieces = [_rows128(dm_local), _rows128(g_norm_part), _rows128(d_final[0]), _pad_rows(_rows128(g_rpb)),
              _rows128(g_scale_part), _rows128(g_taps[:3]), _rows128(g_cbias[0])]
    small_out = _gather_small(jnp.concatenate(pieces, axis=0), "gather_small_grads")
    marks = np.cumsum([0] + [p.shape[0] for p in pieces])

    def piece(k):
        return small_out[:, marks[k]:marks[k + 1]]

    dm_all = piece(0).reshape(N_DEV, 8, 3 * d).transpose(1, 0, 2)
    dm_mine = lax.dynamic_slice(dm_all, (0, 0, me * nb), (8, N_DEV, nb))
    g_ada_w, g_ada_b, cctx_part, dsilu_cond = _mod_bwd(cond, ada_w, dm_all, dm_mine)
    cctx_all = _gather_small(_rows128(cctx_part[0]), "gather_cctx")

    def my_shard(a, n):
        a = a.reshape(N_DEV, n, N_DEV, shard)
        return lax.dynamic_index_in_dim(a, me, 2, keepdims=False)

    zeros7 = lambda r: jnp.zeros((N_DEV - 1, r, LANES), F32)
    ada_b_rows = _rows128(g_ada_b)
    small_parts = jnp.concatenate([
        cctx_all, piece(1), jnp.concatenate([ada_b_rows[None], zeros7(ada_b_rows.shape[0])], axis=0), piece(2), piece(3),
        my_shard(piece(4), n_pool), my_shard(piece(5), 3), my_shard(piece(6), 1)], axis=1)
    n_small = small_parts.shape[1]
    small_parts = jnp.concatenate([small_parts, jnp.zeros((N_DEV, (-n_small) % 8, LANES), F32)], axis=1)

    def pack(c_ctx_, norm_g_, ada_b_, final_g_, rpb_, scale_, taps_, cbias_):
        rows = [_rows128(c_ctx_), _rows128(norm_g_), _rows128(ada_b_), _rows128(final_g_), _pad_rows(_rows128(rpb_)),
                scale_, taps_[0], cbias_]
        return _pad_rows(jnp.concatenate(rows, axis=0)), np.cumsum([0] + [r.shape[0] for r in rows])

    w_small, smarks = pack(c_ctx, norm_g, ada_b, final_g, na_rpb, pool_scale, conv_dw, conv_db)
    m_small, _ = pack(m_c_ctx, m_norm_g, m_ada_b, m_final_g, m_na_rpb, m_pool_scale, m_conv_dw, m_conv_db)
    v_small, _ = pack(v_c_ctx, v_norm_g, v_ada_b, v_final_g, v_na_rpb, v_pool_scale, v_conv_dw, v_conv_db)
    mult = jnp.concatenate([_rows128(dsilu_cond[8]), jnp.ones((w_small.shape[0] - d_rows, LANES), F32)], axis=0)
    small_res = _adamw(w_small, m_small, v_small, [(small_parts, k) for k in range(N_DEV)], "adamw_small", mult=mult)

    def unpack(k, like):
        out = []
        for r in small_res:
            flat = r[smarks[k]:smarks[k + 1]].reshape(-1)
            out.append(flat[:like.size].reshape(like.shape))
        return out

    res = {"c_ctx": unpack(0, c_ctx), "norm_g": unpack(1, norm_g), "ada_b": unpack(2, ada_b),
           "final_g": unpack(3, final_g), "na_rpb": unpack(4, na_rpb), "pool_scale": unpack(5, pool_scale),
           "conv_dw": unpack(6, conv_dw), "conv_db": unpack(7, conv_db)}

    res["ada_w"] = [r.reshape(ada_w.shape) for r in _adamw(
        ada_w.reshape(-1, nb), m_ada_w.reshape(-1, nb), v_ada_w.reshape(-1, nb), [g_ada_w.reshape(-1, nb)], "adamw_ada_w")]

    def landed(state, i):
        return _exchange_wait(state, True, grad_x, f"grads_wait{i}")

    in3, grp3, out3 = landed(sent3, 3)
    in2, out2 = landed(sent2, 2)
    in1, out1 = landed(sent1, 1)
    in0, grp0, out0 = landed(sent0, 0)

    def big(parts, w, m, v, name):
        shape = w.shape
        view = (-1, shape[-1])
        parts = [(parts.reshape((N_DEV,) + w.reshape(view).shape), k) for k in range(N_DEV)]
        return [r.reshape(shape) for r in _adamw(w.reshape(view), m.reshape(view), v.reshape(view), parts, name)]

    def two_layers(first, second, w, m, v, name):
        a = big(first, w[0], m[0], v[0], name + "0")
        b = big(second, w[1], m[1], v[1], name + "1")
        return [jnp.stack([p, q]) for p, q in zip(a, b)]

    res["pool_w_in"] = two_layers(in0, in3, pool_w_in, m_pool_w_in, v_pool_w_in, "adamw_pool_in")
    res["pool_w_grp"] = two_layers(grp0, grp3, pool_w_grp, m_pool_w_grp, v_pool_w_grp, "adamw_pool_grp")
    res["pool_w_out"] = two_layers(out0, out3, pool_w_out, m_pool_w_out, v_pool_w_out, "adamw_pool_out")
    res["na_w_in"] = [r[None] for r in big(in1, na_w_in[0], m_na_w_in[0], v_na_w_in[0], "adamw_na_in")]
    res["na_w_out"] = [r[None] for r in big(out1, na_w_out[0], m_na_w_out[0], v_na_w_out[0], "adamw_na_out")]
    res["conv_w_in"] = [r[None] for r in big(in2, conv_w_in[0], m_conv_w_in[0], v_conv_w_in[0], "adamw_conv_in")]
    res["conv_w_out"] = [r[None] for r in big(out2, conv_w_out[0], m_conv_w_out[0], v_conv_w_out[0], "adamw_conv_out")]

    order = ["c_ctx", "norm_g", "ada_w", "ada_b", "pool_w_in", "pool_w_grp", "pool_scale", "pool_w_out", "na_w_in",
             "na_rpb", "na_w_out", "conv_w_in", "conv_dw", "conv_db", "conv_w_out", "final_g"]
    outs = [loss, grad_x]
    for j in range(4):
        outs += [res[n][j] for n in order]
    return tuple(outs)
```

```python
import functools
import math

import numpy as np
import jax
import jax.numpy as jnp
from jax import lax
from jax.experimental import pallas as pl
from jax.experimental.pallas import tpu as pltpu

F32 = jnp.float32
BF16 = jnp.bfloat16
N_DEV = 8
LANES = 128
RMS_EPS = 1e-6
GRID_W = 64
WIN_ROWS = 8
WIN_COLS = 16
HEAD_DIM = 64
POOL_WINDOWS = (2, 4, 8, 16)
HALO = 8
CHUNK = 128
MASKED = -1e30
ADAM_LR = 0.001
ADAM_B1 = 0.9
ADAM_B2 = 0.999
ADAM_EPS = 1e-08
ADAM_WD = 0.01
ADAM_STEP = 10
VMEM_LIMIT = 56 * 1024 * 1024
MESH = pl.DeviceIdType.MESH
ANY = pl.BlockSpec(memory_space=pl.ANY)
HBM = pl.BlockSpec(memory_space=pltpu.HBM)
SEM = pl.BlockSpec(memory_space=pltpu.SEMAPHORE)
EFFECT = pltpu.SideEffectType.DATAFLOW_SIDE_EFFECTING


def _pc(body, *, name, **kw):
    return pl.pallas_call(body, name=name, **kw)


def _params(*sem):
    return pltpu.CompilerParams(dimension_semantics=sem if sem else None, vmem_limit_bytes=VMEM_LIMIT)


def _dot(a, b, ca=1, cb=0, precision=None):
    return lax.dot_general(a, b, (((ca,), (cb,)), ((), ())), preferred_element_type=F32, precision=precision)


def _tile(n, pref, unit=LANES):
    best = None
    for t in range(unit, min(n, pref) + 1, unit):
        if n % t == 0:
            best = t
    return best if best is not None else n


def _sigmoid(x):
    return 1.0 / (1.0 + jnp.exp(-x))


def _silu(x):
    return x * _sigmoid(x)


def _dsilu(x):
    s = _sigmoid(x)
    return s * (1.0 + x * (1.0 - s))


def _my_place():
    return lax.axis_index("x"), lax.axis_index("y"), lax.axis_index("c")


def _flip(v, f):
    return 1 - v if f else v


def _gather_small(block, name):
    rows, cols = block.shape

    def body(x_ref, out_ref, send_sems, recv_sems):
        x, y, c = _my_place()
        me = 4 * x + 2 * y + c
        out_ref[me] = x_ref[...]
        copies = []
        for k in range(1, N_DEV):
            peer = (_flip(x, k & 4), _flip(y, k & 2), _flip(c, k & 1))
            cp = pltpu.make_async_remote_copy(
                src_ref=x_ref, dst_ref=out_ref.at[me], send_sem=send_sems.at[k - 1], recv_sem=recv_sems.at[k - 1],
                device_id=peer, device_id_type=MESH)
            cp.start()
            copies.append(cp)
        for cp in copies:
            cp.wait()

    return _pc(
        body, name=name,
        out_shape=jax.ShapeDtypeStruct((N_DEV, rows, cols), block.dtype),
        in_specs=[pl.BlockSpec(memory_space=pltpu.VMEM)],
        out_specs=pl.BlockSpec(memory_space=pltpu.VMEM),
        scratch_shapes=[pltpu.SemaphoreType.DMA((N_DEV - 1,)), pltpu.SemaphoreType.DMA((N_DEV - 1,))],
    )(block)


def _exchange_copies(srcs, lands, send_sems, recv_sems, per_dest):
    x, y, c = _my_place()
    me = 4 * x + 2 * y + c
    copies = []
    for t, (src, land) in enumerate(zip(srcs, lands)):
        for k in range(1, N_DEV):
            peer = (_flip(x, k & 4), _flip(y, k & 2), _flip(c, k & 1))
            dest = 4 * peer[0] + 2 * peer[1] + peer[2]
            s = t * (N_DEV - 1) + k - 1
            copies.append(pltpu.make_async_remote_copy(
                src_ref=src.at[dest] if per_dest else src, dst_ref=land.at[me],
                send_sem=send_sems[s], recv_sem=recv_sems[s], device_id=peer, device_id_type=MESH))
    return copies


def _exchange_start(srcs, per_dest, after, name):
    nt = len(srcs)
    ns = nt * (N_DEV - 1)
    lands = [lax.empty((N_DEV,) + (s.shape[1:] if per_dest else s.shape), s.dtype) for s in srcs]

    def body(*refs):
        ins, outs = refs[:2 * nt + 1], refs[2 * nt + 1:]
        for cp in _exchange_copies(ins[:nt], ins[nt:2 * nt], outs[:ns], outs[ns:2 * ns], per_dest):
            cp.start()
        outs[-1][...] = jnp.zeros_like(outs[-1])

    hbm = [pltpu.with_memory_space_constraint(a, pltpu.HBM) for a in list(srcs) + lands]
    res = _pc(
        body, name=name,
        out_shape=(*[pltpu.SemaphoreType.DMA(())] * (2 * ns), *[pltpu.HBM(a.shape, a.dtype) for a in hbm],
                   jax.ShapeDtypeStruct((8, LANES), F32)),
        in_specs=[HBM] * (2 * nt) + [ANY],
        out_specs=(*[SEM] * (2 * ns), *[HBM] * (2 * nt), pl.BlockSpec(memory_space=pltpu.VMEM)),
        input_output_aliases={i: 2 * ns + i for i in range(2 * nt)},
        compiler_params=pltpu.CompilerParams(has_side_effects=EFFECT),
    )(*hbm, after)
    sems, rest = res[:2 * ns], res[2 * ns:]
    return list(sems[:ns]), list(sems[ns:]), list(rest[:nt]), list(rest[nt:2 * nt]), rest[-1]


def _exchange_wait(state, per_dest, after, name):
    send_sems, recv_sems, srcs, lands, _ = state
    nt = len(srcs)
    ns = len(send_sems)

    def body(*refs):
        sems = refs[2 * nt:2 * nt + 2 * ns]
        for cp in _exchange_copies(refs[:nt], refs[nt:2 * nt], sems[:ns], sems[ns:], per_dest):
            cp.wait_send()
            cp.wait_recv()

    thru = list(srcs) + list(lands)
    res = _pc(
        body, name=name,
        out_shape=tuple(pltpu.HBM(a.shape, a.dtype) for a in thru),
        in_specs=[HBM] * (2 * nt) + [SEM] * (2 * ns) + [ANY],
        out_specs=tuple([HBM] * (2 * nt)),
        input_output_aliases={i: i for i in range(2 * nt)},
        compiler_params=pltpu.CompilerParams(has_side_effects=EFFECT),
    )(*thru, *send_sems, *recv_sems, after)
    me = 4 * lax.axis_index("x") + 2 * lax.axis_index("y") + lax.axis_index("c")
    out = []
    for src, land in zip(res[:nt], res[nt:]):
        own = lax.dynamic_index_in_dim(src, me, 0, keepdims=True) if per_dest else src[None]
        out.append(lax.dynamic_update_slice_in_dim(land, own, me, 0))
    return out


def _mod_fwd(cond, ada_w, bias):
    depth, d, nb = ada_w.shape

    def body(c_ref, w_ref, b_ref, o_ref):
        s = _silu(c_ref[...]).astype(BF16)
        o_ref[...] = _dot(s, w_ref[...].astype(BF16)) + b_ref[...]

    return _pc(
        body, name="mod_fwd", grid=(depth,),
        in_specs=[pl.BlockSpec((16, d), lambda i: (0, 0)), pl.BlockSpec((None, d, nb), lambda i: (i, 0, 0)),
                  pl.BlockSpec((None, 1, nb), lambda i: (i, 0, 0))],
        out_specs=pl.BlockSpec((None, 16, nb), lambda i: (i, 0, 0)),
        out_shape=jax.ShapeDtypeStruct((depth, 16, nb), F32),
        compiler_params=_params("parallel"),
    )(cond, ada_w, bias.reshape(depth, 1, nb))


def _mod_bwd(cond, ada_w, dm_all, dm_mine):
    depth, d, nb = ada_w.shape
    d3 = dm_all.shape[-1]

    def body(c_ref, w_ref, all_ref, call_ref, mine_ref, cmine_ref, gw_ref, gb_ref, part_ref, ds_ref):
        i = pl.program_id(0)
        cond_v = c_ref[...]
        s = _silu(cond_v).astype(BF16)
        has_ctx = jnp.where(i < 2, 1.0, 0.0)
        tot_all = jnp.sum(call_ref[...], axis=0, keepdims=True) * has_ctx
        tot_mine = jnp.broadcast_to(jnp.sum(cmine_ref[...], axis=0, keepdims=True) * has_ctx, (8, nb)).astype(BF16)
        gb_ref[...] = jnp.sum(all_ref[...], axis=0, keepdims=True) + tot_all
        gw_ref[...] = _dot(s[0:8], mine_ref[...].astype(BF16), 0, 0) + _dot(s[8:16], tot_mine, 0, 0)
        part = _dot(tot_mine, w_ref[...].astype(BF16), 1, 1)

        @pl.when(i == 0)
        def _():
            part_ref[...] = jnp.zeros_like(part_ref)
            ds_ref[...] = _dsilu(cond_v)

        part_ref[...] += part

    def rows(width, which):
        return pl.BlockSpec((None, N_DEV, width), which)

    layer = lambda i: (i, 0, 0)
    ctx_layer = lambda i: (jnp.minimum(i, 1) + 4, 0, 0)
    return _pc(
        body, name="mod_bwd", grid=(depth,),
        in_specs=[pl.BlockSpec((16, d), lambda i: (0, 0)), pl.BlockSpec((None, d, nb), layer),
                  rows(d3, layer), rows(d3, ctx_layer), rows(nb, layer), rows(nb, ctx_layer)],
        out_specs=[pl.BlockSpec((None, d, nb), layer), pl.BlockSpec((None, 1, d3), layer),
                   pl.BlockSpec((8, d), lambda i: (0, 0)), pl.BlockSpec((16, d), lambda i: (0, 0))],
        out_shape=[jax.ShapeDtypeStruct((depth, d, nb), F32), jax.ShapeDtypeStruct((depth, 1, d3), F32),
                   jax.ShapeDtypeStruct((8, d), F32), jax.ShapeDtypeStruct((16, d), F32)],
        compiler_params=_params("arbitrary"),
    )(cond, ada_w, dm_all, dm_all, dm_mine, dm_mine)


def _norm_fwd(xs, g, mod, tr, seg_tiles, name):
    t, d = xs.shape

    def body(x_ref, g_ref, mod_ref, h_ref):
        x = x_ref[...]
        r = lax.rsqrt(jnp.mean(x * x, axis=-1, keepdims=True) + RMS_EPS)
        y = (x * r) * g_ref[...]
        h_ref[...] = (y * (1.0 + mod_ref[1:2, :]) + mod_ref[0:1, :]).astype(BF16)

    return _pc(
        body, name=name, grid=(t // tr,),
        in_specs=[pl.BlockSpec((tr, d), lambda i: (i, 0)), pl.BlockSpec((1, d), lambda i: (0, 0)),
                  pl.BlockSpec((None, 8, d), lambda i: (i // seg_tiles, 0, 0))],
        out_specs=pl.BlockSpec((tr, d), lambda i: (i, 0)),
        out_shape=jax.ShapeDtypeStruct((t, d), BF16),
        compiler_params=_params("parallel"),
    )(xs, g, mod)


def _norm_bwd(xs, dh, dres, g, mod, tr, seg_tiles, name):
    t, d = xs.shape
    nseg = mod.shape[0]

    def body(x_ref, dh_ref, dres_ref, g_ref, mod_ref, dx_ref, sum_ref):
        i = pl.program_id(0)
        x = x_ref[...]
        r = lax.rsqrt(jnp.mean(x * x, axis=-1, keepdims=True) + RMS_EPS)
        xn = x * r
        dhv = dh_ref[...]
        gain = g_ref[...]
        one_scale = 1.0 + mod_ref[1:2, :]
        dxn = dhv * (gain * one_scale)
        dx_ref[...] = dres_ref[...] + r * (dxn - xn * jnp.mean(dxn * xn, axis=-1, keepdims=True))

        @pl.when(i % seg_tiles == 0)
        def _():
            sum_ref[...] = jnp.zeros_like(sum_ref)

        sum_ref[0:1, :] += jnp.sum(dhv, axis=0, keepdims=True)
        sum_ref[1:2, :] += jnp.sum(dhv * (xn * gain), axis=0, keepdims=True)
        sum_ref[2:3, :] += jnp.sum(dhv * one_scale * xn, axis=0, keepdims=True)

    row = pl.BlockSpec((tr, d), lambda i: (i, 0))
    seg = pl.BlockSpec((None, 8, d), lambda i: (i // seg_tiles, 0, 0))
    return _pc(
        body, name=name, grid=(t // tr,),
        in_specs=[row, row, row, pl.BlockSpec((1, d), lambda i: (0, 0)), seg],
        out_specs=[row, seg],
        out_shape=[jax.ShapeDtypeStruct((t, d), F32), jax.ShapeDtypeStruct((nseg, 8, d), F32)],
        compiler_params=_params("arbitrary"),
    )(xs, dh, dres, g, mod)


def _resid_bwd(dx, yx, mod, tr, seg_tiles, name):
    t, d = yx.shape
    nseg = mod.shape[0]

    def body(dx_ref, yx_ref, mod_ref, dyx_ref, sum_ref):
        i = pl.program_id(0)
        dxv = dx_ref[...]
        dyx_ref[...] = (dxv * mod_ref[2:3, :]).astype(BF16)

        @pl.when(i % seg_tiles == 0)
        def _():
            sum_ref[...] = jnp.zeros_like(sum_ref)

        sum_ref[0:1, :] += jnp.sum(dxv * yx_ref[...], axis=0, keepdims=True)

    row = pl.BlockSpec((tr, d), lambda i: (i, 0))
    seg = pl.BlockSpec((None, 8, d), lambda i: (i // seg_tiles, 0, 0))
    return _pc(
        body, name=name, grid=(t // tr,),
        in_specs=[row, row, seg], out_specs=[row, seg],
        out_shape=[jax.ShapeDtypeStruct((t, d), BF16), jax.ShapeDtypeStruct((nseg, 8, d), F32)],
        compiler_params=_params("arbitrary"),
    )(dx, yx, mod)


def _loss_head(xs, target, g, tr):
    t, d = xs.shape

    def body(x_ref, t_ref, g_ref, loss_ref, dx_ref, dg_ref):
        i = pl.program_id(0)
        x = x_ref[...]
        r = lax.rsqrt(jnp.mean(x * x, axis=-1, keepdims=True) + RMS_EPS)
        xn = x * r
        gain = g_ref[...]
        err = xn * gain - t_ref[...]
        dy = err * (1.0 / d)
        dxn = dy * gain
        dx_ref[...] = r * (dxn - xn * jnp.mean(dxn * xn, axis=-1, keepdims=True))

        @pl.when(i == 0)
        def _():
            loss_ref[...] = jnp.zeros_like(loss_ref)
            dg_ref[...] = jnp.zeros_like(dg_ref)

        loss_ref[...] += 0.5 * jnp.sum(jnp.mean(err * err, axis=-1, keepdims=True))
        dg_ref[0:1, :] += jnp.sum(dy * xn, axis=0, keepdims=True)

    row = pl.BlockSpec((tr, d), lambda i: (i, 0))
    return _pc(
        body, name="loss_head", grid=(t // tr,),
        in_specs=[row, row, pl.BlockSpec((1, d), lambda i: (0, 0))],
        out_specs=[pl.BlockSpec((8, LANES), lambda i: (0, 0)), row, pl.BlockSpec((8, d), lambda i: (0, 0))],
        out_shape=[jax.ShapeDtypeStruct((8, LANES), F32), jax.ShapeDtypeStruct((t, d), F32),
                   jax.ShapeDtypeStruct((8, d), F32)],
        compiler_params=_params("arbitrary"),
    )(xs, target, g)


def _proj_in(h, w, layer, width, name, blocks=None, dtype=F32):
    t, d = h.shape
    n8 = w.shape[-1]
    first, count = blocks if blocks is not None else (0, N_DEV)
    per_part = width // n8
    tm = _tile(t, 1152)

    def body(a_ref, b_ref, o_ref):
        o_ref[...] = _dot(a_ref[...], b_ref[...]).astype(dtype)

    return _pc(
        body, name=name, grid=(t // tm, count),
        in_specs=[pl.BlockSpec((tm, d), lambda i, j: (i, 0)),
                  pl.BlockSpec((None, None, d, n8), lambda i, j: (first + j, layer, 0, 0))],
        out_specs=pl.BlockSpec((None, tm, n8), lambda i, j: (j // per_part, i, j % per_part)),
        out_shape=jax.ShapeDtypeStruct((count // per_part, t, width), dtype),
        compiler_params=_params("parallel", "parallel"),
    )(h, w)


def _proj_out(z, w, res, mod, tm, seg_tiles, name):
    t, k = z.shape
    d = w.shape[1]

    def body(z_ref, w_ref, res_ref, mod_ref, yx_ref, x_ref):
        yx = _dot(z_ref[...], w_ref[...])
        yx_ref[...] = yx
        x_ref[...] = res_ref[...] + mod_ref[2:3, :] * yx

    tile = pl.BlockSpec((tm, d), lambda i: (i, 0))
    return _pc(
        body, name=name, grid=(t // tm,),
        in_specs=[pl.BlockSpec((tm, k), lambda i: (i, 0)), pl.BlockSpec((k, d), lambda i: (0, 0)), tile,
                  pl.BlockSpec((None, 8, d), lambda i: (i // seg_tiles, 0, 0))],
        out_specs=[tile, tile],
        out_shape=[jax.ShapeDtypeStruct((t, d), F32), jax.ShapeDtypeStruct((t, d), F32)],
        compiler_params=_params("parallel"),
    )(z, w, res, mod)


def _proj_out_dz(dyx, w, name):
    t, d = dyx.shape
    width = w.shape[0]
    tm, tn = _tile(t, 1024), _tile(width, 512)

    def body(a_ref, w_ref, o_ref):
        o_ref[...] = _dot(a_ref[...], w_ref[...], 1, 1)

    return _pc(
        body, name=name, grid=(t // tm, width // tn),
        in_specs=[pl.BlockSpec((tm, d), lambda i, j: (i, 0)), pl.BlockSpec((tn, d), lambda i, j: (j, 0))],
        out_specs=pl.BlockSpec((tm, tn), lambda i, j: (i, j)),
        out_shape=jax.ShapeDtypeStruct((t, width), F32),
        compiler_params=_params("parallel", "parallel"),
    )(dyx, w)


def _proj_in_dh(dpre, w, layer, name):
    parts, t, width = dpre.shape
    d, n8 = w.shape[-2:]
    per_part = width // n8
    tm, tn = _tile(t, 1152), _tile(d, 512)

    def body(a_ref, w_ref, o_ref, acc_ref):
        k = pl.program_id(2)

        @pl.when(k == 0)
        def _():
            acc_ref[...] = jnp.zeros_like(acc_ref)

        acc_ref[...] += _dot(a_ref[...], w_ref[...], 1, 1)

        @pl.when(k == N_DEV - 1)
        def _():
            o_ref[...] = acc_ref[...]

    return _pc(
        body, name=name, grid=(t // tm, d // tn, N_DEV),
        in_specs=[pl.BlockSpec((None, tm, n8), lambda i, j, k: (k // per_part, i, k % per_part)),
                  pl.BlockSpec((None, None, tn, n8), lambda i, j, k: (k, layer, j, 0))],
        out_specs=pl.BlockSpec((tm, tn), lambda i, j, k: (i, j)),
        out_shape=jax.ShapeDtypeStruct((t, d), F32),
        scratch_shapes=[pltpu.VMEM((tm, tn), F32)],
        compiler_params=_params("parallel", "parallel", "arbitrary"),
    )(dpre, w)


def _transposed(a_ref):
    return a_ref[...].astype(F32).T.astype(BF16)


def _grad_w_in(h, dpre, n8, name):
    t, d = h.shape
    parts, _, width = dpre.shape
    per_part = width // n8
    tm, tk = _tile(d, 512), _tile(t, 384)
    nk = t // tk

    def body(a_ref, b_ref, o_ref, acc_ref):
        k = pl.program_id(1)

        @pl.when(k == 0)
        def _():
            acc_ref[...] = jnp.zeros_like(acc_ref)

        at = _transposed(a_ref)
        for p in range(parts):
            r = _dot(at, b_ref[p])
            for s in range(per_part):
                acc_ref[p * per_part + s] += r[:, s * n8:(s + 1) * n8]

        @pl.when(k == nk - 1)
        def _():
            o_ref[...] = acc_ref[...].astype(BF16)

    return _pc(
        body, name=name, grid=(d // tm, nk),
        in_specs=[pl.BlockSpec((tk, tm), lambda i, k: (k, i)), pl.BlockSpec((parts, tk, width), lambda i, k: (0, k, 0))],
        out_specs=pl.BlockSpec((parts * per_part, tm, n8), lambda i, k: (0, i, 0)),
        out_shape=jax.ShapeDtypeStruct((parts * per_part, d, n8), BF16),
        scratch_shapes=[pltpu.VMEM((parts * per_part, tm, n8), F32)],
        compiler_params=_params("parallel", "arbitrary"),
    )(h, dpre)


def _grad_w_out(z, dyx, name):
    width = z.shape[1]
    t, d = dyx.shape
    tm, tk = _tile(width, 512), _tile(t, 384)
    nk = t // tk

    def body(a_ref, b_ref, o_ref, acc_ref):
        k = pl.program_id(1)

        @pl.when(k == 0)
        def _():
            acc_ref[...] = jnp.zeros_like(acc_ref)

        acc_ref[...] += _dot(_transposed(a_ref), b_ref[...])

        @pl.when(k == nk - 1)
        def _():
            o_ref[...] = acc_ref[...].astype(BF16)

    return _pc(
        body, name=name, grid=(width // tm, nk),
        in_specs=[pl.BlockSpec((tk, tm), lambda i, k: (k, i)), pl.BlockSpec((tk, d), lambda i, k: (k, 0))],
        out_specs=pl.BlockSpec((tm, d), lambda i, k: (i, 0)),
        out_shape=jax.ShapeDtypeStruct((width, d), BF16),
        scratch_shapes=[pltpu.VMEM((tm, d), F32)],
        compiler_params=_params("parallel", "arbitrary"),
    )(z, dyx)


def _shift(v, k):
    n = v.shape[0]
    return pltpu.roll(v, k % n, 0)


def _window_sum(v, win):
    s = v + _shift(v, 1)
    step = 1
    while 2 * step < win:
        s = _shift(s, step) + _shift(s, -step)
        step *= 2
    return s


def _window_count(base, seg_len, win, shape):
    t = base + lax.broadcasted_iota(jnp.int32, shape, 0)
    hi = jnp.minimum(t + win // 2, seg_len)
    lo = jnp.maximum(t - win // 2, 0)
    return (hi - lo).astype(F32)


def _pad_offsets(segs):
    return [HALO * (s + 1) + st for s, (st, _) in enumerate(segs)]


def _for_chunks(segs, fn):
    offs = _pad_offsets(segs)
    for s, (st, ln) in enumerate(segs):
        def step(ci, carry, s=s, st=st, ln=ln):
            fn(s, st, ln, offs[s], pl.multiple_of(ci * CHUNK, CHUNK))
            return carry
        lax.fori_loop(0, ln // CHUNK, step, 0)


def _pool_fwd(pre, w_grp, scale, segs, name):
    _, t, width = pre.shape
    grp = width // len(POOL_WINDOWS)
    padded = t + HALO * (len(segs) + 1)

    def group(win, pre_ref, w_ref, sc_ref, z_ref, diff_ref, pad_ref):
        pad_ref[...] = jnp.zeros_like(pad_ref)

        def fill(s, st, ln, off, b):
            pad_ref[pl.ds(off + b, CHUNK), :] = pre_ref[0, pl.ds(st + b, CHUNK), :]

        _for_chunks(segs, fill)

        def mix(s, st, ln, off, b):
            ext = pad_ref[pl.ds(off - HALO + b, CHUNK + 2 * HALO), :]
            total = _window_sum(ext, win)[HALO:HALO + CHUNK]
            u = pre_ref[0, pl.ds(st + b, CHUNK), :]
            diff = (total / _window_count(b, ln, win, u.shape) - u).astype(BF16)
            mixed = _dot(diff, w_ref[...])
            gate = _silu(pre_ref[1, pl.ds(st + b, CHUNK), :])
            z_ref[pl.ds(st + b, CHUNK), :] = (mixed * sc_ref[...] * gate).astype(BF16)
            diff_ref[pl.ds(st + b, CHUNK), :] = diff

        _for_chunks(segs, mix)

    def body(pre_ref, w_ref, sc_ref, z_ref, diff_ref, pad_ref):
        gi = pl.program_id(0)
        for widx, win in enumerate(POOL_WINDOWS):
            @pl.when(gi == widx)
            def _(win=win):
                group(win, pre_ref, w_ref, sc_ref, z_ref, diff_ref, pad_ref)

    col = pl.BlockSpec((t, grp), lambda g: (0, g))
    return _pc(
        body, name=name, grid=(len(POOL_WINDOWS),),
        in_specs=[pl.BlockSpec((2, t, grp), lambda g: (0, 0, g)), pl.BlockSpec((None, grp, grp), lambda g: (g, 0, 0)),
                  pl.BlockSpec((1, grp), lambda g: (0, g))],
        out_specs=[col, col],
        out_shape=[jax.ShapeDtypeStruct((t, width), BF16), jax.ShapeDtypeStruct((t, width), BF16)],
        scratch_shapes=[pltpu.VMEM((padded, grp), F32)],
        compiler_params=_params("parallel"),
    )(pre, w_grp, scale)


def _pool_bwd(dz, diff, pre, w_grp, scale, segs, name):
    _, t, width = pre.shape
    grp = width // len(POOL_WINDOWS)
    padded = t + HALO * (len(segs) + 1)

    def group(win, dz_ref, diff_ref, pre_ref, w_ref, sc_ref, dpre_ref, dw_ref, dsc_ref, pad_ref, dd_ref):
        pad_ref[...] = jnp.zeros_like(pad_ref)
        dw_ref[...] = jnp.zeros_like(dw_ref)
        dsc_ref[...] = jnp.zeros_like(dsc_ref)

        def first(s, st, ln, off, b):
            rows = pl.ds(st + b, CHUNK)
            diff_v = diff_ref[rows, :]
            mixed = _dot(diff_v, w_ref[...])
            g = pre_ref[1, rows, :]
            sg = _silu(g)
            dzv = dz_ref[rows, :]
            dmixed = (dzv * sc_ref[...] * sg).astype(BF16)
            dsc_ref[...] += jnp.sum(dzv * mixed * sg, axis=0, keepdims=True)
            dpre_ref[1, rows, :] = (dzv * mixed * sc_ref[...] * _dsilu(g)).astype(BF16)
            ddiff = _dot(dmixed, w_ref[...], 1, 1)
            dw_ref[...] += _dot(diff_v, dmixed, 0, 0)
            dd_ref[rows, :] = ddiff
            pad_ref[pl.ds(off + b, CHUNK), :] = ddiff / _window_count(b, ln, win, ddiff.shape)

        _for_chunks(segs, first)

        def second(s, st, ln, off, b):
            rows = pl.ds(st + b, CHUNK)
            ext = pad_ref[pl.ds(off - HALO + b, CHUNK + 2 * HALO), :]
            total = _shift(_window_sum(ext, win), -1)[HALO:HALO + CHUNK]
            dpre_ref[0, rows, :] = (total - dd_ref[rows, :]).astype(BF16)

        _for_chunks(segs, second)

    def body(dz_ref, diff_ref, pre_ref, w_ref, sc_ref, dpre_ref, dw_ref, dsc_ref, pad_ref, dd_ref):
        gi = pl.program_id(0)
        for widx, win in enumerate(POOL_WINDOWS):
            @pl.when(gi == widx)
            def _(win=win):
                group(win, dz_ref, diff_ref, pre_ref, w_ref, sc_ref, dpre_ref, dw_ref, dsc_ref, pad_ref, dd_ref)

    col = pl.BlockSpec((t, grp), lambda g: (0, g))
    both = pl.BlockSpec((2, t, grp), lambda g: (0, 0, g))
    wspec = pl.BlockSpec((None, grp, grp), lambda g: (g, 0, 0))
    sspec = pl.BlockSpec((1, grp), lambda g: (0, g))
    return _pc(
        body, name=name, grid=(len(POOL_WINDOWS),),
        in_specs=[col, col, both, wspec, sspec],
        out_specs=[both, wspec, sspec],
        out_shape=[jax.ShapeDtypeStruct((2, t, width), BF16), jax.ShapeDtypeStruct((len(POOL_WINDOWS), grp, grp), F32),
                   jax.ShapeDtypeStruct((1, width), F32)],
        scratch_shapes=[pltpu.VMEM((padded, grp), F32), pltpu.VMEM((t, grp), F32)],
        compiler_params=_params("parallel"),
    )(dz, diff, pre, w_grp, scale)


def _conv_fwd(pre, dw, db, name):
    _, t, width = pre.shape
    cb = LANES
    segs = [(0, t)]

    def body(pre_ref, dw_ref, db_ref, z_ref, pad_ref):
        pad_ref[...] = jnp.zeros_like(pad_ref)

        def fill(s, st, ln, off, b):
            rows = pl.ds(b, CHUNK)
            pad_ref[pl.ds(off + b, CHUNK), :] = pre_ref[1, rows, :] * pre_ref[2, rows, :]

        _for_chunks(segs, fill)

        def mix(s, st, ln, off, b):
            rows = pl.ds(b, CHUNK)
            ext = pad_ref[pl.ds(off - HALO + b, CHUNK + 2 * HALO), :]
            conv = (dw_ref[0:1, :] * _shift(ext, 1) + dw_ref[1:2, :] * ext + dw_ref[2:3, :] * _shift(ext, -1))
            conv = conv[HALO:HALO + CHUNK] + db_ref[...]
            y = pre_ref[0, rows, :] * conv
            z_ref[rows, :] = (y * _silu(pre_ref[3, rows, :])).astype(BF16)

        _for_chunks(segs, mix)

    return _pc(
        body, name=name, grid=(width // cb,),
        in_specs=[pl.BlockSpec((4, t, cb), lambda j: (0, 0, j)), pl.BlockSpec((8, cb), lambda j: (0, j)),
                  pl.BlockSpec((1, cb), lambda j: (0, j))],
        out_specs=pl.BlockSpec((t, cb), lambda j: (0, j)),
        out_shape=jax.ShapeDtypeStruct((t, width), BF16),
        scratch_shapes=[pltpu.VMEM((t + 2 * HALO, cb), F32)],
        compiler_params=_params("parallel"),
    )(pre, dw, db)


def _conv_bwd(dz, pre, dw, db, name):
    _, t, width = pre.shape
    cb = LANES
    segs = [(0, t)]

    def body(dz_ref, pre_ref, dw_ref, db_ref, dpre_ref, ddw_ref, ddb_ref, pad_a, pad_c):
        pad_a[...] = jnp.zeros_like(pad_a)
        pad_c[...] = jnp.zeros_like(pad_c)
        ddw_ref[...] = jnp.zeros_like(ddw_ref)
        ddb_ref[...] = jnp.zeros_like(ddb_ref)

        def fill(s, st, ln, off, b):
            rows = pl.ds(b, CHUNK)
            pad_a[pl.ds(off + b, CHUNK), :] = pre_ref[1, rows, :] * pre_ref[2, rows, :]

        _for_chunks(segs, fill)

        def first(s, st, ln, off, b):
            rows = pl.ds(b, CHUNK)
            ext = pad_a[pl.ds(off - HALO + b, CHUNK + 2 * HALO), :]
            prev, nxt = _shift(ext, 1)[HALO:HALO + CHUNK], _shift(ext, -1)[HALO:HALO + CHUNK]
            here = ext[HALO:HALO + CHUNK]
            conv = dw_ref[0:1, :] * prev + dw_ref[1:2, :] * here + dw_ref[2:3, :] * nxt + db_ref[...]
            bg, g = pre_ref[0, rows, :], pre_ref[3, rows, :]
            dzv = dz_ref[rows, :]
            dy = dzv * _silu(g)
            dpre_ref[3, rows, :] = (dzv * (bg * conv) * _dsilu(g)).astype(BF16)
            dpre_ref[0, rows, :] = (dy * conv).astype(BF16)
            dconv = dy * bg
            pad_c[pl.ds(off + b, CHUNK), :] = dconv
            ddw_ref[0:1, :] += jnp.sum(dconv * prev, axis=0, keepdims=True)
            ddw_ref[1:2, :] += jnp.sum(dconv * here, axis=0, keepdims=True)
            ddw_ref[2:3, :] += jnp.sum(dconv * nxt, axis=0, keepdims=True)
            ddb_ref[0:1, :] += jnp.sum(dconv, axis=0, keepdims=True)

        _for_chunks(segs, first)

        def second(s, st, ln, off, b):
            rows = pl.ds(b, CHUNK)
            ext = pad_c[pl.ds(off - HALO + b, CHUNK + 2 * HALO), :]
            da = (dw_ref[0:1, :] * _shift(ext, -1) + dw_ref[1:2, :] * ext + dw_ref[2:3, :] * _shift(ext, 1))
            da = da[HALO:HALO + CHUNK]
            dpre_ref[1, rows, :] = (da * pre_ref[2, rows, :]).astype(BF16)
            dpre_ref[2, rows, :] = (da * pre_ref[1, rows, :]).astype(BF16)

        _for_chunks(segs, second)

    quad = pl.BlockSpec((4, t, cb), lambda j: (0, 0, j))
    rows8 = pl.BlockSpec((8, cb), lambda j: (0, j))
    return _pc(
        body, name=name, grid=(width // cb,),
        in_specs=[pl.BlockSpec((t, cb), lambda j: (0, j)), quad, rows8, pl.BlockSpec((1, cb), lambda j: (0, j))],
        out_specs=[quad, rows8, rows8],
        out_shape=[jax.ShapeDtypeStruct((4, t, width), BF16), jax.ShapeDtypeStruct((8, width), F32),
                   jax.ShapeDtypeStruct((8, width), F32)],
        scratch_shapes=[pltpu.VMEM((t + 2 * HALO, cb), F32), pltpu.VMEM((t + 2 * HALO, cb), F32)],
        compiler_params=_params("parallel"),
    )(dz, pre, dw, db)


PAIR_TILES = 2 * WIN_ROWS - 2


def _pair_geometry():
    lane = lax.broadcasted_iota(jnp.int32, (GRID_W, LANES), 1)
    qcol = lax.broadcasted_iota(jnp.int32, (GRID_W, LANES), 0)
    low = lane < GRID_W
    kcol = jnp.where(low, lane, lane - GRID_W)
    start = jnp.clip(qcol - WIN_COLS // 2, 0, GRID_W - WIN_COLS)
    inside = (kcol >= start) & (kcol < start + WIN_COLS)
    return low, inside


def _bias_tiles(rpb_ref, rows_ref, tiles_ref, inside):
    for h in range(2):
        rows = rpb_ref[h]
        rows_ref[h] = (pltpu.roll(rows, LANES - (WIN_COLS - 1), 1)
                       + pltpu.roll(pltpu.roll(rows, GRID_W - (WIN_COLS - 1), 1), 2 * WIN_ROWS - 1, 0))
        for t in range(PAIR_TILES):
            both = jnp.broadcast_to(rows_ref[h, t:t + 1, :], (GRID_W, LANES))
            tiles_ref[h, t] = jnp.where(inside, pltpu.roll(both, 0, 1, stride=1, stride_axis=0), MASKED)


def _bias_tiles_grad(dtiles_ref, drpb_ref):
    n = PAIR_TILES * GRID_W
    qcol = lax.broadcasted_iota(jnp.int32, (n, LANES), 0) & (GRID_W - 1)
    lane = lax.broadcasted_iota(jnp.int32, (1, LANES), 1)
    zero = jnp.zeros((1, LANES), F32)
    for h in range(2):
        v = pltpu.roll(dtiles_ref[h].reshape(n, LANES), WIN_COLS - 1, 1)
        for bit in range(6):
            v = jnp.where((qcol >> bit) & 1 == 1, pltpu.roll(v, LANES - (1 << bit), 1), v)
        sums = [jnp.sum(v[t * GRID_W:(t + 1) * GRID_W], axis=0, keepdims=True) for t in range(PAIR_TILES)]
        for r in range(2 * WIN_ROWS):
            here = sums[r] if r < PAIR_TILES else zero
            prev = pltpu.roll(sums[r - 1], GRID_W, 1) if 1 <= r <= PAIR_TILES else zero
            drpb_ref[h, r:r + 1, :] = jnp.where(lane < 2 * WIN_COLS - 1, here + prev, 0.0)


def _attn_rows(r, n_rows):
    first = jnp.clip(r - WIN_ROWS // 2, 0, n_rows - WIN_ROWS)
    return first, first - r + WIN_ROWS - 1


def _softmax(s_loc, s_ctx):
    m = jnp.maximum(jnp.max(s_loc, axis=-1, keepdims=True), jnp.max(s_ctx, axis=-1, keepdims=True))
    e_loc, e_ctx = jnp.exp(s_loc - m), jnp.exp(s_ctx - m)
    inv = 1.0 / (jnp.sum(e_loc, axis=-1, keepdims=True) + jnp.sum(e_ctx, axis=-1, keepdims=True))
    return e_loc * inv, e_ctx * inv


def _pair_bias(tiles_ref, h, j):
    return jnp.concatenate([tiles_ref[h, j + 2 * m] for m in range(WIN_ROWS // 2)], axis=1)


ROWS_PER_STEP = 2


def _attn_items(step, n_rows, q_ref, low):
    items = []
    for u in range(ROWS_PER_STEP):
        r = step * ROWS_PER_STEP + u
        first, j = _attn_rows(r, n_rows)
        rows = pl.ds(pl.multiple_of(r * GRID_W, GRID_W), GRID_W)
        keys = pl.ds(pl.multiple_of(first * GRID_W, GRID_W), WIN_ROWS * GRID_W)
        q = (q_ref[rows, :].astype(F32) * HEAD_DIM ** -0.5).astype(BF16)
        zero = jnp.zeros_like(q)
        items.append((rows, keys, j, 0, jnp.where(low, q, zero)))
        items.append((rows, keys, j, 1, jnp.where(low, zero, q)))
    return items


def _attn_fwd(qkv, gate, rpb, seq):
    _, t, width = qkv.shape
    n_rows = seq // GRID_W
    n_ctx = t - seq
    blk = WIN_ROWS * GRID_W

    def body(q_ref, k_ref, v_ref, g_ref, rpb_ref, z_ref, o_ref, rows_ref, tiles_ref):
        low, inside = _pair_geometry()
        _bias_tiles(rpb_ref, rows_ref, tiles_ref, inside)
        ctx = pl.ds(seq, n_ctx)

        def step(i, carry):
            items = _attn_items(i, n_rows, q_ref, low)
            k_ctx, v_ctx = k_ref[ctx, :], v_ref[ctx, :]
            scores = [(_dot(q, k_ref[keys, :], 1, 1) + _pair_bias(tiles_ref, h, j), _dot(q, k_ctx, 1, 1))
                      for _, keys, j, h, q in items]
            probs = [_softmax(s_loc, s_ctx) for s_loc, s_ctx in scores]
            outs = [_dot(p_loc.astype(BF16), v_ref[keys, :]) + _dot(p_ctx.astype(BF16), v_ctx)
                    for (_, keys, _, _, _), (p_loc, p_ctx) in zip(items, probs)]
            for u in range(ROWS_PER_STEP):
                rows = items[2 * u][0]
                o = jnp.where(low, outs[2 * u], outs[2 * u + 1])
                o_ref[rows, :] = o
                z_ref[rows, :] = (o * _silu(g_ref[rows, :])).astype(BF16)
            return carry

        lax.fori_loop(0, n_rows // ROWS_PER_STEP, step, 0)

    def part(p):
        return pl.BlockSpec((None, t, LANES), lambda h: (p, 0, h))

    out = pl.BlockSpec((seq, LANES), lambda h: (0, h))
    return _pc(
        body, name="attn_fwd", grid=(width // LANES,),
        in_specs=[part(0), part(1), part(2), part(0), pl.BlockSpec((2, 2 * WIN_ROWS, LANES), lambda h: (h, 0, 0))],
        out_specs=[out, out],
        out_shape=[jax.ShapeDtypeStruct((seq, width), BF16), jax.ShapeDtypeStruct((seq, width), F32)],
        scratch_shapes=[pltpu.VMEM((2, 2 * WIN_ROWS, LANES), F32), pltpu.VMEM((2, PAIR_TILES, GRID_W, LANES), F32)],
        compiler_params=_params("parallel"),
    )(qkv, qkv, qkv, gate, rpb)


def _attn_bwd(qkv, gate, o, dz, rpb, seq):
    _, t, width = qkv.shape
    n_rows = seq // GRID_W
    n_ctx = t - seq
    blk = WIN_ROWS * GRID_W
    heads = 2 * width // LANES

    def body(q_ref, k_ref, v_ref, g_ref, o_ref, dz_ref, rpb_ref, dpre_ref, drpb_ref,
             rows_ref, tiles_ref, dtiles_ref, dk_ref, dv_ref):
        low, inside = _pair_geometry()
        _bias_tiles(rpb_ref, rows_ref, tiles_ref, inside)
        dtiles_ref[...] = jnp.zeros_like(dtiles_ref)
        dk_ref[...] = jnp.zeros_like(dk_ref)
        dv_ref[...] = jnp.zeros_like(dv_ref)
        ctx = pl.ds(seq, n_ctx)

        def step(i, carry):
            items = _attn_items(i, n_rows, q_ref, low)
            k_ctx, v_ctx = k_ref[ctx, :], v_ref[ctx, :]
            d_outs = []
            for u in range(ROWS_PER_STEP):
                rows = items[2 * u][0]
                g = g_ref[rows, :]
                dzv = dz_ref[rows, :]
                dpre_ref[3, rows, :] = (dzv * o_ref[rows, :] * _dsilu(g)).astype(BF16)
                d_o = (dzv * _silu(g)).astype(BF16)
                zero = jnp.zeros_like(d_o)
                d_outs += [jnp.where(low, d_o, zero), jnp.where(low, zero, d_o)]
            scores = [(_dot(q, k_ref[keys, :], 1, 1) + _pair_bias(tiles_ref, h, j), _dot(q, k_ctx, 1, 1))
                      for _, keys, j, h, q in items]
            dprobs = [(_dot(doh, v_ref[keys, :], 1, 1), _dot(doh, v_ctx, 1, 1))
                      for (_, keys, _, _, _), doh in zip(items, d_outs)]
            probs = [_softmax(s_loc, s_ctx) for s_loc, s_ctx in scores]
            dscores = []
            for (p_loc, p_ctx), (dp_loc, dp_ctx) in zip(probs, dprobs):
                delta = (jnp.sum(p_loc * dp_loc, axis=-1, keepdims=True)
                         + jnp.sum(p_ctx * dp_ctx, axis=-1, keepdims=True))
                dscores.append((p_loc * (dp_loc - delta), p_ctx * (dp_ctx - delta)))
            dqs = [_dot(ds_loc.astype(BF16), k_ref[keys, :]) + _dot(ds_ctx.astype(BF16), k_ctx)
                   for (_, keys, _, _, _), (ds_loc, ds_ctx) in zip(items, dscores)]
            for u in range(ROWS_PER_STEP):
                rows = items[2 * u][0]
                dpre_ref[0, rows, :] = (jnp.where(low, dqs[2 * u], dqs[2 * u + 1]) * HEAD_DIM ** -0.5).astype(BF16)
            for (_, keys, j, h, q), doh, (p_loc, p_ctx), (ds_loc, ds_ctx) in zip(items, d_outs, probs, dscores):
                dk_ref[keys, :] += _dot(ds_loc.astype(BF16), q, 0, 0)
                dk_ref[ctx, :] += _dot(ds_ctx.astype(BF16), q, 0, 0)
                dv_ref[keys, :] += _dot(p_loc.astype(BF16), doh, 0, 0)
                dv_ref[ctx, :] += _dot(p_ctx.astype(BF16), doh, 0, 0)
                for m in range(WIN_ROWS // 2):
                    dtiles_ref[h, j + 2 * m] += ds_loc[:, m * LANES:(m + 1) * LANES]
            return carry

        lax.fori_loop(0, n_rows // ROWS_PER_STEP, step, 0)
        dpre_ref[1] = dk_ref[...].astype(BF16)
        dpre_ref[2] = dv_ref[...].astype(BF16)
        dpre_ref[0, ctx, :] = jnp.zeros((n_ctx, LANES), BF16)
        dpre_ref[3, ctx, :] = jnp.zeros((n_ctx, LANES), BF16)
        _bias_tiles_grad(dtiles_ref, drpb_ref)

    def part(p):
        return pl.BlockSpec((None, t, LANES), lambda h: (p, 0, h))

    lat = pl.BlockSpec((seq, LANES), lambda h: (0, h))
    rspec = pl.BlockSpec((2, 2 * WIN_ROWS, LANES), lambda h: (h, 0, 0))
    tiles = pltpu.VMEM((2, PAIR_TILES, GRID_W, LANES), F32)
    return _pc(
        body, name="attn_bwd", grid=(width // LANES,),
        in_specs=[part(0), part(1), part(2), part(0), lat, lat, rspec],
        out_specs=[pl.BlockSpec((4, t, LANES), lambda h: (0, 0, h)), rspec],
        out_shape=[jax.ShapeDtypeStruct((4, t, width), BF16), jax.ShapeDtypeStruct((heads, 2 * WIN_ROWS, LANES), F32)],
        scratch_shapes=[pltpu.VMEM((2, 2 * WIN_ROWS, LANES), F32), tiles, tiles,
                        pltpu.VMEM((t, LANES), F32), pltpu.VMEM((t, LANES), F32)],
        compiler_params=_params("parallel"),
    )(qkv, qkv, qkv, gate, o, dz, rpb)


def _adamw(w, m, v, parts, name, mult=None):
    rows, cols = w.shape
    tr = _tile(rows, max(8, 131072 // cols), 8)
    n_parts = len(parts)
    c1 = 1.0 - ADAM_B1 ** ADAM_STEP
    c2 = 1.0 - ADAM_B2 ** ADAM_STEP

    def body(*refs):
        w_ref, m_ref, v_ref = refs[:3]
        part_refs = refs[3:3 + n_parts]
        rest = refs[3 + n_parts:]
        g = part_refs[0][...].astype(F32)
        for p in part_refs[1:]:
            g = g + p[...].astype(F32)
        if mult is not None:
            g = g * rest[0][...]
            rest = rest[1:]
        g_ref, d_ref, nm_ref, nv_ref = rest
        m2 = ADAM_B1 * m_ref[...] + (1.0 - ADAM_B1) * g
        v2 = ADAM_B2 * v_ref[...] + (1.0 - ADAM_B2) * (g * g)
        m_hat = m2 / c1
        v_hat = v2 / c2
        g_ref[...] = g
        d_ref[...] = -ADAM_LR * (m_hat / (jnp.sqrt(v_hat) + ADAM_EPS) + ADAM_WD * w_ref[...])
        nm_ref[...] = m2
        nv_ref[...] = v2

    tile = pl.BlockSpec((tr, cols), lambda i: (i, 0))
    in_specs, args = [tile, tile, tile], [w, m, v]
    for p in parts:
        if isinstance(p, tuple):
            arr, k = p
            in_specs.append(pl.BlockSpec((None, tr, cols), lambda i, k=k: (k, i, 0)))
            args.append(arr)
        else:
            in_specs.append(tile)
            args.append(p)
    if mult is not None:
        in_specs.append(tile)
        args.append(mult)
    shape = jax.ShapeDtypeStruct((rows, cols), F32)
    return _pc(
        body, name=name, grid=(rows // tr,), in_specs=in_specs, out_specs=[tile] * 4, out_shape=[shape] * 4,
        compiler_params=_params("parallel"),
    )(*args)


def _rows128(a):
    flat = a.reshape(-1)
    pad = (-flat.shape[0]) % LANES
    if pad:
        flat = jnp.concatenate([flat, jnp.zeros((pad,), flat.dtype)])
    return flat.reshape(-1, LANES)


def _pad_rows(a, mult=8):
    pad = (-a.shape[0]) % mult
    if pad:
        a = jnp.concatenate([a, jnp.zeros((pad,) + a.shape[1:], a.dtype)], axis=0)
    return a


def kernel(x, c, ctx, c_ctx, norm_g, ada_w, ada_b, pool_w_in, pool_w_grp, pool_scale, pool_w_out, na_w_in, na_rpb, na_w_out, conv_w_in, conv_dw, conv_db, conv_w_out, final_g, loss_target, m_c_ctx, m_norm_g, m_ada_w, m_ada_b, m_pool_w_in, m_pool_w_grp, m_pool_scale, m_pool_w_out, m_na_w_in, m_na_rpb, m_na_w_out, m_conv_w_in, m_conv_dw, m_conv_db, m_conv_w_out, m_final_g, v_c_ctx, v_norm_g, v_ada_w, v_ada_b, v_pool_w_in, v_pool_w_grp, v_pool_scale, v_pool_w_out, v_na_w_in, v_na_rpb, v_na_w_out, v_conv_w_in, v_conv_dw, v_conv_db, v_conv_w_out, v_final_g):
    xi, yi, ci = _my_place()
    me = 4 * xi + 2 * yi + ci
    seq, d = x.shape[1], x.shape[2]
    n_ctx = ctx.shape[1]
    t_all = seq + n_ctx
    width = d
    depth = norm_g.shape[0]
    nb = ada_w.shape[2]
    shard = width // N_DEV
    d_rows = d // LANES
    tr = math.gcd(math.gcd(seq, n_ctx), 256)
    x_tiles = seq // tr

    n_pool = pool_scale.shape[0]
    n_grp = pool_w_grp.shape[1]
    grp = width // n_grp
    layer_weights = [[pool_w_in[0], pool_w_grp[0], pool_w_out[0]], [na_w_in[0], na_w_out[0]],
                     [conv_w_in[0], conv_w_out[0]], [pool_w_in[1], pool_w_grp[1], pool_w_out[1]]]
    in_flight, token = [], jnp.zeros((8, LANES), F32)
    for i, ws in enumerate(layer_weights):
        state = _exchange_start([w.astype(BF16) for w in ws], False, token, f"weights_start{i}")
        token = state[-1]
        in_flight.append(state)

    def pool_weights(i, after):
        w_in, w_grp, w_out = _exchange_wait(in_flight[i], False, after, f"weights_wait{i}")
        return w_in[:, None], w_grp.transpose(1, 0, 2, 3).reshape(n_grp, grp, grp), w_out.reshape(width, d)

    def pair_weights(i, after):
        w_in, w_out = _exchange_wait(in_flight[i], False, after, f"weights_wait{i}")
        return w_in[:, None], w_out.reshape(width, d)

    small_in =_pad_rows(jnp.concatenate([_rows128(c), pool_scale, conv_dw[0], conv_db], axis=0))
    got = _gather_small(small_in, "gather_inputs")
    r0 = d_rows
    c_all = got[:, :r0].reshape(N_DEV, d)
    n_pool = pool_scale.shape[0]
    scale_full = got[:, r0:r0 + n_pool].transpose(1, 0, 2).reshape(n_pool, width)
    r1 = r0 + n_pool
    taps_full = _pad_rows(got[:, r1:r1 + 3].transpose(1, 0, 2).reshape(3, width))
    bias_full = got[:, r1 + 3:r1 + 4].transpose(1, 0, 2).reshape(1, width)

    cond = jnp.concatenate([c_all, c_ctx[None], jnp.zeros((7, d), F32)], axis=0)
    bias_mine = lax.dynamic_slice(ada_b, (0, me * nb), (depth, nb))
    mod_mine = _mod_fwd(cond, ada_w, bias_mine)
    mod_all = _gather_small(mod_mine.reshape(-1, LANES), "gather_mod")
    mod_all = mod_all.reshape(N_DEV, depth, 16, nb).transpose(1, 2, 0, 3).reshape(depth, 16, 3 * d)
    mod_x = lax.dynamic_index_in_dim(mod_all, me, 1, keepdims=False).reshape(depth, 3, d)
    mod_c = mod_all[:, 8].reshape(depth, 3, d)
    pad5 = jnp.zeros((depth, 5, d), F32)
    mod_x = jnp.concatenate([mod_x, pad5], axis=1)
    mod_c = jnp.concatenate([mod_c, pad5], axis=1)
    mods = [jnp.stack([mod_x[i], mod_c[i]]) if i < 2 else mod_x[i][None] for i in range(depth)]

    both = [(0, seq), (seq, n_ctx)]
    latent = [(0, seq)]

    def grp_slots(g):
        return g.reshape(n_grp, N_DEV, grp // N_DEV, grp).transpose(1, 0, 2, 3).reshape(N_DEV, -1, grp).astype(BF16)

    def send_grads(i, grads):
        return _exchange_start(grads, True, jnp.zeros((8, LANES), F32), f"grads_start{i}")

    xs0 = jnp.concatenate([x[0], ctx[0]], axis=0)
    h0 = _norm_fwd(xs0, norm_g[0:1] + token[0, 0], mods[0], tr, x_tiles, "norm_fwd0")
    pool_in_w0, pool_grp_w0, pool_out_w0 = pool_weights(0, h0)
    pre0 = _proj_in(h0, pool_in_w0, 0, width, "proj_in0")
    z0, diff0 = _pool_fwd(pre0, pool_grp_w0, scale_full[0:1], both, "pool_fwd0")
    yx0, xs1 = _proj_out(z0, pool_out_w0, xs0, mods[0], tr, x_tiles, "proj_out0")

    h1 = _norm_fwd(xs1, norm_g[1:2], mods[1], tr, x_tiles, "norm_fwd1")
    na_in_w, na_out_w = pair_weights(1, h1)
    per_part = width // na_w_in.shape[2]
    qkv1 = _proj_in(h1, na_in_w, 0, width, "proj_in1_qkv", blocks=(0, 3 * per_part), dtype=BF16)
    gpre1 = _proj_in(h1, na_in_w, 0, width, "proj_in1_gate", blocks=(3 * per_part, per_part))
    rpb_rows = jnp.pad(na_rpb[0], ((0, 0), (0, 2 * WIN_ROWS - na_rpb.shape[2]), (0, LANES - na_rpb.shape[3])))
    z1, o1 = _attn_fwd(qkv1, gpre1, rpb_rows, seq)
    yx1, x2 = _proj_out(z1, na_out_w, xs1, mods[1], tr, x_tiles, "proj_out1")

    h2 = _norm_fwd(x2, norm_g[2:3], mods[2], tr, x_tiles, "norm_fwd2")
    conv_in_w, conv_out_w = pair_weights(2, h2)
    pre2 = _proj_in(h2, conv_in_w, 0, width, "proj_in2")
    z2 = _conv_fwd(pre2, taps_full, bias_full, "conv_fwd")
    yx2, x3 = _proj_out(z2, conv_out_w, x2, mods[2], tr, x_tiles, "proj_out2")

    h3 = _norm_fwd(x3, norm_g[3:4], mods[3], tr, x_tiles, "norm_fwd3")
    pool_in_w3, pool_grp_w3, pool_out_w3 = pool_weights(3, h3)
    pre3 = _proj_in(h3, pool_in_w3, 0, width, "proj_in3")
    z3, diff3 = _pool_fwd(pre3, pool_grp_w3, scale_full[1:2], latent, "pool_fwd3")
    yx3, x4 = _proj_out(z3, pool_out_w3, x3, mods[3], tr, x_tiles, "proj_out3")

    loss_part, dx4, d_final = _loss_head(x4, loss_target[0], final_g[None], tr)
    loss = lax.psum(loss_part[0, 0], ("x", "y", "c"))

    dyx3, gate3 = _resid_bwd(dx4, yx3, mods[3], tr, x_tiles, "resid_bwd3")
    dz3 = _proj_out_dz(dyx3, pool_out_w3, "proj_out_dz3")
    g_pool_out1 = _grad_w_out(z3, dyx3, "grad_w_out3")
    dpre3, g_grp1, g_scale1 = _pool_bwd(dz3, diff3, pre3, pool_grp_w3, scale_full[1:2], latent, "pool_bwd3")
    dh3 = _proj_in_dh(dpre3, pool_in_w3, 0, "proj_in_dh3")
    g_pool_in1 = _grad_w_in(h3, dpre3, pool_w_in.shape[2], "grad_w_in3")
    sent3 = send_grads(3, [g_pool_in1, grp_slots(g_grp1), g_pool_out1.reshape(N_DEV, shard, d)])
    dx3, norm3 = _norm_bwd(x3, dh3, dx4, norm_g[3:4] + sent3[-1][0, 0], mods[3], tr, x_tiles, "norm_bwd3")

    dyx2, gate2 = _resid_bwd(dx3, yx2, mods[2], tr, x_tiles, "resid_bwd2")
    dz2 = _proj_out_dz(dyx2, conv_out_w, "proj_out_dz2")
    g_conv_out = _grad_w_out(z2, dyx2, "grad_w_out2")
    dpre2, g_taps, g_cbias = _conv_bwd(dz2, pre2, taps_full, bias_full, "conv_bwd")
    dh2 = _proj_in_dh(dpre2, conv_in_w, 0, "proj_in_dh2")
    g_conv_in = _grad_w_in(h2, dpre2, conv_w_in.shape[2], "grad_w_in2")
    sent2 = send_grads(2, [g_conv_in, g_conv_out.reshape(N_DEV, shard, d)])
    dx2, norm2 = _norm_bwd(x2, dh2, dx3, norm_g[2:3] + sent2[-1][0, 0], mods[2], tr, x_tiles, "norm_bwd2")

    dyx1, gate1 = _resid_bwd(dx2, yx1, mods[1][:1], tr, x_tiles, "resid_bwd1")
    dz1 = _proj_out_dz(dyx1, na_out_w, "proj_out_dz1")
    g_na_out = _grad_w_out(z1, dyx1, "grad_w_out1")
    dpre1, g_rpb = _attn_bwd(qkv1, gpre1, o1, dz1, rpb_rows, seq)
    g_rpb = g_rpb[:, :na_rpb.shape[2], :na_rpb.shape[3]]
    dh1 = _proj_in_dh(dpre1, na_in_w, 0, "proj_in_dh1")
    g_na_in = _grad_w_in(h1, dpre1, na_w_in.shape[2], "grad_w_in1")
    sent1 = send_grads(1, [g_na_in, g_na_out.reshape(N_DEV, shard, d)])
    dres1 = jnp.concatenate([dx2, jnp.zeros((n_ctx, d), F32)], axis=0)
    dxs1, norm1 = _norm_bwd(xs1, dh1, dres1, norm_g[1:2] + sent1[-1][0, 0], mods[1], tr, x_tiles, "norm_bwd1")

    dyx0, gate0 = _resid_bwd(dxs1, yx0, mods[0], tr, x_tiles, "resid_bwd0")
    dz0 = _proj_out_dz(dyx0, pool_out_w0, "proj_out_dz0")
    g_pool_out0 = _grad_w_out(z0, dyx0, "grad_w_out0")
    dpre0, g_grp0, g_scale0 = _pool_bwd(dz0, diff0, pre0, pool_grp_w0, scale_full[0:1], both, "pool_bwd0")
    dh0 = _proj_in_dh(dpre0, pool_in_w0, 0, "proj_in_dh0")
    g_pool_in0 = _grad_w_in(h0, dpre0, pool_w_in.shape[2], "grad_w_in0")
    sent0 = send_grads(0, [g_pool_in0, grp_slots(g_grp0), g_pool_out0.reshape(N_DEV, shard, d)])
    dxs0, norm0 = _norm_bwd(xs0, dh0, dxs1, norm_g[0:1] + sent0[-1][0, 0], mods[0], tr, x_tiles, "norm_bwd0")
    grad_x = dxs0[:seq][None]

    norms, gates = [norm0, norm1, norm2, norm3], [gate0, gate1, gate2, gate3]
    zero_d = jnp.zeros((d,), F32)
    dm_rows = [jnp.concatenate([norms[i][0, 0], norms[i][0, 1], gates[i][0, 0]]) for i in range(depth)]
    dm_rows.append(jnp.concatenate([norm0[1, 0], norm0[1, 1], gate0[1, 0]]))
    dm_rows.append(jnp.concatenate([norm1[1, 0], norm1[1, 1], zero_d]))
    dm_local = jnp.stack(dm_rows + [jnp.zeros((3 * d,), F32)] * 2)
    g_norm_part = jnp.stack([norm0[0, 2] + norm0[1, 2], norm1[0, 2] + norm1[1, 2], norm2[0, 2], norm3[0, 2]])
    g_scale_part = jnp.concatenate([g_scale0, g_scale1], axis=0)
    pieces = [_rows128(dm_local), _rows128(g_norm_part), _rows128(d_final[0]), _pad_rows(_rows128(g_rpb)),
              _rows128(g_scale_part), _rows128(g_taps[:3]), _rows128(g_cbias[0])]
    small_out = _gather_small(jnp.concatenate(pieces, axis=0), "gather_small_grads")
    marks = np.cumsum([0] + [p.shape[0] for p in pieces])

    def piece(k):
        return small_out[:, marks[k]:marks[k + 1]]

    dm_all = piece(0).reshape(N_DEV, 8, 3 * d).transpose(1, 0, 2)
    dm_mine = lax.dynamic_slice(dm_all, (0, 0, me * nb), (8, N_DEV, nb))
    g_ada_w, g_ada_b, cctx_part, dsilu_cond = _mod_bwd(cond, ada_w, dm_all, dm_mine)
    cctx_all = _gather_small(_rows128(cctx_part[0]), "gather_cctx")

    def my_shard(a, n):
        a = a.reshape(N_DEV, n, N_DEV, shard)
        return lax.dynamic_index_in_dim(a, me, 2, keepdims=False)

    zeros7 = lambda r: jnp.zeros((N_DEV - 1, r, LANES), F32)
    ada_b_rows = _rows128(g_ada_b)
    small_parts = jnp.concatenate([
        cctx_all, piece(1), jnp.concatenate([ada_b_rows[None], zeros7(ada_b_rows.shape[0])], axis=0), piece(2), piece(3),
        my_shard(piece(4), n_pool), my_shard(piece(5), 3), my_shard(piece(6), 1)], axis=1)
    n_small = small_parts.shape[1]
    small_parts = jnp.concatenate([small_parts, jnp.zeros((N_DEV, (-n_small) % 8, LANES), F32)], axis=1)

    def pack(c_ctx_, norm_g_, ada_b_, final_g_, rpb_, scale_, taps_, cbias_):
        rows = [_rows128(c_ctx_), _rows128(norm_g_), _rows128(ada_b_), _rows128(final_g_), _pad_rows(_rows128(rpb_)),
                scale_, taps_[0], cbias_]
        return _pad_rows(jnp.concatenate(rows, axis=0)), np.cumsum([0] + [r.shape[0] for r in rows])

    w_small, smarks = pack(c_ctx, norm_g, ada_b, final_g, na_rpb, pool_scale, conv_dw, conv_db)
    m_small, _ = pack(m_c_ctx, m_norm_g, m_ada_b, m_final_g, m_na_rpb, m_pool_scale, m_conv_dw, m_conv_db)
    v_small, _ = pack(v_c_ctx, v_norm_g, v_ada_b, v_final_g, v_na_rpb, v_pool_scale, v_conv_dw, v_conv_db)
    mult = jnp.concatenate([_rows128(dsilu_cond[8]), jnp.ones((w_small.shape[0] - d_rows, LANES), F32)], axis=0)
    small_res = _adamw(w_small, m_small, v_small, [(small_parts, k) for k in range(N_DEV)], "adamw_small", mult=mult)

    def unpack(k, like):
        out = []
        for r in small_res:
            flat = r[smarks[k]:smarks[k + 1]].reshape(-1)
            out.append(flat[:like.size].reshape(like.shape))
        return out

    res = {"c_ctx": unpack(0, c_ctx), "norm_g": unpack(1, norm_g), "ada_b": unpack(2, ada_b),
           "final_g": unpack(3, final_g), "na_rpb": unpack(4, na_rpb), "pool_scale": unpack(5, pool_scale),
           "conv_dw": unpack(6, conv_dw), "conv_db": unpack(7, conv_db)}

    res["ada_w"] = [r.reshape(ada_w.shape) for r in _adamw(
        ada_w.reshape(-1, nb), m_ada_w.reshape(-1, nb), v_ada_w.reshape(-1, nb), [g_ada_w.reshape(-1, nb)], "adamw_ada_w")]

    def landed(state, i):
        return _exchange_wait(state, True, grad_x, f"grads_wait{i}")

    in3, grp3, out3 = landed(sent3, 3)
    in2, out2 = landed(sent2, 2)
    in1, out1 = landed(sent1, 1)
    in0, grp0, out0 = landed(sent0, 0)

    def big(parts, w, m, v, name):
        shape = w.shape
        view = (-1, shape[-1])
        parts = [(parts.reshape((N_DEV,) + w.reshape(view).shape), k) for k in range(N_DEV)]
        return [r.reshape(shape) for r in _adamw(w.reshape(view), m.reshape(view), v.reshape(view), parts, name)]

    def two_layers(first, second, w, m, v, name):
        a = big(first, w[0], m[0], v[0], name + "0")
        b = big(second, w[1], m[1], v[1], name + "1")
        return [jnp.stack([p, q]) for p, q in zip(a, b)]

    res["pool_w_in"] = two_layers(in0, in3, pool_w_in, m_pool_w_in, v_pool_w_in, "adamw_pool_in")
    res["pool_w_grp"] = two_layers(grp0, grp3, pool_w_grp, m_pool_w_grp, v_pool_w_grp, "adamw_pool_grp")
    res["pool_w_out"] = two_layers(out0, out3, pool_w_out, m_pool_w_out, v_pool_w_out, "adamw_pool_out")
    res["na_w_in"] = [r[None] for r in big(in1, na_w_in[0], m_na_w_in[0], v_na_w_in[0], "adamw_na_in")]
    res["na_w_out"] = [r[None] for r in big(out1, na_w_out[0], m_na_w_out[0], v_na_w_out[0], "adamw_na_out")]
    res["conv_w_in"] = [r[None] for r in big(in2, conv_w_in[0], m_conv_w_in[0], v_conv_w_in[0], "adamw_conv_in")]
    res["conv_w_out"] = [r[None] for r in big(out2, conv_w_out[0], m_conv_w_out[0], v_conv_w_out[0], "adamw_conv_out")]

    order = ["c_ctx", "norm_g", "ada_w", "ada_b", "pool_w_in", "pool_w_grp", "pool_scale", "pool_w_out", "na_w_in",
             "na_rpb", "na_w_out", "conv_w_in", "conv_dw", "conv_db", "conv_w_out", "final_g"]
    outs = [loss, grad_x]
    for j in range(4):
        outs += [res[n][j] for n in order]
    return tuple(outs)
```

```python
import functools
import math

import numpy as np
import jax
import jax.numpy as jnp
from jax import lax
from jax.experimental import pallas as pl
from jax.experimental.pallas import tpu as pltpu

F32 = jnp.float32
BF16 = jnp.bfloat16
N_DEV = 8
LANES = 128
RMS_EPS = 1e-6
GRID_W = 64
WIN_ROWS = 8
WIN_COLS = 16
HEAD_DIM = 64
POOL_WINDOWS = (2, 4, 8, 16)
HALO = 8
CHUNK = 128
MASKED = -1e30
ADAM_LR = 0.001
ADAM_B1 = 0.9
ADAM_B2 = 0.999
ADAM_EPS = 1e-08
ADAM_WD = 0.01
ADAM_STEP = 10
VMEM_LIMIT = 56 * 1024 * 1024
MESH = pl.DeviceIdType.MESH
ANY = pl.BlockSpec(memory_space=pl.ANY)
HBM = pl.BlockSpec(memory_space=pltpu.HBM)
SEM = pl.BlockSpec(memory_space=pltpu.SEMAPHORE)
EFFECT = pltpu.SideEffectType.DATAFLOW_SIDE_EFFECTING


def _pc(body, *, name, **kw):
    return pl.pallas_call(body, name=name, **kw)


def _params(*sem):
    return pltpu.CompilerParams(dimension_semantics=sem if sem else None, vmem_limit_bytes=VMEM_LIMIT)


def _dot(a, b, ca=1, cb=0, precision=None):
    return lax.dot_general(a, b, (((ca,), (cb,)), ((), ())), preferred_element_type=F32, precision=precision)


def _tile(n, pref, unit=LANES):
    best = None
    for t in range(unit, min(n, pref) + 1, unit):
        if n % t == 0:
            best = t
    return best if best is not None else n


def _sigmoid(x):
    return 1.0 / (1.0 + jnp.exp(-x))


def _silu(x):
    return x * _sigmoid(x)


def _dsilu(x):
    s = _sigmoid(x)
    return s * (1.0 + x * (1.0 - s))


def _my_place():
    return lax.axis_index("x"), lax.axis_index("y"), lax.axis_index("c")


def _flip(v, f):
    return 1 - v if f else v


def _gather_small(block, name):
    rows, cols = block.shape

    def body(x_ref, out_ref, send_sems, recv_sems):
        x, y, c = _my_place()
        me = 4 * x + 2 * y + c
        out_ref[me] = x_ref[...]
        copies = []
        for k in range(1, N_DEV):
            peer = (_flip(x, k & 4), _flip(y, k & 2), _flip(c, k & 1))
            cp = pltpu.make_async_remote_copy(
                src_ref=x_ref, dst_ref=out_ref.at[me], send_sem=send_sems.at[k - 1], recv_sem=recv_sems.at[k - 1],
                device_id=peer, device_id_type=MESH)
            cp.start()
            copies.append(cp)
        for cp in copies:
            cp.wait()

    return _pc(
        body, name=name,
        out_shape=jax.ShapeDtypeStruct((N_DEV, rows, cols), block.dtype),
        in_specs=[pl.BlockSpec(memory_space=pltpu.VMEM)],
        out_specs=pl.BlockSpec(memory_space=pltpu.VMEM),
        scratch_shapes=[pltpu.SemaphoreType.DMA((N_DEV - 1,)), pltpu.SemaphoreType.DMA((N_DEV - 1,))],
    )(block)


def _exchange_copies(srcs, lands, send_sems, recv_sems, per_dest):
    x, y, c = _my_place()
    me = 4 * x + 2 * y + c
    copies = []
    for t, (src, land) in enumerate(zip(srcs, lands)):
        for k in range(1, N_DEV):
            peer = (_flip(x, k & 4), _flip(y, k & 2), _flip(c, k & 1))
            dest = 4 * peer[0] + 2 * peer[1] + peer[2]
            s = t * (N_DEV - 1) + k - 1
            copies.append(pltpu.make_async_remote_copy(
                src_ref=src.at[dest] if per_dest else src, dst_ref=land.at[me],
                send_sem=send_sems[s], recv_sem=recv_sems[s], device_id=peer, device_id_type=MESH))
    return copies


def _exchange_start(srcs, per_dest, after, name):
    nt = len(srcs)
    ns = nt * (N_DEV - 1)
    lands = [lax.empty((N_DEV,) + (s.shape[1:] if per_dest else s.shape), s.dtype) for s in srcs]

    def body(*refs):
        ins, outs = refs[:2 * nt + 1], refs[2 * nt + 1:]
        for cp in _exchange_copies(ins[:nt], ins[nt:2 * nt], outs[:ns], outs[ns:2 * ns], per_dest):
            cp.start()
        outs[-1][...] = jnp.zeros_like(outs[-1])

    hbm = [pltpu.with_memory_space_constraint(a, pltpu.HBM) for a in list(srcs) + lands]
    res = _pc(
        body, name=name,
        out_shape=(*[pltpu.SemaphoreType.DMA(())] * (2 * ns), *[pltpu.HBM(a.shape, a.dtype) for a in hbm],
                   jax.ShapeDtypeStruct((8, LANES), F32)),
        in_specs=[HBM] * (2 * nt) + [ANY],
        out_specs=(*[SEM] * (2 * ns), *[HBM] * (2 * nt), pl.BlockSpec(memory_space=pltpu.VMEM)),
        input_output_aliases={i: 2 * ns + i for i in range(2 * nt)},
        compiler_params=pltpu.CompilerParams(has_side_effects=EFFECT),
    )(*hbm, after)
    sems, rest = res[:2 * ns], res[2 * ns:]
    return list(sems[:ns]), list(sems[ns:]), list(rest[:nt]), list(rest[nt:2 * nt]), rest[-1]


def _exchange_wait(state, per_dest, after, name, which=None):
    send_sems, recv_sems, srcs, lands, _ = state
    which = list(range(len(srcs))) if which is None else which
    pick = [t * (N_DEV - 1) + k for t in which for k in range(N_DEV - 1)]
    send_sems, recv_sems = [send_sems[s] for s in pick], [recv_sems[s] for s in pick]
    srcs, lands = [srcs[t] for t in which], [lands[t] for t in which]
    nt = len(srcs)
    ns = len(send_sems)

    def body(*refs):
        sems = refs[2 * nt:2 * nt + 2 * ns]
        for cp in _exchange_copies(refs[:nt], refs[nt:2 * nt], sems[:ns], sems[ns:], per_dest):
            cp.wait_send()
            cp.wait_recv()

    thru = list(srcs) + list(lands)
    res = _pc(
        body, name=name,
        out_shape=tuple(pltpu.HBM(a.shape, a.dtype) for a in thru),
        in_specs=[HBM] * (2 * nt) + [SEM] * (2 * ns) + [ANY],
        out_specs=tuple([HBM] * (2 * nt)),
        input_output_aliases={i: i for i in range(2 * nt)},
        compiler_params=pltpu.CompilerParams(has_side_effects=EFFECT),
    )(*thru, *send_sems, *recv_sems, after)
    me = 4 * lax.axis_index("x") + 2 * lax.axis_index("y") + lax.axis_index("c")
    out = []
    for src, land in zip(res[:nt], res[nt:]):
        own = lax.dynamic_index_in_dim(src, me, 0, keepdims=True) if per_dest else src[None]
        out.append(lax.dynamic_update_slice_in_dim(land, own, me, 0))
    return out


def _mod_fwd(cond, ada_w, bias):
    depth, d, nb = ada_w.shape

    def body(c_ref, w_ref, b_ref, o_ref):
        s = _silu(c_ref[...]).astype(BF16)
        o_ref[...] = _dot(s, w_ref[...].astype(BF16)) + b_ref[...]

    return _pc(
        body, name="mod_fwd", grid=(depth,),
        in_specs=[pl.BlockSpec((16, d), lambda i: (0, 0)), pl.BlockSpec((None, d, nb), lambda i: (i, 0, 0)),
                  pl.BlockSpec((None, 1, nb), lambda i: (i, 0, 0))],
        out_specs=pl.BlockSpec((None, 16, nb), lambda i: (i, 0, 0)),
        out_shape=jax.ShapeDtypeStruct((depth, 16, nb), F32),
        compiler_params=_params("parallel"),
    )(cond, ada_w, bias.reshape(depth, 1, nb))


def _mod_bwd(cond, ada_w, dm_all, dm_mine):
    depth, d, nb = ada_w.shape
    d3 = dm_all.shape[-1]

    def body(c_ref, w_ref, all_ref, call_ref, mine_ref, cmine_ref, gw_ref, gb_ref, part_ref, ds_ref):
        i = pl.program_id(0)
        cond_v = c_ref[...]
        s = _silu(cond_v).astype(BF16)
        has_ctx = jnp.where(i < 2, 1.0, 0.0)
        tot_all = jnp.sum(call_ref[...], axis=0, keepdims=True) * has_ctx
        tot_mine = jnp.broadcast_to(jnp.sum(cmine_ref[...], axis=0, keepdims=True) * has_ctx, (8, nb)).astype(BF16)
        gb_ref[...] = jnp.sum(all_ref[...], axis=0, keepdims=True) + tot_all
        gw_ref[...] = _dot(s[0:8], mine_ref[...].astype(BF16), 0, 0) + _dot(s[8:16], tot_mine, 0, 0)
        part = _dot(tot_mine, w_ref[...].astype(BF16), 1, 1)

        @pl.when(i == 0)
        def _():
            part_ref[...] = jnp.zeros_like(part_ref)
            ds_ref[...] = _dsilu(cond_v)

        part_ref[...] += part

    def rows(width, which):
        return pl.BlockSpec((None, N_DEV, width), which)

    layer = lambda i: (i, 0, 0)
    ctx_layer = lambda i: (jnp.minimum(i, 1) + 4, 0, 0)
    return _pc(
        body, name="mod_bwd", grid=(depth,),
        in_specs=[pl.BlockSpec((16, d), lambda i: (0, 0)), pl.BlockSpec((None, d, nb), layer),
                  rows(d3, layer), rows(d3, ctx_layer), rows(nb, layer), rows(nb, ctx_layer)],
        out_specs=[pl.BlockSpec((None, d, nb), layer), pl.BlockSpec((None, 1, d3), layer),
                   pl.BlockSpec((8, d), lambda i: (0, 0)), pl.BlockSpec((16, d), lambda i: (0, 0))],
        out_shape=[jax.ShapeDtypeStruct((depth, d, nb), F32), jax.ShapeDtypeStruct((depth, 1, d3), F32),
                   jax.ShapeDtypeStruct((8, d), F32), jax.ShapeDtypeStruct((16, d), F32)],
        compiler_params=_params("arbitrary"),
    )(cond, ada_w, dm_all, dm_all, dm_mine, dm_mine)


def _norm_fwd(xs, g, mod, tr, seg_tiles, name):
    t, d = xs.shape

    def body(x_ref, g_ref, mod_ref, h_ref):
        x = x_ref[...]
        r = lax.rsqrt(jnp.mean(x * x, axis=-1, keepdims=True) + RMS_EPS)
        y = (x * r) * g_ref[...]
        h_ref[...] = (y * (1.0 + mod_ref[1:2, :]) + mod_ref[0:1, :]).astype(BF16)

    return _pc(
        body, name=name, grid=(t // tr,),
        in_specs=[pl.BlockSpec((tr, d), lambda i: (i, 0)), pl.BlockSpec((1, d), lambda i: (0, 0)),
                  pl.BlockSpec((None, 8, d), lambda i: (i // seg_tiles, 0, 0))],
        out_specs=pl.BlockSpec((tr, d), lambda i: (i, 0)),
        out_shape=jax.ShapeDtypeStruct((t, d), BF16),
        compiler_params=_params("parallel"),
    )(xs, g, mod)


def _norm_bwd(xs, dh, dres, g, mod, tr, seg_tiles, name):
    t, d = xs.shape
    nseg = mod.shape[0]

    def body(x_ref, dh_ref, dres_ref, g_ref, mod_ref, dx_ref, sum_ref):
        i = pl.program_id(0)
        x = x_ref[...]
        r = lax.rsqrt(jnp.mean(x * x, axis=-1, keepdims=True) + RMS_EPS)
        xn = x * r
        dhv = dh_ref[...]
        gain = g_ref[...]
        one_scale = 1.0 + mod_ref[1:2, :]
        dxn = dhv * (gain * one_scale)
        dx_ref[...] = dres_ref[...] + r * (dxn - xn * jnp.mean(dxn * xn, axis=-1, keepdims=True))

        @pl.when(i % seg_tiles == 0)
        def _():
            sum_ref[...] = jnp.zeros_like(sum_ref)

        sum_ref[0:1, :] += jnp.sum(dhv, axis=0, keepdims=True)
        sum_ref[1:2, :] += jnp.sum(dhv * (xn * gain), axis=0, keepdims=True)
        sum_ref[2:3, :] += jnp.sum(dhv * one_scale * xn, axis=0, keepdims=True)

    row = pl.BlockSpec((tr, d), lambda i: (i, 0))
    seg = pl.BlockSpec((None, 8, d), lambda i: (i // seg_tiles, 0, 0))
    return _pc(
        body, name=name, grid=(t // tr,),
        in_specs=[row, row, row, pl.BlockSpec((1, d), lambda i: (0, 0)), seg],
        out_specs=[row, seg],
        out_shape=[jax.ShapeDtypeStruct((t, d), F32), jax.ShapeDtypeStruct((nseg, 8, d), F32)],
        compiler_params=_params("arbitrary"),
    )(xs, dh, dres, g, mod)


def _resid_bwd(dx, yx, mod, tr, seg_tiles, name):
    t, d = yx.shape
    nseg = mod.shape[0]

    def body(dx_ref, yx_ref, mod_ref, dyx_ref, sum_ref):
        i = pl.program_id(0)
        dxv = dx_ref[...]
        dyx_ref[...] = (dxv * mod_ref[2:3, :]).astype(BF16)

        @pl.when(i % seg_tiles == 0)
        def _():
            sum_ref[...] = jnp.zeros_like(sum_ref)

        sum_ref[0:1, :] += jnp.sum(dxv * yx_ref[...], axis=0, keepdims=True)

    row = pl.BlockSpec((tr, d), lambda i: (i, 0))
    seg = pl.BlockSpec((None, 8, d), lambda i: (i // seg_tiles, 0, 0))
    return _pc(
        body, name=name, grid=(t // tr,),
        in_specs=[row, row, seg], out_specs=[row, seg],
        out_shape=[jax.ShapeDtypeStruct((t, d), BF16), jax.ShapeDtypeStruct((nseg, 8, d), F32)],
        compiler_params=_params("arbitrary"),
    )(dx, yx, mod)


def _loss_head(xs, target, g, tr):
    t, d = xs.shape

    def body(x_ref, t_ref, g_ref, loss_ref, dx_ref, dg_ref):
        i = pl.program_id(0)
        x = x_ref[...]
        r = lax.rsqrt(jnp.mean(x * x, axis=-1, keepdims=True) + RMS_EPS)
        xn = x * r
        gain = g_ref[...]
        err = xn * gain - t_ref[...]
        dy = err * (1.0 / d)
        dxn = dy * gain
        dx_ref[...] = r * (dxn - xn * jnp.mean(dxn * xn, axis=-1, keepdims=True))

        @pl.when(i == 0)
        def _():
            loss_ref[...] = jnp.zeros_like(loss_ref)
            dg_ref[...] = jnp.zeros_like(dg_ref)

        loss_ref[...] += 0.5 * jnp.sum(jnp.mean(err * err, axis=-1, keepdims=True))
        dg_ref[0:1, :] += jnp.sum(dy * xn, axis=0, keepdims=True)

    row = pl.BlockSpec((tr, d), lambda i: (i, 0))
    return _pc(
        body, name="loss_head", grid=(t // tr,),
        in_specs=[row, row, pl.BlockSpec((1, d), lambda i: (0, 0))],
        out_specs=[pl.BlockSpec((8, LANES), lambda i: (0, 0)), row, pl.BlockSpec((8, d), lambda i: (0, 0))],
        out_shape=[jax.ShapeDtypeStruct((8, LANES), F32), jax.ShapeDtypeStruct((t, d), F32),
                   jax.ShapeDtypeStruct((8, d), F32)],
        compiler_params=_params("arbitrary"),
    )(xs, target, g)


def _proj_in(h, w, layer, width, name, blocks=None, dtype=F32):
    t, d = h.shape
    n8 = w.shape[-1]
    first, count = blocks if blocks is not None else (0, N_DEV)
    per_part = width // n8
    tm = _tile(t, 1152)

    def body(a_ref, b_ref, o_ref):
        o_ref[...] = _dot(a_ref[...], b_ref[...]).astype(dtype)

    return _pc(
        body, name=name, grid=(t // tm, count),
        in_specs=[pl.BlockSpec((tm, d), lambda i, j: (i, 0)),
                  pl.BlockSpec((None, None, d, n8), lambda i, j: (first + j, layer, 0, 0))],
        out_specs=pl.BlockSpec((None, tm, n8), lambda i, j: (j // per_part, i, j % per_part)),
        out_shape=jax.ShapeDtypeStruct((count // per_part, t, width), dtype),
        compiler_params=_params("parallel", "parallel"),
    )(h, w)


def _proj_out(z, w, res, mod, tm, seg_tiles, name):
    t, k = z.shape
    d = w.shape[1]

    def body(z_ref, w_ref, res_ref, mod_ref, yx_ref, x_ref):
        yx = _dot(z_ref[...], w_ref[...])
        yx_ref[...] = yx
        x_ref[...] = res_ref[...] + mod_ref[2:3, :] * yx

    tile = pl.BlockSpec((tm, d), lambda i: (i, 0))
    return _pc(
        body, name=name, grid=(t // tm,),
        in_specs=[pl.BlockSpec((tm, k), lambda i: (i, 0)), pl.BlockSpec((k, d), lambda i: (0, 0)), tile,
                  pl.BlockSpec((None, 8, d), lambda i: (i // seg_tiles, 0, 0))],
        out_specs=[tile, tile],
        out_shape=[jax.ShapeDtypeStruct((t, d), F32), jax.ShapeDtypeStruct((t, d), F32)],
        compiler_params=_params("parallel"),
    )(z, w, res, mod)


def _proj_out_dz(dyx, w, name):
    t, d = dyx.shape
    width = w.shape[0]
    tm, tn = _tile(t, 1024), _tile(width, 512)

    def body(a_ref, w_ref, o_ref):
        o_ref[...] = _dot(a_ref[...], w_ref[...], 1, 1)

    return _pc(
        body, name=name, grid=(t // tm, width // tn),
        in_specs=[pl.BlockSpec((tm, d), lambda i, j: (i, 0)), pl.BlockSpec((tn, d), lambda i, j: (j, 0))],
        out_specs=pl.BlockSpec((tm, tn), lambda i, j: (i, j)),
        out_shape=jax.ShapeDtypeStruct((t, width), F32),
        compiler_params=_params("parallel", "parallel"),
    )(dyx, w)


def _proj_in_dh(dpre, w, layer, name):
    parts, t, width = dpre.shape
    d, n8 = w.shape[-2:]
    per_part = width // n8
    tm, tn = _tile(t, 1152), _tile(d, 512)

    def body(a_ref, w_ref, o_ref):
        part = _dot(a_ref[:, 0:n8], w_ref[0], 1, 1)
        for s in range(1, per_part):
            part += _dot(a_ref[:, s * n8:(s + 1) * n8], w_ref[s], 1, 1)

        @pl.when(pl.program_id(2) == 0)
        def _():
            o_ref[...] = part

        @pl.when(pl.program_id(2) != 0)
        def _():
            o_ref[...] += part

    return _pc(
        body, name=name, grid=(t // tm, d // tn, parts),
        in_specs=[pl.BlockSpec((None, tm, width), lambda i, j, k: (k, i, 0)),
                  pl.BlockSpec((per_part, None, tn, n8), lambda i, j, k: (k, layer, j, 0))],
        out_specs=pl.BlockSpec((tm, tn), lambda i, j, k: (i, j)),
        out_shape=jax.ShapeDtypeStruct((t, d), F32),
        compiler_params=_params("parallel", "parallel", "arbitrary"),
    )(dpre, w)


def _transposed(a_ref):
    return a_ref[...].T


def _grad_w_in(h, dpre, n8, name):
    t, d = h.shape
    parts, _, width = dpre.shape
    per_part = width // n8
    tm, tk = _tile(d, 512), _tile(t, 1152)
    nk = t // tk

    def body(a_ref, b_ref, o_ref, acc_ref):
        k = pl.program_id(1)

        @pl.when(k == 0)
        def _():
            acc_ref[...] = jnp.zeros_like(acc_ref)

        at = _transposed(a_ref)
        for p in range(parts):
            r = _dot(at, b_ref[p])
            for s in range(per_part):
                acc_ref[p * per_part + s] += r[:, s * n8:(s + 1) * n8]

        @pl.when(k == nk - 1)
        def _():
            o_ref[...] = acc_ref[...].astype(BF16)

    return _pc(
        body, name=name, grid=(d // tm, nk),
        in_specs=[pl.BlockSpec((tk, tm), lambda i, k: (k, i)), pl.BlockSpec((parts, tk, width), lambda i, k: (0, k, 0))],
        out_specs=pl.BlockSpec((parts * per_part, tm, n8), lambda i, k: (0, i, 0)),
        out_shape=jax.ShapeDtypeStruct((parts * per_part, d, n8), BF16),
        scratch_shapes=[pltpu.VMEM((parts * per_part, tm, n8), F32)],
        compiler_params=_params("parallel", "arbitrary"),
    )(h, dpre)


def _grad_w_out(z, dyx, name):
    width = z.shape[1]
    t, d = dyx.shape
    tm, tk = _tile(width, 512), _tile(t, 1152)
    nk = t // tk

    def body(a_ref, b_ref, o_ref, acc_ref):
        k = pl.program_id(1)

        @pl.when(k == 0)
        def _():
            acc_ref[...] = jnp.zeros_like(acc_ref)

        acc_ref[...] += _dot(_transposed(a_ref), b_ref[...])

        @pl.when(k == nk - 1)
        def _():
            o_ref[...] = acc_ref[...].astype(BF16)

    return _pc(
        body, name=name, grid=(width // tm, nk),
        in_specs=[pl.BlockSpec((tk, tm), lambda i, k: (k, i)), pl.BlockSpec((tk, d), lambda i, k: (k, 0))],
        out_specs=pl.BlockSpec((tm, d), lambda i, k: (i, 0)),
        out_shape=jax.ShapeDtypeStruct((width, d), BF16),
        scratch_shapes=[pltpu.VMEM((tm, d), F32)],
        compiler_params=_params("parallel", "arbitrary"),
    )(z, dyx)


def _shift(v, k):
    n = v.shape[0]
    return pltpu.roll(v, k % n, 0)


def _window_sum(v, win):
    s = v + _shift(v, 1)
    step = 1
    while 2 * step < win:
        s = _shift(s, step) + _shift(s, -step)
        step *= 2
    return s


def _window_count(base, seg_len, win, shape):
    t = base + lax.broadcasted_iota(jnp.int32, shape, 0)
    hi = jnp.minimum(t + win // 2, seg_len)
    lo = jnp.maximum(t - win // 2, 0)
    return (hi - lo).astype(F32)


def _pad_offsets(segs):
    return [HALO * (s + 1) + st for s, (st, _) in enumerate(segs)]


def _for_chunks(segs, fn):
    offs = _pad_offsets(segs)
    for s, (st, ln) in enumerate(segs):
        def step(ci, carry, s=s, st=st, ln=ln):
            fn(s, st, ln, offs[s], pl.multiple_of(ci * CHUNK, CHUNK))
            return carry
        lax.fori_loop(0, ln // CHUNK, step, 0)


def _pool_fwd(pre, w_grp, scale, segs, name):
    _, t, width = pre.shape
    grp = width // len(POOL_WINDOWS)
    padded = t + HALO * (len(segs) + 1)

    def group(win, pre_ref, w_ref, sc_ref, z_ref, diff_ref, pad_ref):
        pad_ref[...] = jnp.zeros_like(pad_ref)

        def fill(s, st, ln, off, b):
            pad_ref[pl.ds(off + b, CHUNK), :] = pre_ref[0, pl.ds(st + b, CHUNK), :]

        _for_chunks(segs, fill)

        def mix(s, st, ln, off, b):
            ext = pad_ref[pl.ds(off - HALO + b, CHUNK + 2 * HALO), :]
            total = _window_sum(ext, win)[HALO:HALO + CHUNK]
            u = pre_ref[0, pl.ds(st + b, CHUNK), :]
            diff = (total / _window_count(b, ln, win, u.shape) - u).astype(BF16)
            mixed = _dot(diff, w_ref[...])
            gate = _silu(pre_ref[1, pl.ds(st + b, CHUNK), :])
            z_ref[pl.ds(st + b, CHUNK), :] = (mixed * sc_ref[...] * gate).astype(BF16)
            diff_ref[pl.ds(st + b, CHUNK), :] = diff

        _for_chunks(segs, mix)

    def body(pre_ref, w_ref, sc_ref, z_ref, diff_ref, pad_ref):
        gi = pl.program_id(0)
        for widx, win in enumerate(POOL_WINDOWS):
            @pl.when(gi == widx)
            def _(win=win):
                group(win, pre_ref, w_ref, sc_ref, z_ref, diff_ref, pad_ref)

    col = pl.BlockSpec((t, grp), lambda g: (0, g))
    return _pc(
        body, name=name, grid=(len(POOL_WINDOWS),),
        in_specs=[pl.BlockSpec((2, t, grp), lambda g: (0, 0, g)), pl.BlockSpec((None, grp, grp), lambda g: (g, 0, 0)),
                  pl.BlockSpec((1, grp), lambda g: (0, g))],
        out_specs=[col, col],
        out_shape=[jax.ShapeDtypeStruct((t, width), BF16), jax.ShapeDtypeStruct((t, width), BF16)],
        scratch_shapes=[pltpu.VMEM((padded, grp), F32)],
        compiler_params=_params("parallel"),
    )(pre, w_grp, scale)


def _pool_bwd(dz, diff, pre, w_grp, scale, segs, name):
    _, t, width = pre.shape
    grp = width // len(POOL_WINDOWS)
    padded = t + HALO * (len(segs) + 1)

    def group(win, dz_ref, diff_ref, pre_ref, w_ref, sc_ref, dpre_ref, dw_ref, dsc_ref, pad_ref, dd_ref):
        pad_ref[...] = jnp.zeros_like(pad_ref)
        dw_ref[...] = jnp.zeros_like(dw_ref)
        dsc_ref[...] = jnp.zeros_like(dsc_ref)

        def first(s, st, ln, off, b):
            rows = pl.ds(st + b, CHUNK)
            diff_v = diff_ref[rows, :]
            mixed = _dot(diff_v, w_ref[...])
            g = pre_ref[1, rows, :]
            sg = _silu(g)
            dzv = dz_ref[rows, :]
            dmixed = (dzv * sc_ref[...] * sg).astype(BF16)
            dsc_ref[...] += jnp.sum(dzv * mixed * sg, axis=0, keepdims=True)
            dpre_ref[1, rows, :] = (dzv * mixed * sc_ref[...] * _dsilu(g)).astype(BF16)
            ddiff = _dot(dmixed, w_ref[...], 1, 1)
            dw_ref[...] += _dot(diff_v, dmixed, 0, 0)
            dd_ref[rows, :] = ddiff
            pad_ref[pl.ds(off + b, CHUNK), :] = ddiff / _window_count(b, ln, win, ddiff.shape)

        _for_chunks(segs, first)

        def second(s, st, ln, off, b):
            rows = pl.ds(st + b, CHUNK)
            ext = pad_ref[pl.ds(off - HALO + b, CHUNK + 2 * HALO), :]
            total = _shift(_window_sum(ext, win), -1)[HALO:HALO + CHUNK]
            dpre_ref[0, rows, :] = (total - dd_ref[rows, :]).astype(BF16)

        _for_chunks(segs, second)

    def body(dz_ref, diff_ref, pre_ref, w_ref, sc_ref, dpre_ref, dw_ref, dsc_ref, pad_ref, dd_ref):
        gi = pl.program_id(0)
        for widx, win in enumerate(POOL_WINDOWS):
            @pl.when(gi == widx)
            def _(win=win):
                group(win, dz_ref, diff_ref, pre_ref, w_ref, sc_ref, dpre_ref, dw_ref, dsc_ref, pad_ref, dd_ref)

    col = pl.BlockSpec((t, grp), lambda g: (0, g))
    both = pl.BlockSpec((2, t, grp), lambda g: (0, 0, g))
    wspec = pl.BlockSpec((None, grp, grp), lambda g: (g, 0, 0))
    sspec = pl.BlockSpec((1, grp), lambda g: (0, g))
    return _pc(
        body, name=name, grid=(len(POOL_WINDOWS),),
        in_specs=[col, col, both, wspec, sspec],
        out_specs=[both, wspec, sspec],
        out_shape=[jax.ShapeDtypeStruct((2, t, width), BF16), jax.ShapeDtypeStruct((len(POOL_WINDOWS), grp, grp), F32),
                   jax.ShapeDtypeStruct((1, width), F32)],
        scratch_shapes=[pltpu.VMEM((padded, grp), F32), pltpu.VMEM((t, grp), F32)],
        compiler_params=_params("parallel"),
    )(dz, diff, pre, w_grp, scale)


def _conv_fwd(pre, dw, db, name):
    _, t, width = pre.shape
    cb = LANES
    segs = [(0, t)]

    def body(pre_ref, dw_ref, db_ref, z_ref, pad_ref):
        pad_ref[...] = jnp.zeros_like(pad_ref)

        def fill(s, st, ln, off, b):
            rows = pl.ds(b, CHUNK)
            pad_ref[pl.ds(off + b, CHUNK), :] = pre_ref[1, rows, :] * pre_ref[2, rows, :]

        _for_chunks(segs, fill)

        def mix(s, st, ln, off, b):
            rows = pl.ds(b, CHUNK)
            ext = pad_ref[pl.ds(off - HALO + b, CHUNK + 2 * HALO), :]
            conv = (dw_ref[0:1, :] * _shift(ext, 1) + dw_ref[1:2, :] * ext + dw_ref[2:3, :] * _shift(ext, -1))
            conv = conv[HALO:HALO + CHUNK] + db_ref[...]
            y = pre_ref[0, rows, :] * conv
            z_ref[rows, :] = (y * _silu(pre_ref[3, rows, :])).astype(BF16)

        _for_chunks(segs, mix)

    return _pc(
        body, name=name, grid=(width // cb,),
        in_specs=[pl.BlockSpec((4, t, cb), lambda j: (0, 0, j)), pl.BlockSpec((8, cb), lambda j: (0, j)),
                  pl.BlockSpec((1, cb), lambda j: (0, j))],
        out_specs=pl.BlockSpec((t, cb), lambda j: (0, j)),
        out_shape=jax.ShapeDtypeStruct((t, width), BF16),
        scratch_shapes=[pltpu.VMEM((t + 2 * HALO, cb), F32)],
        compiler_params=_params("parallel"),
    )(pre, dw, db)


def _conv_bwd(dz, pre, dw, db, name):
    _, t, width = pre.shape
    cb = LANES
    segs = [(0, t)]

    def body(dz_ref, pre_ref, dw_ref, db_ref, dpre_ref, ddw_ref, ddb_ref, pad_a, pad_c):
        pad_a[...] = jnp.zeros_like(pad_a)
        pad_c[...] = jnp.zeros_like(pad_c)
        ddw_ref[...] = jnp.zeros_like(ddw_ref)
        ddb_ref[...] = jnp.zeros_like(ddb_ref)

        def fill(s, st, ln, off, b):
            rows = pl.ds(b, CHUNK)
            pad_a[pl.ds(off + b, CHUNK), :] = pre_ref[1, rows, :] * pre_ref[2, rows, :]

        _for_chunks(segs, fill)

        def first(s, st, ln, off, b):
            rows = pl.ds(b, CHUNK)
            ext = pad_a[pl.ds(off - HALO + b, CHUNK + 2 * HALO), :]
            prev, nxt = _shift(ext, 1)[HALO:HALO + CHUNK], _shift(ext, -1)[HALO:HALO + CHUNK]
            here = ext[HALO:HALO + CHUNK]
            conv = dw_ref[0:1, :] * prev + dw_ref[1:2, :] * here + dw_ref[2:3, :] * nxt + db_ref[...]
            bg, g = pre_ref[0, rows, :], pre_ref[3, rows, :]
            dzv = dz_ref[rows, :]
            dy = dzv * _silu(g)
            dpre_ref[3, rows, :] = (dzv * (bg * conv) * _dsilu(g)).astype(BF16)
            dpre_ref[0, rows, :] = (dy * conv).astype(BF16)
            dconv = dy * bg
            pad_c[pl.ds(off + b, CHUNK), :] = dconv
            ddw_ref[0:1, :] += jnp.sum(dconv * prev, axis=0, keepdims=True)
            ddw_ref[1:2, :] += jnp.sum(dconv * here, axis=0, keepdims=True)
            ddw_ref[2:3, :] += jnp.sum(dconv * nxt, axis=0, keepdims=True)
            ddb_ref[0:1, :] += jnp.sum(dconv, axis=0, keepdims=True)

        _for_chunks(segs, first)

        def second(s, st, ln, off, b):
            rows = pl.ds(b, CHUNK)
            ext = pad_c[pl.ds(off - HALO + b, CHUNK + 2 * HALO), :]
            da = (dw_ref[0:1, :] * _shift(ext, -1) + dw_ref[1:2, :] * ext + dw_ref[2:3, :] * _shift(ext, 1))
            da = da[HALO:HALO + CHUNK]
            dpre_ref[1, rows, :] = (da * pre_ref[2, rows, :]).astype(BF16)
            dpre_ref[2, rows, :] = (da * pre_ref[1, rows, :]).astype(BF16)

        _for_chunks(segs, second)

    quad = pl.BlockSpec((4, t, cb), lambda j: (0, 0, j))
    rows8 = pl.BlockSpec((8, cb), lambda j: (0, j))
    return _pc(
        body, name=name, grid=(width // cb,),
        in_specs=[pl.BlockSpec((t, cb), lambda j: (0, j)), quad, rows8, pl.BlockSpec((1, cb), lambda j: (0, j))],
        out_specs=[quad, rows8, rows8],
        out_shape=[jax.ShapeDtypeStruct((4, t, width), BF16), jax.ShapeDtypeStruct((8, width), F32),
                   jax.ShapeDtypeStruct((8, width), F32)],
        scratch_shapes=[pltpu.VMEM((t + 2 * HALO, cb), F32), pltpu.VMEM((t + 2 * HALO, cb), F32)],
        compiler_params=_params("parallel"),
    )(dz, pre, dw, db)


PAIR_TILES = 2 * WIN_ROWS - 2


def _pair_geometry():
    lane = lax.broadcasted_iota(jnp.int32, (GRID_W, LANES), 1)
    qcol = lax.broadcasted_iota(jnp.int32, (GRID_W, LANES), 0)
    low = lane < GRID_W
    kcol = jnp.where(low, lane, lane - GRID_W)
    start = jnp.clip(qcol - WIN_COLS // 2, 0, GRID_W - WIN_COLS)
    inside = (kcol >= start) & (kcol < start + WIN_COLS)
    return low, inside


def _bias_tiles(rpb_ref, rows_ref, tiles_ref, inside):
    for h in range(2):
        rows = rpb_ref[h]
        rows_ref[h] = (pltpu.roll(rows, LANES - (WIN_COLS - 1), 1)
                       + pltpu.roll(pltpu.roll(rows, GRID_W - (WIN_COLS - 1), 1), 2 * WIN_ROWS - 1, 0))
        for t in range(PAIR_TILES):
            both = jnp.broadcast_to(rows_ref[h, t:t + 1, :], (GRID_W, LANES))
            tiles_ref[h, t] = jnp.where(inside, pltpu.roll(both, 0, 1, stride=1, stride_axis=0), MASKED)


def _bias_tiles_grad(dtiles_ref, drpb_ref):
    n = PAIR_TILES * GRID_W
    qcol = lax.broadcasted_iota(jnp.int32, (n, LANES), 0) & (GRID_W - 1)
    lane = lax.broadcasted_iota(jnp.int32, (1, LANES), 1)
    zero = jnp.zeros((1, LANES), F32)
    for h in range(2):
        v = pltpu.roll(dtiles_ref[h].reshape(n, LANES), WIN_COLS - 1, 1)
        for bit in range(6):
            v = jnp.where((qcol >> bit) & 1 == 1, pltpu.roll(v, LANES - (1 << bit), 1), v)
        sums = [jnp.sum(v[t * GRID_W:(t + 1) * GRID_W], axis=0, keepdims=True) for t in range(PAIR_TILES)]
        for r in range(2 * WIN_ROWS):
            here = sums[r] if r < PAIR_TILES else zero
            prev = pltpu.roll(sums[r - 1], GRID_W, 1) if 1 <= r <= PAIR_TILES else zero
            drpb_ref[h, r:r + 1, :] = jnp.where(lane < 2 * WIN_COLS - 1, here + prev, 0.0)


def _attn_rows(r, n_rows):
    first = jnp.clip(r - WIN_ROWS // 2, 0, n_rows - WIN_ROWS)
    return first, first - r + WIN_ROWS - 1


def _softmax(s_loc, s_ctx):
    m = jnp.maximum(jnp.max(s_loc, axis=-1, keepdims=True), jnp.max(s_ctx, axis=-1, keepdims=True))
    e_loc, e_ctx = jnp.exp(s_loc - m), jnp.exp(s_ctx - m)
    inv = 1.0 / (jnp.sum(e_loc, axis=-1, keepdims=True) + jnp.sum(e_ctx, axis=-1, keepdims=True))
    return e_loc * inv, e_ctx * inv


def _pair_bias(tiles_ref, h, j):
    return jnp.concatenate([tiles_ref[h, j + 2 * m] for m in range(WIN_ROWS // 2)], axis=1)


ROWS_PER_STEP = 4


def _attn_items(step, n_rows, q_ref, low):
    items = []
    for u in range(ROWS_PER_STEP):
        r = step * ROWS_PER_STEP + u
        first, j = _attn_rows(r, n_rows)
        rows = pl.ds(pl.multiple_of(r * GRID_W, GRID_W), GRID_W)
        keys = pl.ds(pl.multiple_of(first * GRID_W, GRID_W), WIN_ROWS * GRID_W)
        q = (q_ref[rows, :].astype(F32) * HEAD_DIM ** -0.5).astype(BF16)
        zero = jnp.zeros_like(q)
        items.append((rows, keys, j, 0, jnp.where(low, q, zero)))
        items.append((rows, keys, j, 1, jnp.where(low, zero, q)))
    return items


def _attn_fwd(qkv, gate, rpb, seq):
    _, t, width = qkv.shape
    n_rows = seq // GRID_W
    n_ctx = t - seq
    blk = WIN_ROWS * GRID_W

    def body(q_ref, k_ref, v_ref, g_ref, rpb_ref, z_ref, o_ref, rows_ref, tiles_ref):
        low, inside = _pair_geometry()
        _bias_tiles(rpb_ref, rows_ref, tiles_ref, inside)
        ctx = pl.ds(seq, n_ctx)

        def step(i, carry):
            items = _attn_items(i, n_rows, q_ref, low)
            k_ctx, v_ctx = k_ref[ctx, :], v_ref[ctx, :]
            scores = [(_dot(q, k_ref[keys, :], 1, 1) + _pair_bias(tiles_ref, h, j), _dot(q, k_ctx, 1, 1))
                      for _, keys, j, h, q in items]
            probs = [_softmax(s_loc, s_ctx) for s_loc, s_ctx in scores]
            outs = [_dot(p_loc.astype(BF16), v_ref[keys, :]) + _dot(p_ctx.astype(BF16), v_ctx)
                    for (_, keys, _, _, _), (p_loc, p_ctx) in zip(items, probs)]
            for u in range(ROWS_PER_STEP):
                rows = items[2 * u][0]
                o = jnp.where(low, outs[2 * u], outs[2 * u + 1])
                o_ref[rows, :] = o
                z_ref[rows, :] = (o * _silu(g_ref[rows, :])).astype(BF16)
            return carry

        lax.fori_loop(0, n_rows // ROWS_PER_STEP, step, 0)

    def part(p):
        return pl.BlockSpec((None, t, LANES), lambda h: (p, 0, h))

    out = pl.BlockSpec((seq, LANES), lambda h: (0, h))
    return _pc(
        body, name="attn_fwd", grid=(width // LANES,),
        in_specs=[part(0), part(1), part(2), part(0), pl.BlockSpec((2, 2 * WIN_ROWS, LANES), lambda h: (h, 0, 0))],
        out_specs=[out, out],
        out_shape=[jax.ShapeDtypeStruct((seq, width), BF16), jax.ShapeDtypeStruct((seq, width), F32)],
        scratch_shapes=[pltpu.VMEM((2, 2 * WIN_ROWS, LANES), F32), pltpu.VMEM((2, PAIR_TILES, GRID_W, LANES), F32)],
        compiler_params=_params("parallel"),
    )(qkv, qkv, qkv, gate, rpb)


def _attn_bwd(qkv, gate, o, dz, rpb, seq):
    _, t, width = qkv.shape
    n_rows = seq // GRID_W
    n_ctx = t - seq
    blk = WIN_ROWS * GRID_W
    heads = 2 * width // LANES

    def body(q_ref, k_ref, v_ref, g_ref, o_ref, dz_ref, rpb_ref, dpre_ref, drpb_ref,
             rows_ref, tiles_ref, dtiles_ref, dk_ref, dv_ref):
        low, inside = _pair_geometry()
        _bias_tiles(rpb_ref, rows_ref, tiles_ref, inside)
        dtiles_ref[...] = jnp.zeros_like(dtiles_ref)
        dk_ref[...] = jnp.zeros_like(dk_ref)
        dv_ref[...] = jnp.zeros_like(dv_ref)
        ctx = pl.ds(seq, n_ctx)

        def step(i, carry):
            items = _attn_items(i, n_rows, q_ref, low)
            k_ctx, v_ctx = k_ref[ctx, :], v_ref[ctx, :]
            d_outs = []
            for u in range(ROWS_PER_STEP):
                rows = items[2 * u][0]
                g = g_ref[rows, :]
                dzv = dz_ref[rows, :]
                dpre_ref[3, rows, :] = (dzv * o_ref[rows, :] * _dsilu(g)).astype(BF16)
                d_o = (dzv * _silu(g)).astype(BF16)
                zero = jnp.zeros_like(d_o)
                d_outs += [jnp.where(low, d_o, zero), jnp.where(low, zero, d_o)]
            scores = [(_dot(q, k_ref[keys, :], 1, 1) + _pair_bias(tiles_ref, h, j), _dot(q, k_ctx, 1, 1))
                      for _, keys, j, h, q in items]
            dprobs = [(_dot(doh, v_ref[keys, :], 1, 1), _dot(doh, v_ctx, 1, 1))
                      for (_, keys, _, _, _), doh in zip(items, d_outs)]
            probs = [_softmax(s_loc, s_ctx) for s_loc, s_ctx in scores]
            dscores = []
            for (p_loc, p_ctx), (dp_loc, dp_ctx) in zip(probs, dprobs):
                delta = (jnp.sum(p_loc * dp_loc, axis=-1, keepdims=True)
                         + jnp.sum(p_ctx * dp_ctx, axis=-1, keepdims=True))
                dscores.append((p_loc * (dp_loc - delta), p_ctx * (dp_ctx - delta)))
            dqs = [_dot(ds_loc.astype(BF16), k_ref[keys, :]) + _dot(ds_ctx.astype(BF16), k_ctx)
                   for (_, keys, _, _, _), (ds_loc, ds_ctx) in zip(items, dscores)]
            for u in range(ROWS_PER_STEP):
                rows = items[2 * u][0]
                dpre_ref[0, rows, :] = (jnp.where(low, dqs[2 * u], dqs[2 * u + 1]) * HEAD_DIM ** -0.5).astype(BF16)
            for (_, keys, j, h, q), doh, (p_loc, p_ctx), (ds_loc, ds_ctx) in zip(items, d_outs, probs, dscores):
                dk_ref[keys, :] += _dot(ds_loc.astype(BF16), q, 0, 0)
                dk_ref[ctx, :] += _dot(ds_ctx.astype(BF16), q, 0, 0)
                dv_ref[keys, :] += _dot(p_loc.astype(BF16), doh, 0, 0)
                dv_ref[ctx, :] += _dot(p_ctx.astype(BF16), doh, 0, 0)
                for m in range(WIN_ROWS // 2):
                    dtiles_ref[h, j + 2 * m] += ds_loc[:, m * LANES:(m + 1) * LANES]
            return carry

        lax.fori_loop(0, n_rows // ROWS_PER_STEP, step, 0)
        dpre_ref[1] = dk_ref[...].astype(BF16)
        dpre_ref[2] = dv_ref[...].astype(BF16)
        dpre_ref[0, ctx, :] = jnp.zeros((n_ctx, LANES), BF16)
        dpre_ref[3, ctx, :] = jnp.zeros((n_ctx, LANES), BF16)
        _bias_tiles_grad(dtiles_ref, drpb_ref)

    def part(p):
        return pl.BlockSpec((None, t, LANES), lambda h: (p, 0, h))

    lat = pl.BlockSpec((seq, LANES), lambda h: (0, h))
    rspec = pl.BlockSpec((2, 2 * WIN_ROWS, LANES), lambda h: (h, 0, 0))
    tiles = pltpu.VMEM((2, PAIR_TILES, GRID_W, LANES), F32)
    return _pc(
        body, name="attn_bwd", grid=(width // LANES,),
        in_specs=[part(0), part(1), part(2), part(0), lat, lat, rspec],
        out_specs=[pl.BlockSpec((4, t, LANES), lambda h: (0, 0, h)), rspec],
        out_shape=[jax.ShapeDtypeStruct((4, t, width), BF16), jax.ShapeDtypeStruct((heads, 2 * WIN_ROWS, LANES), F32)],
        scratch_shapes=[pltpu.VMEM((2, 2 * WIN_ROWS, LANES), F32), tiles, tiles,
                        pltpu.VMEM((t, LANES), F32), pltpu.VMEM((t, LANES), F32)],
        compiler_params=_params("parallel"),
    )(qkv, qkv, qkv, gate, o, dz, rpb)


def _adamw(w, m, v, parts, name, mult=None):
    rows, cols = w.shape
    tr = _tile(rows, max(8, 131072 // cols), 8)
    n_parts = len(parts)
    c1 = 1.0 - ADAM_B1 ** ADAM_STEP
    c2 = 1.0 - ADAM_B2 ** ADAM_STEP

    def body(*refs):
        w_ref, m_ref, v_ref = refs[:3]
        part_refs = refs[3:3 + n_parts]
        rest = refs[3 + n_parts:]
        g = part_refs[0][...].astype(F32)
        for p in part_refs[1:]:
            g = g + p[...].astype(F32)
        if mult is not None:
            g = g * rest[0][...]
            rest = rest[1:]
        g_ref, d_ref, nm_ref, nv_ref = rest
        m2 = ADAM_B1 * m_ref[...] + (1.0 - ADAM_B1) * g
        v2 = ADAM_B2 * v_ref[...] + (1.0 - ADAM_B2) * (g * g)
        m_hat = m2 / c1
        v_hat = v2 / c2
        g_ref[...] = g
        d_ref[...] = -ADAM_LR * (m_hat / (jnp.sqrt(v_hat) + ADAM_EPS) + ADAM_WD * w_ref[...])
        nm_ref[...] = m2
        nv_ref[...] = v2

    tile = pl.BlockSpec((tr, cols), lambda i: (i, 0))
    in_specs, args = [tile, tile, tile], [w, m, v]
    for p in parts:
        if isinstance(p, tuple):
            arr, k = p
            in_specs.append(pl.BlockSpec((None, tr, cols), lambda i, k=k: (k, i, 0)))
            args.append(arr)
        else:
            in_specs.append(tile)
            args.append(p)
    if mult is not None:
        in_specs.append(tile)
        args.append(mult)
    shape = jax.ShapeDtypeStruct((rows, cols), F32)
    return _pc(
        body, name=name, grid=(rows // tr,), in_specs=in_specs, out_specs=[tile] * 4, out_shape=[shape] * 4,
        compiler_params=_params("parallel"),
    )(*args)


def _rows128(a):
    flat = a.reshape(-1)
    pad = (-flat.shape[0]) % LANES
    if pad:
        flat = jnp.concatenate([flat, jnp.zeros((pad,), flat.dtype)])
    return flat.reshape(-1, LANES)


def _pad_rows(a, mult=8):
    pad = (-a.shape[0]) % mult
    if pad:
        a = jnp.concatenate([a, jnp.zeros((pad,) + a.shape[1:], a.dtype)], axis=0)
    return a


def kernel(x, c, ctx, c_ctx, norm_g, ada_w, ada_b, pool_w_in, pool_w_grp, pool_scale, pool_w_out, na_w_in, na_rpb, na_w_out, conv_w_in, conv_dw, conv_db, conv_w_out, final_g, loss_target, m_c_ctx, m_norm_g, m_ada_w, m_ada_b, m_pool_w_in, m_pool_w_grp, m_pool_scale, m_pool_w_out, m_na_w_in, m_na_rpb, m_na_w_out, m_conv_w_in, m_conv_dw, m_conv_db, m_conv_w_out, m_final_g, v_c_ctx, v_norm_g, v_ada_w, v_ada_b, v_pool_w_in, v_pool_w_grp, v_pool_scale, v_pool_w_out, v_na_w_in, v_na_rpb, v_na_w_out, v_conv_w_in, v_conv_dw, v_conv_db, v_conv_w_out, v_final_g):
    xi, yi, ci = _my_place()
    me = 4 * xi + 2 * yi + ci
    seq, d = x.shape[1], x.shape[2]
    n_ctx = ctx.shape[1]
    t_all = seq + n_ctx
    width = d
    depth = norm_g.shape[0]
    nb = ada_w.shape[2]
    shard = width // N_DEV
    d_rows = d // LANES
    tr = math.gcd(math.gcd(seq, n_ctx), 256)
    x_tiles = seq // tr

    n_pool = pool_scale.shape[0]
    n_grp = pool_w_grp.shape[1]
    grp = width // n_grp
    layer_weights = [[pool_w_in[0], pool_w_grp[0], pool_w_out[0]], [na_w_in[0], na_w_out[0]],
                     [conv_w_in[0], conv_w_out[0]], [pool_w_in[1], pool_w_grp[1], pool_w_out[1]]]
    in_flight, token = [], jnp.zeros((8, LANES), F32)
    for i, ws in enumerate(layer_weights):
        state = _exchange_start([w.astype(BF16) for w in ws], False, token, f"weights_start{i}")
        token = state[-1]
        in_flight.append(state)

    def landed_weight(i, t, after):
        return _exchange_wait(in_flight[i], False, after, f"weights_wait{i}_{t}", which=[t])[0]

    def as_in(w):
        return w[:, None]

    def as_grp(w):
        return w.transpose(1, 0, 2, 3).reshape(n_grp, grp, grp)

    def as_out(w):
        return w.reshape(width, d)

    small_in =_pad_rows(jnp.concatenate([_rows128(c), pool_scale, conv_dw[0], conv_db], axis=0))
    got = _gather_small(small_in, "gather_inputs")
    r0 = d_rows
    c_all = got[:, :r0].reshape(N_DEV, d)
    n_pool = pool_scale.shape[0]
    scale_full = got[:, r0:r0 + n_pool].transpose(1, 0, 2).reshape(n_pool, width)
    r1 = r0 + n_pool
    taps_full = _pad_rows(got[:, r1:r1 + 3].transpose(1, 0, 2).reshape(3, width))
    bias_full = got[:, r1 + 3:r1 + 4].transpose(1, 0, 2).reshape(1, width)

    cond = jnp.concatenate([c_all, c_ctx[None], jnp.zeros((7, d), F32)], axis=0)
    bias_mine = lax.dynamic_slice(ada_b, (0, me * nb), (depth, nb))
    mod_mine = _mod_fwd(cond, ada_w, bias_mine)
    mod_all = _gather_small(mod_mine.reshape(-1, LANES), "gather_mod")
    mod_all = mod_all.reshape(N_DEV, depth, 16, nb).transpose(1, 2, 0, 3).reshape(depth, 16, 3 * d)
    mod_x = lax.dynamic_index_in_dim(mod_all, me, 1, keepdims=False).reshape(depth, 3, d)
    mod_c = mod_all[:, 8].reshape(depth, 3, d)
    pad5 = jnp.zeros((depth, 5, d), F32)
    mod_x = jnp.concatenate([mod_x, pad5], axis=1)
    mod_c = jnp.concatenate([mod_c, pad5], axis=1)
    mods = [jnp.stack([mod_x[i], mod_c[i]]) if i < 2 else mod_x[i][None] for i in range(depth)]

    both = [(0, seq), (seq, n_ctx)]
    latent = [(0, seq)]

    def grp_slots(g):
        return g.reshape(n_grp, N_DEV, grp // N_DEV, grp).transpose(1, 0, 2, 3).reshape(N_DEV, -1, grp).astype(BF16)

    def send_grads(i, grads):
        return _exchange_start(grads, True, jnp.zeros((8, LANES), F32), f"grads_start{i}")

    xs0 = jnp.concatenate([x[0], ctx[0]], axis=0)
    h0 = _norm_fwd(xs0, norm_g[0:1] + token[0, 0], mods[0], tr, x_tiles, "norm_fwd0")
    pool_in_w0 = as_in(landed_weight(0, 0, h0))
    pre0 = _proj_in(h0, pool_in_w0, 0, width, "proj_in0")
    pool_grp_w0 = as_grp(landed_weight(0, 1, pre0))
    z0, diff0 = _pool_fwd(pre0, pool_grp_w0, scale_full[0:1], both, "pool_fwd0")
    pool_out_w0 = as_out(landed_weight(0, 2, z0))
    yx0, xs1 = _proj_out(z0, pool_out_w0, xs0, mods[0], tr, x_tiles, "proj_out0")

    h1 = _norm_fwd(xs1, norm_g[1:2], mods[1], tr, x_tiles, "norm_fwd1")
    na_in_w = as_in(landed_weight(1, 0, h1))
    per_part = width // na_w_in.shape[2]
    qkv1 = _proj_in(h1, na_in_w, 0, width, "proj_in1_qkv", blocks=(0, 3 * per_part), dtype=BF16)
    gpre1 = _proj_in(h1, na_in_w, 0, width, "proj_in1_gate", blocks=(3 * per_part, per_part))
    rpb_rows = jnp.pad(na_rpb[0], ((0, 0), (0, 2 * WIN_ROWS - na_rpb.shape[2]), (0, LANES - na_rpb.shape[3])))
    z1, o1 = _attn_fwd(qkv1, gpre1, rpb_rows, seq)
    na_out_w = as_out(landed_weight(1, 1, z1))
    yx1, x2 = _proj_out(z1, na_out_w, xs1, mods[1], tr, x_tiles, "proj_out1")

    h2 = _norm_fwd(x2, norm_g[2:3], mods[2], tr, x_tiles, "norm_fwd2")
    conv_in_w = as_in(landed_weight(2, 0, h2))
    pre2 = _proj_in(h2, conv_in_w, 0, width, "proj_in2")
    z2 = _conv_fwd(pre2, taps_full, bias_full, "conv_fwd")
    conv_out_w = as_out(landed_weight(2, 1, z2))
    yx2, x3 = _proj_out(z2, conv_out_w, x2, mods[2], tr, x_tiles, "proj_out2")

    h3 = _norm_fwd(x3, norm_g[3:4], mods[3], tr, x_tiles, "norm_fwd3")
    pool_in_w3 = as_in(landed_weight(3, 0, h3))
    pre3 = _proj_in(h3, pool_in_w3, 0, width, "proj_in3")
    pool_grp_w3 = as_grp(landed_weight(3, 1, pre3))
    z3, diff3 = _pool_fwd(pre3, pool_grp_w3, scale_full[1:2], latent, "pool_fwd3")
    pool_out_w3 = as_out(landed_weight(3, 2, z3))
    yx3, x4 = _proj_out(z3, pool_out_w3, x3, mods[3], tr, x_tiles, "proj_out3")

    loss_part, dx4, d_final = _loss_head(x4, loss_target[0], final_g[None], tr)
    loss = lax.psum(loss_part[0, 0], ("x", "y", "c"))

    dyx3, gate3 = _resid_bwd(dx4, yx3, mods[3], tr, x_tiles, "resid_bwd3")
    dz3 = _proj_out_dz(dyx3, pool_out_w3, "proj_out_dz3")
    g_pool_out1 = _grad_w_out(z3, dyx3, "grad_w_out3")
    dpre3, g_grp1, g_scale1 = _pool_bwd(dz3, diff3, pre3, pool_grp_w3, scale_full[1:2], latent, "pool_bwd3")
    dh3 = _proj_in_dh(dpre3, pool_in_w3, 0, "proj_in_dh3")
    g_pool_in1 = _grad_w_in(h3, dpre3, pool_w_in.shape[2], "grad_w_in3")
    sent3 = send_grads(3, [g_pool_in1, grp_slots(g_grp1), g_pool_out1.reshape(N_DEV, shard, d)])
    dx3, norm3 = _norm_bwd(x3, dh3, dx4, norm_g[3:4] + sent3[-1][0, 0], mods[3], tr, x_tiles, "norm_bwd3")

    dyx2, gate2 = _resid_bwd(dx3, yx2, mods[2], tr, x_tiles, "resid_bwd2")
    dz2 = _proj_out_dz(dyx2, conv_out_w, "proj_out_dz2")
    g_conv_out = _grad_w_out(z2, dyx2, "grad_w_out2")
    dpre2, g_taps, g_cbias = _conv_bwd(dz2, pre2, taps_full, bias_full, "conv_bwd")
    dh2 = _proj_in_dh(dpre2, conv_in_w, 0, "proj_in_dh2")
    g_conv_in = _grad_w_in(h2, dpre2, conv_w_in.shape[2], "grad_w_in2")
    sent2 = send_grads(2, [g_conv_in, g_conv_out.reshape(N_DEV, shard, d)])
    dx2, norm2 = _norm_bwd(x2, dh2, dx3, norm_g[2:3] + sent2[-1][0, 0], mods[2], tr, x_tiles, "norm_bwd2")

    dyx1, gate1 = _resid_bwd(dx2, yx1, mods[1][:1], tr, x_tiles, "resid_bwd1")
    dz1 = _proj_out_dz(dyx1, na_out_w, "proj_out_dz1")
    g_na_out = _grad_w_out(z1, dyx1, "grad_w_out1")
    dpre1, g_rpb = _attn_bwd(qkv1, gpre1, o1, dz1, rpb_rows, seq)
    g_rpb = g_rpb[:, :na_rpb.shape[2], :na_rpb.shape[3]]
    dh1 = _proj_in_dh(dpre1, na_in_w, 0, "proj_in_dh1")
    g_na_in = _grad_w_in(h1, dpre1, na_w_in.shape[2], "grad_w_in1")
    sent1 = send_grads(1, [g_na_in, g_na_out.reshape(N_DEV, shard, d)])
    dres1 = jnp.concatenate([dx2, jnp.zeros((n_ctx, d), F32)], axis=0)
    dxs1, norm1 = _norm_bwd(xs1, dh1, dres1, norm_g[1:2] + sent1[-1][0, 0], mods[1], tr, x_tiles, "norm_bwd1")

    dyx0, gate0 = _resid_bwd(dxs1, yx0, mods[0], tr, x_tiles, "resid_bwd0")
    dz0 = _proj_out_dz(dyx0, pool_out_w0, "proj_out_dz0")
    g_pool_out0 = _grad_w_out(z0, dyx0, "grad_w_out0")
    dpre0, g_grp0, g_scale0 = _pool_bwd(dz0, diff0, pre0, pool_grp_w0, scale_full[0:1], both, "pool_bwd0")
    dh0 = _proj_in_dh(dpre0, pool_in_w0, 0, "proj_in_dh0")
    g_pool_in0 = _grad_w_in(h0, dpre0, pool_w_in.shape[2], "grad_w_in0")
    sent0 = send_grads(0, [g_pool_in0, grp_slots(g_grp0), g_pool_out0.reshape(N_DEV, shard, d)])
    dxs0, norm0 = _norm_bwd(xs0, dh0, dxs1, norm_g[0:1] + sent0[-1][0, 0], mods[0], tr, x_tiles, "norm_bwd0")
    grad_x = dxs0[:seq][None]

    norms, gates = [norm0, norm1, norm2, norm3], [gate0, gate1, gate2, gate3]
    zero_d = jnp.zeros((d,), F32)
    dm_rows = [jnp.concatenate([norms[i][0, 0], norms[i][0, 1], gates[i][0, 0]]) for i in range(depth)]
    dm_rows.append(jnp.concatenate([norm0[1, 0], norm0[1, 1], gate0[1, 0]]))
    dm_rows.append(jnp.concatenate([norm1[1, 0], norm1[1, 1], zero_d]))
    dm_local = jnp.stack(dm_rows + [jnp.zeros((3 * d,), F32)] * 2)
    g_norm_part = jnp.stack([norm0[0, 2] + norm0[1, 2], norm1[0, 2] + norm1[1, 2], norm2[0, 2], norm3[0, 2]])
    g_scale_part = jnp.concatenate([g_scale0, g_scale1], axis=0)
    pieces = [_rows128(dm_local), _rows128(g_norm_part), _rows128(d_final[0]), _pad_rows(_rows128(g_rpb)),
              _rows128(g_scale_part), _rows128(g_taps[:3]), _rows128(g_cbias[0])]
    small_out = _gather_small(jnp.concatenate(pieces, axis=0), "gather_small_grads")
    marks = np.cumsum([0] + [p.shape[0] for p in pieces])

    def piece(k):
        return small_out[:, marks[k]:marks[k + 1]]

    dm_all = piece(0).reshape(N_DEV, 8, 3 * d).transpose(1, 0, 2)
    dm_mine = lax.dynamic_slice(dm_all, (0, 0, me * nb), (8, N_DEV, nb))
    g_ada_w, g_ada_b, cctx_part, dsilu_cond = _mod_bwd(cond, ada_w, dm_all, dm_mine)
    cctx_all = _gather_small(_rows128(cctx_part[0]), "gather_cctx")

    def my_shard(a, n):
        a = a.reshape(N_DEV, n, N_DEV, shard)
        return lax.dynamic_index_in_dim(a, me, 2, keepdims=False)

    zeros7 = lambda r: jnp.zeros((N_DEV - 1, r, LANES), F32)
    ada_b_rows = _rows128(g_ada_b)
    small_parts = jnp.concatenate([
        cctx_all, piece(1), jnp.concatenate([ada_b_rows[None], zeros7(ada_b_rows.shape[0])], axis=0), piece(2), piece(3),
        my_shard(piece(4), n_pool), my_shard(piece(5), 3), my_shard(piece(6), 1)], axis=1)
    n_small = small_parts.shape[1]
    small_parts = jnp.concatenate([small_parts, jnp.zeros((N_DEV, (-n_small) % 8, LANES), F32)], axis=1)

    def pack(c_ctx_, norm_g_, ada_b_, final_g_, rpb_, scale_, taps_, cbias_):
        rows = [_rows128(c_ctx_), _rows128(norm_g_), _rows128(ada_b_), _rows128(final_g_), _pad_rows(_rows128(rpb_)),
                scale_, taps_[0], cbias_]
        return _pad_rows(jnp.concatenate(rows, axis=0)), np.cumsum([0] + [r.shape[0] for r in rows])

    w_small, smarks = pack(c_ctx, norm_g, ada_b, final_g, na_rpb, pool_scale, conv_dw, conv_db)
    m_small, _ = pack(m_c_ctx, m_norm_g, m_ada_b, m_final_g, m_na_rpb, m_pool_scale, m_conv_dw, m_conv_db)
    v_small, _ = pack(v_c_ctx, v_norm_g, v_ada_b, v_final_g, v_na_rpb, v_pool_scale, v_conv_dw, v_conv_db)
    mult = jnp.concatenate([_rows128(dsilu_cond[8]), jnp.ones((w_small.shape[0] - d_rows, LANES), F32)], axis=0)
    small_res = _adamw(w_small, m_small, v_small, [(small_parts, k) for k in range(N_DEV)], "adamw_small", mult=mult)

    def unpack(k, like):
        out = []
        for r in small_res:
            flat = r[smarks[k]:smarks[k + 1]].reshape(-1)
            out.append(flat[:like.size].reshape(like.shape))
        return out

    res = {"c_ctx": unpack(0, c_ctx), "norm_g": unpack(1, norm_g), "ada_b": unpack(2, ada_b),
           "final_g": unpack(3, final_g), "na_rpb": unpack(4, na_rpb), "pool_scale": unpack(5, pool_scale),
           "conv_dw": unpack(6, conv_dw), "conv_db": unpack(7, conv_db)}

    res["ada_w"] = [r.reshape(ada_w.shape) for r in _adamw(
        ada_w.reshape(-1, nb), m_ada_w.reshape(-1, nb), v_ada_w.reshape(-1, nb), [g_ada_w.reshape(-1, nb)], "adamw_ada_w")]

    def landed(state, i):
        return _exchange_wait(state, True, grad_x, f"grads_wait{i}")

    in3, grp3, out3 = landed(sent3, 3)
    in2, out2 = landed(sent2, 2)
    in1, out1 = landed(sent1, 1)
    in0, grp0, out0 = landed(sent0, 0)

    def big(parts, w, m, v, name):
        shape = w.shape
        view = (-1, shape[-1])
        parts = [(parts.reshape((N_DEV,) + w.reshape(view).shape), k) for k in range(N_DEV)]
        return [r.reshape(shape) for r in _adamw(w.reshape(view), m.reshape(view), v.reshape(view), parts, name)]

    def two_layers(first, second, w, m, v, name):
        a = big(first, w[0], m[0], v[0], name + "0")
        b = big(second, w[1], m[1], v[1], name + "1")
        return [jnp.stack([p, q]) for p, q in zip(a, b)]

    res["pool_w_in"] = two_layers(in0, in3, pool_w_in, m_pool_w_in, v_pool_w_in, "adamw_pool_in")
    res["pool_w_grp"] = two_layers(grp0, grp3, pool_w_grp, m_pool_w_grp, v_pool_w_grp, "adamw_pool_grp")
    res["pool_w_out"] = two_layers(out0, out3, pool_w_out, m_pool_w_out, v_pool_w_out, "adamw_pool_out")
    res["na_w_in"] = [r[None] for r in big(in1, na_w_in[0], m_na_w_in[0], v_na_w_in[0], "adamw_na_in")]
    res["na_w_out"] = [r[None] for r in big(out1, na_w_out[0], m_na_w_out[0], v_na_w_out[0], "adamw_na_out")]
    res["conv_w_in"] = [r[None] for r in big(in2, conv_w_in[0], m_conv_w_in[0], v_conv_w_in[0], "adamw_conv_in")]
    res["conv_w_out"] = [r[None] for r in big(out2, conv_w_out[0], m_conv_w_out[0], v_conv_w_out[0], "adamw_conv_out")]

    order = ["c_ctx", "norm_g", "ada_w", "ada_b", "pool_w_in", "pool_w_grp", "pool_scale", "pool_w_out", "na_w_in",
             "na_rpb", "na_w_out", "conv_w_in", "conv_dw", "conv_db", "conv_w_out", "final_g"]
    outs = [loss, grad_x]
    for j in range(4):
        outs += [res[n][j] for n in order]
    return tuple(outs)
```

```python
import functools
import math

import numpy as np
import jax
import jax.numpy as jnp
from jax import lax
from jax.experimental import pallas as pl
from jax.experimental.pallas import tpu as pltpu

F32 = jnp.float32
BF16 = jnp.bfloat16
N_DEV = 8
LANES = 128
RMS_EPS = 1e-6
GRID_W = 64
WIN_ROWS = 8
WIN_COLS = 16
HEAD_DIM = 64
POOL_WINDOWS = (2, 4, 8, 16)
HALO = 8
CHUNK = 128
MASKED = -1e30
ADAM_LR = 0.001
ADAM_B1 = 0.9
ADAM_B2 = 0.999
ADAM_EPS = 1e-08
ADAM_WD = 0.01
ADAM_STEP = 10
VMEM_LIMIT = 56 * 1024 * 1024
MESH = pl.DeviceIdType.MESH
ANY = pl.BlockSpec(memory_space=pl.ANY)
HBM = pl.BlockSpec(memory_space=pltpu.HBM)
SEM = pl.BlockSpec(memory_space=pltpu.SEMAPHORE)
EFFECT = pltpu.SideEffectType.DATAFLOW_SIDE_EFFECTING


def _pc(body, *, name, **kw):
    return pl.pallas_call(body, name=name, **kw)


def _params(*sem):
    return pltpu.CompilerParams(dimension_semantics=sem if sem else None, vmem_limit_bytes=VMEM_LIMIT)


def _dot(a, b, ca=1, cb=0, precision=None):
    return lax.dot_general(a, b, (((ca,), (cb,)), ((), ())), preferred_element_type=F32, precision=precision)


def _tile(n, pref, unit=LANES):
    best = None
    for t in range(unit, min(n, pref) + 1, unit):
        if n % t == 0:
            best = t
    return best if best is not None else n


def _sigmoid(x):
    return 1.0 / (1.0 + jnp.exp(-x))


def _silu(x):
    return x * _sigmoid(x)


def _dsilu(x):
    s = _sigmoid(x)
    return s * (1.0 + x * (1.0 - s))


def _my_place():
    return lax.axis_index("x"), lax.axis_index("y"), lax.axis_index("c")


def _flip(v, f):
    return 1 - v if f else v


def _gather_small(block, name):
    rows, cols = block.shape

    def body(x_ref, out_ref, send_sems, recv_sems):
        x, y, c = _my_place()
        me = 4 * x + 2 * y + c
        out_ref[me] = x_ref[...]
        copies = []
        for k in range(1, N_DEV):
            peer = (_flip(x, k & 4), _flip(y, k & 2), _flip(c, k & 1))
            cp = pltpu.make_async_remote_copy(
                src_ref=x_ref, dst_ref=out_ref.at[me], send_sem=send_sems.at[k - 1], recv_sem=recv_sems.at[k - 1],
                device_id=peer, device_id_type=MESH)
            cp.start()
            copies.append(cp)
        for cp in copies:
            cp.wait()

    return _pc(
        body, name=name,
        out_shape=jax.ShapeDtypeStruct((N_DEV, rows, cols), block.dtype),
        in_specs=[pl.BlockSpec(memory_space=pltpu.VMEM)],
        out_specs=pl.BlockSpec(memory_space=pltpu.VMEM),
        scratch_shapes=[pltpu.SemaphoreType.DMA((N_DEV - 1,)), pltpu.SemaphoreType.DMA((N_DEV - 1,))],
    )(block)


def _exchange_copies(srcs, lands, send_sems, recv_sems, per_dest):
    x, y, c = _my_place()
    me = 4 * x + 2 * y + c
    copies = []
    for t, (src, land) in enumerate(zip(srcs, lands)):
        for k in range(1, N_DEV):
            peer = (_flip(x, k & 4), _flip(y, k & 2), _flip(c, k & 1))
            dest = 4 * peer[0] + 2 * peer[1] + peer[2]
            s = t * (N_DEV - 1) + k - 1
            copies.append(pltpu.make_async_remote_copy(
                src_ref=src.at[dest] if per_dest else src, dst_ref=land.at[me],
                send_sem=send_sems[s], recv_sem=recv_sems[s], device_id=peer, device_id_type=MESH))
    return copies


def _exchange_start(srcs, per_dest, after, name):
    nt = len(srcs)
    ns = nt * (N_DEV - 1)
    lands = [lax.empty((N_DEV,) + (s.shape[1:] if per_dest else s.shape), s.dtype) for s in srcs]

    def body(*refs):
        ins, outs = refs[:2 * nt + 1], refs[2 * nt + 1:]
        for cp in _exchange_copies(ins[:nt], ins[nt:2 * nt], outs[:ns], outs[ns:2 * ns], per_dest):
            cp.start()
        outs[-1][...] = jnp.zeros_like(outs[-1])

    hbm = [pltpu.with_memory_space_constraint(a, pltpu.HBM) for a in list(srcs) + lands]
    res = _pc(
        body, name=name,
        out_shape=(*[pltpu.SemaphoreType.DMA(())] * (2 * ns), *[pltpu.HBM(a.shape, a.dtype) for a in hbm],
                   jax.ShapeDtypeStruct((8, LANES), F32)),
        in_specs=[HBM] * (2 * nt) + [ANY],
        out_specs=(*[SEM] * (2 * ns), *[HBM] * (2 * nt), pl.BlockSpec(memory_space=pltpu.VMEM)),
        input_output_aliases={i: 2 * ns + i for i in range(2 * nt)},
        compiler_params=pltpu.CompilerParams(has_side_effects=EFFECT),
    )(*hbm, after)
    sems, rest = res[:2 * ns], res[2 * ns:]
    return list(sems[:ns]), list(sems[ns:]), list(rest[:nt]), list(rest[nt:2 * nt]), rest[-1]


def _exchange_wait(state, per_dest, after, name, which=None):
    send_sems, recv_sems, srcs, lands, _ = state
    which = list(range(len(srcs))) if which is None else which
    pick = [t * (N_DEV - 1) + k for t in which for k in range(N_DEV - 1)]
    send_sems, recv_sems = [send_sems[s] for s in pick], [recv_sems[s] for s in pick]
    srcs, lands = [srcs[t] for t in which], [lands[t] for t in which]
    nt = len(srcs)
    ns = len(send_sems)

    def body(*refs):
        sems = refs[2 * nt:2 * nt + 2 * ns]
        for cp in _exchange_copies(refs[:nt], refs[nt:2 * nt], sems[:ns], sems[ns:], per_dest):
            cp.wait_send()
            cp.wait_recv()

    thru = list(srcs) + list(lands)
    res = _pc(
        body, name=name,
        out_shape=tuple(pltpu.HBM(a.shape, a.dtype) for a in thru),
        in_specs=[HBM] * (2 * nt) + [SEM] * (2 * ns) + [ANY],
        out_specs=tuple([HBM] * (2 * nt)),
        input_output_aliases={i: i for i in range(2 * nt)},
        compiler_params=pltpu.CompilerParams(has_side_effects=EFFECT),
    )(*thru, *send_sems, *recv_sems, after)
    me = 4 * lax.axis_index("x") + 2 * lax.axis_index("y") + lax.axis_index("c")
    out = []
    for src, land in zip(res[:nt], res[nt:]):
        own = lax.dynamic_index_in_dim(src, me, 0, keepdims=True) if per_dest else src[None]
        out.append(lax.dynamic_update_slice_in_dim(land, own, me, 0))
    return out


def _mod_fwd(cond, ada_w, bias):
    depth, d, nb = ada_w.shape

    def body(c_ref, w_ref, b_ref, o_ref):
        s = _silu(c_ref[...]).astype(BF16)
        o_ref[...] = _dot(s, w_ref[...].astype(BF16)) + b_ref[...]

    return _pc(
        body, name="mod_fwd", grid=(depth,),
        in_specs=[pl.BlockSpec((16, d), lambda i: (0, 0)), pl.BlockSpec((None, d, nb), lambda i: (i, 0, 0)),
                  pl.BlockSpec((None, 1, nb), lambda i: (i, 0, 0))],
        out_specs=pl.BlockSpec((None, 16, nb), lambda i: (i, 0, 0)),
        out_shape=jax.ShapeDtypeStruct((depth, 16, nb), F32),
        compiler_params=_params("parallel"),
    )(cond, ada_w, bias.reshape(depth, 1, nb))


def _mod_bwd(cond, ada_w, dm_all, dm_mine):
    depth, d, nb = ada_w.shape
    d3 = dm_all.shape[-1]

    def body(c_ref, w_ref, all_ref, call_ref, mine_ref, cmine_ref, gw_ref, gb_ref, part_ref, ds_ref):
        i = pl.program_id(0)
        cond_v = c_ref[...]
        s = _silu(cond_v).astype(BF16)
        has_ctx = jnp.where(i < 2, 1.0, 0.0)
        tot_all = jnp.sum(call_ref[...], axis=0, keepdims=True) * has_ctx
        tot_mine = jnp.broadcast_to(jnp.sum(cmine_ref[...], axis=0, keepdims=True) * has_ctx, (8, nb)).astype(BF16)
        gb_ref[...] = jnp.sum(all_ref[...], axis=0, keepdims=True) + tot_all
        gw_ref[...] = _dot(s[0:8], mine_ref[...].astype(BF16), 0, 0) + _dot(s[8:16], tot_mine, 0, 0)
        part = _dot(tot_mine, w_ref[...].astype(BF16), 1, 1)

        @pl.when(i == 0)
        def _():
            part_ref[...] = jnp.zeros_like(part_ref)
            ds_ref[...] = _dsilu(cond_v)

        part_ref[...] += part

    def rows(width, which):
        return pl.BlockSpec((None, N_DEV, width), which)

    layer = lambda i: (i, 0, 0)
    ctx_layer = lambda i: (jnp.minimum(i, 1) + 4, 0, 0)
    return _pc(
        body, name="mod_bwd", grid=(depth,),
        in_specs=[pl.BlockSpec((16, d), lambda i: (0, 0)), pl.BlockSpec((None, d, nb), layer),
                  rows(d3, layer), rows(d3, ctx_layer), rows(nb, layer), rows(nb, ctx_layer)],
        out_specs=[pl.BlockSpec((None, d, nb), layer), pl.BlockSpec((None, 1, d3), layer),
                   pl.BlockSpec((8, d), lambda i: (0, 0)), pl.BlockSpec((16, d), lambda i: (0, 0))],
        out_shape=[jax.ShapeDtypeStruct((depth, d, nb), F32), jax.ShapeDtypeStruct((depth, 1, d3), F32),
                   jax.ShapeDtypeStruct((8, d), F32), jax.ShapeDtypeStruct((16, d), F32)],
        compiler_params=_params("arbitrary"),
    )(cond, ada_w, dm_all, dm_all, dm_mine, dm_mine)


def _norm_fwd(xs, g, mod, tr, seg_tiles, name):
    t, d = xs.shape

    def body(x_ref, g_ref, mod_ref, h_ref):
        x = x_ref[...]
        r = lax.rsqrt(jnp.mean(x * x, axis=-1, keepdims=True) + RMS_EPS)
        y = (x * r) * g_ref[...]
        h_ref[...] = (y * (1.0 + mod_ref[1:2, :]) + mod_ref[0:1, :]).astype(BF16)

    return _pc(
        body, name=name, grid=(t // tr,),
        in_specs=[pl.BlockSpec((tr, d), lambda i: (i, 0)), pl.BlockSpec((1, d), lambda i: (0, 0)),
                  pl.BlockSpec((None, 8, d), lambda i: (i // seg_tiles, 0, 0))],
        out_specs=pl.BlockSpec((tr, d), lambda i: (i, 0)),
        out_shape=jax.ShapeDtypeStruct((t, d), BF16),
        compiler_params=_params("parallel"),
    )(xs, g, mod)


def _resid_grad(dx, i, seg_tiles, yx_ref, gate_ref, dyx_ref, gsum_ref):
    dyx_ref[...] = (dx * gate_ref[2:3, :]).astype(BF16)

    @pl.when(i % seg_tiles == 0)
    def _():
        gsum_ref[...] = jnp.zeros_like(gsum_ref)

    gsum_ref[0:1, :] += jnp.sum(dx * yx_ref[...], axis=0, keepdims=True)


def _norm_bwd(xs, dh, dres, g, mod, tr, seg_tiles, name, res_tiles=None, out_tiles=None, below=None):
    t, d = xs.shape
    n_tiles = t // tr
    res_tiles = n_tiles if res_tiles is None else res_tiles
    out_tiles = n_tiles if out_tiles is None else out_tiles

    def body(x_ref, dh_ref, dres_ref, g_ref, mod_ref, *rest):
        i = pl.program_id(0)
        x = x_ref[...]
        r = lax.rsqrt(jnp.mean(x * x, axis=-1, keepdims=True) + RMS_EPS)
        xn = x * r
        dhv = dh_ref[...]
        gain = g_ref[...]
        one_scale = 1.0 + mod_ref[1:2, :]
        dxn = dhv * (gain * one_scale)
        dx = r * (dxn - xn * jnp.mean(dxn * xn, axis=-1, keepdims=True))
        if res_tiles == n_tiles:
            dx = dx + dres_ref[...]
        else:
            dx = dx + jnp.where(i < res_tiles, dres_ref[...], 0.0)
        if below is None:
            dx_ref, sum_ref = rest
        else:
            yx_ref, gate_ref, dx_ref, sum_ref, dyx_ref, gsum_ref = rest
            _resid_grad(dx, i, seg_tiles, yx_ref, gate_ref, dyx_ref, gsum_ref)
        if out_tiles == n_tiles:
            dx_ref[...] = dx
        else:
            @pl.when(i < out_tiles)
            def _():
                dx_ref[...] = dx

        @pl.when(i % seg_tiles == 0)
        def _():
            sum_ref[...] = jnp.zeros_like(sum_ref)

        sum_ref[0:1, :] += jnp.sum(dhv, axis=0, keepdims=True)
        sum_ref[1:2, :] += jnp.sum(dhv * (xn * gain), axis=0, keepdims=True)
        sum_ref[2:3, :] += jnp.sum(dhv * one_scale * xn, axis=0, keepdims=True)

    row = pl.BlockSpec((tr, d), lambda i: (i, 0))
    seg = pl.BlockSpec((None, 8, d), lambda i: (i // seg_tiles, 0, 0))
    in_specs = [row, row, pl.BlockSpec((tr, d), lambda i: (jnp.minimum(i, res_tiles - 1), 0)),
                pl.BlockSpec((1, d), lambda i: (0, 0)), seg]
    out_specs = [pl.BlockSpec((tr, d), lambda i: (jnp.minimum(i, out_tiles - 1), 0)), seg]
    out_shape = [jax.ShapeDtypeStruct((out_tiles * tr, d), F32), jax.ShapeDtypeStruct((mod.shape[0], 8, d), F32)]
    args = [xs, dh, dres, g, mod]
    if below is not None:
        in_specs += [row, seg]
        out_specs += [row, seg]
        out_shape += [jax.ShapeDtypeStruct((t, d), BF16), jax.ShapeDtypeStruct((below[1].shape[0], 8, d), F32)]
        args += list(below)
    return _pc(
        body, name=name, grid=(n_tiles,), in_specs=in_specs, out_specs=out_specs, out_shape=out_shape,
        compiler_params=_params("arbitrary"),
    )(*args)


def _loss_head(xs, target, g, yx, mod, tr):
    t, d = xs.shape

    def body(x_ref, t_ref, g_ref, yx_ref, gate_ref, loss_ref, dx_ref, dg_ref, dyx_ref, gsum_ref):
        i = pl.program_id(0)
        x = x_ref[...]
        r = lax.rsqrt(jnp.mean(x * x, axis=-1, keepdims=True) + RMS_EPS)
        xn = x * r
        gain = g_ref[...]
        err = xn * gain - t_ref[...]
        dy = err * (1.0 / d)
        dxn = dy * gain
        dx = r * (dxn - xn * jnp.mean(dxn * xn, axis=-1, keepdims=True))
        dx_ref[...] = dx
        _resid_grad(dx, i, t // tr, yx_ref, gate_ref, dyx_ref, gsum_ref)

        @pl.when(i == 0)
        def _():
            loss_ref[...] = jnp.zeros_like(loss_ref)
            dg_ref[...] = jnp.zeros_like(dg_ref)

        loss_ref[...] += 0.5 * jnp.sum(jnp.mean(err * err, axis=-1, keepdims=True))
        dg_ref[0:1, :] += jnp.sum(dy * xn, axis=0, keepdims=True)

    row = pl.BlockSpec((tr, d), lambda i: (i, 0))
    seg = pl.BlockSpec((None, 8, d), lambda i: (0, 0, 0))
    return _pc(
        body, name="loss_head", grid=(t // tr,),
        in_specs=[row, row, pl.BlockSpec((1, d), lambda i: (0, 0)), row, seg],
        out_specs=[pl.BlockSpec((8, LANES), lambda i: (0, 0)), row, pl.BlockSpec((8, d), lambda i: (0, 0)), row, seg],
        out_shape=[jax.ShapeDtypeStruct((8, LANES), F32), jax.ShapeDtypeStruct((t, d), F32),
                   jax.ShapeDtypeStruct((8, d), F32), jax.ShapeDtypeStruct((t, d), BF16),
                   jax.ShapeDtypeStruct((1, 8, d), F32)],
        compiler_params=_params("arbitrary"),
    )(xs, target, g, yx, mod)


def _proj_in(h, w, layer, width, name, blocks=None, dtype=F32):
    t, d = h.shape
    n8 = w.shape[-1]
    first, count = blocks if blocks is not None else (0, N_DEV)
    per_part = width // n8
    tm = _tile(t, 1152)

    def body(a_ref, b_ref, o_ref):
        o_ref[...] = _dot(a_ref[...], b_ref[...]).astype(dtype)

    return _pc(
        body, name=name, grid=(t // tm, count),
        in_specs=[pl.BlockSpec((tm, d), lambda i, j: (i, 0)),
                  pl.BlockSpec((None, None, d, n8), lambda i, j: (first + j, layer, 0, 0))],
        out_specs=pl.BlockSpec((None, tm, n8), lambda i, j: (j // per_part, i, j % per_part)),
        out_shape=jax.ShapeDtypeStruct((count // per_part, t, width), dtype),
        compiler_params=_params("parallel", "parallel"),
    )(h, w)


def _proj_out(z, w, res, mod, tm, seg_tiles, name, nxt=None):
    t, k = z.shape
    d = w.shape[1]

    def body(z_ref, w_ref, res_ref, mod_ref, *rest):
        yx = _dot(z_ref[...], w_ref[...])
        x = res_ref[...] + mod_ref[2:3, :] * yx
        if nxt is None:
            yx_ref, x_ref = rest
        else:
            g_ref, nmod_ref, yx_ref, x_ref, h_ref = rest
            r = lax.rsqrt(jnp.mean(x * x, axis=-1, keepdims=True) + RMS_EPS)
            h_ref[...] = (((x * r) * g_ref[...]) * (1.0 + nmod_ref[1:2, :]) + nmod_ref[0:1, :]).astype(BF16)
        yx_ref[...] = yx
        x_ref[...] = x

    tile = pl.BlockSpec((tm, d), lambda i: (i, 0))
    seg = pl.BlockSpec((None, 8, d), lambda i: (i // seg_tiles, 0, 0))
    in_specs = [pl.BlockSpec((tm, k), lambda i: (i, 0)), pl.BlockSpec((k, d), lambda i: (0, 0)), tile, seg]
    out_specs = [tile, tile]
    out_shape = [jax.ShapeDtypeStruct((t, d), F32), jax.ShapeDtypeStruct((t, d), F32)]
    args = [z, w, res, mod]
    if nxt is not None:
        in_specs += [pl.BlockSpec((1, d), lambda i: (0, 0)), seg]
        out_specs.append(tile)
        out_shape.append(jax.ShapeDtypeStruct((t, d), BF16))
        args += list(nxt)
    return _pc(
        body, name=name, grid=(t // tm,), in_specs=in_specs, out_specs=out_specs, out_shape=out_shape,
        compiler_params=_params("parallel"),
    )(*args)


def _proj_out_dz(dyx, w, name):
    t, d = dyx.shape
    width = w.shape[0]
    tm, tn = _tile(t, 1024), _tile(width, 512)

    def body(a_ref, w_ref, o_ref):
        o_ref[...] = _dot(a_ref[...], w_ref[...], 1, 1)

    return _pc(
        body, name=name, grid=(t // tm, width // tn),
        in_specs=[pl.BlockSpec((tm, d), lambda i, j: (i, 0)), pl.BlockSpec((tn, d), lambda i, j: (j, 0))],
        out_specs=pl.BlockSpec((tm, tn), lambda i, j: (i, j)),
        out_shape=jax.ShapeDtypeStruct((t, width), F32),
        compiler_params=_params("parallel", "parallel"),
    )(dyx, w)


def _proj_in_dh(dpre, w, layer, name):
    parts, t, width = dpre.shape
    d, n8 = w.shape[-2:]
    per_part = width // n8
    tm, tn = _tile(t, 1152), _tile(d, 512)

    def body(a_ref, w_ref, o_ref):
        part = _dot(a_ref[:, 0:n8], w_ref[0], 1, 1)
        for s in range(1, per_part):
            part += _dot(a_ref[:, s * n8:(s + 1) * n8], w_ref[s], 1, 1)

        @pl.when(pl.program_id(2) == 0)
        def _():
            o_ref[...] = part

        @pl.when(pl.program_id(2) != 0)
        def _():
            o_ref[...] += part

    return _pc(
        body, name=name, grid=(t // tm, d // tn, parts),
        in_specs=[pl.BlockSpec((None, tm, width), lambda i, j, k: (k, i, 0)),
                  pl.BlockSpec((per_part, None, tn, n8), lambda i, j, k: (k, layer, j, 0))],
        out_specs=pl.BlockSpec((tm, tn), lambda i, j, k: (i, j)),
        out_shape=jax.ShapeDtypeStruct((t, d), F32),
        compiler_params=_params("parallel", "parallel", "arbitrary"),
    )(dpre, w)


def _transposed(a_ref):
    return a_ref[...].T


def _grad_w_in(h, dpre, n8, name):
    t, d = h.shape
    parts, _, width = dpre.shape
    per_part = width // n8
    tm, tk = _tile(d, 512), _tile(t, 1152)
    nk = t // tk

    def body(a_ref, b_ref, o_ref, acc_ref):
        k = pl.program_id(1)

        @pl.when(k == 0)
        def _():
            acc_ref[...] = jnp.zeros_like(acc_ref)

        at = _transposed(a_ref)
        for p in range(parts):
            r = _dot(at, b_ref[p])
            for s in range(per_part):
                acc_ref[p * per_part + s] += r[:, s * n8:(s + 1) * n8]

        @pl.when(k == nk - 1)
        def _():
            o_ref[...] = acc_ref[...].astype(BF16)

    return _pc(
        body, name=name, grid=(d // tm, nk),
        in_specs=[pl.BlockSpec((tk, tm), lambda i, k: (k, i)), pl.BlockSpec((parts, tk, width), lambda i, k: (0, k, 0))],
        out_specs=pl.BlockSpec((parts * per_part, tm, n8), lambda i, k: (0, i, 0)),
        out_shape=jax.ShapeDtypeStruct((parts * per_part, d, n8), BF16),
        scratch_shapes=[pltpu.VMEM((parts * per_part, tm, n8), F32)],
        compiler_params=_params("parallel", "arbitrary"),
    )(h, dpre)


def _grad_w_out(z, dyx, name):
    width = z.shape[1]
    t, d = dyx.shape
    tm, tk = _tile(width, 512), _tile(t, 1152)
    nk = t // tk

    def body(a_ref, b_ref, o_ref, acc_ref):
        k = pl.program_id(1)

        @pl.when(k == 0)
        def _():
            acc_ref[...] = jnp.zeros_like(acc_ref)

        acc_ref[...] += _dot(_transposed(a_ref), b_ref[...])

        @pl.when(k == nk - 1)
        def _():
            o_ref[...] = acc_ref[...].astype(BF16)

    return _pc(
        body, name=name, grid=(width // tm, nk),
        in_specs=[pl.BlockSpec((tk, tm), lambda i, k: (k, i)), pl.BlockSpec((tk, d), lambda i, k: (k, 0))],
        out_specs=pl.BlockSpec((tm, d), lambda i, k: (i, 0)),
        out_shape=jax.ShapeDtypeStruct((width, d), BF16),
        scratch_shapes=[pltpu.VMEM((tm, d), F32)],
        compiler_params=_params("parallel", "arbitrary"),
    )(z, dyx)


def _shift(v, k):
    n = v.shape[0]
    return pltpu.roll(v, k % n, 0)


def _window_sum(v, win):
    s = v + _shift(v, 1)
    step = 1
    while 2 * step < win:
        s = _shift(s, step) + _shift(s, -step)
        step *= 2
    return s


def _window_count(base, seg_len, win, shape):
    t = base + lax.broadcasted_iota(jnp.int32, shape, 0)
    hi = jnp.minimum(t + win // 2, seg_len)
    lo = jnp.maximum(t - win // 2, 0)
    return (hi - lo).astype(F32)


def _pad_offsets(segs):
    return [HALO * (s + 1) + st for s, (st, _) in enumerate(segs)]


def _for_chunks(segs, fn):
    offs = _pad_offsets(segs)
    for s, (st, ln) in enumerate(segs):
        def step(ci, carry, s=s, st=st, ln=ln):
            fn(s, st, ln, offs[s], pl.multiple_of(ci * CHUNK, CHUNK))
            return carry
        lax.fori_loop(0, ln // CHUNK, step, 0)


def _pool_fwd(pre, w_grp, scale, segs, name):
    _, t, width = pre.shape
    grp = width // len(POOL_WINDOWS)
    padded = t + HALO * (len(segs) + 1)

    def group(win, pre_ref, w_ref, sc_ref, z_ref, diff_ref, pad_ref):
        pad_ref[...] = jnp.zeros_like(pad_ref)

        def fill(s, st, ln, off, b):
            pad_ref[pl.ds(off + b, CHUNK), :] = pre_ref[0, pl.ds(st + b, CHUNK), :]

        _for_chunks(segs, fill)

        def mix(s, st, ln, off, b):
            ext = pad_ref[pl.ds(off - HALO + b, CHUNK + 2 * HALO), :]
            total = _window_sum(ext, win)[HALO:HALO + CHUNK]
            u = pre_ref[0, pl.ds(st + b, CHUNK), :]
            diff = (total / _window_count(b, ln, win, u.shape) - u).astype(BF16)
            mixed = _dot(diff, w_ref[...])
            gate = _silu(pre_ref[1, pl.ds(st + b, CHUNK), :])
            z_ref[pl.ds(st + b, CHUNK), :] = (mixed * sc_ref[...] * gate).astype(BF16)
            diff_ref[pl.ds(st + b, CHUNK), :] = diff

        _for_chunks(segs, mix)

    def body(pre_ref, w_ref, sc_ref, z_ref, diff_ref, pad_ref):
        gi = pl.program_id(0)
        for widx, win in enumerate(POOL_WINDOWS):
            @pl.when(gi == widx)
            def _(win=win):
                group(win, pre_ref, w_ref, sc_ref, z_ref, diff_ref, pad_ref)

    col = pl.BlockSpec((t, grp), lambda g: (0, g))
    return _pc(
        body, name=name, grid=(len(POOL_WINDOWS),),
        in_specs=[pl.BlockSpec((2, t, grp), lambda g: (0, 0, g)), pl.BlockSpec((None, grp, grp), lambda g: (g, 0, 0)),
                  pl.BlockSpec((1, grp), lambda g: (0, g))],
        out_specs=[col, col],
        out_shape=[jax.ShapeDtypeStruct((t, width), BF16), jax.ShapeDtypeStruct((t, width), BF16)],
        scratch_shapes=[pltpu.VMEM((padded, grp), F32)],
        compiler_params=_params("parallel"),
    )(pre, w_grp, scale)


def _pool_bwd(dz, diff, pre, w_grp, scale, segs, name):
    _, t, width = pre.shape
    grp = width // len(POOL_WINDOWS)
    padded = t + HALO * (len(segs) + 1)

    def group(win, dz_ref, diff_ref, pre_ref, w_ref, sc_ref, dpre_ref, dw_ref, dsc_ref, pad_ref, dd_ref):
        pad_ref[...] = jnp.zeros_like(pad_ref)
        dw_ref[...] = jnp.zeros_like(dw_ref)
        dsc_ref[...] = jnp.zeros_like(dsc_ref)

        def first(s, st, ln, off, b):
            rows = pl.ds(st + b, CHUNK)
            diff_v = diff_ref[rows, :]
            mixed = _dot(diff_v, w_ref[...])
            g = pre_ref[1, rows, :]
            sg = _silu(g)
            dzv = dz_ref[rows, :]
            dmixed = (dzv * sc_ref[...] * sg).astype(BF16)
            dsc_ref[...] += jnp.sum(dzv * mixed * sg, axis=0, keepdims=True)
            dpre_ref[1, rows, :] = (dzv * mixed * sc_ref[...] * _dsilu(g)).astype(BF16)
            ddiff = _dot(dmixed, w_ref[...], 1, 1)
            dw_ref[...] += _dot(diff_v, dmixed, 0, 0)
            dd_ref[rows, :] = ddiff
            pad_ref[pl.ds(off + b, CHUNK), :] = ddiff / _window_count(b, ln, win, ddiff.shape)

        _for_chunks(segs, first)

        def second(s, st, ln, off, b):
            rows = pl.ds(st + b, CHUNK)
            ext = pad_ref[pl.ds(off - HALO + b, CHUNK + 2 * HALO), :]
            total = _shift(_window_sum(ext, win), -1)[HALO:HALO + CHUNK]
            dpre_ref[0, rows, :] = (total - dd_ref[rows, :]).astype(BF16)

        _for_chunks(segs, second)

    def body(dz_ref, diff_ref, pre_ref, w_ref, sc_ref, dpre_ref, dw_ref, dsc_ref, pad_ref, dd_ref):
        gi = pl.program_id(0)
        for widx, win in enumerate(POOL_WINDOWS):
            @pl.when(gi == widx)
            def _(win=win):
                group(win, dz_ref, diff_ref, pre_ref, w_ref, sc_ref, dpre_ref, dw_ref, dsc_ref, pad_ref, dd_ref)

    col = pl.BlockSpec((t, grp), lambda g: (0, g))
    both = pl.BlockSpec((2, t, grp), lambda g: (0, 0, g))
    wspec = pl.BlockSpec((None, grp, grp), lambda g: (g, 0, 0))
    sspec = pl.BlockSpec((1, grp), lambda g: (0, g))
    return _pc(
        body, name=name, grid=(len(POOL_WINDOWS),),
        in_specs=[col, col, both, wspec, sspec],
        out_specs=[both, wspec, sspec],
        out_shape=[jax.ShapeDtypeStruct((2, t, width), BF16), jax.ShapeDtypeStruct((len(POOL_WINDOWS), grp, grp), F32),
                   jax.ShapeDtypeStruct((1, width), F32)],
        scratch_shapes=[pltpu.VMEM((padded, grp), F32), pltpu.VMEM((t, grp), F32)],
        compiler_params=_params("parallel"),
    )(dz, diff, pre, w_grp, scale)


def _conv_fwd(pre, dw, db, name):
    _, t, width = pre.shape
    cb = LANES
    segs = [(0, t)]

    def body(pre_ref, dw_ref, db_ref, z_ref, pad_ref):
        pad_ref[...] = jnp.zeros_like(pad_ref)

        def fill(s, st, ln, off, b):
            rows = pl.ds(b, CHUNK)
            pad_ref[pl.ds(off + b, CHUNK), :] = pre_ref[1, rows, :] * pre_ref[2, rows, :]

        _for_chunks(segs, fill)

        def mix(s, st, ln, off, b):
            rows = pl.ds(b, CHUNK)
            ext = pad_ref[pl.ds(off - HALO + b, CHUNK + 2 * HALO), :]
            conv = (dw_ref[0:1, :] * _shift(ext, 1) + dw_ref[1:2, :] * ext + dw_ref[2:3, :] * _shift(ext, -1))
            conv = conv[HALO:HALO + CHUNK] + db_ref[...]
            y = pre_ref[0, rows, :] * conv
            z_ref[rows, :] = (y * _silu(pre_ref[3, rows, :])).astype(BF16)

        _for_chunks(segs, mix)

    return _pc(
        body, name=name, grid=(width // cb,),
        in_specs=[pl.BlockSpec((4, t, cb), lambda j: (0, 0, j)), pl.BlockSpec((8, cb), lambda j: (0, j)),
                  pl.BlockSpec((1, cb), lambda j: (0, j))],
        out_specs=pl.BlockSpec((t, cb), lambda j: (0, j)),
        out_shape=jax.ShapeDtypeStruct((t, width), BF16),
        scratch_shapes=[pltpu.VMEM((t + 2 * HALO, cb), F32)],
        compiler_params=_params("parallel"),
    )(pre, dw, db)


def _conv_bwd(dz, pre, dw, db, name):
    _, t, width = pre.shape
    cb = LANES
    segs = [(0, t)]

    def body(dz_ref, pre_ref, dw_ref, db_ref, dpre_ref, ddw_ref, ddb_ref, pad_a, pad_c):
        pad_a[...] = jnp.zeros_like(pad_a)
        pad_c[...] = jnp.zeros_like(pad_c)
        ddw_ref[...] = jnp.zeros_like(ddw_ref)
        ddb_ref[...] = jnp.zeros_like(ddb_ref)

        def fill(s, st, ln, off, b):
            rows = pl.ds(b, CHUNK)
            pad_a[pl.ds(off + b, CHUNK), :] = pre_ref[1, rows, :] * pre_ref[2, rows, :]

        _for_chunks(segs, fill)

        def first(s, st, ln, off, b):
            rows = pl.ds(b, CHUNK)
            ext = pad_a[pl.ds(off - HALO + b, CHUNK + 2 * HALO), :]
            prev, nxt = _shift(ext, 1)[HALO:HALO + CHUNK], _shift(ext, -1)[HALO:HALO + CHUNK]
            here = ext[HALO:HALO + CHUNK]
            conv = dw_ref[0:1, :] * prev + dw_ref[1:2, :] * here + dw_ref[2:3, :] * nxt + db_ref[...]
            bg, g = pre_ref[0, rows, :], pre_ref[3, rows, :]
            dzv = dz_ref[rows, :]
            dy = dzv * _silu(g)
            dpre_ref[3, rows, :] = (dzv * (bg * conv) * _dsilu(g)).astype(BF16)
            dpre_ref[0, rows, :] = (dy * conv).astype(BF16)
            dconv = dy * bg
            pad_c[pl.ds(off + b, CHUNK), :] = dconv
            ddw_ref[0:1, :] += jnp.sum(dconv * prev, axis=0, keepdims=True)
            ddw_ref[1:2, :] += jnp.sum(dconv * here, axis=0, keepdims=True)
            ddw_ref[2:3, :] += jnp.sum(dconv * nxt, axis=0, keepdims=True)
            ddb_ref[0:1, :] += jnp.sum(dconv, axis=0, keepdims=True)

        _for_chunks(segs, first)

        def second(s, st, ln, off, b):
            rows = pl.ds(b, CHUNK)
            ext = pad_c[pl.ds(off - HALO + b, CHUNK + 2 * HALO), :]
            da = (dw_ref[0:1, :] * _shift(ext, -1) + dw_ref[1:2, :] * ext + dw_ref[2:3, :] * _shift(ext, 1))
            da = da[HALO:HALO + CHUNK]
            dpre_ref[1, rows, :] = (da * pre_ref[2, rows, :]).astype(BF16)
            dpre_ref[2, rows, :] = (da * pre_ref[1, rows, :]).astype(BF16)

        _for_chunks(segs, second)

    quad = pl.BlockSpec((4, t, cb), lambda j: (0, 0, j))
    rows8 = pl.BlockSpec((8, cb), lambda j: (0, j))
    return _pc(
        body, name=name, grid=(width // cb,),
        in_specs=[pl.BlockSpec((t, cb), lambda j: (0, j)), quad, rows8, pl.BlockSpec((1, cb), lambda j: (0, j))],
        out_specs=[quad, rows8, rows8],
        out_shape=[jax.ShapeDtypeStruct((4, t, width), BF16), jax.ShapeDtypeStruct((8, width), F32),
                   jax.ShapeDtypeStruct((8, width), F32)],
        scratch_shapes=[pltpu.VMEM((t + 2 * HALO, cb), F32), pltpu.VMEM((t + 2 * HALO, cb), F32)],
        compiler_params=_params("parallel"),
    )(dz, pre, dw, db)


PAIR_TILES = 2 * WIN_ROWS - 2


def _pair_geometry():
    lane = lax.broadcasted_iota(jnp.int32, (GRID_W, LANES), 1)
    qcol = lax.broadcasted_iota(jnp.int32, (GRID_W, LANES), 0)
    low = lane < GRID_W
    kcol = jnp.where(low, lane, lane - GRID_W)
    start = jnp.clip(qcol - WIN_COLS // 2, 0, GRID_W - WIN_COLS)
    inside = (kcol >= start) & (kcol < start + WIN_COLS)
    return low, inside


def _bias_tiles(rpb_ref, rows_ref, tiles_ref, inside):
    for h in range(2):
        rows = rpb_ref[h]
        rows_ref[h] = (pltpu.roll(rows, LANES - (WIN_COLS - 1), 1)
                       + pltpu.roll(pltpu.roll(rows, GRID_W - (WIN_COLS - 1), 1), 2 * WIN_ROWS - 1, 0))
        for t in range(PAIR_TILES):
            both = jnp.broadcast_to(rows_ref[h, t:t + 1, :], (GRID_W, LANES))
            tiles_ref[h, t] = jnp.where(inside, pltpu.roll(both, 0, 1, stride=1, stride_axis=0), MASKED)


def _bias_tiles_grad(dtiles_ref, drpb_ref):
    n = PAIR_TILES * GRID_W
    qcol = lax.broadcasted_iota(jnp.int32, (n, LANES), 0) & (GRID_W - 1)
    lane = lax.broadcasted_iota(jnp.int32, (1, LANES), 1)
    zero = jnp.zeros((1, LANES), F32)
    for h in range(2):
        v = pltpu.roll(dtiles_ref[h].reshape(n, LANES), WIN_COLS - 1, 1)
        for bit in range(6):
            v = jnp.where((qcol >> bit) & 1 == 1, pltpu.roll(v, LANES - (1 << bit), 1), v)
        sums = [jnp.sum(v[t * GRID_W:(t + 1) * GRID_W], axis=0, keepdims=True) for t in range(PAIR_TILES)]
        for r in range(2 * WIN_ROWS):
            here = sums[r] if r < PAIR_TILES else zero
            prev = pltpu.roll(sums[r - 1], GRID_W, 1) if 1 <= r <= PAIR_TILES else zero
            drpb_ref[h, r:r + 1, :] = jnp.where(lane < 2 * WIN_COLS - 1, here + prev, 0.0)


def _attn_rows(r, n_rows):
    first = jnp.clip(r - WIN_ROWS // 2, 0, n_rows - WIN_ROWS)
    return first, first - r + WIN_ROWS - 1


def _softmax(s_loc, s_ctx):
    m = jnp.maximum(jnp.max(s_loc, axis=-1, keepdims=True), jnp.max(s_ctx, axis=-1, keepdims=True))
    e_loc, e_ctx = jnp.exp(s_loc - m), jnp.exp(s_ctx - m)
    inv = 1.0 / (jnp.sum(e_loc, axis=-1, keepdims=True) + jnp.sum(e_ctx, axis=-1, keepdims=True))
    return e_loc * inv, e_ctx * inv


def _pair_bias(tiles_ref, h, j):
    return jnp.concatenate([tiles_ref[h, j + 2 * m] for m in range(WIN_ROWS // 2)], axis=1)


ROWS_PER_STEP = 4


def _attn_items(step, n_rows, q_ref, low):
    items = []
    for u in range(ROWS_PER_STEP):
        r = step * ROWS_PER_STEP + u
        first, j = _attn_rows(r, n_rows)
        rows = pl.ds(pl.multiple_of(r * GRID_W, GRID_W), GRID_W)
        keys = pl.ds(pl.multiple_of(first * GRID_W, GRID_W), WIN_ROWS * GRID_W)
        q = (q_ref[rows, :].astype(F32) * HEAD_DIM ** -0.5).astype(BF16)
        zero = jnp.zeros_like(q)
        items.append((rows, keys, j, 0, jnp.where(low, q, zero)))
        items.append((rows, keys, j, 1, jnp.where(low, zero, q)))
    return items


def _attn_fwd(qkv, gate, rpb, seq):
    _, t, width = qkv.shape
    n_rows = seq // GRID_W
    n_ctx = t - seq
    blk = WIN_ROWS * GRID_W

    def body(q_ref, k_ref, v_ref, g_ref, rpb_ref, z_ref, o_ref, rows_ref, tiles_ref):
        low, inside = _pair_geometry()
        _bias_tiles(rpb_ref, rows_ref, tiles_ref, inside)
        ctx = pl.ds(seq, n_ctx)

        def step(i, carry):
            items = _attn_items(i, n_rows, q_ref, low)
            k_ctx, v_ctx = k_ref[ctx, :], v_ref[ctx, :]
            scores = [(_dot(q, k_ref[keys, :], 1, 1) + _pair_bias(tiles_ref, h, j), _dot(q, k_ctx, 1, 1))
                      for _, keys, j, h, q in items]
            probs = [_softmax(s_loc, s_ctx) for s_loc, s_ctx in scores]
            outs = [_dot(p_loc.astype(BF16), v_ref[keys, :]) + _dot(p_ctx.astype(BF16), v_ctx)
                    for (_, keys, _, _, _), (p_loc, p_ctx) in zip(items, probs)]
            for u in range(ROWS_PER_STEP):
                rows = items[2 * u][0]
                o = jnp.where(low, outs[2 * u], outs[2 * u + 1])
                o_ref[rows, :] = o
                z_ref[rows, :] = (o * _silu(g_ref[rows, :])).astype(BF16)
            return carry

        lax.fori_loop(0, n_rows // ROWS_PER_STEP, step, 0)

    def part(p):
        return pl.BlockSpec((None, t, LANES), lambda h: (p, 0, h))

    out = pl.BlockSpec((seq, LANES), lambda h: (0, h))
    return _pc(
        body, name="attn_fwd", grid=(width // LANES,),
        in_specs=[part(0), part(1), part(2), part(0), pl.BlockSpec((2, 2 * WIN_ROWS, LANES), lambda h: (h, 0, 0))],
        out_specs=[out, out],
        out_shape=[jax.ShapeDtypeStruct((seq, width), BF16), jax.ShapeDtypeStruct((seq, width), F32)],
        scratch_shapes=[pltpu.VMEM((2, 2 * WIN_ROWS, LANES), F32), pltpu.VMEM((2, PAIR_TILES, GRID_W, LANES), F32)],
        compiler_params=_params("parallel"),
    )(qkv, qkv, qkv, gate, rpb)


def _attn_bwd(qkv, gate, o, dz, rpb, seq):
    _, t, width = qkv.shape
    n_rows = seq // GRID_W
    n_ctx = t - seq
    blk = WIN_ROWS * GRID_W
    heads = 2 * width // LANES

    def body(q_ref, k_ref, v_ref, g_ref, o_ref, dz_ref, rpb_ref, dpre_ref, drpb_ref,
             rows_ref, tiles_ref, dtiles_ref, dk_ref, dv_ref):
        low, inside = _pair_geometry()
        _bias_tiles(rpb_ref, rows_ref, tiles_ref, inside)
        dtiles_ref[...] = jnp.zeros_like(dtiles_ref)
        dk_ref[...] = jnp.zeros_like(dk_ref)
        dv_ref[...] = jnp.zeros_like(dv_ref)
        ctx = pl.ds(seq, n_ctx)

        def step(i, carry):
            items = _attn_items(i, n_rows, q_ref, low)
            k_ctx, v_ctx = k_ref[ctx, :], v_ref[ctx, :]
            d_outs = []
            for u in range(ROWS_PER_STEP):
                rows = items[2 * u][0]
                g = g_ref[rows, :]
                dzv = dz_ref[rows, :]
                dpre_ref[3, rows, :] = (dzv * o_ref[rows, :] * _dsilu(g)).astype(BF16)
                d_o = (dzv * _silu(g)).astype(BF16)
                zero = jnp.zeros_like(d_o)
                d_outs += [jnp.where(low, d_o, zero), jnp.where(low, zero, d_o)]
            scores = [(_dot(q, k_ref[keys, :], 1, 1) + _pair_bias(tiles_ref, h, j), _dot(q, k_ctx, 1, 1))
                      for _, keys, j, h, q in items]
            dprobs = [(_dot(doh, v_ref[keys, :], 1, 1), _dot(doh, v_ctx, 1, 1))
                      for (_, keys, _, _, _), doh in zip(items, d_outs)]
            probs = [_softmax(s_loc, s_ctx) for s_loc, s_ctx in scores]
            dscores = []
            for (p_loc, p_ctx), (dp_loc, dp_ctx) in zip(probs, dprobs):
                delta = (jnp.sum(p_loc * dp_loc, axis=-1, keepdims=True)
                         + jnp.sum(p_ctx * dp_ctx, axis=-1, keepdims=True))
                dscores.append((p_loc * (dp_loc - delta), p_ctx * (dp_ctx - delta)))
            dqs = [_dot(ds_loc.astype(BF16), k_ref[keys, :]) + _dot(ds_ctx.astype(BF16), k_ctx)
                   for (_, keys, _, _, _), (ds_loc, ds_ctx) in zip(items, dscores)]
            for u in range(ROWS_PER_STEP):
                rows = items[2 * u][0]
                dpre_ref[0, rows, :] = (jnp.where(low, dqs[2 * u], dqs[2 * u + 1]) * HEAD_DIM ** -0.5).astype(BF16)
            for (_, keys, j, h, q), doh, (p_loc, p_ctx), (ds_loc, ds_ctx) in zip(items, d_outs, probs, dscores):
                dk_ref[keys, :] += _dot(ds_loc.astype(BF16), q, 0, 0)
                dk_ref[ctx, :] += _dot(ds_ctx.astype(BF16), q, 0, 0)
                dv_ref[keys, :] += _dot(p_loc.astype(BF16), doh, 0, 0)
                dv_ref[ctx, :] += _dot(p_ctx.astype(BF16), doh, 0, 0)
                for m in range(WIN_ROWS // 2):
                    dtiles_ref[h, j + 2 * m] += ds_loc[:, m * LANES:(m + 1) * LANES]
            return carry

        lax.fori_loop(0, n_rows // ROWS_PER_STEP, step, 0)
        dpre_ref[1] = dk_ref[...].astype(BF16)
        dpre_ref[2] = dv_ref[...].astype(BF16)
        dpre_ref[0, ctx, :] = jnp.zeros((n_ctx, LANES), BF16)
        dpre_ref[3, ctx, :] = jnp.zeros((n_ctx, LANES), BF16)
        _bias_tiles_grad(dtiles_ref, drpb_ref)

    def part(p):
        return pl.BlockSpec((None, t, LANES), lambda h: (p, 0, h))

    lat = pl.BlockSpec((seq, LANES), lambda h: (0, h))
    rspec = pl.BlockSpec((2, 2 * WIN_ROWS, LANES), lambda h: (h, 0, 0))
    tiles = pltpu.VMEM((2, PAIR_TILES, GRID_W, LANES), F32)
    return _pc(
        body, name="attn_bwd", grid=(width // LANES,),
        in_specs=[part(0), part(1), part(2), part(0), lat, lat, rspec],
        out_specs=[pl.BlockSpec((4, t, LANES), lambda h: (0, 0, h)), rspec],
        out_shape=[jax.ShapeDtypeStruct((4, t, width), BF16), jax.ShapeDtypeStruct((heads, 2 * WIN_ROWS, LANES), F32)],
        scratch_shapes=[pltpu.VMEM((2, 2 * WIN_ROWS, LANES), F32), tiles, tiles,
                        pltpu.VMEM((t, LANES), F32), pltpu.VMEM((t, LANES), F32)],
        compiler_params=_params("parallel"),
    )(qkv, qkv, qkv, gate, o, dz, rpb)


def _adamw(w, m, v, parts, name, mult=None):
    rows, cols = w.shape
    tr = _tile(rows, max(8, 131072 // cols), 8)
    n_parts = len(parts)
    c1 = 1.0 - ADAM_B1 ** ADAM_STEP
    c2 = 1.0 - ADAM_B2 ** ADAM_STEP

    def body(*refs):
        w_ref, m_ref, v_ref = refs[:3]
        part_refs = refs[3:3 + n_parts]
        rest = refs[3 + n_parts:]
        g = part_refs[0][...].astype(F32)
        for p in part_refs[1:]:
            g = g + p[...].astype(F32)
        if mult is not None:
            g = g * rest[0][...]
            rest = rest[1:]
        g_ref, d_ref, nm_ref, nv_ref = rest
        m2 = ADAM_B1 * m_ref[...] + (1.0 - ADAM_B1) * g
        v2 = ADAM_B2 * v_ref[...] + (1.0 - ADAM_B2) * (g * g)
        m_hat = m2 / c1
        v_hat = v2 / c2
        g_ref[...] = g
        d_ref[...] = -ADAM_LR * (m_hat / (jnp.sqrt(v_hat) + ADAM_EPS) + ADAM_WD * w_ref[...])
        nm_ref[...] = m2
        nv_ref[...] = v2

    tile = pl.BlockSpec((tr, cols), lambda i: (i, 0))
    in_specs, args = [tile, tile, tile], [w, m, v]
    for p in parts:
        if isinstance(p, tuple):
            arr, k = p
            in_specs.append(pl.BlockSpec((None, tr, cols), lambda i, k=k: (k, i, 0)))
            args.append(arr)
        else:
            in_specs.append(tile)
            args.append(p)
    if mult is not None:
        in_specs.append(tile)
        args.append(mult)
    shape = jax.ShapeDtypeStruct((rows, cols), F32)
    return _pc(
        body, name=name, grid=(rows // tr,), in_specs=in_specs, out_specs=[tile] * 4, out_shape=[shape] * 4,
        compiler_params=_params("parallel"),
    )(*args)


def _rows128(a):
    flat = a.reshape(-1)
    pad = (-flat.shape[0]) % LANES
    if pad:
        flat = jnp.concatenate([flat, jnp.zeros((pad,), flat.dtype)])
    return flat.reshape(-1, LANES)


def _pad_rows(a, mult=8):
    pad = (-a.shape[0]) % mult
    if pad:
        a = jnp.concatenate([a, jnp.zeros((pad,) + a.shape[1:], a.dtype)], axis=0)
    return a


def kernel(x, c, ctx, c_ctx, norm_g, ada_w, ada_b, pool_w_in, pool_w_grp, pool_scale, pool_w_out, na_w_in, na_rpb, na_w_out, conv_w_in, conv_dw, conv_db, conv_w_out, final_g, loss_target, m_c_ctx, m_norm_g, m_ada_w, m_ada_b, m_pool_w_in, m_pool_w_grp, m_pool_scale, m_pool_w_out, m_na_w_in, m_na_rpb, m_na_w_out, m_conv_w_in, m_conv_dw, m_conv_db, m_conv_w_out, m_final_g, v_c_ctx, v_norm_g, v_ada_w, v_ada_b, v_pool_w_in, v_pool_w_grp, v_pool_scale, v_pool_w_out, v_na_w_in, v_na_rpb, v_na_w_out, v_conv_w_in, v_conv_dw, v_conv_db, v_conv_w_out, v_final_g):
    xi, yi, ci = _my_place()
    me = 4 * xi + 2 * yi + ci
    seq, d = x.shape[1], x.shape[2]
    n_ctx = ctx.shape[1]
    t_all = seq + n_ctx
    width = d
    depth = norm_g.shape[0]
    nb = ada_w.shape[2]
    shard = width // N_DEV
    d_rows = d // LANES
    tr = math.gcd(math.gcd(seq, n_ctx), 256)
    x_tiles = seq // tr

    n_pool = pool_scale.shape[0]
    n_grp = pool_w_grp.shape[1]
    grp = width // n_grp
    layer_weights = [[pool_w_in[0], pool_w_grp[0], pool_w_out[0]], [na_w_in[0], na_w_out[0]],
                     [conv_w_in[0], conv_w_out[0]], [pool_w_in[1], pool_w_grp[1], pool_w_out[1]]]
    in_flight, token = [], jnp.zeros((8, LANES), F32)
    for i, ws in enumerate(layer_weights):
        state = _exchange_start([w.astype(BF16) for w in ws], False, token, f"weights_start{i}")
        token = state[-1]
        in_flight.append(state)

    def landed_weight(i, t, after):
        return _exchange_wait(in_flight[i], False, after, f"weights_wait{i}_{t}", which=[t])[0]

    def as_in(w):
        return w[:, None]

    def as_grp(w):
        return w.transpose(1, 0, 2, 3).reshape(n_grp, grp, grp)

    def as_out(w):
        return w.reshape(width, d)

    small_in =_pad_rows(jnp.concatenate([_rows128(c), pool_scale, conv_dw[0], conv_db], axis=0))
    got = _gather_small(small_in, "gather_inputs")
    r0 = d_rows
    c_all = got[:, :r0].reshape(N_DEV, d)
    n_pool = pool_scale.shape[0]
    scale_full = got[:, r0:r0 + n_pool].transpose(1, 0, 2).reshape(n_pool, width)
    r1 = r0 + n_pool
    taps_full = _pad_rows(got[:, r1:r1 + 3].transpose(1, 0, 2).reshape(3, width))
    bias_full = got[:, r1 + 3:r1 + 4].transpose(1, 0, 2).reshape(1, width)

    cond = jnp.concatenate([c_all, c_ctx[None], jnp.zeros((7, d), F32)], axis=0)
    bias_mine = lax.dynamic_slice(ada_b, (0, me * nb), (depth, nb))
    mod_mine = _mod_fwd(cond, ada_w, bias_mine)
    mod_all = _gather_small(mod_mine.reshape(-1, LANES), "gather_mod")
    mod_all = mod_all.reshape(N_DEV, depth, 16, nb).transpose(1, 2, 0, 3).reshape(depth, 16, 3 * d)
    mod_x = lax.dynamic_index_in_dim(mod_all, me, 1, keepdims=False).reshape(depth, 3, d)
    mod_c = mod_all[:, 8].reshape(depth, 3, d)
    pad5 = jnp.zeros((depth, 5, d), F32)
    mod_x = jnp.concatenate([mod_x, pad5], axis=1)
    mod_c = jnp.concatenate([mod_c, pad5], axis=1)
    mods = [jnp.stack([mod_x[i], mod_c[i]]) if i < 2 else mod_x[i][None] for i in range(depth)]

    both = [(0, seq), (seq, n_ctx)]
    latent = [(0, seq)]

    def grp_slots(g):
        return g.reshape(n_grp, N_DEV, grp // N_DEV, grp).transpose(1, 0, 2, 3).reshape(N_DEV, -1, grp).astype(BF16)

    def send_grads(i, grads):
        return _exchange_start(grads, True, jnp.zeros((8, LANES), F32), f"grads_start{i}")

    xs0 = jnp.concatenate([x[0], ctx[0]], axis=0)
    h0 = _norm_fwd(xs0, norm_g[0:1] + token[0, 0], mods[0], tr, x_tiles, "norm_fwd0")
    pool_in_w0 = as_in(landed_weight(0, 0, h0))
    pre0 = _proj_in(h0, pool_in_w0, 0, width, "proj_in0")
    pool_grp_w0 = as_grp(landed_weight(0, 1, pre0))
    z0, diff0 = _pool_fwd(pre0, pool_grp_w0, scale_full[0:1], both, "pool_fwd0")
    pool_out_w0 = as_out(landed_weight(0, 2, z0))
    yx0, xs1, h1 = _proj_out(z0, pool_out_w0, xs0, mods[0], tr, x_tiles, "proj_out0", nxt=(norm_g[1:2], mods[1]))

    na_in_w = as_in(landed_weight(1, 0, h1))
    per_part = width // na_w_in.shape[2]
    qkv1 = _proj_in(h1, na_in_w, 0, width, "proj_in1_qkv", blocks=(0, 3 * per_part), dtype=BF16)
    gpre1 = _proj_in(h1, na_in_w, 0, width, "proj_in1_gate", blocks=(3 * per_part, per_part))
    rpb_rows = jnp.pad(na_rpb[0], ((0, 0), (0, 2 * WIN_ROWS - na_rpb.shape[2]), (0, LANES - na_rpb.shape[3])))
    z1, o1 = _attn_fwd(qkv1, gpre1, rpb_rows, seq)
    na_out_w = as_out(landed_weight(1, 1, z1))
    yx1, x2, h2 = _proj_out(z1, na_out_w, xs1, mods[1], tr, x_tiles, "proj_out1", nxt=(norm_g[2:3], mods[2]))

    conv_in_w = as_in(landed_weight(2, 0, h2))
    pre2 = _proj_in(h2, conv_in_w, 0, width, "proj_in2")
    z2 = _conv_fwd(pre2, taps_full, bias_full, "conv_fwd")
    conv_out_w = as_out(landed_weight(2, 1, z2))
    yx2, x3, h3 = _proj_out(z2, conv_out_w, x2, mods[2], tr, x_tiles, "proj_out2", nxt=(norm_g[3:4], mods[3]))

    pool_in_w3 = as_in(landed_weight(3, 0, h3))
    pre3 = _proj_in(h3, pool_in_w3, 0, width, "proj_in3")
    pool_grp_w3 = as_grp(landed_weight(3, 1, pre3))
    z3, diff3 = _pool_fwd(pre3, pool_grp_w3, scale_full[1:2], latent, "pool_fwd3")
    pool_out_w3 = as_out(landed_weight(3, 2, z3))
    yx3, x4 = _proj_out(z3, pool_out_w3, x3, mods[3], tr, x_tiles, "proj_out3")

    loss_part, dx4, d_final, dyx3, gate3 = _loss_head(x4, loss_target[0], final_g[None], yx3, mods[3], tr)
    loss = lax.psum(loss_part[0, 0], ("x", "y", "c"))

    dz3 = _proj_out_dz(dyx3, pool_out_w3, "proj_out_dz3")
    g_pool_out1 = _grad_w_out(z3, dyx3, "grad_w_out3")
    dpre3, g_grp1, g_scale1 = _pool_bwd(dz3, diff3, pre3, pool_grp_w3, scale_full[1:2], latent, "pool_bwd3")
    dh3 = _proj_in_dh(dpre3, pool_in_w3, 0, "proj_in_dh3")
    g_pool_in1 = _grad_w_in(h3, dpre3, pool_w_in.shape[2], "grad_w_in3")
    sent3 = send_grads(3, [g_pool_in1, grp_slots(g_grp1), g_pool_out1.reshape(N_DEV, shard, d)])
    dx3, norm3, dyx2, gate2 = _norm_bwd(x3, dh3, dx4, norm_g[3:4] + sent3[-1][0, 0], mods[3], tr, x_tiles, "norm_bwd3",
                                        below=(yx2, mods[2]))

    dz2 = _proj_out_dz(dyx2, conv_out_w, "proj_out_dz2")
    g_conv_out = _grad_w_out(z2, dyx2, "grad_w_out2")
    dpre2, g_taps, g_cbias = _conv_bwd(dz2, pre2, taps_full, bias_full, "conv_bwd")
    dh2 = _proj_in_dh(dpre2, conv_in_w, 0, "proj_in_dh2")
    g_conv_in = _grad_w_in(h2, dpre2, conv_w_in.shape[2], "grad_w_in2")
    sent2 = send_grads(2, [g_conv_in, g_conv_out.reshape(N_DEV, shard, d)])
    dx2, norm2, dyx1, gate1 = _norm_bwd(x2, dh2, dx3, norm_g[2:3] + sent2[-1][0, 0], mods[2], tr, x_tiles, "norm_bwd2",
                                        below=(yx1, mods[1][:1]))

    dz1 = _proj_out_dz(dyx1, na_out_w, "proj_out_dz1")
    g_na_out = _grad_w_out(z1, dyx1, "grad_w_out1")
    dpre1, g_rpb = _attn_bwd(qkv1, gpre1, o1, dz1, rpb_rows, seq)
    g_rpb = g_rpb[:, :na_rpb.shape[2], :na_rpb.shape[3]]
    dh1 = _proj_in_dh(dpre1, na_in_w, 0, "proj_in_dh1")
    g_na_in = _grad_w_in(h1, dpre1, na_w_in.shape[2], "grad_w_in1")
    sent1 = send_grads(1, [g_na_in, g_na_out.reshape(N_DEV, shard, d)])
    dxs1, norm1, dyx0, gate0 = _norm_bwd(xs1, dh1, dx2, norm_g[1:2] + sent1[-1][0, 0], mods[1], tr, x_tiles, "norm_bwd1",
                                         res_tiles=x_tiles, below=(yx0, mods[0]))

    dz0 = _proj_out_dz(dyx0, pool_out_w0, "proj_out_dz0")
    g_pool_out0 = _grad_w_out(z0, dyx0, "grad_w_out0")
    dpre0, g_grp0, g_scale0 = _pool_bwd(dz0, diff0, pre0, pool_grp_w0, scale_full[0:1], both, "pool_bwd0")
    dh0 = _proj_in_dh(dpre0, pool_in_w0, 0, "proj_in_dh0")
    g_pool_in0 = _grad_w_in(h0, dpre0, pool_w_in.shape[2], "grad_w_in0")
    sent0 = send_grads(0, [g_pool_in0, grp_slots(g_grp0), g_pool_out0.reshape(N_DEV, shard, d)])
    dx0, norm0 = _norm_bwd(xs0, dh0, dxs1, norm_g[0:1] + sent0[-1][0, 0], mods[0], tr, x_tiles, "norm_bwd0",
                           out_tiles=x_tiles)
    grad_x = dx0[None]

    norms, gates = [norm0, norm1, norm2, norm3], [gate0, gate1, gate2, gate3]
    zero_d = jnp.zeros((d,), F32)
    dm_rows = [jnp.concatenate([norms[i][0, 0], norms[i][0, 1], gates[i][0, 0]]) for i in range(depth)]
    dm_rows.append(jnp.concatenate([norm0[1, 0], norm0[1, 1], gate0[1, 0]]))
    dm_rows.append(jnp.concatenate([norm1[1, 0], norm1[1, 1], zero_d]))
    dm_local = jnp.stack(dm_rows + [jnp.zeros((3 * d,), F32)] * 2)
    g_norm_part = jnp.stack([norm0[0, 2] + norm0[1, 2], norm1[0, 2] + norm1[1, 2], norm2[0, 2], norm3[0, 2]])
    g_scale_part = jnp.concatenate([g_scale0, g_scale1], axis=0)
    pieces = [_rows128(dm_local), _rows128(g_norm_part), _rows128(d_final[0]), _pad_rows(_rows128(g_rpb)),
              _rows128(g_scale_part), _rows128(g_taps[:3]), _rows128(g_cbias[0])]
    small_sent = _exchange_start([jnp.concatenate(pieces, axis=0)], False, jnp.zeros((8, LANES), F32), "small_grads_start")
    marks = np.cumsum([0] + [p.shape[0] for p in pieces])

    def big(parts, w, m, v, name):
        shape = w.shape
        view = (-1, shape[-1])
        parts = [(parts.reshape((N_DEV,) + w.reshape(view).shape), k) for k in range(N_DEV)]
        return [r.reshape(shape) for r in _adamw(w.reshape(view), m.reshape(view), v.reshape(view), parts, name)]

    in3, grp3, out3 = _exchange_wait(sent3, True, small_sent[-1], "grads_wait3")
    in2, out2 = _exchange_wait(sent2, True, small_sent[-1], "grads_wait2")
    in1, out1 = _exchange_wait(sent1, True, small_sent[-1], "grads_wait1")
    res = {}
    res["na_w_in"] = [r[None] for r in big(in1, na_w_in[0], m_na_w_in[0], v_na_w_in[0], "adamw_na_in")]
    res["na_w_out"] = [r[None] for r in big(out1, na_w_out[0], m_na_w_out[0], v_na_w_out[0], "adamw_na_out")]
    res["conv_w_in"] = [r[None] for r in big(in2, conv_w_in[0], m_conv_w_in[0], v_conv_w_in[0], "adamw_conv_in")]
    res["conv_w_out"] = [r[None] for r in big(out2, conv_w_out[0], m_conv_w_out[0], v_conv_w_out[0], "adamw_conv_out")]
    pool_in1 = big(in3, pool_w_in[1], m_pool_w_in[1], v_pool_w_in[1], "adamw_pool_in1")
    pool_grp1 = big(grp3, pool_w_grp[1], m_pool_w_grp[1], v_pool_w_grp[1], "adamw_pool_grp1")
    pool_out1 = big(out3, pool_w_out[1], m_pool_w_out[1], v_pool_w_out[1], "adamw_pool_out1")

    small_out = _exchange_wait(small_sent, False, pool_out1[0], "small_grads_wait")[0]

    def piece(k):
        return small_out[:, marks[k]:marks[k + 1]]

    dm_all = piece(0).reshape(N_DEV, 8, 3 * d).transpose(1, 0, 2)
    dm_mine = lax.dynamic_slice(dm_all, (0, 0, me * nb), (8, N_DEV, nb))
    g_ada_w, g_ada_b, cctx_part, dsilu_cond = _mod_bwd(cond, ada_w, dm_all, dm_mine)
    cctx_all = _gather_small(_rows128(cctx_part[0]), "gather_cctx")

    def my_shard(a, n):
        a = a.reshape(N_DEV, n, N_DEV, shard)
        return lax.dynamic_index_in_dim(a, me, 2, keepdims=False)

    zeros7 = lambda r: jnp.zeros((N_DEV - 1, r, LANES), F32)
    ada_b_rows = _rows128(g_ada_b)
    small_parts = jnp.concatenate([
        cctx_all, piece(1), jnp.concatenate([ada_b_rows[None], zeros7(ada_b_rows.shape[0])], axis=0), piece(2), piece(3),
        my_shard(piece(4), n_pool), my_shard(piece(5), 3), my_shard(piece(6), 1)], axis=1)
    n_small = small_parts.shape[1]
    small_parts = jnp.concatenate([small_parts, jnp.zeros((N_DEV, (-n_small) % 8, LANES), F32)], axis=1)

    def pack(c_ctx_, norm_g_, ada_b_, final_g_, rpb_, scale_, taps_, cbias_):
        rows = [_rows128(c_ctx_), _rows128(norm_g_), _rows128(ada_b_), _rows128(final_g_), _pad_rows(_rows128(rpb_)),
                scale_, taps_[0], cbias_]
        return _pad_rows(jnp.concatenate(rows, axis=0)), np.cumsum([0] + [r.shape[0] for r in rows])

    w_small, smarks = pack(c_ctx, norm_g, ada_b, final_g, na_rpb, pool_scale, conv_dw, conv_db)
    m_small, _ = pack(m_c_ctx, m_norm_g, m_ada_b, m_final_g, m_na_rpb, m_pool_scale, m_conv_dw, m_conv_db)
    v_small, _ = pack(v_c_ctx, v_norm_g, v_ada_b, v_final_g, v_na_rpb, v_pool_scale, v_conv_dw, v_conv_db)
    mult = jnp.concatenate([_rows128(dsilu_cond[8]), jnp.ones((w_small.shape[0] - d_rows, LANES), F32)], axis=0)
    small_res = _adamw(w_small, m_small, v_small, [(small_parts, k) for k in range(N_DEV)], "adamw_small", mult=mult)

    def unpack(k, like):
        out = []
        for r in small_res:
            flat = r[smarks[k]:smarks[k + 1]].reshape(-1)
            out.append(flat[:like.size].reshape(like.shape))
        return out

    res.update({"c_ctx": unpack(0, c_ctx), "norm_g": unpack(1, norm_g), "ada_b": unpack(2, ada_b),
                "final_g": unpack(3, final_g), "na_rpb": unpack(4, na_rpb), "pool_scale": unpack(5, pool_scale),
                "conv_dw": unpack(6, conv_dw), "conv_db": unpack(7, conv_db)})

    res["ada_w"] = [r.reshape(ada_w.shape) for r in _adamw(
        ada_w.reshape(-1, nb), m_ada_w.reshape(-1, nb), v_ada_w.reshape(-1, nb), [g_ada_w.reshape(-1, nb)], "adamw_ada_w")]

    in0, grp0, out0 = _exchange_wait(sent0, True, small_res[0], "grads_wait0")
    pool_in0 = big(in0, pool_w_in[0], m_pool_w_in[0], v_pool_w_in[0], "adamw_pool_in0")
    pool_grp0 = big(grp0, pool_w_grp[0], m_pool_w_grp[0], v_pool_w_grp[0], "adamw_pool_grp0")
    pool_out0 = big(out0, pool_w_out[0], m_pool_w_out[0], v_pool_w_out[0], "adamw_pool_out0")
    res["pool_w_in"] = [jnp.stack([p, q]) for p, q in zip(pool_in0, pool_in1)]
    res["pool_w_grp"] = [jnp.stack([p, q]) for p, q in zip(pool_grp0, pool_grp1)]
    res["pool_w_out"] = [jnp.stack([p, q]) for p, q in zip(pool_out0, pool_out1)]

    order = ["c_ctx", "norm_g", "ada_w", "ada_b", "pool_w_in", "pool_w_grp", "pool_scale", "pool_w_out", "na_w_in",
             "na_rpb", "na_w_out", "conv_w_in", "conv_dw", "conv_db", "conv_w_out", "final_g"]
    outs = [loss, grad_x]
    for j in range(4):
        outs += [res[n][j] for n in order]
    return tuple(outs)
```

```python
import functools
import math

import numpy as np
import jax
import jax.numpy as jnp
from jax import lax
from jax.experimental import pallas as pl
from jax.experimental.pallas import tpu as pltpu

F32 = jnp.float32
BF16 = jnp.bfloat16
N_DEV = 8
LANES = 128
RMS_EPS = 1e-6
GRID_W = 64
WIN_ROWS = 8
WIN_COLS = 16
HEAD_DIM = 64
POOL_WINDOWS = (2, 4, 8, 16)
HALO = 8
CHUNK = 128
MASKED = -1e30
ADAM_LR = 0.001
ADAM_B1 = 0.9
ADAM_B2 = 0.999
ADAM_EPS = 1e-08
ADAM_WD = 0.01
ADAM_STEP = 10
VMEM_LIMIT = 56 * 1024 * 1024
MESH = pl.DeviceIdType.MESH
ANY = pl.BlockSpec(memory_space=pl.ANY)
HBM = pl.BlockSpec(memory_space=pltpu.HBM)
SEM = pl.BlockSpec(memory_space=pltpu.SEMAPHORE)
EFFECT = pltpu.SideEffectType.DATAFLOW_SIDE_EFFECTING


def _pc(body, *, name, **kw):
    return pl.pallas_call(body, name=name, **kw)


def _params(*sem):
    return pltpu.CompilerParams(dimension_semantics=sem if sem else None, vmem_limit_bytes=VMEM_LIMIT)


def _dot(a, b, ca=1, cb=0, precision=None):
    return lax.dot_general(a, b, (((ca,), (cb,)), ((), ())), preferred_element_type=F32, precision=precision)


def _tile(n, pref, unit=LANES):
    best = None
    for t in range(unit, min(n, pref) + 1, unit):
        if n % t == 0:
            best = t
    return best if best is not None else n


def _sigmoid(x):
    return 1.0 / (1.0 + jnp.exp(-x))


def _silu(x):
    return x * _sigmoid(x)


def _dsilu(x):
    s = _sigmoid(x)
    return s * (1.0 + x * (1.0 - s))


def _my_place():
    return lax.axis_index("x"), lax.axis_index("y"), lax.axis_index("c")


def _flip(v, f):
    return 1 - v if f else v


def _gather_small(block, name):
    rows, cols = block.shape

    def body(x_ref, out_ref, send_sems, recv_sems):
        x, y, c = _my_place()
        me = 4 * x + 2 * y + c
        out_ref[me] = x_ref[...]
        copies = []
        for k in range(1, N_DEV):
            peer = (_flip(x, k & 4), _flip(y, k & 2), _flip(c, k & 1))
            cp = pltpu.make_async_remote_copy(
                src_ref=x_ref, dst_ref=out_ref.at[me], send_sem=send_sems.at[k - 1], recv_sem=recv_sems.at[k - 1],
                device_id=peer, device_id_type=MESH)
            cp.start()
            copies.append(cp)
        for cp in copies:
            cp.wait()

    return _pc(
        body, name=name,
        out_shape=jax.ShapeDtypeStruct((N_DEV, rows, cols), block.dtype),
        in_specs=[pl.BlockSpec(memory_space=pltpu.VMEM)],
        out_specs=pl.BlockSpec(memory_space=pltpu.VMEM),
        scratch_shapes=[pltpu.SemaphoreType.DMA((N_DEV - 1,)), pltpu.SemaphoreType.DMA((N_DEV - 1,))],
    )(block)


def _exchange_copies(srcs, lands, send_sems, recv_sems, per_dest):
    x, y, c = _my_place()
    me = 4 * x + 2 * y + c
    copies = []
    for t, (src, land) in enumerate(zip(srcs, lands)):
        for k in range(N_DEV):
            peer = (_flip(x, k & 4), _flip(y, k & 2), _flip(c, k & 1))
            dest = 4 * peer[0] + 2 * peer[1] + peer[2]
            s = t * N_DEV + k
            copies.append(pltpu.make_async_remote_copy(
                src_ref=src.at[dest] if per_dest else src, dst_ref=land.at[me],
                send_sem=send_sems[s], recv_sem=recv_sems[s], device_id=peer, device_id_type=MESH))
    return copies


def _exchange_start(srcs, per_dest, after, name):
    nt = len(srcs)
    ns = nt * N_DEV
    lands = [lax.empty((N_DEV,) + (s.shape[1:] if per_dest else s.shape), s.dtype) for s in srcs]

    def body(*refs):
        ins, outs = refs[:2 * nt + 1], refs[2 * nt + 1:]
        for cp in _exchange_copies(ins[:nt], ins[nt:2 * nt], outs[:ns], outs[ns:2 * ns], per_dest):
            cp.start()
        outs[-1][...] = jnp.zeros_like(outs[-1])

    hbm = [pltpu.with_memory_space_constraint(a, pltpu.HBM) for a in list(srcs) + lands]
    res = _pc(
        body, name=name,
        out_shape=(*[pltpu.SemaphoreType.DMA(())] * (2 * ns), *[pltpu.HBM(a.shape, a.dtype) for a in hbm],
                   jax.ShapeDtypeStruct((8, LANES), F32)),
        in_specs=[HBM] * (2 * nt) + [ANY],
        out_specs=(*[SEM] * (2 * ns), *[HBM] * (2 * nt), pl.BlockSpec(memory_space=pltpu.VMEM)),
        input_output_aliases={i: 2 * ns + i for i in range(2 * nt)},
        compiler_params=pltpu.CompilerParams(has_side_effects=EFFECT),
    )(*hbm, after)
    sems, rest = res[:2 * ns], res[2 * ns:]
    return list(sems[:ns]), list(sems[ns:]), list(rest[:nt]), list(rest[nt:2 * nt]), rest[-1]


def _exchange_wait(state, per_dest, after, name, which=None):
    send_sems, recv_sems, srcs, lands, _ = state
    which = list(range(len(srcs))) if which is None else which
    pick = [t * N_DEV + k for t in which for k in range(N_DEV)]
    send_sems, recv_sems = [send_sems[s] for s in pick], [recv_sems[s] for s in pick]
    srcs, lands = [srcs[t] for t in which], [lands[t] for t in which]
    nt = len(srcs)
    ns = len(send_sems)

    def body(*refs):
        sems = refs[2 * nt:2 * nt + 2 * ns]
        for cp in _exchange_copies(refs[:nt], refs[nt:2 * nt], sems[:ns], sems[ns:], per_dest):
            cp.wait_send()
            cp.wait_recv()

    thru = list(srcs) + list(lands)
    res = _pc(
        body, name=name,
        out_shape=tuple(pltpu.HBM(a.shape, a.dtype) for a in thru),
        in_specs=[HBM] * (2 * nt) + [SEM] * (2 * ns) + [ANY],
        out_specs=tuple([HBM] * (2 * nt)),
        input_output_aliases={i: i for i in range(2 * nt)},
        compiler_params=pltpu.CompilerParams(has_side_effects=EFFECT),
    )(*thru, *send_sems, *recv_sems, after)
    return list(res[nt:])


def _mod_fwd(cond, ada_w, bias):
    depth, d, nb = ada_w.shape

    def body(c_ref, w_ref, b_ref, o_ref):
        s = _silu(c_ref[...]).astype(BF16)
        o_ref[...] = _dot(s, w_ref[...].astype(BF16)) + b_ref[...]

    return _pc(
        body, name="mod_fwd", grid=(depth,),
        in_specs=[pl.BlockSpec((16, d), lambda i: (0, 0)), pl.BlockSpec((None, d, nb), lambda i: (i, 0, 0)),
                  pl.BlockSpec((None, 1, nb), lambda i: (i, 0, 0))],
        out_specs=pl.BlockSpec((None, 16, nb), lambda i: (i, 0, 0)),
        out_shape=jax.ShapeDtypeStruct((depth, 16, nb), F32),
        compiler_params=_params("parallel"),
    )(cond, ada_w, bias.reshape(depth, 1, nb))


def _mod_bwd(cond, ada_w, dm_all, dm_mine):
    depth, d, nb = ada_w.shape
    d3 = dm_all.shape[-1]

    def body(c_ref, w_ref, all_ref, call_ref, mine_ref, cmine_ref, gw_ref, gb_ref, part_ref, ds_ref):
        i = pl.program_id(0)
        cond_v = c_ref[...]
        s = _silu(cond_v).astype(BF16)
        has_ctx = jnp.where(i < 2, 1.0, 0.0)
        tot_all = jnp.sum(call_ref[...], axis=0, keepdims=True) * has_ctx
        tot_mine = jnp.broadcast_to(jnp.sum(cmine_ref[...], axis=0, keepdims=True) * has_ctx, (8, nb)).astype(BF16)
        gb_ref[...] = jnp.sum(all_ref[...], axis=0, keepdims=True) + tot_all
        gw_ref[...] = _dot(s[0:8], mine_ref[...].astype(BF16), 0, 0) + _dot(s[8:16], tot_mine, 0, 0)
        part = _dot(tot_mine, w_ref[...].astype(BF16), 1, 1)

        @pl.when(i == 0)
        def _():
            part_ref[...] = jnp.zeros_like(part_ref)
            ds_ref[...] = _dsilu(cond_v)

        part_ref[...] += part

    def rows(width, which):
        return pl.BlockSpec((None, N_DEV, width), which)

    layer = lambda i: (i, 0, 0)
    ctx_layer = lambda i: (jnp.minimum(i, 1) + 4, 0, 0)
    return _pc(
        body, name="mod_bwd", grid=(depth,),
        in_specs=[pl.BlockSpec((16, d), lambda i: (0, 0)), pl.BlockSpec((None, d, nb), layer),
                  rows(d3, layer), rows(d3, ctx_layer), rows(nb, layer), rows(nb, ctx_layer)],
        out_specs=[pl.BlockSpec((None, d, nb), layer), pl.BlockSpec((None, 1, d3), layer),
                   pl.BlockSpec((8, d), lambda i: (0, 0)), pl.BlockSpec((16, d), lambda i: (0, 0))],
        out_shape=[jax.ShapeDtypeStruct((depth, d, nb), F32), jax.ShapeDtypeStruct((depth, 1, d3), F32),
                   jax.ShapeDtypeStruct((8, d), F32), jax.ShapeDtypeStruct((16, d), F32)],
        compiler_params=_params("arbitrary"),
    )(cond, ada_w, dm_all, dm_all, dm_mine, dm_mine)


def _norm_fwd(xs, g, mod, tr, seg_tiles, name):
    t, d = xs.shape

    def body(x_ref, g_ref, mod_ref, h_ref):
        x = x_ref[...]
        r = lax.rsqrt(jnp.mean(x * x, axis=-1, keepdims=True) + RMS_EPS)
        y = (x * r) * g_ref[...]
        h_ref[...] = (y * (1.0 + mod_ref[1:2, :]) + mod_ref[0:1, :]).astype(BF16)

    return _pc(
        body, name=name, grid=(t // tr,),
        in_specs=[pl.BlockSpec((tr, d), lambda i: (i, 0)), pl.BlockSpec((1, d), lambda i: (0, 0)),
                  pl.BlockSpec((None, 8, d), lambda i: (i // seg_tiles, 0, 0))],
        out_specs=pl.BlockSpec((tr, d), lambda i: (i, 0)),
        out_shape=jax.ShapeDtypeStruct((t, d), BF16),
        compiler_params=_params("parallel"),
    )(xs, g, mod)


def _resid_grad(dx, i, seg_tiles, yx_ref, gate_ref, dyx_ref, gsum_ref):
    dyx_ref[...] = (dx * gate_ref[2:3, :]).astype(BF16)

    @pl.when(i % seg_tiles == 0)
    def _():
        gsum_ref[...] = jnp.zeros_like(gsum_ref)

    gsum_ref[0:1, :] += jnp.sum(dx * yx_ref[...], axis=0, keepdims=True)


def _norm_bwd(xs, dh, dres, g, mod, tr, seg_tiles, name, res_tiles=None, out_tiles=None, below=None):
    t, d = xs.shape
    n_tiles = t // tr
    res_tiles = n_tiles if res_tiles is None else res_tiles
    out_tiles = n_tiles if out_tiles is None else out_tiles

    def body(x_ref, dh_ref, dres_ref, g_ref, mod_ref, *rest):
        i = pl.program_id(0)
        x = x_ref[...]
        r = lax.rsqrt(jnp.mean(x * x, axis=-1, keepdims=True) + RMS_EPS)
        xn = x * r
        dhv = dh_ref[...]
        gain = g_ref[...]
        one_scale = 1.0 + mod_ref[1:2, :]
        dxn = dhv * (gain * one_scale)
        dx = r * (dxn - xn * jnp.mean(dxn * xn, axis=-1, keepdims=True))
        if res_tiles == n_tiles:
            dx = dx + dres_ref[...]
        else:
            dx = dx + jnp.where(i < res_tiles, dres_ref[...], 0.0)
        if below is None:
            dx_ref, sum_ref = rest
        else:
            yx_ref, gate_ref, dx_ref, sum_ref, dyx_ref, gsum_ref = rest
            _resid_grad(dx, i, seg_tiles, yx_ref, gate_ref, dyx_ref, gsum_ref)
        if out_tiles == n_tiles:
            dx_ref[...] = dx
        else:
            @pl.when(i < out_tiles)
            def _():
                dx_ref[...] = dx

        @pl.when(i % seg_tiles == 0)
        def _():
            sum_ref[...] = jnp.zeros_like(sum_ref)

        sum_ref[0:1, :] += jnp.sum(dhv, axis=0, keepdims=True)
        sum_ref[1:2, :] += jnp.sum(dhv * (xn * gain), axis=0, keepdims=True)
        sum_ref[2:3, :] += jnp.sum(dhv * one_scale * xn, axis=0, keepdims=True)

    row = pl.BlockSpec((tr, d), lambda i: (i, 0))
    seg = pl.BlockSpec((None, 8, d), lambda i: (i // seg_tiles, 0, 0))
    in_specs = [row, row, pl.BlockSpec((tr, d), lambda i: (jnp.minimum(i, res_tiles - 1), 0)),
                pl.BlockSpec((1, d), lambda i: (0, 0)), seg]
    out_specs = [pl.BlockSpec((tr, d), lambda i: (jnp.minimum(i, out_tiles - 1), 0)), seg]
    out_shape = [jax.ShapeDtypeStruct((out_tiles * tr, d), F32), jax.ShapeDtypeStruct((mod.shape[0], 8, d), F32)]
    args = [xs, dh, dres, g, mod]
    if below is not None:
        in_specs += [row, seg]
        out_specs += [row, seg]
        out_shape += [jax.ShapeDtypeStruct((t, d), BF16), jax.ShapeDtypeStruct((below[1].shape[0], 8, d), F32)]
        args += list(below)
    return _pc(
        body, name=name, grid=(n_tiles,), in_specs=in_specs, out_specs=out_specs, out_shape=out_shape,
        compiler_params=_params("arbitrary"),
    )(*args)


def _loss_head(xs, target, g, yx, mod, tr):
    t, d = xs.shape

    def body(x_ref, t_ref, g_ref, yx_ref, gate_ref, loss_ref, dx_ref, dg_ref, dyx_ref, gsum_ref):
        i = pl.program_id(0)
        x = x_ref[...]
        r = lax.rsqrt(jnp.mean(x * x, axis=-1, keepdims=True) + RMS_EPS)
        xn = x * r
        gain = g_ref[...]
        err = xn * gain - t_ref[...]
        dy = err * (1.0 / d)
        dxn = dy * gain
        dx = r * (dxn - xn * jnp.mean(dxn * xn, axis=-1, keepdims=True))
        dx_ref[...] = dx
        _resid_grad(dx, i, t // tr, yx_ref, gate_ref, dyx_ref, gsum_ref)

        @pl.when(i == 0)
        def _():
            loss_ref[...] = jnp.zeros_like(loss_ref)
            dg_ref[...] = jnp.zeros_like(dg_ref)

        loss_ref[...] += 0.5 * jnp.sum(jnp.mean(err * err, axis=-1, keepdims=True))
        dg_ref[0:1, :] += jnp.sum(dy * xn, axis=0, keepdims=True)

    row = pl.BlockSpec((tr, d), lambda i: (i, 0))
    seg = pl.BlockSpec((None, 8, d), lambda i: (0, 0, 0))
    return _pc(
        body, name="loss_head", grid=(t // tr,),
        in_specs=[row, row, pl.BlockSpec((1, d), lambda i: (0, 0)), row, seg],
        out_specs=[pl.BlockSpec((8, LANES), lambda i: (0, 0)), row, pl.BlockSpec((8, d), lambda i: (0, 0)), row, seg],
        out_shape=[jax.ShapeDtypeStruct((8, LANES), F32), jax.ShapeDtypeStruct((t, d), F32),
                   jax.ShapeDtypeStruct((8, d), F32), jax.ShapeDtypeStruct((t, d), BF16),
                   jax.ShapeDtypeStruct((1, 8, d), F32)],
        compiler_params=_params("arbitrary"),
    )(xs, target, g, yx, mod)


def _proj_in(h, w, layer, width, name, blocks=None, dtype=F32):
    t, d = h.shape
    n8 = w.shape[-1]
    first, count = blocks if blocks is not None else (0, N_DEV)
    per_part = width // n8
    tm = _tile(t, 1152)

    def body(a_ref, b_ref, o_ref):
        o_ref[...] = _dot(a_ref[...], b_ref[...]).astype(dtype)

    return _pc(
        body, name=name, grid=(t // tm, count),
        in_specs=[pl.BlockSpec((tm, d), lambda i, j: (i, 0)),
                  pl.BlockSpec((None, None, d, n8), lambda i, j: (first + j, layer, 0, 0))],
        out_specs=pl.BlockSpec((None, tm, n8), lambda i, j: (j // per_part, i, j % per_part)),
        out_shape=jax.ShapeDtypeStruct((count // per_part, t, width), dtype),
        compiler_params=_params("parallel", "parallel"),
    )(h, w)


def _proj_out(z, w, res, mod, tm, seg_tiles, name, nxt=None):
    t, k = z.shape
    d = w.shape[1]

    def body(z_ref, w_ref, res_ref, mod_ref, *rest):
        yx = _dot(z_ref[...], w_ref[...])
        x = res_ref[...] + mod_ref[2:3, :] * yx
        if nxt is None:
            yx_ref, x_ref = rest
        else:
            g_ref, nmod_ref, yx_ref, x_ref, h_ref = rest
            r = lax.rsqrt(jnp.mean(x * x, axis=-1, keepdims=True) + RMS_EPS)
            h_ref[...] = (((x * r) * g_ref[...]) * (1.0 + nmod_ref[1:2, :]) + nmod_ref[0:1, :]).astype(BF16)
        yx_ref[...] = yx
        x_ref[...] = x

    tile = pl.BlockSpec((tm, d), lambda i: (i, 0))
    seg = pl.BlockSpec((None, 8, d), lambda i: (i // seg_tiles, 0, 0))
    in_specs = [pl.BlockSpec((tm, k), lambda i: (i, 0)), pl.BlockSpec((k, d), lambda i: (0, 0)), tile, seg]
    out_specs = [tile, tile]
    out_shape = [jax.ShapeDtypeStruct((t, d), F32), jax.ShapeDtypeStruct((t, d), F32)]
    args = [z, w, res, mod]
    if nxt is not None:
        in_specs += [pl.BlockSpec((1, d), lambda i: (0, 0)), seg]
        out_specs.append(tile)
        out_shape.append(jax.ShapeDtypeStruct((t, d), BF16))
        args += list(nxt)
    return _pc(
        body, name=name, grid=(t // tm,), in_specs=in_specs, out_specs=out_specs, out_shape=out_shape,
        compiler_params=_params("parallel"),
    )(*args)


def _proj_out_dz(dyx, w, name):
    t, d = dyx.shape
    width = w.shape[0]
    tm, tn = _tile(t, 1024), _tile(width, 512)

    def body(a_ref, w_ref, o_ref):
        o_ref[...] = _dot(a_ref[...], w_ref[...], 1, 1)

    return _pc(
        body, name=name, grid=(t // tm, width // tn),
        in_specs=[pl.BlockSpec((tm, d), lambda i, j: (i, 0)), pl.BlockSpec((tn, d), lambda i, j: (j, 0))],
        out_specs=pl.BlockSpec((tm, tn), lambda i, j: (i, j)),
        out_shape=jax.ShapeDtypeStruct((t, width), F32),
        compiler_params=_params("parallel", "parallel"),
    )(dyx, w)


def _proj_in_dh(dpre, w, layer, name):
    parts, t, width = dpre.shape
    d, n8 = w.shape[-2:]
    per_part = width // n8
    tm, tn = _tile(t, 1152), _tile(d, 512)

    def body(a_ref, w_ref, o_ref):
        part = _dot(a_ref[:, 0:n8], w_ref[0], 1, 1)
        for s in range(1, per_part):
            part += _dot(a_ref[:, s * n8:(s + 1) * n8], w_ref[s], 1, 1)

        @pl.when(pl.program_id(2) == 0)
        def _():
            o_ref[...] = part

        @pl.when(pl.program_id(2) != 0)
        def _():
            o_ref[...] += part

    return _pc(
        body, name=name, grid=(t // tm, d // tn, parts),
        in_specs=[pl.BlockSpec((None, tm, width), lambda i, j, k: (k, i, 0)),
                  pl.BlockSpec((per_part, None, tn, n8), lambda i, j, k: (k, layer, j, 0))],
        out_specs=pl.BlockSpec((tm, tn), lambda i, j, k: (i, j)),
        out_shape=jax.ShapeDtypeStruct((t, d), F32),
        compiler_params=_params("parallel", "parallel", "arbitrary"),
    )(dpre, w)


def _transposed(a_ref):
    return a_ref[...].T


def _grad_w_in(h, dpre, n8, name):
    t, d = h.shape
    parts, _, width = dpre.shape
    per_part = width // n8
    tm, tk = _tile(d, 512), _tile(t, 1152)
    nk = t // tk

    def body(a_ref, b_ref, o_ref, acc_ref):
        k = pl.program_id(1)

        @pl.when(k == 0)
        def _():
            acc_ref[...] = jnp.zeros_like(acc_ref)

        at = _transposed(a_ref)
        for p in range(parts):
            r = _dot(at, b_ref[p])
            for s in range(per_part):
                acc_ref[p * per_part + s] += r[:, s * n8:(s + 1) * n8]

        @pl.when(k == nk - 1)
        def _():
            o_ref[...] = acc_ref[...].astype(BF16)

    return _pc(
        body, name=name, grid=(d // tm, nk),
        in_specs=[pl.BlockSpec((tk, tm), lambda i, k: (k, i)), pl.BlockSpec((parts, tk, width), lambda i, k: (0, k, 0))],
        out_specs=pl.BlockSpec((parts * per_part, tm, n8), lambda i, k: (0, i, 0)),
        out_shape=jax.ShapeDtypeStruct((parts * per_part, d, n8), BF16),
        scratch_shapes=[pltpu.VMEM((parts * per_part, tm, n8), F32)],
        compiler_params=_params("parallel", "arbitrary"),
    )(h, dpre)


def _grad_w_out(z, dyx, name):
    width = z.shape[1]
    t, d = dyx.shape
    tm, tk = _tile(width, 512), _tile(t, 1152)
    nk = t // tk

    def body(a_ref, b_ref, o_ref, acc_ref):
        k = pl.program_id(1)

        @pl.when(k == 0)
        def _():
            acc_ref[...] = jnp.zeros_like(acc_ref)

        acc_ref[...] += _dot(_transposed(a_ref), b_ref[...])

        @pl.when(k == nk - 1)
        def _():
            o_ref[...] = acc_ref[...].astype(BF16)

    return _pc(
        body, name=name, grid=(width // tm, nk),
        in_specs=[pl.BlockSpec((tk, tm), lambda i, k: (k, i)), pl.BlockSpec((tk, d), lambda i, k: (k, 0))],
        out_specs=pl.BlockSpec((tm, d), lambda i, k: (i, 0)),
        out_shape=jax.ShapeDtypeStruct((width, d), BF16),
        scratch_shapes=[pltpu.VMEM((tm, d), F32)],
        compiler_params=_params("parallel", "arbitrary"),
    )(z, dyx)


def _shift(v, k):
    n = v.shape[0]
    return pltpu.roll(v, k % n, 0)


def _window_sum(v, win):
    s = v + _shift(v, 1)
    step = 1
    while 2 * step < win:
        s = _shift(s, step) + _shift(s, -step)
        step *= 2
    return s


def _window_count(base, seg_len, win, shape):
    t = base + lax.broadcasted_iota(jnp.int32, shape, 0)
    hi = jnp.minimum(t + win // 2, seg_len)
    lo = jnp.maximum(t - win // 2, 0)
    return (hi - lo).astype(F32)


def _pad_offsets(segs):
    return [HALO * (s + 1) + st for s, (st, _) in enumerate(segs)]


def _for_chunks(segs, fn):
    offs = _pad_offsets(segs)
    for s, (st, ln) in enumerate(segs):
        def step(ci, carry, s=s, st=st, ln=ln):
            fn(s, st, ln, offs[s], pl.multiple_of(ci * CHUNK, CHUNK))
            return carry
        lax.fori_loop(0, ln // CHUNK, step, 0)


def _pool_fwd(pre, w_grp, scale, segs, name):
    _, t, width = pre.shape
    grp = width // len(POOL_WINDOWS)
    padded = t + HALO * (len(segs) + 1)

    def group(win, pre_ref, w_ref, sc_ref, z_ref, diff_ref, pad_ref):
        pad_ref[...] = jnp.zeros_like(pad_ref)

        def fill(s, st, ln, off, b):
            pad_ref[pl.ds(off + b, CHUNK), :] = pre_ref[0, pl.ds(st + b, CHUNK), :]

        _for_chunks(segs, fill)

        def mix(s, st, ln, off, b):
            ext = pad_ref[pl.ds(off - HALO + b, CHUNK + 2 * HALO), :]
            total = _window_sum(ext, win)[HALO:HALO + CHUNK]
            u = pre_ref[0, pl.ds(st + b, CHUNK), :]
            diff = (total / _window_count(b, ln, win, u.shape) - u).astype(BF16)
            mixed = _dot(diff, w_ref[...])
            gate = _silu(pre_ref[1, pl.ds(st + b, CHUNK), :])
            z_ref[pl.ds(st + b, CHUNK), :] = (mixed * sc_ref[...] * gate).astype(BF16)
            diff_ref[pl.ds(st + b, CHUNK), :] = diff

        _for_chunks(segs, mix)

    def body(pre_ref, w_ref, sc_ref, z_ref, diff_ref, pad_ref):
        gi = pl.program_id(0)
        for widx, win in enumerate(POOL_WINDOWS):
            @pl.when(gi == widx)
            def _(win=win):
                group(win, pre_ref, w_ref, sc_ref, z_ref, diff_ref, pad_ref)

    col = pl.BlockSpec((t, grp), lambda g: (0, g))
    return _pc(
        body, name=name, grid=(len(POOL_WINDOWS),),
        in_specs=[pl.BlockSpec((2, t, grp), lambda g: (0, 0, g)), pl.BlockSpec((None, grp, grp), lambda g: (g, 0, 0)),
                  pl.BlockSpec((1, grp), lambda g: (0, g))],
        out_specs=[col, col],
        out_shape=[jax.ShapeDtypeStruct((t, width), BF16), jax.ShapeDtypeStruct((t, width), BF16)],
        scratch_shapes=[pltpu.VMEM((padded, grp), F32)],
        compiler_params=_params("parallel"),
    )(pre, w_grp, scale)


def _pool_bwd(dz, diff, pre, w_grp, scale, segs, name):
    _, t, width = pre.shape
    grp = width // len(POOL_WINDOWS)
    padded = t + HALO * (len(segs) + 1)

    def group(win, dz_ref, diff_ref, pre_ref, w_ref, sc_ref, dpre_ref, dw_ref, dsc_ref, pad_ref, dd_ref):
        pad_ref[...] = jnp.zeros_like(pad_ref)
        dw_ref[...] = jnp.zeros_like(dw_ref)
        dsc_ref[...] = jnp.zeros_like(dsc_ref)

        def first(s, st, ln, off, b):
            rows = pl.ds(st + b, CHUNK)
            diff_v = diff_ref[rows, :]
            mixed = _dot(diff_v, w_ref[...])
            g = pre_ref[1, rows, :]
            sg = _silu(g)
            dzv = dz_ref[rows, :]
            dmixed = (dzv * sc_ref[...] * sg).astype(BF16)
            dsc_ref[...] += jnp.sum(dzv * mixed * sg, axis=0, keepdims=True)
            dpre_ref[1, rows, :] = (dzv * mixed * sc_ref[...] * _dsilu(g)).astype(BF16)
            ddiff = _dot(dmixed, w_ref[...], 1, 1)
            dw_ref[...] += _dot(diff_v, dmixed, 0, 0)
            dd_ref[rows, :] = ddiff
            pad_ref[pl.ds(off + b, CHUNK), :] = ddiff / _window_count(b, ln, win, ddiff.shape)

        _for_chunks(segs, first)

        def second(s, st, ln, off, b):
            rows = pl.ds(st + b, CHUNK)
            ext = pad_ref[pl.ds(off - HALO + b, CHUNK + 2 * HALO), :]
            total = _shift(_window_sum(ext, win), -1)[HALO:HALO + CHUNK]
            dpre_ref[0, rows, :] = (total - dd_ref[rows, :]).astype(BF16)

        _for_chunks(segs, second)

    def body(dz_ref, diff_ref, pre_ref, w_ref, sc_ref, dpre_ref, dw_ref, dsc_ref, pad_ref, dd_ref):
        gi = pl.program_id(0)
        for widx, win in enumerate(POOL_WINDOWS):
            @pl.when(gi == widx)
            def _(win=win):
                group(win, dz_ref, diff_ref, pre_ref, w_ref, sc_ref, dpre_ref, dw_ref, dsc_ref, pad_ref, dd_ref)

    col = pl.BlockSpec((t, grp), lambda g: (0, g))
    both = pl.BlockSpec((2, t, grp), lambda g: (0, 0, g))
    wspec = pl.BlockSpec((None, grp, grp), lambda g: (g, 0, 0))
    sspec = pl.BlockSpec((1, grp), lambda g: (0, g))
    return _pc(
        body, name=name, grid=(len(POOL_WINDOWS),),
        in_specs=[col, col, both, wspec, sspec],
        out_specs=[both, wspec, sspec],
        out_shape=[jax.ShapeDtypeStruct((2, t, width), BF16), jax.ShapeDtypeStruct((len(POOL_WINDOWS), grp, grp), F32),
                   jax.ShapeDtypeStruct((1, width), F32)],
        scratch_shapes=[pltpu.VMEM((padded, grp), F32), pltpu.VMEM((t, grp), F32)],
        compiler_params=_params("parallel"),
    )(dz, diff, pre, w_grp, scale)


def _conv_fwd(pre, dw, db, name):
    _, t, width = pre.shape
    cb = LANES
    segs = [(0, t)]

    def body(pre_ref, dw_ref, db_ref, z_ref, pad_ref):
        pad_ref[...] = jnp.zeros_like(pad_ref)

        def fill(s, st, ln, off, b):
            rows = pl.ds(b, CHUNK)
            pad_ref[pl.ds(off + b, CHUNK), :] = pre_ref[1, rows, :] * pre_ref[2, rows, :]

        _for_chunks(segs, fill)

        def mix(s, st, ln, off, b):
            rows = pl.ds(b, CHUNK)
            ext = pad_ref[pl.ds(off - HALO + b, CHUNK + 2 * HALO), :]
            conv = (dw_ref[0:1, :] * _shift(ext, 1) + dw_ref[1:2, :] * ext + dw_ref[2:3, :] * _shift(ext, -1))
            conv = conv[HALO:HALO + CHUNK] + db_ref[...]
            y = pre_ref[0, rows, :] * conv
            z_ref[rows, :] = (y * _silu(pre_ref[3, rows, :])).astype(BF16)

        _for_chunks(segs, mix)

    return _pc(
        body, name=name, grid=(width // cb,),
        in_specs=[pl.BlockSpec((4, t, cb), lambda j: (0, 0, j)), pl.BlockSpec((8, cb), lambda j: (0, j)),
                  pl.BlockSpec((1, cb), lambda j: (0, j))],
        out_specs=pl.BlockSpec((t, cb), lambda j: (0, j)),
        out_shape=jax.ShapeDtypeStruct((t, width), BF16),
        scratch_shapes=[pltpu.VMEM((t + 2 * HALO, cb), F32)],
        compiler_params=_params("parallel"),
    )(pre, dw, db)


def _conv_bwd(dz, pre, dw, db, name):
    _, t, width = pre.shape
    cb = LANES
    segs = [(0, t)]

    def body(dz_ref, pre_ref, dw_ref, db_ref, dpre_ref, ddw_ref, ddb_ref, pad_a, pad_c):
        pad_a[...] = jnp.zeros_like(pad_a)
        pad_c[...] = jnp.zeros_like(pad_c)
        ddw_ref[...] = jnp.zeros_like(ddw_ref)
        ddb_ref[...] = jnp.zeros_like(ddb_ref)

        def fill(s, st, ln, off, b):
            rows = pl.ds(b, CHUNK)
            pad_a[pl.ds(off + b, CHUNK), :] = pre_ref[1, rows, :] * pre_ref[2, rows, :]

        _for_chunks(segs, fill)

        def first(s, st, ln, off, b):
            rows = pl.ds(b, CHUNK)
            ext = pad_a[pl.ds(off - HALO + b, CHUNK + 2 * HALO), :]
            prev, nxt = _shift(ext, 1)[HALO:HALO + CHUNK], _shift(ext, -1)[HALO:HALO + CHUNK]
            here = ext[HALO:HALO + CHUNK]
            conv = dw_ref[0:1, :] * prev + dw_ref[1:2, :] * here + dw_ref[2:3, :] * nxt + db_ref[...]
            bg, g = pre_ref[0, rows, :], pre_ref[3, rows, :]
            dzv = dz_ref[rows, :]
            dy = dzv * _silu(g)
            dpre_ref[3, rows, :] = (dzv * (bg * conv) * _dsilu(g)).astype(BF16)
            dpre_ref[0, rows, :] = (dy * conv).astype(BF16)
            dconv = dy * bg
            pad_c[pl.ds(off + b, CHUNK), :] = dconv
            ddw_ref[0:1, :] += jnp.sum(dconv * prev, axis=0, keepdims=True)
            ddw_ref[1:2, :] += jnp.sum(dconv * here, axis=0, keepdims=True)
            ddw_ref[2:3, :] += jnp.sum(dconv * nxt, axis=0, keepdims=True)
            ddb_ref[0:1, :] += jnp.sum(dconv, axis=0, keepdims=True)

        _for_chunks(segs, first)

        def second(s, st, ln, off, b):
            rows = pl.ds(b, CHUNK)
            ext = pad_c[pl.ds(off - HALO + b, CHUNK + 2 * HALO), :]
            da = (dw_ref[0:1, :] * _shift(ext, -1) + dw_ref[1:2, :] * ext + dw_ref[2:3, :] * _shift(ext, 1))
            da = da[HALO:HALO + CHUNK]
            dpre_ref[1, rows, :] = (da * pre_ref[2, rows, :]).astype(BF16)
            dpre_ref[2, rows, :] = (da * pre_ref[1, rows, :]).astype(BF16)

        _for_chunks(segs, second)

    quad = pl.BlockSpec((4, t, cb), lambda j: (0, 0, j))
    rows8 = pl.BlockSpec((8, cb), lambda j: (0, j))
    return _pc(
        body, name=name, grid=(width // cb,),
        in_specs=[pl.BlockSpec((t, cb), lambda j: (0, j)), quad, rows8, pl.BlockSpec((1, cb), lambda j: (0, j))],
        out_specs=[quad, rows8, rows8],
        out_shape=[jax.ShapeDtypeStruct((4, t, width), BF16), jax.ShapeDtypeStruct((8, width), F32),
                   jax.ShapeDtypeStruct((8, width), F32)],
        scratch_shapes=[pltpu.VMEM((t + 2 * HALO, cb), F32), pltpu.VMEM((t + 2 * HALO, cb), F32)],
        compiler_params=_params("parallel"),
    )(dz, pre, dw, db)


PAIR_TILES = 2 * WIN_ROWS - 2


def _pair_geometry():
    lane = lax.broadcasted_iota(jnp.int32, (GRID_W, LANES), 1)
    qcol = lax.broadcasted_iota(jnp.int32, (GRID_W, LANES), 0)
    low = lane < GRID_W
    kcol = jnp.where(low, lane, lane - GRID_W)
    start = jnp.clip(qcol - WIN_COLS // 2, 0, GRID_W - WIN_COLS)
    inside = (kcol >= start) & (kcol < start + WIN_COLS)
    return low, inside


def _bias_tiles(rpb_ref, rows_ref, tiles_ref, inside):
    for h in range(2):
        rows = rpb_ref[h]
        rows_ref[h] = (pltpu.roll(rows, LANES - (WIN_COLS - 1), 1)
                       + pltpu.roll(pltpu.roll(rows, GRID_W - (WIN_COLS - 1), 1), 2 * WIN_ROWS - 1, 0))
        for t in range(PAIR_TILES):
            both = jnp.broadcast_to(rows_ref[h, t:t + 1, :], (GRID_W, LANES))
            tiles_ref[h, t] = jnp.where(inside, pltpu.roll(both, 0, 1, stride=1, stride_axis=0), MASKED)


def _bias_tiles_grad(dtiles_ref, drpb_ref):
    n = PAIR_TILES * GRID_W
    qcol = lax.broadcasted_iota(jnp.int32, (n, LANES), 0) & (GRID_W - 1)
    lane = lax.broadcasted_iota(jnp.int32, (1, LANES), 1)
    zero = jnp.zeros((1, LANES), F32)
    for h in range(2):
        v = pltpu.roll(dtiles_ref[h].reshape(n, LANES), WIN_COLS - 1, 1)
        for bit in range(6):
            v = jnp.where((qcol >> bit) & 1 == 1, pltpu.roll(v, LANES - (1 << bit), 1), v)
        sums = [jnp.sum(v[t * GRID_W:(t + 1) * GRID_W], axis=0, keepdims=True) for t in range(PAIR_TILES)]
        for r in range(2 * WIN_ROWS):
            here = sums[r] if r < PAIR_TILES else zero
            prev = pltpu.roll(sums[r - 1], GRID_W, 1) if 1 <= r <= PAIR_TILES else zero
            drpb_ref[h, r:r + 1, :] = jnp.where(lane < 2 * WIN_COLS - 1, here + prev, 0.0)


def _attn_rows(r, n_rows):
    first = jnp.clip(r - WIN_ROWS // 2, 0, n_rows - WIN_ROWS)
    return first, first - r + WIN_ROWS - 1


def _softmax(s_loc, s_ctx):
    m = jnp.maximum(jnp.max(s_loc, axis=-1, keepdims=True), jnp.max(s_ctx, axis=-1, keepdims=True))
    e_loc, e_ctx = jnp.exp(s_loc - m), jnp.exp(s_ctx - m)
    inv = 1.0 / (jnp.sum(e_loc, axis=-1, keepdims=True) + jnp.sum(e_ctx, axis=-1, keepdims=True))
    return e_loc * inv, e_ctx * inv


def _pair_bias(tiles_ref, h, j):
    return jnp.concatenate([tiles_ref[h, j + 2 * m] for m in range(WIN_ROWS // 2)], axis=1)


ROWS_PER_STEP = 4


def _attn_items(step, n_rows, q_ref, low):
    items = []
    for u in range(ROWS_PER_STEP):
        r = step * ROWS_PER_STEP + u
        first, j = _attn_rows(r, n_rows)
        rows = pl.ds(pl.multiple_of(r * GRID_W, GRID_W), GRID_W)
        keys = pl.ds(pl.multiple_of(first * GRID_W, GRID_W), WIN_ROWS * GRID_W)
        q = (q_ref[rows, :].astype(F32) * HEAD_DIM ** -0.5).astype(BF16)
        zero = jnp.zeros_like(q)
        items.append((rows, keys, j, 0, jnp.where(low, q, zero)))
        items.append((rows, keys, j, 1, jnp.where(low, zero, q)))
    return items


def _attn_fwd(qkv, gate, rpb, seq):
    _, t, width = qkv.shape
    n_rows = seq // GRID_W
    n_ctx = t - seq
    blk = WIN_ROWS * GRID_W

    def body(q_ref, k_ref, v_ref, g_ref, rpb_ref, z_ref, o_ref, rows_ref, tiles_ref):
        low, inside = _pair_geometry()
        _bias_tiles(rpb_ref, rows_ref, tiles_ref, inside)
        ctx = pl.ds(seq, n_ctx)

        def step(i, carry):
            items = _attn_items(i, n_rows, q_ref, low)
            k_ctx, v_ctx = k_ref[ctx, :], v_ref[ctx, :]
            scores = [(_dot(q, k_ref[keys, :], 1, 1) + _pair_bias(tiles_ref, h, j), _dot(q, k_ctx, 1, 1))
                      for _, keys, j, h, q in items]
            probs = [_softmax(s_loc, s_ctx) for s_loc, s_ctx in scores]
            outs = [_dot(p_loc.astype(BF16), v_ref[keys, :]) + _dot(p_ctx.astype(BF16), v_ctx)
                    for (_, keys, _, _, _), (p_loc, p_ctx) in zip(items, probs)]
            for u in range(ROWS_PER_STEP):
                rows = items[2 * u][0]
                o = jnp.where(low, outs[2 * u], outs[2 * u + 1])
                o_ref[rows, :] = o
                z_ref[rows, :] = (o * _silu(g_ref[rows, :])).astype(BF16)
            return carry

        lax.fori_loop(0, n_rows // ROWS_PER_STEP, step, 0)

    def part(p):
        return pl.BlockSpec((None, t, LANES), lambda h: (p, 0, h))

    out = pl.BlockSpec((seq, LANES), lambda h: (0, h))
    return _pc(
        body, name="attn_fwd", grid=(width // LANES,),
        in_specs=[part(0), part(1), part(2), part(0), pl.BlockSpec((2, 2 * WIN_ROWS, LANES), lambda h: (h, 0, 0))],
        out_specs=[out, out],
        out_shape=[jax.ShapeDtypeStruct((seq, width), BF16), jax.ShapeDtypeStruct((seq, width), F32)],
        scratch_shapes=[pltpu.VMEM((2, 2 * WIN_ROWS, LANES), F32), pltpu.VMEM((2, PAIR_TILES, GRID_W, LANES), F32)],
        compiler_params=_params("parallel"),
    )(qkv, qkv, qkv, gate, rpb)


def _attn_bwd(qkv, gate, o, dz, rpb, seq):
    _, t, width = qkv.shape
    n_rows = seq // GRID_W
    n_ctx = t - seq
    blk = WIN_ROWS * GRID_W
    heads = 2 * width // LANES

    def body(q_ref, k_ref, v_ref, g_ref, o_ref, dz_ref, rpb_ref, dpre_ref, drpb_ref,
             rows_ref, tiles_ref, dtiles_ref, dk_ref, dv_ref):
        low, inside = _pair_geometry()
        _bias_tiles(rpb_ref, rows_ref, tiles_ref, inside)
        dtiles_ref[...] = jnp.zeros_like(dtiles_ref)
        dk_ref[...] = jnp.zeros_like(dk_ref)
        dv_ref[...] = jnp.zeros_like(dv_ref)
        ctx = pl.ds(seq, n_ctx)

        def step(i, carry):
            items = _attn_items(i, n_rows, q_ref, low)
            k_ctx, v_ctx = k_ref[ctx, :], v_ref[ctx, :]
            d_outs = []
            for u in range(ROWS_PER_STEP):
                rows = items[2 * u][0]
                g = g_ref[rows, :]
                dzv = dz_ref[rows, :]
                dpre_ref[3, rows, :] = (dzv * o_ref[rows, :] * _dsilu(g)).astype(BF16)
                d_o = (dzv * _silu(g)).astype(BF16)
                zero = jnp.zeros_like(d_o)
                d_outs += [jnp.where(low, d_o, zero), jnp.where(low, zero, d_o)]
            scores = [(_dot(q, k_ref[keys, :], 1, 1) + _pair_bias(tiles_ref, h, j), _dot(q, k_ctx, 1, 1))
                      for _, keys, j, h, q in items]
            dprobs = [(_dot(doh, v_ref[keys, :], 1, 1), _dot(doh, v_ctx, 1, 1))
                      for (_, keys, _, _, _), doh in zip(items, d_outs)]
            probs = [_softmax(s_loc, s_ctx) for s_loc, s_ctx in scores]
            dscores = []
            for (p_loc, p_ctx), (dp_loc, dp_ctx) in zip(probs, dprobs):
                delta = (jnp.sum(p_loc * dp_loc, axis=-1, keepdims=True)
                         + jnp.sum(p_ctx * dp_ctx, axis=-1, keepdims=True))
                dscores.append((p_loc * (dp_loc - delta), p_ctx * (dp_ctx - delta)))
            dqs = [_dot(ds_loc.astype(BF16), k_ref[keys, :]) + _dot(ds_ctx.astype(BF16), k_ctx)
                   for (_, keys, _, _, _), (ds_loc, ds_ctx) in zip(items, dscores)]
            for u in range(ROWS_PER_STEP):
                rows = items[2 * u][0]
                dpre_ref[0, rows, :] = (jnp.where(low, dqs[2 * u], dqs[2 * u + 1]) * HEAD_DIM ** -0.5).astype(BF16)
            for (_, keys, j, h, q), doh, (p_loc, p_ctx), (ds_loc, ds_ctx) in zip(items, d_outs, probs, dscores):
                dk_ref[keys, :] += _dot(ds_loc.astype(BF16), q, 0, 0)
                dk_ref[ctx, :] += _dot(ds_ctx.astype(BF16), q, 0, 0)
                dv_ref[keys, :] += _dot(p_loc.astype(BF16), doh, 0, 0)
                dv_ref[ctx, :] += _dot(p_ctx.astype(BF16), doh, 0, 0)
                for m in range(WIN_ROWS // 2):
                    dtiles_ref[h, j + 2 * m] += ds_loc[:, m * LANES:(m + 1) * LANES]
            return carry

        lax.fori_loop(0, n_rows // ROWS_PER_STEP, step, 0)
        dpre_ref[1] = dk_ref[...].astype(BF16)
        dpre_ref[2] = dv_ref[...].astype(BF16)
        dpre_ref[0, ctx, :] = jnp.zeros((n_ctx, LANES), BF16)
        dpre_ref[3, ctx, :] = jnp.zeros((n_ctx, LANES), BF16)
        _bias_tiles_grad(dtiles_ref, drpb_ref)

    def part(p):
        return pl.BlockSpec((None, t, LANES), lambda h: (p, 0, h))

    lat = pl.BlockSpec((seq, LANES), lambda h: (0, h))
    rspec = pl.BlockSpec((2, 2 * WIN_ROWS, LANES), lambda h: (h, 0, 0))
    tiles = pltpu.VMEM((2, PAIR_TILES, GRID_W, LANES), F32)
    return _pc(
        body, name="attn_bwd", grid=(width // LANES,),
        in_specs=[part(0), part(1), part(2), part(0), lat, lat, rspec],
        out_specs=[pl.BlockSpec((4, t, LANES), lambda h: (0, 0, h)), rspec],
        out_shape=[jax.ShapeDtypeStruct((4, t, width), BF16), jax.ShapeDtypeStruct((heads, 2 * WIN_ROWS, LANES), F32)],
        scratch_shapes=[pltpu.VMEM((2, 2 * WIN_ROWS, LANES), F32), tiles, tiles,
                        pltpu.VMEM((t, LANES), F32), pltpu.VMEM((t, LANES), F32)],
        compiler_params=_params("parallel"),
    )(qkv, qkv, qkv, gate, o, dz, rpb)


def _adamw(w, m, v, parts, name, mult=None):
    rows, cols = w.shape
    tr = _tile(rows, max(8, 131072 // cols), 8)
    n_parts = len(parts)
    c1 = 1.0 - ADAM_B1 ** ADAM_STEP
    c2 = 1.0 - ADAM_B2 ** ADAM_STEP

    def body(*refs):
        w_ref, m_ref, v_ref = refs[:3]
        part_refs = refs[3:3 + n_parts]
        rest = refs[3 + n_parts:]
        g = part_refs[0][...].astype(F32)
        for p in part_refs[1:]:
            g = g + p[...].astype(F32)
        if mult is not None:
            g = g * rest[0][...]
            rest = rest[1:]
        g_ref, d_ref, nm_ref, nv_ref = rest
        m2 = ADAM_B1 * m_ref[...] + (1.0 - ADAM_B1) * g
        v2 = ADAM_B2 * v_ref[...] + (1.0 - ADAM_B2) * (g * g)
        m_hat = m2 / c1
        v_hat = v2 / c2
        g_ref[...] = g
        d_ref[...] = -ADAM_LR * (m_hat / (jnp.sqrt(v_hat) + ADAM_EPS) + ADAM_WD * w_ref[...])
        nm_ref[...] = m2
        nv_ref[...] = v2

    tile = pl.BlockSpec((tr, cols), lambda i: (i, 0))
    in_specs, args = [tile, tile, tile], [w, m, v]
    for p in parts:
        if isinstance(p, tuple):
            arr, k = p
            in_specs.append(pl.BlockSpec((None, tr, cols), lambda i, k=k: (k, i, 0)))
            args.append(arr)
        else:
            in_specs.append(tile)
            args.append(p)
    if mult is not None:
        in_specs.append(tile)
        args.append(mult)
    shape = jax.ShapeDtypeStruct((rows, cols), F32)
    return _pc(
        body, name=name, grid=(rows // tr,), in_specs=in_specs, out_specs=[tile] * 4, out_shape=[shape] * 4,
        compiler_params=_params("parallel"),
    )(*args)


def _rows128(a):
    flat = a.reshape(-1)
    pad = (-flat.shape[0]) % LANES
    if pad:
        flat = jnp.concatenate([flat, jnp.zeros((pad,), flat.dtype)])
    return flat.reshape(-1, LANES)


def _pad_rows(a, mult=8):
    pad = (-a.shape[0]) % mult
    if pad:
        a = jnp.concatenate([a, jnp.zeros((pad,) + a.shape[1:], a.dtype)], axis=0)
    return a


def kernel(x, c, ctx, c_ctx, norm_g, ada_w, ada_b, pool_w_in, pool_w_grp, pool_scale, pool_w_out, na_w_in, na_rpb, na_w_out, conv_w_in, conv_dw, conv_db, conv_w_out, final_g, loss_target, m_c_ctx, m_norm_g, m_ada_w, m_ada_b, m_pool_w_in, m_pool_w_grp, m_pool_scale, m_pool_w_out, m_na_w_in, m_na_rpb, m_na_w_out, m_conv_w_in, m_conv_dw, m_conv_db, m_conv_w_out, m_final_g, v_c_ctx, v_norm_g, v_ada_w, v_ada_b, v_pool_w_in, v_pool_w_grp, v_pool_scale, v_pool_w_out, v_na_w_in, v_na_rpb, v_na_w_out, v_conv_w_in, v_conv_dw, v_conv_db, v_conv_w_out, v_final_g):
    xi, yi, ci = _my_place()
    me = 4 * xi + 2 * yi + ci
    seq, d = x.shape[1], x.shape[2]
    n_ctx = ctx.shape[1]
    t_all = seq + n_ctx
    width = d
    depth = norm_g.shape[0]
    nb = ada_w.shape[2]
    shard = width // N_DEV
    d_rows = d // LANES
    tr = math.gcd(math.gcd(seq, n_ctx), 256)
    x_tiles = seq // tr

    n_pool = pool_scale.shape[0]
    n_grp = pool_w_grp.shape[1]
    grp = width // n_grp
    layer_weights = [[pool_w_in[0], pool_w_grp[0], pool_w_out[0]], [na_w_in[0], na_w_out[0]],
                     [conv_w_in[0], conv_w_out[0]], [pool_w_in[1], pool_w_grp[1], pool_w_out[1]]]
    in_flight, token = [], jnp.zeros((8, LANES), F32)
    for i, ws in enumerate(layer_weights):
        state = _exchange_start([w.astype(BF16) for w in ws], False, token, f"weights_start{i}")
        token = state[-1]
        in_flight.append(state)

    def landed_weight(i, t, after):
        return _exchange_wait(in_flight[i], False, after, f"weights_wait{i}_{t}", which=[t])[0]

    def as_in(w):
        return w[:, None]

    def as_grp(w):
        return w.transpose(1, 0, 2, 3).reshape(n_grp, grp, grp)

    def as_out(w):
        return w.reshape(width, d)

    small_in =_pad_rows(jnp.concatenate([_rows128(c), pool_scale, conv_dw[0], conv_db], axis=0))
    got = _gather_small(small_in, "gather_inputs")
    r0 = d_rows
    c_all = got[:, :r0].reshape(N_DEV, d)
    n_pool = pool_scale.shape[0]
    scale_full = got[:, r0:r0 + n_pool].transpose(1, 0, 2).reshape(n_pool, width)
    r1 = r0 + n_pool
    taps_full = _pad_rows(got[:, r1:r1 + 3].transpose(1, 0, 2).reshape(3, width))
    bias_full = got[:, r1 + 3:r1 + 4].transpose(1, 0, 2).reshape(1, width)

    cond = jnp.concatenate([c_all, c_ctx[None], jnp.zeros((7, d), F32)], axis=0)
    bias_mine = lax.dynamic_slice(ada_b, (0, me * nb), (depth, nb))
    mod_mine = _mod_fwd(cond, ada_w, bias_mine)
    mod_all = _gather_small(mod_mine.reshape(-1, LANES), "gather_mod")
    mod_all = mod_all.reshape(N_DEV, depth, 16, nb).transpose(1, 2, 0, 3).reshape(depth, 16, 3 * d)
    mod_x = lax.dynamic_index_in_dim(mod_all, me, 1, keepdims=False).reshape(depth, 3, d)
    mod_c = mod_all[:, 8].reshape(depth, 3, d)
    pad5 = jnp.zeros((depth, 5, d), F32)
    mod_x = jnp.concatenate([mod_x, pad5], axis=1)
    mod_c = jnp.concatenate([mod_c, pad5], axis=1)
    mods = [jnp.stack([mod_x[i], mod_c[i]]) if i < 2 else mod_x[i][None] for i in range(depth)]

    both = [(0, seq), (seq, n_ctx)]
    latent = [(0, seq)]

    def grp_slots(g):
        return g.reshape(n_grp, N_DEV, grp // N_DEV, grp).transpose(1, 0, 2, 3).reshape(N_DEV, -1, grp).astype(BF16)

    def send_grads(i, grads):
        return _exchange_start(grads, True, jnp.zeros((8, LANES), F32), f"grads_start{i}")

    xs0 = jnp.concatenate([x[0], ctx[0]], axis=0)
    h0 = _norm_fwd(xs0, norm_g[0:1] + token[0, 0], mods[0], tr, x_tiles, "norm_fwd0")
    pool_in_w0 = as_in(landed_weight(0, 0, h0))
    pre0 = _proj_in(h0, pool_in_w0, 0, width, "proj_in0")
    pool_grp_w0 = as_grp(landed_weight(0, 1, pre0))
    z0, diff0 = _pool_fwd(pre0, pool_grp_w0, scale_full[0:1], both, "pool_fwd0")
    pool_out_w0 = as_out(landed_weight(0, 2, z0))
    yx0, xs1, h1 = _proj_out(z0, pool_out_w0, xs0, mods[0], tr, x_tiles, "proj_out0", nxt=(norm_g[1:2], mods[1]))

    na_in_w = as_in(landed_weight(1, 0, h1))
    per_part = width // na_w_in.shape[2]
    qkv1 = _proj_in(h1, na_in_w, 0, width, "proj_in1_qkv", blocks=(0, 3 * per_part), dtype=BF16)
    gpre1 = _proj_in(h1, na_in_w, 0, width, "proj_in1_gate", blocks=(3 * per_part, per_part))
    rpb_rows = jnp.pad(na_rpb[0], ((0, 0), (0, 2 * WIN_ROWS - na_rpb.shape[2]), (0, LANES - na_rpb.shape[3])))
    z1, o1 = _attn_fwd(qkv1, gpre1, rpb_rows, seq)
    na_out_w = as_out(landed_weight(1, 1, z1))
    yx1, x2, h2 = _proj_out(z1, na_out_w, xs1, mods[1], tr, x_tiles, "proj_out1", nxt=(norm_g[2:3], mods[2]))

    conv_in_w = as_in(landed_weight(2, 0, h2))
    pre2 = _proj_in(h2, conv_in_w, 0, width, "proj_in2")
    z2 = _conv_fwd(pre2, taps_full, bias_full, "conv_fwd")
    conv_out_w = as_out(landed_weight(2, 1, z2))
    yx2, x3, h3 = _proj_out(z2, conv_out_w, x2, mods[2], tr, x_tiles, "proj_out2", nxt=(norm_g[3:4], mods[3]))

    pool_in_w3 = as_in(landed_weight(3, 0, h3))
    pre3 = _proj_in(h3, pool_in_w3, 0, width, "proj_in3")
    pool_grp_w3 = as_grp(landed_weight(3, 1, pre3))
    z3, diff3 = _pool_fwd(pre3, pool_grp_w3, scale_full[1:2], latent, "pool_fwd3")
    pool_out_w3 = as_out(landed_weight(3, 2, z3))
    yx3, x4 = _proj_out(z3, pool_out_w3, x3, mods[3], tr, x_tiles, "proj_out3")

    loss_part, dx4, d_final, dyx3, gate3 = _loss_head(x4, loss_target[0], final_g[None], yx3, mods[3], tr)
    loss = lax.psum(loss_part[0, 0], ("x", "y", "c"))

    dz3 = _proj_out_dz(dyx3, pool_out_w3, "proj_out_dz3")
    g_pool_out1 = _grad_w_out(z3, dyx3, "grad_w_out3")
    dpre3, g_grp1, g_scale1 = _pool_bwd(dz3, diff3, pre3, pool_grp_w3, scale_full[1:2], latent, "pool_bwd3")
    dh3 = _proj_in_dh(dpre3, pool_in_w3, 0, "proj_in_dh3")
    g_pool_in1 = _grad_w_in(h3, dpre3, pool_w_in.shape[2], "grad_w_in3")
    sent3 = send_grads(3, [g_pool_in1, grp_slots(g_grp1), g_pool_out1.reshape(N_DEV, shard, d)])
    dx3, norm3, dyx2, gate2 = _norm_bwd(x3, dh3, dx4, norm_g[3:4] + sent3[-1][0, 0], mods[3], tr, x_tiles, "norm_bwd3",
                                        below=(yx2, mods[2]))

    dz2 = _proj_out_dz(dyx2, conv_out_w, "proj_out_dz2")
    g_conv_out = _grad_w_out(z2, dyx2, "grad_w_out2")
    dpre2, g_taps, g_cbias = _conv_bwd(dz2, pre2, taps_full, bias_full, "conv_bwd")
    dh2 = _proj_in_dh(dpre2, conv_in_w, 0, "proj_in_dh2")
    g_conv_in = _grad_w_in(h2, dpre2, conv_w_in.shape[2], "grad_w_in2")
    sent2 = send_grads(2, [g_conv_in, g_conv_out.reshape(N_DEV, shard, d)])
    dx2, norm2, dyx1, gate1 = _norm_bwd(x2, dh2, dx3, norm_g[2:3] + sent2[-1][0, 0], mods[2], tr, x_tiles, "norm_bwd2",
                                        below=(yx1, mods[1][:1]))

    dz1 = _proj_out_dz(dyx1, na_out_w, "proj_out_dz1")
    g_na_out = _grad_w_out(z1, dyx1, "grad_w_out1")
    dpre1, g_rpb = _attn_bwd(qkv1, gpre1, o1, dz1, rpb_rows, seq)
    g_rpb = g_rpb[:, :na_rpb.shape[2], :na_rpb.shape[3]]
    dh1 = _proj_in_dh(dpre1, na_in_w, 0, "proj_in_dh1")
    g_na_in = _grad_w_in(h1, dpre1, na_w_in.shape[2], "grad_w_in1")
    sent1 = send_grads(1, [g_na_in, g_na_out.reshape(N_DEV, shard, d)])
    dxs1, norm1, dyx0, gate0 = _norm_bwd(xs1, dh1, dx2, norm_g[1:2] + sent1[-1][0, 0], mods[1], tr, x_tiles, "norm_bwd1",
                                         res_tiles=x_tiles, below=(yx0, mods[0]))

    dz0 = _proj_out_dz(dyx0, pool_out_w0, "proj_out_dz0")
    g_pool_out0 = _grad_w_out(z0, dyx0, "grad_w_out0")
    dpre0, g_grp0, g_scale0 = _pool_bwd(dz0, diff0, pre0, pool_grp_w0, scale_full[0:1], both, "pool_bwd0")
    dh0 = _proj_in_dh(dpre0, pool_in_w0, 0, "proj_in_dh0")
    g_pool_in0 = _grad_w_in(h0, dpre0, pool_w_in.shape[2], "grad_w_in0")
    dx0, norm0 = _norm_bwd(xs0, dh0, dxs1, norm_g[0:1], mods[0], tr, x_tiles, "norm_bwd0", out_tiles=x_tiles)
    grad_x = dx0[None]

    norms, gates = [norm0, norm1, norm2, norm3], [gate0, gate1, gate2, gate3]
    zero_d = jnp.zeros((d,), F32)
    dm_rows = [jnp.concatenate([norms[i][0, 0], norms[i][0, 1], gates[i][0, 0]]) for i in range(depth)]
    dm_rows.append(jnp.concatenate([norm0[1, 0], norm0[1, 1], gate0[1, 0]]))
    dm_rows.append(jnp.concatenate([norm1[1, 0], norm1[1, 1], zero_d]))
    dm_local = jnp.stack(dm_rows + [jnp.zeros((3 * d,), F32)] * 2)
    g_norm_part = jnp.stack([norm0[0, 2] + norm0[1, 2], norm1[0, 2] + norm1[1, 2], norm2[0, 2], norm3[0, 2]])
    g_scale_part = jnp.concatenate([g_scale0, g_scale1], axis=0)
    pieces = [_rows128(dm_local), _rows128(g_norm_part), _rows128(d_final[0]), _pad_rows(_rows128(g_rpb)),
              _rows128(g_scale_part), _rows128(g_taps[:3]), _rows128(g_cbias[0])]
    small_sent = _exchange_start([jnp.concatenate(pieces, axis=0)], False, jnp.zeros((8, LANES), F32), "small_grads_start")
    marks = np.cumsum([0] + [p.shape[0] for p in pieces])
    sent0 = _exchange_start([g_pool_in0, grp_slots(g_grp0), g_pool_out0.reshape(N_DEV, shard, d)], True, small_sent[-1],
                            "grads_start0")

    def big(parts, w, m, v, name):
        shape = w.shape
        view = (-1, shape[-1])
        parts = [(parts.reshape((N_DEV,) + w.reshape(view).shape), k) for k in range(N_DEV)]
        return [r.reshape(shape) for r in _adamw(w.reshape(view), m.reshape(view), v.reshape(view), parts, name)]

    in3, grp3, out3 = _exchange_wait(sent3, True, sent0[-1], "grads_wait3")
    in2, out2 = _exchange_wait(sent2, True, sent0[-1], "grads_wait2")
    in1, out1 = _exchange_wait(sent1, True, sent0[-1], "grads_wait1")
    res = {}
    res["na_w_in"] = [r[None] for r in big(in1, na_w_in[0], m_na_w_in[0], v_na_w_in[0], "adamw_na_in")]
    res["na_w_out"] = [r[None] for r in big(out1, na_w_out[0], m_na_w_out[0], v_na_w_out[0], "adamw_na_out")]
    res["conv_w_in"] = [r[None] for r in big(in2, conv_w_in[0], m_conv_w_in[0], v_conv_w_in[0], "adamw_conv_in")]
    res["conv_w_out"] = [r[None] for r in big(out2, conv_w_out[0], m_conv_w_out[0], v_conv_w_out[0], "adamw_conv_out")]
    pool_in1 = big(in3, pool_w_in[1], m_pool_w_in[1], v_pool_w_in[1], "adamw_pool_in1")
    pool_grp1 = big(grp3, pool_w_grp[1], m_pool_w_grp[1], v_pool_w_grp[1], "adamw_pool_grp1")
    pool_out1 = big(out3, pool_w_out[1], m_pool_w_out[1], v_pool_w_out[1], "adamw_pool_out1")

    small_out = _exchange_wait(small_sent, False, pool_out1[0], "small_grads_wait")[0]

    def piece(k):
        return small_out[:, marks[k]:marks[k + 1]]

    dm_all = piece(0).reshape(N_DEV, 8, 3 * d).transpose(1, 0, 2)
    dm_mine = lax.dynamic_slice(dm_all, (0, 0, me * nb), (8, N_DEV, nb))
    g_ada_w, g_ada_b, cctx_part, dsilu_cond = _mod_bwd(cond, ada_w, dm_all, dm_mine)
    cctx_all = _gather_small(_rows128(cctx_part[0]), "gather_cctx")

    def my_shard(a, n):
        a = a.reshape(N_DEV, n, N_DEV, shard)
        return lax.dynamic_index_in_dim(a, me, 2, keepdims=False)

    zeros7 = lambda r: jnp.zeros((N_DEV - 1, r, LANES), F32)
    ada_b_rows = _rows128(g_ada_b)
    small_parts = jnp.concatenate([
        cctx_all, piece(1), jnp.concatenate([ada_b_rows[None], zeros7(ada_b_rows.shape[0])], axis=0), piece(2), piece(3),
        my_shard(piece(4), n_pool), my_shard(piece(5), 3), my_shard(piece(6), 1)], axis=1)
    n_small = small_parts.shape[1]
    small_parts = jnp.concatenate([small_parts, jnp.zeros((N_DEV, (-n_small) % 8, LANES), F32)], axis=1)

    def pack(c_ctx_, norm_g_, ada_b_, final_g_, rpb_, scale_, taps_, cbias_):
        rows = [_rows128(c_ctx_), _rows128(norm_g_), _rows128(ada_b_), _rows128(final_g_), _pad_rows(_rows128(rpb_)),
                scale_, taps_[0], cbias_]
        return _pad_rows(jnp.concatenate(rows, axis=0)), np.cumsum([0] + [r.shape[0] for r in rows])

    w_small, smarks = pack(c_ctx, norm_g, ada_b, final_g, na_rpb, pool_scale, conv_dw, conv_db)
    m_small, _ = pack(m_c_ctx, m_norm_g, m_ada_b, m_final_g, m_na_rpb, m_pool_scale, m_conv_dw, m_conv_db)
    v_small, _ = pack(v_c_ctx, v_norm_g, v_ada_b, v_final_g, v_na_rpb, v_pool_scale, v_conv_dw, v_conv_db)
    mult = jnp.concatenate([_rows128(dsilu_cond[8]), jnp.ones((w_small.shape[0] - d_rows, LANES), F32)], axis=0)
    small_res = _adamw(w_small, m_small, v_small, [(small_parts, k) for k in range(N_DEV)], "adamw_small", mult=mult)

    def unpack(k, like):
        out = []
        for r in small_res:
            flat = r[smarks[k]:smarks[k + 1]].reshape(-1)
            out.append(flat[:like.size].reshape(like.shape))
        return out

    res.update({"c_ctx": unpack(0, c_ctx), "norm_g": unpack(1, norm_g), "ada_b": unpack(2, ada_b),
                "final_g": unpack(3, final_g), "na_rpb": unpack(4, na_rpb), "pool_scale": unpack(5, pool_scale),
                "conv_dw": unpack(6, conv_dw), "conv_db": unpack(7, conv_db)})

    res["ada_w"] = [r.reshape(ada_w.shape) for r in _adamw(
        ada_w.reshape(-1, nb), m_ada_w.reshape(-1, nb), v_ada_w.reshape(-1, nb), [g_ada_w.reshape(-1, nb)], "adamw_ada_w")]

    in0, grp0, out0 = _exchange_wait(sent0, True, small_res[0], "grads_wait0")
    pool_in0 = big(in0, pool_w_in[0], m_pool_w_in[0], v_pool_w_in[0], "adamw_pool_in0")
    pool_grp0 = big(grp0, pool_w_grp[0], m_pool_w_grp[0], v_pool_w_grp[0], "adamw_pool_grp0")
    pool_out0 = big(out0, pool_w_out[0], m_pool_w_out[0], v_pool_w_out[0], "adamw_pool_out0")
    res["pool_w_in"] = [jnp.stack([p, q]) for p, q in zip(pool_in0, pool_in1)]
    res["pool_w_grp"] = [jnp.stack([p, q]) for p, q in zip(pool_grp0, pool_grp1)]
    res["pool_w_out"] = [jnp.stack([p, q]) for p, q in zip(pool_out0, pool_out1)]

    order = ["c_ctx", "norm_g", "ada_w", "ada_b", "pool_w_in", "pool_w_grp", "pool_scale", "pool_w_out", "na_w_in",
             "na_rpb", "na_w_out", "conv_w_in", "conv_dw", "conv_db", "conv_w_out", "final_g"]
    outs = [loss, grad_x]
    for j in range(4):
        outs += [res[n][j] for n in order]
    return tuple(outs)
```

```python
import functools
import math

import numpy as np
import jax
import jax.numpy as jnp
from jax import lax
from jax.experimental import pallas as pl
from jax.experimental.pallas import tpu as pltpu

F32 = jnp.float32
BF16 = jnp.bfloat16
N_DEV = 8
LANES = 128
RMS_EPS = 1e-6
GRID_W = 64
WIN_ROWS = 8
WIN_COLS = 16
HEAD_DIM = 64
POOL_WINDOWS = (2, 4, 8, 16)
HALO = 8
CHUNK = 128
MASKED = -1e30
ADAM_LR = 0.001
ADAM_B1 = 0.9
ADAM_B2 = 0.999
ADAM_EPS = 1e-08
ADAM_WD = 0.01
ADAM_STEP = 10
VMEM_LIMIT = 56 * 1024 * 1024
MESH = pl.DeviceIdType.MESH
ANY = pl.BlockSpec(memory_space=pl.ANY)
HBM = pl.BlockSpec(memory_space=pltpu.HBM)
SEM = pl.BlockSpec(memory_space=pltpu.SEMAPHORE)
EFFECT = pltpu.SideEffectType.DATAFLOW_SIDE_EFFECTING


def _pc(body, *, name, **kw):
    return pl.pallas_call(body, name=name, **kw)


def _params(*sem):
    return pltpu.CompilerParams(dimension_semantics=sem if sem else None, vmem_limit_bytes=VMEM_LIMIT)


def _dot(a, b, ca=1, cb=0, precision=None):
    return lax.dot_general(a, b, (((ca,), (cb,)), ((), ())), preferred_element_type=F32, precision=precision)


def _tile(n, pref, unit=LANES):
    best = None
    for t in range(unit, min(n, pref) + 1, unit):
        if n % t == 0:
            best = t
    return best if best is not None else n


def _sigmoid(x):
    return 1.0 / (1.0 + jnp.exp(-x))


def _silu(x):
    return x * _sigmoid(x)


def _dsilu(x):
    s = _sigmoid(x)
    return s * (1.0 + x * (1.0 - s))


def _my_place():
    return lax.axis_index("x"), lax.axis_index("y"), lax.axis_index("c")


def _flip(v, f):
    return 1 - v if f else v


def _gather_small(block, name):
    rows, cols = block.shape

    def body(x_ref, out_ref, send_sems, recv_sems):
        x, y, c = _my_place()
        me = 4 * x + 2 * y + c
        out_ref[me] = x_ref[...]
        copies = []
        for k in range(1, N_DEV):
            peer = (_flip(x, k & 4), _flip(y, k & 2), _flip(c, k & 1))
            cp = pltpu.make_async_remote_copy(
                src_ref=x_ref, dst_ref=out_ref.at[me], send_sem=send_sems.at[k - 1], recv_sem=recv_sems.at[k - 1],
                device_id=peer, device_id_type=MESH)
            cp.start()
            copies.append(cp)
        for cp in copies:
            cp.wait()

    return _pc(
        body, name=name,
        out_shape=jax.ShapeDtypeStruct((N_DEV, rows, cols), block.dtype),
        in_specs=[pl.BlockSpec(memory_space=pltpu.VMEM)],
        out_specs=pl.BlockSpec(memory_space=pltpu.VMEM),
        scratch_shapes=[pltpu.SemaphoreType.DMA((N_DEV - 1,)), pltpu.SemaphoreType.DMA((N_DEV - 1,))],
    )(block)


def _exchange_copies(srcs, lands, send_sems, recv_sems, per_dest):
    x, y, c = _my_place()
    me = 4 * x + 2 * y + c
    copies = []
    for t, (src, land) in enumerate(zip(srcs, lands)):
        for k in range(N_DEV):
            peer = (_flip(x, k & 4), _flip(y, k & 2), _flip(c, k & 1))
            dest = 4 * peer[0] + 2 * peer[1] + peer[2]
            s = t * N_DEV + k
            copies.append(pltpu.make_async_remote_copy(
                src_ref=src.at[dest] if per_dest else src, dst_ref=land.at[me],
                send_sem=send_sems[s], recv_sem=recv_sems[s], device_id=peer, device_id_type=MESH))
    return copies


def _exchange_start(srcs, per_dest, after, name):
    nt = len(srcs)
    ns = nt * N_DEV
    lands = [lax.empty((N_DEV,) + (s.shape[1:] if per_dest else s.shape), s.dtype) for s in srcs]

    def body(*refs):
        ins, outs = refs[:2 * nt + 1], refs[2 * nt + 1:]
        for cp in _exchange_copies(ins[:nt], ins[nt:2 * nt], outs[:ns], outs[ns:2 * ns], per_dest):
            cp.start()
        outs[-1][...] = jnp.zeros_like(outs[-1])

    hbm = [pltpu.with_memory_space_constraint(a, pltpu.HBM) for a in list(srcs) + lands]
    res = _pc(
        body, name=name,
        out_shape=(*[pltpu.SemaphoreType.DMA(())] * (2 * ns), *[pltpu.HBM(a.shape, a.dtype) for a in hbm],
                   jax.ShapeDtypeStruct((8, LANES), F32)),
        in_specs=[HBM] * (2 * nt) + [ANY],
        out_specs=(*[SEM] * (2 * ns), *[HBM] * (2 * nt), pl.BlockSpec(memory_space=pltpu.VMEM)),
        input_output_aliases={i: 2 * ns + i for i in range(2 * nt)},
        compiler_params=pltpu.CompilerParams(has_side_effects=EFFECT),
    )(*hbm, after)
    sems, rest = res[:2 * ns], res[2 * ns:]
    return list(sems[:ns]), list(sems[ns:]), list(rest[:nt]), list(rest[nt:2 * nt]), rest[-1]


def _exchange_wait(state, per_dest, after, name, which=None):
    send_sems, recv_sems, srcs, lands, _ = state
    which = list(range(len(srcs))) if which is None else which
    pick = [t * N_DEV + k for t in which for k in range(N_DEV)]
    send_sems, recv_sems = [send_sems[s] for s in pick], [recv_sems[s] for s in pick]
    srcs, lands = [srcs[t] for t in which], [lands[t] for t in which]
    nt = len(srcs)
    ns = len(send_sems)

    def body(*refs):
        sems = refs[2 * nt:2 * nt + 2 * ns]
        for cp in _exchange_copies(refs[:nt], refs[nt:2 * nt], sems[:ns], sems[ns:], per_dest):
            cp.wait_send()
            cp.wait_recv()

    thru = list(srcs) + list(lands)
    res = _pc(
        body, name=name,
        out_shape=tuple(pltpu.HBM(a.shape, a.dtype) for a in thru),
        in_specs=[HBM] * (2 * nt) + [SEM] * (2 * ns) + [ANY],
        out_specs=tuple([HBM] * (2 * nt)),
        input_output_aliases={i: i for i in range(2 * nt)},
        compiler_params=pltpu.CompilerParams(has_side_effects=EFFECT),
    )(*thru, *send_sems, *recv_sems, after)
    return list(res[nt:])


def _mod_fwd(cond, ada_w, bias):
    depth, d, nb = ada_w.shape

    def body(c_ref, w_ref, b_ref, o_ref):
        s = _silu(c_ref[...]).astype(BF16)
        o_ref[...] = _dot(s, w_ref[...].astype(BF16)) + b_ref[...]

    return _pc(
        body, name="mod_fwd", grid=(depth,),
        in_specs=[pl.BlockSpec((16, d), lambda i: (0, 0)), pl.BlockSpec((None, d, nb), lambda i: (i, 0, 0)),
                  pl.BlockSpec((None, 1, nb), lambda i: (i, 0, 0))],
        out_specs=pl.BlockSpec((None, 16, nb), lambda i: (i, 0, 0)),
        out_shape=jax.ShapeDtypeStruct((depth, 16, nb), F32),
        compiler_params=_params("parallel"),
    )(cond, ada_w, bias.reshape(depth, 1, nb))


def _mod_bwd(cond, ada_w, dm_all, dm_mine):
    depth, d, nb = ada_w.shape
    d3 = dm_all.shape[-1]

    def body(c_ref, w_ref, all_ref, call_ref, mine_ref, cmine_ref, gw_ref, gb_ref, part_ref, ds_ref):
        i = pl.program_id(0)
        cond_v = c_ref[...]
        s = _silu(cond_v).astype(BF16)
        has_ctx = jnp.where(i < 2, 1.0, 0.0)
        tot_all = jnp.sum(call_ref[...], axis=0, keepdims=True) * has_ctx
        tot_mine = jnp.broadcast_to(jnp.sum(cmine_ref[...], axis=0, keepdims=True) * has_ctx, (8, nb)).astype(BF16)
        gb_ref[...] = jnp.sum(all_ref[...], axis=0, keepdims=True) + tot_all
        gw_ref[...] = _dot(s[0:8], mine_ref[...].astype(BF16), 0, 0) + _dot(s[8:16], tot_mine, 0, 0)
        part = _dot(tot_mine, w_ref[...].astype(BF16), 1, 1)

        @pl.when(i == 0)
        def _():
            part_ref[...] = jnp.zeros_like(part_ref)
            ds_ref[...] = _dsilu(cond_v)

        part_ref[...] += part

    def rows(width, which):
        return pl.BlockSpec((None, N_DEV, width), which)

    layer = lambda i: (i, 0, 0)
    ctx_layer = lambda i: (jnp.minimum(i, 1) + 4, 0, 0)
    return _pc(
        body, name="mod_bwd", grid=(depth,),
        in_specs=[pl.BlockSpec((16, d), lambda i: (0, 0)), pl.BlockSpec((None, d, nb), layer),
                  rows(d3, layer), rows(d3, ctx_layer), rows(nb, layer), rows(nb, ctx_layer)],
        out_specs=[pl.BlockSpec((None, d, nb), layer), pl.BlockSpec((None, 1, d3), layer),
                   pl.BlockSpec((8, d), lambda i: (0, 0)), pl.BlockSpec((16, d), lambda i: (0, 0))],
        out_shape=[jax.ShapeDtypeStruct((depth, d, nb), F32), jax.ShapeDtypeStruct((depth, 1, d3), F32),
                   jax.ShapeDtypeStruct((8, d), F32), jax.ShapeDtypeStruct((16, d), F32)],
        compiler_params=_params("arbitrary"),
    )(cond, ada_w, dm_all, dm_all, dm_mine, dm_mine)


def _norm_fwd(xs, g, mod, tr, seg_tiles, name):
    t, d = xs.shape

    def body(x_ref, g_ref, mod_ref, h_ref):
        x = x_ref[...]
        r = lax.rsqrt(jnp.mean(x * x, axis=-1, keepdims=True) + RMS_EPS)
        y = (x * r) * g_ref[...]
        h_ref[...] = (y * (1.0 + mod_ref[1:2, :]) + mod_ref[0:1, :]).astype(BF16)

    return _pc(
        body, name=name, grid=(t // tr,),
        in_specs=[pl.BlockSpec((tr, d), lambda i: (i, 0)), pl.BlockSpec((1, d), lambda i: (0, 0)),
                  pl.BlockSpec((None, 8, d), lambda i: (i // seg_tiles, 0, 0))],
        out_specs=pl.BlockSpec((tr, d), lambda i: (i, 0)),
        out_shape=jax.ShapeDtypeStruct((t, d), BF16),
        compiler_params=_params("parallel"),
    )(xs, g, mod)


def _resid_grad(dx, i, seg_tiles, yx_ref, gate_ref, dyx_ref, gsum_ref):
    dyx_ref[...] = (dx * gate_ref[2:3, :]).astype(BF16)

    @pl.when(i % seg_tiles == 0)
    def _():
        gsum_ref[...] = jnp.zeros_like(gsum_ref)

    gsum_ref[0:1, :] += jnp.sum(dx * yx_ref[...], axis=0, keepdims=True)


def _norm_bwd(xs, dh, dres, g, mod, tr, seg_tiles, name, res_tiles=None, out_tiles=None, below=None):
    t, d = xs.shape
    n_tiles = t // tr
    res_tiles = n_tiles if res_tiles is None else res_tiles
    out_tiles = n_tiles if out_tiles is None else out_tiles

    def body(x_ref, dh_ref, dres_ref, g_ref, mod_ref, *rest):
        i = pl.program_id(0)
        x = x_ref[...]
        r = lax.rsqrt(jnp.mean(x * x, axis=-1, keepdims=True) + RMS_EPS)
        xn = x * r
        dhv = dh_ref[...]
        gain = g_ref[...]
        one_scale = 1.0 + mod_ref[1:2, :]
        dxn = dhv * (gain * one_scale)
        dx = r * (dxn - xn * jnp.mean(dxn * xn, axis=-1, keepdims=True))
        if res_tiles == n_tiles:
            dx = dx + dres_ref[...]
        else:
            dx = dx + jnp.where(i < res_tiles, dres_ref[...], 0.0)
        if below is None:
            dx_ref, sum_ref = rest
        else:
            yx_ref, gate_ref, dx_ref, sum_ref, dyx_ref, gsum_ref = rest
            _resid_grad(dx, i, seg_tiles, yx_ref, gate_ref, dyx_ref, gsum_ref)
        if out_tiles == n_tiles:
            dx_ref[...] = dx
        else:
            @pl.when(i < out_tiles)
            def _():
                dx_ref[...] = dx

        @pl.when(i % seg_tiles == 0)
        def _():
            sum_ref[...] = jnp.zeros_like(sum_ref)

        sum_ref[0:1, :] += jnp.sum(dhv, axis=0, keepdims=True)
        sum_ref[1:2, :] += jnp.sum(dhv * (xn * gain), axis=0, keepdims=True)
        sum_ref[2:3, :] += jnp.sum(dhv * one_scale * xn, axis=0, keepdims=True)

    row = pl.BlockSpec((tr, d), lambda i: (i, 0))
    seg = pl.BlockSpec((None, 8, d), lambda i: (i // seg_tiles, 0, 0))
    in_specs = [row, row, pl.BlockSpec((tr, d), lambda i: (jnp.minimum(i, res_tiles - 1), 0)),
                pl.BlockSpec((1, d), lambda i: (0, 0)), seg]
    out_specs = [pl.BlockSpec((tr, d), lambda i: (jnp.minimum(i, out_tiles - 1), 0)), seg]
    out_shape = [jax.ShapeDtypeStruct((out_tiles * tr, d), F32), jax.ShapeDtypeStruct((mod.shape[0], 8, d), F32)]
    args = [xs, dh, dres, g, mod]
    if below is not None:
        in_specs += [row, seg]
        out_specs += [row, seg]
        out_shape += [jax.ShapeDtypeStruct((t, d), BF16), jax.ShapeDtypeStruct((below[1].shape[0], 8, d), F32)]
        args += list(below)
    return _pc(
        body, name=name, grid=(n_tiles,), in_specs=in_specs, out_specs=out_specs, out_shape=out_shape,
        compiler_params=_params("arbitrary"),
    )(*args)


def _loss_head(xs, target, g, yx, mod, tr):
    t, d = xs.shape

    def body(x_ref, t_ref, g_ref, yx_ref, gate_ref, loss_ref, dx_ref, dg_ref, dyx_ref, gsum_ref):
        i = pl.program_id(0)
        x = x_ref[...]
        r = lax.rsqrt(jnp.mean(x * x, axis=-1, keepdims=True) + RMS_EPS)
        xn = x * r
        gain = g_ref[...]
        err = xn * gain - t_ref[...]
        dy = err * (1.0 / d)
        dxn = dy * gain
        dx = r * (dxn - xn * jnp.mean(dxn * xn, axis=-1, keepdims=True))
        dx_ref[...] = dx
        _resid_grad(dx, i, t // tr, yx_ref, gate_ref, dyx_ref, gsum_ref)

        @pl.when(i == 0)
        def _():
            loss_ref[...] = jnp.zeros_like(loss_ref)
            dg_ref[...] = jnp.zeros_like(dg_ref)

        loss_ref[...] += 0.5 * jnp.sum(jnp.mean(err * err, axis=-1, keepdims=True))
        dg_ref[0:1, :] += jnp.sum(dy * xn, axis=0, keepdims=True)

    row = pl.BlockSpec((tr, d), lambda i: (i, 0))
    seg = pl.BlockSpec((None, 8, d), lambda i: (0, 0, 0))
    return _pc(
        body, name="loss_head", grid=(t // tr,),
        in_specs=[row, row, pl.BlockSpec((1, d), lambda i: (0, 0)), row, seg],
        out_specs=[pl.BlockSpec((8, LANES), lambda i: (0, 0)), row, pl.BlockSpec((8, d), lambda i: (0, 0)), row, seg],
        out_shape=[jax.ShapeDtypeStruct((8, LANES), F32), jax.ShapeDtypeStruct((t, d), F32),
                   jax.ShapeDtypeStruct((8, d), F32), jax.ShapeDtypeStruct((t, d), BF16),
                   jax.ShapeDtypeStruct((1, 8, d), F32)],
        compiler_params=_params("arbitrary"),
    )(xs, target, g, yx, mod)


def _proj_in(h, w, layer, width, name, blocks=None, dtype=F32):
    t, d = h.shape
    n8 = w.shape[-1]
    first, count = blocks if blocks is not None else (0, N_DEV)
    per_part = width // n8
    tm = _tile(t, 1152)

    def body(a_ref, b_ref, o_ref):
        o_ref[...] = _dot(a_ref[...], b_ref[...]).astype(dtype)

    return _pc(
        body, name=name, grid=(t // tm, count),
        in_specs=[pl.BlockSpec((tm, d), lambda i, j: (i, 0)),
                  pl.BlockSpec((None, None, d, n8), lambda i, j: (first + j, layer, 0, 0))],
        out_specs=pl.BlockSpec((None, tm, n8), lambda i, j: (j // per_part, i, j % per_part)),
        out_shape=jax.ShapeDtypeStruct((count // per_part, t, width), dtype),
        compiler_params=_params("parallel", "parallel"),
    )(h, w)


def _proj_out(z, w, res, mod, tm, seg_tiles, name, nxt=None):
    t, k = z.shape
    d = w.shape[1]

    def body(z_ref, w_ref, res_ref, mod_ref, *rest):
        yx = _dot(z_ref[...], w_ref[...])
        x = res_ref[...] + mod_ref[2:3, :] * yx
        if nxt is None:
            yx_ref, x_ref = rest
        else:
            g_ref, nmod_ref, yx_ref, x_ref, h_ref = rest
            r = lax.rsqrt(jnp.mean(x * x, axis=-1, keepdims=True) + RMS_EPS)
            h_ref[...] = (((x * r) * g_ref[...]) * (1.0 + nmod_ref[1:2, :]) + nmod_ref[0:1, :]).astype(BF16)
        yx_ref[...] = yx
        x_ref[...] = x

    tile = pl.BlockSpec((tm, d), lambda i: (i, 0))
    seg = pl.BlockSpec((None, 8, d), lambda i: (i // seg_tiles, 0, 0))
    in_specs = [pl.BlockSpec((tm, k), lambda i: (i, 0)), pl.BlockSpec((k, d), lambda i: (0, 0)), tile, seg]
    out_specs = [tile, tile]
    out_shape = [jax.ShapeDtypeStruct((t, d), F32), jax.ShapeDtypeStruct((t, d), F32)]
    args = [z, w, res, mod]
    if nxt is not None:
        in_specs += [pl.BlockSpec((1, d), lambda i: (0, 0)), seg]
        out_specs.append(tile)
        out_shape.append(jax.ShapeDtypeStruct((t, d), BF16))
        args += list(nxt)
    return _pc(
        body, name=name, grid=(t // tm,), in_specs=in_specs, out_specs=out_specs, out_shape=out_shape,
        compiler_params=_params("parallel"),
    )(*args)


def _proj_out_dz(dyx, w, name):
    t, d = dyx.shape
    width = w.shape[0]
    tm, tn = _tile(t, 1024), _tile(width, 512)

    def body(a_ref, w_ref, o_ref):
        o_ref[...] = _dot(a_ref[...], w_ref[...], 1, 1)

    return _pc(
        body, name=name, grid=(t // tm, width // tn),
        in_specs=[pl.BlockSpec((tm, d), lambda i, j: (i, 0)), pl.BlockSpec((tn, d), lambda i, j: (j, 0))],
        out_specs=pl.BlockSpec((tm, tn), lambda i, j: (i, j)),
        out_shape=jax.ShapeDtypeStruct((t, width), F32),
        compiler_params=_params("parallel", "parallel"),
    )(dyx, w)


def _proj_in_dh(dpre, w, layer, name, after=None):
    parts, t, width = dpre.shape
    d, n8 = w.shape[-2:]
    per_part = width // n8
    tm, tn = _tile(t, 1152), _tile(d, 512)

    def body(a_ref, w_ref, *rest):
        o_ref = rest[-1]
        part = _dot(a_ref[:, 0:n8], w_ref[0], 1, 1)
        for s in range(1, per_part):
            part += _dot(a_ref[:, s * n8:(s + 1) * n8], w_ref[s], 1, 1)

        @pl.when(pl.program_id(2) == 0)
        def _():
            o_ref[...] = part

        @pl.when(pl.program_id(2) != 0)
        def _():
            o_ref[...] += part

    extra = [] if after is None else [after]
    return _pc(
        body, name=name, grid=(t // tm, d // tn, parts),
        in_specs=[pl.BlockSpec((None, tm, width), lambda i, j, k: (k, i, 0)),
                  pl.BlockSpec((per_part, None, tn, n8), lambda i, j, k: (k, layer, j, 0))] + [ANY] * len(extra),
        out_specs=pl.BlockSpec((tm, tn), lambda i, j, k: (i, j)),
        out_shape=jax.ShapeDtypeStruct((t, d), F32),
        compiler_params=_params("parallel", "parallel", "arbitrary"),
    )(dpre, w, *extra)


def _transposed(a_ref):
    return a_ref[...].T


def _grad_w_in(h, dpre, n8, name):
    t, d = h.shape
    parts, _, width = dpre.shape
    per_part = width // n8
    tm, tk = _tile(d, 512), _tile(t, 1152)
    nk = t // tk

    def body(a_ref, b_ref, o_ref, acc_ref):
        k = pl.program_id(1)

        @pl.when(k == 0)
        def _():
            acc_ref[...] = jnp.zeros_like(acc_ref)

        at = _transposed(a_ref)
        for p in range(parts):
            r = _dot(at, b_ref[p])
            for s in range(per_part):
                acc_ref[p * per_part + s] += r[:, s * n8:(s + 1) * n8]

        @pl.when(k == nk - 1)
        def _():
            o_ref[...] = acc_ref[...].astype(BF16)

    return _pc(
        body, name=name, grid=(d // tm, nk),
        in_specs=[pl.BlockSpec((tk, tm), lambda i, k: (k, i)), pl.BlockSpec((parts, tk, width), lambda i, k: (0, k, 0))],
        out_specs=pl.BlockSpec((parts * per_part, tm, n8), lambda i, k: (0, i, 0)),
        out_shape=jax.ShapeDtypeStruct((parts * per_part, d, n8), BF16),
        scratch_shapes=[pltpu.VMEM((parts * per_part, tm, n8), F32)],
        compiler_params=_params("parallel", "arbitrary"),
    )(h, dpre)


def _grad_w_out(z, dyx, name):
    width = z.shape[1]
    t, d = dyx.shape
    tm, tk = _tile(width, 512), _tile(t, 1152)
    nk = t // tk

    def body(a_ref, b_ref, o_ref, acc_ref):
        k = pl.program_id(1)

        @pl.when(k == 0)
        def _():
            acc_ref[...] = jnp.zeros_like(acc_ref)

        acc_ref[...] += _dot(_transposed(a_ref), b_ref[...])

        @pl.when(k == nk - 1)
        def _():
            o_ref[...] = acc_ref[...].astype(BF16)

    return _pc(
        body, name=name, grid=(width // tm, nk),
        in_specs=[pl.BlockSpec((tk, tm), lambda i, k: (k, i)), pl.BlockSpec((tk, d), lambda i, k: (k, 0))],
        out_specs=pl.BlockSpec((tm, d), lambda i, k: (i, 0)),
        out_shape=jax.ShapeDtypeStruct((width, d), BF16),
        scratch_shapes=[pltpu.VMEM((tm, d), F32)],
        compiler_params=_params("parallel", "arbitrary"),
    )(z, dyx)


def _shift(v, k):
    n = v.shape[0]
    return pltpu.roll(v, k % n, 0)


def _window_sum(v, win):
    s = v + _shift(v, 1)
    step = 1
    while 2 * step < win:
        s = _shift(s, step) + _shift(s, -step)
        step *= 2
    return s


def _window_count(base, seg_len, win, shape):
    t = base + lax.broadcasted_iota(jnp.int32, shape, 0)
    hi = jnp.minimum(t + win // 2, seg_len)
    lo = jnp.maximum(t - win // 2, 0)
    return (hi - lo).astype(F32)


def _pad_offsets(segs):
    return [HALO * (s + 1) + st for s, (st, _) in enumerate(segs)]


def _for_chunks(segs, fn):
    offs = _pad_offsets(segs)
    for s, (st, ln) in enumerate(segs):
        def step(ci, carry, s=s, st=st, ln=ln):
            fn(s, st, ln, offs[s], pl.multiple_of(ci * CHUNK, CHUNK))
            return carry
        lax.fori_loop(0, ln // CHUNK, step, 0)


def _pool_fwd(pre, w_grp, scale, segs, name):
    _, t, width = pre.shape
    grp = width // len(POOL_WINDOWS)
    padded = t + HALO * (len(segs) + 1)

    def group(win, pre_ref, w_ref, sc_ref, z_ref, diff_ref, pad_ref):
        pad_ref[...] = jnp.zeros_like(pad_ref)

        def fill(s, st, ln, off, b):
            pad_ref[pl.ds(off + b, CHUNK), :] = pre_ref[0, pl.ds(st + b, CHUNK), :]

        _for_chunks(segs, fill)

        def mix(s, st, ln, off, b):
            ext = pad_ref[pl.ds(off - HALO + b, CHUNK + 2 * HALO), :]
            total = _window_sum(ext, win)[HALO:HALO + CHUNK]
            u = pre_ref[0, pl.ds(st + b, CHUNK), :]
            diff = (total / _window_count(b, ln, win, u.shape) - u).astype(BF16)
            mixed = _dot(diff, w_ref[...])
            gate = _silu(pre_ref[1, pl.ds(st + b, CHUNK), :])
            z_ref[pl.ds(st + b, CHUNK), :] = (mixed * sc_ref[...] * gate).astype(BF16)
            diff_ref[pl.ds(st + b, CHUNK), :] = diff

        _for_chunks(segs, mix)

    def body(pre_ref, w_ref, sc_ref, z_ref, diff_ref, pad_ref):
        gi = pl.program_id(0)
        for widx, win in enumerate(POOL_WINDOWS):
            @pl.when(gi == widx)
            def _(win=win):
                group(win, pre_ref, w_ref, sc_ref, z_ref, diff_ref, pad_ref)

    col = pl.BlockSpec((t, grp), lambda g: (0, g))
    return _pc(
        body, name=name, grid=(len(POOL_WINDOWS),),
        in_specs=[pl.BlockSpec((2, t, grp), lambda g: (0, 0, g)), pl.BlockSpec((None, grp, grp), lambda g: (g, 0, 0)),
                  pl.BlockSpec((1, grp), lambda g: (0, g))],
        out_specs=[col, col],
        out_shape=[jax.ShapeDtypeStruct((t, width), BF16), jax.ShapeDtypeStruct((t, width), BF16)],
        scratch_shapes=[pltpu.VMEM((padded, grp), F32)],
        compiler_params=_params("parallel"),
    )(pre, w_grp, scale)


def _pool_bwd(dz, diff, pre, w_grp, scale, segs, name):
    _, t, width = pre.shape
    grp = width // len(POOL_WINDOWS)
    padded = t + HALO * (len(segs) + 1)

    def group(win, dz_ref, diff_ref, pre_ref, w_ref, sc_ref, dpre_ref, dw_ref, dsc_ref, pad_ref, dd_ref):
        pad_ref[...] = jnp.zeros_like(pad_ref)
        dw_ref[...] = jnp.zeros_like(dw_ref)
        dsc_ref[...] = jnp.zeros_like(dsc_ref)

        def first(s, st, ln, off, b):
            rows = pl.ds(st + b, CHUNK)
            diff_v = diff_ref[rows, :]
            mixed = _dot(diff_v, w_ref[...])
            g = pre_ref[1, rows, :]
            sg = _silu(g)
            dzv = dz_ref[rows, :]
            dmixed = (dzv * sc_ref[...] * sg).astype(BF16)
            dsc_ref[...] += jnp.sum(dzv * mixed * sg, axis=0, keepdims=True)
            dpre_ref[1, rows, :] = (dzv * mixed * sc_ref[...] * _dsilu(g)).astype(BF16)
            ddiff = _dot(dmixed, w_ref[...], 1, 1)
            dw_ref[...] += _dot(diff_v, dmixed, 0, 0)
            dd_ref[rows, :] = ddiff
            pad_ref[pl.ds(off + b, CHUNK), :] = ddiff / _window_count(b, ln, win, ddiff.shape)

        _for_chunks(segs, first)

        def second(s, st, ln, off, b):
            rows = pl.ds(st + b, CHUNK)
            ext = pad_ref[pl.ds(off - HALO + b, CHUNK + 2 * HALO), :]
            total = _shift(_window_sum(ext, win), -1)[HALO:HALO + CHUNK]
            dpre_ref[0, rows, :] = (total - dd_ref[rows, :]).astype(BF16)

        _for_chunks(segs, second)

    def body(dz_ref, diff_ref, pre_ref, w_ref, sc_ref, dpre_ref, dw_ref, dsc_ref, pad_ref, dd_ref):
        gi = pl.program_id(0)
        for widx, win in enumerate(POOL_WINDOWS):
            @pl.when(gi == widx)
            def _(win=win):
                group(win, dz_ref, diff_ref, pre_ref, w_ref, sc_ref, dpre_ref, dw_ref, dsc_ref, pad_ref, dd_ref)

    col = pl.BlockSpec((t, grp), lambda g: (0, g))
    both = pl.BlockSpec((2, t, grp), lambda g: (0, 0, g))
    wspec = pl.BlockSpec((None, grp, grp), lambda g: (g, 0, 0))
    sspec = pl.BlockSpec((1, grp), lambda g: (0, g))
    return _pc(
        body, name=name, grid=(len(POOL_WINDOWS),),
        in_specs=[col, col, both, wspec, sspec],
        out_specs=[both, wspec, sspec],
        out_shape=[jax.ShapeDtypeStruct((2, t, width), BF16), jax.ShapeDtypeStruct((len(POOL_WINDOWS), grp, grp), F32),
                   jax.ShapeDtypeStruct((1, width), F32)],
        scratch_shapes=[pltpu.VMEM((padded, grp), F32), pltpu.VMEM((t, grp), F32)],
        compiler_params=_params("parallel"),
    )(dz, diff, pre, w_grp, scale)


def _conv_fwd(pre, dw, db, name):
    _, t, width = pre.shape
    cb = LANES
    segs = [(0, t)]

    def body(pre_ref, dw_ref, db_ref, z_ref, pad_ref):
        pad_ref[...] = jnp.zeros_like(pad_ref)

        def fill(s, st, ln, off, b):
            rows = pl.ds(b, CHUNK)
            pad_ref[pl.ds(off + b, CHUNK), :] = pre_ref[1, rows, :] * pre_ref[2, rows, :]

        _for_chunks(segs, fill)

        def mix(s, st, ln, off, b):
            rows = pl.ds(b, CHUNK)
            ext = pad_ref[pl.ds(off - HALO + b, CHUNK + 2 * HALO), :]
            conv = (dw_ref[0:1, :] * _shift(ext, 1) + dw_ref[1:2, :] * ext + dw_ref[2:3, :] * _shift(ext, -1))
            conv = conv[HALO:HALO + CHUNK] + db_ref[...]
            y = pre_ref[0, rows, :] * conv
            z_ref[rows, :] = (y * _silu(pre_ref[3, rows, :])).astype(BF16)

        _for_chunks(segs, mix)

    return _pc(
        body, name=name, grid=(width // cb,),
        in_specs=[pl.BlockSpec((4, t, cb), lambda j: (0, 0, j)), pl.BlockSpec((8, cb), lambda j: (0, j)),
                  pl.BlockSpec((1, cb), lambda j: (0, j))],
        out_specs=pl.BlockSpec((t, cb), lambda j: (0, j)),
        out_shape=jax.ShapeDtypeStruct((t, width), BF16),
        scratch_shapes=[pltpu.VMEM((t + 2 * HALO, cb), F32)],
        compiler_params=_params("parallel"),
    )(pre, dw, db)


def _conv_bwd(dz, pre, dw, db, name):
    _, t, width = pre.shape
    cb = LANES
    segs = [(0, t)]

    def body(dz_ref, pre_ref, dw_ref, db_ref, dpre_ref, ddw_ref, ddb_ref, pad_a, pad_c):
        pad_a[...] = jnp.zeros_like(pad_a)
        pad_c[...] = jnp.zeros_like(pad_c)
        ddw_ref[...] = jnp.zeros_like(ddw_ref)
        ddb_ref[...] = jnp.zeros_like(ddb_ref)

        def fill(s, st, ln, off, b):
            rows = pl.ds(b, CHUNK)
            pad_a[pl.ds(off + b, CHUNK), :] = pre_ref[1, rows, :] * pre_ref[2, rows, :]

        _for_chunks(segs, fill)

        def first(s, st, ln, off, b):
            rows = pl.ds(b, CHUNK)
            ext = pad_a[pl.ds(off - HALO + b, CHUNK + 2 * HALO), :]
            prev, nxt = _shift(ext, 1)[HALO:HALO + CHUNK], _shift(ext, -1)[HALO:HALO + CHUNK]
            here = ext[HALO:HALO + CHUNK]
            conv = dw_ref[0:1, :] * prev + dw_ref[1:2, :] * here + dw_ref[2:3, :] * nxt + db_ref[...]
            bg, g = pre_ref[0, rows, :], pre_ref[3, rows, :]
            dzv = dz_ref[rows, :]
            dy = dzv * _silu(g)
            dpre_ref[3, rows, :] = (dzv * (bg * conv) * _dsilu(g)).astype(BF16)
            dpre_ref[0, rows, :] = (dy * conv).astype(BF16)
            dconv = dy * bg
            pad_c[pl.ds(off + b, CHUNK), :] = dconv
            ddw_ref[0:1, :] += jnp.sum(dconv * prev, axis=0, keepdims=True)
            ddw_ref[1:2, :] += jnp.sum(dconv * here, axis=0, keepdims=True)
            ddw_ref[2:3, :] += jnp.sum(dconv * nxt, axis=0, keepdims=True)
            ddb_ref[0:1, :] += jnp.sum(dconv, axis=0, keepdims=True)

        _for_chunks(segs, first)

        def second(s, st, ln, off, b):
            rows = pl.ds(b, CHUNK)
            ext = pad_c[pl.ds(off - HALO + b, CHUNK + 2 * HALO), :]
            da = (dw_ref[0:1, :] * _shift(ext, -1) + dw_ref[1:2, :] * ext + dw_ref[2:3, :] * _shift(ext, 1))
            da = da[HALO:HALO + CHUNK]
            dpre_ref[1, rows, :] = (da * pre_ref[2, rows, :]).astype(BF16)
            dpre_ref[2, rows, :] = (da * pre_ref[1, rows, :]).astype(BF16)

        _for_chunks(segs, second)

    quad = pl.BlockSpec((4, t, cb), lambda j: (0, 0, j))
    rows8 = pl.BlockSpec((8, cb), lambda j: (0, j))
    return _pc(
        body, name=name, grid=(width // cb,),
        in_specs=[pl.BlockSpec((t, cb), lambda j: (0, j)), quad, rows8, pl.BlockSpec((1, cb), lambda j: (0, j))],
        out_specs=[quad, rows8, rows8],
        out_shape=[jax.ShapeDtypeStruct((4, t, width), BF16), jax.ShapeDtypeStruct((8, width), F32),
                   jax.ShapeDtypeStruct((8, width), F32)],
        scratch_shapes=[pltpu.VMEM((t + 2 * HALO, cb), F32), pltpu.VMEM((t + 2 * HALO, cb), F32)],
        compiler_params=_params("parallel"),
    )(dz, pre, dw, db)


PAIR_TILES = 2 * WIN_ROWS - 2


def _pair_geometry():
    lane = lax.broadcasted_iota(jnp.int32, (GRID_W, LANES), 1)
    qcol = lax.broadcasted_iota(jnp.int32, (GRID_W, LANES), 0)
    low = lane < GRID_W
    kcol = jnp.where(low, lane, lane - GRID_W)
    start = jnp.clip(qcol - WIN_COLS // 2, 0, GRID_W - WIN_COLS)
    inside = (kcol >= start) & (kcol < start + WIN_COLS)
    return low, inside


def _bias_tiles(rpb_ref, rows_ref, tiles_ref, inside):
    for h in range(2):
        rows = rpb_ref[h]
        rows_ref[h] = (pltpu.roll(rows, LANES - (WIN_COLS - 1), 1)
                       + pltpu.roll(pltpu.roll(rows, GRID_W - (WIN_COLS - 1), 1), 2 * WIN_ROWS - 1, 0))
        for t in range(PAIR_TILES):
            both = jnp.broadcast_to(rows_ref[h, t:t + 1, :], (GRID_W, LANES))
            tiles_ref[h, t] = jnp.where(inside, pltpu.roll(both, 0, 1, stride=1, stride_axis=0), MASKED)


def _bias_tiles_grad(dtiles_ref, drpb_ref):
    n = PAIR_TILES * GRID_W
    qcol = lax.broadcasted_iota(jnp.int32, (n, LANES), 0) & (GRID_W - 1)
    lane = lax.broadcasted_iota(jnp.int32, (1, LANES), 1)
    zero = jnp.zeros((1, LANES), F32)
    for h in range(2):
        v = pltpu.roll(dtiles_ref[h].reshape(n, LANES), WIN_COLS - 1, 1)
        for bit in range(6):
            v = jnp.where((qcol >> bit) & 1 == 1, pltpu.roll(v, LANES - (1 << bit), 1), v)
        sums = [jnp.sum(v[t * GRID_W:(t + 1) * GRID_W], axis=0, keepdims=True) for t in range(PAIR_TILES)]
        for r in range(2 * WIN_ROWS):
            here = sums[r] if r < PAIR_TILES else zero
            prev = pltpu.roll(sums[r - 1], GRID_W, 1) if 1 <= r <= PAIR_TILES else zero
            drpb_ref[h, r:r + 1, :] = jnp.where(lane < 2 * WIN_COLS - 1, here + prev, 0.0)


def _attn_rows(r, n_rows):
    first = jnp.clip(r - WIN_ROWS // 2, 0, n_rows - WIN_ROWS)
    return first, first - r + WIN_ROWS - 1


def _softmax(s_loc, s_ctx):
    m = jnp.maximum(jnp.max(s_loc, axis=-1, keepdims=True), jnp.max(s_ctx, axis=-1, keepdims=True))
    e_loc, e_ctx = jnp.exp(s_loc - m), jnp.exp(s_ctx - m)
    inv = 1.0 / (jnp.sum(e_loc, axis=-1, keepdims=True) + jnp.sum(e_ctx, axis=-1, keepdims=True))
    return e_loc * inv, e_ctx * inv


def _pair_bias(tiles_ref, h, j):
    return jnp.concatenate([tiles_ref[h, j + 2 * m] for m in range(WIN_ROWS // 2)], axis=1)


ROWS_PER_STEP = 4


def _attn_items(step, n_rows, q_ref, low):
    items = []
    for u in range(ROWS_PER_STEP):
        r = step * ROWS_PER_STEP + u
        first, j = _attn_rows(r, n_rows)
        rows = pl.ds(pl.multiple_of(r * GRID_W, GRID_W), GRID_W)
        keys = pl.ds(pl.multiple_of(first * GRID_W, GRID_W), WIN_ROWS * GRID_W)
        q = (q_ref[rows, :].astype(F32) * HEAD_DIM ** -0.5).astype(BF16)
        zero = jnp.zeros_like(q)
        items.append((rows, keys, j, 0, jnp.where(low, q, zero)))
        items.append((rows, keys, j, 1, jnp.where(low, zero, q)))
    return items


def _attn_fwd(qkv, gate, rpb, seq):
    _, t, width = qkv.shape
    n_rows = seq // GRID_W
    n_ctx = t - seq
    blk = WIN_ROWS * GRID_W

    def body(q_ref, k_ref, v_ref, g_ref, rpb_ref, z_ref, o_ref, rows_ref, tiles_ref):
        low, inside = _pair_geometry()
        _bias_tiles(rpb_ref, rows_ref, tiles_ref, inside)
        ctx = pl.ds(seq, n_ctx)

        def step(i, carry):
            items = _attn_items(i, n_rows, q_ref, low)
            k_ctx, v_ctx = k_ref[ctx, :], v_ref[ctx, :]
            scores = [(_dot(q, k_ref[keys, :], 1, 1) + _pair_bias(tiles_ref, h, j), _dot(q, k_ctx, 1, 1))
                      for _, keys, j, h, q in items]
            probs = [_softmax(s_loc, s_ctx) for s_loc, s_ctx in scores]
            outs = [_dot(p_loc.astype(BF16), v_ref[keys, :]) + _dot(p_ctx.astype(BF16), v_ctx)
                    for (_, keys, _, _, _), (p_loc, p_ctx) in zip(items, probs)]
            for u in range(ROWS_PER_STEP):
                rows = items[2 * u][0]
                o = jnp.where(low, outs[2 * u], outs[2 * u + 1])
                o_ref[rows, :] = o
                z_ref[rows, :] = (o * _silu(g_ref[rows, :])).astype(BF16)
            return carry

        lax.fori_loop(0, n_rows // ROWS_PER_STEP, step, 0)

    def part(p):
        return pl.BlockSpec((None, t, LANES), lambda h: (p, 0, h))

    out = pl.BlockSpec((seq, LANES), lambda h: (0, h))
    return _pc(
        body, name="attn_fwd", grid=(width // LANES,),
        in_specs=[part(0), part(1), part(2), part(0), pl.BlockSpec((2, 2 * WIN_ROWS, LANES), lambda h: (h, 0, 0))],
        out_specs=[out, out],
        out_shape=[jax.ShapeDtypeStruct((seq, width), BF16), jax.ShapeDtypeStruct((seq, width), F32)],
        scratch_shapes=[pltpu.VMEM((2, 2 * WIN_ROWS, LANES), F32), pltpu.VMEM((2, PAIR_TILES, GRID_W, LANES), F32)],
        compiler_params=_params("parallel"),
    )(qkv, qkv, qkv, gate, rpb)


def _attn_bwd(qkv, gate, o, dz, rpb, seq):
    _, t, width = qkv.shape
    n_rows = seq // GRID_W
    n_ctx = t - seq
    blk = WIN_ROWS * GRID_W
    heads = 2 * width // LANES

    def body(q_ref, k_ref, v_ref, g_ref, o_ref, dz_ref, rpb_ref, dpre_ref, drpb_ref,
             rows_ref, tiles_ref, dtiles_ref, dk_ref, dv_ref):
        low, inside = _pair_geometry()
        _bias_tiles(rpb_ref, rows_ref, tiles_ref, inside)
        dtiles_ref[...] = jnp.zeros_like(dtiles_ref)
        dk_ref[...] = jnp.zeros_like(dk_ref)
        dv_ref[...] = jnp.zeros_like(dv_ref)
        ctx = pl.ds(seq, n_ctx)

        def step(i, carry):
            items = _attn_items(i, n_rows, q_ref, low)
            k_ctx, v_ctx = k_ref[ctx, :], v_ref[ctx, :]
            d_outs = []
            for u in range(ROWS_PER_STEP):
                rows = items[2 * u][0]
                g = g_ref[rows, :]
                dzv = dz_ref[rows, :]
                dpre_ref[3, rows, :] = (dzv * o_ref[rows, :] * _dsilu(g)).astype(BF16)
                d_o = (dzv * _silu(g)).astype(BF16)
                zero = jnp.zeros_like(d_o)
                d_outs += [jnp.where(low, d_o, zero), jnp.where(low, zero, d_o)]
            scores = [(_dot(q, k_ref[keys, :], 1, 1) + _pair_bias(tiles_ref, h, j), _dot(q, k_ctx, 1, 1))
                      for _, keys, j, h, q in items]
            dprobs = [(_dot(doh, v_ref[keys, :], 1, 1), _dot(doh, v_ctx, 1, 1))
                      for (_, keys, _, _, _), doh in zip(items, d_outs)]
            probs = [_softmax(s_loc, s_ctx) for s_loc, s_ctx in scores]
            dscores = []
            for (p_loc, p_ctx), (dp_loc, dp_ctx) in zip(probs, dprobs):
                delta = (jnp.sum(p_loc * dp_loc, axis=-1, keepdims=True)
                         + jnp.sum(p_ctx * dp_ctx, axis=-1, keepdims=True))
                dscores.append((p_loc * (dp_loc - delta), p_ctx * (dp_ctx - delta)))
            dqs = [_dot(ds_loc.astype(BF16), k_ref[keys, :]) + _dot(ds_ctx.astype(BF16), k_ctx)
                   for (_, keys, _, _, _), (ds_loc, ds_ctx) in zip(items, dscores)]
            for u in range(ROWS_PER_STEP):
                rows = items[2 * u][0]
                dpre_ref[0, rows, :] = (jnp.where(low, dqs[2 * u], dqs[2 * u + 1]) * HEAD_DIM ** -0.5).astype(BF16)
            for (_, keys, j, h, q), doh, (p_loc, p_ctx), (ds_loc, ds_ctx) in zip(items, d_outs, probs, dscores):
                dk_ref[keys, :] += _dot(ds_loc.astype(BF16), q, 0, 0)
                dk_ref[ctx, :] += _dot(ds_ctx.astype(BF16), q, 0, 0)
                dv_ref[keys, :] += _dot(p_loc.astype(BF16), doh, 0, 0)
                dv_ref[ctx, :] += _dot(p_ctx.astype(BF16), doh, 0, 0)
                for m in range(WIN_ROWS // 2):
                    dtiles_ref[h, j + 2 * m] += ds_loc[:, m * LANES:(m + 1) * LANES]
            return carry

        lax.fori_loop(0, n_rows // ROWS_PER_STEP, step, 0)
        dpre_ref[1] = dk_ref[...].astype(BF16)
        dpre_ref[2] = dv_ref[...].astype(BF16)
        dpre_ref[0, ctx, :] = jnp.zeros((n_ctx, LANES), BF16)
        dpre_ref[3, ctx, :] = jnp.zeros((n_ctx, LANES), BF16)
        _bias_tiles_grad(dtiles_ref, drpb_ref)

    def part(p):
        return pl.BlockSpec((None, t, LANES), lambda h: (p, 0, h))

    lat = pl.BlockSpec((seq, LANES), lambda h: (0, h))
    rspec = pl.BlockSpec((2, 2 * WIN_ROWS, LANES), lambda h: (h, 0, 0))
    tiles = pltpu.VMEM((2, PAIR_TILES, GRID_W, LANES), F32)
    return _pc(
        body, name="attn_bwd", grid=(width // LANES,),
        in_specs=[part(0), part(1), part(2), part(0), lat, lat, rspec],
        out_specs=[pl.BlockSpec((4, t, LANES), lambda h: (0, 0, h)), rspec],
        out_shape=[jax.ShapeDtypeStruct((4, t, width), BF16), jax.ShapeDtypeStruct((heads, 2 * WIN_ROWS, LANES), F32)],
        scratch_shapes=[pltpu.VMEM((2, 2 * WIN_ROWS, LANES), F32), tiles, tiles,
                        pltpu.VMEM((t, LANES), F32), pltpu.VMEM((t, LANES), F32)],
        compiler_params=_params("parallel"),
    )(qkv, qkv, qkv, gate, o, dz, rpb)


def _adamw(w, m, v, parts, name, mult=None):
    rows, cols = w.shape
    tr = _tile(rows, max(8, 131072 // cols), 8)
    n_parts = len(parts)
    c1 = 1.0 - ADAM_B1 ** ADAM_STEP
    c2 = 1.0 - ADAM_B2 ** ADAM_STEP

    def body(*refs):
        w_ref, m_ref, v_ref = refs[:3]
        part_refs = refs[3:3 + n_parts]
        rest = refs[3 + n_parts:]
        g = part_refs[0][...].astype(F32)
        for p in part_refs[1:]:
            g = g + p[...].astype(F32)
        if mult is not None:
            g = g * rest[0][...]
            rest = rest[1:]
        g_ref, d_ref, nm_ref, nv_ref = rest
        m2 = ADAM_B1 * m_ref[...] + (1.0 - ADAM_B1) * g
        v2 = ADAM_B2 * v_ref[...] + (1.0 - ADAM_B2) * (g * g)
        m_hat = m2 / c1
        v_hat = v2 / c2
        g_ref[...] = g
        d_ref[...] = -ADAM_LR * (m_hat / (jnp.sqrt(v_hat) + ADAM_EPS) + ADAM_WD * w_ref[...])
        nm_ref[...] = m2
        nv_ref[...] = v2

    tile = pl.BlockSpec((tr, cols), lambda i: (i, 0))
    in_specs, args = [tile, tile, tile], [w, m, v]
    for p in parts:
        if isinstance(p, tuple):
            arr, k = p
            in_specs.append(pl.BlockSpec((None, tr, cols), lambda i, k=k: (k, i, 0)))
            args.append(arr)
        else:
            in_specs.append(tile)
            args.append(p)
    if mult is not None:
        in_specs.append(tile)
        args.append(mult)
    shape = jax.ShapeDtypeStruct((rows, cols), F32)
    return _pc(
        body, name=name, grid=(rows // tr,), in_specs=in_specs, out_specs=[tile] * 4, out_shape=[shape] * 4,
        compiler_params=_params("parallel"),
    )(*args)


def _rows128(a):
    flat = a.reshape(-1)
    pad = (-flat.shape[0]) % LANES
    if pad:
        flat = jnp.concatenate([flat, jnp.zeros((pad,), flat.dtype)])
    return flat.reshape(-1, LANES)


def _pad_rows(a, mult=8):
    pad = (-a.shape[0]) % mult
    if pad:
        a = jnp.concatenate([a, jnp.zeros((pad,) + a.shape[1:], a.dtype)], axis=0)
    return a


def kernel(x, c, ctx, c_ctx, norm_g, ada_w, ada_b, pool_w_in, pool_w_grp, pool_scale, pool_w_out, na_w_in, na_rpb, na_w_out, conv_w_in, conv_dw, conv_db, conv_w_out, final_g, loss_target, m_c_ctx, m_norm_g, m_ada_w, m_ada_b, m_pool_w_in, m_pool_w_grp, m_pool_scale, m_pool_w_out, m_na_w_in, m_na_rpb, m_na_w_out, m_conv_w_in, m_conv_dw, m_conv_db, m_conv_w_out, m_final_g, v_c_ctx, v_norm_g, v_ada_w, v_ada_b, v_pool_w_in, v_pool_w_grp, v_pool_scale, v_pool_w_out, v_na_w_in, v_na_rpb, v_na_w_out, v_conv_w_in, v_conv_dw, v_conv_db, v_conv_w_out, v_final_g):
    xi, yi, ci = _my_place()
    me = 4 * xi + 2 * yi + ci
    seq, d = x.shape[1], x.shape[2]
    n_ctx = ctx.shape[1]
    t_all = seq + n_ctx
    width = d
    depth = norm_g.shape[0]
    nb = ada_w.shape[2]
    shard = width // N_DEV
    d_rows = d // LANES
    tr = math.gcd(math.gcd(seq, n_ctx), 256)
    x_tiles = seq // tr

    n_pool = pool_scale.shape[0]
    n_grp = pool_w_grp.shape[1]
    grp = width // n_grp
    layer_weights = [[pool_w_in[0], pool_w_grp[0], pool_w_out[0]], [na_w_in[0], na_w_out[0]],
                     [conv_w_in[0], conv_w_out[0]], [pool_w_in[1], pool_w_grp[1], pool_w_out[1]]]
    in_flight, token = [], jnp.zeros((8, LANES), F32)
    for i, ws in enumerate(layer_weights):
        state = _exchange_start([w.astype(BF16) for w in ws], False, token, f"weights_start{i}")
        token = state[-1]
        in_flight.append(state)

    def landed_weight(i, t, after):
        return _exchange_wait(in_flight[i], False, after, f"weights_wait{i}_{t}", which=[t])[0]

    def as_in(w):
        return w[:, None]

    def as_grp(w):
        return w.transpose(1, 0, 2, 3).reshape(n_grp, grp, grp)

    def as_out(w):
        return w.reshape(width, d)

    small_in =_pad_rows(jnp.concatenate([_rows128(c), pool_scale, conv_dw[0], conv_db], axis=0))
    got = _gather_small(small_in, "gather_inputs")
    r0 = d_rows
    c_all = got[:, :r0].reshape(N_DEV, d)
    n_pool = pool_scale.shape[0]
    scale_full = got[:, r0:r0 + n_pool].transpose(1, 0, 2).reshape(n_pool, width)
    r1 = r0 + n_pool
    taps_full = _pad_rows(got[:, r1:r1 + 3].transpose(1, 0, 2).reshape(3, width))
    bias_full = got[:, r1 + 3:r1 + 4].transpose(1, 0, 2).reshape(1, width)

    cond = jnp.concatenate([c_all, c_ctx[None], jnp.zeros((7, d), F32)], axis=0)
    bias_mine = lax.dynamic_slice(ada_b, (0, me * nb), (depth, nb))
    mod_mine = _mod_fwd(cond, ada_w, bias_mine)
    mod_all = _gather_small(mod_mine.reshape(-1, LANES), "gather_mod")
    mod_all = mod_all.reshape(N_DEV, depth, 16, nb).transpose(1, 2, 0, 3).reshape(depth, 16, 3 * d)
    mod_x = lax.dynamic_index_in_dim(mod_all, me, 1, keepdims=False).reshape(depth, 3, d)
    mod_c = mod_all[:, 8].reshape(depth, 3, d)
    pad5 = jnp.zeros((depth, 5, d), F32)
    mod_x = jnp.concatenate([mod_x, pad5], axis=1)
    mod_c = jnp.concatenate([mod_c, pad5], axis=1)
    mods = [jnp.stack([mod_x[i], mod_c[i]]) if i < 2 else mod_x[i][None] for i in range(depth)]

    both = [(0, seq), (seq, n_ctx)]
    latent = [(0, seq)]

    def grp_slots(g):
        return g.reshape(n_grp, N_DEV, grp // N_DEV, grp).transpose(1, 0, 2, 3).reshape(N_DEV, -1, grp).astype(BF16)

    def send_grads(i, grads):
        return _exchange_start(grads, True, jnp.zeros((8, LANES), F32), f"grads_start{i}")

    xs0 = jnp.concatenate([x[0], ctx[0]], axis=0)
    h0 = _norm_fwd(xs0, norm_g[0:1] + token[0, 0], mods[0], tr, x_tiles, "norm_fwd0")
    pool_in_w0 = as_in(landed_weight(0, 0, h0))
    pre0 = _proj_in(h0, pool_in_w0, 0, width, "proj_in0")
    pool_grp_w0 = as_grp(landed_weight(0, 1, pre0))
    z0, diff0 = _pool_fwd(pre0, pool_grp_w0, scale_full[0:1], both, "pool_fwd0")
    pool_out_w0 = as_out(landed_weight(0, 2, z0))
    yx0, xs1, h1 = _proj_out(z0, pool_out_w0, xs0, mods[0], tr, x_tiles, "proj_out0", nxt=(norm_g[1:2], mods[1]))

    na_in_w = as_in(landed_weight(1, 0, h1))
    per_part = width // na_w_in.shape[2]
    qkv1 = _proj_in(h1, na_in_w, 0, width, "proj_in1_qkv", blocks=(0, 3 * per_part), dtype=BF16)
    gpre1 = _proj_in(h1, na_in_w, 0, width, "proj_in1_gate", blocks=(3 * per_part, per_part))
    rpb_rows = jnp.pad(na_rpb[0], ((0, 0), (0, 2 * WIN_ROWS - na_rpb.shape[2]), (0, LANES - na_rpb.shape[3])))
    z1, o1 = _attn_fwd(qkv1, gpre1, rpb_rows, seq)
    na_out_w = as_out(landed_weight(1, 1, z1))
    yx1, x2, h2 = _proj_out(z1, na_out_w, xs1, mods[1], tr, x_tiles, "proj_out1", nxt=(norm_g[2:3], mods[2]))

    conv_in_w = as_in(landed_weight(2, 0, h2))
    pre2 = _proj_in(h2, conv_in_w, 0, width, "proj_in2")
    z2 = _conv_fwd(pre2, taps_full, bias_full, "conv_fwd")
    conv_out_w = as_out(landed_weight(2, 1, z2))
    yx2, x3, h3 = _proj_out(z2, conv_out_w, x2, mods[2], tr, x_tiles, "proj_out2", nxt=(norm_g[3:4], mods[3]))

    pool_in_w3 = as_in(landed_weight(3, 0, h3))
    pre3 = _proj_in(h3, pool_in_w3, 0, width, "proj_in3")
    pool_grp_w3 = as_grp(landed_weight(3, 1, pre3))
    z3, diff3 = _pool_fwd(pre3, pool_grp_w3, scale_full[1:2], latent, "pool_fwd3")
    pool_out_w3 = as_out(landed_weight(3, 2, z3))
    yx3, x4 = _proj_out(z3, pool_out_w3, x3, mods[3], tr, x_tiles, "proj_out3")

    loss_part, dx4, d_final, dyx3, gate3 = _loss_head(x4, loss_target[0], final_g[None], yx3, mods[3], tr)
    loss = lax.psum(loss_part[0, 0], ("x", "y", "c"))

    dz3 = _proj_out_dz(dyx3, pool_out_w3, "proj_out_dz3")
    g_pool_out1 = _grad_w_out(z3, dyx3, "grad_w_out3")
    dpre3, g_grp1, g_scale1 = _pool_bwd(dz3, diff3, pre3, pool_grp_w3, scale_full[1:2], latent, "pool_bwd3")
    dh3 = _proj_in_dh(dpre3, pool_in_w3, 0, "proj_in_dh3")
    g_pool_in1 = _grad_w_in(h3, dpre3, pool_w_in.shape[2], "grad_w_in3")
    sent3 = send_grads(3, [g_pool_in1, grp_slots(g_grp1), g_pool_out1.reshape(N_DEV, shard, d)])
    dx3, norm3, dyx2, gate2 = _norm_bwd(x3, dh3, dx4, norm_g[3:4] + sent3[-1][0, 0], mods[3], tr, x_tiles, "norm_bwd3",
                                        below=(yx2, mods[2]))

    dz2 = _proj_out_dz(dyx2, conv_out_w, "proj_out_dz2")
    g_conv_out = _grad_w_out(z2, dyx2, "grad_w_out2")
    dpre2, g_taps, g_cbias = _conv_bwd(dz2, pre2, taps_full, bias_full, "conv_bwd")
    dh2 = _proj_in_dh(dpre2, conv_in_w, 0, "proj_in_dh2")
    g_conv_in = _grad_w_in(h2, dpre2, conv_w_in.shape[2], "grad_w_in2")
    sent2 = send_grads(2, [g_conv_in, g_conv_out.reshape(N_DEV, shard, d)])
    dx2, norm2, dyx1, gate1 = _norm_bwd(x2, dh2, dx3, norm_g[2:3] + sent2[-1][0, 0], mods[2], tr, x_tiles, "norm_bwd2",
                                        below=(yx1, mods[1][:1]))

    dz1 = _proj_out_dz(dyx1, na_out_w, "proj_out_dz1")
    g_na_out = _grad_w_out(z1, dyx1, "grad_w_out1")
    dpre1, g_rpb = _attn_bwd(qkv1, gpre1, o1, dz1, rpb_rows, seq)
    g_rpb = g_rpb[:, :na_rpb.shape[2], :na_rpb.shape[3]]
    dh1 = _proj_in_dh(dpre1, na_in_w, 0, "proj_in_dh1")
    g_na_in = _grad_w_in(h1, dpre1, na_w_in.shape[2], "grad_w_in1")
    sent1 = send_grads(1, [g_na_in, g_na_out.reshape(N_DEV, shard, d)])
    dxs1, norm1, dyx0, gate0 = _norm_bwd(xs1, dh1, dx2, norm_g[1:2] + sent1[-1][0, 0], mods[1], tr, x_tiles, "norm_bwd1",
                                         res_tiles=x_tiles, below=(yx0, mods[0]))

    dz0 = _proj_out_dz(dyx0, pool_out_w0, "proj_out_dz0")
    g_pool_out0 = _grad_w_out(z0, dyx0, "grad_w_out0")
    sent0a = _exchange_start([g_pool_out0.reshape(N_DEV, shard, d)], True, jnp.zeros((8, LANES), F32), "grads_start0a")
    dpre0, g_grp0, g_scale0 = _pool_bwd(dz0, diff0, pre0, pool_grp_w0, scale_full[0:1], both, "pool_bwd0")
    g_pool_in0 = _grad_w_in(h0, dpre0, pool_w_in.shape[2], "grad_w_in0")
    sent0b = _exchange_start([g_pool_in0, grp_slots(g_grp0)], True, sent0a[-1], "grads_start0b")
    dh0 = _proj_in_dh(dpre0, pool_in_w0, 0, "proj_in_dh0", after=sent0b[-1])
    dx0, norm0 = _norm_bwd(xs0, dh0, dxs1, norm_g[0:1], mods[0], tr, x_tiles, "norm_bwd0", out_tiles=x_tiles)
    grad_x = dx0[None]

    norms, gates = [norm0, norm1, norm2, norm3], [gate0, gate1, gate2, gate3]
    zero_d = jnp.zeros((d,), F32)
    dm_rows = [jnp.concatenate([norms[i][0, 0], norms[i][0, 1], gates[i][0, 0]]) for i in range(depth)]
    dm_rows.append(jnp.concatenate([norm0[1, 0], norm0[1, 1], gate0[1, 0]]))
    dm_rows.append(jnp.concatenate([norm1[1, 0], norm1[1, 1], zero_d]))
    dm_local = jnp.stack(dm_rows + [jnp.zeros((3 * d,), F32)] * 2)
    g_norm_part = jnp.stack([norm0[0, 2] + norm0[1, 2], norm1[0, 2] + norm1[1, 2], norm2[0, 2], norm3[0, 2]])
    g_scale_part = jnp.concatenate([g_scale0, g_scale1], axis=0)
    pieces = [_rows128(dm_local), _rows128(g_norm_part), _rows128(d_final[0]), _pad_rows(_rows128(g_rpb)),
              _rows128(g_scale_part), _rows128(g_taps[:3]), _rows128(g_cbias[0])]
    small_sent = _exchange_start([jnp.concatenate(pieces, axis=0)], False, jnp.zeros((8, LANES), F32), "small_grads_start")
    marks = np.cumsum([0] + [p.shape[0] for p in pieces])

    def big(parts, w, m, v, name):
        shape = w.shape
        view = (-1, shape[-1])
        parts = [(parts.reshape((N_DEV,) + w.reshape(view).shape), k) for k in range(N_DEV)]
        return [r.reshape(shape) for r in _adamw(w.reshape(view), m.reshape(view), v.reshape(view), parts, name)]

    in3, grp3, out3 = _exchange_wait(sent3, True, small_sent[-1], "grads_wait3")
    in2, out2 = _exchange_wait(sent2, True, small_sent[-1], "grads_wait2")
    in1, out1 = _exchange_wait(sent1, True, small_sent[-1], "grads_wait1")
    res = {}
    res["na_w_in"] = [r[None] for r in big(in1, na_w_in[0], m_na_w_in[0], v_na_w_in[0], "adamw_na_in")]
    res["na_w_out"] = [r[None] for r in big(out1, na_w_out[0], m_na_w_out[0], v_na_w_out[0], "adamw_na_out")]
    res["conv_w_in"] = [r[None] for r in big(in2, conv_w_in[0], m_conv_w_in[0], v_conv_w_in[0], "adamw_conv_in")]
    res["conv_w_out"] = [r[None] for r in big(out2, conv_w_out[0], m_conv_w_out[0], v_conv_w_out[0], "adamw_conv_out")]
    pool_in1 = big(in3, pool_w_in[1], m_pool_w_in[1], v_pool_w_in[1], "adamw_pool_in1")
    pool_grp1 = big(grp3, pool_w_grp[1], m_pool_w_grp[1], v_pool_w_grp[1], "adamw_pool_grp1")
    pool_out1 = big(out3, pool_w_out[1], m_pool_w_out[1], v_pool_w_out[1], "adamw_pool_out1")

    small_out = _exchange_wait(small_sent, False, pool_out1[0], "small_grads_wait")[0]

    def piece(k):
        return small_out[:, marks[k]:marks[k + 1]]

    dm_all = piece(0).reshape(N_DEV, 8, 3 * d).transpose(1, 0, 2)
    dm_mine = lax.dynamic_slice(dm_all, (0, 0, me * nb), (8, N_DEV, nb))
    g_ada_w, g_ada_b, cctx_part, dsilu_cond = _mod_bwd(cond, ada_w, dm_all, dm_mine)
    cctx_all = _gather_small(_rows128(cctx_part[0]), "gather_cctx")

    def my_shard(a, n):
        a = a.reshape(N_DEV, n, N_DEV, shard)
        return lax.dynamic_index_in_dim(a, me, 2, keepdims=False)

    zeros7 = lambda r: jnp.zeros((N_DEV - 1, r, LANES), F32)
    ada_b_rows = _rows128(g_ada_b)
    small_parts = jnp.concatenate([
        cctx_all, piece(1), jnp.concatenate([ada_b_rows[None], zeros7(ada_b_rows.shape[0])], axis=0), piece(2), piece(3),
        my_shard(piece(4), n_pool), my_shard(piece(5), 3), my_shard(piece(6), 1)], axis=1)
    n_small = small_parts.shape[1]
    small_parts = jnp.concatenate([small_parts, jnp.zeros((N_DEV, (-n_small) % 8, LANES), F32)], axis=1)

    def pack(c_ctx_, norm_g_, ada_b_, final_g_, rpb_, scale_, taps_, cbias_):
        rows = [_rows128(c_ctx_), _rows128(norm_g_), _rows128(ada_b_), _rows128(final_g_), _pad_rows(_rows128(rpb_)),
                scale_, taps_[0], cbias_]
        return _pad_rows(jnp.concatenate(rows, axis=0)), np.cumsum([0] + [r.shape[0] for r in rows])

    w_small, smarks = pack(c_ctx, norm_g, ada_b, final_g, na_rpb, pool_scale, conv_dw, conv_db)
    m_small, _ = pack(m_c_ctx, m_norm_g, m_ada_b, m_final_g, m_na_rpb, m_pool_scale, m_conv_dw, m_conv_db)
    v_small, _ = pack(v_c_ctx, v_norm_g, v_ada_b, v_final_g, v_na_rpb, v_pool_scale, v_conv_dw, v_conv_db)
    mult = jnp.concatenate([_rows128(dsilu_cond[8]), jnp.ones((w_small.shape[0] - d_rows, LANES), F32)], axis=0)
    small_res = _adamw(w_small, m_small, v_small, [(small_parts, k) for k in range(N_DEV)], "adamw_small", mult=mult)

    def unpack(k, like):
        out = []
        for r in small_res:
            flat = r[smarks[k]:smarks[k + 1]].reshape(-1)
            out.append(flat[:like.size].reshape(like.shape))
        return out

    res.update({"c_ctx": unpack(0, c_ctx), "norm_g": unpack(1, norm_g), "ada_b": unpack(2, ada_b),
                "final_g": unpack(3, final_g), "na_rpb": unpack(4, na_rpb), "pool_scale": unpack(5, pool_scale),
                "conv_dw": unpack(6, conv_dw), "conv_db": unpack(7, conv_db)})

    res["ada_w"] = [r.reshape(ada_w.shape) for r in _adamw(
        ada_w.reshape(-1, nb), m_ada_w.reshape(-1, nb), v_ada_w.reshape(-1, nb), [g_ada_w.reshape(-1, nb)], "adamw_ada_w")]

    out0, = _exchange_wait(sent0a, True, small_res[0], "grads_wait0a")
    in0, grp0 = _exchange_wait(sent0b, True, small_res[0], "grads_wait0b")
    pool_in0 = big(in0, pool_w_in[0], m_pool_w_in[0], v_pool_w_in[0], "adamw_pool_in0")
    pool_grp0 = big(grp0, pool_w_grp[0], m_pool_w_grp[0], v_pool_w_grp[0], "adamw_pool_grp0")
    pool_out0 = big(out0, pool_w_out[0], m_pool_w_out[0], v_pool_w_out[0], "adamw_pool_out0")
    res["pool_w_in"] = [jnp.stack([p, q]) for p, q in zip(pool_in0, pool_in1)]
    res["pool_w_grp"] = [jnp.stack([p, q]) for p, q in zip(pool_grp0, pool_grp1)]
    res["pool_w_out"] = [jnp.stack([p, q]) for p, q in zip(pool_out0, pool_out1)]

    order = ["c_ctx", "norm_g", "ada_w", "ada_b", "pool_w_in", "pool_w_grp", "pool_scale", "pool_w_out", "na_w_in",
             "na_rpb", "na_w_out", "conv_w_in", "conv_dw", "conv_db", "conv_w_out", "final_g"]
    outs = [loss, grad_x]
    for j in range(4):
        outs += [res[n][j] for n in order]
    return tuple(outs)
```

```python
import functools
import math

import numpy as np
import jax
import jax.numpy as jnp
from jax import lax
from jax.experimental import pallas as pl
from jax.experimental.pallas import tpu as pltpu

F32 = jnp.float32
BF16 = jnp.bfloat16
N_DEV = 8
LANES = 128
RMS_EPS = 1e-6
GRID_W = 64
WIN_ROWS = 8
WIN_COLS = 16
HEAD_DIM = 64
POOL_WINDOWS = (2, 4, 8, 16)
HALO = 8
CHUNK = 128
MASKED = -1e30
ADAM_LR = 0.001
ADAM_B1 = 0.9
ADAM_B2 = 0.999
ADAM_EPS = 1e-08
ADAM_WD = 0.01
ADAM_STEP = 10
VMEM_LIMIT = 56 * 1024 * 1024
MESH = pl.DeviceIdType.MESH
ANY = pl.BlockSpec(memory_space=pl.ANY)
HBM = pl.BlockSpec(memory_space=pltpu.HBM)
SEM = pl.BlockSpec(memory_space=pltpu.SEMAPHORE)
EFFECT = pltpu.SideEffectType.DATAFLOW_SIDE_EFFECTING


def _pc(body, *, name, **kw):
    return pl.pallas_call(body, name=name, **kw)


def _params(*sem):
    return pltpu.CompilerParams(dimension_semantics=sem if sem else None, vmem_limit_bytes=VMEM_LIMIT)


def _dot(a, b, ca=1, cb=0, precision=None):
    return lax.dot_general(a, b, (((ca,), (cb,)), ((), ())), preferred_element_type=F32, precision=precision)


def _tile(n, pref, unit=LANES):
    best = None
    for t in range(unit, min(n, pref) + 1, unit):
        if n % t == 0:
            best = t
    return best if best is not None else n


def _sigmoid(x):
    return 1.0 / (1.0 + jnp.exp(-x))


def _silu(x):
    return x * _sigmoid(x)


def _dsilu(x):
    s = _sigmoid(x)
    return s * (1.0 + x * (1.0 - s))


def _my_place():
    return lax.axis_index("x"), lax.axis_index("y"), lax.axis_index("c")


def _flip(v, f):
    return 1 - v if f else v


def _gather_small(block, name):
    rows, cols = block.shape

    def body(x_ref, out_ref, send_sems, recv_sems):
        x, y, c = _my_place()
        me = 4 * x + 2 * y + c
        out_ref[me] = x_ref[...]
        copies = []
        for k in range(1, N_DEV):
            peer = (_flip(x, k & 4), _flip(y, k & 2), _flip(c, k & 1))
            cp = pltpu.make_async_remote_copy(
                src_ref=x_ref, dst_ref=out_ref.at[me], send_sem=send_sems.at[k - 1], recv_sem=recv_sems.at[k - 1],
                device_id=peer, device_id_type=MESH)
            cp.start()
            copies.append(cp)
        for cp in copies:
            cp.wait()

    return _pc(
        body, name=name,
        out_shape=jax.ShapeDtypeStruct((N_DEV, rows, cols), block.dtype),
        in_specs=[pl.BlockSpec(memory_space=pltpu.VMEM)],
        out_specs=pl.BlockSpec(memory_space=pltpu.VMEM),
        scratch_shapes=[pltpu.SemaphoreType.DMA((N_DEV - 1,)), pltpu.SemaphoreType.DMA((N_DEV - 1,))],
    )(block)


ALL_PEERS = tuple(range(N_DEV))
CHIP_PEERS = (0, 1, 2, 4, 6)
OTHER_CHIPS = (2, 4, 6)


def _peer(k):
    x, y, c = _my_place()
    peer = (_flip(x, k & 4), _flip(y, k & 2), _flip(c, k & 1))
    return peer, 4 * peer[0] + 2 * peer[1] + peer[2]


def _exchange_copies(srcs, lands, send_sems, recv_sems, per_dest, peers=ALL_PEERS):
    x, y, c = _my_place()
    me = 4 * x + 2 * y + c
    copies = []
    for t, (src, land) in enumerate(zip(srcs, lands)):
        for n, k in enumerate(peers):
            peer, dest = _peer(k)
            s = t * len(peers) + n
            copies.append(pltpu.make_async_remote_copy(
                src_ref=src.at[dest] if per_dest else src, dst_ref=land.at[me],
                send_sem=send_sems[s], recv_sem=recv_sems[s], device_id=peer, device_id_type=MESH))
    return copies


def _forward_copies(lands, send_sems, recv_sems):
    sibling, _ = _peer(1)
    copies = []
    for t, land in enumerate(lands):
        for n, k in enumerate(OTHER_CHIPS):
            _, slot = _peer(k)
            s = t * len(OTHER_CHIPS) + n
            copies.append(pltpu.make_async_remote_copy(
                src_ref=land.at[slot], dst_ref=land.at[slot], send_sem=send_sems[s], recv_sem=recv_sems[s],
                device_id=sibling, device_id_type=MESH))
    return copies


def _forward_start(lands, after, name):
    nt = len(lands)
    ns = nt * len(OTHER_CHIPS)

    def body(*refs):
        ins, outs = refs[:nt + 1], refs[nt + 1:]
        for cp in _forward_copies(ins[:nt], outs[:ns], outs[ns:2 * ns]):
            cp.start()
        outs[-1][...] = jnp.zeros_like(outs[-1])

    res = _pc(
        body, name=name,
        out_shape=(*[pltpu.SemaphoreType.DMA(())] * (2 * ns), *[pltpu.HBM(a.shape, a.dtype) for a in lands],
                   jax.ShapeDtypeStruct((8, LANES), F32)),
        in_specs=[HBM] * nt + [ANY],
        out_specs=(*[SEM] * (2 * ns), *[HBM] * nt, pl.BlockSpec(memory_space=pltpu.VMEM)),
        input_output_aliases={i: 2 * ns + i for i in range(nt)},
        compiler_params=pltpu.CompilerParams(has_side_effects=EFFECT),
    )(*lands, after)
    return list(res[:ns]), list(res[ns:2 * ns]), list(res[2 * ns:2 * ns + nt]), res[-1]


def _forward_wait(state, after, name):
    send_sems, recv_sems, lands, _ = state
    nt, ns = len(lands), len(send_sems)

    def body(*refs):
        sems = refs[nt:nt + 2 * ns]
        for cp in _forward_copies(refs[:nt], sems[:ns], sems[ns:]):
            cp.wait_send()
            cp.wait_recv()

    res = _pc(
        body, name=name,
        out_shape=tuple(pltpu.HBM(a.shape, a.dtype) for a in lands),
        in_specs=[HBM] * nt + [SEM] * (2 * ns) + [ANY],
        out_specs=tuple([HBM] * nt),
        input_output_aliases={i: i for i in range(nt)},
        compiler_params=pltpu.CompilerParams(has_side_effects=EFFECT),
    )(*lands, *send_sems, *recv_sems, after)
    return list(res)


def _exchange_start(srcs, per_dest, after, name, peers=ALL_PEERS):
    nt = len(srcs)
    ns = nt * len(peers)
    lands = [lax.empty((N_DEV,) + (s.shape[1:] if per_dest else s.shape), s.dtype) for s in srcs]

    def body(*refs):
        ins, outs = refs[:2 * nt + 1], refs[2 * nt + 1:]
        for cp in _exchange_copies(ins[:nt], ins[nt:2 * nt], outs[:ns], outs[ns:2 * ns], per_dest, peers):
            cp.start()
        outs[-1][...] = jnp.zeros_like(outs[-1])

    hbm = [pltpu.with_memory_space_constraint(a, pltpu.HBM) for a in list(srcs) + lands]
    res = _pc(
        body, name=name,
        out_shape=(*[pltpu.SemaphoreType.DMA(())] * (2 * ns), *[pltpu.HBM(a.shape, a.dtype) for a in hbm],
                   jax.ShapeDtypeStruct((8, LANES), F32)),
        in_specs=[HBM] * (2 * nt) + [ANY],
        out_specs=(*[SEM] * (2 * ns), *[HBM] * (2 * nt), pl.BlockSpec(memory_space=pltpu.VMEM)),
        input_output_aliases={i: 2 * ns + i for i in range(2 * nt)},
        compiler_params=pltpu.CompilerParams(has_side_effects=EFFECT),
    )(*hbm, after)
    sems, rest = res[:2 * ns], res[2 * ns:]
    return list(sems[:ns]), list(sems[ns:]), list(rest[:nt]), list(rest[nt:2 * nt]), rest[-1]


def _exchange_wait(state, per_dest, after, name, which=None, peers=ALL_PEERS):
    send_sems, recv_sems, srcs, lands, _ = state
    which = list(range(len(srcs))) if which is None else which
    pick = [t * len(peers) + n for t in which for n in range(len(peers))]
    send_sems, recv_sems = [send_sems[s] for s in pick], [recv_sems[s] for s in pick]
    srcs, lands = [srcs[t] for t in which], [lands[t] for t in which]
    nt = len(srcs)
    ns = len(send_sems)

    def body(*refs):
        sems = refs[2 * nt:2 * nt + 2 * ns]
        for cp in _exchange_copies(refs[:nt], refs[nt:2 * nt], sems[:ns], sems[ns:], per_dest, peers):
            cp.wait_send()
            cp.wait_recv()

    thru = list(srcs) + list(lands)
    res = _pc(
        body, name=name,
        out_shape=tuple(pltpu.HBM(a.shape, a.dtype) for a in thru),
        in_specs=[HBM] * (2 * nt) + [SEM] * (2 * ns) + [ANY],
        out_specs=tuple([HBM] * (2 * nt)),
        input_output_aliases={i: i for i in range(2 * nt)},
        compiler_params=pltpu.CompilerParams(has_side_effects=EFFECT),
    )(*thru, *send_sems, *recv_sems, after)
    return list(res[nt:])


def _mod_fwd(cond, ada_w, bias):
    depth, d, nb = ada_w.shape

    def body(c_ref, w_ref, b_ref, o_ref):
        s = _silu(c_ref[...]).astype(BF16)
        o_ref[...] = _dot(s, w_ref[...].astype(BF16)) + b_ref[...]

    return _pc(
        body, name="mod_fwd", grid=(depth,),
        in_specs=[pl.BlockSpec((16, d), lambda i: (0, 0)), pl.BlockSpec((None, d, nb), lambda i: (i, 0, 0)),
                  pl.BlockSpec((None, 1, nb), lambda i: (i, 0, 0))],
        out_specs=pl.BlockSpec((None, 16, nb), lambda i: (i, 0, 0)),
        out_shape=jax.ShapeDtypeStruct((depth, 16, nb), F32),
        compiler_params=_params("parallel"),
    )(cond, ada_w, bias.reshape(depth, 1, nb))


def _mod_bwd(cond, ada_w, dm_all, dm_mine):
    depth, d, nb = ada_w.shape
    d3 = dm_all.shape[-1]

    def body(c_ref, w_ref, all_ref, call_ref, mine_ref, cmine_ref, gw_ref, gb_ref, part_ref, ds_ref):
        i = pl.program_id(0)
        cond_v = c_ref[...]
        s = _silu(cond_v).astype(BF16)
        has_ctx = jnp.where(i < 2, 1.0, 0.0)
        tot_all = jnp.sum(call_ref[...], axis=0, keepdims=True) * has_ctx
        tot_mine = jnp.broadcast_to(jnp.sum(cmine_ref[...], axis=0, keepdims=True) * has_ctx, (8, nb)).astype(BF16)
        gb_ref[...] = jnp.sum(all_ref[...], axis=0, keepdims=True) + tot_all
        gw_ref[...] = _dot(s[0:8], mine_ref[...].astype(BF16), 0, 0) + _dot(s[8:16], tot_mine, 0, 0)
        part = _dot(tot_mine, w_ref[...].astype(BF16), 1, 1)

        @pl.when(i == 0)
        def _():
            part_ref[...] = jnp.zeros_like(part_ref)
            ds_ref[...] = _dsilu(cond_v)

        part_ref[...] += part

    def rows(width, which):
        return pl.BlockSpec((None, N_DEV, width), which)

    layer = lambda i: (i, 0, 0)
    ctx_layer = lambda i: (jnp.minimum(i, 1) + 4, 0, 0)
    return _pc(
        body, name="mod_bwd", grid=(depth,),
        in_specs=[pl.BlockSpec((16, d), lambda i: (0, 0)), pl.BlockSpec((None, d, nb), layer),
                  rows(d3, layer), rows(d3, ctx_layer), rows(nb, layer), rows(nb, ctx_layer)],
        out_specs=[pl.BlockSpec((None, d, nb), layer), pl.BlockSpec((None, 1, d3), layer),
                   pl.BlockSpec((8, d), lambda i: (0, 0)), pl.BlockSpec((16, d), lambda i: (0, 0))],
        out_shape=[jax.ShapeDtypeStruct((depth, d, nb), F32), jax.ShapeDtypeStruct((depth, 1, d3), F32),
                   jax.ShapeDtypeStruct((8, d), F32), jax.ShapeDtypeStruct((16, d), F32)],
        compiler_params=_params("arbitrary"),
    )(cond, ada_w, dm_all, dm_all, dm_mine, dm_mine)


def _norm_fwd(xs, g, mod, tr, seg_tiles, name):
    t, d = xs.shape

    def body(x_ref, g_ref, mod_ref, h_ref):
        x = x_ref[...]
        r = lax.rsqrt(jnp.mean(x * x, axis=-1, keepdims=True) + RMS_EPS)
        y = (x * r) * g_ref[...]
        h_ref[...] = (y * (1.0 + mod_ref[1:2, :]) + mod_ref[0:1, :]).astype(BF16)

    return _pc(
        body, name=name, grid=(t // tr,),
        in_specs=[pl.BlockSpec((tr, d), lambda i: (i, 0)), pl.BlockSpec((1, d), lambda i: (0, 0)),
                  pl.BlockSpec((None, 8, d), lambda i: (i // seg_tiles, 0, 0))],
        out_specs=pl.BlockSpec((tr, d), lambda i: (i, 0)),
        out_shape=jax.ShapeDtypeStruct((t, d), BF16),
        compiler_params=_params("parallel"),
    )(xs, g, mod)


def _resid_grad(dx, i, seg_tiles, yx_ref, gate_ref, dyx_ref, gsum_ref):
    dyx_ref[...] = (dx * gate_ref[2:3, :]).astype(BF16)

    @pl.when(i % seg_tiles == 0)
    def _():
        gsum_ref[...] = jnp.zeros_like(gsum_ref)

    gsum_ref[0:1, :] += jnp.sum(dx * yx_ref[...], axis=0, keepdims=True)


def _norm_bwd(xs, dh, dres, g, mod, tr, seg_tiles, name, res_tiles=None, out_tiles=None, below=None):
    t, d = xs.shape
    n_tiles = t // tr
    res_tiles = n_tiles if res_tiles is None else res_tiles
    out_tiles = n_tiles if out_tiles is None else out_tiles

    def body(x_ref, dh_ref, dres_ref, g_ref, mod_ref, *rest):
        i = pl.program_id(0)
        x = x_ref[...]
        r = lax.rsqrt(jnp.mean(x * x, axis=-1, keepdims=True) + RMS_EPS)
        xn = x * r
        dhv = dh_ref[...]
        gain = g_ref[...]
        one_scale = 1.0 + mod_ref[1:2, :]
        dxn = dhv * (gain * one_scale)
        dx = r * (dxn - xn * jnp.mean(dxn * xn, axis=-1, keepdims=True))
        if res_tiles == n_tiles:
            dx = dx + dres_ref[...]
        else:
            dx = dx + jnp.where(i < res_tiles, dres_ref[...], 0.0)
        if below is None:
            dx_ref, sum_ref = rest
        else:
            yx_ref, gate_ref, dx_ref, sum_ref, dyx_ref, gsum_ref = rest
            _resid_grad(dx, i, seg_tiles, yx_ref, gate_ref, dyx_ref, gsum_ref)
        if out_tiles == n_tiles:
            dx_ref[...] = dx
        else:
            @pl.when(i < out_tiles)
            def _():
                dx_ref[...] = dx

        @pl.when(i % seg_tiles == 0)
        def _():
            sum_ref[...] = jnp.zeros_like(sum_ref)

        sum_ref[0:1, :] += jnp.sum(dhv, axis=0, keepdims=True)
        sum_ref[1:2, :] += jnp.sum(dhv * (xn * gain), axis=0, keepdims=True)
        sum_ref[2:3, :] += jnp.sum(dhv * one_scale * xn, axis=0, keepdims=True)

    row = pl.BlockSpec((tr, d), lambda i: (i, 0))
    seg = pl.BlockSpec((None, 8, d), lambda i: (i // seg_tiles, 0, 0))
    in_specs = [row, row, pl.BlockSpec((tr, d), lambda i: (jnp.minimum(i, res_tiles - 1), 0)),
                pl.BlockSpec((1, d), lambda i: (0, 0)), seg]
    out_specs = [pl.BlockSpec((tr, d), lambda i: (jnp.minimum(i, out_tiles - 1), 0)), seg]
    out_shape = [jax.ShapeDtypeStruct((out_tiles * tr, d), F32), jax.ShapeDtypeStruct((mod.shape[0], 8, d), F32)]
    args = [xs, dh, dres, g, mod]
    if below is not None:
        in_specs += [row, seg]
        out_specs += [row, seg]
        out_shape += [jax.ShapeDtypeStruct((t, d), BF16), jax.ShapeDtypeStruct((below[1].shape[0], 8, d), F32)]
        args += list(below)
    return _pc(
        body, name=name, grid=(n_tiles,), in_specs=in_specs, out_specs=out_specs, out_shape=out_shape,
        compiler_params=_params("arbitrary"),
    )(*args)


def _loss_head(xs, target, g, yx, mod, tr):
    t, d = xs.shape

    def body(x_ref, t_ref, g_ref, yx_ref, gate_ref, loss_ref, dx_ref, dg_ref, dyx_ref, gsum_ref):
        i = pl.program_id(0)
        x = x_ref[...]
        r = lax.rsqrt(jnp.mean(x * x, axis=-1, keepdims=True) + RMS_EPS)
        xn = x * r
        gain = g_ref[...]
        err = xn * gain - t_ref[...]
        dy = err * (1.0 / d)
        dxn = dy * gain
        dx = r * (dxn - xn * jnp.mean(dxn * xn, axis=-1, keepdims=True))
        dx_ref[...] = dx
        _resid_grad(dx, i, t // tr, yx_ref, gate_ref, dyx_ref, gsum_ref)

        @pl.when(i == 0)
        def _():
            loss_ref[...] = jnp.zeros_like(loss_ref)
            dg_ref[...] = jnp.zeros_like(dg_ref)

        loss_ref[...] += 0.5 * jnp.sum(jnp.mean(err * err, axis=-1, keepdims=True))
        dg_ref[0:1, :] += jnp.sum(dy * xn, axis=0, keepdims=True)

    row = pl.BlockSpec((tr, d), lambda i: (i, 0))
    seg = pl.BlockSpec((None, 8, d), lambda i: (0, 0, 0))
    return _pc(
        body, name="loss_head", grid=(t // tr,),
        in_specs=[row, row, pl.BlockSpec((1, d), lambda i: (0, 0)), row, seg],
        out_specs=[pl.BlockSpec((8, LANES), lambda i: (0, 0)), row, pl.BlockSpec((8, d), lambda i: (0, 0)), row, seg],
        out_shape=[jax.ShapeDtypeStruct((8, LANES), F32), jax.ShapeDtypeStruct((t, d), F32),
                   jax.ShapeDtypeStruct((8, d), F32), jax.ShapeDtypeStruct((t, d), BF16),
                   jax.ShapeDtypeStruct((1, 8, d), F32)],
        compiler_params=_params("arbitrary"),
    )(xs, target, g, yx, mod)


def _proj_in(h, w, layer, width, name, blocks=None, dtype=F32):
    t, d = h.shape
    n8 = w.shape[-1]
    first, count = blocks if blocks is not None else (0, N_DEV)
    per_part = width // n8
    tm = _tile(t, 1152)

    def body(a_ref, b_ref, o_ref):
        o_ref[...] = _dot(a_ref[...], b_ref[...]).astype(dtype)

    return _pc(
        body, name=name, grid=(t // tm, count),
        in_specs=[pl.BlockSpec((tm, d), lambda i, j: (i, 0)),
                  pl.BlockSpec((None, None, d, n8), lambda i, j: (first + j, layer, 0, 0))],
        out_specs=pl.BlockSpec((None, tm, n8), lambda i, j: (j // per_part, i, j % per_part)),
        out_shape=jax.ShapeDtypeStruct((count // per_part, t, width), dtype),
        compiler_params=_params("parallel", "parallel"),
    )(h, w)


def _proj_out(z, w, res, mod, tm, seg_tiles, name, nxt=None):
    t, k = z.shape
    d = w.shape[1]

    def body(z_ref, w_ref, res_ref, mod_ref, *rest):
        yx = _dot(z_ref[...], w_ref[...])
        x = res_ref[...] + mod_ref[2:3, :] * yx
        if nxt is None:
            yx_ref, x_ref = rest
        else:
            g_ref, nmod_ref, yx_ref, x_ref, h_ref = rest
            r = lax.rsqrt(jnp.mean(x * x, axis=-1, keepdims=True) + RMS_EPS)
            h_ref[...] = (((x * r) * g_ref[...]) * (1.0 + nmod_ref[1:2, :]) + nmod_ref[0:1, :]).astype(BF16)
        yx_ref[...] = yx
        x_ref[...] = x

    tile = pl.BlockSpec((tm, d), lambda i: (i, 0))
    seg = pl.BlockSpec((None, 8, d), lambda i: (i // seg_tiles, 0, 0))
    in_specs = [pl.BlockSpec((tm, k), lambda i: (i, 0)), pl.BlockSpec((k, d), lambda i: (0, 0)), tile, seg]
    out_specs = [tile, tile]
    out_shape = [jax.ShapeDtypeStruct((t, d), F32), jax.ShapeDtypeStruct((t, d), F32)]
    args = [z, w, res, mod]
    if nxt is not None:
        in_specs += [pl.BlockSpec((1, d), lambda i: (0, 0)), seg]
        out_specs.append(tile)
        out_shape.append(jax.ShapeDtypeStruct((t, d), BF16))
        args += list(nxt)
    return _pc(
        body, name=name, grid=(t // tm,), in_specs=in_specs, out_specs=out_specs, out_shape=out_shape,
        compiler_params=_params("parallel"),
    )(*args)


def _proj_out_dz(dyx, w, name):
    t, d = dyx.shape
    width = w.shape[0]
    tm, tn = _tile(t, 1024), _tile(width, 512)

    def body(a_ref, w_ref, o_ref):
        o_ref[...] = _dot(a_ref[...], w_ref[...], 1, 1)

    return _pc(
        body, name=name, grid=(t // tm, width // tn),
        in_specs=[pl.BlockSpec((tm, d), lambda i, j: (i, 0)), pl.BlockSpec((tn, d), lambda i, j: (j, 0))],
        out_specs=pl.BlockSpec((tm, tn), lambda i, j: (i, j)),
        out_shape=jax.ShapeDtypeStruct((t, width), F32),
        compiler_params=_params("parallel", "parallel"),
    )(dyx, w)


def _proj_in_dh(dpre, w, layer, name, after=None):
    parts, t, width = dpre.shape
    d, n8 = w.shape[-2:]
    per_part = width // n8
    tm, tn = _tile(t, 1152), _tile(d, 512)

    def body(a_ref, w_ref, *rest):
        o_ref = rest[-1]
        part = _dot(a_ref[:, 0:n8], w_ref[0], 1, 1)
        for s in range(1, per_part):
            part += _dot(a_ref[:, s * n8:(s + 1) * n8], w_ref[s], 1, 1)

        @pl.when(pl.program_id(2) == 0)
        def _():
            o_ref[...] = part

        @pl.when(pl.program_id(2) != 0)
        def _():
            o_ref[...] += part

    extra = [] if after is None else [after]
    return _pc(
        body, name=name, grid=(t // tm, d // tn, parts),
        in_specs=[pl.BlockSpec((None, tm, width), lambda i, j, k: (k, i, 0)),
                  pl.BlockSpec((per_part, None, tn, n8), lambda i, j, k: (k, layer, j, 0))] + [ANY] * len(extra),
        out_specs=pl.BlockSpec((tm, tn), lambda i, j, k: (i, j)),
        out_shape=jax.ShapeDtypeStruct((t, d), F32),
        compiler_params=_params("parallel", "parallel", "arbitrary"),
    )(dpre, w, *extra)


def _transposed(a_ref):
    return a_ref[...].T


def _grad_w_in(h, dpre, n8, name):
    t, d = h.shape
    parts, _, width = dpre.shape
    per_part = width // n8
    tm, tk = _tile(d, 512), _tile(t, 1152)
    nk = t // tk

    def body(a_ref, b_ref, o_ref, acc_ref):
        k = pl.program_id(1)

        @pl.when(k == 0)
        def _():
            acc_ref[...] = jnp.zeros_like(acc_ref)

        at = _transposed(a_ref)
        for p in range(parts):
            r = _dot(at, b_ref[p])
            for s in range(per_part):
                acc_ref[p * per_part + s] += r[:, s * n8:(s + 1) * n8]

        @pl.when(k == nk - 1)
        def _():
            o_ref[...] = acc_ref[...].astype(BF16)

    return _pc(
        body, name=name, grid=(d // tm, nk),
        in_specs=[pl.BlockSpec((tk, tm), lambda i, k: (k, i)), pl.BlockSpec((parts, tk, width), lambda i, k: (0, k, 0))],
        out_specs=pl.BlockSpec((parts * per_part, tm, n8), lambda i, k: (0, i, 0)),
        out_shape=jax.ShapeDtypeStruct((parts * per_part, d, n8), BF16),
        scratch_shapes=[pltpu.VMEM((parts * per_part, tm, n8), F32)],
        compiler_params=_params("parallel", "arbitrary"),
    )(h, dpre)


def _grad_w_out(z, dyx, name):
    width = z.shape[1]
    t, d = dyx.shape
    tm, tk = _tile(width, 512), _tile(t, 1152)
    nk = t // tk

    def body(a_ref, b_ref, o_ref, acc_ref):
        k = pl.program_id(1)

        @pl.when(k == 0)
        def _():
            acc_ref[...] = jnp.zeros_like(acc_ref)

        acc_ref[...] += _dot(_transposed(a_ref), b_ref[...])

        @pl.when(k == nk - 1)
        def _():
            o_ref[...] = acc_ref[...].astype(BF16)

    return _pc(
        body, name=name, grid=(width // tm, nk),
        in_specs=[pl.BlockSpec((tk, tm), lambda i, k: (k, i)), pl.BlockSpec((tk, d), lambda i, k: (k, 0))],
        out_specs=pl.BlockSpec((tm, d), lambda i, k: (i, 0)),
        out_shape=jax.ShapeDtypeStruct((width, d), BF16),
        scratch_shapes=[pltpu.VMEM((tm, d), F32)],
        compiler_params=_params("parallel", "arbitrary"),
    )(z, dyx)


def _shift(v, k):
    n = v.shape[0]
    return pltpu.roll(v, k % n, 0)


def _window_sum(v, win):
    s = v + _shift(v, 1)
    step = 1
    while 2 * step < win:
        s = _shift(s, step) + _shift(s, -step)
        step *= 2
    return s


def _window_count(base, seg_len, win, shape):
    t = base + lax.broadcasted_iota(jnp.int32, shape, 0)
    hi = jnp.minimum(t + win // 2, seg_len)
    lo = jnp.maximum(t - win // 2, 0)
    return (hi - lo).astype(F32)


def _pad_offsets(segs):
    return [HALO * (s + 1) + st for s, (st, _) in enumerate(segs)]


def _for_chunks(segs, fn):
    offs = _pad_offsets(segs)
    for s, (st, ln) in enumerate(segs):
        def step(ci, carry, s=s, st=st, ln=ln):
            fn(s, st, ln, offs[s], pl.multiple_of(ci * CHUNK, CHUNK))
            return carry
        lax.fori_loop(0, ln // CHUNK, step, 0)


def _pool_fwd(pre, w_grp, scale, segs, name):
    _, t, width = pre.shape
    grp = width // len(POOL_WINDOWS)
    padded = t + HALO * (len(segs) + 1)

    def group(win, pre_ref, w_ref, sc_ref, z_ref, diff_ref, pad_ref):
        pad_ref[...] = jnp.zeros_like(pad_ref)

        def fill(s, st, ln, off, b):
            pad_ref[pl.ds(off + b, CHUNK), :] = pre_ref[0, pl.ds(st + b, CHUNK), :]

        _for_chunks(segs, fill)

        def mix(s, st, ln, off, b):
            ext = pad_ref[pl.ds(off - HALO + b, CHUNK + 2 * HALO), :]
            total = _window_sum(ext, win)[HALO:HALO + CHUNK]
            u = pre_ref[0, pl.ds(st + b, CHUNK), :]
            diff = (total / _window_count(b, ln, win, u.shape) - u).astype(BF16)
            mixed = _dot(diff, w_ref[...])
            gate = _silu(pre_ref[1, pl.ds(st + b, CHUNK), :])
            z_ref[pl.ds(st + b, CHUNK), :] = (mixed * sc_ref[...] * gate).astype(BF16)
            diff_ref[pl.ds(st + b, CHUNK), :] = diff

        _for_chunks(segs, mix)

    def body(pre_ref, w_ref, sc_ref, z_ref, diff_ref, pad_ref):
        gi = pl.program_id(0)
        for widx, win in enumerate(POOL_WINDOWS):
            @pl.when(gi == widx)
            def _(win=win):
                group(win, pre_ref, w_ref, sc_ref, z_ref, diff_ref, pad_ref)

    col = pl.BlockSpec((t, grp), lambda g: (0, g))
    return _pc(
        body, name=name, grid=(len(POOL_WINDOWS),),
        in_specs=[pl.BlockSpec((2, t, grp), lambda g: (0, 0, g)), pl.BlockSpec((None, grp, grp), lambda g: (g, 0, 0)),
                  pl.BlockSpec((1, grp), lambda g: (0, g))],
        out_specs=[col, col],
        out_shape=[jax.ShapeDtypeStruct((t, width), BF16), jax.ShapeDtypeStruct((t, width), BF16)],
        scratch_shapes=[pltpu.VMEM((padded, grp), F32)],
        compiler_params=_params("parallel"),
    )(pre, w_grp, scale)


def _pool_bwd(dz, diff, pre, w_grp, scale, segs, name):
    _, t, width = pre.shape
    grp = width // len(POOL_WINDOWS)
    padded = t + HALO * (len(segs) + 1)

    def group(win, dz_ref, diff_ref, pre_ref, w_ref, sc_ref, dpre_ref, dw_ref, dsc_ref, pad_ref, dd_ref):
        pad_ref[...] = jnp.zeros_like(pad_ref)
        dw_ref[...] = jnp.zeros_like(dw_ref)
        dsc_ref[...] = jnp.zeros_like(dsc_ref)

        def first(s, st, ln, off, b):
            rows = pl.ds(st + b, CHUNK)
            diff_v = diff_ref[rows, :]
            mixed = _dot(diff_v, w_ref[...])
            g = pre_ref[1, rows, :]
            sg = _silu(g)
            dzv = dz_ref[rows, :]
            dmixed = (dzv * sc_ref[...] * sg).astype(BF16)
            dsc_ref[...] += jnp.sum(dzv * mixed * sg, axis=0, keepdims=True)
            dpre_ref[1, rows, :] = (dzv * mixed * sc_ref[...] * _dsilu(g)).astype(BF16)
            ddiff = _dot(dmixed, w_ref[...], 1, 1)
            dw_ref[...] += _dot(diff_v, dmixed, 0, 0)
            dd_ref[rows, :] = ddiff
            pad_ref[pl.ds(off + b, CHUNK), :] = ddiff / _window_count(b, ln, win, ddiff.shape)

        _for_chunks(segs, first)

        def second(s, st, ln, off, b):
            rows = pl.ds(st + b, CHUNK)
            ext = pad_ref[pl.ds(off - HALO + b, CHUNK + 2 * HALO), :]
            total = _shift(_window_sum(ext, win), -1)[HALO:HALO + CHUNK]
            dpre_ref[0, rows, :] = (total - dd_ref[rows, :]).astype(BF16)

        _for_chunks(segs, second)

    def body(dz_ref, diff_ref, pre_ref, w_ref, sc_ref, dpre_ref, dw_ref, dsc_ref, pad_ref, dd_ref):
        gi = pl.program_id(0)
        for widx, win in enumerate(POOL_WINDOWS):
            @pl.when(gi == widx)
            def _(win=win):
                group(win, dz_ref, diff_ref, pre_ref, w_ref, sc_ref, dpre_ref, dw_ref, dsc_ref, pad_ref, dd_ref)

    col = pl.BlockSpec((t, grp), lambda g: (0, g))
    both = pl.BlockSpec((2, t, grp), lambda g: (0, 0, g))
    wspec = pl.BlockSpec((None, grp, grp), lambda g: (g, 0, 0))
    sspec = pl.BlockSpec((1, grp), lambda g: (0, g))
    return _pc(
        body, name=name, grid=(len(POOL_WINDOWS),),
        in_specs=[col, col, both, wspec, sspec],
        out_specs=[both, wspec, sspec],
        out_shape=[jax.ShapeDtypeStruct((2, t, width), BF16), jax.ShapeDtypeStruct((len(POOL_WINDOWS), grp, grp), F32),
                   jax.ShapeDtypeStruct((1, width), F32)],
        scratch_shapes=[pltpu.VMEM((padded, grp), F32), pltpu.VMEM((t, grp), F32)],
        compiler_params=_params("parallel"),
    )(dz, diff, pre, w_grp, scale)


def _conv_fwd(pre, dw, db, name):
    _, t, width = pre.shape
    cb = LANES
    segs = [(0, t)]

    def body(pre_ref, dw_ref, db_ref, z_ref, pad_ref):
        pad_ref[...] = jnp.zeros_like(pad_ref)

        def fill(s, st, ln, off, b):
            rows = pl.ds(b, CHUNK)
            pad_ref[pl.ds(off + b, CHUNK), :] = pre_ref[1, rows, :] * pre_ref[2, rows, :]

        _for_chunks(segs, fill)

        def mix(s, st, ln, off, b):
            rows = pl.ds(b, CHUNK)
            ext = pad_ref[pl.ds(off - HALO + b, CHUNK + 2 * HALO), :]
            conv = (dw_ref[0:1, :] * _shift(ext, 1) + dw_ref[1:2, :] * ext + dw_ref[2:3, :] * _shift(ext, -1))
            conv = conv[HALO:HALO + CHUNK] + db_ref[...]
            y = pre_ref[0, rows, :] * conv
            z_ref[rows, :] = (y * _silu(pre_ref[3, rows, :])).astype(BF16)

        _for_chunks(segs, mix)

    return _pc(
        body, name=name, grid=(width // cb,),
        in_specs=[pl.BlockSpec((4, t, cb), lambda j: (0, 0, j)), pl.BlockSpec((8, cb), lambda j: (0, j)),
                  pl.BlockSpec((1, cb), lambda j: (0, j))],
        out_specs=pl.BlockSpec((t, cb), lambda j: (0, j)),
        out_shape=jax.ShapeDtypeStruct((t, width), BF16),
        scratch_shapes=[pltpu.VMEM((t + 2 * HALO, cb), F32)],
        compiler_params=_params("parallel"),
    )(pre, dw, db)


def _conv_bwd(dz, pre, dw, db, name):
    _, t, width = pre.shape
    cb = LANES
    segs = [(0, t)]

    def body(dz_ref, pre_ref, dw_ref, db_ref, dpre_ref, ddw_ref, ddb_ref, pad_a, pad_c):
        pad_a[...] = jnp.zeros_like(pad_a)
        pad_c[...] = jnp.zeros_like(pad_c)
        ddw_ref[...] = jnp.zeros_like(ddw_ref)
        ddb_ref[...] = jnp.zeros_like(ddb_ref)

        def fill(s, st, ln, off, b):
            rows = pl.ds(b, CHUNK)
            pad_a[pl.ds(off + b, CHUNK), :] = pre_ref[1, rows, :] * pre_ref[2, rows, :]

        _for_chunks(segs, fill)

        def first(s, st, ln, off, b):
            rows = pl.ds(b, CHUNK)
            ext = pad_a[pl.ds(off - HALO + b, CHUNK + 2 * HALO), :]
            prev, nxt = _shift(ext, 1)[HALO:HALO + CHUNK], _shift(ext, -1)[HALO:HALO + CHUNK]
            here = ext[HALO:HALO + CHUNK]
            conv = dw_ref[0:1, :] * prev + dw_ref[1:2, :] * here + dw_ref[2:3, :] * nxt + db_ref[...]
            bg, g = pre_ref[0, rows, :], pre_ref[3, rows, :]
            dzv = dz_ref[rows, :]
            dy = dzv * _silu(g)
            dpre_ref[3, rows, :] = (dzv * (bg * conv) * _dsilu(g)).astype(BF16)
            dpre_ref[0, rows, :] = (dy * conv).astype(BF16)
            dconv = dy * bg
            pad_c[pl.ds(off + b, CHUNK), :] = dconv
            ddw_ref[0:1, :] += jnp.sum(dconv * prev, axis=0, keepdims=True)
            ddw_ref[1:2, :] += jnp.sum(dconv * here, axis=0, keepdims=True)
            ddw_ref[2:3, :] += jnp.sum(dconv * nxt, axis=0, keepdims=True)
            ddb_ref[0:1, :] += jnp.sum(dconv, axis=0, keepdims=True)

        _for_chunks(segs, first)

        def second(s, st, ln, off, b):
            rows = pl.ds(b, CHUNK)
            ext = pad_c[pl.ds(off - HALO + b, CHUNK + 2 * HALO), :]
            da = (dw_ref[0:1, :] * _shift(ext, -1) + dw_ref[1:2, :] * ext + dw_ref[2:3, :] * _shift(ext, 1))
            da = da[HALO:HALO + CHUNK]
            dpre_ref[1, rows, :] = (da * pre_ref[2, rows, :]).astype(BF16)
            dpre_ref[2, rows, :] = (da * pre_ref[1, rows, :]).astype(BF16)

        _for_chunks(segs, second)

    quad = pl.BlockSpec((4, t, cb), lambda j: (0, 0, j))
    rows8 = pl.BlockSpec((8, cb), lambda j: (0, j))
    return _pc(
        body, name=name, grid=(width // cb,),
        in_specs=[pl.BlockSpec((t, cb), lambda j: (0, j)), quad, rows8, pl.BlockSpec((1, cb), lambda j: (0, j))],
        out_specs=[quad, rows8, rows8],
        out_shape=[jax.ShapeDtypeStruct((4, t, width), BF16), jax.ShapeDtypeStruct((8, width), F32),
                   jax.ShapeDtypeStruct((8, width), F32)],
        scratch_shapes=[pltpu.VMEM((t + 2 * HALO, cb), F32), pltpu.VMEM((t + 2 * HALO, cb), F32)],
        compiler_params=_params("parallel"),
    )(dz, pre, dw, db)


PAIR_TILES = 2 * WIN_ROWS - 2


def _pair_geometry():
    lane = lax.broadcasted_iota(jnp.int32, (GRID_W, LANES), 1)
    qcol = lax.broadcasted_iota(jnp.int32, (GRID_W, LANES), 0)
    low = lane < GRID_W
    kcol = jnp.where(low, lane, lane - GRID_W)
    start = jnp.clip(qcol - WIN_COLS // 2, 0, GRID_W - WIN_COLS)
    inside = (kcol >= start) & (kcol < start + WIN_COLS)
    return low, inside


def _bias_tiles(rpb_ref, rows_ref, tiles_ref, inside):
    for h in range(2):
        rows = rpb_ref[h]
        rows_ref[h] = (pltpu.roll(rows, LANES - (WIN_COLS - 1), 1)
                       + pltpu.roll(pltpu.roll(rows, GRID_W - (WIN_COLS - 1), 1), 2 * WIN_ROWS - 1, 0))
        for t in range(PAIR_TILES):
            both = jnp.broadcast_to(rows_ref[h, t:t + 1, :], (GRID_W, LANES))
            tiles_ref[h, t] = jnp.where(inside, pltpu.roll(both, 0, 1, stride=1, stride_axis=0), MASKED)


def _bias_tiles_grad(dtiles_ref, drpb_ref):
    n = PAIR_TILES * GRID_W
    qcol = lax.broadcasted_iota(jnp.int32, (n, LANES), 0) & (GRID_W - 1)
    lane = lax.broadcasted_iota(jnp.int32, (1, LANES), 1)
    zero = jnp.zeros((1, LANES), F32)
    for h in range(2):
        v = pltpu.roll(dtiles_ref[h].reshape(n, LANES), WIN_COLS - 1, 1)
        for bit in range(6):
            v = jnp.where((qcol >> bit) & 1 == 1, pltpu.roll(v, LANES - (1 << bit), 1), v)
        sums = [jnp.sum(v[t * GRID_W:(t + 1) * GRID_W], axis=0, keepdims=True) for t in range(PAIR_TILES)]
        for r in range(2 * WIN_ROWS):
            here = sums[r] if r < PAIR_TILES else zero
            prev = pltpu.roll(sums[r - 1], GRID_W, 1) if 1 <= r <= PAIR_TILES else zero
            drpb_ref[h, r:r + 1, :] = jnp.where(lane < 2 * WIN_COLS - 1, here + prev, 0.0)


def _attn_rows(r, n_rows):
    first = jnp.clip(r - WIN_ROWS // 2, 0, n_rows - WIN_ROWS)
    return first, first - r + WIN_ROWS - 1


def _softmax(s_loc, s_ctx):
    m = jnp.maximum(jnp.max(s_loc, axis=-1, keepdims=True), jnp.max(s_ctx, axis=-1, keepdims=True))
    e_loc, e_ctx = jnp.exp(s_loc - m), jnp.exp(s_ctx - m)
    inv = 1.0 / (jnp.sum(e_loc, axis=-1, keepdims=True) + jnp.sum(e_ctx, axis=-1, keepdims=True))
    return e_loc * inv, e_ctx * inv


def _pair_bias(tiles_ref, h, j):
    return jnp.concatenate([tiles_ref[h, j + 2 * m] for m in range(WIN_ROWS // 2)], axis=1)


ROWS_PER_STEP = 4


def _attn_items(step, n_rows, q_ref, low):
    items = []
    for u in range(ROWS_PER_STEP):
        r = step * ROWS_PER_STEP + u
        first, j = _attn_rows(r, n_rows)
        rows = pl.ds(pl.multiple_of(r * GRID_W, GRID_W), GRID_W)
        keys = pl.ds(pl.multiple_of(first * GRID_W, GRID_W), WIN_ROWS * GRID_W)
        q = (q_ref[rows, :].astype(F32) * HEAD_DIM ** -0.5).astype(BF16)
        zero = jnp.zeros_like(q)
        items.append((rows, keys, j, 0, jnp.where(low, q, zero)))
        items.append((rows, keys, j, 1, jnp.where(low, zero, q)))
    return items


def _attn_fwd(qkv, gate, rpb, seq):
    _, t, width = qkv.shape
    n_rows = seq // GRID_W
    n_ctx = t - seq
    blk = WIN_ROWS * GRID_W

    def body(q_ref, k_ref, v_ref, g_ref, rpb_ref, z_ref, o_ref, rows_ref, tiles_ref):
        low, inside = _pair_geometry()
        _bias_tiles(rpb_ref, rows_ref, tiles_ref, inside)
        ctx = pl.ds(seq, n_ctx)

        def step(i, carry):
            items = _attn_items(i, n_rows, q_ref, low)
            k_ctx, v_ctx = k_ref[ctx, :], v_ref[ctx, :]
            scores = [(_dot(q, k_ref[keys, :], 1, 1) + _pair_bias(tiles_ref, h, j), _dot(q, k_ctx, 1, 1))
                      for _, keys, j, h, q in items]
            probs = [_softmax(s_loc, s_ctx) for s_loc, s_ctx in scores]
            outs = [_dot(p_loc.astype(BF16), v_ref[keys, :]) + _dot(p_ctx.astype(BF16), v_ctx)
                    for (_, keys, _, _, _), (p_loc, p_ctx) in zip(items, probs)]
            for u in range(ROWS_PER_STEP):
                rows = items[2 * u][0]
                o = jnp.where(low, outs[2 * u], outs[2 * u + 1])
                o_ref[rows, :] = o
                z_ref[rows, :] = (o * _silu(g_ref[rows, :])).astype(BF16)
            return carry

        lax.fori_loop(0, n_rows // ROWS_PER_STEP, step, 0)

    def part(p):
        return pl.BlockSpec((None, t, LANES), lambda h: (p, 0, h))

    out = pl.BlockSpec((seq, LANES), lambda h: (0, h))
    return _pc(
        body, name="attn_fwd", grid=(width // LANES,),
        in_specs=[part(0), part(1), part(2), part(0), pl.BlockSpec((2, 2 * WIN_ROWS, LANES), lambda h: (h, 0, 0))],
        out_specs=[out, out],
        out_shape=[jax.ShapeDtypeStruct((seq, width), BF16), jax.ShapeDtypeStruct((seq, width), F32)],
        scratch_shapes=[pltpu.VMEM((2, 2 * WIN_ROWS, LANES), F32), pltpu.VMEM((2, PAIR_TILES, GRID_W, LANES), F32)],
        compiler_params=_params("parallel"),
    )(qkv, qkv, qkv, gate, rpb)


def _attn_bwd(qkv, gate, o, dz, rpb, seq):
    _, t, width = qkv.shape
    n_rows = seq // GRID_W
    n_ctx = t - seq
    blk = WIN_ROWS * GRID_W
    heads = 2 * width // LANES

    def body(q_ref, k_ref, v_ref, g_ref, o_ref, dz_ref, rpb_ref, dpre_ref, drpb_ref,
             rows_ref, tiles_ref, dtiles_ref, dk_ref, dv_ref):
        low, inside = _pair_geometry()
        _bias_tiles(rpb_ref, rows_ref, tiles_ref, inside)
        dtiles_ref[...] = jnp.zeros_like(dtiles_ref)
        dk_ref[...] = jnp.zeros_like(dk_ref)
        dv_ref[...] = jnp.zeros_like(dv_ref)
        ctx = pl.ds(seq, n_ctx)

        def step(i, carry):
            items = _attn_items(i, n_rows, q_ref, low)
            k_ctx, v_ctx = k_ref[ctx, :], v_ref[ctx, :]
            d_outs = []
            for u in range(ROWS_PER_STEP):
                rows = items[2 * u][0]
                g = g_ref[rows, :]
                dzv = dz_ref[rows, :]
                dpre_ref[3, rows, :] = (dzv * o_ref[rows, :] * _dsilu(g)).astype(BF16)
                d_o = (dzv * _silu(g)).astype(BF16)
                zero = jnp.zeros_like(d_o)
                d_outs += [jnp.where(low, d_o, zero), jnp.where(low, zero, d_o)]
            scores = [(_dot(q, k_ref[keys, :], 1, 1) + _pair_bias(tiles_ref, h, j), _dot(q, k_ctx, 1, 1))
                      for _, keys, j, h, q in items]
            dprobs = [(_dot(doh, v_ref[keys, :], 1, 1), _dot(doh, v_ctx, 1, 1))
                      for (_, keys, _, _, _), doh in zip(items, d_outs)]
            probs = [_softmax(s_loc, s_ctx) for s_loc, s_ctx in scores]
            dscores = []
            for (p_loc, p_ctx), (dp_loc, dp_ctx) in zip(probs, dprobs):
                delta = (jnp.sum(p_loc * dp_loc, axis=-1, keepdims=True)
                         + jnp.sum(p_ctx * dp_ctx, axis=-1, keepdims=True))
                dscores.append((p_loc * (dp_loc - delta), p_ctx * (dp_ctx - delta)))
            dqs = [_dot(ds_loc.astype(BF16), k_ref[keys, :]) + _dot(ds_ctx.astype(BF16), k_ctx)
                   for (_, keys, _, _, _), (ds_loc, ds_ctx) in zip(items, dscores)]
            for u in range(ROWS_PER_STEP):
                rows = items[2 * u][0]
                dpre_ref[0, rows, :] = (jnp.where(low, dqs[2 * u], dqs[2 * u + 1]) * HEAD_DIM ** -0.5).astype(BF16)
            for (_, keys, j, h, q), doh, (p_loc, p_ctx), (ds_loc, ds_ctx) in zip(items, d_outs, probs, dscores):
                dk_ref[keys, :] += _dot(ds_loc.astype(BF16), q, 0, 0)
                dk_ref[ctx, :] += _dot(ds_ctx.astype(BF16), q, 0, 0)
                dv_ref[keys, :] += _dot(p_loc.astype(BF16), doh, 0, 0)
                dv_ref[ctx, :] += _dot(p_ctx.astype(BF16), doh, 0, 0)
                for m in range(WIN_ROWS // 2):
                    dtiles_ref[h, j + 2 * m] += ds_loc[:, m * LANES:(m + 1) * LANES]
            return carry

        lax.fori_loop(0, n_rows // ROWS_PER_STEP, step, 0)
        dpre_ref[1] = dk_ref[...].astype(BF16)
        dpre_ref[2] = dv_ref[...].astype(BF16)
        dpre_ref[0, ctx, :] = jnp.zeros((n_ctx, LANES), BF16)
        dpre_ref[3, ctx, :] = jnp.zeros((n_ctx, LANES), BF16)
        _bias_tiles_grad(dtiles_ref, drpb_ref)

    def part(p):
        return pl.BlockSpec((None, t, LANES), lambda h: (p, 0, h))

    lat = pl.BlockSpec((seq, LANES), lambda h: (0, h))
    rspec = pl.BlockSpec((2, 2 * WIN_ROWS, LANES), lambda h: (h, 0, 0))
    tiles = pltpu.VMEM((2, PAIR_TILES, GRID_W, LANES), F32)
    return _pc(
        body, name="attn_bwd", grid=(width // LANES,),
        in_specs=[part(0), part(1), part(2), part(0), lat, lat, rspec],
        out_specs=[pl.BlockSpec((4, t, LANES), lambda h: (0, 0, h)), rspec],
        out_shape=[jax.ShapeDtypeStruct((4, t, width), BF16), jax.ShapeDtypeStruct((heads, 2 * WIN_ROWS, LANES), F32)],
        scratch_shapes=[pltpu.VMEM((2, 2 * WIN_ROWS, LANES), F32), tiles, tiles,
                        pltpu.VMEM((t, LANES), F32), pltpu.VMEM((t, LANES), F32)],
        compiler_params=_params("parallel"),
    )(qkv, qkv, qkv, gate, o, dz, rpb)


def _adamw(w, m, v, parts, name, mult=None):
    rows, cols = w.shape
    tr = _tile(rows, max(8, 131072 // cols), 8)
    n_parts = len(parts)
    c1 = 1.0 - ADAM_B1 ** ADAM_STEP
    c2 = 1.0 - ADAM_B2 ** ADAM_STEP

    def body(*refs):
        w_ref, m_ref, v_ref = refs[:3]
        part_refs = refs[3:3 + n_parts]
        rest = refs[3 + n_parts:]
        g = part_refs[0][...].astype(F32)
        for p in part_refs[1:]:
            g = g + p[...].astype(F32)
        if mult is not None:
            g = g * rest[0][...]
            rest = rest[1:]
        g_ref, d_ref, nm_ref, nv_ref = rest
        m2 = ADAM_B1 * m_ref[...] + (1.0 - ADAM_B1) * g
        v2 = ADAM_B2 * v_ref[...] + (1.0 - ADAM_B2) * (g * g)
        m_hat = m2 / c1
        v_hat = v2 / c2
        g_ref[...] = g
        d_ref[...] = -ADAM_LR * (m_hat / (jnp.sqrt(v_hat) + ADAM_EPS) + ADAM_WD * w_ref[...])
        nm_ref[...] = m2
        nv_ref[...] = v2

    tile = pl.BlockSpec((tr, cols), lambda i: (i, 0))
    in_specs, args = [tile, tile, tile], [w, m, v]
    for p in parts:
        if isinstance(p, tuple):
            arr, k = p
            in_specs.append(pl.BlockSpec((None, tr, cols), lambda i, k=k: (k, i, 0)))
            args.append(arr)
        else:
            in_specs.append(tile)
            args.append(p)
    if mult is not None:
        in_specs.append(tile)
        args.append(mult)
    shape = jax.ShapeDtypeStruct((rows, cols), F32)
    return _pc(
        body, name=name, grid=(rows // tr,), in_specs=in_specs, out_specs=[tile] * 4, out_shape=[shape] * 4,
        compiler_params=_params("parallel"),
    )(*args)


def _rows128(a):
    flat = a.reshape(-1)
    pad = (-flat.shape[0]) % LANES
    if pad:
        flat = jnp.concatenate([flat, jnp.zeros((pad,), flat.dtype)])
    return flat.reshape(-1, LANES)


def _pad_rows(a, mult=8):
    pad = (-a.shape[0]) % mult
    if pad:
        a = jnp.concatenate([a, jnp.zeros((pad,) + a.shape[1:], a.dtype)], axis=0)
    return a


def kernel(x, c, ctx, c_ctx, norm_g, ada_w, ada_b, pool_w_in, pool_w_grp, pool_scale, pool_w_out, na_w_in, na_rpb, na_w_out, conv_w_in, conv_dw, conv_db, conv_w_out, final_g, loss_target, m_c_ctx, m_norm_g, m_ada_w, m_ada_b, m_pool_w_in, m_pool_w_grp, m_pool_scale, m_pool_w_out, m_na_w_in, m_na_rpb, m_na_w_out, m_conv_w_in, m_conv_dw, m_conv_db, m_conv_w_out, m_final_g, v_c_ctx, v_norm_g, v_ada_w, v_ada_b, v_pool_w_in, v_pool_w_grp, v_pool_scale, v_pool_w_out, v_na_w_in, v_na_rpb, v_na_w_out, v_conv_w_in, v_conv_dw, v_conv_db, v_conv_w_out, v_final_g):
    xi, yi, ci = _my_place()
    me = 4 * xi + 2 * yi + ci
    seq, d = x.shape[1], x.shape[2]
    n_ctx = ctx.shape[1]
    t_all = seq + n_ctx
    width = d
    depth = norm_g.shape[0]
    nb = ada_w.shape[2]
    shard = width // N_DEV
    d_rows = d // LANES
    tr = math.gcd(math.gcd(seq, n_ctx), 256)
    x_tiles = seq // tr

    n_pool = pool_scale.shape[0]
    n_grp = pool_w_grp.shape[1]
    grp = width // n_grp
    layer_weights = [[pool_w_in[0], pool_w_grp[0], pool_w_out[0]], [na_w_out[0]],
                     [conv_w_in[0], conv_w_out[0]], [pool_w_in[1], pool_w_grp[1], pool_w_out[1]]]
    in_flight, token = [], jnp.zeros((8, LANES), F32)
    for i, ws in enumerate(layer_weights):
        if i == 1:
            na_in_first = _exchange_start([na_w_in[0].astype(BF16)], False, token, "weights_start1_chips", peers=CHIP_PEERS)
            token = na_in_first[-1]
        state = _exchange_start([w.astype(BF16) for w in ws], False, token, f"weights_start{i}")
        token = state[-1]
        in_flight.append(state)

    def landed_weight(i, t, after):
        return _exchange_wait(in_flight[i], False, after, f"weights_wait{i}_{t}", which=[t])[0]

    def as_in(w):
        return w[:, None]

    def as_grp(w):
        return w.transpose(1, 0, 2, 3).reshape(n_grp, grp, grp)

    def as_out(w):
        return w.reshape(width, d)

    small_in =_pad_rows(jnp.concatenate([_rows128(c), pool_scale, conv_dw[0], conv_db], axis=0))
    got = _gather_small(small_in, "gather_inputs")
    r0 = d_rows
    c_all = got[:, :r0].reshape(N_DEV, d)
    n_pool = pool_scale.shape[0]
    scale_full = got[:, r0:r0 + n_pool].transpose(1, 0, 2).reshape(n_pool, width)
    r1 = r0 + n_pool
    taps_full = _pad_rows(got[:, r1:r1 + 3].transpose(1, 0, 2).reshape(3, width))
    bias_full = got[:, r1 + 3:r1 + 4].transpose(1, 0, 2).reshape(1, width)

    cond = jnp.concatenate([c_all, c_ctx[None], jnp.zeros((7, d), F32)], axis=0)
    bias_mine = lax.dynamic_slice(ada_b, (0, me * nb), (depth, nb))
    mod_mine = _mod_fwd(cond, ada_w, bias_mine)
    mod_all = _gather_small(mod_mine.reshape(-1, LANES), "gather_mod")
    mod_all = mod_all.reshape(N_DEV, depth, 16, nb).transpose(1, 2, 0, 3).reshape(depth, 16, 3 * d)
    mod_x = lax.dynamic_index_in_dim(mod_all, me, 1, keepdims=False).reshape(depth, 3, d)
    mod_c = mod_all[:, 8].reshape(depth, 3, d)
    pad5 = jnp.zeros((depth, 5, d), F32)
    mod_x = jnp.concatenate([mod_x, pad5], axis=1)
    mod_c = jnp.concatenate([mod_c, pad5], axis=1)
    mods = [jnp.stack([mod_x[i], mod_c[i]]) if i < 2 else mod_x[i][None] for i in range(depth)]

    both = [(0, seq), (seq, n_ctx)]
    latent = [(0, seq)]

    def grp_slots(g):
        return g.reshape(n_grp, N_DEV, grp // N_DEV, grp).transpose(1, 0, 2, 3).reshape(N_DEV, -1, grp).astype(BF16)

    def send_grads(i, grads):
        return _exchange_start(grads, True, jnp.zeros((8, LANES), F32), f"grads_start{i}")

    xs0 = jnp.concatenate([x[0], ctx[0]], axis=0)
    h0 = _norm_fwd(xs0, norm_g[0:1] + token[0, 0], mods[0], tr, x_tiles, "norm_fwd0")
    pool_in_w0 = as_in(landed_weight(0, 0, h0))
    pre0 = _proj_in(h0, pool_in_w0, 0, width, "proj_in0")
    pool_grp_w0 = as_grp(landed_weight(0, 1, pre0))
    z0, diff0 = _pool_fwd(pre0, pool_grp_w0, scale_full[0:1], both, "pool_fwd0")
    pool_out_w0 = as_out(landed_weight(0, 2, z0))
    yx0, xs1, h1 = _proj_out(z0, pool_out_w0, xs0, mods[0], tr, x_tiles, "proj_out0", nxt=(norm_g[1:2], mods[1]))

    na_in_half = _exchange_wait(na_in_first, False, xs1, "weights_wait1_chips", peers=CHIP_PEERS)
    na_in_rest = _forward_start(na_in_half, h1, "weights_forward1")
    na_in_w = as_in(_forward_wait(na_in_rest, na_in_rest[-1], "weights_forward_wait1")[0])
    per_part = width // na_w_in.shape[2]
    qkv1 = _proj_in(h1, na_in_w, 0, width, "proj_in1_qkv", blocks=(0, 3 * per_part), dtype=BF16)
    gpre1 = _proj_in(h1, na_in_w, 0, width, "proj_in1_gate", blocks=(3 * per_part, per_part))
    rpb_rows = jnp.pad(na_rpb[0], ((0, 0), (0, 2 * WIN_ROWS - na_rpb.shape[2]), (0, LANES - na_rpb.shape[3])))
    z1, o1 = _attn_fwd(qkv1, gpre1, rpb_rows, seq)
    na_out_w = as_out(landed_weight(1, 0, z1))
    yx1, x2, h2 = _proj_out(z1, na_out_w, xs1, mods[1], tr, x_tiles, "proj_out1", nxt=(norm_g[2:3], mods[2]))

    conv_in_w = as_in(landed_weight(2, 0, h2))
    pre2 = _proj_in(h2, conv_in_w, 0, width, "proj_in2")
    z2 = _conv_fwd(pre2, taps_full, bias_full, "conv_fwd")
    conv_out_w = as_out(landed_weight(2, 1, z2))
    yx2, x3, h3 = _proj_out(z2, conv_out_w, x2, mods[2], tr, x_tiles, "proj_out2", nxt=(norm_g[3:4], mods[3]))

    pool_in_w3 = as_in(landed_weight(3, 0, h3))
    pre3 = _proj_in(h3, pool_in_w3, 0, width, "proj_in3")
    pool_grp_w3 = as_grp(landed_weight(3, 1, pre3))
    z3, diff3 = _pool_fwd(pre3, pool_grp_w3, scale_full[1:2], latent, "pool_fwd3")
    pool_out_w3 = as_out(landed_weight(3, 2, z3))
    yx3, x4 = _proj_out(z3, pool_out_w3, x3, mods[3], tr, x_tiles, "proj_out3")

    loss_part, dx4, d_final, dyx3, gate3 = _loss_head(x4, loss_target[0], final_g[None], yx3, mods[3], tr)
    loss = lax.psum(loss_part[0, 0], ("x", "y", "c"))

    dz3 = _proj_out_dz(dyx3, pool_out_w3, "proj_out_dz3")
    g_pool_out1 = _grad_w_out(z3, dyx3, "grad_w_out3")
    dpre3, g_grp1, g_scale1 = _pool_bwd(dz3, diff3, pre3, pool_grp_w3, scale_full[1:2], latent, "pool_bwd3")
    dh3 = _proj_in_dh(dpre3, pool_in_w3, 0, "proj_in_dh3")
    g_pool_in1 = _grad_w_in(h3, dpre3, pool_w_in.shape[2], "grad_w_in3")
    sent3 = send_grads(3, [g_pool_in1, grp_slots(g_grp1), g_pool_out1.reshape(N_DEV, shard, d)])
    dx3, norm3, dyx2, gate2 = _norm_bwd(x3, dh3, dx4, norm_g[3:4] + sent3[-1][0, 0], mods[3], tr, x_tiles, "norm_bwd3",
                                        below=(yx2, mods[2]))

    dz2 = _proj_out_dz(dyx2, conv_out_w, "proj_out_dz2")
    g_conv_out = _grad_w_out(z2, dyx2, "grad_w_out2")
    dpre2, g_taps, g_cbias = _conv_bwd(dz2, pre2, taps_full, bias_full, "conv_bwd")
    dh2 = _proj_in_dh(dpre2, conv_in_w, 0, "proj_in_dh2")
    g_conv_in = _grad_w_in(h2, dpre2, conv_w_in.shape[2], "grad_w_in2")
    sent2 = send_grads(2, [g_conv_in, g_conv_out.reshape(N_DEV, shard, d)])
    dx2, norm2, dyx1, gate1 = _norm_bwd(x2, dh2, dx3, norm_g[2:3] + sent2[-1][0, 0], mods[2], tr, x_tiles, "norm_bwd2",
                                        below=(yx1, mods[1][:1]))

    dz1 = _proj_out_dz(dyx1, na_out_w, "proj_out_dz1")
    g_na_out = _grad_w_out(z1, dyx1, "grad_w_out1")
    dpre1, g_rpb = _attn_bwd(qkv1, gpre1, o1, dz1, rpb_rows, seq)
    g_rpb = g_rpb[:, :na_rpb.shape[2], :na_rpb.shape[3]]
    dh1 = _proj_in_dh(dpre1, na_in_w, 0, "proj_in_dh1")
    g_na_in = _grad_w_in(h1, dpre1, na_w_in.shape[2], "grad_w_in1")
    sent1 = send_grads(1, [g_na_in, g_na_out.reshape(N_DEV, shard, d)])
    dxs1, norm1, dyx0, gate0 = _norm_bwd(xs1, dh1, dx2, norm_g[1:2] + sent1[-1][0, 0], mods[1], tr, x_tiles, "norm_bwd1",
                                         res_tiles=x_tiles, below=(yx0, mods[0]))

    dz0 = _proj_out_dz(dyx0, pool_out_w0, "proj_out_dz0")
    g_pool_out0 = _grad_w_out(z0, dyx0, "grad_w_out0")
    sent0a = _exchange_start([g_pool_out0.reshape(N_DEV, shard, d)], True, jnp.zeros((8, LANES), F32), "grads_start0a")
    dpre0, g_grp0, g_scale0 = _pool_bwd(dz0, diff0, pre0, pool_grp_w0, scale_full[0:1], both, "pool_bwd0")
    g_pool_in0 = _grad_w_in(h0, dpre0, pool_w_in.shape[2], "grad_w_in0")
    sent0b = _exchange_start([g_pool_in0, grp_slots(g_grp0)], True, sent0a[-1], "grads_start0b")
    dh0 = _proj_in_dh(dpre0, pool_in_w0, 0, "proj_in_dh0", after=sent0b[-1])
    dx0, norm0 = _norm_bwd(xs0, dh0, dxs1, norm_g[0:1], mods[0], tr, x_tiles, "norm_bwd0", out_tiles=x_tiles)
    grad_x = dx0[None]

    norms, gates = [norm0, norm1, norm2, norm3], [gate0, gate1, gate2, gate3]
    zero_d = jnp.zeros((d,), F32)
    dm_rows = [jnp.concatenate([norms[i][0, 0], norms[i][0, 1], gates[i][0, 0]]) for i in range(depth)]
    dm_rows.append(jnp.concatenate([norm0[1, 0], norm0[1, 1], gate0[1, 0]]))
    dm_rows.append(jnp.concatenate([norm1[1, 0], norm1[1, 1], zero_d]))
    dm_local = jnp.stack(dm_rows + [jnp.zeros((3 * d,), F32)] * 2)
    g_norm_part = jnp.stack([norm0[0, 2] + norm0[1, 2], norm1[0, 2] + norm1[1, 2], norm2[0, 2], norm3[0, 2]])
    g_scale_part = jnp.concatenate([g_scale0, g_scale1], axis=0)
    pieces = [_rows128(dm_local), _rows128(g_norm_part), _rows128(d_final[0]), _pad_rows(_rows128(g_rpb)),
              _rows128(g_scale_part), _rows128(g_taps[:3]), _rows128(g_cbias[0])]
    small_sent = _exchange_start([jnp.concatenate(pieces, axis=0)], False, jnp.zeros((8, LANES), F32), "small_grads_start")
    marks = np.cumsum([0] + [p.shape[0] for p in pieces])

    def big(parts, w, m, v, name):
        shape = w.shape
        view = (-1, shape[-1])
        parts = [(parts.reshape((N_DEV,) + w.reshape(view).shape), k) for k in range(N_DEV)]
        return [r.reshape(shape) for r in _adamw(w.reshape(view), m.reshape(view), v.reshape(view), parts, name)]

    in3, grp3, out3 = _exchange_wait(sent3, True, small_sent[-1], "grads_wait3")
    in2, out2 = _exchange_wait(sent2, True, small_sent[-1], "grads_wait2")
    in1, out1 = _exchange_wait(sent1, True, small_sent[-1], "grads_wait1")
    res = {}
    res["na_w_in"] = [r[None] for r in big(in1, na_w_in[0], m_na_w_in[0], v_na_w_in[0], "adamw_na_in")]
    res["na_w_out"] = [r[None] for r in big(out1, na_w_out[0], m_na_w_out[0], v_na_w_out[0], "adamw_na_out")]
    res["conv_w_in"] = [r[None] for r in big(in2, conv_w_in[0], m_conv_w_in[0], v_conv_w_in[0], "adamw_conv_in")]
    res["conv_w_out"] = [r[None] for r in big(out2, conv_w_out[0], m_conv_w_out[0], v_conv_w_out[0], "adamw_conv_out")]
    pool_in1 = big(in3, pool_w_in[1], m_pool_w_in[1], v_pool_w_in[1], "adamw_pool_in1")
    pool_grp1 = big(grp3, pool_w_grp[1], m_pool_w_grp[1], v_pool_w_grp[1], "adamw_pool_grp1")
    pool_out1 = big(out3, pool_w_out[1], m_pool_w_out[1], v_pool_w_out[1], "adamw_pool_out1")

    small_out = _exchange_wait(small_sent, False, pool_out1[0], "small_grads_wait")[0]

    def piece(k):
        return small_out[:, marks[k]:marks[k + 1]]

    dm_all = piece(0).reshape(N_DEV, 8, 3 * d).transpose(1, 0, 2)
    dm_mine = lax.dynamic_slice(dm_all, (0, 0, me * nb), (8, N_DEV, nb))
    g_ada_w, g_ada_b, cctx_part, dsilu_cond = _mod_bwd(cond, ada_w, dm_all, dm_mine)
    cctx_all = _gather_small(_rows128(cctx_part[0]), "gather_cctx")

    def my_shard(a, n):
        a = a.reshape(N_DEV, n, N_DEV, shard)
        return lax.dynamic_index_in_dim(a, me, 2, keepdims=False)

    zeros7 = lambda r: jnp.zeros((N_DEV - 1, r, LANES), F32)
    ada_b_rows = _rows128(g_ada_b)
    small_parts = jnp.concatenate([
        cctx_all, piece(1), jnp.concatenate([ada_b_rows[None], zeros7(ada_b_rows.shape[0])], axis=0), piece(2), piece(3),
        my_shard(piece(4), n_pool), my_shard(piece(5), 3), my_shard(piece(6), 1)], axis=1)
    n_small = small_parts.shape[1]
    small_parts = jnp.concatenate([small_parts, jnp.zeros((N_DEV, (-n_small) % 8, LANES), F32)], axis=1)

    def pack(c_ctx_, norm_g_, ada_b_, final_g_, rpb_, scale_, taps_, cbias_):
        rows = [_rows128(c_ctx_), _rows128(norm_g_), _rows128(ada_b_), _rows128(final_g_), _pad_rows(_rows128(rpb_)),
                scale_, taps_[0], cbias_]
        return _pad_rows(jnp.concatenate(rows, axis=0)), np.cumsum([0] + [r.shape[0] for r in rows])

    w_small, smarks = pack(c_ctx, norm_g, ada_b, final_g, na_rpb, pool_scale, conv_dw, conv_db)
    m_small, _ = pack(m_c_ctx, m_norm_g, m_ada_b, m_final_g, m_na_rpb, m_pool_scale, m_conv_dw, m_conv_db)
    v_small, _ = pack(v_c_ctx, v_norm_g, v_ada_b, v_final_g, v_na_rpb, v_pool_scale, v_conv_dw, v_conv_db)
    mult = jnp.concatenate([_rows128(dsilu_cond[8]), jnp.ones((w_small.shape[0] - d_rows, LANES), F32)], axis=0)
    small_res = _adamw(w_small, m_small, v_small, [(small_parts, k) for k in range(N_DEV)], "adamw_small", mult=mult)

    def unpack(k, like):
        out = []
        for r in small_res:
            flat = r[smarks[k]:smarks[k + 1]].reshape(-1)
            out.append(flat[:like.size].reshape(like.shape))
        return out

    res.update({"c_ctx": unpack(0, c_ctx), "norm_g": unpack(1, norm_g), "ada_b": unpack(2, ada_b),
                "final_g": unpack(3, final_g), "na_rpb": unpack(4, na_rpb), "pool_scale": unpack(5, pool_scale),
                "conv_dw": unpack(6, conv_dw), "conv_db": unpack(7, conv_db)})

    res["ada_w"] = [r.reshape(ada_w.shape) for r in _adamw(
        ada_w.reshape(-1, nb), m_ada_w.reshape(-1, nb), v_ada_w.reshape(-1, nb), [g_ada_w.reshape(-1, nb)], "adamw_ada_w")]

    out0, = _exchange_wait(sent0a, True, small_res[0], "grads_wait0a")
    in0, grp0 = _exchange_wait(sent0b, True, small_res[0], "grads_wait0b")
    pool_in0 = big(in0, pool_w_in[0], m_pool_w_in[0], v_pool_w_in[0], "adamw_pool_in0")
    pool_grp0 = big(grp0, pool_w_grp[0], m_pool_w_grp[0], v_pool_w_grp[0], "adamw_pool_grp0")
    pool_out0 = big(out0, pool_w_out[0], m_pool_w_out[0], v_pool_w_out[0], "adamw_pool_out0")
    res["pool_w_in"] = [jnp.stack([p, q]) for p, q in zip(pool_in0, pool_in1)]
    res["pool_w_grp"] = [jnp.stack([p, q]) for p, q in zip(pool_grp0, pool_grp1)]
    res["pool_w_out"] = [jnp.stack([p, q]) for p, q in zip(pool_out0, pool_out1)]

    order = ["c_ctx", "norm_g", "ada_w", "ada_b", "pool_w_in", "pool_w_grp", "pool_scale", "pool_w_out", "na_w_in",
             "na_rpb", "na_w_out", "conv_w_in", "conv_dw", "conv_db", "conv_w_out", "final_g"]
    outs = [loss, grad_x]
    for j in range(4):
        outs += [res[n][j] for n in order]
    return tuple(outs)
```

```python
import functools
import math

import numpy as np
import jax
import jax.numpy as jnp
from jax import lax
from jax.experimental import pallas as pl
from jax.experimental.pallas import tpu as pltpu

F32 = jnp.float32
BF16 = jnp.bfloat16
N_DEV = 8
LANES = 128
RMS_EPS = 1e-6
GRID_W = 64
WIN_ROWS = 8
WIN_COLS = 16
HEAD_DIM = 64
POOL_WINDOWS = (2, 4, 8, 16)
HALO = 8
CHUNK = 128
MASKED = -1e30
ADAM_LR = 0.001
ADAM_B1 = 0.9
ADAM_B2 = 0.999
ADAM_EPS = 1e-08
ADAM_WD = 0.01
ADAM_STEP = 10
VMEM_LIMIT = 56 * 1024 * 1024
MESH = pl.DeviceIdType.MESH
ANY = pl.BlockSpec(memory_space=pl.ANY)
HBM = pl.BlockSpec(memory_space=pltpu.HBM)
SEM = pl.BlockSpec(memory_space=pltpu.SEMAPHORE)
EFFECT = pltpu.SideEffectType.DATAFLOW_SIDE_EFFECTING


def _pc(body, *, name, **kw):
    return pl.pallas_call(body, name=name, **kw)


def _params(*sem):
    return pltpu.CompilerParams(dimension_semantics=sem if sem else None, vmem_limit_bytes=VMEM_LIMIT)


def _dot(a, b, ca=1, cb=0, precision=None):
    return lax.dot_general(a, b, (((ca,), (cb,)), ((), ())), preferred_element_type=F32, precision=precision)


def _tile(n, pref, unit=LANES):
    best = None
    for t in range(unit, min(n, pref) + 1, unit):
        if n % t == 0:
            best = t
    return best if best is not None else n


def _sigmoid(x):
    return 1.0 / (1.0 + jnp.exp(-x))


def _silu(x):
    return x * _sigmoid(x)


def _dsilu(x):
    s = _sigmoid(x)
    return s * (1.0 + x * (1.0 - s))


def _my_place():
    return lax.axis_index("x"), lax.axis_index("y"), lax.axis_index("c")


def _flip(v, f):
    return 1 - v if f else v


def _gather_small(block, name):
    rows, cols = block.shape

    def body(x_ref, out_ref, send_sems, recv_sems):
        x, y, c = _my_place()
        me = 4 * x + 2 * y + c
        out_ref[me] = x_ref[...]
        copies = []
        for k in range(1, N_DEV):
            peer = (_flip(x, k & 4), _flip(y, k & 2), _flip(c, k & 1))
            cp = pltpu.make_async_remote_copy(
                src_ref=x_ref, dst_ref=out_ref.at[me], send_sem=send_sems.at[k - 1], recv_sem=recv_sems.at[k - 1],
                device_id=peer, device_id_type=MESH)
            cp.start()
            copies.append(cp)
        for cp in copies:
            cp.wait()

    return _pc(
        body, name=name,
        out_shape=jax.ShapeDtypeStruct((N_DEV, rows, cols), block.dtype),
        in_specs=[pl.BlockSpec(memory_space=pltpu.VMEM)],
        out_specs=pl.BlockSpec(memory_space=pltpu.VMEM),
        scratch_shapes=[pltpu.SemaphoreType.DMA((N_DEV - 1,)), pltpu.SemaphoreType.DMA((N_DEV - 1,))],
    )(block)


ALL_PEERS = tuple(range(N_DEV))
CHIP_PEERS = (0, 1, 2, 4, 6)
OTHER_CHIPS = (2, 4, 6)


def _peer(k):
    x, y, c = _my_place()
    peer = (_flip(x, k & 4), _flip(y, k & 2), _flip(c, k & 1))
    return peer, 4 * peer[0] + 2 * peer[1] + peer[2]


def _exchange_copies(srcs, lands, send_sems, recv_sems, per_dest, peers=ALL_PEERS):
    x, y, c = _my_place()
    me = 4 * x + 2 * y + c
    copies = []
    for t, (src, land) in enumerate(zip(srcs, lands)):
        for n, k in enumerate(peers):
            peer, dest = _peer(k)
            s = t * len(peers) + n
            copies.append(pltpu.make_async_remote_copy(
                src_ref=src.at[dest] if per_dest else src, dst_ref=land.at[me],
                send_sem=send_sems[s], recv_sem=recv_sems[s], device_id=peer, device_id_type=MESH))
    return copies


def _forward_copies(lands, send_sems, recv_sems):
    sibling, _ = _peer(1)
    copies = []
    for t, land in enumerate(lands):
        for n, k in enumerate(OTHER_CHIPS):
            _, slot = _peer(k)
            s = t * len(OTHER_CHIPS) + n
            copies.append(pltpu.make_async_remote_copy(
                src_ref=land.at[slot], dst_ref=land.at[slot], send_sem=send_sems[s], recv_sem=recv_sems[s],
                device_id=sibling, device_id_type=MESH))
    return copies


def _forward_start(lands, after, name):
    nt = len(lands)
    ns = nt * len(OTHER_CHIPS)

    def body(*refs):
        ins, outs = refs[:nt + 1], refs[nt + 1:]
        for cp in _forward_copies(ins[:nt], outs[:ns], outs[ns:2 * ns]):
            cp.start()
        outs[-1][...] = jnp.zeros_like(outs[-1])

    res = _pc(
        body, name=name,
        out_shape=(*[pltpu.SemaphoreType.DMA(())] * (2 * ns), *[pltpu.HBM(a.shape, a.dtype) for a in lands],
                   jax.ShapeDtypeStruct((8, LANES), F32)),
        in_specs=[HBM] * nt + [ANY],
        out_specs=(*[SEM] * (2 * ns), *[HBM] * nt, pl.BlockSpec(memory_space=pltpu.VMEM)),
        input_output_aliases={i: 2 * ns + i for i in range(nt)},
        compiler_params=pltpu.CompilerParams(has_side_effects=EFFECT),
    )(*lands, after)
    return list(res[:ns]), list(res[ns:2 * ns]), list(res[2 * ns:2 * ns + nt]), res[-1]


def _forward_wait(state, after, name):
    send_sems, recv_sems, lands, _ = state
    nt, ns = len(lands), len(send_sems)

    def body(*refs):
        sems = refs[nt:nt + 2 * ns]
        for cp in _forward_copies(refs[:nt], sems[:ns], sems[ns:]):
            cp.wait_send()
            cp.wait_recv()

    res = _pc(
        body, name=name,
        out_shape=tuple(pltpu.HBM(a.shape, a.dtype) for a in lands),
        in_specs=[HBM] * nt + [SEM] * (2 * ns) + [ANY],
        out_specs=tuple([HBM] * nt),
        input_output_aliases={i: i for i in range(nt)},
        compiler_params=pltpu.CompilerParams(has_side_effects=EFFECT),
    )(*lands, *send_sems, *recv_sems, after)
    return list(res)


def _exchange_start(srcs, per_dest, after, name, peers=ALL_PEERS):
    nt = len(srcs)
    ns = nt * len(peers)
    lands = [lax.empty((N_DEV,) + (s.shape[1:] if per_dest else s.shape), s.dtype) for s in srcs]

    def body(*refs):
        ins, outs = refs[:2 * nt + 1], refs[2 * nt + 1:]
        for cp in _exchange_copies(ins[:nt], ins[nt:2 * nt], outs[:ns], outs[ns:2 * ns], per_dest, peers):
            cp.start()
        outs[-1][...] = jnp.zeros_like(outs[-1])

    hbm = [pltpu.with_memory_space_constraint(a, pltpu.HBM) for a in list(srcs) + lands]
    res = _pc(
        body, name=name,
        out_shape=(*[pltpu.SemaphoreType.DMA(())] * (2 * ns), *[pltpu.HBM(a.shape, a.dtype) for a in hbm],
                   jax.ShapeDtypeStruct((8, LANES), F32)),
        in_specs=[HBM] * (2 * nt) + [ANY],
        out_specs=(*[SEM] * (2 * ns), *[HBM] * (2 * nt), pl.BlockSpec(memory_space=pltpu.VMEM)),
        input_output_aliases={i: 2 * ns + i for i in range(2 * nt)},
        compiler_params=pltpu.CompilerParams(has_side_effects=EFFECT),
    )(*hbm, after)
    sems, rest = res[:2 * ns], res[2 * ns:]
    return list(sems[:ns]), list(sems[ns:]), list(rest[:nt]), list(rest[nt:2 * nt]), rest[-1]


def _exchange_wait(state, per_dest, after, name, which=None, peers=ALL_PEERS):
    send_sems, recv_sems, srcs, lands, _ = state
    which = list(range(len(srcs))) if which is None else which
    pick = [t * len(peers) + n for t in which for n in range(len(peers))]
    send_sems, recv_sems = [send_sems[s] for s in pick], [recv_sems[s] for s in pick]
    srcs, lands = [srcs[t] for t in which], [lands[t] for t in which]
    nt = len(srcs)
    ns = len(send_sems)

    def body(*refs):
        sems = refs[2 * nt:2 * nt + 2 * ns]
        for cp in _exchange_copies(refs[:nt], refs[nt:2 * nt], sems[:ns], sems[ns:], per_dest, peers):
            cp.wait_send()
            cp.wait_recv()

    thru = list(srcs) + list(lands)
    res = _pc(
        body, name=name,
        out_shape=tuple(pltpu.HBM(a.shape, a.dtype) for a in thru),
        in_specs=[HBM] * (2 * nt) + [SEM] * (2 * ns) + [ANY],
        out_specs=tuple([HBM] * (2 * nt)),
        input_output_aliases={i: i for i in range(2 * nt)},
        compiler_params=pltpu.CompilerParams(has_side_effects=EFFECT),
    )(*thru, *send_sems, *recv_sems, after)
    return list(res[nt:])


def _mod_fwd(cond, ada_w, bias):
    depth, d, nb = ada_w.shape

    def body(c_ref, w_ref, b_ref, o_ref):
        s = _silu(c_ref[...]).astype(BF16)
        o_ref[...] = _dot(s, w_ref[...].astype(BF16)) + b_ref[...]

    return _pc(
        body, name="mod_fwd", grid=(depth,),
        in_specs=[pl.BlockSpec((16, d), lambda i: (0, 0)), pl.BlockSpec((None, d, nb), lambda i: (i, 0, 0)),
                  pl.BlockSpec((None, 1, nb), lambda i: (i, 0, 0))],
        out_specs=pl.BlockSpec((None, 16, nb), lambda i: (i, 0, 0)),
        out_shape=jax.ShapeDtypeStruct((depth, 16, nb), F32),
        compiler_params=_params("parallel"),
    )(cond, ada_w, bias.reshape(depth, 1, nb))


def _mod_bwd(cond, ada_w, dm_all, dm_mine):
    depth, d, nb = ada_w.shape
    d3 = dm_all.shape[-1]

    def body(c_ref, w_ref, all_ref, call_ref, mine_ref, cmine_ref, gw_ref, gb_ref, part_ref, ds_ref):
        i = pl.program_id(0)
        cond_v = c_ref[...]
        s = _silu(cond_v).astype(BF16)
        has_ctx = jnp.where(i < 2, 1.0, 0.0)
        tot_all = jnp.sum(call_ref[...], axis=0, keepdims=True) * has_ctx
        tot_mine = jnp.broadcast_to(jnp.sum(cmine_ref[...], axis=0, keepdims=True) * has_ctx, (8, nb)).astype(BF16)
        gb_ref[...] = jnp.sum(all_ref[...], axis=0, keepdims=True) + tot_all
        gw_ref[...] = _dot(s[0:8], mine_ref[...].astype(BF16), 0, 0) + _dot(s[8:16], tot_mine, 0, 0)
        part = _dot(tot_mine, w_ref[...].astype(BF16), 1, 1)

        @pl.when(i == 0)
        def _():
            part_ref[...] = jnp.zeros_like(part_ref)
            ds_ref[...] = _dsilu(cond_v)

        part_ref[...] += part

    def rows(width, which):
        return pl.BlockSpec((None, N_DEV, width), which)

    layer = lambda i: (i, 0, 0)
    ctx_layer = lambda i: (jnp.minimum(i, 1) + 4, 0, 0)
    return _pc(
        body, name="mod_bwd", grid=(depth,),
        in_specs=[pl.BlockSpec((16, d), lambda i: (0, 0)), pl.BlockSpec((None, d, nb), layer),
                  rows(d3, layer), rows(d3, ctx_layer), rows(nb, layer), rows(nb, ctx_layer)],
        out_specs=[pl.BlockSpec((None, d, nb), layer), pl.BlockSpec((None, 1, d3), layer),
                   pl.BlockSpec((8, d), lambda i: (0, 0)), pl.BlockSpec((16, d), lambda i: (0, 0))],
        out_shape=[jax.ShapeDtypeStruct((depth, d, nb), F32), jax.ShapeDtypeStruct((depth, 1, d3), F32),
                   jax.ShapeDtypeStruct((8, d), F32), jax.ShapeDtypeStruct((16, d), F32)],
        compiler_params=_params("arbitrary"),
    )(cond, ada_w, dm_all, dm_all, dm_mine, dm_mine)


def _norm_fwd(xs, g, mod, tr, seg_tiles, name):
    t, d = xs.shape

    def body(x_ref, g_ref, mod_ref, h_ref):
        x = x_ref[...]
        r = lax.rsqrt(jnp.mean(x * x, axis=-1, keepdims=True) + RMS_EPS)
        y = (x * r) * g_ref[...]
        h_ref[...] = (y * (1.0 + mod_ref[1:2, :]) + mod_ref[0:1, :]).astype(BF16)

    return _pc(
        body, name=name, grid=(t // tr,),
        in_specs=[pl.BlockSpec((tr, d), lambda i: (i, 0)), pl.BlockSpec((1, d), lambda i: (0, 0)),
                  pl.BlockSpec((None, 8, d), lambda i: (i // seg_tiles, 0, 0))],
        out_specs=pl.BlockSpec((tr, d), lambda i: (i, 0)),
        out_shape=jax.ShapeDtypeStruct((t, d), BF16),
        compiler_params=_params("parallel"),
    )(xs, g, mod)


def _resid_grad(dx, i, seg_tiles, yx_ref, gate_ref, dyx_ref, gsum_ref):
    dyx_ref[...] = (dx * gate_ref[2:3, :]).astype(BF16)

    @pl.when(i % seg_tiles == 0)
    def _():
        gsum_ref[...] = jnp.zeros_like(gsum_ref)

    gsum_ref[0:1, :] += jnp.sum(dx * yx_ref[...], axis=0, keepdims=True)


def _norm_bwd(xs, dh, dres, g, mod, tr, seg_tiles, name, res_tiles=None, out_tiles=None, below=None):
    t, d = xs.shape
    n_tiles = t // tr
    res_tiles = n_tiles if res_tiles is None else res_tiles
    out_tiles = n_tiles if out_tiles is None else out_tiles

    def body(x_ref, dh_ref, dres_ref, g_ref, mod_ref, *rest):
        i = pl.program_id(0)
        x = x_ref[...]
        r = lax.rsqrt(jnp.mean(x * x, axis=-1, keepdims=True) + RMS_EPS)
        xn = x * r
        dhv = dh_ref[...]
        gain = g_ref[...]
        one_scale = 1.0 + mod_ref[1:2, :]
        dxn = dhv * (gain * one_scale)
        dx = r * (dxn - xn * jnp.mean(dxn * xn, axis=-1, keepdims=True))
        if res_tiles == n_tiles:
            dx = dx + dres_ref[...]
        else:
            dx = dx + jnp.where(i < res_tiles, dres_ref[...], 0.0)
        if below is None:
            dx_ref, sum_ref = rest
        else:
            yx_ref, gate_ref, dx_ref, sum_ref, dyx_ref, gsum_ref = rest
            _resid_grad(dx, i, seg_tiles, yx_ref, gate_ref, dyx_ref, gsum_ref)
        if out_tiles == n_tiles:
            dx_ref[...] = dx
        else:
            @pl.when(i < out_tiles)
            def _():
                dx_ref[...] = dx

        @pl.when(i % seg_tiles == 0)
        def _():
            sum_ref[...] = jnp.zeros_like(sum_ref)

        sum_ref[0:1, :] += jnp.sum(dhv, axis=0, keepdims=True)
        sum_ref[1:2, :] += jnp.sum(dhv * (xn * gain), axis=0, keepdims=True)
        sum_ref[2:3, :] += jnp.sum(dhv * one_scale * xn, axis=0, keepdims=True)

    row = pl.BlockSpec((tr, d), lambda i: (i, 0))
    seg = pl.BlockSpec((None, 8, d), lambda i: (i // seg_tiles, 0, 0))
    in_specs = [row, row, pl.BlockSpec((tr, d), lambda i: (jnp.minimum(i, res_tiles - 1), 0)),
                pl.BlockSpec((1, d), lambda i: (0, 0)), seg]
    out_specs = [pl.BlockSpec((tr, d), lambda i: (jnp.minimum(i, out_tiles - 1), 0)), seg]
    out_shape = [jax.ShapeDtypeStruct((out_tiles * tr, d), F32), jax.ShapeDtypeStruct((mod.shape[0], 8, d), F32)]
    args = [xs, dh, dres, g, mod]
    if below is not None:
        in_specs += [row, seg]
        out_specs += [row, seg]
        out_shape += [jax.ShapeDtypeStruct((t, d), BF16), jax.ShapeDtypeStruct((below[1].shape[0], 8, d), F32)]
        args += list(below)
    return _pc(
        body, name=name, grid=(n_tiles,), in_specs=in_specs, out_specs=out_specs, out_shape=out_shape,
        compiler_params=_params("arbitrary"),
    )(*args)


def _loss_head(xs, target, g, yx, mod, tr):
    t, d = xs.shape

    def body(x_ref, t_ref, g_ref, yx_ref, gate_ref, loss_ref, dx_ref, dg_ref, dyx_ref, gsum_ref):
        i = pl.program_id(0)
        x = x_ref[...]
        r = lax.rsqrt(jnp.mean(x * x, axis=-1, keepdims=True) + RMS_EPS)
        xn = x * r
        gain = g_ref[...]
        err = xn * gain - t_ref[...]
        dy = err * (1.0 / d)
        dxn = dy * gain
        dx = r * (dxn - xn * jnp.mean(dxn * xn, axis=-1, keepdims=True))
        dx_ref[...] = dx
        _resid_grad(dx, i, t // tr, yx_ref, gate_ref, dyx_ref, gsum_ref)

        @pl.when(i == 0)
        def _():
            loss_ref[...] = jnp.zeros_like(loss_ref)
            dg_ref[...] = jnp.zeros_like(dg_ref)

        loss_ref[...] += 0.5 * jnp.sum(jnp.mean(err * err, axis=-1, keepdims=True))
        dg_ref[0:1, :] += jnp.sum(dy * xn, axis=0, keepdims=True)

    row = pl.BlockSpec((tr, d), lambda i: (i, 0))
    seg = pl.BlockSpec((None, 8, d), lambda i: (0, 0, 0))
    return _pc(
        body, name="loss_head", grid=(t // tr,),
        in_specs=[row, row, pl.BlockSpec((1, d), lambda i: (0, 0)), row, seg],
        out_specs=[pl.BlockSpec((8, LANES), lambda i: (0, 0)), row, pl.BlockSpec((8, d), lambda i: (0, 0)), row, seg],
        out_shape=[jax.ShapeDtypeStruct((8, LANES), F32), jax.ShapeDtypeStruct((t, d), F32),
                   jax.ShapeDtypeStruct((8, d), F32), jax.ShapeDtypeStruct((t, d), BF16),
                   jax.ShapeDtypeStruct((1, 8, d), F32)],
        compiler_params=_params("arbitrary"),
    )(xs, target, g, yx, mod)


def _proj_in(h, w, layer, width, name, blocks=None, dtype=F32):
    t, d = h.shape
    n8 = w.shape[-1]
    first, count = blocks if blocks is not None else (0, N_DEV)
    per_part = width // n8
    tm = _tile(t, 1152)

    def body(a_ref, b_ref, o_ref):
        o_ref[...] = _dot(a_ref[...], b_ref[...]).astype(dtype)

    return _pc(
        body, name=name, grid=(t // tm, count),
        in_specs=[pl.BlockSpec((tm, d), lambda i, j: (i, 0)),
                  pl.BlockSpec((None, None, d, n8), lambda i, j: (first + j, layer, 0, 0))],
        out_specs=pl.BlockSpec((None, tm, n8), lambda i, j: (j // per_part, i, j % per_part)),
        out_shape=jax.ShapeDtypeStruct((count // per_part, t, width), dtype),
        compiler_params=_params("parallel", "parallel"),
    )(h, w)


def _proj_out(z, w, res, mod, tm, seg_tiles, name, nxt=None):
    t, k = z.shape
    d = w.shape[1]

    def body(z_ref, w_ref, res_ref, mod_ref, *rest):
        yx = _dot(z_ref[...], w_ref[...])
        x = res_ref[...] + mod_ref[2:3, :] * yx
        if nxt is None:
            yx_ref, x_ref = rest
        else:
            g_ref, nmod_ref, yx_ref, x_ref, h_ref = rest
            r = lax.rsqrt(jnp.mean(x * x, axis=-1, keepdims=True) + RMS_EPS)
            h_ref[...] = (((x * r) * g_ref[...]) * (1.0 + nmod_ref[1:2, :]) + nmod_ref[0:1, :]).astype(BF16)
        yx_ref[...] = yx
        x_ref[...] = x

    tile = pl.BlockSpec((tm, d), lambda i: (i, 0))
    seg = pl.BlockSpec((None, 8, d), lambda i: (i // seg_tiles, 0, 0))
    in_specs = [pl.BlockSpec((tm, k), lambda i: (i, 0)), pl.BlockSpec((k, d), lambda i: (0, 0)), tile, seg]
    out_specs = [tile, tile]
    out_shape = [jax.ShapeDtypeStruct((t, d), F32), jax.ShapeDtypeStruct((t, d), F32)]
    args = [z, w, res, mod]
    if nxt is not None:
        in_specs += [pl.BlockSpec((1, d), lambda i: (0, 0)), seg]
        out_specs.append(tile)
        out_shape.append(jax.ShapeDtypeStruct((t, d), BF16))
        args += list(nxt)
    return _pc(
        body, name=name, grid=(t // tm,), in_specs=in_specs, out_specs=out_specs, out_shape=out_shape,
        compiler_params=_params("parallel"),
    )(*args)


def _proj_out_dz(dyx, w, name):
    t, d = dyx.shape
    width = w.shape[0]
    tm, tn = _tile(t, 1024), _tile(width, 512)

    def body(a_ref, w_ref, o_ref):
        o_ref[...] = _dot(a_ref[...], w_ref[...], 1, 1)

    return _pc(
        body, name=name, grid=(t // tm, width // tn),
        in_specs=[pl.BlockSpec((tm, d), lambda i, j: (i, 0)), pl.BlockSpec((tn, d), lambda i, j: (j, 0))],
        out_specs=pl.BlockSpec((tm, tn), lambda i, j: (i, j)),
        out_shape=jax.ShapeDtypeStruct((t, width), F32),
        compiler_params=_params("parallel", "parallel"),
    )(dyx, w)


def _proj_in_dh(dpre, w, layer, name, after=None):
    parts, t, width = dpre.shape
    d, n8 = w.shape[-2:]
    per_part = width // n8
    tm, tn = _tile(t, 1152), _tile(d, 512)

    def body(a_ref, w_ref, *rest):
        o_ref = rest[-1]
        part = _dot(a_ref[:, 0:n8], w_ref[0], 1, 1)
        for s in range(1, per_part):
            part += _dot(a_ref[:, s * n8:(s + 1) * n8], w_ref[s], 1, 1)

        @pl.when(pl.program_id(2) == 0)
        def _():
            o_ref[...] = part

        @pl.when(pl.program_id(2) != 0)
        def _():
            o_ref[...] += part

    extra = [] if after is None else [after]
    return _pc(
        body, name=name, grid=(t // tm, d // tn, parts),
        in_specs=[pl.BlockSpec((None, tm, width), lambda i, j, k: (k, i, 0)),
                  pl.BlockSpec((per_part, None, tn, n8), lambda i, j, k: (k, layer, j, 0))] + [ANY] * len(extra),
        out_specs=pl.BlockSpec((tm, tn), lambda i, j, k: (i, j)),
        out_shape=jax.ShapeDtypeStruct((t, d), F32),
        compiler_params=_params("parallel", "parallel", "arbitrary"),
    )(dpre, w, *extra)


def _transposed(a_ref):
    return a_ref[...].T


def _grad_w_in(h, dpre, n8, name):
    t, d = h.shape
    parts, _, width = dpre.shape
    per_part = width // n8
    tm, tk = _tile(d, 512), _tile(t, 1152)
    nk = t // tk

    def body(a_ref, b_ref, o_ref, acc_ref):
        k = pl.program_id(1)

        @pl.when(k == 0)
        def _():
            acc_ref[...] = jnp.zeros_like(acc_ref)

        at = _transposed(a_ref)
        for p in range(parts):
            r = _dot(at, b_ref[p])
            for s in range(per_part):
                acc_ref[p * per_part + s] += r[:, s * n8:(s + 1) * n8]

        @pl.when(k == nk - 1)
        def _():
            o_ref[...] = acc_ref[...].astype(BF16)

    return _pc(
        body, name=name, grid=(d // tm, nk),
        in_specs=[pl.BlockSpec((tk, tm), lambda i, k: (k, i)), pl.BlockSpec((parts, tk, width), lambda i, k: (0, k, 0))],
        out_specs=pl.BlockSpec((parts * per_part, tm, n8), lambda i, k: (0, i, 0)),
        out_shape=jax.ShapeDtypeStruct((parts * per_part, d, n8), BF16),
        scratch_shapes=[pltpu.VMEM((parts * per_part, tm, n8), F32)],
        compiler_params=_params("parallel", "arbitrary"),
    )(h, dpre)


def _grad_w_out(z, dyx, name):
    width = z.shape[1]
    t, d = dyx.shape
    tm, tk = _tile(width, 512), _tile(t, 1152)
    nk = t // tk

    def body(a_ref, b_ref, o_ref, acc_ref):
        k = pl.program_id(1)

        @pl.when(k == 0)
        def _():
            acc_ref[...] = jnp.zeros_like(acc_ref)

        acc_ref[...] += _dot(_transposed(a_ref), b_ref[...])

        @pl.when(k == nk - 1)
        def _():
            o_ref[...] = acc_ref[...].astype(BF16)

    return _pc(
        body, name=name, grid=(width // tm, nk),
        in_specs=[pl.BlockSpec((tk, tm), lambda i, k: (k, i)), pl.BlockSpec((tk, d), lambda i, k: (k, 0))],
        out_specs=pl.BlockSpec((tm, d), lambda i, k: (i, 0)),
        out_shape=jax.ShapeDtypeStruct((width, d), BF16),
        scratch_shapes=[pltpu.VMEM((tm, d), F32)],
        compiler_params=_params("parallel", "arbitrary"),
    )(z, dyx)


def _shift(v, k):
    n = v.shape[0]
    return pltpu.roll(v, k % n, 0)


def _window_sum(v, win):
    s = v + _shift(v, 1)
    step = 1
    while 2 * step < win:
        s = _shift(s, step) + _shift(s, -step)
        step *= 2
    return s


def _window_count(base, seg_len, win, shape):
    t = base + lax.broadcasted_iota(jnp.int32, shape, 0)
    hi = jnp.minimum(t + win // 2, seg_len)
    lo = jnp.maximum(t - win // 2, 0)
    return (hi - lo).astype(F32)


def _pad_offsets(segs):
    return [HALO * (s + 1) + st for s, (st, _) in enumerate(segs)]


def _for_chunks(segs, fn):
    offs = _pad_offsets(segs)
    for s, (st, ln) in enumerate(segs):
        def step(ci, carry, s=s, st=st, ln=ln):
            fn(s, st, ln, offs[s], pl.multiple_of(ci * CHUNK, CHUNK))
            return carry
        lax.fori_loop(0, ln // CHUNK, step, 0)


def _pool_fwd(pre, w_grp, scale, segs, name):
    _, t, width = pre.shape
    grp = width // len(POOL_WINDOWS)
    padded = t + HALO * (len(segs) + 1)

    def group(win, pre_ref, w_ref, sc_ref, z_ref, diff_ref, pad_ref):
        pad_ref[...] = jnp.zeros_like(pad_ref)

        def fill(s, st, ln, off, b):
            pad_ref[pl.ds(off + b, CHUNK), :] = pre_ref[0, pl.ds(st + b, CHUNK), :]

        _for_chunks(segs, fill)

        def mix(s, st, ln, off, b):
            ext = pad_ref[pl.ds(off - HALO + b, CHUNK + 2 * HALO), :]
            total = _window_sum(ext, win)[HALO:HALO + CHUNK]
            u = pre_ref[0, pl.ds(st + b, CHUNK), :]
            diff = (total / _window_count(b, ln, win, u.shape) - u).astype(BF16)
            mixed = _dot(diff, w_ref[...])
            gate = _silu(pre_ref[1, pl.ds(st + b, CHUNK), :])
            z_ref[pl.ds(st + b, CHUNK), :] = (mixed * sc_ref[...] * gate).astype(BF16)
            diff_ref[pl.ds(st + b, CHUNK), :] = diff

        _for_chunks(segs, mix)

    def body(pre_ref, w_ref, sc_ref, z_ref, diff_ref, pad_ref):
        gi = pl.program_id(0)
        for widx, win in enumerate(POOL_WINDOWS):
            @pl.when(gi == widx)
            def _(win=win):
                group(win, pre_ref, w_ref, sc_ref, z_ref, diff_ref, pad_ref)

    col = pl.BlockSpec((t, grp), lambda g: (0, g))
    return _pc(
        body, name=name, grid=(len(POOL_WINDOWS),),
        in_specs=[pl.BlockSpec((2, t, grp), lambda g: (0, 0, g)), pl.BlockSpec((None, grp, grp), lambda g: (g, 0, 0)),
                  pl.BlockSpec((1, grp), lambda g: (0, g))],
        out_specs=[col, col],
        out_shape=[jax.ShapeDtypeStruct((t, width), BF16), jax.ShapeDtypeStruct((t, width), BF16)],
        scratch_shapes=[pltpu.VMEM((padded, grp), F32)],
        compiler_params=_params("parallel"),
    )(pre, w_grp, scale)


def _pool_bwd(dz, diff, pre, w_grp, scale, segs, name):
    _, t, width = pre.shape
    grp = width // len(POOL_WINDOWS)
    padded = t + HALO * (len(segs) + 1)

    def group(win, dz_ref, diff_ref, pre_ref, w_ref, sc_ref, dpre_ref, dw_ref, dsc_ref, pad_ref, dd_ref):
        pad_ref[...] = jnp.zeros_like(pad_ref)
        dw_ref[...] = jnp.zeros_like(dw_ref)
        dsc_ref[...] = jnp.zeros_like(dsc_ref)

        def first(s, st, ln, off, b):
            rows = pl.ds(st + b, CHUNK)
            diff_v = diff_ref[rows, :]
            mixed = _dot(diff_v, w_ref[...])
            g = pre_ref[1, rows, :]
            sg = _silu(g)
            dzv = dz_ref[rows, :]
            dmixed = (dzv * sc_ref[...] * sg).astype(BF16)
            dsc_ref[...] += jnp.sum(dzv * mixed * sg, axis=0, keepdims=True)
            dpre_ref[1, rows, :] = (dzv * mixed * sc_ref[...] * _dsilu(g)).astype(BF16)
            ddiff = _dot(dmixed, w_ref[...], 1, 1)
            dw_ref[...] += _dot(diff_v, dmixed, 0, 0)
            dd_ref[rows, :] = ddiff
            pad_ref[pl.ds(off + b, CHUNK), :] = ddiff / _window_count(b, ln, win, ddiff.shape)

        _for_chunks(segs, first)

        def second(s, st, ln, off, b):
            rows = pl.ds(st + b, CHUNK)
            ext = pad_ref[pl.ds(off - HALO + b, CHUNK + 2 * HALO), :]
            total = _shift(_window_sum(ext, win), -1)[HALO:HALO + CHUNK]
            dpre_ref[0, rows, :] = (total - dd_ref[rows, :]).astype(BF16)

        _for_chunks(segs, second)

    def body(dz_ref, diff_ref, pre_ref, w_ref, sc_ref, dpre_ref, dw_ref, dsc_ref, pad_ref, dd_ref):
        gi = pl.program_id(0)
        for widx, win in enumerate(POOL_WINDOWS):
            @pl.when(gi == widx)
            def _(win=win):
                group(win, dz_ref, diff_ref, pre_ref, w_ref, sc_ref, dpre_ref, dw_ref, dsc_ref, pad_ref, dd_ref)

    col = pl.BlockSpec((t, grp), lambda g: (0, g))
    both = pl.BlockSpec((2, t, grp), lambda g: (0, 0, g))
    wspec = pl.BlockSpec((None, grp, grp), lambda g: (g, 0, 0))
    sspec = pl.BlockSpec((1, grp), lambda g: (0, g))
    return _pc(
        body, name=name, grid=(len(POOL_WINDOWS),),
        in_specs=[col, col, both, wspec, sspec],
        out_specs=[both, wspec, sspec],
        out_shape=[jax.ShapeDtypeStruct((2, t, width), BF16), jax.ShapeDtypeStruct((len(POOL_WINDOWS), grp, grp), F32),
                   jax.ShapeDtypeStruct((1, width), F32)],
        scratch_shapes=[pltpu.VMEM((padded, grp), F32), pltpu.VMEM((t, grp), F32)],
        compiler_params=_params("parallel"),
    )(dz, diff, pre, w_grp, scale)


def _conv_fwd(pre, dw, db, name):
    _, t, width = pre.shape
    cb = LANES
    segs = [(0, t)]

    def body(pre_ref, dw_ref, db_ref, z_ref, pad_ref):
        pad_ref[...] = jnp.zeros_like(pad_ref)

        def fill(s, st, ln, off, b):
            rows = pl.ds(b, CHUNK)
            pad_ref[pl.ds(off + b, CHUNK), :] = pre_ref[1, rows, :] * pre_ref[2, rows, :]

        _for_chunks(segs, fill)

        def mix(s, st, ln, off, b):
            rows = pl.ds(b, CHUNK)
            ext = pad_ref[pl.ds(off - HALO + b, CHUNK + 2 * HALO), :]
            conv = (dw_ref[0:1, :] * _shift(ext, 1) + dw_ref[1:2, :] * ext + dw_ref[2:3, :] * _shift(ext, -1))
            conv = conv[HALO:HALO + CHUNK] + db_ref[...]
            y = pre_ref[0, rows, :] * conv
            z_ref[rows, :] = (y * _silu(pre_ref[3, rows, :])).astype(BF16)

        _for_chunks(segs, mix)

    return _pc(
        body, name=name, grid=(width // cb,),
        in_specs=[pl.BlockSpec((4, t, cb), lambda j: (0, 0, j)), pl.BlockSpec((8, cb), lambda j: (0, j)),
                  pl.BlockSpec((1, cb), lambda j: (0, j))],
        out_specs=pl.BlockSpec((t, cb), lambda j: (0, j)),
        out_shape=jax.ShapeDtypeStruct((t, width), BF16),
        scratch_shapes=[pltpu.VMEM((t + 2 * HALO, cb), F32)],
        compiler_params=_params("parallel"),
    )(pre, dw, db)


def _conv_bwd(dz, pre, dw, db, name):
    _, t, width = pre.shape
    cb = LANES
    segs = [(0, t)]

    def body(dz_ref, pre_ref, dw_ref, db_ref, dpre_ref, ddw_ref, ddb_ref, pad_a, pad_c):
        pad_a[...] = jnp.zeros_like(pad_a)
        pad_c[...] = jnp.zeros_like(pad_c)
        ddw_ref[...] = jnp.zeros_like(ddw_ref)
        ddb_ref[...] = jnp.zeros_like(ddb_ref)

        def fill(s, st, ln, off, b):
            rows = pl.ds(b, CHUNK)
            pad_a[pl.ds(off + b, CHUNK), :] = pre_ref[1, rows, :] * pre_ref[2, rows, :]

        _for_chunks(segs, fill)

        def first(s, st, ln, off, b):
            rows = pl.ds(b, CHUNK)
            ext = pad_a[pl.ds(off - HALO + b, CHUNK + 2 * HALO), :]
            prev, nxt = _shift(ext, 1)[HALO:HALO + CHUNK], _shift(ext, -1)[HALO:HALO + CHUNK]
            here = ext[HALO:HALO + CHUNK]
            conv = dw_ref[0:1, :] * prev + dw_ref[1:2, :] * here + dw_ref[2:3, :] * nxt + db_ref[...]
            bg, g = pre_ref[0, rows, :], pre_ref[3, rows, :]
            dzv = dz_ref[rows, :]
            dy = dzv * _silu(g)
            dpre_ref[3, rows, :] = (dzv * (bg * conv) * _dsilu(g)).astype(BF16)
            dpre_ref[0, rows, :] = (dy * conv).astype(BF16)
            dconv = dy * bg
            pad_c[pl.ds(off + b, CHUNK), :] = dconv
            ddw_ref[0:1, :] += jnp.sum(dconv * prev, axis=0, keepdims=True)
            ddw_ref[1:2, :] += jnp.sum(dconv * here, axis=0, keepdims=True)
            ddw_ref[2:3, :] += jnp.sum(dconv * nxt, axis=0, keepdims=True)
            ddb_ref[0:1, :] += jnp.sum(dconv, axis=0, keepdims=True)

        _for_chunks(segs, first)

        def second(s, st, ln, off, b):
            rows = pl.ds(b, CHUNK)
            ext = pad_c[pl.ds(off - HALO + b, CHUNK + 2 * HALO), :]
            da = (dw_ref[0:1, :] * _shift(ext, -1) + dw_ref[1:2, :] * ext + dw_ref[2:3, :] * _shift(ext, 1))
            da = da[HALO:HALO + CHUNK]
            dpre_ref[1, rows, :] = (da * pre_ref[2, rows, :]).astype(BF16)
            dpre_ref[2, rows, :] = (da * pre_ref[1, rows, :]).astype(BF16)

        _for_chunks(segs, second)

    quad = pl.BlockSpec((4, t, cb), lambda j: (0, 0, j))
    rows8 = pl.BlockSpec((8, cb), lambda j: (0, j))
    return _pc(
        body, name=name, grid=(width // cb,),
        in_specs=[pl.BlockSpec((t, cb), lambda j: (0, j)), quad, rows8, pl.BlockSpec((1, cb), lambda j: (0, j))],
        out_specs=[quad, rows8, rows8],
        out_shape=[jax.ShapeDtypeStruct((4, t, width), BF16), jax.ShapeDtypeStruct((8, width), F32),
                   jax.ShapeDtypeStruct((8, width), F32)],
        scratch_shapes=[pltpu.VMEM((t + 2 * HALO, cb), F32), pltpu.VMEM((t + 2 * HALO, cb), F32)],
        compiler_params=_params("parallel"),
    )(dz, pre, dw, db)


PAIR_TILES = 2 * WIN_ROWS - 2


def _pair_geometry():
    lane = lax.broadcasted_iota(jnp.int32, (GRID_W, LANES), 1)
    qcol = lax.broadcasted_iota(jnp.int32, (GRID_W, LANES), 0)
    low = lane < GRID_W
    kcol = jnp.where(low, lane, lane - GRID_W)
    start = jnp.clip(qcol - WIN_COLS // 2, 0, GRID_W - WIN_COLS)
    inside = (kcol >= start) & (kcol < start + WIN_COLS)
    return low, inside


def _bias_tiles(rpb_ref, rows_ref, tiles_ref, inside):
    for h in range(2):
        rows = rpb_ref[h]
        rows_ref[h] = (pltpu.roll(rows, LANES - (WIN_COLS - 1), 1)
                       + pltpu.roll(pltpu.roll(rows, GRID_W - (WIN_COLS - 1), 1), 2 * WIN_ROWS - 1, 0))
        for t in range(PAIR_TILES):
            both = jnp.broadcast_to(rows_ref[h, t:t + 1, :], (GRID_W, LANES))
            tiles_ref[h, t] = jnp.where(inside, pltpu.roll(both, 0, 1, stride=1, stride_axis=0), MASKED)


def _bias_tiles_grad(dtiles_ref, drpb_ref):
    n = PAIR_TILES * GRID_W
    qcol = lax.broadcasted_iota(jnp.int32, (n, LANES), 0) & (GRID_W - 1)
    lane = lax.broadcasted_iota(jnp.int32, (1, LANES), 1)
    zero = jnp.zeros((1, LANES), F32)
    for h in range(2):
        v = pltpu.roll(dtiles_ref[h].reshape(n, LANES), WIN_COLS - 1, 1)
        for bit in range(6):
            v = jnp.where((qcol >> bit) & 1 == 1, pltpu.roll(v, LANES - (1 << bit), 1), v)
        sums = [jnp.sum(v[t * GRID_W:(t + 1) * GRID_W], axis=0, keepdims=True) for t in range(PAIR_TILES)]
        for r in range(2 * WIN_ROWS):
            here = sums[r] if r < PAIR_TILES else zero
            prev = pltpu.roll(sums[r - 1], GRID_W, 1) if 1 <= r <= PAIR_TILES else zero
            drpb_ref[h, r:r + 1, :] = jnp.where(lane < 2 * WIN_COLS - 1, here + prev, 0.0)


def _attn_rows(r, n_rows):
    first = jnp.clip(r - WIN_ROWS // 2, 0, n_rows - WIN_ROWS)
    return first, first - r + WIN_ROWS - 1


def _softmax(s_loc, s_ctx):
    m = jnp.maximum(jnp.max(s_loc, axis=-1, keepdims=True), jnp.max(s_ctx, axis=-1, keepdims=True))
    e_loc, e_ctx = jnp.exp(s_loc - m), jnp.exp(s_ctx - m)
    inv = 1.0 / (jnp.sum(e_loc, axis=-1, keepdims=True) + jnp.sum(e_ctx, axis=-1, keepdims=True))
    return e_loc * inv, e_ctx * inv


def _pair_bias(tiles_ref, j):
    return jnp.concatenate(
        [jnp.concatenate([tiles_ref[h, j + 2 * m] for m in range(WIN_ROWS // 2)], axis=1) for h in range(2)], axis=0)


ROWS_PER_STEP = 4


def _by_head(tile, low):
    zero = jnp.zeros_like(tile)
    return jnp.concatenate([jnp.where(low, tile, zero), jnp.where(low, zero, tile)], axis=0)


def _merge_heads(stacked, low):
    return jnp.where(low, stacked[:GRID_W], stacked[GRID_W:])


def _attn_items(step, n_rows, q_ref, low):
    items = []
    for u in range(ROWS_PER_STEP):
        r = step * ROWS_PER_STEP + u
        first, j = _attn_rows(r, n_rows)
        rows = pl.ds(pl.multiple_of(r * GRID_W, GRID_W), GRID_W)
        keys = pl.ds(pl.multiple_of(first * GRID_W, GRID_W), WIN_ROWS * GRID_W)
        q = (q_ref[rows, :].astype(F32) * HEAD_DIM ** -0.5).astype(BF16)
        items.append((rows, keys, j, _by_head(q, low)))
    return items


def _attn_fwd(qkv, gate, rpb, seq):
    _, t, width = qkv.shape
    n_rows = seq // GRID_W
    n_ctx = t - seq
    blk = WIN_ROWS * GRID_W

    def body(q_ref, k_ref, v_ref, g_ref, rpb_ref, z_ref, o_ref, rows_ref, tiles_ref):
        low, inside = _pair_geometry()
        _bias_tiles(rpb_ref, rows_ref, tiles_ref, inside)
        ctx = pl.ds(seq, n_ctx)

        def step(i, carry):
            items = _attn_items(i, n_rows, q_ref, low)
            k_ctx, v_ctx = k_ref[ctx, :], v_ref[ctx, :]
            scores = [(_dot(q, k_ref[keys, :], 1, 1) + _pair_bias(tiles_ref, j), _dot(q, k_ctx, 1, 1))
                      for _, keys, j, q in items]
            probs = [_softmax(s_loc, s_ctx) for s_loc, s_ctx in scores]
            outs = [_dot(p_loc.astype(BF16), v_ref[keys, :]) + _dot(p_ctx.astype(BF16), v_ctx)
                    for (_, keys, _, _), (p_loc, p_ctx) in zip(items, probs)]
            for (rows, _, _, _), out in zip(items, outs):
                o = _merge_heads(out, low)
                o_ref[rows, :] = o
                z_ref[rows, :] = (o * _silu(g_ref[rows, :])).astype(BF16)
            return carry

        lax.fori_loop(0, n_rows // ROWS_PER_STEP, step, 0)

    def part(p):
        return pl.BlockSpec((None, t, LANES), lambda h: (p, 0, h))

    out = pl.BlockSpec((seq, LANES), lambda h: (0, h))
    return _pc(
        body, name="attn_fwd", grid=(width // LANES,),
        in_specs=[part(0), part(1), part(2), part(0), pl.BlockSpec((2, 2 * WIN_ROWS, LANES), lambda h: (h, 0, 0))],
        out_specs=[out, out],
        out_shape=[jax.ShapeDtypeStruct((seq, width), BF16), jax.ShapeDtypeStruct((seq, width), F32)],
        scratch_shapes=[pltpu.VMEM((2, 2 * WIN_ROWS, LANES), F32), pltpu.VMEM((2, PAIR_TILES, GRID_W, LANES), F32)],
        compiler_params=_params("parallel"),
    )(qkv, qkv, qkv, gate, rpb)


def _attn_bwd(qkv, gate, o, dz, rpb, seq):
    _, t, width = qkv.shape
    n_rows = seq // GRID_W
    n_ctx = t - seq
    blk = WIN_ROWS * GRID_W
    heads = 2 * width // LANES

    def body(q_ref, k_ref, v_ref, g_ref, o_ref, dz_ref, rpb_ref, dpre_ref, drpb_ref,
             rows_ref, tiles_ref, dtiles_ref, dk_ref, dv_ref):
        low, inside = _pair_geometry()
        _bias_tiles(rpb_ref, rows_ref, tiles_ref, inside)
        dtiles_ref[...] = jnp.zeros_like(dtiles_ref)
        dk_ref[...] = jnp.zeros_like(dk_ref)
        dv_ref[...] = jnp.zeros_like(dv_ref)
        ctx = pl.ds(seq, n_ctx)

        def step(i, carry):
            items = _attn_items(i, n_rows, q_ref, low)
            k_ctx, v_ctx = k_ref[ctx, :], v_ref[ctx, :]
            d_outs = []
            for rows, _, _, _ in items:
                g = g_ref[rows, :]
                dzv = dz_ref[rows, :]
                dpre_ref[3, rows, :] = (dzv * o_ref[rows, :] * _dsilu(g)).astype(BF16)
                d_outs.append(_by_head((dzv * _silu(g)).astype(BF16), low))
            scores = [(_dot(q, k_ref[keys, :], 1, 1) + _pair_bias(tiles_ref, j), _dot(q, k_ctx, 1, 1))
                      for _, keys, j, q in items]
            dprobs = [(_dot(d_o, v_ref[keys, :], 1, 1), _dot(d_o, v_ctx, 1, 1))
                      for (_, keys, _, _), d_o in zip(items, d_outs)]
            probs = [_softmax(s_loc, s_ctx) for s_loc, s_ctx in scores]
            dscores = []
            for (p_loc, p_ctx), (dp_loc, dp_ctx) in zip(probs, dprobs):
                delta = (jnp.sum(p_loc * dp_loc, axis=-1, keepdims=True)
                         + jnp.sum(p_ctx * dp_ctx, axis=-1, keepdims=True))
                dscores.append((p_loc * (dp_loc - delta), p_ctx * (dp_ctx - delta)))
            dqs = [_dot(ds_loc.astype(BF16), k_ref[keys, :]) + _dot(ds_ctx.astype(BF16), k_ctx)
                   for (_, keys, _, _), (ds_loc, ds_ctx) in zip(items, dscores)]
            for (rows, _, _, _), dq in zip(items, dqs):
                dpre_ref[0, rows, :] = (_merge_heads(dq, low) * HEAD_DIM ** -0.5).astype(BF16)
            for (_, keys, j, q), d_o, (p_loc, p_ctx), (ds_loc, ds_ctx) in zip(items, d_outs, probs, dscores):
                dk_ref[keys, :] += _dot(ds_loc.astype(BF16), q, 0, 0)
                dk_ref[ctx, :] += _dot(ds_ctx.astype(BF16), q, 0, 0)
                dv_ref[keys, :] += _dot(p_loc.astype(BF16), d_o, 0, 0)
                dv_ref[ctx, :] += _dot(p_ctx.astype(BF16), d_o, 0, 0)
                for h in range(2):
                    for m in range(WIN_ROWS // 2):
                        dtiles_ref[h, j + 2 * m] += ds_loc[h * GRID_W:(h + 1) * GRID_W, m * LANES:(m + 1) * LANES]
            return carry

        lax.fori_loop(0, n_rows // ROWS_PER_STEP, step, 0)
        dpre_ref[1] = dk_ref[...].astype(BF16)
        dpre_ref[2] = dv_ref[...].astype(BF16)
        dpre_ref[0, ctx, :] = jnp.zeros((n_ctx, LANES), BF16)
        dpre_ref[3, ctx, :] = jnp.zeros((n_ctx, LANES), BF16)
        _bias_tiles_grad(dtiles_ref, drpb_ref)

    def part(p):
        return pl.BlockSpec((None, t, LANES), lambda h: (p, 0, h))

    lat = pl.BlockSpec((seq, LANES), lambda h: (0, h))
    rspec = pl.BlockSpec((2, 2 * WIN_ROWS, LANES), lambda h: (h, 0, 0))
    tiles = pltpu.VMEM((2, PAIR_TILES, GRID_W, LANES), F32)
    return _pc(
        body, name="attn_bwd", grid=(width // LANES,),
        in_specs=[part(0), part(1), part(2), part(0), lat, lat, rspec],
        out_specs=[pl.BlockSpec((4, t, LANES), lambda h: (0, 0, h)), rspec],
        out_shape=[jax.ShapeDtypeStruct((4, t, width), BF16), jax.ShapeDtypeStruct((heads, 2 * WIN_ROWS, LANES), F32)],
        scratch_shapes=[pltpu.VMEM((2, 2 * WIN_ROWS, LANES), F32), tiles, tiles,
                        pltpu.VMEM((t, LANES), F32), pltpu.VMEM((t, LANES), F32)],
        compiler_params=_params("parallel"),
    )(qkv, qkv, qkv, gate, o, dz, rpb)


def _adamw(w, m, v, parts, name, mult=None):
    rows, cols = w.shape
    tr = _tile(rows, max(8, 131072 // cols), 8)
    n_parts = len(parts)
    c1 = 1.0 - ADAM_B1 ** ADAM_STEP
    c2 = 1.0 - ADAM_B2 ** ADAM_STEP

    def body(*refs):
        w_ref, m_ref, v_ref = refs[:3]
        part_refs = refs[3:3 + n_parts]
        rest = refs[3 + n_parts:]
        g = part_refs[0][...].astype(F32)
        for p in part_refs[1:]:
            g = g + p[...].astype(F32)
        if mult is not None:
            g = g * rest[0][...]
            rest = rest[1:]
        g_ref, d_ref, nm_ref, nv_ref = rest
        m2 = ADAM_B1 * m_ref[...] + (1.0 - ADAM_B1) * g
        v2 = ADAM_B2 * v_ref[...] + (1.0 - ADAM_B2) * (g * g)
        m_hat = m2 / c1
        v_hat = v2 / c2
        g_ref[...] = g
        d_ref[...] = -ADAM_LR * (m_hat / (jnp.sqrt(v_hat) + ADAM_EPS) + ADAM_WD * w_ref[...])
        nm_ref[...] = m2
        nv_ref[...] = v2

    tile = pl.BlockSpec((tr, cols), lambda i: (i, 0))
    in_specs, args = [tile, tile, tile], [w, m, v]
    for p in parts:
        if isinstance(p, tuple):
            arr, k = p
            in_specs.append(pl.BlockSpec((None, tr, cols), lambda i, k=k: (k, i, 0)))
            args.append(arr)
        else:
            in_specs.append(tile)
            args.append(p)
    if mult is not None:
        in_specs.append(tile)
        args.append(mult)
    shape = jax.ShapeDtypeStruct((rows, cols), F32)
    return _pc(
        body, name=name, grid=(rows // tr,), in_specs=in_specs, out_specs=[tile] * 4, out_shape=[shape] * 4,
        compiler_params=_params("parallel"),
    )(*args)


def _rows128(a):
    flat = a.reshape(-1)
    pad = (-flat.shape[0]) % LANES
    if pad:
        flat = jnp.concatenate([flat, jnp.zeros((pad,), flat.dtype)])
    return flat.reshape(-1, LANES)


def _pad_rows(a, mult=8):
    pad = (-a.shape[0]) % mult
    if pad:
        a = jnp.concatenate([a, jnp.zeros((pad,) + a.shape[1:], a.dtype)], axis=0)
    return a


def kernel(x, c, ctx, c_ctx, norm_g, ada_w, ada_b, pool_w_in, pool_w_grp, pool_scale, pool_w_out, na_w_in, na_rpb, na_w_out, conv_w_in, conv_dw, conv_db, conv_w_out, final_g, loss_target, m_c_ctx, m_norm_g, m_ada_w, m_ada_b, m_pool_w_in, m_pool_w_grp, m_pool_scale, m_pool_w_out, m_na_w_in, m_na_rpb, m_na_w_out, m_conv_w_in, m_conv_dw, m_conv_db, m_conv_w_out, m_final_g, v_c_ctx, v_norm_g, v_ada_w, v_ada_b, v_pool_w_in, v_pool_w_grp, v_pool_scale, v_pool_w_out, v_na_w_in, v_na_rpb, v_na_w_out, v_conv_w_in, v_conv_dw, v_conv_db, v_conv_w_out, v_final_g):
    xi, yi, ci = _my_place()
    me = 4 * xi + 2 * yi + ci
    seq, d = x.shape[1], x.shape[2]
    n_ctx = ctx.shape[1]
    t_all = seq + n_ctx
    width = d
    depth = norm_g.shape[0]
    nb = ada_w.shape[2]
    shard = width // N_DEV
    d_rows = d // LANES
    tr = math.gcd(math.gcd(seq, n_ctx), 256)
    x_tiles = seq // tr

    n_pool = pool_scale.shape[0]
    n_grp = pool_w_grp.shape[1]
    grp = width // n_grp
    layer_weights = [[pool_w_in[0], pool_w_grp[0], pool_w_out[0]], [na_w_out[0]],
                     [conv_w_in[0], conv_w_out[0]], [pool_w_in[1], pool_w_grp[1], pool_w_out[1]]]
    in_flight, token = [], jnp.zeros((8, LANES), F32)
    for i, ws in enumerate(layer_weights):
        if i == 1:
            na_in_first = _exchange_start([na_w_in[0].astype(BF16)], False, token, "weights_start1_chips", peers=CHIP_PEERS)
            token = na_in_first[-1]
        state = _exchange_start([w.astype(BF16) for w in ws], False, token, f"weights_start{i}")
        token = state[-1]
        in_flight.append(state)

    def landed_weight(i, t, after):
        return _exchange_wait(in_flight[i], False, after, f"weights_wait{i}_{t}", which=[t])[0]

    def as_in(w):
        return w[:, None]

    def as_grp(w):
        return w.transpose(1, 0, 2, 3).reshape(n_grp, grp, grp)

    def as_out(w):
        return w.reshape(width, d)

    small_in =_pad_rows(jnp.concatenate([_rows128(c), pool_scale, conv_dw[0], conv_db], axis=0))
    got = _gather_small(small_in, "gather_inputs")
    r0 = d_rows
    c_all = got[:, :r0].reshape(N_DEV, d)
    n_pool = pool_scale.shape[0]
    scale_full = got[:, r0:r0 + n_pool].transpose(1, 0, 2).reshape(n_pool, width)
    r1 = r0 + n_pool
    taps_full = _pad_rows(got[:, r1:r1 + 3].transpose(1, 0, 2).reshape(3, width))
    bias_full = got[:, r1 + 3:r1 + 4].transpose(1, 0, 2).reshape(1, width)

    cond = jnp.concatenate([c_all, c_ctx[None], jnp.zeros((7, d), F32)], axis=0)
    bias_mine = lax.dynamic_slice(ada_b, (0, me * nb), (depth, nb))
    mod_mine = _mod_fwd(cond, ada_w, bias_mine)
    mod_all = _gather_small(mod_mine.reshape(-1, LANES), "gather_mod")
    mod_all = mod_all.reshape(N_DEV, depth, 16, nb).transpose(1, 2, 0, 3).reshape(depth, 16, 3 * d)
    mod_x = lax.dynamic_index_in_dim(mod_all, me, 1, keepdims=False).reshape(depth, 3, d)
    mod_c = mod_all[:, 8].reshape(depth, 3, d)
    pad5 = jnp.zeros((depth, 5, d), F32)
    mod_x = jnp.concatenate([mod_x, pad5], axis=1)
    mod_c = jnp.concatenate([mod_c, pad5], axis=1)
    mods = [jnp.stack([mod_x[i], mod_c[i]]) if i < 2 else mod_x[i][None] for i in range(depth)]

    both = [(0, seq), (seq, n_ctx)]
    latent = [(0, seq)]

    def grp_slots(g):
        return g.reshape(n_grp, N_DEV, grp // N_DEV, grp).transpose(1, 0, 2, 3).reshape(N_DEV, -1, grp).astype(BF16)

    def send_grads(i, grads):
        return _exchange_start(grads, True, jnp.zeros((8, LANES), F32), f"grads_start{i}")

    xs0 = jnp.concatenate([x[0], ctx[0]], axis=0)
    h0 = _norm_fwd(xs0, norm_g[0:1] + token[0, 0], mods[0], tr, x_tiles, "norm_fwd0")
    pool_in_w0 = as_in(landed_weight(0, 0, h0))
    pre0 = _proj_in(h0, pool_in_w0, 0, width, "proj_in0")
    pool_grp_w0 = as_grp(landed_weight(0, 1, pre0))
    z0, diff0 = _pool_fwd(pre0, pool_grp_w0, scale_full[0:1], both, "pool_fwd0")
    pool_out_w0 = as_out(landed_weight(0, 2, z0))
    yx0, xs1, h1 = _proj_out(z0, pool_out_w0, xs0, mods[0], tr, x_tiles, "proj_out0", nxt=(norm_g[1:2], mods[1]))

    na_in_half = _exchange_wait(na_in_first, False, xs1, "weights_wait1_chips", peers=CHIP_PEERS)
    na_in_rest = _forward_start(na_in_half, h1, "weights_forward1")
    na_in_w = as_in(_forward_wait(na_in_rest, na_in_rest[-1], "weights_forward_wait1")[0])
    per_part = width // na_w_in.shape[2]
    qkv1 = _proj_in(h1, na_in_w, 0, width, "proj_in1_qkv", blocks=(0, 3 * per_part), dtype=BF16)
    gpre1 = _proj_in(h1, na_in_w, 0, width, "proj_in1_gate", blocks=(3 * per_part, per_part))
    rpb_rows = jnp.pad(na_rpb[0], ((0, 0), (0, 2 * WIN_ROWS - na_rpb.shape[2]), (0, LANES - na_rpb.shape[3])))
    z1, o1 = _attn_fwd(qkv1, gpre1, rpb_rows, seq)
    na_out_w = as_out(landed_weight(1, 0, z1))
    yx1, x2, h2 = _proj_out(z1, na_out_w, xs1, mods[1], tr, x_tiles, "proj_out1", nxt=(norm_g[2:3], mods[2]))

    conv_in_w = as_in(landed_weight(2, 0, h2))
    pre2 = _proj_in(h2, conv_in_w, 0, width, "proj_in2")
    z2 = _conv_fwd(pre2, taps_full, bias_full, "conv_fwd")
    conv_out_w = as_out(landed_weight(2, 1, z2))
    yx2, x3, h3 = _proj_out(z2, conv_out_w, x2, mods[2], tr, x_tiles, "proj_out2", nxt=(norm_g[3:4], mods[3]))

    pool_in_w3 = as_in(landed_weight(3, 0, h3))
    pre3 = _proj_in(h3, pool_in_w3, 0, width, "proj_in3")
    pool_grp_w3 = as_grp(landed_weight(3, 1, pre3))
    z3, diff3 = _pool_fwd(pre3, pool_grp_w3, scale_full[1:2], latent, "pool_fwd3")
    pool_out_w3 = as_out(landed_weight(3, 2, z3))
    yx3, x4 = _proj_out(z3, pool_out_w3, x3, mods[3], tr, x_tiles, "proj_out3")

    loss_part, dx4, d_final, dyx3, gate3 = _loss_head(x4, loss_target[0], final_g[None], yx3, mods[3], tr)
    loss = lax.psum(loss_part[0, 0], ("x", "y", "c"))

    dz3 = _proj_out_dz(dyx3, pool_out_w3, "proj_out_dz3")
    g_pool_out1 = _grad_w_out(z3, dyx3, "grad_w_out3")
    dpre3, g_grp1, g_scale1 = _pool_bwd(dz3, diff3, pre3, pool_grp_w3, scale_full[1:2], latent, "pool_bwd3")
    dh3 = _proj_in_dh(dpre3, pool_in_w3, 0, "proj_in_dh3")
    g_pool_in1 = _grad_w_in(h3, dpre3, pool_w_in.shape[2], "grad_w_in3")
    sent3 = send_grads(3, [g_pool_in1, grp_slots(g_grp1), g_pool_out1.reshape(N_DEV, shard, d)])
    dx3, norm3, dyx2, gate2 = _norm_bwd(x3, dh3, dx4, norm_g[3:4] + sent3[-1][0, 0], mods[3], tr, x_tiles, "norm_bwd3",
                                        below=(yx2, mods[2]))

    dz2 = _proj_out_dz(dyx2, conv_out_w, "proj_out_dz2")
    g_conv_out = _grad_w_out(z2, dyx2, "grad_w_out2")
    dpre2, g_taps, g_cbias = _conv_bwd(dz2, pre2, taps_full, bias_full, "conv_bwd")
    dh2 = _proj_in_dh(dpre2, conv_in_w, 0, "proj_in_dh2")
    g_conv_in = _grad_w_in(h2, dpre2, conv_w_in.shape[2], "grad_w_in2")
    sent2 = send_grads(2, [g_conv_in, g_conv_out.reshape(N_DEV, shard, d)])
    dx2, norm2, dyx1, gate1 = _norm_bwd(x2, dh2, dx3, norm_g[2:3] + sent2[-1][0, 0], mods[2], tr, x_tiles, "norm_bwd2",
                                        below=(yx1, mods[1][:1]))

    dz1 = _proj_out_dz(dyx1, na_out_w, "proj_out_dz1")
    g_na_out = _grad_w_out(z1, dyx1, "grad_w_out1")
    dpre1, g_rpb = _attn_bwd(qkv1, gpre1, o1, dz1, rpb_rows, seq)
    g_rpb = g_rpb[:, :na_rpb.shape[2], :na_rpb.shape[3]]
    dh1 = _proj_in_dh(dpre1, na_in_w, 0, "proj_in_dh1")
    g_na_in = _grad_w_in(h1, dpre1, na_w_in.shape[2], "grad_w_in1")
    sent1 = send_grads(1, [g_na_in, g_na_out.reshape(N_DEV, shard, d)])
    dxs1, norm1, dyx0, gate0 = _norm_bwd(xs1, dh1, dx2, norm_g[1:2] + sent1[-1][0, 0], mods[1], tr, x_tiles, "norm_bwd1",
                                         res_tiles=x_tiles, below=(yx0, mods[0]))

    dz0 = _proj_out_dz(dyx0, pool_out_w0, "proj_out_dz0")
    g_pool_out0 = _grad_w_out(z0, dyx0, "grad_w_out0")
    sent0a = _exchange_start([g_pool_out0.reshape(N_DEV, shard, d)], True, jnp.zeros((8, LANES), F32), "grads_start0a")
    dpre0, g_grp0, g_scale0 = _pool_bwd(dz0, diff0, pre0, pool_grp_w0, scale_full[0:1], both, "pool_bwd0")
    g_pool_in0 = _grad_w_in(h0, dpre0, pool_w_in.shape[2], "grad_w_in0")
    sent0b = _exchange_start([g_pool_in0, grp_slots(g_grp0)], True, sent0a[-1], "grads_start0b")
    dh0 = _proj_in_dh(dpre0, pool_in_w0, 0, "proj_in_dh0", after=sent0b[-1])
    dx0, norm0 = _norm_bwd(xs0, dh0, dxs1, norm_g[0:1], mods[0], tr, x_tiles, "norm_bwd0", out_tiles=x_tiles)
    grad_x = dx0[None]

    norms, gates = [norm0, norm1, norm2, norm3], [gate0, gate1, gate2, gate3]
    zero_d = jnp.zeros((d,), F32)
    dm_rows = [jnp.concatenate([norms[i][0, 0], norms[i][0, 1], gates[i][0, 0]]) for i in range(depth)]
    dm_rows.append(jnp.concatenate([norm0[1, 0], norm0[1, 1], gate0[1, 0]]))
    dm_rows.append(jnp.concatenate([norm1[1, 0], norm1[1, 1], zero_d]))
    dm_local = jnp.stack(dm_rows + [jnp.zeros((3 * d,), F32)] * 2)
    g_norm_part = jnp.stack([norm0[0, 2] + norm0[1, 2], norm1[0, 2] + norm1[1, 2], norm2[0, 2], norm3[0, 2]])
    g_scale_part = jnp.concatenate([g_scale0, g_scale1], axis=0)
    pieces = [_rows128(dm_local), _rows128(g_norm_part), _rows128(d_final[0]), _pad_rows(_rows128(g_rpb)),
              _rows128(g_scale_part), _rows128(g_taps[:3]), _rows128(g_cbias[0])]
    small_sent = _exchange_start([jnp.concatenate(pieces, axis=0)], False, jnp.zeros((8, LANES), F32), "small_grads_start")
    marks = np.cumsum([0] + [p.shape[0] for p in pieces])

    def big(parts, w, m, v, name):
        shape = w.shape
        view = (-1, shape[-1])
        parts = [(parts.reshape((N_DEV,) + w.reshape(view).shape), k) for k in range(N_DEV)]
        return [r.reshape(shape) for r in _adamw(w.reshape(view), m.reshape(view), v.reshape(view), parts, name)]

    in3, grp3, out3 = _exchange_wait(sent3, True, small_sent[-1], "grads_wait3")
    in2, out2 = _exchange_wait(sent2, True, small_sent[-1], "grads_wait2")
    in1, out1 = _exchange_wait(sent1, True, small_sent[-1], "grads_wait1")
    res = {}
    res["na_w_in"] = [r[None] for r in big(in1, na_w_in[0], m_na_w_in[0], v_na_w_in[0], "adamw_na_in")]
    res["na_w_out"] = [r[None] for r in big(out1, na_w_out[0], m_na_w_out[0], v_na_w_out[0], "adamw_na_out")]
    res["conv_w_in"] = [r[None] for r in big(in2, conv_w_in[0], m_conv_w_in[0], v_conv_w_in[0], "adamw_conv_in")]
    res["conv_w_out"] = [r[None] for r in big(out2, conv_w_out[0], m_conv_w_out[0], v_conv_w_out[0], "adamw_conv_out")]
    pool_in1 = big(in3, pool_w_in[1], m_pool_w_in[1], v_pool_w_in[1], "adamw_pool_in1")
    pool_grp1 = big(grp3, pool_w_grp[1], m_pool_w_grp[1], v_pool_w_grp[1], "adamw_pool_grp1")
    pool_out1 = big(out3, pool_w_out[1], m_pool_w_out[1], v_pool_w_out[1], "adamw_pool_out1")

    small_out = _exchange_wait(small_sent, False, pool_out1[0], "small_grads_wait")[0]

    def piece(k):
        return small_out[:, marks[k]:marks[k + 1]]

    dm_all = piece(0).reshape(N_DEV, 8, 3 * d).transpose(1, 0, 2)
    dm_mine = lax.dynamic_slice(dm_all, (0, 0, me * nb), (8, N_DEV, nb))
    g_ada_w, g_ada_b, cctx_part, dsilu_cond = _mod_bwd(cond, ada_w, dm_all, dm_mine)
    cctx_all = _gather_small(_rows128(cctx_part[0]), "gather_cctx")

    def my_shard(a, n):
        a = a.reshape(N_DEV, n, N_DEV, shard)
        return lax.dynamic_index_in_dim(a, me, 2, keepdims=False)

    zeros7 = lambda r: jnp.zeros((N_DEV - 1, r, LANES), F32)
    ada_b_rows = _rows128(g_ada_b)
    small_parts = jnp.concatenate([
        cctx_all, piece(1), jnp.concatenate([ada_b_rows[None], zeros7(ada_b_rows.shape[0])], axis=0), piece(2), piece(3),
        my_shard(piece(4), n_pool), my_shard(piece(5), 3), my_shard(piece(6), 1)], axis=1)
    n_small = small_parts.shape[1]
    small_parts = jnp.concatenate([small_parts, jnp.zeros((N_DEV, (-n_small) % 8, LANES), F32)], axis=1)

    def pack(c_ctx_, norm_g_, ada_b_, final_g_, rpb_, scale_, taps_, cbias_):
        rows = [_rows128(c_ctx_), _rows128(norm_g_), _rows128(ada_b_), _rows128(final_g_), _pad_rows(_rows128(rpb_)),
                scale_, taps_[0], cbias_]
        return _pad_rows(jnp.concatenate(rows, axis=0)), np.cumsum([0] + [r.shape[0] for r in rows])

    w_small, smarks = pack(c_ctx, norm_g, ada_b, final_g, na_rpb, pool_scale, conv_dw, conv_db)
    m_small, _ = pack(m_c_ctx, m_norm_g, m_ada_b, m_final_g, m_na_rpb, m_pool_scale, m_conv_dw, m_conv_db)
    v_small, _ = pack(v_c_ctx, v_norm_g, v_ada_b, v_final_g, v_na_rpb, v_pool_scale, v_conv_dw, v_conv_db)
    mult = jnp.concatenate([_rows128(dsilu_cond[8]), jnp.ones((w_small.shape[0] - d_rows, LANES), F32)], axis=0)
    small_res = _adamw(w_small, m_small, v_small, [(small_parts, k) for k in range(N_DEV)], "adamw_small", mult=mult)

    def unpack(k, like):
        out = []
        for r in small_res:
            flat = r[smarks[k]:smarks[k + 1]].reshape(-1)
            out.append(flat[:like.size].reshape(like.shape))
        return out

    res.update({"c_ctx": unpack(0, c_ctx), "norm_g": unpack(1, norm_g), "ada_b": unpack(2, ada_b),
                "final_g": unpack(3, final_g), "na_rpb": unpack(4, na_rpb), "pool_scale": unpack(5, pool_scale),
                "conv_dw": unpack(6, conv_dw), "conv_db": unpack(7, conv_db)})

    res["ada_w"] = [r.reshape(ada_w.shape) for r in _adamw(
        ada_w.reshape(-1, nb), m_ada_w.reshape(-1, nb), v_ada_w.reshape(-1, nb), [g_ada_w.reshape(-1, nb)], "adamw_ada_w")]

    out0, = _exchange_wait(sent0a, True, small_res[0], "grads_wait0a")
    in0, grp0 = _exchange_wait(sent0b, True, small_res[0], "grads_wait0b")
    pool_in0 = big(in0, pool_w_in[0], m_pool_w_in[0], v_pool_w_in[0], "adamw_pool_in0")
    pool_grp0 = big(grp0, pool_w_grp[0], m_pool_w_grp[0], v_pool_w_grp[0], "adamw_pool_grp0")
    pool_out0 = big(out0, pool_w_out[0], m_pool_w_out[0], v_pool_w_out[0], "adamw_pool_out0")
    res["pool_w_in"] = [jnp.stack([p, q]) for p, q in zip(pool_in0, pool_in1)]
    res["pool_w_grp"] = [jnp.stack([p, q]) for p, q in zip(pool_grp0, pool_grp1)]
    res["pool_w_out"] = [jnp.stack([p, q]) for p, q in zip(pool_out0, pool_out1)]

    order = ["c_ctx", "norm_g", "ada_w", "ada_b", "pool_w_in", "pool_w_grp", "pool_scale", "pool_w_out", "na_w_in",
             "na_rpb", "na_w_out", "conv_w_in", "conv_dw", "conv_db", "conv_w_out", "final_g"]
    outs = [loss, grad_x]
    for j in range(4):
        outs += [res[n][j] for n in order]
    return tuple(outs)
```

```python
import functools
import math

import numpy as np
import jax
import jax.numpy as jnp
from jax import lax
from jax.experimental import pallas as pl
from jax.experimental.pallas import tpu as pltpu

F32 = jnp.float32
BF16 = jnp.bfloat16
N_DEV = 8
LANES = 128
RMS_EPS = 1e-6
GRID_W = 64
WIN_ROWS = 8
WIN_COLS = 16
HEAD_DIM = 64
POOL_WINDOWS = (2, 4, 8, 16)
HALO = 8
CHUNK = 128
MASKED = -1e30
ADAM_LR = 0.001
ADAM_B1 = 0.9
ADAM_B2 = 0.999
ADAM_EPS = 1e-08
ADAM_WD = 0.01
ADAM_STEP = 10
VMEM_LIMIT = 56 * 1024 * 1024
MESH = pl.DeviceIdType.MESH
ANY = pl.BlockSpec(memory_space=pl.ANY)
HBM = pl.BlockSpec(memory_space=pltpu.HBM)
SEM = pl.BlockSpec(memory_space=pltpu.SEMAPHORE)
EFFECT = pltpu.SideEffectType.DATAFLOW_SIDE_EFFECTING


def _pc(body, *, name, **kw):
    return pl.pallas_call(body, name=name, **kw)


def _params(*sem):
    return pltpu.CompilerParams(dimension_semantics=sem if sem else None, vmem_limit_bytes=VMEM_LIMIT)


def _dot(a, b, ca=1, cb=0, precision=None):
    return lax.dot_general(a, b, (((ca,), (cb,)), ((), ())), preferred_element_type=F32, precision=precision)


def _tile(n, pref, unit=LANES):
    best = None
    for t in range(unit, min(n, pref) + 1, unit):
        if n % t == 0:
            best = t
    return best if best is not None else n


def _sigmoid(x):
    return 1.0 / (1.0 + jnp.exp(-x))


def _silu(x):
    return x * _sigmoid(x)


def _dsilu(x):
    s = _sigmoid(x)
    return s * (1.0 + x * (1.0 - s))


def _my_place():
    return lax.axis_index("x"), lax.axis_index("y"), lax.axis_index("c")


def _flip(v, f):
    return 1 - v if f else v


def _gather_small(block, name):
    rows, cols = block.shape

    def body(x_ref, out_ref, send_sems, recv_sems):
        x, y, c = _my_place()
        me = 4 * x + 2 * y + c
        out_ref[me] = x_ref[...]
        copies = []
        for k in range(1, N_DEV):
            peer = (_flip(x, k & 4), _flip(y, k & 2), _flip(c, k & 1))
            cp = pltpu.make_async_remote_copy(
                src_ref=x_ref, dst_ref=out_ref.at[me], send_sem=send_sems.at[k - 1], recv_sem=recv_sems.at[k - 1],
                device_id=peer, device_id_type=MESH)
            cp.start()
            copies.append(cp)
        for cp in copies:
            cp.wait()

    return _pc(
        body, name=name,
        out_shape=jax.ShapeDtypeStruct((N_DEV, rows, cols), block.dtype),
        in_specs=[pl.BlockSpec(memory_space=pltpu.VMEM)],
        out_specs=pl.BlockSpec(memory_space=pltpu.VMEM),
        scratch_shapes=[pltpu.SemaphoreType.DMA((N_DEV - 1,)), pltpu.SemaphoreType.DMA((N_DEV - 1,))],
    )(block)


ALL_PEERS = tuple(range(N_DEV))
CHIP_PEERS = (0, 1, 2, 4, 6)
OTHER_CHIPS = (2, 4, 6)


def _flag(per_dest, t):
    return per_dest[t] if isinstance(per_dest, (list, tuple)) else per_dest


def _peer(k):
    x, y, c = _my_place()
    peer = (_flip(x, k & 4), _flip(y, k & 2), _flip(c, k & 1))
    return peer, 4 * peer[0] + 2 * peer[1] + peer[2]


def _exchange_copies(srcs, lands, send_sems, recv_sems, per_dest, peers=ALL_PEERS):
    x, y, c = _my_place()
    me = 4 * x + 2 * y + c
    copies = []
    for t, (src, land) in enumerate(zip(srcs, lands)):
        for n, k in enumerate(peers):
            peer, dest = _peer(k)
            s = t * len(peers) + n
            copies.append(pltpu.make_async_remote_copy(
                src_ref=src.at[dest] if _flag(per_dest, t) else src, dst_ref=land.at[me],
                send_sem=send_sems[s], recv_sem=recv_sems[s], device_id=peer, device_id_type=MESH))
    return copies


def _forward_copies(lands, send_sems, recv_sems):
    sibling, _ = _peer(1)
    copies = []
    for t, land in enumerate(lands):
        for n, k in enumerate(OTHER_CHIPS):
            _, slot = _peer(k)
            s = t * len(OTHER_CHIPS) + n
            copies.append(pltpu.make_async_remote_copy(
                src_ref=land.at[slot], dst_ref=land.at[slot], send_sem=send_sems[s], recv_sem=recv_sems[s],
                device_id=sibling, device_id_type=MESH))
    return copies


def _forward_start(lands, after, name):
    nt = len(lands)
    ns = nt * len(OTHER_CHIPS)

    def body(*refs):
        ins, outs = refs[:nt + 1], refs[nt + 1:]
        for cp in _forward_copies(ins[:nt], outs[:ns], outs[ns:2 * ns]):
            cp.start()
        outs[-1][...] = jnp.zeros_like(outs[-1])

    res = _pc(
        body, name=name,
        out_shape=(*[pltpu.SemaphoreType.DMA(())] * (2 * ns), *[pltpu.HBM(a.shape, a.dtype) for a in lands],
                   jax.ShapeDtypeStruct((8, LANES), F32)),
        in_specs=[HBM] * nt + [ANY],
        out_specs=(*[SEM] * (2 * ns), *[HBM] * nt, pl.BlockSpec(memory_space=pltpu.VMEM)),
        input_output_aliases={i: 2 * ns + i for i in range(nt)},
        compiler_params=pltpu.CompilerParams(has_side_effects=EFFECT),
    )(*lands, after)
    return list(res[:ns]), list(res[ns:2 * ns]), list(res[2 * ns:2 * ns + nt]), res[-1]


def _forward_wait(state, after, name):
    send_sems, recv_sems, lands, _ = state
    nt, ns = len(lands), len(send_sems)

    def body(*refs):
        sems = refs[nt:nt + 2 * ns]
        for cp in _forward_copies(refs[:nt], sems[:ns], sems[ns:]):
            cp.wait_send()
            cp.wait_recv()

    res = _pc(
        body, name=name,
        out_shape=tuple(pltpu.HBM(a.shape, a.dtype) for a in lands),
        in_specs=[HBM] * nt + [SEM] * (2 * ns) + [ANY],
        out_specs=tuple([HBM] * nt),
        input_output_aliases={i: i for i in range(nt)},
        compiler_params=pltpu.CompilerParams(has_side_effects=EFFECT),
    )(*lands, *send_sems, *recv_sems, after)
    return list(res)


def _exchange_start(srcs, per_dest, after, name, peers=ALL_PEERS):
    nt = len(srcs)
    ns = nt * len(peers)
    lands = [lax.empty((N_DEV,) + (s.shape[1:] if _flag(per_dest, t) else s.shape), s.dtype) for t, s in enumerate(srcs)]

    def body(*refs):
        ins, outs = refs[:2 * nt + 1], refs[2 * nt + 1:]
        for cp in _exchange_copies(ins[:nt], ins[nt:2 * nt], outs[:ns], outs[ns:2 * ns], per_dest, peers):
            cp.start()
        outs[-1][...] = jnp.zeros_like(outs[-1])

    hbm = [pltpu.with_memory_space_constraint(a, pltpu.HBM) for a in list(srcs) + lands]
    res = _pc(
        body, name=name,
        out_shape=(*[pltpu.SemaphoreType.DMA(())] * (2 * ns), *[pltpu.HBM(a.shape, a.dtype) for a in hbm],
                   jax.ShapeDtypeStruct((8, LANES), F32)),
        in_specs=[HBM] * (2 * nt) + [ANY],
        out_specs=(*[SEM] * (2 * ns), *[HBM] * (2 * nt), pl.BlockSpec(memory_space=pltpu.VMEM)),
        input_output_aliases={i: 2 * ns + i for i in range(2 * nt)},
        compiler_params=pltpu.CompilerParams(has_side_effects=EFFECT),
    )(*hbm, after)
    sems, rest = res[:2 * ns], res[2 * ns:]
    return list(sems[:ns]), list(sems[ns:]), list(rest[:nt]), list(rest[nt:2 * nt]), rest[-1]


def _exchange_wait(state, per_dest, after, name, which=None, peers=ALL_PEERS):
    send_sems, recv_sems, srcs, lands, _ = state
    which = list(range(len(srcs))) if which is None else which
    per_dest = [_flag(per_dest, t) for t in which]
    pick = [t * len(peers) + n for t in which for n in range(len(peers))]
    send_sems, recv_sems = [send_sems[s] for s in pick], [recv_sems[s] for s in pick]
    srcs, lands = [srcs[t] for t in which], [lands[t] for t in which]
    nt = len(srcs)
    ns = len(send_sems)

    def body(*refs):
        sems = refs[2 * nt:2 * nt + 2 * ns]
        for cp in _exchange_copies(refs[:nt], refs[nt:2 * nt], sems[:ns], sems[ns:], per_dest, peers):
            cp.wait_send()
            cp.wait_recv()

    thru = list(srcs) + list(lands)
    res = _pc(
        body, name=name,
        out_shape=tuple(pltpu.HBM(a.shape, a.dtype) for a in thru),
        in_specs=[HBM] * (2 * nt) + [SEM] * (2 * ns) + [ANY],
        out_specs=tuple([HBM] * (2 * nt)),
        input_output_aliases={i: i for i in range(2 * nt)},
        compiler_params=pltpu.CompilerParams(has_side_effects=EFFECT),
    )(*thru, *send_sems, *recv_sems, after)
    return list(res[nt:])


def _mod_fwd(cond, ada_w, bias):
    depth, d, nb = ada_w.shape

    def body(c_ref, w_ref, b_ref, o_ref):
        s = _silu(c_ref[...]).astype(BF16)
        o_ref[...] = _dot(s, w_ref[...].astype(BF16)) + b_ref[...]

    return _pc(
        body, name="mod_fwd", grid=(depth,),
        in_specs=[pl.BlockSpec((16, d), lambda i: (0, 0)), pl.BlockSpec((None, d, nb), lambda i: (i, 0, 0)),
                  pl.BlockSpec((None, 1, nb), lambda i: (i, 0, 0))],
        out_specs=pl.BlockSpec((None, 16, nb), lambda i: (i, 0, 0)),
        out_shape=jax.ShapeDtypeStruct((depth, 16, nb), F32),
        compiler_params=_params("parallel"),
    )(cond, ada_w, bias.reshape(depth, 1, nb))


def _mod_bwd(cond, ada_w, dm_all, dm_mine):
    depth, d, nb = ada_w.shape
    d3 = dm_all.shape[-1]

    def body(c_ref, w_ref, all_ref, call_ref, mine_ref, cmine_ref, gw_ref, gb_ref, part_ref, ds_ref):
        i = pl.program_id(0)
        cond_v = c_ref[...]
        s = _silu(cond_v).astype(BF16)
        has_ctx = jnp.where(i < 2, 1.0, 0.0)
        tot_all = jnp.sum(call_ref[...], axis=0, keepdims=True) * has_ctx
        tot_mine = jnp.broadcast_to(jnp.sum(cmine_ref[...], axis=0, keepdims=True) * has_ctx, (8, nb)).astype(BF16)
        gb_ref[...] = jnp.sum(all_ref[...], axis=0, keepdims=True) + tot_all
        gw_ref[...] = _dot(s[0:8], mine_ref[...].astype(BF16), 0, 0) + _dot(s[8:16], tot_mine, 0, 0)
        part = _dot(tot_mine, w_ref[...].astype(BF16), 1, 1)

        @pl.when(i == 0)
        def _():
            part_ref[...] = jnp.zeros_like(part_ref)
            ds_ref[...] = _dsilu(cond_v)

        part_ref[...] += part

    def rows(width, which):
        return pl.BlockSpec((None, N_DEV, width), which)

    layer = lambda i: (i, 0, 0)
    ctx_layer = lambda i: (jnp.minimum(i, 1) + 4, 0, 0)
    return _pc(
        body, name="mod_bwd", grid=(depth,),
        in_specs=[pl.BlockSpec((16, d), lambda i: (0, 0)), pl.BlockSpec((None, d, nb), layer),
                  rows(d3, layer), rows(d3, ctx_layer), rows(nb, layer), rows(nb, ctx_layer)],
        out_specs=[pl.BlockSpec((None, d, nb), layer), pl.BlockSpec((None, 1, d3), layer),
                   pl.BlockSpec((8, d), lambda i: (0, 0)), pl.BlockSpec((16, d), lambda i: (0, 0))],
        out_shape=[jax.ShapeDtypeStruct((depth, d, nb), F32), jax.ShapeDtypeStruct((depth, 1, d3), F32),
                   jax.ShapeDtypeStruct((8, d), F32), jax.ShapeDtypeStruct((16, d), F32)],
        compiler_params=_params("arbitrary"),
    )(cond, ada_w, dm_all, dm_all, dm_mine, dm_mine)


def _norm_fwd(xs, g, mod, tr, seg_tiles, name):
    t, d = xs.shape

    def body(x_ref, g_ref, mod_ref, h_ref):
        x = x_ref[...]
        r = lax.rsqrt(jnp.mean(x * x, axis=-1, keepdims=True) + RMS_EPS)
        y = (x * r) * g_ref[...]
        h_ref[...] = (y * (1.0 + mod_ref[1:2, :]) + mod_ref[0:1, :]).astype(BF16)

    return _pc(
        body, name=name, grid=(t // tr,),
        in_specs=[pl.BlockSpec((tr, d), lambda i: (i, 0)), pl.BlockSpec((1, d), lambda i: (0, 0)),
                  pl.BlockSpec((None, 8, d), lambda i: (i // seg_tiles, 0, 0))],
        out_specs=pl.BlockSpec((tr, d), lambda i: (i, 0)),
        out_shape=jax.ShapeDtypeStruct((t, d), BF16),
        compiler_params=_params("parallel"),
    )(xs, g, mod)


def _resid_grad(dx, i, seg_tiles, yx_ref, gate_ref, dyx_ref, gsum_ref):
    dyx_ref[...] = (dx * gate_ref[2:3, :]).astype(BF16)

    @pl.when(i % seg_tiles == 0)
    def _():
        gsum_ref[...] = jnp.zeros_like(gsum_ref)

    gsum_ref[0:1, :] += jnp.sum(dx * yx_ref[...], axis=0, keepdims=True)


def _norm_bwd(xs, dh, dres, g, mod, tr, seg_tiles, name, res_tiles=None, out_tiles=None, below=None):
    t, d = xs.shape
    n_tiles = t // tr
    res_tiles = n_tiles if res_tiles is None else res_tiles
    out_tiles = n_tiles if out_tiles is None else out_tiles

    def body(x_ref, dh_ref, dres_ref, g_ref, mod_ref, *rest):
        i = pl.program_id(0)
        x = x_ref[...]
        r = lax.rsqrt(jnp.mean(x * x, axis=-1, keepdims=True) + RMS_EPS)
        xn = x * r
        dhv = dh_ref[...]
        gain = g_ref[...]
        one_scale = 1.0 + mod_ref[1:2, :]
        dxn = dhv * (gain * one_scale)
        dx = r * (dxn - xn * jnp.mean(dxn * xn, axis=-1, keepdims=True))
        if res_tiles == n_tiles:
            dx = dx + dres_ref[...]
        else:
            dx = dx + jnp.where(i < res_tiles, dres_ref[...], 0.0)
        if below is None:
            dx_ref, sum_ref = rest
        else:
            yx_ref, gate_ref, dx_ref, sum_ref, dyx_ref, gsum_ref = rest
            _resid_grad(dx, i, seg_tiles, yx_ref, gate_ref, dyx_ref, gsum_ref)
        if out_tiles == n_tiles:
            dx_ref[...] = dx
        else:
            @pl.when(i < out_tiles)
            def _():
                dx_ref[...] = dx

        @pl.when(i % seg_tiles == 0)
        def _():
            sum_ref[...] = jnp.zeros_like(sum_ref)

        sum_ref[0:1, :] += jnp.sum(dhv, axis=0, keepdims=True)
        sum_ref[1:2, :] += jnp.sum(dhv * (xn * gain), axis=0, keepdims=True)
        sum_ref[2:3, :] += jnp.sum(dhv * one_scale * xn, axis=0, keepdims=True)

    row = pl.BlockSpec((tr, d), lambda i: (i, 0))
    seg = pl.BlockSpec((None, 8, d), lambda i: (i // seg_tiles, 0, 0))
    in_specs = [row, row, pl.BlockSpec((tr, d), lambda i: (jnp.minimum(i, res_tiles - 1), 0)),
                pl.BlockSpec((1, d), lambda i: (0, 0)), seg]
    out_specs = [pl.BlockSpec((tr, d), lambda i: (jnp.minimum(i, out_tiles - 1), 0)), seg]
    out_shape = [jax.ShapeDtypeStruct((out_tiles * tr, d), F32), jax.ShapeDtypeStruct((mod.shape[0], 8, d), F32)]
    args = [xs, dh, dres, g, mod]
    if below is not None:
        in_specs += [row, seg]
        out_specs += [row, seg]
        out_shape += [jax.ShapeDtypeStruct((t, d), BF16), jax.ShapeDtypeStruct((below[1].shape[0], 8, d), F32)]
        args += list(below)
    return _pc(
        body, name=name, grid=(n_tiles,), in_specs=in_specs, out_specs=out_specs, out_shape=out_shape,
        compiler_params=_params("arbitrary"),
    )(*args)


def _loss_head(xs, target, g, yx, mod, tr):
    t, d = xs.shape

    def body(x_ref, t_ref, g_ref, yx_ref, gate_ref, loss_ref, dx_ref, dg_ref, dyx_ref, gsum_ref):
        i = pl.program_id(0)
        x = x_ref[...]
        r = lax.rsqrt(jnp.mean(x * x, axis=-1, keepdims=True) + RMS_EPS)
        xn = x * r
        gain = g_ref[...]
        err = xn * gain - t_ref[...]
        dy = err * (1.0 / d)
        dxn = dy * gain
        dx = r * (dxn - xn * jnp.mean(dxn * xn, axis=-1, keepdims=True))
        dx_ref[...] = dx
        _resid_grad(dx, i, t // tr, yx_ref, gate_ref, dyx_ref, gsum_ref)

        @pl.when(i == 0)
        def _():
            loss_ref[...] = jnp.zeros_like(loss_ref)
            dg_ref[...] = jnp.zeros_like(dg_ref)

        loss_ref[...] += 0.5 * jnp.sum(jnp.mean(err * err, axis=-1, keepdims=True))
        dg_ref[0:1, :] += jnp.sum(dy * xn, axis=0, keepdims=True)

    row = pl.BlockSpec((tr, d), lambda i: (i, 0))
    seg = pl.BlockSpec((None, 8, d), lambda i: (0, 0, 0))
    return _pc(
        body, name="loss_head", grid=(t // tr,),
        in_specs=[row, row, pl.BlockSpec((1, d), lambda i: (0, 0)), row, seg],
        out_specs=[pl.BlockSpec((8, LANES), lambda i: (0, 0)), row, pl.BlockSpec((8, d), lambda i: (0, 0)), row, seg],
        out_shape=[jax.ShapeDtypeStruct((8, LANES), F32), jax.ShapeDtypeStruct((t, d), F32),
                   jax.ShapeDtypeStruct((8, d), F32), jax.ShapeDtypeStruct((t, d), BF16),
                   jax.ShapeDtypeStruct((1, 8, d), F32)],
        compiler_params=_params("arbitrary"),
    )(xs, target, g, yx, mod)


def _proj_in(h, w, layer, width, name, blocks=None, dtype=F32):
    t, d = h.shape
    n8 = w.shape[-1]
    first, count = blocks if blocks is not None else (0, N_DEV)
    per_part = width // n8
    tm = _tile(t, 1152)

    def body(a_ref, b_ref, o_ref):
        o_ref[...] = _dot(a_ref[...], b_ref[...]).astype(dtype)

    return _pc(
        body, name=name, grid=(t // tm, count),
        in_specs=[pl.BlockSpec((tm, d), lambda i, j: (i, 0)),
                  pl.BlockSpec((None, None, d, n8), lambda i, j: (first + j, layer, 0, 0))],
        out_specs=pl.BlockSpec((None, tm, n8), lambda i, j: (j // per_part, i, j % per_part)),
        out_shape=jax.ShapeDtypeStruct((count // per_part, t, width), dtype),
        compiler_params=_params("parallel", "parallel"),
    )(h, w)


def _proj_out(z, w, res, mod, tm, seg_tiles, name, nxt=None):
    t, k = z.shape
    d = w.shape[1]

    def body(z_ref, w_ref, res_ref, mod_ref, *rest):
        yx = _dot(z_ref[...], w_ref[...])
        x = res_ref[...] + mod_ref[2:3, :] * yx
        if nxt is None:
            yx_ref, x_ref = rest
        else:
            g_ref, nmod_ref, yx_ref, x_ref, h_ref = rest
            r = lax.rsqrt(jnp.mean(x * x, axis=-1, keepdims=True) + RMS_EPS)
            h_ref[...] = (((x * r) * g_ref[...]) * (1.0 + nmod_ref[1:2, :]) + nmod_ref[0:1, :]).astype(BF16)
        yx_ref[...] = yx
        x_ref[...] = x

    tile = pl.BlockSpec((tm, d), lambda i: (i, 0))
    seg = pl.BlockSpec((None, 8, d), lambda i: (i // seg_tiles, 0, 0))
    in_specs = [pl.BlockSpec((tm, k), lambda i: (i, 0)), pl.BlockSpec((k, d), lambda i: (0, 0)), tile, seg]
    out_specs = [tile, tile]
    out_shape = [jax.ShapeDtypeStruct((t, d), F32), jax.ShapeDtypeStruct((t, d), F32)]
    args = [z, w, res, mod]
    if nxt is not None:
        in_specs += [pl.BlockSpec((1, d), lambda i: (0, 0)), seg]
        out_specs.append(tile)
        out_shape.append(jax.ShapeDtypeStruct((t, d), BF16))
        args += list(nxt)
    return _pc(
        body, name=name, grid=(t // tm,), in_specs=in_specs, out_specs=out_specs, out_shape=out_shape,
        compiler_params=_params("parallel"),
    )(*args)


def _proj_out_dz(dyx, w, name):
    t, d = dyx.shape
    width = w.shape[0]
    tm, tn = _tile(t, 1024), _tile(width, 512)

    def body(a_ref, w_ref, o_ref):
        o_ref[...] = _dot(a_ref[...], w_ref[...], 1, 1)

    return _pc(
        body, name=name, grid=(t // tm, width // tn),
        in_specs=[pl.BlockSpec((tm, d), lambda i, j: (i, 0)), pl.BlockSpec((tn, d), lambda i, j: (j, 0))],
        out_specs=pl.BlockSpec((tm, tn), lambda i, j: (i, j)),
        out_shape=jax.ShapeDtypeStruct((t, width), F32),
        compiler_params=_params("parallel", "parallel"),
    )(dyx, w)


def _proj_in_dh(dpre, w, layer, name, after=None):
    parts, t, width = dpre.shape
    d, n8 = w.shape[-2:]
    per_part = width // n8
    tm, tn = _tile(t, 1152), _tile(d, 512)

    def body(a_ref, w_ref, *rest):
        o_ref = rest[-1]
        part = _dot(a_ref[:, 0:n8], w_ref[0], 1, 1)
        for s in range(1, per_part):
            part += _dot(a_ref[:, s * n8:(s + 1) * n8], w_ref[s], 1, 1)

        @pl.when(pl.program_id(2) == 0)
        def _():
            o_ref[...] = part

        @pl.when(pl.program_id(2) != 0)
        def _():
            o_ref[...] += part

    extra = [] if after is None else [after]
    return _pc(
        body, name=name, grid=(t // tm, d // tn, parts),
        in_specs=[pl.BlockSpec((None, tm, width), lambda i, j, k: (k, i, 0)),
                  pl.BlockSpec((per_part, None, tn, n8), lambda i, j, k: (k, layer, j, 0))] + [ANY] * len(extra),
        out_specs=pl.BlockSpec((tm, tn), lambda i, j, k: (i, j)),
        out_shape=jax.ShapeDtypeStruct((t, d), F32),
        compiler_params=_params("parallel", "parallel", "arbitrary"),
    )(dpre, w, *extra)


def _transposed(a_ref):
    return a_ref[...].T


def _grad_w_in(h, dpre, n8, name):
    t, d = h.shape
    parts, _, width = dpre.shape
    per_part = width // n8
    tm, tk = _tile(d, 512), _tile(t, 1152)
    nk = t // tk

    def body(a_ref, b_ref, o_ref, acc_ref):
        k = pl.program_id(1)

        @pl.when(k == 0)
        def _():
            acc_ref[...] = jnp.zeros_like(acc_ref)

        at = _transposed(a_ref)
        for p in range(parts):
            r = _dot(at, b_ref[p])
            for s in range(per_part):
                acc_ref[p * per_part + s] += r[:, s * n8:(s + 1) * n8]

        @pl.when(k == nk - 1)
        def _():
            o_ref[...] = acc_ref[...].astype(BF16)

    return _pc(
        body, name=name, grid=(d // tm, nk),
        in_specs=[pl.BlockSpec((tk, tm), lambda i, k: (k, i)), pl.BlockSpec((parts, tk, width), lambda i, k: (0, k, 0))],
        out_specs=pl.BlockSpec((parts * per_part, tm, n8), lambda i, k: (0, i, 0)),
        out_shape=jax.ShapeDtypeStruct((parts * per_part, d, n8), BF16),
        scratch_shapes=[pltpu.VMEM((parts * per_part, tm, n8), F32)],
        compiler_params=_params("parallel", "arbitrary"),
    )(h, dpre)


def _grad_w_out(z, dyx, name):
    width = z.shape[1]
    t, d = dyx.shape
    tm, tk = _tile(width, 512), _tile(t, 1152)
    nk = t // tk

    def body(a_ref, b_ref, o_ref, acc_ref):
        k = pl.program_id(1)

        @pl.when(k == 0)
        def _():
            acc_ref[...] = jnp.zeros_like(acc_ref)

        acc_ref[...] += _dot(_transposed(a_ref), b_ref[...])

        @pl.when(k == nk - 1)
        def _():
            o_ref[...] = acc_ref[...].astype(BF16)

    return _pc(
        body, name=name, grid=(width // tm, nk),
        in_specs=[pl.BlockSpec((tk, tm), lambda i, k: (k, i)), pl.BlockSpec((tk, d), lambda i, k: (k, 0))],
        out_specs=pl.BlockSpec((tm, d), lambda i, k: (i, 0)),
        out_shape=jax.ShapeDtypeStruct((width, d), BF16),
        scratch_shapes=[pltpu.VMEM((tm, d), F32)],
        compiler_params=_params("parallel", "arbitrary"),
    )(z, dyx)


def _shift(v, k):
    n = v.shape[0]
    return pltpu.roll(v, k % n, 0)


def _window_sum(v, win):
    s = v + _shift(v, 1)
    step = 1
    while 2 * step < win:
        s = _shift(s, step) + _shift(s, -step)
        step *= 2
    return s


def _window_count(base, seg_len, win, shape):
    t = base + lax.broadcasted_iota(jnp.int32, shape, 0)
    hi = jnp.minimum(t + win // 2, seg_len)
    lo = jnp.maximum(t - win // 2, 0)
    return (hi - lo).astype(F32)


def _pad_offsets(segs):
    return [HALO * (s + 1) + st for s, (st, _) in enumerate(segs)]


def _for_chunks(segs, fn):
    offs = _pad_offsets(segs)
    for s, (st, ln) in enumerate(segs):
        def step(ci, carry, s=s, st=st, ln=ln):
            fn(s, st, ln, offs[s], pl.multiple_of(ci * CHUNK, CHUNK))
            return carry
        lax.fori_loop(0, ln // CHUNK, step, 0)


def _pool_fwd(pre, w_grp, scale, segs, name):
    _, t, width = pre.shape
    grp = width // len(POOL_WINDOWS)
    padded = t + HALO * (len(segs) + 1)

    def group(win, pre_ref, w_ref, sc_ref, z_ref, diff_ref, pad_ref):
        pad_ref[...] = jnp.zeros_like(pad_ref)

        def fill(s, st, ln, off, b):
            pad_ref[pl.ds(off + b, CHUNK), :] = pre_ref[0, pl.ds(st + b, CHUNK), :]

        _for_chunks(segs, fill)

        def mix(s, st, ln, off, b):
            ext = pad_ref[pl.ds(off - HALO + b, CHUNK + 2 * HALO), :]
            total = _window_sum(ext, win)[HALO:HALO + CHUNK]
            u = pre_ref[0, pl.ds(st + b, CHUNK), :]
            diff = (total / _window_count(b, ln, win, u.shape) - u).astype(BF16)
            mixed = _dot(diff, w_ref[...])
            gate = _silu(pre_ref[1, pl.ds(st + b, CHUNK), :])
            z_ref[pl.ds(st + b, CHUNK), :] = (mixed * sc_ref[...] * gate).astype(BF16)
            diff_ref[pl.ds(st + b, CHUNK), :] = diff

        _for_chunks(segs, mix)

    def body(pre_ref, w_ref, sc_ref, z_ref, diff_ref, pad_ref):
        gi = pl.program_id(0)
        for widx, win in enumerate(POOL_WINDOWS):
            @pl.when(gi == widx)
            def _(win=win):
                group(win, pre_ref, w_ref, sc_ref, z_ref, diff_ref, pad_ref)

    col = pl.BlockSpec((t, grp), lambda g: (0, g))
    return _pc(
        body, name=name, grid=(len(POOL_WINDOWS),),
        in_specs=[pl.BlockSpec((2, t, grp), lambda g: (0, 0, g)), pl.BlockSpec((None, grp, grp), lambda g: (g, 0, 0)),
                  pl.BlockSpec((1, grp), lambda g: (0, g))],
        out_specs=[col, col],
        out_shape=[jax.ShapeDtypeStruct((t, width), BF16), jax.ShapeDtypeStruct((t, width), BF16)],
        scratch_shapes=[pltpu.VMEM((padded, grp), F32)],
        compiler_params=_params("parallel"),
    )(pre, w_grp, scale)


def _pool_bwd(dz, diff, pre, w_grp, scale, segs, name):
    _, t, width = pre.shape
    grp = width // len(POOL_WINDOWS)
    padded = t + HALO * (len(segs) + 1)

    def group(win, dz_ref, diff_ref, pre_ref, w_ref, sc_ref, dpre_ref, dw_ref, dsc_ref, pad_ref, dd_ref):
        pad_ref[...] = jnp.zeros_like(pad_ref)
        dw_ref[...] = jnp.zeros_like(dw_ref)
        dsc_ref[...] = jnp.zeros_like(dsc_ref)

        def first(s, st, ln, off, b):
            rows = pl.ds(st + b, CHUNK)
            diff_v = diff_ref[rows, :]
            mixed = _dot(diff_v, w_ref[...])
            g = pre_ref[1, rows, :]
            sg = _silu(g)
            dzv = dz_ref[rows, :]
            dmixed = (dzv * sc_ref[...] * sg).astype(BF16)
            dsc_ref[...] += jnp.sum(dzv * mixed * sg, axis=0, keepdims=True)
            dpre_ref[1, rows, :] = (dzv * mixed * sc_ref[...] * _dsilu(g)).astype(BF16)
            ddiff = _dot(dmixed, w_ref[...], 1, 1)
            dw_ref[...] += _dot(diff_v, dmixed, 0, 0)
            dd_ref[rows, :] = ddiff
            pad_ref[pl.ds(off + b, CHUNK), :] = ddiff / _window_count(b, ln, win, ddiff.shape)

        _for_chunks(segs, first)

        def second(s, st, ln, off, b):
            rows = pl.ds(st + b, CHUNK)
            ext = pad_ref[pl.ds(off - HALO + b, CHUNK + 2 * HALO), :]
            total = _shift(_window_sum(ext, win), -1)[HALO:HALO + CHUNK]
            dpre_ref[0, rows, :] = (total - dd_ref[rows, :]).astype(BF16)

        _for_chunks(segs, second)

    def body(dz_ref, diff_ref, pre_ref, w_ref, sc_ref, dpre_ref, dw_ref, dsc_ref, pad_ref, dd_ref):
        gi = pl.program_id(0)
        for widx, win in enumerate(POOL_WINDOWS):
            @pl.when(gi == widx)
            def _(win=win):
                group(win, dz_ref, diff_ref, pre_ref, w_ref, sc_ref, dpre_ref, dw_ref, dsc_ref, pad_ref, dd_ref)

    col = pl.BlockSpec((t, grp), lambda g: (0, g))
    both = pl.BlockSpec((2, t, grp), lambda g: (0, 0, g))
    wspec = pl.BlockSpec((None, grp, grp), lambda g: (g, 0, 0))
    sspec = pl.BlockSpec((1, grp), lambda g: (0, g))
    return _pc(
        body, name=name, grid=(len(POOL_WINDOWS),),
        in_specs=[col, col, both, wspec, sspec],
        out_specs=[both, wspec, sspec],
        out_shape=[jax.ShapeDtypeStruct((2, t, width), BF16), jax.ShapeDtypeStruct((len(POOL_WINDOWS), grp, grp), F32),
                   jax.ShapeDtypeStruct((1, width), F32)],
        scratch_shapes=[pltpu.VMEM((padded, grp), F32), pltpu.VMEM((t, grp), F32)],
        compiler_params=_params("parallel"),
    )(dz, diff, pre, w_grp, scale)


def _conv_fwd(pre, dw, db, name):
    _, t, width = pre.shape
    cb = LANES
    segs = [(0, t)]

    def body(pre_ref, dw_ref, db_ref, z_ref, pad_ref):
        pad_ref[...] = jnp.zeros_like(pad_ref)

        def fill(s, st, ln, off, b):
            rows = pl.ds(b, CHUNK)
            pad_ref[pl.ds(off + b, CHUNK), :] = pre_ref[1, rows, :] * pre_ref[2, rows, :]

        _for_chunks(segs, fill)

        def mix(s, st, ln, off, b):
            rows = pl.ds(b, CHUNK)
            ext = pad_ref[pl.ds(off - HALO + b, CHUNK + 2 * HALO), :]
            conv = (dw_ref[0:1, :] * _shift(ext, 1) + dw_ref[1:2, :] * ext + dw_ref[2:3, :] * _shift(ext, -1))
            conv = conv[HALO:HALO + CHUNK] + db_ref[...]
            y = pre_ref[0, rows, :] * conv
            z_ref[rows, :] = (y * _silu(pre_ref[3, rows, :])).astype(BF16)

        _for_chunks(segs, mix)

    return _pc(
        body, name=name, grid=(width // cb,),
        in_specs=[pl.BlockSpec((4, t, cb), lambda j: (0, 0, j)), pl.BlockSpec((8, cb), lambda j: (0, j)),
                  pl.BlockSpec((1, cb), lambda j: (0, j))],
        out_specs=pl.BlockSpec((t, cb), lambda j: (0, j)),
        out_shape=jax.ShapeDtypeStruct((t, width), BF16),
        scratch_shapes=[pltpu.VMEM((t + 2 * HALO, cb), F32)],
        compiler_params=_params("parallel"),
    )(pre, dw, db)


def _conv_bwd(dz, pre, dw, db, name):
    _, t, width = pre.shape
    cb = LANES
    segs = [(0, t)]

    def body(dz_ref, pre_ref, dw_ref, db_ref, dpre_ref, ddw_ref, ddb_ref, pad_a, pad_c):
        pad_a[...] = jnp.zeros_like(pad_a)
        pad_c[...] = jnp.zeros_like(pad_c)
        ddw_ref[...] = jnp.zeros_like(ddw_ref)
        ddb_ref[...] = jnp.zeros_like(ddb_ref)

        def fill(s, st, ln, off, b):
            rows = pl.ds(b, CHUNK)
            pad_a[pl.ds(off + b, CHUNK), :] = pre_ref[1, rows, :] * pre_ref[2, rows, :]

        _for_chunks(segs, fill)

        def first(s, st, ln, off, b):
            rows = pl.ds(b, CHUNK)
            ext = pad_a[pl.ds(off - HALO + b, CHUNK + 2 * HALO), :]
            prev, nxt = _shift(ext, 1)[HALO:HALO + CHUNK], _shift(ext, -1)[HALO:HALO + CHUNK]
            here = ext[HALO:HALO + CHUNK]
            conv = dw_ref[0:1, :] * prev + dw_ref[1:2, :] * here + dw_ref[2:3, :] * nxt + db_ref[...]
            bg, g = pre_ref[0, rows, :], pre_ref[3, rows, :]
            dzv = dz_ref[rows, :]
            dy = dzv * _silu(g)
            dpre_ref[3, rows, :] = (dzv * (bg * conv) * _dsilu(g)).astype(BF16)
            dpre_ref[0, rows, :] = (dy * conv).astype(BF16)
            dconv = dy * bg
            pad_c[pl.ds(off + b, CHUNK), :] = dconv
            ddw_ref[0:1, :] += jnp.sum(dconv * prev, axis=0, keepdims=True)
            ddw_ref[1:2, :] += jnp.sum(dconv * here, axis=0, keepdims=True)
            ddw_ref[2:3, :] += jnp.sum(dconv * nxt, axis=0, keepdims=True)
            ddb_ref[0:1, :] += jnp.sum(dconv, axis=0, keepdims=True)

        _for_chunks(segs, first)

        def second(s, st, ln, off, b):
            rows = pl.ds(b, CHUNK)
            ext = pad_c[pl.ds(off - HALO + b, CHUNK + 2 * HALO), :]
            da = (dw_ref[0:1, :] * _shift(ext, -1) + dw_ref[1:2, :] * ext + dw_ref[2:3, :] * _shift(ext, 1))
            da = da[HALO:HALO + CHUNK]
            dpre_ref[1, rows, :] = (da * pre_ref[2, rows, :]).astype(BF16)
            dpre_ref[2, rows, :] = (da * pre_ref[1, rows, :]).astype(BF16)

        _for_chunks(segs, second)

    quad = pl.BlockSpec((4, t, cb), lambda j: (0, 0, j))
    rows8 = pl.BlockSpec((8, cb), lambda j: (0, j))
    return _pc(
        body, name=name, grid=(width // cb,),
        in_specs=[pl.BlockSpec((t, cb), lambda j: (0, j)), quad, rows8, pl.BlockSpec((1, cb), lambda j: (0, j))],
        out_specs=[quad, rows8, rows8],
        out_shape=[jax.ShapeDtypeStruct((4, t, width), BF16), jax.ShapeDtypeStruct((8, width), F32),
                   jax.ShapeDtypeStruct((8, width), F32)],
        scratch_shapes=[pltpu.VMEM((t + 2 * HALO, cb), F32), pltpu.VMEM((t + 2 * HALO, cb), F32)],
        compiler_params=_params("parallel"),
    )(dz, pre, dw, db)


PAIR_TILES = 2 * WIN_ROWS - 2


def _pair_geometry():
    lane = lax.broadcasted_iota(jnp.int32, (GRID_W, LANES), 1)
    qcol = lax.broadcasted_iota(jnp.int32, (GRID_W, LANES), 0)
    low = lane < GRID_W
    kcol = jnp.where(low, lane, lane - GRID_W)
    start = jnp.clip(qcol - WIN_COLS // 2, 0, GRID_W - WIN_COLS)
    inside = (kcol >= start) & (kcol < start + WIN_COLS)
    return low, inside


def _bias_tiles(rpb_ref, rows_ref, tiles_ref, inside):
    for h in range(2):
        rows = rpb_ref[h]
        rows_ref[h] = (pltpu.roll(rows, LANES - (WIN_COLS - 1), 1)
                       + pltpu.roll(pltpu.roll(rows, GRID_W - (WIN_COLS - 1), 1), 2 * WIN_ROWS - 1, 0))
        for t in range(PAIR_TILES):
            both = jnp.broadcast_to(rows_ref[h, t:t + 1, :], (GRID_W, LANES))
            tiles_ref[h, t] = jnp.where(inside, pltpu.roll(both, 0, 1, stride=1, stride_axis=0), MASKED)


def _bias_tiles_grad(dtiles_ref, drpb_ref):
    n = PAIR_TILES * GRID_W
    qcol = lax.broadcasted_iota(jnp.int32, (n, LANES), 0) & (GRID_W - 1)
    lane = lax.broadcasted_iota(jnp.int32, (1, LANES), 1)
    zero = jnp.zeros((1, LANES), F32)
    for h in range(2):
        v = pltpu.roll(dtiles_ref[h].reshape(n, LANES), WIN_COLS - 1, 1)
        for bit in range(6):
            v = jnp.where((qcol >> bit) & 1 == 1, pltpu.roll(v, LANES - (1 << bit), 1), v)
        sums = [jnp.sum(v[t * GRID_W:(t + 1) * GRID_W], axis=0, keepdims=True) for t in range(PAIR_TILES)]
        for r in range(2 * WIN_ROWS):
            here = sums[r] if r < PAIR_TILES else zero
            prev = pltpu.roll(sums[r - 1], GRID_W, 1) if 1 <= r <= PAIR_TILES else zero
            drpb_ref[h, r:r + 1, :] = jnp.where(lane < 2 * WIN_COLS - 1, here + prev, 0.0)


def _attn_rows(r, n_rows):
    first = jnp.clip(r - WIN_ROWS // 2, 0, n_rows - WIN_ROWS)
    return first, first - r + WIN_ROWS - 1


def _softmax(s_loc, s_ctx):
    m = jnp.maximum(jnp.max(s_loc, axis=-1, keepdims=True), jnp.max(s_ctx, axis=-1, keepdims=True))
    e_loc, e_ctx = jnp.exp(s_loc - m), jnp.exp(s_ctx - m)
    inv = 1.0 / (jnp.sum(e_loc, axis=-1, keepdims=True) + jnp.sum(e_ctx, axis=-1, keepdims=True))
    return e_loc * inv, e_ctx * inv


def _pair_bias(tiles_ref, j):
    return jnp.concatenate(
        [jnp.concatenate([tiles_ref[h, j + 2 * m] for m in range(WIN_ROWS // 2)], axis=1) for h in range(2)], axis=0)


ROWS_PER_STEP = 4


def _by_head(tile, low):
    zero = jnp.zeros_like(tile)
    return jnp.concatenate([jnp.where(low, tile, zero), jnp.where(low, zero, tile)], axis=0)


def _merge_heads(stacked, low):
    return jnp.where(low, stacked[:GRID_W], stacked[GRID_W:])


def _attn_items(step, n_rows, q_ref, low):
    items = []
    for u in range(ROWS_PER_STEP):
        r = step * ROWS_PER_STEP + u
        first, j = _attn_rows(r, n_rows)
        rows = pl.ds(pl.multiple_of(r * GRID_W, GRID_W), GRID_W)
        keys = pl.ds(pl.multiple_of(first * GRID_W, GRID_W), WIN_ROWS * GRID_W)
        q = (q_ref[rows, :].astype(F32) * HEAD_DIM ** -0.5).astype(BF16)
        items.append((rows, keys, j, _by_head(q, low)))
    return items


def _attn_fwd(qkv, gate, rpb, seq):
    _, t, width = qkv.shape
    n_rows = seq // GRID_W
    n_ctx = t - seq
    blk = WIN_ROWS * GRID_W

    def body(q_ref, k_ref, v_ref, g_ref, rpb_ref, z_ref, o_ref, rows_ref, tiles_ref):
        low, inside = _pair_geometry()
        _bias_tiles(rpb_ref, rows_ref, tiles_ref, inside)
        ctx = pl.ds(seq, n_ctx)

        def step(i, carry):
            items = _attn_items(i, n_rows, q_ref, low)
            k_ctx, v_ctx = k_ref[ctx, :], v_ref[ctx, :]
            scores = [(_dot(q, k_ref[keys, :], 1, 1) + _pair_bias(tiles_ref, j), _dot(q, k_ctx, 1, 1))
                      for _, keys, j, q in items]
            probs = [_softmax(s_loc, s_ctx) for s_loc, s_ctx in scores]
            outs = [_dot(p_loc.astype(BF16), v_ref[keys, :]) + _dot(p_ctx.astype(BF16), v_ctx)
                    for (_, keys, _, _), (p_loc, p_ctx) in zip(items, probs)]
            for (rows, _, _, _), out in zip(items, outs):
                o = _merge_heads(out, low)
                o_ref[rows, :] = o
                z_ref[rows, :] = (o * _silu(g_ref[rows, :])).astype(BF16)
            return carry

        lax.fori_loop(0, n_rows // ROWS_PER_STEP, step, 0)

    def part(p):
        return pl.BlockSpec((None, t, LANES), lambda h: (p, 0, h))

    out = pl.BlockSpec((seq, LANES), lambda h: (0, h))
    return _pc(
        body, name="attn_fwd", grid=(width // LANES,),
        in_specs=[part(0), part(1), part(2), part(0), pl.BlockSpec((2, 2 * WIN_ROWS, LANES), lambda h: (h, 0, 0))],
        out_specs=[out, out],
        out_shape=[jax.ShapeDtypeStruct((seq, width), BF16), jax.ShapeDtypeStruct((seq, width), F32)],
        scratch_shapes=[pltpu.VMEM((2, 2 * WIN_ROWS, LANES), F32), pltpu.VMEM((2, PAIR_TILES, GRID_W, LANES), F32)],
        compiler_params=_params("parallel"),
    )(qkv, qkv, qkv, gate, rpb)


def _attn_bwd(qkv, gate, o, dz, rpb, seq):
    _, t, width = qkv.shape
    n_rows = seq // GRID_W
    n_ctx = t - seq
    blk = WIN_ROWS * GRID_W
    heads = 2 * width // LANES

    def body(q_ref, k_ref, v_ref, g_ref, o_ref, dz_ref, rpb_ref, dpre_ref, drpb_ref,
             rows_ref, tiles_ref, dtiles_ref, dk_ref, dv_ref):
        low, inside = _pair_geometry()
        _bias_tiles(rpb_ref, rows_ref, tiles_ref, inside)
        dtiles_ref[...] = jnp.zeros_like(dtiles_ref)
        dk_ref[...] = jnp.zeros_like(dk_ref)
        dv_ref[...] = jnp.zeros_like(dv_ref)
        ctx = pl.ds(seq, n_ctx)

        def step(i, carry):
            items = _attn_items(i, n_rows, q_ref, low)
            k_ctx, v_ctx = k_ref[ctx, :], v_ref[ctx, :]
            d_outs = []
            for rows, _, _, _ in items:
                g = g_ref[rows, :]
                dzv = dz_ref[rows, :]
                dpre_ref[3, rows, :] = (dzv * o_ref[rows, :] * _dsilu(g)).astype(BF16)
                d_outs.append(_by_head((dzv * _silu(g)).astype(BF16), low))
            scores = [(_dot(q, k_ref[keys, :], 1, 1) + _pair_bias(tiles_ref, j), _dot(q, k_ctx, 1, 1))
                      for _, keys, j, q in items]
            dprobs = [(_dot(d_o, v_ref[keys, :], 1, 1), _dot(d_o, v_ctx, 1, 1))
                      for (_, keys, _, _), d_o in zip(items, d_outs)]
            probs = [_softmax(s_loc, s_ctx) for s_loc, s_ctx in scores]
            dscores = []
            for (p_loc, p_ctx), (dp_loc, dp_ctx) in zip(probs, dprobs):
                delta = (jnp.sum(p_loc * dp_loc, axis=-1, keepdims=True)
                         + jnp.sum(p_ctx * dp_ctx, axis=-1, keepdims=True))
                dscores.append((p_loc * (dp_loc - delta), p_ctx * (dp_ctx - delta)))
            dqs = [_dot(ds_loc.astype(BF16), k_ref[keys, :]) + _dot(ds_ctx.astype(BF16), k_ctx)
                   for (_, keys, _, _), (ds_loc, ds_ctx) in zip(items, dscores)]
            for (rows, _, _, _), dq in zip(items, dqs):
                dpre_ref[0, rows, :] = (_merge_heads(dq, low) * HEAD_DIM ** -0.5).astype(BF16)
            for (_, keys, j, q), d_o, (p_loc, p_ctx), (ds_loc, ds_ctx) in zip(items, d_outs, probs, dscores):
                dk_ref[keys, :] += _dot(ds_loc.astype(BF16), q, 0, 0)
                dk_ref[ctx, :] += _dot(ds_ctx.astype(BF16), q, 0, 0)
                dv_ref[keys, :] += _dot(p_loc.astype(BF16), d_o, 0, 0)
                dv_ref[ctx, :] += _dot(p_ctx.astype(BF16), d_o, 0, 0)
                for h in range(2):
                    for m in range(WIN_ROWS // 2):
                        dtiles_ref[h, j + 2 * m] += ds_loc[h * GRID_W:(h + 1) * GRID_W, m * LANES:(m + 1) * LANES]
            return carry

        lax.fori_loop(0, n_rows // ROWS_PER_STEP, step, 0)
        dpre_ref[1] = dk_ref[...].astype(BF16)
        dpre_ref[2] = dv_ref[...].astype(BF16)
        dpre_ref[0, ctx, :] = jnp.zeros((n_ctx, LANES), BF16)
        dpre_ref[3, ctx, :] = jnp.zeros((n_ctx, LANES), BF16)
        _bias_tiles_grad(dtiles_ref, drpb_ref)

    def part(p):
        return pl.BlockSpec((None, t, LANES), lambda h: (p, 0, h))

    lat = pl.BlockSpec((seq, LANES), lambda h: (0, h))
    rspec = pl.BlockSpec((2, 2 * WIN_ROWS, LANES), lambda h: (h, 0, 0))
    tiles = pltpu.VMEM((2, PAIR_TILES, GRID_W, LANES), F32)
    return _pc(
        body, name="attn_bwd", grid=(width // LANES,),
        in_specs=[part(0), part(1), part(2), part(0), lat, lat, rspec],
        out_specs=[pl.BlockSpec((4, t, LANES), lambda h: (0, 0, h)), rspec],
        out_shape=[jax.ShapeDtypeStruct((4, t, width), BF16), jax.ShapeDtypeStruct((heads, 2 * WIN_ROWS, LANES), F32)],
        scratch_shapes=[pltpu.VMEM((2, 2 * WIN_ROWS, LANES), F32), tiles, tiles,
                        pltpu.VMEM((t, LANES), F32), pltpu.VMEM((t, LANES), F32)],
        compiler_params=_params("parallel"),
    )(qkv, qkv, qkv, gate, o, dz, rpb)


def _adam_update(w, m, v, g):
    m2 = ADAM_B1 * m + (1.0 - ADAM_B1) * g
    v2 = ADAM_B2 * v + (1.0 - ADAM_B2) * (g * g)
    m_hat = m2 / (1.0 - ADAM_B1 ** ADAM_STEP)
    v_hat = v2 / (1.0 - ADAM_B2 ** ADAM_STEP)
    return -ADAM_LR * (m_hat / (jnp.sqrt(v_hat) + ADAM_EPS) + ADAM_WD * w), m2, v2


def _adamw(w, m, v, parts, name):
    rows, cols = w.shape
    tr = _tile(rows, max(8, 131072 // cols), 8)
    n_parts = len(parts)

    def body(*refs):
        w_ref, m_ref, v_ref = refs[:3]
        part_refs = refs[3:3 + n_parts]
        g_ref, d_ref, nm_ref, nv_ref = refs[3 + n_parts:]
        g = part_refs[0][...].astype(F32)
        for p in part_refs[1:]:
            g = g + p[...].astype(F32)
        g_ref[...] = g
        d_ref[...], nm_ref[...], nv_ref[...] = _adam_update(w_ref[...], m_ref[...], v_ref[...], g)

    tile = pl.BlockSpec((tr, cols), lambda i: (i, 0))
    in_specs, args = [tile, tile, tile], [w, m, v]
    for p in parts:
        if isinstance(p, tuple):
            arr, k = p
            in_specs.append(pl.BlockSpec((None, tr, cols), lambda i, k=k: (k, i, 0)))
            args.append(arr)
        else:
            in_specs.append(tile)
            args.append(p)
    shape = jax.ShapeDtypeStruct((rows, cols), F32)
    return _pc(
        body, name=name, grid=(rows // tr,), in_specs=in_specs, out_specs=[tile] * 4, out_shape=[shape] * 4,
        compiler_params=_params("parallel"),
    )(*args)


def _adamw_small(states, grads):
    sources, makers = grads
    n, ns = len(states), len(sources)

    def body(*refs):
        src = refs[:ns]
        ins = refs[ns:ns + 3 * n]
        outs = refs[ns + 3 * n:]
        for k in range(n):
            w_ref, m_ref, v_ref = ins[3 * k:3 * k + 3]
            g = makers[k](*src)
            outs[4 * k][...] = g
            outs[4 * k + 1][...], outs[4 * k + 2][...], outs[4 * k + 3][...] = _adam_update(
                w_ref[...], m_ref[...], v_ref[...], g)

    flat = [a for s in states for a in s]
    vmem = pl.BlockSpec(memory_space=pltpu.VMEM)
    res = _pc(
        body, name="adamw_small",
        in_specs=[vmem] * (ns + 3 * n), out_specs=[vmem] * (4 * n),
        out_shape=[jax.ShapeDtypeStruct(s[0].shape, F32) for s in states for _ in range(4)],
        compiler_params=pltpu.CompilerParams(vmem_limit_bytes=VMEM_LIMIT),
    )(*sources, *flat)
    return [res[4 * k:4 * k + 4] for k in range(n)]


def _rows128(a):
    flat = a.reshape(-1)
    pad = (-flat.shape[0]) % LANES
    if pad:
        flat = jnp.concatenate([flat, jnp.zeros((pad,), flat.dtype)])
    return flat.reshape(-1, LANES)


def _pad_rows(a, mult=8):
    pad = (-a.shape[0]) % mult
    if pad:
        a = jnp.concatenate([a, jnp.zeros((pad,) + a.shape[1:], a.dtype)], axis=0)
    return a


def kernel(x, c, ctx, c_ctx, norm_g, ada_w, ada_b, pool_w_in, pool_w_grp, pool_scale, pool_w_out, na_w_in, na_rpb, na_w_out, conv_w_in, conv_dw, conv_db, conv_w_out, final_g, loss_target, m_c_ctx, m_norm_g, m_ada_w, m_ada_b, m_pool_w_in, m_pool_w_grp, m_pool_scale, m_pool_w_out, m_na_w_in, m_na_rpb, m_na_w_out, m_conv_w_in, m_conv_dw, m_conv_db, m_conv_w_out, m_final_g, v_c_ctx, v_norm_g, v_ada_w, v_ada_b, v_pool_w_in, v_pool_w_grp, v_pool_scale, v_pool_w_out, v_na_w_in, v_na_rpb, v_na_w_out, v_conv_w_in, v_conv_dw, v_conv_db, v_conv_w_out, v_final_g):
    xi, yi, ci = _my_place()
    me = 4 * xi + 2 * yi + ci
    seq, d = x.shape[1], x.shape[2]
    n_ctx = ctx.shape[1]
    t_all = seq + n_ctx
    width = d
    depth = norm_g.shape[0]
    nb = ada_w.shape[2]
    shard = width // N_DEV
    d_rows = d // LANES
    tr = math.gcd(math.gcd(seq, n_ctx), 256)
    x_tiles = seq // tr

    n_pool = pool_scale.shape[0]
    n_grp = pool_w_grp.shape[1]
    grp = width // n_grp
    layer_weights = [[pool_w_in[0], pool_w_grp[0], pool_w_out[0]], [na_w_out[0]],
                     [conv_w_in[0], conv_w_out[0]], [pool_w_in[1], pool_w_grp[1], pool_w_out[1]]]
    in_flight, token = [], jnp.zeros((8, LANES), F32)
    for i, ws in enumerate(layer_weights):
        if i == 1:
            na_in_first = _exchange_start([na_w_in[0].astype(BF16)], False, token, "weights_start1_chips", peers=CHIP_PEERS)
            token = na_in_first[-1]
        state = _exchange_start([w.astype(BF16) for w in ws], False, token, f"weights_start{i}")
        token = state[-1]
        in_flight.append(state)

    def landed_weight(i, t, after):
        return _exchange_wait(in_flight[i], False, after, f"weights_wait{i}_{t}", which=[t])[0]

    def as_in(w):
        return w[:, None]

    def as_grp(w):
        return w.transpose(1, 0, 2, 3).reshape(n_grp, grp, grp)

    def as_out(w):
        return w.reshape(width, d)

    small_in =_pad_rows(jnp.concatenate([_rows128(c), pool_scale, conv_dw[0], conv_db], axis=0))
    got = _gather_small(small_in, "gather_inputs")
    r0 = d_rows
    c_all = got[:, :r0].reshape(N_DEV, d)
    n_pool = pool_scale.shape[0]
    scale_full = got[:, r0:r0 + n_pool].transpose(1, 0, 2).reshape(n_pool, width)
    r1 = r0 + n_pool
    taps_full = _pad_rows(got[:, r1:r1 + 3].transpose(1, 0, 2).reshape(3, width))
    bias_full = got[:, r1 + 3:r1 + 4].transpose(1, 0, 2).reshape(1, width)

    cond = jnp.concatenate([c_all, c_ctx[None], jnp.zeros((7, d), F32)], axis=0)
    bias_mine = lax.dynamic_slice(ada_b, (0, me * nb), (depth, nb))
    mod_mine = _mod_fwd(cond, ada_w, bias_mine)
    mod_all = _gather_small(mod_mine.reshape(-1, LANES), "gather_mod")
    mod_all = mod_all.reshape(N_DEV, depth, 16, nb).transpose(1, 2, 0, 3).reshape(depth, 16, 3 * d)
    mod_x = lax.dynamic_index_in_dim(mod_all, me, 1, keepdims=False).reshape(depth, 3, d)
    mod_c = mod_all[:, 8].reshape(depth, 3, d)
    pad5 = jnp.zeros((depth, 5, d), F32)
    mod_x = jnp.concatenate([mod_x, pad5], axis=1)
    mod_c = jnp.concatenate([mod_c, pad5], axis=1)
    mods = [jnp.stack([mod_x[i], mod_c[i]]) if i < 2 else mod_x[i][None] for i in range(depth)]

    both = [(0, seq), (seq, n_ctx)]
    latent = [(0, seq)]

    def grp_slots(g):
        return g.reshape(n_grp, N_DEV, grp // N_DEV, grp).transpose(1, 0, 2, 3).reshape(N_DEV, -1, grp).astype(BF16)

    def send_grads(i, grads):
        return _exchange_start(grads, True, jnp.zeros((8, LANES), F32), f"grads_start{i}")

    xs0 = jnp.concatenate([x[0], ctx[0]], axis=0)
    h0 = _norm_fwd(xs0, norm_g[0:1] + token[0, 0], mods[0], tr, x_tiles, "norm_fwd0")
    pool_in_w0 = as_in(landed_weight(0, 0, h0))
    pre0 = _proj_in(h0, pool_in_w0, 0, width, "proj_in0")
    pool_grp_w0 = as_grp(landed_weight(0, 1, pre0))
    z0, diff0 = _pool_fwd(pre0, pool_grp_w0, scale_full[0:1], both, "pool_fwd0")
    pool_out_w0 = as_out(landed_weight(0, 2, z0))
    yx0, xs1, h1 = _proj_out(z0, pool_out_w0, xs0, mods[0], tr, x_tiles, "proj_out0", nxt=(norm_g[1:2], mods[1]))

    na_in_half = _exchange_wait(na_in_first, False, xs1, "weights_wait1_chips", peers=CHIP_PEERS)
    na_in_rest = _forward_start(na_in_half, h1, "weights_forward1")
    na_in_w = as_in(_forward_wait(na_in_rest, na_in_rest[-1], "weights_forward_wait1")[0])
    per_part = width // na_w_in.shape[2]
    qkv1 = _proj_in(h1, na_in_w, 0, width, "proj_in1_qkv", blocks=(0, 3 * per_part), dtype=BF16)
    gpre1 = _proj_in(h1, na_in_w, 0, width, "proj_in1_gate", blocks=(3 * per_part, per_part))
    rpb_rows = jnp.pad(na_rpb[0], ((0, 0), (0, 2 * WIN_ROWS - na_rpb.shape[2]), (0, LANES - na_rpb.shape[3])))
    z1, o1 = _attn_fwd(qkv1, gpre1, rpb_rows, seq)
    na_out_w = as_out(landed_weight(1, 0, z1))
    yx1, x2, h2 = _proj_out(z1, na_out_w, xs1, mods[1], tr, x_tiles, "proj_out1", nxt=(norm_g[2:3], mods[2]))

    conv_in_w = as_in(landed_weight(2, 0, h2))
    pre2 = _proj_in(h2, conv_in_w, 0, width, "proj_in2")
    z2 = _conv_fwd(pre2, taps_full, bias_full, "conv_fwd")
    conv_out_w = as_out(landed_weight(2, 1, z2))
    yx2, x3, h3 = _proj_out(z2, conv_out_w, x2, mods[2], tr, x_tiles, "proj_out2", nxt=(norm_g[3:4], mods[3]))

    pool_in_w3 = as_in(landed_weight(3, 0, h3))
    pre3 = _proj_in(h3, pool_in_w3, 0, width, "proj_in3")
    pool_grp_w3 = as_grp(landed_weight(3, 1, pre3))
    z3, diff3 = _pool_fwd(pre3, pool_grp_w3, scale_full[1:2], latent, "pool_fwd3")
    pool_out_w3 = as_out(landed_weight(3, 2, z3))
    yx3, x4 = _proj_out(z3, pool_out_w3, x3, mods[3], tr, x_tiles, "proj_out3")

    loss_part, dx4, d_final, dyx3, gate3 = _loss_head(x4, loss_target[0], final_g[None], yx3, mods[3], tr)

    dz3 = _proj_out_dz(dyx3, pool_out_w3, "proj_out_dz3")
    g_pool_out1 = _grad_w_out(z3, dyx3, "grad_w_out3")
    dpre3, g_grp1, g_scale1 = _pool_bwd(dz3, diff3, pre3, pool_grp_w3, scale_full[1:2], latent, "pool_bwd3")
    dh3 = _proj_in_dh(dpre3, pool_in_w3, 0, "proj_in_dh3")
    g_pool_in1 = _grad_w_in(h3, dpre3, pool_w_in.shape[2], "grad_w_in3")
    sent3 = send_grads(3, [g_pool_in1, grp_slots(g_grp1), g_pool_out1.reshape(N_DEV, shard, d)])
    dx3, norm3, dyx2, gate2 = _norm_bwd(x3, dh3, dx4, norm_g[3:4] + sent3[-1][0, 0], mods[3], tr, x_tiles, "norm_bwd3",
                                        below=(yx2, mods[2]))

    dz2 = _proj_out_dz(dyx2, conv_out_w, "proj_out_dz2")
    g_conv_out = _grad_w_out(z2, dyx2, "grad_w_out2")
    dpre2, g_taps, g_cbias = _conv_bwd(dz2, pre2, taps_full, bias_full, "conv_bwd")
    dh2 = _proj_in_dh(dpre2, conv_in_w, 0, "proj_in_dh2")
    g_conv_in = _grad_w_in(h2, dpre2, conv_w_in.shape[2], "grad_w_in2")
    sent2 = send_grads(2, [g_conv_in, g_conv_out.reshape(N_DEV, shard, d)])
    dx2, norm2, dyx1, gate1 = _norm_bwd(x2, dh2, dx3, norm_g[2:3] + sent2[-1][0, 0], mods[2], tr, x_tiles, "norm_bwd2",
                                        below=(yx1, mods[1][:1]))

    dz1 = _proj_out_dz(dyx1, na_out_w, "proj_out_dz1")
    g_na_out = _grad_w_out(z1, dyx1, "grad_w_out1")
    dpre1, g_rpb = _attn_bwd(qkv1, gpre1, o1, dz1, rpb_rows, seq)
    g_rpb = g_rpb[:, :na_rpb.shape[2], :na_rpb.shape[3]]
    dh1 = _proj_in_dh(dpre1, na_in_w, 0, "proj_in_dh1")
    g_na_in = _grad_w_in(h1, dpre1, na_w_in.shape[2], "grad_w_in1")
    sent1 = send_grads(1, [g_na_in, g_na_out.reshape(N_DEV, shard, d)])
    dxs1, norm1, dyx0, gate0 = _norm_bwd(xs1, dh1, dx2, norm_g[1:2] + sent1[-1][0, 0], mods[1], tr, x_tiles, "norm_bwd1",
                                         res_tiles=x_tiles, below=(yx0, mods[0]))

    dz0 = _proj_out_dz(dyx0, pool_out_w0, "proj_out_dz0")
    g_pool_out0 = _grad_w_out(z0, dyx0, "grad_w_out0")
    sent0a = _exchange_start([g_pool_out0.reshape(N_DEV, shard, d)], True, jnp.zeros((8, LANES), F32), "grads_start0a")
    dpre0, g_grp0, g_scale0 = _pool_bwd(dz0, diff0, pre0, pool_grp_w0, scale_full[0:1], both, "pool_bwd0")
    g_pool_in0 = _grad_w_in(h0, dpre0, pool_w_in.shape[2], "grad_w_in0")
    sent0b = _exchange_start([g_pool_in0, grp_slots(g_grp0)], True, sent0a[-1], "grads_start0b")
    dh0 = _proj_in_dh(dpre0, pool_in_w0, 0, "proj_in_dh0", after=sent0b[-1])
    dx0, norm0 = _norm_bwd(xs0, dh0, dxs1, norm_g[0:1], mods[0], tr, x_tiles, "norm_bwd0", out_tiles=x_tiles)
    grad_x = dx0[None]

    norms, gates = [norm0, norm1, norm2, norm3], [gate0, gate1, gate2, gate3]
    zero_d = jnp.zeros((d,), F32)
    dm_rows = [jnp.concatenate([norms[i][0, 0], norms[i][0, 1], gates[i][0, 0]]) for i in range(depth)]
    dm_rows.append(jnp.concatenate([norm0[1, 0], norm0[1, 1], gate0[1, 0]]))
    dm_rows.append(jnp.concatenate([norm1[1, 0], norm1[1, 1], zero_d]))
    dm_local = jnp.stack(dm_rows + [jnp.zeros((3 * d,), F32)] * 2)
    g_norm_part = jnp.stack([norm0[0, 2] + norm0[1, 2], norm1[0, 2] + norm1[1, 2], norm2[0, 2], norm3[0, 2]])
    pieces = [_rows128(dm_local), _rows128(g_norm_part), _rows128(d_final[0]), _pad_rows(_rows128(g_rpb)), loss_part]
    marks = np.cumsum([0] + [p.shape[0] for p in pieces])
    by_owner = [a.reshape(-1, N_DEV, shard).transpose(1, 0, 2) for a in (g_scale0, g_scale1, g_taps[:3], g_cbias[0:1])]
    by_owner = jnp.concatenate(by_owner + [jnp.zeros((N_DEV, 8 - n_pool - 4, shard), F32)], axis=1)
    small_sent = _exchange_start([jnp.concatenate(pieces, axis=0), by_owner], [False, True], jnp.zeros((8, LANES), F32),
                                 "small_grads_start")

    def big(parts, w, m, v, name):
        shape = w.shape
        view = (-1, shape[-1])
        parts = [(parts.reshape((N_DEV,) + w.reshape(view).shape), k) for k in range(N_DEV)]
        return [r.reshape(shape) for r in _adamw(w.reshape(view), m.reshape(view), v.reshape(view), parts, name)]

    in3, grp3, out3 = _exchange_wait(sent3, True, small_sent[-1], "grads_wait3")
    in2, out2 = _exchange_wait(sent2, True, small_sent[-1], "grads_wait2")
    in1, out1 = _exchange_wait(sent1, True, small_sent[-1], "grads_wait1")
    res = {}
    res["na_w_in"] = [r[None] for r in big(in1, na_w_in[0], m_na_w_in[0], v_na_w_in[0], "adamw_na_in")]
    res["na_w_out"] = [r[None] for r in big(out1, na_w_out[0], m_na_w_out[0], v_na_w_out[0], "adamw_na_out")]
    res["conv_w_in"] = [r[None] for r in big(in2, conv_w_in[0], m_conv_w_in[0], v_conv_w_in[0], "adamw_conv_in")]
    res["conv_w_out"] = [r[None] for r in big(out2, conv_w_out[0], m_conv_w_out[0], v_conv_w_out[0], "adamw_conv_out")]
    pool_in1 = big(in3, pool_w_in[1], m_pool_w_in[1], v_pool_w_in[1], "adamw_pool_in1")
    pool_grp1 = big(grp3, pool_w_grp[1], m_pool_w_grp[1], v_pool_w_grp[1], "adamw_pool_grp1")
    pool_out1 = big(out3, pool_w_out[1], m_pool_w_out[1], v_pool_w_out[1], "adamw_pool_out1")

    small_out, owned = _exchange_wait(small_sent, [False, True], pool_out1[0], "small_grads_wait")
    loss = jnp.sum(small_out[:, marks[4], 0])
    dm_all = small_out[:, :marks[1]].reshape(N_DEV, 8, 3 * d).transpose(1, 0, 2)
    dm_mine = lax.dynamic_slice(dm_all, (0, 0, me * nb), (8, N_DEV, nb))
    g_ada_w, g_ada_b, cctx_part, dsilu_cond = _mod_bwd(cond, ada_w, dm_all, dm_mine)
    cctx_all = _gather_small(_rows128(cctx_part[0]), "gather_cctx")

    def summed(ref, lo, hi):
        g = ref[0, lo:hi, :]
        for k in range(1, N_DEV):
            g = g + ref[k, lo:hi, :]
        return g

    makers = [
        lambda so, ow, cc, ab, ds: summed(cc, 0, d_rows) * ds[...],
        lambda so, ow, cc, ab, ds: summed(so, marks[1], marks[2]),
        lambda so, ow, cc, ab, ds: ab[...],
        lambda so, ow, cc, ab, ds: summed(so, marks[2], marks[3]),
        lambda so, ow, cc, ab, ds: summed(so, marks[3], marks[4]),
        lambda so, ow, cc, ab, ds: summed(ow, 0, n_pool),
        lambda so, ow, cc, ab, ds: summed(ow, n_pool, n_pool + 3),
        lambda so, ow, cc, ab, ds: summed(ow, n_pool + 3, n_pool + 4),
    ]
    rpb_rows128 = lambda a: _pad_rows(_rows128(a))
    views = [_rows128] * 4 + [rpb_rows128] + [lambda a: a.reshape(-1, LANES)] * 3
    small = [(c_ctx, m_c_ctx, v_c_ctx), (norm_g, m_norm_g, v_norm_g), (ada_b, m_ada_b, v_ada_b),
             (final_g, m_final_g, v_final_g), (na_rpb, m_na_rpb, v_na_rpb), (pool_scale, m_pool_scale, v_pool_scale),
             (conv_dw, m_conv_dw, v_conv_dw), (conv_db, m_conv_db, v_conv_db)]
    states = [tuple(view(a) for a in triple) for view, triple in zip(views, small)]
    sources = (small_out, owned, cctx_all, _rows128(g_ada_b), _rows128(dsilu_cond[8]))
    small_res = _adamw_small(states, (sources, makers))
    names = ["c_ctx", "norm_g", "ada_b", "final_g", "na_rpb", "pool_scale", "conv_dw", "conv_db"]
    for name, (w, _, _), outs4 in zip(names, small, small_res):
        res[name] = [r.reshape(-1)[:w.size].reshape(w.shape) for r in outs4]

    res["ada_w"] = [r.reshape(ada_w.shape) for r in _adamw(
        ada_w.reshape(-1, nb), m_ada_w.reshape(-1, nb), v_ada_w.reshape(-1, nb), [g_ada_w.reshape(-1, nb)], "adamw_ada_w")]

    out0, = _exchange_wait(sent0a, True, small_res[0][0], "grads_wait0a")
    in0, grp0 = _exchange_wait(sent0b, True, small_res[0][0], "grads_wait0b")
    pool_in0 = big(in0, pool_w_in[0], m_pool_w_in[0], v_pool_w_in[0], "adamw_pool_in0")
    pool_grp0 = big(grp0, pool_w_grp[0], m_pool_w_grp[0], v_pool_w_grp[0], "adamw_pool_grp0")
    pool_out0 = big(out0, pool_w_out[0], m_pool_w_out[0], v_pool_w_out[0], "adamw_pool_out0")
    res["pool_w_in"] = [jnp.stack([p, q]) for p, q in zip(pool_in0, pool_in1)]
    res["pool_w_grp"] = [jnp.stack([p, q]) for p, q in zip(pool_grp0, pool_grp1)]
    res["pool_w_out"] = [jnp.stack([p, q]) for p, q in zip(pool_out0, pool_out1)]

    order = ["c_ctx", "norm_g", "ada_w", "ada_b", "pool_w_in", "pool_w_grp", "pool_scale", "pool_w_out", "na_w_in",
             "na_rpb", "na_w_out", "conv_w_in", "conv_dw", "conv_db", "conv_w_out", "final_g"]
    outs = [loss, grad_x]
    for j in range(4):
        outs += [res[n][j] for n in order]
    return tuple(outs)
```

```python
import functools
import math

import numpy as np
import jax
import jax.numpy as jnp
from jax import lax
from jax.experimental import pallas as pl
from jax.experimental.pallas import tpu as pltpu

F32 = jnp.float32
BF16 = jnp.bfloat16
N_DEV = 8
LANES = 128
RMS_EPS = 1e-6
GRID_W = 64
WIN_ROWS = 8
WIN_COLS = 16
HEAD_DIM = 64
POOL_WINDOWS = (2, 4, 8, 16)
HALO = 8
CHUNK = 128
MASKED = -1e30
ADAM_LR = 0.001
ADAM_B1 = 0.9
ADAM_B2 = 0.999
ADAM_EPS = 1e-08
ADAM_WD = 0.01
ADAM_STEP = 10
VMEM_LIMIT = 56 * 1024 * 1024
MESH = pl.DeviceIdType.MESH
ANY = pl.BlockSpec(memory_space=pl.ANY)
HBM = pl.BlockSpec(memory_space=pltpu.HBM)
SEM = pl.BlockSpec(memory_space=pltpu.SEMAPHORE)
EFFECT = pltpu.SideEffectType.DATAFLOW_SIDE_EFFECTING


def _pc(body, *, name, **kw):
    return pl.pallas_call(body, name=name, **kw)


def _params(*sem):
    return pltpu.CompilerParams(dimension_semantics=sem if sem else None, vmem_limit_bytes=VMEM_LIMIT)


def _dot(a, b, ca=1, cb=0, precision=None):
    return lax.dot_general(a, b, (((ca,), (cb,)), ((), ())), preferred_element_type=F32, precision=precision)


def _tile(n, pref, unit=LANES):
    best = None
    for t in range(unit, min(n, pref) + 1, unit):
        if n % t == 0:
            best = t
    return best if best is not None else n


def _sigmoid(x):
    return 1.0 / (1.0 + jnp.exp(-x))


def _silu(x):
    return x * _sigmoid(x)


def _dsilu(x):
    s = _sigmoid(x)
    return s * (1.0 + x * (1.0 - s))


def _my_place():
    return lax.axis_index("x"), lax.axis_index("y"), lax.axis_index("c")


def _flip(v, f):
    return 1 - v if f else v


def _gather_small(block, name):
    rows, cols = block.shape

    def body(x_ref, out_ref, send_sems, recv_sems):
        x, y, c = _my_place()
        me = 4 * x + 2 * y + c
        out_ref[me] = x_ref[...]
        copies = []
        for k in range(1, N_DEV):
            peer = (_flip(x, k & 4), _flip(y, k & 2), _flip(c, k & 1))
            cp = pltpu.make_async_remote_copy(
                src_ref=x_ref, dst_ref=out_ref.at[me], send_sem=send_sems.at[k - 1], recv_sem=recv_sems.at[k - 1],
                device_id=peer, device_id_type=MESH)
            cp.start()
            copies.append(cp)
        for cp in copies:
            cp.wait()

    return _pc(
        body, name=name,
        out_shape=jax.ShapeDtypeStruct((N_DEV, rows, cols), block.dtype),
        in_specs=[pl.BlockSpec(memory_space=pltpu.VMEM)],
        out_specs=pl.BlockSpec(memory_space=pltpu.VMEM),
        scratch_shapes=[pltpu.SemaphoreType.DMA((N_DEV - 1,)), pltpu.SemaphoreType.DMA((N_DEV - 1,))],
    )(block)


ALL_PEERS = tuple(range(N_DEV))
CHIP_PEERS = (0, 1, 2, 4, 6)
OTHER_CHIPS = (2, 4, 6)


def _flag(per_dest, t):
    return per_dest[t] if isinstance(per_dest, (list, tuple)) else per_dest


def _peer(k):
    x, y, c = _my_place()
    peer = (_flip(x, k & 4), _flip(y, k & 2), _flip(c, k & 1))
    return peer, 4 * peer[0] + 2 * peer[1] + peer[2]


def _exchange_copies(srcs, lands, send_sems, recv_sems, per_dest, peers=ALL_PEERS):
    x, y, c = _my_place()
    me = 4 * x + 2 * y + c
    copies = []
    for t, (src, land) in enumerate(zip(srcs, lands)):
        for n, k in enumerate(peers):
            peer, dest = _peer(k)
            s = t * len(peers) + n
            copies.append(pltpu.make_async_remote_copy(
                src_ref=src.at[dest] if _flag(per_dest, t) else src, dst_ref=land.at[me],
                send_sem=send_sems[s], recv_sem=recv_sems[s], device_id=peer, device_id_type=MESH))
    return copies


def _forward_copies(lands, send_sems, recv_sems):
    sibling, _ = _peer(1)
    copies = []
    for t, land in enumerate(lands):
        for n, k in enumerate(OTHER_CHIPS):
            _, slot = _peer(k)
            s = t * len(OTHER_CHIPS) + n
            copies.append(pltpu.make_async_remote_copy(
                src_ref=land.at[slot], dst_ref=land.at[slot], send_sem=send_sems[s], recv_sem=recv_sems[s],
                device_id=sibling, device_id_type=MESH))
    return copies


def _forward_start(lands, after, name):
    nt = len(lands)
    ns = nt * len(OTHER_CHIPS)

    def body(*refs):
        ins, outs = refs[:nt + 1], refs[nt + 1:]
        for cp in _forward_copies(ins[:nt], outs[:ns], outs[ns:2 * ns]):
            cp.start()
        outs[-1][...] = jnp.zeros_like(outs[-1])

    res = _pc(
        body, name=name,
        out_shape=(*[pltpu.SemaphoreType.DMA(())] * (2 * ns), *[pltpu.HBM(a.shape, a.dtype) for a in lands],
                   jax.ShapeDtypeStruct((8, LANES), F32)),
        in_specs=[HBM] * nt + [ANY],
        out_specs=(*[SEM] * (2 * ns), *[HBM] * nt, pl.BlockSpec(memory_space=pltpu.VMEM)),
        input_output_aliases={i: 2 * ns + i for i in range(nt)},
        compiler_params=pltpu.CompilerParams(has_side_effects=EFFECT),
    )(*lands, after)
    return list(res[:ns]), list(res[ns:2 * ns]), list(res[2 * ns:2 * ns + nt]), res[-1]


def _forward_wait(state, after, name):
    send_sems, recv_sems, lands, _ = state
    nt, ns = len(lands), len(send_sems)

    def body(*refs):
        sems = refs[nt:nt + 2 * ns]
        for cp in _forward_copies(refs[:nt], sems[:ns], sems[ns:]):
            cp.wait_send()
            cp.wait_recv()

    res = _pc(
        body, name=name,
        out_shape=tuple(pltpu.HBM(a.shape, a.dtype) for a in lands),
        in_specs=[HBM] * nt + [SEM] * (2 * ns) + [ANY],
        out_specs=tuple([HBM] * nt),
        input_output_aliases={i: i for i in range(nt)},
        compiler_params=pltpu.CompilerParams(has_side_effects=EFFECT),
    )(*lands, *send_sems, *recv_sems, after)
    return list(res)


def _exchange_start(srcs, per_dest, after, name, peers=ALL_PEERS):
    nt = len(srcs)
    ns = nt * len(peers)
    lands = [lax.empty((N_DEV,) + (s.shape[1:] if _flag(per_dest, t) else s.shape), s.dtype) for t, s in enumerate(srcs)]

    def body(*refs):
        ins, outs = refs[:2 * nt + 1], refs[2 * nt + 1:]
        for cp in _exchange_copies(ins[:nt], ins[nt:2 * nt], outs[:ns], outs[ns:2 * ns], per_dest, peers):
            cp.start()
        outs[-1][...] = jnp.zeros_like(outs[-1])

    hbm = [pltpu.with_memory_space_constraint(a, pltpu.HBM) for a in list(srcs) + lands]
    res = _pc(
        body, name=name,
        out_shape=(*[pltpu.SemaphoreType.DMA(())] * (2 * ns), *[pltpu.HBM(a.shape, a.dtype) for a in hbm],
                   jax.ShapeDtypeStruct((8, LANES), F32)),
        in_specs=[HBM] * (2 * nt) + [ANY],
        out_specs=(*[SEM] * (2 * ns), *[HBM] * (2 * nt), pl.BlockSpec(memory_space=pltpu.VMEM)),
        input_output_aliases={i: 2 * ns + i for i in range(2 * nt)},
        compiler_params=pltpu.CompilerParams(has_side_effects=EFFECT),
    )(*hbm, after)
    sems, rest = res[:2 * ns], res[2 * ns:]
    return list(sems[:ns]), list(sems[ns:]), list(rest[:nt]), list(rest[nt:2 * nt]), rest[-1]


def _exchange_wait(state, per_dest, after, name, which=None, peers=ALL_PEERS):
    send_sems, recv_sems, srcs, lands, _ = state
    which = list(range(len(srcs))) if which is None else which
    per_dest = [_flag(per_dest, t) for t in which]
    pick = [t * len(peers) + n for t in which for n in range(len(peers))]
    send_sems, recv_sems = [send_sems[s] for s in pick], [recv_sems[s] for s in pick]
    srcs, lands = [srcs[t] for t in which], [lands[t] for t in which]
    nt = len(srcs)
    ns = len(send_sems)

    def body(*refs):
        sems = refs[2 * nt:2 * nt + 2 * ns]
        for cp in _exchange_copies(refs[:nt], refs[nt:2 * nt], sems[:ns], sems[ns:], per_dest, peers):
            cp.wait_send()
            cp.wait_recv()

    thru = list(srcs) + list(lands)
    res = _pc(
        body, name=name,
        out_shape=tuple(pltpu.HBM(a.shape, a.dtype) for a in thru),
        in_specs=[HBM] * (2 * nt) + [SEM] * (2 * ns) + [ANY],
        out_specs=tuple([HBM] * (2 * nt)),
        input_output_aliases={i: i for i in range(2 * nt)},
        compiler_params=pltpu.CompilerParams(has_side_effects=EFFECT),
    )(*thru, *send_sems, *recv_sems, after)
    return list(res[nt:])


def _mod_fwd(cond, ada_w, bias):
    depth, d, nb = ada_w.shape

    def body(c_ref, w_ref, b_ref, o_ref):
        s = _silu(c_ref[...]).astype(BF16)
        o_ref[...] = _dot(s, w_ref[...].astype(BF16)) + b_ref[...]

    return _pc(
        body, name="mod_fwd", grid=(depth,),
        in_specs=[pl.BlockSpec((16, d), lambda i: (0, 0)), pl.BlockSpec((None, d, nb), lambda i: (i, 0, 0)),
                  pl.BlockSpec((None, 1, nb), lambda i: (i, 0, 0))],
        out_specs=pl.BlockSpec((None, 16, nb), lambda i: (i, 0, 0)),
        out_shape=jax.ShapeDtypeStruct((depth, 16, nb), F32),
        compiler_params=_params("parallel"),
    )(cond, ada_w, bias.reshape(depth, 1, nb))


def _mod_bwd(cond, ada_w, dm_all, dm_mine):
    depth, d, nb = ada_w.shape
    d3 = dm_all.shape[-1]

    def body(c_ref, w_ref, all_ref, call_ref, mine_ref, cmine_ref, gw_ref, gb_ref, part_ref, ds_ref):
        i = pl.program_id(0)
        cond_v = c_ref[...]
        s = _silu(cond_v).astype(BF16)
        has_ctx = jnp.where(i < 2, 1.0, 0.0)
        tot_all = jnp.sum(call_ref[...], axis=0, keepdims=True) * has_ctx
        tot_mine = jnp.broadcast_to(jnp.sum(cmine_ref[...], axis=0, keepdims=True) * has_ctx, (8, nb)).astype(BF16)
        gb_ref[...] = jnp.sum(all_ref[...], axis=0, keepdims=True) + tot_all
        gw_ref[...] = _dot(s[0:8], mine_ref[...].astype(BF16), 0, 0) + _dot(s[8:16], tot_mine, 0, 0)
        part = _dot(tot_mine, w_ref[...].astype(BF16), 1, 1)

        @pl.when(i == 0)
        def _():
            part_ref[...] = jnp.zeros_like(part_ref)
            ds_ref[...] = _dsilu(cond_v)

        part_ref[...] += part

    def rows(width, which):
        return pl.BlockSpec((None, N_DEV, width), which)

    layer = lambda i: (i, 0, 0)
    ctx_layer = lambda i: (jnp.minimum(i, 1) + 4, 0, 0)
    return _pc(
        body, name="mod_bwd", grid=(depth,),
        in_specs=[pl.BlockSpec((16, d), lambda i: (0, 0)), pl.BlockSpec((None, d, nb), layer),
                  rows(d3, layer), rows(d3, ctx_layer), rows(nb, layer), rows(nb, ctx_layer)],
        out_specs=[pl.BlockSpec((None, d, nb), layer), pl.BlockSpec((None, 1, d3), layer),
                   pl.BlockSpec((8, d), lambda i: (0, 0)), pl.BlockSpec((16, d), lambda i: (0, 0))],
        out_shape=[jax.ShapeDtypeStruct((depth, d, nb), F32), jax.ShapeDtypeStruct((depth, 1, d3), F32),
                   jax.ShapeDtypeStruct((8, d), F32), jax.ShapeDtypeStruct((16, d), F32)],
        compiler_params=_params("arbitrary"),
    )(cond, ada_w, dm_all, dm_all, dm_mine, dm_mine)


def _norm_fwd(xs, g, mod, tr, seg_tiles, name):
    t, d = xs.shape

    def body(x_ref, g_ref, mod_ref, h_ref):
        x = x_ref[...]
        r = lax.rsqrt(jnp.mean(x * x, axis=-1, keepdims=True) + RMS_EPS)
        y = (x * r) * g_ref[...]
        h_ref[...] = (y * (1.0 + mod_ref[1:2, :]) + mod_ref[0:1, :]).astype(BF16)

    return _pc(
        body, name=name, grid=(t // tr,),
        in_specs=[pl.BlockSpec((tr, d), lambda i: (i, 0)), pl.BlockSpec((1, d), lambda i: (0, 0)),
                  pl.BlockSpec((None, 8, d), lambda i: (i // seg_tiles, 0, 0))],
        out_specs=pl.BlockSpec((tr, d), lambda i: (i, 0)),
        out_shape=jax.ShapeDtypeStruct((t, d), BF16),
        compiler_params=_params("parallel"),
    )(xs, g, mod)


def _resid_grad(dx, i, seg_tiles, yx_ref, gate_ref, dyx_ref, gsum_ref):
    dyx_ref[...] = (dx * gate_ref[2:3, :]).astype(BF16)

    @pl.when(i % seg_tiles == 0)
    def _():
        gsum_ref[...] = jnp.zeros_like(gsum_ref)

    gsum_ref[0:1, :] += jnp.sum(dx * yx_ref[...], axis=0, keepdims=True)


def _norm_bwd(xs, dh, dres, g, mod, tr, seg_tiles, name, res_tiles=None, out_tiles=None, below=None):
    t, d = xs.shape
    n_tiles = t // tr
    res_tiles = n_tiles if res_tiles is None else res_tiles
    out_tiles = n_tiles if out_tiles is None else out_tiles

    def body(x_ref, dh_ref, dres_ref, g_ref, mod_ref, *rest):
        i = pl.program_id(0)
        x = x_ref[...]
        r = lax.rsqrt(jnp.mean(x * x, axis=-1, keepdims=True) + RMS_EPS)
        xn = x * r
        dhv = dh_ref[...]
        gain = g_ref[...]
        one_scale = 1.0 + mod_ref[1:2, :]
        dxn = dhv * (gain * one_scale)
        dx = r * (dxn - xn * jnp.mean(dxn * xn, axis=-1, keepdims=True))
        if res_tiles == n_tiles:
            dx = dx + dres_ref[...]
        else:
            dx = dx + jnp.where(i < res_tiles, dres_ref[...], 0.0)
        if below is None:
            dx_ref, sum_ref = rest
        else:
            yx_ref, gate_ref, dx_ref, sum_ref, dyx_ref, gsum_ref = rest
            _resid_grad(dx, i, seg_tiles, yx_ref, gate_ref, dyx_ref, gsum_ref)
        if out_tiles == n_tiles:
            dx_ref[...] = dx
        else:
            @pl.when(i < out_tiles)
            def _():
                dx_ref[...] = dx

        @pl.when(i % seg_tiles == 0)
        def _():
            sum_ref[...] = jnp.zeros_like(sum_ref)

        sum_ref[0:1, :] += jnp.sum(dhv, axis=0, keepdims=True)
        sum_ref[1:2, :] += jnp.sum(dhv * (xn * gain), axis=0, keepdims=True)
        sum_ref[2:3, :] += jnp.sum(dhv * one_scale * xn, axis=0, keepdims=True)

    row = pl.BlockSpec((tr, d), lambda i: (i, 0))
    seg = pl.BlockSpec((None, 8, d), lambda i: (i // seg_tiles, 0, 0))
    in_specs = [row, row, pl.BlockSpec((tr, d), lambda i: (jnp.minimum(i, res_tiles - 1), 0)),
                pl.BlockSpec((1, d), lambda i: (0, 0)), seg]
    out_specs = [pl.BlockSpec((tr, d), lambda i: (jnp.minimum(i, out_tiles - 1), 0)), seg]
    out_shape = [jax.ShapeDtypeStruct((out_tiles * tr, d), F32), jax.ShapeDtypeStruct((mod.shape[0], 8, d), F32)]
    args = [xs, dh, dres, g, mod]
    if below is not None:
        in_specs += [row, seg]
        out_specs += [row, seg]
        out_shape += [jax.ShapeDtypeStruct((t, d), BF16), jax.ShapeDtypeStruct((below[1].shape[0], 8, d), F32)]
        args += list(below)
    return _pc(
        body, name=name, grid=(n_tiles,), in_specs=in_specs, out_specs=out_specs, out_shape=out_shape,
        compiler_params=_params("arbitrary"),
    )(*args)


def _loss_head(xs, target, g, yx, mod, tr):
    t, d = xs.shape

    def body(x_ref, t_ref, g_ref, yx_ref, gate_ref, loss_ref, dx_ref, dg_ref, dyx_ref, gsum_ref):
        i = pl.program_id(0)
        x = x_ref[...]
        r = lax.rsqrt(jnp.mean(x * x, axis=-1, keepdims=True) + RMS_EPS)
        xn = x * r
        gain = g_ref[...]
        err = xn * gain - t_ref[...]
        dy = err * (1.0 / d)
        dxn = dy * gain
        dx = r * (dxn - xn * jnp.mean(dxn * xn, axis=-1, keepdims=True))
        dx_ref[...] = dx
        _resid_grad(dx, i, t // tr, yx_ref, gate_ref, dyx_ref, gsum_ref)

        @pl.when(i == 0)
        def _():
            loss_ref[...] = jnp.zeros_like(loss_ref)
            dg_ref[...] = jnp.zeros_like(dg_ref)

        loss_ref[...] += 0.5 * jnp.sum(jnp.mean(err * err, axis=-1, keepdims=True))
        dg_ref[0:1, :] += jnp.sum(dy * xn, axis=0, keepdims=True)

    row = pl.BlockSpec((tr, d), lambda i: (i, 0))
    seg = pl.BlockSpec((None, 8, d), lambda i: (0, 0, 0))
    return _pc(
        body, name="loss_head", grid=(t // tr,),
        in_specs=[row, row, pl.BlockSpec((1, d), lambda i: (0, 0)), row, seg],
        out_specs=[pl.BlockSpec((8, LANES), lambda i: (0, 0)), row, pl.BlockSpec((8, d), lambda i: (0, 0)), row, seg],
        out_shape=[jax.ShapeDtypeStruct((8, LANES), F32), jax.ShapeDtypeStruct((t, d), F32),
                   jax.ShapeDtypeStruct((8, d), F32), jax.ShapeDtypeStruct((t, d), BF16),
                   jax.ShapeDtypeStruct((1, 8, d), F32)],
        compiler_params=_params("arbitrary"),
    )(xs, target, g, yx, mod)


def _proj_in(h, w, layer, width, name, blocks=None, dtype=F32):
    t, d = h.shape
    n8 = w.shape[-1]
    first, count = blocks if blocks is not None else (0, N_DEV)
    per_part = width // n8
    tm = _tile(t, 1152)

    def body(a_ref, b_ref, o_ref):
        o_ref[...] = _dot(a_ref[...], b_ref[...]).astype(dtype)

    return _pc(
        body, name=name, grid=(t // tm, count),
        in_specs=[pl.BlockSpec((tm, d), lambda i, j: (i, 0)),
                  pl.BlockSpec((None, None, d, n8), lambda i, j: (first + j, layer, 0, 0))],
        out_specs=pl.BlockSpec((None, tm, n8), lambda i, j: (j // per_part, i, j % per_part)),
        out_shape=jax.ShapeDtypeStruct((count // per_part, t, width), dtype),
        compiler_params=_params("parallel", "parallel"),
    )(h, w)


def _proj_out(z, w, res, mod, tm, seg_tiles, name, nxt=None):
    t, k = z.shape
    d = w.shape[1]

    def body(z_ref, w_ref, res_ref, mod_ref, *rest):
        yx = _dot(z_ref[...], w_ref[...])
        x = res_ref[...] + mod_ref[2:3, :] * yx
        if nxt is None:
            yx_ref, x_ref = rest
        else:
            g_ref, nmod_ref, yx_ref, x_ref, h_ref = rest
            r = lax.rsqrt(jnp.mean(x * x, axis=-1, keepdims=True) + RMS_EPS)
            h_ref[...] = (((x * r) * g_ref[...]) * (1.0 + nmod_ref[1:2, :]) + nmod_ref[0:1, :]).astype(BF16)
        yx_ref[...] = yx
        x_ref[...] = x

    tile = pl.BlockSpec((tm, d), lambda i: (i, 0))
    seg = pl.BlockSpec((None, 8, d), lambda i: (i // seg_tiles, 0, 0))
    in_specs = [pl.BlockSpec((tm, k), lambda i: (i, 0)), pl.BlockSpec((k, d), lambda i: (0, 0)), tile, seg]
    out_specs = [tile, tile]
    out_shape = [jax.ShapeDtypeStruct((t, d), F32), jax.ShapeDtypeStruct((t, d), F32)]
    args = [z, w, res, mod]
    if nxt is not None:
        in_specs += [pl.BlockSpec((1, d), lambda i: (0, 0)), seg]
        out_specs.append(tile)
        out_shape.append(jax.ShapeDtypeStruct((t, d), BF16))
        args += list(nxt)
    return _pc(
        body, name=name, grid=(t // tm,), in_specs=in_specs, out_specs=out_specs, out_shape=out_shape,
        compiler_params=_params("parallel"),
    )(*args)


def _proj_out_dz(dyx, w, name):
    t, d = dyx.shape
    width = w.shape[0]
    tm, tn = _tile(t, 1024), _tile(width, 512)

    def body(a_ref, w_ref, o_ref):
        o_ref[...] = _dot(a_ref[...], w_ref[...], 1, 1)

    return _pc(
        body, name=name, grid=(t // tm, width // tn),
        in_specs=[pl.BlockSpec((tm, d), lambda i, j: (i, 0)), pl.BlockSpec((tn, d), lambda i, j: (j, 0))],
        out_specs=pl.BlockSpec((tm, tn), lambda i, j: (i, j)),
        out_shape=jax.ShapeDtypeStruct((t, width), F32),
        compiler_params=_params("parallel", "parallel"),
    )(dyx, w)


def _proj_in_dh(dpre, w, layer, name, after=None):
    parts, t, width = dpre.shape
    d, n8 = w.shape[-2:]
    per_part = width // n8
    tm, tn = _tile(t, 1152), _tile(d, 512)

    def body(a_ref, w_ref, *rest):
        o_ref = rest[-1]
        part = _dot(a_ref[:, 0:n8], w_ref[0], 1, 1)
        for s in range(1, per_part):
            part += _dot(a_ref[:, s * n8:(s + 1) * n8], w_ref[s], 1, 1)

        @pl.when(pl.program_id(2) == 0)
        def _():
            o_ref[...] = part

        @pl.when(pl.program_id(2) != 0)
        def _():
            o_ref[...] += part

    extra = [] if after is None else [after]
    return _pc(
        body, name=name, grid=(t // tm, d // tn, parts),
        in_specs=[pl.BlockSpec((None, tm, width), lambda i, j, k: (k, i, 0)),
                  pl.BlockSpec((per_part, None, tn, n8), lambda i, j, k: (k, layer, j, 0))] + [ANY] * len(extra),
        out_specs=pl.BlockSpec((tm, tn), lambda i, j, k: (i, j)),
        out_shape=jax.ShapeDtypeStruct((t, d), F32),
        compiler_params=_params("parallel", "parallel", "arbitrary"),
    )(dpre, w, *extra)


def _transposed(a_ref):
    return a_ref[...].T


def _grad_w_in(h, dpre, n8, name):
    t, d = h.shape
    parts, _, width = dpre.shape
    per_part = width // n8
    tm, tk = _tile(d, 512), _tile(t, 1152)
    nk = t // tk

    def body(a_ref, b_ref, o_ref, acc_ref):
        k = pl.program_id(1)

        @pl.when(k == 0)
        def _():
            acc_ref[...] = jnp.zeros_like(acc_ref)

        at = _transposed(a_ref)
        for p in range(parts):
            r = _dot(at, b_ref[p])
            for s in range(per_part):
                acc_ref[p * per_part + s] += r[:, s * n8:(s + 1) * n8]

        @pl.when(k == nk - 1)
        def _():
            o_ref[...] = acc_ref[...].astype(BF16)

    return _pc(
        body, name=name, grid=(d // tm, nk),
        in_specs=[pl.BlockSpec((tk, tm), lambda i, k: (k, i)), pl.BlockSpec((parts, tk, width), lambda i, k: (0, k, 0))],
        out_specs=pl.BlockSpec((parts * per_part, tm, n8), lambda i, k: (0, i, 0)),
        out_shape=jax.ShapeDtypeStruct((parts * per_part, d, n8), BF16),
        scratch_shapes=[pltpu.VMEM((parts * per_part, tm, n8), F32)],
        compiler_params=_params("parallel", "arbitrary"),
    )(h, dpre)


def _grad_w_out(z, dyx, name):
    width = z.shape[1]
    t, d = dyx.shape
    tm, tk = _tile(width, 512), _tile(t, 1152)
    nk = t // tk

    def body(a_ref, b_ref, o_ref, acc_ref):
        k = pl.program_id(1)

        @pl.when(k == 0)
        def _():
            acc_ref[...] = jnp.zeros_like(acc_ref)

        acc_ref[...] += _dot(_transposed(a_ref), b_ref[...])

        @pl.when(k == nk - 1)
        def _():
            o_ref[...] = acc_ref[...].astype(BF16)

    return _pc(
        body, name=name, grid=(width // tm, nk),
        in_specs=[pl.BlockSpec((tk, tm), lambda i, k: (k, i)), pl.BlockSpec((tk, d), lambda i, k: (k, 0))],
        out_specs=pl.BlockSpec((tm, d), lambda i, k: (i, 0)),
        out_shape=jax.ShapeDtypeStruct((width, d), BF16),
        scratch_shapes=[pltpu.VMEM((tm, d), F32)],
        compiler_params=_params("parallel", "arbitrary"),
    )(z, dyx)


def _shift(v, k):
    n = v.shape[0]
    return pltpu.roll(v, k % n, 0)


def _window_sum(v, win):
    s = v + _shift(v, 1)
    step = 1
    while 2 * step < win:
        s = _shift(s, step) + _shift(s, -step)
        step *= 2
    return s


def _window_count(base, seg_len, win, shape):
    t = base + lax.broadcasted_iota(jnp.int32, shape, 0)
    hi = jnp.minimum(t + win // 2, seg_len)
    lo = jnp.maximum(t - win // 2, 0)
    return (hi - lo).astype(F32)


def _pad_offsets(segs):
    return [HALO * (s + 1) + st for s, (st, _) in enumerate(segs)]


def _for_chunks(segs, fn):
    offs = _pad_offsets(segs)
    for s, (st, ln) in enumerate(segs):
        def step(ci, carry, s=s, st=st, ln=ln):
            fn(s, st, ln, offs[s], pl.multiple_of(ci * CHUNK, CHUNK))
            return carry
        lax.fori_loop(0, ln // CHUNK, step, 0)


def _pool_fwd(pre, w_grp, scale, segs, name):
    _, t, width = pre.shape
    grp = width // len(POOL_WINDOWS)
    padded = t + HALO * (len(segs) + 1)

    def group(win, pre_ref, w_ref, sc_ref, z_ref, diff_ref, pad_ref):
        pad_ref[...] = jnp.zeros_like(pad_ref)

        def fill(s, st, ln, off, b):
            pad_ref[pl.ds(off + b, CHUNK), :] = pre_ref[0, pl.ds(st + b, CHUNK), :]

        _for_chunks(segs, fill)

        def mix(s, st, ln, off, b):
            ext = pad_ref[pl.ds(off - HALO + b, CHUNK + 2 * HALO), :]
            total = _window_sum(ext, win)[HALO:HALO + CHUNK]
            u = pre_ref[0, pl.ds(st + b, CHUNK), :]
            diff = (total / _window_count(b, ln, win, u.shape) - u).astype(BF16)
            mixed = _dot(diff, w_ref[...])
            gate = _silu(pre_ref[1, pl.ds(st + b, CHUNK), :])
            z_ref[pl.ds(st + b, CHUNK), :] = (mixed * sc_ref[...] * gate).astype(BF16)
            diff_ref[pl.ds(st + b, CHUNK), :] = diff

        _for_chunks(segs, mix)

    def body(pre_ref, w_ref, sc_ref, z_ref, diff_ref, pad_ref):
        gi = pl.program_id(0)
        for widx, win in enumerate(POOL_WINDOWS):
            @pl.when(gi == widx)
            def _(win=win):
                group(win, pre_ref, w_ref, sc_ref, z_ref, diff_ref, pad_ref)

    col = pl.BlockSpec((t, grp), lambda g: (0, g))
    return _pc(
        body, name=name, grid=(len(POOL_WINDOWS),),
        in_specs=[pl.BlockSpec((2, t, grp), lambda g: (0, 0, g)), pl.BlockSpec((None, grp, grp), lambda g: (g, 0, 0)),
                  pl.BlockSpec((1, grp), lambda g: (0, g))],
        out_specs=[col, col],
        out_shape=[jax.ShapeDtypeStruct((t, width), BF16), jax.ShapeDtypeStruct((t, width), BF16)],
        scratch_shapes=[pltpu.VMEM((padded, grp), F32)],
        compiler_params=_params("parallel"),
    )(pre, w_grp, scale)


def _pool_bwd(dz, diff, pre, w_grp, scale, segs, name):
    _, t, width = pre.shape
    grp = width // len(POOL_WINDOWS)
    padded = t + HALO * (len(segs) + 1)

    def group(win, dz_ref, diff_ref, pre_ref, w_ref, sc_ref, dpre_ref, dw_ref, dsc_ref, pad_ref, dd_ref):
        pad_ref[...] = jnp.zeros_like(pad_ref)
        dw_ref[...] = jnp.zeros_like(dw_ref)
        dsc_ref[...] = jnp.zeros_like(dsc_ref)

        def first(s, st, ln, off, b):
            rows = pl.ds(st + b, CHUNK)
            diff_v = diff_ref[rows, :]
            mixed = _dot(diff_v, w_ref[...])
            g = pre_ref[1, rows, :]
            sg = _silu(g)
            dzv = dz_ref[rows, :]
            dmixed = (dzv * sc_ref[...] * sg).astype(BF16)
            dsc_ref[...] += jnp.sum(dzv * mixed * sg, axis=0, keepdims=True)
            dpre_ref[1, rows, :] = (dzv * mixed * sc_ref[...] * _dsilu(g)).astype(BF16)
            ddiff = _dot(dmixed, w_ref[...], 1, 1)
            dw_ref[...] += _dot(diff_v, dmixed, 0, 0)
            dd_ref[rows, :] = ddiff
            pad_ref[pl.ds(off + b, CHUNK), :] = ddiff / _window_count(b, ln, win, ddiff.shape)

        _for_chunks(segs, first)

        def second(s, st, ln, off, b):
            rows = pl.ds(st + b, CHUNK)
            ext = pad_ref[pl.ds(off - HALO + b, CHUNK + 2 * HALO), :]
            total = _shift(_window_sum(ext, win), -1)[HALO:HALO + CHUNK]
            dpre_ref[0, rows, :] = (total - dd_ref[rows, :]).astype(BF16)

        _for_chunks(segs, second)

    def body(dz_ref, diff_ref, pre_ref, w_ref, sc_ref, dpre_ref, dw_ref, dsc_ref, pad_ref, dd_ref):
        gi = pl.program_id(0)
        for widx, win in enumerate(POOL_WINDOWS):
            @pl.when(gi == widx)
            def _(win=win):
                group(win, dz_ref, diff_ref, pre_ref, w_ref, sc_ref, dpre_ref, dw_ref, dsc_ref, pad_ref, dd_ref)

    col = pl.BlockSpec((t, grp), lambda g: (0, g))
    both = pl.BlockSpec((2, t, grp), lambda g: (0, 0, g))
    wspec = pl.BlockSpec((None, grp, grp), lambda g: (g, 0, 0))
    sspec = pl.BlockSpec((1, grp), lambda g: (0, g))
    return _pc(
        body, name=name, grid=(len(POOL_WINDOWS),),
        in_specs=[col, col, both, wspec, sspec],
        out_specs=[both, wspec, sspec],
        out_shape=[jax.ShapeDtypeStruct((2, t, width), BF16), jax.ShapeDtypeStruct((len(POOL_WINDOWS), grp, grp), F32),
                   jax.ShapeDtypeStruct((1, width), F32)],
        scratch_shapes=[pltpu.VMEM((padded, grp), F32), pltpu.VMEM((t, grp), F32)],
        compiler_params=_params("parallel"),
    )(dz, diff, pre, w_grp, scale)


def _conv_fwd(pre, dw, db, name):
    _, t, width = pre.shape
    cb = LANES
    segs = [(0, t)]

    def body(pre_ref, dw_ref, db_ref, z_ref, pad_ref):
        pad_ref[...] = jnp.zeros_like(pad_ref)

        def fill(s, st, ln, off, b):
            rows = pl.ds(b, CHUNK)
            pad_ref[pl.ds(off + b, CHUNK), :] = pre_ref[1, rows, :] * pre_ref[2, rows, :]

        _for_chunks(segs, fill)

        def mix(s, st, ln, off, b):
            rows = pl.ds(b, CHUNK)
            ext = pad_ref[pl.ds(off - HALO + b, CHUNK + 2 * HALO), :]
            conv = (dw_ref[0:1, :] * _shift(ext, 1) + dw_ref[1:2, :] * ext + dw_ref[2:3, :] * _shift(ext, -1))
            conv = conv[HALO:HALO + CHUNK] + db_ref[...]
            y = pre_ref[0, rows, :] * conv
            z_ref[rows, :] = (y * _silu(pre_ref[3, rows, :])).astype(BF16)

        _for_chunks(segs, mix)

    return _pc(
        body, name=name, grid=(width // cb,),
        in_specs=[pl.BlockSpec((4, t, cb), lambda j: (0, 0, j)), pl.BlockSpec((8, cb), lambda j: (0, j)),
                  pl.BlockSpec((1, cb), lambda j: (0, j))],
        out_specs=pl.BlockSpec((t, cb), lambda j: (0, j)),
        out_shape=jax.ShapeDtypeStruct((t, width), BF16),
        scratch_shapes=[pltpu.VMEM((t + 2 * HALO, cb), F32)],
        compiler_params=_params("parallel"),
    )(pre, dw, db)


def _conv_bwd(dz, pre, dw, db, name):
    _, t, width = pre.shape
    cb = LANES
    segs = [(0, t)]

    def body(dz_ref, pre_ref, dw_ref, db_ref, dpre_ref, ddw_ref, ddb_ref, pad_a, pad_c):
        pad_a[...] = jnp.zeros_like(pad_a)
        pad_c[...] = jnp.zeros_like(pad_c)
        ddw_ref[...] = jnp.zeros_like(ddw_ref)
        ddb_ref[...] = jnp.zeros_like(ddb_ref)

        def fill(s, st, ln, off, b):
            rows = pl.ds(b, CHUNK)
            pad_a[pl.ds(off + b, CHUNK), :] = pre_ref[1, rows, :] * pre_ref[2, rows, :]

        _for_chunks(segs, fill)

        def first(s, st, ln, off, b):
            rows = pl.ds(b, CHUNK)
            ext = pad_a[pl.ds(off - HALO + b, CHUNK + 2 * HALO), :]
            prev, nxt = _shift(ext, 1)[HALO:HALO + CHUNK], _shift(ext, -1)[HALO:HALO + CHUNK]
            here = ext[HALO:HALO + CHUNK]
            conv = dw_ref[0:1, :] * prev + dw_ref[1:2, :] * here + dw_ref[2:3, :] * nxt + db_ref[...]
            bg, g = pre_ref[0, rows, :], pre_ref[3, rows, :]
            dzv = dz_ref[rows, :]
            dy = dzv * _silu(g)
            dpre_ref[3, rows, :] = (dzv * (bg * conv) * _dsilu(g)).astype(BF16)
            dpre_ref[0, rows, :] = (dy * conv).astype(BF16)
            dconv = dy * bg
            pad_c[pl.ds(off + b, CHUNK), :] = dconv
            ddw_ref[0:1, :] += jnp.sum(dconv * prev, axis=0, keepdims=True)
            ddw_ref[1:2, :] += jnp.sum(dconv * here, axis=0, keepdims=True)
            ddw_ref[2:3, :] += jnp.sum(dconv * nxt, axis=0, keepdims=True)
            ddb_ref[0:1, :] += jnp.sum(dconv, axis=0, keepdims=True)

        _for_chunks(segs, first)

        def second(s, st, ln, off, b):
            rows = pl.ds(b, CHUNK)
            ext = pad_c[pl.ds(off - HALO + b, CHUNK + 2 * HALO), :]
            da = (dw_ref[0:1, :] * _shift(ext, -1) + dw_ref[1:2, :] * ext + dw_ref[2:3, :] * _shift(ext, 1))
            da = da[HALO:HALO + CHUNK]
            dpre_ref[1, rows, :] = (da * pre_ref[2, rows, :]).astype(BF16)
            dpre_ref[2, rows, :] = (da * pre_ref[1, rows, :]).astype(BF16)

        _for_chunks(segs, second)

    quad = pl.BlockSpec((4, t, cb), lambda j: (0, 0, j))
    rows8 = pl.BlockSpec((8, cb), lambda j: (0, j))
    return _pc(
        body, name=name, grid=(width // cb,),
        in_specs=[pl.BlockSpec((t, cb), lambda j: (0, j)), quad, rows8, pl.BlockSpec((1, cb), lambda j: (0, j))],
        out_specs=[quad, rows8, rows8],
        out_shape=[jax.ShapeDtypeStruct((4, t, width), BF16), jax.ShapeDtypeStruct((8, width), F32),
                   jax.ShapeDtypeStruct((8, width), F32)],
        scratch_shapes=[pltpu.VMEM((t + 2 * HALO, cb), F32), pltpu.VMEM((t + 2 * HALO, cb), F32)],
        compiler_params=_params("parallel"),
    )(dz, pre, dw, db)


PAIR_TILES = 2 * WIN_ROWS - 2


def _pair_geometry():
    lane = lax.broadcasted_iota(jnp.int32, (GRID_W, LANES), 1)
    qcol = lax.broadcasted_iota(jnp.int32, (GRID_W, LANES), 0)
    low = lane < GRID_W
    kcol = jnp.where(low, lane, lane - GRID_W)
    start = jnp.clip(qcol - WIN_COLS // 2, 0, GRID_W - WIN_COLS)
    inside = (kcol >= start) & (kcol < start + WIN_COLS)
    return low, inside


def _bias_tiles(rpb_ref, rows_ref, tiles_ref, inside):
    for h in range(2):
        rows = rpb_ref[h]
        rows_ref[h] = (pltpu.roll(rows, LANES - (WIN_COLS - 1), 1)
                       + pltpu.roll(pltpu.roll(rows, GRID_W - (WIN_COLS - 1), 1), 2 * WIN_ROWS - 1, 0))
        for t in range(PAIR_TILES):
            both = jnp.broadcast_to(rows_ref[h, t:t + 1, :], (GRID_W, LANES))
            tiles_ref[h, t] = jnp.where(inside, pltpu.roll(both, 0, 1, stride=1, stride_axis=0), MASKED)


def _bias_tiles_grad(dtiles_ref, drpb_ref):
    n = PAIR_TILES * GRID_W
    qcol = lax.broadcasted_iota(jnp.int32, (n, LANES), 0) & (GRID_W - 1)
    lane = lax.broadcasted_iota(jnp.int32, (1, LANES), 1)
    zero = jnp.zeros((1, LANES), F32)
    for h in range(2):
        v = pltpu.roll(dtiles_ref[h].reshape(n, LANES), WIN_COLS - 1, 1)
        for bit in range(6):
            v = jnp.where((qcol >> bit) & 1 == 1, pltpu.roll(v, LANES - (1 << bit), 1), v)
        sums = [jnp.sum(v[t * GRID_W:(t + 1) * GRID_W], axis=0, keepdims=True) for t in range(PAIR_TILES)]
        for r in range(2 * WIN_ROWS):
            here = sums[r] if r < PAIR_TILES else zero
            prev = pltpu.roll(sums[r - 1], GRID_W, 1) if 1 <= r <= PAIR_TILES else zero
            drpb_ref[h, r:r + 1, :] = jnp.where(lane < 2 * WIN_COLS - 1, here + prev, 0.0)


def _attn_rows(r, n_rows):
    first = jnp.clip(r - WIN_ROWS // 2, 0, n_rows - WIN_ROWS)
    return first, first - r + WIN_ROWS - 1


def _softmax(s_loc, s_ctx):
    m = jnp.maximum(jnp.max(s_loc, axis=-1, keepdims=True), jnp.max(s_ctx, axis=-1, keepdims=True))
    e_loc, e_ctx = jnp.exp(s_loc - m), jnp.exp(s_ctx - m)
    inv = 1.0 / (jnp.sum(e_loc, axis=-1, keepdims=True) + jnp.sum(e_ctx, axis=-1, keepdims=True))
    return e_loc * inv, e_ctx * inv


def _pair_bias(tiles_ref, j):
    return jnp.concatenate(
        [jnp.concatenate([tiles_ref[h, j + 2 * m] for m in range(WIN_ROWS // 2)], axis=1) for h in range(2)], axis=0)


ROWS_PER_STEP = 4


def _by_head(tile, low):
    zero = jnp.zeros_like(tile)
    return jnp.concatenate([jnp.where(low, tile, zero), jnp.where(low, zero, tile)], axis=0)


def _merge_heads(stacked, low):
    return jnp.where(low, stacked[:GRID_W], stacked[GRID_W:])


def _attn_items(step, n_rows, q_ref, low):
    items = []
    for u in range(ROWS_PER_STEP):
        r = step * ROWS_PER_STEP + u
        first, j = _attn_rows(r, n_rows)
        rows = pl.ds(pl.multiple_of(r * GRID_W, GRID_W), GRID_W)
        keys = pl.ds(pl.multiple_of(first * GRID_W, GRID_W), WIN_ROWS * GRID_W)
        q = (q_ref[rows, :].astype(F32) * HEAD_DIM ** -0.5).astype(BF16)
        items.append((rows, keys, j, _by_head(q, low)))
    return items


def _attn_fwd(qkv, gate, rpb, seq):
    _, t, width = qkv.shape
    n_rows = seq // GRID_W
    n_ctx = t - seq
    blk = WIN_ROWS * GRID_W

    def body(q_ref, k_ref, v_ref, g_ref, rpb_ref, z_ref, o_ref, rows_ref, tiles_ref):
        low, inside = _pair_geometry()
        _bias_tiles(rpb_ref, rows_ref, tiles_ref, inside)
        ctx = pl.ds(seq, n_ctx)

        def step(i, carry):
            items = _attn_items(i, n_rows, q_ref, low)
            k_ctx, v_ctx = k_ref[ctx, :], v_ref[ctx, :]
            scores = [(_dot(q, k_ref[keys, :], 1, 1) + _pair_bias(tiles_ref, j), _dot(q, k_ctx, 1, 1))
                      for _, keys, j, q in items]
            probs = [_softmax(s_loc, s_ctx) for s_loc, s_ctx in scores]
            outs = [_dot(p_loc.astype(BF16), v_ref[keys, :]) + _dot(p_ctx.astype(BF16), v_ctx)
                    for (_, keys, _, _), (p_loc, p_ctx) in zip(items, probs)]
            for (rows, _, _, _), out in zip(items, outs):
                o = _merge_heads(out, low)
                o_ref[rows, :] = o
                z_ref[rows, :] = (o * _silu(g_ref[rows, :])).astype(BF16)
            return carry

        lax.fori_loop(0, n_rows // ROWS_PER_STEP, step, 0)

    def part(p):
        return pl.BlockSpec((None, t, LANES), lambda h: (p, 0, h))

    out = pl.BlockSpec((seq, LANES), lambda h: (0, h))
    return _pc(
        body, name="attn_fwd", grid=(width // LANES,),
        in_specs=[part(0), part(1), part(2), part(0), pl.BlockSpec((2, 2 * WIN_ROWS, LANES), lambda h: (h, 0, 0))],
        out_specs=[out, out],
        out_shape=[jax.ShapeDtypeStruct((seq, width), BF16), jax.ShapeDtypeStruct((seq, width), F32)],
        scratch_shapes=[pltpu.VMEM((2, 2 * WIN_ROWS, LANES), F32), pltpu.VMEM((2, PAIR_TILES, GRID_W, LANES), F32)],
        compiler_params=_params("parallel"),
    )(qkv, qkv, qkv, gate, rpb)


def _attn_bwd(qkv, gate, o, dz, rpb, seq):
    _, t, width = qkv.shape
    n_rows = seq // GRID_W
    n_ctx = t - seq
    blk = WIN_ROWS * GRID_W
    heads = 2 * width // LANES

    def body(q_ref, k_ref, v_ref, g_ref, o_ref, dz_ref, rpb_ref, dpre_ref, drpb_ref,
             rows_ref, tiles_ref, dtiles_ref, dk_ref, dv_ref):
        low, inside = _pair_geometry()
        _bias_tiles(rpb_ref, rows_ref, tiles_ref, inside)
        dtiles_ref[...] = jnp.zeros_like(dtiles_ref)
        dk_ref[...] = jnp.zeros_like(dk_ref)
        dv_ref[...] = jnp.zeros_like(dv_ref)
        ctx = pl.ds(seq, n_ctx)

        def step(i, carry):
            items = _attn_items(i, n_rows, q_ref, low)
            k_ctx, v_ctx = k_ref[ctx, :], v_ref[ctx, :]
            d_outs = []
            for rows, _, _, _ in items:
                g = g_ref[rows, :]
                dzv = dz_ref[rows, :]
                dpre_ref[3, rows, :] = (dzv * o_ref[rows, :] * _dsilu(g)).astype(BF16)
                d_outs.append(_by_head((dzv * _silu(g)).astype(BF16), low))
            scores = [(_dot(q, k_ref[keys, :], 1, 1) + _pair_bias(tiles_ref, j), _dot(q, k_ctx, 1, 1))
                      for _, keys, j, q in items]
            dprobs = [(_dot(d_o, v_ref[keys, :], 1, 1), _dot(d_o, v_ctx, 1, 1))
                      for (_, keys, _, _), d_o in zip(items, d_outs)]
            probs = [_softmax(s_loc, s_ctx) for s_loc, s_ctx in scores]
            dscores = []
            for (p_loc, p_ctx), (dp_loc, dp_ctx) in zip(probs, dprobs):
                delta = (jnp.sum(p_loc * dp_loc, axis=-1, keepdims=True)
                         + jnp.sum(p_ctx * dp_ctx, axis=-1, keepdims=True))
                dscores.append((p_loc * (dp_loc - delta), p_ctx * (dp_ctx - delta)))
            dqs = [_dot(ds_loc.astype(BF16), k_ref[keys, :]) + _dot(ds_ctx.astype(BF16), k_ctx)
                   for (_, keys, _, _), (ds_loc, ds_ctx) in zip(items, dscores)]
            for (rows, _, _, _), dq in zip(items, dqs):
                dpre_ref[0, rows, :] = (_merge_heads(dq, low) * HEAD_DIM ** -0.5).astype(BF16)
            for (_, keys, j, q), d_o, (p_loc, p_ctx), (ds_loc, ds_ctx) in zip(items, d_outs, probs, dscores):
                dk_ref[keys, :] += _dot(ds_loc.astype(BF16), q, 0, 0)
                dk_ref[ctx, :] += _dot(ds_ctx.astype(BF16), q, 0, 0)
                dv_ref[keys, :] += _dot(p_loc.astype(BF16), d_o, 0, 0)
                dv_ref[ctx, :] += _dot(p_ctx.astype(BF16), d_o, 0, 0)
                for h in range(2):
                    for m in range(WIN_ROWS // 2):
                        dtiles_ref[h, j + 2 * m] += ds_loc[h * GRID_W:(h + 1) * GRID_W, m * LANES:(m + 1) * LANES]
            return carry

        lax.fori_loop(0, n_rows // ROWS_PER_STEP, step, 0)
        dpre_ref[1] = dk_ref[...].astype(BF16)
        dpre_ref[2] = dv_ref[...].astype(BF16)
        dpre_ref[0, ctx, :] = jnp.zeros((n_ctx, LANES), BF16)
        dpre_ref[3, ctx, :] = jnp.zeros((n_ctx, LANES), BF16)
        _bias_tiles_grad(dtiles_ref, drpb_ref)

    def part(p):
        return pl.BlockSpec((None, t, LANES), lambda h: (p, 0, h))

    lat = pl.BlockSpec((seq, LANES), lambda h: (0, h))
    rspec = pl.BlockSpec((2, 2 * WIN_ROWS, LANES), lambda h: (h, 0, 0))
    tiles = pltpu.VMEM((2, PAIR_TILES, GRID_W, LANES), F32)
    return _pc(
        body, name="attn_bwd", grid=(width // LANES,),
        in_specs=[part(0), part(1), part(2), part(0), lat, lat, rspec],
        out_specs=[pl.BlockSpec((4, t, LANES), lambda h: (0, 0, h)), rspec],
        out_shape=[jax.ShapeDtypeStruct((4, t, width), BF16), jax.ShapeDtypeStruct((heads, 2 * WIN_ROWS, LANES), F32)],
        scratch_shapes=[pltpu.VMEM((2, 2 * WIN_ROWS, LANES), F32), tiles, tiles,
                        pltpu.VMEM((t, LANES), F32), pltpu.VMEM((t, LANES), F32)],
        compiler_params=_params("parallel"),
    )(qkv, qkv, qkv, gate, o, dz, rpb)


def _adam_update(w, m, v, g):
    m2 = ADAM_B1 * m + (1.0 - ADAM_B1) * g
    v2 = ADAM_B2 * v + (1.0 - ADAM_B2) * (g * g)
    m_hat = m2 / (1.0 - ADAM_B1 ** ADAM_STEP)
    v_hat = v2 / (1.0 - ADAM_B2 ** ADAM_STEP)
    return -ADAM_LR * (m_hat / (jnp.sqrt(v_hat) + ADAM_EPS) + ADAM_WD * w), m2, v2


def _adamw(w, m, v, parts, name):
    rows, cols = w.shape
    tr = _tile(rows, max(8, 131072 // cols), 8)
    n_parts = len(parts)

    def body(*refs):
        w_ref, m_ref, v_ref = refs[:3]
        part_refs = refs[3:3 + n_parts]
        g_ref, d_ref, nm_ref, nv_ref = refs[3 + n_parts:]
        g = part_refs[0][...].astype(F32)
        for p in part_refs[1:]:
            g = g + p[...].astype(F32)
        g_ref[...] = g
        d_ref[...], nm_ref[...], nv_ref[...] = _adam_update(w_ref[...], m_ref[...], v_ref[...], g)

    tile = pl.BlockSpec((tr, cols), lambda i: (i, 0))
    in_specs, args = [tile, tile, tile], [w, m, v]
    for p in parts:
        if isinstance(p, tuple):
            arr, k = p
            in_specs.append(pl.BlockSpec((None, tr, cols), lambda i, k=k: (k, i, 0)))
            args.append(arr)
        else:
            in_specs.append(tile)
            args.append(p)
    shape = jax.ShapeDtypeStruct((rows, cols), F32)
    return _pc(
        body, name=name, grid=(rows // tr,), in_specs=in_specs, out_specs=[tile] * 4, out_shape=[shape] * 4,
        compiler_params=_params("parallel"),
    )(*args)


def _adamw_small(states, grads):
    sources, makers = grads
    n, ns = len(states), len(sources)

    def body(*refs):
        src = refs[:ns]
        ins = refs[ns:ns + 3 * n]
        outs = refs[ns + 3 * n:]
        for k in range(n):
            w_ref, m_ref, v_ref = ins[3 * k:3 * k + 3]
            g = makers[k](*src)
            outs[4 * k][...] = g
            outs[4 * k + 1][...], outs[4 * k + 2][...], outs[4 * k + 3][...] = _adam_update(
                w_ref[...], m_ref[...], v_ref[...], g)

    flat = [a for s in states for a in s]
    vmem = pl.BlockSpec(memory_space=pltpu.VMEM)
    res = _pc(
        body, name="adamw_small",
        in_specs=[vmem] * (ns + 3 * n), out_specs=[vmem] * (4 * n),
        out_shape=[jax.ShapeDtypeStruct(s[0].shape, F32) for s in states for _ in range(4)],
        compiler_params=pltpu.CompilerParams(vmem_limit_bytes=VMEM_LIMIT),
    )(*sources, *flat)
    return [res[4 * k:4 * k + 4] for k in range(n)]


def _rows128(a):
    flat = a.reshape(-1)
    pad = (-flat.shape[0]) % LANES
    if pad:
        flat = jnp.concatenate([flat, jnp.zeros((pad,), flat.dtype)])
    return flat.reshape(-1, LANES)


def _pad_rows(a, mult=8):
    pad = (-a.shape[0]) % mult
    if pad:
        a = jnp.concatenate([a, jnp.zeros((pad,) + a.shape[1:], a.dtype)], axis=0)
    return a


def kernel(x, c, ctx, c_ctx, norm_g, ada_w, ada_b, pool_w_in, pool_w_grp, pool_scale, pool_w_out, na_w_in, na_rpb, na_w_out, conv_w_in, conv_dw, conv_db, conv_w_out, final_g, loss_target, m_c_ctx, m_norm_g, m_ada_w, m_ada_b, m_pool_w_in, m_pool_w_grp, m_pool_scale, m_pool_w_out, m_na_w_in, m_na_rpb, m_na_w_out, m_conv_w_in, m_conv_dw, m_conv_db, m_conv_w_out, m_final_g, v_c_ctx, v_norm_g, v_ada_w, v_ada_b, v_pool_w_in, v_pool_w_grp, v_pool_scale, v_pool_w_out, v_na_w_in, v_na_rpb, v_na_w_out, v_conv_w_in, v_conv_dw, v_conv_db, v_conv_w_out, v_final_g):
    xi, yi, ci = _my_place()
    me = 4 * xi + 2 * yi + ci
    seq, d = x.shape[1], x.shape[2]
    n_ctx = ctx.shape[1]
    t_all = seq + n_ctx
    width = d
    depth = norm_g.shape[0]
    nb = ada_w.shape[2]
    shard = width // N_DEV
    d_rows = d // LANES
    tr = math.gcd(math.gcd(seq, n_ctx), 256)
    x_tiles = seq // tr

    n_pool = pool_scale.shape[0]
    n_grp = pool_w_grp.shape[1]
    grp = width // n_grp

    small_in = _pad_rows(jnp.concatenate([_rows128(c), pool_scale, conv_dw[0], conv_db], axis=0))
    got = _gather_small(small_in, "gather_inputs")
    r0 = d_rows
    c_all = got[:, :r0].reshape(N_DEV, d)
    scale_full = got[:, r0:r0 + n_pool].transpose(1, 0, 2).reshape(n_pool, width)
    r1 = r0 + n_pool
    taps_full = _pad_rows(got[:, r1:r1 + 3].transpose(1, 0, 2).reshape(3, width))
    bias_full = got[:, r1 + 3:r1 + 4].transpose(1, 0, 2).reshape(1, width)

    cond = jnp.concatenate([c_all, c_ctx[None], jnp.zeros((7, d), F32)], axis=0)
    bias_mine = lax.dynamic_slice(ada_b, (0, me * nb), (depth, nb))
    mod_mine = _mod_fwd(cond, ada_w, bias_mine)
    mod_all = _gather_small(mod_mine.reshape(-1, LANES), "gather_mod")
    mod_all = mod_all.reshape(N_DEV, depth, 16, nb).transpose(1, 2, 0, 3).reshape(depth, 16, 3 * d)
    mod_x = lax.dynamic_index_in_dim(mod_all, me, 1, keepdims=False).reshape(depth, 3, d)
    mod_c = mod_all[:, 8].reshape(depth, 3, d)
    pad5 = jnp.zeros((depth, 5, d), F32)
    mod_x = jnp.concatenate([mod_x, pad5], axis=1)
    mod_c = jnp.concatenate([mod_c, pad5], axis=1)
    mods = [jnp.stack([mod_x[i], mod_c[i]]) if i < 2 else mod_x[i][None] for i in range(depth)]

    layer_weights = [[pool_w_in[0], pool_w_grp[0], pool_w_out[0]], [na_w_out[0]],
                     [conv_w_in[0], conv_w_out[0]], [pool_w_in[1], pool_w_grp[1], pool_w_out[1]]]
    in_flight, token = [], mod_c
    for i, ws in enumerate(layer_weights):
        if i == 1:
            na_in_first = _exchange_start([na_w_in[0].astype(BF16)], False, token, "weights_start1_chips", peers=CHIP_PEERS)
            token = na_in_first[-1]
        state = _exchange_start([w.astype(BF16) for w in ws], False, token, f"weights_start{i}")
        token = state[-1]
        in_flight.append(state)

    def landed_weight(i, t, after):
        return _exchange_wait(in_flight[i], False, after, f"weights_wait{i}_{t}", which=[t])[0]

    def as_in(w):
        return w[:, None]

    def as_grp(w):
        return w.transpose(1, 0, 2, 3).reshape(n_grp, grp, grp)

    def as_out(w):
        return w.reshape(width, d)

    both = [(0, seq), (seq, n_ctx)]
    latent = [(0, seq)]

    def grp_slots(g):
        return g.reshape(n_grp, N_DEV, grp // N_DEV, grp).transpose(1, 0, 2, 3).reshape(N_DEV, -1, grp).astype(BF16)

    def send_grads(i, grads):
        return _exchange_start(grads, True, jnp.zeros((8, LANES), F32), f"grads_start{i}")

    xs0 = jnp.concatenate([x[0], ctx[0]], axis=0)
    h0 = _norm_fwd(xs0, norm_g[0:1] + token[0, 0], mods[0], tr, x_tiles, "norm_fwd0")
    pool_in_w0 = as_in(landed_weight(0, 0, h0))
    pre0 = _proj_in(h0, pool_in_w0, 0, width, "proj_in0")
    pool_grp_w0 = as_grp(landed_weight(0, 1, pre0))
    z0, diff0 = _pool_fwd(pre0, pool_grp_w0, scale_full[0:1], both, "pool_fwd0")
    pool_out_w0 = as_out(landed_weight(0, 2, z0))
    yx0, xs1, h1 = _proj_out(z0, pool_out_w0, xs0, mods[0], tr, x_tiles, "proj_out0", nxt=(norm_g[1:2], mods[1]))

    na_in_half = _exchange_wait(na_in_first, False, xs1, "weights_wait1_chips", peers=CHIP_PEERS)
    na_in_rest = _forward_start(na_in_half, h1, "weights_forward1")
    na_in_w = as_in(_forward_wait(na_in_rest, na_in_rest[-1], "weights_forward_wait1")[0])
    per_part = width // na_w_in.shape[2]
    qkv1 = _proj_in(h1, na_in_w, 0, width, "proj_in1_qkv", blocks=(0, 3 * per_part), dtype=BF16)
    gpre1 = _proj_in(h1, na_in_w, 0, width, "proj_in1_gate", blocks=(3 * per_part, per_part))
    rpb_rows = jnp.pad(na_rpb[0], ((0, 0), (0, 2 * WIN_ROWS - na_rpb.shape[2]), (0, LANES - na_rpb.shape[3])))
    z1, o1 = _attn_fwd(qkv1, gpre1, rpb_rows, seq)
    na_out_w = as_out(landed_weight(1, 0, z1))
    yx1, x2, h2 = _proj_out(z1, na_out_w, xs1, mods[1], tr, x_tiles, "proj_out1", nxt=(norm_g[2:3], mods[2]))

    conv_in_w = as_in(landed_weight(2, 0, h2))
    pre2 = _proj_in(h2, conv_in_w, 0, width, "proj_in2")
    z2 = _conv_fwd(pre2, taps_full, bias_full, "conv_fwd")
    conv_out_w = as_out(landed_weight(2, 1, z2))
    yx2, x3, h3 = _proj_out(z2, conv_out_w, x2, mods[2], tr, x_tiles, "proj_out2", nxt=(norm_g[3:4], mods[3]))

    pool_in_w3 = as_in(landed_weight(3, 0, h3))
    pre3 = _proj_in(h3, pool_in_w3, 0, width, "proj_in3")
    pool_grp_w3 = as_grp(landed_weight(3, 1, pre3))
    z3, diff3 = _pool_fwd(pre3, pool_grp_w3, scale_full[1:2], latent, "pool_fwd3")
    pool_out_w3 = as_out(landed_weight(3, 2, z3))
    yx3, x4 = _proj_out(z3, pool_out_w3, x3, mods[3], tr, x_tiles, "proj_out3")

    loss_part, dx4, d_final, dyx3, gate3 = _loss_head(x4, loss_target[0], final_g[None], yx3, mods[3], tr)

    dz3 = _proj_out_dz(dyx3, pool_out_w3, "proj_out_dz3")
    g_pool_out1 = _grad_w_out(z3, dyx3, "grad_w_out3")
    dpre3, g_grp1, g_scale1 = _pool_bwd(dz3, diff3, pre3, pool_grp_w3, scale_full[1:2], latent, "pool_bwd3")
    dh3 = _proj_in_dh(dpre3, pool_in_w3, 0, "proj_in_dh3")
    g_pool_in1 = _grad_w_in(h3, dpre3, pool_w_in.shape[2], "grad_w_in3")
    sent3 = send_grads(3, [g_pool_in1, grp_slots(g_grp1), g_pool_out1.reshape(N_DEV, shard, d)])
    dx3, norm3, dyx2, gate2 = _norm_bwd(x3, dh3, dx4, norm_g[3:4] + sent3[-1][0, 0], mods[3], tr, x_tiles, "norm_bwd3",
                                        below=(yx2, mods[2]))

    dz2 = _proj_out_dz(dyx2, conv_out_w, "proj_out_dz2")
    g_conv_out = _grad_w_out(z2, dyx2, "grad_w_out2")
    dpre2, g_taps, g_cbias = _conv_bwd(dz2, pre2, taps_full, bias_full, "conv_bwd")
    dh2 = _proj_in_dh(dpre2, conv_in_w, 0, "proj_in_dh2")
    g_conv_in = _grad_w_in(h2, dpre2, conv_w_in.shape[2], "grad_w_in2")
    sent2 = send_grads(2, [g_conv_in, g_conv_out.reshape(N_DEV, shard, d)])
    dx2, norm2, dyx1, gate1 = _norm_bwd(x2, dh2, dx3, norm_g[2:3] + sent2[-1][0, 0], mods[2], tr, x_tiles, "norm_bwd2",
                                        below=(yx1, mods[1][:1]))

    dz1 = _proj_out_dz(dyx1, na_out_w, "proj_out_dz1")
    g_na_out = _grad_w_out(z1, dyx1, "grad_w_out1")
    dpre1, g_rpb = _attn_bwd(qkv1, gpre1, o1, dz1, rpb_rows, seq)
    g_rpb = g_rpb[:, :na_rpb.shape[2], :na_rpb.shape[3]]
    dh1 = _proj_in_dh(dpre1, na_in_w, 0, "proj_in_dh1")
    g_na_in = _grad_w_in(h1, dpre1, na_w_in.shape[2], "grad_w_in1")
    sent1 = send_grads(1, [g_na_in, g_na_out.reshape(N_DEV, shard, d)])
    dxs1, norm1, dyx0, gate0 = _norm_bwd(xs1, dh1, dx2, norm_g[1:2] + sent1[-1][0, 0], mods[1], tr, x_tiles, "norm_bwd1",
                                         res_tiles=x_tiles, below=(yx0, mods[0]))

    dz0 = _proj_out_dz(dyx0, pool_out_w0, "proj_out_dz0")
    g_pool_out0 = _grad_w_out(z0, dyx0, "grad_w_out0")
    sent0a = _exchange_start([g_pool_out0.reshape(N_DEV, shard, d)], True, jnp.zeros((8, LANES), F32), "grads_start0a")
    dpre0, g_grp0, g_scale0 = _pool_bwd(dz0, diff0, pre0, pool_grp_w0, scale_full[0:1], both, "pool_bwd0")
    g_pool_in0 = _grad_w_in(h0, dpre0, pool_w_in.shape[2], "grad_w_in0")
    sent0b = _exchange_start([g_pool_in0, grp_slots(g_grp0)], True, sent0a[-1], "grads_start0b")
    dh0 = _proj_in_dh(dpre0, pool_in_w0, 0, "proj_in_dh0", after=sent0b[-1])
    dx0, norm0 = _norm_bwd(xs0, dh0, dxs1, norm_g[0:1], mods[0], tr, x_tiles, "norm_bwd0", out_tiles=x_tiles)
    grad_x = dx0[None]

    norms, gates = [norm0, norm1, norm2, norm3], [gate0, gate1, gate2, gate3]
    zero_d = jnp.zeros((d,), F32)
    dm_rows = [jnp.concatenate([norms[i][0, 0], norms[i][0, 1], gates[i][0, 0]]) for i in range(depth)]
    dm_rows.append(jnp.concatenate([norm0[1, 0], norm0[1, 1], gate0[1, 0]]))
    dm_rows.append(jnp.concatenate([norm1[1, 0], norm1[1, 1], zero_d]))
    dm_local = jnp.stack(dm_rows + [jnp.zeros((3 * d,), F32)] * 2)
    g_norm_part = jnp.stack([norm0[0, 2] + norm0[1, 2], norm1[0, 2] + norm1[1, 2], norm2[0, 2], norm3[0, 2]])
    pieces = [_rows128(dm_local), _rows128(g_norm_part), _rows128(d_final[0]), _pad_rows(_rows128(g_rpb)), loss_part]
    marks = np.cumsum([0] + [p.shape[0] for p in pieces])
    by_owner = [a.reshape(-1, N_DEV, shard).transpose(1, 0, 2) for a in (g_scale0, g_scale1, g_taps[:3], g_cbias[0:1])]
    by_owner = jnp.concatenate(by_owner + [jnp.zeros((N_DEV, 8 - n_pool - 4, shard), F32)], axis=1)
    small_sent = _exchange_start([jnp.concatenate(pieces, axis=0), by_owner], [False, True], jnp.zeros((8, LANES), F32),
                                 "small_grads_start")

    def big(parts, w, m, v, name):
        shape = w.shape
        view = (-1, shape[-1])
        parts = [(parts.reshape((N_DEV,) + w.reshape(view).shape), k) for k in range(N_DEV)]
        return [r.reshape(shape) for r in _adamw(w.reshape(view), m.reshape(view), v.reshape(view), parts, name)]

    in3, grp3, out3 = _exchange_wait(sent3, True, small_sent[-1], "grads_wait3")
    in2, out2 = _exchange_wait(sent2, True, small_sent[-1], "grads_wait2")
    in1, out1 = _exchange_wait(sent1, True, small_sent[-1], "grads_wait1")
    res = {}
    res["na_w_in"] = [r[None] for r in big(in1, na_w_in[0], m_na_w_in[0], v_na_w_in[0], "adamw_na_in")]
    res["na_w_out"] = [r[None] for r in big(out1, na_w_out[0], m_na_w_out[0], v_na_w_out[0], "adamw_na_out")]
    res["conv_w_in"] = [r[None] for r in big(in2, conv_w_in[0], m_conv_w_in[0], v_conv_w_in[0], "adamw_conv_in")]
    res["conv_w_out"] = [r[None] for r in big(out2, conv_w_out[0], m_conv_w_out[0], v_conv_w_out[0], "adamw_conv_out")]
    pool_in1 = big(in3, pool_w_in[1], m_pool_w_in[1], v_pool_w_in[1], "adamw_pool_in1")
    pool_grp1 = big(grp3, pool_w_grp[1], m_pool_w_grp[1], v_pool_w_grp[1], "adamw_pool_grp1")
    pool_out1 = big(out3, pool_w_out[1], m_pool_w_out[1], v_pool_w_out[1], "adamw_pool_out1")

    small_out, owned = _exchange_wait(small_sent, [False, True], pool_out1[0], "small_grads_wait")
    loss = jnp.sum(small_out[:, marks[4], 0])
    dm_all = small_out[:, :marks[1]].reshape(N_DEV, 8, 3 * d).transpose(1, 0, 2)
    dm_mine = lax.dynamic_slice(dm_all, (0, 0, me * nb), (8, N_DEV, nb))
    g_ada_w, g_ada_b, cctx_part, dsilu_cond = _mod_bwd(cond, ada_w, dm_all, dm_mine)
    cctx_all = _gather_small(_rows128(cctx_part[0]), "gather_cctx")

    def summed(ref, lo, hi):
        g = ref[0, lo:hi, :]
        for k in range(1, N_DEV):
            g = g + ref[k, lo:hi, :]
        return g

    makers = [
        lambda so, ow, cc, ab, ds: summed(cc, 0, d_rows) * ds[...],
        lambda so, ow, cc, ab, ds: summed(so, marks[1], marks[2]),
        lambda so, ow, cc, ab, ds: ab[...],
        lambda so, ow, cc, ab, ds: summed(so, marks[2], marks[3]),
        lambda so, ow, cc, ab, ds: summed(so, marks[3], marks[4]),
        lambda so, ow, cc, ab, ds: summed(ow, 0, n_pool),
        lambda so, ow, cc, ab, ds: summed(ow, n_pool, n_pool + 3),
        lambda so, ow, cc, ab, ds: summed(ow, n_pool + 3, n_pool + 4),
    ]
    rpb_rows128 = lambda a: _pad_rows(_rows128(a))
    views = [_rows128] * 4 + [rpb_rows128] + [lambda a: a.reshape(-1, LANES)] * 3
    small = [(c_ctx, m_c_ctx, v_c_ctx), (norm_g, m_norm_g, v_norm_g), (ada_b, m_ada_b, v_ada_b),
             (final_g, m_final_g, v_final_g), (na_rpb, m_na_rpb, v_na_rpb), (pool_scale, m_pool_scale, v_pool_scale),
             (conv_dw, m_conv_dw, v_conv_dw), (conv_db, m_conv_db, v_conv_db)]
    states = [tuple(view(a) for a in triple) for view, triple in zip(views, small)]
    sources = (small_out, owned, cctx_all, _rows128(g_ada_b), _rows128(dsilu_cond[8]))
    small_res = _adamw_small(states, (sources, makers))
    names = ["c_ctx", "norm_g", "ada_b", "final_g", "na_rpb", "pool_scale", "conv_dw", "conv_db"]
    for name, (w, _, _), outs4 in zip(names, small, small_res):
        res[name] = [r.reshape(-1)[:w.size].reshape(w.shape) for r in outs4]

    res["ada_w"] = [r.reshape(ada_w.shape) for r in _adamw(
        ada_w.reshape(-1, nb), m_ada_w.reshape(-1, nb), v_ada_w.reshape(-1, nb), [g_ada_w.reshape(-1, nb)], "adamw_ada_w")]

    out0, = _exchange_wait(sent0a, True, small_res[0][0], "grads_wait0a")
    in0, grp0 = _exchange_wait(sent0b, True, small_res[0][0], "grads_wait0b")
    pool_in0 = big(in0, pool_w_in[0], m_pool_w_in[0], v_pool_w_in[0], "adamw_pool_in0")
    pool_grp0 = big(grp0, pool_w_grp[0], m_pool_w_grp[0], v_pool_w_grp[0], "adamw_pool_grp0")
    pool_out0 = big(out0, pool_w_out[0], m_pool_w_out[0], v_pool_w_out[0], "adamw_pool_out0")
    res["pool_w_in"] = [jnp.stack([p, q]) for p, q in zip(pool_in0, pool_in1)]
    res["pool_w_grp"] = [jnp.stack([p, q]) for p, q in zip(pool_grp0, pool_grp1)]
    res["pool_w_out"] = [jnp.stack([p, q]) for p, q in zip(pool_out0, pool_out1)]

    order = ["c_ctx", "norm_g", "ada_w", "ada_b", "pool_w_in", "pool_w_grp", "pool_scale", "pool_w_out", "na_w_in",
             "na_rpb", "na_w_out", "conv_w_in", "conv_dw", "conv_db", "conv_w_out", "final_g"]
    outs = [loss, grad_x]
    for j in range(4):
        outs += [res[n][j] for n in order]
    return tuple(outs)
```

```python
import functools
import math

import numpy as np
import jax
import jax.numpy as jnp
from jax import lax
from jax.experimental import pallas as pl
from jax.experimental.pallas import tpu as pltpu

F32 = jnp.float32
BF16 = jnp.bfloat16
N_DEV = 8
LANES = 128
RMS_EPS = 1e-6
GRID_W = 64
WIN_ROWS = 8
WIN_COLS = 16
HEAD_DIM = 64
POOL_WINDOWS = (2, 4, 8, 16)
HALO = 8
CHUNK = 128
MASKED = -1e30
ADAM_LR = 0.001
ADAM_B1 = 0.9
ADAM_B2 = 0.999
ADAM_EPS = 1e-08
ADAM_WD = 0.01
ADAM_STEP = 10
VMEM_LIMIT = 56 * 1024 * 1024
MESH = pl.DeviceIdType.MESH
ANY = pl.BlockSpec(memory_space=pl.ANY)
HBM = pl.BlockSpec(memory_space=pltpu.HBM)
SEM = pl.BlockSpec(memory_space=pltpu.SEMAPHORE)
EFFECT = pltpu.SideEffectType.DATAFLOW_SIDE_EFFECTING


def _pc(body, *, name, **kw):
    return pl.pallas_call(body, name=name, **kw)


def _params(*sem):
    return pltpu.CompilerParams(dimension_semantics=sem if sem else None, vmem_limit_bytes=VMEM_LIMIT)


def _dot(a, b, ca=1, cb=0, precision=None):
    return lax.dot_general(a, b, (((ca,), (cb,)), ((), ())), preferred_element_type=F32, precision=precision)


def _tile(n, pref, unit=LANES):
    best = None
    for t in range(unit, min(n, pref) + 1, unit):
        if n % t == 0:
            best = t
    return best if best is not None else n


def _sigmoid(x):
    return 1.0 / (1.0 + jnp.exp(-x))


def _silu(x):
    return x * _sigmoid(x)


def _dsilu(x):
    s = _sigmoid(x)
    return s * (1.0 + x * (1.0 - s))


def _my_place():
    return lax.axis_index("x"), lax.axis_index("y"), lax.axis_index("c")


def _flip(v, f):
    return 1 - v if f else v


def _gather_small(block, name):
    rows, cols = block.shape

    def body(x_ref, out_ref, send_sems, recv_sems):
        x, y, c = _my_place()
        me = 4 * x + 2 * y + c
        out_ref[me] = x_ref[...]
        copies = []
        for k in range(1, N_DEV):
            peer = (_flip(x, k & 4), _flip(y, k & 2), _flip(c, k & 1))
            cp = pltpu.make_async_remote_copy(
                src_ref=x_ref, dst_ref=out_ref.at[me], send_sem=send_sems.at[k - 1], recv_sem=recv_sems.at[k - 1],
                device_id=peer, device_id_type=MESH)
            cp.start()
            copies.append(cp)
        for cp in copies:
            cp.wait()

    return _pc(
        body, name=name,
        out_shape=jax.ShapeDtypeStruct((N_DEV, rows, cols), block.dtype),
        in_specs=[pl.BlockSpec(memory_space=pltpu.VMEM)],
        out_specs=pl.BlockSpec(memory_space=pltpu.VMEM),
        scratch_shapes=[pltpu.SemaphoreType.DMA((N_DEV - 1,)), pltpu.SemaphoreType.DMA((N_DEV - 1,))],
    )(block)


ALL_PEERS = tuple(range(N_DEV))
CHIP_PEERS = (0, 1, 2, 4, 6)
OTHER_CHIPS = (2, 4, 6)


def _flag(per_dest, t):
    return per_dest[t] if isinstance(per_dest, (list, tuple)) else per_dest


def _peer(k):
    x, y, c = _my_place()
    peer = (_flip(x, k & 4), _flip(y, k & 2), _flip(c, k & 1))
    return peer, 4 * peer[0] + 2 * peer[1] + peer[2]


def _exchange_copies(srcs, lands, send_sems, recv_sems, per_dest, peers=ALL_PEERS):
    x, y, c = _my_place()
    me = 4 * x + 2 * y + c
    copies = []
    for t, (src, land) in enumerate(zip(srcs, lands)):
        for n, k in enumerate(peers):
            peer, dest = _peer(k)
            s = t * len(peers) + n
            copies.append(pltpu.make_async_remote_copy(
                src_ref=src.at[dest] if _flag(per_dest, t) else src, dst_ref=land.at[me],
                send_sem=send_sems[s], recv_sem=recv_sems[s], device_id=peer, device_id_type=MESH))
    return copies


def _forward_copies(lands, send_sems, recv_sems):
    sibling, _ = _peer(1)
    copies = []
    for t, land in enumerate(lands):
        for n, k in enumerate(OTHER_CHIPS):
            _, slot = _peer(k)
            s = t * len(OTHER_CHIPS) + n
            copies.append(pltpu.make_async_remote_copy(
                src_ref=land.at[slot], dst_ref=land.at[slot], send_sem=send_sems[s], recv_sem=recv_sems[s],
                device_id=sibling, device_id_type=MESH))
    return copies


def _forward_start(lands, after, name):
    nt = len(lands)
    ns = nt * len(OTHER_CHIPS)

    def body(*refs):
        ins, outs = refs[:nt + 1], refs[nt + 1:]
        for cp in _forward_copies(ins[:nt], outs[:ns], outs[ns:2 * ns]):
            cp.start()
        outs[-1][...] = jnp.zeros_like(outs[-1])

    res = _pc(
        body, name=name,
        out_shape=(*[pltpu.SemaphoreType.DMA(())] * (2 * ns), *[pltpu.HBM(a.shape, a.dtype) for a in lands],
                   jax.ShapeDtypeStruct((8, LANES), F32)),
        in_specs=[HBM] * nt + [ANY],
        out_specs=(*[SEM] * (2 * ns), *[HBM] * nt, pl.BlockSpec(memory_space=pltpu.VMEM)),
        input_output_aliases={i: 2 * ns + i for i in range(nt)},
        compiler_params=pltpu.CompilerParams(has_side_effects=EFFECT),
    )(*lands, after)
    return list(res[:ns]), list(res[ns:2 * ns]), list(res[2 * ns:2 * ns + nt]), res[-1]


def _forward_wait(state, after, name):
    send_sems, recv_sems, lands, _ = state
    nt, ns = len(lands), len(send_sems)

    def body(*refs):
        sems = refs[nt:nt + 2 * ns]
        for cp in _forward_copies(refs[:nt], sems[:ns], sems[ns:]):
            cp.wait_send()
            cp.wait_recv()

    res = _pc(
        body, name=name,
        out_shape=tuple(pltpu.HBM(a.shape, a.dtype) for a in lands),
        in_specs=[HBM] * nt + [SEM] * (2 * ns) + [ANY],
        out_specs=tuple([HBM] * nt),
        input_output_aliases={i: i for i in range(nt)},
        compiler_params=pltpu.CompilerParams(has_side_effects=EFFECT),
    )(*lands, *send_sems, *recv_sems, after)
    return list(res)


def _exchange_start(srcs, per_dest, after, name, peers=ALL_PEERS):
    nt = len(srcs)
    ns = nt * len(peers)
    lands = [lax.empty((N_DEV,) + (s.shape[1:] if _flag(per_dest, t) else s.shape), s.dtype) for t, s in enumerate(srcs)]

    def body(*refs):
        ins, outs = refs[:2 * nt + 1], refs[2 * nt + 1:]
        for cp in _exchange_copies(ins[:nt], ins[nt:2 * nt], outs[:ns], outs[ns:2 * ns], per_dest, peers):
            cp.start()
        outs[-1][...] = jnp.zeros_like(outs[-1])

    hbm = [pltpu.with_memory_space_constraint(a, pltpu.HBM) for a in list(srcs) + lands]
    res = _pc(
        body, name=name,
        out_shape=(*[pltpu.SemaphoreType.DMA(())] * (2 * ns), *[pltpu.HBM(a.shape, a.dtype) for a in hbm],
                   jax.ShapeDtypeStruct((8, LANES), F32)),
        in_specs=[HBM] * (2 * nt) + [ANY],
        out_specs=(*[SEM] * (2 * ns), *[HBM] * (2 * nt), pl.BlockSpec(memory_space=pltpu.VMEM)),
        input_output_aliases={i: 2 * ns + i for i in range(2 * nt)},
        compiler_params=pltpu.CompilerParams(has_side_effects=EFFECT),
    )(*hbm, after)
    sems, rest = res[:2 * ns], res[2 * ns:]
    return list(sems[:ns]), list(sems[ns:]), list(rest[:nt]), list(rest[nt:2 * nt]), rest[-1]


def _exchange_wait(state, per_dest, after, name, which=None, peers=ALL_PEERS):
    send_sems, recv_sems, srcs, lands, _ = state
    which = list(range(len(srcs))) if which is None else which
    per_dest = [_flag(per_dest, t) for t in which]
    after = list(after) if isinstance(after, (list, tuple)) else [after]
    pick =[t * len(peers) + n for t in which for n in range(len(peers))]
    send_sems, recv_sems = [send_sems[s] for s in pick], [recv_sems[s] for s in pick]
    srcs, lands = [srcs[t] for t in which], [lands[t] for t in which]
    nt = len(srcs)
    ns = len(send_sems)

    def body(*refs):
        sems = refs[2 * nt:2 * nt + 2 * ns]
        for cp in _exchange_copies(refs[:nt], refs[nt:2 * nt], sems[:ns], sems[ns:], per_dest, peers):
            cp.wait_send()
            cp.wait_recv()

    thru = list(srcs) + list(lands)
    res = _pc(
        body, name=name,
        out_shape=tuple(pltpu.HBM(a.shape, a.dtype) for a in thru),
        in_specs=[HBM] * (2 * nt) + [SEM] * (2 * ns) + [ANY] * len(after),
        out_specs=tuple([HBM] * (2 * nt)),
        input_output_aliases={i: i for i in range(2 * nt)},
        compiler_params=pltpu.CompilerParams(has_side_effects=EFFECT),
    )(*thru, *send_sems, *recv_sems, *after)
    return list(res[nt:])


def _mod_fwd(cond, ada_w, bias):
    depth, d, nb = ada_w.shape

    def body(c_ref, w_ref, b_ref, o_ref):
        s = _silu(c_ref[...]).astype(BF16)
        o_ref[...] = _dot(s, w_ref[...].astype(BF16)) + b_ref[...]

    return _pc(
        body, name="mod_fwd", grid=(depth,),
        in_specs=[pl.BlockSpec((16, d), lambda i: (0, 0)), pl.BlockSpec((None, d, nb), lambda i: (i, 0, 0)),
                  pl.BlockSpec((None, 1, nb), lambda i: (i, 0, 0))],
        out_specs=pl.BlockSpec((None, 16, nb), lambda i: (i, 0, 0)),
        out_shape=jax.ShapeDtypeStruct((depth, 16, nb), F32),
        compiler_params=_params("parallel"),
    )(cond, ada_w, bias.reshape(depth, 1, nb))


def _mod_bwd(cond, ada_w, dm_all, dm_mine):
    depth, d, nb = ada_w.shape
    d3 = dm_all.shape[-1]

    def body(c_ref, w_ref, all_ref, call_ref, mine_ref, cmine_ref, gw_ref, gb_ref, part_ref, ds_ref):
        i = pl.program_id(0)
        cond_v = c_ref[...]
        s = _silu(cond_v).astype(BF16)
        has_ctx = jnp.where(i < 2, 1.0, 0.0)
        tot_all = jnp.sum(call_ref[...], axis=0, keepdims=True) * has_ctx
        tot_mine = jnp.broadcast_to(jnp.sum(cmine_ref[...], axis=0, keepdims=True) * has_ctx, (8, nb)).astype(BF16)
        gb_ref[...] = jnp.sum(all_ref[...], axis=0, keepdims=True) + tot_all
        gw_ref[...] = _dot(s[0:8], mine_ref[...].astype(BF16), 0, 0) + _dot(s[8:16], tot_mine, 0, 0)
        part = _dot(tot_mine, w_ref[...].astype(BF16), 1, 1)

        @pl.when(i == 0)
        def _():
            part_ref[...] = jnp.zeros_like(part_ref)
            ds_ref[...] = _dsilu(cond_v)

        part_ref[...] += part

    def rows(width, which):
        return pl.BlockSpec((None, N_DEV, width), which)

    layer = lambda i: (i, 0, 0)
    ctx_layer = lambda i: (jnp.minimum(i, 1) + 4, 0, 0)
    return _pc(
        body, name="mod_bwd", grid=(depth,),
        in_specs=[pl.BlockSpec((16, d), lambda i: (0, 0)), pl.BlockSpec((None, d, nb), layer),
                  rows(d3, layer), rows(d3, ctx_layer), rows(nb, layer), rows(nb, ctx_layer)],
        out_specs=[pl.BlockSpec((None, d, nb), layer), pl.BlockSpec((None, 1, d3), layer),
                   pl.BlockSpec((8, d), lambda i: (0, 0)), pl.BlockSpec((16, d), lambda i: (0, 0))],
        out_shape=[jax.ShapeDtypeStruct((depth, d, nb), F32), jax.ShapeDtypeStruct((depth, 1, d3), F32),
                   jax.ShapeDtypeStruct((8, d), F32), jax.ShapeDtypeStruct((16, d), F32)],
        compiler_params=_params("arbitrary"),
    )(cond, ada_w, dm_all, dm_all, dm_mine, dm_mine)


def _norm_fwd(xs, g, mod, tr, seg_tiles, name):
    t, d = xs.shape

    def body(x_ref, g_ref, mod_ref, h_ref):
        x = x_ref[...]
        r = lax.rsqrt(jnp.mean(x * x, axis=-1, keepdims=True) + RMS_EPS)
        y = (x * r) * g_ref[...]
        h_ref[...] = (y * (1.0 + mod_ref[1:2, :]) + mod_ref[0:1, :]).astype(BF16)

    return _pc(
        body, name=name, grid=(t // tr,),
        in_specs=[pl.BlockSpec((tr, d), lambda i: (i, 0)), pl.BlockSpec((1, d), lambda i: (0, 0)),
                  pl.BlockSpec((None, 8, d), lambda i: (i // seg_tiles, 0, 0))],
        out_specs=pl.BlockSpec((tr, d), lambda i: (i, 0)),
        out_shape=jax.ShapeDtypeStruct((t, d), BF16),
        compiler_params=_params("parallel"),
    )(xs, g, mod)


def _resid_grad(dx, i, seg_tiles, yx_ref, gate_ref, dyx_ref, gsum_ref):
    dyx_ref[...] = (dx * gate_ref[2:3, :]).astype(BF16)

    @pl.when(i % seg_tiles == 0)
    def _():
        gsum_ref[...] = jnp.zeros_like(gsum_ref)

    gsum_ref[0:1, :] += jnp.sum(dx * yx_ref[...], axis=0, keepdims=True)


def _norm_bwd(xs, dh, dres, g, mod, tr, seg_tiles, name, res_tiles=None, out_tiles=None, below=None):
    t, d = xs.shape
    n_tiles = t // tr
    res_tiles = n_tiles if res_tiles is None else res_tiles
    out_tiles = n_tiles if out_tiles is None else out_tiles

    def body(x_ref, dh_ref, dres_ref, g_ref, mod_ref, *rest):
        i = pl.program_id(0)
        x = x_ref[...]
        r = lax.rsqrt(jnp.mean(x * x, axis=-1, keepdims=True) + RMS_EPS)
        xn = x * r
        dhv = dh_ref[...]
        gain = g_ref[...]
        one_scale = 1.0 + mod_ref[1:2, :]
        dxn = dhv * (gain * one_scale)
        dx = r * (dxn - xn * jnp.mean(dxn * xn, axis=-1, keepdims=True))
        if res_tiles == n_tiles:
            dx = dx + dres_ref[...]
        else:
            dx = dx + jnp.where(i < res_tiles, dres_ref[...], 0.0)
        if below is None:
            dx_ref, sum_ref = rest
        else:
            yx_ref, gate_ref, dx_ref, sum_ref, dyx_ref, gsum_ref = rest
            _resid_grad(dx, i, seg_tiles, yx_ref, gate_ref, dyx_ref, gsum_ref)
        if out_tiles == n_tiles:
            dx_ref[...] = dx
        else:
            @pl.when(i < out_tiles)
            def _():
                dx_ref[...] = dx

        @pl.when(i % seg_tiles == 0)
        def _():
            sum_ref[...] = jnp.zeros_like(sum_ref)

        sum_ref[0:1, :] += jnp.sum(dhv, axis=0, keepdims=True)
        sum_ref[1:2, :] += jnp.sum(dhv * (xn * gain), axis=0, keepdims=True)
        sum_ref[2:3, :] += jnp.sum(dhv * one_scale * xn, axis=0, keepdims=True)

    row = pl.BlockSpec((tr, d), lambda i: (i, 0))
    seg = pl.BlockSpec((None, 8, d), lambda i: (i // seg_tiles, 0, 0))
    in_specs = [row, row, pl.BlockSpec((tr, d), lambda i: (jnp.minimum(i, res_tiles - 1), 0)),
                pl.BlockSpec((1, d), lambda i: (0, 0)), seg]
    out_specs = [pl.BlockSpec((tr, d), lambda i: (jnp.minimum(i, out_tiles - 1), 0)), seg]
    out_shape = [jax.ShapeDtypeStruct((out_tiles * tr, d), F32), jax.ShapeDtypeStruct((mod.shape[0], 8, d), F32)]
    args = [xs, dh, dres, g, mod]
    if below is not None:
        in_specs += [row, seg]
        out_specs += [row, seg]
        out_shape += [jax.ShapeDtypeStruct((t, d), BF16), jax.ShapeDtypeStruct((below[1].shape[0], 8, d), F32)]
        args += list(below)
    return _pc(
        body, name=name, grid=(n_tiles,), in_specs=in_specs, out_specs=out_specs, out_shape=out_shape,
        compiler_params=_params("arbitrary"),
    )(*args)


def _loss_head(xs, target, g, yx, mod, tr):
    t, d = xs.shape

    def body(x_ref, t_ref, g_ref, yx_ref, gate_ref, loss_ref, dx_ref, dg_ref, dyx_ref, gsum_ref):
        i = pl.program_id(0)
        x = x_ref[...]
        r = lax.rsqrt(jnp.mean(x * x, axis=-1, keepdims=True) + RMS_EPS)
        xn = x * r
        gain = g_ref[...]
        err = xn * gain - t_ref[...]
        dy = err * (1.0 / d)
        dxn = dy * gain
        dx = r * (dxn - xn * jnp.mean(dxn * xn, axis=-1, keepdims=True))
        dx_ref[...] = dx
        _resid_grad(dx, i, t // tr, yx_ref, gate_ref, dyx_ref, gsum_ref)

        @pl.when(i == 0)
        def _():
            loss_ref[...] = jnp.zeros_like(loss_ref)
            dg_ref[...] = jnp.zeros_like(dg_ref)

        loss_ref[...] += 0.5 * jnp.sum(jnp.mean(err * err, axis=-1, keepdims=True))
        dg_ref[0:1, :] += jnp.sum(dy * xn, axis=0, keepdims=True)

    row = pl.BlockSpec((tr, d), lambda i: (i, 0))
    seg = pl.BlockSpec((None, 8, d), lambda i: (0, 0, 0))
    return _pc(
        body, name="loss_head", grid=(t // tr,),
        in_specs=[row, row, pl.BlockSpec((1, d), lambda i: (0, 0)), row, seg],
        out_specs=[pl.BlockSpec((8, LANES), lambda i: (0, 0)), row, pl.BlockSpec((8, d), lambda i: (0, 0)), row, seg],
        out_shape=[jax.ShapeDtypeStruct((8, LANES), F32), jax.ShapeDtypeStruct((t, d), F32),
                   jax.ShapeDtypeStruct((8, d), F32), jax.ShapeDtypeStruct((t, d), BF16),
                   jax.ShapeDtypeStruct((1, 8, d), F32)],
        compiler_params=_params("arbitrary"),
    )(xs, target, g, yx, mod)


def _proj_in(h, w, layer, width, name, blocks=None, dtype=F32):
    t, d = h.shape
    n8 = w.shape[-1]
    first, count = blocks if blocks is not None else (0, N_DEV)
    per_part = width // n8
    tm = _tile(t, 1152)

    def body(a_ref, b_ref, o_ref):
        o_ref[...] = _dot(a_ref[...], b_ref[...]).astype(dtype)

    return _pc(
        body, name=name, grid=(t // tm, count),
        in_specs=[pl.BlockSpec((tm, d), lambda i, j: (i, 0)),
                  pl.BlockSpec((None, None, d, n8), lambda i, j: (first + j, layer, 0, 0))],
        out_specs=pl.BlockSpec((None, tm, n8), lambda i, j: (j // per_part, i, j % per_part)),
        out_shape=jax.ShapeDtypeStruct((count // per_part, t, width), dtype),
        compiler_params=_params("parallel", "parallel"),
    )(h, w)


def _proj_out(z, w, res, mod, tm, seg_tiles, name, nxt=None):
    t, k = z.shape
    d = w.shape[1]

    def body(z_ref, w_ref, res_ref, mod_ref, *rest):
        yx = _dot(z_ref[...], w_ref[...])
        x = res_ref[...] + mod_ref[2:3, :] * yx
        if nxt is None:
            yx_ref, x_ref = rest
        else:
            g_ref, nmod_ref, yx_ref, x_ref, h_ref = rest
            r = lax.rsqrt(jnp.mean(x * x, axis=-1, keepdims=True) + RMS_EPS)
            h_ref[...] = (((x * r) * g_ref[...]) * (1.0 + nmod_ref[1:2, :]) + nmod_ref[0:1, :]).astype(BF16)
        yx_ref[...] = yx
        x_ref[...] = x

    tile = pl.BlockSpec((tm, d), lambda i: (i, 0))
    seg = pl.BlockSpec((None, 8, d), lambda i: (i // seg_tiles, 0, 0))
    in_specs = [pl.BlockSpec((tm, k), lambda i: (i, 0)), pl.BlockSpec((k, d), lambda i: (0, 0)), tile, seg]
    out_specs = [tile, tile]
    out_shape = [jax.ShapeDtypeStruct((t, d), F32), jax.ShapeDtypeStruct((t, d), F32)]
    args = [z, w, res, mod]
    if nxt is not None:
        in_specs += [pl.BlockSpec((1, d), lambda i: (0, 0)), seg]
        out_specs.append(tile)
        out_shape.append(jax.ShapeDtypeStruct((t, d), BF16))
        args += list(nxt)
    return _pc(
        body, name=name, grid=(t // tm,), in_specs=in_specs, out_specs=out_specs, out_shape=out_shape,
        compiler_params=_params("parallel"),
    )(*args)


def _proj_out_dz(dyx, w, name):
    t, d = dyx.shape
    width = w.shape[0]
    tm, tn = _tile(t, 1024), _tile(width, 512)

    def body(a_ref, w_ref, o_ref):
        o_ref[...] = _dot(a_ref[...], w_ref[...], 1, 1)

    return _pc(
        body, name=name, grid=(t // tm, width // tn),
        in_specs=[pl.BlockSpec((tm, d), lambda i, j: (i, 0)), pl.BlockSpec((tn, d), lambda i, j: (j, 0))],
        out_specs=pl.BlockSpec((tm, tn), lambda i, j: (i, j)),
        out_shape=jax.ShapeDtypeStruct((t, width), F32),
        compiler_params=_params("parallel", "parallel"),
    )(dyx, w)


def _proj_in_dh(dpre, w, layer, name, after=None):
    parts, t, width = dpre.shape
    d, n8 = w.shape[-2:]
    per_part = width // n8
    tm, tn = _tile(t, 1152), _tile(d, 512)

    def body(a_ref, w_ref, *rest):
        o_ref = rest[-1]
        part = _dot(a_ref[:, 0:n8], w_ref[0], 1, 1)
        for s in range(1, per_part):
            part += _dot(a_ref[:, s * n8:(s + 1) * n8], w_ref[s], 1, 1)

        @pl.when(pl.program_id(2) == 0)
        def _():
            o_ref[...] = part

        @pl.when(pl.program_id(2) != 0)
        def _():
            o_ref[...] += part

    extra = [] if after is None else [after]
    return _pc(
        body, name=name, grid=(t // tm, d // tn, parts),
        in_specs=[pl.BlockSpec((None, tm, width), lambda i, j, k: (k, i, 0)),
                  pl.BlockSpec((per_part, None, tn, n8), lambda i, j, k: (k, layer, j, 0))] + [ANY] * len(extra),
        out_specs=pl.BlockSpec((tm, tn), lambda i, j, k: (i, j)),
        out_shape=jax.ShapeDtypeStruct((t, d), F32),
        compiler_params=_params("parallel", "parallel", "arbitrary"),
    )(dpre, w, *extra)


def _transposed(a_ref):
    return a_ref[...].T


def _grad_w_in(h, dpre, n8, name):
    t, d = h.shape
    parts, _, width = dpre.shape
    per_part = width // n8
    tm, tk = _tile(d, 512), _tile(t, 1152)
    nk = t // tk

    def body(a_ref, b_ref, o_ref, acc_ref):
        k = pl.program_id(1)

        @pl.when(k == 0)
        def _():
            acc_ref[...] = jnp.zeros_like(acc_ref)

        at = _transposed(a_ref)
        for p in range(parts):
            r = _dot(at, b_ref[p])
            for s in range(per_part):
                acc_ref[p * per_part + s] += r[:, s * n8:(s + 1) * n8]

        @pl.when(k == nk - 1)
        def _():
            o_ref[...] = acc_ref[...].astype(BF16)

    return _pc(
        body, name=name, grid=(d // tm, nk),
        in_specs=[pl.BlockSpec((tk, tm), lambda i, k: (k, i)), pl.BlockSpec((parts, tk, width), lambda i, k: (0, k, 0))],
        out_specs=pl.BlockSpec((parts * per_part, tm, n8), lambda i, k: (0, i, 0)),
        out_shape=jax.ShapeDtypeStruct((parts * per_part, d, n8), BF16),
        scratch_shapes=[pltpu.VMEM((parts * per_part, tm, n8), F32)],
        compiler_params=_params("parallel", "arbitrary"),
    )(h, dpre)


def _grad_w_out(z, dyx, name):
    width = z.shape[1]
    t, d = dyx.shape
    tm, tk = _tile(width, 512), _tile(t, 1152)
    nk = t // tk

    def body(a_ref, b_ref, o_ref, acc_ref):
        k = pl.program_id(1)

        @pl.when(k == 0)
        def _():
            acc_ref[...] = jnp.zeros_like(acc_ref)

        acc_ref[...] += _dot(_transposed(a_ref), b_ref[...])

        @pl.when(k == nk - 1)
        def _():
            o_ref[...] = acc_ref[...].astype(BF16)

    return _pc(
        body, name=name, grid=(width // tm, nk),
        in_specs=[pl.BlockSpec((tk, tm), lambda i, k: (k, i)), pl.BlockSpec((tk, d), lambda i, k: (k, 0))],
        out_specs=pl.BlockSpec((tm, d), lambda i, k: (i, 0)),
        out_shape=jax.ShapeDtypeStruct((width, d), BF16),
        scratch_shapes=[pltpu.VMEM((tm, d), F32)],
        compiler_params=_params("parallel", "arbitrary"),
    )(z, dyx)


def _shift(v, k):
    n = v.shape[0]
    return pltpu.roll(v, k % n, 0)


def _window_sum(v, win):
    s = v + _shift(v, 1)
    step = 1
    while 2 * step < win:
        s = _shift(s, step) + _shift(s, -step)
        step *= 2
    return s


def _window_count(base, seg_len, win, shape):
    t = base + lax.broadcasted_iota(jnp.int32, shape, 0)
    hi = jnp.minimum(t + win // 2, seg_len)
    lo = jnp.maximum(t - win // 2, 0)
    return (hi - lo).astype(F32)


def _pad_offsets(segs):
    return [HALO * (s + 1) + st for s, (st, _) in enumerate(segs)]


def _for_chunks(segs, fn):
    offs = _pad_offsets(segs)
    for s, (st, ln) in enumerate(segs):
        def step(ci, carry, s=s, st=st, ln=ln):
            fn(s, st, ln, offs[s], pl.multiple_of(ci * CHUNK, CHUNK))
            return carry
        lax.fori_loop(0, ln // CHUNK, step, 0)


def _pool_fwd(pre, w_grp, scale, segs, name):
    _, t, width = pre.shape
    grp = width // len(POOL_WINDOWS)
    padded = t + HALO * (len(segs) + 1)

    def group(win, pre_ref, w_ref, sc_ref, z_ref, diff_ref, pad_ref):
        pad_ref[...] = jnp.zeros_like(pad_ref)

        def fill(s, st, ln, off, b):
            pad_ref[pl.ds(off + b, CHUNK), :] = pre_ref[0, pl.ds(st + b, CHUNK), :]

        _for_chunks(segs, fill)

        def mix(s, st, ln, off, b):
            ext = pad_ref[pl.ds(off - HALO + b, CHUNK + 2 * HALO), :]
            total = _window_sum(ext, win)[HALO:HALO + CHUNK]
            u = pre_ref[0, pl.ds(st + b, CHUNK), :]
            diff = (total / _window_count(b, ln, win, u.shape) - u).astype(BF16)
            mixed = _dot(diff, w_ref[...])
            gate = _silu(pre_ref[1, pl.ds(st + b, CHUNK), :])
            z_ref[pl.ds(st + b, CHUNK), :] = (mixed * sc_ref[...] * gate).astype(BF16)
            diff_ref[pl.ds(st + b, CHUNK), :] = diff

        _for_chunks(segs, mix)

    def body(pre_ref, w_ref, sc_ref, z_ref, diff_ref, pad_ref):
        gi = pl.program_id(0)
        for widx, win in enumerate(POOL_WINDOWS):
            @pl.when(gi == widx)
            def _(win=win):
                group(win, pre_ref, w_ref, sc_ref, z_ref, diff_ref, pad_ref)

    col = pl.BlockSpec((t, grp), lambda g: (0, g))
    return _pc(
        body, name=name, grid=(len(POOL_WINDOWS),),
        in_specs=[pl.BlockSpec((2, t, grp), lambda g: (0, 0, g)), pl.BlockSpec((None, grp, grp), lambda g: (g, 0, 0)),
                  pl.BlockSpec((1, grp), lambda g: (0, g))],
        out_specs=[col, col],
        out_shape=[jax.ShapeDtypeStruct((t, width), BF16), jax.ShapeDtypeStruct((t, width), BF16)],
        scratch_shapes=[pltpu.VMEM((padded, grp), F32)],
        compiler_params=_params("parallel"),
    )(pre, w_grp, scale)


def _pool_bwd(dz, diff, pre, w_grp, scale, segs, name):
    _, t, width = pre.shape
    grp = width // len(POOL_WINDOWS)
    padded = t + HALO * (len(segs) + 1)

    def group(win, dz_ref, diff_ref, pre_ref, w_ref, sc_ref, dpre_ref, dw_ref, dsc_ref, pad_ref, dd_ref):
        pad_ref[...] = jnp.zeros_like(pad_ref)
        dw_ref[...] = jnp.zeros_like(dw_ref)
        dsc_ref[...] = jnp.zeros_like(dsc_ref)

        def first(s, st, ln, off, b):
            rows = pl.ds(st + b, CHUNK)
            diff_v = diff_ref[rows, :]
            mixed = _dot(diff_v, w_ref[...])
            g = pre_ref[1, rows, :]
            sg = _silu(g)
            dzv = dz_ref[rows, :]
            dmixed = (dzv * sc_ref[...] * sg).astype(BF16)
            dsc_ref[...] += jnp.sum(dzv * mixed * sg, axis=0, keepdims=True)
            dpre_ref[1, rows, :] = (dzv * mixed * sc_ref[...] * _dsilu(g)).astype(BF16)
            ddiff = _dot(dmixed, w_ref[...], 1, 1)
            dw_ref[...] += _dot(diff_v, dmixed, 0, 0)
            dd_ref[rows, :] = ddiff
            pad_ref[pl.ds(off + b, CHUNK), :] = ddiff / _window_count(b, ln, win, ddiff.shape)

        _for_chunks(segs, first)

        def second(s, st, ln, off, b):
            rows = pl.ds(st + b, CHUNK)
            ext = pad_ref[pl.ds(off - HALO + b, CHUNK + 2 * HALO), :]
            total = _shift(_window_sum(ext, win), -1)[HALO:HALO + CHUNK]
            dpre_ref[0, rows, :] = (total - dd_ref[rows, :]).astype(BF16)

        _for_chunks(segs, second)

    def body(dz_ref, diff_ref, pre_ref, w_ref, sc_ref, dpre_ref, dw_ref, dsc_ref, pad_ref, dd_ref):
        gi = pl.program_id(0)
        for widx, win in enumerate(POOL_WINDOWS):
            @pl.when(gi == widx)
            def _(win=win):
                group(win, dz_ref, diff_ref, pre_ref, w_ref, sc_ref, dpre_ref, dw_ref, dsc_ref, pad_ref, dd_ref)

    col = pl.BlockSpec((t, grp), lambda g: (0, g))
    both = pl.BlockSpec((2, t, grp), lambda g: (0, 0, g))
    wspec = pl.BlockSpec((None, grp, grp), lambda g: (g, 0, 0))
    sspec = pl.BlockSpec((1, grp), lambda g: (0, g))
    return _pc(
        body, name=name, grid=(len(POOL_WINDOWS),),
        in_specs=[col, col, both, wspec, sspec],
        out_specs=[both, wspec, sspec],
        out_shape=[jax.ShapeDtypeStruct((2, t, width), BF16), jax.ShapeDtypeStruct((len(POOL_WINDOWS), grp, grp), F32),
                   jax.ShapeDtypeStruct((1, width), F32)],
        scratch_shapes=[pltpu.VMEM((padded, grp), F32), pltpu.VMEM((t, grp), F32)],
        compiler_params=_params("parallel"),
    )(dz, diff, pre, w_grp, scale)


def _conv_fwd(pre, dw, db, name):
    _, t, width = pre.shape
    cb = LANES
    segs = [(0, t)]

    def body(pre_ref, dw_ref, db_ref, z_ref, pad_ref):
        pad_ref[...] = jnp.zeros_like(pad_ref)

        def fill(s, st, ln, off, b):
            rows = pl.ds(b, CHUNK)
            pad_ref[pl.ds(off + b, CHUNK), :] = pre_ref[1, rows, :] * pre_ref[2, rows, :]

        _for_chunks(segs, fill)

        def mix(s, st, ln, off, b):
            rows = pl.ds(b, CHUNK)
            ext = pad_ref[pl.ds(off - HALO + b, CHUNK + 2 * HALO), :]
            conv = (dw_ref[0:1, :] * _shift(ext, 1) + dw_ref[1:2, :] * ext + dw_ref[2:3, :] * _shift(ext, -1))
            conv = conv[HALO:HALO + CHUNK] + db_ref[...]
            y = pre_ref[0, rows, :] * conv
            z_ref[rows, :] = (y * _silu(pre_ref[3, rows, :])).astype(BF16)

        _for_chunks(segs, mix)

    return _pc(
        body, name=name, grid=(width // cb,),
        in_specs=[pl.BlockSpec((4, t, cb), lambda j: (0, 0, j)), pl.BlockSpec((8, cb), lambda j: (0, j)),
                  pl.BlockSpec((1, cb), lambda j: (0, j))],
        out_specs=pl.BlockSpec((t, cb), lambda j: (0, j)),
        out_shape=jax.ShapeDtypeStruct((t, width), BF16),
        scratch_shapes=[pltpu.VMEM((t + 2 * HALO, cb), F32)],
        compiler_params=_params("parallel"),
    )(pre, dw, db)


def _conv_bwd(dz, pre, dw, db, name):
    _, t, width = pre.shape
    cb = LANES
    segs = [(0, t)]

    def body(dz_ref, pre_ref, dw_ref, db_ref, dpre_ref, ddw_ref, ddb_ref, pad_a, pad_c):
        pad_a[...] = jnp.zeros_like(pad_a)
        pad_c[...] = jnp.zeros_like(pad_c)
        ddw_ref[...] = jnp.zeros_like(ddw_ref)
        ddb_ref[...] = jnp.zeros_like(ddb_ref)

        def fill(s, st, ln, off, b):
            rows = pl.ds(b, CHUNK)
            pad_a[pl.ds(off + b, CHUNK), :] = pre_ref[1, rows, :] * pre_ref[2, rows, :]

        _for_chunks(segs, fill)

        def first(s, st, ln, off, b):
            rows = pl.ds(b, CHUNK)
            ext = pad_a[pl.ds(off - HALO + b, CHUNK + 2 * HALO), :]
            prev, nxt = _shift(ext, 1)[HALO:HALO + CHUNK], _shift(ext, -1)[HALO:HALO + CHUNK]
            here = ext[HALO:HALO + CHUNK]
            conv = dw_ref[0:1, :] * prev + dw_ref[1:2, :] * here + dw_ref[2:3, :] * nxt + db_ref[...]
            bg, g = pre_ref[0, rows, :], pre_ref[3, rows, :]
            dzv = dz_ref[rows, :]
            dy = dzv * _silu(g)
            dpre_ref[3, rows, :] = (dzv * (bg * conv) * _dsilu(g)).astype(BF16)
            dpre_ref[0, rows, :] = (dy * conv).astype(BF16)
            dconv = dy * bg
            pad_c[pl.ds(off + b, CHUNK), :] = dconv
            ddw_ref[0:1, :] += jnp.sum(dconv * prev, axis=0, keepdims=True)
            ddw_ref[1:2, :] += jnp.sum(dconv * here, axis=0, keepdims=True)
            ddw_ref[2:3, :] += jnp.sum(dconv * nxt, axis=0, keepdims=True)
            ddb_ref[0:1, :] += jnp.sum(dconv, axis=0, keepdims=True)

        _for_chunks(segs, first)

        def second(s, st, ln, off, b):
            rows = pl.ds(b, CHUNK)
            ext = pad_c[pl.ds(off - HALO + b, CHUNK + 2 * HALO), :]
            da = (dw_ref[0:1, :] * _shift(ext, -1) + dw_ref[1:2, :] * ext + dw_ref[2:3, :] * _shift(ext, 1))
            da = da[HALO:HALO + CHUNK]
            dpre_ref[1, rows, :] = (da * pre_ref[2, rows, :]).astype(BF16)
            dpre_ref[2, rows, :] = (da * pre_ref[1, rows, :]).astype(BF16)

        _for_chunks(segs, second)

    quad = pl.BlockSpec((4, t, cb), lambda j: (0, 0, j))
    rows8 = pl.BlockSpec((8, cb), lambda j: (0, j))
    return _pc(
        body, name=name, grid=(width // cb,),
        in_specs=[pl.BlockSpec((t, cb), lambda j: (0, j)), quad, rows8, pl.BlockSpec((1, cb), lambda j: (0, j))],
        out_specs=[quad, rows8, rows8],
        out_shape=[jax.ShapeDtypeStruct((4, t, width), BF16), jax.ShapeDtypeStruct((8, width), F32),
                   jax.ShapeDtypeStruct((8, width), F32)],
        scratch_shapes=[pltpu.VMEM((t + 2 * HALO, cb), F32), pltpu.VMEM((t + 2 * HALO, cb), F32)],
        compiler_params=_params("parallel"),
    )(dz, pre, dw, db)


PAIR_TILES = 2 * WIN_ROWS - 2


def _pair_geometry():
    lane = lax.broadcasted_iota(jnp.int32, (GRID_W, LANES), 1)
    qcol = lax.broadcasted_iota(jnp.int32, (GRID_W, LANES), 0)
    low = lane < GRID_W
    kcol = jnp.where(low, lane, lane - GRID_W)
    start = jnp.clip(qcol - WIN_COLS // 2, 0, GRID_W - WIN_COLS)
    inside = (kcol >= start) & (kcol < start + WIN_COLS)
    return low, inside


def _bias_tiles(rpb_ref, rows_ref, tiles_ref, inside):
    for h in range(2):
        rows = rpb_ref[h]
        rows_ref[h] = (pltpu.roll(rows, LANES - (WIN_COLS - 1), 1)
                       + pltpu.roll(pltpu.roll(rows, GRID_W - (WIN_COLS - 1), 1), 2 * WIN_ROWS - 1, 0))
        for t in range(PAIR_TILES):
            both = jnp.broadcast_to(rows_ref[h, t:t + 1, :], (GRID_W, LANES))
            tiles_ref[h, t] = jnp.where(inside, pltpu.roll(both, 0, 1, stride=1, stride_axis=0), MASKED)


def _bias_tiles_grad(dtiles_ref, drpb_ref):
    n = PAIR_TILES * GRID_W
    qcol = lax.broadcasted_iota(jnp.int32, (n, LANES), 0) & (GRID_W - 1)
    lane = lax.broadcasted_iota(jnp.int32, (1, LANES), 1)
    zero = jnp.zeros((1, LANES), F32)
    for h in range(2):
        v = pltpu.roll(dtiles_ref[h].reshape(n, LANES), WIN_COLS - 1, 1)
        for bit in range(6):
            v = jnp.where((qcol >> bit) & 1 == 1, pltpu.roll(v, LANES - (1 << bit), 1), v)
        sums = [jnp.sum(v[t * GRID_W:(t + 1) * GRID_W], axis=0, keepdims=True) for t in range(PAIR_TILES)]
        for r in range(2 * WIN_ROWS):
            here = sums[r] if r < PAIR_TILES else zero
            prev = pltpu.roll(sums[r - 1], GRID_W, 1) if 1 <= r <= PAIR_TILES else zero
            drpb_ref[h, r:r + 1, :] = jnp.where(lane < 2 * WIN_COLS - 1, here + prev, 0.0)


def _attn_rows(r, n_rows):
    first = jnp.clip(r - WIN_ROWS // 2, 0, n_rows - WIN_ROWS)
    return first, first - r + WIN_ROWS - 1


def _softmax(s_loc, s_ctx):
    m = jnp.maximum(jnp.max(s_loc, axis=-1, keepdims=True), jnp.max(s_ctx, axis=-1, keepdims=True))
    e_loc, e_ctx = jnp.exp(s_loc - m), jnp.exp(s_ctx - m)
    inv = 1.0 / (jnp.sum(e_loc, axis=-1, keepdims=True) + jnp.sum(e_ctx, axis=-1, keepdims=True))
    return e_loc * inv, e_ctx * inv


def _pair_bias(tiles_ref, j):
    return jnp.concatenate(
        [jnp.concatenate([tiles_ref[h, j + 2 * m] for m in range(WIN_ROWS // 2)], axis=1) for h in range(2)], axis=0)


ROWS_PER_STEP = 4


def _by_head(tile, low):
    zero = jnp.zeros_like(tile)
    return jnp.concatenate([jnp.where(low, tile, zero), jnp.where(low, zero, tile)], axis=0)


def _merge_heads(stacked, low):
    return jnp.where(low, stacked[:GRID_W], stacked[GRID_W:])


def _attn_items(step, n_rows, q_ref, low):
    items = []
    for u in range(ROWS_PER_STEP):
        r = step * ROWS_PER_STEP + u
        first, j = _attn_rows(r, n_rows)
        rows = pl.ds(pl.multiple_of(r * GRID_W, GRID_W), GRID_W)
        keys = pl.ds(pl.multiple_of(first * GRID_W, GRID_W), WIN_ROWS * GRID_W)
        q = (q_ref[rows, :].astype(F32) * HEAD_DIM ** -0.5).astype(BF16)
        items.append((rows, keys, j, _by_head(q, low)))
    return items


def _attn_fwd(qkv, gate, rpb, seq):
    _, t, width = qkv.shape
    n_rows = seq // GRID_W
    n_ctx = t - seq
    blk = WIN_ROWS * GRID_W

    def body(q_ref, k_ref, v_ref, g_ref, rpb_ref, z_ref, o_ref, rows_ref, tiles_ref):
        low, inside = _pair_geometry()
        _bias_tiles(rpb_ref, rows_ref, tiles_ref, inside)
        ctx = pl.ds(seq, n_ctx)

        def step(i, carry):
            items = _attn_items(i, n_rows, q_ref, low)
            k_ctx, v_ctx = k_ref[ctx, :], v_ref[ctx, :]
            scores = [(_dot(q, k_ref[keys, :], 1, 1) + _pair_bias(tiles_ref, j), _dot(q, k_ctx, 1, 1))
                      for _, keys, j, q in items]
            probs = [_softmax(s_loc, s_ctx) for s_loc, s_ctx in scores]
            outs = [_dot(p_loc.astype(BF16), v_ref[keys, :]) + _dot(p_ctx.astype(BF16), v_ctx)
                    for (_, keys, _, _), (p_loc, p_ctx) in zip(items, probs)]
            for (rows, _, _, _), out in zip(items, outs):
                o = _merge_heads(out, low)
                o_ref[rows, :] = o
                z_ref[rows, :] = (o * _silu(g_ref[rows, :])).astype(BF16)
            return carry

        lax.fori_loop(0, n_rows // ROWS_PER_STEP, step, 0)

    def part(p):
        return pl.BlockSpec((None, t, LANES), lambda h: (p, 0, h))

    out = pl.BlockSpec((seq, LANES), lambda h: (0, h))
    return _pc(
        body, name="attn_fwd", grid=(width // LANES,),
        in_specs=[part(0), part(1), part(2), part(0), pl.BlockSpec((2, 2 * WIN_ROWS, LANES), lambda h: (h, 0, 0))],
        out_specs=[out, out],
        out_shape=[jax.ShapeDtypeStruct((seq, width), BF16), jax.ShapeDtypeStruct((seq, width), F32)],
        scratch_shapes=[pltpu.VMEM((2, 2 * WIN_ROWS, LANES), F32), pltpu.VMEM((2, PAIR_TILES, GRID_W, LANES), F32)],
        compiler_params=_params("parallel"),
    )(qkv, qkv, qkv, gate, rpb)


def _attn_bwd(qkv, gate, o, dz, rpb, seq):
    _, t, width = qkv.shape
    n_rows = seq // GRID_W
    n_ctx = t - seq
    blk = WIN_ROWS * GRID_W
    heads = 2 * width // LANES

    def body(q_ref, k_ref, v_ref, g_ref, o_ref, dz_ref, rpb_ref, dpre_ref, drpb_ref,
             rows_ref, tiles_ref, dtiles_ref, dk_ref, dv_ref):
        low, inside = _pair_geometry()
        _bias_tiles(rpb_ref, rows_ref, tiles_ref, inside)
        dtiles_ref[...] = jnp.zeros_like(dtiles_ref)
        dk_ref[...] = jnp.zeros_like(dk_ref)
        dv_ref[...] = jnp.zeros_like(dv_ref)
        ctx = pl.ds(seq, n_ctx)

        def step(i, carry):
            items = _attn_items(i, n_rows, q_ref, low)
            k_ctx, v_ctx = k_ref[ctx, :], v_ref[ctx, :]
            d_outs = []
            for rows, _, _, _ in items:
                g = g_ref[rows, :]
                dzv = dz_ref[rows, :]
                dpre_ref[3, rows, :] = (dzv * o_ref[rows, :] * _dsilu(g)).astype(BF16)
                d_outs.append(_by_head((dzv * _silu(g)).astype(BF16), low))
            scores = [(_dot(q, k_ref[keys, :], 1, 1) + _pair_bias(tiles_ref, j), _dot(q, k_ctx, 1, 1))
                      for _, keys, j, q in items]
            dprobs = [(_dot(d_o, v_ref[keys, :], 1, 1), _dot(d_o, v_ctx, 1, 1))
                      for (_, keys, _, _), d_o in zip(items, d_outs)]
            probs = [_softmax(s_loc, s_ctx) for s_loc, s_ctx in scores]
            dscores = []
            for (p_loc, p_ctx), (dp_loc, dp_ctx) in zip(probs, dprobs):
                delta = (jnp.sum(p_loc * dp_loc, axis=-1, keepdims=True)
                         + jnp.sum(p_ctx * dp_ctx, axis=-1, keepdims=True))
                dscores.append((p_loc * (dp_loc - delta), p_ctx * (dp_ctx - delta)))
            dqs = [_dot(ds_loc.astype(BF16), k_ref[keys, :]) + _dot(ds_ctx.astype(BF16), k_ctx)
                   for (_, keys, _, _), (ds_loc, ds_ctx) in zip(items, dscores)]
            for (rows, _, _, _), dq in zip(items, dqs):
                dpre_ref[0, rows, :] = (_merge_heads(dq, low) * HEAD_DIM ** -0.5).astype(BF16)
            for (_, keys, j, q), d_o, (p_loc, p_ctx), (ds_loc, ds_ctx) in zip(items, d_outs, probs, dscores):
                dk_ref[keys, :] += _dot(ds_loc.astype(BF16), q, 0, 0)
                dk_ref[ctx, :] += _dot(ds_ctx.astype(BF16), q, 0, 0)
                dv_ref[keys, :] += _dot(p_loc.astype(BF16), d_o, 0, 0)
                dv_ref[ctx, :] += _dot(p_ctx.astype(BF16), d_o, 0, 0)
                for h in range(2):
                    for m in range(WIN_ROWS // 2):
                        dtiles_ref[h, j + 2 * m] += ds_loc[h * GRID_W:(h + 1) * GRID_W, m * LANES:(m + 1) * LANES]
            return carry

        lax.fori_loop(0, n_rows // ROWS_PER_STEP, step, 0)
        dpre_ref[1] = dk_ref[...].astype(BF16)
        dpre_ref[2] = dv_ref[...].astype(BF16)
        dpre_ref[0, ctx, :] = jnp.zeros((n_ctx, LANES), BF16)
        dpre_ref[3, ctx, :] = jnp.zeros((n_ctx, LANES), BF16)
        _bias_tiles_grad(dtiles_ref, drpb_ref)

    def part(p):
        return pl.BlockSpec((None, t, LANES), lambda h: (p, 0, h))

    lat = pl.BlockSpec((seq, LANES), lambda h: (0, h))
    rspec = pl.BlockSpec((2, 2 * WIN_ROWS, LANES), lambda h: (h, 0, 0))
    tiles = pltpu.VMEM((2, PAIR_TILES, GRID_W, LANES), F32)
    return _pc(
        body, name="attn_bwd", grid=(width // LANES,),
        in_specs=[part(0), part(1), part(2), part(0), lat, lat, rspec],
        out_specs=[pl.BlockSpec((4, t, LANES), lambda h: (0, 0, h)), rspec],
        out_shape=[jax.ShapeDtypeStruct((4, t, width), BF16), jax.ShapeDtypeStruct((heads, 2 * WIN_ROWS, LANES), F32)],
        scratch_shapes=[pltpu.VMEM((2, 2 * WIN_ROWS, LANES), F32), tiles, tiles,
                        pltpu.VMEM((t, LANES), F32), pltpu.VMEM((t, LANES), F32)],
        compiler_params=_params("parallel"),
    )(qkv, qkv, qkv, gate, o, dz, rpb)


def _adam_update(w, m, v, g):
    m2 = ADAM_B1 * m + (1.0 - ADAM_B1) * g
    v2 = ADAM_B2 * v + (1.0 - ADAM_B2) * (g * g)
    m_hat = m2 / (1.0 - ADAM_B1 ** ADAM_STEP)
    v_hat = v2 / (1.0 - ADAM_B2 ** ADAM_STEP)
    return -ADAM_LR * (m_hat / (jnp.sqrt(v_hat) + ADAM_EPS) + ADAM_WD * w), m2, v2


def _adamw(w, m, v, parts, name):
    rows, cols = w.shape
    tr = _tile(rows, max(8, 131072 // cols), 8)
    n_parts = len(parts)

    def body(*refs):
        w_ref, m_ref, v_ref = refs[:3]
        part_refs = refs[3:3 + n_parts]
        g_ref, d_ref, nm_ref, nv_ref = refs[3 + n_parts:]
        g = part_refs[0][...].astype(F32)
        for p in part_refs[1:]:
            g = g + p[...].astype(F32)
        g_ref[...] = g
        d_ref[...], nm_ref[...], nv_ref[...] = _adam_update(w_ref[...], m_ref[...], v_ref[...], g)

    tile = pl.BlockSpec((tr, cols), lambda i: (i, 0))
    in_specs, args = [tile, tile, tile], [w, m, v]
    for p in parts:
        if isinstance(p, tuple):
            arr, k = p
            in_specs.append(pl.BlockSpec((None, tr, cols), lambda i, k=k: (k, i, 0)))
            args.append(arr)
        else:
            in_specs.append(tile)
            args.append(p)
    shape = jax.ShapeDtypeStruct((rows, cols), F32)
    return _pc(
        body, name=name, grid=(rows // tr,), in_specs=in_specs, out_specs=[tile] * 4, out_shape=[shape] * 4,
        compiler_params=_params("parallel"),
    )(*args)


def _adamw_layers(w, m, v, landed, name):
    n_layers, rows, cols = w.shape
    tr = _tile(rows, max(8, 131072 // cols), 8)

    def body(*refs):
        w_ref, m_ref, v_ref = refs[:3]
        part_refs = refs[3:3 + n_layers * N_DEV]
        g_ref, d_ref, nm_ref, nv_ref = refs[3 + n_layers * N_DEV:]
        layer = pl.program_id(0)
        g = None
        for l in range(n_layers):
            s = part_refs[l * N_DEV][...].astype(F32)
            for p in part_refs[l * N_DEV + 1:(l + 1) * N_DEV]:
                s = s + p[...].astype(F32)
            g = s if g is None else jnp.where(layer == l, s, g)
        g_ref[...] = g
        d_ref[...], nm_ref[...], nv_ref[...] = _adam_update(w_ref[...], m_ref[...], v_ref[...], g)

    tile = pl.BlockSpec((None, tr, cols), lambda l, i: (l, i, 0))
    in_specs, args = [tile, tile, tile], [w, m, v]
    for l, arr in enumerate(landed):
        for k in range(N_DEV):
            in_specs.append(pl.BlockSpec((None, tr, cols), lambda ll, i, l=l, k=k: (k, jnp.where(ll == l, i, 0), 0)))
            args.append(arr)
    shape = jax.ShapeDtypeStruct(w.shape, F32)
    return _pc(
        body, name=name, grid=(n_layers, rows // tr), in_specs=in_specs, out_specs=[tile] * 4, out_shape=[shape] * 4,
        compiler_params=_params("arbitrary", "arbitrary"),
    )(*args)


def _adamw_small(states, grads):
    sources, makers = grads
    n, ns = len(states), len(sources)

    def body(*refs):
        src = refs[:ns]
        ins = refs[ns:ns + 3 * n]
        outs = refs[ns + 3 * n:]
        for k in range(n):
            w_ref, m_ref, v_ref = ins[3 * k:3 * k + 3]
            g = makers[k](*src)
            outs[4 * k][...] = g
            outs[4 * k + 1][...], outs[4 * k + 2][...], outs[4 * k + 3][...] = _adam_update(
                w_ref[...], m_ref[...], v_ref[...], g)

    flat = [a for s in states for a in s]
    vmem = pl.BlockSpec(memory_space=pltpu.VMEM)
    res = _pc(
        body, name="adamw_small",
        in_specs=[vmem] * (ns + 3 * n), out_specs=[vmem] * (4 * n),
        out_shape=[jax.ShapeDtypeStruct(s[0].shape, F32) for s in states for _ in range(4)],
        compiler_params=pltpu.CompilerParams(vmem_limit_bytes=VMEM_LIMIT),
    )(*sources, *flat)
    return [res[4 * k:4 * k + 4] for k in range(n)]


def _rows128(a):
    flat = a.reshape(-1)
    pad = (-flat.shape[0]) % LANES
    if pad:
        flat = jnp.concatenate([flat, jnp.zeros((pad,), flat.dtype)])
    return flat.reshape(-1, LANES)


def _pad_rows(a, mult=8):
    pad = (-a.shape[0]) % mult
    if pad:
        a = jnp.concatenate([a, jnp.zeros((pad,) + a.shape[1:], a.dtype)], axis=0)
    return a


def kernel(x, c, ctx, c_ctx, norm_g, ada_w, ada_b, pool_w_in, pool_w_grp, pool_scale, pool_w_out, na_w_in, na_rpb, na_w_out, conv_w_in, conv_dw, conv_db, conv_w_out, final_g, loss_target, m_c_ctx, m_norm_g, m_ada_w, m_ada_b, m_pool_w_in, m_pool_w_grp, m_pool_scale, m_pool_w_out, m_na_w_in, m_na_rpb, m_na_w_out, m_conv_w_in, m_conv_dw, m_conv_db, m_conv_w_out, m_final_g, v_c_ctx, v_norm_g, v_ada_w, v_ada_b, v_pool_w_in, v_pool_w_grp, v_pool_scale, v_pool_w_out, v_na_w_in, v_na_rpb, v_na_w_out, v_conv_w_in, v_conv_dw, v_conv_db, v_conv_w_out, v_final_g):
    xi, yi, ci = _my_place()
    me = 4 * xi + 2 * yi + ci
    seq, d = x.shape[1], x.shape[2]
    n_ctx = ctx.shape[1]
    t_all = seq + n_ctx
    width = d
    depth = norm_g.shape[0]
    nb = ada_w.shape[2]
    shard = width // N_DEV
    d_rows = d // LANES
    tr = math.gcd(math.gcd(seq, n_ctx), 256)
    x_tiles = seq // tr

    n_pool = pool_scale.shape[0]
    n_grp = pool_w_grp.shape[1]
    grp = width // n_grp

    small_in = _pad_rows(jnp.concatenate([_rows128(c), pool_scale, conv_dw[0], conv_db], axis=0))
    got = _gather_small(small_in, "gather_inputs")
    r0 = d_rows
    c_all = got[:, :r0].reshape(N_DEV, d)
    scale_full = got[:, r0:r0 + n_pool].transpose(1, 0, 2).reshape(n_pool, width)
    r1 = r0 + n_pool
    taps_full = _pad_rows(got[:, r1:r1 + 3].transpose(1, 0, 2).reshape(3, width))
    bias_full = got[:, r1 + 3:r1 + 4].transpose(1, 0, 2).reshape(1, width)

    cond = jnp.concatenate([c_all, c_ctx[None], jnp.zeros((7, d), F32)], axis=0)
    bias_mine = lax.dynamic_slice(ada_b, (0, me * nb), (depth, nb))
    mod_mine = _mod_fwd(cond, ada_w, bias_mine)
    mod_all = _gather_small(mod_mine.reshape(-1, LANES), "gather_mod")
    mod_all = mod_all.reshape(N_DEV, depth, 16, nb).transpose(1, 2, 0, 3).reshape(depth, 16, 3 * d)
    mod_x = lax.dynamic_index_in_dim(mod_all, me, 1, keepdims=False).reshape(depth, 3, d)
    mod_c = mod_all[:, 8].reshape(depth, 3, d)
    pad5 = jnp.zeros((depth, 5, d), F32)
    mod_x = jnp.concatenate([mod_x, pad5], axis=1)
    mod_c = jnp.concatenate([mod_c, pad5], axis=1)
    mods = [jnp.stack([mod_x[i], mod_c[i]]) if i < 2 else mod_x[i][None] for i in range(depth)]

    layer_weights = [[pool_w_grp[0], pool_w_out[0]], [na_w_out[0]],
                     [conv_w_in[0], conv_w_out[0]], [pool_w_in[1], pool_w_grp[1], pool_w_out[1]]]
    two_level = {0: pool_w_in[0], 1: na_w_in[0]}
    in_flight, first_level, token = [], {}, mod_c
    for i, ws in enumerate(layer_weights):
        if i in two_level:
            first_level[i] = _exchange_start([two_level[i].astype(BF16)], False, token, f"weights_start{i}_chips",
                                             peers=CHIP_PEERS)
            token = first_level[i][-1]
        state = _exchange_start([w.astype(BF16) for w in ws], False, token, f"weights_start{i}")
        token = state[-1]
        in_flight.append(state)

    def handed_on(i, after):
        half = _exchange_wait(first_level[i], False, after, f"weights_wait{i}_chips", peers=CHIP_PEERS)
        rest = _forward_start(half, after, f"weights_forward{i}")
        return _forward_wait(rest, rest[-1], f"weights_forward_wait{i}")[0][:, None]

    def landed_weight(i, t, after):
        return _exchange_wait(in_flight[i], False, after, f"weights_wait{i}_{t}", which=[t])[0]

    def as_in(w):
        return w[:, None]

    def as_grp(w):
        return w.transpose(1, 0, 2, 3).reshape(n_grp, grp, grp)

    def as_out(w):
        return w.reshape(width, d)

    both = [(0, seq), (seq, n_ctx)]
    latent = [(0, seq)]

    def grp_slots(g):
        return g.reshape(n_grp, N_DEV, grp // N_DEV, grp).transpose(1, 0, 2, 3).reshape(N_DEV, -1, grp).astype(BF16)

    def send_grads(i, grads):
        return _exchange_start(grads, True, jnp.zeros((8, LANES), F32), f"grads_start{i}")

    xs0 = jnp.concatenate([x[0], ctx[0]], axis=0)
    h0 = _norm_fwd(xs0, norm_g[0:1] + token[0, 0], mods[0], tr, x_tiles, "norm_fwd0")
    pool_in_w0 = handed_on(0, h0)
    pre0 = _proj_in(h0, pool_in_w0, 0, width, "proj_in0")
    pool_grp_w0 = as_grp(landed_weight(0, 0, pre0))
    z0, diff0 = _pool_fwd(pre0, pool_grp_w0, scale_full[0:1], both, "pool_fwd0")
    pool_out_w0 = as_out(landed_weight(0, 1, z0))
    yx0, xs1, h1 = _proj_out(z0, pool_out_w0, xs0, mods[0], tr, x_tiles, "proj_out0", nxt=(norm_g[1:2], mods[1]))

    na_in_w = handed_on(1, h1)
    per_part = width // na_w_in.shape[2]
    qkv1 = _proj_in(h1, na_in_w, 0, width, "proj_in1_qkv", blocks=(0, 3 * per_part), dtype=BF16)
    gpre1 = _proj_in(h1, na_in_w, 0, width, "proj_in1_gate", blocks=(3 * per_part, per_part))
    rpb_rows = jnp.pad(na_rpb[0], ((0, 0), (0, 2 * WIN_ROWS - na_rpb.shape[2]), (0, LANES - na_rpb.shape[3])))
    z1, o1 = _attn_fwd(qkv1, gpre1, rpb_rows, seq)
    na_out_w = as_out(landed_weight(1, 0, z1))
    yx1, x2, h2 = _proj_out(z1, na_out_w, xs1, mods[1], tr, x_tiles, "proj_out1", nxt=(norm_g[2:3], mods[2]))

    conv_in_w = as_in(landed_weight(2, 0, h2))
    pre2 = _proj_in(h2, conv_in_w, 0, width, "proj_in2")
    z2 = _conv_fwd(pre2, taps_full, bias_full, "conv_fwd")
    conv_out_w = as_out(landed_weight(2, 1, z2))
    yx2, x3, h3 = _proj_out(z2, conv_out_w, x2, mods[2], tr, x_tiles, "proj_out2", nxt=(norm_g[3:4], mods[3]))

    pool_in_w3 = as_in(landed_weight(3, 0, h3))
    pre3 = _proj_in(h3, pool_in_w3, 0, width, "proj_in3")
    pool_grp_w3 = as_grp(landed_weight(3, 1, pre3))
    z3, diff3 = _pool_fwd(pre3, pool_grp_w3, scale_full[1:2], latent, "pool_fwd3")
    pool_out_w3 = as_out(landed_weight(3, 2, z3))
    yx3, x4 = _proj_out(z3, pool_out_w3, x3, mods[3], tr, x_tiles, "proj_out3")

    loss_part, dx4, d_final, dyx3, gate3 = _loss_head(x4, loss_target[0], final_g[None], yx3, mods[3], tr)

    dz3 = _proj_out_dz(dyx3, pool_out_w3, "proj_out_dz3")
    g_pool_out1 = _grad_w_out(z3, dyx3, "grad_w_out3")
    dpre3, g_grp1, g_scale1 = _pool_bwd(dz3, diff3, pre3, pool_grp_w3, scale_full[1:2], latent, "pool_bwd3")
    dh3 = _proj_in_dh(dpre3, pool_in_w3, 0, "proj_in_dh3")
    g_pool_in1 = _grad_w_in(h3, dpre3, pool_w_in.shape[2], "grad_w_in3")
    sent3 = send_grads(3, [g_pool_in1, grp_slots(g_grp1), g_pool_out1.reshape(N_DEV, shard, d)])
    dx3, norm3, dyx2, gate2 = _norm_bwd(x3, dh3, dx4, norm_g[3:4] + sent3[-1][0, 0], mods[3], tr, x_tiles, "norm_bwd3",
                                        below=(yx2, mods[2]))

    dz2 = _proj_out_dz(dyx2, conv_out_w, "proj_out_dz2")
    g_conv_out = _grad_w_out(z2, dyx2, "grad_w_out2")
    dpre2, g_taps, g_cbias = _conv_bwd(dz2, pre2, taps_full, bias_full, "conv_bwd")
    dh2 = _proj_in_dh(dpre2, conv_in_w, 0, "proj_in_dh2")
    g_conv_in = _grad_w_in(h2, dpre2, conv_w_in.shape[2], "grad_w_in2")
    sent2 = send_grads(2, [g_conv_in, g_conv_out.reshape(N_DEV, shard, d)])
    dx2, norm2, dyx1, gate1 = _norm_bwd(x2, dh2, dx3, norm_g[2:3] + sent2[-1][0, 0], mods[2], tr, x_tiles, "norm_bwd2",
                                        below=(yx1, mods[1][:1]))

    dz1 = _proj_out_dz(dyx1, na_out_w, "proj_out_dz1")
    g_na_out = _grad_w_out(z1, dyx1, "grad_w_out1")
    dpre1, g_rpb = _attn_bwd(qkv1, gpre1, o1, dz1, rpb_rows, seq)
    g_rpb = g_rpb[:, :na_rpb.shape[2], :na_rpb.shape[3]]
    dh1 = _proj_in_dh(dpre1, na_in_w, 0, "proj_in_dh1")
    g_na_in = _grad_w_in(h1, dpre1, na_w_in.shape[2], "grad_w_in1")
    sent1 = send_grads(1, [g_na_in, g_na_out.reshape(N_DEV, shard, d)])
    dxs1, norm1, dyx0, gate0 = _norm_bwd(xs1, dh1, dx2, norm_g[1:2] + sent1[-1][0, 0], mods[1], tr, x_tiles, "norm_bwd1",
                                         res_tiles=x_tiles, below=(yx0, mods[0]))

    dz0 = _proj_out_dz(dyx0, pool_out_w0, "proj_out_dz0")
    g_pool_out0 = _grad_w_out(z0, dyx0, "grad_w_out0")
    sent0a = _exchange_start([g_pool_out0.reshape(N_DEV, shard, d)], True, jnp.zeros((8, LANES), F32), "grads_start0a")
    dpre0, g_grp0, g_scale0 = _pool_bwd(dz0, diff0, pre0, pool_grp_w0, scale_full[0:1], both, "pool_bwd0")
    g_pool_in0 = _grad_w_in(h0, dpre0, pool_w_in.shape[2], "grad_w_in0")
    sent0b = _exchange_start([g_pool_in0, grp_slots(g_grp0)], True, sent0a[-1], "grads_start0b")
    dh0 = _proj_in_dh(dpre0, pool_in_w0, 0, "proj_in_dh0", after=sent0b[-1])
    dx0, norm0 = _norm_bwd(xs0, dh0, dxs1, norm_g[0:1], mods[0], tr, x_tiles, "norm_bwd0", out_tiles=x_tiles)
    grad_x = dx0[None]

    norms, gates = [norm0, norm1, norm2, norm3], [gate0, gate1, gate2, gate3]
    zero_d = jnp.zeros((d,), F32)
    dm_rows = [jnp.concatenate([norms[i][0, 0], norms[i][0, 1], gates[i][0, 0]]) for i in range(depth)]
    dm_rows.append(jnp.concatenate([norm0[1, 0], norm0[1, 1], gate0[1, 0]]))
    dm_rows.append(jnp.concatenate([norm1[1, 0], norm1[1, 1], zero_d]))
    dm_local = jnp.stack(dm_rows + [jnp.zeros((3 * d,), F32)] * 2)
    g_norm_part = jnp.stack([norm0[0, 2] + norm0[1, 2], norm1[0, 2] + norm1[1, 2], norm2[0, 2], norm3[0, 2]])
    pieces = [_rows128(dm_local), _rows128(g_norm_part), _rows128(d_final[0]), _pad_rows(_rows128(g_rpb)), loss_part]
    marks = np.cumsum([0] + [p.shape[0] for p in pieces])
    by_owner = [a.reshape(-1, N_DEV, shard).transpose(1, 0, 2) for a in (g_scale0, g_scale1, g_taps[:3], g_cbias[0:1])]
    by_owner = jnp.concatenate(by_owner + [jnp.zeros((N_DEV, 8 - n_pool - 4, shard), F32)], axis=1)
    small_sent = _exchange_start([jnp.concatenate(pieces, axis=0), by_owner], [False, True], jnp.zeros((8, LANES), F32),
                                 "small_grads_start")

    def big(parts, w, m, v, name):
        shape = w.shape
        view = (-1, shape[-1])
        parts = [(parts.reshape((N_DEV,) + w.reshape(view).shape), k) for k in range(N_DEV)]
        return [r.reshape(shape) for r in _adamw(w.reshape(view), m.reshape(view), v.reshape(view), parts, name)]

    in3, grp3, out3 = _exchange_wait(sent3, True, small_sent[-1], "grads_wait3")
    in2, out2 = _exchange_wait(sent2, True, small_sent[-1], "grads_wait2")
    in1, out1 = _exchange_wait(sent1, True, small_sent[-1], "grads_wait1")
    res = {}
    res["na_w_in"] = [r[None] for r in big(in1, na_w_in[0], m_na_w_in[0], v_na_w_in[0], "adamw_na_in")]
    res["na_w_out"] = [r[None] for r in big(out1, na_w_out[0], m_na_w_out[0], v_na_w_out[0], "adamw_na_out")]
    res["conv_w_in"] = [r[None] for r in big(in2, conv_w_in[0], m_conv_w_in[0], v_conv_w_in[0], "adamw_conv_in")]
    res["conv_w_out"] = [r[None] for r in big(out2, conv_w_out[0], m_conv_w_out[0], v_conv_w_out[0], "adamw_conv_out")]

    done = [res[n][0] for n in ("na_w_in", "na_w_out", "conv_w_in", "conv_w_out")]
    small_out, owned = _exchange_wait(small_sent, [False, True], done, "small_grads_wait")
    loss = jnp.sum(small_out[:, marks[4], 0])
    dm_all = small_out[:, :marks[1]].reshape(N_DEV, 8, 3 * d).transpose(1, 0, 2)
    dm_mine = lax.dynamic_slice(dm_all, (0, 0, me * nb), (8, N_DEV, nb))
    g_ada_w, g_ada_b, cctx_part, dsilu_cond = _mod_bwd(cond, ada_w, dm_all, dm_mine)
    cctx_all = _gather_small(_rows128(cctx_part[0]), "gather_cctx")

    def summed(ref, lo, hi):
        g = ref[0, lo:hi, :]
        for k in range(1, N_DEV):
            g = g + ref[k, lo:hi, :]
        return g

    makers = [
        lambda so, ow, cc, ab, ds: summed(cc, 0, d_rows) * ds[...],
        lambda so, ow, cc, ab, ds: summed(so, marks[1], marks[2]),
        lambda so, ow, cc, ab, ds: ab[...],
        lambda so, ow, cc, ab, ds: summed(so, marks[2], marks[3]),
        lambda so, ow, cc, ab, ds: summed(so, marks[3], marks[4]),
        lambda so, ow, cc, ab, ds: summed(ow, 0, n_pool),
        lambda so, ow, cc, ab, ds: summed(ow, n_pool, n_pool + 3),
        lambda so, ow, cc, ab, ds: summed(ow, n_pool + 3, n_pool + 4),
    ]
    rpb_rows128 = lambda a: _pad_rows(_rows128(a))
    views = [_rows128] * 4 + [rpb_rows128] + [lambda a: a.reshape(-1, LANES)] * 3
    small = [(c_ctx, m_c_ctx, v_c_ctx), (norm_g, m_norm_g, v_norm_g), (ada_b, m_ada_b, v_ada_b),
             (final_g, m_final_g, v_final_g), (na_rpb, m_na_rpb, v_na_rpb), (pool_scale, m_pool_scale, v_pool_scale),
             (conv_dw, m_conv_dw, v_conv_dw), (conv_db, m_conv_db, v_conv_db)]
    states = [tuple(view(a) for a in triple) for view, triple in zip(views, small)]
    sources = (small_out, owned, cctx_all, _rows128(g_ada_b), _rows128(dsilu_cond[8]))
    small_res = _adamw_small(states, (sources, makers))
    names = ["c_ctx", "norm_g", "ada_b", "final_g", "na_rpb", "pool_scale", "conv_dw", "conv_db"]
    for name, (w, _, _), outs4 in zip(names, small, small_res):
        res[name] = [r.reshape(-1)[:w.size].reshape(w.shape) for r in outs4]

    res["ada_w"] = [r.reshape(ada_w.shape) for r in _adamw(
        ada_w.reshape(-1, nb), m_ada_w.reshape(-1, nb), v_ada_w.reshape(-1, nb), [g_ada_w.reshape(-1, nb)], "adamw_ada_w")]

    out0, = _exchange_wait(sent0a, True, small_res[0][0], "grads_wait0a")
    in0, grp0 = _exchange_wait(sent0b, True, small_res[0][0], "grads_wait0b")
    def both_layers(first, second, w, m, v, name):
        view = (w.shape[0], -1, w.shape[-1])
        landed = [first.reshape((N_DEV,) + w.reshape(view).shape[1:]), second.reshape((N_DEV,) + w.reshape(view).shape[1:])]
        return [r.reshape(w.shape) for r in _adamw_layers(w.reshape(view), m.reshape(view), v.reshape(view), landed, name)]

    res["pool_w_in"] = both_layers(in0, in3, pool_w_in, m_pool_w_in, v_pool_w_in, "adamw_pool_in")
    res["pool_w_grp"] = both_layers(grp0, grp3, pool_w_grp, m_pool_w_grp, v_pool_w_grp, "adamw_pool_grp")
    res["pool_w_out"] = both_layers(out0, out3, pool_w_out, m_pool_w_out, v_pool_w_out, "adamw_pool_out")

    order = ["c_ctx", "norm_g", "ada_w", "ada_b", "pool_w_in", "pool_w_grp", "pool_scale", "pool_w_out", "na_w_in",
             "na_rpb", "na_w_out", "conv_w_in", "conv_dw", "conv_db", "conv_w_out", "final_g"]
    outs = [loss, grad_x]
    for j in range(4):
        outs += [res[n][j] for n in order]
    return tuple(outs)
```

```python
import functools
import math

import numpy as np
import jax
import jax.numpy as jnp
from jax import lax
from jax.experimental import pallas as pl
from jax.experimental.pallas import tpu as pltpu

F32 = jnp.float32
BF16 = jnp.bfloat16
N_DEV = 8
LANES = 128
RMS_EPS = 1e-6
GRID_W = 64
WIN_ROWS = 8
WIN_COLS = 16
HEAD_DIM = 64
POOL_WINDOWS = (2, 4, 8, 16)
HALO = 8
CHUNK = 128
MASKED = -1e30
ADAM_LR = 0.001
ADAM_B1 = 0.9
ADAM_B2 = 0.999
ADAM_EPS = 1e-08
ADAM_WD = 0.01
ADAM_STEP = 10
VMEM_LIMIT = 56 * 1024 * 1024
MESH = pl.DeviceIdType.MESH
ANY = pl.BlockSpec(memory_space=pl.ANY)
HBM = pl.BlockSpec(memory_space=pltpu.HBM)
SEM = pl.BlockSpec(memory_space=pltpu.SEMAPHORE)
EFFECT = pltpu.SideEffectType.DATAFLOW_SIDE_EFFECTING


def _pc(body, *, name, **kw):
    return pl.pallas_call(body, name=name, **kw)


def _params(*sem):
    return pltpu.CompilerParams(dimension_semantics=sem if sem else None, vmem_limit_bytes=VMEM_LIMIT)


def _dot(a, b, ca=1, cb=0, precision=None):
    return lax.dot_general(a, b, (((ca,), (cb,)), ((), ())), preferred_element_type=F32, precision=precision)


def _tile(n, pref, unit=LANES):
    best = None
    for t in range(unit, min(n, pref) + 1, unit):
        if n % t == 0:
            best = t
    return best if best is not None else n


def _sigmoid(x):
    return 1.0 / (1.0 + jnp.exp(-x))


def _silu(x):
    return x * _sigmoid(x)


def _dsilu(x):
    s = _sigmoid(x)
    return s * (1.0 + x * (1.0 - s))


def _my_place():
    return lax.axis_index("x"), lax.axis_index("y"), lax.axis_index("c")


def _flip(v, f):
    return 1 - v if f else v


def _gather_small(block, name, per_dest=False):
    rows, cols = block.shape[-2:]

    def body(x_ref, out_ref, send_sems, recv_sems):
        x, y, c = _my_place()
        me = 4 * x + 2 * y + c
        out_ref[me] = x_ref[me] if per_dest else x_ref[...]
        copies = []
        for k in range(1, N_DEV):
            peer = (_flip(x, k & 4), _flip(y, k & 2), _flip(c, k & 1))
            dest = 4 * peer[0] + 2 * peer[1] + peer[2]
            cp = pltpu.make_async_remote_copy(
                src_ref=x_ref.at[dest] if per_dest else x_ref, dst_ref=out_ref.at[me],
                send_sem=send_sems.at[k - 1], recv_sem=recv_sems.at[k - 1], device_id=peer, device_id_type=MESH)
            cp.start()
            copies.append(cp)
        for cp in copies:
            cp.wait()

    return _pc(
        body, name=name,
        out_shape=jax.ShapeDtypeStruct((N_DEV, rows, cols), block.dtype),
        in_specs=[pl.BlockSpec(memory_space=pltpu.VMEM)],
        out_specs=pl.BlockSpec(memory_space=pltpu.VMEM),
        scratch_shapes=[pltpu.SemaphoreType.DMA((N_DEV - 1,)), pltpu.SemaphoreType.DMA((N_DEV - 1,))],
    )(block)


ALL_PEERS = tuple(range(N_DEV))
CHIP_PEERS = (0, 1, 2, 4, 6)
OTHER_CHIPS = (2, 4, 6)


def _flag(per_dest, t):
    return per_dest[t] if isinstance(per_dest, (list, tuple)) else per_dest


def _peer(k):
    x, y, c = _my_place()
    peer = (_flip(x, k & 4), _flip(y, k & 2), _flip(c, k & 1))
    return peer, 4 * peer[0] + 2 * peer[1] + peer[2]


def _peer_lists(peers, nt):
    return list(peers) if isinstance(peers, list) else [peers] * nt


def _exchange_copies(srcs, lands, send_sems, recv_sems, per_dest, peers=ALL_PEERS):
    x, y, c = _my_place()
    me = 4 * x + 2 * y + c
    copies = []
    for t, (src, land, ks) in enumerate(zip(srcs, lands, _peer_lists(peers, len(srcs)))):
        for k in ks:
            peer, dest = _peer(k)
            s = len(copies)
            copies.append(pltpu.make_async_remote_copy(
                src_ref=src.at[dest] if _flag(per_dest, t) else src, dst_ref=land.at[me],
                send_sem=send_sems[s], recv_sem=recv_sems[s], device_id=peer, device_id_type=MESH))
    return copies


def _forward_copies(lands, send_sems, recv_sems):
    sibling, _ = _peer(1)
    copies = []
    for t, land in enumerate(lands):
        for n, k in enumerate(OTHER_CHIPS):
            _, slot = _peer(k)
            s = t * len(OTHER_CHIPS) + n
            copies.append(pltpu.make_async_remote_copy(
                src_ref=land.at[slot], dst_ref=land.at[slot], send_sem=send_sems[s], recv_sem=recv_sems[s],
                device_id=sibling, device_id_type=MESH))
    return copies


def _forward_start(lands, after, name):
    nt = len(lands)
    ns = nt * len(OTHER_CHIPS)

    def body(*refs):
        ins, outs = refs[:nt + 1], refs[nt + 1:]
        for cp in _forward_copies(ins[:nt], outs[:ns], outs[ns:2 * ns]):
            cp.start()
        outs[-1][...] = jnp.zeros_like(outs[-1])

    res = _pc(
        body, name=name,
        out_shape=(*[pltpu.SemaphoreType.DMA(())] * (2 * ns), *[pltpu.HBM(a.shape, a.dtype) for a in lands],
                   jax.ShapeDtypeStruct((8, LANES), F32)),
        in_specs=[HBM] * nt + [ANY],
        out_specs=(*[SEM] * (2 * ns), *[HBM] * nt, pl.BlockSpec(memory_space=pltpu.VMEM)),
        input_output_aliases={i: 2 * ns + i for i in range(nt)},
        compiler_params=pltpu.CompilerParams(has_side_effects=EFFECT),
    )(*lands, after)
    return list(res[:ns]), list(res[ns:2 * ns]), list(res[2 * ns:2 * ns + nt]), res[-1]


def _forward_wait(state, after, name):
    send_sems, recv_sems, lands, _ = state
    nt, ns = len(lands), len(send_sems)

    def body(*refs):
        sems = refs[nt:nt + 2 * ns]
        for cp in _forward_copies(refs[:nt], sems[:ns], sems[ns:]):
            cp.wait_send()
            cp.wait_recv()

    res = _pc(
        body, name=name,
        out_shape=tuple(pltpu.HBM(a.shape, a.dtype) for a in lands),
        in_specs=[HBM] * nt + [SEM] * (2 * ns) + [ANY],
        out_specs=tuple([HBM] * nt),
        input_output_aliases={i: i for i in range(nt)},
        compiler_params=pltpu.CompilerParams(has_side_effects=EFFECT),
    )(*lands, *send_sems, *recv_sems, after)
    return list(res)


def _exchange_start(srcs, per_dest, after, name, peers=ALL_PEERS):
    nt = len(srcs)
    peers = _peer_lists(peers, nt)
    ns = sum(len(ks) for ks in peers)
    lands = [lax.empty((N_DEV,) + (s.shape[1:] if _flag(per_dest, t) else s.shape), s.dtype) for t, s in enumerate(srcs)]

    def body(*refs):
        ins, outs = refs[:2 * nt + 1], refs[2 * nt + 1:]
        for cp in _exchange_copies(ins[:nt], ins[nt:2 * nt], outs[:ns], outs[ns:2 * ns], per_dest, peers):
            cp.start()
        outs[-1][...] = jnp.zeros_like(outs[-1])

    hbm = [pltpu.with_memory_space_constraint(a, pltpu.HBM) for a in list(srcs) + lands]
    res = _pc(
        body, name=name,
        out_shape=(*[pltpu.SemaphoreType.DMA(())] * (2 * ns), *[pltpu.HBM(a.shape, a.dtype) for a in hbm],
                   jax.ShapeDtypeStruct((8, LANES), F32)),
        in_specs=[HBM] * (2 * nt) + [ANY],
        out_specs=(*[SEM] * (2 * ns), *[HBM] * (2 * nt), pl.BlockSpec(memory_space=pltpu.VMEM)),
        input_output_aliases={i: 2 * ns + i for i in range(2 * nt)},
        compiler_params=pltpu.CompilerParams(has_side_effects=EFFECT),
    )(*hbm, after)
    sems, rest = res[:2 * ns], res[2 * ns:]
    return list(sems[:ns]), list(sems[ns:]), list(rest[:nt]), list(rest[nt:2 * nt]), peers, rest[-1]


def _exchange_wait(state, per_dest, after, name, which=None):
    send_sems, recv_sems, srcs, lands, peers, _ = state
    which = list(range(len(srcs))) if which is None else which
    per_dest = [_flag(per_dest, t) for t in which]
    after = list(after) if isinstance(after, (list, tuple)) else [after]
    first = np.cumsum([0] + [len(ks) for ks in peers])
    pick = [first[t] + n for t in which for n in range(len(peers[t]))]
    peers = [peers[t] for t in which]
    send_sems, recv_sems = [send_sems[s] for s in pick], [recv_sems[s] for s in pick]
    srcs, lands = [srcs[t] for t in which], [lands[t] for t in which]
    nt = len(srcs)
    ns = len(send_sems)

    def body(*refs):
        sems = refs[2 * nt:2 * nt + 2 * ns]
        for cp in _exchange_copies(refs[:nt], refs[nt:2 * nt], sems[:ns], sems[ns:], per_dest, peers):
            cp.wait_send()
            cp.wait_recv()

    thru = list(srcs) + list(lands)
    res = _pc(
        body, name=name,
        out_shape=tuple(pltpu.HBM(a.shape, a.dtype) for a in thru),
        in_specs=[HBM] * (2 * nt) + [SEM] * (2 * ns) + [ANY] * len(after),
        out_specs=tuple([HBM] * (2 * nt)),
        input_output_aliases={i: i for i in range(2 * nt)},
        compiler_params=pltpu.CompilerParams(has_side_effects=EFFECT),
    )(*thru, *send_sems, *recv_sems, *after)
    return list(res[nt:])


def _mod_fwd(cond, ada_w, bias):
    depth, d, nb = ada_w.shape

    def body(c_ref, w_ref, b_ref, o_ref):
        s = _silu(c_ref[...]).astype(BF16)
        o_ref[...] = _dot(s, w_ref[...].astype(BF16)) + b_ref[...]

    return _pc(
        body, name="mod_fwd", grid=(depth,),
        in_specs=[pl.BlockSpec((16, d), lambda i: (0, 0)), pl.BlockSpec((None, d, nb), lambda i: (i, 0, 0)),
                  pl.BlockSpec((None, 1, nb), lambda i: (i, 0, 0))],
        out_specs=pl.BlockSpec((None, 16, nb), lambda i: (i, 0, 0)),
        out_shape=jax.ShapeDtypeStruct((depth, 16, nb), F32),
        compiler_params=_params("parallel"),
    )(cond, ada_w, bias.reshape(depth, 1, nb))


def _mod_bwd(cond, ada_w, dm_all, dm_mine):
    depth, d, nb = ada_w.shape
    d3 = dm_all.shape[-1]

    def body(c_ref, w_ref, all_ref, call_ref, mine_ref, cmine_ref, gw_ref, gb_ref, part_ref, ds_ref):
        i = pl.program_id(0)
        cond_v = c_ref[...]
        s = _silu(cond_v).astype(BF16)
        has_ctx = jnp.where(i < 2, 1.0, 0.0)
        tot_all = jnp.sum(call_ref[...], axis=0, keepdims=True) * has_ctx
        tot_mine = jnp.broadcast_to(jnp.sum(cmine_ref[...], axis=0, keepdims=True) * has_ctx, (8, nb)).astype(BF16)
        gb_ref[...] = jnp.sum(all_ref[...], axis=0, keepdims=True) + tot_all
        gw_ref[...] = _dot(s[0:8], mine_ref[...].astype(BF16), 0, 0) + _dot(s[8:16], tot_mine, 0, 0)
        part = _dot(tot_mine, w_ref[...].astype(BF16), 1, 1)

        @pl.when(i == 0)
        def _():
            part_ref[...] = jnp.zeros_like(part_ref)
            ds_ref[...] = _dsilu(cond_v)

        part_ref[...] += part

    def rows(width, which):
        return pl.BlockSpec((None, N_DEV, width), which)

    layer = lambda i: (i, 0, 0)
    ctx_layer = lambda i: (jnp.minimum(i, 1) + 4, 0, 0)
    return _pc(
        body, name="mod_bwd", grid=(depth,),
        in_specs=[pl.BlockSpec((16, d), lambda i: (0, 0)), pl.BlockSpec((None, d, nb), layer),
                  rows(d3, layer), rows(d3, ctx_layer), rows(nb, layer), rows(nb, ctx_layer)],
        out_specs=[pl.BlockSpec((None, d, nb), layer), pl.BlockSpec((None, 1, d3), layer),
                   pl.BlockSpec((8, d), lambda i: (0, 0)), pl.BlockSpec((16, d), lambda i: (0, 0))],
        out_shape=[jax.ShapeDtypeStruct((depth, d, nb), F32), jax.ShapeDtypeStruct((depth, 1, d3), F32),
                   jax.ShapeDtypeStruct((8, d), F32), jax.ShapeDtypeStruct((16, d), F32)],
        compiler_params=_params("arbitrary"),
    )(cond, ada_w, dm_all, dm_all, dm_mine, dm_mine)


def _norm_fwd(xs, g, mod, tr, seg_tiles, name):
    t, d = xs.shape

    def body(x_ref, g_ref, mod_ref, h_ref):
        x = x_ref[...]
        r = lax.rsqrt(jnp.mean(x * x, axis=-1, keepdims=True) + RMS_EPS)
        y = (x * r) * g_ref[...]
        h_ref[...] = (y * (1.0 + mod_ref[1:2, :]) + mod_ref[0:1, :]).astype(BF16)

    return _pc(
        body, name=name, grid=(t // tr,),
        in_specs=[pl.BlockSpec((tr, d), lambda i: (i, 0)), pl.BlockSpec((1, d), lambda i: (0, 0)),
                  pl.BlockSpec((None, 8, d), lambda i: (i // seg_tiles, 0, 0))],
        out_specs=pl.BlockSpec((tr, d), lambda i: (i, 0)),
        out_shape=jax.ShapeDtypeStruct((t, d), BF16),
        compiler_params=_params("parallel"),
    )(xs, g, mod)


def _resid_grad(dx, i, seg_tiles, yx_ref, gate_ref, dyx_ref, gsum_ref):
    dyx_ref[...] = (dx * gate_ref[2:3, :]).astype(BF16)

    @pl.when(i % seg_tiles == 0)
    def _():
        gsum_ref[...] = jnp.zeros_like(gsum_ref)

    gsum_ref[0:1, :] += jnp.sum(dx * yx_ref[...], axis=0, keepdims=True)


def _norm_bwd(xs, dh, dres, g, mod, tr, seg_tiles, name, res_tiles=None, out_tiles=None, below=None):
    t, d = xs.shape
    n_tiles = t // tr
    res_tiles = n_tiles if res_tiles is None else res_tiles
    out_tiles = n_tiles if out_tiles is None else out_tiles

    def body(x_ref, dh_ref, dres_ref, g_ref, mod_ref, *rest):
        i = pl.program_id(0)
        x = x_ref[...]
        r = lax.rsqrt(jnp.mean(x * x, axis=-1, keepdims=True) + RMS_EPS)
        xn = x * r
        dhv = dh_ref[...]
        gain = g_ref[...]
        one_scale = 1.0 + mod_ref[1:2, :]
        dxn = dhv * (gain * one_scale)
        dx = r * (dxn - xn * jnp.mean(dxn * xn, axis=-1, keepdims=True))
        if res_tiles == n_tiles:
            dx = dx + dres_ref[...]
        else:
            dx = dx + jnp.where(i < res_tiles, dres_ref[...], 0.0)
        if below is None:
            dx_ref, sum_ref = rest
        else:
            yx_ref, gate_ref, dx_ref, sum_ref, dyx_ref, gsum_ref = rest
            _resid_grad(dx, i, seg_tiles, yx_ref, gate_ref, dyx_ref, gsum_ref)
        if out_tiles == n_tiles:
            dx_ref[...] = dx
        else:
            @pl.when(i < out_tiles)
            def _():
                dx_ref[...] = dx

        @pl.when(i % seg_tiles == 0)
        def _():
            sum_ref[...] = jnp.zeros_like(sum_ref)

        sum_ref[0:1, :] += jnp.sum(dhv, axis=0, keepdims=True)
        sum_ref[1:2, :] += jnp.sum(dhv * (xn * gain), axis=0, keepdims=True)
        sum_ref[2:3, :] += jnp.sum(dhv * one_scale * xn, axis=0, keepdims=True)

    row = pl.BlockSpec((tr, d), lambda i: (i, 0))
    seg = pl.BlockSpec((None, 8, d), lambda i: (i // seg_tiles, 0, 0))
    in_specs = [row, row, pl.BlockSpec((tr, d), lambda i: (jnp.minimum(i, res_tiles - 1), 0)),
                pl.BlockSpec((1, d), lambda i: (0, 0)), seg]
    out_specs = [pl.BlockSpec((tr, d), lambda i: (jnp.minimum(i, out_tiles - 1), 0)), seg]
    out_shape = [jax.ShapeDtypeStruct((out_tiles * tr, d), F32), jax.ShapeDtypeStruct((mod.shape[0], 8, d), F32)]
    args = [xs, dh, dres, g, mod]
    if below is not None:
        in_specs += [row, seg]
        out_specs += [row, seg]
        out_shape += [jax.ShapeDtypeStruct((t, d), BF16), jax.ShapeDtypeStruct((below[1].shape[0], 8, d), F32)]
        args += list(below)
    return _pc(
        body, name=name, grid=(n_tiles,), in_specs=in_specs, out_specs=out_specs, out_shape=out_shape,
        compiler_params=_params("arbitrary"),
    )(*args)


def _loss_head(xs, target, g, yx, mod, tr):
    t, d = xs.shape

    def body(x_ref, t_ref, g_ref, yx_ref, gate_ref, loss_ref, dx_ref, dg_ref, dyx_ref, gsum_ref):
        i = pl.program_id(0)
        x = x_ref[...]
        r = lax.rsqrt(jnp.mean(x * x, axis=-1, keepdims=True) + RMS_EPS)
        xn = x * r
        gain = g_ref[...]
        err = xn * gain - t_ref[...]
        dy = err * (1.0 / d)
        dxn = dy * gain
        dx = r * (dxn - xn * jnp.mean(dxn * xn, axis=-1, keepdims=True))
        dx_ref[...] = dx
        _resid_grad(dx, i, t // tr, yx_ref, gate_ref, dyx_ref, gsum_ref)

        @pl.when(i == 0)
        def _():
            loss_ref[...] = jnp.zeros_like(loss_ref)
            dg_ref[...] = jnp.zeros_like(dg_ref)

        loss_ref[...] += 0.5 * jnp.sum(jnp.mean(err * err, axis=-1, keepdims=True))
        dg_ref[0:1, :] += jnp.sum(dy * xn, axis=0, keepdims=True)

    row = pl.BlockSpec((tr, d), lambda i: (i, 0))
    seg = pl.BlockSpec((None, 8, d), lambda i: (0, 0, 0))
    return _pc(
        body, name="loss_head", grid=(t // tr,),
        in_specs=[row, row, pl.BlockSpec((1, d), lambda i: (0, 0)), row, seg],
        out_specs=[pl.BlockSpec((8, LANES), lambda i: (0, 0)), row, pl.BlockSpec((8, d), lambda i: (0, 0)), row, seg],
        out_shape=[jax.ShapeDtypeStruct((8, LANES), F32), jax.ShapeDtypeStruct((t, d), F32),
                   jax.ShapeDtypeStruct((8, d), F32), jax.ShapeDtypeStruct((t, d), BF16),
                   jax.ShapeDtypeStruct((1, 8, d), F32)],
        compiler_params=_params("arbitrary"),
    )(xs, target, g, yx, mod)


def _proj_in(h, w, layer, width, name, blocks=None, dtype=F32):
    t, d = h.shape
    n8 = w.shape[-1]
    first, count = blocks if blocks is not None else (0, N_DEV)
    per_part = width // n8
    tm = _tile(t, 1152)

    def body(a_ref, b_ref, o_ref):
        o_ref[...] = _dot(a_ref[...], b_ref[...]).astype(dtype)

    return _pc(
        body, name=name, grid=(t // tm, count),
        in_specs=[pl.BlockSpec((tm, d), lambda i, j: (i, 0)),
                  pl.BlockSpec((None, None, d, n8), lambda i, j: (first + j, layer, 0, 0))],
        out_specs=pl.BlockSpec((None, tm, n8), lambda i, j: (j // per_part, i, j % per_part)),
        out_shape=jax.ShapeDtypeStruct((count // per_part, t, width), dtype),
        compiler_params=_params("parallel", "parallel"),
    )(h, w)


def _proj_out(z, w, res, mod, tm, seg_tiles, name, nxt=None):
    t, k = z.shape
    d = w.shape[1]

    def body(z_ref, w_ref, res_ref, mod_ref, *rest):
        yx = _dot(z_ref[...], w_ref[...])
        x = res_ref[...] + mod_ref[2:3, :] * yx
        if nxt is None:
            yx_ref, x_ref = rest
        else:
            g_ref, nmod_ref, yx_ref, x_ref, h_ref = rest
            r = lax.rsqrt(jnp.mean(x * x, axis=-1, keepdims=True) + RMS_EPS)
            h_ref[...] = (((x * r) * g_ref[...]) * (1.0 + nmod_ref[1:2, :]) + nmod_ref[0:1, :]).astype(BF16)
        yx_ref[...] = yx
        x_ref[...] = x

    tile = pl.BlockSpec((tm, d), lambda i: (i, 0))
    seg = pl.BlockSpec((None, 8, d), lambda i: (i // seg_tiles, 0, 0))
    in_specs = [pl.BlockSpec((tm, k), lambda i: (i, 0)), pl.BlockSpec((k, d), lambda i: (0, 0)), tile, seg]
    out_specs = [tile, tile]
    out_shape = [jax.ShapeDtypeStruct((t, d), F32), jax.ShapeDtypeStruct((t, d), F32)]
    args = [z, w, res, mod]
    if nxt is not None:
        in_specs += [pl.BlockSpec((1, d), lambda i: (0, 0)), seg]
        out_specs.append(tile)
        out_shape.append(jax.ShapeDtypeStruct((t, d), BF16))
        args += list(nxt)
    return _pc(
        body, name=name, grid=(t // tm,), in_specs=in_specs, out_specs=out_specs, out_shape=out_shape,
        compiler_params=_params("parallel"),
    )(*args)


def _proj_out_dz(dyx, w, name):
    t, d = dyx.shape
    width = w.shape[0]
    tm, tn = _tile(t, 1024), _tile(width, 512)

    def body(a_ref, w_ref, o_ref):
        o_ref[...] = _dot(a_ref[...], w_ref[...], 1, 1)

    return _pc(
        body, name=name, grid=(t // tm, width // tn),
        in_specs=[pl.BlockSpec((tm, d), lambda i, j: (i, 0)), pl.BlockSpec((tn, d), lambda i, j: (j, 0))],
        out_specs=pl.BlockSpec((tm, tn), lambda i, j: (i, j)),
        out_shape=jax.ShapeDtypeStruct((t, width), F32),
        compiler_params=_params("parallel", "parallel"),
    )(dyx, w)


def _proj_in_dh(dpre, w, layer, name, after=None):
    parts, t, width = dpre.shape
    d, n8 = w.shape[-2:]
    per_part = width // n8
    tm, tn = _tile(t, 1152), _tile(d, 512)

    def body(a_ref, w_ref, *rest):
        o_ref = rest[-1]
        part = _dot(a_ref[:, 0:n8], w_ref[0], 1, 1)
        for s in range(1, per_part):
            part += _dot(a_ref[:, s * n8:(s + 1) * n8], w_ref[s], 1, 1)

        @pl.when(pl.program_id(2) == 0)
        def _():
            o_ref[...] = part

        @pl.when(pl.program_id(2) != 0)
        def _():
            o_ref[...] += part

    extra = [] if after is None else [after]
    return _pc(
        body, name=name, grid=(t // tm, d // tn, parts),
        in_specs=[pl.BlockSpec((None, tm, width), lambda i, j, k: (k, i, 0)),
                  pl.BlockSpec((per_part, None, tn, n8), lambda i, j, k: (k, layer, j, 0))] + [ANY] * len(extra),
        out_specs=pl.BlockSpec((tm, tn), lambda i, j, k: (i, j)),
        out_shape=jax.ShapeDtypeStruct((t, d), F32),
        compiler_params=_params("parallel", "parallel", "arbitrary"),
    )(dpre, w, *extra)


def _transposed(a_ref):
    return a_ref[...].T


def _grad_w_in(h, dpre, n8, name):
    t, d = h.shape
    parts, _, width = dpre.shape
    per_part = width // n8
    tm, tk = _tile(d, 512), _tile(t, 1152)
    nk = t // tk

    def body(a_ref, b_ref, o_ref, acc_ref):
        k = pl.program_id(1)

        @pl.when(k == 0)
        def _():
            acc_ref[...] = jnp.zeros_like(acc_ref)

        at = _transposed(a_ref)
        for p in range(parts):
            r = _dot(at, b_ref[p])
            for s in range(per_part):
                acc_ref[p * per_part + s] += r[:, s * n8:(s + 1) * n8]

        @pl.when(k == nk - 1)
        def _():
            o_ref[...] = acc_ref[...].astype(BF16)

    return _pc(
        body, name=name, grid=(d // tm, nk),
        in_specs=[pl.BlockSpec((tk, tm), lambda i, k: (k, i)), pl.BlockSpec((parts, tk, width), lambda i, k: (0, k, 0))],
        out_specs=pl.BlockSpec((parts * per_part, tm, n8), lambda i, k: (0, i, 0)),
        out_shape=jax.ShapeDtypeStruct((parts * per_part, d, n8), BF16),
        scratch_shapes=[pltpu.VMEM((parts * per_part, tm, n8), F32)],
        compiler_params=_params("parallel", "arbitrary"),
    )(h, dpre)


def _grad_w_out(z, dyx, name):
    width = z.shape[1]
    t, d = dyx.shape
    tm, tk = _tile(width, 512), _tile(t, 1152)
    nk = t // tk

    def body(a_ref, b_ref, o_ref, acc_ref):
        k = pl.program_id(1)

        @pl.when(k == 0)
        def _():
            acc_ref[...] = jnp.zeros_like(acc_ref)

        acc_ref[...] += _dot(_transposed(a_ref), b_ref[...])

        @pl.when(k == nk - 1)
        def _():
            o_ref[...] = acc_ref[...].astype(BF16)

    return _pc(
        body, name=name, grid=(width // tm, nk),
        in_specs=[pl.BlockSpec((tk, tm), lambda i, k: (k, i)), pl.BlockSpec((tk, d), lambda i, k: (k, 0))],
        out_specs=pl.BlockSpec((tm, d), lambda i, k: (i, 0)),
        out_shape=jax.ShapeDtypeStruct((width, d), BF16),
        scratch_shapes=[pltpu.VMEM((tm, d), F32)],
        compiler_params=_params("parallel", "arbitrary"),
    )(z, dyx)


def _shift(v, k):
    n = v.shape[0]
    return pltpu.roll(v, k % n, 0)


def _window_sum(v, win):
    s = v + _shift(v, 1)
    step = 1
    while 2 * step < win:
        s = _shift(s, step) + _shift(s, -step)
        step *= 2
    return s


def _window_count(base, seg_len, win, shape):
    t = base + lax.broadcasted_iota(jnp.int32, shape, 0)
    hi = jnp.minimum(t + win // 2, seg_len)
    lo = jnp.maximum(t - win // 2, 0)
    return (hi - lo).astype(F32)


def _pad_offsets(segs):
    return [HALO * (s + 1) + st for s, (st, _) in enumerate(segs)]


def _for_chunks(segs, fn):
    offs = _pad_offsets(segs)
    for s, (st, ln) in enumerate(segs):
        def step(ci, carry, s=s, st=st, ln=ln):
            fn(s, st, ln, offs[s], pl.multiple_of(ci * CHUNK, CHUNK))
            return carry
        lax.fori_loop(0, ln // CHUNK, step, 0)


def _pool_fwd(pre, w_grp, scale, segs, name):
    _, t, width = pre.shape
    grp = width // len(POOL_WINDOWS)
    padded = t + HALO * (len(segs) + 1)

    def group(win, pre_ref, w_ref, sc_ref, z_ref, diff_ref, pad_ref):
        pad_ref[...] = jnp.zeros_like(pad_ref)

        def fill(s, st, ln, off, b):
            pad_ref[pl.ds(off + b, CHUNK), :] = pre_ref[0, pl.ds(st + b, CHUNK), :]

        _for_chunks(segs, fill)

        def mix(s, st, ln, off, b):
            ext = pad_ref[pl.ds(off - HALO + b, CHUNK + 2 * HALO), :]
            total = _window_sum(ext, win)[HALO:HALO + CHUNK]
            u = pre_ref[0, pl.ds(st + b, CHUNK), :]
            diff = (total / _window_count(b, ln, win, u.shape) - u).astype(BF16)
            mixed = _dot(diff, w_ref[...])
            gate = _silu(pre_ref[1, pl.ds(st + b, CHUNK), :])
            z_ref[pl.ds(st + b, CHUNK), :] = (mixed * sc_ref[...] * gate).astype(BF16)
            diff_ref[pl.ds(st + b, CHUNK), :] = diff

        _for_chunks(segs, mix)

    def body(pre_ref, w_ref, sc_ref, z_ref, diff_ref, pad_ref):
        gi = pl.program_id(0)
        for widx, win in enumerate(POOL_WINDOWS):
            @pl.when(gi == widx)
            def _(win=win):
                group(win, pre_ref, w_ref, sc_ref, z_ref, diff_ref, pad_ref)

    col = pl.BlockSpec((t, grp), lambda g: (0, g))
    return _pc(
        body, name=name, grid=(len(POOL_WINDOWS),),
        in_specs=[pl.BlockSpec((2, t, grp), lambda g: (0, 0, g)), pl.BlockSpec((None, grp, grp), lambda g: (g, 0, 0)),
                  pl.BlockSpec((1, grp), lambda g: (0, g))],
        out_specs=[col, col],
        out_shape=[jax.ShapeDtypeStruct((t, width), BF16), jax.ShapeDtypeStruct((t, width), BF16)],
        scratch_shapes=[pltpu.VMEM((padded, grp), F32)],
        compiler_params=_params("parallel"),
    )(pre, w_grp, scale)


def _pool_bwd(dz, diff, pre, w_grp, scale, segs, name):
    _, t, width = pre.shape
    grp = width // len(POOL_WINDOWS)
    padded = t + HALO * (len(segs) + 1)

    def group(win, dz_ref, diff_ref, pre_ref, w_ref, sc_ref, dpre_ref, dw_ref, dsc_ref, pad_ref, dd_ref):
        pad_ref[...] = jnp.zeros_like(pad_ref)
        dw_ref[...] = jnp.zeros_like(dw_ref)
        dsc_ref[...] = jnp.zeros_like(dsc_ref)

        def first(s, st, ln, off, b):
            rows = pl.ds(st + b, CHUNK)
            diff_v = diff_ref[rows, :]
            mixed = _dot(diff_v, w_ref[...])
            g = pre_ref[1, rows, :]
            sg = _silu(g)
            dzv = dz_ref[rows, :]
            dmixed = (dzv * sc_ref[...] * sg).astype(BF16)
            dsc_ref[...] += jnp.sum(dzv * mixed * sg, axis=0, keepdims=True)
            dpre_ref[1, rows, :] = (dzv * mixed * sc_ref[...] * _dsilu(g)).astype(BF16)
            ddiff = _dot(dmixed, w_ref[...], 1, 1)
            dw_ref[...] += _dot(diff_v, dmixed, 0, 0)
            dd_ref[rows, :] = ddiff
            pad_ref[pl.ds(off + b, CHUNK), :] = ddiff / _window_count(b, ln, win, ddiff.shape)

        _for_chunks(segs, first)

        def second(s, st, ln, off, b):
            rows = pl.ds(st + b, CHUNK)
            ext = pad_ref[pl.ds(off - HALO + b, CHUNK + 2 * HALO), :]
            total = _shift(_window_sum(ext, win), -1)[HALO:HALO + CHUNK]
            dpre_ref[0, rows, :] = (total - dd_ref[rows, :]).astype(BF16)

        _for_chunks(segs, second)

    def body(dz_ref, diff_ref, pre_ref, w_ref, sc_ref, dpre_ref, dw_ref, dsc_ref, pad_ref, dd_ref):
        gi = pl.program_id(0)
        for widx, win in enumerate(POOL_WINDOWS):
            @pl.when(gi == widx)
            def _(win=win):
                group(win, dz_ref, diff_ref, pre_ref, w_ref, sc_ref, dpre_ref, dw_ref, dsc_ref, pad_ref, dd_ref)

    col = pl.BlockSpec((t, grp), lambda g: (0, g))
    both = pl.BlockSpec((2, t, grp), lambda g: (0, 0, g))
    wspec = pl.BlockSpec((None, grp, grp), lambda g: (g, 0, 0))
    sspec = pl.BlockSpec((1, grp), lambda g: (0, g))
    return _pc(
        body, name=name, grid=(len(POOL_WINDOWS),),
        in_specs=[col, col, both, wspec, sspec],
        out_specs=[both, wspec, sspec],
        out_shape=[jax.ShapeDtypeStruct((2, t, width), BF16), jax.ShapeDtypeStruct((len(POOL_WINDOWS), grp, grp), F32),
                   jax.ShapeDtypeStruct((1, width), F32)],
        scratch_shapes=[pltpu.VMEM((padded, grp), F32), pltpu.VMEM((t, grp), F32)],
        compiler_params=_params("parallel"),
    )(dz, diff, pre, w_grp, scale)


def _conv_fwd(pre, dw, db, name):
    _, t, width = pre.shape
    cb = LANES
    segs = [(0, t)]

    def body(pre_ref, dw_ref, db_ref, z_ref, pad_ref):
        pad_ref[...] = jnp.zeros_like(pad_ref)

        def fill(s, st, ln, off, b):
            rows = pl.ds(b, CHUNK)
            pad_ref[pl.ds(off + b, CHUNK), :] = pre_ref[1, rows, :] * pre_ref[2, rows, :]

        _for_chunks(segs, fill)

        def mix(s, st, ln, off, b):
            rows = pl.ds(b, CHUNK)
            ext = pad_ref[pl.ds(off - HALO + b, CHUNK + 2 * HALO), :]
            conv = (dw_ref[0:1, :] * _shift(ext, 1) + dw_ref[1:2, :] * ext + dw_ref[2:3, :] * _shift(ext, -1))
            conv = conv[HALO:HALO + CHUNK] + db_ref[...]
            y = pre_ref[0, rows, :] * conv
            z_ref[rows, :] = (y * _silu(pre_ref[3, rows, :])).astype(BF16)

        _for_chunks(segs, mix)

    return _pc(
        body, name=name, grid=(width // cb,),
        in_specs=[pl.BlockSpec((4, t, cb), lambda j: (0, 0, j)), pl.BlockSpec((8, cb), lambda j: (0, j)),
                  pl.BlockSpec((1, cb), lambda j: (0, j))],
        out_specs=pl.BlockSpec((t, cb), lambda j: (0, j)),
        out_shape=jax.ShapeDtypeStruct((t, width), BF16),
        scratch_shapes=[pltpu.VMEM((t + 2 * HALO, cb), F32)],
        compiler_params=_params("parallel"),
    )(pre, dw, db)


def _conv_bwd(dz, pre, dw, db, name):
    _, t, width = pre.shape
    cb = LANES
    segs = [(0, t)]

    def body(dz_ref, pre_ref, dw_ref, db_ref, dpre_ref, ddw_ref, ddb_ref, pad_a, pad_c):
        pad_a[...] = jnp.zeros_like(pad_a)
        pad_c[...] = jnp.zeros_like(pad_c)
        ddw_ref[...] = jnp.zeros_like(ddw_ref)
        ddb_ref[...] = jnp.zeros_like(ddb_ref)

        def fill(s, st, ln, off, b):
            rows = pl.ds(b, CHUNK)
            pad_a[pl.ds(off + b, CHUNK), :] = pre_ref[1, rows, :] * pre_ref[2, rows, :]

        _for_chunks(segs, fill)

        def first(s, st, ln, off, b):
            rows = pl.ds(b, CHUNK)
            ext = pad_a[pl.ds(off - HALO + b, CHUNK + 2 * HALO), :]
            prev, nxt = _shift(ext, 1)[HALO:HALO + CHUNK], _shift(ext, -1)[HALO:HALO + CHUNK]
            here = ext[HALO:HALO + CHUNK]
            conv = dw_ref[0:1, :] * prev + dw_ref[1:2, :] * here + dw_ref[2:3, :] * nxt + db_ref[...]
            bg, g = pre_ref[0, rows, :], pre_ref[3, rows, :]
            dzv = dz_ref[rows, :]
            dy = dzv * _silu(g)
            dpre_ref[3, rows, :] = (dzv * (bg * conv) * _dsilu(g)).astype(BF16)
            dpre_ref[0, rows, :] = (dy * conv).astype(BF16)
            dconv = dy * bg
            pad_c[pl.ds(off + b, CHUNK), :] = dconv
            ddw_ref[0:1, :] += jnp.sum(dconv * prev, axis=0, keepdims=True)
            ddw_ref[1:2, :] += jnp.sum(dconv * here, axis=0, keepdims=True)
            ddw_ref[2:3, :] += jnp.sum(dconv * nxt, axis=0, keepdims=True)
            ddb_ref[0:1, :] += jnp.sum(dconv, axis=0, keepdims=True)

        _for_chunks(segs, first)

        def second(s, st, ln, off, b):
            rows = pl.ds(b, CHUNK)
            ext = pad_c[pl.ds(off - HALO + b, CHUNK + 2 * HALO), :]
            da = (dw_ref[0:1, :] * _shift(ext, -1) + dw_ref[1:2, :] * ext + dw_ref[2:3, :] * _shift(ext, 1))
            da = da[HALO:HALO + CHUNK]
            dpre_ref[1, rows, :] = (da * pre_ref[2, rows, :]).astype(BF16)
            dpre_ref[2, rows, :] = (da * pre_ref[1, rows, :]).astype(BF16)

        _for_chunks(segs, second)

    quad = pl.BlockSpec((4, t, cb), lambda j: (0, 0, j))
    rows8 = pl.BlockSpec((8, cb), lambda j: (0, j))
    return _pc(
        body, name=name, grid=(width // cb,),
        in_specs=[pl.BlockSpec((t, cb), lambda j: (0, j)), quad, rows8, pl.BlockSpec((1, cb), lambda j: (0, j))],
        out_specs=[quad, rows8, rows8],
        out_shape=[jax.ShapeDtypeStruct((4, t, width), BF16), jax.ShapeDtypeStruct((8, width), F32),
                   jax.ShapeDtypeStruct((8, width), F32)],
        scratch_shapes=[pltpu.VMEM((t + 2 * HALO, cb), F32), pltpu.VMEM((t + 2 * HALO, cb), F32)],
        compiler_params=_params("parallel"),
    )(dz, pre, dw, db)


PAIR_TILES = 2 * WIN_ROWS - 2


def _pair_geometry():
    lane = lax.broadcasted_iota(jnp.int32, (GRID_W, LANES), 1)
    qcol = lax.broadcasted_iota(jnp.int32, (GRID_W, LANES), 0)
    low = lane < GRID_W
    kcol = jnp.where(low, lane, lane - GRID_W)
    start = jnp.clip(qcol - WIN_COLS // 2, 0, GRID_W - WIN_COLS)
    inside = (kcol >= start) & (kcol < start + WIN_COLS)
    return low, inside


def _bias_tiles(rpb_ref, rows_ref, tiles_ref, inside):
    for h in range(2):
        rows = rpb_ref[h]
        rows_ref[h] = (pltpu.roll(rows, LANES - (WIN_COLS - 1), 1)
                       + pltpu.roll(pltpu.roll(rows, GRID_W - (WIN_COLS - 1), 1), 2 * WIN_ROWS - 1, 0))
        for t in range(PAIR_TILES):
            both = jnp.broadcast_to(rows_ref[h, t:t + 1, :], (GRID_W, LANES))
            tiles_ref[h, t] = jnp.where(inside, pltpu.roll(both, 0, 1, stride=1, stride_axis=0), MASKED)


def _bias_tiles_grad(dtiles_ref, drpb_ref):
    n = PAIR_TILES * GRID_W
    qcol = lax.broadcasted_iota(jnp.int32, (n, LANES), 0) & (GRID_W - 1)
    lane = lax.broadcasted_iota(jnp.int32, (1, LANES), 1)
    zero = jnp.zeros((1, LANES), F32)
    for h in range(2):
        v = pltpu.roll(dtiles_ref[h].reshape(n, LANES), WIN_COLS - 1, 1)
        for bit in range(6):
            v = jnp.where((qcol >> bit) & 1 == 1, pltpu.roll(v, LANES - (1 << bit), 1), v)
        sums = [jnp.sum(v[t * GRID_W:(t + 1) * GRID_W], axis=0, keepdims=True) for t in range(PAIR_TILES)]
        for r in range(2 * WIN_ROWS):
            here = sums[r] if r < PAIR_TILES else zero
            prev = pltpu.roll(sums[r - 1], GRID_W, 1) if 1 <= r <= PAIR_TILES else zero
            drpb_ref[h, r:r + 1, :] = jnp.where(lane < 2 * WIN_COLS - 1, here + prev, 0.0)


def _attn_rows(r, n_rows):
    first = jnp.clip(r - WIN_ROWS // 2, 0, n_rows - WIN_ROWS)
    return first, first - r + WIN_ROWS - 1


def _softmax(s_loc, s_ctx):
    m = jnp.maximum(jnp.max(s_loc, axis=-1, keepdims=True), jnp.max(s_ctx, axis=-1, keepdims=True))
    e_loc, e_ctx = jnp.exp(s_loc - m), jnp.exp(s_ctx - m)
    inv = 1.0 / (jnp.sum(e_loc, axis=-1, keepdims=True) + jnp.sum(e_ctx, axis=-1, keepdims=True))
    return e_loc * inv, e_ctx * inv


def _pair_bias(tiles_ref, j):
    return jnp.concatenate(
        [jnp.concatenate([tiles_ref[h, j + 2 * m] for m in range(WIN_ROWS // 2)], axis=1) for h in range(2)], axis=0)


ROWS_PER_STEP = 4


def _by_head(tile, low):
    zero = jnp.zeros_like(tile)
    return jnp.concatenate([jnp.where(low, tile, zero), jnp.where(low, zero, tile)], axis=0)


def _merge_heads(stacked, low):
    return jnp.where(low, stacked[:GRID_W], stacked[GRID_W:])


def _attn_items(step, n_rows, q_ref, low):
    items = []
    for u in range(ROWS_PER_STEP):
        r = step * ROWS_PER_STEP + u
        first, j = _attn_rows(r, n_rows)
        rows = pl.ds(pl.multiple_of(r * GRID_W, GRID_W), GRID_W)
        keys = pl.ds(pl.multiple_of(first * GRID_W, GRID_W), WIN_ROWS * GRID_W)
        q = (q_ref[rows, :].astype(F32) * HEAD_DIM ** -0.5).astype(BF16)
        items.append((rows, keys, j, _by_head(q, low)))
    return items


def _attn_fwd(qkv, gate, rpb, seq):
    _, t, width = qkv.shape
    n_rows = seq // GRID_W
    n_ctx = t - seq
    blk = WIN_ROWS * GRID_W

    def body(q_ref, k_ref, v_ref, g_ref, rpb_ref, z_ref, o_ref, rows_ref, tiles_ref):
        low, inside = _pair_geometry()
        _bias_tiles(rpb_ref, rows_ref, tiles_ref, inside)
        ctx = pl.ds(seq, n_ctx)

        def step(i, carry):
            items = _attn_items(i, n_rows, q_ref, low)
            k_ctx, v_ctx = k_ref[ctx, :], v_ref[ctx, :]
            scores = [(_dot(q, k_ref[keys, :], 1, 1) + _pair_bias(tiles_ref, j), _dot(q, k_ctx, 1, 1))
                      for _, keys, j, q in items]
            probs = [_softmax(s_loc, s_ctx) for s_loc, s_ctx in scores]
            outs = [_dot(p_loc.astype(BF16), v_ref[keys, :]) + _dot(p_ctx.astype(BF16), v_ctx)
                    for (_, keys, _, _), (p_loc, p_ctx) in zip(items, probs)]
            for (rows, _, _, _), out in zip(items, outs):
                o = _merge_heads(out, low)
                o_ref[rows, :] = o
                z_ref[rows, :] = (o * _silu(g_ref[rows, :])).astype(BF16)
            return carry

        lax.fori_loop(0, n_rows // ROWS_PER_STEP, step, 0)

    def part(p):
        return pl.BlockSpec((None, t, LANES), lambda h: (p, 0, h))

    out = pl.BlockSpec((seq, LANES), lambda h: (0, h))
    return _pc(
        body, name="attn_fwd", grid=(width // LANES,),
        in_specs=[part(0), part(1), part(2), part(0), pl.BlockSpec((2, 2 * WIN_ROWS, LANES), lambda h: (h, 0, 0))],
        out_specs=[out, out],
        out_shape=[jax.ShapeDtypeStruct((seq, width), BF16), jax.ShapeDtypeStruct((seq, width), F32)],
        scratch_shapes=[pltpu.VMEM((2, 2 * WIN_ROWS, LANES), F32), pltpu.VMEM((2, PAIR_TILES, GRID_W, LANES), F32)],
        compiler_params=_params("parallel"),
    )(qkv, qkv, qkv, gate, rpb)


def _attn_bwd(qkv, gate, o, dz, rpb, seq):
    _, t, width = qkv.shape
    n_rows = seq // GRID_W
    n_ctx = t - seq
    blk = WIN_ROWS * GRID_W
    heads = 2 * width // LANES

    def body(q_ref, k_ref, v_ref, g_ref, o_ref, dz_ref, rpb_ref, dpre_ref, drpb_ref,
             rows_ref, tiles_ref, dtiles_ref, dk_ref, dv_ref):
        low, inside = _pair_geometry()
        _bias_tiles(rpb_ref, rows_ref, tiles_ref, inside)
        dtiles_ref[...] = jnp.zeros_like(dtiles_ref)
        dk_ref[...] = jnp.zeros_like(dk_ref)
        dv_ref[...] = jnp.zeros_like(dv_ref)
        ctx = pl.ds(seq, n_ctx)

        def step(i, carry):
            items = _attn_items(i, n_rows, q_ref, low)
            k_ctx, v_ctx = k_ref[ctx, :], v_ref[ctx, :]
            d_outs = []
            for rows, _, _, _ in items:
                g = g_ref[rows, :]
                dzv = dz_ref[rows, :]
                dpre_ref[3, rows, :] = (dzv * o_ref[rows, :] * _dsilu(g)).astype(BF16)
                d_outs.append(_by_head((dzv * _silu(g)).astype(BF16), low))
            scores = [(_dot(q, k_ref[keys, :], 1, 1) + _pair_bias(tiles_ref, j), _dot(q, k_ctx, 1, 1))
                      for _, keys, j, q in items]
            dprobs = [(_dot(d_o, v_ref[keys, :], 1, 1), _dot(d_o, v_ctx, 1, 1))
                      for (_, keys, _, _), d_o in zip(items, d_outs)]
            probs = [_softmax(s_loc, s_ctx) for s_loc, s_ctx in scores]
            dscores = []
            for (p_loc, p_ctx), (dp_loc, dp_ctx) in zip(probs, dprobs):
                delta = (jnp.sum(p_loc * dp_loc, axis=-1, keepdims=True)
                         + jnp.sum(p_ctx * dp_ctx, axis=-1, keepdims=True))
                dscores.append((p_loc * (dp_loc - delta), p_ctx * (dp_ctx - delta)))
            dqs = [_dot(ds_loc.astype(BF16), k_ref[keys, :]) + _dot(ds_ctx.astype(BF16), k_ctx)
                   for (_, keys, _, _), (ds_loc, ds_ctx) in zip(items, dscores)]
            for (rows, _, _, _), dq in zip(items, dqs):
                dpre_ref[0, rows, :] = (_merge_heads(dq, low) * HEAD_DIM ** -0.5).astype(BF16)
            for (_, keys, j, q), d_o, (p_loc, p_ctx), (ds_loc, ds_ctx) in zip(items, d_outs, probs, dscores):
                dk_ref[keys, :] += _dot(ds_loc.astype(BF16), q, 0, 0)
                dk_ref[ctx, :] += _dot(ds_ctx.astype(BF16), q, 0, 0)
                dv_ref[keys, :] += _dot(p_loc.astype(BF16), d_o, 0, 0)
                dv_ref[ctx, :] += _dot(p_ctx.astype(BF16), d_o, 0, 0)
                for h in range(2):
                    for m in range(WIN_ROWS // 2):
                        dtiles_ref[h, j + 2 * m] += ds_loc[h * GRID_W:(h + 1) * GRID_W, m * LANES:(m + 1) * LANES]
            return carry

        lax.fori_loop(0, n_rows // ROWS_PER_STEP, step, 0)
        dpre_ref[1] = dk_ref[...].astype(BF16)
        dpre_ref[2] = dv_ref[...].astype(BF16)
        dpre_ref[0, ctx, :] = jnp.zeros((n_ctx, LANES), BF16)
        dpre_ref[3, ctx, :] = jnp.zeros((n_ctx, LANES), BF16)
        _bias_tiles_grad(dtiles_ref, drpb_ref)

    def part(p):
        return pl.BlockSpec((None, t, LANES), lambda h: (p, 0, h))

    lat = pl.BlockSpec((seq, LANES), lambda h: (0, h))
    rspec = pl.BlockSpec((2, 2 * WIN_ROWS, LANES), lambda h: (h, 0, 0))
    tiles = pltpu.VMEM((2, PAIR_TILES, GRID_W, LANES), F32)
    return _pc(
        body, name="attn_bwd", grid=(width // LANES,),
        in_specs=[part(0), part(1), part(2), part(0), lat, lat, rspec],
        out_specs=[pl.BlockSpec((4, t, LANES), lambda h: (0, 0, h)), rspec],
        out_shape=[jax.ShapeDtypeStruct((4, t, width), BF16), jax.ShapeDtypeStruct((heads, 2 * WIN_ROWS, LANES), F32)],
        scratch_shapes=[pltpu.VMEM((2, 2 * WIN_ROWS, LANES), F32), tiles, tiles,
                        pltpu.VMEM((t, LANES), F32), pltpu.VMEM((t, LANES), F32)],
        compiler_params=_params("parallel"),
    )(qkv, qkv, qkv, gate, o, dz, rpb)


def _adam_update(w, m, v, g):
    m2 = ADAM_B1 * m + (1.0 - ADAM_B1) * g
    v2 = ADAM_B2 * v + (1.0 - ADAM_B2) * (g * g)
    m_hat = m2 / (1.0 - ADAM_B1 ** ADAM_STEP)
    v_hat = v2 / (1.0 - ADAM_B2 ** ADAM_STEP)
    return -ADAM_LR * (m_hat / (jnp.sqrt(v_hat) + ADAM_EPS) + ADAM_WD * w), m2, v2


def _adamw(w, m, v, parts, name):
    rows, cols = w.shape
    tr = _tile(rows, max(8, 131072 // cols), 8)
    n_parts = len(parts)

    def body(*refs):
        w_ref, m_ref, v_ref = refs[:3]
        part_refs = refs[3:3 + n_parts]
        g_ref, d_ref, nm_ref, nv_ref = refs[3 + n_parts:]
        g = part_refs[0][...].astype(F32)
        for p in part_refs[1:]:
            g = g + p[...].astype(F32)
        g_ref[...] = g
        d_ref[...], nm_ref[...], nv_ref[...] = _adam_update(w_ref[...], m_ref[...], v_ref[...], g)

    tile = pl.BlockSpec((tr, cols), lambda i: (i, 0))
    in_specs, args = [tile, tile, tile], [w, m, v]
    for p in parts:
        if isinstance(p, tuple):
            arr, k = p
            in_specs.append(pl.BlockSpec((None, tr, cols), lambda i, k=k: (k, i, 0)))
            args.append(arr)
        else:
            in_specs.append(tile)
            args.append(p)
    shape = jax.ShapeDtypeStruct((rows, cols), F32)
    return _pc(
        body, name=name, grid=(rows // tr,), in_specs=in_specs, out_specs=[tile] * 4, out_shape=[shape] * 4,
        compiler_params=_params("parallel"),
    )(*args)


def _adamw_layers(w, m, v, landed, name):
    n_layers, rows, cols = w.shape
    tr = _tile(rows, max(8, 131072 // cols), 8)

    def body(*refs):
        w_ref, m_ref, v_ref = refs[:3]
        part_refs = refs[3:3 + n_layers * N_DEV]
        g_ref, d_ref, nm_ref, nv_ref = refs[3 + n_layers * N_DEV:]
        layer = pl.program_id(0)
        g = None
        for l in range(n_layers):
            s = part_refs[l * N_DEV][...].astype(F32)
            for p in part_refs[l * N_DEV + 1:(l + 1) * N_DEV]:
                s = s + p[...].astype(F32)
            g = s if g is None else jnp.where(layer == l, s, g)
        g_ref[...] = g
        d_ref[...], nm_ref[...], nv_ref[...] = _adam_update(w_ref[...], m_ref[...], v_ref[...], g)

    tile = pl.BlockSpec((None, tr, cols), lambda l, i: (l, i, 0))
    in_specs, args = [tile, tile, tile], [w, m, v]
    for l, arr in enumerate(landed):
        for k in range(N_DEV):
            in_specs.append(pl.BlockSpec((None, tr, cols), lambda ll, i, l=l, k=k: (k, jnp.where(ll == l, i, 0), 0)))
            args.append(arr)
    shape = jax.ShapeDtypeStruct(w.shape, F32)
    return _pc(
        body, name=name, grid=(n_layers, rows // tr), in_specs=in_specs, out_specs=[tile] * 4, out_shape=[shape] * 4,
        compiler_params=_params("arbitrary", "arbitrary"),
    )(*args)


def _adamw_small(states, grads):
    sources, makers = grads
    n, ns = len(states), len(sources)

    def body(*refs):
        src = refs[:ns]
        ins = refs[ns:ns + 3 * n]
        outs = refs[ns + 3 * n:]
        for k in range(n):
            w_ref, m_ref, v_ref = ins[3 * k:3 * k + 3]
            g = makers[k](*src)
            outs[4 * k][...] = g
            outs[4 * k + 1][...], outs[4 * k + 2][...], outs[4 * k + 3][...] = _adam_update(
                w_ref[...], m_ref[...], v_ref[...], g)

    flat = [a for s in states for a in s]
    vmem = pl.BlockSpec(memory_space=pltpu.VMEM)
    res = _pc(
        body, name="adamw_small",
        in_specs=[vmem] * (ns + 3 * n), out_specs=[vmem] * (4 * n),
        out_shape=[jax.ShapeDtypeStruct(s[0].shape, F32) for s in states for _ in range(4)],
        compiler_params=pltpu.CompilerParams(vmem_limit_bytes=VMEM_LIMIT),
    )(*sources, *flat)
    return [res[4 * k:4 * k + 4] for k in range(n)]


def _rows128(a):
    flat = a.reshape(-1)
    pad = (-flat.shape[0]) % LANES
    if pad:
        flat = jnp.concatenate([flat, jnp.zeros((pad,), flat.dtype)])
    return flat.reshape(-1, LANES)


def _pad_rows(a, mult=8):
    pad = (-a.shape[0]) % mult
    if pad:
        a = jnp.concatenate([a, jnp.zeros((pad,) + a.shape[1:], a.dtype)], axis=0)
    return a


def kernel(x, c, ctx, c_ctx, norm_g, ada_w, ada_b, pool_w_in, pool_w_grp, pool_scale, pool_w_out, na_w_in, na_rpb, na_w_out, conv_w_in, conv_dw, conv_db, conv_w_out, final_g, loss_target, m_c_ctx, m_norm_g, m_ada_w, m_ada_b, m_pool_w_in, m_pool_w_grp, m_pool_scale, m_pool_w_out, m_na_w_in, m_na_rpb, m_na_w_out, m_conv_w_in, m_conv_dw, m_conv_db, m_conv_w_out, m_final_g, v_c_ctx, v_norm_g, v_ada_w, v_ada_b, v_pool_w_in, v_pool_w_grp, v_pool_scale, v_pool_w_out, v_na_w_in, v_na_rpb, v_na_w_out, v_conv_w_in, v_conv_dw, v_conv_db, v_conv_w_out, v_final_g):
    xi, yi, ci = _my_place()
    me = 4 * xi + 2 * yi + ci
    seq, d = x.shape[1], x.shape[2]
    n_ctx = ctx.shape[1]
    t_all = seq + n_ctx
    width = d
    depth = norm_g.shape[0]
    nb = ada_w.shape[2]
    shard = width // N_DEV
    d_rows = d // LANES
    tr = math.gcd(math.gcd(seq, n_ctx), 256)
    x_tiles = seq // tr

    n_pool = pool_scale.shape[0]
    n_grp = pool_w_grp.shape[1]
    grp = width // n_grp

    small_in = _pad_rows(jnp.concatenate([_rows128(c), pool_scale, conv_dw[0], conv_db], axis=0))
    got = _gather_small(small_in, "gather_inputs")
    r0 = d_rows
    c_all = got[:, :r0].reshape(N_DEV, d)
    scale_full = got[:, r0:r0 + n_pool].transpose(1, 0, 2).reshape(n_pool, width)
    r1 = r0 + n_pool
    taps_full = _pad_rows(got[:, r1:r1 + 3].transpose(1, 0, 2).reshape(3, width))
    bias_full = got[:, r1 + 3:r1 + 4].transpose(1, 0, 2).reshape(1, width)

    cond = jnp.concatenate([c_all, c_ctx[None], jnp.zeros((7, d), F32)], axis=0)
    bias_mine = lax.dynamic_slice(ada_b, (0, me * nb), (depth, nb))
    mod_mine = _mod_fwd(cond, ada_w, bias_mine)
    by_example = jnp.stack([mod_mine[:, :N_DEV].transpose(1, 0, 2),
                            jnp.broadcast_to(mod_mine[:, N_DEV][None], (N_DEV, depth, nb))], axis=2)
    mod_all = _gather_small(by_example.reshape(N_DEV, -1, LANES), "gather_mod", per_dest=True)
    mod_all = mod_all.reshape(N_DEV, depth, 2, nb).transpose(1, 2, 0, 3).reshape(depth, 2, 3, d)
    mod_all = jnp.pad(mod_all, ((0, 0), (0, 0), (0, 5), (0, 0)))
    mods = [mod_all[i] if i < 2 else mod_all[i, :1] for i in range(depth)]

    layer_weights = [[pool_w_in[0], pool_w_grp[0], pool_w_out[0]], [na_w_in[0], na_w_out[0]],
                     [conv_w_in[0], conv_w_out[0]], [pool_w_in[1], pool_w_grp[1], pool_w_out[1]]]
    slot = {(i, t): n for n, (i, t) in enumerate((i, t) for i, ws in enumerate(layer_weights) for t in range(len(ws)))}
    two_level = [(0, 0), (1, 0)]
    weights_sent = _exchange_start(
        [w.astype(BF16) for ws in layer_weights for w in ws], False, mod_all, "weights_start",
        peers=[CHIP_PEERS if key in two_level else ALL_PEERS for key in slot])
    token = weights_sent[-1]

    def landed_weight(i, t, after):
        return _exchange_wait(weights_sent, False, after, f"weights_wait{i}_{t}", which=[slot[i, t]])[0]

    def handed_on(i, after):
        half = [landed_weight(i, 0, after)]
        rest = _forward_start(half, after, f"weights_forward{i}")
        return _forward_wait(rest, rest[-1], f"weights_forward_wait{i}")[0][:, None]

    def as_in(w):
        return w[:, None]

    def as_grp(w):
        return w.transpose(1, 0, 2, 3).reshape(n_grp, grp, grp)

    def as_out(w):
        return w.reshape(width, d)

    both = [(0, seq), (seq, n_ctx)]
    latent = [(0, seq)]

    def grp_slots(g):
        return g.reshape(n_grp, N_DEV, grp // N_DEV, grp).transpose(1, 0, 2, 3).reshape(N_DEV, -1, grp).astype(BF16)

    def send_grads(i, grads):
        return _exchange_start(grads, True, jnp.zeros((8, LANES), F32), f"grads_start{i}")

    xs0 = jnp.concatenate([x[0], ctx[0]], axis=0)
    h0 = _norm_fwd(xs0, norm_g[0:1] + token[0, 0], mods[0], tr, x_tiles, "norm_fwd0")
    pool_in_w0 = handed_on(0, h0)
    pre0 = _proj_in(h0, pool_in_w0, 0, width, "proj_in0")
    pool_grp_w0 = as_grp(landed_weight(0, 1, pre0))
    z0, diff0 = _pool_fwd(pre0, pool_grp_w0, scale_full[0:1], both, "pool_fwd0")
    pool_out_w0 = as_out(landed_weight(0, 2, z0))
    yx0, xs1, h1 = _proj_out(z0, pool_out_w0, xs0, mods[0], tr, x_tiles, "proj_out0", nxt=(norm_g[1:2], mods[1]))

    na_in_w = handed_on(1, h1)
    per_part = width // na_w_in.shape[2]
    qkv1 = _proj_in(h1, na_in_w, 0, width, "proj_in1_qkv", blocks=(0, 3 * per_part), dtype=BF16)
    gpre1 = _proj_in(h1, na_in_w, 0, width, "proj_in1_gate", blocks=(3 * per_part, per_part))
    rpb_rows = jnp.pad(na_rpb[0], ((0, 0), (0, 2 * WIN_ROWS - na_rpb.shape[2]), (0, LANES - na_rpb.shape[3])))
    z1, o1 = _attn_fwd(qkv1, gpre1, rpb_rows, seq)
    na_out_w = as_out(landed_weight(1, 1, z1))
    yx1, x2, h2 = _proj_out(z1, na_out_w, xs1, mods[1], tr, x_tiles, "proj_out1", nxt=(norm_g[2:3], mods[2]))

    conv_in_w = as_in(landed_weight(2, 0, h2))
    pre2 = _proj_in(h2, conv_in_w, 0, width, "proj_in2")
    z2 = _conv_fwd(pre2, taps_full, bias_full, "conv_fwd")
    conv_out_w = as_out(landed_weight(2, 1, z2))
    yx2, x3, h3 = _proj_out(z2, conv_out_w, x2, mods[2], tr, x_tiles, "proj_out2", nxt=(norm_g[3:4], mods[3]))

    pool_in_w3 = as_in(landed_weight(3, 0, h3))
    pre3 = _proj_in(h3, pool_in_w3, 0, width, "proj_in3")
    pool_grp_w3 = as_grp(landed_weight(3, 1, pre3))
    z3, diff3 = _pool_fwd(pre3, pool_grp_w3, scale_full[1:2], latent, "pool_fwd3")
    pool_out_w3 = as_out(landed_weight(3, 2, z3))
    yx3, x4 = _proj_out(z3, pool_out_w3, x3, mods[3], tr, x_tiles, "proj_out3")

    loss_part, dx4, d_final, dyx3, gate3 = _loss_head(x4, loss_target[0], final_g[None], yx3, mods[3], tr)

    dz3 = _proj_out_dz(dyx3, pool_out_w3, "proj_out_dz3")
    g_pool_out1 = _grad_w_out(z3, dyx3, "grad_w_out3")
    dpre3, g_grp1, g_scale1 = _pool_bwd(dz3, diff3, pre3, pool_grp_w3, scale_full[1:2], latent, "pool_bwd3")
    dh3 = _proj_in_dh(dpre3, pool_in_w3, 0, "proj_in_dh3")
    g_pool_in1 = _grad_w_in(h3, dpre3, pool_w_in.shape[2], "grad_w_in3")
    sent3 = send_grads(3, [g_pool_in1, grp_slots(g_grp1), g_pool_out1.reshape(N_DEV, shard, d)])
    dx3, norm3, dyx2, gate2 = _norm_bwd(x3, dh3, dx4, norm_g[3:4] + sent3[-1][0, 0], mods[3], tr, x_tiles, "norm_bwd3",
                                        below=(yx2, mods[2]))

    dz2 = _proj_out_dz(dyx2, conv_out_w, "proj_out_dz2")
    g_conv_out = _grad_w_out(z2, dyx2, "grad_w_out2")
    dpre2, g_taps, g_cbias = _conv_bwd(dz2, pre2, taps_full, bias_full, "conv_bwd")
    dh2 = _proj_in_dh(dpre2, conv_in_w, 0, "proj_in_dh2")
    g_conv_in = _grad_w_in(h2, dpre2, conv_w_in.shape[2], "grad_w_in2")
    sent2 = send_grads(2, [g_conv_in, g_conv_out.reshape(N_DEV, shard, d)])
    dx2, norm2, dyx1, gate1 = _norm_bwd(x2, dh2, dx3, norm_g[2:3] + sent2[-1][0, 0], mods[2], tr, x_tiles, "norm_bwd2",
                                        below=(yx1, mods[1][:1]))

    dz1 = _proj_out_dz(dyx1, na_out_w, "proj_out_dz1")
    g_na_out = _grad_w_out(z1, dyx1, "grad_w_out1")
    dpre1, g_rpb = _attn_bwd(qkv1, gpre1, o1, dz1, rpb_rows, seq)
    g_rpb = g_rpb[:, :na_rpb.shape[2], :na_rpb.shape[3]]
    dh1 = _proj_in_dh(dpre1, na_in_w, 0, "proj_in_dh1")
    g_na_in = _grad_w_in(h1, dpre1, na_w_in.shape[2], "grad_w_in1")
    sent1 = send_grads(1, [g_na_in, g_na_out.reshape(N_DEV, shard, d)])
    dxs1, norm1, dyx0, gate0 = _norm_bwd(xs1, dh1, dx2, norm_g[1:2] + sent1[-1][0, 0], mods[1], tr, x_tiles, "norm_bwd1",
                                         res_tiles=x_tiles, below=(yx0, mods[0]))

    dz0 = _proj_out_dz(dyx0, pool_out_w0, "proj_out_dz0")
    g_pool_out0 = _grad_w_out(z0, dyx0, "grad_w_out0")
    sent0a = _exchange_start([g_pool_out0.reshape(N_DEV, shard, d)], True, jnp.zeros((8, LANES), F32), "grads_start0a")
    dpre0, g_grp0, g_scale0 = _pool_bwd(dz0, diff0, pre0, pool_grp_w0, scale_full[0:1], both, "pool_bwd0")
    g_pool_in0 = _grad_w_in(h0, dpre0, pool_w_in.shape[2], "grad_w_in0")
    sent0b = _exchange_start([g_pool_in0, grp_slots(g_grp0)], True, sent0a[-1], "grads_start0b")
    dh0 = _proj_in_dh(dpre0, pool_in_w0, 0, "proj_in_dh0", after=sent0b[-1])
    dx0, norm0 = _norm_bwd(xs0, dh0, dxs1, norm_g[0:1], mods[0], tr, x_tiles, "norm_bwd0", out_tiles=x_tiles)
    grad_x = dx0[None]

    norms, gates = [norm0, norm1, norm2, norm3], [gate0, gate1, gate2, gate3]
    zero_d = jnp.zeros((d,), F32)
    dm_rows = [jnp.concatenate([norms[i][0, 0], norms[i][0, 1], gates[i][0, 0]]) for i in range(depth)]
    dm_rows.append(jnp.concatenate([norm0[1, 0], norm0[1, 1], gate0[1, 0]]))
    dm_rows.append(jnp.concatenate([norm1[1, 0], norm1[1, 1], zero_d]))
    dm_local = jnp.stack(dm_rows + [jnp.zeros((3 * d,), F32)] * 2)
    g_norm_part = jnp.stack([norm0[0, 2] + norm0[1, 2], norm1[0, 2] + norm1[1, 2], norm2[0, 2], norm3[0, 2]])
    pieces = [_rows128(dm_local), _rows128(g_norm_part), _rows128(d_final[0]), _pad_rows(_rows128(g_rpb)), loss_part]
    marks = np.cumsum([0] + [p.shape[0] for p in pieces])
    by_owner = [a.reshape(-1, N_DEV, shard).transpose(1, 0, 2) for a in (g_scale0, g_scale1, g_taps[:3], g_cbias[0:1])]
    by_owner = jnp.concatenate(by_owner + [jnp.zeros((N_DEV, 8 - n_pool - 4, shard), F32)], axis=1)
    small_sent = _exchange_start([jnp.concatenate(pieces, axis=0), by_owner], [False, True], jnp.zeros((8, LANES), F32),
                                 "small_grads_start")

    def big(parts, w, m, v, name):
        shape = w.shape
        view = (-1, shape[-1])
        parts = [(parts.reshape((N_DEV,) + w.reshape(view).shape), k) for k in range(N_DEV)]
        return [r.reshape(shape) for r in _adamw(w.reshape(view), m.reshape(view), v.reshape(view), parts, name)]

    in3, grp3, out3 = _exchange_wait(sent3, True, small_sent[-1], "grads_wait3")
    in2, out2 = _exchange_wait(sent2, True, small_sent[-1], "grads_wait2")
    in1, out1 = _exchange_wait(sent1, True, small_sent[-1], "grads_wait1")
    res = {}
    res["na_w_in"] = [r[None] for r in big(in1, na_w_in[0], m_na_w_in[0], v_na_w_in[0], "adamw_na_in")]
    res["na_w_out"] = [r[None] for r in big(out1, na_w_out[0], m_na_w_out[0], v_na_w_out[0], "adamw_na_out")]
    res["conv_w_in"] = [r[None] for r in big(in2, conv_w_in[0], m_conv_w_in[0], v_conv_w_in[0], "adamw_conv_in")]
    res["conv_w_out"] = [r[None] for r in big(out2, conv_w_out[0], m_conv_w_out[0], v_conv_w_out[0], "adamw_conv_out")]

    done = [res[n][0] for n in ("na_w_in", "na_w_out", "conv_w_in", "conv_w_out")]
    small_out, owned = _exchange_wait(small_sent, [False, True], done, "small_grads_wait")
    loss = jnp.sum(small_out[:, marks[4], 0])
    dm_all = small_out[:, :marks[1]].reshape(N_DEV, 8, 3 * d).transpose(1, 0, 2)
    dm_mine = lax.dynamic_slice(dm_all, (0, 0, me * nb), (8, N_DEV, nb))
    g_ada_w, g_ada_b, cctx_part, dsilu_cond = _mod_bwd(cond, ada_w, dm_all, dm_mine)
    cctx_all = _gather_small(_rows128(cctx_part[0]), "gather_cctx")

    def summed(ref, lo, hi):
        g = ref[0, lo:hi, :]
        for k in range(1, N_DEV):
            g = g + ref[k, lo:hi, :]
        return g

    makers = [
        lambda so, ow, cc, ab, ds: summed(cc, 0, d_rows) * ds[...],
        lambda so, ow, cc, ab, ds: summed(so, marks[1], marks[2]),
        lambda so, ow, cc, ab, ds: ab[...],
        lambda so, ow, cc, ab, ds: summed(so, marks[2], marks[3]),
        lambda so, ow, cc, ab, ds: summed(so, marks[3], marks[4]),
        lambda so, ow, cc, ab, ds: summed(ow, 0, n_pool),
        lambda so, ow, cc, ab, ds: summed(ow, n_pool, n_pool + 3),
        lambda so, ow, cc, ab, ds: summed(ow, n_pool + 3, n_pool + 4),
    ]
    rpb_rows128 = lambda a: _pad_rows(_rows128(a))
    views = [_rows128] * 4 + [rpb_rows128] + [lambda a: a.reshape(-1, LANES)] * 3
    small = [(c_ctx, m_c_ctx, v_c_ctx), (norm_g, m_norm_g, v_norm_g), (ada_b, m_ada_b, v_ada_b),
             (final_g, m_final_g, v_final_g), (na_rpb, m_na_rpb, v_na_rpb), (pool_scale, m_pool_scale, v_pool_scale),
             (conv_dw, m_conv_dw, v_conv_dw), (conv_db, m_conv_db, v_conv_db)]
    states = [tuple(view(a) for a in triple) for view, triple in zip(views, small)]
    sources = (small_out, owned, cctx_all, _rows128(g_ada_b), _rows128(dsilu_cond[8]))
    small_res = _adamw_small(states, (sources, makers))
    names = ["c_ctx", "norm_g", "ada_b", "final_g", "na_rpb", "pool_scale", "conv_dw", "conv_db"]
    for name, (w, _, _), outs4 in zip(names, small, small_res):
        res[name] = [r.reshape(-1)[:w.size].reshape(w.shape) for r in outs4]

    res["ada_w"] = [r.reshape(ada_w.shape) for r in _adamw(
        ada_w.reshape(-1, nb), m_ada_w.reshape(-1, nb), v_ada_w.reshape(-1, nb), [g_ada_w.reshape(-1, nb)], "adamw_ada_w")]

    out0, = _exchange_wait(sent0a, True, small_res[0][0], "grads_wait0a")
    in0, grp0 = _exchange_wait(sent0b, True, small_res[0][0], "grads_wait0b")
    def both_layers(first, second, w, m, v, name):
        view = (w.shape[0], -1, w.shape[-1])
        landed = [first.reshape((N_DEV,) + w.reshape(view).shape[1:]), second.reshape((N_DEV,) + w.reshape(view).shape[1:])]
        return [r.reshape(w.shape) for r in _adamw_layers(w.reshape(view), m.reshape(view), v.reshape(view), landed, name)]

    res["pool_w_in"] = both_layers(in0, in3, pool_w_in, m_pool_w_in, v_pool_w_in, "adamw_pool_in")
    res["pool_w_grp"] = both_layers(grp0, grp3, pool_w_grp, m_pool_w_grp, v_pool_w_grp, "adamw_pool_grp")
    res["pool_w_out"] = both_layers(out0, out3, pool_w_out, m_pool_w_out, v_pool_w_out, "adamw_pool_out")

    order = ["c_ctx", "norm_g", "ada_w", "ada_b", "pool_w_in", "pool_w_grp", "pool_scale", "pool_w_out", "na_w_in",
             "na_rpb", "na_w_out", "conv_w_in", "conv_dw", "conv_db", "conv_w_out", "final_g"]
    outs = [loss, grad_x]
    for j in range(4):
        outs += [res[n][j] for n in order]
    return tuple(outs)
```

```python
import functools
import math

import numpy as np
import jax
import jax.numpy as jnp
from jax import lax
from jax.experimental import pallas as pl
from jax.experimental.pallas import tpu as pltpu

F32 = jnp.float32
BF16 = jnp.bfloat16
N_DEV = 8
LANES = 128
RMS_EPS = 1e-6
GRID_W = 64
WIN_ROWS = 8
WIN_COLS = 16
HEAD_DIM = 64
POOL_WINDOWS = (2, 4, 8, 16)
HALO = 8
CHUNK = 256
MASKED = -1e30
ADAM_LR = 0.001
ADAM_B1 = 0.9
ADAM_B2 = 0.999
ADAM_EPS = 1e-08
ADAM_WD = 0.01
ADAM_STEP = 10
VMEM_LIMIT = 56 * 1024 * 1024
MESH = pl.DeviceIdType.MESH
ANY = pl.BlockSpec(memory_space=pl.ANY)
HBM = pl.BlockSpec(memory_space=pltpu.HBM)
SEM = pl.BlockSpec(memory_space=pltpu.SEMAPHORE)
EFFECT = pltpu.SideEffectType.DATAFLOW_SIDE_EFFECTING


def _pc(body, *, name, **kw):
    return pl.pallas_call(body, name=name, **kw)


def _params(*sem):
    return pltpu.CompilerParams(dimension_semantics=sem if sem else None, vmem_limit_bytes=VMEM_LIMIT)


def _dot(a, b, ca=1, cb=0, precision=None):
    return lax.dot_general(a, b, (((ca,), (cb,)), ((), ())), preferred_element_type=F32, precision=precision)


def _tile(n, pref, unit=LANES):
    best = None
    for t in range(unit, min(n, pref) + 1, unit):
        if n % t == 0:
            best = t
    return best if best is not None else n


def _sigmoid(x):
    return 1.0 / (1.0 + jnp.exp(-x))


def _silu(x):
    return x * _sigmoid(x)


def _dsilu(x):
    s = _sigmoid(x)
    return s * (1.0 + x * (1.0 - s))


def _my_place():
    return lax.axis_index("x"), lax.axis_index("y"), lax.axis_index("c")


def _flip(v, f):
    return 1 - v if f else v


def _gather_small(block, name, per_dest=False):
    rows, cols = block.shape[-2:]

    def body(x_ref, out_ref, send_sems, recv_sems):
        x, y, c = _my_place()
        me = 4 * x + 2 * y + c
        out_ref[me] = x_ref[me] if per_dest else x_ref[...]
        copies = []
        for k in range(1, N_DEV):
            peer = (_flip(x, k & 4), _flip(y, k & 2), _flip(c, k & 1))
            dest = 4 * peer[0] + 2 * peer[1] + peer[2]
            cp = pltpu.make_async_remote_copy(
                src_ref=x_ref.at[dest] if per_dest else x_ref, dst_ref=out_ref.at[me],
                send_sem=send_sems.at[k - 1], recv_sem=recv_sems.at[k - 1], device_id=peer, device_id_type=MESH)
            cp.start()
            copies.append(cp)
        for cp in copies:
            cp.wait()

    return _pc(
        body, name=name,
        out_shape=jax.ShapeDtypeStruct((N_DEV, rows, cols), block.dtype),
        in_specs=[pl.BlockSpec(memory_space=pltpu.VMEM)],
        out_specs=pl.BlockSpec(memory_space=pltpu.VMEM),
        scratch_shapes=[pltpu.SemaphoreType.DMA((N_DEV - 1,)), pltpu.SemaphoreType.DMA((N_DEV - 1,))],
    )(block)


ALL_PEERS = tuple(range(N_DEV))
CHIP_PEERS = (0, 1, 2, 4, 6)
OTHER_CHIPS = (2, 4, 6)


def _flag(per_dest, t):
    return per_dest[t] if isinstance(per_dest, (list, tuple)) else per_dest


def _peer(k):
    x, y, c = _my_place()
    peer = (_flip(x, k & 4), _flip(y, k & 2), _flip(c, k & 1))
    return peer, 4 * peer[0] + 2 * peer[1] + peer[2]


def _peer_lists(peers, nt):
    return list(peers) if isinstance(peers, list) else [peers] * nt


def _exchange_copies(srcs, lands, send_sems, recv_sems, per_dest, peers=ALL_PEERS):
    x, y, c = _my_place()
    me = 4 * x + 2 * y + c
    copies = []
    for t, (src, land, ks) in enumerate(zip(srcs, lands, _peer_lists(peers, len(srcs)))):
        for k in ks:
            peer, dest = _peer(k)
            s = len(copies)
            copies.append(pltpu.make_async_remote_copy(
                src_ref=src.at[dest] if _flag(per_dest, t) else src, dst_ref=land.at[me],
                send_sem=send_sems[s], recv_sem=recv_sems[s], device_id=peer, device_id_type=MESH))
    return copies


def _forward_copies(lands, send_sems, recv_sems):
    sibling, _ = _peer(1)
    copies = []
    for t, land in enumerate(lands):
        for n, k in enumerate(OTHER_CHIPS):
            _, slot = _peer(k)
            s = t * len(OTHER_CHIPS) + n
            copies.append(pltpu.make_async_remote_copy(
                src_ref=land.at[slot], dst_ref=land.at[slot], send_sem=send_sems[s], recv_sem=recv_sems[s],
                device_id=sibling, device_id_type=MESH))
    return copies


def _forward_start(lands, after, name):
    nt = len(lands)
    ns = nt * len(OTHER_CHIPS)

    def body(*refs):
        ins, outs = refs[:nt + 1], refs[nt + 1:]
        for cp in _forward_copies(ins[:nt], outs[:ns], outs[ns:2 * ns]):
            cp.start()
        outs[-1][...] = jnp.zeros_like(outs[-1])

    res = _pc(
        body, name=name,
        out_shape=(*[pltpu.SemaphoreType.DMA(())] * (2 * ns), *[pltpu.HBM(a.shape, a.dtype) for a in lands],
                   jax.ShapeDtypeStruct((8, LANES), F32)),
        in_specs=[HBM] * nt + [ANY],
        out_specs=(*[SEM] * (2 * ns), *[HBM] * nt, pl.BlockSpec(memory_space=pltpu.VMEM)),
        input_output_aliases={i: 2 * ns + i for i in range(nt)},
        compiler_params=pltpu.CompilerParams(has_side_effects=EFFECT),
    )(*lands, after)
    return list(res[:ns]), list(res[ns:2 * ns]), list(res[2 * ns:2 * ns + nt]), res[-1]


def _forward_wait(state, after, name):
    send_sems, recv_sems, lands, _ = state
    nt, ns = len(lands), len(send_sems)

    def body(*refs):
        sems = refs[nt:nt + 2 * ns]
        for cp in _forward_copies(refs[:nt], sems[:ns], sems[ns:]):
            cp.wait_send()
            cp.wait_recv()

    res = _pc(
        body, name=name,
        out_shape=tuple(pltpu.HBM(a.shape, a.dtype) for a in lands),
        in_specs=[HBM] * nt + [SEM] * (2 * ns) + [ANY],
        out_specs=tuple([HBM] * nt),
        input_output_aliases={i: i for i in range(nt)},
        compiler_params=pltpu.CompilerParams(has_side_effects=EFFECT),
    )(*lands, *send_sems, *recv_sems, after)
    return list(res)


def _exchange_start(srcs, per_dest, after, name, peers=ALL_PEERS):
    nt = len(srcs)
    peers = _peer_lists(peers, nt)
    ns = sum(len(ks) for ks in peers)
    lands = [lax.empty((N_DEV,) + (s.shape[1:] if _flag(per_dest, t) else s.shape), s.dtype) for t, s in enumerate(srcs)]

    def body(*refs):
        ins, outs = refs[:2 * nt + 1], refs[2 * nt + 1:]
        for cp in _exchange_copies(ins[:nt], ins[nt:2 * nt], outs[:ns], outs[ns:2 * ns], per_dest, peers):
            cp.start()
        outs[-1][...] = jnp.zeros_like(outs[-1])

    hbm = [pltpu.with_memory_space_constraint(a, pltpu.HBM) for a in list(srcs) + lands]
    res = _pc(
        body, name=name,
        out_shape=(*[pltpu.SemaphoreType.DMA(())] * (2 * ns), *[pltpu.HBM(a.shape, a.dtype) for a in hbm],
                   jax.ShapeDtypeStruct((8, LANES), F32)),
        in_specs=[HBM] * (2 * nt) + [ANY],
        out_specs=(*[SEM] * (2 * ns), *[HBM] * (2 * nt), pl.BlockSpec(memory_space=pltpu.VMEM)),
        input_output_aliases={i: 2 * ns + i for i in range(2 * nt)},
        compiler_params=pltpu.CompilerParams(has_side_effects=EFFECT),
    )(*hbm, after)
    sems, rest = res[:2 * ns], res[2 * ns:]
    return list(sems[:ns]), list(sems[ns:]), list(rest[:nt]), list(rest[nt:2 * nt]), peers, rest[-1]


def _exchange_wait(state, per_dest, after, name, which=None):
    send_sems, recv_sems, srcs, lands, peers, _ = state
    which = list(range(len(srcs))) if which is None else which
    per_dest = [_flag(per_dest, t) for t in which]
    after = list(after) if isinstance(after, (list, tuple)) else [after]
    first = np.cumsum([0] + [len(ks) for ks in peers])
    pick = [first[t] + n for t in which for n in range(len(peers[t]))]
    peers = [peers[t] for t in which]
    send_sems, recv_sems = [send_sems[s] for s in pick], [recv_sems[s] for s in pick]
    srcs, lands = [srcs[t] for t in which], [lands[t] for t in which]
    nt = len(srcs)
    ns = len(send_sems)

    def body(*refs):
        sems = refs[2 * nt:2 * nt + 2 * ns]
        for cp in _exchange_copies(refs[:nt], refs[nt:2 * nt], sems[:ns], sems[ns:], per_dest, peers):
            cp.wait_send()
            cp.wait_recv()

    thru = list(srcs) + list(lands)
    res = _pc(
        body, name=name,
        out_shape=tuple(pltpu.HBM(a.shape, a.dtype) for a in thru),
        in_specs=[HBM] * (2 * nt) + [SEM] * (2 * ns) + [ANY] * len(after),
        out_specs=tuple([HBM] * (2 * nt)),
        input_output_aliases={i: i for i in range(2 * nt)},
        compiler_params=pltpu.CompilerParams(has_side_effects=EFFECT),
    )(*thru, *send_sems, *recv_sems, *after)
    return list(res[nt:])


def _mod_fwd(cond, ada_w, bias):
    depth, d, nb = ada_w.shape

    def body(c_ref, w_ref, b_ref, o_ref):
        s = _silu(c_ref[...]).astype(BF16)
        o_ref[...] = _dot(s, w_ref[...].astype(BF16)) + b_ref[...]

    return _pc(
        body, name="mod_fwd", grid=(depth,),
        in_specs=[pl.BlockSpec((16, d), lambda i: (0, 0)), pl.BlockSpec((None, d, nb), lambda i: (i, 0, 0)),
                  pl.BlockSpec((None, 1, nb), lambda i: (i, 0, 0))],
        out_specs=pl.BlockSpec((None, 16, nb), lambda i: (i, 0, 0)),
        out_shape=jax.ShapeDtypeStruct((depth, 16, nb), F32),
        compiler_params=_params("parallel"),
    )(cond, ada_w, bias.reshape(depth, 1, nb))


def _mod_bwd(cond, ada_w, dm_all, dm_mine):
    depth, d, nb = ada_w.shape
    d3 = dm_all.shape[-1]

    def body(c_ref, w_ref, all_ref, call_ref, mine_ref, cmine_ref, gw_ref, gb_ref, part_ref, ds_ref):
        i = pl.program_id(0)
        cond_v = c_ref[...]
        s = _silu(cond_v).astype(BF16)
        has_ctx = jnp.where(i < 2, 1.0, 0.0)
        tot_all = jnp.sum(call_ref[...], axis=0, keepdims=True) * has_ctx
        tot_mine = jnp.broadcast_to(jnp.sum(cmine_ref[...], axis=0, keepdims=True) * has_ctx, (8, nb)).astype(BF16)
        gb_ref[...] = jnp.sum(all_ref[...], axis=0, keepdims=True) + tot_all
        gw_ref[...] = _dot(s[0:8], mine_ref[...].astype(BF16), 0, 0) + _dot(s[8:16], tot_mine, 0, 0)
        part = _dot(tot_mine, w_ref[...].astype(BF16), 1, 1)

        @pl.when(i == 0)
        def _():
            part_ref[...] = jnp.zeros_like(part_ref)
            ds_ref[...] = _dsilu(cond_v)

        part_ref[...] += part

    def rows(width, which):
        return pl.BlockSpec((None, N_DEV, width), which)

    layer = lambda i: (i, 0, 0)
    ctx_layer = lambda i: (jnp.minimum(i, 1) + 4, 0, 0)
    return _pc(
        body, name="mod_bwd", grid=(depth,),
        in_specs=[pl.BlockSpec((16, d), lambda i: (0, 0)), pl.BlockSpec((None, d, nb), layer),
                  rows(d3, layer), rows(d3, ctx_layer), rows(nb, layer), rows(nb, ctx_layer)],
        out_specs=[pl.BlockSpec((None, d, nb), layer), pl.BlockSpec((None, 1, d3), layer),
                   pl.BlockSpec((8, d), lambda i: (0, 0)), pl.BlockSpec((16, d), lambda i: (0, 0))],
        out_shape=[jax.ShapeDtypeStruct((depth, d, nb), F32), jax.ShapeDtypeStruct((depth, 1, d3), F32),
                   jax.ShapeDtypeStruct((8, d), F32), jax.ShapeDtypeStruct((16, d), F32)],
        compiler_params=_params("arbitrary"),
    )(cond, ada_w, dm_all, dm_all, dm_mine, dm_mine)


def _norm_fwd(xs, g, mod, tr, seg_tiles, name):
    t, d = xs.shape

    def body(x_ref, g_ref, mod_ref, h_ref):
        x = x_ref[...]
        r = lax.rsqrt(jnp.mean(x * x, axis=-1, keepdims=True) + RMS_EPS)
        y = (x * r) * g_ref[...]
        h_ref[...] = (y * (1.0 + mod_ref[1:2, :]) + mod_ref[0:1, :]).astype(BF16)

    return _pc(
        body, name=name, grid=(t // tr,),
        in_specs=[pl.BlockSpec((tr, d), lambda i: (i, 0)), pl.BlockSpec((1, d), lambda i: (0, 0)),
                  pl.BlockSpec((None, 8, d), lambda i: (i // seg_tiles, 0, 0))],
        out_specs=pl.BlockSpec((tr, d), lambda i: (i, 0)),
        out_shape=jax.ShapeDtypeStruct((t, d), BF16),
        compiler_params=_params("parallel"),
    )(xs, g, mod)


def _resid_grad(dx, i, seg_tiles, yx_ref, gate_ref, dyx_ref, gsum_ref):
    dyx_ref[...] = (dx * gate_ref[2:3, :]).astype(BF16)

    @pl.when(i % seg_tiles == 0)
    def _():
        gsum_ref[...] = jnp.zeros_like(gsum_ref)

    gsum_ref[0:1, :] += jnp.sum(dx * yx_ref[...], axis=0, keepdims=True)


def _norm_bwd(xs, dh, dres, g, mod, tr, seg_tiles, name, res_tiles=None, out_tiles=None, below=None):
    t, d = xs.shape
    n_tiles = t // tr
    res_tiles = n_tiles if res_tiles is None else res_tiles
    out_tiles = n_tiles if out_tiles is None else out_tiles

    def body(x_ref, dh_ref, dres_ref, g_ref, mod_ref, *rest):
        i = pl.program_id(0)
        x = x_ref[...]
        r = lax.rsqrt(jnp.mean(x * x, axis=-1, keepdims=True) + RMS_EPS)
        xn = x * r
        dhv = dh_ref[...]
        gain = g_ref[...]
        one_scale = 1.0 + mod_ref[1:2, :]
        dxn = dhv * (gain * one_scale)
        dx = r * (dxn - xn * jnp.mean(dxn * xn, axis=-1, keepdims=True))
        if res_tiles == n_tiles:
            dx = dx + dres_ref[...]
        else:
            dx = dx + jnp.where(i < res_tiles, dres_ref[...], 0.0)
        if below is None:
            dx_ref, sum_ref = rest
        else:
            yx_ref, gate_ref, dx_ref, sum_ref, dyx_ref, gsum_ref = rest
            _resid_grad(dx, i, seg_tiles, yx_ref, gate_ref, dyx_ref, gsum_ref)
        if out_tiles == n_tiles:
            dx_ref[...] = dx
        else:
            @pl.when(i < out_tiles)
            def _():
                dx_ref[...] = dx

        @pl.when(i % seg_tiles == 0)
        def _():
            sum_ref[...] = jnp.zeros_like(sum_ref)

        sum_ref[0:1, :] += jnp.sum(dhv, axis=0, keepdims=True)
        sum_ref[1:2, :] += jnp.sum(dhv * (xn * gain), axis=0, keepdims=True)
        sum_ref[2:3, :] += jnp.sum(dhv * one_scale * xn, axis=0, keepdims=True)

    row = pl.BlockSpec((tr, d), lambda i: (i, 0))
    seg = pl.BlockSpec((None, 8, d), lambda i: (i // seg_tiles, 0, 0))
    in_specs = [row, row, pl.BlockSpec((tr, d), lambda i: (jnp.minimum(i, res_tiles - 1), 0)),
                pl.BlockSpec((1, d), lambda i: (0, 0)), seg]
    out_specs = [pl.BlockSpec((tr, d), lambda i: (jnp.minimum(i, out_tiles - 1), 0)), seg]
    out_shape = [jax.ShapeDtypeStruct((out_tiles * tr, d), F32), jax.ShapeDtypeStruct((mod.shape[0], 8, d), F32)]
    args = [xs, dh, dres, g, mod]
    if below is not None:
        in_specs += [row, seg]
        out_specs += [row, seg]
        out_shape += [jax.ShapeDtypeStruct((t, d), BF16), jax.ShapeDtypeStruct((below[1].shape[0], 8, d), F32)]
        args += list(below)
    return _pc(
        body, name=name, grid=(n_tiles,), in_specs=in_specs, out_specs=out_specs, out_shape=out_shape,
        compiler_params=_params("arbitrary"),
    )(*args)


def _loss_head(xs, target, g, yx, mod, tr):
    t, d = xs.shape

    def body(x_ref, t_ref, g_ref, yx_ref, gate_ref, loss_ref, dx_ref, dg_ref, dyx_ref, gsum_ref):
        i = pl.program_id(0)
        x = x_ref[...]
        r = lax.rsqrt(jnp.mean(x * x, axis=-1, keepdims=True) + RMS_EPS)
        xn = x * r
        gain = g_ref[...]
        err = xn * gain - t_ref[...]
        dy = err * (1.0 / d)
        dxn = dy * gain
        dx = r * (dxn - xn * jnp.mean(dxn * xn, axis=-1, keepdims=True))
        dx_ref[...] = dx
        _resid_grad(dx, i, t // tr, yx_ref, gate_ref, dyx_ref, gsum_ref)

        @pl.when(i == 0)
        def _():
            loss_ref[...] = jnp.zeros_like(loss_ref)
            dg_ref[...] = jnp.zeros_like(dg_ref)

        loss_ref[...] += 0.5 * jnp.sum(jnp.mean(err * err, axis=-1, keepdims=True))
        dg_ref[0:1, :] += jnp.sum(dy * xn, axis=0, keepdims=True)

    row = pl.BlockSpec((tr, d), lambda i: (i, 0))
    seg = pl.BlockSpec((None, 8, d), lambda i: (0, 0, 0))
    return _pc(
        body, name="loss_head", grid=(t // tr,),
        in_specs=[row, row, pl.BlockSpec((1, d), lambda i: (0, 0)), row, seg],
        out_specs=[pl.BlockSpec((8, LANES), lambda i: (0, 0)), row, pl.BlockSpec((8, d), lambda i: (0, 0)), row, seg],
        out_shape=[jax.ShapeDtypeStruct((8, LANES), F32), jax.ShapeDtypeStruct((t, d), F32),
                   jax.ShapeDtypeStruct((8, d), F32), jax.ShapeDtypeStruct((t, d), BF16),
                   jax.ShapeDtypeStruct((1, 8, d), F32)],
        compiler_params=_params("arbitrary"),
    )(xs, target, g, yx, mod)


def _proj_in(h, w, layer, width, name, blocks=None, dtype=F32):
    t, d = h.shape
    n8 = w.shape[-1]
    first, count = blocks if blocks is not None else (0, N_DEV)
    per_part = width // n8
    tm = _tile(t, 1152)

    def body(a_ref, b_ref, o_ref):
        o_ref[...] = _dot(a_ref[...], b_ref[...]).astype(dtype)

    return _pc(
        body, name=name, grid=(t // tm, count),
        in_specs=[pl.BlockSpec((tm, d), lambda i, j: (i, 0)),
                  pl.BlockSpec((None, None, d, n8), lambda i, j: (first + j, layer, 0, 0))],
        out_specs=pl.BlockSpec((None, tm, n8), lambda i, j: (j // per_part, i, j % per_part)),
        out_shape=jax.ShapeDtypeStruct((count // per_part, t, width), dtype),
        compiler_params=_params("parallel", "parallel"),
    )(h, w)


def _proj_out(z, w, res, mod, tm, seg_tiles, name, nxt=None):
    t, k = z.shape
    d = w.shape[1]

    def body(z_ref, w_ref, res_ref, mod_ref, *rest):
        yx = _dot(z_ref[...], w_ref[...])
        x = res_ref[...] + mod_ref[2:3, :] * yx
        if nxt is None:
            yx_ref, x_ref = rest
        else:
            g_ref, nmod_ref, yx_ref, x_ref, h_ref = rest
            r = lax.rsqrt(jnp.mean(x * x, axis=-1, keepdims=True) + RMS_EPS)
            h_ref[...] = (((x * r) * g_ref[...]) * (1.0 + nmod_ref[1:2, :]) + nmod_ref[0:1, :]).astype(BF16)
        yx_ref[...] = yx
        x_ref[...] = x

    tile = pl.BlockSpec((tm, d), lambda i: (i, 0))
    seg = pl.BlockSpec((None, 8, d), lambda i: (i // seg_tiles, 0, 0))
    in_specs = [pl.BlockSpec((tm, k), lambda i: (i, 0)), pl.BlockSpec((k, d), lambda i: (0, 0)), tile, seg]
    out_specs = [tile, tile]
    out_shape = [jax.ShapeDtypeStruct((t, d), F32), jax.ShapeDtypeStruct((t, d), F32)]
    args = [z, w, res, mod]
    if nxt is not None:
        in_specs += [pl.BlockSpec((1, d), lambda i: (0, 0)), seg]
        out_specs.append(tile)
        out_shape.append(jax.ShapeDtypeStruct((t, d), BF16))
        args += list(nxt)
    return _pc(
        body, name=name, grid=(t // tm,), in_specs=in_specs, out_specs=out_specs, out_shape=out_shape,
        compiler_params=_params("parallel"),
    )(*args)


def _proj_out_dz(dyx, w, name):
    t, d = dyx.shape
    width = w.shape[0]
    tm, tn = _tile(t, 1024), _tile(width, 512)

    def body(a_ref, w_ref, o_ref):
        o_ref[...] = _dot(a_ref[...], w_ref[...], 1, 1)

    return _pc(
        body, name=name, grid=(t // tm, width // tn),
        in_specs=[pl.BlockSpec((tm, d), lambda i, j: (i, 0)), pl.BlockSpec((tn, d), lambda i, j: (j, 0))],
        out_specs=pl.BlockSpec((tm, tn), lambda i, j: (i, j)),
        out_shape=jax.ShapeDtypeStruct((t, width), F32),
        compiler_params=_params("parallel", "parallel"),
    )(dyx, w)


def _proj_in_dh(dpre, w, layer, name, after=None):
    parts, t, width = dpre.shape
    d, n8 = w.shape[-2:]
    per_part = width // n8
    tm, tn = _tile(t, 1152), _tile(d, 512)

    def body(a_ref, w_ref, *rest):
        o_ref = rest[-1]
        part = _dot(a_ref[:, 0:n8], w_ref[0], 1, 1)
        for s in range(1, per_part):
            part += _dot(a_ref[:, s * n8:(s + 1) * n8], w_ref[s], 1, 1)

        @pl.when(pl.program_id(2) == 0)
        def _():
            o_ref[...] = part

        @pl.when(pl.program_id(2) != 0)
        def _():
            o_ref[...] += part

    extra = [] if after is None else [after]
    return _pc(
        body, name=name, grid=(t // tm, d // tn, parts),
        in_specs=[pl.BlockSpec((None, tm, width), lambda i, j, k: (k, i, 0)),
                  pl.BlockSpec((per_part, None, tn, n8), lambda i, j, k: (k, layer, j, 0))] + [ANY] * len(extra),
        out_specs=pl.BlockSpec((tm, tn), lambda i, j, k: (i, j)),
        out_shape=jax.ShapeDtypeStruct((t, d), F32),
        compiler_params=_params("parallel", "parallel", "arbitrary"),
    )(dpre, w, *extra)


def _transposed(a_ref):
    return a_ref[...].T


def _grad_w_in(h, dpre, n8, name):
    t, d = h.shape
    parts, _, width = dpre.shape
    per_part = width // n8
    tm, tk = _tile(d, 512), _tile(t, 1152)
    nk = t // tk

    def body(a_ref, b_ref, o_ref, acc_ref):
        k = pl.program_id(1)

        @pl.when(k == 0)
        def _():
            acc_ref[...] = jnp.zeros_like(acc_ref)

        at = _transposed(a_ref)
        for p in range(parts):
            r = _dot(at, b_ref[p])
            for s in range(per_part):
                acc_ref[p * per_part + s] += r[:, s * n8:(s + 1) * n8]

        @pl.when(k == nk - 1)
        def _():
            o_ref[...] = acc_ref[...].astype(BF16)

    return _pc(
        body, name=name, grid=(d // tm, nk),
        in_specs=[pl.BlockSpec((tk, tm), lambda i, k: (k, i)), pl.BlockSpec((parts, tk, width), lambda i, k: (0, k, 0))],
        out_specs=pl.BlockSpec((parts * per_part, tm, n8), lambda i, k: (0, i, 0)),
        out_shape=jax.ShapeDtypeStruct((parts * per_part, d, n8), BF16),
        scratch_shapes=[pltpu.VMEM((parts * per_part, tm, n8), F32)],
        compiler_params=_params("parallel", "arbitrary"),
    )(h, dpre)


def _grad_w_out(z, dyx, name):
    width = z.shape[1]
    t, d = dyx.shape
    tm, tk = _tile(width, 512), _tile(t, 1152)
    nk = t // tk

    def body(a_ref, b_ref, o_ref, acc_ref):
        k = pl.program_id(1)

        @pl.when(k == 0)
        def _():
            acc_ref[...] = jnp.zeros_like(acc_ref)

        acc_ref[...] += _dot(_transposed(a_ref), b_ref[...])

        @pl.when(k == nk - 1)
        def _():
            o_ref[...] = acc_ref[...].astype(BF16)

    return _pc(
        body, name=name, grid=(width // tm, nk),
        in_specs=[pl.BlockSpec((tk, tm), lambda i, k: (k, i)), pl.BlockSpec((tk, d), lambda i, k: (k, 0))],
        out_specs=pl.BlockSpec((tm, d), lambda i, k: (i, 0)),
        out_shape=jax.ShapeDtypeStruct((width, d), BF16),
        scratch_shapes=[pltpu.VMEM((tm, d), F32)],
        compiler_params=_params("parallel", "arbitrary"),
    )(z, dyx)


def _shift(v, k):
    n = v.shape[0]
    return pltpu.roll(v, k % n, 0)


def _window_sum(v, win):
    s = v + _shift(v, 1)
    step = 1
    while 2 * step < win:
        s = _shift(s, step) + _shift(s, -step)
        step *= 2
    return s


def _window_count(base, seg_len, win, shape):
    t = base + lax.broadcasted_iota(jnp.int32, shape, 0)
    hi = jnp.minimum(t + win // 2, seg_len)
    lo = jnp.maximum(t - win // 2, 0)
    return (hi - lo).astype(F32)


def _pad_offsets(segs):
    return [HALO * (s + 1) + st for s, (st, _) in enumerate(segs)]


def _for_chunks(segs, fn):
    offs = _pad_offsets(segs)
    for s, (st, ln) in enumerate(segs):
        def step(ci, carry, s=s, st=st, ln=ln):
            fn(s, st, ln, offs[s], pl.multiple_of(ci * CHUNK, CHUNK))
            return carry
        lax.fori_loop(0, ln // CHUNK, step, 0)


def _pool_fwd(pre, w_grp, scale, segs, name):
    _, t, width = pre.shape
    grp = width // len(POOL_WINDOWS)
    padded = t + HALO * (len(segs) + 1)

    def group(win, pre_ref, w_ref, sc_ref, z_ref, diff_ref, pad_ref):
        pad_ref[...] = jnp.zeros_like(pad_ref)

        def fill(s, st, ln, off, b):
            pad_ref[pl.ds(off + b, CHUNK), :] = pre_ref[0, pl.ds(st + b, CHUNK), :]

        _for_chunks(segs, fill)

        def mix(s, st, ln, off, b):
            ext = pad_ref[pl.ds(off - HALO + b, CHUNK + 2 * HALO), :]
            total = _window_sum(ext, win)[HALO:HALO + CHUNK]
            u = pre_ref[0, pl.ds(st + b, CHUNK), :]
            diff = (total / _window_count(b, ln, win, u.shape) - u).astype(BF16)
            mixed = _dot(diff, w_ref[...])
            gate = _silu(pre_ref[1, pl.ds(st + b, CHUNK), :])
            z_ref[pl.ds(st + b, CHUNK), :] = (mixed * sc_ref[...] * gate).astype(BF16)
            diff_ref[pl.ds(st + b, CHUNK), :] = diff

        _for_chunks(segs, mix)

    def body(pre_ref, w_ref, sc_ref, z_ref, diff_ref, pad_ref):
        gi = pl.program_id(0)
        for widx, win in enumerate(POOL_WINDOWS):
            @pl.when(gi == widx)
            def _(win=win):
                group(win, pre_ref, w_ref, sc_ref, z_ref, diff_ref, pad_ref)

    col = pl.BlockSpec((t, grp), lambda g: (0, g))
    return _pc(
        body, name=name, grid=(len(POOL_WINDOWS),),
        in_specs=[pl.BlockSpec((2, t, grp), lambda g: (0, 0, g)), pl.BlockSpec((None, grp, grp), lambda g: (g, 0, 0)),
                  pl.BlockSpec((1, grp), lambda g: (0, g))],
        out_specs=[col, col],
        out_shape=[jax.ShapeDtypeStruct((t, width), BF16), jax.ShapeDtypeStruct((t, width), BF16)],
        scratch_shapes=[pltpu.VMEM((padded, grp), F32)],
        compiler_params=_params("parallel"),
    )(pre, w_grp, scale)


def _pool_bwd(dz, diff, pre, w_grp, scale, segs, name):
    _, t, width = pre.shape
    grp = width // len(POOL_WINDOWS)
    padded = t + HALO * (len(segs) + 1)

    def group(win, dz_ref, diff_ref, pre_ref, w_ref, sc_ref, dpre_ref, dw_ref, dsc_ref, pad_ref, dd_ref):
        pad_ref[...] = jnp.zeros_like(pad_ref)
        dw_ref[...] = jnp.zeros_like(dw_ref)
        dsc_ref[...] = jnp.zeros_like(dsc_ref)

        def first(s, st, ln, off, b):
            rows = pl.ds(st + b, CHUNK)
            diff_v = diff_ref[rows, :]
            mixed = _dot(diff_v, w_ref[...])
            g = pre_ref[1, rows, :]
            sg = _silu(g)
            dzv = dz_ref[rows, :]
            dmixed = (dzv * sc_ref[...] * sg).astype(BF16)
            dsc_ref[...] += jnp.sum(dzv * mixed * sg, axis=0, keepdims=True)
            dpre_ref[1, rows, :] = (dzv * mixed * sc_ref[...] * _dsilu(g)).astype(BF16)
            ddiff = _dot(dmixed, w_ref[...], 1, 1)
            dw_ref[...] += _dot(diff_v, dmixed, 0, 0)
            dd_ref[rows, :] = ddiff
            pad_ref[pl.ds(off + b, CHUNK), :] = ddiff / _window_count(b, ln, win, ddiff.shape)

        _for_chunks(segs, first)

        def second(s, st, ln, off, b):
            rows = pl.ds(st + b, CHUNK)
            ext = pad_ref[pl.ds(off - HALO + b, CHUNK + 2 * HALO), :]
            total = _shift(_window_sum(ext, win), -1)[HALO:HALO + CHUNK]
            dpre_ref[0, rows, :] = (total - dd_ref[rows, :]).astype(BF16)

        _for_chunks(segs, second)

    def body(dz_ref, diff_ref, pre_ref, w_ref, sc_ref, dpre_ref, dw_ref, dsc_ref, pad_ref, dd_ref):
        gi = pl.program_id(0)
        for widx, win in enumerate(POOL_WINDOWS):
            @pl.when(gi == widx)
            def _(win=win):
                group(win, dz_ref, diff_ref, pre_ref, w_ref, sc_ref, dpre_ref, dw_ref, dsc_ref, pad_ref, dd_ref)

    col = pl.BlockSpec((t, grp), lambda g: (0, g))
    both = pl.BlockSpec((2, t, grp), lambda g: (0, 0, g))
    wspec = pl.BlockSpec((None, grp, grp), lambda g: (g, 0, 0))
    sspec = pl.BlockSpec((1, grp), lambda g: (0, g))
    return _pc(
        body, name=name, grid=(len(POOL_WINDOWS),),
        in_specs=[col, col, both, wspec, sspec],
        out_specs=[both, wspec, sspec],
        out_shape=[jax.ShapeDtypeStruct((2, t, width), BF16), jax.ShapeDtypeStruct((len(POOL_WINDOWS), grp, grp), F32),
                   jax.ShapeDtypeStruct((1, width), F32)],
        scratch_shapes=[pltpu.VMEM((padded, grp), F32), pltpu.VMEM((t, grp), F32)],
        compiler_params=_params("parallel"),
    )(dz, diff, pre, w_grp, scale)


def _conv_fwd(pre, dw, db, name):
    _, t, width = pre.shape
    cb = LANES
    segs = [(0, t)]

    def body(pre_ref, dw_ref, db_ref, z_ref, pad_ref):
        pad_ref[...] = jnp.zeros_like(pad_ref)

        def fill(s, st, ln, off, b):
            rows = pl.ds(b, CHUNK)
            pad_ref[pl.ds(off + b, CHUNK), :] = pre_ref[1, rows, :] * pre_ref[2, rows, :]

        _for_chunks(segs, fill)

        def mix(s, st, ln, off, b):
            rows = pl.ds(b, CHUNK)
            ext = pad_ref[pl.ds(off - HALO + b, CHUNK + 2 * HALO), :]
            conv = (dw_ref[0:1, :] * _shift(ext, 1) + dw_ref[1:2, :] * ext + dw_ref[2:3, :] * _shift(ext, -1))
            conv = conv[HALO:HALO + CHUNK] + db_ref[...]
            y = pre_ref[0, rows, :] * conv
            z_ref[rows, :] = (y * _silu(pre_ref[3, rows, :])).astype(BF16)

        _for_chunks(segs, mix)

    return _pc(
        body, name=name, grid=(width // cb,),
        in_specs=[pl.BlockSpec((4, t, cb), lambda j: (0, 0, j)), pl.BlockSpec((8, cb), lambda j: (0, j)),
                  pl.BlockSpec((1, cb), lambda j: (0, j))],
        out_specs=pl.BlockSpec((t, cb), lambda j: (0, j)),
        out_shape=jax.ShapeDtypeStruct((t, width), BF16),
        scratch_shapes=[pltpu.VMEM((t + 2 * HALO, cb), F32)],
        compiler_params=_params("parallel"),
    )(pre, dw, db)


def _conv_bwd(dz, pre, dw, db, name):
    _, t, width = pre.shape
    cb = LANES
    segs = [(0, t)]

    def body(dz_ref, pre_ref, dw_ref, db_ref, dpre_ref, ddw_ref, ddb_ref, pad_a, pad_c):
        pad_a[...] = jnp.zeros_like(pad_a)
        pad_c[...] = jnp.zeros_like(pad_c)
        ddw_ref[...] = jnp.zeros_like(ddw_ref)
        ddb_ref[...] = jnp.zeros_like(ddb_ref)

        def fill(s, st, ln, off, b):
            rows = pl.ds(b, CHUNK)
            pad_a[pl.ds(off + b, CHUNK), :] = pre_ref[1, rows, :] * pre_ref[2, rows, :]

        _for_chunks(segs, fill)

        def first(s, st, ln, off, b):
            rows = pl.ds(b, CHUNK)
            ext = pad_a[pl.ds(off - HALO + b, CHUNK + 2 * HALO), :]
            prev, nxt = _shift(ext, 1)[HALO:HALO + CHUNK], _shift(ext, -1)[HALO:HALO + CHUNK]
            here = ext[HALO:HALO + CHUNK]
            conv = dw_ref[0:1, :] * prev + dw_ref[1:2, :] * here + dw_ref[2:3, :] * nxt + db_ref[...]
            bg, g = pre_ref[0, rows, :], pre_ref[3, rows, :]
            dzv = dz_ref[rows, :]
            dy = dzv * _silu(g)
            dpre_ref[3, rows, :] = (dzv * (bg * conv) * _dsilu(g)).astype(BF16)
            dpre_ref[0, rows, :] = (dy * conv).astype(BF16)
            dconv = dy * bg
            pad_c[pl.ds(off + b, CHUNK), :] = dconv
            ddw_ref[0:1, :] += jnp.sum(dconv * prev, axis=0, keepdims=True)
            ddw_ref[1:2, :] += jnp.sum(dconv * here, axis=0, keepdims=True)
            ddw_ref[2:3, :] += jnp.sum(dconv * nxt, axis=0, keepdims=True)
            ddb_ref[0:1, :] += jnp.sum(dconv, axis=0, keepdims=True)

        _for_chunks(segs, first)

        def second(s, st, ln, off, b):
            rows = pl.ds(b, CHUNK)
            ext = pad_c[pl.ds(off - HALO + b, CHUNK + 2 * HALO), :]
            da = (dw_ref[0:1, :] * _shift(ext, -1) + dw_ref[1:2, :] * ext + dw_ref[2:3, :] * _shift(ext, 1))
            da = da[HALO:HALO + CHUNK]
            dpre_ref[1, rows, :] = (da * pre_ref[2, rows, :]).astype(BF16)
            dpre_ref[2, rows, :] = (da * pre_ref[1, rows, :]).astype(BF16)

        _for_chunks(segs, second)

    quad = pl.BlockSpec((4, t, cb), lambda j: (0, 0, j))
    rows8 = pl.BlockSpec((8, cb), lambda j: (0, j))
    return _pc(
        body, name=name, grid=(width // cb,),
        in_specs=[pl.BlockSpec((t, cb), lambda j: (0, j)), quad, rows8, pl.BlockSpec((1, cb), lambda j: (0, j))],
        out_specs=[quad, rows8, rows8],
        out_shape=[jax.ShapeDtypeStruct((4, t, width), BF16), jax.ShapeDtypeStruct((8, width), F32),
                   jax.ShapeDtypeStruct((8, width), F32)],
        scratch_shapes=[pltpu.VMEM((t + 2 * HALO, cb), F32), pltpu.VMEM((t + 2 * HALO, cb), F32)],
        compiler_params=_params("parallel"),
    )(dz, pre, dw, db)


PAIR_TILES = 2 * WIN_ROWS - 2


def _pair_geometry():
    lane = lax.broadcasted_iota(jnp.int32, (GRID_W, LANES), 1)
    qcol = lax.broadcasted_iota(jnp.int32, (GRID_W, LANES), 0)
    low = lane < GRID_W
    kcol = jnp.where(low, lane, lane - GRID_W)
    start = jnp.clip(qcol - WIN_COLS // 2, 0, GRID_W - WIN_COLS)
    inside = (kcol >= start) & (kcol < start + WIN_COLS)
    return low, inside


def _bias_tiles(rpb_ref, rows_ref, tiles_ref, inside):
    for h in range(2):
        rows = rpb_ref[h]
        rows_ref[h] = (pltpu.roll(rows, LANES - (WIN_COLS - 1), 1)
                       + pltpu.roll(pltpu.roll(rows, GRID_W - (WIN_COLS - 1), 1), 2 * WIN_ROWS - 1, 0))
        for t in range(PAIR_TILES):
            both = jnp.broadcast_to(rows_ref[h, t:t + 1, :], (GRID_W, LANES))
            tiles_ref[h, t] = jnp.where(inside, pltpu.roll(both, 0, 1, stride=1, stride_axis=0), MASKED)


def _bias_tiles_grad(dtiles_ref, drpb_ref):
    n = PAIR_TILES * GRID_W
    qcol = lax.broadcasted_iota(jnp.int32, (n, LANES), 0) & (GRID_W - 1)
    lane = lax.broadcasted_iota(jnp.int32, (1, LANES), 1)
    zero = jnp.zeros((1, LANES), F32)
    for h in range(2):
        v = pltpu.roll(dtiles_ref[h].reshape(n, LANES), WIN_COLS - 1, 1)
        for bit in range(6):
            v = jnp.where((qcol >> bit) & 1 == 1, pltpu.roll(v, LANES - (1 << bit), 1), v)
        sums = [jnp.sum(v[t * GRID_W:(t + 1) * GRID_W], axis=0, keepdims=True) for t in range(PAIR_TILES)]
        for r in range(2 * WIN_ROWS):
            here = sums[r] if r < PAIR_TILES else zero
            prev = pltpu.roll(sums[r - 1], GRID_W, 1) if 1 <= r <= PAIR_TILES else zero
            drpb_ref[h, r:r + 1, :] = jnp.where(lane < 2 * WIN_COLS - 1, here + prev, 0.0)


def _attn_rows(r, n_rows):
    first = jnp.clip(r - WIN_ROWS // 2, 0, n_rows - WIN_ROWS)
    return first, first - r + WIN_ROWS - 1


def _softmax(s_loc, s_ctx):
    m = jnp.maximum(jnp.max(s_loc, axis=-1, keepdims=True), jnp.max(s_ctx, axis=-1, keepdims=True))
    e_loc, e_ctx = jnp.exp(s_loc - m), jnp.exp(s_ctx - m)
    inv = 1.0 / (jnp.sum(e_loc, axis=-1, keepdims=True) + jnp.sum(e_ctx, axis=-1, keepdims=True))
    return e_loc * inv, e_ctx * inv


def _pair_bias(tiles_ref, j):
    return jnp.concatenate(
        [jnp.concatenate([tiles_ref[h, j + 2 * m] for m in range(WIN_ROWS // 2)], axis=1) for h in range(2)], axis=0)


ROWS_PER_STEP = 4


def _by_head(tile, low):
    zero = jnp.zeros_like(tile)
    return jnp.concatenate([jnp.where(low, tile, zero), jnp.where(low, zero, tile)], axis=0)


def _merge_heads(stacked, low):
    return jnp.where(low, stacked[:GRID_W], stacked[GRID_W:])


def _attn_items(step, n_rows, q_ref, low):
    items = []
    for u in range(ROWS_PER_STEP):
        r = step * ROWS_PER_STEP + u
        first, j = _attn_rows(r, n_rows)
        rows = pl.ds(pl.multiple_of(r * GRID_W, GRID_W), GRID_W)
        keys = pl.ds(pl.multiple_of(first * GRID_W, GRID_W), WIN_ROWS * GRID_W)
        q = (q_ref[rows, :].astype(F32) * HEAD_DIM ** -0.5).astype(BF16)
        items.append((rows, keys, j, _by_head(q, low)))
    return items


def _attn_fwd(qkv, gate, rpb, seq):
    _, t, width = qkv.shape
    n_rows = seq // GRID_W
    n_ctx = t - seq
    blk = WIN_ROWS * GRID_W

    def body(q_ref, k_ref, v_ref, g_ref, rpb_ref, z_ref, o_ref, rows_ref, tiles_ref):
        low, inside = _pair_geometry()
        _bias_tiles(rpb_ref, rows_ref, tiles_ref, inside)
        ctx = pl.ds(seq, n_ctx)

        def step(i, carry):
            items = _attn_items(i, n_rows, q_ref, low)
            k_ctx, v_ctx = k_ref[ctx, :], v_ref[ctx, :]
            scores = [(_dot(q, k_ref[keys, :], 1, 1) + _pair_bias(tiles_ref, j), _dot(q, k_ctx, 1, 1))
                      for _, keys, j, q in items]
            probs = [_softmax(s_loc, s_ctx) for s_loc, s_ctx in scores]
            outs = [_dot(p_loc.astype(BF16), v_ref[keys, :]) + _dot(p_ctx.astype(BF16), v_ctx)
                    for (_, keys, _, _), (p_loc, p_ctx) in zip(items, probs)]
            for (rows, _, _, _), out in zip(items, outs):
                o = _merge_heads(out, low)
                o_ref[rows, :] = o
                z_ref[rows, :] = (o * _silu(g_ref[rows, :])).astype(BF16)
            return carry

        lax.fori_loop(0, n_rows // ROWS_PER_STEP, step, 0)

    def part(p):
        return pl.BlockSpec((None, t, LANES), lambda h: (p, 0, h))

    out = pl.BlockSpec((seq, LANES), lambda h: (0, h))
    return _pc(
        body, name="attn_fwd", grid=(width // LANES,),
        in_specs=[part(0), part(1), part(2), part(0), pl.BlockSpec((2, 2 * WIN_ROWS, LANES), lambda h: (h, 0, 0))],
        out_specs=[out, out],
        out_shape=[jax.ShapeDtypeStruct((seq, width), BF16), jax.ShapeDtypeStruct((seq, width), F32)],
        scratch_shapes=[pltpu.VMEM((2, 2 * WIN_ROWS, LANES), F32), pltpu.VMEM((2, PAIR_TILES, GRID_W, LANES), F32)],
        compiler_params=_params("parallel"),
    )(qkv, qkv, qkv, gate, rpb)


def _attn_bwd(qkv, gate, o, dz, rpb, seq):
    _, t, width = qkv.shape
    n_rows = seq // GRID_W
    n_ctx = t - seq
    blk = WIN_ROWS * GRID_W
    heads = 2 * width // LANES

    def body(q_ref, k_ref, v_ref, g_ref, o_ref, dz_ref, rpb_ref, dpre_ref, drpb_ref,
             rows_ref, tiles_ref, dtiles_ref, dk_ref, dv_ref):
        low, inside = _pair_geometry()
        _bias_tiles(rpb_ref, rows_ref, tiles_ref, inside)
        dtiles_ref[...] = jnp.zeros_like(dtiles_ref)
        dk_ref[...] = jnp.zeros_like(dk_ref)
        dv_ref[...] = jnp.zeros_like(dv_ref)
        ctx = pl.ds(seq, n_ctx)

        def step(i, carry):
            items = _attn_items(i, n_rows, q_ref, low)
            k_ctx, v_ctx = k_ref[ctx, :], v_ref[ctx, :]
            d_outs = []
            for rows, _, _, _ in items:
                g = g_ref[rows, :]
                dzv = dz_ref[rows, :]
                dpre_ref[3, rows, :] = (dzv * o_ref[rows, :] * _dsilu(g)).astype(BF16)
                d_outs.append(_by_head((dzv * _silu(g)).astype(BF16), low))
            scores = [(_dot(q, k_ref[keys, :], 1, 1) + _pair_bias(tiles_ref, j), _dot(q, k_ctx, 1, 1))
                      for _, keys, j, q in items]
            dprobs = [(_dot(d_o, v_ref[keys, :], 1, 1), _dot(d_o, v_ctx, 1, 1))
                      for (_, keys, _, _), d_o in zip(items, d_outs)]
            probs = [_softmax(s_loc, s_ctx) for s_loc, s_ctx in scores]
            dscores = []
            for (p_loc, p_ctx), (dp_loc, dp_ctx) in zip(probs, dprobs):
                delta = (jnp.sum(p_loc * dp_loc, axis=-1, keepdims=True)
                         + jnp.sum(p_ctx * dp_ctx, axis=-1, keepdims=True))
                dscores.append((p_loc * (dp_loc - delta), p_ctx * (dp_ctx - delta)))
            dqs = [_dot(ds_loc.astype(BF16), k_ref[keys, :]) + _dot(ds_ctx.astype(BF16), k_ctx)
                   for (_, keys, _, _), (ds_loc, ds_ctx) in zip(items, dscores)]
            for (rows, _, _, _), dq in zip(items, dqs):
                dpre_ref[0, rows, :] = (_merge_heads(dq, low) * HEAD_DIM ** -0.5).astype(BF16)
            for (_, keys, j, q), d_o, (p_loc, p_ctx), (ds_loc, ds_ctx) in zip(items, d_outs, probs, dscores):
                dk_ref[keys, :] += _dot(ds_loc.astype(BF16), q, 0, 0)
                dk_ref[ctx, :] += _dot(ds_ctx.astype(BF16), q, 0, 0)
                dv_ref[keys, :] += _dot(p_loc.astype(BF16), d_o, 0, 0)
                dv_ref[ctx, :] += _dot(p_ctx.astype(BF16), d_o, 0, 0)
                for h in range(2):
                    for m in range(WIN_ROWS // 2):
                        dtiles_ref[h, j + 2 * m] += ds_loc[h * GRID_W:(h + 1) * GRID_W, m * LANES:(m + 1) * LANES]
            return carry

        lax.fori_loop(0, n_rows // ROWS_PER_STEP, step, 0)
        dpre_ref[1] = dk_ref[...].astype(BF16)
        dpre_ref[2] = dv_ref[...].astype(BF16)
        dpre_ref[0, ctx, :] = jnp.zeros((n_ctx, LANES), BF16)
        dpre_ref[3, ctx, :] = jnp.zeros((n_ctx, LANES), BF16)
        _bias_tiles_grad(dtiles_ref, drpb_ref)

    def part(p):
        return pl.BlockSpec((None, t, LANES), lambda h: (p, 0, h))

    lat = pl.BlockSpec((seq, LANES), lambda h: (0, h))
    rspec = pl.BlockSpec((2, 2 * WIN_ROWS, LANES), lambda h: (h, 0, 0))
    tiles = pltpu.VMEM((2, PAIR_TILES, GRID_W, LANES), F32)
    return _pc(
        body, name="attn_bwd", grid=(width // LANES,),
        in_specs=[part(0), part(1), part(2), part(0), lat, lat, rspec],
        out_specs=[pl.BlockSpec((4, t, LANES), lambda h: (0, 0, h)), rspec],
        out_shape=[jax.ShapeDtypeStruct((4, t, width), BF16), jax.ShapeDtypeStruct((heads, 2 * WIN_ROWS, LANES), F32)],
        scratch_shapes=[pltpu.VMEM((2, 2 * WIN_ROWS, LANES), F32), tiles, tiles,
                        pltpu.VMEM((t, LANES), F32), pltpu.VMEM((t, LANES), F32)],
        compiler_params=_params("parallel"),
    )(qkv, qkv, qkv, gate, o, dz, rpb)


def _adam_update(w, m, v, g):
    m2 = ADAM_B1 * m + (1.0 - ADAM_B1) * g
    v2 = ADAM_B2 * v + (1.0 - ADAM_B2) * (g * g)
    m_hat = m2 / (1.0 - ADAM_B1 ** ADAM_STEP)
    v_hat = v2 / (1.0 - ADAM_B2 ** ADAM_STEP)
    return -ADAM_LR * (m_hat / (jnp.sqrt(v_hat) + ADAM_EPS) + ADAM_WD * w), m2, v2


def _adamw(w, m, v, parts, name):
    rows, cols = w.shape
    tr = _tile(rows, max(8, 131072 // cols), 8)
    n_parts = len(parts)

    def body(*refs):
        w_ref, m_ref, v_ref = refs[:3]
        part_refs = refs[3:3 + n_parts]
        g_ref, d_ref, nm_ref, nv_ref = refs[3 + n_parts:]
        g = part_refs[0][...].astype(F32)
        for p in part_refs[1:]:
            g = g + p[...].astype(F32)
        g_ref[...] = g
        d_ref[...], nm_ref[...], nv_ref[...] = _adam_update(w_ref[...], m_ref[...], v_ref[...], g)

    tile = pl.BlockSpec((tr, cols), lambda i: (i, 0))
    in_specs, args = [tile, tile, tile], [w, m, v]
    for p in parts:
        if isinstance(p, tuple):
            arr, k = p
            in_specs.append(pl.BlockSpec((None, tr, cols), lambda i, k=k: (k, i, 0)))
            args.append(arr)
        else:
            in_specs.append(tile)
            args.append(p)
    shape = jax.ShapeDtypeStruct((rows, cols), F32)
    return _pc(
        body, name=name, grid=(rows // tr,), in_specs=in_specs, out_specs=[tile] * 4, out_shape=[shape] * 4,
        compiler_params=_params("parallel"),
    )(*args)


def _adamw_layers(w, m, v, landed, name):
    n_layers, rows, cols = w.shape
    tr = _tile(rows, max(8, 131072 // cols), 8)

    def body(*refs):
        w_ref, m_ref, v_ref = refs[:3]
        part_refs = refs[3:3 + n_layers * N_DEV]
        g_ref, d_ref, nm_ref, nv_ref = refs[3 + n_layers * N_DEV:]
        layer = pl.program_id(0)
        g = None
        for l in range(n_layers):
            s = part_refs[l * N_DEV][...].astype(F32)
            for p in part_refs[l * N_DEV + 1:(l + 1) * N_DEV]:
                s = s + p[...].astype(F32)
            g = s if g is None else jnp.where(layer == l, s, g)
        g_ref[...] = g
        d_ref[...], nm_ref[...], nv_ref[...] = _adam_update(w_ref[...], m_ref[...], v_ref[...], g)

    tile = pl.BlockSpec((None, tr, cols), lambda l, i: (l, i, 0))
    in_specs, args = [tile, tile, tile], [w, m, v]
    for l, arr in enumerate(landed):
        for k in range(N_DEV):
            in_specs.append(pl.BlockSpec((None, tr, cols), lambda ll, i, l=l, k=k: (k, jnp.where(ll == l, i, 0), 0)))
            args.append(arr)
    shape = jax.ShapeDtypeStruct(w.shape, F32)
    return _pc(
        body, name=name, grid=(n_layers, rows // tr), in_specs=in_specs, out_specs=[tile] * 4, out_shape=[shape] * 4,
        compiler_params=_params("arbitrary", "arbitrary"),
    )(*args)


def _adamw_small(states, grads):
    sources, makers = grads
    n, ns = len(states), len(sources)

    def body(*refs):
        src = refs[:ns]
        ins = refs[ns:ns + 3 * n]
        outs = refs[ns + 3 * n:]
        for k in range(n):
            w_ref, m_ref, v_ref = ins[3 * k:3 * k + 3]
            g = makers[k](*src)
            outs[4 * k][...] = g
            outs[4 * k + 1][...], outs[4 * k + 2][...], outs[4 * k + 3][...] = _adam_update(
                w_ref[...], m_ref[...], v_ref[...], g)

    flat = [a for s in states for a in s]
    vmem = pl.BlockSpec(memory_space=pltpu.VMEM)
    res = _pc(
        body, name="adamw_small",
        in_specs=[vmem] * (ns + 3 * n), out_specs=[vmem] * (4 * n),
        out_shape=[jax.ShapeDtypeStruct(s[0].shape, F32) for s in states for _ in range(4)],
        compiler_params=pltpu.CompilerParams(vmem_limit_bytes=VMEM_LIMIT),
    )(*sources, *flat)
    return [res[4 * k:4 * k + 4] for k in range(n)]


def _rows128(a):
    flat = a.reshape(-1)
    pad = (-flat.shape[0]) % LANES
    if pad:
        flat = jnp.concatenate([flat, jnp.zeros((pad,), flat.dtype)])
    return flat.reshape(-1, LANES)


def _pad_rows(a, mult=8):
    pad = (-a.shape[0]) % mult
    if pad:
        a = jnp.concatenate([a, jnp.zeros((pad,) + a.shape[1:], a.dtype)], axis=0)
    return a


def kernel(x, c, ctx, c_ctx, norm_g, ada_w, ada_b, pool_w_in, pool_w_grp, pool_scale, pool_w_out, na_w_in, na_rpb, na_w_out, conv_w_in, conv_dw, conv_db, conv_w_out, final_g, loss_target, m_c_ctx, m_norm_g, m_ada_w, m_ada_b, m_pool_w_in, m_pool_w_grp, m_pool_scale, m_pool_w_out, m_na_w_in, m_na_rpb, m_na_w_out, m_conv_w_in, m_conv_dw, m_conv_db, m_conv_w_out, m_final_g, v_c_ctx, v_norm_g, v_ada_w, v_ada_b, v_pool_w_in, v_pool_w_grp, v_pool_scale, v_pool_w_out, v_na_w_in, v_na_rpb, v_na_w_out, v_conv_w_in, v_conv_dw, v_conv_db, v_conv_w_out, v_final_g):
    xi, yi, ci = _my_place()
    me = 4 * xi + 2 * yi + ci
    seq, d = x.shape[1], x.shape[2]
    n_ctx = ctx.shape[1]
    t_all = seq + n_ctx
    width = d
    depth = norm_g.shape[0]
    nb = ada_w.shape[2]
    shard = width // N_DEV
    d_rows = d // LANES
    assert seq % CHUNK == 0 and n_ctx % CHUNK == 0 and (seq // GRID_W) % ROWS_PER_STEP == 0 and seq >= WIN_ROWS * GRID_W
    tr = math.gcd(math.gcd(seq, n_ctx), 256)
    x_tiles = seq // tr

    n_pool = pool_scale.shape[0]
    n_grp = pool_w_grp.shape[1]
    grp = width // n_grp

    small_in = _pad_rows(jnp.concatenate([_rows128(c), pool_scale, conv_dw[0], conv_db], axis=0))
    got = _gather_small(small_in, "gather_inputs")
    r0 = d_rows
    c_all = got[:, :r0].reshape(N_DEV, d)
    scale_full = got[:, r0:r0 + n_pool].transpose(1, 0, 2).reshape(n_pool, width)
    r1 = r0 + n_pool
    taps_full = _pad_rows(got[:, r1:r1 + 3].transpose(1, 0, 2).reshape(3, width))
    bias_full = got[:, r1 + 3:r1 + 4].transpose(1, 0, 2).reshape(1, width)

    cond = jnp.concatenate([c_all, c_ctx[None], jnp.zeros((7, d), F32)], axis=0)
    bias_mine = lax.dynamic_slice(ada_b, (0, me * nb), (depth, nb))
    mod_mine = _mod_fwd(cond, ada_w, bias_mine)
    by_example = jnp.stack([mod_mine[:, :N_DEV].transpose(1, 0, 2),
                            jnp.broadcast_to(mod_mine[:, N_DEV][None], (N_DEV, depth, nb))], axis=2)
    mod_all = _gather_small(by_example.reshape(N_DEV, -1, LANES), "gather_mod", per_dest=True)
    mod_all = mod_all.reshape(N_DEV, depth, 2, nb).transpose(1, 2, 0, 3).reshape(depth, 2, 3, d)
    mod_all = jnp.pad(mod_all, ((0, 0), (0, 0), (0, 5), (0, 0)))
    mods = [mod_all[i] if i < 2 else mod_all[i, :1] for i in range(depth)]

    layer_weights = [[pool_w_in[0], pool_w_grp[0], pool_w_out[0]], [na_w_in[0], na_w_out[0]],
                     [conv_w_in[0], conv_w_out[0]], [pool_w_in[1], pool_w_grp[1], pool_w_out[1]]]
    slot = {(i, t): n for n, (i, t) in enumerate((i, t) for i, ws in enumerate(layer_weights) for t in range(len(ws)))}
    two_level = [(0, 0), (1, 0)]
    weights_sent = _exchange_start(
        [w.astype(BF16) for ws in layer_weights for w in ws], False, mod_all, "weights_start",
        peers=[CHIP_PEERS if key in two_level else ALL_PEERS for key in slot])
    token = weights_sent[-1]

    def landed_weight(i, t, after):
        return _exchange_wait(weights_sent, False, after, f"weights_wait{i}_{t}", which=[slot[i, t]])[0]

    def handed_on(i, after):
        half = [landed_weight(i, 0, after)]
        rest = _forward_start(half, after, f"weights_forward{i}")
        return _forward_wait(rest, rest[-1], f"weights_forward_wait{i}")[0][:, None]

    def as_in(w):
        return w[:, None]

    def as_grp(w):
        return w.transpose(1, 0, 2, 3).reshape(n_grp, grp, grp)

    def as_out(w):
        return w.reshape(width, d)

    both = [(0, seq), (seq, n_ctx)]
    latent = [(0, seq)]

    def grp_slots(g):
        return g.reshape(n_grp, N_DEV, grp // N_DEV, grp).transpose(1, 0, 2, 3).reshape(N_DEV, -1, grp).astype(BF16)

    def send_grads(i, grads):
        return _exchange_start(grads, True, jnp.zeros((8, LANES), F32), f"grads_start{i}")

    xs0 = jnp.concatenate([x[0], ctx[0]], axis=0)
    h0 = _norm_fwd(xs0, norm_g[0:1] + token[0, 0], mods[0], tr, x_tiles, "norm_fwd0")
    pool_in_w0 = handed_on(0, h0)
    pre0 = _proj_in(h0, pool_in_w0, 0, width, "proj_in0")
    pool_grp_w0 = as_grp(landed_weight(0, 1, pre0))
    z0, diff0 = _pool_fwd(pre0, pool_grp_w0, scale_full[0:1], both, "pool_fwd0")
    pool_out_w0 = as_out(landed_weight(0, 2, z0))
    yx0, xs1, h1 = _proj_out(z0, pool_out_w0, xs0, mods[0], tr, x_tiles, "proj_out0", nxt=(norm_g[1:2], mods[1]))

    na_in_w = handed_on(1, h1)
    per_part = width // na_w_in.shape[2]
    qkv1 = _proj_in(h1, na_in_w, 0, width, "proj_in1_qkv", blocks=(0, 3 * per_part), dtype=BF16)
    gpre1 = _proj_in(h1, na_in_w, 0, width, "proj_in1_gate", blocks=(3 * per_part, per_part))
    rpb_rows = jnp.pad(na_rpb[0], ((0, 0), (0, 2 * WIN_ROWS - na_rpb.shape[2]), (0, LANES - na_rpb.shape[3])))
    z1, o1 = _attn_fwd(qkv1, gpre1, rpb_rows, seq)
    na_out_w = as_out(landed_weight(1, 1, z1))
    yx1, x2, h2 = _proj_out(z1, na_out_w, xs1, mods[1], tr, x_tiles, "proj_out1", nxt=(norm_g[2:3], mods[2]))

    conv_in_w = as_in(landed_weight(2, 0, h2))
    pre2 = _proj_in(h2, conv_in_w, 0, width, "proj_in2")
    z2 = _conv_fwd(pre2, taps_full, bias_full, "conv_fwd")
    conv_out_w = as_out(landed_weight(2, 1, z2))
    yx2, x3, h3 = _proj_out(z2, conv_out_w, x2, mods[2], tr, x_tiles, "proj_out2", nxt=(norm_g[3:4], mods[3]))

    pool_in_w3 = as_in(landed_weight(3, 0, h3))
    pre3 = _proj_in(h3, pool_in_w3, 0, width, "proj_in3")
    pool_grp_w3 = as_grp(landed_weight(3, 1, pre3))
    z3, diff3 = _pool_fwd(pre3, pool_grp_w3, scale_full[1:2], latent, "pool_fwd3")
    pool_out_w3 = as_out(landed_weight(3, 2, z3))
    yx3, x4 = _proj_out(z3, pool_out_w3, x3, mods[3], tr, x_tiles, "proj_out3")

    loss_part, dx4, d_final, dyx3, gate3 = _loss_head(x4, loss_target[0], final_g[None], yx3, mods[3], tr)

    dz3 = _proj_out_dz(dyx3, pool_out_w3, "proj_out_dz3")
    g_pool_out1 = _grad_w_out(z3, dyx3, "grad_w_out3")
    dpre3, g_grp1, g_scale1 = _pool_bwd(dz3, diff3, pre3, pool_grp_w3, scale_full[1:2], latent, "pool_bwd3")
    dh3 = _proj_in_dh(dpre3, pool_in_w3, 0, "proj_in_dh3")
    g_pool_in1 = _grad_w_in(h3, dpre3, pool_w_in.shape[2], "grad_w_in3")
    sent3 = send_grads(3, [g_pool_in1, grp_slots(g_grp1), g_pool_out1.reshape(N_DEV, shard, d)])
    dx3, norm3, dyx2, gate2 = _norm_bwd(x3, dh3, dx4, norm_g[3:4] + sent3[-1][0, 0], mods[3], tr, x_tiles, "norm_bwd3",
                                        below=(yx2, mods[2]))

    dz2 = _proj_out_dz(dyx2, conv_out_w, "proj_out_dz2")
    g_conv_out = _grad_w_out(z2, dyx2, "grad_w_out2")
    dpre2, g_taps, g_cbias = _conv_bwd(dz2, pre2, taps_full, bias_full, "conv_bwd")
    dh2 = _proj_in_dh(dpre2, conv_in_w, 0, "proj_in_dh2")
    g_conv_in = _grad_w_in(h2, dpre2, conv_w_in.shape[2], "grad_w_in2")
    sent2 = send_grads(2, [g_conv_in, g_conv_out.reshape(N_DEV, shard, d)])
    dx2, norm2, dyx1, gate1 = _norm_bwd(x2, dh2, dx3, norm_g[2:3] + sent2[-1][0, 0], mods[2], tr, x_tiles, "norm_bwd2",
                                        below=(yx1, mods[1][:1]))

    dz1 = _proj_out_dz(dyx1, na_out_w, "proj_out_dz1")
    g_na_out = _grad_w_out(z1, dyx1, "grad_w_out1")
    dpre1, g_rpb = _attn_bwd(qkv1, gpre1, o1, dz1, rpb_rows, seq)
    g_rpb = g_rpb[:, :na_rpb.shape[2], :na_rpb.shape[3]]
    dh1 = _proj_in_dh(dpre1, na_in_w, 0, "proj_in_dh1")
    g_na_in = _grad_w_in(h1, dpre1, na_w_in.shape[2], "grad_w_in1")
    sent1 = send_grads(1, [g_na_in, g_na_out.reshape(N_DEV, shard, d)])
    dxs1, norm1, dyx0, gate0 = _norm_bwd(xs1, dh1, dx2, norm_g[1:2] + sent1[-1][0, 0], mods[1], tr, x_tiles, "norm_bwd1",
                                         res_tiles=x_tiles, below=(yx0, mods[0]))

    dz0 = _proj_out_dz(dyx0, pool_out_w0, "proj_out_dz0")
    g_pool_out0 = _grad_w_out(z0, dyx0, "grad_w_out0")
    sent0a = _exchange_start([g_pool_out0.reshape(N_DEV, shard, d)], True, jnp.zeros((8, LANES), F32), "grads_start0a")
    dpre0, g_grp0, g_scale0 = _pool_bwd(dz0, diff0, pre0, pool_grp_w0, scale_full[0:1], both, "pool_bwd0")
    g_pool_in0 = _grad_w_in(h0, dpre0, pool_w_in.shape[2], "grad_w_in0")
    sent0b = _exchange_start([g_pool_in0, grp_slots(g_grp0)], True, sent0a[-1], "grads_start0b")
    dh0 = _proj_in_dh(dpre0, pool_in_w0, 0, "proj_in_dh0", after=sent0b[-1])
    dx0, norm0 = _norm_bwd(xs0, dh0, dxs1, norm_g[0:1], mods[0], tr, x_tiles, "norm_bwd0", out_tiles=x_tiles)
    grad_x = dx0[None]

    norms, gates = [norm0, norm1, norm2, norm3], [gate0, gate1, gate2, gate3]
    zero_d = jnp.zeros((d,), F32)
    dm_rows = [jnp.concatenate([norms[i][0, 0], norms[i][0, 1], gates[i][0, 0]]) for i in range(depth)]
    dm_rows.append(jnp.concatenate([norm0[1, 0], norm0[1, 1], gate0[1, 0]]))
    dm_rows.append(jnp.concatenate([norm1[1, 0], norm1[1, 1], zero_d]))
    dm_local = jnp.stack(dm_rows + [jnp.zeros((3 * d,), F32)] * 2)
    g_norm_part = jnp.stack([norm0[0, 2] + norm0[1, 2], norm1[0, 2] + norm1[1, 2], norm2[0, 2], norm3[0, 2]])
    pieces = [_rows128(dm_local), _rows128(g_norm_part), _rows128(d_final[0]), _pad_rows(_rows128(g_rpb)), loss_part]
    marks = np.cumsum([0] + [p.shape[0] for p in pieces])
    by_owner = [a.reshape(-1, N_DEV, shard).transpose(1, 0, 2) for a in (g_scale0, g_scale1, g_taps[:3], g_cbias[0:1])]
    by_owner = jnp.concatenate(by_owner + [jnp.zeros((N_DEV, 8 - n_pool - 4, shard), F32)], axis=1)
    small_sent = _exchange_start([jnp.concatenate(pieces, axis=0), by_owner], [False, True], jnp.zeros((8, LANES), F32),
                                 "small_grads_start")

    def big(parts, w, m, v, name):
        shape = w.shape
        view = (-1, shape[-1])
        parts = [(parts.reshape((N_DEV,) + w.reshape(view).shape), k) for k in range(N_DEV)]
        return [r.reshape(shape) for r in _adamw(w.reshape(view), m.reshape(view), v.reshape(view), parts, name)]

    in3, grp3, out3 = _exchange_wait(sent3, True, small_sent[-1], "grads_wait3")
    in2, out2 = _exchange_wait(sent2, True, small_sent[-1], "grads_wait2")
    in1, out1 = _exchange_wait(sent1, True, small_sent[-1], "grads_wait1")
    res = {}
    res["na_w_in"] = [r[None] for r in big(in1, na_w_in[0], m_na_w_in[0], v_na_w_in[0], "adamw_na_in")]
    res["na_w_out"] = [r[None] for r in big(out1, na_w_out[0], m_na_w_out[0], v_na_w_out[0], "adamw_na_out")]
    res["conv_w_in"] = [r[None] for r in big(in2, conv_w_in[0], m_conv_w_in[0], v_conv_w_in[0], "adamw_conv_in")]
    res["conv_w_out"] = [r[None] for r in big(out2, conv_w_out[0], m_conv_w_out[0], v_conv_w_out[0], "adamw_conv_out")]

    done = [res[n][0] for n in ("na_w_in", "na_w_out", "conv_w_in", "conv_w_out")]
    small_out, owned = _exchange_wait(small_sent, [False, True], done, "small_grads_wait")
    loss = jnp.sum(small_out[:, marks[4], 0])
    dm_all = small_out[:, :marks[1]].reshape(N_DEV, 8, 3 * d).transpose(1, 0, 2)
    dm_mine = lax.dynamic_slice(dm_all, (0, 0, me * nb), (8, N_DEV, nb))
    g_ada_w, g_ada_b, cctx_part, dsilu_cond = _mod_bwd(cond, ada_w, dm_all, dm_mine)
    cctx_all = _gather_small(_rows128(cctx_part[0]), "gather_cctx")

    def summed(ref, lo, hi):
        g = ref[0, lo:hi, :]
        for k in range(1, N_DEV):
            g = g + ref[k, lo:hi, :]
        return g

    makers = [
        lambda so, ow, cc, ab, ds: summed(cc, 0, d_rows) * ds[...],
        lambda so, ow, cc, ab, ds: summed(so, marks[1], marks[2]),
        lambda so, ow, cc, ab, ds: ab[...],
        lambda so, ow, cc, ab, ds: summed(so, marks[2], marks[3]),
        lambda so, ow, cc, ab, ds: summed(so, marks[3], marks[4]),
        lambda so, ow, cc, ab, ds: summed(ow, 0, n_pool),
        lambda so, ow, cc, ab, ds: summed(ow, n_pool, n_pool + 3),
        lambda so, ow, cc, ab, ds: summed(ow, n_pool + 3, n_pool + 4),
    ]
    rpb_rows128 = lambda a: _pad_rows(_rows128(a))
    views = [_rows128] * 4 + [rpb_rows128] + [lambda a: a.reshape(-1, LANES)] * 3
    small = [(c_ctx, m_c_ctx, v_c_ctx), (norm_g, m_norm_g, v_norm_g), (ada_b, m_ada_b, v_ada_b),
             (final_g, m_final_g, v_final_g), (na_rpb, m_na_rpb, v_na_rpb), (pool_scale, m_pool_scale, v_pool_scale),
             (conv_dw, m_conv_dw, v_conv_dw), (conv_db, m_conv_db, v_conv_db)]
    states = [tuple(view(a) for a in triple) for view, triple in zip(views, small)]
    sources = (small_out, owned, cctx_all, _rows128(g_ada_b), _rows128(dsilu_cond[8]))
    small_res = _adamw_small(states, (sources, makers))
    names = ["c_ctx", "norm_g", "ada_b", "final_g", "na_rpb", "pool_scale", "conv_dw", "conv_db"]
    for name, (w, _, _), outs4 in zip(names, small, small_res):
        res[name] = [r.reshape(-1)[:w.size].reshape(w.shape) for r in outs4]

    res["ada_w"] = [r.reshape(ada_w.shape) for r in _adamw(
        ada_w.reshape(-1, nb), m_ada_w.reshape(-1, nb), v_ada_w.reshape(-1, nb), [g_ada_w.reshape(-1, nb)], "adamw_ada_w")]

    out0, = _exchange_wait(sent0a, True, small_res[0][0], "grads_wait0a")
    in0, grp0 = _exchange_wait(sent0b, True, small_res[0][0], "grads_wait0b")
    def both_layers(first, second, w, m, v, name):
        view = (w.shape[0], -1, w.shape[-1])
        landed = [first.reshape((N_DEV,) + w.reshape(view).shape[1:]), second.reshape((N_DEV,) + w.reshape(view).shape[1:])]
        return [r.reshape(w.shape) for r in _adamw_layers(w.reshape(view), m.reshape(view), v.reshape(view), landed, name)]

    res["pool_w_in"] = both_layers(in0, in3, pool_w_in, m_pool_w_in, v_pool_w_in, "adamw_pool_in")
    res["pool_w_grp"] = both_layers(grp0, grp3, pool_w_grp, m_pool_w_grp, v_pool_w_grp, "adamw_pool_grp")
    res["pool_w_out"] = both_layers(out0, out3, pool_w_out, m_pool_w_out, v_pool_w_out, "adamw_pool_out")

    order = ["c_ctx", "norm_g", "ada_w", "ada_b", "pool_w_in", "pool_w_grp", "pool_scale", "pool_w_out", "na_w_in",
             "na_rpb", "na_w_out", "conv_w_in", "conv_dw", "conv_db", "conv_w_out", "final_g"]
    outs = [loss, grad_x]
    for j in range(4):
        outs += [res[n][j] for n in order]
    return tuple(outs)
```

```python
import functools
import math

import numpy as np
import jax
import jax.numpy as jnp
from jax import lax
from jax.experimental import pallas as pl
from jax.experimental.pallas import tpu as pltpu

F32 = jnp.float32
BF16 = jnp.bfloat16
N_DEV = 8
LANES = 128
RMS_EPS = 1e-6
GRID_W = 64
WIN_ROWS = 8
WIN_COLS = 16
HEAD_DIM = 64
POOL_WINDOWS = (2, 4, 8, 16)
HALO = 8
CHUNK = 256
MASKED = -1e30
ADAM_LR = 0.001
ADAM_B1 = 0.9
ADAM_B2 = 0.999
ADAM_EPS = 1e-08
ADAM_WD = 0.01
ADAM_STEP = 10
VMEM_LIMIT = 56 * 1024 * 1024
MESH = pl.DeviceIdType.MESH
ANY = pl.BlockSpec(memory_space=pl.ANY)
HBM = pl.BlockSpec(memory_space=pltpu.HBM)
SEM = pl.BlockSpec(memory_space=pltpu.SEMAPHORE)
EFFECT = pltpu.SideEffectType.DATAFLOW_SIDE_EFFECTING


def _pc(body, *, name, **kw):
    return pl.pallas_call(body, name=name, **kw)


def _params(*sem):
    return pltpu.CompilerParams(dimension_semantics=sem if sem else None, vmem_limit_bytes=VMEM_LIMIT)


def _dot(a, b, ca=1, cb=0, precision=None):
    return lax.dot_general(a, b, (((ca,), (cb,)), ((), ())), preferred_element_type=F32, precision=precision)


def _tile(n, pref, unit=LANES):
    best = None
    for t in range(unit, min(n, pref) + 1, unit):
        if n % t == 0:
            best = t
    return best if best is not None else n


def _sigmoid(x):
    return 1.0 / (1.0 + jnp.exp(-x))


def _silu(x):
    return x * _sigmoid(x)


def _dsilu(x):
    s = _sigmoid(x)
    return s * (1.0 + x * (1.0 - s))


def _my_place():
    return lax.axis_index("x"), lax.axis_index("y"), lax.axis_index("c")


def _flip(v, f):
    return 1 - v if f else v


def _gather_small(block, name, per_dest=False):
    rows, cols = block.shape[-2:]

    def body(x_ref, out_ref, send_sems, recv_sems):
        x, y, c = _my_place()
        me = 4 * x + 2 * y + c
        out_ref[me] = x_ref[me] if per_dest else x_ref[...]
        copies = []
        for k in range(1, N_DEV):
            peer = (_flip(x, k & 4), _flip(y, k & 2), _flip(c, k & 1))
            dest = 4 * peer[0] + 2 * peer[1] + peer[2]
            cp = pltpu.make_async_remote_copy(
                src_ref=x_ref.at[dest] if per_dest else x_ref, dst_ref=out_ref.at[me],
                send_sem=send_sems.at[k - 1], recv_sem=recv_sems.at[k - 1], device_id=peer, device_id_type=MESH)
            cp.start()
            copies.append(cp)
        for cp in copies:
            cp.wait()

    return _pc(
        body, name=name,
        out_shape=jax.ShapeDtypeStruct((N_DEV, rows, cols), block.dtype),
        in_specs=[pl.BlockSpec(memory_space=pltpu.VMEM)],
        out_specs=pl.BlockSpec(memory_space=pltpu.VMEM),
        scratch_shapes=[pltpu.SemaphoreType.DMA((N_DEV - 1,)), pltpu.SemaphoreType.DMA((N_DEV - 1,))],
    )(block)


ALL_PEERS = tuple(range(N_DEV))
CHIP_PEERS = (0, 1, 2, 4, 6)
OTHER_CHIPS = (2, 4, 6)


def _flag(per_dest, t):
    return per_dest[t] if isinstance(per_dest, (list, tuple)) else per_dest


def _peer(k):
    x, y, c = _my_place()
    peer = (_flip(x, k & 4), _flip(y, k & 2), _flip(c, k & 1))
    return peer, 4 * peer[0] + 2 * peer[1] + peer[2]


def _peer_lists(peers, nt):
    return list(peers) if isinstance(peers, list) else [peers] * nt


def _exchange_copies(srcs, lands, send_sems, recv_sems, per_dest, peers=ALL_PEERS):
    x, y, c = _my_place()
    me = 4 * x + 2 * y + c
    copies = []
    for t, (src, land, ks) in enumerate(zip(srcs, lands, _peer_lists(peers, len(srcs)))):
        for k in ks:
            peer, dest = _peer(k)
            s = len(copies)
            copies.append(pltpu.make_async_remote_copy(
                src_ref=src.at[dest] if _flag(per_dest, t) else src, dst_ref=land.at[me],
                send_sem=send_sems[s], recv_sem=recv_sems[s], device_id=peer, device_id_type=MESH))
    return copies


def _forward_copies(lands, send_sems, recv_sems):
    sibling, _ = _peer(1)
    copies = []
    for t, land in enumerate(lands):
        for n, k in enumerate(OTHER_CHIPS):
            _, slot = _peer(k)
            s = t * len(OTHER_CHIPS) + n
            copies.append(pltpu.make_async_remote_copy(
                src_ref=land.at[slot], dst_ref=land.at[slot], send_sem=send_sems[s], recv_sem=recv_sems[s],
                device_id=sibling, device_id_type=MESH))
    return copies


def _forward_start(lands, after, name):
    nt = len(lands)
    ns = nt * len(OTHER_CHIPS)

    def body(*refs):
        ins, outs = refs[:nt + 1], refs[nt + 1:]
        for cp in _forward_copies(ins[:nt], outs[:ns], outs[ns:2 * ns]):
            cp.start()
        outs[-1][...] = jnp.zeros_like(outs[-1])

    res = _pc(
        body, name=name,
        out_shape=(*[pltpu.SemaphoreType.DMA(())] * (2 * ns), *[pltpu.HBM(a.shape, a.dtype) for a in lands],
                   jax.ShapeDtypeStruct((8, LANES), F32)),
        in_specs=[HBM] * nt + [ANY],
        out_specs=(*[SEM] * (2 * ns), *[HBM] * nt, pl.BlockSpec(memory_space=pltpu.VMEM)),
        input_output_aliases={i: 2 * ns + i for i in range(nt)},
        compiler_params=pltpu.CompilerParams(has_side_effects=EFFECT),
    )(*lands, after)
    return list(res[:ns]), list(res[ns:2 * ns]), list(res[2 * ns:2 * ns + nt]), res[-1]


def _forward_wait(state, after, name):
    send_sems, recv_sems, lands, _ = state
    nt, ns = len(lands), len(send_sems)

    def body(*refs):
        sems = refs[nt:nt + 2 * ns]
        for cp in _forward_copies(refs[:nt], sems[:ns], sems[ns:]):
            cp.wait_send()
            cp.wait_recv()

    res = _pc(
        body, name=name,
        out_shape=tuple(pltpu.HBM(a.shape, a.dtype) for a in lands),
        in_specs=[HBM] * nt + [SEM] * (2 * ns) + [ANY],
        out_specs=tuple([HBM] * nt),
        input_output_aliases={i: i for i in range(nt)},
        compiler_params=pltpu.CompilerParams(has_side_effects=EFFECT),
    )(*lands, *send_sems, *recv_sems, after)
    return list(res)


def _exchange_start(srcs, per_dest, after, name, peers=ALL_PEERS):
    nt = len(srcs)
    peers = _peer_lists(peers, nt)
    ns = sum(len(ks) for ks in peers)
    lands = [lax.empty((N_DEV,) + (s.shape[1:] if _flag(per_dest, t) else s.shape), s.dtype) for t, s in enumerate(srcs)]

    def body(*refs):
        ins, outs = refs[:2 * nt + 1], refs[2 * nt + 1:]
        for cp in _exchange_copies(ins[:nt], ins[nt:2 * nt], outs[:ns], outs[ns:2 * ns], per_dest, peers):
            cp.start()
        outs[-1][...] = jnp.zeros_like(outs[-1])

    hbm = [pltpu.with_memory_space_constraint(a, pltpu.HBM) for a in list(srcs) + lands]
    res = _pc(
        body, name=name,
        out_shape=(*[pltpu.SemaphoreType.DMA(())] * (2 * ns), *[pltpu.HBM(a.shape, a.dtype) for a in hbm],
                   jax.ShapeDtypeStruct((8, LANES), F32)),
        in_specs=[HBM] * (2 * nt) + [ANY],
        out_specs=(*[SEM] * (2 * ns), *[HBM] * (2 * nt), pl.BlockSpec(memory_space=pltpu.VMEM)),
        input_output_aliases={i: 2 * ns + i for i in range(2 * nt)},
        compiler_params=pltpu.CompilerParams(has_side_effects=EFFECT),
    )(*hbm, after)
    sems, rest = res[:2 * ns], res[2 * ns:]
    return list(sems[:ns]), list(sems[ns:]), list(rest[:nt]), list(rest[nt:2 * nt]), peers, rest[-1]


def _exchange_wait(state, per_dest, after, name, which=None):
    send_sems, recv_sems, srcs, lands, peers, _ = state
    which = list(range(len(srcs))) if which is None else which
    per_dest = [_flag(per_dest, t) for t in which]
    after = list(after) if isinstance(after, (list, tuple)) else [after]
    first = np.cumsum([0] + [len(ks) for ks in peers])
    pick = [first[t] + n for t in which for n in range(len(peers[t]))]
    peers = [peers[t] for t in which]
    send_sems, recv_sems = [send_sems[s] for s in pick], [recv_sems[s] for s in pick]
    srcs, lands = [srcs[t] for t in which], [lands[t] for t in which]
    nt = len(srcs)
    ns = len(send_sems)

    def body(*refs):
        sems = refs[2 * nt:2 * nt + 2 * ns]
        for cp in _exchange_copies(refs[:nt], refs[nt:2 * nt], sems[:ns], sems[ns:], per_dest, peers):
            cp.wait_send()
            cp.wait_recv()

    thru = list(srcs) + list(lands)
    res = _pc(
        body, name=name,
        out_shape=tuple(pltpu.HBM(a.shape, a.dtype) for a in thru),
        in_specs=[HBM] * (2 * nt) + [SEM] * (2 * ns) + [ANY] * len(after),
        out_specs=tuple([HBM] * (2 * nt)),
        input_output_aliases={i: i for i in range(2 * nt)},
        compiler_params=pltpu.CompilerParams(has_side_effects=EFFECT),
    )(*thru, *send_sems, *recv_sems, *after)
    return list(res[nt:])


def _mod_fwd(cond, ada_w, bias):
    depth, d, nb = ada_w.shape

    def body(c_ref, w_ref, b_ref, o_ref):
        s = _silu(c_ref[...]).astype(BF16)
        o_ref[...] = _dot(s, w_ref[...].astype(BF16)) + b_ref[...]

    return _pc(
        body, name="mod_fwd", grid=(depth,),
        in_specs=[pl.BlockSpec((16, d), lambda i: (0, 0)), pl.BlockSpec((None, d, nb), lambda i: (i, 0, 0)),
                  pl.BlockSpec((None, 1, nb), lambda i: (i, 0, 0))],
        out_specs=pl.BlockSpec((None, 16, nb), lambda i: (i, 0, 0)),
        out_shape=jax.ShapeDtypeStruct((depth, 16, nb), F32),
        compiler_params=_params("parallel"),
    )(cond, ada_w, bias.reshape(depth, 1, nb))


def _mod_bwd(cond, ada_w, dm_all, dm_mine):
    depth, d, nb = ada_w.shape
    d3 = dm_all.shape[-1]

    def body(c_ref, w_ref, all_ref, call_ref, mine_ref, cmine_ref, gw_ref, gb_ref, part_ref, ds_ref):
        i = pl.program_id(0)
        cond_v = c_ref[...]
        s = _silu(cond_v).astype(BF16)
        has_ctx = jnp.where(i < 2, 1.0, 0.0)
        tot_all = jnp.sum(call_ref[...], axis=0, keepdims=True) * has_ctx
        tot_mine = jnp.broadcast_to(jnp.sum(cmine_ref[...], axis=0, keepdims=True) * has_ctx, (8, nb)).astype(BF16)
        gb_ref[...] = jnp.sum(all_ref[...], axis=0, keepdims=True) + tot_all
        gw_ref[...] = _dot(s[0:8], mine_ref[...].astype(BF16), 0, 0) + _dot(s[8:16], tot_mine, 0, 0)
        part = _dot(tot_mine, w_ref[...].astype(BF16), 1, 1)

        @pl.when(i == 0)
        def _():
            part_ref[...] = jnp.zeros_like(part_ref)
            ds_ref[...] = _dsilu(cond_v)

        part_ref[...] += part

    def rows(width, which):
        return pl.BlockSpec((None, N_DEV, width), which)

    layer = lambda i: (i, 0, 0)
    ctx_layer = lambda i: (jnp.minimum(i, 1) + 4, 0, 0)
    return _pc(
        body, name="mod_bwd", grid=(depth,),
        in_specs=[pl.BlockSpec((16, d), lambda i: (0, 0)), pl.BlockSpec((None, d, nb), layer),
                  rows(d3, layer), rows(d3, ctx_layer), rows(nb, layer), rows(nb, ctx_layer)],
        out_specs=[pl.BlockSpec((None, d, nb), layer), pl.BlockSpec((None, 1, d3), layer),
                   pl.BlockSpec((8, d), lambda i: (0, 0)), pl.BlockSpec((16, d), lambda i: (0, 0))],
        out_shape=[jax.ShapeDtypeStruct((depth, d, nb), F32), jax.ShapeDtypeStruct((depth, 1, d3), F32),
                   jax.ShapeDtypeStruct((8, d), F32), jax.ShapeDtypeStruct((16, d), F32)],
        compiler_params=_params("arbitrary"),
    )(cond, ada_w, dm_all, dm_all, dm_mine, dm_mine)


def _norm_fwd(xs, g, mod, tr, seg_tiles, name):
    t, d = xs.shape

    def body(x_ref, g_ref, mod_ref, h_ref):
        x = x_ref[...]
        r = lax.rsqrt(jnp.mean(x * x, axis=-1, keepdims=True) + RMS_EPS)
        y = (x * r) * g_ref[...]
        h_ref[...] = (y * (1.0 + mod_ref[1:2, :]) + mod_ref[0:1, :]).astype(BF16)

    return _pc(
        body, name=name, grid=(t // tr,),
        in_specs=[pl.BlockSpec((tr, d), lambda i: (i, 0)), pl.BlockSpec((1, d), lambda i: (0, 0)),
                  pl.BlockSpec((None, 8, d), lambda i: (i // seg_tiles, 0, 0))],
        out_specs=pl.BlockSpec((tr, d), lambda i: (i, 0)),
        out_shape=jax.ShapeDtypeStruct((t, d), BF16),
        compiler_params=_params("parallel"),
    )(xs, g, mod)


def _resid_grad(dx, i, seg_tiles, yx_ref, gate_ref, dyx_ref, gsum_ref):
    dyx_ref[...] = (dx * gate_ref[2:3, :]).astype(BF16)

    @pl.when(i % seg_tiles == 0)
    def _():
        gsum_ref[...] = jnp.zeros_like(gsum_ref)

    gsum_ref[0:1, :] += jnp.sum(dx * yx_ref[...], axis=0, keepdims=True)


def _norm_bwd(xs, dh, dres, g, mod, tr, seg_tiles, name, res_tiles=None, out_tiles=None, below=None):
    t, d = xs.shape
    n_tiles = t // tr
    res_tiles = n_tiles if res_tiles is None else res_tiles
    out_tiles = n_tiles if out_tiles is None else out_tiles

    def body(x_ref, dh_ref, dres_ref, g_ref, mod_ref, *rest):
        i = pl.program_id(0)
        x = x_ref[...]
        r = lax.rsqrt(jnp.mean(x * x, axis=-1, keepdims=True) + RMS_EPS)
        xn = x * r
        dhv = dh_ref[...]
        gain = g_ref[...]
        one_scale = 1.0 + mod_ref[1:2, :]
        dxn = dhv * (gain * one_scale)
        dx = r * (dxn - xn * jnp.mean(dxn * xn, axis=-1, keepdims=True))
        if res_tiles == n_tiles:
            dx = dx + dres_ref[...]
        else:
            dx = dx + jnp.where(i < res_tiles, dres_ref[...], 0.0)
        if below is None:
            dx_ref, sum_ref = rest
        else:
            yx_ref, gate_ref, dx_ref, sum_ref, dyx_ref, gsum_ref = rest
            _resid_grad(dx, i, seg_tiles, yx_ref, gate_ref, dyx_ref, gsum_ref)
        if out_tiles == n_tiles:
            dx_ref[...] = dx
        else:
            @pl.when(i < out_tiles)
            def _():
                dx_ref[...] = dx

        @pl.when(i % seg_tiles == 0)
        def _():
            sum_ref[...] = jnp.zeros_like(sum_ref)

        sum_ref[0:1, :] += jnp.sum(dhv, axis=0, keepdims=True)
        sum_ref[1:2, :] += jnp.sum(dhv * (xn * gain), axis=0, keepdims=True)
        sum_ref[2:3, :] += jnp.sum(dhv * one_scale * xn, axis=0, keepdims=True)

    row = pl.BlockSpec((tr, d), lambda i: (i, 0))
    seg = pl.BlockSpec((None, 8, d), lambda i: (i // seg_tiles, 0, 0))
    in_specs = [row, row, pl.BlockSpec((tr, d), lambda i: (jnp.minimum(i, res_tiles - 1), 0)),
                pl.BlockSpec((1, d), lambda i: (0, 0)), seg]
    out_specs = [pl.BlockSpec((tr, d), lambda i: (jnp.minimum(i, out_tiles - 1), 0)), seg]
    out_shape = [jax.ShapeDtypeStruct((out_tiles * tr, d), F32), jax.ShapeDtypeStruct((mod.shape[0], 8, d), F32)]
    args = [xs, dh, dres, g, mod]
    if below is not None:
        in_specs += [row, seg]
        out_specs += [row, seg]
        out_shape += [jax.ShapeDtypeStruct((t, d), BF16), jax.ShapeDtypeStruct((below[1].shape[0], 8, d), F32)]
        args += list(below)
    return _pc(
        body, name=name, grid=(n_tiles,), in_specs=in_specs, out_specs=out_specs, out_shape=out_shape,
        compiler_params=_params("arbitrary"),
    )(*args)


def _loss_head(xs, target, g, yx, mod, tr):
    t, d = xs.shape

    def body(x_ref, t_ref, g_ref, yx_ref, gate_ref, loss_ref, dx_ref, dg_ref, dyx_ref, gsum_ref):
        i = pl.program_id(0)
        x = x_ref[...]
        r = lax.rsqrt(jnp.mean(x * x, axis=-1, keepdims=True) + RMS_EPS)
        xn = x * r
        gain = g_ref[...]
        err = xn * gain - t_ref[...]
        dy = err * (1.0 / d)
        dxn = dy * gain
        dx = r * (dxn - xn * jnp.mean(dxn * xn, axis=-1, keepdims=True))
        dx_ref[...] = dx
        _resid_grad(dx, i, t // tr, yx_ref, gate_ref, dyx_ref, gsum_ref)

        @pl.when(i == 0)
        def _():
            loss_ref[...] = jnp.zeros_like(loss_ref)
            dg_ref[...] = jnp.zeros_like(dg_ref)

        loss_ref[...] += 0.5 * jnp.sum(jnp.mean(err * err, axis=-1, keepdims=True))
        dg_ref[0:1, :] += jnp.sum(dy * xn, axis=0, keepdims=True)

    row = pl.BlockSpec((tr, d), lambda i: (i, 0))
    seg = pl.BlockSpec((None, 8, d), lambda i: (0, 0, 0))
    return _pc(
        body, name="loss_head", grid=(t // tr,),
        in_specs=[row, row, pl.BlockSpec((1, d), lambda i: (0, 0)), row, seg],
        out_specs=[pl.BlockSpec((8, LANES), lambda i: (0, 0)), row, pl.BlockSpec((8, d), lambda i: (0, 0)), row, seg],
        out_shape=[jax.ShapeDtypeStruct((8, LANES), F32), jax.ShapeDtypeStruct((t, d), F32),
                   jax.ShapeDtypeStruct((8, d), F32), jax.ShapeDtypeStruct((t, d), BF16),
                   jax.ShapeDtypeStruct((1, 8, d), F32)],
        compiler_params=_params("arbitrary"),
    )(xs, target, g, yx, mod)


def _proj_in(h, w, layer, width, name, blocks=None, dtype=F32):
    t, d = h.shape
    n8 = w.shape[-1]
    first, count = blocks if blocks is not None else (0, N_DEV)
    per_part = width // n8
    tm = _tile(t, 1152)

    def body(a_ref, b_ref, o_ref):
        o_ref[...] = _dot(a_ref[...], b_ref[...]).astype(dtype)

    return _pc(
        body, name=name, grid=(t // tm, count),
        in_specs=[pl.BlockSpec((tm, d), lambda i, j: (i, 0)),
                  pl.BlockSpec((None, None, d, n8), lambda i, j: (first + j, layer, 0, 0))],
        out_specs=pl.BlockSpec((None, tm, n8), lambda i, j: (j // per_part, i, j % per_part)),
        out_shape=jax.ShapeDtypeStruct((count // per_part, t, width), dtype),
        compiler_params=_params("parallel", "parallel"),
    )(h, w)


def _proj_out(z, w, res, mod, tm, seg_tiles, name, nxt=None):
    t, k = z.shape
    d = w.shape[1]

    def body(z_ref, w_ref, res_ref, mod_ref, *rest):
        yx = _dot(z_ref[...], w_ref[...])
        x = res_ref[...] + mod_ref[2:3, :] * yx
        if nxt is None:
            yx_ref, x_ref = rest
        else:
            g_ref, nmod_ref, yx_ref, x_ref, h_ref = rest
            r = lax.rsqrt(jnp.mean(x * x, axis=-1, keepdims=True) + RMS_EPS)
            h_ref[...] = (((x * r) * g_ref[...]) * (1.0 + nmod_ref[1:2, :]) + nmod_ref[0:1, :]).astype(BF16)
        yx_ref[...] = yx
        x_ref[...] = x

    tile = pl.BlockSpec((tm, d), lambda i: (i, 0))
    seg = pl.BlockSpec((None, 8, d), lambda i: (i // seg_tiles, 0, 0))
    in_specs = [pl.BlockSpec((tm, k), lambda i: (i, 0)), pl.BlockSpec((k, d), lambda i: (0, 0)), tile, seg]
    out_specs = [tile, tile]
    out_shape = [jax.ShapeDtypeStruct((t, d), F32), jax.ShapeDtypeStruct((t, d), F32)]
    args = [z, w, res, mod]
    if nxt is not None:
        in_specs += [pl.BlockSpec((1, d), lambda i: (0, 0)), seg]
        out_specs.append(tile)
        out_shape.append(jax.ShapeDtypeStruct((t, d), BF16))
        args += list(nxt)
    return _pc(
        body, name=name, grid=(t // tm,), in_specs=in_specs, out_specs=out_specs, out_shape=out_shape,
        compiler_params=_params("parallel"),
    )(*args)


def _proj_out_dz(dyx, w, name):
    t, d = dyx.shape
    width = w.shape[0]
    tm, tn = _tile(t, 1024), _tile(width, 512)

    def body(a_ref, w_ref, o_ref):
        o_ref[...] = _dot(a_ref[...], w_ref[...], 1, 1)

    return _pc(
        body, name=name, grid=(t // tm, width // tn),
        in_specs=[pl.BlockSpec((tm, d), lambda i, j: (i, 0)), pl.BlockSpec((tn, d), lambda i, j: (j, 0))],
        out_specs=pl.BlockSpec((tm, tn), lambda i, j: (i, j)),
        out_shape=jax.ShapeDtypeStruct((t, width), F32),
        compiler_params=_params("parallel", "parallel"),
    )(dyx, w)


def _proj_in_dh(dpre, w, layer, name, after=None):
    parts, t, width = dpre.shape
    d, n8 = w.shape[-2:]
    per_part = width // n8
    tm, tn = _tile(t, 768), _tile(d, 512)

    def body(a_ref, w_ref, *rest):
        o_ref = rest[-1]
        acc = None
        for p in range(parts):
            for s in range(per_part):
                term = _dot(a_ref[p, :, s * n8:(s + 1) * n8], w_ref[p * per_part + s], 1, 1)
                acc = term if acc is None else acc + term
        o_ref[...] = acc

    extra = [] if after is None else [after]
    return _pc(
        body, name=name, grid=(t // tm, d // tn),
        in_specs=[pl.BlockSpec((parts, tm, width), lambda i, j: (0, i, 0)),
                  pl.BlockSpec((N_DEV, None, tn, n8), lambda i, j: (0, layer, j, 0))] + [ANY] * len(extra),
        out_specs=pl.BlockSpec((tm, tn), lambda i, j: (i, j)),
        out_shape=jax.ShapeDtypeStruct((t, d), F32),
        compiler_params=_params("parallel", "parallel"),
    )(dpre, w, *extra)


def _transposed(a_ref):
    return a_ref[...].T


def _grad_w_in(h, dpre, n8, name):
    t, d = h.shape
    parts, _, width = dpre.shape
    per_part = width // n8
    tm, tk = _tile(d, 512), _tile(t, 1152)
    nk = t // tk

    def body(a_ref, b_ref, o_ref, acc_ref):
        k = pl.program_id(1)

        @pl.when(k == 0)
        def _():
            acc_ref[...] = jnp.zeros_like(acc_ref)

        at = _transposed(a_ref)
        for p in range(parts):
            r = _dot(at, b_ref[p])
            for s in range(per_part):
                acc_ref[p * per_part + s] += r[:, s * n8:(s + 1) * n8]

        @pl.when(k == nk - 1)
        def _():
            o_ref[...] = acc_ref[...].astype(BF16)

    return _pc(
        body, name=name, grid=(d // tm, nk),
        in_specs=[pl.BlockSpec((tk, tm), lambda i, k: (k, i)), pl.BlockSpec((parts, tk, width), lambda i, k: (0, k, 0))],
        out_specs=pl.BlockSpec((parts * per_part, tm, n8), lambda i, k: (0, i, 0)),
        out_shape=jax.ShapeDtypeStruct((parts * per_part, d, n8), BF16),
        scratch_shapes=[pltpu.VMEM((parts * per_part, tm, n8), F32)],
        compiler_params=_params("parallel", "arbitrary"),
    )(h, dpre)


def _grad_w_out(z, dyx, name):
    width = z.shape[1]
    t, d = dyx.shape
    tm, tk = _tile(width, 512), _tile(t, 1152)
    nk = t // tk

    def body(a_ref, b_ref, o_ref, acc_ref):
        k = pl.program_id(1)

        @pl.when(k == 0)
        def _():
            acc_ref[...] = jnp.zeros_like(acc_ref)

        acc_ref[...] += _dot(_transposed(a_ref), b_ref[...])

        @pl.when(k == nk - 1)
        def _():
            o_ref[...] = acc_ref[...].astype(BF16)

    return _pc(
        body, name=name, grid=(width // tm, nk),
        in_specs=[pl.BlockSpec((tk, tm), lambda i, k: (k, i)), pl.BlockSpec((tk, d), lambda i, k: (k, 0))],
        out_specs=pl.BlockSpec((tm, d), lambda i, k: (i, 0)),
        out_shape=jax.ShapeDtypeStruct((width, d), BF16),
        scratch_shapes=[pltpu.VMEM((tm, d), F32)],
        compiler_params=_params("parallel", "arbitrary"),
    )(z, dyx)


def _shift(v, k):
    n = v.shape[0]
    return pltpu.roll(v, k % n, 0)


def _window_sum(v, win):
    s = v + _shift(v, 1)
    step = 1
    while 2 * step < win:
        s = _shift(s, step) + _shift(s, -step)
        step *= 2
    return s


def _window_count(base, seg_len, win, shape):
    t = base + lax.broadcasted_iota(jnp.int32, shape, 0)
    hi = jnp.minimum(t + win // 2, seg_len)
    lo = jnp.maximum(t - win // 2, 0)
    return (hi - lo).astype(F32)


def _pad_offsets(segs):
    return [HALO * (s + 1) + st for s, (st, _) in enumerate(segs)]


def _for_chunks(segs, fn):
    offs = _pad_offsets(segs)
    for s, (st, ln) in enumerate(segs):
        def step(ci, carry, s=s, st=st, ln=ln):
            fn(s, st, ln, offs[s], pl.multiple_of(ci * CHUNK, CHUNK))
            return carry
        lax.fori_loop(0, ln // CHUNK, step, 0)


def _pool_fwd(pre, w_grp, scale, segs, name):
    _, t, width = pre.shape
    grp = width // len(POOL_WINDOWS)
    padded = t + HALO * (len(segs) + 1)

    def group(win, pre_ref, w_ref, sc_ref, z_ref, diff_ref, pad_ref):
        pad_ref[...] = jnp.zeros_like(pad_ref)

        def fill(s, st, ln, off, b):
            pad_ref[pl.ds(off + b, CHUNK), :] = pre_ref[0, pl.ds(st + b, CHUNK), :]

        _for_chunks(segs, fill)

        def mix(s, st, ln, off, b):
            ext = pad_ref[pl.ds(off - HALO + b, CHUNK + 2 * HALO), :]
            total = _window_sum(ext, win)[HALO:HALO + CHUNK]
            u = pre_ref[0, pl.ds(st + b, CHUNK), :]
            diff = (total / _window_count(b, ln, win, u.shape) - u).astype(BF16)
            mixed = _dot(diff, w_ref[...])
            gate = _silu(pre_ref[1, pl.ds(st + b, CHUNK), :])
            z_ref[pl.ds(st + b, CHUNK), :] = (mixed * sc_ref[...] * gate).astype(BF16)
            diff_ref[pl.ds(st + b, CHUNK), :] = diff

        _for_chunks(segs, mix)

    def body(pre_ref, w_ref, sc_ref, z_ref, diff_ref, pad_ref):
        gi = pl.program_id(0)
        for widx, win in enumerate(POOL_WINDOWS):
            @pl.when(gi == widx)
            def _(win=win):
                group(win, pre_ref, w_ref, sc_ref, z_ref, diff_ref, pad_ref)

    col = pl.BlockSpec((t, grp), lambda g: (0, g))
    return _pc(
        body, name=name, grid=(len(POOL_WINDOWS),),
        in_specs=[pl.BlockSpec((2, t, grp), lambda g: (0, 0, g)), pl.BlockSpec((None, grp, grp), lambda g: (g, 0, 0)),
                  pl.BlockSpec((1, grp), lambda g: (0, g))],
        out_specs=[col, col],
        out_shape=[jax.ShapeDtypeStruct((t, width), BF16), jax.ShapeDtypeStruct((t, width), BF16)],
        scratch_shapes=[pltpu.VMEM((padded, grp), F32)],
        compiler_params=_params("parallel"),
    )(pre, w_grp, scale)


def _pool_bwd(dz, diff, pre, w_grp, scale, segs, name):
    _, t, width = pre.shape
    grp = width // len(POOL_WINDOWS)
    padded = t + HALO * (len(segs) + 1)

    def group(win, dz_ref, diff_ref, pre_ref, w_ref, sc_ref, dpre_ref, dw_ref, dsc_ref, pad_ref, dd_ref):
        pad_ref[...] = jnp.zeros_like(pad_ref)
        dw_ref[...] = jnp.zeros_like(dw_ref)
        dsc_ref[...] = jnp.zeros_like(dsc_ref)

        def first(s, st, ln, off, b):
            rows = pl.ds(st + b, CHUNK)
            diff_v = diff_ref[rows, :]
            mixed = _dot(diff_v, w_ref[...])
            g = pre_ref[1, rows, :]
            sg = _silu(g)
            dzv = dz_ref[rows, :]
            dmixed = (dzv * sc_ref[...] * sg).astype(BF16)
            dsc_ref[...] += jnp.sum(dzv * mixed * sg, axis=0, keepdims=True)
            dpre_ref[1, rows, :] = (dzv * mixed * sc_ref[...] * _dsilu(g)).astype(BF16)
            ddiff = _dot(dmixed, w_ref[...], 1, 1)
            dw_ref[...] += _dot(diff_v, dmixed, 0, 0)
            dd_ref[rows, :] = ddiff
            pad_ref[pl.ds(off + b, CHUNK), :] = ddiff / _window_count(b, ln, win, ddiff.shape)

        _for_chunks(segs, first)

        def second(s, st, ln, off, b):
            rows = pl.ds(st + b, CHUNK)
            ext = pad_ref[pl.ds(off - HALO + b, CHUNK + 2 * HALO), :]
            total = _shift(_window_sum(ext, win), -1)[HALO:HALO + CHUNK]
            dpre_ref[0, rows, :] = (total - dd_ref[rows, :]).astype(BF16)

        _for_chunks(segs, second)

    def body(dz_ref, diff_ref, pre_ref, w_ref, sc_ref, dpre_ref, dw_ref, dsc_ref, pad_ref, dd_ref):
        gi = pl.program_id(0)
        for widx, win in enumerate(POOL_WINDOWS):
            @pl.when(gi == widx)
            def _(win=win):
                group(win, dz_ref, diff_ref, pre_ref, w_ref, sc_ref, dpre_ref, dw_ref, dsc_ref, pad_ref, dd_ref)

    col = pl.BlockSpec((t, grp), lambda g: (0, g))
    both = pl.BlockSpec((2, t, grp), lambda g: (0, 0, g))
    wspec = pl.BlockSpec((None, grp, grp), lambda g: (g, 0, 0))
    sspec = pl.BlockSpec((1, grp), lambda g: (0, g))
    return _pc(
        body, name=name, grid=(len(POOL_WINDOWS),),
        in_specs=[col, col, both, wspec, sspec],
        out_specs=[both, wspec, sspec],
        out_shape=[jax.ShapeDtypeStruct((2, t, width), BF16), jax.ShapeDtypeStruct((len(POOL_WINDOWS), grp, grp), F32),
                   jax.ShapeDtypeStruct((1, width), F32)],
        scratch_shapes=[pltpu.VMEM((padded, grp), F32), pltpu.VMEM((t, grp), F32)],
        compiler_params=_params("parallel"),
    )(dz, diff, pre, w_grp, scale)


def _conv_fwd(pre, dw, db, name):
    _, t, width = pre.shape
    cb = LANES
    segs = [(0, t)]

    def body(pre_ref, dw_ref, db_ref, z_ref, pad_ref):
        pad_ref[...] = jnp.zeros_like(pad_ref)

        def fill(s, st, ln, off, b):
            rows = pl.ds(b, CHUNK)
            pad_ref[pl.ds(off + b, CHUNK), :] = pre_ref[1, rows, :] * pre_ref[2, rows, :]

        _for_chunks(segs, fill)

        def mix(s, st, ln, off, b):
            rows = pl.ds(b, CHUNK)
            ext = pad_ref[pl.ds(off - HALO + b, CHUNK + 2 * HALO), :]
            conv = (dw_ref[0:1, :] * _shift(ext, 1) + dw_ref[1:2, :] * ext + dw_ref[2:3, :] * _shift(ext, -1))
            conv = conv[HALO:HALO + CHUNK] + db_ref[...]
            y = pre_ref[0, rows, :] * conv
            z_ref[rows, :] = (y * _silu(pre_ref[3, rows, :])).astype(BF16)

        _for_chunks(segs, mix)

    return _pc(
        body, name=name, grid=(width // cb,),
        in_specs=[pl.BlockSpec((4, t, cb), lambda j: (0, 0, j)), pl.BlockSpec((8, cb), lambda j: (0, j)),
                  pl.BlockSpec((1, cb), lambda j: (0, j))],
        out_specs=pl.BlockSpec((t, cb), lambda j: (0, j)),
        out_shape=jax.ShapeDtypeStruct((t, width), BF16),
        scratch_shapes=[pltpu.VMEM((t + 2 * HALO, cb), F32)],
        compiler_params=_params("parallel"),
    )(pre, dw, db)


def _conv_bwd(dz, pre, dw, db, name):
    _, t, width = pre.shape
    cb = LANES
    segs = [(0, t)]

    def body(dz_ref, pre_ref, dw_ref, db_ref, dpre_ref, ddw_ref, ddb_ref, pad_a, pad_c):
        pad_a[...] = jnp.zeros_like(pad_a)
        pad_c[...] = jnp.zeros_like(pad_c)
        ddw_ref[...] = jnp.zeros_like(ddw_ref)
        ddb_ref[...] = jnp.zeros_like(ddb_ref)

        def fill(s, st, ln, off, b):
            rows = pl.ds(b, CHUNK)
            pad_a[pl.ds(off + b, CHUNK), :] = pre_ref[1, rows, :] * pre_ref[2, rows, :]

        _for_chunks(segs, fill)

        def first(s, st, ln, off, b):
            rows = pl.ds(b, CHUNK)
            ext = pad_a[pl.ds(off - HALO + b, CHUNK + 2 * HALO), :]
            prev, nxt = _shift(ext, 1)[HALO:HALO + CHUNK], _shift(ext, -1)[HALO:HALO + CHUNK]
            here = ext[HALO:HALO + CHUNK]
            conv = dw_ref[0:1, :] * prev + dw_ref[1:2, :] * here + dw_ref[2:3, :] * nxt + db_ref[...]
            bg, g = pre_ref[0, rows, :], pre_ref[3, rows, :]
            dzv = dz_ref[rows, :]
            dy = dzv * _silu(g)
            dpre_ref[3, rows, :] = (dzv * (bg * conv) * _dsilu(g)).astype(BF16)
            dpre_ref[0, rows, :] = (dy * conv).astype(BF16)
            dconv = dy * bg
            pad_c[pl.ds(off + b, CHUNK), :] = dconv
            ddw_ref[0:1, :] += jnp.sum(dconv * prev, axis=0, keepdims=True)
            ddw_ref[1:2, :] += jnp.sum(dconv * here, axis=0, keepdims=True)
            ddw_ref[2:3, :] += jnp.sum(dconv * nxt, axis=0, keepdims=True)
            ddb_ref[0:1, :] += jnp.sum(dconv, axis=0, keepdims=True)

        _for_chunks(segs, first)

        def second(s, st, ln, off, b):
            rows = pl.ds(b, CHUNK)
            ext = pad_c[pl.ds(off - HALO + b, CHUNK + 2 * HALO), :]
            da = (dw_ref[0:1, :] * _shift(ext, -1) + dw_ref[1:2, :] * ext + dw_ref[2:3, :] * _shift(ext, 1))
            da = da[HALO:HALO + CHUNK]
            dpre_ref[1, rows, :] = (da * pre_ref[2, rows, :]).astype(BF16)
            dpre_ref[2, rows, :] = (da * pre_ref[1, rows, :]).astype(BF16)

        _for_chunks(segs, second)

    quad = pl.BlockSpec((4, t, cb), lambda j: (0, 0, j))
    rows8 = pl.BlockSpec((8, cb), lambda j: (0, j))
    return _pc(
        body, name=name, grid=(width // cb,),
        in_specs=[pl.BlockSpec((t, cb), lambda j: (0, j)), quad, rows8, pl.BlockSpec((1, cb), lambda j: (0, j))],
        out_specs=[quad, rows8, rows8],
        out_shape=[jax.ShapeDtypeStruct((4, t, width), BF16), jax.ShapeDtypeStruct((8, width), F32),
                   jax.ShapeDtypeStruct((8, width), F32)],
        scratch_shapes=[pltpu.VMEM((t + 2 * HALO, cb), F32), pltpu.VMEM((t + 2 * HALO, cb), F32)],
        compiler_params=_params("parallel"),
    )(dz, pre, dw, db)


PAIR_TILES = 2 * WIN_ROWS - 2


def _pair_geometry():
    lane = lax.broadcasted_iota(jnp.int32, (GRID_W, LANES), 1)
    qcol = lax.broadcasted_iota(jnp.int32, (GRID_W, LANES), 0)
    low = lane < GRID_W
    kcol = jnp.where(low, lane, lane - GRID_W)
    start = jnp.clip(qcol - WIN_COLS // 2, 0, GRID_W - WIN_COLS)
    inside = (kcol >= start) & (kcol < start + WIN_COLS)
    return low, inside


def _bias_tiles(rpb_ref, rows_ref, tiles_ref, inside):
    for h in range(2):
        rows = rpb_ref[h]
        rows_ref[h] = (pltpu.roll(rows, LANES - (WIN_COLS - 1), 1)
                       + pltpu.roll(pltpu.roll(rows, GRID_W - (WIN_COLS - 1), 1), 2 * WIN_ROWS - 1, 0))
        for t in range(PAIR_TILES):
            both = jnp.broadcast_to(rows_ref[h, t:t + 1, :], (GRID_W, LANES))
            tiles_ref[h, t] = jnp.where(inside, pltpu.roll(both, 0, 1, stride=1, stride_axis=0), MASKED)


def _bias_tiles_grad(dtiles_ref, drpb_ref):
    n = PAIR_TILES * GRID_W
    qcol = lax.broadcasted_iota(jnp.int32, (n, LANES), 0) & (GRID_W - 1)
    lane = lax.broadcasted_iota(jnp.int32, (1, LANES), 1)
    zero = jnp.zeros((1, LANES), F32)
    for h in range(2):
        v = pltpu.roll(dtiles_ref[h].reshape(n, LANES), WIN_COLS - 1, 1)
        for bit in range(6):
            v = jnp.where((qcol >> bit) & 1 == 1, pltpu.roll(v, LANES - (1 << bit), 1), v)
        sums = [jnp.sum(v[t * GRID_W:(t + 1) * GRID_W], axis=0, keepdims=True) for t in range(PAIR_TILES)]
        for r in range(2 * WIN_ROWS):
            here = sums[r] if r < PAIR_TILES else zero
            prev = pltpu.roll(sums[r - 1], GRID_W, 1) if 1 <= r <= PAIR_TILES else zero
            drpb_ref[h, r:r + 1, :] = jnp.where(lane < 2 * WIN_COLS - 1, here + prev, 0.0)


def _attn_rows(r, n_rows):
    first = jnp.clip(r - WIN_ROWS // 2, 0, n_rows - WIN_ROWS)
    return first, first - r + WIN_ROWS - 1


def _softmax(s_loc, s_ctx):
    m = jnp.maximum(jnp.max(s_loc, axis=-1, keepdims=True), jnp.max(s_ctx, axis=-1, keepdims=True))
    e_loc, e_ctx = jnp.exp(s_loc - m), jnp.exp(s_ctx - m)
    inv = 1.0 / (jnp.sum(e_loc, axis=-1, keepdims=True) + jnp.sum(e_ctx, axis=-1, keepdims=True))
    return e_loc * inv, e_ctx * inv


def _pair_bias(tiles_ref, j):
    return jnp.concatenate(
        [jnp.concatenate([tiles_ref[h, j + 2 * m] for m in range(WIN_ROWS // 2)], axis=1) for h in range(2)], axis=0)


ROWS_PER_STEP = 4


def _by_head(tile, low):
    zero = jnp.zeros_like(tile)
    return jnp.concatenate([jnp.where(low, tile, zero), jnp.where(low, zero, tile)], axis=0)


def _merge_heads(stacked, low):
    return jnp.where(low, stacked[:GRID_W], stacked[GRID_W:])


def _attn_items(step, n_rows, q_ref, low):
    items = []
    for u in range(ROWS_PER_STEP):
        r = step * ROWS_PER_STEP + u
        first, j = _attn_rows(r, n_rows)
        rows = pl.ds(pl.multiple_of(r * GRID_W, GRID_W), GRID_W)
        keys = pl.ds(pl.multiple_of(first * GRID_W, GRID_W), WIN_ROWS * GRID_W)
        q = (q_ref[rows, :].astype(F32) * HEAD_DIM ** -0.5).astype(BF16)
        items.append((rows, keys, j, _by_head(q, low)))
    return items


def _attn_fwd(qkv, gate, rpb, seq):
    _, t, width = qkv.shape
    n_rows = seq // GRID_W
    n_ctx = t - seq
    blk = WIN_ROWS * GRID_W

    def body(q_ref, k_ref, v_ref, g_ref, rpb_ref, z_ref, o_ref, rows_ref, tiles_ref):
        low, inside = _pair_geometry()
        _bias_tiles(rpb_ref, rows_ref, tiles_ref, inside)
        ctx = pl.ds(seq, n_ctx)

        def step(i, carry):
            items = _attn_items(i, n_rows, q_ref, low)
            k_ctx, v_ctx = k_ref[ctx, :], v_ref[ctx, :]
            scores = [(_dot(q, k_ref[keys, :], 1, 1) + _pair_bias(tiles_ref, j), _dot(q, k_ctx, 1, 1))
                      for _, keys, j, q in items]
            probs = [_softmax(s_loc, s_ctx) for s_loc, s_ctx in scores]
            outs = [_dot(p_loc.astype(BF16), v_ref[keys, :]) + _dot(p_ctx.astype(BF16), v_ctx)
                    for (_, keys, _, _), (p_loc, p_ctx) in zip(items, probs)]
            for (rows, _, _, _), out in zip(items, outs):
                o = _merge_heads(out, low)
                o_ref[rows, :] = o
                z_ref[rows, :] = (o * _silu(g_ref[rows, :])).astype(BF16)
            return carry

        lax.fori_loop(0, n_rows // ROWS_PER_STEP, step, 0)

    def part(p):
        return pl.BlockSpec((None, t, LANES), lambda h: (p, 0, h))

    out = pl.BlockSpec((seq, LANES), lambda h: (0, h))
    return _pc(
        body, name="attn_fwd", grid=(width // LANES,),
        in_specs=[part(0), part(1), part(2), part(0), pl.BlockSpec((2, 2 * WIN_ROWS, LANES), lambda h: (h, 0, 0))],
        out_specs=[out, out],
        out_shape=[jax.ShapeDtypeStruct((seq, width), BF16), jax.ShapeDtypeStruct((seq, width), F32)],
        scratch_shapes=[pltpu.VMEM((2, 2 * WIN_ROWS, LANES), F32), pltpu.VMEM((2, PAIR_TILES, GRID_W, LANES), F32)],
        compiler_params=_params("parallel"),
    )(qkv, qkv, qkv, gate, rpb)


def _attn_bwd(qkv, gate, o, dz, rpb, seq):
    _, t, width = qkv.shape
    n_rows = seq // GRID_W
    n_ctx = t - seq
    blk = WIN_ROWS * GRID_W
    heads = 2 * width // LANES

    def body(q_ref, k_ref, v_ref, g_ref, o_ref, dz_ref, rpb_ref, dpre_ref, drpb_ref,
             rows_ref, tiles_ref, dtiles_ref, dk_ref, dv_ref):
        low, inside = _pair_geometry()
        _bias_tiles(rpb_ref, rows_ref, tiles_ref, inside)
        dtiles_ref[...] = jnp.zeros_like(dtiles_ref)
        dk_ref[...] = jnp.zeros_like(dk_ref)
        dv_ref[...] = jnp.zeros_like(dv_ref)
        ctx = pl.ds(seq, n_ctx)

        def step(i, carry):
            items = _attn_items(i, n_rows, q_ref, low)
            k_ctx, v_ctx = k_ref[ctx, :], v_ref[ctx, :]
            d_outs = []
            for rows, _, _, _ in items:
                g = g_ref[rows, :]
                dzv = dz_ref[rows, :]
                dpre_ref[3, rows, :] = (dzv * o_ref[rows, :] * _dsilu(g)).astype(BF16)
                d_outs.append(_by_head((dzv * _silu(g)).astype(BF16), low))
            scores = [(_dot(q, k_ref[keys, :], 1, 1) + _pair_bias(tiles_ref, j), _dot(q, k_ctx, 1, 1))
                      for _, keys, j, q in items]
            dprobs = [(_dot(d_o, v_ref[keys, :], 1, 1), _dot(d_o, v_ctx, 1, 1))
                      for (_, keys, _, _), d_o in zip(items, d_outs)]
            probs = [_softmax(s_loc, s_ctx) for s_loc, s_ctx in scores]
            dscores = []
            for (p_loc, p_ctx), (dp_loc, dp_ctx) in zip(probs, dprobs):
                delta = (jnp.sum(p_loc * dp_loc, axis=-1, keepdims=True)
                         + jnp.sum(p_ctx * dp_ctx, axis=-1, keepdims=True))
                dscores.append((p_loc * (dp_loc - delta), p_ctx * (dp_ctx - delta)))
            dqs = [_dot(ds_loc.astype(BF16), k_ref[keys, :]) + _dot(ds_ctx.astype(BF16), k_ctx)
                   for (_, keys, _, _), (ds_loc, ds_ctx) in zip(items, dscores)]
            for (rows, _, _, _), dq in zip(items, dqs):
                dpre_ref[0, rows, :] = (_merge_heads(dq, low) * HEAD_DIM ** -0.5).astype(BF16)
            for (_, keys, j, q), d_o, (p_loc, p_ctx), (ds_loc, ds_ctx) in zip(items, d_outs, probs, dscores):
                dk_ref[keys, :] += _dot(ds_loc.astype(BF16), q, 0, 0)
                dk_ref[ctx, :] += _dot(ds_ctx.astype(BF16), q, 0, 0)
                dv_ref[keys, :] += _dot(p_loc.astype(BF16), d_o, 0, 0)
                dv_ref[ctx, :] += _dot(p_ctx.astype(BF16), d_o, 0, 0)
                for h in range(2):
                    for m in range(WIN_ROWS // 2):
                        dtiles_ref[h, j + 2 * m] += ds_loc[h * GRID_W:(h + 1) * GRID_W, m * LANES:(m + 1) * LANES]
            return carry

        lax.fori_loop(0, n_rows // ROWS_PER_STEP, step, 0)
        dpre_ref[1] = dk_ref[...].astype(BF16)
        dpre_ref[2] = dv_ref[...].astype(BF16)
        dpre_ref[0, ctx, :] = jnp.zeros((n_ctx, LANES), BF16)
        dpre_ref[3, ctx, :] = jnp.zeros((n_ctx, LANES), BF16)
        _bias_tiles_grad(dtiles_ref, drpb_ref)

    def part(p):
        return pl.BlockSpec((None, t, LANES), lambda h: (p, 0, h))

    lat = pl.BlockSpec((seq, LANES), lambda h: (0, h))
    rspec = pl.BlockSpec((2, 2 * WIN_ROWS, LANES), lambda h: (h, 0, 0))
    tiles = pltpu.VMEM((2, PAIR_TILES, GRID_W, LANES), F32)
    return _pc(
        body, name="attn_bwd", grid=(width // LANES,),
        in_specs=[part(0), part(1), part(2), part(0), lat, lat, rspec],
        out_specs=[pl.BlockSpec((4, t, LANES), lambda h: (0, 0, h)), rspec],
        out_shape=[jax.ShapeDtypeStruct((4, t, width), BF16), jax.ShapeDtypeStruct((heads, 2 * WIN_ROWS, LANES), F32)],
        scratch_shapes=[pltpu.VMEM((2, 2 * WIN_ROWS, LANES), F32), tiles, tiles,
                        pltpu.VMEM((t, LANES), F32), pltpu.VMEM((t, LANES), F32)],
        compiler_params=_params("parallel"),
    )(qkv, qkv, qkv, gate, o, dz, rpb)


def _adam_update(w, m, v, g):
    m2 = ADAM_B1 * m + (1.0 - ADAM_B1) * g
    v2 = ADAM_B2 * v + (1.0 - ADAM_B2) * (g * g)
    m_hat = m2 / (1.0 - ADAM_B1 ** ADAM_STEP)
    v_hat = v2 / (1.0 - ADAM_B2 ** ADAM_STEP)
    return -ADAM_LR * (m_hat / (jnp.sqrt(v_hat) + ADAM_EPS) + ADAM_WD * w), m2, v2


def _adamw(w, m, v, parts, name):
    rows, cols = w.shape
    tr = _tile(rows, max(8, 131072 // cols), 8)
    n_parts = len(parts)

    def body(*refs):
        w_ref, m_ref, v_ref = refs[:3]
        part_refs = refs[3:3 + n_parts]
        g_ref, d_ref, nm_ref, nv_ref = refs[3 + n_parts:]
        g = part_refs[0][...].astype(F32)
        for p in part_refs[1:]:
            g = g + p[...].astype(F32)
        g_ref[...] = g
        d_ref[...], nm_ref[...], nv_ref[...] = _adam_update(w_ref[...], m_ref[...], v_ref[...], g)

    tile = pl.BlockSpec((tr, cols), lambda i: (i, 0))
    in_specs, args = [tile, tile, tile], [w, m, v]
    for p in parts:
        if isinstance(p, tuple):
            arr, k = p
            in_specs.append(pl.BlockSpec((None, tr, cols), lambda i, k=k: (k, i, 0)))
            args.append(arr)
        else:
            in_specs.append(tile)
            args.append(p)
    shape = jax.ShapeDtypeStruct((rows, cols), F32)
    return _pc(
        body, name=name, grid=(rows // tr,), in_specs=in_specs, out_specs=[tile] * 4, out_shape=[shape] * 4,
        compiler_params=_params("parallel"),
    )(*args)


def _adamw_layers(w, m, v, landed, name):
    n_layers, rows, cols = w.shape
    tr = _tile(rows, max(8, 131072 // cols), 8)

    def body(*refs):
        w_ref, m_ref, v_ref = refs[:3]
        part_refs = refs[3:3 + n_layers * N_DEV]
        g_ref, d_ref, nm_ref, nv_ref = refs[3 + n_layers * N_DEV:]
        layer = pl.program_id(0)
        g = None
        for l in range(n_layers):
            s = part_refs[l * N_DEV][...].astype(F32)
            for p in part_refs[l * N_DEV + 1:(l + 1) * N_DEV]:
                s = s + p[...].astype(F32)
            g = s if g is None else jnp.where(layer == l, s, g)
        g_ref[...] = g
        d_ref[...], nm_ref[...], nv_ref[...] = _adam_update(w_ref[...], m_ref[...], v_ref[...], g)

    tile = pl.BlockSpec((None, tr, cols), lambda l, i: (l, i, 0))
    in_specs, args = [tile, tile, tile], [w, m, v]
    for l, arr in enumerate(landed):
        for k in range(N_DEV):
            in_specs.append(pl.BlockSpec((None, tr, cols), lambda ll, i, l=l, k=k: (k, jnp.where(ll == l, i, 0), 0)))
            args.append(arr)
    shape = jax.ShapeDtypeStruct(w.shape, F32)
    return _pc(
        body, name=name, grid=(n_layers, rows // tr), in_specs=in_specs, out_specs=[tile] * 4, out_shape=[shape] * 4,
        compiler_params=_params("arbitrary", "arbitrary"),
    )(*args)


def _adamw_small(states, grads):
    sources, makers = grads
    n, ns = len(states), len(sources)

    def body(*refs):
        src = refs[:ns]
        ins = refs[ns:ns + 3 * n]
        outs = refs[ns + 3 * n:]
        for k in range(n):
            w_ref, m_ref, v_ref = ins[3 * k:3 * k + 3]
            g = makers[k](*src)
            outs[4 * k][...] = g
            outs[4 * k + 1][...], outs[4 * k + 2][...], outs[4 * k + 3][...] = _adam_update(
                w_ref[...], m_ref[...], v_ref[...], g)

    flat = [a for s in states for a in s]
    vmem = pl.BlockSpec(memory_space=pltpu.VMEM)
    res = _pc(
        body, name="adamw_small",
        in_specs=[vmem] * (ns + 3 * n), out_specs=[vmem] * (4 * n),
        out_shape=[jax.ShapeDtypeStruct(s[0].shape, F32) for s in states for _ in range(4)],
        compiler_params=pltpu.CompilerParams(vmem_limit_bytes=VMEM_LIMIT),
    )(*sources, *flat)
    return [res[4 * k:4 * k + 4] for k in range(n)]


def _rows128(a):
    flat = a.reshape(-1)
    pad = (-flat.shape[0]) % LANES
    if pad:
        flat = jnp.concatenate([flat, jnp.zeros((pad,), flat.dtype)])
    return flat.reshape(-1, LANES)


def _pad_rows(a, mult=8):
    pad = (-a.shape[0]) % mult
    if pad:
        a = jnp.concatenate([a, jnp.zeros((pad,) + a.shape[1:], a.dtype)], axis=0)
    return a


def kernel(x, c, ctx, c_ctx, norm_g, ada_w, ada_b, pool_w_in, pool_w_grp, pool_scale, pool_w_out, na_w_in, na_rpb, na_w_out, conv_w_in, conv_dw, conv_db, conv_w_out, final_g, loss_target, m_c_ctx, m_norm_g, m_ada_w, m_ada_b, m_pool_w_in, m_pool_w_grp, m_pool_scale, m_pool_w_out, m_na_w_in, m_na_rpb, m_na_w_out, m_conv_w_in, m_conv_dw, m_conv_db, m_conv_w_out, m_final_g, v_c_ctx, v_norm_g, v_ada_w, v_ada_b, v_pool_w_in, v_pool_w_grp, v_pool_scale, v_pool_w_out, v_na_w_in, v_na_rpb, v_na_w_out, v_conv_w_in, v_conv_dw, v_conv_db, v_conv_w_out, v_final_g):
    xi, yi, ci = _my_place()
    me = 4 * xi + 2 * yi + ci
    seq, d = x.shape[1], x.shape[2]
    n_ctx = ctx.shape[1]
    t_all = seq + n_ctx
    width = d
    depth = norm_g.shape[0]
    nb = ada_w.shape[2]
    shard = width // N_DEV
    d_rows = d // LANES
    assert seq % CHUNK == 0 and n_ctx % CHUNK == 0 and (seq // GRID_W) % ROWS_PER_STEP == 0 and seq >= WIN_ROWS * GRID_W
    tr = math.gcd(math.gcd(seq, n_ctx), 256)
    x_tiles = seq // tr

    n_pool = pool_scale.shape[0]
    n_grp = pool_w_grp.shape[1]
    grp = width // n_grp

    small_in = _pad_rows(jnp.concatenate([_rows128(c), pool_scale, conv_dw[0], conv_db], axis=0))
    got = _gather_small(small_in, "gather_inputs")
    r0 = d_rows
    c_all = got[:, :r0].reshape(N_DEV, d)
    scale_full = got[:, r0:r0 + n_pool].transpose(1, 0, 2).reshape(n_pool, width)
    r1 = r0 + n_pool
    taps_full = _pad_rows(got[:, r1:r1 + 3].transpose(1, 0, 2).reshape(3, width))
    bias_full = got[:, r1 + 3:r1 + 4].transpose(1, 0, 2).reshape(1, width)

    cond = jnp.concatenate([c_all, c_ctx[None], jnp.zeros((7, d), F32)], axis=0)
    bias_mine = lax.dynamic_slice(ada_b, (0, me * nb), (depth, nb))
    mod_mine = _mod_fwd(cond, ada_w, bias_mine)
    by_example = jnp.stack([mod_mine[:, :N_DEV].transpose(1, 0, 2),
                            jnp.broadcast_to(mod_mine[:, N_DEV][None], (N_DEV, depth, nb))], axis=2)
    mod_all = _gather_small(by_example.reshape(N_DEV, -1, LANES), "gather_mod", per_dest=True)
    mod_all = mod_all.reshape(N_DEV, depth, 2, nb).transpose(1, 2, 0, 3).reshape(depth, 2, 3, d)
    mod_all = jnp.pad(mod_all, ((0, 0), (0, 0), (0, 5), (0, 0)))
    mods = [mod_all[i] if i < 2 else mod_all[i, :1] for i in range(depth)]

    layer_weights = [[pool_w_in[0], pool_w_grp[0], pool_w_out[0]], [na_w_in[0], na_w_out[0]],
                     [conv_w_in[0], conv_w_out[0]], [pool_w_in[1], pool_w_grp[1], pool_w_out[1]]]
    slot = {(i, t): n for n, (i, t) in enumerate((i, t) for i, ws in enumerate(layer_weights) for t in range(len(ws)))}
    two_level = [(0, 0), (1, 0)]
    weights_sent = _exchange_start(
        [w.astype(BF16) for ws in layer_weights for w in ws], False, mod_all, "weights_start",
        peers=[CHIP_PEERS if key in two_level else ALL_PEERS for key in slot])
    token = weights_sent[-1]

    def landed_weight(i, t, after):
        return _exchange_wait(weights_sent, False, after, f"weights_wait{i}_{t}", which=[slot[i, t]])[0]

    def handed_on(i, after):
        half = [landed_weight(i, 0, after)]
        rest = _forward_start(half, after, f"weights_forward{i}")
        return _forward_wait(rest, rest[-1], f"weights_forward_wait{i}")[0][:, None]

    def as_in(w):
        return w[:, None]

    def as_grp(w):
        return w.transpose(1, 0, 2, 3).reshape(n_grp, grp, grp)

    def as_out(w):
        return w.reshape(width, d)

    both = [(0, seq), (seq, n_ctx)]
    latent = [(0, seq)]

    def grp_slots(g):
        return g.reshape(n_grp, N_DEV, grp // N_DEV, grp).transpose(1, 0, 2, 3).reshape(N_DEV, -1, grp).astype(BF16)

    def send_grads(i, grads):
        return _exchange_start(grads, True, jnp.zeros((8, LANES), F32), f"grads_start{i}")

    xs0 = jnp.concatenate([x[0], ctx[0]], axis=0)
    h0 = _norm_fwd(xs0, norm_g[0:1] + token[0, 0], mods[0], tr, x_tiles, "norm_fwd0")
    pool_in_w0 = handed_on(0, h0)
    pre0 = _proj_in(h0, pool_in_w0, 0, width, "proj_in0")
    pool_grp_w0 = as_grp(landed_weight(0, 1, pre0))
    z0, diff0 = _pool_fwd(pre0, pool_grp_w0, scale_full[0:1], both, "pool_fwd0")
    pool_out_w0 = as_out(landed_weight(0, 2, z0))
    yx0, xs1, h1 = _proj_out(z0, pool_out_w0, xs0, mods[0], tr, x_tiles, "proj_out0", nxt=(norm_g[1:2], mods[1]))

    na_in_w = handed_on(1, h1)
    per_part = width // na_w_in.shape[2]
    qkv1 = _proj_in(h1, na_in_w, 0, width, "proj_in1_qkv", blocks=(0, 3 * per_part), dtype=BF16)
    gpre1 = _proj_in(h1, na_in_w, 0, width, "proj_in1_gate", blocks=(3 * per_part, per_part))
    rpb_rows = jnp.pad(na_rpb[0], ((0, 0), (0, 2 * WIN_ROWS - na_rpb.shape[2]), (0, LANES - na_rpb.shape[3])))
    z1, o1 = _attn_fwd(qkv1, gpre1, rpb_rows, seq)
    na_out_w = as_out(landed_weight(1, 1, z1))
    yx1, x2, h2 = _proj_out(z1, na_out_w, xs1, mods[1], tr, x_tiles, "proj_out1", nxt=(norm_g[2:3], mods[2]))

    conv_in_w = as_in(landed_weight(2, 0, h2))
    pre2 = _proj_in(h2, conv_in_w, 0, width, "proj_in2")
    z2 = _conv_fwd(pre2, taps_full, bias_full, "conv_fwd")
    conv_out_w = as_out(landed_weight(2, 1, z2))
    yx2, x3, h3 = _proj_out(z2, conv_out_w, x2, mods[2], tr, x_tiles, "proj_out2", nxt=(norm_g[3:4], mods[3]))

    pool_in_w3 = as_in(landed_weight(3, 0, h3))
    pre3 = _proj_in(h3, pool_in_w3, 0, width, "proj_in3")
    pool_grp_w3 = as_grp(landed_weight(3, 1, pre3))
    z3, diff3 = _pool_fwd(pre3, pool_grp_w3, scale_full[1:2], latent, "pool_fwd3")
    pool_out_w3 = as_out(landed_weight(3, 2, z3))
    yx3, x4 = _proj_out(z3, pool_out_w3, x3, mods[3], tr, x_tiles, "proj_out3")

    loss_part, dx4, d_final, dyx3, gate3 = _loss_head(x4, loss_target[0], final_g[None], yx3, mods[3], tr)

    dz3 = _proj_out_dz(dyx3, pool_out_w3, "proj_out_dz3")
    g_pool_out1 = _grad_w_out(z3, dyx3, "grad_w_out3")
    dpre3, g_grp1, g_scale1 = _pool_bwd(dz3, diff3, pre3, pool_grp_w3, scale_full[1:2], latent, "pool_bwd3")
    dh3 = _proj_in_dh(dpre3, pool_in_w3, 0, "proj_in_dh3")
    g_pool_in1 = _grad_w_in(h3, dpre3, pool_w_in.shape[2], "grad_w_in3")
    sent3 = send_grads(3, [g_pool_in1, grp_slots(g_grp1), g_pool_out1.reshape(N_DEV, shard, d)])
    dx3, norm3, dyx2, gate2 = _norm_bwd(x3, dh3, dx4, norm_g[3:4] + sent3[-1][0, 0], mods[3], tr, x_tiles, "norm_bwd3",
                                        below=(yx2, mods[2]))

    dz2 = _proj_out_dz(dyx2, conv_out_w, "proj_out_dz2")
    g_conv_out = _grad_w_out(z2, dyx2, "grad_w_out2")
    dpre2, g_taps, g_cbias = _conv_bwd(dz2, pre2, taps_full, bias_full, "conv_bwd")
    dh2 = _proj_in_dh(dpre2, conv_in_w, 0, "proj_in_dh2")
    g_conv_in = _grad_w_in(h2, dpre2, conv_w_in.shape[2], "grad_w_in2")
    sent2 = send_grads(2, [g_conv_in, g_conv_out.reshape(N_DEV, shard, d)])
    dx2, norm2, dyx1, gate1 = _norm_bwd(x2, dh2, dx3, norm_g[2:3] + sent2[-1][0, 0], mods[2], tr, x_tiles, "norm_bwd2",
                                        below=(yx1, mods[1][:1]))

    dz1 = _proj_out_dz(dyx1, na_out_w, "proj_out_dz1")
    g_na_out = _grad_w_out(z1, dyx1, "grad_w_out1")
    dpre1, g_rpb = _attn_bwd(qkv1, gpre1, o1, dz1, rpb_rows, seq)
    g_rpb = g_rpb[:, :na_rpb.shape[2], :na_rpb.shape[3]]
    dh1 = _proj_in_dh(dpre1, na_in_w, 0, "proj_in_dh1")
    g_na_in = _grad_w_in(h1, dpre1, na_w_in.shape[2], "grad_w_in1")
    sent1 = send_grads(1, [g_na_in, g_na_out.reshape(N_DEV, shard, d)])
    dxs1, norm1, dyx0, gate0 = _norm_bwd(xs1, dh1, dx2, norm_g[1:2] + sent1[-1][0, 0], mods[1], tr, x_tiles, "norm_bwd1",
                                         res_tiles=x_tiles, below=(yx0, mods[0]))

    dz0 = _proj_out_dz(dyx0, pool_out_w0, "proj_out_dz0")
    g_pool_out0 = _grad_w_out(z0, dyx0, "grad_w_out0")
    sent0a = _exchange_start([g_pool_out0.reshape(N_DEV, shard, d)], True, jnp.zeros((8, LANES), F32), "grads_start0a")
    dpre0, g_grp0, g_scale0 = _pool_bwd(dz0, diff0, pre0, pool_grp_w0, scale_full[0:1], both, "pool_bwd0")
    g_pool_in0 = _grad_w_in(h0, dpre0, pool_w_in.shape[2], "grad_w_in0")
    dh0 = _proj_in_dh(dpre0, pool_in_w0, 0, "proj_in_dh0", after=sent0a[-1])
    dx0, norm0 = _norm_bwd(xs0, dh0, dxs1, norm_g[0:1], mods[0], tr, x_tiles, "norm_bwd0", out_tiles=x_tiles)
    grad_x = dx0[None]

    norms, gates = [norm0, norm1, norm2, norm3], [gate0, gate1, gate2, gate3]
    zero_d = jnp.zeros((d,), F32)
    dm_rows = [jnp.concatenate([norms[i][0, 0], norms[i][0, 1], gates[i][0, 0]]) for i in range(depth)]
    dm_rows.append(jnp.concatenate([norm0[1, 0], norm0[1, 1], gate0[1, 0]]))
    dm_rows.append(jnp.concatenate([norm1[1, 0], norm1[1, 1], zero_d]))
    dm_local = jnp.stack(dm_rows + [jnp.zeros((3 * d,), F32)] * 2)
    g_norm_part = jnp.stack([norm0[0, 2] + norm0[1, 2], norm1[0, 2] + norm1[1, 2], norm2[0, 2], norm3[0, 2]])
    pieces = [_rows128(dm_local), _rows128(g_norm_part), _rows128(d_final[0]), _pad_rows(_rows128(g_rpb)), loss_part]
    marks = np.cumsum([0] + [p.shape[0] for p in pieces])
    by_owner = [a.reshape(-1, N_DEV, shard).transpose(1, 0, 2) for a in (g_scale0, g_scale1, g_taps[:3], g_cbias[0:1])]
    by_owner = jnp.concatenate(by_owner + [jnp.zeros((N_DEV, 8 - n_pool - 4, shard), F32)], axis=1)
    small_sent = _exchange_start([jnp.concatenate(pieces, axis=0), by_owner], [False, True], jnp.zeros((8, LANES), F32),
                                 "small_grads_start")
    sent0b = _exchange_start([g_pool_in0, grp_slots(g_grp0)], True, small_sent[-1], "grads_start0b")

    def big(parts, w, m, v, name):
        shape = w.shape
        view = (-1, shape[-1])
        parts = [(parts.reshape((N_DEV,) + w.reshape(view).shape), k) for k in range(N_DEV)]
        return [r.reshape(shape) for r in _adamw(w.reshape(view), m.reshape(view), v.reshape(view), parts, name)]

    in3, grp3, out3 = _exchange_wait(sent3, True, sent0b[-1], "grads_wait3")
    in2, out2 = _exchange_wait(sent2, True, sent0b[-1], "grads_wait2")
    in1, out1 = _exchange_wait(sent1, True, sent0b[-1], "grads_wait1")
    res = {}
    res["na_w_in"] = [r[None] for r in big(in1, na_w_in[0], m_na_w_in[0], v_na_w_in[0], "adamw_na_in")]
    res["na_w_out"] = [r[None] for r in big(out1, na_w_out[0], m_na_w_out[0], v_na_w_out[0], "adamw_na_out")]
    res["conv_w_in"] = [r[None] for r in big(in2, conv_w_in[0], m_conv_w_in[0], v_conv_w_in[0], "adamw_conv_in")]
    res["conv_w_out"] = [r[None] for r in big(out2, conv_w_out[0], m_conv_w_out[0], v_conv_w_out[0], "adamw_conv_out")]

    done = [res[n][0] for n in ("na_w_in", "na_w_out", "conv_w_in", "conv_w_out")]
    small_out, owned = _exchange_wait(small_sent, [False, True], done, "small_grads_wait")
    loss = jnp.sum(small_out[:, marks[4], 0])
    dm_all = small_out[:, :marks[1]].reshape(N_DEV, 8, 3 * d).transpose(1, 0, 2)
    dm_mine = lax.dynamic_slice(dm_all, (0, 0, me * nb), (8, N_DEV, nb))
    g_ada_w, g_ada_b, cctx_part, dsilu_cond = _mod_bwd(cond, ada_w, dm_all, dm_mine)
    cctx_all = _gather_small(_rows128(cctx_part[0]), "gather_cctx")

    def summed(ref, lo, hi):
        g = ref[0, lo:hi, :]
        for k in range(1, N_DEV):
            g = g + ref[k, lo:hi, :]
        return g

    makers = [
        lambda so, ow, cc, ab, ds: summed(cc, 0, d_rows) * ds[...],
        lambda so, ow, cc, ab, ds: summed(so, marks[1], marks[2]),
        lambda so, ow, cc, ab, ds: ab[...],
        lambda so, ow, cc, ab, ds: summed(so, marks[2], marks[3]),
        lambda so, ow, cc, ab, ds: summed(so, marks[3], marks[4]),
        lambda so, ow, cc, ab, ds: summed(ow, 0, n_pool),
        lambda so, ow, cc, ab, ds: summed(ow, n_pool, n_pool + 3),
        lambda so, ow, cc, ab, ds: summed(ow, n_pool + 3, n_pool + 4),
    ]
    rpb_rows128 = lambda a: _pad_rows(_rows128(a))
    views = [_rows128] * 4 + [rpb_rows128] + [lambda a: a.reshape(-1, LANES)] * 3
    small = [(c_ctx, m_c_ctx, v_c_ctx), (norm_g, m_norm_g, v_norm_g), (ada_b, m_ada_b, v_ada_b),
             (final_g, m_final_g, v_final_g), (na_rpb, m_na_rpb, v_na_rpb), (pool_scale, m_pool_scale, v_pool_scale),
             (conv_dw, m_conv_dw, v_conv_dw), (conv_db, m_conv_db, v_conv_db)]
    states = [tuple(view(a) for a in triple) for view, triple in zip(views, small)]
    sources = (small_out, owned, cctx_all, _rows128(g_ada_b), _rows128(dsilu_cond[8]))
    small_res = _adamw_small(states, (sources, makers))
    names = ["c_ctx", "norm_g", "ada_b", "final_g", "na_rpb", "pool_scale", "conv_dw", "conv_db"]
    for name, (w, _, _), outs4 in zip(names, small, small_res):
        res[name] = [r.reshape(-1)[:w.size].reshape(w.shape) for r in outs4]

    res["ada_w"] = [r.reshape(ada_w.shape) for r in _adamw(
        ada_w.reshape(-1, nb), m_ada_w.reshape(-1, nb), v_ada_w.reshape(-1, nb), [g_ada_w.reshape(-1, nb)], "adamw_ada_w")]

    out0, = _exchange_wait(sent0a, True, small_res[0][0], "grads_wait0a")
    in0, grp0 = _exchange_wait(sent0b, True, small_res[0][0], "grads_wait0b")
    def both_layers(first, second, w, m, v, name):
        view = (w.shape[0], -1, w.shape[-1])
        landed = [first.reshape((N_DEV,) + w.reshape(view).shape[1:]), second.reshape((N_DEV,) + w.reshape(view).shape[1:])]
        return [r.reshape(w.shape) for r in _adamw_layers(w.reshape(view), m.reshape(view), v.reshape(view), landed, name)]

    res["pool_w_in"] = both_layers(in0, in3, pool_w_in, m_pool_w_in, v_pool_w_in, "adamw_pool_in")
    res["pool_w_grp"] = both_layers(grp0, grp3, pool_w_grp, m_pool_w_grp, v_pool_w_grp, "adamw_pool_grp")
    res["pool_w_out"] = both_layers(out0, out3, pool_w_out, m_pool_w_out, v_pool_w_out, "adamw_pool_out")

    order = ["c_ctx", "norm_g", "ada_w", "ada_b", "pool_w_in", "pool_w_grp", "pool_scale", "pool_w_out", "na_w_in",
             "na_rpb", "na_w_out", "conv_w_in", "conv_dw", "conv_db", "conv_w_out", "final_g"]
    outs = [loss, grad_x]
    for j in range(4):
        outs += [res[n][j] for n in order]
    return tuple(outs)
```

```python
import functools
import math

import numpy as np
import jax
import jax.numpy as jnp
from jax import lax
from jax.experimental import pallas as pl
from jax.experimental.pallas import tpu as pltpu

F32 = jnp.float32
BF16 = jnp.bfloat16
N_DEV = 8
LANES = 128
RMS_EPS = 1e-6
GRID_W = 64
WIN_ROWS = 8
WIN_COLS = 16
HEAD_DIM = 64
POOL_WINDOWS = (2, 4, 8, 16)
HALO = 8
CHUNK = 256
MASKED = -1e30
ADAM_LR = 0.001
ADAM_B1 = 0.9
ADAM_B2 = 0.999
ADAM_EPS = 1e-08
ADAM_WD = 0.01
ADAM_STEP = 10
VMEM_LIMIT = 56 * 1024 * 1024
ADAM_TILE_ELEMS = 256 * 1024
MESH = pl.DeviceIdType.MESH
ANY = pl.BlockSpec(memory_space=pl.ANY)
HBM = pl.BlockSpec(memory_space=pltpu.HBM)
SEM = pl.BlockSpec(memory_space=pltpu.SEMAPHORE)
EFFECT = pltpu.SideEffectType.DATAFLOW_SIDE_EFFECTING


def _pc(body, *, name, **kw):
    return pl.pallas_call(body, name=name, **kw)


def _params(*sem):
    return pltpu.CompilerParams(dimension_semantics=sem if sem else None, vmem_limit_bytes=VMEM_LIMIT)


def _dot(a, b, ca=1, cb=0, precision=None):
    return lax.dot_general(a, b, (((ca,), (cb,)), ((), ())), preferred_element_type=F32, precision=precision)


def _tile(n, pref, unit=LANES):
    best = None
    for t in range(unit, min(n, pref) + 1, unit):
        if n % t == 0:
            best = t
    return best if best is not None else n


def _sigmoid(x):
    return 1.0 / (1.0 + jnp.exp(-x))


def _silu(x):
    return x * _sigmoid(x)


def _dsilu(x):
    s = _sigmoid(x)
    return s * (1.0 + x * (1.0 - s))


def _my_place():
    return lax.axis_index("x"), lax.axis_index("y"), lax.axis_index("c")


def _flip(v, f):
    return 1 - v if f else v


def _gather_small(block, name, per_dest=False):
    rows, cols = block.shape[-2:]

    def body(x_ref, out_ref, send_sems, recv_sems):
        x, y, c = _my_place()
        me = 4 * x + 2 * y + c
        out_ref[me] = x_ref[me] if per_dest else x_ref[...]
        copies = []
        for k in range(1, N_DEV):
            peer = (_flip(x, k & 4), _flip(y, k & 2), _flip(c, k & 1))
            dest = 4 * peer[0] + 2 * peer[1] + peer[2]
            cp = pltpu.make_async_remote_copy(
                src_ref=x_ref.at[dest] if per_dest else x_ref, dst_ref=out_ref.at[me],
                send_sem=send_sems.at[k - 1], recv_sem=recv_sems.at[k - 1], device_id=peer, device_id_type=MESH)
            cp.start()
            copies.append(cp)
        for cp in copies:
            cp.wait()

    return _pc(
        body, name=name,
        out_shape=jax.ShapeDtypeStruct((N_DEV, rows, cols), block.dtype),
        in_specs=[pl.BlockSpec(memory_space=pltpu.VMEM)],
        out_specs=pl.BlockSpec(memory_space=pltpu.VMEM),
        scratch_shapes=[pltpu.SemaphoreType.DMA((N_DEV - 1,)), pltpu.SemaphoreType.DMA((N_DEV - 1,))],
    )(block)


ALL_PEERS = tuple(range(N_DEV))
CHIP_PEERS = (0, 1, 2, 4, 6)
OTHER_CHIPS = (2, 4, 6)


def _flag(per_dest, t):
    return per_dest[t] if isinstance(per_dest, (list, tuple)) else per_dest


def _peer(k):
    x, y, c = _my_place()
    peer = (_flip(x, k & 4), _flip(y, k & 2), _flip(c, k & 1))
    return peer, 4 * peer[0] + 2 * peer[1] + peer[2]


def _peer_lists(peers, nt):
    return list(peers) if isinstance(peers, list) else [peers] * nt


def _exchange_copies(srcs, lands, send_sems, recv_sems, per_dest, peers=ALL_PEERS):
    x, y, c = _my_place()
    me = 4 * x + 2 * y + c
    copies = []
    for t, (src, land, ks) in enumerate(zip(srcs, lands, _peer_lists(peers, len(srcs)))):
        for k in ks:
            peer, dest = _peer(k)
            s = len(copies)
            copies.append(pltpu.make_async_remote_copy(
                src_ref=src.at[dest] if _flag(per_dest, t) else src, dst_ref=land.at[me],
                send_sem=send_sems[s], recv_sem=recv_sems[s], device_id=peer, device_id_type=MESH))
    return copies


def _forward_copies(lands, send_sems, recv_sems):
    sibling, _ = _peer(1)
    copies = []
    for t, land in enumerate(lands):
        for n, k in enumerate(OTHER_CHIPS):
            _, slot = _peer(k)
            s = t * len(OTHER_CHIPS) + n
            copies.append(pltpu.make_async_remote_copy(
                src_ref=land.at[slot], dst_ref=land.at[slot], send_sem=send_sems[s], recv_sem=recv_sems[s],
                device_id=sibling, device_id_type=MESH))
    return copies


def _forward_start(lands, after, name):
    nt = len(lands)
    ns = nt * len(OTHER_CHIPS)

    def body(*refs):
        ins, outs = refs[:nt + 1], refs[nt + 1:]
        for cp in _forward_copies(ins[:nt], outs[:ns], outs[ns:2 * ns]):
            cp.start()
        outs[-1][...] = jnp.zeros_like(outs[-1])

    res = _pc(
        body, name=name,
        out_shape=(*[pltpu.SemaphoreType.DMA(())] * (2 * ns), *[pltpu.HBM(a.shape, a.dtype) for a in lands],
                   jax.ShapeDtypeStruct((8, LANES), F32)),
        in_specs=[HBM] * nt + [ANY],
        out_specs=(*[SEM] * (2 * ns), *[HBM] * nt, pl.BlockSpec(memory_space=pltpu.VMEM)),
        input_output_aliases={i: 2 * ns + i for i in range(nt)},
        compiler_params=pltpu.CompilerParams(has_side_effects=EFFECT),
    )(*lands, after)
    return list(res[:ns]), list(res[ns:2 * ns]), list(res[2 * ns:2 * ns + nt]), res[-1]


def _forward_wait(state, after, name):
    send_sems, recv_sems, lands, _ = state
    nt, ns = len(lands), len(send_sems)

    def body(*refs):
        sems = refs[nt:nt + 2 * ns]
        for cp in _forward_copies(refs[:nt], sems[:ns], sems[ns:]):
            cp.wait_send()
            cp.wait_recv()

    res = _pc(
        body, name=name,
        out_shape=tuple(pltpu.HBM(a.shape, a.dtype) for a in lands),
        in_specs=[HBM] * nt + [SEM] * (2 * ns) + [ANY],
        out_specs=tuple([HBM] * nt),
        input_output_aliases={i: i for i in range(nt)},
        compiler_params=pltpu.CompilerParams(has_side_effects=EFFECT),
    )(*lands, *send_sems, *recv_sems, after)
    return list(res)


def _exchange_start(srcs, per_dest, after, name, peers=ALL_PEERS):
    nt = len(srcs)
    peers = _peer_lists(peers, nt)
    ns = sum(len(ks) for ks in peers)
    lands = [lax.empty((N_DEV,) + (s.shape[1:] if _flag(per_dest, t) else s.shape), s.dtype) for t, s in enumerate(srcs)]

    def body(*refs):
        ins, outs = refs[:2 * nt + 1], refs[2 * nt + 1:]
        for cp in _exchange_copies(ins[:nt], ins[nt:2 * nt], outs[:ns], outs[ns:2 * ns], per_dest, peers):
            cp.start()
        outs[-1][...] = jnp.zeros_like(outs[-1])

    hbm = [pltpu.with_memory_space_constraint(a, pltpu.HBM) for a in list(srcs) + lands]
    res = _pc(
        body, name=name,
        out_shape=(*[pltpu.SemaphoreType.DMA(())] * (2 * ns), *[pltpu.HBM(a.shape, a.dtype) for a in hbm],
                   jax.ShapeDtypeStruct((8, LANES), F32)),
        in_specs=[HBM] * (2 * nt) + [ANY],
        out_specs=(*[SEM] * (2 * ns), *[HBM] * (2 * nt), pl.BlockSpec(memory_space=pltpu.VMEM)),
        input_output_aliases={i: 2 * ns + i for i in range(2 * nt)},
        compiler_params=pltpu.CompilerParams(has_side_effects=EFFECT),
    )(*hbm, after)
    sems, rest = res[:2 * ns], res[2 * ns:]
    return list(sems[:ns]), list(sems[ns:]), list(rest[:nt]), list(rest[nt:2 * nt]), peers, rest[-1]


def _exchange_wait(state, per_dest, after, name, which=None):
    send_sems, recv_sems, srcs, lands, peers, _ = state
    which = list(range(len(srcs))) if which is None else which
    per_dest = [_flag(per_dest, t) for t in which]
    after = list(after) if isinstance(after, (list, tuple)) else [after]
    first = np.cumsum([0] + [len(ks) for ks in peers])
    pick = [first[t] + n for t in which for n in range(len(peers[t]))]
    peers = [peers[t] for t in which]
    send_sems, recv_sems = [send_sems[s] for s in pick], [recv_sems[s] for s in pick]
    srcs, lands = [srcs[t] for t in which], [lands[t] for t in which]
    nt = len(srcs)
    ns = len(send_sems)

    def body(*refs):
        sems = refs[2 * nt:2 * nt + 2 * ns]
        for cp in _exchange_copies(refs[:nt], refs[nt:2 * nt], sems[:ns], sems[ns:], per_dest, peers):
            cp.wait_send()
            cp.wait_recv()

    thru = list(srcs) + list(lands)
    res = _pc(
        body, name=name,
        out_shape=tuple(pltpu.HBM(a.shape, a.dtype) for a in thru),
        in_specs=[HBM] * (2 * nt) + [SEM] * (2 * ns) + [ANY] * len(after),
        out_specs=tuple([HBM] * (2 * nt)),
        input_output_aliases={i: i for i in range(2 * nt)},
        compiler_params=pltpu.CompilerParams(has_side_effects=EFFECT),
    )(*thru, *send_sems, *recv_sems, *after)
    return list(res[nt:])


def _mod_fwd(cond, ada_w, bias):
    depth, d, nb = ada_w.shape

    def body(c_ref, w_ref, b_ref, o_ref):
        s = _silu(c_ref[...]).astype(BF16)
        o_ref[...] = _dot(s, w_ref[...].astype(BF16)) + b_ref[...]

    return _pc(
        body, name="mod_fwd", grid=(depth,),
        in_specs=[pl.BlockSpec((16, d), lambda i: (0, 0)), pl.BlockSpec((None, d, nb), lambda i: (i, 0, 0)),
                  pl.BlockSpec((None, 1, nb), lambda i: (i, 0, 0))],
        out_specs=pl.BlockSpec((None, 16, nb), lambda i: (i, 0, 0)),
        out_shape=jax.ShapeDtypeStruct((depth, 16, nb), F32),
        compiler_params=_params("parallel"),
    )(cond, ada_w, bias.reshape(depth, 1, nb))


def _mod_bwd(cond, ada_w, dm_all, dm_mine):
    depth, d, nb = ada_w.shape
    d3 = dm_all.shape[-1]

    def body(c_ref, w_ref, all_ref, call_ref, mine_ref, cmine_ref, gw_ref, gb_ref, part_ref, ds_ref):
        i = pl.program_id(0)
        cond_v = c_ref[...]
        s = _silu(cond_v).astype(BF16)
        has_ctx = jnp.where(i < 2, 1.0, 0.0)
        tot_all = jnp.sum(call_ref[...], axis=0, keepdims=True) * has_ctx
        tot_mine = jnp.broadcast_to(jnp.sum(cmine_ref[...], axis=0, keepdims=True) * has_ctx, (8, nb)).astype(BF16)
        gb_ref[...] = jnp.sum(all_ref[...], axis=0, keepdims=True) + tot_all
        gw_ref[...] = _dot(s[0:8], mine_ref[...].astype(BF16), 0, 0) + _dot(s[8:16], tot_mine, 0, 0)
        part = _dot(tot_mine, w_ref[...].astype(BF16), 1, 1)

        @pl.when(i == 0)
        def _():
            part_ref[...] = jnp.zeros_like(part_ref)
            ds_ref[...] = _dsilu(cond_v)

        part_ref[...] += part

    def rows(width, which):
        return pl.BlockSpec((None, N_DEV, width), which)

    layer = lambda i: (i, 0, 0)
    ctx_layer = lambda i: (jnp.minimum(i, 1) + 4, 0, 0)
    return _pc(
        body, name="mod_bwd", grid=(depth,),
        in_specs=[pl.BlockSpec((16, d), lambda i: (0, 0)), pl.BlockSpec((None, d, nb), layer),
                  rows(d3, layer), rows(d3, ctx_layer), rows(nb, layer), rows(nb, ctx_layer)],
        out_specs=[pl.BlockSpec((None, d, nb), layer), pl.BlockSpec((None, 1, d3), layer),
                   pl.BlockSpec((8, d), lambda i: (0, 0)), pl.BlockSpec((16, d), lambda i: (0, 0))],
        out_shape=[jax.ShapeDtypeStruct((depth, d, nb), F32), jax.ShapeDtypeStruct((depth, 1, d3), F32),
                   jax.ShapeDtypeStruct((8, d), F32), jax.ShapeDtypeStruct((16, d), F32)],
        compiler_params=_params("arbitrary"),
    )(cond, ada_w, dm_all, dm_all, dm_mine, dm_mine)


def _norm_fwd(xs, g, mod, tr, seg_tiles, name):
    t, d = xs.shape

    def body(x_ref, g_ref, mod_ref, h_ref):
        x = x_ref[...]
        r = lax.rsqrt(jnp.mean(x * x, axis=-1, keepdims=True) + RMS_EPS)
        y = (x * r) * g_ref[...]
        h_ref[...] = (y * (1.0 + mod_ref[1:2, :]) + mod_ref[0:1, :]).astype(BF16)

    return _pc(
        body, name=name, grid=(t // tr,),
        in_specs=[pl.BlockSpec((tr, d), lambda i: (i, 0)), pl.BlockSpec((1, d), lambda i: (0, 0)),
                  pl.BlockSpec((None, 8, d), lambda i: (i // seg_tiles, 0, 0))],
        out_specs=pl.BlockSpec((tr, d), lambda i: (i, 0)),
        out_shape=jax.ShapeDtypeStruct((t, d), BF16),
        compiler_params=_params("parallel"),
    )(xs, g, mod)


def _resid_grad(dx, i, seg_tiles, yx_ref, gate_ref, dyx_ref, gsum_ref):
    dyx_ref[...] = (dx * gate_ref[2:3, :]).astype(BF16)

    @pl.when(i % seg_tiles == 0)
    def _():
        gsum_ref[...] = jnp.zeros_like(gsum_ref)

    gsum_ref[0:1, :] += jnp.sum(dx * yx_ref[...], axis=0, keepdims=True)


def _norm_bwd(xs, dh, dres, g, mod, tr, seg_tiles, name, res_tiles=None, out_tiles=None, below=None):
    t, d = xs.shape
    n_tiles = t // tr
    res_tiles = n_tiles if res_tiles is None else res_tiles
    out_tiles = n_tiles if out_tiles is None else out_tiles

    def body(x_ref, dh_ref, dres_ref, g_ref, mod_ref, *rest):
        i = pl.program_id(0)
        x = x_ref[...]
        r = lax.rsqrt(jnp.mean(x * x, axis=-1, keepdims=True) + RMS_EPS)
        xn = x * r
        dhv = dh_ref[...]
        gain = g_ref[...]
        one_scale = 1.0 + mod_ref[1:2, :]
        dxn = dhv * (gain * one_scale)
        dx = r * (dxn - xn * jnp.mean(dxn * xn, axis=-1, keepdims=True))
        if res_tiles == n_tiles:
            dx = dx + dres_ref[...]
        else:
            dx = dx + jnp.where(i < res_tiles, dres_ref[...], 0.0)
        if below is None:
            dx_ref, sum_ref = rest
        else:
            yx_ref, gate_ref, dx_ref, sum_ref, dyx_ref, gsum_ref = rest
            _resid_grad(dx, i, seg_tiles, yx_ref, gate_ref, dyx_ref, gsum_ref)
        if out_tiles == n_tiles:
            dx_ref[...] = dx
        else:
            @pl.when(i < out_tiles)
            def _():
                dx_ref[...] = dx

        @pl.when(i % seg_tiles == 0)
        def _():
            sum_ref[...] = jnp.zeros_like(sum_ref)

        sum_ref[0:1, :] += jnp.sum(dhv, axis=0, keepdims=True)
        sum_ref[1:2, :] += jnp.sum(dhv * (xn * gain), axis=0, keepdims=True)
        sum_ref[2:3, :] += jnp.sum(dhv * one_scale * xn, axis=0, keepdims=True)

    row = pl.BlockSpec((tr, d), lambda i: (i, 0))
    seg = pl.BlockSpec((None, 8, d), lambda i: (i // seg_tiles, 0, 0))
    in_specs = [row, row, pl.BlockSpec((tr, d), lambda i: (jnp.minimum(i, res_tiles - 1), 0)),
                pl.BlockSpec((1, d), lambda i: (0, 0)), seg]
    out_specs = [pl.BlockSpec((tr, d), lambda i: (jnp.minimum(i, out_tiles - 1), 0)), seg]
    out_shape = [jax.ShapeDtypeStruct((out_tiles * tr, d), F32), jax.ShapeDtypeStruct((mod.shape[0], 8, d), F32)]
    args = [xs, dh, dres, g, mod]
    if below is not None:
        in_specs += [row, seg]
        out_specs += [row, seg]
        out_shape += [jax.ShapeDtypeStruct((t, d), BF16), jax.ShapeDtypeStruct((below[1].shape[0], 8, d), F32)]
        args += list(below)
    return _pc(
        body, name=name, grid=(n_tiles,), in_specs=in_specs, out_specs=out_specs, out_shape=out_shape,
        compiler_params=_params("arbitrary"),
    )(*args)


def _loss_head(xs, target, g, yx, mod, tr):
    t, d = xs.shape

    def body(x_ref, t_ref, g_ref, yx_ref, gate_ref, loss_ref, dx_ref, dg_ref, dyx_ref, gsum_ref):
        i = pl.program_id(0)
        x = x_ref[...]
        r = lax.rsqrt(jnp.mean(x * x, axis=-1, keepdims=True) + RMS_EPS)
        xn = x * r
        gain = g_ref[...]
        err = xn * gain - t_ref[...]
        dy = err * (1.0 / d)
        dxn = dy * gain
        dx = r * (dxn - xn * jnp.mean(dxn * xn, axis=-1, keepdims=True))
        dx_ref[...] = dx
        _resid_grad(dx, i, t // tr, yx_ref, gate_ref, dyx_ref, gsum_ref)

        @pl.when(i == 0)
        def _():
            loss_ref[...] = jnp.zeros_like(loss_ref)
            dg_ref[...] = jnp.zeros_like(dg_ref)

        loss_ref[...] += 0.5 * jnp.sum(jnp.mean(err * err, axis=-1, keepdims=True))
        dg_ref[0:1, :] += jnp.sum(dy * xn, axis=0, keepdims=True)

    row = pl.BlockSpec((tr, d), lambda i: (i, 0))
    seg = pl.BlockSpec((None, 8, d), lambda i: (0, 0, 0))
    return _pc(
        body, name="loss_head", grid=(t // tr,),
        in_specs=[row, row, pl.BlockSpec((1, d), lambda i: (0, 0)), row, seg],
        out_specs=[pl.BlockSpec((8, LANES), lambda i: (0, 0)), row, pl.BlockSpec((8, d), lambda i: (0, 0)), row, seg],
        out_shape=[jax.ShapeDtypeStruct((8, LANES), F32), jax.ShapeDtypeStruct((t, d), F32),
                   jax.ShapeDtypeStruct((8, d), F32), jax.ShapeDtypeStruct((t, d), BF16),
                   jax.ShapeDtypeStruct((1, 8, d), F32)],
        compiler_params=_params("arbitrary"),
    )(xs, target, g, yx, mod)


def _proj_in(h, w, layer, width, name, blocks=None, dtype=F32):
    t, d = h.shape
    n8 = w.shape[-1]
    first, count = blocks if blocks is not None else (0, N_DEV)
    per_part = width // n8
    assert first % per_part == 0 and count % per_part == 0
    tm = _tile(t, 1152)

    def body(a_ref, b_ref, o_ref):
        a = a_ref[...]
        for s in range(per_part):
            o_ref[:, s * n8:(s + 1) * n8] = _dot(a, b_ref[s]).astype(dtype)

    return _pc(
        body, name=name, grid=(t // tm, count // per_part),
        in_specs=[pl.BlockSpec((tm, d), lambda i, j: (i, 0)),
                  pl.BlockSpec((per_part, None, d, n8), lambda i, j: (first // per_part + j, layer, 0, 0))],
        out_specs=pl.BlockSpec((None, tm, width), lambda i, j: (j, i, 0)),
        out_shape=jax.ShapeDtypeStruct((count // per_part, t, width), dtype),
        compiler_params=_params("parallel", "parallel"),
    )(h, w)


def _proj_out(z, w, res, mod, tm, seg_tiles, name, nxt=None):
    t, k = z.shape
    d = w.shape[1]

    def body(z_ref, w_ref, res_ref, mod_ref, *rest):
        yx = _dot(z_ref[...], w_ref[...])
        x = res_ref[...] + mod_ref[2:3, :] * yx
        if nxt is None:
            yx_ref, x_ref = rest
        else:
            g_ref, nmod_ref, yx_ref, x_ref, h_ref = rest
            r = lax.rsqrt(jnp.mean(x * x, axis=-1, keepdims=True) + RMS_EPS)
            h_ref[...] = (((x * r) * g_ref[...]) * (1.0 + nmod_ref[1:2, :]) + nmod_ref[0:1, :]).astype(BF16)
        yx_ref[...] = yx
        x_ref[...] = x

    tile = pl.BlockSpec((tm, d), lambda i: (i, 0))
    seg = pl.BlockSpec((None, 8, d), lambda i: (i // seg_tiles, 0, 0))
    in_specs = [pl.BlockSpec((tm, k), lambda i: (i, 0)), pl.BlockSpec((k, d), lambda i: (0, 0)), tile, seg]
    out_specs = [tile, tile]
    out_shape = [jax.ShapeDtypeStruct((t, d), F32), jax.ShapeDtypeStruct((t, d), F32)]
    args = [z, w, res, mod]
    if nxt is not None:
        in_specs += [pl.BlockSpec((1, d), lambda i: (0, 0)), seg]
        out_specs.append(tile)
        out_shape.append(jax.ShapeDtypeStruct((t, d), BF16))
        args += list(nxt)
    return _pc(
        body, name=name, grid=(t // tm,), in_specs=in_specs, out_specs=out_specs, out_shape=out_shape,
        compiler_params=_params("parallel"),
    )(*args)


def _proj_out_dz(dyx, w, name):
    t, d = dyx.shape
    width = w.shape[0]
    tm, tn = _tile(t, 1024), _tile(width, 512)

    def body(a_ref, w_ref, o_ref):
        o_ref[...] = _dot(a_ref[...], w_ref[...], 1, 1)

    return _pc(
        body, name=name, grid=(t // tm, width // tn),
        in_specs=[pl.BlockSpec((tm, d), lambda i, j: (i, 0)), pl.BlockSpec((tn, d), lambda i, j: (j, 0))],
        out_specs=pl.BlockSpec((tm, tn), lambda i, j: (i, j)),
        out_shape=jax.ShapeDtypeStruct((t, width), F32),
        compiler_params=_params("parallel", "parallel"),
    )(dyx, w)


def _proj_in_dh(dpre, w, layer, name, after=None):
    parts, t, width = dpre.shape
    d, n8 = w.shape[-2:]
    per_part = width // n8
    tm, tn = _tile(t, 768), _tile(d, 512)

    def body(a_ref, w_ref, *rest):
        o_ref = rest[-1]
        acc = None
        for p in range(parts):
            for s in range(per_part):
                term = _dot(a_ref[p, :, s * n8:(s + 1) * n8], w_ref[p * per_part + s], 1, 1)
                acc = term if acc is None else acc + term
        o_ref[...] = acc

    extra = [] if after is None else [after]
    return _pc(
        body, name=name, grid=(t // tm, d // tn),
        in_specs=[pl.BlockSpec((parts, tm, width), lambda i, j: (0, i, 0)),
                  pl.BlockSpec((N_DEV, None, tn, n8), lambda i, j: (0, layer, j, 0))] + [ANY] * len(extra),
        out_specs=pl.BlockSpec((tm, tn), lambda i, j: (i, j)),
        out_shape=jax.ShapeDtypeStruct((t, d), F32),
        compiler_params=_params("parallel", "parallel"),
    )(dpre, w, *extra)


def _transposed(a_ref):
    return a_ref[...].T


def _grad_w_in(h, dpre, n8, name):
    t, d = h.shape
    parts, _, width = dpre.shape
    per_part = width // n8
    tm, tk = _tile(d, 512), _tile(t, 1152)
    nk = t // tk

    def body(a_ref, b_ref, o_ref, acc_ref):
        k = pl.program_id(1)

        @pl.when(k == 0)
        def _():
            acc_ref[...] = jnp.zeros_like(acc_ref)

        at = _transposed(a_ref)
        for p in range(parts):
            r = _dot(at, b_ref[p])
            for s in range(per_part):
                acc_ref[p * per_part + s] += r[:, s * n8:(s + 1) * n8]

        @pl.when(k == nk - 1)
        def _():
            o_ref[...] = acc_ref[...].astype(BF16)

    return _pc(
        body, name=name, grid=(d // tm, nk),
        in_specs=[pl.BlockSpec((tk, tm), lambda i, k: (k, i)), pl.BlockSpec((parts, tk, width), lambda i, k: (0, k, 0))],
        out_specs=pl.BlockSpec((parts * per_part, tm, n8), lambda i, k: (0, i, 0)),
        out_shape=jax.ShapeDtypeStruct((parts * per_part, d, n8), BF16),
        scratch_shapes=[pltpu.VMEM((parts * per_part, tm, n8), F32)],
        compiler_params=_params("parallel", "arbitrary"),
    )(h, dpre)


def _grad_w_out(z, dyx, name):
    width = z.shape[1]
    t, d = dyx.shape
    tm, tk = _tile(width, 512), _tile(t, 1152)
    nk = t // tk

    def body(a_ref, b_ref, o_ref, acc_ref):
        k = pl.program_id(1)

        @pl.when(k == 0)
        def _():
            acc_ref[...] = jnp.zeros_like(acc_ref)

        acc_ref[...] += _dot(_transposed(a_ref), b_ref[...])

        @pl.when(k == nk - 1)
        def _():
            o_ref[...] = acc_ref[...].astype(BF16)

    return _pc(
        body, name=name, grid=(width // tm, nk),
        in_specs=[pl.BlockSpec((tk, tm), lambda i, k: (k, i)), pl.BlockSpec((tk, d), lambda i, k: (k, 0))],
        out_specs=pl.BlockSpec((tm, d), lambda i, k: (i, 0)),
        out_shape=jax.ShapeDtypeStruct((width, d), BF16),
        scratch_shapes=[pltpu.VMEM((tm, d), F32)],
        compiler_params=_params("parallel", "arbitrary"),
    )(z, dyx)


def _shift(v, k):
    n = v.shape[0]
    return pltpu.roll(v, k % n, 0)


def _window_sum(v, win):
    s = v + _shift(v, 1)
    step = 1
    while 2 * step < win:
        s = _shift(s, step) + _shift(s, -step)
        step *= 2
    return s


def _window_count(base, seg_len, win, shape):
    t = base + lax.broadcasted_iota(jnp.int32, shape, 0)
    hi = jnp.minimum(t + win // 2, seg_len)
    lo = jnp.maximum(t - win // 2, 0)
    return (hi - lo).astype(F32)


def _pad_offsets(segs):
    return [HALO * (s + 1) + st for s, (st, _) in enumerate(segs)]


def _for_chunks(segs, fn):
    offs = _pad_offsets(segs)
    for s, (st, ln) in enumerate(segs):
        def step(ci, carry, s=s, st=st, ln=ln):
            fn(s, st, ln, offs[s], pl.multiple_of(ci * CHUNK, CHUNK))
            return carry
        lax.fori_loop(0, ln // CHUNK, step, 0)


def _pool_fwd(pre, w_grp, scale, segs, name):
    _, t, width = pre.shape
    grp = width // len(POOL_WINDOWS)
    padded = t + HALO * (len(segs) + 1)

    def group(win, pre_ref, w_ref, sc_ref, z_ref, diff_ref, pad_ref):
        pad_ref[...] = jnp.zeros_like(pad_ref)

        def fill(s, st, ln, off, b):
            pad_ref[pl.ds(off + b, CHUNK), :] = pre_ref[0, pl.ds(st + b, CHUNK), :]

        _for_chunks(segs, fill)

        def mix(s, st, ln, off, b):
            ext = pad_ref[pl.ds(off - HALO + b, CHUNK + 2 * HALO), :]
            total = _window_sum(ext, win)[HALO:HALO + CHUNK]
            u = pre_ref[0, pl.ds(st + b, CHUNK), :]
            diff = (total / _window_count(b, ln, win, u.shape) - u).astype(BF16)
            mixed = _dot(diff, w_ref[...])
            gate = _silu(pre_ref[1, pl.ds(st + b, CHUNK), :])
            z_ref[pl.ds(st + b, CHUNK), :] = (mixed * sc_ref[...] * gate).astype(BF16)
            diff_ref[pl.ds(st + b, CHUNK), :] = diff

        _for_chunks(segs, mix)

    def body(pre_ref, w_ref, sc_ref, z_ref, diff_ref, pad_ref):
        gi = pl.program_id(0)
        for widx, win in enumerate(POOL_WINDOWS):
            @pl.when(gi == widx)
            def _(win=win):
                group(win, pre_ref, w_ref, sc_ref, z_ref, diff_ref, pad_ref)

    col = pl.BlockSpec((t, grp), lambda g: (0, g))
    return _pc(
        body, name=name, grid=(len(POOL_WINDOWS),),
        in_specs=[pl.BlockSpec((2, t, grp), lambda g: (0, 0, g)), pl.BlockSpec((None, grp, grp), lambda g: (g, 0, 0)),
                  pl.BlockSpec((1, grp), lambda g: (0, g))],
        out_specs=[col, col],
        out_shape=[jax.ShapeDtypeStruct((t, width), BF16), jax.ShapeDtypeStruct((t, width), BF16)],
        scratch_shapes=[pltpu.VMEM((padded, grp), F32)],
        compiler_params=_params("parallel"),
    )(pre, w_grp, scale)


def _pool_bwd(dz, diff, pre, w_grp, scale, segs, name):
    _, t, width = pre.shape
    grp = width // len(POOL_WINDOWS)
    padded = t + HALO * (len(segs) + 1)

    def group(win, dz_ref, diff_ref, pre_ref, w_ref, sc_ref, dpre_ref, dw_ref, dsc_ref, pad_ref, dd_ref):
        pad_ref[...] = jnp.zeros_like(pad_ref)
        dw_ref[...] = jnp.zeros_like(dw_ref)
        dsc_ref[...] = jnp.zeros_like(dsc_ref)

        def first(s, st, ln, off, b):
            rows = pl.ds(st + b, CHUNK)
            diff_v = diff_ref[rows, :]
            mixed = _dot(diff_v, w_ref[...])
            g = pre_ref[1, rows, :]
            sg = _silu(g)
            dzv = dz_ref[rows, :]
            dmixed = (dzv * sc_ref[...] * sg).astype(BF16)
            dsc_ref[...] += jnp.sum(dzv * mixed * sg, axis=0, keepdims=True)
            dpre_ref[1, rows, :] = (dzv * mixed * sc_ref[...] * _dsilu(g)).astype(BF16)
            ddiff = _dot(dmixed, w_ref[...], 1, 1)
            dw_ref[...] += _dot(diff_v, dmixed, 0, 0)
            dd_ref[rows, :] = ddiff
            pad_ref[pl.ds(off + b, CHUNK), :] = ddiff / _window_count(b, ln, win, ddiff.shape)

        _for_chunks(segs, first)

        def second(s, st, ln, off, b):
            rows = pl.ds(st + b, CHUNK)
            ext = pad_ref[pl.ds(off - HALO + b, CHUNK + 2 * HALO), :]
            total = _shift(_window_sum(ext, win), -1)[HALO:HALO + CHUNK]
            dpre_ref[0, rows, :] = (total - dd_ref[rows, :]).astype(BF16)

        _for_chunks(segs, second)

    def body(dz_ref, diff_ref, pre_ref, w_ref, sc_ref, dpre_ref, dw_ref, dsc_ref, pad_ref, dd_ref):
        gi = pl.program_id(0)
        for widx, win in enumerate(POOL_WINDOWS):
            @pl.when(gi == widx)
            def _(win=win):
                group(win, dz_ref, diff_ref, pre_ref, w_ref, sc_ref, dpre_ref, dw_ref, dsc_ref, pad_ref, dd_ref)

    col = pl.BlockSpec((t, grp), lambda g: (0, g))
    both = pl.BlockSpec((2, t, grp), lambda g: (0, 0, g))
    wspec = pl.BlockSpec((None, grp, grp), lambda g: (g, 0, 0))
    sspec = pl.BlockSpec((1, grp), lambda g: (0, g))
    return _pc(
        body, name=name, grid=(len(POOL_WINDOWS),),
        in_specs=[col, col, both, wspec, sspec],
        out_specs=[both, wspec, sspec],
        out_shape=[jax.ShapeDtypeStruct((2, t, width), BF16), jax.ShapeDtypeStruct((len(POOL_WINDOWS), grp, grp), F32),
                   jax.ShapeDtypeStruct((1, width), F32)],
        scratch_shapes=[pltpu.VMEM((padded, grp), F32), pltpu.VMEM((t, grp), F32)],
        compiler_params=_params("parallel"),
    )(dz, diff, pre, w_grp, scale)


def _conv_fwd(pre, dw, db, name):
    _, t, width = pre.shape
    cb = LANES
    segs = [(0, t)]

    def body(pre_ref, dw_ref, db_ref, z_ref, pad_ref):
        pad_ref[...] = jnp.zeros_like(pad_ref)

        def fill(s, st, ln, off, b):
            rows = pl.ds(b, CHUNK)
            pad_ref[pl.ds(off + b, CHUNK), :] = pre_ref[1, rows, :] * pre_ref[2, rows, :]

        _for_chunks(segs, fill)

        def mix(s, st, ln, off, b):
            rows = pl.ds(b, CHUNK)
            ext = pad_ref[pl.ds(off - HALO + b, CHUNK + 2 * HALO), :]
            conv = (dw_ref[0:1, :] * _shift(ext, 1) + dw_ref[1:2, :] * ext + dw_ref[2:3, :] * _shift(ext, -1))
            conv = conv[HALO:HALO + CHUNK] + db_ref[...]
            y = pre_ref[0, rows, :] * conv
            z_ref[rows, :] = (y * _silu(pre_ref[3, rows, :])).astype(BF16)

        _for_chunks(segs, mix)

    return _pc(
        body, name=name, grid=(width // cb,),
        in_specs=[pl.BlockSpec((4, t, cb), lambda j: (0, 0, j)), pl.BlockSpec((8, cb), lambda j: (0, j)),
                  pl.BlockSpec((1, cb), lambda j: (0, j))],
        out_specs=pl.BlockSpec((t, cb), lambda j: (0, j)),
        out_shape=jax.ShapeDtypeStruct((t, width), BF16),
        scratch_shapes=[pltpu.VMEM((t + 2 * HALO, cb), F32)],
        compiler_params=_params("parallel"),
    )(pre, dw, db)


def _conv_bwd(dz, pre, dw, db, name):
    _, t, width = pre.shape
    cb = LANES
    segs = [(0, t)]

    def body(dz_ref, pre_ref, dw_ref, db_ref, dpre_ref, ddw_ref, ddb_ref, pad_a, pad_c):
        pad_a[...] = jnp.zeros_like(pad_a)
        pad_c[...] = jnp.zeros_like(pad_c)
        ddw_ref[...] = jnp.zeros_like(ddw_ref)
        ddb_ref[...] = jnp.zeros_like(ddb_ref)

        def fill(s, st, ln, off, b):
            rows = pl.ds(b, CHUNK)
            pad_a[pl.ds(off + b, CHUNK), :] = pre_ref[1, rows, :] * pre_ref[2, rows, :]

        _for_chunks(segs, fill)

        def first(s, st, ln, off, b):
            rows = pl.ds(b, CHUNK)
            ext = pad_a[pl.ds(off - HALO + b, CHUNK + 2 * HALO), :]
            prev, nxt = _shift(ext, 1)[HALO:HALO + CHUNK], _shift(ext, -1)[HALO:HALO + CHUNK]
            here = ext[HALO:HALO + CHUNK]
            conv = dw_ref[0:1, :] * prev + dw_ref[1:2, :] * here + dw_ref[2:3, :] * nxt + db_ref[...]
            bg, g = pre_ref[0, rows, :], pre_ref[3, rows, :]
            dzv = dz_ref[rows, :]
            dy = dzv * _silu(g)
            dpre_ref[3, rows, :] = (dzv * (bg * conv) * _dsilu(g)).astype(BF16)
            dpre_ref[0, rows, :] = (dy * conv).astype(BF16)
            dconv = dy * bg
            pad_c[pl.ds(off + b, CHUNK), :] = dconv
            ddw_ref[0:1, :] += jnp.sum(dconv * prev, axis=0, keepdims=True)
            ddw_ref[1:2, :] += jnp.sum(dconv * here, axis=0, keepdims=True)
            ddw_ref[2:3, :] += jnp.sum(dconv * nxt, axis=0, keepdims=True)
            ddb_ref[0:1, :] += jnp.sum(dconv, axis=0, keepdims=True)

        _for_chunks(segs, first)

        def second(s, st, ln, off, b):
            rows = pl.ds(b, CHUNK)
            ext = pad_c[pl.ds(off - HALO + b, CHUNK + 2 * HALO), :]
            da = (dw_ref[0:1, :] * _shift(ext, -1) + dw_ref[1:2, :] * ext + dw_ref[2:3, :] * _shift(ext, 1))
            da = da[HALO:HALO + CHUNK]
            dpre_ref[1, rows, :] = (da * pre_ref[2, rows, :]).astype(BF16)
            dpre_ref[2, rows, :] = (da * pre_ref[1, rows, :]).astype(BF16)

        _for_chunks(segs, second)

    quad = pl.BlockSpec((4, t, cb), lambda j: (0, 0, j))
    rows8 = pl.BlockSpec((8, cb), lambda j: (0, j))
    return _pc(
        body, name=name, grid=(width // cb,),
        in_specs=[pl.BlockSpec((t, cb), lambda j: (0, j)), quad, rows8, pl.BlockSpec((1, cb), lambda j: (0, j))],
        out_specs=[quad, rows8, rows8],
        out_shape=[jax.ShapeDtypeStruct((4, t, width), BF16), jax.ShapeDtypeStruct((8, width), F32),
                   jax.ShapeDtypeStruct((8, width), F32)],
        scratch_shapes=[pltpu.VMEM((t + 2 * HALO, cb), F32), pltpu.VMEM((t + 2 * HALO, cb), F32)],
        compiler_params=_params("parallel"),
    )(dz, pre, dw, db)


PAIR_TILES = 2 * WIN_ROWS - 2


def _pair_geometry():
    lane = lax.broadcasted_iota(jnp.int32, (GRID_W, LANES), 1)
    qcol = lax.broadcasted_iota(jnp.int32, (GRID_W, LANES), 0)
    low = lane < GRID_W
    kcol = jnp.where(low, lane, lane - GRID_W)
    start = jnp.clip(qcol - WIN_COLS // 2, 0, GRID_W - WIN_COLS)
    inside = (kcol >= start) & (kcol < start + WIN_COLS)
    return low, inside


def _bias_tiles(rpb_ref, rows_ref, tiles_ref, inside):
    for h in range(2):
        rows = rpb_ref[h]
        rows_ref[h] = (pltpu.roll(rows, LANES - (WIN_COLS - 1), 1)
                       + pltpu.roll(pltpu.roll(rows, GRID_W - (WIN_COLS - 1), 1), 2 * WIN_ROWS - 1, 0))
        for t in range(PAIR_TILES):
            both = jnp.broadcast_to(rows_ref[h, t:t + 1, :], (GRID_W, LANES))
            tiles_ref[h, t] = jnp.where(inside, pltpu.roll(both, 0, 1, stride=1, stride_axis=0), MASKED)


def _bias_tiles_grad(dtiles_ref, drpb_ref):
    n = PAIR_TILES * GRID_W
    qcol = lax.broadcasted_iota(jnp.int32, (n, LANES), 0) & (GRID_W - 1)
    lane = lax.broadcasted_iota(jnp.int32, (1, LANES), 1)
    zero = jnp.zeros((1, LANES), F32)
    for h in range(2):
        v = pltpu.roll(dtiles_ref[h].reshape(n, LANES), WIN_COLS - 1, 1)
        for bit in range(6):
            v = jnp.where((qcol >> bit) & 1 == 1, pltpu.roll(v, LANES - (1 << bit), 1), v)
        sums = [jnp.sum(v[t * GRID_W:(t + 1) * GRID_W], axis=0, keepdims=True) for t in range(PAIR_TILES)]
        for r in range(2 * WIN_ROWS):
            here = sums[r] if r < PAIR_TILES else zero
            prev = pltpu.roll(sums[r - 1], GRID_W, 1) if 1 <= r <= PAIR_TILES else zero
            drpb_ref[h, r:r + 1, :] = jnp.where(lane < 2 * WIN_COLS - 1, here + prev, 0.0)


def _attn_rows(r, n_rows):
    first = jnp.clip(r - WIN_ROWS // 2, 0, n_rows - WIN_ROWS)
    return first, first - r + WIN_ROWS - 1


def _softmax(s_loc, s_ctx):
    m = jnp.maximum(jnp.max(s_loc, axis=-1, keepdims=True), jnp.max(s_ctx, axis=-1, keepdims=True))
    e_loc, e_ctx = jnp.exp(s_loc - m), jnp.exp(s_ctx - m)
    inv = 1.0 / (jnp.sum(e_loc, axis=-1, keepdims=True) + jnp.sum(e_ctx, axis=-1, keepdims=True))
    return e_loc * inv, e_ctx * inv


def _pair_bias(tiles_ref, j):
    return jnp.concatenate(
        [jnp.concatenate([tiles_ref[h, j + 2 * m] for m in range(WIN_ROWS // 2)], axis=1) for h in range(2)], axis=0)


ROWS_PER_STEP = 4


def _by_head(tile, low):
    zero = jnp.zeros_like(tile)
    return jnp.concatenate([jnp.where(low, tile, zero), jnp.where(low, zero, tile)], axis=0)


def _merge_heads(stacked, low):
    return jnp.where(low, stacked[:GRID_W], stacked[GRID_W:])


def _attn_items(step, n_rows, q_ref, low):
    items = []
    for u in range(ROWS_PER_STEP):
        r = step * ROWS_PER_STEP + u
        first, j = _attn_rows(r, n_rows)
        rows = pl.ds(pl.multiple_of(r * GRID_W, GRID_W), GRID_W)
        keys = pl.ds(pl.multiple_of(first * GRID_W, GRID_W), WIN_ROWS * GRID_W)
        q = (q_ref[rows, :].astype(F32) * HEAD_DIM ** -0.5).astype(BF16)
        items.append((rows, keys, j, _by_head(q, low)))
    return items


def _attn_fwd(qkv, gate, rpb, seq):
    _, t, width = qkv.shape
    n_rows = seq // GRID_W
    n_ctx = t - seq
    blk = WIN_ROWS * GRID_W

    def body(q_ref, k_ref, v_ref, g_ref, rpb_ref, z_ref, o_ref, rows_ref, tiles_ref):
        low, inside = _pair_geometry()
        _bias_tiles(rpb_ref, rows_ref, tiles_ref, inside)
        ctx = pl.ds(seq, n_ctx)

        def step(i, carry):
            items = _attn_items(i, n_rows, q_ref, low)
            k_ctx, v_ctx = k_ref[ctx, :], v_ref[ctx, :]
            scores = [(_dot(q, k_ref[keys, :], 1, 1) + _pair_bias(tiles_ref, j), _dot(q, k_ctx, 1, 1))
                      for _, keys, j, q in items]
            probs = [_softmax(s_loc, s_ctx) for s_loc, s_ctx in scores]
            outs = [_dot(p_loc.astype(BF16), v_ref[keys, :]) + _dot(p_ctx.astype(BF16), v_ctx)
                    for (_, keys, _, _), (p_loc, p_ctx) in zip(items, probs)]
            for (rows, _, _, _), out in zip(items, outs):
                o = _merge_heads(out, low)
                o_ref[rows, :] = o
                z_ref[rows, :] = (o * _silu(g_ref[rows, :])).astype(BF16)
            return carry

        lax.fori_loop(0, n_rows // ROWS_PER_STEP, step, 0)

    def part(p):
        return pl.BlockSpec((None, t, LANES), lambda h: (p, 0, h))

    out = pl.BlockSpec((seq, LANES), lambda h: (0, h))
    return _pc(
        body, name="attn_fwd", grid=(width // LANES,),
        in_specs=[part(0), part(1), part(2), part(0), pl.BlockSpec((2, 2 * WIN_ROWS, LANES), lambda h: (h, 0, 0))],
        out_specs=[out, out],
        out_shape=[jax.ShapeDtypeStruct((seq, width), BF16), jax.ShapeDtypeStruct((seq, width), F32)],
        scratch_shapes=[pltpu.VMEM((2, 2 * WIN_ROWS, LANES), F32), pltpu.VMEM((2, PAIR_TILES, GRID_W, LANES), F32)],
        compiler_params=_params("parallel"),
    )(qkv, qkv, qkv, gate, rpb)


def _attn_bwd(qkv, gate, o, dz, rpb, seq):
    _, t, width = qkv.shape
    n_rows = seq // GRID_W
    n_ctx = t - seq
    blk = WIN_ROWS * GRID_W
    heads = 2 * width // LANES

    def body(q_ref, k_ref, v_ref, g_ref, o_ref, dz_ref, rpb_ref, dpre_ref, drpb_ref,
             rows_ref, tiles_ref, dtiles_ref, dk_ref, dv_ref):
        low, inside = _pair_geometry()
        _bias_tiles(rpb_ref, rows_ref, tiles_ref, inside)
        dtiles_ref[...] = jnp.zeros_like(dtiles_ref)
        dk_ref[...] = jnp.zeros_like(dk_ref)
        dv_ref[...] = jnp.zeros_like(dv_ref)
        ctx = pl.ds(seq, n_ctx)

        def step(i, carry):
            items = _attn_items(i, n_rows, q_ref, low)
            k_ctx, v_ctx = k_ref[ctx, :], v_ref[ctx, :]
            d_outs = []
            for rows, _, _, _ in items:
                g = g_ref[rows, :]
                dzv = dz_ref[rows, :]
                dpre_ref[3, rows, :] = (dzv * o_ref[rows, :] * _dsilu(g)).astype(BF16)
                d_outs.append(_by_head((dzv * _silu(g)).astype(BF16), low))
            scores = [(_dot(q, k_ref[keys, :], 1, 1) + _pair_bias(tiles_ref, j), _dot(q, k_ctx, 1, 1))
                      for _, keys, j, q in items]
            dprobs = [(_dot(d_o, v_ref[keys, :], 1, 1), _dot(d_o, v_ctx, 1, 1))
                      for (_, keys, _, _), d_o in zip(items, d_outs)]
            probs = [_softmax(s_loc, s_ctx) for s_loc, s_ctx in scores]
            dscores = []
            for (p_loc, p_ctx), (dp_loc, dp_ctx) in zip(probs, dprobs):
                delta = (jnp.sum(p_loc * dp_loc, axis=-1, keepdims=True)
                         + jnp.sum(p_ctx * dp_ctx, axis=-1, keepdims=True))
                dscores.append((p_loc * (dp_loc - delta), p_ctx * (dp_ctx - delta)))
            dqs = [_dot(ds_loc.astype(BF16), k_ref[keys, :]) + _dot(ds_ctx.astype(BF16), k_ctx)
                   for (_, keys, _, _), (ds_loc, ds_ctx) in zip(items, dscores)]
            for (rows, _, _, _), dq in zip(items, dqs):
                dpre_ref[0, rows, :] = (_merge_heads(dq, low) * HEAD_DIM ** -0.5).astype(BF16)
            for (_, keys, j, q), d_o, (p_loc, p_ctx), (ds_loc, ds_ctx) in zip(items, d_outs, probs, dscores):
                dk_ref[keys, :] += _dot(ds_loc.astype(BF16), q, 0, 0)
                dk_ref[ctx, :] += _dot(ds_ctx.astype(BF16), q, 0, 0)
                dv_ref[keys, :] += _dot(p_loc.astype(BF16), d_o, 0, 0)
                dv_ref[ctx, :] += _dot(p_ctx.astype(BF16), d_o, 0, 0)
                for h in range(2):
                    for m in range(WIN_ROWS // 2):
                        dtiles_ref[h, j + 2 * m] += ds_loc[h * GRID_W:(h + 1) * GRID_W, m * LANES:(m + 1) * LANES]
            return carry

        lax.fori_loop(0, n_rows // ROWS_PER_STEP, step, 0)
        dpre_ref[1] = dk_ref[...].astype(BF16)
        dpre_ref[2] = dv_ref[...].astype(BF16)
        dpre_ref[0, ctx, :] = jnp.zeros((n_ctx, LANES), BF16)
        dpre_ref[3, ctx, :] = jnp.zeros((n_ctx, LANES), BF16)
        _bias_tiles_grad(dtiles_ref, drpb_ref)

    def part(p):
        return pl.BlockSpec((None, t, LANES), lambda h: (p, 0, h))

    lat = pl.BlockSpec((seq, LANES), lambda h: (0, h))
    rspec = pl.BlockSpec((2, 2 * WIN_ROWS, LANES), lambda h: (h, 0, 0))
    tiles = pltpu.VMEM((2, PAIR_TILES, GRID_W, LANES), F32)
    return _pc(
        body, name="attn_bwd", grid=(width // LANES,),
        in_specs=[part(0), part(1), part(2), part(0), lat, lat, rspec],
        out_specs=[pl.BlockSpec((4, t, LANES), lambda h: (0, 0, h)), rspec],
        out_shape=[jax.ShapeDtypeStruct((4, t, width), BF16), jax.ShapeDtypeStruct((heads, 2 * WIN_ROWS, LANES), F32)],
        scratch_shapes=[pltpu.VMEM((2, 2 * WIN_ROWS, LANES), F32), tiles, tiles,
                        pltpu.VMEM((t, LANES), F32), pltpu.VMEM((t, LANES), F32)],
        compiler_params=_params("parallel"),
    )(qkv, qkv, qkv, gate, o, dz, rpb)


def _adam_update(w, m, v, g):
    m2 = ADAM_B1 * m + (1.0 - ADAM_B1) * g
    v2 = ADAM_B2 * v + (1.0 - ADAM_B2) * (g * g)
    m_hat = m2 / (1.0 - ADAM_B1 ** ADAM_STEP)
    v_hat = v2 / (1.0 - ADAM_B2 ** ADAM_STEP)
    return -ADAM_LR * (m_hat / (jnp.sqrt(v_hat) + ADAM_EPS) + ADAM_WD * w), m2, v2


def _adamw(w, m, v, parts, name):
    rows, cols = w.shape
    tr = _tile(rows, max(8, ADAM_TILE_ELEMS // cols), 8)
    n_parts = len(parts)

    def body(*refs):
        w_ref, m_ref, v_ref = refs[:3]
        part_refs = refs[3:3 + n_parts]
        g_ref, d_ref, nm_ref, nv_ref = refs[3 + n_parts:]
        g = part_refs[0][...].astype(F32)
        for p in part_refs[1:]:
            g = g + p[...].astype(F32)
        g_ref[...] = g
        d_ref[...], nm_ref[...], nv_ref[...] = _adam_update(w_ref[...], m_ref[...], v_ref[...], g)

    tile = pl.BlockSpec((tr, cols), lambda i: (i, 0))
    in_specs, args = [tile, tile, tile], [w, m, v]
    for p in parts:
        if isinstance(p, tuple):
            arr, k = p
            in_specs.append(pl.BlockSpec((None, tr, cols), lambda i, k=k: (k, i, 0)))
            args.append(arr)
        else:
            in_specs.append(tile)
            args.append(p)
    shape = jax.ShapeDtypeStruct((rows, cols), F32)
    return _pc(
        body, name=name, grid=(rows // tr,), in_specs=in_specs, out_specs=[tile] * 4, out_shape=[shape] * 4,
        compiler_params=_params("parallel"),
    )(*args)


def _adamw_layers(w, m, v, landed, name):
    n_layers, rows, cols = w.shape
    tr = _tile(rows, max(8, ADAM_TILE_ELEMS // cols), 8)

    def body(*refs):
        w_ref, m_ref, v_ref = refs[:3]
        part_refs = refs[3:3 + n_layers * N_DEV]
        g_ref, d_ref, nm_ref, nv_ref = refs[3 + n_layers * N_DEV:]
        layer = pl.program_id(0)
        g = None
        for l in range(n_layers):
            s = part_refs[l * N_DEV][...].astype(F32)
            for p in part_refs[l * N_DEV + 1:(l + 1) * N_DEV]:
                s = s + p[...].astype(F32)
            g = s if g is None else jnp.where(layer == l, s, g)
        g_ref[...] = g
        d_ref[...], nm_ref[...], nv_ref[...] = _adam_update(w_ref[...], m_ref[...], v_ref[...], g)

    tile = pl.BlockSpec((None, tr, cols), lambda l, i: (l, i, 0))
    in_specs, args = [tile, tile, tile], [w, m, v]
    for l, arr in enumerate(landed):
        for k in range(N_DEV):
            in_specs.append(pl.BlockSpec((None, tr, cols), lambda ll, i, l=l, k=k: (k, jnp.where(ll == l, i, 0), 0)))
            args.append(arr)
    shape = jax.ShapeDtypeStruct(w.shape, F32)
    return _pc(
        body, name=name, grid=(n_layers, rows // tr), in_specs=in_specs, out_specs=[tile] * 4, out_shape=[shape] * 4,
        compiler_params=_params("arbitrary", "arbitrary"),
    )(*args)


def _adamw_small(states, grads):
    sources, makers = grads
    n, ns = len(states), len(sources)

    def body(*refs):
        src = refs[:ns]
        ins = refs[ns:ns + 3 * n]
        outs = refs[ns + 3 * n:]
        for k in range(n):
            w_ref, m_ref, v_ref = ins[3 * k:3 * k + 3]
            g = makers[k](*src)
            outs[4 * k][...] = g
            outs[4 * k + 1][...], outs[4 * k + 2][...], outs[4 * k + 3][...] = _adam_update(
                w_ref[...], m_ref[...], v_ref[...], g)

    flat = [a for s in states for a in s]
    vmem = pl.BlockSpec(memory_space=pltpu.VMEM)
    res = _pc(
        body, name="adamw_small",
        in_specs=[vmem] * (ns + 3 * n), out_specs=[vmem] * (4 * n),
        out_shape=[jax.ShapeDtypeStruct(s[0].shape, F32) for s in states for _ in range(4)],
        compiler_params=pltpu.CompilerParams(vmem_limit_bytes=VMEM_LIMIT),
    )(*sources, *flat)
    return [res[4 * k:4 * k + 4] for k in range(n)]


def _rows128(a):
    flat = a.reshape(-1)
    pad = (-flat.shape[0]) % LANES
    if pad:
        flat = jnp.concatenate([flat, jnp.zeros((pad,), flat.dtype)])
    return flat.reshape(-1, LANES)


def _pad_rows(a, mult=8):
    pad = (-a.shape[0]) % mult
    if pad:
        a = jnp.concatenate([a, jnp.zeros((pad,) + a.shape[1:], a.dtype)], axis=0)
    return a


def kernel(x, c, ctx, c_ctx, norm_g, ada_w, ada_b, pool_w_in, pool_w_grp, pool_scale, pool_w_out, na_w_in, na_rpb, na_w_out, conv_w_in, conv_dw, conv_db, conv_w_out, final_g, loss_target, m_c_ctx, m_norm_g, m_ada_w, m_ada_b, m_pool_w_in, m_pool_w_grp, m_pool_scale, m_pool_w_out, m_na_w_in, m_na_rpb, m_na_w_out, m_conv_w_in, m_conv_dw, m_conv_db, m_conv_w_out, m_final_g, v_c_ctx, v_norm_g, v_ada_w, v_ada_b, v_pool_w_in, v_pool_w_grp, v_pool_scale, v_pool_w_out, v_na_w_in, v_na_rpb, v_na_w_out, v_conv_w_in, v_conv_dw, v_conv_db, v_conv_w_out, v_final_g):
    xi, yi, ci = _my_place()
    me = 4 * xi + 2 * yi + ci
    seq, d = x.shape[1], x.shape[2]
    n_ctx = ctx.shape[1]
    t_all = seq + n_ctx
    width = d
    depth = norm_g.shape[0]
    nb = ada_w.shape[2]
    shard = width // N_DEV
    d_rows = d // LANES
    assert seq % CHUNK == 0 and n_ctx % CHUNK == 0 and (seq // GRID_W) % ROWS_PER_STEP == 0 and seq >= WIN_ROWS * GRID_W
    tr = math.gcd(math.gcd(seq, n_ctx), 256)
    x_tiles = seq // tr

    n_pool = pool_scale.shape[0]
    n_grp = pool_w_grp.shape[1]
    grp = width // n_grp

    small_in = _pad_rows(jnp.concatenate([_rows128(c), pool_scale, conv_dw[0], conv_db], axis=0))
    got = _gather_small(small_in, "gather_inputs")
    r0 = d_rows
    c_all = got[:, :r0].reshape(N_DEV, d)
    scale_full = got[:, r0:r0 + n_pool].transpose(1, 0, 2).reshape(n_pool, width)
    r1 = r0 + n_pool
    taps_full = _pad_rows(got[:, r1:r1 + 3].transpose(1, 0, 2).reshape(3, width))
    bias_full = got[:, r1 + 3:r1 + 4].transpose(1, 0, 2).reshape(1, width)

    cond = jnp.concatenate([c_all, c_ctx[None], jnp.zeros((7, d), F32)], axis=0)
    bias_mine = lax.dynamic_slice(ada_b, (0, me * nb), (depth, nb))
    mod_mine = _mod_fwd(cond, ada_w, bias_mine)
    by_example = jnp.stack([mod_mine[:, :N_DEV].transpose(1, 0, 2),
                            jnp.broadcast_to(mod_mine[:, N_DEV][None], (N_DEV, depth, nb))], axis=2)
    mod_all = _gather_small(by_example.reshape(N_DEV, -1, LANES), "gather_mod", per_dest=True)
    mod_all = mod_all.reshape(N_DEV, depth, 2, nb).transpose(1, 2, 0, 3).reshape(depth, 2, 3, d)
    mod_all = jnp.pad(mod_all, ((0, 0), (0, 0), (0, 5), (0, 0)))
    mods = [mod_all[i] if i < 2 else mod_all[i, :1] for i in range(depth)]

    layer_weights = [[pool_w_in[0], pool_w_grp[0], pool_w_out[0]], [na_w_in[0], na_w_out[0]],
                     [conv_w_in[0], conv_w_out[0]], [pool_w_in[1], pool_w_grp[1], pool_w_out[1]]]
    slot = {(i, t): n for n, (i, t) in enumerate((i, t) for i, ws in enumerate(layer_weights) for t in range(len(ws)))}
    two_level = [(0, 0), (1, 0)]
    weights_sent = _exchange_start(
        [w.astype(BF16) for ws in layer_weights for w in ws], False, mod_all, "weights_start",
        peers=[CHIP_PEERS if key in two_level else ALL_PEERS for key in slot])
    token = weights_sent[-1]

    def landed_weight(i, t, after):
        return _exchange_wait(weights_sent, False, after, f"weights_wait{i}_{t}", which=[slot[i, t]])[0]

    def handed_on(i, after):
        half = [landed_weight(i, 0, after)]
        rest = _forward_start(half, after, f"weights_forward{i}")
        return _forward_wait(rest, rest[-1], f"weights_forward_wait{i}")[0][:, None]

    def as_in(w):
        return w[:, None]

    def as_grp(w):
        return w.transpose(1, 0, 2, 3).reshape(n_grp, grp, grp)

    def as_out(w):
        return w.reshape(width, d)

    both = [(0, seq), (seq, n_ctx)]
    latent = [(0, seq)]

    def grp_slots(g):
        return g.reshape(n_grp, N_DEV, grp // N_DEV, grp).transpose(1, 0, 2, 3).reshape(N_DEV, -1, grp).astype(BF16)

    def send_grads(i, grads):
        return _exchange_start(grads, True, jnp.zeros((8, LANES), F32), f"grads_start{i}")

    xs0 = jnp.concatenate([x[0], ctx[0]], axis=0)
    h0 = _norm_fwd(xs0, norm_g[0:1] + token[0, 0], mods[0], tr, x_tiles, "norm_fwd0")
    pool_in_w0 = handed_on(0, h0)
    pre0 = _proj_in(h0, pool_in_w0, 0, width, "proj_in0")
    pool_grp_w0 = as_grp(landed_weight(0, 1, pre0))
    z0, diff0 = _pool_fwd(pre0, pool_grp_w0, scale_full[0:1], both, "pool_fwd0")
    pool_out_w0 = as_out(landed_weight(0, 2, z0))
    yx0, xs1, h1 = _proj_out(z0, pool_out_w0, xs0, mods[0], tr, x_tiles, "proj_out0", nxt=(norm_g[1:2], mods[1]))

    na_in_w = handed_on(1, h1)
    per_part = width // na_w_in.shape[2]
    qkv1 = _proj_in(h1, na_in_w, 0, width, "proj_in1_qkv", blocks=(0, 3 * per_part), dtype=BF16)
    gpre1 = _proj_in(h1, na_in_w, 0, width, "proj_in1_gate", blocks=(3 * per_part, per_part))
    rpb_rows = jnp.pad(na_rpb[0], ((0, 0), (0, 2 * WIN_ROWS - na_rpb.shape[2]), (0, LANES - na_rpb.shape[3])))
    z1, o1 = _attn_fwd(qkv1, gpre1, rpb_rows, seq)
    na_out_w = as_out(landed_weight(1, 1, z1))
    yx1, x2, h2 = _proj_out(z1, na_out_w, xs1, mods[1], tr, x_tiles, "proj_out1", nxt=(norm_g[2:3], mods[2]))

    conv_in_w = as_in(landed_weight(2, 0, h2))
    pre2 = _proj_in(h2, conv_in_w, 0, width, "proj_in2")
    z2 = _conv_fwd(pre2, taps_full, bias_full, "conv_fwd")
    conv_out_w = as_out(landed_weight(2, 1, z2))
    yx2, x3, h3 = _proj_out(z2, conv_out_w, x2, mods[2], tr, x_tiles, "proj_out2", nxt=(norm_g[3:4], mods[3]))

    pool_in_w3 = as_in(landed_weight(3, 0, h3))
    pre3 = _proj_in(h3, pool_in_w3, 0, width, "proj_in3")
    pool_grp_w3 = as_grp(landed_weight(3, 1, pre3))
    z3, diff3 = _pool_fwd(pre3, pool_grp_w3, scale_full[1:2], latent, "pool_fwd3")
    pool_out_w3 = as_out(landed_weight(3, 2, z3))
    yx3, x4 = _proj_out(z3, pool_out_w3, x3, mods[3], tr, x_tiles, "proj_out3")

    loss_part, dx4, d_final, dyx3, gate3 = _loss_head(x4, loss_target[0], final_g[None], yx3, mods[3], tr)

    dz3 = _proj_out_dz(dyx3, pool_out_w3, "proj_out_dz3")
    g_pool_out1 = _grad_w_out(z3, dyx3, "grad_w_out3")
    dpre3, g_grp1, g_scale1 = _pool_bwd(dz3, diff3, pre3, pool_grp_w3, scale_full[1:2], latent, "pool_bwd3")
    dh3 = _proj_in_dh(dpre3, pool_in_w3, 0, "proj_in_dh3")
    g_pool_in1 = _grad_w_in(h3, dpre3, pool_w_in.shape[2], "grad_w_in3")
    sent3 = send_grads(3, [g_pool_in1, grp_slots(g_grp1), g_pool_out1.reshape(N_DEV, shard, d)])
    dx3, norm3, dyx2, gate2 = _norm_bwd(x3, dh3, dx4, norm_g[3:4] + sent3[-1][0, 0], mods[3], tr, x_tiles, "norm_bwd3",
                                        below=(yx2, mods[2]))

    dz2 = _proj_out_dz(dyx2, conv_out_w, "proj_out_dz2")
    g_conv_out = _grad_w_out(z2, dyx2, "grad_w_out2")
    dpre2, g_taps, g_cbias = _conv_bwd(dz2, pre2, taps_full, bias_full, "conv_bwd")
    dh2 = _proj_in_dh(dpre2, conv_in_w, 0, "proj_in_dh2")
    g_conv_in = _grad_w_in(h2, dpre2, conv_w_in.shape[2], "grad_w_in2")
    sent2 = send_grads(2, [g_conv_in, g_conv_out.reshape(N_DEV, shard, d)])
    dx2, norm2, dyx1, gate1 = _norm_bwd(x2, dh2, dx3, norm_g[2:3] + sent2[-1][0, 0], mods[2], tr, x_tiles, "norm_bwd2",
                                        below=(yx1, mods[1][:1]))

    dz1 = _proj_out_dz(dyx1, na_out_w, "proj_out_dz1")
    g_na_out = _grad_w_out(z1, dyx1, "grad_w_out1")
    dpre1, g_rpb = _attn_bwd(qkv1, gpre1, o1, dz1, rpb_rows, seq)
    g_rpb = g_rpb[:, :na_rpb.shape[2], :na_rpb.shape[3]]
    dh1 = _proj_in_dh(dpre1, na_in_w, 0, "proj_in_dh1")
    g_na_in = _grad_w_in(h1, dpre1, na_w_in.shape[2], "grad_w_in1")
    sent1 = send_grads(1, [g_na_in, g_na_out.reshape(N_DEV, shard, d)])
    dxs1, norm1, dyx0, gate0 = _norm_bwd(xs1, dh1, dx2, norm_g[1:2] + sent1[-1][0, 0], mods[1], tr, x_tiles, "norm_bwd1",
                                         res_tiles=x_tiles, below=(yx0, mods[0]))

    dz0 = _proj_out_dz(dyx0, pool_out_w0, "proj_out_dz0")
    g_pool_out0 = _grad_w_out(z0, dyx0, "grad_w_out0")
    sent0a = _exchange_start([g_pool_out0.reshape(N_DEV, shard, d)], True, jnp.zeros((8, LANES), F32), "grads_start0a")
    dpre0, g_grp0, g_scale0 = _pool_bwd(dz0, diff0, pre0, pool_grp_w0, scale_full[0:1], both, "pool_bwd0")
    g_pool_in0 = _grad_w_in(h0, dpre0, pool_w_in.shape[2], "grad_w_in0")
    dh0 = _proj_in_dh(dpre0, pool_in_w0, 0, "proj_in_dh0", after=sent0a[-1])
    dx0, norm0 = _norm_bwd(xs0, dh0, dxs1, norm_g[0:1], mods[0], tr, x_tiles, "norm_bwd0", out_tiles=x_tiles)
    grad_x = dx0[None]

    norms, gates = [norm0, norm1, norm2, norm3], [gate0, gate1, gate2, gate3]
    zero_d = jnp.zeros((d,), F32)
    dm_rows = [jnp.concatenate([norms[i][0, 0], norms[i][0, 1], gates[i][0, 0]]) for i in range(depth)]
    dm_rows.append(jnp.concatenate([norm0[1, 0], norm0[1, 1], gate0[1, 0]]))
    dm_rows.append(jnp.concatenate([norm1[1, 0], norm1[1, 1], zero_d]))
    dm_local = jnp.stack(dm_rows + [jnp.zeros((3 * d,), F32)] * 2)
    g_norm_part = jnp.stack([norm0[0, 2] + norm0[1, 2], norm1[0, 2] + norm1[1, 2], norm2[0, 2], norm3[0, 2]])
    pieces = [_rows128(dm_local), _rows128(g_norm_part), _rows128(d_final[0]), _pad_rows(_rows128(g_rpb)), loss_part]
    marks = np.cumsum([0] + [p.shape[0] for p in pieces])
    by_owner = [a.reshape(-1, N_DEV, shard).transpose(1, 0, 2) for a in (g_scale0, g_scale1, g_taps[:3], g_cbias[0:1])]
    by_owner = jnp.concatenate(by_owner + [jnp.zeros((N_DEV, 8 - n_pool - 4, shard), F32)], axis=1)
    small_sent = _exchange_start([jnp.concatenate(pieces, axis=0), by_owner], [False, True], jnp.zeros((8, LANES), F32),
                                 "small_grads_start")
    sent0b = _exchange_start([g_pool_in0, grp_slots(g_grp0)], True, small_sent[-1], "grads_start0b")

    def big(parts, w, m, v, name):
        shape = w.shape
        view = (-1, shape[-1])
        parts = [(parts.reshape((N_DEV,) + w.reshape(view).shape), k) for k in range(N_DEV)]
        return [r.reshape(shape) for r in _adamw(w.reshape(view), m.reshape(view), v.reshape(view), parts, name)]

    in3, grp3, out3 = _exchange_wait(sent3, True, sent0b[-1], "grads_wait3")
    in2, out2 = _exchange_wait(sent2, True, sent0b[-1], "grads_wait2")
    in1, out1 = _exchange_wait(sent1, True, sent0b[-1], "grads_wait1")
    res = {}
    res["na_w_in"] = [r[None] for r in big(in1, na_w_in[0], m_na_w_in[0], v_na_w_in[0], "adamw_na_in")]
    res["na_w_out"] = [r[None] for r in big(out1, na_w_out[0], m_na_w_out[0], v_na_w_out[0], "adamw_na_out")]
    res["conv_w_in"] = [r[None] for r in big(in2, conv_w_in[0], m_conv_w_in[0], v_conv_w_in[0], "adamw_conv_in")]
    res["conv_w_out"] = [r[None] for r in big(out2, conv_w_out[0], m_conv_w_out[0], v_conv_w_out[0], "adamw_conv_out")]

    done = [res[n][0] for n in ("na_w_in", "na_w_out", "conv_w_in", "conv_w_out")]
    small_out, owned = _exchange_wait(small_sent, [False, True], done, "small_grads_wait")
    loss = jnp.sum(small_out[:, marks[4], 0])
    dm_all = small_out[:, :marks[1]].reshape(N_DEV, 8, 3 * d).transpose(1, 0, 2)
    dm_mine = lax.dynamic_slice(dm_all, (0, 0, me * nb), (8, N_DEV, nb))
    g_ada_w, g_ada_b, cctx_part, dsilu_cond = _mod_bwd(cond, ada_w, dm_all, dm_mine)
    cctx_all = _gather_small(_rows128(cctx_part[0]), "gather_cctx")

    def summed(ref, lo, hi):
        g = ref[0, lo:hi, :]
        for k in range(1, N_DEV):
            g = g + ref[k, lo:hi, :]
        return g

    makers = [
        lambda so, ow, cc, ab, ds: summed(cc, 0, d_rows) * ds[...],
        lambda so, ow, cc, ab, ds: summed(so, marks[1], marks[2]),
        lambda so, ow, cc, ab, ds: ab[...],
        lambda so, ow, cc, ab, ds: summed(so, marks[2], marks[3]),
        lambda so, ow, cc, ab, ds: summed(so, marks[3], marks[4]),
        lambda so, ow, cc, ab, ds: summed(ow, 0, n_pool),
        lambda so, ow, cc, ab, ds: summed(ow, n_pool, n_pool + 3),
        lambda so, ow, cc, ab, ds: summed(ow, n_pool + 3, n_pool + 4),
    ]
    rpb_rows128 = lambda a: _pad_rows(_rows128(a))
    views = [_rows128] * 4 + [rpb_rows128] + [lambda a: a.reshape(-1, LANES)] * 3
    small = [(c_ctx, m_c_ctx, v_c_ctx), (norm_g, m_norm_g, v_norm_g), (ada_b, m_ada_b, v_ada_b),
             (final_g, m_final_g, v_final_g), (na_rpb, m_na_rpb, v_na_rpb), (pool_scale, m_pool_scale, v_pool_scale),
             (conv_dw, m_conv_dw, v_conv_dw), (conv_db, m_conv_db, v_conv_db)]
    states = [tuple(view(a) for a in triple) for view, triple in zip(views, small)]
    sources = (small_out, owned, cctx_all, _rows128(g_ada_b), _rows128(dsilu_cond[8]))
    small_res = _adamw_small(states, (sources, makers))
    names = ["c_ctx", "norm_g", "ada_b", "final_g", "na_rpb", "pool_scale", "conv_dw", "conv_db"]
    for name, (w, _, _), outs4 in zip(names, small, small_res):
        res[name] = [r.reshape(-1)[:w.size].reshape(w.shape) for r in outs4]

    res["ada_w"] = [r.reshape(ada_w.shape) for r in _adamw(
        ada_w.reshape(-1, nb), m_ada_w.reshape(-1, nb), v_ada_w.reshape(-1, nb), [g_ada_w.reshape(-1, nb)], "adamw_ada_w")]

    out0, = _exchange_wait(sent0a, True, small_res[0][0], "grads_wait0a")
    in0, grp0 = _exchange_wait(sent0b, True, small_res[0][0], "grads_wait0b")
    def both_layers(first, second, w, m, v, name):
        view = (w.shape[0], -1, w.shape[-1])
        landed = [first.reshape((N_DEV,) + w.reshape(view).shape[1:]), second.reshape((N_DEV,) + w.reshape(view).shape[1:])]
        return [r.reshape(w.shape) for r in _adamw_layers(w.reshape(view), m.reshape(view), v.reshape(view), landed, name)]

    res["pool_w_in"] = both_layers(in0, in3, pool_w_in, m_pool_w_in, v_pool_w_in, "adamw_pool_in")
    res["pool_w_grp"] = both_layers(grp0, grp3, pool_w_grp, m_pool_w_grp, v_pool_w_grp, "adamw_pool_grp")
    res["pool_w_out"] = both_layers(out0, out3, pool_w_out, m_pool_w_out, v_pool_w_out, "adamw_pool_out")

    order = ["c_ctx", "norm_g", "ada_w", "ada_b", "pool_w_in", "pool_w_grp", "pool_scale", "pool_w_out", "na_w_in",
             "na_rpb", "na_w_out", "conv_w_in", "conv_dw", "conv_db", "conv_w_out", "final_g"]
    outs = [loss, grad_x]
    for j in range(4):
        outs += [res[n][j] for n in order]
    return tuple(outs)
```

```python
import functools
import math

import numpy as np
import jax
import jax.numpy as jnp
from jax import lax
from jax.experimental import pallas as pl
from jax.experimental.pallas import tpu as pltpu

F32 = jnp.float32
BF16 = jnp.bfloat16
N_DEV = 8
LANES = 128
RMS_EPS = 1e-6
GRID_W = 64
WIN_ROWS = 8
WIN_COLS = 16
HEAD_DIM = 64
POOL_WINDOWS = (2, 4, 8, 16)
HALO = 8
CHUNK = 256
MASKED = -1e30
ADAM_LR = 0.001
ADAM_B1 = 0.9
ADAM_B2 = 0.999
ADAM_EPS = 1e-08
ADAM_WD = 0.01
ADAM_STEP = 10
VMEM_LIMIT = 56 * 1024 * 1024
ADAM_TILE_ELEMS = 256 * 1024
MESH = pl.DeviceIdType.MESH
ANY = pl.BlockSpec(memory_space=pl.ANY)
HBM = pl.BlockSpec(memory_space=pltpu.HBM)
SEM = pl.BlockSpec(memory_space=pltpu.SEMAPHORE)
EFFECT = pltpu.SideEffectType.DATAFLOW_SIDE_EFFECTING


def _pc(body, *, name, **kw):
    return pl.pallas_call(body, name=name, **kw)


def _params(*sem):
    return pltpu.CompilerParams(dimension_semantics=sem if sem else None, vmem_limit_bytes=VMEM_LIMIT)


def _dot(a, b, ca=1, cb=0, precision=None):
    return lax.dot_general(a, b, (((ca,), (cb,)), ((), ())), preferred_element_type=F32, precision=precision)


def _tile(n, pref, unit=LANES):
    best = None
    for t in range(unit, min(n, pref) + 1, unit):
        if n % t == 0:
            best = t
    return best if best is not None else n


def _sigmoid(x):
    return 1.0 / (1.0 + jnp.exp(-x))


def _silu(x):
    return x * _sigmoid(x)


def _dsilu(x):
    s = _sigmoid(x)
    return s * (1.0 + x * (1.0 - s))


def _my_place():
    return lax.axis_index("x"), lax.axis_index("y"), lax.axis_index("c")


def _flip(v, f):
    return 1 - v if f else v


def _gather_small(block, name, per_dest=False):
    rows, cols = block.shape[-2:]

    def body(x_ref, out_ref, send_sems, recv_sems):
        x, y, c = _my_place()
        me = 4 * x + 2 * y + c
        out_ref[me] = x_ref[me] if per_dest else x_ref[...]
        copies = []
        for k in range(1, N_DEV):
            peer = (_flip(x, k & 4), _flip(y, k & 2), _flip(c, k & 1))
            dest = 4 * peer[0] + 2 * peer[1] + peer[2]
            cp = pltpu.make_async_remote_copy(
                src_ref=x_ref.at[dest] if per_dest else x_ref, dst_ref=out_ref.at[me],
                send_sem=send_sems.at[k - 1], recv_sem=recv_sems.at[k - 1], device_id=peer, device_id_type=MESH)
            cp.start()
            copies.append(cp)
        for cp in copies:
            cp.wait()

    return _pc(
        body, name=name,
        out_shape=jax.ShapeDtypeStruct((N_DEV, rows, cols), block.dtype),
        in_specs=[pl.BlockSpec(memory_space=pltpu.VMEM)],
        out_specs=pl.BlockSpec(memory_space=pltpu.VMEM),
        scratch_shapes=[pltpu.SemaphoreType.DMA((N_DEV - 1,)), pltpu.SemaphoreType.DMA((N_DEV - 1,))],
    )(block)


ALL_PEERS = tuple(range(N_DEV))
CHIP_PEERS = (0, 1, 2, 4, 6)
OTHER_CHIPS = (2, 4, 6)


def _flag(per_dest, t):
    return per_dest[t] if isinstance(per_dest, (list, tuple)) else per_dest


def _peer(k):
    x, y, c = _my_place()
    peer = (_flip(x, k & 4), _flip(y, k & 2), _flip(c, k & 1))
    return peer, 4 * peer[0] + 2 * peer[1] + peer[2]


def _peer_lists(peers, nt):
    return list(peers) if isinstance(peers, list) else [peers] * nt


def _exchange_copies(srcs, lands, send_sems, recv_sems, per_dest, peers=ALL_PEERS):
    x, y, c = _my_place()
    me = 4 * x + 2 * y + c
    copies = []
    for t, (src, land, ks) in enumerate(zip(srcs, lands, _peer_lists(peers, len(srcs)))):
        for k in ks:
            peer, dest = _peer(k)
            s = len(copies)
            copies.append(pltpu.make_async_remote_copy(
                src_ref=src.at[dest] if _flag(per_dest, t) else src, dst_ref=land.at[me],
                send_sem=send_sems[s], recv_sem=recv_sems[s], device_id=peer, device_id_type=MESH))
    return copies


def _forward_copies(lands, send_sems, recv_sems):
    sibling, _ = _peer(1)
    copies = []
    for t, land in enumerate(lands):
        for n, k in enumerate(OTHER_CHIPS):
            _, slot = _peer(k)
            s = t * len(OTHER_CHIPS) + n
            copies.append(pltpu.make_async_remote_copy(
                src_ref=land.at[slot], dst_ref=land.at[slot], send_sem=send_sems[s], recv_sem=recv_sems[s],
                device_id=sibling, device_id_type=MESH))
    return copies


def _forward_start(lands, after, name):
    nt = len(lands)
    ns = nt * len(OTHER_CHIPS)

    def body(*refs):
        ins, outs = refs[:nt + 1], refs[nt + 1:]
        for cp in _forward_copies(ins[:nt], outs[:ns], outs[ns:2 * ns]):
            cp.start()
        outs[-1][...] = jnp.zeros_like(outs[-1])

    res = _pc(
        body, name=name,
        out_shape=(*[pltpu.SemaphoreType.DMA(())] * (2 * ns), *[pltpu.HBM(a.shape, a.dtype) for a in lands],
                   jax.ShapeDtypeStruct((8, LANES), F32)),
        in_specs=[HBM] * nt + [ANY],
        out_specs=(*[SEM] * (2 * ns), *[HBM] * nt, pl.BlockSpec(memory_space=pltpu.VMEM)),
        input_output_aliases={i: 2 * ns + i for i in range(nt)},
        compiler_params=pltpu.CompilerParams(has_side_effects=EFFECT),
    )(*lands, after)
    return list(res[:ns]), list(res[ns:2 * ns]), list(res[2 * ns:2 * ns + nt]), res[-1]


def _forward_wait(state, after, name):
    send_sems, recv_sems, lands, _ = state
    nt, ns = len(lands), len(send_sems)

    def body(*refs):
        sems = refs[nt:nt + 2 * ns]
        for cp in _forward_copies(refs[:nt], sems[:ns], sems[ns:]):
            cp.wait_send()
            cp.wait_recv()

    res = _pc(
        body, name=name,
        out_shape=tuple(pltpu.HBM(a.shape, a.dtype) for a in lands),
        in_specs=[HBM] * nt + [SEM] * (2 * ns) + [ANY],
        out_specs=tuple([HBM] * nt),
        input_output_aliases={i: i for i in range(nt)},
        compiler_params=pltpu.CompilerParams(has_side_effects=EFFECT),
    )(*lands, *send_sems, *recv_sems, after)
    return list(res)


def _exchange_start(srcs, per_dest, after, name, peers=ALL_PEERS):
    nt = len(srcs)
    peers = _peer_lists(peers, nt)
    ns = sum(len(ks) for ks in peers)
    lands = [lax.empty((N_DEV,) + (s.shape[1:] if _flag(per_dest, t) else s.shape), s.dtype) for t, s in enumerate(srcs)]

    def body(*refs):
        ins, outs = refs[:2 * nt + 1], refs[2 * nt + 1:]
        for cp in _exchange_copies(ins[:nt], ins[nt:2 * nt], outs[:ns], outs[ns:2 * ns], per_dest, peers):
            cp.start()
        outs[-1][...] = jnp.zeros_like(outs[-1])

    hbm = [pltpu.with_memory_space_constraint(a, pltpu.HBM) for a in list(srcs) + lands]
    res = _pc(
        body, name=name,
        out_shape=(*[pltpu.SemaphoreType.DMA(())] * (2 * ns), *[pltpu.HBM(a.shape, a.dtype) for a in hbm],
                   jax.ShapeDtypeStruct((8, LANES), F32)),
        in_specs=[HBM] * (2 * nt) + [ANY],
        out_specs=(*[SEM] * (2 * ns), *[HBM] * (2 * nt), pl.BlockSpec(memory_space=pltpu.VMEM)),
        input_output_aliases={i: 2 * ns + i for i in range(2 * nt)},
        compiler_params=pltpu.CompilerParams(has_side_effects=EFFECT),
    )(*hbm, after)
    sems, rest = res[:2 * ns], res[2 * ns:]
    return list(sems[:ns]), list(sems[ns:]), list(rest[:nt]), list(rest[nt:2 * nt]), peers, rest[-1]


def _exchange_wait(state, per_dest, after, name, which=None):
    send_sems, recv_sems, srcs, lands, peers, _ = state
    which = list(range(len(srcs))) if which is None else which
    per_dest = [_flag(per_dest, t) for t in which]
    after = list(after) if isinstance(after, (list, tuple)) else [after]
    first = np.cumsum([0] + [len(ks) for ks in peers])
    pick = [first[t] + n for t in which for n in range(len(peers[t]))]
    peers = [peers[t] for t in which]
    send_sems, recv_sems = [send_sems[s] for s in pick], [recv_sems[s] for s in pick]
    srcs, lands = [srcs[t] for t in which], [lands[t] for t in which]
    nt = len(srcs)
    ns = len(send_sems)

    def body(*refs):
        sems = refs[2 * nt:2 * nt + 2 * ns]
        for cp in _exchange_copies(refs[:nt], refs[nt:2 * nt], sems[:ns], sems[ns:], per_dest, peers):
            cp.wait_send()
            cp.wait_recv()

    thru = list(srcs) + list(lands)
    res = _pc(
        body, name=name,
        out_shape=tuple(pltpu.HBM(a.shape, a.dtype) for a in thru),
        in_specs=[HBM] * (2 * nt) + [SEM] * (2 * ns) + [ANY] * len(after),
        out_specs=tuple([HBM] * (2 * nt)),
        input_output_aliases={i: i for i in range(2 * nt)},
        compiler_params=pltpu.CompilerParams(has_side_effects=EFFECT),
    )(*thru, *send_sems, *recv_sems, *after)
    return list(res[nt:])


def _mod_fwd(cond, ada_w, bias):
    depth, d, nb = ada_w.shape

    def body(c_ref, w_ref, b_ref, o_ref):
        s = _silu(c_ref[...]).astype(BF16)
        o_ref[...] = _dot(s, w_ref[...].astype(BF16)) + b_ref[...]

    return _pc(
        body, name="mod_fwd", grid=(depth,),
        in_specs=[pl.BlockSpec((16, d), lambda i: (0, 0)), pl.BlockSpec((None, d, nb), lambda i: (i, 0, 0)),
                  pl.BlockSpec((None, 1, nb), lambda i: (i, 0, 0))],
        out_specs=pl.BlockSpec((None, 16, nb), lambda i: (i, 0, 0)),
        out_shape=jax.ShapeDtypeStruct((depth, 16, nb), F32),
        compiler_params=_params("parallel"),
    )(cond, ada_w, bias.reshape(depth, 1, nb))


def _mod_bwd(cond, ada_w, dm_all, dm_mine):
    depth, d, nb = ada_w.shape
    d3 = dm_all.shape[-1]

    def body(c_ref, w_ref, all_ref, call_ref, mine_ref, cmine_ref, gw_ref, gb_ref, part_ref, ds_ref):
        i = pl.program_id(0)
        cond_v = c_ref[...]
        s = _silu(cond_v).astype(BF16)
        has_ctx = jnp.where(i < 2, 1.0, 0.0)
        tot_all = jnp.sum(call_ref[...], axis=0, keepdims=True) * has_ctx
        tot_mine = jnp.broadcast_to(jnp.sum(cmine_ref[...], axis=0, keepdims=True) * has_ctx, (8, nb)).astype(BF16)
        gb_ref[...] = jnp.sum(all_ref[...], axis=0, keepdims=True) + tot_all
        gw_ref[...] = _dot(s[0:8], mine_ref[...].astype(BF16), 0, 0) + _dot(s[8:16], tot_mine, 0, 0)
        part = _dot(tot_mine, w_ref[...].astype(BF16), 1, 1)

        @pl.when(i == 0)
        def _():
            part_ref[...] = jnp.zeros_like(part_ref)
            ds_ref[...] = _dsilu(cond_v)

        part_ref[...] += part

    def rows(width, which):
        return pl.BlockSpec((None, N_DEV, width), which)

    layer = lambda i: (i, 0, 0)
    ctx_layer = lambda i: (jnp.minimum(i, 1) + 4, 0, 0)
    return _pc(
        body, name="mod_bwd", grid=(depth,),
        in_specs=[pl.BlockSpec((16, d), lambda i: (0, 0)), pl.BlockSpec((None, d, nb), layer),
                  rows(d3, layer), rows(d3, ctx_layer), rows(nb, layer), rows(nb, ctx_layer)],
        out_specs=[pl.BlockSpec((None, d, nb), layer), pl.BlockSpec((None, 1, d3), layer),
                   pl.BlockSpec((8, d), lambda i: (0, 0)), pl.BlockSpec((16, d), lambda i: (0, 0))],
        out_shape=[jax.ShapeDtypeStruct((depth, d, nb), F32), jax.ShapeDtypeStruct((depth, 1, d3), F32),
                   jax.ShapeDtypeStruct((8, d), F32), jax.ShapeDtypeStruct((16, d), F32)],
        compiler_params=_params("arbitrary"),
    )(cond, ada_w, dm_all, dm_all, dm_mine, dm_mine)


def _norm_fwd(xs, g, mod, tr, seg_tiles, name):
    t, d = xs.shape

    def body(x_ref, g_ref, mod_ref, h_ref):
        x = x_ref[...]
        r = lax.rsqrt(jnp.mean(x * x, axis=-1, keepdims=True) + RMS_EPS)
        y = (x * r) * g_ref[...]
        h_ref[...] = (y * (1.0 + mod_ref[1:2, :]) + mod_ref[0:1, :]).astype(BF16)

    return _pc(
        body, name=name, grid=(t // tr,),
        in_specs=[pl.BlockSpec((tr, d), lambda i: (i, 0)), pl.BlockSpec((1, d), lambda i: (0, 0)),
                  pl.BlockSpec((None, 8, d), lambda i: (i // seg_tiles, 0, 0))],
        out_specs=pl.BlockSpec((tr, d), lambda i: (i, 0)),
        out_shape=jax.ShapeDtypeStruct((t, d), BF16),
        compiler_params=_params("parallel"),
    )(xs, g, mod)


def _resid_grad(dx, i, seg_tiles, yx_ref, gate_ref, dyx_ref, gsum_ref):
    dyx_ref[...] = (dx * gate_ref[2:3, :]).astype(BF16)

    @pl.when(i % seg_tiles == 0)
    def _():
        gsum_ref[...] = jnp.zeros_like(gsum_ref)

    gsum_ref[0:1, :] += jnp.sum(dx * yx_ref[...], axis=0, keepdims=True)


def _norm_bwd(xs, dh, dres, g, mod, tr, seg_tiles, name, res_tiles=None, out_tiles=None, below=None):
    t, d = xs.shape
    n_tiles = t // tr
    res_tiles = n_tiles if res_tiles is None else res_tiles
    out_tiles = n_tiles if out_tiles is None else out_tiles

    def body(x_ref, dh_ref, dres_ref, g_ref, mod_ref, *rest):
        i = pl.program_id(0)
        x = x_ref[...]
        r = lax.rsqrt(jnp.mean(x * x, axis=-1, keepdims=True) + RMS_EPS)
        xn = x * r
        dhv = dh_ref[...]
        gain = g_ref[...]
        one_scale = 1.0 + mod_ref[1:2, :]
        dxn = dhv * (gain * one_scale)
        dx = r * (dxn - xn * jnp.mean(dxn * xn, axis=-1, keepdims=True))
        if res_tiles == n_tiles:
            dx = dx + dres_ref[...]
        else:
            dx = dx + jnp.where(i < res_tiles, dres_ref[...], 0.0)
        if below is None:
            dx_ref, sum_ref = rest
        else:
            yx_ref, gate_ref, dx_ref, sum_ref, dyx_ref, gsum_ref = rest
            _resid_grad(dx, i, seg_tiles, yx_ref, gate_ref, dyx_ref, gsum_ref)
        if out_tiles == n_tiles:
            dx_ref[...] = dx
        else:
            @pl.when(i < out_tiles)
            def _():
                dx_ref[...] = dx

        @pl.when(i % seg_tiles == 0)
        def _():
            sum_ref[...] = jnp.zeros_like(sum_ref)

        sum_ref[0:1, :] += jnp.sum(dhv, axis=0, keepdims=True)
        sum_ref[1:2, :] += jnp.sum(dhv * (xn * gain), axis=0, keepdims=True)
        sum_ref[2:3, :] += jnp.sum(dhv * one_scale * xn, axis=0, keepdims=True)

    row = pl.BlockSpec((tr, d), lambda i: (i, 0))
    seg = pl.BlockSpec((None, 8, d), lambda i: (i // seg_tiles, 0, 0))
    in_specs = [row, row, pl.BlockSpec((tr, d), lambda i: (jnp.minimum(i, res_tiles - 1), 0)),
                pl.BlockSpec((1, d), lambda i: (0, 0)), seg]
    out_specs = [pl.BlockSpec((tr, d), lambda i: (jnp.minimum(i, out_tiles - 1), 0)), seg]
    out_shape = [jax.ShapeDtypeStruct((out_tiles * tr, d), F32), jax.ShapeDtypeStruct((mod.shape[0], 8, d), F32)]
    args = [xs, dh, dres, g, mod]
    if below is not None:
        in_specs += [row, seg]
        out_specs += [row, seg]
        out_shape += [jax.ShapeDtypeStruct((t, d), BF16), jax.ShapeDtypeStruct((below[1].shape[0], 8, d), F32)]
        args += list(below)
    return _pc(
        body, name=name, grid=(n_tiles,), in_specs=in_specs, out_specs=out_specs, out_shape=out_shape,
        compiler_params=_params("arbitrary"),
    )(*args)


def _loss_head(xs, target, g, yx, mod, tr):
    t, d = xs.shape

    def body(x_ref, t_ref, g_ref, yx_ref, gate_ref, loss_ref, dx_ref, dg_ref, dyx_ref, gsum_ref):
        i = pl.program_id(0)
        x = x_ref[...]
        r = lax.rsqrt(jnp.mean(x * x, axis=-1, keepdims=True) + RMS_EPS)
        xn = x * r
        gain = g_ref[...]
        err = xn * gain - t_ref[...]
        dy = err * (1.0 / d)
        dxn = dy * gain
        dx = r * (dxn - xn * jnp.mean(dxn * xn, axis=-1, keepdims=True))
        dx_ref[...] = dx
        _resid_grad(dx, i, t // tr, yx_ref, gate_ref, dyx_ref, gsum_ref)

        @pl.when(i == 0)
        def _():
            loss_ref[...] = jnp.zeros_like(loss_ref)
            dg_ref[...] = jnp.zeros_like(dg_ref)

        loss_ref[...] += 0.5 * jnp.sum(jnp.mean(err * err, axis=-1, keepdims=True))
        dg_ref[0:1, :] += jnp.sum(dy * xn, axis=0, keepdims=True)

    row = pl.BlockSpec((tr, d), lambda i: (i, 0))
    seg = pl.BlockSpec((None, 8, d), lambda i: (0, 0, 0))
    return _pc(
        body, name="loss_head", grid=(t // tr,),
        in_specs=[row, row, pl.BlockSpec((1, d), lambda i: (0, 0)), row, seg],
        out_specs=[pl.BlockSpec((8, LANES), lambda i: (0, 0)), row, pl.BlockSpec((8, d), lambda i: (0, 0)), row, seg],
        out_shape=[jax.ShapeDtypeStruct((8, LANES), F32), jax.ShapeDtypeStruct((t, d), F32),
                   jax.ShapeDtypeStruct((8, d), F32), jax.ShapeDtypeStruct((t, d), BF16),
                   jax.ShapeDtypeStruct((1, 8, d), F32)],
        compiler_params=_params("arbitrary"),
    )(xs, target, g, yx, mod)


def _proj_in(h, w, layer, width, name, blocks=None, dtype=F32):
    t, d = h.shape
    n8 = w.shape[-1]
    first, count = blocks if blocks is not None else (0, N_DEV)
    per_part = width // n8
    assert first % per_part == 0 and count % per_part == 0
    tm = _tile(t, 1152)

    def body(a_ref, b_ref, o_ref):
        a = a_ref[...]
        for s in range(per_part):
            o_ref[:, s * n8:(s + 1) * n8] = _dot(a, b_ref[s]).astype(dtype)

    return _pc(
        body, name=name, grid=(t // tm, count // per_part),
        in_specs=[pl.BlockSpec((tm, d), lambda i, j: (i, 0)),
                  pl.BlockSpec((per_part, None, d, n8), lambda i, j: (first // per_part + j, layer, 0, 0))],
        out_specs=pl.BlockSpec((None, tm, width), lambda i, j: (j, i, 0)),
        out_shape=jax.ShapeDtypeStruct((count // per_part, t, width), dtype),
        compiler_params=_params("parallel", "parallel"),
    )(h, w)


def _proj_out(z, w, res, mod, tm, seg_tiles, name, nxt=None):
    t, k = z.shape
    d = w.shape[1]

    def body(z_ref, w_ref, res_ref, mod_ref, *rest):
        yx = _dot(z_ref[...], w_ref[...])
        x = res_ref[...] + mod_ref[2:3, :] * yx
        if nxt is None:
            yx_ref, x_ref = rest
        else:
            g_ref, nmod_ref, yx_ref, x_ref, h_ref = rest
            r = lax.rsqrt(jnp.mean(x * x, axis=-1, keepdims=True) + RMS_EPS)
            h_ref[...] = (((x * r) * g_ref[...]) * (1.0 + nmod_ref[1:2, :]) + nmod_ref[0:1, :]).astype(BF16)
        yx_ref[...] = yx
        x_ref[...] = x

    tile = pl.BlockSpec((tm, d), lambda i: (i, 0))
    seg = pl.BlockSpec((None, 8, d), lambda i: (i // seg_tiles, 0, 0))
    in_specs = [pl.BlockSpec((tm, k), lambda i: (i, 0)), pl.BlockSpec((k, d), lambda i: (0, 0)), tile, seg]
    out_specs = [tile, tile]
    out_shape = [jax.ShapeDtypeStruct((t, d), F32), jax.ShapeDtypeStruct((t, d), F32)]
    args = [z, w, res, mod]
    if nxt is not None:
        in_specs += [pl.BlockSpec((1, d), lambda i: (0, 0)), seg]
        out_specs.append(tile)
        out_shape.append(jax.ShapeDtypeStruct((t, d), BF16))
        args += list(nxt)
    return _pc(
        body, name=name, grid=(t // tm,), in_specs=in_specs, out_specs=out_specs, out_shape=out_shape,
        compiler_params=_params("parallel"),
    )(*args)


def _proj_out_dz(dyx, w, name):
    t, d = dyx.shape
    width = w.shape[0]
    tm, tn = _tile(t, 1024), _tile(width, 512)

    def body(a_ref, w_ref, o_ref):
        o_ref[...] = _dot(a_ref[...], w_ref[...], 1, 1)

    return _pc(
        body, name=name, grid=(t // tm, width // tn),
        in_specs=[pl.BlockSpec((tm, d), lambda i, j: (i, 0)), pl.BlockSpec((tn, d), lambda i, j: (j, 0))],
        out_specs=pl.BlockSpec((tm, tn), lambda i, j: (i, j)),
        out_shape=jax.ShapeDtypeStruct((t, width), F32),
        compiler_params=_params("parallel", "parallel"),
    )(dyx, w)


def _proj_in_dh(dpre, w, layer, name, after=None):
    parts, t, width = dpre.shape
    d, n8 = w.shape[-2:]
    per_part = width // n8
    tm, tn = _tile(t, 768), _tile(d, 512)

    def body(a_ref, w_ref, *rest):
        o_ref = rest[-1]
        acc = None
        for p in range(parts):
            for s in range(per_part):
                term = _dot(a_ref[p, :, s * n8:(s + 1) * n8], w_ref[p * per_part + s], 1, 1)
                acc = term if acc is None else acc + term
        o_ref[...] = acc

    extra = [] if after is None else [after]
    return _pc(
        body, name=name, grid=(t // tm, d // tn),
        in_specs=[pl.BlockSpec((parts, tm, width), lambda i, j: (0, i, 0)),
                  pl.BlockSpec((N_DEV, None, tn, n8), lambda i, j: (0, layer, j, 0))] + [ANY] * len(extra),
        out_specs=pl.BlockSpec((tm, tn), lambda i, j: (i, j)),
        out_shape=jax.ShapeDtypeStruct((t, d), F32),
        compiler_params=_params("parallel", "parallel"),
    )(dpre, w, *extra)


def _transposed(a_ref):
    return a_ref[...].T


def _grad_w_in(h, dpre, n8, name):
    t, d = h.shape
    parts, _, width = dpre.shape
    per_part = width // n8
    tm, tk = _tile(d, 512), _tile(t, 1152)
    nk = t // tk

    def body(a_ref, b_ref, o_ref, acc_ref):
        k = pl.program_id(1)

        @pl.when(k == 0)
        def _():
            acc_ref[...] = jnp.zeros_like(acc_ref)

        at = _transposed(a_ref)
        for p in range(parts):
            r = _dot(at, b_ref[p])
            for s in range(per_part):
                acc_ref[p * per_part + s] += r[:, s * n8:(s + 1) * n8]

        @pl.when(k == nk - 1)
        def _():
            o_ref[...] = acc_ref[...].astype(BF16)

    return _pc(
        body, name=name, grid=(d // tm, nk),
        in_specs=[pl.BlockSpec((tk, tm), lambda i, k: (k, i)), pl.BlockSpec((parts, tk, width), lambda i, k: (0, k, 0))],
        out_specs=pl.BlockSpec((parts * per_part, tm, n8), lambda i, k: (0, i, 0)),
        out_shape=jax.ShapeDtypeStruct((parts * per_part, d, n8), BF16),
        scratch_shapes=[pltpu.VMEM((parts * per_part, tm, n8), F32)],
        compiler_params=_params("parallel", "arbitrary"),
    )(h, dpre)


def _grad_w_out(z, dyx, name):
    width = z.shape[1]
    t, d = dyx.shape
    tm, tk = _tile(width, 512), _tile(t, 1152)
    nk = t // tk

    def body(a_ref, b_ref, o_ref, acc_ref):
        k = pl.program_id(1)

        @pl.when(k == 0)
        def _():
            acc_ref[...] = jnp.zeros_like(acc_ref)

        acc_ref[...] += _dot(_transposed(a_ref), b_ref[...])

        @pl.when(k == nk - 1)
        def _():
            o_ref[...] = acc_ref[...].astype(BF16)

    return _pc(
        body, name=name, grid=(width // tm, nk),
        in_specs=[pl.BlockSpec((tk, tm), lambda i, k: (k, i)), pl.BlockSpec((tk, d), lambda i, k: (k, 0))],
        out_specs=pl.BlockSpec((tm, d), lambda i, k: (i, 0)),
        out_shape=jax.ShapeDtypeStruct((width, d), BF16),
        scratch_shapes=[pltpu.VMEM((tm, d), F32)],
        compiler_params=_params("parallel", "arbitrary"),
    )(z, dyx)


def _shift(v, k):
    n = v.shape[0]
    return pltpu.roll(v, k % n, 0)


def _window_sum(v, win):
    s = v + _shift(v, 1)
    step = 1
    while 2 * step < win:
        s = _shift(s, step) + _shift(s, -step)
        step *= 2
    return s


def _window_count(base, seg_len, win, shape):
    t = base + lax.broadcasted_iota(jnp.int32, shape, 0)
    hi = jnp.minimum(t + win // 2, seg_len)
    lo = jnp.maximum(t - win // 2, 0)
    return (hi - lo).astype(F32)


def _pad_offsets(segs):
    return [HALO * (s + 1) + st for s, (st, _) in enumerate(segs)]


def _for_chunks(segs, fn):
    offs = _pad_offsets(segs)
    for s, (st, ln) in enumerate(segs):
        def step(ci, carry, s=s, st=st, ln=ln):
            fn(s, st, ln, offs[s], pl.multiple_of(ci * CHUNK, CHUNK))
            return carry
        lax.fori_loop(0, ln // CHUNK, step, 0)


def _pool_fwd(pre, w_grp, scale, segs, name):
    _, t, width = pre.shape
    grp = width // len(POOL_WINDOWS)
    padded = t + HALO * (len(segs) + 1)

    def group(win, pre_ref, w_ref, sc_ref, z_ref, diff_ref, pad_ref):
        pad_ref[...] = jnp.zeros_like(pad_ref)

        def fill(s, st, ln, off, b):
            pad_ref[pl.ds(off + b, CHUNK), :] = pre_ref[0, pl.ds(st + b, CHUNK), :]

        _for_chunks(segs, fill)

        def mix(s, st, ln, off, b):
            ext = pad_ref[pl.ds(off - HALO + b, CHUNK + 2 * HALO), :]
            total = _window_sum(ext, win)[HALO:HALO + CHUNK]
            u = pre_ref[0, pl.ds(st + b, CHUNK), :]
            diff = (total / _window_count(b, ln, win, u.shape) - u).astype(BF16)
            mixed = _dot(diff, w_ref[...])
            gate = _silu(pre_ref[1, pl.ds(st + b, CHUNK), :])
            z_ref[pl.ds(st + b, CHUNK), :] = (mixed * sc_ref[...] * gate).astype(BF16)
            diff_ref[pl.ds(st + b, CHUNK), :] = diff

        _for_chunks(segs, mix)

    def body(pre_ref, w_ref, sc_ref, z_ref, diff_ref, pad_ref):
        gi = pl.program_id(0)
        for widx, win in enumerate(POOL_WINDOWS):
            @pl.when(gi == widx)
            def _(win=win):
                group(win, pre_ref, w_ref, sc_ref, z_ref, diff_ref, pad_ref)

    col = pl.BlockSpec((t, grp), lambda g: (0, g))
    return _pc(
        body, name=name, grid=(len(POOL_WINDOWS),),
        in_specs=[pl.BlockSpec((2, t, grp), lambda g: (0, 0, g)), pl.BlockSpec((None, grp, grp), lambda g: (g, 0, 0)),
                  pl.BlockSpec((1, grp), lambda g: (0, g))],
        out_specs=[col, col],
        out_shape=[jax.ShapeDtypeStruct((t, width), BF16), jax.ShapeDtypeStruct((t, width), BF16)],
        scratch_shapes=[pltpu.VMEM((padded, grp), F32)],
        compiler_params=_params("parallel"),
    )(pre, w_grp, scale)


def _pool_bwd(dz, diff, pre, w_grp, scale, segs, name):
    _, t, width = pre.shape
    grp = width // len(POOL_WINDOWS)
    padded = t + HALO * (len(segs) + 1)

    def group(win, dz_ref, diff_ref, pre_ref, w_ref, sc_ref, dpre_ref, dw_ref, dsc_ref, pad_ref, dd_ref):
        pad_ref[...] = jnp.zeros_like(pad_ref)
        dw_ref[...] = jnp.zeros_like(dw_ref)
        dsc_ref[...] = jnp.zeros_like(dsc_ref)

        def first(s, st, ln, off, b):
            rows = pl.ds(st + b, CHUNK)
            diff_v = diff_ref[rows, :]
            mixed = _dot(diff_v, w_ref[...])
            g = pre_ref[1, rows, :]
            sg = _silu(g)
            dzv = dz_ref[rows, :]
            dmixed = (dzv * sc_ref[...] * sg).astype(BF16)
            dsc_ref[...] += jnp.sum(dzv * mixed * sg, axis=0, keepdims=True)
            dpre_ref[1, rows, :] = (dzv * mixed * sc_ref[...] * _dsilu(g)).astype(BF16)
            ddiff = _dot(dmixed, w_ref[...], 1, 1)
            dw_ref[...] += _dot(diff_v, dmixed, 0, 0)
            dd_ref[rows, :] = ddiff
            pad_ref[pl.ds(off + b, CHUNK), :] = ddiff / _window_count(b, ln, win, ddiff.shape)

        _for_chunks(segs, first)

        def second(s, st, ln, off, b):
            rows = pl.ds(st + b, CHUNK)
            ext = pad_ref[pl.ds(off - HALO + b, CHUNK + 2 * HALO), :]
            total = _shift(_window_sum(ext, win), -1)[HALO:HALO + CHUNK]
            dpre_ref[0, rows, :] = (total - dd_ref[rows, :]).astype(BF16)

        _for_chunks(segs, second)

    def body(dz_ref, diff_ref, pre_ref, w_ref, sc_ref, dpre_ref, dw_ref, dsc_ref, pad_ref, dd_ref):
        gi = pl.program_id(0)
        for widx, win in enumerate(POOL_WINDOWS):
            @pl.when(gi == widx)
            def _(win=win):
                group(win, dz_ref, diff_ref, pre_ref, w_ref, sc_ref, dpre_ref, dw_ref, dsc_ref, pad_ref, dd_ref)

    col = pl.BlockSpec((t, grp), lambda g: (0, g))
    both = pl.BlockSpec((2, t, grp), lambda g: (0, 0, g))
    wspec = pl.BlockSpec((None, grp, grp), lambda g: (g, 0, 0))
    sspec = pl.BlockSpec((1, grp), lambda g: (0, g))
    return _pc(
        body, name=name, grid=(len(POOL_WINDOWS),),
        in_specs=[col, col, both, wspec, sspec],
        out_specs=[both, wspec, sspec],
        out_shape=[jax.ShapeDtypeStruct((2, t, width), BF16), jax.ShapeDtypeStruct((len(POOL_WINDOWS), grp, grp), F32),
                   jax.ShapeDtypeStruct((1, width), F32)],
        scratch_shapes=[pltpu.VMEM((padded, grp), F32), pltpu.VMEM((t, grp), F32)],
        compiler_params=_params("parallel"),
    )(dz, diff, pre, w_grp, scale)


def _conv_fwd(pre, dw, db, name):
    _, t, width = pre.shape
    cb = LANES
    segs = [(0, t)]

    def body(pre_ref, dw_ref, db_ref, z_ref, pad_ref):
        pad_ref[...] = jnp.zeros_like(pad_ref)

        def fill(s, st, ln, off, b):
            rows = pl.ds(b, CHUNK)
            pad_ref[pl.ds(off + b, CHUNK), :] = pre_ref[1, rows, :] * pre_ref[2, rows, :]

        _for_chunks(segs, fill)

        def mix(s, st, ln, off, b):
            rows = pl.ds(b, CHUNK)
            ext = pad_ref[pl.ds(off - HALO + b, CHUNK + 2 * HALO), :]
            conv = (dw_ref[0:1, :] * _shift(ext, 1) + dw_ref[1:2, :] * ext + dw_ref[2:3, :] * _shift(ext, -1))
            conv = conv[HALO:HALO + CHUNK] + db_ref[...]
            y = pre_ref[0, rows, :] * conv
            z_ref[rows, :] = (y * _silu(pre_ref[3, rows, :])).astype(BF16)

        _for_chunks(segs, mix)

    return _pc(
        body, name=name, grid=(width // cb,),
        in_specs=[pl.BlockSpec((4, t, cb), lambda j: (0, 0, j)), pl.BlockSpec((8, cb), lambda j: (0, j)),
                  pl.BlockSpec((1, cb), lambda j: (0, j))],
        out_specs=pl.BlockSpec((t, cb), lambda j: (0, j)),
        out_shape=jax.ShapeDtypeStruct((t, width), BF16),
        scratch_shapes=[pltpu.VMEM((t + 2 * HALO, cb), F32)],
        compiler_params=_params("parallel"),
    )(pre, dw, db)


def _conv_bwd(dz, pre, dw, db, name):
    _, t, width = pre.shape
    cb = LANES
    segs = [(0, t)]

    def body(dz_ref, pre_ref, dw_ref, db_ref, dpre_ref, ddw_ref, ddb_ref, pad_a, pad_c):
        pad_a[...] = jnp.zeros_like(pad_a)
        pad_c[...] = jnp.zeros_like(pad_c)
        ddw_ref[...] = jnp.zeros_like(ddw_ref)
        ddb_ref[...] = jnp.zeros_like(ddb_ref)

        def fill(s, st, ln, off, b):
            rows = pl.ds(b, CHUNK)
            pad_a[pl.ds(off + b, CHUNK), :] = pre_ref[1, rows, :] * pre_ref[2, rows, :]

        _for_chunks(segs, fill)

        def first(s, st, ln, off, b):
            rows = pl.ds(b, CHUNK)
            ext = pad_a[pl.ds(off - HALO + b, CHUNK + 2 * HALO), :]
            prev, nxt = _shift(ext, 1)[HALO:HALO + CHUNK], _shift(ext, -1)[HALO:HALO + CHUNK]
            here = ext[HALO:HALO + CHUNK]
            conv = dw_ref[0:1, :] * prev + dw_ref[1:2, :] * here + dw_ref[2:3, :] * nxt + db_ref[...]
            bg, g = pre_ref[0, rows, :], pre_ref[3, rows, :]
            dzv = dz_ref[rows, :]
            dy = dzv * _silu(g)
            dpre_ref[3, rows, :] = (dzv * (bg * conv) * _dsilu(g)).astype(BF16)
            dpre_ref[0, rows, :] = (dy * conv).astype(BF16)
            dconv = dy * bg
            pad_c[pl.ds(off + b, CHUNK), :] = dconv
            ddw_ref[0:1, :] += jnp.sum(dconv * prev, axis=0, keepdims=True)
            ddw_ref[1:2, :] += jnp.sum(dconv * here, axis=0, keepdims=True)
            ddw_ref[2:3, :] += jnp.sum(dconv * nxt, axis=0, keepdims=True)
            ddb_ref[0:1, :] += jnp.sum(dconv, axis=0, keepdims=True)

        _for_chunks(segs, first)

        def second(s, st, ln, off, b):
            rows = pl.ds(b, CHUNK)
            ext = pad_c[pl.ds(off - HALO + b, CHUNK + 2 * HALO), :]
            da = (dw_ref[0:1, :] * _shift(ext, -1) + dw_ref[1:2, :] * ext + dw_ref[2:3, :] * _shift(ext, 1))
            da = da[HALO:HALO + CHUNK]
            dpre_ref[1, rows, :] = (da * pre_ref[2, rows, :]).astype(BF16)
            dpre_ref[2, rows, :] = (da * pre_ref[1, rows, :]).astype(BF16)

        _for_chunks(segs, second)

    quad = pl.BlockSpec((4, t, cb), lambda j: (0, 0, j))
    rows8 = pl.BlockSpec((8, cb), lambda j: (0, j))
    return _pc(
        body, name=name, grid=(width // cb,),
        in_specs=[pl.BlockSpec((t, cb), lambda j: (0, j)), quad, rows8, pl.BlockSpec((1, cb), lambda j: (0, j))],
        out_specs=[quad, rows8, rows8],
        out_shape=[jax.ShapeDtypeStruct((4, t, width), BF16), jax.ShapeDtypeStruct((8, width), F32),
                   jax.ShapeDtypeStruct((8, width), F32)],
        scratch_shapes=[pltpu.VMEM((t + 2 * HALO, cb), F32), pltpu.VMEM((t + 2 * HALO, cb), F32)],
        compiler_params=_params("parallel"),
    )(dz, pre, dw, db)


PAIR_TILES = 2 * WIN_ROWS - 2


def _pair_geometry():
    lane = lax.broadcasted_iota(jnp.int32, (GRID_W, LANES), 1)
    qcol = lax.broadcasted_iota(jnp.int32, (GRID_W, LANES), 0)
    low = lane < GRID_W
    kcol = jnp.where(low, lane, lane - GRID_W)
    start = jnp.clip(qcol - WIN_COLS // 2, 0, GRID_W - WIN_COLS)
    inside = (kcol >= start) & (kcol < start + WIN_COLS)
    return low, inside


def _bias_tiles(rpb_ref, rows_ref, tiles_ref, inside):
    for h in range(2):
        rows = rpb_ref[h]
        rows_ref[h] = (pltpu.roll(rows, LANES - (WIN_COLS - 1), 1)
                       + pltpu.roll(pltpu.roll(rows, GRID_W - (WIN_COLS - 1), 1), 2 * WIN_ROWS - 1, 0))
        for t in range(PAIR_TILES):
            both = jnp.broadcast_to(rows_ref[h, t:t + 1, :], (GRID_W, LANES))
            tiles_ref[h, t] = jnp.where(inside, pltpu.roll(both, 0, 1, stride=1, stride_axis=0), MASKED)


def _bias_tiles_grad(dtiles_ref, drpb_ref):
    n = PAIR_TILES * GRID_W
    qcol = lax.broadcasted_iota(jnp.int32, (n, LANES), 0) & (GRID_W - 1)
    lane = lax.broadcasted_iota(jnp.int32, (1, LANES), 1)
    zero = jnp.zeros((1, LANES), F32)
    for h in range(2):
        v = pltpu.roll(dtiles_ref[h].reshape(n, LANES), WIN_COLS - 1, 1)
        for bit in range(6):
            v = jnp.where((qcol >> bit) & 1 == 1, pltpu.roll(v, LANES - (1 << bit), 1), v)
        sums = [jnp.sum(v[t * GRID_W:(t + 1) * GRID_W], axis=0, keepdims=True) for t in range(PAIR_TILES)]
        for r in range(2 * WIN_ROWS):
            here = sums[r] if r < PAIR_TILES else zero
            prev = pltpu.roll(sums[r - 1], GRID_W, 1) if 1 <= r <= PAIR_TILES else zero
            drpb_ref[h, r:r + 1, :] = jnp.where(lane < 2 * WIN_COLS - 1, here + prev, 0.0)


def _attn_rows(r, n_rows):
    first = jnp.clip(r - WIN_ROWS // 2, 0, n_rows - WIN_ROWS)
    return first, first - r + WIN_ROWS - 1


def _softmax(s_loc, s_ctx):
    m = jnp.maximum(jnp.max(s_loc, axis=-1, keepdims=True), jnp.max(s_ctx, axis=-1, keepdims=True))
    e_loc, e_ctx = jnp.exp(s_loc - m), jnp.exp(s_ctx - m)
    inv = 1.0 / (jnp.sum(e_loc, axis=-1, keepdims=True) + jnp.sum(e_ctx, axis=-1, keepdims=True))
    return e_loc * inv, e_ctx * inv


def _pair_bias(tiles_ref, j):
    return jnp.concatenate(
        [jnp.concatenate([tiles_ref[h, j + 2 * m] for m in range(WIN_ROWS // 2)], axis=1) for h in range(2)], axis=0)


ROWS_PER_STEP = 8
ROWS_PER_STEP_BWD = 4


def _by_head(tile, low):
    zero = jnp.zeros_like(tile)
    return jnp.concatenate([jnp.where(low, tile, zero), jnp.where(low, zero, tile)], axis=0)


def _merge_heads(stacked, low):
    return jnp.where(low, stacked[:GRID_W], stacked[GRID_W:])


def _attn_items(step, n_rows, q_ref, low, per_step):
    items = []
    for u in range(per_step):
        r = step * per_step + u
        first, j = _attn_rows(r, n_rows)
        rows = pl.ds(pl.multiple_of(r * GRID_W, GRID_W), GRID_W)
        keys = pl.ds(pl.multiple_of(first * GRID_W, GRID_W), WIN_ROWS * GRID_W)
        q = (q_ref[rows, :].astype(F32) * HEAD_DIM ** -0.5).astype(BF16)
        items.append((rows, keys, j, _by_head(q, low)))
    return items


def _attn_fwd(qkv, gate, rpb, seq):
    _, t, width = qkv.shape
    n_rows = seq // GRID_W
    n_ctx = t - seq
    blk = WIN_ROWS * GRID_W

    def body(q_ref, k_ref, v_ref, g_ref, rpb_ref, z_ref, o_ref, rows_ref, tiles_ref):
        low, inside = _pair_geometry()
        _bias_tiles(rpb_ref, rows_ref, tiles_ref, inside)
        ctx = pl.ds(seq, n_ctx)

        def step(i, carry):
            items = _attn_items(i, n_rows, q_ref, low, ROWS_PER_STEP)
            k_ctx, v_ctx = k_ref[ctx, :], v_ref[ctx, :]
            scores = [(_dot(q, k_ref[keys, :], 1, 1) + _pair_bias(tiles_ref, j), _dot(q, k_ctx, 1, 1))
                      for _, keys, j, q in items]
            probs = [_softmax(s_loc, s_ctx) for s_loc, s_ctx in scores]
            outs = [_dot(p_loc.astype(BF16), v_ref[keys, :]) + _dot(p_ctx.astype(BF16), v_ctx)
                    for (_, keys, _, _), (p_loc, p_ctx) in zip(items, probs)]
            for (rows, _, _, _), out in zip(items, outs):
                o = _merge_heads(out, low)
                o_ref[rows, :] = o
                z_ref[rows, :] = (o * _silu(g_ref[rows, :])).astype(BF16)
            return carry

        lax.fori_loop(0, n_rows // ROWS_PER_STEP, step, 0)

    def part(p):
        return pl.BlockSpec((None, t, LANES), lambda h: (p, 0, h))

    out = pl.BlockSpec((seq, LANES), lambda h: (0, h))
    return _pc(
        body, name="attn_fwd", grid=(width // LANES,),
        in_specs=[part(0), part(1), part(2), part(0), pl.BlockSpec((2, 2 * WIN_ROWS, LANES), lambda h: (h, 0, 0))],
        out_specs=[out, out],
        out_shape=[jax.ShapeDtypeStruct((seq, width), BF16), jax.ShapeDtypeStruct((seq, width), F32)],
        scratch_shapes=[pltpu.VMEM((2, 2 * WIN_ROWS, LANES), F32), pltpu.VMEM((2, PAIR_TILES, GRID_W, LANES), F32)],
        compiler_params=_params("parallel"),
    )(qkv, qkv, qkv, gate, rpb)


def _attn_bwd(qkv, gate, o, dz, rpb, seq):
    _, t, width = qkv.shape
    n_rows = seq // GRID_W
    n_ctx = t - seq
    blk = WIN_ROWS * GRID_W
    heads = 2 * width // LANES

    def body(q_ref, k_ref, v_ref, g_ref, o_ref, dz_ref, rpb_ref, dpre_ref, drpb_ref,
             rows_ref, tiles_ref, dtiles_ref, dk_ref, dv_ref):
        low, inside = _pair_geometry()
        _bias_tiles(rpb_ref, rows_ref, tiles_ref, inside)
        dtiles_ref[...] = jnp.zeros_like(dtiles_ref)
        dk_ref[...] = jnp.zeros_like(dk_ref)
        dv_ref[...] = jnp.zeros_like(dv_ref)
        ctx = pl.ds(seq, n_ctx)

        def step(i, carry):
            items = _attn_items(i, n_rows, q_ref, low, ROWS_PER_STEP_BWD)
            k_ctx, v_ctx = k_ref[ctx, :], v_ref[ctx, :]
            d_outs = []
            for rows, _, _, _ in items:
                g = g_ref[rows, :]
                dzv = dz_ref[rows, :]
                dpre_ref[3, rows, :] = (dzv * o_ref[rows, :] * _dsilu(g)).astype(BF16)
                d_outs.append(_by_head((dzv * _silu(g)).astype(BF16), low))
            scores = [(_dot(q, k_ref[keys, :], 1, 1) + _pair_bias(tiles_ref, j), _dot(q, k_ctx, 1, 1))
                      for _, keys, j, q in items]
            dprobs = [(_dot(d_o, v_ref[keys, :], 1, 1), _dot(d_o, v_ctx, 1, 1))
                      for (_, keys, _, _), d_o in zip(items, d_outs)]
            probs = [_softmax(s_loc, s_ctx) for s_loc, s_ctx in scores]
            dscores = []
            for (p_loc, p_ctx), (dp_loc, dp_ctx) in zip(probs, dprobs):
                delta = (jnp.sum(p_loc * dp_loc, axis=-1, keepdims=True)
                         + jnp.sum(p_ctx * dp_ctx, axis=-1, keepdims=True))
                dscores.append((p_loc * (dp_loc - delta), p_ctx * (dp_ctx - delta)))
            dqs = [_dot(ds_loc.astype(BF16), k_ref[keys, :]) + _dot(ds_ctx.astype(BF16), k_ctx)
                   for (_, keys, _, _), (ds_loc, ds_ctx) in zip(items, dscores)]
            for (rows, _, _, _), dq in zip(items, dqs):
                dpre_ref[0, rows, :] = (_merge_heads(dq, low) * HEAD_DIM ** -0.5).astype(BF16)
            for (_, keys, j, q), d_o, (p_loc, p_ctx), (ds_loc, ds_ctx) in zip(items, d_outs, probs, dscores):
                dk_ref[keys, :] += _dot(ds_loc.astype(BF16), q, 0, 0)
                dk_ref[ctx, :] += _dot(ds_ctx.astype(BF16), q, 0, 0)
                dv_ref[keys, :] += _dot(p_loc.astype(BF16), d_o, 0, 0)
                dv_ref[ctx, :] += _dot(p_ctx.astype(BF16), d_o, 0, 0)
                for h in range(2):
                    for m in range(WIN_ROWS // 2):
                        dtiles_ref[h, j + 2 * m] += ds_loc[h * GRID_W:(h + 1) * GRID_W, m * LANES:(m + 1) * LANES]
            return carry

        lax.fori_loop(0, n_rows // ROWS_PER_STEP_BWD, step, 0)
        dpre_ref[1] = dk_ref[...].astype(BF16)
        dpre_ref[2] = dv_ref[...].astype(BF16)
        dpre_ref[0, ctx, :] = jnp.zeros((n_ctx, LANES), BF16)
        dpre_ref[3, ctx, :] = jnp.zeros((n_ctx, LANES), BF16)
        _bias_tiles_grad(dtiles_ref, drpb_ref)

    def part(p):
        return pl.BlockSpec((None, t, LANES), lambda h: (p, 0, h))

    lat = pl.BlockSpec((seq, LANES), lambda h: (0, h))
    rspec = pl.BlockSpec((2, 2 * WIN_ROWS, LANES), lambda h: (h, 0, 0))
    tiles = pltpu.VMEM((2, PAIR_TILES, GRID_W, LANES), F32)
    return _pc(
        body, name="attn_bwd", grid=(width // LANES,),
        in_specs=[part(0), part(1), part(2), part(0), lat, lat, rspec],
        out_specs=[pl.BlockSpec((4, t, LANES), lambda h: (0, 0, h)), rspec],
        out_shape=[jax.ShapeDtypeStruct((4, t, width), BF16), jax.ShapeDtypeStruct((heads, 2 * WIN_ROWS, LANES), F32)],
        scratch_shapes=[pltpu.VMEM((2, 2 * WIN_ROWS, LANES), F32), tiles, tiles,
                        pltpu.VMEM((t, LANES), F32), pltpu.VMEM((t, LANES), F32)],
        compiler_params=_params("parallel"),
    )(qkv, qkv, qkv, gate, o, dz, rpb)


def _adam_update(w, m, v, g):
    m2 = ADAM_B1 * m + (1.0 - ADAM_B1) * g
    v2 = ADAM_B2 * v + (1.0 - ADAM_B2) * (g * g)
    m_hat = m2 / (1.0 - ADAM_B1 ** ADAM_STEP)
    v_hat = v2 / (1.0 - ADAM_B2 ** ADAM_STEP)
    return -ADAM_LR * (m_hat / (jnp.sqrt(v_hat) + ADAM_EPS) + ADAM_WD * w), m2, v2


def _adamw(w, m, v, parts, name):
    rows, cols = w.shape
    tr = _tile(rows, max(8, ADAM_TILE_ELEMS // cols), 8)
    n_parts = len(parts)

    def body(*refs):
        w_ref, m_ref, v_ref = refs[:3]
        part_refs = refs[3:3 + n_parts]
        g_ref, d_ref, nm_ref, nv_ref = refs[3 + n_parts:]
        g = part_refs[0][...].astype(F32)
        for p in part_refs[1:]:
            g = g + p[...].astype(F32)
        g_ref[...] = g
        d_ref[...], nm_ref[...], nv_ref[...] = _adam_update(w_ref[...], m_ref[...], v_ref[...], g)

    tile = pl.BlockSpec((tr, cols), lambda i: (i, 0))
    in_specs, args = [tile, tile, tile], [w, m, v]
    for p in parts:
        if isinstance(p, tuple):
            arr, k = p
            in_specs.append(pl.BlockSpec((None, tr, cols), lambda i, k=k: (k, i, 0)))
            args.append(arr)
        else:
            in_specs.append(tile)
            args.append(p)
    shape = jax.ShapeDtypeStruct((rows, cols), F32)
    return _pc(
        body, name=name, grid=(rows // tr,), in_specs=in_specs, out_specs=[tile] * 4, out_shape=[shape] * 4,
        compiler_params=_params("parallel"),
    )(*args)


def _adamw_layers(w, m, v, landed, name):
    n_layers, rows, cols = w.shape
    tr = _tile(rows, max(8, ADAM_TILE_ELEMS // cols), 8)

    def body(*refs):
        w_ref, m_ref, v_ref = refs[:3]
        part_refs = refs[3:3 + n_layers * N_DEV]
        g_ref, d_ref, nm_ref, nv_ref = refs[3 + n_layers * N_DEV:]
        layer = pl.program_id(0)
        g = None
        for l in range(n_layers):
            s = part_refs[l * N_DEV][...].astype(F32)
            for p in part_refs[l * N_DEV + 1:(l + 1) * N_DEV]:
                s = s + p[...].astype(F32)
            g = s if g is None else jnp.where(layer == l, s, g)
        g_ref[...] = g
        d_ref[...], nm_ref[...], nv_ref[...] = _adam_update(w_ref[...], m_ref[...], v_ref[...], g)

    tile = pl.BlockSpec((None, tr, cols), lambda l, i: (l, i, 0))
    in_specs, args = [tile, tile, tile], [w, m, v]
    for l, arr in enumerate(landed):
        for k in range(N_DEV):
            in_specs.append(pl.BlockSpec((None, tr, cols), lambda ll, i, l=l, k=k: (k, jnp.where(ll == l, i, 0), 0)))
            args.append(arr)
    shape = jax.ShapeDtypeStruct(w.shape, F32)
    return _pc(
        body, name=name, grid=(n_layers, rows // tr), in_specs=in_specs, out_specs=[tile] * 4, out_shape=[shape] * 4,
        compiler_params=_params("arbitrary", "arbitrary"),
    )(*args)


def _adamw_small(states, grads):
    sources, makers = grads
    n, ns = len(states), len(sources)

    def body(*refs):
        src = refs[:ns]
        ins = refs[ns:ns + 3 * n]
        outs = refs[ns + 3 * n:]
        for k in range(n):
            w_ref, m_ref, v_ref = ins[3 * k:3 * k + 3]
            g = makers[k](*src)
            outs[4 * k][...] = g
            outs[4 * k + 1][...], outs[4 * k + 2][...], outs[4 * k + 3][...] = _adam_update(
                w_ref[...], m_ref[...], v_ref[...], g)

    flat = [a for s in states for a in s]
    vmem = pl.BlockSpec(memory_space=pltpu.VMEM)
    res = _pc(
        body, name="adamw_small",
        in_specs=[vmem] * (ns + 3 * n), out_specs=[vmem] * (4 * n),
        out_shape=[jax.ShapeDtypeStruct(s[0].shape, F32) for s in states for _ in range(4)],
        compiler_params=pltpu.CompilerParams(vmem_limit_bytes=VMEM_LIMIT),
    )(*sources, *flat)
    return [res[4 * k:4 * k + 4] for k in range(n)]


def _rows128(a):
    flat = a.reshape(-1)
    pad = (-flat.shape[0]) % LANES
    if pad:
        flat = jnp.concatenate([flat, jnp.zeros((pad,), flat.dtype)])
    return flat.reshape(-1, LANES)


def _pad_rows(a, mult=8):
    pad = (-a.shape[0]) % mult
    if pad:
        a = jnp.concatenate([a, jnp.zeros((pad,) + a.shape[1:], a.dtype)], axis=0)
    return a


def kernel(x, c, ctx, c_ctx, norm_g, ada_w, ada_b, pool_w_in, pool_w_grp, pool_scale, pool_w_out, na_w_in, na_rpb, na_w_out, conv_w_in, conv_dw, conv_db, conv_w_out, final_g, loss_target, m_c_ctx, m_norm_g, m_ada_w, m_ada_b, m_pool_w_in, m_pool_w_grp, m_pool_scale, m_pool_w_out, m_na_w_in, m_na_rpb, m_na_w_out, m_conv_w_in, m_conv_dw, m_conv_db, m_conv_w_out, m_final_g, v_c_ctx, v_norm_g, v_ada_w, v_ada_b, v_pool_w_in, v_pool_w_grp, v_pool_scale, v_pool_w_out, v_na_w_in, v_na_rpb, v_na_w_out, v_conv_w_in, v_conv_dw, v_conv_db, v_conv_w_out, v_final_g):
    xi, yi, ci = _my_place()
    me = 4 * xi + 2 * yi + ci
    seq, d = x.shape[1], x.shape[2]
    n_ctx = ctx.shape[1]
    t_all = seq + n_ctx
    width = d
    depth = norm_g.shape[0]
    nb = ada_w.shape[2]
    shard = width // N_DEV
    d_rows = d // LANES
    assert seq % CHUNK == 0 and n_ctx % CHUNK == 0 and (seq // GRID_W) % ROWS_PER_STEP == 0 and seq >= WIN_ROWS * GRID_W
    tr = math.gcd(math.gcd(seq, n_ctx), 256)
    x_tiles = seq // tr
    tr_lat = math.gcd(seq, 512)
    lat_tiles = seq // tr_lat

    n_pool = pool_scale.shape[0]
    n_grp = pool_w_grp.shape[1]
    grp = width // n_grp

    small_in = _pad_rows(jnp.concatenate([_rows128(c), pool_scale, conv_dw[0], conv_db], axis=0))
    got = _gather_small(small_in, "gather_inputs")
    r0 = d_rows
    c_all = got[:, :r0].reshape(N_DEV, d)
    scale_full = got[:, r0:r0 + n_pool].transpose(1, 0, 2).reshape(n_pool, width)
    r1 = r0 + n_pool
    taps_full = _pad_rows(got[:, r1:r1 + 3].transpose(1, 0, 2).reshape(3, width))
    bias_full = got[:, r1 + 3:r1 + 4].transpose(1, 0, 2).reshape(1, width)

    cond = jnp.concatenate([c_all, c_ctx[None], jnp.zeros((7, d), F32)], axis=0)
    bias_mine = lax.dynamic_slice(ada_b, (0, me * nb), (depth, nb))
    mod_mine = _mod_fwd(cond, ada_w, bias_mine)
    by_example = jnp.stack([mod_mine[:, :N_DEV].transpose(1, 0, 2),
                            jnp.broadcast_to(mod_mine[:, N_DEV][None], (N_DEV, depth, nb))], axis=2)
    mod_all = _gather_small(by_example.reshape(N_DEV, -1, LANES), "gather_mod", per_dest=True)
    mod_all = mod_all.reshape(N_DEV, depth, 2, nb).transpose(1, 2, 0, 3).reshape(depth, 2, 3, d)
    mod_all = jnp.pad(mod_all, ((0, 0), (0, 0), (0, 5), (0, 0)))
    mods = [mod_all[i] if i < 2 else mod_all[i, :1] for i in range(depth)]

    layer_weights = [[pool_w_in[0], pool_w_grp[0], pool_w_out[0]], [na_w_in[0], na_w_out[0]],
                     [conv_w_in[0], conv_w_out[0]], [pool_w_in[1], pool_w_grp[1], pool_w_out[1]]]
    slot = {(i, t): n for n, (i, t) in enumerate((i, t) for i, ws in enumerate(layer_weights) for t in range(len(ws)))}
    two_level = [(0, 0), (1, 0)]
    weights_sent = _exchange_start(
        [w.astype(BF16) for ws in layer_weights for w in ws], False, mod_all, "weights_start",
        peers=[CHIP_PEERS if key in two_level else ALL_PEERS for key in slot])
    token = weights_sent[-1]

    def landed_weight(i, t, after):
        return _exchange_wait(weights_sent, False, after, f"weights_wait{i}_{t}", which=[slot[i, t]])[0]

    def handed_on(i, after):
        half = [landed_weight(i, 0, after)]
        rest = _forward_start(half, after, f"weights_forward{i}")
        return _forward_wait(rest, rest[-1], f"weights_forward_wait{i}")[0][:, None]

    def as_in(w):
        return w[:, None]

    def as_grp(w):
        return w.transpose(1, 0, 2, 3).reshape(n_grp, grp, grp)

    def as_out(w):
        return w.reshape(width, d)

    both = [(0, seq), (seq, n_ctx)]
    latent = [(0, seq)]

    def grp_slots(g):
        return g.reshape(n_grp, N_DEV, grp // N_DEV, grp).transpose(1, 0, 2, 3).reshape(N_DEV, -1, grp).astype(BF16)

    def send_grads(i, grads):
        return _exchange_start(grads, True, jnp.zeros((8, LANES), F32), f"grads_start{i}")

    xs0 = jnp.concatenate([x[0], ctx[0]], axis=0)
    h0 = _norm_fwd(xs0, norm_g[0:1] + token[0, 0], mods[0], tr, x_tiles, "norm_fwd0")
    pool_in_w0 = handed_on(0, h0)
    pre0 = _proj_in(h0, pool_in_w0, 0, width, "proj_in0")
    pool_grp_w0 = as_grp(landed_weight(0, 1, pre0))
    z0, diff0 = _pool_fwd(pre0, pool_grp_w0, scale_full[0:1], both, "pool_fwd0")
    pool_out_w0 = as_out(landed_weight(0, 2, z0))
    yx0, xs1, h1 = _proj_out(z0, pool_out_w0, xs0, mods[0], tr, x_tiles, "proj_out0", nxt=(norm_g[1:2], mods[1]))

    na_in_w = handed_on(1, h1)
    per_part = width // na_w_in.shape[2]
    qkv1 = _proj_in(h1, na_in_w, 0, width, "proj_in1_qkv", blocks=(0, 3 * per_part), dtype=BF16)
    gpre1 = _proj_in(h1, na_in_w, 0, width, "proj_in1_gate", blocks=(3 * per_part, per_part))
    rpb_rows = jnp.pad(na_rpb[0], ((0, 0), (0, 2 * WIN_ROWS - na_rpb.shape[2]), (0, LANES - na_rpb.shape[3])))
    z1, o1 = _attn_fwd(qkv1, gpre1, rpb_rows, seq)
    na_out_w = as_out(landed_weight(1, 1, z1))
    yx1, x2, h2 = _proj_out(z1, na_out_w, xs1, mods[1], tr_lat, lat_tiles, "proj_out1", nxt=(norm_g[2:3], mods[2]))

    conv_in_w = as_in(landed_weight(2, 0, h2))
    pre2 = _proj_in(h2, conv_in_w, 0, width, "proj_in2")
    z2 = _conv_fwd(pre2, taps_full, bias_full, "conv_fwd")
    conv_out_w = as_out(landed_weight(2, 1, z2))
    yx2, x3, h3 = _proj_out(z2, conv_out_w, x2, mods[2], tr_lat, lat_tiles, "proj_out2", nxt=(norm_g[3:4], mods[3]))

    pool_in_w3 = as_in(landed_weight(3, 0, h3))
    pre3 = _proj_in(h3, pool_in_w3, 0, width, "proj_in3")
    pool_grp_w3 = as_grp(landed_weight(3, 1, pre3))
    z3, diff3 = _pool_fwd(pre3, pool_grp_w3, scale_full[1:2], latent, "pool_fwd3")
    pool_out_w3 = as_out(landed_weight(3, 2, z3))
    yx3, x4 = _proj_out(z3, pool_out_w3, x3, mods[3], tr_lat, lat_tiles, "proj_out3")

    loss_part, dx4, d_final, dyx3, gate3 = _loss_head(x4, loss_target[0], final_g[None], yx3, mods[3], tr_lat)

    dz3 = _proj_out_dz(dyx3, pool_out_w3, "proj_out_dz3")
    g_pool_out1 = _grad_w_out(z3, dyx3, "grad_w_out3")
    dpre3, g_grp1, g_scale1 = _pool_bwd(dz3, diff3, pre3, pool_grp_w3, scale_full[1:2], latent, "pool_bwd3")
    dh3 = _proj_in_dh(dpre3, pool_in_w3, 0, "proj_in_dh3")
    g_pool_in1 = _grad_w_in(h3, dpre3, pool_w_in.shape[2], "grad_w_in3")
    sent3 = send_grads(3, [g_pool_in1, grp_slots(g_grp1), g_pool_out1.reshape(N_DEV, shard, d)])
    dx3, norm3, dyx2, gate2 = _norm_bwd(x3, dh3, dx4, norm_g[3:4] + sent3[-1][0, 0], mods[3], tr_lat, lat_tiles, "norm_bwd3",
                                        below=(yx2, mods[2]))

    dz2 = _proj_out_dz(dyx2, conv_out_w, "proj_out_dz2")
    g_conv_out = _grad_w_out(z2, dyx2, "grad_w_out2")
    dpre2, g_taps, g_cbias = _conv_bwd(dz2, pre2, taps_full, bias_full, "conv_bwd")
    dh2 = _proj_in_dh(dpre2, conv_in_w, 0, "proj_in_dh2")
    g_conv_in = _grad_w_in(h2, dpre2, conv_w_in.shape[2], "grad_w_in2")
    sent2 = send_grads(2, [g_conv_in, g_conv_out.reshape(N_DEV, shard, d)])
    dx2, norm2, dyx1, gate1 = _norm_bwd(x2, dh2, dx3, norm_g[2:3] + sent2[-1][0, 0], mods[2], tr_lat, lat_tiles, "norm_bwd2",
                                        below=(yx1, mods[1][:1]))

    dz1 = _proj_out_dz(dyx1, na_out_w, "proj_out_dz1")
    g_na_out = _grad_w_out(z1, dyx1, "grad_w_out1")
    dpre1, g_rpb = _attn_bwd(qkv1, gpre1, o1, dz1, rpb_rows, seq)
    g_rpb = g_rpb[:, :na_rpb.shape[2], :na_rpb.shape[3]]
    dh1 = _proj_in_dh(dpre1, na_in_w, 0, "proj_in_dh1")
    g_na_in = _grad_w_in(h1, dpre1, na_w_in.shape[2], "grad_w_in1")
    sent1 = send_grads(1, [g_na_in, g_na_out.reshape(N_DEV, shard, d)])
    dxs1, norm1, dyx0, gate0 = _norm_bwd(xs1, dh1, dx2, norm_g[1:2] + sent1[-1][0, 0], mods[1], tr, x_tiles, "norm_bwd1",
                                         res_tiles=x_tiles, below=(yx0, mods[0]))

    dz0 = _proj_out_dz(dyx0, pool_out_w0, "proj_out_dz0")
    g_pool_out0 = _grad_w_out(z0, dyx0, "grad_w_out0")
    sent0a = _exchange_start([g_pool_out0.reshape(N_DEV, shard, d)], True, jnp.zeros((8, LANES), F32), "grads_start0a")
    dpre0, g_grp0, g_scale0 = _pool_bwd(dz0, diff0, pre0, pool_grp_w0, scale_full[0:1], both, "pool_bwd0")
    g_pool_in0 = _grad_w_in(h0, dpre0, pool_w_in.shape[2], "grad_w_in0")
    dh0 = _proj_in_dh(dpre0, pool_in_w0, 0, "proj_in_dh0", after=sent0a[-1])
    dx0, norm0 = _norm_bwd(xs0, dh0, dxs1, norm_g[0:1], mods[0], tr, x_tiles, "norm_bwd0", out_tiles=x_tiles)
    grad_x = dx0[None]

    norms, gates = [norm0, norm1, norm2, norm3], [gate0, gate1, gate2, gate3]
    zero_d = jnp.zeros((d,), F32)
    dm_rows = [jnp.concatenate([norms[i][0, 0], norms[i][0, 1], gates[i][0, 0]]) for i in range(depth)]
    dm_rows.append(jnp.concatenate([norm0[1, 0], norm0[1, 1], gate0[1, 0]]))
    dm_rows.append(jnp.concatenate([norm1[1, 0], norm1[1, 1], zero_d]))
    dm_local = jnp.stack(dm_rows + [jnp.zeros((3 * d,), F32)] * 2)
    g_norm_part = jnp.stack([norm0[0, 2] + norm0[1, 2], norm1[0, 2] + norm1[1, 2], norm2[0, 2], norm3[0, 2]])
    pieces = [_rows128(dm_local), _rows128(g_norm_part), _rows128(d_final[0]), _pad_rows(_rows128(g_rpb)), loss_part]
    marks = np.cumsum([0] + [p.shape[0] for p in pieces])
    by_owner = [a.reshape(-1, N_DEV, shard).transpose(1, 0, 2) for a in (g_scale0, g_scale1, g_taps[:3], g_cbias[0:1])]
    by_owner = jnp.concatenate(by_owner + [jnp.zeros((N_DEV, 8 - n_pool - 4, shard), F32)], axis=1)
    small_sent = _exchange_start([jnp.concatenate(pieces, axis=0), by_owner], [False, True], jnp.zeros((8, LANES), F32),
                                 "small_grads_start")
    sent0b = _exchange_start([g_pool_in0, grp_slots(g_grp0)], True, small_sent[-1], "grads_start0b")

    def big(parts, w, m, v, name):
        shape = w.shape
        view = (-1, shape[-1])
        parts = [(parts.reshape((N_DEV,) + w.reshape(view).shape), k) for k in range(N_DEV)]
        return [r.reshape(shape) for r in _adamw(w.reshape(view), m.reshape(view), v.reshape(view), parts, name)]

    in3, grp3, out3 = _exchange_wait(sent3, True, sent0b[-1], "grads_wait3")
    in2, out2 = _exchange_wait(sent2, True, sent0b[-1], "grads_wait2")
    in1, out1 = _exchange_wait(sent1, True, sent0b[-1], "grads_wait1")
    res = {}
    res["na_w_in"] = [r[None] for r in big(in1, na_w_in[0], m_na_w_in[0], v_na_w_in[0], "adamw_na_in")]
    res["na_w_out"] = [r[None] for r in big(out1, na_w_out[0], m_na_w_out[0], v_na_w_out[0], "adamw_na_out")]
    res["conv_w_in"] = [r[None] for r in big(in2, conv_w_in[0], m_conv_w_in[0], v_conv_w_in[0], "adamw_conv_in")]
    res["conv_w_out"] = [r[None] for r in big(out2, conv_w_out[0], m_conv_w_out[0], v_conv_w_out[0], "adamw_conv_out")]

    done = [res[n][0] for n in ("na_w_in", "na_w_out", "conv_w_in", "conv_w_out")]
    small_out, owned = _exchange_wait(small_sent, [False, True], done, "small_grads_wait")
    loss = jnp.sum(small_out[:, marks[4], 0])
    dm_all = small_out[:, :marks[1]].reshape(N_DEV, 8, 3 * d).transpose(1, 0, 2)
    dm_mine = lax.dynamic_slice(dm_all, (0, 0, me * nb), (8, N_DEV, nb))
    g_ada_w, g_ada_b, cctx_part, dsilu_cond = _mod_bwd(cond, ada_w, dm_all, dm_mine)
    cctx_all = _gather_small(_rows128(cctx_part[0]), "gather_cctx")

    def summed(ref, lo, hi):
        g = ref[0, lo:hi, :]
        for k in range(1, N_DEV):
            g = g + ref[k, lo:hi, :]
        return g

    makers = [
        lambda so, ow, cc, ab, ds: summed(cc, 0, d_rows) * ds[...],
        lambda so, ow, cc, ab, ds: summed(so, marks[1], marks[2]),
        lambda so, ow, cc, ab, ds: ab[...],
        lambda so, ow, cc, ab, ds: summed(so, marks[2], marks[3]),
        lambda so, ow, cc, ab, ds: summed(so, marks[3], marks[4]),
        lambda so, ow, cc, ab, ds: summed(ow, 0, n_pool),
        lambda so, ow, cc, ab, ds: summed(ow, n_pool, n_pool + 3),
        lambda so, ow, cc, ab, ds: summed(ow, n_pool + 3, n_pool + 4),
    ]
    rpb_rows128 = lambda a: _pad_rows(_rows128(a))
    views = [_rows128] * 4 + [rpb_rows128] + [lambda a: a.reshape(-1, LANES)] * 3
    small = [(c_ctx, m_c_ctx, v_c_ctx), (norm_g, m_norm_g, v_norm_g), (ada_b, m_ada_b, v_ada_b),
             (final_g, m_final_g, v_final_g), (na_rpb, m_na_rpb, v_na_rpb), (pool_scale, m_pool_scale, v_pool_scale),
             (conv_dw, m_conv_dw, v_conv_dw), (conv_db, m_conv_db, v_conv_db)]
    states = [tuple(view(a) for a in triple) for view, triple in zip(views, small)]
    sources = (small_out, owned, cctx_all, _rows128(g_ada_b), _rows128(dsilu_cond[8]))
    small_res = _adamw_small(states, (sources, makers))
    names = ["c_ctx", "norm_g", "ada_b", "final_g", "na_rpb", "pool_scale", "conv_dw", "conv_db"]
    for name, (w, _, _), outs4 in zip(names, small, small_res):
        res[name] = [r.reshape(-1)[:w.size].reshape(w.shape) for r in outs4]

    res["ada_w"] = [r.reshape(ada_w.shape) for r in _adamw(
        ada_w.reshape(-1, nb), m_ada_w.reshape(-1, nb), v_ada_w.reshape(-1, nb), [g_ada_w.reshape(-1, nb)], "adamw_ada_w")]

    out0, = _exchange_wait(sent0a, True, small_res[0][0], "grads_wait0a")
    in0, grp0 = _exchange_wait(sent0b, True, small_res[0][0], "grads_wait0b")
    def both_layers(first, second, w, m, v, name):
        view = (w.shape[0], -1, w.shape[-1])
        landed = [first.reshape((N_DEV,) + w.reshape(view).shape[1:]), second.reshape((N_DEV,) + w.reshape(view).shape[1:])]
        return [r.reshape(w.shape) for r in _adamw_layers(w.reshape(view), m.reshape(view), v.reshape(view), landed, name)]

    res["pool_w_in"] = both_layers(in0, in3, pool_w_in, m_pool_w_in, v_pool_w_in, "adamw_pool_in")
    res["pool_w_grp"] = both_layers(grp0, grp3, pool_w_grp, m_pool_w_grp, v_pool_w_grp, "adamw_pool_grp")
    res["pool_w_out"] = both_layers(out0, out3, pool_w_out, m_pool_w_out, v_pool_w_out, "adamw_pool_out")

    order = ["c_ctx", "norm_g", "ada_w", "ada_b", "pool_w_in", "pool_w_grp", "pool_scale", "pool_w_out", "na_w_in",
             "na_rpb", "na_w_out", "conv_w_in", "conv_dw", "conv_db", "conv_w_out", "final_g"]
    outs = [loss, grad_x]
    for j in range(4):
        outs += [res[n][j] for n in order]
    return tuple(outs)
```

```python
import functools
import math

import numpy as np
import jax
import jax.numpy as jnp
from jax import lax
from jax.experimental import pallas as pl
from jax.experimental.pallas import tpu as pltpu

F32 = jnp.float32
BF16 = jnp.bfloat16
N_DEV = 8
LANES = 128
RMS_EPS = 1e-6
GRID_W = 64
WIN_ROWS = 8
WIN_COLS = 16
HEAD_DIM = 64
POOL_WINDOWS = (2, 4, 8, 16)
HALO = 8
CHUNK = 256
MASKED = -1e30
ADAM_LR = 0.001
ADAM_B1 = 0.9
ADAM_B2 = 0.999
ADAM_EPS = 1e-08
ADAM_WD = 0.01
ADAM_STEP = 10
VMEM_LIMIT = 56 * 1024 * 1024
ADAM_TILE_ELEMS = 256 * 1024
MESH = pl.DeviceIdType.MESH
ANY = pl.BlockSpec(memory_space=pl.ANY)
HBM = pl.BlockSpec(memory_space=pltpu.HBM)
SEM = pl.BlockSpec(memory_space=pltpu.SEMAPHORE)
EFFECT = pltpu.SideEffectType.DATAFLOW_SIDE_EFFECTING


def _pc(body, *, name, **kw):
    return pl.pallas_call(body, name=name, **kw)


def _params(*sem):
    return pltpu.CompilerParams(dimension_semantics=sem if sem else None, vmem_limit_bytes=VMEM_LIMIT)


def _dot(a, b, ca=1, cb=0, precision=None):
    return lax.dot_general(a, b, (((ca,), (cb,)), ((), ())), preferred_element_type=F32, precision=precision)


def _tile(n, pref, unit=LANES):
    best = None
    for t in range(unit, min(n, pref) + 1, unit):
        if n % t == 0:
            best = t
    return best if best is not None else n


def _sigmoid(x):
    return 1.0 / (1.0 + jnp.exp(-x))


def _silu(x):
    return x * _sigmoid(x)


def _dsilu(x):
    s = _sigmoid(x)
    return s * (1.0 + x * (1.0 - s))


def _my_place():
    return lax.axis_index("x"), lax.axis_index("y"), lax.axis_index("c")


def _flip(v, f):
    return 1 - v if f else v


def _gather_small(block, name, per_dest=False):
    rows, cols = block.shape[-2:]

    def body(x_ref, out_ref, send_sems, recv_sems):
        x, y, c = _my_place()
        me = 4 * x + 2 * y + c
        out_ref[me] = x_ref[me] if per_dest else x_ref[...]
        copies = []
        for k in range(1, N_DEV):
            peer = (_flip(x, k & 4), _flip(y, k & 2), _flip(c, k & 1))
            dest = 4 * peer[0] + 2 * peer[1] + peer[2]
            cp = pltpu.make_async_remote_copy(
                src_ref=x_ref.at[dest] if per_dest else x_ref, dst_ref=out_ref.at[me],
                send_sem=send_sems.at[k - 1], recv_sem=recv_sems.at[k - 1], device_id=peer, device_id_type=MESH)
            cp.start()
            copies.append(cp)
        for cp in copies:
            cp.wait()

    return _pc(
        body, name=name,
        out_shape=jax.ShapeDtypeStruct((N_DEV, rows, cols), block.dtype),
        in_specs=[pl.BlockSpec(memory_space=pltpu.VMEM)],
        out_specs=pl.BlockSpec(memory_space=pltpu.VMEM),
        scratch_shapes=[pltpu.SemaphoreType.DMA((N_DEV - 1,)), pltpu.SemaphoreType.DMA((N_DEV - 1,))],
    )(block)


ALL_PEERS = tuple(range(N_DEV))
CHIP_PEERS = (0, 1, 2, 4, 6)
OTHER_CHIPS = (2, 4, 6)


def _flag(per_dest, t):
    return per_dest[t] if isinstance(per_dest, (list, tuple)) else per_dest


def _peer(k):
    x, y, c = _my_place()
    peer = (_flip(x, k & 4), _flip(y, k & 2), _flip(c, k & 1))
    return peer, 4 * peer[0] + 2 * peer[1] + peer[2]


def _peer_lists(peers, nt):
    return list(peers) if isinstance(peers, list) else [peers] * nt


def _exchange_copies(srcs, lands, send_sems, recv_sems, per_dest, peers=ALL_PEERS):
    x, y, c = _my_place()
    me = 4 * x + 2 * y + c
    copies = []
    for t, (src, land, ks) in enumerate(zip(srcs, lands, _peer_lists(peers, len(srcs)))):
        for k in ks:
            peer, dest = _peer(k)
            s = len(copies)
            copies.append(pltpu.make_async_remote_copy(
                src_ref=src.at[dest] if _flag(per_dest, t) else src, dst_ref=land.at[me],
                send_sem=send_sems[s], recv_sem=recv_sems[s], device_id=peer, device_id_type=MESH))
    return copies


def _forward_copies(lands, send_sems, recv_sems):
    sibling, _ = _peer(1)
    copies = []
    for t, land in enumerate(lands):
        for n, k in enumerate(OTHER_CHIPS):
            _, slot = _peer(k)
            s = t * len(OTHER_CHIPS) + n
            copies.append(pltpu.make_async_remote_copy(
                src_ref=land.at[slot], dst_ref=land.at[slot], send_sem=send_sems[s], recv_sem=recv_sems[s],
                device_id=sibling, device_id_type=MESH))
    return copies


def _forward_start(lands, after, name):
    nt = len(lands)
    ns = nt * len(OTHER_CHIPS)

    def body(*refs):
        ins, outs = refs[:nt + 1], refs[nt + 1:]
        for cp in _forward_copies(ins[:nt], outs[:ns], outs[ns:2 * ns]):
            cp.start()
        outs[-1][...] = jnp.zeros_like(outs[-1])

    res = _pc(
        body, name=name,
        out_shape=(*[pltpu.SemaphoreType.DMA(())] * (2 * ns), *[pltpu.HBM(a.shape, a.dtype) for a in lands],
                   jax.ShapeDtypeStruct((8, LANES), F32)),
        in_specs=[HBM] * nt + [ANY],
        out_specs=(*[SEM] * (2 * ns), *[HBM] * nt, pl.BlockSpec(memory_space=pltpu.VMEM)),
        input_output_aliases={i: 2 * ns + i for i in range(nt)},
        compiler_params=pltpu.CompilerParams(has_side_effects=EFFECT),
    )(*lands, after)
    return list(res[:ns]), list(res[ns:2 * ns]), list(res[2 * ns:2 * ns + nt]), res[-1]


def _forward_wait(state, after, name):
    send_sems, recv_sems, lands, _ = state
    nt, ns = len(lands), len(send_sems)

    def body(*refs):
        sems = refs[nt:nt + 2 * ns]
        for cp in _forward_copies(refs[:nt], sems[:ns], sems[ns:]):
            cp.wait_send()
            cp.wait_recv()

    res = _pc(
        body, name=name,
        out_shape=tuple(pltpu.HBM(a.shape, a.dtype) for a in lands),
        in_specs=[HBM] * nt + [SEM] * (2 * ns) + [ANY],
        out_specs=tuple([HBM] * nt),
        input_output_aliases={i: i for i in range(nt)},
        compiler_params=pltpu.CompilerParams(has_side_effects=EFFECT),
    )(*lands, *send_sems, *recv_sems, after)
    return list(res)


def _exchange_start(srcs, per_dest, after, name, peers=ALL_PEERS):
    nt = len(srcs)
    peers = _peer_lists(peers, nt)
    ns = sum(len(ks) for ks in peers)
    lands = [lax.empty((N_DEV,) + (s.shape[1:] if _flag(per_dest, t) else s.shape), s.dtype) for t, s in enumerate(srcs)]

    def body(*refs):
        ins, outs = refs[:2 * nt + 1], refs[2 * nt + 1:]
        for cp in _exchange_copies(ins[:nt], ins[nt:2 * nt], outs[:ns], outs[ns:2 * ns], per_dest, peers):
            cp.start()
        outs[-1][...] = jnp.zeros_like(outs[-1])

    hbm = [pltpu.with_memory_space_constraint(a, pltpu.HBM) for a in list(srcs) + lands]
    res = _pc(
        body, name=name,
        out_shape=(*[pltpu.SemaphoreType.DMA(())] * (2 * ns), *[pltpu.HBM(a.shape, a.dtype) for a in hbm],
                   jax.ShapeDtypeStruct((8, LANES), F32)),
        in_specs=[HBM] * (2 * nt) + [ANY],
        out_specs=(*[SEM] * (2 * ns), *[HBM] * (2 * nt), pl.BlockSpec(memory_space=pltpu.VMEM)),
        input_output_aliases={i: 2 * ns + i for i in range(2 * nt)},
        compiler_params=pltpu.CompilerParams(has_side_effects=EFFECT),
    )(*hbm, after)
    sems, rest = res[:2 * ns], res[2 * ns:]
    return list(sems[:ns]), list(sems[ns:]), list(rest[:nt]), list(rest[nt:2 * nt]), peers, rest[-1]


def _exchange_wait(state, per_dest, after, name, which=None):
    send_sems, recv_sems, srcs, lands, peers, _ = state
    which = list(range(len(srcs))) if which is None else which
    per_dest = [_flag(per_dest, t) for t in which]
    after = list(after) if isinstance(after, (list, tuple)) else [after]
    first = np.cumsum([0] + [len(ks) for ks in peers])
    pick = [first[t] + n for t in which for n in range(len(peers[t]))]
    peers = [peers[t] for t in which]
    send_sems, recv_sems = [send_sems[s] for s in pick], [recv_sems[s] for s in pick]
    srcs, lands = [srcs[t] for t in which], [lands[t] for t in which]
    nt = len(srcs)
    ns = len(send_sems)

    def body(*refs):
        sems = refs[2 * nt:2 * nt + 2 * ns]
        for cp in _exchange_copies(refs[:nt], refs[nt:2 * nt], sems[:ns], sems[ns:], per_dest, peers):
            cp.wait_send()
            cp.wait_recv()

    thru = list(srcs) + list(lands)
    res = _pc(
        body, name=name,
        out_shape=tuple(pltpu.HBM(a.shape, a.dtype) for a in thru),
        in_specs=[HBM] * (2 * nt) + [SEM] * (2 * ns) + [ANY] * len(after),
        out_specs=tuple([HBM] * (2 * nt)),
        input_output_aliases={i: i for i in range(2 * nt)},
        compiler_params=pltpu.CompilerParams(has_side_effects=EFFECT),
    )(*thru, *send_sems, *recv_sems, *after)
    return list(res[nt:])


def _mod_fwd(cond, ada_w, bias):
    depth, d, nb = ada_w.shape

    def body(c_ref, w_ref, b_ref, o_ref):
        s = _silu(c_ref[...]).astype(BF16)
        o_ref[...] = _dot(s, w_ref[...].astype(BF16)) + b_ref[...]

    return _pc(
        body, name="mod_fwd", grid=(depth,),
        in_specs=[pl.BlockSpec((16, d), lambda i: (0, 0)), pl.BlockSpec((None, d, nb), lambda i: (i, 0, 0)),
                  pl.BlockSpec((None, 1, nb), lambda i: (i, 0, 0))],
        out_specs=pl.BlockSpec((None, 16, nb), lambda i: (i, 0, 0)),
        out_shape=jax.ShapeDtypeStruct((depth, 16, nb), F32),
        compiler_params=_params("parallel"),
    )(cond, ada_w, bias.reshape(depth, 1, nb))


def _mod_bwd(cond, ada_w, dm_all, dm_mine):
    depth, d, nb = ada_w.shape
    d3 = dm_all.shape[-1]

    def body(c_ref, w_ref, all_ref, call_ref, mine_ref, cmine_ref, gw_ref, gb_ref, part_ref, ds_ref):
        i = pl.program_id(0)
        cond_v = c_ref[...]
        s = _silu(cond_v).astype(BF16)
        has_ctx = jnp.where(i < 2, 1.0, 0.0)
        tot_all = jnp.sum(call_ref[...], axis=0, keepdims=True) * has_ctx
        tot_mine = jnp.broadcast_to(jnp.sum(cmine_ref[...], axis=0, keepdims=True) * has_ctx, (8, nb)).astype(BF16)
        gb_ref[...] = jnp.sum(all_ref[...], axis=0, keepdims=True) + tot_all
        gw_ref[...] = _dot(s[0:8], mine_ref[...].astype(BF16), 0, 0) + _dot(s[8:16], tot_mine, 0, 0)
        part = _dot(tot_mine, w_ref[...].astype(BF16), 1, 1)

        @pl.when(i == 0)
        def _():
            part_ref[...] = jnp.zeros_like(part_ref)
            ds_ref[...] = _dsilu(cond_v)

        part_ref[...] += part

    def rows(width, which):
        return pl.BlockSpec((None, N_DEV, width), which)

    layer = lambda i: (i, 0, 0)
    ctx_layer = lambda i: (jnp.minimum(i, 1) + 4, 0, 0)
    return _pc(
        body, name="mod_bwd", grid=(depth,),
        in_specs=[pl.BlockSpec((16, d), lambda i: (0, 0)), pl.BlockSpec((None, d, nb), layer),
                  rows(d3, layer), rows(d3, ctx_layer), rows(nb, layer), rows(nb, ctx_layer)],
        out_specs=[pl.BlockSpec((None, d, nb), layer), pl.BlockSpec((None, 1, d3), layer),
                   pl.BlockSpec((8, d), lambda i: (0, 0)), pl.BlockSpec((16, d), lambda i: (0, 0))],
        out_shape=[jax.ShapeDtypeStruct((depth, d, nb), F32), jax.ShapeDtypeStruct((depth, 1, d3), F32),
                   jax.ShapeDtypeStruct((8, d), F32), jax.ShapeDtypeStruct((16, d), F32)],
        compiler_params=_params("arbitrary"),
    )(cond, ada_w, dm_all, dm_all, dm_mine, dm_mine)


def _norm_fwd(xs, g, mod, tr, seg_tiles, name):
    t, d = xs.shape

    def body(x_ref, g_ref, mod_ref, h_ref):
        x = x_ref[...]
        r = lax.rsqrt(jnp.mean(x * x, axis=-1, keepdims=True) + RMS_EPS)
        y = (x * r) * g_ref[...]
        h_ref[...] = (y * (1.0 + mod_ref[1:2, :]) + mod_ref[0:1, :]).astype(BF16)

    return _pc(
        body, name=name, grid=(t // tr,),
        in_specs=[pl.BlockSpec((tr, d), lambda i: (i, 0)), pl.BlockSpec((1, d), lambda i: (0, 0)),
                  pl.BlockSpec((None, 8, d), lambda i: (i // seg_tiles, 0, 0))],
        out_specs=pl.BlockSpec((tr, d), lambda i: (i, 0)),
        out_shape=jax.ShapeDtypeStruct((t, d), BF16),
        compiler_params=_params("parallel"),
    )(xs, g, mod)


def _resid_grad(dx, i, seg_tiles, yx_ref, gate_ref, dyx_ref, gsum_ref):
    dyx_ref[...] = (dx * gate_ref[2:3, :]).astype(BF16)

    @pl.when(i % seg_tiles == 0)
    def _():
        gsum_ref[...] = jnp.zeros_like(gsum_ref)

    gsum_ref[0:1, :] += jnp.sum(dx * yx_ref[...], axis=0, keepdims=True)


def _norm_bwd(xs, dh, dres, g, mod, tr, seg_tiles, name, res_tiles=None, out_tiles=None, below=None):
    t, d = xs.shape
    n_tiles = t // tr
    res_tiles = n_tiles if res_tiles is None else res_tiles
    out_tiles = n_tiles if out_tiles is None else out_tiles

    def body(x_ref, dh_ref, dres_ref, g_ref, mod_ref, *rest):
        i = pl.program_id(0)
        x = x_ref[...]
        r = lax.rsqrt(jnp.mean(x * x, axis=-1, keepdims=True) + RMS_EPS)
        xn = x * r
        dhv = dh_ref[...]
        gain = g_ref[...]
        one_scale = 1.0 + mod_ref[1:2, :]
        dxn = dhv * (gain * one_scale)
        dx = r * (dxn - xn * jnp.mean(dxn * xn, axis=-1, keepdims=True))
        if res_tiles == n_tiles:
            dx = dx + dres_ref[...]
        else:
            dx = dx + jnp.where(i < res_tiles, dres_ref[...], 0.0)
        if below is None:
            dx_ref, sum_ref = rest
        else:
            yx_ref, gate_ref, dx_ref, sum_ref, dyx_ref, gsum_ref = rest
            _resid_grad(dx, i, seg_tiles, yx_ref, gate_ref, dyx_ref, gsum_ref)
        if out_tiles == n_tiles:
            dx_ref[...] = dx
        else:
            @pl.when(i < out_tiles)
            def _():
                dx_ref[...] = dx

        @pl.when(i % seg_tiles == 0)
        def _():
            sum_ref[...] = jnp.zeros_like(sum_ref)

        sum_ref[0:1, :] += jnp.sum(dhv, axis=0, keepdims=True)
        sum_ref[1:2, :] += jnp.sum(dhv * (xn * gain), axis=0, keepdims=True)
        sum_ref[2:3, :] += jnp.sum(dhv * one_scale * xn, axis=0, keepdims=True)

    row = pl.BlockSpec((tr, d), lambda i: (i, 0))
    seg = pl.BlockSpec((None, 8, d), lambda i: (i // seg_tiles, 0, 0))
    in_specs = [row, row, pl.BlockSpec((tr, d), lambda i: (jnp.minimum(i, res_tiles - 1), 0)),
                pl.BlockSpec((1, d), lambda i: (0, 0)), seg]
    out_specs = [pl.BlockSpec((tr, d), lambda i: (jnp.minimum(i, out_tiles - 1), 0)), seg]
    out_shape = [jax.ShapeDtypeStruct((out_tiles * tr, d), F32), jax.ShapeDtypeStruct((mod.shape[0], 8, d), F32)]
    args = [xs, dh, dres, g, mod]
    if below is not None:
        in_specs += [row, seg]
        out_specs += [row, seg]
        out_shape += [jax.ShapeDtypeStruct((t, d), BF16), jax.ShapeDtypeStruct((below[1].shape[0], 8, d), F32)]
        args += list(below)
    return _pc(
        body, name=name, grid=(n_tiles,), in_specs=in_specs, out_specs=out_specs, out_shape=out_shape,
        compiler_params=_params("arbitrary"),
    )(*args)


def _loss_head(xs, target, g, yx, mod, tr):
    t, d = xs.shape

    def body(x_ref, t_ref, g_ref, yx_ref, gate_ref, loss_ref, dx_ref, dg_ref, dyx_ref, gsum_ref):
        i = pl.program_id(0)
        x = x_ref[...]
        r = lax.rsqrt(jnp.mean(x * x, axis=-1, keepdims=True) + RMS_EPS)
        xn = x * r
        gain = g_ref[...]
        err = xn * gain - t_ref[...]
        dy = err * (1.0 / d)
        dxn = dy * gain
        dx = r * (dxn - xn * jnp.mean(dxn * xn, axis=-1, keepdims=True))
        dx_ref[...] = dx
        _resid_grad(dx, i, t // tr, yx_ref, gate_ref, dyx_ref, gsum_ref)

        @pl.when(i == 0)
        def _():
            loss_ref[...] = jnp.zeros_like(loss_ref)
            dg_ref[...] = jnp.zeros_like(dg_ref)

        loss_ref[...] += 0.5 * jnp.sum(jnp.mean(err * err, axis=-1, keepdims=True))
        dg_ref[0:1, :] += jnp.sum(dy * xn, axis=0, keepdims=True)

    row = pl.BlockSpec((tr, d), lambda i: (i, 0))
    seg = pl.BlockSpec((None, 8, d), lambda i: (0, 0, 0))
    return _pc(
        body, name="loss_head", grid=(t // tr,),
        in_specs=[row, row, pl.BlockSpec((1, d), lambda i: (0, 0)), row, seg],
        out_specs=[pl.BlockSpec((8, LANES), lambda i: (0, 0)), row, pl.BlockSpec((8, d), lambda i: (0, 0)), row, seg],
        out_shape=[jax.ShapeDtypeStruct((8, LANES), F32), jax.ShapeDtypeStruct((t, d), F32),
                   jax.ShapeDtypeStruct((8, d), F32), jax.ShapeDtypeStruct((t, d), BF16),
                   jax.ShapeDtypeStruct((1, 8, d), F32)],
        compiler_params=_params("arbitrary"),
    )(xs, target, g, yx, mod)


def _proj_in(h, w, layer, width, name, blocks=None, dtype=F32):
    t, d = h.shape
    n8 = w.shape[-1]
    first, count = blocks if blocks is not None else (0, N_DEV)
    per_part = width // n8
    assert first % per_part == 0 and count % per_part == 0
    tm = _tile(t, 1152)

    def body(a_ref, b_ref, o_ref):
        a = a_ref[...]
        for s in range(per_part):
            o_ref[:, s * n8:(s + 1) * n8] = _dot(a, b_ref[s]).astype(dtype)

    return _pc(
        body, name=name, grid=(t // tm, count // per_part),
        in_specs=[pl.BlockSpec((tm, d), lambda i, j: (i, 0)),
                  pl.BlockSpec((per_part, None, d, n8), lambda i, j: (first // per_part + j, layer, 0, 0))],
        out_specs=pl.BlockSpec((None, tm, width), lambda i, j: (j, i, 0)),
        out_shape=jax.ShapeDtypeStruct((count // per_part, t, width), dtype),
        compiler_params=_params("parallel", "parallel"),
    )(h, w)


def _proj_out(z, w, res, mod, tm, seg_tiles, name, nxt=None):
    t, k = z.shape
    d = w.shape[1]

    def body(z_ref, w_ref, res_ref, mod_ref, *rest):
        yx = _dot(z_ref[...], w_ref[...])
        x = res_ref[...] + mod_ref[2:3, :] * yx
        if nxt is None:
            yx_ref, x_ref = rest
        else:
            g_ref, nmod_ref, yx_ref, x_ref, h_ref = rest
            r = lax.rsqrt(jnp.mean(x * x, axis=-1, keepdims=True) + RMS_EPS)
            h_ref[...] = (((x * r) * g_ref[...]) * (1.0 + nmod_ref[1:2, :]) + nmod_ref[0:1, :]).astype(BF16)
        yx_ref[...] = yx
        x_ref[...] = x

    tile = pl.BlockSpec((tm, d), lambda i: (i, 0))
    seg = pl.BlockSpec((None, 8, d), lambda i: (i // seg_tiles, 0, 0))
    in_specs = [pl.BlockSpec((tm, k), lambda i: (i, 0)), pl.BlockSpec((k, d), lambda i: (0, 0)), tile, seg]
    out_specs = [tile, tile]
    out_shape = [jax.ShapeDtypeStruct((t, d), F32), jax.ShapeDtypeStruct((t, d), F32)]
    args = [z, w, res, mod]
    if nxt is not None:
        in_specs += [pl.BlockSpec((1, d), lambda i: (0, 0)), seg]
        out_specs.append(tile)
        out_shape.append(jax.ShapeDtypeStruct((t, d), BF16))
        args += list(nxt)
    return _pc(
        body, name=name, grid=(t // tm,), in_specs=in_specs, out_specs=out_specs, out_shape=out_shape,
        compiler_params=_params("parallel"),
    )(*args)


def _proj_out_bwd(dyx, z, w, name):
    t, d = dyx.shape
    width = w.shape[0]
    tm = _tile(t, 768)
    nk = t // tm

    def body(a_ref, z_ref, w_ref, dz_ref, dw_ref, acc_ref):
        k = pl.program_id(0)

        @pl.when(k == 0)
        def _():
            acc_ref[...] = jnp.zeros_like(acc_ref)

        a = a_ref[...]
        dz_ref[...] = _dot(a, w_ref[...], 1, 1)
        acc_ref[...] += _dot(z_ref[...].T, a)

        @pl.when(k == nk - 1)
        def _():
            dw_ref[...] = acc_ref[...].astype(BF16)

    return _pc(
        body, name=name, grid=(nk,),
        in_specs=[pl.BlockSpec((tm, d), lambda k: (k, 0)), pl.BlockSpec((tm, width), lambda k: (k, 0)),
                  pl.BlockSpec((width, d), lambda k: (0, 0))],
        out_specs=[pl.BlockSpec((tm, width), lambda k: (k, 0)), pl.BlockSpec((width, d), lambda k: (0, 0))],
        out_shape=[jax.ShapeDtypeStruct((t, width), F32), jax.ShapeDtypeStruct((width, d), BF16)],
        scratch_shapes=[pltpu.VMEM((width, d), F32)],
        compiler_params=_params("arbitrary"),
    )(dyx, z, w)


def _proj_in_dh(dpre, w, layer, name, after=None):
    parts, t, width = dpre.shape
    d, n8 = w.shape[-2:]
    per_part = width // n8
    tm, tn = _tile(t, 768), _tile(d, 512)

    def body(a_ref, w_ref, *rest):
        o_ref = rest[-1]
        acc = None
        for p in range(parts):
            for s in range(per_part):
                term = _dot(a_ref[p, :, s * n8:(s + 1) * n8], w_ref[p * per_part + s], 1, 1)
                acc = term if acc is None else acc + term
        o_ref[...] = acc

    extra = [] if after is None else [after]
    return _pc(
        body, name=name, grid=(t // tm, d // tn),
        in_specs=[pl.BlockSpec((parts, tm, width), lambda i, j: (0, i, 0)),
                  pl.BlockSpec((N_DEV, None, tn, n8), lambda i, j: (0, layer, j, 0))] + [ANY] * len(extra),
        out_specs=pl.BlockSpec((tm, tn), lambda i, j: (i, j)),
        out_shape=jax.ShapeDtypeStruct((t, d), F32),
        compiler_params=_params("parallel", "parallel"),
    )(dpre, w, *extra)


def _transposed(a_ref):
    return a_ref[...].T


def _grad_w_in(h, dpre, n8, name):
    t, d = h.shape
    parts, _, width = dpre.shape
    per_part = width // n8
    tm, tk = _tile(d, 512), _tile(t, 1152)
    nk = t // tk

    def body(a_ref, b_ref, o_ref, acc_ref):
        k = pl.program_id(1)

        @pl.when(k == 0)
        def _():
            acc_ref[...] = jnp.zeros_like(acc_ref)

        at = _transposed(a_ref)
        for p in range(parts):
            r = _dot(at, b_ref[p])
            for s in range(per_part):
                acc_ref[p * per_part + s] += r[:, s * n8:(s + 1) * n8]

        @pl.when(k == nk - 1)
        def _():
            o_ref[...] = acc_ref[...].astype(BF16)

    return _pc(
        body, name=name, grid=(d // tm, nk),
        in_specs=[pl.BlockSpec((tk, tm), lambda i, k: (k, i)), pl.BlockSpec((parts, tk, width), lambda i, k: (0, k, 0))],
        out_specs=pl.BlockSpec((parts * per_part, tm, n8), lambda i, k: (0, i, 0)),
        out_shape=jax.ShapeDtypeStruct((parts * per_part, d, n8), BF16),
        scratch_shapes=[pltpu.VMEM((parts * per_part, tm, n8), F32)],
        compiler_params=_params("parallel", "arbitrary"),
    )(h, dpre)


def _shift(v, k):
    n = v.shape[0]
    return pltpu.roll(v, k % n, 0)


def _window_sum(v, win):
    s = v + _shift(v, 1)
    step = 1
    while 2 * step < win:
        s = _shift(s, step) + _shift(s, -step)
        step *= 2
    return s


def _window_count(base, seg_len, win, shape):
    t = base + lax.broadcasted_iota(jnp.int32, shape, 0)
    hi = jnp.minimum(t + win // 2, seg_len)
    lo = jnp.maximum(t - win // 2, 0)
    return (hi - lo).astype(F32)


def _pad_offsets(segs):
    return [HALO * (s + 1) + st for s, (st, _) in enumerate(segs)]


def _for_chunks(segs, fn):
    offs = _pad_offsets(segs)
    for s, (st, ln) in enumerate(segs):
        def step(ci, carry, s=s, st=st, ln=ln):
            fn(s, st, ln, offs[s], pl.multiple_of(ci * CHUNK, CHUNK))
            return carry
        lax.fori_loop(0, ln // CHUNK, step, 0)


def _pool_fwd(pre, w_grp, scale, segs, name):
    _, t, width = pre.shape
    grp = width // len(POOL_WINDOWS)
    padded = t + HALO * (len(segs) + 1)

    def group(win, pre_ref, w_ref, sc_ref, z_ref, diff_ref, pad_ref):
        pad_ref[...] = jnp.zeros_like(pad_ref)

        def fill(s, st, ln, off, b):
            pad_ref[pl.ds(off + b, CHUNK), :] = pre_ref[0, pl.ds(st + b, CHUNK), :]

        _for_chunks(segs, fill)

        def mix(s, st, ln, off, b):
            ext = pad_ref[pl.ds(off - HALO + b, CHUNK + 2 * HALO), :]
            total = _window_sum(ext, win)[HALO:HALO + CHUNK]
            u = pre_ref[0, pl.ds(st + b, CHUNK), :]
            diff = (total / _window_count(b, ln, win, u.shape) - u).astype(BF16)
            mixed = _dot(diff, w_ref[...])
            gate = _silu(pre_ref[1, pl.ds(st + b, CHUNK), :])
            z_ref[pl.ds(st + b, CHUNK), :] = (mixed * sc_ref[...] * gate).astype(BF16)
            diff_ref[pl.ds(st + b, CHUNK), :] = diff

        _for_chunks(segs, mix)

    def body(pre_ref, w_ref, sc_ref, z_ref, diff_ref, pad_ref):
        gi = pl.program_id(0)
        for widx, win in enumerate(POOL_WINDOWS):
            @pl.when(gi == widx)
            def _(win=win):
                group(win, pre_ref, w_ref, sc_ref, z_ref, diff_ref, pad_ref)

    col = pl.BlockSpec((t, grp), lambda g: (0, g))
    return _pc(
        body, name=name, grid=(len(POOL_WINDOWS),),
        in_specs=[pl.BlockSpec((2, t, grp), lambda g: (0, 0, g)), pl.BlockSpec((None, grp, grp), lambda g: (g, 0, 0)),
                  pl.BlockSpec((1, grp), lambda g: (0, g))],
        out_specs=[col, col],
        out_shape=[jax.ShapeDtypeStruct((t, width), BF16), jax.ShapeDtypeStruct((t, width), BF16)],
        scratch_shapes=[pltpu.VMEM((padded, grp), F32)],
        compiler_params=_params("parallel"),
    )(pre, w_grp, scale)


def _pool_bwd(dz, diff, pre, w_grp, scale, segs, name):
    _, t, width = pre.shape
    grp = width // len(POOL_WINDOWS)
    padded = t + HALO * (len(segs) + 1)

    def group(win, dz_ref, diff_ref, pre_ref, w_ref, sc_ref, dpre_ref, dw_ref, dsc_ref, pad_ref, dd_ref):
        pad_ref[...] = jnp.zeros_like(pad_ref)
        dw_ref[...] = jnp.zeros_like(dw_ref)
        dsc_ref[...] = jnp.zeros_like(dsc_ref)

        def first(s, st, ln, off, b):
            rows = pl.ds(st + b, CHUNK)
            diff_v = diff_ref[rows, :]
            mixed = _dot(diff_v, w_ref[...])
            g = pre_ref[1, rows, :]
            sg = _silu(g)
            dzv = dz_ref[rows, :]
            dmixed = (dzv * sc_ref[...] * sg).astype(BF16)
            dsc_ref[...] += jnp.sum(dzv * mixed * sg, axis=0, keepdims=True)
            dpre_ref[1, rows, :] = (dzv * mixed * sc_ref[...] * _dsilu(g)).astype(BF16)
            ddiff = _dot(dmixed, w_ref[...], 1, 1)
            dw_ref[...] += _dot(diff_v, dmixed, 0, 0)
            dd_ref[rows, :] = ddiff
            pad_ref[pl.ds(off + b, CHUNK), :] = ddiff / _window_count(b, ln, win, ddiff.shape)

        _for_chunks(segs, first)

        def second(s, st, ln, off, b):
            rows = pl.ds(st + b, CHUNK)
            ext = pad_ref[pl.ds(off - HALO + b, CHUNK + 2 * HALO), :]
            total = _shift(_window_sum(ext, win), -1)[HALO:HALO + CHUNK]
            dpre_ref[0, rows, :] = (total - dd_ref[rows, :]).astype(BF16)

        _for_chunks(segs, second)

    def body(dz_ref, diff_ref, pre_ref, w_ref, sc_ref, dpre_ref, dw_ref, dsc_ref, pad_ref, dd_ref):
        gi = pl.program_id(0)
        for widx, win in enumerate(POOL_WINDOWS):
            @pl.when(gi == widx)
            def _(win=win):
                group(win, dz_ref, diff_ref, pre_ref, w_ref, sc_ref, dpre_ref, dw_ref, dsc_ref, pad_ref, dd_ref)

    col = pl.BlockSpec((t, grp), lambda g: (0, g))
    both = pl.BlockSpec((2, t, grp), lambda g: (0, 0, g))
    wspec = pl.BlockSpec((None, grp, grp), lambda g: (g, 0, 0))
    sspec = pl.BlockSpec((1, grp), lambda g: (0, g))
    return _pc(
        body, name=name, grid=(len(POOL_WINDOWS),),
        in_specs=[col, col, both, wspec, sspec],
        out_specs=[both, wspec, sspec],
        out_shape=[jax.ShapeDtypeStruct((2, t, width), BF16), jax.ShapeDtypeStruct((len(POOL_WINDOWS), grp, grp), F32),
                   jax.ShapeDtypeStruct((1, width), F32)],
        scratch_shapes=[pltpu.VMEM((padded, grp), F32), pltpu.VMEM((t, grp), F32)],
        compiler_params=_params("parallel"),
    )(dz, diff, pre, w_grp, scale)


def _conv_fwd(pre, dw, db, name):
    _, t, width = pre.shape
    cb = LANES
    segs = [(0, t)]

    def body(pre_ref, dw_ref, db_ref, z_ref, pad_ref):
        pad_ref[...] = jnp.zeros_like(pad_ref)

        def fill(s, st, ln, off, b):
            rows = pl.ds(b, CHUNK)
            pad_ref[pl.ds(off + b, CHUNK), :] = pre_ref[1, rows, :] * pre_ref[2, rows, :]

        _for_chunks(segs, fill)

        def mix(s, st, ln, off, b):
            rows = pl.ds(b, CHUNK)
            ext = pad_ref[pl.ds(off - HALO + b, CHUNK + 2 * HALO), :]
            conv = (dw_ref[0:1, :] * _shift(ext, 1) + dw_ref[1:2, :] * ext + dw_ref[2:3, :] * _shift(ext, -1))
            conv = conv[HALO:HALO + CHUNK] + db_ref[...]
            y = pre_ref[0, rows, :] * conv
            z_ref[rows, :] = (y * _silu(pre_ref[3, rows, :])).astype(BF16)

        _for_chunks(segs, mix)

    return _pc(
        body, name=name, grid=(width // cb,),
        in_specs=[pl.BlockSpec((4, t, cb), lambda j: (0, 0, j)), pl.BlockSpec((8, cb), lambda j: (0, j)),
                  pl.BlockSpec((1, cb), lambda j: (0, j))],
        out_specs=pl.BlockSpec((t, cb), lambda j: (0, j)),
        out_shape=jax.ShapeDtypeStruct((t, width), BF16),
        scratch_shapes=[pltpu.VMEM((t + 2 * HALO, cb), F32)],
        compiler_params=_params("parallel"),
    )(pre, dw, db)


def _conv_bwd(dz, pre, dw, db, name):
    _, t, width = pre.shape
    cb = LANES
    segs = [(0, t)]

    def body(dz_ref, pre_ref, dw_ref, db_ref, dpre_ref, ddw_ref, ddb_ref, pad_a, pad_c):
        pad_a[...] = jnp.zeros_like(pad_a)
        pad_c[...] = jnp.zeros_like(pad_c)
        ddw_ref[...] = jnp.zeros_like(ddw_ref)
        ddb_ref[...] = jnp.zeros_like(ddb_ref)

        def fill(s, st, ln, off, b):
            rows = pl.ds(b, CHUNK)
            pad_a[pl.ds(off + b, CHUNK), :] = pre_ref[1, rows, :] * pre_ref[2, rows, :]

        _for_chunks(segs, fill)

        def first(s, st, ln, off, b):
            rows = pl.ds(b, CHUNK)
            ext = pad_a[pl.ds(off - HALO + b, CHUNK + 2 * HALO), :]
            prev, nxt = _shift(ext, 1)[HALO:HALO + CHUNK], _shift(ext, -1)[HALO:HALO + CHUNK]
            here = ext[HALO:HALO + CHUNK]
            conv = dw_ref[0:1, :] * prev + dw_ref[1:2, :] * here + dw_ref[2:3, :] * nxt + db_ref[...]
            bg, g = pre_ref[0, rows, :], pre_ref[3, rows, :]
            dzv = dz_ref[rows, :]
            dy = dzv * _silu(g)
            dpre_ref[3, rows, :] = (dzv * (bg * conv) * _dsilu(g)).astype(BF16)
            dpre_ref[0, rows, :] = (dy * conv).astype(BF16)
            dconv = dy * bg
            pad_c[pl.ds(off + b, CHUNK), :] = dconv
            ddw_ref[0:1, :] += jnp.sum(dconv * prev, axis=0, keepdims=True)
            ddw_ref[1:2, :] += jnp.sum(dconv * here, axis=0, keepdims=True)
            ddw_ref[2:3, :] += jnp.sum(dconv * nxt, axis=0, keepdims=True)
            ddb_ref[0:1, :] += jnp.sum(dconv, axis=0, keepdims=True)

        _for_chunks(segs, first)

        def second(s, st, ln, off, b):
            rows = pl.ds(b, CHUNK)
            ext = pad_c[pl.ds(off - HALO + b, CHUNK + 2 * HALO), :]
            da = (dw_ref[0:1, :] * _shift(ext, -1) + dw_ref[1:2, :] * ext + dw_ref[2:3, :] * _shift(ext, 1))
            da = da[HALO:HALO + CHUNK]
            dpre_ref[1, rows, :] = (da * pre_ref[2, rows, :]).astype(BF16)
            dpre_ref[2, rows, :] = (da * pre_ref[1, rows, :]).astype(BF16)

        _for_chunks(segs, second)

    quad = pl.BlockSpec((4, t, cb), lambda j: (0, 0, j))
    rows8 = pl.BlockSpec((8, cb), lambda j: (0, j))
    return _pc(
        body, name=name, grid=(width // cb,),
        in_specs=[pl.BlockSpec((t, cb), lambda j: (0, j)), quad, rows8, pl.BlockSpec((1, cb), lambda j: (0, j))],
        out_specs=[quad, rows8, rows8],
        out_shape=[jax.ShapeDtypeStruct((4, t, width), BF16), jax.ShapeDtypeStruct((8, width), F32),
                   jax.ShapeDtypeStruct((8, width), F32)],
        scratch_shapes=[pltpu.VMEM((t + 2 * HALO, cb), F32), pltpu.VMEM((t + 2 * HALO, cb), F32)],
        compiler_params=_params("parallel"),
    )(dz, pre, dw, db)


PAIR_TILES = 2 * WIN_ROWS - 2


def _pair_geometry():
    lane = lax.broadcasted_iota(jnp.int32, (GRID_W, LANES), 1)
    qcol = lax.broadcasted_iota(jnp.int32, (GRID_W, LANES), 0)
    low = lane < GRID_W
    kcol = jnp.where(low, lane, lane - GRID_W)
    start = jnp.clip(qcol - WIN_COLS // 2, 0, GRID_W - WIN_COLS)
    inside = (kcol >= start) & (kcol < start + WIN_COLS)
    return low, inside


def _bias_tiles(rpb_ref, rows_ref, tiles_ref, inside):
    for h in range(2):
        rows = rpb_ref[h]
        rows_ref[h] = (pltpu.roll(rows, LANES - (WIN_COLS - 1), 1)
                       + pltpu.roll(pltpu.roll(rows, GRID_W - (WIN_COLS - 1), 1), 2 * WIN_ROWS - 1, 0))
        for t in range(PAIR_TILES):
            both = jnp.broadcast_to(rows_ref[h, t:t + 1, :], (GRID_W, LANES))
            tiles_ref[h, t] = jnp.where(inside, pltpu.roll(both, 0, 1, stride=1, stride_axis=0), MASKED)


def _bias_tiles_grad(dtiles_ref, drpb_ref):
    n = PAIR_TILES * GRID_W
    qcol = lax.broadcasted_iota(jnp.int32, (n, LANES), 0) & (GRID_W - 1)
    lane = lax.broadcasted_iota(jnp.int32, (1, LANES), 1)
    zero = jnp.zeros((1, LANES), F32)
    for h in range(2):
        v = pltpu.roll(dtiles_ref[h].reshape(n, LANES), WIN_COLS - 1, 1)
        for bit in range(6):
            v = jnp.where((qcol >> bit) & 1 == 1, pltpu.roll(v, LANES - (1 << bit), 1), v)
        sums = [jnp.sum(v[t * GRID_W:(t + 1) * GRID_W], axis=0, keepdims=True) for t in range(PAIR_TILES)]
        for r in range(2 * WIN_ROWS):
            here = sums[r] if r < PAIR_TILES else zero
            prev = pltpu.roll(sums[r - 1], GRID_W, 1) if 1 <= r <= PAIR_TILES else zero
            drpb_ref[h, r:r + 1, :] = jnp.where(lane < 2 * WIN_COLS - 1, here + prev, 0.0)


def _attn_rows(r, n_rows):
    first = jnp.clip(r - WIN_ROWS // 2, 0, n_rows - WIN_ROWS)
    return first, first - r + WIN_ROWS - 1


def _softmax(s_loc, s_ctx):
    m = jnp.maximum(jnp.max(s_loc, axis=-1, keepdims=True), jnp.max(s_ctx, axis=-1, keepdims=True))
    e_loc, e_ctx = jnp.exp(s_loc - m), jnp.exp(s_ctx - m)
    inv = 1.0 / (jnp.sum(e_loc, axis=-1, keepdims=True) + jnp.sum(e_ctx, axis=-1, keepdims=True))
    return e_loc * inv, e_ctx * inv


def _pair_bias(tiles_ref, j):
    return jnp.concatenate(
        [jnp.concatenate([tiles_ref[h, j + 2 * m] for m in range(WIN_ROWS // 2)], axis=1) for h in range(2)], axis=0)


ROWS_PER_STEP = 8
ROWS_PER_STEP_BWD = 4


def _by_head(tile, low):
    zero = jnp.zeros_like(tile)
    return jnp.concatenate([jnp.where(low, tile, zero), jnp.where(low, zero, tile)], axis=0)


def _merge_heads(stacked, low):
    return jnp.where(low, stacked[:GRID_W], stacked[GRID_W:])


def _attn_items(step, n_rows, q_ref, low, per_step):
    items = []
    for u in range(per_step):
        r = step * per_step + u
        first, j = _attn_rows(r, n_rows)
        rows = pl.ds(pl.multiple_of(r * GRID_W, GRID_W), GRID_W)
        keys = pl.ds(pl.multiple_of(first * GRID_W, GRID_W), WIN_ROWS * GRID_W)
        q = (q_ref[rows, :].astype(F32) * HEAD_DIM ** -0.5).astype(BF16)
        items.append((rows, keys, j, _by_head(q, low)))
    return items


def _attn_fwd(qkv, gate, rpb, seq):
    _, t, width = qkv.shape
    n_rows = seq // GRID_W
    n_ctx = t - seq
    blk = WIN_ROWS * GRID_W

    def body(q_ref, k_ref, v_ref, g_ref, rpb_ref, z_ref, o_ref, rows_ref, tiles_ref):
        low, inside = _pair_geometry()
        _bias_tiles(rpb_ref, rows_ref, tiles_ref, inside)
        ctx = pl.ds(seq, n_ctx)

        def step(i, carry):
            items = _attn_items(i, n_rows, q_ref, low, ROWS_PER_STEP)
            k_ctx, v_ctx = k_ref[ctx, :], v_ref[ctx, :]
            scores = [(_dot(q, k_ref[keys, :], 1, 1) + _pair_bias(tiles_ref, j), _dot(q, k_ctx, 1, 1))
                      for _, keys, j, q in items]
            probs = [_softmax(s_loc, s_ctx) for s_loc, s_ctx in scores]
            outs = [_dot(p_loc.astype(BF16), v_ref[keys, :]) + _dot(p_ctx.astype(BF16), v_ctx)
                    for (_, keys, _, _), (p_loc, p_ctx) in zip(items, probs)]
            for (rows, _, _, _), out in zip(items, outs):
                o = _merge_heads(out, low)
                o_ref[rows, :] = o
                z_ref[rows, :] = (o * _silu(g_ref[rows, :])).astype(BF16)
            return carry

        lax.fori_loop(0, n_rows // ROWS_PER_STEP, step, 0)

    def part(p):
        return pl.BlockSpec((None, t, LANES), lambda h: (p, 0, h))

    out = pl.BlockSpec((seq, LANES), lambda h: (0, h))
    return _pc(
        body, name="attn_fwd", grid=(width // LANES,),
        in_specs=[part(0), part(1), part(2), part(0), pl.BlockSpec((2, 2 * WIN_ROWS, LANES), lambda h: (h, 0, 0))],
        out_specs=[out, out],
        out_shape=[jax.ShapeDtypeStruct((seq, width), BF16), jax.ShapeDtypeStruct((seq, width), F32)],
        scratch_shapes=[pltpu.VMEM((2, 2 * WIN_ROWS, LANES), F32), pltpu.VMEM((2, PAIR_TILES, GRID_W, LANES), F32)],
        compiler_params=_params("parallel"),
    )(qkv, qkv, qkv, gate, rpb)


def _attn_bwd(qkv, gate, o, dz, rpb, seq):
    _, t, width = qkv.shape
    n_rows = seq // GRID_W
    n_ctx = t - seq
    blk = WIN_ROWS * GRID_W
    heads = 2 * width // LANES

    def body(q_ref, k_ref, v_ref, g_ref, o_ref, dz_ref, rpb_ref, dpre_ref, drpb_ref,
             rows_ref, tiles_ref, dtiles_ref, dk_ref, dv_ref):
        low, inside = _pair_geometry()
        _bias_tiles(rpb_ref, rows_ref, tiles_ref, inside)
        dtiles_ref[...] = jnp.zeros_like(dtiles_ref)
        dk_ref[...] = jnp.zeros_like(dk_ref)
        dv_ref[...] = jnp.zeros_like(dv_ref)
        ctx = pl.ds(seq, n_ctx)

        def step(i, carry):
            items = _attn_items(i, n_rows, q_ref, low, ROWS_PER_STEP_BWD)
            k_ctx, v_ctx = k_ref[ctx, :], v_ref[ctx, :]
            d_outs = []
            for rows, _, _, _ in items:
                g = g_ref[rows, :]
                dzv = dz_ref[rows, :]
                dpre_ref[3, rows, :] = (dzv * o_ref[rows, :] * _dsilu(g)).astype(BF16)
                d_outs.append(_by_head((dzv * _silu(g)).astype(BF16), low))
            scores = [(_dot(q, k_ref[keys, :], 1, 1) + _pair_bias(tiles_ref, j), _dot(q, k_ctx, 1, 1))
                      for _, keys, j, q in items]
            dprobs = [(_dot(d_o, v_ref[keys, :], 1, 1), _dot(d_o, v_ctx, 1, 1))
                      for (_, keys, _, _), d_o in zip(items, d_outs)]
            probs = [_softmax(s_loc, s_ctx) for s_loc, s_ctx in scores]
            dscores = []
            for (p_loc, p_ctx), (dp_loc, dp_ctx) in zip(probs, dprobs):
                delta = (jnp.sum(p_loc * dp_loc, axis=-1, keepdims=True)
                         + jnp.sum(p_ctx * dp_ctx, axis=-1, keepdims=True))
                dscores.append((p_loc * (dp_loc - delta), p_ctx * (dp_ctx - delta)))
            dqs = [_dot(ds_loc.astype(BF16), k_ref[keys, :]) + _dot(ds_ctx.astype(BF16), k_ctx)
                   for (_, keys, _, _), (ds_loc, ds_ctx) in zip(items, dscores)]
            for (rows, _, _, _), dq in zip(items, dqs):
                dpre_ref[0, rows, :] = (_merge_heads(dq, low) * HEAD_DIM ** -0.5).astype(BF16)
            for (_, keys, j, q), d_o, (p_loc, p_ctx), (ds_loc, ds_ctx) in zip(items, d_outs, probs, dscores):
                dk_ref[keys, :] += _dot(ds_loc.astype(BF16), q, 0, 0)
                dk_ref[ctx, :] += _dot(ds_ctx.astype(BF16), q, 0, 0)
                dv_ref[keys, :] += _dot(p_loc.astype(BF16), d_o, 0, 0)
                dv_ref[ctx, :] += _dot(p_ctx.astype(BF16), d_o, 0, 0)
                for h in range(2):
                    for m in range(WIN_ROWS // 2):
                        dtiles_ref[h, j + 2 * m] += ds_loc[h * GRID_W:(h + 1) * GRID_W, m * LANES:(m + 1) * LANES]
            return carry

        lax.fori_loop(0, n_rows // ROWS_PER_STEP_BWD, step, 0)
        dpre_ref[1] = dk_ref[...].astype(BF16)
        dpre_ref[2] = dv_ref[...].astype(BF16)
        dpre_ref[0, ctx, :] = jnp.zeros((n_ctx, LANES), BF16)
        dpre_ref[3, ctx, :] = jnp.zeros((n_ctx, LANES), BF16)
        _bias_tiles_grad(dtiles_ref, drpb_ref)

    def part(p):
        return pl.BlockSpec((None, t, LANES), lambda h: (p, 0, h))

    lat = pl.BlockSpec((seq, LANES), lambda h: (0, h))
    rspec = pl.BlockSpec((2, 2 * WIN_ROWS, LANES), lambda h: (h, 0, 0))
    tiles = pltpu.VMEM((2, PAIR_TILES, GRID_W, LANES), F32)
    return _pc(
        body, name="attn_bwd", grid=(width // LANES,),
        in_specs=[part(0), part(1), part(2), part(0), lat, lat, rspec],
        out_specs=[pl.BlockSpec((4, t, LANES), lambda h: (0, 0, h)), rspec],
        out_shape=[jax.ShapeDtypeStruct((4, t, width), BF16), jax.ShapeDtypeStruct((heads, 2 * WIN_ROWS, LANES), F32)],
        scratch_shapes=[pltpu.VMEM((2, 2 * WIN_ROWS, LANES), F32), tiles, tiles,
                        pltpu.VMEM((t, LANES), F32), pltpu.VMEM((t, LANES), F32)],
        compiler_params=_params("parallel"),
    )(qkv, qkv, qkv, gate, o, dz, rpb)


def _adam_update(w, m, v, g):
    m2 = ADAM_B1 * m + (1.0 - ADAM_B1) * g
    v2 = ADAM_B2 * v + (1.0 - ADAM_B2) * (g * g)
    m_hat = m2 / (1.0 - ADAM_B1 ** ADAM_STEP)
    v_hat = v2 / (1.0 - ADAM_B2 ** ADAM_STEP)
    return -ADAM_LR * (m_hat / (jnp.sqrt(v_hat) + ADAM_EPS) + ADAM_WD * w), m2, v2


def _adamw(w, m, v, parts, name):
    rows, cols = w.shape
    tr = _tile(rows, max(8, ADAM_TILE_ELEMS // cols), 8)
    n_parts = len(parts)

    def body(*refs):
        w_ref, m_ref, v_ref = refs[:3]
        part_refs = refs[3:3 + n_parts]
        g_ref, d_ref, nm_ref, nv_ref = refs[3 + n_parts:]
        g = part_refs[0][...].astype(F32)
        for p in part_refs[1:]:
            g = g + p[...].astype(F32)
        g_ref[...] = g
        d_ref[...], nm_ref[...], nv_ref[...] = _adam_update(w_ref[...], m_ref[...], v_ref[...], g)

    tile = pl.BlockSpec((tr, cols), lambda i: (i, 0))
    in_specs, args = [tile, tile, tile], [w, m, v]
    for p in parts:
        if isinstance(p, tuple):
            arr, k = p
            in_specs.append(pl.BlockSpec((None, tr, cols), lambda i, k=k: (k, i, 0)))
            args.append(arr)
        else:
            in_specs.append(tile)
            args.append(p)
    shape = jax.ShapeDtypeStruct((rows, cols), F32)
    return _pc(
        body, name=name, grid=(rows // tr,), in_specs=in_specs, out_specs=[tile] * 4, out_shape=[shape] * 4,
        compiler_params=_params("parallel"),
    )(*args)


def _adamw_layers(w, m, v, landed, name):
    n_layers, rows, cols = w.shape
    tr = _tile(rows, max(8, ADAM_TILE_ELEMS // cols), 8)

    def body(*refs):
        w_ref, m_ref, v_ref = refs[:3]
        part_refs = refs[3:3 + n_layers * N_DEV]
        g_ref, d_ref, nm_ref, nv_ref = refs[3 + n_layers * N_DEV:]
        layer = pl.program_id(0)
        g = None
        for l in range(n_layers):
            s = part_refs[l * N_DEV][...].astype(F32)
            for p in part_refs[l * N_DEV + 1:(l + 1) * N_DEV]:
                s = s + p[...].astype(F32)
            g = s if g is None else jnp.where(layer == l, s, g)
        g_ref[...] = g
        d_ref[...], nm_ref[...], nv_ref[...] = _adam_update(w_ref[...], m_ref[...], v_ref[...], g)

    tile = pl.BlockSpec((None, tr, cols), lambda l, i: (l, i, 0))
    in_specs, args = [tile, tile, tile], [w, m, v]
    for l, arr in enumerate(landed):
        for k in range(N_DEV):
            in_specs.append(pl.BlockSpec((None, tr, cols), lambda ll, i, l=l, k=k: (k, jnp.where(ll == l, i, 0), 0)))
            args.append(arr)
    shape = jax.ShapeDtypeStruct(w.shape, F32)
    return _pc(
        body, name=name, grid=(n_layers, rows // tr), in_specs=in_specs, out_specs=[tile] * 4, out_shape=[shape] * 4,
        compiler_params=_params("arbitrary", "arbitrary"),
    )(*args)


def _adamw_small(states, grads):
    sources, makers = grads
    n, ns = len(states), len(sources)

    def body(*refs):
        src = refs[:ns]
        ins = refs[ns:ns + 3 * n]
        outs = refs[ns + 3 * n:]
        for k in range(n):
            w_ref, m_ref, v_ref = ins[3 * k:3 * k + 3]
            g = makers[k](*src)
            outs[4 * k][...] = g
            outs[4 * k + 1][...], outs[4 * k + 2][...], outs[4 * k + 3][...] = _adam_update(
                w_ref[...], m_ref[...], v_ref[...], g)

    flat = [a for s in states for a in s]
    vmem = pl.BlockSpec(memory_space=pltpu.VMEM)
    res = _pc(
        body, name="adamw_small",
        in_specs=[vmem] * (ns + 3 * n), out_specs=[vmem] * (4 * n),
        out_shape=[jax.ShapeDtypeStruct(s[0].shape, F32) for s in states for _ in range(4)],
        compiler_params=pltpu.CompilerParams(vmem_limit_bytes=VMEM_LIMIT),
    )(*sources, *flat)
    return [res[4 * k:4 * k + 4] for k in range(n)]


def _rows128(a):
    flat = a.reshape(-1)
    pad = (-flat.shape[0]) % LANES
    if pad:
        flat = jnp.concatenate([flat, jnp.zeros((pad,), flat.dtype)])
    return flat.reshape(-1, LANES)


def _pad_rows(a, mult=8):
    pad = (-a.shape[0]) % mult
    if pad:
        a = jnp.concatenate([a, jnp.zeros((pad,) + a.shape[1:], a.dtype)], axis=0)
    return a


def kernel(x, c, ctx, c_ctx, norm_g, ada_w, ada_b, pool_w_in, pool_w_grp, pool_scale, pool_w_out, na_w_in, na_rpb, na_w_out, conv_w_in, conv_dw, conv_db, conv_w_out, final_g, loss_target, m_c_ctx, m_norm_g, m_ada_w, m_ada_b, m_pool_w_in, m_pool_w_grp, m_pool_scale, m_pool_w_out, m_na_w_in, m_na_rpb, m_na_w_out, m_conv_w_in, m_conv_dw, m_conv_db, m_conv_w_out, m_final_g, v_c_ctx, v_norm_g, v_ada_w, v_ada_b, v_pool_w_in, v_pool_w_grp, v_pool_scale, v_pool_w_out, v_na_w_in, v_na_rpb, v_na_w_out, v_conv_w_in, v_conv_dw, v_conv_db, v_conv_w_out, v_final_g):
    xi, yi, ci = _my_place()
    me = 4 * xi + 2 * yi + ci
    seq, d = x.shape[1], x.shape[2]
    n_ctx = ctx.shape[1]
    t_all = seq + n_ctx
    width = d
    depth = norm_g.shape[0]
    nb = ada_w.shape[2]
    shard = width // N_DEV
    d_rows = d // LANES
    assert seq % CHUNK == 0 and n_ctx % CHUNK == 0 and (seq // GRID_W) % ROWS_PER_STEP == 0 and seq >= WIN_ROWS * GRID_W
    tr = math.gcd(math.gcd(seq, n_ctx), 256)
    x_tiles = seq // tr
    tr_lat = math.gcd(seq, 512)
    lat_tiles = seq // tr_lat

    n_pool = pool_scale.shape[0]
    n_grp = pool_w_grp.shape[1]
    grp = width // n_grp

    small_in = _pad_rows(jnp.concatenate([_rows128(c), pool_scale, conv_dw[0], conv_db], axis=0))
    got = _gather_small(small_in, "gather_inputs")
    r0 = d_rows
    c_all = got[:, :r0].reshape(N_DEV, d)
    scale_full = got[:, r0:r0 + n_pool].transpose(1, 0, 2).reshape(n_pool, width)
    r1 = r0 + n_pool
    taps_full = _pad_rows(got[:, r1:r1 + 3].transpose(1, 0, 2).reshape(3, width))
    bias_full = got[:, r1 + 3:r1 + 4].transpose(1, 0, 2).reshape(1, width)

    cond = jnp.concatenate([c_all, c_ctx[None], jnp.zeros((7, d), F32)], axis=0)
    bias_mine = lax.dynamic_slice(ada_b, (0, me * nb), (depth, nb))
    mod_mine = _mod_fwd(cond, ada_w, bias_mine)
    by_example = jnp.stack([mod_mine[:, :N_DEV].transpose(1, 0, 2),
                            jnp.broadcast_to(mod_mine[:, N_DEV][None], (N_DEV, depth, nb))], axis=2)
    mod_all = _gather_small(by_example.reshape(N_DEV, -1, LANES), "gather_mod", per_dest=True)
    mod_all = mod_all.reshape(N_DEV, depth, 2, nb).transpose(1, 2, 0, 3).reshape(depth, 2, 3, d)
    mod_all = jnp.pad(mod_all, ((0, 0), (0, 0), (0, 5), (0, 0)))
    mods = [mod_all[i] if i < 2 else mod_all[i, :1] for i in range(depth)]

    layer_weights = [[pool_w_in[0], pool_w_grp[0], pool_w_out[0]], [na_w_in[0], na_w_out[0]],
                     [conv_w_in[0], conv_w_out[0]], [pool_w_in[1], pool_w_grp[1], pool_w_out[1]]]
    slot = {(i, t): n for n, (i, t) in enumerate((i, t) for i, ws in enumerate(layer_weights) for t in range(len(ws)))}
    two_level = [(0, 0), (1, 0)]
    weights_sent = _exchange_start(
        [w.astype(BF16) for ws in layer_weights for w in ws], False, mod_all, "weights_start",
        peers=[CHIP_PEERS if key in two_level else ALL_PEERS for key in slot])
    token = weights_sent[-1]

    def landed_weight(i, t, after):
        return _exchange_wait(weights_sent, False, after, f"weights_wait{i}_{t}", which=[slot[i, t]])[0]

    def handed_on(i, after):
        half = [landed_weight(i, 0, after)]
        rest = _forward_start(half, after, f"weights_forward{i}")
        return _forward_wait(rest, rest[-1], f"weights_forward_wait{i}")[0][:, None]

    def as_in(w):
        return w[:, None]

    def as_grp(w):
        return w.transpose(1, 0, 2, 3).reshape(n_grp, grp, grp)

    def as_out(w):
        return w.reshape(width, d)

    both = [(0, seq), (seq, n_ctx)]
    latent = [(0, seq)]

    def grp_slots(g):
        return g.reshape(n_grp, N_DEV, grp // N_DEV, grp).transpose(1, 0, 2, 3).reshape(N_DEV, -1, grp).astype(BF16)

    def send_grads(i, grads):
        return _exchange_start(grads, True, jnp.zeros((8, LANES), F32), f"grads_start{i}")

    xs0 = jnp.concatenate([x[0], ctx[0]], axis=0)
    h0 = _norm_fwd(xs0, norm_g[0:1] + token[0, 0], mods[0], tr, x_tiles, "norm_fwd0")
    pool_in_w0 = handed_on(0, h0)
    pre0 = _proj_in(h0, pool_in_w0, 0, width, "proj_in0")
    pool_grp_w0 = as_grp(landed_weight(0, 1, pre0))
    z0, diff0 = _pool_fwd(pre0, pool_grp_w0, scale_full[0:1], both, "pool_fwd0")
    pool_out_w0 = as_out(landed_weight(0, 2, z0))
    yx0, xs1, h1 = _proj_out(z0, pool_out_w0, xs0, mods[0], tr, x_tiles, "proj_out0", nxt=(norm_g[1:2], mods[1]))

    na_in_w = handed_on(1, h1)
    per_part = width // na_w_in.shape[2]
    qkv1 = _proj_in(h1, na_in_w, 0, width, "proj_in1_qkv", blocks=(0, 3 * per_part), dtype=BF16)
    gpre1 = _proj_in(h1, na_in_w, 0, width, "proj_in1_gate", blocks=(3 * per_part, per_part))
    rpb_rows = jnp.pad(na_rpb[0], ((0, 0), (0, 2 * WIN_ROWS - na_rpb.shape[2]), (0, LANES - na_rpb.shape[3])))
    z1, o1 = _attn_fwd(qkv1, gpre1, rpb_rows, seq)
    na_out_w = as_out(landed_weight(1, 1, z1))
    yx1, x2, h2 = _proj_out(z1, na_out_w, xs1, mods[1], tr_lat, lat_tiles, "proj_out1", nxt=(norm_g[2:3], mods[2]))

    conv_in_w = as_in(landed_weight(2, 0, h2))
    pre2 = _proj_in(h2, conv_in_w, 0, width, "proj_in2")
    z2 = _conv_fwd(pre2, taps_full, bias_full, "conv_fwd")
    conv_out_w = as_out(landed_weight(2, 1, z2))
    yx2, x3, h3 = _proj_out(z2, conv_out_w, x2, mods[2], tr_lat, lat_tiles, "proj_out2", nxt=(norm_g[3:4], mods[3]))

    pool_in_w3 = as_in(landed_weight(3, 0, h3))
    pre3 = _proj_in(h3, pool_in_w3, 0, width, "proj_in3")
    pool_grp_w3 = as_grp(landed_weight(3, 1, pre3))
    z3, diff3 = _pool_fwd(pre3, pool_grp_w3, scale_full[1:2], latent, "pool_fwd3")
    pool_out_w3 = as_out(landed_weight(3, 2, z3))
    yx3, x4 = _proj_out(z3, pool_out_w3, x3, mods[3], tr_lat, lat_tiles, "proj_out3")

    loss_part, dx4, d_final, dyx3, gate3 = _loss_head(x4, loss_target[0], final_g[None], yx3, mods[3], tr_lat)

    dz3, g_pool_out1 = _proj_out_bwd(dyx3, z3, pool_out_w3, "proj_out_bwd3")
    dpre3, g_grp1, g_scale1 = _pool_bwd(dz3, diff3, pre3, pool_grp_w3, scale_full[1:2], latent, "pool_bwd3")
    dh3 = _proj_in_dh(dpre3, pool_in_w3, 0, "proj_in_dh3")
    g_pool_in1 = _grad_w_in(h3, dpre3, pool_w_in.shape[2], "grad_w_in3")
    sent3 = send_grads(3, [g_pool_in1, grp_slots(g_grp1), g_pool_out1.reshape(N_DEV, shard, d)])
    dx3, norm3, dyx2, gate2 = _norm_bwd(x3, dh3, dx4, norm_g[3:4] + sent3[-1][0, 0], mods[3], tr_lat, lat_tiles, "norm_bwd3",
                                        below=(yx2, mods[2]))

    dz2, g_conv_out = _proj_out_bwd(dyx2, z2, conv_out_w, "proj_out_bwd2")
    dpre2, g_taps, g_cbias = _conv_bwd(dz2, pre2, taps_full, bias_full, "conv_bwd")
    dh2 = _proj_in_dh(dpre2, conv_in_w, 0, "proj_in_dh2")
    g_conv_in = _grad_w_in(h2, dpre2, conv_w_in.shape[2], "grad_w_in2")
    sent2 = send_grads(2, [g_conv_in, g_conv_out.reshape(N_DEV, shard, d)])
    dx2, norm2, dyx1, gate1 = _norm_bwd(x2, dh2, dx3, norm_g[2:3] + sent2[-1][0, 0], mods[2], tr_lat, lat_tiles, "norm_bwd2",
                                        below=(yx1, mods[1][:1]))

    dz1, g_na_out = _proj_out_bwd(dyx1, z1, na_out_w, "proj_out_bwd1")
    dpre1, g_rpb = _attn_bwd(qkv1, gpre1, o1, dz1, rpb_rows, seq)
    g_rpb = g_rpb[:, :na_rpb.shape[2], :na_rpb.shape[3]]
    dh1 = _proj_in_dh(dpre1, na_in_w, 0, "proj_in_dh1")
    g_na_in = _grad_w_in(h1, dpre1, na_w_in.shape[2], "grad_w_in1")
    sent1 = send_grads(1, [g_na_in, g_na_out.reshape(N_DEV, shard, d)])
    dxs1, norm1, dyx0, gate0 = _norm_bwd(xs1, dh1, dx2, norm_g[1:2] + sent1[-1][0, 0], mods[1], tr, x_tiles, "norm_bwd1",
                                         res_tiles=x_tiles, below=(yx0, mods[0]))

    dz0, g_pool_out0 = _proj_out_bwd(dyx0, z0, pool_out_w0, "proj_out_bwd0")
    sent0a = _exchange_start([g_pool_out0.reshape(N_DEV, shard, d)], True, jnp.zeros((8, LANES), F32), "grads_start0a")
    dpre0, g_grp0, g_scale0 = _pool_bwd(dz0, diff0, pre0, pool_grp_w0, scale_full[0:1], both, "pool_bwd0")
    g_pool_in0 = _grad_w_in(h0, dpre0, pool_w_in.shape[2], "grad_w_in0")
    dh0 = _proj_in_dh(dpre0, pool_in_w0, 0, "proj_in_dh0", after=sent0a[-1])
    dx0, norm0 = _norm_bwd(xs0, dh0, dxs1, norm_g[0:1], mods[0], tr, x_tiles, "norm_bwd0", out_tiles=x_tiles)
    grad_x = dx0[None]

    norms, gates = [norm0, norm1, norm2, norm3], [gate0, gate1, gate2, gate3]
    zero_d = jnp.zeros((d,), F32)
    dm_rows = [jnp.concatenate([norms[i][0, 0], norms[i][0, 1], gates[i][0, 0]]) for i in range(depth)]
    dm_rows.append(jnp.concatenate([norm0[1, 0], norm0[1, 1], gate0[1, 0]]))
    dm_rows.append(jnp.concatenate([norm1[1, 0], norm1[1, 1], zero_d]))
    dm_local = jnp.stack(dm_rows + [jnp.zeros((3 * d,), F32)] * 2)
    g_norm_part = jnp.stack([norm0[0, 2] + norm0[1, 2], norm1[0, 2] + norm1[1, 2], norm2[0, 2], norm3[0, 2]])
    pieces = [_rows128(dm_local), _rows128(g_norm_part), _rows128(d_final[0]), _pad_rows(_rows128(g_rpb)), loss_part]
    marks = np.cumsum([0] + [p.shape[0] for p in pieces])
    by_owner = [a.reshape(-1, N_DEV, shard).transpose(1, 0, 2) for a in (g_scale0, g_scale1, g_taps[:3], g_cbias[0:1])]
    by_owner = jnp.concatenate(by_owner + [jnp.zeros((N_DEV, 8 - n_pool - 4, shard), F32)], axis=1)
    small_sent = _exchange_start([jnp.concatenate(pieces, axis=0), by_owner], [False, True], jnp.zeros((8, LANES), F32),
                                 "small_grads_start")
    sent0b = _exchange_start([g_pool_in0, grp_slots(g_grp0)], True, small_sent[-1], "grads_start0b")

    def big(parts, w, m, v, name):
        shape = w.shape
        view = (-1, shape[-1])
        parts = [(parts.reshape((N_DEV,) + w.reshape(view).shape), k) for k in range(N_DEV)]
        return [r.reshape(shape) for r in _adamw(w.reshape(view), m.reshape(view), v.reshape(view), parts, name)]

    in3, grp3, out3 = _exchange_wait(sent3, True, sent0b[-1], "grads_wait3")
    in2, out2 = _exchange_wait(sent2, True, sent0b[-1], "grads_wait2")
    in1, out1 = _exchange_wait(sent1, True, sent0b[-1], "grads_wait1")
    res = {}
    res["na_w_in"] = [r[None] for r in big(in1, na_w_in[0], m_na_w_in[0], v_na_w_in[0], "adamw_na_in")]
    res["na_w_out"] = [r[None] for r in big(out1, na_w_out[0], m_na_w_out[0], v_na_w_out[0], "adamw_na_out")]
    res["conv_w_in"] = [r[None] for r in big(in2, conv_w_in[0], m_conv_w_in[0], v_conv_w_in[0], "adamw_conv_in")]
    res["conv_w_out"] = [r[None] for r in big(out2, conv_w_out[0], m_conv_w_out[0], v_conv_w_out[0], "adamw_conv_out")]

    done = [res[n][0] for n in ("na_w_in", "na_w_out", "conv_w_in", "conv_w_out")]
    small_out, owned = _exchange_wait(small_sent, [False, True], done, "small_grads_wait")
    loss = jnp.sum(small_out[:, marks[4], 0])
    dm_all = small_out[:, :marks[1]].reshape(N_DEV, 8, 3 * d).transpose(1, 0, 2)
    dm_mine = lax.dynamic_slice(dm_all, (0, 0, me * nb), (8, N_DEV, nb))
    g_ada_w, g_ada_b, cctx_part, dsilu_cond = _mod_bwd(cond, ada_w, dm_all, dm_mine)
    cctx_all = _gather_small(_rows128(cctx_part[0]), "gather_cctx")

    def summed(ref, lo, hi):
        g = ref[0, lo:hi, :]
        for k in range(1, N_DEV):
            g = g + ref[k, lo:hi, :]
        return g

    makers = [
        lambda so, ow, cc, ab, ds: summed(cc, 0, d_rows) * ds[...],
        lambda so, ow, cc, ab, ds: summed(so, marks[1], marks[2]),
        lambda so, ow, cc, ab, ds: ab[...],
        lambda so, ow, cc, ab, ds: summed(so, marks[2], marks[3]),
        lambda so, ow, cc, ab, ds: summed(so, marks[3], marks[4]),
        lambda so, ow, cc, ab, ds: summed(ow, 0, n_pool),
        lambda so, ow, cc, ab, ds: summed(ow, n_pool, n_pool + 3),
        lambda so, ow, cc, ab, ds: summed(ow, n_pool + 3, n_pool + 4),
    ]
    rpb_rows128 = lambda a: _pad_rows(_rows128(a))
    views = [_rows128] * 4 + [rpb_rows128] + [lambda a: a.reshape(-1, LANES)] * 3
    small = [(c_ctx, m_c_ctx, v_c_ctx), (norm_g, m_norm_g, v_norm_g), (ada_b, m_ada_b, v_ada_b),
             (final_g, m_final_g, v_final_g), (na_rpb, m_na_rpb, v_na_rpb), (pool_scale, m_pool_scale, v_pool_scale),
             (conv_dw, m_conv_dw, v_conv_dw), (conv_db, m_conv_db, v_conv_db)]
    states = [tuple(view(a) for a in triple) for view, triple in zip(views, small)]
    sources = (small_out, owned, cctx_all, _rows128(g_ada_b), _rows128(dsilu_cond[8]))
    small_res = _adamw_small(states, (sources, makers))
    names = ["c_ctx", "norm_g", "ada_b", "final_g", "na_rpb", "pool_scale", "conv_dw", "conv_db"]
    for name, (w, _, _), outs4 in zip(names, small, small_res):
        res[name] = [r.reshape(-1)[:w.size].reshape(w.shape) for r in outs4]

    res["ada_w"] = [r.reshape(ada_w.shape) for r in _adamw(
        ada_w.reshape(-1, nb), m_ada_w.reshape(-1, nb), v_ada_w.reshape(-1, nb), [g_ada_w.reshape(-1, nb)], "adamw_ada_w")]

    out0, = _exchange_wait(sent0a, True, small_res[0][0], "grads_wait0a")
    in0, grp0 = _exchange_wait(sent0b, True, small_res[0][0], "grads_wait0b")
    def both_layers(first, second, w, m, v, name):
        view = (w.shape[0], -1, w.shape[-1])
        landed = [first.reshape((N_DEV,) + w.reshape(view).shape[1:]), second.reshape((N_DEV,) + w.reshape(view).shape[1:])]
        return [r.reshape(w.shape) for r in _adamw_layers(w.reshape(view), m.reshape(view), v.reshape(view), landed, name)]

    res["pool_w_in"] = both_layers(in0, in3, pool_w_in, m_pool_w_in, v_pool_w_in, "adamw_pool_in")
    res["pool_w_grp"] = both_layers(grp0, grp3, pool_w_grp, m_pool_w_grp, v_pool_w_grp, "adamw_pool_grp")
    res["pool_w_out"] = both_layers(out0, out3, pool_w_out, m_pool_w_out, v_pool_w_out, "adamw_pool_out")

    order = ["c_ctx", "norm_g", "ada_w", "ada_b", "pool_w_in", "pool_w_grp", "pool_scale", "pool_w_out", "na_w_in",
             "na_rpb", "na_w_out", "conv_w_in", "conv_dw", "conv_db", "conv_w_out", "final_g"]
    outs = [loss, grad_x]
    for j in range(4):
        outs += [res[n][j] for n in order]
    return tuple(outs)
```

```python
import math

import numpy as np
import jax
import jax.numpy as jnp
from jax import lax
from jax.experimental import pallas as pl
from jax.experimental.pallas import tpu as pltpu

F32 = jnp.float32
BF16 = jnp.bfloat16
N_DEV = 8
LANES = 128
RMS_EPS = 1e-6
GRID_W = 64
WIN_ROWS = 8
WIN_COLS = 16
HEAD_DIM = 64
POOL_WINDOWS = (2, 4, 8, 16)
HALO = 8
CHUNK = 256
MASKED = -1e30
ADAM_LR = 0.001
ADAM_B1 = 0.9
ADAM_B2 = 0.999
ADAM_EPS = 1e-08
ADAM_WD = 0.01
ADAM_STEP = 10
VMEM_LIMIT = 56 * 1024 * 1024
ADAM_TILE_ELEMS = 256 * 1024
MESH = pl.DeviceIdType.MESH
ANY = pl.BlockSpec(memory_space=pl.ANY)
HBM = pl.BlockSpec(memory_space=pltpu.HBM)
SEM = pl.BlockSpec(memory_space=pltpu.SEMAPHORE)
EFFECT = pltpu.SideEffectType.DATAFLOW_SIDE_EFFECTING


def _pc(body, *, name, **kw):
    return pl.pallas_call(body, name=name, **kw)


def _params(*sem):
    return pltpu.CompilerParams(dimension_semantics=sem if sem else None, vmem_limit_bytes=VMEM_LIMIT)


def _dot(a, b, ca=1, cb=0, precision=None):
    return lax.dot_general(a, b, (((ca,), (cb,)), ((), ())), preferred_element_type=F32, precision=precision)


def _tile(n, pref, unit=LANES):
    best = None
    for t in range(unit, min(n, pref) + 1, unit):
        if n % t == 0:
            best = t
    return best if best is not None else n


def _sigmoid(x):
    return 1.0 / (1.0 + jnp.exp(-x))


def _silu(x):
    return x * _sigmoid(x)


def _dsilu(x):
    s = _sigmoid(x)
    return s * (1.0 + x * (1.0 - s))


def _my_place():
    return lax.axis_index("x"), lax.axis_index("y"), lax.axis_index("c")


def _flip(v, f):
    return 1 - v if f else v


def _gather_small(block, name, per_dest=False):
    rows, cols = block.shape[-2:]

    def body(x_ref, out_ref, send_sems, recv_sems):
        x, y, c = _my_place()
        me = 4 * x + 2 * y + c
        out_ref[me] = x_ref[me] if per_dest else x_ref[...]
        copies = []
        for k in range(1, N_DEV):
            peer = (_flip(x, k & 4), _flip(y, k & 2), _flip(c, k & 1))
            dest = 4 * peer[0] + 2 * peer[1] + peer[2]
            cp = pltpu.make_async_remote_copy(
                src_ref=x_ref.at[dest] if per_dest else x_ref, dst_ref=out_ref.at[me],
                send_sem=send_sems.at[k - 1], recv_sem=recv_sems.at[k - 1], device_id=peer, device_id_type=MESH)
            cp.start()
            copies.append(cp)
        for cp in copies:
            cp.wait()

    return _pc(
        body, name=name,
        out_shape=jax.ShapeDtypeStruct((N_DEV, rows, cols), block.dtype),
        in_specs=[pl.BlockSpec(memory_space=pltpu.VMEM)],
        out_specs=pl.BlockSpec(memory_space=pltpu.VMEM),
        scratch_shapes=[pltpu.SemaphoreType.DMA((N_DEV - 1,)), pltpu.SemaphoreType.DMA((N_DEV - 1,))],
    )(block)


ALL_PEERS = tuple(range(N_DEV))
CHIP_PEERS = (0, 1, 2, 4, 6)
OTHER_CHIPS = (2, 4, 6)


def _flag(per_dest, t):
    return per_dest[t] if isinstance(per_dest, (list, tuple)) else per_dest


def _peer(k):
    x, y, c = _my_place()
    peer = (_flip(x, k & 4), _flip(y, k & 2), _flip(c, k & 1))
    return peer, 4 * peer[0] + 2 * peer[1] + peer[2]


def _peer_lists(peers, nt):
    return list(peers) if isinstance(peers, list) else [peers] * nt


def _exchange_copies(srcs, lands, send_sems, recv_sems, per_dest, peers=ALL_PEERS):
    x, y, c = _my_place()
    me = 4 * x + 2 * y + c
    copies = []
    for t, (src, land, ks) in enumerate(zip(srcs, lands, _peer_lists(peers, len(srcs)))):
        for k in ks:
            peer, dest = _peer(k)
            s = len(copies)
            copies.append(pltpu.make_async_remote_copy(
                src_ref=src.at[dest] if _flag(per_dest, t) else src, dst_ref=land.at[me],
                send_sem=send_sems[s], recv_sem=recv_sems[s], device_id=peer, device_id_type=MESH))
    return copies


def _forward_copies(lands, send_sems, recv_sems):
    sibling, _ = _peer(1)
    copies = []
    for t, land in enumerate(lands):
        for n, k in enumerate(OTHER_CHIPS):
            _, slot = _peer(k)
            s = t * len(OTHER_CHIPS) + n
            copies.append(pltpu.make_async_remote_copy(
                src_ref=land.at[slot], dst_ref=land.at[slot], send_sem=send_sems[s], recv_sem=recv_sems[s],
                device_id=sibling, device_id_type=MESH))
    return copies


def _forward_start(lands, after, name):
    nt = len(lands)
    ns = nt * len(OTHER_CHIPS)

    def body(*refs):
        ins, outs = refs[:nt + 1], refs[nt + 1:]
        for cp in _forward_copies(ins[:nt], outs[:ns], outs[ns:2 * ns]):
            cp.start()
        outs[-1][...] = jnp.zeros_like(outs[-1])

    res = _pc(
        body, name=name,
        out_shape=(*[pltpu.SemaphoreType.DMA(())] * (2 * ns), *[pltpu.HBM(a.shape, a.dtype) for a in lands],
                   jax.ShapeDtypeStruct((8, LANES), F32)),
        in_specs=[HBM] * nt + [ANY],
        out_specs=(*[SEM] * (2 * ns), *[HBM] * nt, pl.BlockSpec(memory_space=pltpu.VMEM)),
        input_output_aliases={i: 2 * ns + i for i in range(nt)},
        compiler_params=pltpu.CompilerParams(has_side_effects=EFFECT),
    )(*lands, after)
    return list(res[:ns]), list(res[ns:2 * ns]), list(res[2 * ns:2 * ns + nt]), res[-1]


def _forward_wait(state, after, name):
    send_sems, recv_sems, lands, _ = state
    nt, ns = len(lands), len(send_sems)

    def body(*refs):
        sems = refs[nt:nt + 2 * ns]
        for cp in _forward_copies(refs[:nt], sems[:ns], sems[ns:]):
            cp.wait_send()
            cp.wait_recv()

    res = _pc(
        body, name=name,
        out_shape=tuple(pltpu.HBM(a.shape, a.dtype) for a in lands),
        in_specs=[HBM] * nt + [SEM] * (2 * ns) + [ANY],
        out_specs=tuple([HBM] * nt),
        input_output_aliases={i: i for i in range(nt)},
        compiler_params=pltpu.CompilerParams(has_side_effects=EFFECT),
    )(*lands, *send_sems, *recv_sems, after)
    return list(res)


def _exchange_start(srcs, per_dest, after, name, peers=ALL_PEERS):
    nt = len(srcs)
    peers = _peer_lists(peers, nt)
    ns = sum(len(ks) for ks in peers)
    lands = [lax.empty((N_DEV,) + (s.shape[1:] if _flag(per_dest, t) else s.shape), s.dtype) for t, s in enumerate(srcs)]

    def body(*refs):
        ins, outs = refs[:2 * nt + 1], refs[2 * nt + 1:]
        for cp in _exchange_copies(ins[:nt], ins[nt:2 * nt], outs[:ns], outs[ns:2 * ns], per_dest, peers):
            cp.start()
        outs[-1][...] = jnp.zeros_like(outs[-1])

    hbm = [pltpu.with_memory_space_constraint(a, pltpu.HBM) for a in list(srcs) + lands]
    res = _pc(
        body, name=name,
        out_shape=(*[pltpu.SemaphoreType.DMA(())] * (2 * ns), *[pltpu.HBM(a.shape, a.dtype) for a in hbm],
                   jax.ShapeDtypeStruct((8, LANES), F32)),
        in_specs=[HBM] * (2 * nt) + [ANY],
        out_specs=(*[SEM] * (2 * ns), *[HBM] * (2 * nt), pl.BlockSpec(memory_space=pltpu.VMEM)),
        input_output_aliases={i: 2 * ns + i for i in range(2 * nt)},
        compiler_params=pltpu.CompilerParams(has_side_effects=EFFECT),
    )(*hbm, after)
    sems, rest = res[:2 * ns], res[2 * ns:]
    return list(sems[:ns]), list(sems[ns:]), list(rest[:nt]), list(rest[nt:2 * nt]), peers, rest[-1]


def _exchange_wait(state, per_dest, after, name, which=None):
    send_sems, recv_sems, srcs, lands, peers, _ = state
    which = list(range(len(srcs))) if which is None else which
    per_dest = [_flag(per_dest, t) for t in which]
    after = list(after) if isinstance(after, (list, tuple)) else [after]
    first = np.cumsum([0] + [len(ks) for ks in peers])
    pick = [first[t] + n for t in which for n in range(len(peers[t]))]
    peers = [peers[t] for t in which]
    send_sems, recv_sems = [send_sems[s] for s in pick], [recv_sems[s] for s in pick]
    srcs, lands = [srcs[t] for t in which], [lands[t] for t in which]
    nt = len(srcs)
    ns = len(send_sems)

    def body(*refs):
        sems = refs[2 * nt:2 * nt + 2 * ns]
        for cp in _exchange_copies(refs[:nt], refs[nt:2 * nt], sems[:ns], sems[ns:], per_dest, peers):
            cp.wait_send()
            cp.wait_recv()

    thru = list(srcs) + list(lands)
    res = _pc(
        body, name=name,
        out_shape=tuple(pltpu.HBM(a.shape, a.dtype) for a in thru),
        in_specs=[HBM] * (2 * nt) + [SEM] * (2 * ns) + [ANY] * len(after),
        out_specs=tuple([HBM] * (2 * nt)),
        input_output_aliases={i: i for i in range(2 * nt)},
        compiler_params=pltpu.CompilerParams(has_side_effects=EFFECT),
    )(*thru, *send_sems, *recv_sems, *after)
    return list(res[nt:])


def _mod_fwd(cond, ada_w, bias):
    depth, d, nb = ada_w.shape

    def body(c_ref, w_ref, b_ref, o_ref):
        s = _silu(c_ref[...]).astype(BF16)
        o_ref[...] = _dot(s, w_ref[...].astype(BF16)) + b_ref[...]

    return _pc(
        body, name="mod_fwd", grid=(depth,),
        in_specs=[pl.BlockSpec((16, d), lambda i: (0, 0)), pl.BlockSpec((None, d, nb), lambda i: (i, 0, 0)),
                  pl.BlockSpec((None, 1, nb), lambda i: (i, 0, 0))],
        out_specs=pl.BlockSpec((None, 16, nb), lambda i: (i, 0, 0)),
        out_shape=jax.ShapeDtypeStruct((depth, 16, nb), F32),
        compiler_params=_params("parallel"),
    )(cond, ada_w, bias.reshape(depth, 1, nb))


def _mod_bwd(cond, ada_w, dm_all, dm_mine):
    depth, d, nb = ada_w.shape
    d3 = dm_all.shape[-1]

    def body(c_ref, w_ref, all_ref, call_ref, mine_ref, cmine_ref, gw_ref, gb_ref, part_ref, ds_ref):
        i = pl.program_id(0)
        cond_v = c_ref[...]
        s = _silu(cond_v).astype(BF16)
        has_ctx = jnp.where(i < 2, 1.0, 0.0)
        tot_all = jnp.sum(call_ref[...], axis=0, keepdims=True) * has_ctx
        tot_mine = jnp.broadcast_to(jnp.sum(cmine_ref[...], axis=0, keepdims=True) * has_ctx, (8, nb)).astype(BF16)
        gb_ref[...] = jnp.sum(all_ref[...], axis=0, keepdims=True) + tot_all
        gw_ref[...] = _dot(s[0:8], mine_ref[...].astype(BF16), 0, 0) + _dot(s[8:16], tot_mine, 0, 0)
        part = _dot(tot_mine, w_ref[...].astype(BF16), 1, 1)

        @pl.when(i == 0)
        def _():
            part_ref[...] = jnp.zeros_like(part_ref)
            ds_ref[...] = _dsilu(cond_v)

        part_ref[...] += part

    def rows(width, which):
        return pl.BlockSpec((None, N_DEV, width), which)

    layer = lambda i: (i, 0, 0)
    ctx_layer = lambda i: (jnp.minimum(i, 1) + 4, 0, 0)
    return _pc(
        body, name="mod_bwd", grid=(depth,),
        in_specs=[pl.BlockSpec((16, d), lambda i: (0, 0)), pl.BlockSpec((None, d, nb), layer),
                  rows(d3, layer), rows(d3, ctx_layer), rows(nb, layer), rows(nb, ctx_layer)],
        out_specs=[pl.BlockSpec((None, d, nb), layer), pl.BlockSpec((None, 1, d3), layer),
                   pl.BlockSpec((8, d), lambda i: (0, 0)), pl.BlockSpec((16, d), lambda i: (0, 0))],
        out_shape=[jax.ShapeDtypeStruct((depth, d, nb), F32), jax.ShapeDtypeStruct((depth, 1, d3), F32),
                   jax.ShapeDtypeStruct((8, d), F32), jax.ShapeDtypeStruct((16, d), F32)],
        compiler_params=_params("arbitrary"),
    )(cond, ada_w, dm_all, dm_all, dm_mine, dm_mine)


def _norm_fwd(xs, g, mod, tr, seg_tiles, name):
    t, d = xs.shape

    def body(x_ref, g_ref, mod_ref, h_ref):
        x = x_ref[...]
        r = lax.rsqrt(jnp.mean(x * x, axis=-1, keepdims=True) + RMS_EPS)
        y = (x * r) * g_ref[...]
        h_ref[...] = (y * (1.0 + mod_ref[1:2, :]) + mod_ref[0:1, :]).astype(BF16)

    return _pc(
        body, name=name, grid=(t // tr,),
        in_specs=[pl.BlockSpec((tr, d), lambda i: (i, 0)), pl.BlockSpec((1, d), lambda i: (0, 0)),
                  pl.BlockSpec((None, 8, d), lambda i: (i // seg_tiles, 0, 0))],
        out_specs=pl.BlockSpec((tr, d), lambda i: (i, 0)),
        out_shape=jax.ShapeDtypeStruct((t, d), BF16),
        compiler_params=_params("parallel"),
    )(xs, g, mod)


def _resid_grad(dx, i, seg_tiles, yx_ref, gate_ref, dyx_ref, gsum_ref):
    dyx_ref[...] = (dx * gate_ref[2:3, :]).astype(BF16)

    @pl.when(i % seg_tiles == 0)
    def _():
        gsum_ref[...] = jnp.zeros_like(gsum_ref)

    gsum_ref[0:1, :] += jnp.sum(dx * yx_ref[...], axis=0, keepdims=True)


def _norm_bwd(xs, dh, dres, g, mod, tr, seg_tiles, name, res_tiles=None, out_tiles=None, below=None):
    t, d = xs.shape
    n_tiles = t // tr
    res_tiles = n_tiles if res_tiles is None else res_tiles
    out_tiles = n_tiles if out_tiles is None else out_tiles

    def body(x_ref, dh_ref, dres_ref, g_ref, mod_ref, *rest):
        i = pl.program_id(0)
        x = x_ref[...]
        r = lax.rsqrt(jnp.mean(x * x, axis=-1, keepdims=True) + RMS_EPS)
        xn = x * r
        dhv = dh_ref[...]
        gain = g_ref[...]
        one_scale = 1.0 + mod_ref[1:2, :]
        dxn = dhv * (gain * one_scale)
        dx = r * (dxn - xn * jnp.mean(dxn * xn, axis=-1, keepdims=True))
        if res_tiles == n_tiles:
            dx = dx + dres_ref[...]
        else:
            dx = dx + jnp.where(i < res_tiles, dres_ref[...], 0.0)
        if below is None:
            dx_ref, sum_ref = rest
        else:
            yx_ref, gate_ref, dx_ref, sum_ref, dyx_ref, gsum_ref = rest
            _resid_grad(dx, i, seg_tiles, yx_ref, gate_ref, dyx_ref, gsum_ref)
        if out_tiles == n_tiles:
            dx_ref[...] = dx
        else:
            @pl.when(i < out_tiles)
            def _():
                dx_ref[...] = dx

        @pl.when(i % seg_tiles == 0)
        def _():
            sum_ref[...] = jnp.zeros_like(sum_ref)

        sum_ref[0:1, :] += jnp.sum(dhv, axis=0, keepdims=True)
        sum_ref[1:2, :] += jnp.sum(dhv * (xn * gain), axis=0, keepdims=True)
        sum_ref[2:3, :] += jnp.sum(dhv * one_scale * xn, axis=0, keepdims=True)

    row = pl.BlockSpec((tr, d), lambda i: (i, 0))
    seg = pl.BlockSpec((None, 8, d), lambda i: (i // seg_tiles, 0, 0))
    in_specs = [row, row, pl.BlockSpec((tr, d), lambda i: (jnp.minimum(i, res_tiles - 1), 0)),
                pl.BlockSpec((1, d), lambda i: (0, 0)), seg]
    out_specs = [pl.BlockSpec((tr, d), lambda i: (jnp.minimum(i, out_tiles - 1), 0)), seg]
    out_shape = [jax.ShapeDtypeStruct((out_tiles * tr, d), F32), jax.ShapeDtypeStruct((mod.shape[0], 8, d), F32)]
    args = [xs, dh, dres, g, mod]
    if below is not None:
        in_specs += [row, seg]
        out_specs += [row, seg]
        out_shape += [jax.ShapeDtypeStruct((t, d), BF16), jax.ShapeDtypeStruct((below[1].shape[0], 8, d), F32)]
        args += list(below)
    return _pc(
        body, name=name, grid=(n_tiles,), in_specs=in_specs, out_specs=out_specs, out_shape=out_shape,
        compiler_params=_params("arbitrary"),
    )(*args)


def _loss_head(xs, target, g, yx, mod, tr):
    t, d = xs.shape

    def body(x_ref, t_ref, g_ref, yx_ref, gate_ref, loss_ref, dx_ref, dg_ref, dyx_ref, gsum_ref):
        i = pl.program_id(0)
        x = x_ref[...]
        r = lax.rsqrt(jnp.mean(x * x, axis=-1, keepdims=True) + RMS_EPS)
        xn = x * r
        gain = g_ref[...]
        err = xn * gain - t_ref[...]
        dy = err * (1.0 / d)
        dxn = dy * gain
        dx = r * (dxn - xn * jnp.mean(dxn * xn, axis=-1, keepdims=True))
        dx_ref[...] = dx
        _resid_grad(dx, i, t // tr, yx_ref, gate_ref, dyx_ref, gsum_ref)

        @pl.when(i == 0)
        def _():
            loss_ref[...] = jnp.zeros_like(loss_ref)
            dg_ref[...] = jnp.zeros_like(dg_ref)

        loss_ref[...] += 0.5 * jnp.sum(jnp.mean(err * err, axis=-1, keepdims=True))
        dg_ref[0:1, :] += jnp.sum(dy * xn, axis=0, keepdims=True)

    row = pl.BlockSpec((tr, d), lambda i: (i, 0))
    seg = pl.BlockSpec((None, 8, d), lambda i: (0, 0, 0))
    return _pc(
        body, name="loss_head", grid=(t // tr,),
        in_specs=[row, row, pl.BlockSpec((1, d), lambda i: (0, 0)), row, seg],
        out_specs=[pl.BlockSpec((8, LANES), lambda i: (0, 0)), row, pl.BlockSpec((8, d), lambda i: (0, 0)), row, seg],
        out_shape=[jax.ShapeDtypeStruct((8, LANES), F32), jax.ShapeDtypeStruct((t, d), F32),
                   jax.ShapeDtypeStruct((8, d), F32), jax.ShapeDtypeStruct((t, d), BF16),
                   jax.ShapeDtypeStruct((1, 8, d), F32)],
        compiler_params=_params("arbitrary"),
    )(xs, target, g, yx, mod)


def _proj_in(h, w, layer, width, name, blocks=None, dtype=F32):
    t, d = h.shape
    n8 = w.shape[-1]
    first, count = blocks if blocks is not None else (0, N_DEV)
    per_part = width // n8
    assert first % per_part == 0 and count % per_part == 0
    tm = _tile(t, 1152)

    def body(a_ref, b_ref, o_ref):
        a = a_ref[...]
        for s in range(per_part):
            o_ref[:, s * n8:(s + 1) * n8] = _dot(a, b_ref[s]).astype(dtype)

    return _pc(
        body, name=name, grid=(t // tm, count // per_part),
        in_specs=[pl.BlockSpec((tm, d), lambda i, j: (i, 0)),
                  pl.BlockSpec((per_part, None, d, n8), lambda i, j: (first // per_part + j, layer, 0, 0))],
        out_specs=pl.BlockSpec((None, tm, width), lambda i, j: (j, i, 0)),
        out_shape=jax.ShapeDtypeStruct((count // per_part, t, width), dtype),
        compiler_params=_params("parallel", "parallel"),
    )(h, w)


def _proj_in_split(h, w, layer, width, name, narrow):
    t, d = h.shape
    n8 = w.shape[-1]
    per_part = width // n8
    parts = N_DEV // per_part
    tm = _tile(t, 1152)

    def body(a_ref, b_ref, lo_ref, hi_ref):
        j = pl.program_id(1)
        a = a_ref[...]
        cols = [_dot(a, b_ref[s]) for s in range(per_part)]

        @pl.when(j < narrow)
        def _():
            for s in range(per_part):
                lo_ref[:, s * n8:(s + 1) * n8] = cols[s].astype(BF16)

        @pl.when(j >= narrow)
        def _():
            for s in range(per_part):
                hi_ref[:, s * n8:(s + 1) * n8] = cols[s]

    return _pc(
        body, name=name, grid=(t // tm, parts),
        in_specs=[pl.BlockSpec((tm, d), lambda i, j: (i, 0)),
                  pl.BlockSpec((per_part, None, d, n8), lambda i, j: (j, layer, 0, 0))],
        out_specs=[pl.BlockSpec((None, tm, width), lambda i, j: (jnp.minimum(j, narrow - 1), i, 0)),
                   pl.BlockSpec((None, tm, width), lambda i, j: (jnp.maximum(j - narrow, 0), i, 0))],
        out_shape=[jax.ShapeDtypeStruct((narrow, t, width), BF16), jax.ShapeDtypeStruct((parts - narrow, t, width), F32)],
        compiler_params=_params("parallel", "arbitrary"),
    )(h, w)


def _proj_out(z, w, res, mod, tm, seg_tiles, name, nxt=None):
    t, k = z.shape
    d = w.shape[1]

    def body(z_ref, w_ref, res_ref, mod_ref, *rest):
        yx = _dot(z_ref[...], w_ref[...])
        x = res_ref[...] + mod_ref[2:3, :] * yx
        if nxt is None:
            yx_ref, x_ref = rest
        else:
            g_ref, nmod_ref, yx_ref, x_ref, h_ref = rest
            r = lax.rsqrt(jnp.mean(x * x, axis=-1, keepdims=True) + RMS_EPS)
            h_ref[...] = (((x * r) * g_ref[...]) * (1.0 + nmod_ref[1:2, :]) + nmod_ref[0:1, :]).astype(BF16)
        yx_ref[...] = yx
        x_ref[...] = x

    tile = pl.BlockSpec((tm, d), lambda i: (i, 0))
    seg = pl.BlockSpec((None, 8, d), lambda i: (i // seg_tiles, 0, 0))
    in_specs = [pl.BlockSpec((tm, k), lambda i: (i, 0)), pl.BlockSpec((k, d), lambda i: (0, 0)), tile, seg]
    out_specs = [tile, tile]
    out_shape = [jax.ShapeDtypeStruct((t, d), F32), jax.ShapeDtypeStruct((t, d), F32)]
    args = [z, w, res, mod]
    if nxt is not None:
        in_specs += [pl.BlockSpec((1, d), lambda i: (0, 0)), seg]
        out_specs.append(tile)
        out_shape.append(jax.ShapeDtypeStruct((t, d), BF16))
        args += list(nxt)
    return _pc(
        body, name=name, grid=(t // tm,), in_specs=in_specs, out_specs=out_specs, out_shape=out_shape,
        compiler_params=_params("parallel"),
    )(*args)


def _proj_out_bwd(dyx, z, w, name):
    t, d = dyx.shape
    width = w.shape[0]
    tm = _tile(t, 768)
    nk = t // tm

    def body(a_ref, z_ref, w_ref, dz_ref, dw_ref, acc_ref):
        k = pl.program_id(0)

        @pl.when(k == 0)
        def _():
            acc_ref[...] = jnp.zeros_like(acc_ref)

        a = a_ref[...]
        dz_ref[...] = _dot(a, w_ref[...], 1, 1)
        acc_ref[...] += _dot(z_ref[...].T, a)

        @pl.when(k == nk - 1)
        def _():
            dw_ref[...] = acc_ref[...].astype(BF16)

    return _pc(
        body, name=name, grid=(nk,),
        in_specs=[pl.BlockSpec((tm, d), lambda k: (k, 0)), pl.BlockSpec((tm, width), lambda k: (k, 0)),
                  pl.BlockSpec((width, d), lambda k: (0, 0))],
        out_specs=[pl.BlockSpec((tm, width), lambda k: (k, 0)), pl.BlockSpec((width, d), lambda k: (0, 0))],
        out_shape=[jax.ShapeDtypeStruct((t, width), F32), jax.ShapeDtypeStruct((width, d), BF16)],
        scratch_shapes=[pltpu.VMEM((width, d), F32)],
        compiler_params=_params("arbitrary"),
    )(dyx, z, w)


def _proj_in_dh(dpre, w, layer, name, after=None):
    parts, t, width = dpre.shape
    d, n8 = w.shape[-2:]
    per_part = width // n8
    tm, tn = _tile(t, 768), _tile(d, 512)

    def body(a_ref, w_ref, *rest):
        o_ref = rest[-1]
        acc = None
        for p in range(parts):
            for s in range(per_part):
                term = _dot(a_ref[p, :, s * n8:(s + 1) * n8], w_ref[p * per_part + s], 1, 1)
                acc = term if acc is None else acc + term
        o_ref[...] = acc

    extra = [] if after is None else [after]
    return _pc(
        body, name=name, grid=(t // tm, d // tn),
        in_specs=[pl.BlockSpec((parts, tm, width), lambda i, j: (0, i, 0)),
                  pl.BlockSpec((N_DEV, None, tn, n8), lambda i, j: (0, layer, j, 0))] + [ANY] * len(extra),
        out_specs=pl.BlockSpec((tm, tn), lambda i, j: (i, j)),
        out_shape=jax.ShapeDtypeStruct((t, d), F32),
        compiler_params=_params("parallel", "parallel"),
    )(dpre, w, *extra)


def _transposed(a_ref):
    return a_ref[...].T


def _grad_w_in(h, dpre, n8, name):
    t, d = h.shape
    parts, _, width = dpre.shape
    per_part = width // n8
    tm, tk = _tile(d, 512), _tile(t, 1152)
    nk = t // tk

    def body(a_ref, b_ref, o_ref, acc_ref):
        k = pl.program_id(1)

        @pl.when(k == 0)
        def _():
            acc_ref[...] = jnp.zeros_like(acc_ref)

        at = _transposed(a_ref)
        for p in range(parts):
            r = _dot(at, b_ref[p])
            for s in range(per_part):
                acc_ref[p * per_part + s] += r[:, s * n8:(s + 1) * n8]

        @pl.when(k == nk - 1)
        def _():
            o_ref[...] = acc_ref[...].astype(BF16)

    return _pc(
        body, name=name, grid=(d // tm, nk),
        in_specs=[pl.BlockSpec((tk, tm), lambda i, k: (k, i)), pl.BlockSpec((parts, tk, width), lambda i, k: (0, k, 0))],
        out_specs=pl.BlockSpec((parts * per_part, tm, n8), lambda i, k: (0, i, 0)),
        out_shape=jax.ShapeDtypeStruct((parts * per_part, d, n8), BF16),
        scratch_shapes=[pltpu.VMEM((parts * per_part, tm, n8), F32)],
        compiler_params=_params("parallel", "arbitrary"),
    )(h, dpre)


def _shift(v, k):
    n = v.shape[0]
    return pltpu.roll(v, k % n, 0)


def _window_sum(v, win):
    s = v + _shift(v, 1)
    step = 1
    while 2 * step < win:
        s = _shift(s, step) + _shift(s, -step)
        step *= 2
    return s


def _window_count(base, seg_len, win, shape):
    t = base + lax.broadcasted_iota(jnp.int32, shape, 0)
    hi = jnp.minimum(t + win // 2, seg_len)
    lo = jnp.maximum(t - win // 2, 0)
    return (hi - lo).astype(F32)


def _pad_offsets(segs):
    return [HALO * (s + 1) + st for s, (st, _) in enumerate(segs)]


def _for_chunks(segs, fn):
    offs = _pad_offsets(segs)
    for s, (st, ln) in enumerate(segs):
        def step(ci, carry, s=s, st=st, ln=ln):
            fn(s, st, ln, offs[s], pl.multiple_of(ci * CHUNK, CHUNK))
            return carry
        lax.fori_loop(0, ln // CHUNK, step, 0)


def _pool_fwd(pre, w_grp, scale, segs, name):
    _, t, width = pre.shape
    grp = width // len(POOL_WINDOWS)
    padded = t + HALO * (len(segs) + 1)

    def group(win, pre_ref, w_ref, sc_ref, z_ref, diff_ref, pad_ref):
        pad_ref[...] = jnp.zeros_like(pad_ref)

        def fill(s, st, ln, off, b):
            pad_ref[pl.ds(off + b, CHUNK), :] = pre_ref[0, pl.ds(st + b, CHUNK), :]

        _for_chunks(segs, fill)

        def mix(s, st, ln, off, b):
            ext = pad_ref[pl.ds(off - HALO + b, CHUNK + 2 * HALO), :]
            total = _window_sum(ext, win)[HALO:HALO + CHUNK]
            u = pre_ref[0, pl.ds(st + b, CHUNK), :]
            diff = (total / _window_count(b, ln, win, u.shape) - u).astype(BF16)
            mixed = _dot(diff, w_ref[...])
            gate = _silu(pre_ref[1, pl.ds(st + b, CHUNK), :])
            z_ref[pl.ds(st + b, CHUNK), :] = (mixed * sc_ref[...] * gate).astype(BF16)
            diff_ref[pl.ds(st + b, CHUNK), :] = diff

        _for_chunks(segs, mix)

    def body(pre_ref, w_ref, sc_ref, z_ref, diff_ref, pad_ref):
        gi = pl.program_id(0)
        for widx, win in enumerate(POOL_WINDOWS):
            @pl.when(gi == widx)
            def _(win=win):
                group(win, pre_ref, w_ref, sc_ref, z_ref, diff_ref, pad_ref)

    col = pl.BlockSpec((t, grp), lambda g: (0, g))
    return _pc(
        body, name=name, grid=(len(POOL_WINDOWS),),
        in_specs=[pl.BlockSpec((2, t, grp), lambda g: (0, 0, g)), pl.BlockSpec((None, grp, grp), lambda g: (g, 0, 0)),
                  pl.BlockSpec((1, grp), lambda g: (0, g))],
        out_specs=[col, col],
        out_shape=[jax.ShapeDtypeStruct((t, width), BF16), jax.ShapeDtypeStruct((t, width), BF16)],
        scratch_shapes=[pltpu.VMEM((padded, grp), F32)],
        compiler_params=_params("parallel"),
    )(pre, w_grp, scale)


def _pool_bwd(dz, diff, pre, w_grp, scale, segs, name):
    _, t, width = pre.shape
    grp = width // len(POOL_WINDOWS)
    padded = t + HALO * (len(segs) + 1)

    def group(win, dz_ref, diff_ref, pre_ref, w_ref, sc_ref, dpre_ref, dw_ref, dsc_ref, pad_ref, dd_ref):
        pad_ref[...] = jnp.zeros_like(pad_ref)
        dw_ref[...] = jnp.zeros_like(dw_ref)
        dsc_ref[...] = jnp.zeros_like(dsc_ref)

        def first(s, st, ln, off, b):
            rows = pl.ds(st + b, CHUNK)
            diff_v = diff_ref[rows, :]
            mixed = _dot(diff_v, w_ref[...])
            g = pre_ref[1, rows, :]
            sg = _silu(g)
            dzv = dz_ref[rows, :]
            dmixed = (dzv * sc_ref[...] * sg).astype(BF16)
            dsc_ref[...] += jnp.sum(dzv * mixed * sg, axis=0, keepdims=True)
            dpre_ref[1, rows, :] = (dzv * mixed * sc_ref[...] * _dsilu(g)).astype(BF16)
            ddiff = _dot(dmixed, w_ref[...], 1, 1)
            dw_ref[...] += _dot(diff_v, dmixed, 0, 0)
            dd_ref[rows, :] = ddiff
            pad_ref[pl.ds(off + b, CHUNK), :] = ddiff / _window_count(b, ln, win, ddiff.shape)

        _for_chunks(segs, first)

        def second(s, st, ln, off, b):
            rows = pl.ds(st + b, CHUNK)
            ext = pad_ref[pl.ds(off - HALO + b, CHUNK + 2 * HALO), :]
            total = _shift(_window_sum(ext, win), -1)[HALO:HALO + CHUNK]
            dpre_ref[0, rows, :] = (total - dd_ref[rows, :]).astype(BF16)

        _for_chunks(segs, second)

    def body(dz_ref, diff_ref, pre_ref, w_ref, sc_ref, dpre_ref, dw_ref, dsc_ref, pad_ref, dd_ref):
        gi = pl.program_id(0)
        for widx, win in enumerate(POOL_WINDOWS):
            @pl.when(gi == widx)
            def _(win=win):
                group(win, dz_ref, diff_ref, pre_ref, w_ref, sc_ref, dpre_ref, dw_ref, dsc_ref, pad_ref, dd_ref)

    col = pl.BlockSpec((t, grp), lambda g: (0, g))
    both = pl.BlockSpec((2, t, grp), lambda g: (0, 0, g))
    wspec = pl.BlockSpec((None, grp, grp), lambda g: (g, 0, 0))
    sspec = pl.BlockSpec((1, grp), lambda g: (0, g))
    return _pc(
        body, name=name, grid=(len(POOL_WINDOWS),),
        in_specs=[col, col, both, wspec, sspec],
        out_specs=[both, wspec, sspec],
        out_shape=[jax.ShapeDtypeStruct((2, t, width), BF16), jax.ShapeDtypeStruct((len(POOL_WINDOWS), grp, grp), F32),
                   jax.ShapeDtypeStruct((1, width), F32)],
        scratch_shapes=[pltpu.VMEM((padded, grp), F32), pltpu.VMEM((t, grp), F32)],
        compiler_params=_params("parallel"),
    )(dz, diff, pre, w_grp, scale)


def _conv_fwd(pre, dw, db, name):
    _, t, width = pre.shape
    cb = LANES
    segs = [(0, t)]

    def body(pre_ref, dw_ref, db_ref, z_ref, pad_ref):
        pad_ref[...] = jnp.zeros_like(pad_ref)

        def fill(s, st, ln, off, b):
            rows = pl.ds(b, CHUNK)
            pad_ref[pl.ds(off + b, CHUNK), :] = pre_ref[1, rows, :] * pre_ref[2, rows, :]

        _for_chunks(segs, fill)

        def mix(s, st, ln, off, b):
            rows = pl.ds(b, CHUNK)
            ext = pad_ref[pl.ds(off - HALO + b, CHUNK + 2 * HALO), :]
            conv = (dw_ref[0:1, :] * _shift(ext, 1) + dw_ref[1:2, :] * ext + dw_ref[2:3, :] * _shift(ext, -1))
            conv = conv[HALO:HALO + CHUNK] + db_ref[...]
            y = pre_ref[0, rows, :] * conv
            z_ref[rows, :] = (y * _silu(pre_ref[3, rows, :])).astype(BF16)

        _for_chunks(segs, mix)

    return _pc(
        body, name=name, grid=(width // cb,),
        in_specs=[pl.BlockSpec((4, t, cb), lambda j: (0, 0, j)), pl.BlockSpec((8, cb), lambda j: (0, j)),
                  pl.BlockSpec((1, cb), lambda j: (0, j))],
        out_specs=pl.BlockSpec((t, cb), lambda j: (0, j)),
        out_shape=jax.ShapeDtypeStruct((t, width), BF16),
        scratch_shapes=[pltpu.VMEM((t + 2 * HALO, cb), F32)],
        compiler_params=_params("parallel"),
    )(pre, dw, db)


def _conv_bwd(dz, pre, dw, db, name):
    _, t, width = pre.shape
    cb = LANES
    segs = [(0, t)]

    def body(dz_ref, pre_ref, dw_ref, db_ref, dpre_ref, ddw_ref, ddb_ref, pad_a, pad_c):
        pad_a[...] = jnp.zeros_like(pad_a)
        pad_c[...] = jnp.zeros_like(pad_c)
        ddw_ref[...] = jnp.zeros_like(ddw_ref)
        ddb_ref[...] = jnp.zeros_like(ddb_ref)

        def fill(s, st, ln, off, b):
            rows = pl.ds(b, CHUNK)
            pad_a[pl.ds(off + b, CHUNK), :] = pre_ref[1, rows, :] * pre_ref[2, rows, :]

        _for_chunks(segs, fill)

        def first(s, st, ln, off, b):
            rows = pl.ds(b, CHUNK)
            ext = pad_a[pl.ds(off - HALO + b, CHUNK + 2 * HALO), :]
            prev, nxt = _shift(ext, 1)[HALO:HALO + CHUNK], _shift(ext, -1)[HALO:HALO + CHUNK]
            here = ext[HALO:HALO + CHUNK]
            conv = dw_ref[0:1, :] * prev + dw_ref[1:2, :] * here + dw_ref[2:3, :] * nxt + db_ref[...]
            bg, g = pre_ref[0, rows, :], pre_ref[3, rows, :]
            dzv = dz_ref[rows, :]
            dy = dzv * _silu(g)
            dpre_ref[3, rows, :] = (dzv * (bg * conv) * _dsilu(g)).astype(BF16)
            dpre_ref[0, rows, :] = (dy * conv).astype(BF16)
            dconv = dy * bg
            pad_c[pl.ds(off + b, CHUNK), :] = dconv
            ddw_ref[0:1, :] += jnp.sum(dconv * prev, axis=0, keepdims=True)
            ddw_ref[1:2, :] += jnp.sum(dconv * here, axis=0, keepdims=True)
            ddw_ref[2:3, :] += jnp.sum(dconv * nxt, axis=0, keepdims=True)
            ddb_ref[0:1, :] += jnp.sum(dconv, axis=0, keepdims=True)

        _for_chunks(segs, first)

        def second(s, st, ln, off, b):
            rows = pl.ds(b, CHUNK)
            ext = pad_c[pl.ds(off - HALO + b, CHUNK + 2 * HALO), :]
            da = (dw_ref[0:1, :] * _shift(ext, -1) + dw_ref[1:2, :] * ext + dw_ref[2:3, :] * _shift(ext, 1))
            da = da[HALO:HALO + CHUNK]
            dpre_ref[1, rows, :] = (da * pre_ref[2, rows, :]).astype(BF16)
            dpre_ref[2, rows, :] = (da * pre_ref[1, rows, :]).astype(BF16)

        _for_chunks(segs, second)

    quad = pl.BlockSpec((4, t, cb), lambda j: (0, 0, j))
    rows8 = pl.BlockSpec((8, cb), lambda j: (0, j))
    return _pc(
        body, name=name, grid=(width // cb,),
        in_specs=[pl.BlockSpec((t, cb), lambda j: (0, j)), quad, rows8, pl.BlockSpec((1, cb), lambda j: (0, j))],
        out_specs=[quad, rows8, rows8],
        out_shape=[jax.ShapeDtypeStruct((4, t, width), BF16), jax.ShapeDtypeStruct((8, width), F32),
                   jax.ShapeDtypeStruct((8, width), F32)],
        scratch_shapes=[pltpu.VMEM((t + 2 * HALO, cb), F32), pltpu.VMEM((t + 2 * HALO, cb), F32)],
        compiler_params=_params("parallel"),
    )(dz, pre, dw, db)


PAIR_TILES = 2 * WIN_ROWS - 2


def _pair_geometry():
    lane = lax.broadcasted_iota(jnp.int32, (GRID_W, LANES), 1)
    qcol = lax.broadcasted_iota(jnp.int32, (GRID_W, LANES), 0)
    low = lane < GRID_W
    kcol = jnp.where(low, lane, lane - GRID_W)
    start = jnp.clip(qcol - WIN_COLS // 2, 0, GRID_W - WIN_COLS)
    inside = (kcol >= start) & (kcol < start + WIN_COLS)
    return low, inside


def _bias_tiles(rpb_ref, rows_ref, tiles_ref, inside):
    for h in range(2):
        rows = rpb_ref[h]
        rows_ref[h] = (pltpu.roll(rows, LANES - (WIN_COLS - 1), 1)
                       + pltpu.roll(pltpu.roll(rows, GRID_W - (WIN_COLS - 1), 1), 2 * WIN_ROWS - 1, 0))
        for t in range(PAIR_TILES):
            both = jnp.broadcast_to(rows_ref[h, t:t + 1, :], (GRID_W, LANES))
            tiles_ref[h, t] = jnp.where(inside, pltpu.roll(both, 0, 1, stride=1, stride_axis=0), MASKED)


def _bias_tiles_grad(dtiles_ref, drpb_ref):
    n = PAIR_TILES * GRID_W
    qcol = lax.broadcasted_iota(jnp.int32, (n, LANES), 0) & (GRID_W - 1)
    lane = lax.broadcasted_iota(jnp.int32, (1, LANES), 1)
    zero = jnp.zeros((1, LANES), F32)
    for h in range(2):
        v = pltpu.roll(dtiles_ref[h].reshape(n, LANES), WIN_COLS - 1, 1)
        for bit in range(6):
            v = jnp.where((qcol >> bit) & 1 == 1, pltpu.roll(v, LANES - (1 << bit), 1), v)
        sums = [jnp.sum(v[t * GRID_W:(t + 1) * GRID_W], axis=0, keepdims=True) for t in range(PAIR_TILES)]
        for r in range(2 * WIN_ROWS):
            here = sums[r] if r < PAIR_TILES else zero
            prev = pltpu.roll(sums[r - 1], GRID_W, 1) if 1 <= r <= PAIR_TILES else zero
            drpb_ref[h, r:r + 1, :] = jnp.where(lane < 2 * WIN_COLS - 1, here + prev, 0.0)


def _attn_rows(r, n_rows):
    first = jnp.clip(r - WIN_ROWS // 2, 0, n_rows - WIN_ROWS)
    return first, first - r + WIN_ROWS - 1


def _softmax(s_loc, s_ctx):
    m = jnp.maximum(jnp.max(s_loc, axis=-1, keepdims=True), jnp.max(s_ctx, axis=-1, keepdims=True))
    e_loc, e_ctx = jnp.exp(s_loc - m), jnp.exp(s_ctx - m)
    inv = 1.0 / (jnp.sum(e_loc, axis=-1, keepdims=True) + jnp.sum(e_ctx, axis=-1, keepdims=True))
    return e_loc * inv, e_ctx * inv


def _pair_bias(tiles_ref, j):
    return jnp.concatenate(
        [jnp.concatenate([tiles_ref[h, j + 2 * m] for m in range(WIN_ROWS // 2)], axis=1) for h in range(2)], axis=0)


ROWS_PER_STEP = 8
ROWS_PER_STEP_BWD = 4


def _by_head(tile, low):
    zero = jnp.zeros_like(tile)
    return jnp.concatenate([jnp.where(low, tile, zero), jnp.where(low, zero, tile)], axis=0)


def _merge_heads(stacked, low):
    return jnp.where(low, stacked[:GRID_W], stacked[GRID_W:])


def _attn_items(step, n_rows, q_ref, low, per_step):
    items = []
    for u in range(per_step):
        r = step * per_step + u
        first, j = _attn_rows(r, n_rows)
        rows = pl.ds(pl.multiple_of(r * GRID_W, GRID_W), GRID_W)
        keys = pl.ds(pl.multiple_of(first * GRID_W, GRID_W), WIN_ROWS * GRID_W)
        q = (q_ref[rows, :].astype(F32) * HEAD_DIM ** -0.5).astype(BF16)
        items.append((rows, keys, j, _by_head(q, low)))
    return items


def _attn_fwd(qkv, gate, rpb, seq):
    _, t, width = qkv.shape
    n_rows = seq // GRID_W
    n_ctx = t - seq

    def body(q_ref, k_ref, v_ref, g_ref, rpb_ref, z_ref, o_ref, rows_ref, tiles_ref):
        low, inside = _pair_geometry()
        _bias_tiles(rpb_ref, rows_ref, tiles_ref, inside)
        ctx = pl.ds(seq, n_ctx)

        def step(i, carry):
            items = _attn_items(i, n_rows, q_ref, low, ROWS_PER_STEP)
            k_ctx, v_ctx = k_ref[ctx, :], v_ref[ctx, :]
            scores = [(_dot(q, k_ref[keys, :], 1, 1) + _pair_bias(tiles_ref, j), _dot(q, k_ctx, 1, 1))
                      for _, keys, j, q in items]
            probs = [_softmax(s_loc, s_ctx) for s_loc, s_ctx in scores]
            outs = [_dot(p_loc.astype(BF16), v_ref[keys, :]) + _dot(p_ctx.astype(BF16), v_ctx)
                    for (_, keys, _, _), (p_loc, p_ctx) in zip(items, probs)]
            for (rows, _, _, _), out in zip(items, outs):
                o = _merge_heads(out, low)
                o_ref[rows, :] = o
                z_ref[rows, :] = (o * _silu(g_ref[rows, :])).astype(BF16)
            return carry

        lax.fori_loop(0, n_rows // ROWS_PER_STEP, step, 0)

    def part(p):
        return pl.BlockSpec((None, t, LANES), lambda h: (p, 0, h))

    out = pl.BlockSpec((seq, LANES), lambda h: (0, h))
    return _pc(
        body, name="attn_fwd", grid=(width // LANES,),
        in_specs=[part(0), part(1), part(2), part(0), pl.BlockSpec((2, 2 * WIN_ROWS, LANES), lambda h: (h, 0, 0))],
        out_specs=[out, out],
        out_shape=[jax.ShapeDtypeStruct((seq, width), BF16), jax.ShapeDtypeStruct((seq, width), F32)],
        scratch_shapes=[pltpu.VMEM((2, 2 * WIN_ROWS, LANES), F32), pltpu.VMEM((2, PAIR_TILES, GRID_W, LANES), F32)],
        compiler_params=_params("parallel"),
    )(qkv, qkv, qkv, gate, rpb)


def _attn_bwd(qkv, gate, o, dz, rpb, seq):
    _, t, width = qkv.shape
    n_rows = seq // GRID_W
    n_ctx = t - seq
    heads = 2 * width // LANES

    def body(q_ref, k_ref, v_ref, g_ref, o_ref, dz_ref, rpb_ref, dpre_ref, drpb_ref,
             rows_ref, tiles_ref, dtiles_ref, dk_ref, dv_ref):
        low, inside = _pair_geometry()
        _bias_tiles(rpb_ref, rows_ref, tiles_ref, inside)
        dtiles_ref[...] = jnp.zeros_like(dtiles_ref)
        dk_ref[...] = jnp.zeros_like(dk_ref)
        dv_ref[...] = jnp.zeros_like(dv_ref)
        ctx = pl.ds(seq, n_ctx)

        def step(i, carry):
            items = _attn_items(i, n_rows, q_ref, low, ROWS_PER_STEP_BWD)
            k_ctx, v_ctx = k_ref[ctx, :], v_ref[ctx, :]
            d_outs = []
            for rows, _, _, _ in items:
                g = g_ref[rows, :]
                dzv = dz_ref[rows, :]
                dpre_ref[3, rows, :] = (dzv * o_ref[rows, :] * _dsilu(g)).astype(BF16)
                d_outs.append(_by_head((dzv * _silu(g)).astype(BF16), low))
            scores = [(_dot(q, k_ref[keys, :], 1, 1) + _pair_bias(tiles_ref, j), _dot(q, k_ctx, 1, 1))
                      for _, keys, j, q in items]
            dprobs = [(_dot(d_o, v_ref[keys, :], 1, 1), _dot(d_o, v_ctx, 1, 1))
                      for (_, keys, _, _), d_o in zip(items, d_outs)]
            probs = [_softmax(s_loc, s_ctx) for s_loc, s_ctx in scores]
            dscores = []
            for (p_loc, p_ctx), (dp_loc, dp_ctx) in zip(probs, dprobs):
                delta = (jnp.sum(p_loc * dp_loc, axis=-1, keepdims=True)
                         + jnp.sum(p_ctx * dp_ctx, axis=-1, keepdims=True))
                dscores.append((p_loc * (dp_loc - delta), p_ctx * (dp_ctx - delta)))
            dqs = [_dot(ds_loc.astype(BF16), k_ref[keys, :]) + _dot(ds_ctx.astype(BF16), k_ctx)
                   for (_, keys, _, _), (ds_loc, ds_ctx) in zip(items, dscores)]
            for (rows, _, _, _), dq in zip(items, dqs):
                dpre_ref[0, rows, :] = (_merge_heads(dq, low) * HEAD_DIM ** -0.5).astype(BF16)
            for (_, keys, j, q), d_o, (p_loc, p_ctx), (ds_loc, ds_ctx) in zip(items, d_outs, probs, dscores):
                dk_ref[keys, :] += _dot(ds_loc.astype(BF16), q, 0, 0)
                dk_ref[ctx, :] += _dot(ds_ctx.astype(BF16), q, 0, 0)
                dv_ref[keys, :] += _dot(p_loc.astype(BF16), d_o, 0, 0)
                dv_ref[ctx, :] += _dot(p_ctx.astype(BF16), d_o, 0, 0)
                for h in range(2):
                    for m in range(WIN_ROWS // 2):
                        dtiles_ref[h, j + 2 * m] += ds_loc[h * GRID_W:(h + 1) * GRID_W, m * LANES:(m + 1) * LANES]
            return carry

        lax.fori_loop(0, n_rows // ROWS_PER_STEP_BWD, step, 0)
        dpre_ref[1] = dk_ref[...].astype(BF16)
        dpre_ref[2] = dv_ref[...].astype(BF16)
        dpre_ref[0, ctx, :] = jnp.zeros((n_ctx, LANES), BF16)
        dpre_ref[3, ctx, :] = jnp.zeros((n_ctx, LANES), BF16)
        _bias_tiles_grad(dtiles_ref, drpb_ref)

    def part(p):
        return pl.BlockSpec((None, t, LANES), lambda h: (p, 0, h))

    lat = pl.BlockSpec((seq, LANES), lambda h: (0, h))
    rspec = pl.BlockSpec((2, 2 * WIN_ROWS, LANES), lambda h: (h, 0, 0))
    tiles = pltpu.VMEM((2, PAIR_TILES, GRID_W, LANES), F32)
    return _pc(
        body, name="attn_bwd", grid=(width // LANES,),
        in_specs=[part(0), part(1), part(2), part(0), lat, lat, rspec],
        out_specs=[pl.BlockSpec((4, t, LANES), lambda h: (0, 0, h)), rspec],
        out_shape=[jax.ShapeDtypeStruct((4, t, width), BF16), jax.ShapeDtypeStruct((heads, 2 * WIN_ROWS, LANES), F32)],
        scratch_shapes=[pltpu.VMEM((2, 2 * WIN_ROWS, LANES), F32), tiles, tiles,
                        pltpu.VMEM((t, LANES), F32), pltpu.VMEM((t, LANES), F32)],
        compiler_params=_params("parallel"),
    )(qkv, qkv, qkv, gate, o, dz, rpb)


def _adam_update(w, m, v, g):
    m2 = ADAM_B1 * m + (1.0 - ADAM_B1) * g
    v2 = ADAM_B2 * v + (1.0 - ADAM_B2) * (g * g)
    m_hat = m2 / (1.0 - ADAM_B1 ** ADAM_STEP)
    v_hat = v2 / (1.0 - ADAM_B2 ** ADAM_STEP)
    return -ADAM_LR * (m_hat / (jnp.sqrt(v_hat) + ADAM_EPS) + ADAM_WD * w), m2, v2


def _adamw(w, m, v, parts, name):
    rows, cols = w.shape
    tr = _tile(rows, max(8, ADAM_TILE_ELEMS // cols), 8)
    n_parts = len(parts)

    def body(*refs):
        w_ref, m_ref, v_ref = refs[:3]
        part_refs = refs[3:3 + n_parts]
        g_ref, d_ref, nm_ref, nv_ref = refs[3 + n_parts:]
        g = part_refs[0][...].astype(F32)
        for p in part_refs[1:]:
            g = g + p[...].astype(F32)
        g_ref[...] = g
        d_ref[...], nm_ref[...], nv_ref[...] = _adam_update(w_ref[...], m_ref[...], v_ref[...], g)

    tile = pl.BlockSpec((tr, cols), lambda i: (i, 0))
    in_specs, args = [tile, tile, tile], [w, m, v]
    for p in parts:
        if isinstance(p, tuple):
            arr, k = p
            in_specs.append(pl.BlockSpec((None, tr, cols), lambda i, k=k: (k, i, 0)))
            args.append(arr)
        else:
            in_specs.append(tile)
            args.append(p)
    shape = jax.ShapeDtypeStruct((rows, cols), F32)
    return _pc(
        body, name=name, grid=(rows // tr,), in_specs=in_specs, out_specs=[tile] * 4, out_shape=[shape] * 4,
        compiler_params=_params("parallel"),
    )(*args)


def _adamw_layers(w, m, v, landed, name):
    n_layers, rows, cols = w.shape
    tr = _tile(rows, max(8, ADAM_TILE_ELEMS // cols), 8)

    def body(*refs):
        w_ref, m_ref, v_ref = refs[:3]
        part_refs = refs[3:3 + n_layers * N_DEV]
        g_ref, d_ref, nm_ref, nv_ref = refs[3 + n_layers * N_DEV:]
        layer = pl.program_id(0)
        g = None
        for l in range(n_layers):
            s = part_refs[l * N_DEV][...].astype(F32)
            for p in part_refs[l * N_DEV + 1:(l + 1) * N_DEV]:
                s = s + p[...].astype(F32)
            g = s if g is None else jnp.where(layer == l, s, g)
        g_ref[...] = g
        d_ref[...], nm_ref[...], nv_ref[...] = _adam_update(w_ref[...], m_ref[...], v_ref[...], g)

    tile = pl.BlockSpec((None, tr, cols), lambda l, i: (l, i, 0))
    in_specs, args = [tile, tile, tile], [w, m, v]
    for l, arr in enumerate(landed):
        for k in range(N_DEV):
            in_specs.append(pl.BlockSpec((None, tr, cols), lambda ll, i, l=l, k=k: (k, jnp.where(ll == l, i, 0), 0)))
            args.append(arr)
    shape = jax.ShapeDtypeStruct(w.shape, F32)
    return _pc(
        body, name=name, grid=(n_layers, rows // tr), in_specs=in_specs, out_specs=[tile] * 4, out_shape=[shape] * 4,
        compiler_params=_params("arbitrary", "arbitrary"),
    )(*args)


def _adamw_small(states, grads):
    sources, makers = grads
    n, ns = len(states), len(sources)

    def body(*refs):
        src = refs[:ns]
        ins = refs[ns:ns + 3 * n]
        outs = refs[ns + 3 * n:]
        for k in range(n):
            w_ref, m_ref, v_ref = ins[3 * k:3 * k + 3]
            g = makers[k](*src)
            outs[4 * k][...] = g
            outs[4 * k + 1][...], outs[4 * k + 2][...], outs[4 * k + 3][...] = _adam_update(
                w_ref[...], m_ref[...], v_ref[...], g)

    flat = [a for s in states for a in s]
    vmem = pl.BlockSpec(memory_space=pltpu.VMEM)
    res = _pc(
        body, name="adamw_small",
        in_specs=[vmem] * (ns + 3 * n), out_specs=[vmem] * (4 * n),
        out_shape=[jax.ShapeDtypeStruct(s[0].shape, F32) for s in states for _ in range(4)],
        compiler_params=pltpu.CompilerParams(vmem_limit_bytes=VMEM_LIMIT),
    )(*sources, *flat)
    return [res[4 * k:4 * k + 4] for k in range(n)]


def _rows128(a):
    flat = a.reshape(-1)
    pad = (-flat.shape[0]) % LANES
    if pad:
        flat = jnp.concatenate([flat, jnp.zeros((pad,), flat.dtype)])
    return flat.reshape(-1, LANES)


def _pad_rows(a, mult=8):
    pad = (-a.shape[0]) % mult
    if pad:
        a = jnp.concatenate([a, jnp.zeros((pad,) + a.shape[1:], a.dtype)], axis=0)
    return a


def kernel(x, c, ctx, c_ctx, norm_g, ada_w, ada_b, pool_w_in, pool_w_grp, pool_scale, pool_w_out, na_w_in, na_rpb, na_w_out, conv_w_in, conv_dw, conv_db, conv_w_out, final_g, loss_target, m_c_ctx, m_norm_g, m_ada_w, m_ada_b, m_pool_w_in, m_pool_w_grp, m_pool_scale, m_pool_w_out, m_na_w_in, m_na_rpb, m_na_w_out, m_conv_w_in, m_conv_dw, m_conv_db, m_conv_w_out, m_final_g, v_c_ctx, v_norm_g, v_ada_w, v_ada_b, v_pool_w_in, v_pool_w_grp, v_pool_scale, v_pool_w_out, v_na_w_in, v_na_rpb, v_na_w_out, v_conv_w_in, v_conv_dw, v_conv_db, v_conv_w_out, v_final_g):
    xi, yi, ci = _my_place()
    me = 4 * xi + 2 * yi + ci
    seq, d = x.shape[1], x.shape[2]
    n_ctx = ctx.shape[1]
    width = d
    depth = norm_g.shape[0]
    nb = ada_w.shape[2]
    shard = width // N_DEV
    d_rows = d // LANES
    assert seq % CHUNK == 0 and n_ctx % CHUNK == 0 and (seq // GRID_W) % ROWS_PER_STEP == 0 and seq >= WIN_ROWS * GRID_W
    tr = math.gcd(math.gcd(seq, n_ctx), 256)
    x_tiles = seq // tr
    tr_lat = math.gcd(seq, 512)
    lat_tiles = seq // tr_lat

    n_pool = pool_scale.shape[0]
    n_grp = pool_w_grp.shape[1]
    grp = width // n_grp

    small_in = _pad_rows(jnp.concatenate([_rows128(c), pool_scale, conv_dw[0], conv_db], axis=0))
    got = _gather_small(small_in, "gather_inputs")
    r0 = d_rows
    c_all = got[:, :r0].reshape(N_DEV, d)
    scale_full = got[:, r0:r0 + n_pool].transpose(1, 0, 2).reshape(n_pool, width)
    r1 = r0 + n_pool
    taps_full = _pad_rows(got[:, r1:r1 + 3].transpose(1, 0, 2).reshape(3, width))
    bias_full = got[:, r1 + 3:r1 + 4].transpose(1, 0, 2).reshape(1, width)

    cond = jnp.concatenate([c_all, c_ctx[None], jnp.zeros((7, d), F32)], axis=0)
    bias_mine = lax.dynamic_slice(ada_b, (0, me * nb), (depth, nb))
    mod_mine = _mod_fwd(cond, ada_w, bias_mine)
    by_example = jnp.stack([mod_mine[:, :N_DEV].transpose(1, 0, 2),
                            jnp.broadcast_to(mod_mine[:, N_DEV][None], (N_DEV, depth, nb))], axis=2)
    mod_all = _gather_small(by_example.reshape(N_DEV, -1, LANES), "gather_mod", per_dest=True)
    mod_all = mod_all.reshape(N_DEV, depth, 2, nb).transpose(1, 2, 0, 3).reshape(depth, 2, 3, d)
    mod_all = jnp.pad(mod_all, ((0, 0), (0, 0), (0, 5), (0, 0)))
    mods = [mod_all[i] if i < 2 else mod_all[i, :1] for i in range(depth)]

    layer_weights = [[pool_w_in[0], pool_w_grp[0], pool_w_out[0]], [na_w_in[0], na_w_out[0]],
                     [conv_w_in[0], conv_w_out[0]], [pool_w_in[1], pool_w_grp[1], pool_w_out[1]]]
    slot = {(i, t): n for n, (i, t) in enumerate((i, t) for i, ws in enumerate(layer_weights) for t in range(len(ws)))}
    two_level = [(0, 0), (1, 0)]
    weights_sent = _exchange_start(
        [w.astype(BF16) for ws in layer_weights for w in ws], False, mod_all, "weights_start",
        peers=[CHIP_PEERS if key in two_level else ALL_PEERS for key in slot])
    token = weights_sent[-1]

    def landed_weight(i, t, after):
        return _exchange_wait(weights_sent, False, after, f"weights_wait{i}_{t}", which=[slot[i, t]])[0]

    def handed_on(i, after):
        half = [landed_weight(i, 0, after)]
        rest = _forward_start(half, after, f"weights_forward{i}")
        return _forward_wait(rest, rest[-1], f"weights_forward_wait{i}")[0][:, None]

    def as_in(w):
        return w[:, None]

    def as_grp(w):
        return w.transpose(1, 0, 2, 3).reshape(n_grp, grp, grp)

    def as_out(w):
        return w.reshape(width, d)

    both = [(0, seq), (seq, n_ctx)]
    latent = [(0, seq)]

    def grp_slots(g):
        return g.reshape(n_grp, N_DEV, grp // N_DEV, grp).transpose(1, 0, 2, 3).reshape(N_DEV, -1, grp).astype(BF16)

    def send_grads(i, grads):
        return _exchange_start(grads, True, jnp.zeros((8, LANES), F32), f"grads_start{i}")

    xs0 = jnp.concatenate([x[0], ctx[0]], axis=0)
    h0 = _norm_fwd(xs0, norm_g[0:1] + token[0, 0], mods[0], tr, x_tiles, "norm_fwd0")
    pool_in_w0 = handed_on(0, h0)
    pre0 = _proj_in(h0, pool_in_w0, 0, width, "proj_in0")
    pool_grp_w0 = as_grp(landed_weight(0, 1, pre0))
    z0, diff0 = _pool_fwd(pre0, pool_grp_w0, scale_full[0:1], both, "pool_fwd0")
    pool_out_w0 = as_out(landed_weight(0, 2, z0))
    yx0, xs1, h1 = _proj_out(z0, pool_out_w0, xs0, mods[0], tr, x_tiles, "proj_out0", nxt=(norm_g[1:2], mods[1]))

    na_in_w = handed_on(1, h1)
    qkv1, gpre1 = _proj_in_split(h1, na_in_w, 0, width, "proj_in1", 3)
    rpb_rows = jnp.pad(na_rpb[0], ((0, 0), (0, 2 * WIN_ROWS - na_rpb.shape[2]), (0, LANES - na_rpb.shape[3])))
    z1, o1 = _attn_fwd(qkv1, gpre1, rpb_rows, seq)
    na_out_w = as_out(landed_weight(1, 1, z1))
    yx1, x2, h2 = _proj_out(z1, na_out_w, xs1, mods[1], tr_lat, lat_tiles, "proj_out1", nxt=(norm_g[2:3], mods[2]))

    conv_in_w = as_in(landed_weight(2, 0, h2))
    pre2 = _proj_in(h2, conv_in_w, 0, width, "proj_in2")
    z2 = _conv_fwd(pre2, taps_full, bias_full, "conv_fwd")
    conv_out_w = as_out(landed_weight(2, 1, z2))
    yx2, x3, h3 = _proj_out(z2, conv_out_w, x2, mods[2], tr_lat, lat_tiles, "proj_out2", nxt=(norm_g[3:4], mods[3]))

    pool_in_w3 = as_in(landed_weight(3, 0, h3))
    pre3 = _proj_in(h3, pool_in_w3, 0, width, "proj_in3")
    pool_grp_w3 = as_grp(landed_weight(3, 1, pre3))
    z3, diff3 = _pool_fwd(pre3, pool_grp_w3, scale_full[1:2], latent, "pool_fwd3")
    pool_out_w3 = as_out(landed_weight(3, 2, z3))
    yx3, x4 = _proj_out(z3, pool_out_w3, x3, mods[3], tr_lat, lat_tiles, "proj_out3")

    loss_part, dx4, d_final, dyx3, gate3 = _loss_head(x4, loss_target[0], final_g[None], yx3, mods[3], tr_lat)

    dz3, g_pool_out1 = _proj_out_bwd(dyx3, z3, pool_out_w3, "proj_out_bwd3")
    dpre3, g_grp1, g_scale1 = _pool_bwd(dz3, diff3, pre3, pool_grp_w3, scale_full[1:2], latent, "pool_bwd3")
    dh3 = _proj_in_dh(dpre3, pool_in_w3, 0, "proj_in_dh3")
    g_pool_in1 = _grad_w_in(h3, dpre3, pool_w_in.shape[2], "grad_w_in3")
    sent3 = send_grads(3, [g_pool_in1, grp_slots(g_grp1), g_pool_out1.reshape(N_DEV, shard, d)])
    dx3, norm3, dyx2, gate2 = _norm_bwd(x3, dh3, dx4, norm_g[3:4] + sent3[-1][0, 0], mods[3], tr_lat, lat_tiles, "norm_bwd3",
                                        below=(yx2, mods[2]))

    dz2, g_conv_out = _proj_out_bwd(dyx2, z2, conv_out_w, "proj_out_bwd2")
    dpre2, g_taps, g_cbias = _conv_bwd(dz2, pre2, taps_full, bias_full, "conv_bwd")
    dh2 = _proj_in_dh(dpre2, conv_in_w, 0, "proj_in_dh2")
    g_conv_in = _grad_w_in(h2, dpre2, conv_w_in.shape[2], "grad_w_in2")
    sent2 = send_grads(2, [g_conv_in, g_conv_out.reshape(N_DEV, shard, d)])
    dx2, norm2, dyx1, gate1 = _norm_bwd(x2, dh2, dx3, norm_g[2:3] + sent2[-1][0, 0], mods[2], tr_lat, lat_tiles, "norm_bwd2",
                                        below=(yx1, mods[1][:1]))

    dz1, g_na_out = _proj_out_bwd(dyx1, z1, na_out_w, "proj_out_bwd1")
    dpre1, g_rpb = _attn_bwd(qkv1, gpre1, o1, dz1, rpb_rows, seq)
    g_rpb = g_rpb[:, :na_rpb.shape[2], :na_rpb.shape[3]]
    dh1 = _proj_in_dh(dpre1, na_in_w, 0, "proj_in_dh1")
    g_na_in = _grad_w_in(h1, dpre1, na_w_in.shape[2], "grad_w_in1")
    sent1 = send_grads(1, [g_na_in, g_na_out.reshape(N_DEV, shard, d)])
    dxs1, norm1, dyx0, gate0 = _norm_bwd(xs1, dh1, dx2, norm_g[1:2] + sent1[-1][0, 0], mods[1], tr, x_tiles, "norm_bwd1",
                                         res_tiles=x_tiles, below=(yx0, mods[0]))

    dz0, g_pool_out0 = _proj_out_bwd(dyx0, z0, pool_out_w0, "proj_out_bwd0")
    sent0a = _exchange_start([g_pool_out0.reshape(N_DEV, shard, d)], True, jnp.zeros((8, LANES), F32), "grads_start0a")
    dpre0, g_grp0, g_scale0 = _pool_bwd(dz0, diff0, pre0, pool_grp_w0, scale_full[0:1], both, "pool_bwd0")
    g_pool_in0 = _grad_w_in(h0, dpre0, pool_w_in.shape[2], "grad_w_in0")
    dh0 = _proj_in_dh(dpre0, pool_in_w0, 0, "proj_in_dh0", after=sent0a[-1])
    dx0, norm0 = _norm_bwd(xs0, dh0, dxs1, norm_g[0:1], mods[0], tr, x_tiles, "norm_bwd0", out_tiles=x_tiles)
    grad_x = dx0[None]

    norms, gates = [norm0, norm1, norm2, norm3], [gate0, gate1, gate2, gate3]
    zero_d = jnp.zeros((d,), F32)
    dm_rows = [jnp.concatenate([norms[i][0, 0], norms[i][0, 1], gates[i][0, 0]]) for i in range(depth)]
    dm_rows.append(jnp.concatenate([norm0[1, 0], norm0[1, 1], gate0[1, 0]]))
    dm_rows.append(jnp.concatenate([norm1[1, 0], norm1[1, 1], zero_d]))
    dm_local = jnp.stack(dm_rows + [jnp.zeros((3 * d,), F32)] * 2)
    g_norm_part = jnp.stack([norm0[0, 2] + norm0[1, 2], norm1[0, 2] + norm1[1, 2], norm2[0, 2], norm3[0, 2]])
    pieces = [_rows128(dm_local), _rows128(g_norm_part), _rows128(d_final[0]), _pad_rows(_rows128(g_rpb)), loss_part]
    marks = np.cumsum([0] + [p.shape[0] for p in pieces])
    by_owner = [a.reshape(-1, N_DEV, shard).transpose(1, 0, 2) for a in (g_scale0, g_scale1, g_taps[:3], g_cbias[0:1])]
    by_owner = jnp.concatenate(by_owner + [jnp.zeros((N_DEV, 8 - n_pool - 4, shard), F32)], axis=1)
    small_sent = _exchange_start([jnp.concatenate(pieces, axis=0), by_owner], [False, True], jnp.zeros((8, LANES), F32),
                                 "small_grads_start")
    sent0b = _exchange_start([g_pool_in0, grp_slots(g_grp0)], True, small_sent[-1], "grads_start0b")

    def big(parts, w, m, v, name):
        shape = w.shape
        view = (-1, shape[-1])
        parts = [(parts.reshape((N_DEV,) + w.reshape(view).shape), k) for k in range(N_DEV)]
        return [r.reshape(shape) for r in _adamw(w.reshape(view), m.reshape(view), v.reshape(view), parts, name)]

    in3, grp3, out3 = _exchange_wait(sent3, True, sent0b[-1], "grads_wait3")
    in2, out2 = _exchange_wait(sent2, True, sent0b[-1], "grads_wait2")
    in1, out1 = _exchange_wait(sent1, True, sent0b[-1], "grads_wait1")
    res = {}
    res["na_w_in"] = [r[None] for r in big(in1, na_w_in[0], m_na_w_in[0], v_na_w_in[0], "adamw_na_in")]
    res["na_w_out"] = [r[None] for r in big(out1, na_w_out[0], m_na_w_out[0], v_na_w_out[0], "adamw_na_out")]
    res["conv_w_in"] = [r[None] for r in big(in2, conv_w_in[0], m_conv_w_in[0], v_conv_w_in[0], "adamw_conv_in")]
    res["conv_w_out"] = [r[None] for r in big(out2, conv_w_out[0], m_conv_w_out[0], v_conv_w_out[0], "adamw_conv_out")]

    done = [res[n][0] for n in ("na_w_in", "na_w_out", "conv_w_in", "conv_w_out")]
    small_out, owned = _exchange_wait(small_sent, [False, True], done, "small_grads_wait")
    loss = jnp.sum(small_out[:, marks[4], 0])
    dm_all = small_out[:, :marks[1]].reshape(N_DEV, 8, 3 * d).transpose(1, 0, 2)
    dm_mine = lax.dynamic_slice(dm_all, (0, 0, me * nb), (8, N_DEV, nb))
    g_ada_w, g_ada_b, cctx_part, dsilu_cond = _mod_bwd(cond, ada_w, dm_all, dm_mine)
    cctx_all = _gather_small(_rows128(cctx_part[0]), "gather_cctx")

    def summed(ref, lo, hi):
        g = ref[0, lo:hi, :]
        for k in range(1, N_DEV):
            g = g + ref[k, lo:hi, :]
        return g

    makers = [
        lambda so, ow, cc, ab, ds: summed(cc, 0, d_rows) * ds[...],
        lambda so, ow, cc, ab, ds: summed(so, marks[1], marks[2]),
        lambda so, ow, cc, ab, ds: ab[...],
        lambda so, ow, cc, ab, ds: summed(so, marks[2], marks[3]),
        lambda so, ow, cc, ab, ds: summed(so, marks[3], marks[4]),
        lambda so, ow, cc, ab, ds: summed(ow, 0, n_pool),
        lambda so, ow, cc, ab, ds: summed(ow, n_pool, n_pool + 3),
        lambda so, ow, cc, ab, ds: summed(ow, n_pool + 3, n_pool + 4),
    ]
    rpb_rows128 = lambda a: _pad_rows(_rows128(a))
    views = [_rows128] * 4 + [rpb_rows128] + [lambda a: a.reshape(-1, LANES)] * 3
    small = [(c_ctx, m_c_ctx, v_c_ctx), (norm_g, m_norm_g, v_norm_g), (ada_b, m_ada_b, v_ada_b),
             (final_g, m_final_g, v_final_g), (na_rpb, m_na_rpb, v_na_rpb), (pool_scale, m_pool_scale, v_pool_scale),
             (conv_dw, m_conv_dw, v_conv_dw), (conv_db, m_conv_db, v_conv_db)]
    states = [tuple(view(a) for a in triple) for view, triple in zip(views, small)]
    sources = (small_out, owned, cctx_all, _rows128(g_ada_b), _rows128(dsilu_cond[8]))
    small_res = _adamw_small(states, (sources, makers))
    names = ["c_ctx", "norm_g", "ada_b", "final_g", "na_rpb", "pool_scale", "conv_dw", "conv_db"]
    for name, (w, _, _), outs4 in zip(names, small, small_res):
        res[name] = [r.reshape(-1)[:w.size].reshape(w.shape) for r in outs4]

    res["ada_w"] = [r.reshape(ada_w.shape) for r in _adamw(
        ada_w.reshape(-1, nb), m_ada_w.reshape(-1, nb), v_ada_w.reshape(-1, nb), [g_ada_w.reshape(-1, nb)], "adamw_ada_w")]

    out0, = _exchange_wait(sent0a, True, small_res[0][0], "grads_wait0a")
    in0, grp0 = _exchange_wait(sent0b, True, small_res[0][0], "grads_wait0b")
    def both_layers(first, second, w, m, v, name):
        view = (w.shape[0], -1, w.shape[-1])
        landed = [first.reshape((N_DEV,) + w.reshape(view).shape[1:]), second.reshape((N_DEV,) + w.reshape(view).shape[1:])]
        return [r.reshape(w.shape) for r in _adamw_layers(w.reshape(view), m.reshape(view), v.reshape(view), landed, name)]

    res["pool_w_in"] = both_layers(in0, in3, pool_w_in, m_pool_w_in, v_pool_w_in, "adamw_pool_in")
    res["pool_w_grp"] = both_layers(grp0, grp3, pool_w_grp, m_pool_w_grp, v_pool_w_grp, "adamw_pool_grp")
    res["pool_w_out"] = both_layers(out0, out3, pool_w_out, m_pool_w_out, v_pool_w_out, "adamw_pool_out")

    order = ["c_ctx", "norm_g", "ada_w", "ada_b", "pool_w_in", "pool_w_grp", "pool_scale", "pool_w_out", "na_w_in",
             "na_rpb", "na_w_out", "conv_w_in", "conv_dw", "conv_db", "conv_w_out", "final_g"]
    outs = [loss, grad_x]
    for j in range(4):
        outs += [res[n][j] for n in order]
    return tuple(outs)
```

```python
import math

import numpy as np
import jax
import jax.numpy as jnp
from jax import lax
from jax.experimental import pallas as pl
from jax.experimental.pallas import tpu as pltpu

F32 = jnp.float32
BF16 = jnp.bfloat16
N_DEV = 8
LANES = 128
RMS_EPS = 1e-6
GRID_W = 64
WIN_ROWS = 8
WIN_COLS = 16
HEAD_DIM = 64
POOL_WINDOWS = (2, 4, 8, 16)
HALO = 8
CHUNK = 256
MASKED = -1e30
ADAM_LR = 0.001
ADAM_B1 = 0.9
ADAM_B2 = 0.999
ADAM_EPS = 1e-08
ADAM_WD = 0.01
ADAM_STEP = 10
VMEM_LIMIT = 56 * 1024 * 1024
ADAM_TILE_ELEMS = 256 * 1024
MESH = pl.DeviceIdType.MESH
ANY = pl.BlockSpec(memory_space=pl.ANY)
HBM = pl.BlockSpec(memory_space=pltpu.HBM)
SEM = pl.BlockSpec(memory_space=pltpu.SEMAPHORE)
EFFECT = pltpu.SideEffectType.DATAFLOW_SIDE_EFFECTING


def _pc(body, *, name, **kw):
    return pl.pallas_call(body, name=name, **kw)


def _params(*sem):
    return pltpu.CompilerParams(dimension_semantics=sem if sem else None, vmem_limit_bytes=VMEM_LIMIT)


def _dot(a, b, ca=1, cb=0, precision=None):
    return lax.dot_general(a, b, (((ca,), (cb,)), ((), ())), preferred_element_type=F32, precision=precision)


def _tile(n, pref, unit=LANES):
    best = None
    for t in range(unit, min(n, pref) + 1, unit):
        if n % t == 0:
            best = t
    return best if best is not None else n


def _sigmoid(x):
    return 1.0 / (1.0 + jnp.exp(-x))


def _silu(x):
    return x * _sigmoid(x)


def _dsilu(x):
    s = _sigmoid(x)
    return s * (1.0 + x * (1.0 - s))


def _my_place():
    return lax.axis_index("x"), lax.axis_index("y"), lax.axis_index("c")


def _flip(v, f):
    return 1 - v if f else v


def _gather_small(block, name, per_dest=False):
    rows, cols = block.shape[-2:]

    def body(x_ref, out_ref, send_sems, recv_sems):
        x, y, c = _my_place()
        me = 4 * x + 2 * y + c
        out_ref[me] = x_ref[me] if per_dest else x_ref[...]
        copies = []
        for k in range(1, N_DEV):
            peer = (_flip(x, k & 4), _flip(y, k & 2), _flip(c, k & 1))
            dest = 4 * peer[0] + 2 * peer[1] + peer[2]
            cp = pltpu.make_async_remote_copy(
                src_ref=x_ref.at[dest] if per_dest else x_ref, dst_ref=out_ref.at[me],
                send_sem=send_sems.at[k - 1], recv_sem=recv_sems.at[k - 1], device_id=peer, device_id_type=MESH)
            cp.start()
            copies.append(cp)
        for cp in copies:
            cp.wait()

    return _pc(
        body, name=name,
        out_shape=jax.ShapeDtypeStruct((N_DEV, rows, cols), block.dtype),
        in_specs=[pl.BlockSpec(memory_space=pltpu.VMEM)],
        out_specs=pl.BlockSpec(memory_space=pltpu.VMEM),
        scratch_shapes=[pltpu.SemaphoreType.DMA((N_DEV - 1,)), pltpu.SemaphoreType.DMA((N_DEV - 1,))],
    )(block)


ALL_PEERS = tuple(range(N_DEV))
CHIP_PEERS = (0, 1, 2, 4, 6)
OTHER_CHIPS = (2, 4, 6)


def _flag(per_dest, t):
    return per_dest[t] if isinstance(per_dest, (list, tuple)) else per_dest


def _peer(k):
    x, y, c = _my_place()
    peer = (_flip(x, k & 4), _flip(y, k & 2), _flip(c, k & 1))
    return peer, 4 * peer[0] + 2 * peer[1] + peer[2]


def _peer_lists(peers, nt):
    return list(peers) if isinstance(peers, list) else [peers] * nt


def _exchange_copies(srcs, lands, send_sems, recv_sems, per_dest, peers=ALL_PEERS):
    x, y, c = _my_place()
    me = 4 * x + 2 * y + c
    copies = []
    for t, (src, land, ks) in enumerate(zip(srcs, lands, _peer_lists(peers, len(srcs)))):
        for k in ks:
            peer, dest = _peer(k)
            s = len(copies)
            copies.append(pltpu.make_async_remote_copy(
                src_ref=src.at[dest] if _flag(per_dest, t) else src, dst_ref=land.at[me],
                send_sem=send_sems[s], recv_sem=recv_sems[s], device_id=peer, device_id_type=MESH))
    return copies


def _forward_copies(lands, send_sems, recv_sems):
    sibling, _ = _peer(1)
    copies = []
    for t, land in enumerate(lands):
        for n, k in enumerate(OTHER_CHIPS):
            _, slot = _peer(k)
            s = t * len(OTHER_CHIPS) + n
            copies.append(pltpu.make_async_remote_copy(
                src_ref=land.at[slot], dst_ref=land.at[slot], send_sem=send_sems[s], recv_sem=recv_sems[s],
                device_id=sibling, device_id_type=MESH))
    return copies


def _forward_start(lands, after, name):
    nt = len(lands)
    ns = nt * len(OTHER_CHIPS)

    def body(*refs):
        ins, outs = refs[:nt + 1], refs[nt + 1:]
        for cp in _forward_copies(ins[:nt], outs[:ns], outs[ns:2 * ns]):
            cp.start()
        outs[-1][...] = jnp.zeros_like(outs[-1])

    res = _pc(
        body, name=name,
        out_shape=(*[pltpu.SemaphoreType.DMA(())] * (2 * ns), *[pltpu.HBM(a.shape, a.dtype) for a in lands],
                   jax.ShapeDtypeStruct((8, LANES), F32)),
        in_specs=[HBM] * nt + [ANY],
        out_specs=(*[SEM] * (2 * ns), *[HBM] * nt, pl.BlockSpec(memory_space=pltpu.VMEM)),
        input_output_aliases={i: 2 * ns + i for i in range(nt)},
        compiler_params=pltpu.CompilerParams(has_side_effects=EFFECT),
    )(*lands, after)
    return list(res[:ns]), list(res[ns:2 * ns]), list(res[2 * ns:2 * ns + nt]), res[-1]


def _forward_wait(state, after, name):
    send_sems, recv_sems, lands, _ = state
    nt, ns = len(lands), len(send_sems)

    def body(*refs):
        sems = refs[nt:nt + 2 * ns]
        for cp in _forward_copies(refs[:nt], sems[:ns], sems[ns:]):
            cp.wait_send()
            cp.wait_recv()

    res = _pc(
        body, name=name,
        out_shape=tuple(pltpu.HBM(a.shape, a.dtype) for a in lands),
        in_specs=[HBM] * nt + [SEM] * (2 * ns) + [ANY],
        out_specs=tuple([HBM] * nt),
        input_output_aliases={i: i for i in range(nt)},
        compiler_params=pltpu.CompilerParams(has_side_effects=EFFECT),
    )(*lands, *send_sems, *recv_sems, after)
    return list(res)


def _exchange_start(srcs, per_dest, after, name, peers=ALL_PEERS):
    nt = len(srcs)
    peers = _peer_lists(peers, nt)
    ns = sum(len(ks) for ks in peers)
    lands = [lax.empty((N_DEV,) + (s.shape[1:] if _flag(per_dest, t) else s.shape), s.dtype) for t, s in enumerate(srcs)]

    def body(*refs):
        ins, outs = refs[:2 * nt + 1], refs[2 * nt + 1:]
        for cp in _exchange_copies(ins[:nt], ins[nt:2 * nt], outs[:ns], outs[ns:2 * ns], per_dest, peers):
            cp.start()
        outs[-1][...] = jnp.zeros_like(outs[-1])

    hbm = [pltpu.with_memory_space_constraint(a, pltpu.HBM) for a in list(srcs) + lands]
    res = _pc(
        body, name=name,
        out_shape=(*[pltpu.SemaphoreType.DMA(())] * (2 * ns), *[pltpu.HBM(a.shape, a.dtype) for a in hbm],
                   jax.ShapeDtypeStruct((8, LANES), F32)),
        in_specs=[HBM] * (2 * nt) + [ANY],
        out_specs=(*[SEM] * (2 * ns), *[HBM] * (2 * nt), pl.BlockSpec(memory_space=pltpu.VMEM)),
        input_output_aliases={i: 2 * ns + i for i in range(2 * nt)},
        compiler_params=pltpu.CompilerParams(has_side_effects=EFFECT),
    )(*hbm, after)
    sems, rest = res[:2 * ns], res[2 * ns:]
    return list(sems[:ns]), list(sems[ns:]), list(rest[:nt]), list(rest[nt:2 * nt]), peers, rest[-1]


def _exchange_wait(state, per_dest, after, name, which=None):
    send_sems, recv_sems, srcs, lands, peers, _ = state
    which = list(range(len(srcs))) if which is None else which
    per_dest = [_flag(per_dest, t) for t in which]
    after = list(after) if isinstance(after, (list, tuple)) else [after]
    first = np.cumsum([0] + [len(ks) for ks in peers])
    pick = [first[t] + n for t in which for n in range(len(peers[t]))]
    peers = [peers[t] for t in which]
    send_sems, recv_sems = [send_sems[s] for s in pick], [recv_sems[s] for s in pick]
    srcs, lands = [srcs[t] for t in which], [lands[t] for t in which]
    nt = len(srcs)
    ns = len(send_sems)

    def body(*refs):
        sems = refs[2 * nt:2 * nt + 2 * ns]
        for cp in _exchange_copies(refs[:nt], refs[nt:2 * nt], sems[:ns], sems[ns:], per_dest, peers):
            cp.wait_send()
            cp.wait_recv()

    thru = list(srcs) + list(lands)
    res = _pc(
        body, name=name,
        out_shape=tuple(pltpu.HBM(a.shape, a.dtype) for a in thru),
        in_specs=[HBM] * (2 * nt) + [SEM] * (2 * ns) + [ANY] * len(after),
        out_specs=tuple([HBM] * (2 * nt)),
        input_output_aliases={i: i for i in range(2 * nt)},
        compiler_params=pltpu.CompilerParams(has_side_effects=EFFECT),
    )(*thru, *send_sems, *recv_sems, *after)
    return list(res[nt:])


def _mod_fwd(cond, ada_w, bias):
    depth, d, nb = ada_w.shape

    def body(c_ref, w_ref, b_ref, o_ref):
        s = _silu(c_ref[...]).astype(BF16)
        o_ref[...] = _dot(s, w_ref[...].astype(BF16)) + b_ref[...]

    return _pc(
        body, name="mod_fwd", grid=(depth,),
        in_specs=[pl.BlockSpec((16, d), lambda i: (0, 0)), pl.BlockSpec((None, d, nb), lambda i: (i, 0, 0)),
                  pl.BlockSpec((None, 1, nb), lambda i: (i, 0, 0))],
        out_specs=pl.BlockSpec((None, 16, nb), lambda i: (i, 0, 0)),
        out_shape=jax.ShapeDtypeStruct((depth, 16, nb), F32),
        compiler_params=_params("parallel"),
    )(cond, ada_w, bias.reshape(depth, 1, nb))


def _mod_bwd(cond, ada_w, dm_all, dm_mine):
    depth, d, nb = ada_w.shape
    d3 = dm_all.shape[-1]

    def body(c_ref, w_ref, all_ref, call_ref, mine_ref, cmine_ref, gw_ref, gb_ref, part_ref, ds_ref):
        i = pl.program_id(0)
        cond_v = c_ref[...]
        s = _silu(cond_v).astype(BF16)
        has_ctx = jnp.where(i < 2, 1.0, 0.0)
        tot_all = jnp.sum(call_ref[...], axis=0, keepdims=True) * has_ctx
        tot_mine = jnp.broadcast_to(jnp.sum(cmine_ref[...], axis=0, keepdims=True) * has_ctx, (8, nb)).astype(BF16)
        gb_ref[...] = jnp.sum(all_ref[...], axis=0, keepdims=True) + tot_all
        gw_ref[...] = _dot(s[0:8], mine_ref[...].astype(BF16), 0, 0) + _dot(s[8:16], tot_mine, 0, 0)
        part = _dot(tot_mine, w_ref[...].astype(BF16), 1, 1)

        @pl.when(i == 0)
        def _():
            part_ref[...] = jnp.zeros_like(part_ref)
            ds_ref[...] = _dsilu(cond_v)

        part_ref[...] += part

    def rows(width, which):
        return pl.BlockSpec((None, N_DEV, width), which)

    layer = lambda i: (i, 0, 0)
    ctx_layer = lambda i: (jnp.minimum(i, 1) + 4, 0, 0)
    return _pc(
        body, name="mod_bwd", grid=(depth,),
        in_specs=[pl.BlockSpec((16, d), lambda i: (0, 0)), pl.BlockSpec((None, d, nb), layer),
                  rows(d3, layer), rows(d3, ctx_layer), rows(nb, layer), rows(nb, ctx_layer)],
        out_specs=[pl.BlockSpec((None, d, nb), layer), pl.BlockSpec((None, 1, d3), layer),
                   pl.BlockSpec((8, d), lambda i: (0, 0)), pl.BlockSpec((16, d), lambda i: (0, 0))],
        out_shape=[jax.ShapeDtypeStruct((depth, d, nb), F32), jax.ShapeDtypeStruct((depth, 1, d3), F32),
                   jax.ShapeDtypeStruct((8, d), F32), jax.ShapeDtypeStruct((16, d), F32)],
        compiler_params=_params("arbitrary"),
    )(cond, ada_w, dm_all, dm_all, dm_mine, dm_mine)


def _norm_fwd(xs, g, mod, tr, seg_tiles, name):
    t, d = xs.shape

    def body(x_ref, g_ref, mod_ref, h_ref):
        x = x_ref[...]
        r = lax.rsqrt(jnp.mean(x * x, axis=-1, keepdims=True) + RMS_EPS)
        y = (x * r) * g_ref[...]
        h_ref[...] = (y * (1.0 + mod_ref[1:2, :]) + mod_ref[0:1, :]).astype(BF16)

    return _pc(
        body, name=name, grid=(t // tr,),
        in_specs=[pl.BlockSpec((tr, d), lambda i: (i, 0)), pl.BlockSpec((1, d), lambda i: (0, 0)),
                  pl.BlockSpec((None, 8, d), lambda i: (i // seg_tiles, 0, 0))],
        out_specs=pl.BlockSpec((tr, d), lambda i: (i, 0)),
        out_shape=jax.ShapeDtypeStruct((t, d), BF16),
        compiler_params=_params("parallel"),
    )(xs, g, mod)


def _resid_grad(dx, i, seg_tiles, yx_ref, gate_ref, dyx_ref, gsum_ref):
    dyx_ref[...] = (dx * gate_ref[2:3, :]).astype(BF16)

    @pl.when(i % seg_tiles == 0)
    def _():
        gsum_ref[...] = jnp.zeros_like(gsum_ref)

    gsum_ref[0:1, :] += jnp.sum(dx * yx_ref[...], axis=0, keepdims=True)


def _norm_bwd(xs, dh, dres, g, mod, tr, seg_tiles, name, res_tiles=None, out_tiles=None, below=None):
    t, d = xs.shape
    n_tiles = t // tr
    res_tiles = n_tiles if res_tiles is None else res_tiles
    out_tiles = n_tiles if out_tiles is None else out_tiles

    def body(x_ref, dh_ref, dres_ref, g_ref, mod_ref, *rest):
        i = pl.program_id(0)
        x = x_ref[...]
        r = lax.rsqrt(jnp.mean(x * x, axis=-1, keepdims=True) + RMS_EPS)
        xn = x * r
        dhv = dh_ref[...]
        gain = g_ref[...]
        one_scale = 1.0 + mod_ref[1:2, :]
        dxn = dhv * (gain * one_scale)
        dx = r * (dxn - xn * jnp.mean(dxn * xn, axis=-1, keepdims=True))
        if res_tiles == n_tiles:
            dx = dx + dres_ref[...]
        else:
            dx = dx + jnp.where(i < res_tiles, dres_ref[...], 0.0)
        if below is None:
            dx_ref, sum_ref = rest
        else:
            yx_ref, gate_ref, dx_ref, sum_ref, dyx_ref, gsum_ref = rest
            _resid_grad(dx, i, seg_tiles, yx_ref, gate_ref, dyx_ref, gsum_ref)
        if out_tiles == n_tiles:
            dx_ref[...] = dx
        else:
            @pl.when(i < out_tiles)
            def _():
                dx_ref[...] = dx

        @pl.when(i % seg_tiles == 0)
        def _():
            sum_ref[...] = jnp.zeros_like(sum_ref)

        sum_ref[0:1, :] += jnp.sum(dhv, axis=0, keepdims=True)
        sum_ref[1:2, :] += jnp.sum(dhv * (xn * gain), axis=0, keepdims=True)
        sum_ref[2:3, :] += jnp.sum(dhv * one_scale * xn, axis=0, keepdims=True)

    row = pl.BlockSpec((tr, d), lambda i: (i, 0))
    seg = pl.BlockSpec((None, 8, d), lambda i: (i // seg_tiles, 0, 0))
    in_specs = [row, row, pl.BlockSpec((tr, d), lambda i: (jnp.minimum(i, res_tiles - 1), 0)),
                pl.BlockSpec((1, d), lambda i: (0, 0)), seg]
    out_specs = [pl.BlockSpec((tr, d), lambda i: (jnp.minimum(i, out_tiles - 1), 0)), seg]
    out_shape = [jax.ShapeDtypeStruct((out_tiles * tr, d), F32), jax.ShapeDtypeStruct((mod.shape[0], 8, d), F32)]
    args = [xs, dh, dres, g, mod]
    if below is not None:
        in_specs += [row, seg]
        out_specs += [row, seg]
        out_shape += [jax.ShapeDtypeStruct((t, d), BF16), jax.ShapeDtypeStruct((below[1].shape[0], 8, d), F32)]
        args += list(below)
    return _pc(
        body, name=name, grid=(n_tiles,), in_specs=in_specs, out_specs=out_specs, out_shape=out_shape,
        compiler_params=_params("arbitrary"),
    )(*args)


def _loss_head(xs, target, g, yx, mod, tr):
    t, d = xs.shape

    def body(x_ref, t_ref, g_ref, yx_ref, gate_ref, loss_ref, dx_ref, dg_ref, dyx_ref, gsum_ref):
        i = pl.program_id(0)
        x = x_ref[...]
        r = lax.rsqrt(jnp.mean(x * x, axis=-1, keepdims=True) + RMS_EPS)
        xn = x * r
        gain = g_ref[...]
        err = xn * gain - t_ref[...]
        dy = err * (1.0 / d)
        dxn = dy * gain
        dx = r * (dxn - xn * jnp.mean(dxn * xn, axis=-1, keepdims=True))
        dx_ref[...] = dx
        _resid_grad(dx, i, t // tr, yx_ref, gate_ref, dyx_ref, gsum_ref)

        @pl.when(i == 0)
        def _():
            loss_ref[...] = jnp.zeros_like(loss_ref)
            dg_ref[...] = jnp.zeros_like(dg_ref)

        loss_ref[...] += 0.5 * jnp.sum(jnp.mean(err * err, axis=-1, keepdims=True))
        dg_ref[0:1, :] += jnp.sum(dy * xn, axis=0, keepdims=True)

    row = pl.BlockSpec((tr, d), lambda i: (i, 0))
    seg = pl.BlockSpec((None, 8, d), lambda i: (0, 0, 0))
    return _pc(
        body, name="loss_head", grid=(t // tr,),
        in_specs=[row, row, pl.BlockSpec((1, d), lambda i: (0, 0)), row, seg],
        out_specs=[pl.BlockSpec((8, LANES), lambda i: (0, 0)), row, pl.BlockSpec((8, d), lambda i: (0, 0)), row, seg],
        out_shape=[jax.ShapeDtypeStruct((8, LANES), F32), jax.ShapeDtypeStruct((t, d), F32),
                   jax.ShapeDtypeStruct((8, d), F32), jax.ShapeDtypeStruct((t, d), BF16),
                   jax.ShapeDtypeStruct((1, 8, d), F32)],
        compiler_params=_params("arbitrary"),
    )(xs, target, g, yx, mod)


def _proj_in(h, w, layer, width, name, blocks=None, dtype=F32):
    t, d = h.shape
    n8 = w.shape[-1]
    first, count = blocks if blocks is not None else (0, N_DEV)
    per_part = width // n8
    assert first % per_part == 0 and count % per_part == 0
    tm = _tile(t, 1152)

    def body(a_ref, b_ref, o_ref):
        a = a_ref[...]
        for s in range(per_part):
            o_ref[:, s * n8:(s + 1) * n8] = _dot(a, b_ref[s]).astype(dtype)

    return _pc(
        body, name=name, grid=(t // tm, count // per_part),
        in_specs=[pl.BlockSpec((tm, d), lambda i, j: (i, 0)),
                  pl.BlockSpec((per_part, None, d, n8), lambda i, j: (first // per_part + j, layer, 0, 0))],
        out_specs=pl.BlockSpec((None, tm, width), lambda i, j: (j, i, 0)),
        out_shape=jax.ShapeDtypeStruct((count // per_part, t, width), dtype),
        compiler_params=_params("parallel", "parallel"),
    )(h, w)


def _proj_in_split(h, w, layer, width, name, narrow):
    t, d = h.shape
    n8 = w.shape[-1]
    per_part = width // n8
    parts = N_DEV // per_part
    tm = _tile(t, 1152)

    def body(a_ref, b_ref, lo_ref, hi_ref):
        j = pl.program_id(1)
        a = a_ref[...]
        cols = [_dot(a, b_ref[s]) for s in range(per_part)]

        @pl.when(j < narrow)
        def _():
            for s in range(per_part):
                lo_ref[:, s * n8:(s + 1) * n8] = cols[s].astype(BF16)

        @pl.when(j >= narrow)
        def _():
            for s in range(per_part):
                hi_ref[:, s * n8:(s + 1) * n8] = cols[s]

    return _pc(
        body, name=name, grid=(t // tm, parts),
        in_specs=[pl.BlockSpec((tm, d), lambda i, j: (i, 0)),
                  pl.BlockSpec((per_part, None, d, n8), lambda i, j: (j, layer, 0, 0))],
        out_specs=[pl.BlockSpec((None, tm, width), lambda i, j: (jnp.minimum(j, narrow - 1), i, 0)),
                   pl.BlockSpec((None, tm, width), lambda i, j: (jnp.maximum(j - narrow, 0), i, 0))],
        out_shape=[jax.ShapeDtypeStruct((narrow, t, width), BF16), jax.ShapeDtypeStruct((parts - narrow, t, width), F32)],
        compiler_params=_params("parallel", "arbitrary"),
    )(h, w)


def _proj_out(z, w, res, mod, tm, seg_tiles, name, nxt=None):
    t, k = z.shape
    d = w.shape[1]

    def body(z_ref, w_ref, res_ref, mod_ref, *rest):
        yx = _dot(z_ref[...], w_ref[...])
        x = res_ref[...] + mod_ref[2:3, :] * yx
        if nxt is None:
            yx_ref, x_ref = rest
        else:
            g_ref, nmod_ref, yx_ref, x_ref, h_ref = rest
            r = lax.rsqrt(jnp.mean(x * x, axis=-1, keepdims=True) + RMS_EPS)
            h_ref[...] = (((x * r) * g_ref[...]) * (1.0 + nmod_ref[1:2, :]) + nmod_ref[0:1, :]).astype(BF16)
        yx_ref[...] = yx
        x_ref[...] = x

    tile = pl.BlockSpec((tm, d), lambda i: (i, 0))
    seg = pl.BlockSpec((None, 8, d), lambda i: (i // seg_tiles, 0, 0))
    in_specs = [pl.BlockSpec((tm, k), lambda i: (i, 0)), pl.BlockSpec((k, d), lambda i: (0, 0)), tile, seg]
    out_specs = [tile, tile]
    out_shape = [jax.ShapeDtypeStruct((t, d), F32), jax.ShapeDtypeStruct((t, d), F32)]
    args = [z, w, res, mod]
    if nxt is not None:
        in_specs += [pl.BlockSpec((1, d), lambda i: (0, 0)), seg]
        out_specs.append(tile)
        out_shape.append(jax.ShapeDtypeStruct((t, d), BF16))
        args += list(nxt)
    return _pc(
        body, name=name, grid=(t // tm,), in_specs=in_specs, out_specs=out_specs, out_shape=out_shape,
        compiler_params=_params("parallel"),
    )(*args)


def _proj_out_bwd(dyx, z, w, name):
    t, d = dyx.shape
    width = w.shape[0]
    tm = _tile(t, 768)
    nk = t // tm

    def body(a_ref, z_ref, w_ref, dz_ref, dw_ref, acc_ref):
        k = pl.program_id(0)

        @pl.when(k == 0)
        def _():
            acc_ref[...] = jnp.zeros_like(acc_ref)

        a = a_ref[...]
        dz_ref[...] = _dot(a, w_ref[...], 1, 1)
        acc_ref[...] += _dot(z_ref[...].T, a)

        @pl.when(k == nk - 1)
        def _():
            dw_ref[...] = acc_ref[...].astype(BF16)

    return _pc(
        body, name=name, grid=(nk,),
        in_specs=[pl.BlockSpec((tm, d), lambda k: (k, 0)), pl.BlockSpec((tm, width), lambda k: (k, 0)),
                  pl.BlockSpec((width, d), lambda k: (0, 0))],
        out_specs=[pl.BlockSpec((tm, width), lambda k: (k, 0)), pl.BlockSpec((width, d), lambda k: (0, 0))],
        out_shape=[jax.ShapeDtypeStruct((t, width), F32), jax.ShapeDtypeStruct((width, d), BF16)],
        scratch_shapes=[pltpu.VMEM((width, d), F32)],
        compiler_params=_params("arbitrary"),
    )(dyx, z, w)


def _proj_in_dh(dpre, w, layer, name, after=None):
    parts, t, width = dpre.shape
    d, n8 = w.shape[-2:]
    per_part = width // n8
    tm, tn = _tile(t, 768), _tile(d, 512)

    def body(a_ref, w_ref, *rest):
        o_ref = rest[-1]
        acc = None
        for p in range(parts):
            for s in range(per_part):
                term = _dot(a_ref[p, :, s * n8:(s + 1) * n8], w_ref[p * per_part + s], 1, 1)
                acc = term if acc is None else acc + term
        o_ref[...] = acc

    extra = [] if after is None else [after]
    return _pc(
        body, name=name, grid=(t // tm, d // tn),
        in_specs=[pl.BlockSpec((parts, tm, width), lambda i, j: (0, i, 0)),
                  pl.BlockSpec((N_DEV, None, tn, n8), lambda i, j: (0, layer, j, 0))] + [ANY] * len(extra),
        out_specs=pl.BlockSpec((tm, tn), lambda i, j: (i, j)),
        out_shape=jax.ShapeDtypeStruct((t, d), F32),
        compiler_params=_params("parallel", "parallel"),
    )(dpre, w, *extra)


def _transposed(a_ref):
    return a_ref[...].T


def _grad_w_in(h, dpre, n8, name):
    t, d = h.shape
    parts, _, width = dpre.shape
    per_part = width // n8
    tm, tk = _tile(d, 512), _tile(t, 1152)
    nk = t // tk

    def body(a_ref, b_ref, o_ref, acc_ref):
        k = pl.program_id(1)

        @pl.when(k == 0)
        def _():
            acc_ref[...] = jnp.zeros_like(acc_ref)

        at = _transposed(a_ref)
        for p in range(parts):
            r = _dot(at, b_ref[p])
            for s in range(per_part):
                acc_ref[p * per_part + s] += r[:, s * n8:(s + 1) * n8]

        @pl.when(k == nk - 1)
        def _():
            o_ref[...] = acc_ref[...].astype(BF16)

    return _pc(
        body, name=name, grid=(d // tm, nk),
        in_specs=[pl.BlockSpec((tk, tm), lambda i, k: (k, i)), pl.BlockSpec((parts, tk, width), lambda i, k: (0, k, 0))],
        out_specs=pl.BlockSpec((parts * per_part, tm, n8), lambda i, k: (0, i, 0)),
        out_shape=jax.ShapeDtypeStruct((parts * per_part, d, n8), BF16),
        scratch_shapes=[pltpu.VMEM((parts * per_part, tm, n8), F32)],
        compiler_params=_params("parallel", "arbitrary"),
    )(h, dpre)


def _shift(v, k):
    n = v.shape[0]
    return pltpu.roll(v, k % n, 0)


def _window_sum(v, win):
    s = v + _shift(v, 1)
    step = 1
    while 2 * step < win:
        s = _shift(s, step) + _shift(s, -step)
        step *= 2
    return s


def _window_count(base, seg_len, win, shape):
    t = base + lax.broadcasted_iota(jnp.int32, shape, 0)
    hi = jnp.minimum(t + win // 2, seg_len)
    lo = jnp.maximum(t - win // 2, 0)
    return (hi - lo).astype(F32)


def _pad_offsets(segs):
    return [HALO * (s + 1) + st for s, (st, _) in enumerate(segs)]


def _for_chunks(segs, fn):
    offs = _pad_offsets(segs)
    for s, (st, ln) in enumerate(segs):
        def step(ci, carry, s=s, st=st, ln=ln):
            fn(s, st, ln, offs[s], pl.multiple_of(ci * CHUNK, CHUNK))
            return carry
        lax.fori_loop(0, ln // CHUNK, step, 0)


def _pool_fwd(pre, w_grp, scale, segs, name):
    _, t, width = pre.shape
    grp = width // len(POOL_WINDOWS)
    padded = t + HALO * (len(segs) + 1)

    def group(win, pre_ref, w_ref, sc_ref, z_ref, diff_ref, pad_ref):
        pad_ref[...] = jnp.zeros_like(pad_ref)

        def fill(s, st, ln, off, b):
            pad_ref[pl.ds(off + b, CHUNK), :] = pre_ref[0, pl.ds(st + b, CHUNK), :]

        _for_chunks(segs, fill)

        def mix(s, st, ln, off, b):
            ext = pad_ref[pl.ds(off - HALO + b, CHUNK + 2 * HALO), :]
            total = _window_sum(ext, win)[HALO:HALO + CHUNK]
            u = pre_ref[0, pl.ds(st + b, CHUNK), :]
            diff = (total / _window_count(b, ln, win, u.shape) - u).astype(BF16)
            mixed = _dot(diff, w_ref[...])
            gate = _silu(pre_ref[1, pl.ds(st + b, CHUNK), :])
            z_ref[pl.ds(st + b, CHUNK), :] = (mixed * sc_ref[...] * gate).astype(BF16)
            diff_ref[pl.ds(st + b, CHUNK), :] = diff

        _for_chunks(segs, mix)

    def body(pre_ref, w_ref, sc_ref, z_ref, diff_ref, pad_ref):
        gi = pl.program_id(0)
        for widx, win in enumerate(POOL_WINDOWS):
            @pl.when(gi == widx)
            def _(win=win):
                group(win, pre_ref, w_ref, sc_ref, z_ref, diff_ref, pad_ref)

    col = pl.BlockSpec((t, grp), lambda g: (0, g))
    return _pc(
        body, name=name, grid=(len(POOL_WINDOWS),),
        in_specs=[pl.BlockSpec((2, t, grp), lambda g: (0, 0, g)), pl.BlockSpec((None, grp, grp), lambda g: (g, 0, 0)),
                  pl.BlockSpec((1, grp), lambda g: (0, g))],
        out_specs=[col, col],
        out_shape=[jax.ShapeDtypeStruct((t, width), BF16), jax.ShapeDtypeStruct((t, width), BF16)],
        scratch_shapes=[pltpu.VMEM((padded, grp), F32)],
        compiler_params=_params("parallel"),
    )(pre, w_grp, scale)


def _pool_bwd(dz, diff, pre, w_grp, scale, segs, name):
    _, t, width = pre.shape
    grp = width // len(POOL_WINDOWS)
    padded = t + HALO * (len(segs) + 1)

    def group(win, dz_ref, diff_ref, pre_ref, w_ref, sc_ref, dpre_ref, dw_ref, dsc_ref, pad_ref, dd_ref):
        pad_ref[...] = jnp.zeros_like(pad_ref)
        dw_ref[...] = jnp.zeros_like(dw_ref)
        dsc_ref[...] = jnp.zeros_like(dsc_ref)

        def first(s, st, ln, off, b):
            rows = pl.ds(st + b, CHUNK)
            diff_v = diff_ref[rows, :]
            mixed = _dot(diff_v, w_ref[...])
            g = pre_ref[1, rows, :]
            sg = _silu(g)
            dzv = dz_ref[rows, :]
            dmixed = (dzv * sc_ref[...] * sg).astype(BF16)
            dsc_ref[...] += jnp.sum(dzv * mixed * sg, axis=0, keepdims=True)
            dpre_ref[1, rows, :] = (dzv * mixed * sc_ref[...] * _dsilu(g)).astype(BF16)
            ddiff = _dot(dmixed, w_ref[...], 1, 1)
            dw_ref[...] += _dot(diff_v, dmixed, 0, 0)
            dd_ref[rows, :] = ddiff
            pad_ref[pl.ds(off + b, CHUNK), :] = ddiff / _window_count(b, ln, win, ddiff.shape)

        _for_chunks(segs, first)

        def second(s, st, ln, off, b):
            rows = pl.ds(st + b, CHUNK)
            ext = pad_ref[pl.ds(off - HALO + b, CHUNK + 2 * HALO), :]
            total = _shift(_window_sum(ext, win), -1)[HALO:HALO + CHUNK]
            dpre_ref[0, rows, :] = (total - dd_ref[rows, :]).astype(BF16)

        _for_chunks(segs, second)

    def body(dz_ref, diff_ref, pre_ref, w_ref, sc_ref, dpre_ref, dw_ref, dsc_ref, pad_ref, dd_ref):
        gi = pl.program_id(0)
        for widx, win in enumerate(POOL_WINDOWS):
            @pl.when(gi == widx)
            def _(win=win):
                group(win, dz_ref, diff_ref, pre_ref, w_ref, sc_ref, dpre_ref, dw_ref, dsc_ref, pad_ref, dd_ref)

    col = pl.BlockSpec((t, grp), lambda g: (0, g))
    both = pl.BlockSpec((2, t, grp), lambda g: (0, 0, g))
    wspec = pl.BlockSpec((None, grp, grp), lambda g: (g, 0, 0))
    sspec = pl.BlockSpec((1, grp), lambda g: (0, g))
    return _pc(
        body, name=name, grid=(len(POOL_WINDOWS),),
        in_specs=[col, col, both, wspec, sspec],
        out_specs=[both, wspec, sspec],
        out_shape=[jax.ShapeDtypeStruct((2, t, width), BF16), jax.ShapeDtypeStruct((len(POOL_WINDOWS), grp, grp), F32),
                   jax.ShapeDtypeStruct((1, width), F32)],
        scratch_shapes=[pltpu.VMEM((padded, grp), F32), pltpu.VMEM((t, grp), F32)],
        compiler_params=_params("parallel"),
    )(dz, diff, pre, w_grp, scale)


def _conv_fwd(pre, dw, db, name):
    _, t, width = pre.shape
    cb = LANES
    segs = [(0, t)]

    def body(pre_ref, dw_ref, db_ref, z_ref, pad_ref):
        pad_ref[...] = jnp.zeros_like(pad_ref)

        def fill(s, st, ln, off, b):
            rows = pl.ds(b, CHUNK)
            pad_ref[pl.ds(off + b, CHUNK), :] = pre_ref[1, rows, :] * pre_ref[2, rows, :]

        _for_chunks(segs, fill)

        def mix(s, st, ln, off, b):
            rows = pl.ds(b, CHUNK)
            ext = pad_ref[pl.ds(off - HALO + b, CHUNK + 2 * HALO), :]
            conv = (dw_ref[0:1, :] * _shift(ext, 1) + dw_ref[1:2, :] * ext + dw_ref[2:3, :] * _shift(ext, -1))
            conv = conv[HALO:HALO + CHUNK] + db_ref[...]
            y = pre_ref[0, rows, :] * conv
            z_ref[rows, :] = (y * _silu(pre_ref[3, rows, :])).astype(BF16)

        _for_chunks(segs, mix)

    return _pc(
        body, name=name, grid=(width // cb,),
        in_specs=[pl.BlockSpec((4, t, cb), lambda j: (0, 0, j)), pl.BlockSpec((8, cb), lambda j: (0, j)),
                  pl.BlockSpec((1, cb), lambda j: (0, j))],
        out_specs=pl.BlockSpec((t, cb), lambda j: (0, j)),
        out_shape=jax.ShapeDtypeStruct((t, width), BF16),
        scratch_shapes=[pltpu.VMEM((t + 2 * HALO, cb), F32)],
        compiler_params=_params("parallel"),
    )(pre, dw, db)


def _conv_bwd(dz, pre, dw, db, name):
    _, t, width = pre.shape
    cb = LANES
    segs = [(0, t)]

    def body(dz_ref, pre_ref, dw_ref, db_ref, dpre_ref, ddw_ref, ddb_ref, pad_a, pad_c):
        pad_a[...] = jnp.zeros_like(pad_a)
        pad_c[...] = jnp.zeros_like(pad_c)
        ddw_ref[...] = jnp.zeros_like(ddw_ref)
        ddb_ref[...] = jnp.zeros_like(ddb_ref)

        def fill(s, st, ln, off, b):
            rows = pl.ds(b, CHUNK)
            pad_a[pl.ds(off + b, CHUNK), :] = pre_ref[1, rows, :] * pre_ref[2, rows, :]

        _for_chunks(segs, fill)

        def first(s, st, ln, off, b):
            rows = pl.ds(b, CHUNK)
            ext = pad_a[pl.ds(off - HALO + b, CHUNK + 2 * HALO), :]
            prev, nxt = _shift(ext, 1)[HALO:HALO + CHUNK], _shift(ext, -1)[HALO:HALO + CHUNK]
            here = ext[HALO:HALO + CHUNK]
            conv = dw_ref[0:1, :] * prev + dw_ref[1:2, :] * here + dw_ref[2:3, :] * nxt + db_ref[...]
            bg, g = pre_ref[0, rows, :], pre_ref[3, rows, :]
            dzv = dz_ref[rows, :]
            dy = dzv * _silu(g)
            dpre_ref[3, rows, :] = (dzv * (bg * conv) * _dsilu(g)).astype(BF16)
            dpre_ref[0, rows, :] = (dy * conv).astype(BF16)
            dconv = dy * bg
            pad_c[pl.ds(off + b, CHUNK), :] = dconv
            ddw_ref[0:1, :] += jnp.sum(dconv * prev, axis=0, keepdims=True)
            ddw_ref[1:2, :] += jnp.sum(dconv * here, axis=0, keepdims=True)
            ddw_ref[2:3, :] += jnp.sum(dconv * nxt, axis=0, keepdims=True)
            ddb_ref[0:1, :] += jnp.sum(dconv, axis=0, keepdims=True)

        _for_chunks(segs, first)

        def second(s, st, ln, off, b):
            rows = pl.ds(b, CHUNK)
            ext = pad_c[pl.ds(off - HALO + b, CHUNK + 2 * HALO), :]
            da = (dw_ref[0:1, :] * _shift(ext, -1) + dw_ref[1:2, :] * ext + dw_ref[2:3, :] * _shift(ext, 1))
            da = da[HALO:HALO + CHUNK]
            dpre_ref[1, rows, :] = (da * pre_ref[2, rows, :]).astype(BF16)
            dpre_ref[2, rows, :] = (da * pre_ref[1, rows, :]).astype(BF16)

        _for_chunks(segs, second)

    quad = pl.BlockSpec((4, t, cb), lambda j: (0, 0, j))
    rows8 = pl.BlockSpec((8, cb), lambda j: (0, j))
    return _pc(
        body, name=name, grid=(width // cb,),
        in_specs=[pl.BlockSpec((t, cb), lambda j: (0, j)), quad, rows8, pl.BlockSpec((1, cb), lambda j: (0, j))],
        out_specs=[quad, rows8, rows8],
        out_shape=[jax.ShapeDtypeStruct((4, t, width), BF16), jax.ShapeDtypeStruct((8, width), F32),
                   jax.ShapeDtypeStruct((8, width), F32)],
        scratch_shapes=[pltpu.VMEM((t + 2 * HALO, cb), F32), pltpu.VMEM((t + 2 * HALO, cb), F32)],
        compiler_params=_params("parallel"),
    )(dz, pre, dw, db)


PAIR_TILES = 2 * WIN_ROWS - 2


def _pair_geometry():
    lane = lax.broadcasted_iota(jnp.int32, (GRID_W, LANES), 1)
    qcol = lax.broadcasted_iota(jnp.int32, (GRID_W, LANES), 0)
    low = lane < GRID_W
    kcol = jnp.where(low, lane, lane - GRID_W)
    start = jnp.clip(qcol - WIN_COLS // 2, 0, GRID_W - WIN_COLS)
    inside = (kcol >= start) & (kcol < start + WIN_COLS)
    return low, inside


def _bias_tiles(rpb_ref, rows_ref, tiles_ref, inside):
    for h in range(2):
        rows = rpb_ref[h]
        rows_ref[h] = (pltpu.roll(rows, LANES - (WIN_COLS - 1), 1)
                       + pltpu.roll(pltpu.roll(rows, GRID_W - (WIN_COLS - 1), 1), 2 * WIN_ROWS - 1, 0))
        for t in range(PAIR_TILES):
            both = jnp.broadcast_to(rows_ref[h, t:t + 1, :], (GRID_W, LANES))
            tiles_ref[h, t] = jnp.where(inside, pltpu.roll(both, 0, 1, stride=1, stride_axis=0), MASKED)


def _bias_tiles_grad(dtiles_ref, drpb_ref):
    n = PAIR_TILES * GRID_W
    qcol = lax.broadcasted_iota(jnp.int32, (n, LANES), 0) & (GRID_W - 1)
    lane = lax.broadcasted_iota(jnp.int32, (1, LANES), 1)
    zero = jnp.zeros((1, LANES), F32)
    for h in range(2):
        v = pltpu.roll(dtiles_ref[h].reshape(n, LANES), WIN_COLS - 1, 1)
        for bit in range(6):
            v = jnp.where((qcol >> bit) & 1 == 1, pltpu.roll(v, LANES - (1 << bit), 1), v)
        sums = [jnp.sum(v[t * GRID_W:(t + 1) * GRID_W], axis=0, keepdims=True) for t in range(PAIR_TILES)]
        for r in range(2 * WIN_ROWS):
            here = sums[r] if r < PAIR_TILES else zero
            prev = pltpu.roll(sums[r - 1], GRID_W, 1) if 1 <= r <= PAIR_TILES else zero
            drpb_ref[h, r:r + 1, :] = jnp.where(lane < 2 * WIN_COLS - 1, here + prev, 0.0)


def _attn_rows(r, n_rows):
    first = jnp.clip(r - WIN_ROWS // 2, 0, n_rows - WIN_ROWS)
    return first, first - r + WIN_ROWS - 1


def _softmax(s_loc, s_ctx):
    m = jnp.maximum(jnp.max(s_loc, axis=-1, keepdims=True), jnp.max(s_ctx, axis=-1, keepdims=True))
    e_loc, e_ctx = jnp.exp(s_loc - m), jnp.exp(s_ctx - m)
    inv = 1.0 / (jnp.sum(e_loc, axis=-1, keepdims=True) + jnp.sum(e_ctx, axis=-1, keepdims=True))
    return e_loc * inv, e_ctx * inv


def _pair_bias(tiles_ref, j):
    return jnp.concatenate(
        [jnp.concatenate([tiles_ref[h, j + 2 * m] for m in range(WIN_ROWS // 2)], axis=1) for h in range(2)], axis=0)


ROWS_PER_STEP = 8
ROWS_PER_STEP_BWD = 4


def _by_head(tile, low):
    zero = jnp.zeros_like(tile)
    return jnp.concatenate([jnp.where(low, tile, zero), jnp.where(low, zero, tile)], axis=0)


def _merge_heads(stacked, low):
    return jnp.where(low, stacked[:GRID_W], stacked[GRID_W:])


def _attn_items(step, n_rows, q_ref, low, per_step):
    items = []
    for u in range(per_step):
        r = step * per_step + u
        first, j = _attn_rows(r, n_rows)
        rows = pl.ds(pl.multiple_of(r * GRID_W, GRID_W), GRID_W)
        keys = pl.ds(pl.multiple_of(first * GRID_W, GRID_W), WIN_ROWS * GRID_W)
        q = (q_ref[rows, :].astype(F32) * HEAD_DIM ** -0.5).astype(BF16)
        items.append((rows, keys, j, _by_head(q, low)))
    return items


def _attn_fwd(qkv, gate, rpb, seq):
    _, t, width = qkv.shape
    n_rows = seq // GRID_W
    n_ctx = t - seq

    def body(q_ref, k_ref, v_ref, g_ref, rpb_ref, z_ref, o_ref, rows_ref, tiles_ref):
        low, inside = _pair_geometry()
        _bias_tiles(rpb_ref, rows_ref, tiles_ref, inside)
        ctx = pl.ds(seq, n_ctx)

        def step(i, carry):
            items = _attn_items(i, n_rows, q_ref, low, ROWS_PER_STEP)
            k_ctx, v_ctx = k_ref[ctx, :], v_ref[ctx, :]
            scores = [(_dot(q, k_ref[keys, :], 1, 1) + _pair_bias(tiles_ref, j), _dot(q, k_ctx, 1, 1))
                      for _, keys, j, q in items]
            probs = [_softmax(s_loc, s_ctx) for s_loc, s_ctx in scores]
            outs = [_dot(p_loc.astype(BF16), v_ref[keys, :]) + _dot(p_ctx.astype(BF16), v_ctx)
                    for (_, keys, _, _), (p_loc, p_ctx) in zip(items, probs)]
            for (rows, _, _, _), out in zip(items, outs):
                o = _merge_heads(out, low)
                o_ref[rows, :] = o
                z_ref[rows, :] = (o * _silu(g_ref[rows, :])).astype(BF16)
            return carry

        lax.fori_loop(0, n_rows // ROWS_PER_STEP, step, 0)

    def part(p):
        return pl.BlockSpec((None, t, LANES), lambda h: (p, 0, h))

    out = pl.BlockSpec((seq, LANES), lambda h: (0, h))
    return _pc(
        body, name="attn_fwd", grid=(width // LANES,),
        in_specs=[part(0), part(1), part(2), part(0), pl.BlockSpec((2, 2 * WIN_ROWS, LANES), lambda h: (h, 0, 0))],
        out_specs=[out, out],
        out_shape=[jax.ShapeDtypeStruct((seq, width), BF16), jax.ShapeDtypeStruct((seq, width), F32)],
        scratch_shapes=[pltpu.VMEM((2, 2 * WIN_ROWS, LANES), F32), pltpu.VMEM((2, PAIR_TILES, GRID_W, LANES), F32)],
        compiler_params=_params("parallel"),
    )(qkv, qkv, qkv, gate, rpb)


def _attn_bwd(qkv, gate, o, dz, rpb, seq):
    _, t, width = qkv.shape
    n_rows = seq // GRID_W
    n_ctx = t - seq
    heads = 2 * width // LANES

    def body(q_ref, k_ref, v_ref, g_ref, o_ref, dz_ref, rpb_ref, dpre_ref, drpb_ref,
             rows_ref, tiles_ref, dtiles_ref, dk_ref, dv_ref):
        low, inside = _pair_geometry()
        _bias_tiles(rpb_ref, rows_ref, tiles_ref, inside)
        dtiles_ref[...] = jnp.zeros_like(dtiles_ref)
        dk_ref[...] = jnp.zeros_like(dk_ref)
        dv_ref[...] = jnp.zeros_like(dv_ref)
        ctx = pl.ds(seq, n_ctx)

        def step(i, carry):
            items = _attn_items(i, n_rows, q_ref, low, ROWS_PER_STEP_BWD)
            k_ctx, v_ctx = k_ref[ctx, :], v_ref[ctx, :]
            d_outs = []
            for rows, _, _, _ in items:
                g = g_ref[rows, :]
                dzv = dz_ref[rows, :]
                dpre_ref[3, rows, :] = (dzv * o_ref[rows, :] * _dsilu(g)).astype(BF16)
                d_outs.append(_by_head((dzv * _silu(g)).astype(BF16), low))
            scores = [(_dot(q, k_ref[keys, :], 1, 1) + _pair_bias(tiles_ref, j), _dot(q, k_ctx, 1, 1))
                      for _, keys, j, q in items]
            dprobs = [(_dot(d_o, v_ref[keys, :], 1, 1), _dot(d_o, v_ctx, 1, 1))
                      for (_, keys, _, _), d_o in zip(items, d_outs)]
            probs = [_softmax(s_loc, s_ctx) for s_loc, s_ctx in scores]
            dscores = []
            for (p_loc, p_ctx), (dp_loc, dp_ctx) in zip(probs, dprobs):
                delta = (jnp.sum(p_loc * dp_loc, axis=-1, keepdims=True)
                         + jnp.sum(p_ctx * dp_ctx, axis=-1, keepdims=True))
                dscores.append((p_loc * (dp_loc - delta), p_ctx * (dp_ctx - delta)))
            dqs = [_dot(ds_loc.astype(BF16), k_ref[keys, :]) + _dot(ds_ctx.astype(BF16), k_ctx)
                   for (_, keys, _, _), (ds_loc, ds_ctx) in zip(items, dscores)]
            for (rows, _, _, _), dq in zip(items, dqs):
                dpre_ref[0, rows, :] = (_merge_heads(dq, low) * HEAD_DIM ** -0.5).astype(BF16)
            for (_, keys, j, q), d_o, (p_loc, p_ctx), (ds_loc, ds_ctx) in zip(items, d_outs, probs, dscores):
                dk_ref[keys, :] += _dot(ds_loc.astype(BF16), q, 0, 0)
                dk_ref[ctx, :] += _dot(ds_ctx.astype(BF16), q, 0, 0)
                dv_ref[keys, :] += _dot(p_loc.astype(BF16), d_o, 0, 0)
                dv_ref[ctx, :] += _dot(p_ctx.astype(BF16), d_o, 0, 0)
                for h in range(2):
                    for m in range(WIN_ROWS // 2):
                        dtiles_ref[h, j + 2 * m] += ds_loc[h * GRID_W:(h + 1) * GRID_W, m * LANES:(m + 1) * LANES]
            return carry

        lax.fori_loop(0, n_rows // ROWS_PER_STEP_BWD, step, 0)
        dpre_ref[1] = dk_ref[...].astype(BF16)
        dpre_ref[2] = dv_ref[...].astype(BF16)
        dpre_ref[0, ctx, :] = jnp.zeros((n_ctx, LANES), BF16)
        dpre_ref[3, ctx, :] = jnp.zeros((n_ctx, LANES), BF16)
        _bias_tiles_grad(dtiles_ref, drpb_ref)

    def part(p):
        return pl.BlockSpec((None, t, LANES), lambda h: (p, 0, h))

    lat = pl.BlockSpec((seq, LANES), lambda h: (0, h))
    rspec = pl.BlockSpec((2, 2 * WIN_ROWS, LANES), lambda h: (h, 0, 0))
    tiles = pltpu.VMEM((2, PAIR_TILES, GRID_W, LANES), F32)
    return _pc(
        body, name="attn_bwd", grid=(width // LANES,),
        in_specs=[part(0), part(1), part(2), part(0), lat, lat, rspec],
        out_specs=[pl.BlockSpec((4, t, LANES), lambda h: (0, 0, h)), rspec],
        out_shape=[jax.ShapeDtypeStruct((4, t, width), BF16), jax.ShapeDtypeStruct((heads, 2 * WIN_ROWS, LANES), F32)],
        scratch_shapes=[pltpu.VMEM((2, 2 * WIN_ROWS, LANES), F32), tiles, tiles,
                        pltpu.VMEM((t, LANES), F32), pltpu.VMEM((t, LANES), F32)],
        compiler_params=_params("parallel"),
    )(qkv, qkv, qkv, gate, o, dz, rpb)


def _adam_update(w, m, v, g):
    m2 = ADAM_B1 * m + (1.0 - ADAM_B1) * g
    v2 = ADAM_B2 * v + (1.0 - ADAM_B2) * (g * g)
    m_hat = m2 / (1.0 - ADAM_B1 ** ADAM_STEP)
    v_hat = v2 / (1.0 - ADAM_B2 ** ADAM_STEP)
    return -ADAM_LR * (m_hat / (jnp.sqrt(v_hat) + ADAM_EPS) + ADAM_WD * w), m2, v2


def _adamw(w, m, v, parts, name, after=None):
    rows, cols = w.shape
    tr = _tile(rows, max(8, ADAM_TILE_ELEMS // cols), 8)
    n_parts = len(parts)

    def body(*refs):
        w_ref, m_ref, v_ref = refs[:3]
        part_refs = refs[3:3 + n_parts]
        g_ref, d_ref, nm_ref, nv_ref = refs[-4:]
        g = part_refs[0][...].astype(F32)
        for p in part_refs[1:]:
            g = g + p[...].astype(F32)
        g_ref[...] = g
        d_ref[...], nm_ref[...], nv_ref[...] = _adam_update(w_ref[...], m_ref[...], v_ref[...], g)

    tile = pl.BlockSpec((tr, cols), lambda i: (i, 0))
    in_specs, args = [tile, tile, tile], [w, m, v]
    for p in parts:
        if isinstance(p, tuple):
            arr, k = p
            in_specs.append(pl.BlockSpec((None, tr, cols), lambda i, k=k: (k, i, 0)))
            args.append(arr)
        else:
            in_specs.append(tile)
            args.append(p)
    if after is not None:
        in_specs.append(ANY)
        args.append(after)
    shape = jax.ShapeDtypeStruct((rows, cols), F32)
    return _pc(
        body, name=name, grid=(rows // tr,), in_specs=in_specs, out_specs=[tile] * 4, out_shape=[shape] * 4,
        compiler_params=_params("parallel"),
    )(*args)


def _adamw_layers(w, m, v, landed, name):
    n_layers, rows, cols = w.shape
    tr = _tile(rows, max(8, ADAM_TILE_ELEMS // cols), 8)

    def body(*refs):
        w_ref, m_ref, v_ref = refs[:3]
        part_refs = refs[3:3 + n_layers * N_DEV]
        g_ref, d_ref, nm_ref, nv_ref = refs[3 + n_layers * N_DEV:]
        layer = pl.program_id(0)
        g = None
        for l in range(n_layers):
            s = part_refs[l * N_DEV][...].astype(F32)
            for p in part_refs[l * N_DEV + 1:(l + 1) * N_DEV]:
                s = s + p[...].astype(F32)
            g = s if g is None else jnp.where(layer == l, s, g)
        g_ref[...] = g
        d_ref[...], nm_ref[...], nv_ref[...] = _adam_update(w_ref[...], m_ref[...], v_ref[...], g)

    tile = pl.BlockSpec((None, tr, cols), lambda l, i: (l, i, 0))
    in_specs, args = [tile, tile, tile], [w, m, v]
    for l, arr in enumerate(landed):
        for k in range(N_DEV):
            in_specs.append(pl.BlockSpec((None, tr, cols), lambda ll, i, l=l, k=k: (k, jnp.where(ll == l, i, 0), 0)))
            args.append(arr)
    shape = jax.ShapeDtypeStruct(w.shape, F32)
    return _pc(
        body, name=name, grid=(n_layers, rows // tr), in_specs=in_specs, out_specs=[tile] * 4, out_shape=[shape] * 4,
        compiler_params=_params("arbitrary", "arbitrary"),
    )(*args)


def _adamw_small(states, grads):
    sources, makers = grads
    n, ns = len(states), len(sources)

    def body(*refs):
        src = refs[:ns]
        ins = refs[ns:ns + 3 * n]
        outs = refs[ns + 3 * n:]
        for k in range(n):
            w_ref, m_ref, v_ref = ins[3 * k:3 * k + 3]
            g = makers[k](*src)
            outs[4 * k][...] = g
            outs[4 * k + 1][...], outs[4 * k + 2][...], outs[4 * k + 3][...] = _adam_update(
                w_ref[...], m_ref[...], v_ref[...], g)

    flat = [a for s in states for a in s]
    vmem = pl.BlockSpec(memory_space=pltpu.VMEM)
    res = _pc(
        body, name="adamw_small",
        in_specs=[vmem] * (ns + 3 * n), out_specs=[vmem] * (4 * n),
        out_shape=[jax.ShapeDtypeStruct(s[0].shape, F32) for s in states for _ in range(4)],
        compiler_params=pltpu.CompilerParams(vmem_limit_bytes=VMEM_LIMIT),
    )(*sources, *flat)
    return [res[4 * k:4 * k + 4] for k in range(n)]


def _rows128(a):
    flat = a.reshape(-1)
    pad = (-flat.shape[0]) % LANES
    if pad:
        flat = jnp.concatenate([flat, jnp.zeros((pad,), flat.dtype)])
    return flat.reshape(-1, LANES)


def _pad_rows(a, mult=8):
    pad = (-a.shape[0]) % mult
    if pad:
        a = jnp.concatenate([a, jnp.zeros((pad,) + a.shape[1:], a.dtype)], axis=0)
    return a


def kernel(x, c, ctx, c_ctx, norm_g, ada_w, ada_b, pool_w_in, pool_w_grp, pool_scale, pool_w_out, na_w_in, na_rpb, na_w_out, conv_w_in, conv_dw, conv_db, conv_w_out, final_g, loss_target, m_c_ctx, m_norm_g, m_ada_w, m_ada_b, m_pool_w_in, m_pool_w_grp, m_pool_scale, m_pool_w_out, m_na_w_in, m_na_rpb, m_na_w_out, m_conv_w_in, m_conv_dw, m_conv_db, m_conv_w_out, m_final_g, v_c_ctx, v_norm_g, v_ada_w, v_ada_b, v_pool_w_in, v_pool_w_grp, v_pool_scale, v_pool_w_out, v_na_w_in, v_na_rpb, v_na_w_out, v_conv_w_in, v_conv_dw, v_conv_db, v_conv_w_out, v_final_g):
    xi, yi, ci = _my_place()
    me = 4 * xi + 2 * yi + ci
    seq, d = x.shape[1], x.shape[2]
    n_ctx = ctx.shape[1]
    width = d
    depth = norm_g.shape[0]
    nb = ada_w.shape[2]
    shard = width // N_DEV
    d_rows = d // LANES
    assert seq % CHUNK == 0 and n_ctx % CHUNK == 0 and (seq // GRID_W) % ROWS_PER_STEP == 0 and seq >= WIN_ROWS * GRID_W
    tr = math.gcd(math.gcd(seq, n_ctx), 256)
    x_tiles = seq // tr
    tr_lat = math.gcd(seq, 512)
    lat_tiles = seq // tr_lat

    n_pool = pool_scale.shape[0]
    n_grp = pool_w_grp.shape[1]
    grp = width // n_grp

    small_in = _pad_rows(jnp.concatenate([_rows128(c), pool_scale, conv_dw[0], conv_db], axis=0))
    got = _gather_small(small_in, "gather_inputs")
    r0 = d_rows
    c_all = got[:, :r0].reshape(N_DEV, d)
    scale_full = got[:, r0:r0 + n_pool].transpose(1, 0, 2).reshape(n_pool, width)
    r1 = r0 + n_pool
    taps_full = _pad_rows(got[:, r1:r1 + 3].transpose(1, 0, 2).reshape(3, width))
    bias_full = got[:, r1 + 3:r1 + 4].transpose(1, 0, 2).reshape(1, width)

    cond = jnp.concatenate([c_all, c_ctx[None], jnp.zeros((7, d), F32)], axis=0)
    bias_mine = lax.dynamic_slice(ada_b, (0, me * nb), (depth, nb))
    mod_mine = _mod_fwd(cond, ada_w, bias_mine)
    by_example = jnp.stack([mod_mine[:, :N_DEV].transpose(1, 0, 2),
                            jnp.broadcast_to(mod_mine[:, N_DEV][None], (N_DEV, depth, nb))], axis=2)
    mod_all = _gather_small(by_example.reshape(N_DEV, -1, LANES), "gather_mod", per_dest=True)
    mod_all = mod_all.reshape(N_DEV, depth, 2, nb).transpose(1, 2, 0, 3).reshape(depth, 2, 3, d)
    mod_all = jnp.pad(mod_all, ((0, 0), (0, 0), (0, 5), (0, 0)))
    mods = [mod_all[i] if i < 2 else mod_all[i, :1] for i in range(depth)]

    layer_weights = [[pool_w_in[0], pool_w_grp[0], pool_w_out[0]], [na_w_in[0], na_w_out[0]],
                     [conv_w_in[0], conv_w_out[0]], [pool_w_in[1], pool_w_grp[1], pool_w_out[1]]]
    slot = {(i, t): n for n, (i, t) in enumerate((i, t) for i, ws in enumerate(layer_weights) for t in range(len(ws)))}
    two_level = [(0, 0), (1, 0)]
    weights_sent = _exchange_start(
        [w.astype(BF16) for ws in layer_weights for w in ws], False, mod_all, "weights_start",
        peers=[CHIP_PEERS if key in two_level else ALL_PEERS for key in slot])
    token = weights_sent[-1]

    def landed_weight(i, t, after):
        return _exchange_wait(weights_sent, False, after, f"weights_wait{i}_{t}", which=[slot[i, t]])[0]

    def handed_on(i, after):
        half = [landed_weight(i, 0, after)]
        rest = _forward_start(half, after, f"weights_forward{i}")
        return _forward_wait(rest, rest[-1], f"weights_forward_wait{i}")[0][:, None]

    def as_in(w):
        return w[:, None]

    def as_grp(w):
        return w.transpose(1, 0, 2, 3).reshape(n_grp, grp, grp)

    def as_out(w):
        return w.reshape(width, d)

    both = [(0, seq), (seq, n_ctx)]
    latent = [(0, seq)]

    def grp_slots(g):
        return g.reshape(n_grp, N_DEV, grp // N_DEV, grp).transpose(1, 0, 2, 3).reshape(N_DEV, -1, grp).astype(BF16)

    def send_grads(i, grads):
        return _exchange_start(grads, True, jnp.zeros((8, LANES), F32), f"grads_start{i}")

    xs0 = jnp.concatenate([x[0], ctx[0]], axis=0)
    h0 = _norm_fwd(xs0, norm_g[0:1] + token[0, 0], mods[0], tr, x_tiles, "norm_fwd0")
    pool_in_w0 = handed_on(0, h0)
    pre0 = _proj_in(h0, pool_in_w0, 0, width, "proj_in0")
    pool_grp_w0 = as_grp(landed_weight(0, 1, pre0))
    z0, diff0 = _pool_fwd(pre0, pool_grp_w0, scale_full[0:1], both, "pool_fwd0")
    pool_out_w0 = as_out(landed_weight(0, 2, z0))
    yx0, xs1, h1 = _proj_out(z0, pool_out_w0, xs0, mods[0], tr, x_tiles, "proj_out0", nxt=(norm_g[1:2], mods[1]))

    na_in_w = handed_on(1, h1)
    qkv1, gpre1 = _proj_in_split(h1, na_in_w, 0, width, "proj_in1", 3)
    rpb_rows = jnp.pad(na_rpb[0], ((0, 0), (0, 2 * WIN_ROWS - na_rpb.shape[2]), (0, LANES - na_rpb.shape[3])))
    z1, o1 = _attn_fwd(qkv1, gpre1, rpb_rows, seq)
    na_out_w = as_out(landed_weight(1, 1, z1))
    yx1, x2, h2 = _proj_out(z1, na_out_w, xs1, mods[1], tr_lat, lat_tiles, "proj_out1", nxt=(norm_g[2:3], mods[2]))

    conv_in_w = as_in(landed_weight(2, 0, h2))
    pre2 = _proj_in(h2, conv_in_w, 0, width, "proj_in2")
    z2 = _conv_fwd(pre2, taps_full, bias_full, "conv_fwd")
    conv_out_w = as_out(landed_weight(2, 1, z2))
    yx2, x3, h3 = _proj_out(z2, conv_out_w, x2, mods[2], tr_lat, lat_tiles, "proj_out2", nxt=(norm_g[3:4], mods[3]))

    pool_in_w3 = as_in(landed_weight(3, 0, h3))
    pre3 = _proj_in(h3, pool_in_w3, 0, width, "proj_in3")
    pool_grp_w3 = as_grp(landed_weight(3, 1, pre3))
    z3, diff3 = _pool_fwd(pre3, pool_grp_w3, scale_full[1:2], latent, "pool_fwd3")
    pool_out_w3 = as_out(landed_weight(3, 2, z3))
    yx3, x4 = _proj_out(z3, pool_out_w3, x3, mods[3], tr_lat, lat_tiles, "proj_out3")

    loss_part, dx4, d_final, dyx3, gate3 = _loss_head(x4, loss_target[0], final_g[None], yx3, mods[3], tr_lat)

    dz3, g_pool_out1 = _proj_out_bwd(dyx3, z3, pool_out_w3, "proj_out_bwd3")
    dpre3, g_grp1, g_scale1 = _pool_bwd(dz3, diff3, pre3, pool_grp_w3, scale_full[1:2], latent, "pool_bwd3")
    dh3 = _proj_in_dh(dpre3, pool_in_w3, 0, "proj_in_dh3")
    g_pool_in1 = _grad_w_in(h3, dpre3, pool_w_in.shape[2], "grad_w_in3")
    sent3 = send_grads(3, [g_pool_in1, grp_slots(g_grp1), g_pool_out1.reshape(N_DEV, shard, d)])
    dx3, norm3, dyx2, gate2 = _norm_bwd(x3, dh3, dx4, norm_g[3:4] + sent3[-1][0, 0], mods[3], tr_lat, lat_tiles, "norm_bwd3",
                                        below=(yx2, mods[2]))

    dz2, g_conv_out = _proj_out_bwd(dyx2, z2, conv_out_w, "proj_out_bwd2")
    dpre2, g_taps, g_cbias = _conv_bwd(dz2, pre2, taps_full, bias_full, "conv_bwd")
    dh2 = _proj_in_dh(dpre2, conv_in_w, 0, "proj_in_dh2")
    g_conv_in = _grad_w_in(h2, dpre2, conv_w_in.shape[2], "grad_w_in2")
    sent2 = send_grads(2, [g_conv_in, g_conv_out.reshape(N_DEV, shard, d)])
    dx2, norm2, dyx1, gate1 = _norm_bwd(x2, dh2, dx3, norm_g[2:3] + sent2[-1][0, 0], mods[2], tr_lat, lat_tiles, "norm_bwd2",
                                        below=(yx1, mods[1][:1]))

    dz1, g_na_out = _proj_out_bwd(dyx1, z1, na_out_w, "proj_out_bwd1")
    dpre1, g_rpb = _attn_bwd(qkv1, gpre1, o1, dz1, rpb_rows, seq)
    g_rpb = g_rpb[:, :na_rpb.shape[2], :na_rpb.shape[3]]
    dh1 = _proj_in_dh(dpre1, na_in_w, 0, "proj_in_dh1")
    g_na_in = _grad_w_in(h1, dpre1, na_w_in.shape[2], "grad_w_in1")
    sent1 = send_grads(1, [g_na_in, g_na_out.reshape(N_DEV, shard, d)])
    dxs1, norm1, dyx0, gate0 = _norm_bwd(xs1, dh1, dx2, norm_g[1:2] + sent1[-1][0, 0], mods[1], tr, x_tiles, "norm_bwd1",
                                         res_tiles=x_tiles, below=(yx0, mods[0]))

    dz0, g_pool_out0 = _proj_out_bwd(dyx0, z0, pool_out_w0, "proj_out_bwd0")
    sent0a = _exchange_start([g_pool_out0.reshape(N_DEV, shard, d)], True, jnp.zeros((8, LANES), F32), "grads_start0a")
    dpre0, g_grp0, g_scale0 = _pool_bwd(dz0, diff0, pre0, pool_grp_w0, scale_full[0:1], both, "pool_bwd0")
    g_pool_in0 = _grad_w_in(h0, dpre0, pool_w_in.shape[2], "grad_w_in0")
    dh0 = _proj_in_dh(dpre0, pool_in_w0, 0, "proj_in_dh0", after=sent0a[-1])
    dx0, norm0 = _norm_bwd(xs0, dh0, dxs1, norm_g[0:1], mods[0], tr, x_tiles, "norm_bwd0", out_tiles=x_tiles)
    grad_x = dx0[None]

    norms, gates = [norm0, norm1, norm2, norm3], [gate0, gate1, gate2, gate3]
    zero_d = jnp.zeros((d,), F32)
    dm_rows = [jnp.concatenate([norms[i][0, 0], norms[i][0, 1], gates[i][0, 0]]) for i in range(depth)]
    dm_rows.append(jnp.concatenate([norm0[1, 0], norm0[1, 1], gate0[1, 0]]))
    dm_rows.append(jnp.concatenate([norm1[1, 0], norm1[1, 1], zero_d]))
    dm_local = jnp.stack(dm_rows + [jnp.zeros((3 * d,), F32)] * 2)
    g_norm_part = jnp.stack([norm0[0, 2] + norm0[1, 2], norm1[0, 2] + norm1[1, 2], norm2[0, 2], norm3[0, 2]])
    pieces = [_rows128(dm_local), _rows128(g_norm_part), _rows128(d_final[0]), _pad_rows(_rows128(g_rpb)), loss_part]
    marks = np.cumsum([0] + [p.shape[0] for p in pieces])
    by_owner = [a.reshape(-1, N_DEV, shard).transpose(1, 0, 2) for a in (g_scale0, g_scale1, g_taps[:3], g_cbias[0:1])]
    by_owner = jnp.concatenate(by_owner + [jnp.zeros((N_DEV, 8 - n_pool - 4, shard), F32)], axis=1)
    small_sent = _exchange_start([jnp.concatenate(pieces, axis=0), by_owner], [False, True], jnp.zeros((8, LANES), F32),
                                 "small_grads_start")
    sent0b = _exchange_start([g_pool_in0, grp_slots(g_grp0)], True, small_sent[-1], "grads_start0b")

    def big(parts, w, m, v, name):
        shape = w.shape
        view = (-1, shape[-1])
        parts = [(parts.reshape((N_DEV,) + w.reshape(view).shape), k) for k in range(N_DEV)]
        return [r.reshape(shape) for r in _adamw(w.reshape(view), m.reshape(view), v.reshape(view), parts, name)]

    in3, grp3, out3 = _exchange_wait(sent3, True, sent0b[-1], "grads_wait3")
    in2, out2 = _exchange_wait(sent2, True, sent0b[-1], "grads_wait2")
    in1, out1 = _exchange_wait(sent1, True, sent0b[-1], "grads_wait1")
    res = {}
    res["na_w_in"] = [r[None] for r in big(in1, na_w_in[0], m_na_w_in[0], v_na_w_in[0], "adamw_na_in")]
    res["na_w_out"] = [r[None] for r in big(out1, na_w_out[0], m_na_w_out[0], v_na_w_out[0], "adamw_na_out")]
    res["conv_w_in"] = [r[None] for r in big(in2, conv_w_in[0], m_conv_w_in[0], v_conv_w_in[0], "adamw_conv_in")]
    res["conv_w_out"] = [r[None] for r in big(out2, conv_w_out[0], m_conv_w_out[0], v_conv_w_out[0], "adamw_conv_out")]

    done = [res[n][0] for n in ("na_w_in", "na_w_out", "conv_w_in", "conv_w_out")]
    small_out, owned = _exchange_wait(small_sent, [False, True], done, "small_grads_wait")
    loss = jnp.sum(small_out[:, marks[4], 0])
    dm_all = small_out[:, :marks[1]].reshape(N_DEV, 8, 3 * d).transpose(1, 0, 2)
    dm_mine = lax.dynamic_slice(dm_all, (0, 0, me * nb), (8, N_DEV, nb))
    g_ada_w, g_ada_b, cctx_part, dsilu_cond = _mod_bwd(cond, ada_w, dm_all, dm_mine)
    cctx_sent = _exchange_start([_rows128(cctx_part[0])], False, jnp.zeros((8, LANES), F32), "cctx_start")
    res["ada_w"] = [r.reshape(ada_w.shape) for r in _adamw(
        ada_w.reshape(-1, nb), m_ada_w.reshape(-1, nb), v_ada_w.reshape(-1, nb),
        [g_ada_w.reshape(-1, nb)], "adamw_ada_w", after=cctx_sent[-1])]
    cctx_all, = _exchange_wait(cctx_sent, False, res["ada_w"][0], "cctx_wait")

    def summed(ref, lo, hi):
        g = ref[0, lo:hi, :]
        for k in range(1, N_DEV):
            g = g + ref[k, lo:hi, :]
        return g

    makers = [
        lambda so, ow, cc, ab, ds: summed(cc, 0, d_rows) * ds[...],
        lambda so, ow, cc, ab, ds: summed(so, marks[1], marks[2]),
        lambda so, ow, cc, ab, ds: ab[...],
        lambda so, ow, cc, ab, ds: summed(so, marks[2], marks[3]),
        lambda so, ow, cc, ab, ds: summed(so, marks[3], marks[4]),
        lambda so, ow, cc, ab, ds: summed(ow, 0, n_pool),
        lambda so, ow, cc, ab, ds: summed(ow, n_pool, n_pool + 3),
        lambda so, ow, cc, ab, ds: summed(ow, n_pool + 3, n_pool + 4),
    ]
    rpb_rows128 = lambda a: _pad_rows(_rows128(a))
    views = [_rows128] * 4 + [rpb_rows128] + [lambda a: a.reshape(-1, LANES)] * 3
    small = [(c_ctx, m_c_ctx, v_c_ctx), (norm_g, m_norm_g, v_norm_g), (ada_b, m_ada_b, v_ada_b),
             (final_g, m_final_g, v_final_g), (na_rpb, m_na_rpb, v_na_rpb), (pool_scale, m_pool_scale, v_pool_scale),
             (conv_dw, m_conv_dw, v_conv_dw), (conv_db, m_conv_db, v_conv_db)]
    states = [tuple(view(a) for a in triple) for view, triple in zip(views, small)]
    sources = (small_out, owned, cctx_all, _rows128(g_ada_b), _rows128(dsilu_cond[8]))
    small_res = _adamw_small(states, (sources, makers))
    names = ["c_ctx", "norm_g", "ada_b", "final_g", "na_rpb", "pool_scale", "conv_dw", "conv_db"]
    for name, (w, _, _), outs4 in zip(names, small, small_res):
        res[name] = [r.reshape(-1)[:w.size].reshape(w.shape) for r in outs4]

    out0, = _exchange_wait(sent0a, True, small_res[0][0], "grads_wait0a")
    in0, grp0 = _exchange_wait(sent0b, True, small_res[0][0], "grads_wait0b")
    def both_layers(first, second, w, m, v, name):
        view = (w.shape[0], -1, w.shape[-1])
        landed = [first.reshape((N_DEV,) + w.reshape(view).shape[1:]), second.reshape((N_DEV,) + w.reshape(view).shape[1:])]
        return [r.reshape(w.shape) for r in _adamw_layers(w.reshape(view), m.reshape(view), v.reshape(view), landed, name)]

    res["pool_w_in"] = both_layers(in0, in3, pool_w_in, m_pool_w_in, v_pool_w_in, "adamw_pool_in")
    res["pool_w_grp"] = both_layers(grp0, grp3, pool_w_grp, m_pool_w_grp, v_pool_w_grp, "adamw_pool_grp")
    res["pool_w_out"] = both_layers(out0, out3, pool_w_out, m_pool_w_out, v_pool_w_out, "adamw_pool_out")

    order = ["c_ctx", "norm_g", "ada_w", "ada_b", "pool_w_in", "pool_w_grp", "pool_scale", "pool_w_out", "na_w_in",
             "na_rpb", "na_w_out", "conv_w_in", "conv_dw", "conv_db", "conv_w_out", "final_g"]
    outs = [loss, grad_x]
    for j in range(4):
        outs += [res[n][j] for n in order]
    return tuple(outs)
```

```python
import math

import numpy as np
import jax
import jax.numpy as jnp
from jax import lax
from jax.experimental import pallas as pl
from jax.experimental.pallas import tpu as pltpu

F32 = jnp.float32
BF16 = jnp.bfloat16
N_DEV = 8
LANES = 128
RMS_EPS = 1e-6
GRID_W = 64
WIN_ROWS = 8
WIN_COLS = 16
HEAD_DIM = 64
POOL_WINDOWS = (2, 4, 8, 16)
HALO = 8
CHUNK = 256
MASKED = -1e30
ADAM_LR = 0.001
ADAM_B1 = 0.9
ADAM_B2 = 0.999
ADAM_EPS = 1e-08
ADAM_WD = 0.01
ADAM_STEP = 10
VMEM_LIMIT = 56 * 1024 * 1024
ADAM_TILE_ELEMS = 256 * 1024
MESH = pl.DeviceIdType.MESH
ANY = pl.BlockSpec(memory_space=pl.ANY)
HBM = pl.BlockSpec(memory_space=pltpu.HBM)
SEM = pl.BlockSpec(memory_space=pltpu.SEMAPHORE)
EFFECT = pltpu.SideEffectType.DATAFLOW_SIDE_EFFECTING


def _pc(body, *, name, **kw):
    return pl.pallas_call(body, name=name, **kw)


def _params(*sem):
    return pltpu.CompilerParams(dimension_semantics=sem if sem else None, vmem_limit_bytes=VMEM_LIMIT)


def _dot(a, b, ca=1, cb=0, precision=None):
    return lax.dot_general(a, b, (((ca,), (cb,)), ((), ())), preferred_element_type=F32, precision=precision)


def _tile(n, pref, unit=LANES):
    best = None
    for t in range(unit, min(n, pref) + 1, unit):
        if n % t == 0:
            best = t
    return best if best is not None else n


def _sigmoid(x):
    return 1.0 / (1.0 + jnp.exp(-x))


def _silu(x):
    return x * _sigmoid(x)


def _dsilu(x):
    s = _sigmoid(x)
    return s * (1.0 + x * (1.0 - s))


def _my_place():
    return lax.axis_index("x"), lax.axis_index("y"), lax.axis_index("c")


def _flip(v, f):
    return 1 - v if f else v


def _gather_small(block, name, per_dest=False):
    rows, cols = block.shape[-2:]

    def body(x_ref, out_ref, send_sems, recv_sems):
        x, y, c = _my_place()
        me = 4 * x + 2 * y + c
        out_ref[me] = x_ref[me] if per_dest else x_ref[...]
        copies = []
        for k in range(1, N_DEV):
            peer = (_flip(x, k & 4), _flip(y, k & 2), _flip(c, k & 1))
            dest = 4 * peer[0] + 2 * peer[1] + peer[2]
            cp = pltpu.make_async_remote_copy(
                src_ref=x_ref.at[dest] if per_dest else x_ref, dst_ref=out_ref.at[me],
                send_sem=send_sems.at[k - 1], recv_sem=recv_sems.at[k - 1], device_id=peer, device_id_type=MESH)
            cp.start()
            copies.append(cp)
        for cp in copies:
            cp.wait()

    return _pc(
        body, name=name,
        out_shape=jax.ShapeDtypeStruct((N_DEV, rows, cols), block.dtype),
        in_specs=[pl.BlockSpec(memory_space=pltpu.VMEM)],
        out_specs=pl.BlockSpec(memory_space=pltpu.VMEM),
        scratch_shapes=[pltpu.SemaphoreType.DMA((N_DEV - 1,)), pltpu.SemaphoreType.DMA((N_DEV - 1,))],
    )(block)


ALL_PEERS = tuple(range(N_DEV))
CHIP_PEERS = (0, 1, 2, 4, 6)
OTHER_CHIPS = (2, 4, 6)


def _flag(per_dest, t):
    return per_dest[t] if isinstance(per_dest, (list, tuple)) else per_dest


def _peer(k):
    x, y, c = _my_place()
    peer = (_flip(x, k & 4), _flip(y, k & 2), _flip(c, k & 1))
    return peer, 4 * peer[0] + 2 * peer[1] + peer[2]


def _peer_lists(peers, nt):
    return list(peers) if isinstance(peers, list) else [peers] * nt


def _exchange_copies(srcs, lands, send_sems, recv_sems, per_dest, peers=ALL_PEERS):
    x, y, c = _my_place()
    me = 4 * x + 2 * y + c
    copies = []
    for t, (src, land, ks) in enumerate(zip(srcs, lands, _peer_lists(peers, len(srcs)))):
        for k in ks:
            peer, dest = _peer(k)
            s = len(copies)
            copies.append(pltpu.make_async_remote_copy(
                src_ref=src.at[dest] if _flag(per_dest, t) else src, dst_ref=land.at[me],
                send_sem=send_sems[s], recv_sem=recv_sems[s], device_id=peer, device_id_type=MESH))
    return copies


def _forward_copies(lands, send_sems, recv_sems):
    sibling, _ = _peer(1)
    copies = []
    for t, land in enumerate(lands):
        for n, k in enumerate(OTHER_CHIPS):
            _, slot = _peer(k)
            s = t * len(OTHER_CHIPS) + n
            copies.append(pltpu.make_async_remote_copy(
                src_ref=land.at[slot], dst_ref=land.at[slot], send_sem=send_sems[s], recv_sem=recv_sems[s],
                device_id=sibling, device_id_type=MESH))
    return copies


def _forward_start(lands, after, name):
    nt = len(lands)
    ns = nt * len(OTHER_CHIPS)

    def body(*refs):
        ins, outs = refs[:nt + 1], refs[nt + 1:]
        for cp in _forward_copies(ins[:nt], outs[:ns], outs[ns:2 * ns]):
            cp.start()
        outs[-1][...] = jnp.zeros_like(outs[-1])

    res = _pc(
        body, name=name,
        out_shape=(*[pltpu.SemaphoreType.DMA(())] * (2 * ns), *[pltpu.HBM(a.shape, a.dtype) for a in lands],
                   jax.ShapeDtypeStruct((8, LANES), F32)),
        in_specs=[HBM] * nt + [ANY],
        out_specs=(*[SEM] * (2 * ns), *[HBM] * nt, pl.BlockSpec(memory_space=pltpu.VMEM)),
        input_output_aliases={i: 2 * ns + i for i in range(nt)},
        compiler_params=pltpu.CompilerParams(has_side_effects=EFFECT),
    )(*lands, after)
    return list(res[:ns]), list(res[ns:2 * ns]), list(res[2 * ns:2 * ns + nt]), res[-1]


def _forward_wait(state, after, name):
    send_sems, recv_sems, lands, _ = state
    nt, ns = len(lands), len(send_sems)

    def body(*refs):
        sems = refs[nt:nt + 2 * ns]
        for cp in _forward_copies(refs[:nt], sems[:ns], sems[ns:]):
            cp.wait_send()
            cp.wait_recv()

    res = _pc(
        body, name=name,
        out_shape=tuple(pltpu.HBM(a.shape, a.dtype) for a in lands),
        in_specs=[HBM] * nt + [SEM] * (2 * ns) + [ANY],
        out_specs=tuple([HBM] * nt),
        input_output_aliases={i: i for i in range(nt)},
        compiler_params=pltpu.CompilerParams(has_side_effects=EFFECT),
    )(*lands, *send_sems, *recv_sems, after)
    return list(res)


def _exchange_start(srcs, per_dest, after, name, peers=ALL_PEERS):
    nt = len(srcs)
    peers = _peer_lists(peers, nt)
    ns = sum(len(ks) for ks in peers)
    lands = [lax.empty((N_DEV,) + (s.shape[1:] if _flag(per_dest, t) else s.shape), s.dtype) for t, s in enumerate(srcs)]

    def body(*refs):
        ins, outs = refs[:2 * nt + 1], refs[2 * nt + 1:]
        for cp in _exchange_copies(ins[:nt], ins[nt:2 * nt], outs[:ns], outs[ns:2 * ns], per_dest, peers):
            cp.start()
        outs[-1][...] = jnp.zeros_like(outs[-1])

    hbm = [pltpu.with_memory_space_constraint(a, pltpu.HBM) for a in list(srcs) + lands]
    res = _pc(
        body, name=name,
        out_shape=(*[pltpu.SemaphoreType.DMA(())] * (2 * ns), *[pltpu.HBM(a.shape, a.dtype) for a in hbm],
                   jax.ShapeDtypeStruct((8, LANES), F32)),
        in_specs=[HBM] * (2 * nt) + [ANY],
        out_specs=(*[SEM] * (2 * ns), *[HBM] * (2 * nt), pl.BlockSpec(memory_space=pltpu.VMEM)),
        input_output_aliases={i: 2 * ns + i for i in range(2 * nt)},
        compiler_params=pltpu.CompilerParams(has_side_effects=EFFECT),
    )(*hbm, after)
    sems, rest = res[:2 * ns], res[2 * ns:]
    return list(sems[:ns]), list(sems[ns:]), list(rest[:nt]), list(rest[nt:2 * nt]), peers, rest[-1]


def _exchange_wait(state, per_dest, after, name, which=None):
    send_sems, recv_sems, srcs, lands, peers, _ = state
    which = list(range(len(srcs))) if which is None else which
    per_dest = [_flag(per_dest, t) for t in which]
    after = list(after) if isinstance(after, (list, tuple)) else [after]
    first = np.cumsum([0] + [len(ks) for ks in peers])
    pick = [first[t] + n for t in which for n in range(len(peers[t]))]
    peers = [peers[t] for t in which]
    send_sems, recv_sems = [send_sems[s] for s in pick], [recv_sems[s] for s in pick]
    srcs, lands = [srcs[t] for t in which], [lands[t] for t in which]
    nt = len(srcs)
    ns = len(send_sems)

    def body(*refs):
        sems = refs[2 * nt:2 * nt + 2 * ns]
        for cp in _exchange_copies(refs[:nt], refs[nt:2 * nt], sems[:ns], sems[ns:], per_dest, peers):
            cp.wait_send()
            cp.wait_recv()

    thru = list(srcs) + list(lands)
    res = _pc(
        body, name=name,
        out_shape=tuple(pltpu.HBM(a.shape, a.dtype) for a in thru),
        in_specs=[HBM] * (2 * nt) + [SEM] * (2 * ns) + [ANY] * len(after),
        out_specs=tuple([HBM] * (2 * nt)),
        input_output_aliases={i: i for i in range(2 * nt)},
        compiler_params=pltpu.CompilerParams(has_side_effects=EFFECT),
    )(*thru, *send_sems, *recv_sems, *after)
    return list(res[nt:])


def _mod_fwd(cond, ada_w, bias):
    depth, d, nb = ada_w.shape

    def body(c_ref, w_ref, b_ref, o_ref):
        s = _silu(c_ref[...]).astype(BF16)
        o_ref[...] = _dot(s, w_ref[...].astype(BF16)) + b_ref[...]

    return _pc(
        body, name="mod_fwd", grid=(depth,),
        in_specs=[pl.BlockSpec((16, d), lambda i: (0, 0)), pl.BlockSpec((None, d, nb), lambda i: (i, 0, 0)),
                  pl.BlockSpec((None, 1, nb), lambda i: (i, 0, 0))],
        out_specs=pl.BlockSpec((None, 16, nb), lambda i: (i, 0, 0)),
        out_shape=jax.ShapeDtypeStruct((depth, 16, nb), F32),
        compiler_params=_params("parallel"),
    )(cond, ada_w, bias.reshape(depth, 1, nb))


def _mod_bwd(cond, ada_w, dm_all, dm_mine):
    depth, d, nb = ada_w.shape
    d3 = dm_all.shape[-1]

    def body(c_ref, w_ref, all_ref, call_ref, mine_ref, cmine_ref, gw_ref, gb_ref, part_ref, ds_ref):
        i = pl.program_id(0)
        cond_v = c_ref[...]
        s = _silu(cond_v).astype(BF16)
        has_ctx = jnp.where(i < 2, 1.0, 0.0)
        tot_all = jnp.sum(call_ref[...], axis=0, keepdims=True) * has_ctx
        tot_mine = jnp.broadcast_to(jnp.sum(cmine_ref[...], axis=0, keepdims=True) * has_ctx, (8, nb)).astype(BF16)
        gb_ref[...] = jnp.sum(all_ref[...], axis=0, keepdims=True) + tot_all
        gw_ref[...] = _dot(s[0:8], mine_ref[...].astype(BF16), 0, 0) + _dot(s[8:16], tot_mine, 0, 0)
        part = _dot(tot_mine, w_ref[...].astype(BF16), 1, 1)

        @pl.when(i == 0)
        def _():
            part_ref[...] = jnp.zeros_like(part_ref)
            ds_ref[...] = _dsilu(cond_v)

        part_ref[...] += part

    def rows(width, which):
        return pl.BlockSpec((None, N_DEV, width), which)

    layer = lambda i: (i, 0, 0)
    ctx_layer = lambda i: (jnp.minimum(i, 1) + 4, 0, 0)
    return _pc(
        body, name="mod_bwd", grid=(depth,),
        in_specs=[pl.BlockSpec((16, d), lambda i: (0, 0)), pl.BlockSpec((None, d, nb), layer),
                  rows(d3, layer), rows(d3, ctx_layer), rows(nb, layer), rows(nb, ctx_layer)],
        out_specs=[pl.BlockSpec((None, d, nb), layer), pl.BlockSpec((None, 1, d3), layer),
                   pl.BlockSpec((8, d), lambda i: (0, 0)), pl.BlockSpec((16, d), lambda i: (0, 0))],
        out_shape=[jax.ShapeDtypeStruct((depth, d, nb), F32), jax.ShapeDtypeStruct((depth, 1, d3), F32),
                   jax.ShapeDtypeStruct((8, d), F32), jax.ShapeDtypeStruct((16, d), F32)],
        compiler_params=_params("arbitrary"),
    )(cond, ada_w, dm_all, dm_all, dm_mine, dm_mine)


def _norm_fwd(xs, g, mod, tr, seg_tiles, name):
    t, d = xs.shape

    def body(x_ref, g_ref, mod_ref, h_ref):
        x = x_ref[...]
        r = lax.rsqrt(jnp.mean(x * x, axis=-1, keepdims=True) + RMS_EPS)
        y = (x * r) * g_ref[...]
        h_ref[...] = (y * (1.0 + mod_ref[1:2, :]) + mod_ref[0:1, :]).astype(BF16)

    return _pc(
        body, name=name, grid=(t // tr,),
        in_specs=[pl.BlockSpec((tr, d), lambda i: (i, 0)), pl.BlockSpec((1, d), lambda i: (0, 0)),
                  pl.BlockSpec((None, 8, d), lambda i: (i // seg_tiles, 0, 0))],
        out_specs=pl.BlockSpec((tr, d), lambda i: (i, 0)),
        out_shape=jax.ShapeDtypeStruct((t, d), BF16),
        compiler_params=_params("parallel"),
    )(xs, g, mod)


def _resid_grad(dx, i, seg_tiles, yx_ref, gate_ref, dyx_ref, gsum_ref):
    dyx_ref[...] = (dx * gate_ref[2:3, :]).astype(BF16)

    @pl.when(i % seg_tiles == 0)
    def _():
        gsum_ref[...] = jnp.zeros_like(gsum_ref)

    gsum_ref[0:1, :] += jnp.sum(dx * yx_ref[...], axis=0, keepdims=True)


def _norm_bwd(xs, dh, dres, g, mod, tr, seg_tiles, name, res_tiles=None, out_tiles=None, below=None):
    t, d = xs.shape
    n_tiles = t // tr
    res_tiles = n_tiles if res_tiles is None else res_tiles
    out_tiles = n_tiles if out_tiles is None else out_tiles

    def body(x_ref, dh_ref, dres_ref, g_ref, mod_ref, *rest):
        i = pl.program_id(0)
        x = x_ref[...]
        r = lax.rsqrt(jnp.mean(x * x, axis=-1, keepdims=True) + RMS_EPS)
        xn = x * r
        dhv = dh_ref[...]
        gain = g_ref[...]
        one_scale = 1.0 + mod_ref[1:2, :]
        dxn = dhv * (gain * one_scale)
        dx = r * (dxn - xn * jnp.mean(dxn * xn, axis=-1, keepdims=True))
        if res_tiles == n_tiles:
            dx = dx + dres_ref[...]
        else:
            dx = dx + jnp.where(i < res_tiles, dres_ref[...], 0.0)
        if below is None:
            dx_ref, sum_ref = rest
        else:
            yx_ref, gate_ref, dx_ref, sum_ref, dyx_ref, gsum_ref = rest
            _resid_grad(dx, i, seg_tiles, yx_ref, gate_ref, dyx_ref, gsum_ref)
        if out_tiles == n_tiles:
            dx_ref[...] = dx
        else:
            @pl.when(i < out_tiles)
            def _():
                dx_ref[...] = dx

        @pl.when(i % seg_tiles == 0)
        def _():
            sum_ref[...] = jnp.zeros_like(sum_ref)

        sum_ref[0:1, :] += jnp.sum(dhv, axis=0, keepdims=True)
        sum_ref[1:2, :] += jnp.sum(dhv * (xn * gain), axis=0, keepdims=True)
        sum_ref[2:3, :] += jnp.sum(dhv * one_scale * xn, axis=0, keepdims=True)

    row = pl.BlockSpec((tr, d), lambda i: (i, 0))
    seg = pl.BlockSpec((None, 8, d), lambda i: (i // seg_tiles, 0, 0))
    in_specs = [row, row, pl.BlockSpec((tr, d), lambda i: (jnp.minimum(i, res_tiles - 1), 0)),
                pl.BlockSpec((1, d), lambda i: (0, 0)), seg]
    out_specs = [pl.BlockSpec((tr, d), lambda i: (jnp.minimum(i, out_tiles - 1), 0)), seg]
    out_shape = [jax.ShapeDtypeStruct((out_tiles * tr, d), F32), jax.ShapeDtypeStruct((mod.shape[0], 8, d), F32)]
    args = [xs, dh, dres, g, mod]
    if below is not None:
        in_specs += [row, seg]
        out_specs += [row, seg]
        out_shape += [jax.ShapeDtypeStruct((t, d), BF16), jax.ShapeDtypeStruct((below[1].shape[0], 8, d), F32)]
        args += list(below)
    return _pc(
        body, name=name, grid=(n_tiles,), in_specs=in_specs, out_specs=out_specs, out_shape=out_shape,
        compiler_params=_params("arbitrary"),
    )(*args)


def _loss_head(xs, target, g, yx, mod, tr):
    t, d = xs.shape

    def body(x_ref, t_ref, g_ref, yx_ref, gate_ref, loss_ref, dx_ref, dg_ref, dyx_ref, gsum_ref):
        i = pl.program_id(0)
        x = x_ref[...]
        r = lax.rsqrt(jnp.mean(x * x, axis=-1, keepdims=True) + RMS_EPS)
        xn = x * r
        gain = g_ref[...]
        err = xn * gain - t_ref[...]
        dy = err * (1.0 / d)
        dxn = dy * gain
        dx = r * (dxn - xn * jnp.mean(dxn * xn, axis=-1, keepdims=True))
        dx_ref[...] = dx
        _resid_grad(dx, i, t // tr, yx_ref, gate_ref, dyx_ref, gsum_ref)

        @pl.when(i == 0)
        def _():
            loss_ref[...] = jnp.zeros_like(loss_ref)
            dg_ref[...] = jnp.zeros_like(dg_ref)

        loss_ref[...] += 0.5 * jnp.sum(jnp.mean(err * err, axis=-1, keepdims=True))
        dg_ref[0:1, :] += jnp.sum(dy * xn, axis=0, keepdims=True)

    row = pl.BlockSpec((tr, d), lambda i: (i, 0))
    seg = pl.BlockSpec((None, 8, d), lambda i: (0, 0, 0))
    return _pc(
        body, name="loss_head", grid=(t // tr,),
        in_specs=[row, row, pl.BlockSpec((1, d), lambda i: (0, 0)), row, seg],
        out_specs=[pl.BlockSpec((8, LANES), lambda i: (0, 0)), row, pl.BlockSpec((8, d), lambda i: (0, 0)), row, seg],
        out_shape=[jax.ShapeDtypeStruct((8, LANES), F32), jax.ShapeDtypeStruct((t, d), F32),
                   jax.ShapeDtypeStruct((8, d), F32), jax.ShapeDtypeStruct((t, d), BF16),
                   jax.ShapeDtypeStruct((1, 8, d), F32)],
        compiler_params=_params("arbitrary"),
    )(xs, target, g, yx, mod)


def _proj_in(h, w, layer, width, name, blocks=None, dtype=F32):
    t, d = h.shape
    n8 = w.shape[-1]
    first, count = blocks if blocks is not None else (0, N_DEV)
    per_part = width // n8
    assert first % per_part == 0 and count % per_part == 0
    tm = _tile(t, 1152)

    def body(a_ref, b_ref, o_ref):
        a = a_ref[...]
        for s in range(per_part):
            o_ref[:, s * n8:(s + 1) * n8] = _dot(a, b_ref[s]).astype(dtype)

    return _pc(
        body, name=name, grid=(t // tm, count // per_part),
        in_specs=[pl.BlockSpec((tm, d), lambda i, j: (i, 0)),
                  pl.BlockSpec((per_part, None, d, n8), lambda i, j: (first // per_part + j, layer, 0, 0))],
        out_specs=pl.BlockSpec((None, tm, width), lambda i, j: (j, i, 0)),
        out_shape=jax.ShapeDtypeStruct((count // per_part, t, width), dtype),
        compiler_params=_params("parallel", "parallel"),
    )(h, w)


def _proj_in_split(h, w, layer, width, name, narrow):
    t, d = h.shape
    n8 = w.shape[-1]
    per_part = width // n8
    parts = N_DEV // per_part
    tm = _tile(t, 1152)

    def body(a_ref, b_ref, lo_ref, hi_ref):
        j = pl.program_id(1)
        a = a_ref[...]
        cols = [_dot(a, b_ref[s]) for s in range(per_part)]

        @pl.when(j < narrow)
        def _():
            for s in range(per_part):
                lo_ref[:, s * n8:(s + 1) * n8] = cols[s].astype(BF16)

        @pl.when(j >= narrow)
        def _():
            for s in range(per_part):
                hi_ref[:, s * n8:(s + 1) * n8] = cols[s]

    return _pc(
        body, name=name, grid=(t // tm, parts),
        in_specs=[pl.BlockSpec((tm, d), lambda i, j: (i, 0)),
                  pl.BlockSpec((per_part, None, d, n8), lambda i, j: (j, layer, 0, 0))],
        out_specs=[pl.BlockSpec((None, tm, width), lambda i, j: (jnp.minimum(j, narrow - 1), i, 0)),
                   pl.BlockSpec((None, tm, width), lambda i, j: (jnp.maximum(j - narrow, 0), i, 0))],
        out_shape=[jax.ShapeDtypeStruct((narrow, t, width), BF16), jax.ShapeDtypeStruct((parts - narrow, t, width), F32)],
        compiler_params=_params("parallel", "arbitrary"),
    )(h, w)


def _proj_out(z, w, res, mod, tm, seg_tiles, name, nxt=None):
    t, k = z.shape
    d = w.shape[1]

    def body(z_ref, w_ref, res_ref, mod_ref, *rest):
        yx = _dot(z_ref[...], w_ref[...])
        x = res_ref[...] + mod_ref[2:3, :] * yx
        if nxt is None:
            yx_ref, x_ref = rest
        else:
            g_ref, nmod_ref, yx_ref, x_ref, h_ref = rest
            r = lax.rsqrt(jnp.mean(x * x, axis=-1, keepdims=True) + RMS_EPS)
            h_ref[...] = (((x * r) * g_ref[...]) * (1.0 + nmod_ref[1:2, :]) + nmod_ref[0:1, :]).astype(BF16)
        yx_ref[...] = yx
        x_ref[...] = x

    tile = pl.BlockSpec((tm, d), lambda i: (i, 0))
    seg = pl.BlockSpec((None, 8, d), lambda i: (i // seg_tiles, 0, 0))
    in_specs = [pl.BlockSpec((tm, k), lambda i: (i, 0)), pl.BlockSpec((k, d), lambda i: (0, 0)), tile, seg]
    out_specs = [tile, tile]
    out_shape = [jax.ShapeDtypeStruct((t, d), F32), jax.ShapeDtypeStruct((t, d), F32)]
    args = [z, w, res, mod]
    if nxt is not None:
        in_specs += [pl.BlockSpec((1, d), lambda i: (0, 0)), seg]
        out_specs.append(tile)
        out_shape.append(jax.ShapeDtypeStruct((t, d), BF16))
        args += list(nxt)
    return _pc(
        body, name=name, grid=(t // tm,), in_specs=in_specs, out_specs=out_specs, out_shape=out_shape,
        compiler_params=_params("parallel"),
    )(*args)


def _proj_out_bwd(dyx, z, w, name):
    t, d = dyx.shape
    width = w.shape[0]
    tm = _tile(t, 768)
    nk = t // tm

    def body(a_ref, z_ref, w_ref, dz_ref, dw_ref, acc_ref):
        k = pl.program_id(0)

        @pl.when(k == 0)
        def _():
            acc_ref[...] = jnp.zeros_like(acc_ref)

        a = a_ref[...]
        dz_ref[...] = _dot(a, w_ref[...], 1, 1)
        acc_ref[...] += _dot(z_ref[...].T, a)

        @pl.when(k == nk - 1)
        def _():
            dw_ref[...] = acc_ref[...].astype(BF16)

    return _pc(
        body, name=name, grid=(nk,),
        in_specs=[pl.BlockSpec((tm, d), lambda k: (k, 0)), pl.BlockSpec((tm, width), lambda k: (k, 0)),
                  pl.BlockSpec((width, d), lambda k: (0, 0))],
        out_specs=[pl.BlockSpec((tm, width), lambda k: (k, 0)), pl.BlockSpec((width, d), lambda k: (0, 0))],
        out_shape=[jax.ShapeDtypeStruct((t, width), F32), jax.ShapeDtypeStruct((width, d), BF16)],
        scratch_shapes=[pltpu.VMEM((width, d), F32)],
        compiler_params=_params("arbitrary"),
    )(dyx, z, w)


def _proj_in_dh(dpre, w, layer, name, after=None):
    parts, t, width = dpre.shape
    d, n8 = w.shape[-2:]
    per_part = width // n8
    tm, tn = _tile(t, 768), _tile(d, 512)

    def body(a_ref, w_ref, *rest):
        o_ref = rest[-1]
        acc = None
        for p in range(parts):
            for s in range(per_part):
                term = _dot(a_ref[p, :, s * n8:(s + 1) * n8], w_ref[p * per_part + s], 1, 1)
                acc = term if acc is None else acc + term
        o_ref[...] = acc

    extra = [] if after is None else [after]
    return _pc(
        body, name=name, grid=(t // tm, d // tn),
        in_specs=[pl.BlockSpec((parts, tm, width), lambda i, j: (0, i, 0)),
                  pl.BlockSpec((N_DEV, None, tn, n8), lambda i, j: (0, layer, j, 0))] + [ANY] * len(extra),
        out_specs=pl.BlockSpec((tm, tn), lambda i, j: (i, j)),
        out_shape=jax.ShapeDtypeStruct((t, d), F32),
        compiler_params=_params("parallel", "parallel"),
    )(dpre, w, *extra)


def _transposed(a_ref):
    return a_ref[...].T


def _grad_w_in(h, dpre, n8, name):
    t, d = h.shape
    parts, _, width = dpre.shape
    per_part = width // n8
    tm, tk = _tile(d, 512), _tile(t, 1152)
    nk = t // tk

    def body(a_ref, b_ref, o_ref, acc_ref):
        k = pl.program_id(1)

        @pl.when(k == 0)
        def _():
            acc_ref[...] = jnp.zeros_like(acc_ref)

        at = _transposed(a_ref)
        for p in range(parts):
            r = _dot(at, b_ref[p])
            for s in range(per_part):
                acc_ref[p * per_part + s] += r[:, s * n8:(s + 1) * n8]

        @pl.when(k == nk - 1)
        def _():
            o_ref[...] = acc_ref[...].astype(BF16)

    return _pc(
        body, name=name, grid=(d // tm, nk),
        in_specs=[pl.BlockSpec((tk, tm), lambda i, k: (k, i)), pl.BlockSpec((parts, tk, width), lambda i, k: (0, k, 0))],
        out_specs=pl.BlockSpec((parts * per_part, tm, n8), lambda i, k: (0, i, 0)),
        out_shape=jax.ShapeDtypeStruct((parts * per_part, d, n8), BF16),
        scratch_shapes=[pltpu.VMEM((parts * per_part, tm, n8), F32)],
        compiler_params=_params("parallel", "arbitrary"),
    )(h, dpre)


def _shift(v, k):
    n = v.shape[0]
    return pltpu.roll(v, k % n, 0)


def _window_sum(v, win):
    s = v + _shift(v, 1)
    step = 1
    while 2 * step < win:
        s = _shift(s, step) + _shift(s, -step)
        step *= 2
    return s


def _window_count(base, seg_len, win, shape):
    t = base + lax.broadcasted_iota(jnp.int32, shape, 0)
    hi = jnp.minimum(t + win // 2, seg_len)
    lo = jnp.maximum(t - win // 2, 0)
    return (hi - lo).astype(F32)


def _pad_offsets(segs):
    return [HALO * (s + 1) + st for s, (st, _) in enumerate(segs)]


def _for_chunks(segs, fn):
    offs = _pad_offsets(segs)
    for s, (st, ln) in enumerate(segs):
        def step(ci, carry, s=s, st=st, ln=ln):
            fn(s, st, ln, offs[s], pl.multiple_of(ci * CHUNK, CHUNK))
            return carry
        lax.fori_loop(0, ln // CHUNK, step, 0)


def _pool_fwd(pre, w_grp, scale, segs, name):
    _, t, width = pre.shape
    grp = width // len(POOL_WINDOWS)
    padded = t + HALO * (len(segs) + 1)

    def group(win, pre_ref, w_ref, sc_ref, z_ref, diff_ref, pad_ref):
        pad_ref[...] = jnp.zeros_like(pad_ref)

        def fill(s, st, ln, off, b):
            pad_ref[pl.ds(off + b, CHUNK), :] = pre_ref[0, pl.ds(st + b, CHUNK), :]

        _for_chunks(segs, fill)

        def mix(s, st, ln, off, b):
            ext = pad_ref[pl.ds(off - HALO + b, CHUNK + 2 * HALO), :]
            total = _window_sum(ext, win)[HALO:HALO + CHUNK]
            u = pre_ref[0, pl.ds(st + b, CHUNK), :]
            diff = (total / _window_count(b, ln, win, u.shape) - u).astype(BF16)
            mixed = _dot(diff, w_ref[...])
            gate = _silu(pre_ref[1, pl.ds(st + b, CHUNK), :])
            z_ref[pl.ds(st + b, CHUNK), :] = (mixed * sc_ref[...] * gate).astype(BF16)
            diff_ref[pl.ds(st + b, CHUNK), :] = diff

        _for_chunks(segs, mix)

    def body(pre_ref, w_ref, sc_ref, z_ref, diff_ref, pad_ref):
        gi = pl.program_id(0)
        for widx, win in enumerate(POOL_WINDOWS):
            @pl.when(gi == widx)
            def _(win=win):
                group(win, pre_ref, w_ref, sc_ref, z_ref, diff_ref, pad_ref)

    col = pl.BlockSpec((t, grp), lambda g: (0, g))
    return _pc(
        body, name=name, grid=(len(POOL_WINDOWS),),
        in_specs=[pl.BlockSpec((2, t, grp), lambda g: (0, 0, g)), pl.BlockSpec((None, grp, grp), lambda g: (g, 0, 0)),
                  pl.BlockSpec((1, grp), lambda g: (0, g))],
        out_specs=[col, col],
        out_shape=[jax.ShapeDtypeStruct((t, width), BF16), jax.ShapeDtypeStruct((t, width), BF16)],
        scratch_shapes=[pltpu.VMEM((padded, grp), F32)],
        compiler_params=_params("parallel"),
    )(pre, w_grp, scale)


def _pool_bwd(dz, diff, pre, w_grp, scale, segs, name):
    _, t, width = pre.shape
    grp = width // len(POOL_WINDOWS)
    padded = t + HALO * (len(segs) + 1)

    def group(win, dz_ref, diff_ref, pre_ref, w_ref, sc_ref, dpre_ref, dw_ref, dsc_ref, pad_ref, dd_ref):
        pad_ref[...] = jnp.zeros_like(pad_ref)
        dw_ref[...] = jnp.zeros_like(dw_ref)
        dsc_ref[...] = jnp.zeros_like(dsc_ref)

        def first(s, st, ln, off, b):
            rows = pl.ds(st + b, CHUNK)
            diff_v = diff_ref[rows, :]
            mixed = _dot(diff_v, w_ref[...])
            g = pre_ref[1, rows, :]
            sg = _silu(g)
            dzv = dz_ref[rows, :]
            dmixed = (dzv * sc_ref[...] * sg).astype(BF16)
            dsc_ref[...] += jnp.sum(dzv * mixed * sg, axis=0, keepdims=True)
            dpre_ref[1, rows, :] = (dzv * mixed * sc_ref[...] * _dsilu(g)).astype(BF16)
            ddiff = _dot(dmixed, w_ref[...], 1, 1)
            dw_ref[...] += _dot(diff_v, dmixed, 0, 0)
            dd_ref[rows, :] = ddiff
            pad_ref[pl.ds(off + b, CHUNK), :] = ddiff / _window_count(b, ln, win, ddiff.shape)

        _for_chunks(segs, first)

        def second(s, st, ln, off, b):
            rows = pl.ds(st + b, CHUNK)
            ext = pad_ref[pl.ds(off - HALO + b, CHUNK + 2 * HALO), :]
            total = _shift(_window_sum(ext, win), -1)[HALO:HALO + CHUNK]
            dpre_ref[0, rows, :] = (total - dd_ref[rows, :]).astype(BF16)

        _for_chunks(segs, second)

    def body(dz_ref, diff_ref, pre_ref, w_ref, sc_ref, dpre_ref, dw_ref, dsc_ref, pad_ref, dd_ref):
        gi = pl.program_id(0)
        for widx, win in enumerate(POOL_WINDOWS):
            @pl.when(gi == widx)
            def _(win=win):
                group(win, dz_ref, diff_ref, pre_ref, w_ref, sc_ref, dpre_ref, dw_ref, dsc_ref, pad_ref, dd_ref)

    col = pl.BlockSpec((t, grp), lambda g: (0, g))
    both = pl.BlockSpec((2, t, grp), lambda g: (0, 0, g))
    wspec = pl.BlockSpec((None, grp, grp), lambda g: (g, 0, 0))
    sspec = pl.BlockSpec((1, grp), lambda g: (0, g))
    return _pc(
        body, name=name, grid=(len(POOL_WINDOWS),),
        in_specs=[col, col, both, wspec, sspec],
        out_specs=[both, wspec, sspec],
        out_shape=[jax.ShapeDtypeStruct((2, t, width), BF16), jax.ShapeDtypeStruct((len(POOL_WINDOWS), grp, grp), F32),
                   jax.ShapeDtypeStruct((1, width), F32)],
        scratch_shapes=[pltpu.VMEM((padded, grp), F32), pltpu.VMEM((t, grp), F32)],
        compiler_params=_params("parallel"),
    )(dz, diff, pre, w_grp, scale)


def _conv_fwd(pre, dw, db, name):
    _, t, width = pre.shape
    cb = LANES
    segs = [(0, t)]

    def body(pre_ref, dw_ref, db_ref, z_ref, pad_ref):
        pad_ref[...] = jnp.zeros_like(pad_ref)

        def fill(s, st, ln, off, b):
            rows = pl.ds(b, CHUNK)
            pad_ref[pl.ds(off + b, CHUNK), :] = pre_ref[1, rows, :] * pre_ref[2, rows, :]

        _for_chunks(segs, fill)

        def mix(s, st, ln, off, b):
            rows = pl.ds(b, CHUNK)
            ext = pad_ref[pl.ds(off - HALO + b, CHUNK + 2 * HALO), :]
            conv = (dw_ref[0:1, :] * _shift(ext, 1) + dw_ref[1:2, :] * ext + dw_ref[2:3, :] * _shift(ext, -1))
            conv = conv[HALO:HALO + CHUNK] + db_ref[...]
            y = pre_ref[0, rows, :] * conv
            z_ref[rows, :] = (y * _silu(pre_ref[3, rows, :])).astype(BF16)

        _for_chunks(segs, mix)

    return _pc(
        body, name=name, grid=(width // cb,),
        in_specs=[pl.BlockSpec((4, t, cb), lambda j: (0, 0, j)), pl.BlockSpec((8, cb), lambda j: (0, j)),
                  pl.BlockSpec((1, cb), lambda j: (0, j))],
        out_specs=pl.BlockSpec((t, cb), lambda j: (0, j)),
        out_shape=jax.ShapeDtypeStruct((t, width), BF16),
        scratch_shapes=[pltpu.VMEM((t + 2 * HALO, cb), F32)],
        compiler_params=_params("parallel"),
    )(pre, dw, db)


def _conv_bwd(dz, pre, dw, db, name):
    _, t, width = pre.shape
    cb = LANES
    segs = [(0, t)]

    def body(dz_ref, pre_ref, dw_ref, db_ref, dpre_ref, ddw_ref, ddb_ref, pad_a, pad_c):
        pad_a[...] = jnp.zeros_like(pad_a)
        pad_c[...] = jnp.zeros_like(pad_c)
        ddw_ref[...] = jnp.zeros_like(ddw_ref)
        ddb_ref[...] = jnp.zeros_like(ddb_ref)

        def fill(s, st, ln, off, b):
            rows = pl.ds(b, CHUNK)
            pad_a[pl.ds(off + b, CHUNK), :] = pre_ref[1, rows, :] * pre_ref[2, rows, :]

        _for_chunks(segs, fill)

        def first(s, st, ln, off, b):
            rows = pl.ds(b, CHUNK)
            ext = pad_a[pl.ds(off - HALO + b, CHUNK + 2 * HALO), :]
            prev, nxt = _shift(ext, 1)[HALO:HALO + CHUNK], _shift(ext, -1)[HALO:HALO + CHUNK]
            here = ext[HALO:HALO + CHUNK]
            conv = dw_ref[0:1, :] * prev + dw_ref[1:2, :] * here + dw_ref[2:3, :] * nxt + db_ref[...]
            bg, g = pre_ref[0, rows, :], pre_ref[3, rows, :]
            dzv = dz_ref[rows, :]
            dy = dzv * _silu(g)
            dpre_ref[3, rows, :] = (dzv * (bg * conv) * _dsilu(g)).astype(BF16)
            dpre_ref[0, rows, :] = (dy * conv).astype(BF16)
            dconv = dy * bg
            pad_c[pl.ds(off + b, CHUNK), :] = dconv
            ddw_ref[0:1, :] += jnp.sum(dconv * prev, axis=0, keepdims=True)
            ddw_ref[1:2, :] += jnp.sum(dconv * here, axis=0, keepdims=True)
            ddw_ref[2:3, :] += jnp.sum(dconv * nxt, axis=0, keepdims=True)
            ddb_ref[0:1, :] += jnp.sum(dconv, axis=0, keepdims=True)

        _for_chunks(segs, first)

        def second(s, st, ln, off, b):
            rows = pl.ds(b, CHUNK)
            ext = pad_c[pl.ds(off - HALO + b, CHUNK + 2 * HALO), :]
            da = (dw_ref[0:1, :] * _shift(ext, -1) + dw_ref[1:2, :] * ext + dw_ref[2:3, :] * _shift(ext, 1))
            da = da[HALO:HALO + CHUNK]
            dpre_ref[1, rows, :] = (da * pre_ref[2, rows, :]).astype(BF16)
            dpre_ref[2, rows, :] = (da * pre_ref[1, rows, :]).astype(BF16)

        _for_chunks(segs, second)

    quad = pl.BlockSpec((4, t, cb), lambda j: (0, 0, j))
    rows8 = pl.BlockSpec((8, cb), lambda j: (0, j))
    return _pc(
        body, name=name, grid=(width // cb,),
        in_specs=[pl.BlockSpec((t, cb), lambda j: (0, j)), quad, rows8, pl.BlockSpec((1, cb), lambda j: (0, j))],
        out_specs=[quad, rows8, rows8],
        out_shape=[jax.ShapeDtypeStruct((4, t, width), BF16), jax.ShapeDtypeStruct((8, width), F32),
                   jax.ShapeDtypeStruct((8, width), F32)],
        scratch_shapes=[pltpu.VMEM((t + 2 * HALO, cb), F32), pltpu.VMEM((t + 2 * HALO, cb), F32)],
        compiler_params=_params("parallel"),
    )(dz, pre, dw, db)


PAIR_TILES = 2 * WIN_ROWS - 2


def _pair_geometry():
    lane = lax.broadcasted_iota(jnp.int32, (GRID_W, LANES), 1)
    qcol = lax.broadcasted_iota(jnp.int32, (GRID_W, LANES), 0)
    low = lane < GRID_W
    kcol = jnp.where(low, lane, lane - GRID_W)
    start = jnp.clip(qcol - WIN_COLS // 2, 0, GRID_W - WIN_COLS)
    inside = (kcol >= start) & (kcol < start + WIN_COLS)
    return low, inside


def _bias_tiles(rpb_ref, rows_ref, tiles_ref, inside):
    for h in range(2):
        rows = rpb_ref[h]
        rows_ref[h] = (pltpu.roll(rows, LANES - (WIN_COLS - 1), 1)
                       + pltpu.roll(pltpu.roll(rows, GRID_W - (WIN_COLS - 1), 1), 2 * WIN_ROWS - 1, 0))
        for t in range(PAIR_TILES):
            both = jnp.broadcast_to(rows_ref[h, t:t + 1, :], (GRID_W, LANES))
            tiles_ref[h, t] = jnp.where(inside, pltpu.roll(both, 0, 1, stride=1, stride_axis=0), MASKED)


def _bias_tiles_grad(dtiles_ref, drpb_ref):
    sub = 8
    qcol = lax.broadcasted_iota(jnp.int32, (PAIR_TILES * sub, LANES), 0) & (sub - 1)
    lane = lax.broadcasted_iota(jnp.int32, (1, LANES), 1)
    zero = jnp.zeros((1, LANES), F32)
    for h in range(2):
        v = dtiles_ref[h]
        rows = GRID_W
        while rows > sub:
            rows //= 2
            upper = pltpu.roll(v[:, rows:].reshape(PAIR_TILES * rows, LANES), LANES - rows, 1)
            v = v[:, :rows] + upper.reshape(PAIR_TILES, rows, LANES)
        v = v.reshape(PAIR_TILES * sub, LANES)
        for bit in range(3):
            v = jnp.where((qcol >> bit) & 1 == 1, pltpu.roll(v, LANES - (1 << bit), 1), v)
        v = pltpu.roll(v, WIN_COLS - 1, 1)
        sums = [jnp.sum(v[t * sub:(t + 1) * sub], axis=0, keepdims=True) for t in range(PAIR_TILES)]
        for r in range(2 * WIN_ROWS):
            here = sums[r] if r < PAIR_TILES else zero
            prev = pltpu.roll(sums[r - 1], GRID_W, 1) if 1 <= r <= PAIR_TILES else zero
            drpb_ref[h, r:r + 1, :] = jnp.where(lane < 2 * WIN_COLS - 1, here + prev, 0.0)


def _attn_rows(r, n_rows):
    first = jnp.clip(r - WIN_ROWS // 2, 0, n_rows - WIN_ROWS)
    return first, first - r + WIN_ROWS - 1


def _softmax(s_loc, s_ctx):
    m = jnp.maximum(jnp.max(s_loc, axis=-1, keepdims=True), jnp.max(s_ctx, axis=-1, keepdims=True))
    e_loc, e_ctx = jnp.exp(s_loc - m), jnp.exp(s_ctx - m)
    inv = 1.0 / (jnp.sum(e_loc, axis=-1, keepdims=True) + jnp.sum(e_ctx, axis=-1, keepdims=True))
    return e_loc * inv, e_ctx * inv


def _pair_bias(tiles_ref, j):
    return jnp.concatenate(
        [jnp.concatenate([tiles_ref[h, j + 2 * m] for m in range(WIN_ROWS // 2)], axis=1) for h in range(2)], axis=0)


ROWS_PER_STEP = 8
ROWS_PER_STEP_BWD = 4


def _by_head(tile, low):
    zero = jnp.zeros_like(tile)
    return jnp.concatenate([jnp.where(low, tile, zero), jnp.where(low, zero, tile)], axis=0)


def _merge_heads(stacked, low):
    return jnp.where(low, stacked[:GRID_W], stacked[GRID_W:])


def _attn_items(step, n_rows, q_ref, low, per_step):
    items = []
    for u in range(per_step):
        r = step * per_step + u
        first, j = _attn_rows(r, n_rows)
        rows = pl.ds(pl.multiple_of(r * GRID_W, GRID_W), GRID_W)
        keys = pl.ds(pl.multiple_of(first * GRID_W, GRID_W), WIN_ROWS * GRID_W)
        q = (q_ref[rows, :].astype(F32) * HEAD_DIM ** -0.5).astype(BF16)
        items.append((rows, keys, j, _by_head(q, low)))
    return items


def _attn_fwd(qkv, gate, rpb, seq):
    _, t, width = qkv.shape
    n_rows = seq // GRID_W
    n_ctx = t - seq

    def body(q_ref, k_ref, v_ref, g_ref, rpb_ref, z_ref, o_ref, rows_ref, tiles_ref):
        low, inside = _pair_geometry()
        _bias_tiles(rpb_ref, rows_ref, tiles_ref, inside)
        ctx = pl.ds(seq, n_ctx)

        def step(i, carry):
            items = _attn_items(i, n_rows, q_ref, low, ROWS_PER_STEP)
            k_ctx, v_ctx = k_ref[ctx, :], v_ref[ctx, :]
            scores = [(_dot(q, k_ref[keys, :], 1, 1) + _pair_bias(tiles_ref, j), _dot(q, k_ctx, 1, 1))
                      for _, keys, j, q in items]
            probs = [_softmax(s_loc, s_ctx) for s_loc, s_ctx in scores]
            outs = [_dot(p_loc.astype(BF16), v_ref[keys, :]) + _dot(p_ctx.astype(BF16), v_ctx)
                    for (_, keys, _, _), (p_loc, p_ctx) in zip(items, probs)]
            for (rows, _, _, _), out in zip(items, outs):
                o = _merge_heads(out, low)
                o_ref[rows, :] = o
                z_ref[rows, :] = (o * _silu(g_ref[rows, :])).astype(BF16)
            return carry

        lax.fori_loop(0, n_rows // ROWS_PER_STEP, step, 0)

    def part(p):
        return pl.BlockSpec((None, t, LANES), lambda h: (p, 0, h))

    out = pl.BlockSpec((seq, LANES), lambda h: (0, h))
    return _pc(
        body, name="attn_fwd", grid=(width // LANES,),
        in_specs=[part(0), part(1), part(2), part(0), pl.BlockSpec((2, 2 * WIN_ROWS, LANES), lambda h: (h, 0, 0))],
        out_specs=[out, out],
        out_shape=[jax.ShapeDtypeStruct((seq, width), BF16), jax.ShapeDtypeStruct((seq, width), F32)],
        scratch_shapes=[pltpu.VMEM((2, 2 * WIN_ROWS, LANES), F32), pltpu.VMEM((2, PAIR_TILES, GRID_W, LANES), F32)],
        compiler_params=_params("parallel"),
    )(qkv, qkv, qkv, gate, rpb)


def _attn_bwd(qkv, gate, o, dz, rpb, seq):
    _, t, width = qkv.shape
    n_rows = seq // GRID_W
    n_ctx = t - seq
    heads = 2 * width // LANES

    def body(q_ref, k_ref, v_ref, g_ref, o_ref, dz_ref, rpb_ref, dpre_ref, drpb_ref,
             rows_ref, tiles_ref, dtiles_ref, dk_ref, dv_ref):
        low, inside = _pair_geometry()
        _bias_tiles(rpb_ref, rows_ref, tiles_ref, inside)
        dtiles_ref[...] = jnp.zeros_like(dtiles_ref)
        dk_ref[...] = jnp.zeros_like(dk_ref)
        dv_ref[...] = jnp.zeros_like(dv_ref)
        ctx = pl.ds(seq, n_ctx)

        def step(i, carry):
            items = _attn_items(i, n_rows, q_ref, low, ROWS_PER_STEP_BWD)
            k_ctx, v_ctx = k_ref[ctx, :], v_ref[ctx, :]
            d_outs = []
            for rows, _, _, _ in items:
                g = g_ref[rows, :]
                dzv = dz_ref[rows, :]
                dpre_ref[3, rows, :] = (dzv * o_ref[rows, :] * _dsilu(g)).astype(BF16)
                d_outs.append(_by_head((dzv * _silu(g)).astype(BF16), low))
            scores = [(_dot(q, k_ref[keys, :], 1, 1) + _pair_bias(tiles_ref, j), _dot(q, k_ctx, 1, 1))
                      for _, keys, j, q in items]
            dprobs = [(_dot(d_o, v_ref[keys, :], 1, 1), _dot(d_o, v_ctx, 1, 1))
                      for (_, keys, _, _), d_o in zip(items, d_outs)]
            probs = [_softmax(s_loc, s_ctx) for s_loc, s_ctx in scores]
            dscores = []
            for (p_loc, p_ctx), (dp_loc, dp_ctx) in zip(probs, dprobs):
                delta = (jnp.sum(p_loc * dp_loc, axis=-1, keepdims=True)
                         + jnp.sum(p_ctx * dp_ctx, axis=-1, keepdims=True))
                dscores.append((p_loc * (dp_loc - delta), p_ctx * (dp_ctx - delta)))
            dqs = [_dot(ds_loc.astype(BF16), k_ref[keys, :]) + _dot(ds_ctx.astype(BF16), k_ctx)
                   for (_, keys, _, _), (ds_loc, ds_ctx) in zip(items, dscores)]
            for (rows, _, _, _), dq in zip(items, dqs):
                dpre_ref[0, rows, :] = (_merge_heads(dq, low) * HEAD_DIM ** -0.5).astype(BF16)
            for (_, keys, j, q), d_o, (p_loc, p_ctx), (ds_loc, ds_ctx) in zip(items, d_outs, probs, dscores):
                dk_ref[keys, :] += _dot(ds_loc.astype(BF16), q, 0, 0)
                dk_ref[ctx, :] += _dot(ds_ctx.astype(BF16), q, 0, 0)
                dv_ref[keys, :] += _dot(p_loc.astype(BF16), d_o, 0, 0)
                dv_ref[ctx, :] += _dot(p_ctx.astype(BF16), d_o, 0, 0)
                for h in range(2):
                    for m in range(WIN_ROWS // 2):
                        dtiles_ref[h, j + 2 * m] += ds_loc[h * GRID_W:(h + 1) * GRID_W, m * LANES:(m + 1) * LANES]
            return carry

        lax.fori_loop(0, n_rows // ROWS_PER_STEP_BWD, step, 0)
        dpre_ref[1] = dk_ref[...].astype(BF16)
        dpre_ref[2] = dv_ref[...].astype(BF16)
        dpre_ref[0, ctx, :] = jnp.zeros((n_ctx, LANES), BF16)
        dpre_ref[3, ctx, :] = jnp.zeros((n_ctx, LANES), BF16)
        _bias_tiles_grad(dtiles_ref, drpb_ref)

    def part(p):
        return pl.BlockSpec((None, t, LANES), lambda h: (p, 0, h))

    lat = pl.BlockSpec((seq, LANES), lambda h: (0, h))
    rspec = pl.BlockSpec((2, 2 * WIN_ROWS, LANES), lambda h: (h, 0, 0))
    tiles = pltpu.VMEM((2, PAIR_TILES, GRID_W, LANES), F32)
    return _pc(
        body, name="attn_bwd", grid=(width // LANES,),
        in_specs=[part(0), part(1), part(2), part(0), lat, lat, rspec],
        out_specs=[pl.BlockSpec((4, t, LANES), lambda h: (0, 0, h)), rspec],
        out_shape=[jax.ShapeDtypeStruct((4, t, width), BF16), jax.ShapeDtypeStruct((heads, 2 * WIN_ROWS, LANES), F32)],
        scratch_shapes=[pltpu.VMEM((2, 2 * WIN_ROWS, LANES), F32), tiles, tiles,
                        pltpu.VMEM((t, LANES), F32), pltpu.VMEM((t, LANES), F32)],
        compiler_params=_params("parallel"),
    )(qkv, qkv, qkv, gate, o, dz, rpb)


def _adam_update(w, m, v, g):
    m2 = ADAM_B1 * m + (1.0 - ADAM_B1) * g
    v2 = ADAM_B2 * v + (1.0 - ADAM_B2) * (g * g)
    m_hat = m2 / (1.0 - ADAM_B1 ** ADAM_STEP)
    v_hat = v2 / (1.0 - ADAM_B2 ** ADAM_STEP)
    return -ADAM_LR * (m_hat / (jnp.sqrt(v_hat) + ADAM_EPS) + ADAM_WD * w), m2, v2


def _adamw(w, m, v, parts, name, after=None):
    rows, cols = w.shape
    tr = _tile(rows, max(8, ADAM_TILE_ELEMS // cols), 8)
    n_parts = len(parts)

    def body(*refs):
        w_ref, m_ref, v_ref = refs[:3]
        part_refs = refs[3:3 + n_parts]
        g_ref, d_ref, nm_ref, nv_ref = refs[-4:]
        g = part_refs[0][...].astype(F32)
        for p in part_refs[1:]:
            g = g + p[...].astype(F32)
        g_ref[...] = g
        d_ref[...], nm_ref[...], nv_ref[...] = _adam_update(w_ref[...], m_ref[...], v_ref[...], g)

    tile = pl.BlockSpec((tr, cols), lambda i: (i, 0))
    in_specs, args = [tile, tile, tile], [w, m, v]
    for p in parts:
        if isinstance(p, tuple):
            arr, k = p
            in_specs.append(pl.BlockSpec((None, tr, cols), lambda i, k=k: (k, i, 0)))
            args.append(arr)
        else:
            in_specs.append(tile)
            args.append(p)
    if after is not None:
        in_specs.append(ANY)
        args.append(after)
    shape = jax.ShapeDtypeStruct((rows, cols), F32)
    return _pc(
        body, name=name, grid=(rows // tr,), in_specs=in_specs, out_specs=[tile] * 4, out_shape=[shape] * 4,
        compiler_params=_params("parallel"),
    )(*args)


def _adamw_layers(w, m, v, landed, name):
    n_layers, rows, cols = w.shape
    tr = _tile(rows, max(8, ADAM_TILE_ELEMS // cols), 8)

    def body(*refs):
        w_ref, m_ref, v_ref = refs[:3]
        part_refs = refs[3:3 + n_layers * N_DEV]
        g_ref, d_ref, nm_ref, nv_ref = refs[3 + n_layers * N_DEV:]
        layer = pl.program_id(0)
        g = None
        for l in range(n_layers):
            s = part_refs[l * N_DEV][...].astype(F32)
            for p in part_refs[l * N_DEV + 1:(l + 1) * N_DEV]:
                s = s + p[...].astype(F32)
            g = s if g is None else jnp.where(layer == l, s, g)
        g_ref[...] = g
        d_ref[...], nm_ref[...], nv_ref[...] = _adam_update(w_ref[...], m_ref[...], v_ref[...], g)

    tile = pl.BlockSpec((None, tr, cols), lambda l, i: (l, i, 0))
    in_specs, args = [tile, tile, tile], [w, m, v]
    for l, arr in enumerate(landed):
        for k in range(N_DEV):
            in_specs.append(pl.BlockSpec((None, tr, cols), lambda ll, i, l=l, k=k: (k, jnp.where(ll == l, i, 0), 0)))
            args.append(arr)
    shape = jax.ShapeDtypeStruct(w.shape, F32)
    return _pc(
        body, name=name, grid=(n_layers, rows // tr), in_specs=in_specs, out_specs=[tile] * 4, out_shape=[shape] * 4,
        compiler_params=_params("arbitrary", "arbitrary"),
    )(*args)


def _adamw_small(states, grads):
    sources, makers = grads
    n, ns = len(states), len(sources)

    def body(*refs):
        src = refs[:ns]
        ins = refs[ns:ns + 3 * n]
        outs = refs[ns + 3 * n:]
        for k in range(n):
            w_ref, m_ref, v_ref = ins[3 * k:3 * k + 3]
            g = makers[k](*src)
            outs[4 * k][...] = g
            outs[4 * k + 1][...], outs[4 * k + 2][...], outs[4 * k + 3][...] = _adam_update(
                w_ref[...], m_ref[...], v_ref[...], g)

    flat = [a for s in states for a in s]
    vmem = pl.BlockSpec(memory_space=pltpu.VMEM)
    res = _pc(
        body, name="adamw_small",
        in_specs=[vmem] * (ns + 3 * n), out_specs=[vmem] * (4 * n),
        out_shape=[jax.ShapeDtypeStruct(s[0].shape, F32) for s in states for _ in range(4)],
        compiler_params=pltpu.CompilerParams(vmem_limit_bytes=VMEM_LIMIT),
    )(*sources, *flat)
    return [res[4 * k:4 * k + 4] for k in range(n)]


def _rows128(a):
    flat = a.reshape(-1)
    pad = (-flat.shape[0]) % LANES
    if pad:
        flat = jnp.concatenate([flat, jnp.zeros((pad,), flat.dtype)])
    return flat.reshape(-1, LANES)


def _pad_rows(a, mult=8):
    pad = (-a.shape[0]) % mult
    if pad:
        a = jnp.concatenate([a, jnp.zeros((pad,) + a.shape[1:], a.dtype)], axis=0)
    return a


def kernel(x, c, ctx, c_ctx, norm_g, ada_w, ada_b, pool_w_in, pool_w_grp, pool_scale, pool_w_out, na_w_in, na_rpb, na_w_out, conv_w_in, conv_dw, conv_db, conv_w_out, final_g, loss_target, m_c_ctx, m_norm_g, m_ada_w, m_ada_b, m_pool_w_in, m_pool_w_grp, m_pool_scale, m_pool_w_out, m_na_w_in, m_na_rpb, m_na_w_out, m_conv_w_in, m_conv_dw, m_conv_db, m_conv_w_out, m_final_g, v_c_ctx, v_norm_g, v_ada_w, v_ada_b, v_pool_w_in, v_pool_w_grp, v_pool_scale, v_pool_w_out, v_na_w_in, v_na_rpb, v_na_w_out, v_conv_w_in, v_conv_dw, v_conv_db, v_conv_w_out, v_final_g):
    xi, yi, ci = _my_place()
    me = 4 * xi + 2 * yi + ci
    seq, d = x.shape[1], x.shape[2]
    n_ctx = ctx.shape[1]
    width = d
    depth = norm_g.shape[0]
    nb = ada_w.shape[2]
    shard = width // N_DEV
    d_rows = d // LANES
    assert seq % CHUNK == 0 and n_ctx % CHUNK == 0 and (seq // GRID_W) % ROWS_PER_STEP == 0 and seq >= WIN_ROWS * GRID_W
    tr = math.gcd(math.gcd(seq, n_ctx), 256)
    x_tiles = seq // tr
    tr_lat = math.gcd(seq, 512)
    lat_tiles = seq // tr_lat

    n_pool = pool_scale.shape[0]
    n_grp = pool_w_grp.shape[1]
    grp = width // n_grp

    small_in = _pad_rows(jnp.concatenate([_rows128(c), pool_scale, conv_dw[0], conv_db], axis=0))
    got = _gather_small(small_in, "gather_inputs")
    r0 = d_rows
    c_all = got[:, :r0].reshape(N_DEV, d)
    scale_full = got[:, r0:r0 + n_pool].transpose(1, 0, 2).reshape(n_pool, width)
    r1 = r0 + n_pool
    taps_full = _pad_rows(got[:, r1:r1 + 3].transpose(1, 0, 2).reshape(3, width))
    bias_full = got[:, r1 + 3:r1 + 4].transpose(1, 0, 2).reshape(1, width)

    cond = jnp.concatenate([c_all, c_ctx[None], jnp.zeros((7, d), F32)], axis=0)
    bias_mine = lax.dynamic_slice(ada_b, (0, me * nb), (depth, nb))
    mod_mine = _mod_fwd(cond, ada_w, bias_mine)
    by_example = jnp.stack([mod_mine[:, :N_DEV].transpose(1, 0, 2),
                            jnp.broadcast_to(mod_mine[:, N_DEV][None], (N_DEV, depth, nb))], axis=2)
    mod_all = _gather_small(by_example.reshape(N_DEV, -1, LANES), "gather_mod", per_dest=True)
    mod_all = mod_all.reshape(N_DEV, depth, 2, nb).transpose(1, 2, 0, 3).reshape(depth, 2, 3, d)
    mod_all = jnp.pad(mod_all, ((0, 0), (0, 0), (0, 5), (0, 0)))
    mods = [mod_all[i] if i < 2 else mod_all[i, :1] for i in range(depth)]

    layer_weights = [[pool_w_in[0], pool_w_grp[0], pool_w_out[0]], [na_w_in[0], na_w_out[0]],
                     [conv_w_in[0], conv_w_out[0]], [pool_w_in[1], pool_w_grp[1], pool_w_out[1]]]
    slot = {(i, t): n for n, (i, t) in enumerate((i, t) for i, ws in enumerate(layer_weights) for t in range(len(ws)))}
    two_level = [(0, 0), (1, 0)]
    weights_sent = _exchange_start(
        [w.astype(BF16) for ws in layer_weights for w in ws], False, mod_all, "weights_start",
        peers=[CHIP_PEERS if key in two_level else ALL_PEERS for key in slot])
    token = weights_sent[-1]

    def landed_weight(i, t, after):
        return _exchange_wait(weights_sent, False, after, f"weights_wait{i}_{t}", which=[slot[i, t]])[0]

    def handed_on(i, after):
        half = [landed_weight(i, 0, after)]
        rest = _forward_start(half, after, f"weights_forward{i}")
        return _forward_wait(rest, rest[-1], f"weights_forward_wait{i}")[0][:, None]

    def as_in(w):
        return w[:, None]

    def as_grp(w):
        return w.transpose(1, 0, 2, 3).reshape(n_grp, grp, grp)

    def as_out(w):
        return w.reshape(width, d)

    both = [(0, seq), (seq, n_ctx)]
    latent = [(0, seq)]

    def grp_slots(g):
        return g.reshape(n_grp, N_DEV, grp // N_DEV, grp).transpose(1, 0, 2, 3).reshape(N_DEV, -1, grp).astype(BF16)

    def send_grads(i, grads):
        return _exchange_start(grads, True, jnp.zeros((8, LANES), F32), f"grads_start{i}")

    xs0 = jnp.concatenate([x[0], ctx[0]], axis=0)
    h0 = _norm_fwd(xs0, norm_g[0:1] + token[0, 0], mods[0], tr, x_tiles, "norm_fwd0")
    pool_in_w0 = handed_on(0, h0)
    pre0 = _proj_in(h0, pool_in_w0, 0, width, "proj_in0")
    pool_grp_w0 = as_grp(landed_weight(0, 1, pre0))
    z0, diff0 = _pool_fwd(pre0, pool_grp_w0, scale_full[0:1], both, "pool_fwd0")
    pool_out_w0 = as_out(landed_weight(0, 2, z0))
    yx0, xs1, h1 = _proj_out(z0, pool_out_w0, xs0, mods[0], tr, x_tiles, "proj_out0", nxt=(norm_g[1:2], mods[1]))

    na_in_w = handed_on(1, h1)
    qkv1, gpre1 = _proj_in_split(h1, na_in_w, 0, width, "proj_in1", 3)
    rpb_rows = jnp.pad(na_rpb[0], ((0, 0), (0, 2 * WIN_ROWS - na_rpb.shape[2]), (0, LANES - na_rpb.shape[3])))
    z1, o1 = _attn_fwd(qkv1, gpre1, rpb_rows, seq)
    na_out_w = as_out(landed_weight(1, 1, z1))
    yx1, x2, h2 = _proj_out(z1, na_out_w, xs1, mods[1], tr_lat, lat_tiles, "proj_out1", nxt=(norm_g[2:3], mods[2]))

    conv_in_w = as_in(landed_weight(2, 0, h2))
    pre2 = _proj_in(h2, conv_in_w, 0, width, "proj_in2")
    z2 = _conv_fwd(pre2, taps_full, bias_full, "conv_fwd")
    conv_out_w = as_out(landed_weight(2, 1, z2))
    yx2, x3, h3 = _proj_out(z2, conv_out_w, x2, mods[2], tr_lat, lat_tiles, "proj_out2", nxt=(norm_g[3:4], mods[3]))

    pool_in_w3 = as_in(landed_weight(3, 0, h3))
    pre3 = _proj_in(h3, pool_in_w3, 0, width, "proj_in3")
    pool_grp_w3 = as_grp(landed_weight(3, 1, pre3))
    z3, diff3 = _pool_fwd(pre3, pool_grp_w3, scale_full[1:2], latent, "pool_fwd3")
    pool_out_w3 = as_out(landed_weight(3, 2, z3))
    yx3, x4 = _proj_out(z3, pool_out_w3, x3, mods[3], tr_lat, lat_tiles, "proj_out3")

    loss_part, dx4, d_final, dyx3, gate3 = _loss_head(x4, loss_target[0], final_g[None], yx3, mods[3], tr_lat)

    dz3, g_pool_out1 = _proj_out_bwd(dyx3, z3, pool_out_w3, "proj_out_bwd3")
    dpre3, g_grp1, g_scale1 = _pool_bwd(dz3, diff3, pre3, pool_grp_w3, scale_full[1:2], latent, "pool_bwd3")
    dh3 = _proj_in_dh(dpre3, pool_in_w3, 0, "proj_in_dh3")
    g_pool_in1 = _grad_w_in(h3, dpre3, pool_w_in.shape[2], "grad_w_in3")
    sent3 = send_grads(3, [g_pool_in1, grp_slots(g_grp1), g_pool_out1.reshape(N_DEV, shard, d)])
    dx3, norm3, dyx2, gate2 = _norm_bwd(x3, dh3, dx4, norm_g[3:4] + sent3[-1][0, 0], mods[3], tr_lat, lat_tiles, "norm_bwd3",
                                        below=(yx2, mods[2]))

    dz2, g_conv_out = _proj_out_bwd(dyx2, z2, conv_out_w, "proj_out_bwd2")
    dpre2, g_taps, g_cbias = _conv_bwd(dz2, pre2, taps_full, bias_full, "conv_bwd")
    dh2 = _proj_in_dh(dpre2, conv_in_w, 0, "proj_in_dh2")
    g_conv_in = _grad_w_in(h2, dpre2, conv_w_in.shape[2], "grad_w_in2")
    sent2 = send_grads(2, [g_conv_in, g_conv_out.reshape(N_DEV, shard, d)])
    dx2, norm2, dyx1, gate1 = _norm_bwd(x2, dh2, dx3, norm_g[2:3] + sent2[-1][0, 0], mods[2], tr_lat, lat_tiles, "norm_bwd2",
                                        below=(yx1, mods[1][:1]))

    dz1, g_na_out = _proj_out_bwd(dyx1, z1, na_out_w, "proj_out_bwd1")
    dpre1, g_rpb = _attn_bwd(qkv1, gpre1, o1, dz1, rpb_rows, seq)
    g_rpb = g_rpb[:, :na_rpb.shape[2], :na_rpb.shape[3]]
    dh1 = _proj_in_dh(dpre1, na_in_w, 0, "proj_in_dh1")
    g_na_in = _grad_w_in(h1, dpre1, na_w_in.shape[2], "grad_w_in1")
    sent1 = send_grads(1, [g_na_in, g_na_out.reshape(N_DEV, shard, d)])
    dxs1, norm1, dyx0, gate0 = _norm_bwd(xs1, dh1, dx2, norm_g[1:2] + sent1[-1][0, 0], mods[1], tr, x_tiles, "norm_bwd1",
                                         res_tiles=x_tiles, below=(yx0, mods[0]))

    dz0, g_pool_out0 = _proj_out_bwd(dyx0, z0, pool_out_w0, "proj_out_bwd0")
    sent0a = _exchange_start([g_pool_out0.reshape(N_DEV, shard, d)], True, jnp.zeros((8, LANES), F32), "grads_start0a")
    dpre0, g_grp0, g_scale0 = _pool_bwd(dz0, diff0, pre0, pool_grp_w0, scale_full[0:1], both, "pool_bwd0")
    g_pool_in0 = _grad_w_in(h0, dpre0, pool_w_in.shape[2], "grad_w_in0")
    dh0 = _proj_in_dh(dpre0, pool_in_w0, 0, "proj_in_dh0", after=sent0a[-1])
    dx0, norm0 = _norm_bwd(xs0, dh0, dxs1, norm_g[0:1], mods[0], tr, x_tiles, "norm_bwd0", out_tiles=x_tiles)
    grad_x = dx0[None]

    norms, gates = [norm0, norm1, norm2, norm3], [gate0, gate1, gate2, gate3]
    zero_d = jnp.zeros((d,), F32)
    dm_rows = [jnp.concatenate([norms[i][0, 0], norms[i][0, 1], gates[i][0, 0]]) for i in range(depth)]
    dm_rows.append(jnp.concatenate([norm0[1, 0], norm0[1, 1], gate0[1, 0]]))
    dm_rows.append(jnp.concatenate([norm1[1, 0], norm1[1, 1], zero_d]))
    dm_local = jnp.stack(dm_rows + [jnp.zeros((3 * d,), F32)] * 2)
    g_norm_part = jnp.stack([norm0[0, 2] + norm0[1, 2], norm1[0, 2] + norm1[1, 2], norm2[0, 2], norm3[0, 2]])
    pieces = [_rows128(dm_local), _rows128(g_norm_part), _rows128(d_final[0]), _pad_rows(_rows128(g_rpb)), loss_part]
    marks = np.cumsum([0] + [p.shape[0] for p in pieces])
    by_owner = [a.reshape(-1, N_DEV, shard).transpose(1, 0, 2) for a in (g_scale0, g_scale1, g_taps[:3], g_cbias[0:1])]
    by_owner = jnp.concatenate(by_owner + [jnp.zeros((N_DEV, 8 - n_pool - 4, shard), F32)], axis=1)
    small_sent = _exchange_start([jnp.concatenate(pieces, axis=0), by_owner], [False, True], jnp.zeros((8, LANES), F32),
                                 "small_grads_start")
    sent0b = _exchange_start([g_pool_in0, grp_slots(g_grp0)], True, small_sent[-1], "grads_start0b")

    def big(parts, w, m, v, name):
        shape = w.shape
        view = (-1, shape[-1])
        parts = [(parts.reshape((N_DEV,) + w.reshape(view).shape), k) for k in range(N_DEV)]
        return [r.reshape(shape) for r in _adamw(w.reshape(view), m.reshape(view), v.reshape(view), parts, name)]

    in3, grp3, out3 = _exchange_wait(sent3, True, sent0b[-1], "grads_wait3")
    in2, out2 = _exchange_wait(sent2, True, sent0b[-1], "grads_wait2")
    in1, out1 = _exchange_wait(sent1, True, sent0b[-1], "grads_wait1")
    res = {}
    res["na_w_in"] = [r[None] for r in big(in1, na_w_in[0], m_na_w_in[0], v_na_w_in[0], "adamw_na_in")]
    res["na_w_out"] = [r[None] for r in big(out1, na_w_out[0], m_na_w_out[0], v_na_w_out[0], "adamw_na_out")]
    res["conv_w_in"] = [r[None] for r in big(in2, conv_w_in[0], m_conv_w_in[0], v_conv_w_in[0], "adamw_conv_in")]
    res["conv_w_out"] = [r[None] for r in big(out2, conv_w_out[0], m_conv_w_out[0], v_conv_w_out[0], "adamw_conv_out")]

    done = [res[n][0] for n in ("na_w_in", "na_w_out", "conv_w_in", "conv_w_out")]
    small_out, owned = _exchange_wait(small_sent, [False, True], done, "small_grads_wait")
    loss = jnp.sum(small_out[:, marks[4], 0])
    dm_all = small_out[:, :marks[1]].reshape(N_DEV, 8, 3 * d).transpose(1, 0, 2)
    dm_mine = lax.dynamic_slice(dm_all, (0, 0, me * nb), (8, N_DEV, nb))
    g_ada_w, g_ada_b, cctx_part, dsilu_cond = _mod_bwd(cond, ada_w, dm_all, dm_mine)
    cctx_sent = _exchange_start([_rows128(cctx_part[0])], False, jnp.zeros((8, LANES), F32), "cctx_start")
    res["ada_w"] = [r.reshape(ada_w.shape) for r in _adamw(
        ada_w.reshape(-1, nb), m_ada_w.reshape(-1, nb), v_ada_w.reshape(-1, nb),
        [g_ada_w.reshape(-1, nb)], "adamw_ada_w", after=cctx_sent[-1])]
    cctx_all, = _exchange_wait(cctx_sent, False, res["ada_w"][0], "cctx_wait")

    def summed(ref, lo, hi):
        g = ref[0, lo:hi, :]
        for k in range(1, N_DEV):
            g = g + ref[k, lo:hi, :]
        return g

    makers = [
        lambda so, ow, cc, ab, ds: summed(cc, 0, d_rows) * ds[...],
        lambda so, ow, cc, ab, ds: summed(so, marks[1], marks[2]),
        lambda so, ow, cc, ab, ds: ab[...],
        lambda so, ow, cc, ab, ds: summed(so, marks[2], marks[3]),
        lambda so, ow, cc, ab, ds: summed(so, marks[3], marks[4]),
        lambda so, ow, cc, ab, ds: summed(ow, 0, n_pool),
        lambda so, ow, cc, ab, ds: summed(ow, n_pool, n_pool + 3),
        lambda so, ow, cc, ab, ds: summed(ow, n_pool + 3, n_pool + 4),
    ]
    rpb_rows128 = lambda a: _pad_rows(_rows128(a))
    views = [_rows128] * 4 + [rpb_rows128] + [lambda a: a.reshape(-1, LANES)] * 3
    small = [(c_ctx, m_c_ctx, v_c_ctx), (norm_g, m_norm_g, v_norm_g), (ada_b, m_ada_b, v_ada_b),
             (final_g, m_final_g, v_final_g), (na_rpb, m_na_rpb, v_na_rpb), (pool_scale, m_pool_scale, v_pool_scale),
             (conv_dw, m_conv_dw, v_conv_dw), (conv_db, m_conv_db, v_conv_db)]
    states = [tuple(view(a) for a in triple) for view, triple in zip(views, small)]
    sources = (small_out, owned, cctx_all, _rows128(g_ada_b), _rows128(dsilu_cond[8]))
    small_res = _adamw_small(states, (sources, makers))
    names = ["c_ctx", "norm_g", "ada_b", "final_g", "na_rpb", "pool_scale", "conv_dw", "conv_db"]
    for name, (w, _, _), outs4 in zip(names, small, small_res):
        res[name] = [r.reshape(-1)[:w.size].reshape(w.shape) for r in outs4]

    out0, = _exchange_wait(sent0a, True, small_res[0][0], "grads_wait0a")
    in0, grp0 = _exchange_wait(sent0b, True, small_res[0][0], "grads_wait0b")
    def both_layers(first, second, w, m, v, name):
        view = (w.shape[0], -1, w.shape[-1])
        landed = [first.reshape((N_DEV,) + w.reshape(view).shape[1:]), second.reshape((N_DEV,) + w.reshape(view).shape[1:])]
        return [r.reshape(w.shape) for r in _adamw_layers(w.reshape(view), m.reshape(view), v.reshape(view), landed, name)]

    res["pool_w_in"] = both_layers(in0, in3, pool_w_in, m_pool_w_in, v_pool_w_in, "adamw_pool_in")
    res["pool_w_grp"] = both_layers(grp0, grp3, pool_w_grp, m_pool_w_grp, v_pool_w_grp, "adamw_pool_grp")
    res["pool_w_out"] = both_layers(out0, out3, pool_w_out, m_pool_w_out, v_pool_w_out, "adamw_pool_out")

    order = ["c_ctx", "norm_g", "ada_w", "ada_b", "pool_w_in", "pool_w_grp", "pool_scale", "pool_w_out", "na_w_in",
             "na_rpb", "na_w_out", "conv_w_in", "conv_dw", "conv_db", "conv_w_out", "final_g"]
    outs = [loss, grad_x]
    for j in range(4):
        outs += [res[n][j] for n in order]
    return tuple(outs)
```

```python
import math

import numpy as np
import jax
import jax.numpy as jnp
from jax import lax
from jax.experimental import pallas as pl
from jax.experimental.pallas import tpu as pltpu

F32 = jnp.float32
BF16 = jnp.bfloat16
N_DEV = 8
LANES = 128
RMS_EPS = 1e-6
GRID_W = 64
WIN_ROWS = 8
WIN_COLS = 16
HEAD_DIM = 64
POOL_WINDOWS = (2, 4, 8, 16)
HALO = 8
CHUNK = 256
MASKED = -1e30
ADAM_LR = 0.001
ADAM_B1 = 0.9
ADAM_B2 = 0.999
ADAM_EPS = 1e-08
ADAM_WD = 0.01
ADAM_STEP = 10
VMEM_LIMIT = 56 * 1024 * 1024
ADAM_TILE_ELEMS = 256 * 1024
MESH = pl.DeviceIdType.MESH
ANY = pl.BlockSpec(memory_space=pl.ANY)
HBM = pl.BlockSpec(memory_space=pltpu.HBM)
SEM = pl.BlockSpec(memory_space=pltpu.SEMAPHORE)
EFFECT = pltpu.SideEffectType.DATAFLOW_SIDE_EFFECTING


def _pc(body, *, name, **kw):
    return pl.pallas_call(body, name=name, **kw)


def _params(*sem):
    return pltpu.CompilerParams(dimension_semantics=sem if sem else None, vmem_limit_bytes=VMEM_LIMIT)


def _dot(a, b, ca=1, cb=0, precision=None):
    return lax.dot_general(a, b, (((ca,), (cb,)), ((), ())), preferred_element_type=F32, precision=precision)


def _tile(n, pref, unit=LANES):
    best = None
    for t in range(unit, min(n, pref) + 1, unit):
        if n % t == 0:
            best = t
    return best if best is not None else n


def _sigmoid(x):
    return 1.0 / (1.0 + jnp.exp(-x))


def _silu(x):
    return x * _sigmoid(x)


def _dsilu(x):
    s = _sigmoid(x)
    return s * (1.0 + x * (1.0 - s))


def _my_place():
    return lax.axis_index("x"), lax.axis_index("y"), lax.axis_index("c")


def _flip(v, f):
    return 1 - v if f else v


def _gather_small(block, name, per_dest=False):
    rows, cols = block.shape[-2:]

    def body(x_ref, out_ref, send_sems, recv_sems):
        x, y, c = _my_place()
        me = 4 * x + 2 * y + c
        out_ref[me] = x_ref[me] if per_dest else x_ref[...]
        copies = []
        for k in range(1, N_DEV):
            peer = (_flip(x, k & 4), _flip(y, k & 2), _flip(c, k & 1))
            dest = 4 * peer[0] + 2 * peer[1] + peer[2]
            cp = pltpu.make_async_remote_copy(
                src_ref=x_ref.at[dest] if per_dest else x_ref, dst_ref=out_ref.at[me],
                send_sem=send_sems.at[k - 1], recv_sem=recv_sems.at[k - 1], device_id=peer, device_id_type=MESH)
            cp.start()
            copies.append(cp)
        for cp in copies:
            cp.wait()

    return _pc(
        body, name=name,
        out_shape=jax.ShapeDtypeStruct((N_DEV, rows, cols), block.dtype),
        in_specs=[pl.BlockSpec(memory_space=pltpu.VMEM)],
        out_specs=pl.BlockSpec(memory_space=pltpu.VMEM),
        scratch_shapes=[pltpu.SemaphoreType.DMA((N_DEV - 1,)), pltpu.SemaphoreType.DMA((N_DEV - 1,))],
    )(block)


ALL_PEERS = tuple(range(N_DEV))
CHIP_PEERS = (0, 1, 2, 4, 6)
OTHER_CHIPS = (2, 4, 6)


def _flag(per_dest, t):
    return per_dest[t] if isinstance(per_dest, (list, tuple)) else per_dest


def _peer(k):
    x, y, c = _my_place()
    peer = (_flip(x, k & 4), _flip(y, k & 2), _flip(c, k & 1))
    return peer, 4 * peer[0] + 2 * peer[1] + peer[2]


def _peer_lists(peers, nt):
    return list(peers) if isinstance(peers, list) else [peers] * nt


def _exchange_copies(srcs, lands, send_sems, recv_sems, per_dest, peers=ALL_PEERS):
    x, y, c = _my_place()
    me = 4 * x + 2 * y + c
    copies = []
    for t, (src, land, ks) in enumerate(zip(srcs, lands, _peer_lists(peers, len(srcs)))):
        for k in ks:
            peer, dest = _peer(k)
            s = len(copies)
            copies.append(pltpu.make_async_remote_copy(
                src_ref=src.at[dest] if _flag(per_dest, t) else src, dst_ref=land.at[me],
                send_sem=send_sems[s], recv_sem=recv_sems[s], device_id=peer, device_id_type=MESH))
    return copies


def _forward_copies(lands, send_sems, recv_sems):
    sibling, _ = _peer(1)
    copies = []
    for t, land in enumerate(lands):
        for n, k in enumerate(OTHER_CHIPS):
            _, slot = _peer(k)
            s = t * len(OTHER_CHIPS) + n
            copies.append(pltpu.make_async_remote_copy(
                src_ref=land.at[slot], dst_ref=land.at[slot], send_sem=send_sems[s], recv_sem=recv_sems[s],
                device_id=sibling, device_id_type=MESH))
    return copies


def _forward_start(lands, after, name):
    nt = len(lands)
    ns = nt * len(OTHER_CHIPS)

    def body(*refs):
        ins, outs = refs[:nt + 1], refs[nt + 1:]
        for cp in _forward_copies(ins[:nt], outs[:ns], outs[ns:2 * ns]):
            cp.start()
        outs[-1][...] = jnp.zeros_like(outs[-1])

    res = _pc(
        body, name=name,
        out_shape=(*[pltpu.SemaphoreType.DMA(())] * (2 * ns), *[pltpu.HBM(a.shape, a.dtype) for a in lands],
                   jax.ShapeDtypeStruct((8, LANES), F32)),
        in_specs=[HBM] * nt + [ANY],
        out_specs=(*[SEM] * (2 * ns), *[HBM] * nt, pl.BlockSpec(memory_space=pltpu.VMEM)),
        input_output_aliases={i: 2 * ns + i for i in range(nt)},
        compiler_params=pltpu.CompilerParams(has_side_effects=EFFECT),
    )(*lands, after)
    return list(res[:ns]), list(res[ns:2 * ns]), list(res[2 * ns:2 * ns + nt]), res[-1]


def _forward_wait(state, after, name):
    send_sems, recv_sems, lands, _ = state
    nt, ns = len(lands), len(send_sems)

    def body(*refs):
        sems = refs[nt:nt + 2 * ns]
        for cp in _forward_copies(refs[:nt], sems[:ns], sems[ns:]):
            cp.wait_send()
            cp.wait_recv()

    res = _pc(
        body, name=name,
        out_shape=tuple(pltpu.HBM(a.shape, a.dtype) for a in lands),
        in_specs=[HBM] * nt + [SEM] * (2 * ns) + [ANY],
        out_specs=tuple([HBM] * nt),
        input_output_aliases={i: i for i in range(nt)},
        compiler_params=pltpu.CompilerParams(has_side_effects=EFFECT),
    )(*lands, *send_sems, *recv_sems, after)
    return list(res)


def _exchange_start(srcs, per_dest, after, name, peers=ALL_PEERS):
    nt = len(srcs)
    peers = _peer_lists(peers, nt)
    ns = sum(len(ks) for ks in peers)
    lands = [lax.empty((N_DEV,) + (s.shape[1:] if _flag(per_dest, t) else s.shape), s.dtype) for t, s in enumerate(srcs)]

    def body(*refs):
        ins, outs = refs[:2 * nt + 1], refs[2 * nt + 1:]
        for cp in _exchange_copies(ins[:nt], ins[nt:2 * nt], outs[:ns], outs[ns:2 * ns], per_dest, peers):
            cp.start()
        outs[-1][...] = jnp.zeros_like(outs[-1])

    hbm = [pltpu.with_memory_space_constraint(a, pltpu.HBM) for a in list(srcs) + lands]
    res = _pc(
        body, name=name,
        out_shape=(*[pltpu.SemaphoreType.DMA(())] * (2 * ns), *[pltpu.HBM(a.shape, a.dtype) for a in hbm],
                   jax.ShapeDtypeStruct((8, LANES), F32)),
        in_specs=[HBM] * (2 * nt) + [ANY],
        out_specs=(*[SEM] * (2 * ns), *[HBM] * (2 * nt), pl.BlockSpec(memory_space=pltpu.VMEM)),
        input_output_aliases={i: 2 * ns + i for i in range(2 * nt)},
        compiler_params=pltpu.CompilerParams(has_side_effects=EFFECT),
    )(*hbm, after)
    sems, rest = res[:2 * ns], res[2 * ns:]
    return list(sems[:ns]), list(sems[ns:]), list(rest[:nt]), list(rest[nt:2 * nt]), peers, rest[-1]


def _exchange_wait(state, per_dest, after, name, which=None):
    send_sems, recv_sems, srcs, lands, peers, _ = state
    which = list(range(len(srcs))) if which is None else which
    per_dest = [_flag(per_dest, t) for t in which]
    after = list(after) if isinstance(after, (list, tuple)) else [after]
    first = np.cumsum([0] + [len(ks) for ks in peers])
    pick = [first[t] + n for t in which for n in range(len(peers[t]))]
    peers = [peers[t] for t in which]
    send_sems, recv_sems = [send_sems[s] for s in pick], [recv_sems[s] for s in pick]
    srcs, lands = [srcs[t] for t in which], [lands[t] for t in which]
    nt = len(srcs)
    ns = len(send_sems)

    def body(*refs):
        sems = refs[2 * nt:2 * nt + 2 * ns]
        for cp in _exchange_copies(refs[:nt], refs[nt:2 * nt], sems[:ns], sems[ns:], per_dest, peers):
            cp.wait_send()
            cp.wait_recv()

    thru = list(srcs) + list(lands)
    res = _pc(
        body, name=name,
        out_shape=tuple(pltpu.HBM(a.shape, a.dtype) for a in thru),
        in_specs=[HBM] * (2 * nt) + [SEM] * (2 * ns) + [ANY] * len(after),
        out_specs=tuple([HBM] * (2 * nt)),
        input_output_aliases={i: i for i in range(2 * nt)},
        compiler_params=pltpu.CompilerParams(has_side_effects=EFFECT),
    )(*thru, *send_sems, *recv_sems, *after)
    return list(res[nt:])


def _mod_fwd(cond, ada_w, bias):
    depth, d, nb = ada_w.shape

    def body(c_ref, w_ref, b_ref, o_ref):
        s = _silu(c_ref[...]).astype(BF16)
        o_ref[...] = _dot(s, w_ref[...].astype(BF16)) + b_ref[...]

    return _pc(
        body, name="mod_fwd", grid=(depth,),
        in_specs=[pl.BlockSpec((16, d), lambda i: (0, 0)), pl.BlockSpec((None, d, nb), lambda i: (i, 0, 0)),
                  pl.BlockSpec((None, 1, nb), lambda i: (i, 0, 0))],
        out_specs=pl.BlockSpec((None, 16, nb), lambda i: (i, 0, 0)),
        out_shape=jax.ShapeDtypeStruct((depth, 16, nb), F32),
        compiler_params=_params("parallel"),
    )(cond, ada_w, bias.reshape(depth, 1, nb))


def _mod_bwd(cond, ada_w, dm_all, dm_mine):
    depth, d, nb = ada_w.shape
    d3 = dm_all.shape[-1]

    def body(c_ref, w_ref, all_ref, call_ref, mine_ref, cmine_ref, gw_ref, gb_ref, part_ref, ds_ref):
        i = pl.program_id(0)
        cond_v = c_ref[...]
        s = _silu(cond_v).astype(BF16)
        has_ctx = jnp.where(i < 2, 1.0, 0.0)
        tot_all = jnp.sum(call_ref[...], axis=0, keepdims=True) * has_ctx
        tot_mine = jnp.broadcast_to(jnp.sum(cmine_ref[...], axis=0, keepdims=True) * has_ctx, (8, nb)).astype(BF16)
        gb_ref[...] = jnp.sum(all_ref[...], axis=0, keepdims=True) + tot_all
        gw_ref[...] = _dot(s[0:8], mine_ref[...].astype(BF16), 0, 0) + _dot(s[8:16], tot_mine, 0, 0)
        part = _dot(tot_mine, w_ref[...].astype(BF16), 1, 1)

        @pl.when(i == 0)
        def _():
            part_ref[...] = jnp.zeros_like(part_ref)
            ds_ref[...] = _dsilu(cond_v)

        part_ref[...] += part

    def rows(width, which):
        return pl.BlockSpec((None, N_DEV, width), which)

    layer = lambda i: (i, 0, 0)
    ctx_layer = lambda i: (jnp.minimum(i, 1) + 4, 0, 0)
    return _pc(
        body, name="mod_bwd", grid=(depth,),
        in_specs=[pl.BlockSpec((16, d), lambda i: (0, 0)), pl.BlockSpec((None, d, nb), layer),
                  rows(d3, layer), rows(d3, ctx_layer), rows(nb, layer), rows(nb, ctx_layer)],
        out_specs=[pl.BlockSpec((None, d, nb), layer), pl.BlockSpec((None, 1, d3), layer),
                   pl.BlockSpec((8, d), lambda i: (0, 0)), pl.BlockSpec((16, d), lambda i: (0, 0))],
        out_shape=[jax.ShapeDtypeStruct((depth, d, nb), F32), jax.ShapeDtypeStruct((depth, 1, d3), F32),
                   jax.ShapeDtypeStruct((8, d), F32), jax.ShapeDtypeStruct((16, d), F32)],
        compiler_params=_params("arbitrary"),
    )(cond, ada_w, dm_all, dm_all, dm_mine, dm_mine)


def _norm_fwd(xs, g, mod, tr, seg_tiles, name):
    t, d = xs.shape

    def body(x_ref, g_ref, mod_ref, h_ref):
        x = x_ref[...]
        r = lax.rsqrt(jnp.mean(x * x, axis=-1, keepdims=True) + RMS_EPS)
        y = (x * r) * g_ref[...]
        h_ref[...] = (y * (1.0 + mod_ref[1:2, :]) + mod_ref[0:1, :]).astype(BF16)

    return _pc(
        body, name=name, grid=(t // tr,),
        in_specs=[pl.BlockSpec((tr, d), lambda i: (i, 0)), pl.BlockSpec((1, d), lambda i: (0, 0)),
                  pl.BlockSpec((None, 8, d), lambda i: (i // seg_tiles, 0, 0))],
        out_specs=pl.BlockSpec((tr, d), lambda i: (i, 0)),
        out_shape=jax.ShapeDtypeStruct((t, d), BF16),
        compiler_params=_params("parallel"),
    )(xs, g, mod)


def _resid_grad(dx, i, seg_tiles, yx_ref, gate_ref, dyx_ref, gsum_ref):
    dyx_ref[...] = (dx * gate_ref[2:3, :]).astype(BF16)

    @pl.when(i % seg_tiles == 0)
    def _():
        gsum_ref[...] = jnp.zeros_like(gsum_ref)

    gsum_ref[0:1, :] += jnp.sum(dx * yx_ref[...], axis=0, keepdims=True)


def _norm_bwd(xs, dh, dres, g, mod, tr, seg_tiles, name, res_tiles=None, out_tiles=None, below=None):
    t, d = xs.shape
    n_tiles = t // tr
    res_tiles = n_tiles if res_tiles is None else res_tiles
    out_tiles = n_tiles if out_tiles is None else out_tiles

    def body(x_ref, dh_ref, dres_ref, g_ref, mod_ref, *rest):
        i = pl.program_id(0)
        x = x_ref[...]
        r = lax.rsqrt(jnp.mean(x * x, axis=-1, keepdims=True) + RMS_EPS)
        xn = x * r
        dhv = dh_ref[...]
        gain = g_ref[...]
        one_scale = 1.0 + mod_ref[1:2, :]
        dxn = dhv * (gain * one_scale)
        dx = r * (dxn - xn * jnp.mean(dxn * xn, axis=-1, keepdims=True))
        if res_tiles == n_tiles:
            dx = dx + dres_ref[...]
        else:
            dx = dx + jnp.where(i < res_tiles, dres_ref[...], 0.0)
        if below is None:
            dx_ref, sum_ref = rest
        else:
            yx_ref, gate_ref, dx_ref, sum_ref, dyx_ref, gsum_ref = rest
            _resid_grad(dx, i, seg_tiles, yx_ref, gate_ref, dyx_ref, gsum_ref)
        if out_tiles == n_tiles:
            dx_ref[...] = dx
        else:
            @pl.when(i < out_tiles)
            def _():
                dx_ref[...] = dx

        @pl.when(i % seg_tiles == 0)
        def _():
            sum_ref[...] = jnp.zeros_like(sum_ref)

        sum_ref[0:1, :] += jnp.sum(dhv, axis=0, keepdims=True)
        sum_ref[1:2, :] += jnp.sum(dhv * (xn * gain), axis=0, keepdims=True)
        sum_ref[2:3, :] += jnp.sum(dhv * one_scale * xn, axis=0, keepdims=True)

    row = pl.BlockSpec((tr, d), lambda i: (i, 0))
    seg = pl.BlockSpec((None, 8, d), lambda i: (i // seg_tiles, 0, 0))
    in_specs = [row, row, pl.BlockSpec((tr, d), lambda i: (jnp.minimum(i, res_tiles - 1), 0)),
                pl.BlockSpec((1, d), lambda i: (0, 0)), seg]
    out_specs = [pl.BlockSpec((tr, d), lambda i: (jnp.minimum(i, out_tiles - 1), 0)), seg]
    out_shape = [jax.ShapeDtypeStruct((out_tiles * tr, d), F32), jax.ShapeDtypeStruct((mod.shape[0], 8, d), F32)]
    args = [xs, dh, dres, g, mod]
    if below is not None:
        in_specs += [row, seg]
        out_specs += [row, seg]
        out_shape += [jax.ShapeDtypeStruct((t, d), BF16), jax.ShapeDtypeStruct((below[1].shape[0], 8, d), F32)]
        args += list(below)
    return _pc(
        body, name=name, grid=(n_tiles,), in_specs=in_specs, out_specs=out_specs, out_shape=out_shape,
        compiler_params=_params("arbitrary"),
    )(*args)


def _loss_head(xs, target, g, yx, mod, tr):
    t, d = xs.shape

    def body(x_ref, t_ref, g_ref, yx_ref, gate_ref, loss_ref, dx_ref, dg_ref, dyx_ref, gsum_ref):
        i = pl.program_id(0)
        x = x_ref[...]
        r = lax.rsqrt(jnp.mean(x * x, axis=-1, keepdims=True) + RMS_EPS)
        xn = x * r
        gain = g_ref[...]
        err = xn * gain - t_ref[...]
        dy = err * (1.0 / d)
        dxn = dy * gain
        dx = r * (dxn - xn * jnp.mean(dxn * xn, axis=-1, keepdims=True))
        dx_ref[...] = dx
        _resid_grad(dx, i, t // tr, yx_ref, gate_ref, dyx_ref, gsum_ref)

        @pl.when(i == 0)
        def _():
            loss_ref[...] = jnp.zeros_like(loss_ref)
            dg_ref[...] = jnp.zeros_like(dg_ref)

        loss_ref[...] += 0.5 * jnp.sum(jnp.mean(err * err, axis=-1, keepdims=True))
        dg_ref[0:1, :] += jnp.sum(dy * xn, axis=0, keepdims=True)

    row = pl.BlockSpec((tr, d), lambda i: (i, 0))
    seg = pl.BlockSpec((None, 8, d), lambda i: (0, 0, 0))
    return _pc(
        body, name="loss_head", grid=(t // tr,),
        in_specs=[row, row, pl.BlockSpec((1, d), lambda i: (0, 0)), row, seg],
        out_specs=[pl.BlockSpec((8, LANES), lambda i: (0, 0)), row, pl.BlockSpec((8, d), lambda i: (0, 0)), row, seg],
        out_shape=[jax.ShapeDtypeStruct((8, LANES), F32), jax.ShapeDtypeStruct((t, d), F32),
                   jax.ShapeDtypeStruct((8, d), F32), jax.ShapeDtypeStruct((t, d), BF16),
                   jax.ShapeDtypeStruct((1, 8, d), F32)],
        compiler_params=_params("arbitrary"),
    )(xs, target, g, yx, mod)


def _proj_in(h, w, layer, width, name, blocks=None, dtype=F32):
    t, d = h.shape
    n8 = w.shape[-1]
    first, count = blocks if blocks is not None else (0, N_DEV)
    per_part = width // n8
    assert first % per_part == 0 and count % per_part == 0
    tm = _tile(t, 1152)

    def body(a_ref, b_ref, o_ref):
        a = a_ref[...]
        for s in range(per_part):
            o_ref[:, s * n8:(s + 1) * n8] = _dot(a, b_ref[s]).astype(dtype)

    return _pc(
        body, name=name, grid=(t // tm, count // per_part),
        in_specs=[pl.BlockSpec((tm, d), lambda i, j: (i, 0)),
                  pl.BlockSpec((per_part, None, d, n8), lambda i, j: (first // per_part + j, layer, 0, 0))],
        out_specs=pl.BlockSpec((None, tm, width), lambda i, j: (j, i, 0)),
        out_shape=jax.ShapeDtypeStruct((count // per_part, t, width), dtype),
        compiler_params=_params("parallel", "parallel"),
    )(h, w)


def _proj_in_split(h, w, layer, width, name, narrow):
    t, d = h.shape
    n8 = w.shape[-1]
    per_part = width // n8
    parts = N_DEV // per_part
    tm = _tile(t, 1152)

    def body(a_ref, b_ref, lo_ref, hi_ref):
        j = pl.program_id(1)
        a = a_ref[...]
        cols = [_dot(a, b_ref[s]) for s in range(per_part)]

        @pl.when(j < narrow)
        def _():
            for s in range(per_part):
                lo_ref[:, s * n8:(s + 1) * n8] = cols[s].astype(BF16)

        @pl.when(j >= narrow)
        def _():
            for s in range(per_part):
                hi_ref[:, s * n8:(s + 1) * n8] = cols[s]

    return _pc(
        body, name=name, grid=(t // tm, parts),
        in_specs=[pl.BlockSpec((tm, d), lambda i, j: (i, 0)),
                  pl.BlockSpec((per_part, None, d, n8), lambda i, j: (j, layer, 0, 0))],
        out_specs=[pl.BlockSpec((None, tm, width), lambda i, j: (jnp.minimum(j, narrow - 1), i, 0)),
                   pl.BlockSpec((None, tm, width), lambda i, j: (jnp.maximum(j - narrow, 0), i, 0))],
        out_shape=[jax.ShapeDtypeStruct((narrow, t, width), BF16), jax.ShapeDtypeStruct((parts - narrow, t, width), F32)],
        compiler_params=_params("parallel", "arbitrary"),
    )(h, w)


def _proj_out(z, w, res, mod, tm, seg_tiles, name, nxt=None):
    t, k = z.shape
    d = w.shape[1]

    def body(z_ref, w_ref, res_ref, mod_ref, *rest):
        yx = _dot(z_ref[...], w_ref[...])
        x = res_ref[...] + mod_ref[2:3, :] * yx
        if nxt is None:
            yx_ref, x_ref = rest
        else:
            g_ref, nmod_ref, yx_ref, x_ref, h_ref = rest
            r = lax.rsqrt(jnp.mean(x * x, axis=-1, keepdims=True) + RMS_EPS)
            h_ref[...] = (((x * r) * g_ref[...]) * (1.0 + nmod_ref[1:2, :]) + nmod_ref[0:1, :]).astype(BF16)
        yx_ref[...] = yx
        x_ref[...] = x

    tile = pl.BlockSpec((tm, d), lambda i: (i, 0))
    seg = pl.BlockSpec((None, 8, d), lambda i: (i // seg_tiles, 0, 0))
    in_specs = [pl.BlockSpec((tm, k), lambda i: (i, 0)), pl.BlockSpec((k, d), lambda i: (0, 0)), tile, seg]
    out_specs = [tile, tile]
    out_shape = [jax.ShapeDtypeStruct((t, d), F32), jax.ShapeDtypeStruct((t, d), F32)]
    args = [z, w, res, mod]
    if nxt is not None:
        in_specs += [pl.BlockSpec((1, d), lambda i: (0, 0)), seg]
        out_specs.append(tile)
        out_shape.append(jax.ShapeDtypeStruct((t, d), BF16))
        args += list(nxt)
    return _pc(
        body, name=name, grid=(t // tm,), in_specs=in_specs, out_specs=out_specs, out_shape=out_shape,
        compiler_params=_params("parallel"),
    )(*args)


def _proj_out_bwd(dyx, z, w, name):
    t, d = dyx.shape
    width = w.shape[0]
    tm = _tile(t, 768)
    nk = t // tm

    def body(a_ref, z_ref, w_ref, dz_ref, dw_ref, acc_ref):
        k = pl.program_id(0)

        @pl.when(k == 0)
        def _():
            acc_ref[...] = jnp.zeros_like(acc_ref)

        a = a_ref[...]
        dz_ref[...] = _dot(a, w_ref[...], 1, 1)
        acc_ref[...] += _dot(z_ref[...].T, a)

        @pl.when(k == nk - 1)
        def _():
            dw_ref[...] = acc_ref[...].astype(BF16)

    return _pc(
        body, name=name, grid=(nk,),
        in_specs=[pl.BlockSpec((tm, d), lambda k: (k, 0)), pl.BlockSpec((tm, width), lambda k: (k, 0)),
                  pl.BlockSpec((width, d), lambda k: (0, 0))],
        out_specs=[pl.BlockSpec((tm, width), lambda k: (k, 0)), pl.BlockSpec((width, d), lambda k: (0, 0))],
        out_shape=[jax.ShapeDtypeStruct((t, width), F32), jax.ShapeDtypeStruct((width, d), BF16)],
        scratch_shapes=[pltpu.VMEM((width, d), F32)],
        compiler_params=_params("arbitrary"),
    )(dyx, z, w)


def _proj_in_dh(dpre, w, layer, name, after=None):
    parts, t, width = dpre.shape
    d, n8 = w.shape[-2:]
    per_part = width // n8
    tm, tn = _tile(t, 768), _tile(d, 512)

    def body(a_ref, w_ref, *rest):
        o_ref = rest[-1]
        acc = None
        for p in range(parts):
            for s in range(per_part):
                term = _dot(a_ref[p, :, s * n8:(s + 1) * n8], w_ref[p * per_part + s], 1, 1)
                acc = term if acc is None else acc + term
        o_ref[...] = acc

    extra = [] if after is None else [after]
    return _pc(
        body, name=name, grid=(t // tm, d // tn),
        in_specs=[pl.BlockSpec((parts, tm, width), lambda i, j: (0, i, 0)),
                  pl.BlockSpec((N_DEV, None, tn, n8), lambda i, j: (0, layer, j, 0))] + [ANY] * len(extra),
        out_specs=pl.BlockSpec((tm, tn), lambda i, j: (i, j)),
        out_shape=jax.ShapeDtypeStruct((t, d), F32),
        compiler_params=_params("parallel", "parallel"),
    )(dpre, w, *extra)


def _transposed(a_ref):
    return a_ref[...].T


def _grad_w_in(h, dpre, n8, name):
    t, d = h.shape
    parts, _, width = dpre.shape
    per_part = width // n8
    tm, tk = _tile(d, 512), _tile(t, 1152)
    nk = t // tk

    def body(a_ref, b_ref, o_ref, acc_ref):
        k = pl.program_id(1)

        @pl.when(k == 0)
        def _():
            acc_ref[...] = jnp.zeros_like(acc_ref)

        at = _transposed(a_ref)
        for p in range(parts):
            r = _dot(at, b_ref[p])
            for s in range(per_part):
                acc_ref[p * per_part + s] += r[:, s * n8:(s + 1) * n8]

        @pl.when(k == nk - 1)
        def _():
            o_ref[...] = acc_ref[...].astype(BF16)

    return _pc(
        body, name=name, grid=(d // tm, nk),
        in_specs=[pl.BlockSpec((tk, tm), lambda i, k: (k, i)), pl.BlockSpec((parts, tk, width), lambda i, k: (0, k, 0))],
        out_specs=pl.BlockSpec((parts * per_part, tm, n8), lambda i, k: (0, i, 0)),
        out_shape=jax.ShapeDtypeStruct((parts * per_part, d, n8), BF16),
        scratch_shapes=[pltpu.VMEM((parts * per_part, tm, n8), F32)],
        compiler_params=_params("parallel", "arbitrary"),
    )(h, dpre)


def _shift(v, k):
    n = v.shape[0]
    return pltpu.roll(v, k % n, 0)


def _window_sum(v, win):
    s = v + _shift(v, 1)
    step = 1
    while 2 * step < win:
        s = _shift(s, step) + _shift(s, -step)
        step *= 2
    return s


def _window_count(base, seg_len, win, shape):
    t = base + lax.broadcasted_iota(jnp.int32, shape, 0)
    hi = jnp.minimum(t + win // 2, seg_len)
    lo = jnp.maximum(t - win // 2, 0)
    return (hi - lo).astype(F32)


def _pad_offsets(segs):
    return [HALO * (s + 1) + st for s, (st, _) in enumerate(segs)]


def _for_chunks(segs, fn):
    offs = _pad_offsets(segs)
    for s, (st, ln) in enumerate(segs):
        def step(ci, carry, s=s, st=st, ln=ln):
            fn(s, st, ln, offs[s], pl.multiple_of(ci * CHUNK, CHUNK))
            return carry
        lax.fori_loop(0, ln // CHUNK, step, 0)


def _pool_fwd(pre, w_grp, scale, segs, name):
    _, t, width = pre.shape
    grp = width // len(POOL_WINDOWS)
    padded = t + HALO * (len(segs) + 1)

    def group(win, pre_ref, w_ref, sc_ref, z_ref, diff_ref, pad_ref):
        pad_ref[...] = jnp.zeros_like(pad_ref)

        def fill(s, st, ln, off, b):
            pad_ref[pl.ds(off + b, CHUNK), :] = pre_ref[0, pl.ds(st + b, CHUNK), :]

        _for_chunks(segs, fill)

        def mix(s, st, ln, off, b):
            ext = pad_ref[pl.ds(off - HALO + b, CHUNK + 2 * HALO), :]
            total = _window_sum(ext, win)[HALO:HALO + CHUNK]
            u = pre_ref[0, pl.ds(st + b, CHUNK), :]
            diff = (total / _window_count(b, ln, win, u.shape) - u).astype(BF16)
            mixed = _dot(diff, w_ref[...])
            gate = _silu(pre_ref[1, pl.ds(st + b, CHUNK), :])
            z_ref[pl.ds(st + b, CHUNK), :] = (mixed * sc_ref[...] * gate).astype(BF16)
            diff_ref[pl.ds(st + b, CHUNK), :] = diff

        _for_chunks(segs, mix)

    def body(pre_ref, w_ref, sc_ref, z_ref, diff_ref, pad_ref):
        gi = pl.program_id(0)
        for widx, win in enumerate(POOL_WINDOWS):
            @pl.when(gi == widx)
            def _(win=win):
                group(win, pre_ref, w_ref, sc_ref, z_ref, diff_ref, pad_ref)

    col = pl.BlockSpec((t, grp), lambda g: (0, g))
    return _pc(
        body, name=name, grid=(len(POOL_WINDOWS),),
        in_specs=[pl.BlockSpec((2, t, grp), lambda g: (0, 0, g)), pl.BlockSpec((None, grp, grp), lambda g: (g, 0, 0)),
                  pl.BlockSpec((1, grp), lambda g: (0, g))],
        out_specs=[col, col],
        out_shape=[jax.ShapeDtypeStruct((t, width), BF16), jax.ShapeDtypeStruct((t, width), BF16)],
        scratch_shapes=[pltpu.VMEM((padded, grp), F32)],
        compiler_params=_params("parallel"),
    )(pre, w_grp, scale)


def _pool_bwd(dz, diff, pre, w_grp, scale, segs, name):
    _, t, width = pre.shape
    grp = width // len(POOL_WINDOWS)
    padded = t + HALO * (len(segs) + 1)

    def group(win, dz_ref, diff_ref, pre_ref, w_ref, sc_ref, dpre_ref, dw_ref, dsc_ref, pad_ref, dd_ref):
        pad_ref[...] = jnp.zeros_like(pad_ref)
        dw_ref[...] = jnp.zeros_like(dw_ref)
        dsc_ref[...] = jnp.zeros_like(dsc_ref)

        def first(s, st, ln, off, b):
            rows = pl.ds(st + b, CHUNK)
            diff_v = diff_ref[rows, :]
            mixed = _dot(diff_v, w_ref[...])
            g = pre_ref[1, rows, :]
            sg = _silu(g)
            dzv = dz_ref[rows, :]
            dmixed = (dzv * sc_ref[...] * sg).astype(BF16)
            dsc_ref[...] += jnp.sum(dzv * mixed * sg, axis=0, keepdims=True)
            dpre_ref[1, rows, :] = (dzv * mixed * sc_ref[...] * _dsilu(g)).astype(BF16)
            ddiff = _dot(dmixed, w_ref[...], 1, 1)
            dw_ref[...] += _dot(diff_v, dmixed, 0, 0)
            dd_ref[rows, :] = ddiff
            pad_ref[pl.ds(off + b, CHUNK), :] = ddiff / _window_count(b, ln, win, ddiff.shape)

        _for_chunks(segs, first)

        def second(s, st, ln, off, b):
            rows = pl.ds(st + b, CHUNK)
            ext = pad_ref[pl.ds(off - HALO + b, CHUNK + 2 * HALO), :]
            total = _shift(_window_sum(ext, win), -1)[HALO:HALO + CHUNK]
            dpre_ref[0, rows, :] = (total - dd_ref[rows, :]).astype(BF16)

        _for_chunks(segs, second)

    def body(dz_ref, diff_ref, pre_ref, w_ref, sc_ref, dpre_ref, dw_ref, dsc_ref, pad_ref, dd_ref):
        gi = pl.program_id(0)
        for widx, win in enumerate(POOL_WINDOWS):
            @pl.when(gi == widx)
            def _(win=win):
                group(win, dz_ref, diff_ref, pre_ref, w_ref, sc_ref, dpre_ref, dw_ref, dsc_ref, pad_ref, dd_ref)

    col = pl.BlockSpec((t, grp), lambda g: (0, g))
    both = pl.BlockSpec((2, t, grp), lambda g: (0, 0, g))
    wspec = pl.BlockSpec((None, grp, grp), lambda g: (g, 0, 0))
    sspec = pl.BlockSpec((1, grp), lambda g: (0, g))
    return _pc(
        body, name=name, grid=(len(POOL_WINDOWS),),
        in_specs=[col, col, both, wspec, sspec],
        out_specs=[both, wspec, sspec],
        out_shape=[jax.ShapeDtypeStruct((2, t, width), BF16), jax.ShapeDtypeStruct((len(POOL_WINDOWS), grp, grp), F32),
                   jax.ShapeDtypeStruct((1, width), F32)],
        scratch_shapes=[pltpu.VMEM((padded, grp), F32), pltpu.VMEM((t, grp), F32)],
        compiler_params=_params("parallel"),
    )(dz, diff, pre, w_grp, scale)


def _conv_fwd(pre, dw, db, name):
    _, t, width = pre.shape
    cb = LANES
    segs = [(0, t)]

    def body(pre_ref, dw_ref, db_ref, z_ref, pad_ref):
        pad_ref[...] = jnp.zeros_like(pad_ref)

        def fill(s, st, ln, off, b):
            rows = pl.ds(b, CHUNK)
            pad_ref[pl.ds(off + b, CHUNK), :] = pre_ref[1, rows, :] * pre_ref[2, rows, :]

        _for_chunks(segs, fill)

        def mix(s, st, ln, off, b):
            rows = pl.ds(b, CHUNK)
            ext = pad_ref[pl.ds(off - HALO + b, CHUNK + 2 * HALO), :]
            conv = (dw_ref[0:1, :] * _shift(ext, 1) + dw_ref[1:2, :] * ext + dw_ref[2:3, :] * _shift(ext, -1))
            conv = conv[HALO:HALO + CHUNK] + db_ref[...]
            y = pre_ref[0, rows, :] * conv
            z_ref[rows, :] = (y * _silu(pre_ref[3, rows, :])).astype(BF16)

        _for_chunks(segs, mix)

    return _pc(
        body, name=name, grid=(width // cb,),
        in_specs=[pl.BlockSpec((4, t, cb), lambda j: (0, 0, j)), pl.BlockSpec((8, cb), lambda j: (0, j)),
                  pl.BlockSpec((1, cb), lambda j: (0, j))],
        out_specs=pl.BlockSpec((t, cb), lambda j: (0, j)),
        out_shape=jax.ShapeDtypeStruct((t, width), BF16),
        scratch_shapes=[pltpu.VMEM((t + 2 * HALO, cb), F32)],
        compiler_params=_params("parallel"),
    )(pre, dw, db)


def _conv_bwd(dz, pre, dw, db, name):
    _, t, width = pre.shape
    cb = LANES
    segs = [(0, t)]

    def body(dz_ref, pre_ref, dw_ref, db_ref, dpre_ref, ddw_ref, ddb_ref, pad_a, pad_c):
        pad_a[...] = jnp.zeros_like(pad_a)
        pad_c[...] = jnp.zeros_like(pad_c)
        ddw_ref[...] = jnp.zeros_like(ddw_ref)
        ddb_ref[...] = jnp.zeros_like(ddb_ref)

        def fill(s, st, ln, off, b):
            rows = pl.ds(b, CHUNK)
            pad_a[pl.ds(off + b, CHUNK), :] = pre_ref[1, rows, :] * pre_ref[2, rows, :]

        _for_chunks(segs, fill)

        def first(s, st, ln, off, b):
            rows = pl.ds(b, CHUNK)
            ext = pad_a[pl.ds(off - HALO + b, CHUNK + 2 * HALO), :]
            prev, nxt = _shift(ext, 1)[HALO:HALO + CHUNK], _shift(ext, -1)[HALO:HALO + CHUNK]
            here = ext[HALO:HALO + CHUNK]
            conv = dw_ref[0:1, :] * prev + dw_ref[1:2, :] * here + dw_ref[2:3, :] * nxt + db_ref[...]
            bg, g = pre_ref[0, rows, :], pre_ref[3, rows, :]
            dzv = dz_ref[rows, :]
            dy = dzv * _silu(g)
            dpre_ref[3, rows, :] = (dzv * (bg * conv) * _dsilu(g)).astype(BF16)
            dpre_ref[0, rows, :] = (dy * conv).astype(BF16)
            dconv = dy * bg
            pad_c[pl.ds(off + b, CHUNK), :] = dconv
            ddw_ref[0:1, :] += jnp.sum(dconv * prev, axis=0, keepdims=True)
            ddw_ref[1:2, :] += jnp.sum(dconv * here, axis=0, keepdims=True)
            ddw_ref[2:3, :] += jnp.sum(dconv * nxt, axis=0, keepdims=True)
            ddb_ref[0:1, :] += jnp.sum(dconv, axis=0, keepdims=True)

        _for_chunks(segs, first)

        def second(s, st, ln, off, b):
            rows = pl.ds(b, CHUNK)
            ext = pad_c[pl.ds(off - HALO + b, CHUNK + 2 * HALO), :]
            da = (dw_ref[0:1, :] * _shift(ext, -1) + dw_ref[1:2, :] * ext + dw_ref[2:3, :] * _shift(ext, 1))
            da = da[HALO:HALO + CHUNK]
            dpre_ref[1, rows, :] = (da * pre_ref[2, rows, :]).astype(BF16)
            dpre_ref[2, rows, :] = (da * pre_ref[1, rows, :]).astype(BF16)

        _for_chunks(segs, second)

    quad = pl.BlockSpec((4, t, cb), lambda j: (0, 0, j))
    rows8 = pl.BlockSpec((8, cb), lambda j: (0, j))
    return _pc(
        body, name=name, grid=(width // cb,),
        in_specs=[pl.BlockSpec((t, cb), lambda j: (0, j)), quad, rows8, pl.BlockSpec((1, cb), lambda j: (0, j))],
        out_specs=[quad, rows8, rows8],
        out_shape=[jax.ShapeDtypeStruct((4, t, width), BF16), jax.ShapeDtypeStruct((8, width), F32),
                   jax.ShapeDtypeStruct((8, width), F32)],
        scratch_shapes=[pltpu.VMEM((t + 2 * HALO, cb), F32), pltpu.VMEM((t + 2 * HALO, cb), F32)],
        compiler_params=_params("parallel"),
    )(dz, pre, dw, db)


PAIR_TILES = 2 * WIN_ROWS - 2


def _pair_geometry():
    lane = lax.broadcasted_iota(jnp.int32, (GRID_W, LANES), 1)
    qcol = lax.broadcasted_iota(jnp.int32, (GRID_W, LANES), 0)
    low = lane < GRID_W
    kcol = jnp.where(low, lane, lane - GRID_W)
    start = jnp.clip(qcol - WIN_COLS // 2, 0, GRID_W - WIN_COLS)
    inside = (kcol >= start) & (kcol < start + WIN_COLS)
    return low, inside


def _bias_tiles(rpb_ref, rows_ref, tiles_ref, inside):
    for h in range(2):
        rows = rpb_ref[h]
        rows_ref[h] = (pltpu.roll(rows, LANES - (WIN_COLS - 1), 1)
                       + pltpu.roll(pltpu.roll(rows, GRID_W - (WIN_COLS - 1), 1), 2 * WIN_ROWS - 1, 0))
        for t in range(PAIR_TILES):
            both = jnp.broadcast_to(rows_ref[h, t:t + 1, :], (GRID_W, LANES))
            tiles_ref[h, t] = jnp.where(inside, pltpu.roll(both, 0, 1, stride=1, stride_axis=0), MASKED)


def _bias_tiles_grad(dtiles_ref, drpb_ref):
    sub = 8
    qcol = lax.broadcasted_iota(jnp.int32, (PAIR_TILES * sub, LANES), 0) & (sub - 1)
    lane = lax.broadcasted_iota(jnp.int32, (1, LANES), 1)
    zero = jnp.zeros((1, LANES), F32)
    for h in range(2):
        v = dtiles_ref[h]
        rows = GRID_W
        while rows > sub:
            rows //= 2
            upper = pltpu.roll(v[:, rows:].reshape(PAIR_TILES * rows, LANES), LANES - rows, 1)
            v = v[:, :rows] + upper.reshape(PAIR_TILES, rows, LANES)
        v = v.reshape(PAIR_TILES * sub, LANES)
        for bit in range(3):
            v = jnp.where((qcol >> bit) & 1 == 1, pltpu.roll(v, LANES - (1 << bit), 1), v)
        v = pltpu.roll(v, WIN_COLS - 1, 1)
        sums = [jnp.sum(v[t * sub:(t + 1) * sub], axis=0, keepdims=True) for t in range(PAIR_TILES)]
        for r in range(2 * WIN_ROWS):
            here = sums[r] if r < PAIR_TILES else zero
            prev = pltpu.roll(sums[r - 1], GRID_W, 1) if 1 <= r <= PAIR_TILES else zero
            drpb_ref[h, r:r + 1, :] = jnp.where(lane < 2 * WIN_COLS - 1, here + prev, 0.0)


def _attn_rows(r, n_rows):
    first = jnp.clip(r - WIN_ROWS // 2, 0, n_rows - WIN_ROWS)
    return first, first - r + WIN_ROWS - 1


def _softmax(s_loc, s_ctx):
    m = jnp.maximum(jnp.max(s_loc, axis=-1, keepdims=True), jnp.max(s_ctx, axis=-1, keepdims=True))
    e_loc, e_ctx = jnp.exp(s_loc - m), jnp.exp(s_ctx - m)
    inv = 1.0 / (jnp.sum(e_loc, axis=-1, keepdims=True) + jnp.sum(e_ctx, axis=-1, keepdims=True))
    return e_loc * inv, e_ctx * inv


def _pair_bias(tiles_ref, j):
    return jnp.concatenate(
        [jnp.concatenate([tiles_ref[h, j + 2 * m] for m in range(WIN_ROWS // 2)], axis=1) for h in range(2)], axis=0)


ROWS_PER_STEP = 8
ROWS_PER_STEP_BWD = 4


def _by_head(tile, low):
    zero = jnp.zeros_like(tile)
    return jnp.concatenate([jnp.where(low, tile, zero), jnp.where(low, zero, tile)], axis=0)


def _merge_heads(stacked, low):
    return jnp.where(low, stacked[:GRID_W], stacked[GRID_W:])


def _attn_items(step, n_rows, q_ref, low, per_step):
    items = []
    for u in range(per_step):
        r = step * per_step + u
        first, j = _attn_rows(r, n_rows)
        rows = pl.ds(pl.multiple_of(r * GRID_W, GRID_W), GRID_W)
        keys = pl.ds(pl.multiple_of(first * GRID_W, GRID_W), WIN_ROWS * GRID_W)
        q = (q_ref[rows, :].astype(F32) * HEAD_DIM ** -0.5).astype(BF16)
        items.append((rows, keys, j, _by_head(q, low)))
    return items


def _attn_fwd(qkv, gate, rpb, seq):
    _, t, width = qkv.shape
    n_rows = seq // GRID_W
    n_ctx = t - seq
    heads = 2 * width // LANES

    def body(q_ref, k_ref, v_ref, g_ref, rpb_ref, z_ref, o_ref, tiles_ref, rows_ref):
        low, inside = _pair_geometry()
        _bias_tiles(rpb_ref, rows_ref, tiles_ref, inside)
        ctx = pl.ds(seq, n_ctx)

        def step(i, carry):
            items = _attn_items(i, n_rows, q_ref, low, ROWS_PER_STEP)
            k_ctx, v_ctx = k_ref[ctx, :], v_ref[ctx, :]
            scores = [(_dot(q, k_ref[keys, :], 1, 1) + _pair_bias(tiles_ref, j), _dot(q, k_ctx, 1, 1))
                      for _, keys, j, q in items]
            probs = [_softmax(s_loc, s_ctx) for s_loc, s_ctx in scores]
            outs = [_dot(p_loc.astype(BF16), v_ref[keys, :]) + _dot(p_ctx.astype(BF16), v_ctx)
                    for (_, keys, _, _), (p_loc, p_ctx) in zip(items, probs)]
            for (rows, _, _, _), out in zip(items, outs):
                o = _merge_heads(out, low)
                o_ref[rows, :] = o
                z_ref[rows, :] = (o * _silu(g_ref[rows, :])).astype(BF16)
            return carry

        lax.fori_loop(0, n_rows // ROWS_PER_STEP, step, 0)

    def part(p):
        return pl.BlockSpec((None, t, LANES), lambda h: (p, 0, h))

    out = pl.BlockSpec((seq, LANES), lambda h: (0, h))
    return _pc(
        body, name="attn_fwd", grid=(width // LANES,),
        in_specs=[part(0), part(1), part(2), part(0), pl.BlockSpec((2, 2 * WIN_ROWS, LANES), lambda h: (h, 0, 0))],
        out_specs=[out, out, pl.BlockSpec((2, PAIR_TILES, GRID_W, LANES), lambda h: (h, 0, 0, 0))],
        out_shape=[jax.ShapeDtypeStruct((seq, width), BF16), jax.ShapeDtypeStruct((seq, width), F32),
                   jax.ShapeDtypeStruct((heads, PAIR_TILES, GRID_W, LANES), F32)],
        scratch_shapes=[pltpu.VMEM((2, 2 * WIN_ROWS, LANES), F32)],
        compiler_params=_params("parallel"),
    )(qkv, qkv, qkv, gate, rpb)


def _attn_bwd(qkv, gate, o, dz, tiles, seq):
    _, t, width = qkv.shape
    n_rows = seq // GRID_W
    n_ctx = t - seq
    heads = 2 * width // LANES

    def body(q_ref, k_ref, v_ref, g_ref, o_ref, dz_ref, tiles_ref, dpre_ref, drpb_ref, dtiles_ref, dk_ref, dv_ref):
        low, _ = _pair_geometry()
        dtiles_ref[...] = jnp.zeros_like(dtiles_ref)
        dk_ref[...] = jnp.zeros_like(dk_ref)
        dv_ref[...] = jnp.zeros_like(dv_ref)
        ctx = pl.ds(seq, n_ctx)

        def step(i, carry):
            items = _attn_items(i, n_rows, q_ref, low, ROWS_PER_STEP_BWD)
            k_ctx, v_ctx = k_ref[ctx, :], v_ref[ctx, :]
            d_outs = []
            for rows, _, _, _ in items:
                g = g_ref[rows, :]
                dzv = dz_ref[rows, :]
                dpre_ref[3, rows, :] = (dzv * o_ref[rows, :] * _dsilu(g)).astype(BF16)
                d_outs.append(_by_head((dzv * _silu(g)).astype(BF16), low))
            scores = [(_dot(q, k_ref[keys, :], 1, 1) + _pair_bias(tiles_ref, j), _dot(q, k_ctx, 1, 1))
                      for _, keys, j, q in items]
            dprobs = [(_dot(d_o, v_ref[keys, :], 1, 1), _dot(d_o, v_ctx, 1, 1))
                      for (_, keys, _, _), d_o in zip(items, d_outs)]
            probs = [_softmax(s_loc, s_ctx) for s_loc, s_ctx in scores]
            dscores = []
            for (p_loc, p_ctx), (dp_loc, dp_ctx) in zip(probs, dprobs):
                delta = (jnp.sum(p_loc * dp_loc, axis=-1, keepdims=True)
                         + jnp.sum(p_ctx * dp_ctx, axis=-1, keepdims=True))
                dscores.append((p_loc * (dp_loc - delta), p_ctx * (dp_ctx - delta)))
            dqs = [_dot(ds_loc.astype(BF16), k_ref[keys, :]) + _dot(ds_ctx.astype(BF16), k_ctx)
                   for (_, keys, _, _), (ds_loc, ds_ctx) in zip(items, dscores)]
            for (rows, _, _, _), dq in zip(items, dqs):
                dpre_ref[0, rows, :] = (_merge_heads(dq, low) * HEAD_DIM ** -0.5).astype(BF16)
            for (_, keys, j, q), d_o, (p_loc, p_ctx), (ds_loc, ds_ctx) in zip(items, d_outs, probs, dscores):
                dk_ref[keys, :] += _dot(ds_loc.astype(BF16), q, 0, 0)
                dk_ref[ctx, :] += _dot(ds_ctx.astype(BF16), q, 0, 0)
                dv_ref[keys, :] += _dot(p_loc.astype(BF16), d_o, 0, 0)
                dv_ref[ctx, :] += _dot(p_ctx.astype(BF16), d_o, 0, 0)
                for h in range(2):
                    for m in range(WIN_ROWS // 2):
                        dtiles_ref[h, j + 2 * m] += ds_loc[h * GRID_W:(h + 1) * GRID_W, m * LANES:(m + 1) * LANES]
            return carry

        lax.fori_loop(0, n_rows // ROWS_PER_STEP_BWD, step, 0)
        dpre_ref[1] = dk_ref[...].astype(BF16)
        dpre_ref[2] = dv_ref[...].astype(BF16)
        dpre_ref[0, ctx, :] = jnp.zeros((n_ctx, LANES), BF16)
        dpre_ref[3, ctx, :] = jnp.zeros((n_ctx, LANES), BF16)
        _bias_tiles_grad(dtiles_ref, drpb_ref)

    def part(p):
        return pl.BlockSpec((None, t, LANES), lambda h: (p, 0, h))

    lat = pl.BlockSpec((seq, LANES), lambda h: (0, h))
    rspec = pl.BlockSpec((2, 2 * WIN_ROWS, LANES), lambda h: (h, 0, 0))
    tspec = pl.BlockSpec((2, PAIR_TILES, GRID_W, LANES), lambda h: (h, 0, 0, 0))
    return _pc(
        body, name="attn_bwd", grid=(width // LANES,),
        in_specs=[part(0), part(1), part(2), part(0), lat, lat, tspec],
        out_specs=[pl.BlockSpec((4, t, LANES), lambda h: (0, 0, h)), rspec],
        out_shape=[jax.ShapeDtypeStruct((4, t, width), BF16), jax.ShapeDtypeStruct((heads, 2 * WIN_ROWS, LANES), F32)],
        scratch_shapes=[pltpu.VMEM((2, PAIR_TILES, GRID_W, LANES), F32),
                        pltpu.VMEM((t, LANES), F32), pltpu.VMEM((t, LANES), F32)],
        compiler_params=_params("parallel"),
    )(qkv, qkv, qkv, gate, o, dz, tiles)


def _adam_update(w, m, v, g):
    m2 = ADAM_B1 * m + (1.0 - ADAM_B1) * g
    v2 = ADAM_B2 * v + (1.0 - ADAM_B2) * (g * g)
    m_hat = m2 / (1.0 - ADAM_B1 ** ADAM_STEP)
    v_hat = v2 / (1.0 - ADAM_B2 ** ADAM_STEP)
    return -ADAM_LR * (m_hat / (jnp.sqrt(v_hat) + ADAM_EPS) + ADAM_WD * w), m2, v2


def _adamw(w, m, v, parts, name, after=None):
    rows, cols = w.shape
    tr = _tile(rows, max(8, ADAM_TILE_ELEMS // cols), 8)
    n_parts = len(parts)

    def body(*refs):
        w_ref, m_ref, v_ref = refs[:3]
        part_refs = refs[3:3 + n_parts]
        g_ref, d_ref, nm_ref, nv_ref = refs[-4:]
        g = part_refs[0][...].astype(F32)
        for p in part_refs[1:]:
            g = g + p[...].astype(F32)
        g_ref[...] = g
        d_ref[...], nm_ref[...], nv_ref[...] = _adam_update(w_ref[...], m_ref[...], v_ref[...], g)

    tile = pl.BlockSpec((tr, cols), lambda i: (i, 0))
    in_specs, args = [tile, tile, tile], [w, m, v]
    for p in parts:
        if isinstance(p, tuple):
            arr, k = p
            in_specs.append(pl.BlockSpec((None, tr, cols), lambda i, k=k: (k, i, 0)))
            args.append(arr)
        else:
            in_specs.append(tile)
            args.append(p)
    if after is not None:
        in_specs.append(ANY)
        args.append(after)
    shape = jax.ShapeDtypeStruct((rows, cols), F32)
    return _pc(
        body, name=name, grid=(rows // tr,), in_specs=in_specs, out_specs=[tile] * 4, out_shape=[shape] * 4,
        compiler_params=_params("parallel"),
    )(*args)


def _adamw_layers(w, m, v, landed, name):
    n_layers, rows, cols = w.shape
    tr = _tile(rows, max(8, ADAM_TILE_ELEMS // cols), 8)

    def body(*refs):
        w_ref, m_ref, v_ref = refs[:3]
        part_refs = refs[3:3 + n_layers * N_DEV]
        g_ref, d_ref, nm_ref, nv_ref = refs[3 + n_layers * N_DEV:]
        layer = pl.program_id(0)
        g = None
        for l in range(n_layers):
            s = part_refs[l * N_DEV][...].astype(F32)
            for p in part_refs[l * N_DEV + 1:(l + 1) * N_DEV]:
                s = s + p[...].astype(F32)
            g = s if g is None else jnp.where(layer == l, s, g)
        g_ref[...] = g
        d_ref[...], nm_ref[...], nv_ref[...] = _adam_update(w_ref[...], m_ref[...], v_ref[...], g)

    tile = pl.BlockSpec((None, tr, cols), lambda l, i: (l, i, 0))
    in_specs, args = [tile, tile, tile], [w, m, v]
    for l, arr in enumerate(landed):
        for k in range(N_DEV):
            in_specs.append(pl.BlockSpec((None, tr, cols), lambda ll, i, l=l, k=k: (k, jnp.where(ll == l, i, 0), 0)))
            args.append(arr)
    shape = jax.ShapeDtypeStruct(w.shape, F32)
    return _pc(
        body, name=name, grid=(n_layers, rows // tr), in_specs=in_specs, out_specs=[tile] * 4, out_shape=[shape] * 4,
        compiler_params=_params("arbitrary", "arbitrary"),
    )(*args)


def _adamw_small(states, grads):
    sources, makers = grads
    n, ns = len(states), len(sources)

    def body(*refs):
        src = refs[:ns]
        ins = refs[ns:ns + 3 * n]
        outs = refs[ns + 3 * n:]
        for k in range(n):
            w_ref, m_ref, v_ref = ins[3 * k:3 * k + 3]
            g = makers[k](*src)
            outs[4 * k][...] = g
            outs[4 * k + 1][...], outs[4 * k + 2][...], outs[4 * k + 3][...] = _adam_update(
                w_ref[...], m_ref[...], v_ref[...], g)

    flat = [a for s in states for a in s]
    vmem = pl.BlockSpec(memory_space=pltpu.VMEM)
    res = _pc(
        body, name="adamw_small",
        in_specs=[vmem] * (ns + 3 * n), out_specs=[vmem] * (4 * n),
        out_shape=[jax.ShapeDtypeStruct(s[0].shape, F32) for s in states for _ in range(4)],
        compiler_params=pltpu.CompilerParams(vmem_limit_bytes=VMEM_LIMIT),
    )(*sources, *flat)
    return [res[4 * k:4 * k + 4] for k in range(n)]


def _rows128(a):
    flat = a.reshape(-1)
    pad = (-flat.shape[0]) % LANES
    if pad:
        flat = jnp.concatenate([flat, jnp.zeros((pad,), flat.dtype)])
    return flat.reshape(-1, LANES)


def _pad_rows(a, mult=8):
    pad = (-a.shape[0]) % mult
    if pad:
        a = jnp.concatenate([a, jnp.zeros((pad,) + a.shape[1:], a.dtype)], axis=0)
    return a


def kernel(x, c, ctx, c_ctx, norm_g, ada_w, ada_b, pool_w_in, pool_w_grp, pool_scale, pool_w_out, na_w_in, na_rpb, na_w_out, conv_w_in, conv_dw, conv_db, conv_w_out, final_g, loss_target, m_c_ctx, m_norm_g, m_ada_w, m_ada_b, m_pool_w_in, m_pool_w_grp, m_pool_scale, m_pool_w_out, m_na_w_in, m_na_rpb, m_na_w_out, m_conv_w_in, m_conv_dw, m_conv_db, m_conv_w_out, m_final_g, v_c_ctx, v_norm_g, v_ada_w, v_ada_b, v_pool_w_in, v_pool_w_grp, v_pool_scale, v_pool_w_out, v_na_w_in, v_na_rpb, v_na_w_out, v_conv_w_in, v_conv_dw, v_conv_db, v_conv_w_out, v_final_g):
    xi, yi, ci = _my_place()
    me = 4 * xi + 2 * yi + ci
    seq, d = x.shape[1], x.shape[2]
    n_ctx = ctx.shape[1]
    width = d
    depth = norm_g.shape[0]
    nb = ada_w.shape[2]
    shard = width // N_DEV
    d_rows = d // LANES
    assert seq % CHUNK == 0 and n_ctx % CHUNK == 0 and (seq // GRID_W) % ROWS_PER_STEP == 0 and seq >= WIN_ROWS * GRID_W
    tr = math.gcd(math.gcd(seq, n_ctx), 256)
    x_tiles = seq // tr
    tr_lat = math.gcd(seq, 512)
    lat_tiles = seq // tr_lat

    n_pool = pool_scale.shape[0]
    n_grp = pool_w_grp.shape[1]
    grp = width // n_grp

    small_in = _pad_rows(jnp.concatenate([_rows128(c), pool_scale, conv_dw[0], conv_db], axis=0))
    got = _gather_small(small_in, "gather_inputs")
    r0 = d_rows
    c_all = got[:, :r0].reshape(N_DEV, d)
    scale_full = got[:, r0:r0 + n_pool].transpose(1, 0, 2).reshape(n_pool, width)
    r1 = r0 + n_pool
    taps_full = _pad_rows(got[:, r1:r1 + 3].transpose(1, 0, 2).reshape(3, width))
    bias_full = got[:, r1 + 3:r1 + 4].transpose(1, 0, 2).reshape(1, width)

    cond = jnp.concatenate([c_all, c_ctx[None], jnp.zeros((7, d), F32)], axis=0)
    bias_mine = lax.dynamic_slice(ada_b, (0, me * nb), (depth, nb))
    mod_mine = _mod_fwd(cond, ada_w, bias_mine)
    by_example = jnp.stack([mod_mine[:, :N_DEV].transpose(1, 0, 2),
                            jnp.broadcast_to(mod_mine[:, N_DEV][None], (N_DEV, depth, nb))], axis=2)
    mod_all = _gather_small(by_example.reshape(N_DEV, -1, LANES), "gather_mod", per_dest=True)
    mod_all = mod_all.reshape(N_DEV, depth, 2, nb).transpose(1, 2, 0, 3).reshape(depth, 2, 3, d)
    mod_all = jnp.pad(mod_all, ((0, 0), (0, 0), (0, 5), (0, 0)))
    mods = [mod_all[i] if i < 2 else mod_all[i, :1] for i in range(depth)]

    layer_weights = [[pool_w_in[0], pool_w_grp[0], pool_w_out[0]], [na_w_in[0], na_w_out[0]],
                     [conv_w_in[0], conv_w_out[0]], [pool_w_in[1], pool_w_grp[1], pool_w_out[1]]]
    slot = {(i, t): n for n, (i, t) in enumerate((i, t) for i, ws in enumerate(layer_weights) for t in range(len(ws)))}
    two_level = [(0, 0), (1, 0)]
    weights_sent = _exchange_start(
        [w.astype(BF16) for ws in layer_weights for w in ws], False, mod_all, "weights_start",
        peers=[CHIP_PEERS if key in two_level else ALL_PEERS for key in slot])
    token = weights_sent[-1]

    def landed_weight(i, t, after):
        return _exchange_wait(weights_sent, False, after, f"weights_wait{i}_{t}", which=[slot[i, t]])[0]

    def handed_on(i, after):
        half = [landed_weight(i, 0, after)]
        rest = _forward_start(half, after, f"weights_forward{i}")
        return _forward_wait(rest, rest[-1], f"weights_forward_wait{i}")[0][:, None]

    def as_in(w):
        return w[:, None]

    def as_grp(w):
        return w.transpose(1, 0, 2, 3).reshape(n_grp, grp, grp)

    def as_out(w):
        return w.reshape(width, d)

    both = [(0, seq), (seq, n_ctx)]
    latent = [(0, seq)]

    def grp_slots(g):
        return g.reshape(n_grp, N_DEV, grp // N_DEV, grp).transpose(1, 0, 2, 3).reshape(N_DEV, -1, grp).astype(BF16)

    def send_grads(i, grads):
        return _exchange_start(grads, True, jnp.zeros((8, LANES), F32), f"grads_start{i}")

    xs0 = jnp.concatenate([x[0], ctx[0]], axis=0)
    h0 = _norm_fwd(xs0, norm_g[0:1] + token[0, 0], mods[0], tr, x_tiles, "norm_fwd0")
    pool_in_w0 = handed_on(0, h0)
    pre0 = _proj_in(h0, pool_in_w0, 0, width, "proj_in0")
    pool_grp_w0 = as_grp(landed_weight(0, 1, pre0))
    z0, diff0 = _pool_fwd(pre0, pool_grp_w0, scale_full[0:1], both, "pool_fwd0")
    pool_out_w0 = as_out(landed_weight(0, 2, z0))
    yx0, xs1, h1 = _proj_out(z0, pool_out_w0, xs0, mods[0], tr, x_tiles, "proj_out0", nxt=(norm_g[1:2], mods[1]))

    na_in_w = handed_on(1, h1)
    qkv1, gpre1 = _proj_in_split(h1, na_in_w, 0, width, "proj_in1", 3)
    rpb_rows = jnp.pad(na_rpb[0], ((0, 0), (0, 2 * WIN_ROWS - na_rpb.shape[2]), (0, LANES - na_rpb.shape[3])))
    z1, o1, bias_tiles = _attn_fwd(qkv1, gpre1, rpb_rows, seq)
    na_out_w = as_out(landed_weight(1, 1, z1))
    yx1, x2, h2 = _proj_out(z1, na_out_w, xs1, mods[1], tr_lat, lat_tiles, "proj_out1", nxt=(norm_g[2:3], mods[2]))

    conv_in_w = as_in(landed_weight(2, 0, h2))
    pre2 = _proj_in(h2, conv_in_w, 0, width, "proj_in2")
    z2 = _conv_fwd(pre2, taps_full, bias_full, "conv_fwd")
    conv_out_w = as_out(landed_weight(2, 1, z2))
    yx2, x3, h3 = _proj_out(z2, conv_out_w, x2, mods[2], tr_lat, lat_tiles, "proj_out2", nxt=(norm_g[3:4], mods[3]))

    pool_in_w3 = as_in(landed_weight(3, 0, h3))
    pre3 = _proj_in(h3, pool_in_w3, 0, width, "proj_in3")
    pool_grp_w3 = as_grp(landed_weight(3, 1, pre3))
    z3, diff3 = _pool_fwd(pre3, pool_grp_w3, scale_full[1:2], latent, "pool_fwd3")
    pool_out_w3 = as_out(landed_weight(3, 2, z3))
    yx3, x4 = _proj_out(z3, pool_out_w3, x3, mods[3], tr_lat, lat_tiles, "proj_out3")

    loss_part, dx4, d_final, dyx3, gate3 = _loss_head(x4, loss_target[0], final_g[None], yx3, mods[3], tr_lat)

    dz3, g_pool_out1 = _proj_out_bwd(dyx3, z3, pool_out_w3, "proj_out_bwd3")
    dpre3, g_grp1, g_scale1 = _pool_bwd(dz3, diff3, pre3, pool_grp_w3, scale_full[1:2], latent, "pool_bwd3")
    dh3 = _proj_in_dh(dpre3, pool_in_w3, 0, "proj_in_dh3")
    g_pool_in1 = _grad_w_in(h3, dpre3, pool_w_in.shape[2], "grad_w_in3")
    sent3 = send_grads(3, [g_pool_in1, grp_slots(g_grp1), g_pool_out1.reshape(N_DEV, shard, d)])
    dx3, norm3, dyx2, gate2 = _norm_bwd(x3, dh3, dx4, norm_g[3:4] + sent3[-1][0, 0], mods[3], tr_lat, lat_tiles, "norm_bwd3",
                                        below=(yx2, mods[2]))

    dz2, g_conv_out = _proj_out_bwd(dyx2, z2, conv_out_w, "proj_out_bwd2")
    dpre2, g_taps, g_cbias = _conv_bwd(dz2, pre2, taps_full, bias_full, "conv_bwd")
    dh2 = _proj_in_dh(dpre2, conv_in_w, 0, "proj_in_dh2")
    g_conv_in = _grad_w_in(h2, dpre2, conv_w_in.shape[2], "grad_w_in2")
    sent2 = send_grads(2, [g_conv_in, g_conv_out.reshape(N_DEV, shard, d)])
    dx2, norm2, dyx1, gate1 = _norm_bwd(x2, dh2, dx3, norm_g[2:3] + sent2[-1][0, 0], mods[2], tr_lat, lat_tiles, "norm_bwd2",
                                        below=(yx1, mods[1][:1]))

    dz1, g_na_out = _proj_out_bwd(dyx1, z1, na_out_w, "proj_out_bwd1")
    dpre1, g_rpb = _attn_bwd(qkv1, gpre1, o1, dz1, bias_tiles, seq)
    g_rpb = g_rpb[:, :na_rpb.shape[2], :na_rpb.shape[3]]
    dh1 = _proj_in_dh(dpre1, na_in_w, 0, "proj_in_dh1")
    g_na_in = _grad_w_in(h1, dpre1, na_w_in.shape[2], "grad_w_in1")
    sent1 = send_grads(1, [g_na_in, g_na_out.reshape(N_DEV, shard, d)])
    dxs1, norm1, dyx0, gate0 = _norm_bwd(xs1, dh1, dx2, norm_g[1:2] + sent1[-1][0, 0], mods[1], tr, x_tiles, "norm_bwd1",
                                         res_tiles=x_tiles, below=(yx0, mods[0]))

    dz0, g_pool_out0 = _proj_out_bwd(dyx0, z0, pool_out_w0, "proj_out_bwd0")
    sent0a = _exchange_start([g_pool_out0.reshape(N_DEV, shard, d)], True, jnp.zeros((8, LANES), F32), "grads_start0a")
    dpre0, g_grp0, g_scale0 = _pool_bwd(dz0, diff0, pre0, pool_grp_w0, scale_full[0:1], both, "pool_bwd0")
    g_pool_in0 = _grad_w_in(h0, dpre0, pool_w_in.shape[2], "grad_w_in0")
    dh0 = _proj_in_dh(dpre0, pool_in_w0, 0, "proj_in_dh0", after=sent0a[-1])
    dx0, norm0 = _norm_bwd(xs0, dh0, dxs1, norm_g[0:1], mods[0], tr, x_tiles, "norm_bwd0", out_tiles=x_tiles)
    grad_x = dx0[None]

    norms, gates = [norm0, norm1, norm2, norm3], [gate0, gate1, gate2, gate3]
    zero_d = jnp.zeros((d,), F32)
    dm_rows = [jnp.concatenate([norms[i][0, 0], norms[i][0, 1], gates[i][0, 0]]) for i in range(depth)]
    dm_rows.append(jnp.concatenate([norm0[1, 0], norm0[1, 1], gate0[1, 0]]))
    dm_rows.append(jnp.concatenate([norm1[1, 0], norm1[1, 1], zero_d]))
    dm_local = jnp.stack(dm_rows + [jnp.zeros((3 * d,), F32)] * 2)
    g_norm_part = jnp.stack([norm0[0, 2] + norm0[1, 2], norm1[0, 2] + norm1[1, 2], norm2[0, 2], norm3[0, 2]])
    pieces = [_rows128(dm_local), _rows128(g_norm_part), _rows128(d_final[0]), _pad_rows(_rows128(g_rpb)), loss_part]
    marks = np.cumsum([0] + [p.shape[0] for p in pieces])
    by_owner = [a.reshape(-1, N_DEV, shard).transpose(1, 0, 2) for a in (g_scale0, g_scale1, g_taps[:3], g_cbias[0:1])]
    by_owner = jnp.concatenate(by_owner + [jnp.zeros((N_DEV, 8 - n_pool - 4, shard), F32)], axis=1)
    small_sent = _exchange_start([jnp.concatenate(pieces, axis=0), by_owner], [False, True], jnp.zeros((8, LANES), F32),
                                 "small_grads_start")
    sent0b = _exchange_start([g_pool_in0, grp_slots(g_grp0)], True, small_sent[-1], "grads_start0b")

    def big(parts, w, m, v, name):
        shape = w.shape
        view = (-1, shape[-1])
        parts = [(parts.reshape((N_DEV,) + w.reshape(view).shape), k) for k in range(N_DEV)]
        return [r.reshape(shape) for r in _adamw(w.reshape(view), m.reshape(view), v.reshape(view), parts, name)]

    in3, grp3, out3 = _exchange_wait(sent3, True, sent0b[-1], "grads_wait3")
    in2, out2 = _exchange_wait(sent2, True, sent0b[-1], "grads_wait2")
    in1, out1 = _exchange_wait(sent1, True, sent0b[-1], "grads_wait1")
    res = {}
    res["na_w_in"] = [r[None] for r in big(in1, na_w_in[0], m_na_w_in[0], v_na_w_in[0], "adamw_na_in")]
    res["na_w_out"] = [r[None] for r in big(out1, na_w_out[0], m_na_w_out[0], v_na_w_out[0], "adamw_na_out")]
    res["conv_w_in"] = [r[None] for r in big(in2, conv_w_in[0], m_conv_w_in[0], v_conv_w_in[0], "adamw_conv_in")]
    res["conv_w_out"] = [r[None] for r in big(out2, conv_w_out[0], m_conv_w_out[0], v_conv_w_out[0], "adamw_conv_out")]

    done = [res[n][0] for n in ("na_w_in", "na_w_out", "conv_w_in", "conv_w_out")]
    small_out, owned = _exchange_wait(small_sent, [False, True], done, "small_grads_wait")
    loss = jnp.sum(small_out[:, marks[4], 0])
    dm_all = small_out[:, :marks[1]].reshape(N_DEV, 8, 3 * d).transpose(1, 0, 2)
    dm_mine = lax.dynamic_slice(dm_all, (0, 0, me * nb), (8, N_DEV, nb))
    g_ada_w, g_ada_b, cctx_part, dsilu_cond = _mod_bwd(cond, ada_w, dm_all, dm_mine)
    cctx_sent = _exchange_start([_rows128(cctx_part[0])], False, jnp.zeros((8, LANES), F32), "cctx_start")
    res["ada_w"] = [r.reshape(ada_w.shape) for r in _adamw(
        ada_w.reshape(-1, nb), m_ada_w.reshape(-1, nb), v_ada_w.reshape(-1, nb),
        [g_ada_w.reshape(-1, nb)], "adamw_ada_w", after=cctx_sent[-1])]
    cctx_all, = _exchange_wait(cctx_sent, False, res["ada_w"][0], "cctx_wait")

    def summed(ref, lo, hi):
        g = ref[0, lo:hi, :]
        for k in range(1, N_DEV):
            g = g + ref[k, lo:hi, :]
        return g

    makers = [
        lambda so, ow, cc, ab, ds: summed(cc, 0, d_rows) * ds[...],
        lambda so, ow, cc, ab, ds: summed(so, marks[1], marks[2]),
        lambda so, ow, cc, ab, ds: ab[...],
        lambda so, ow, cc, ab, ds: summed(so, marks[2], marks[3]),
        lambda so, ow, cc, ab, ds: summed(so, marks[3], marks[4]),
        lambda so, ow, cc, ab, ds: summed(ow, 0, n_pool),
        lambda so, ow, cc, ab, ds: summed(ow, n_pool, n_pool + 3),
        lambda so, ow, cc, ab, ds: summed(ow, n_pool + 3, n_pool + 4),
    ]
    rpb_rows128 = lambda a: _pad_rows(_rows128(a))
    views = [_rows128] * 4 + [rpb_rows128] + [lambda a: a.reshape(-1, LANES)] * 3
    small = [(c_ctx, m_c_ctx, v_c_ctx), (norm_g, m_norm_g, v_norm_g), (ada_b, m_ada_b, v_ada_b),
             (final_g, m_final_g, v_final_g), (na_rpb, m_na_rpb, v_na_rpb), (pool_scale, m_pool_scale, v_pool_scale),
             (conv_dw, m_conv_dw, v_conv_dw), (conv_db, m_conv_db, v_conv_db)]
    states = [tuple(view(a) for a in triple) for view, triple in zip(views, small)]
    sources = (small_out, owned, cctx_all, _rows128(g_ada_b), _rows128(dsilu_cond[8]))
    small_res = _adamw_small(states, (sources, makers))
    names = ["c_ctx", "norm_g", "ada_b", "final_g", "na_rpb", "pool_scale", "conv_dw", "conv_db"]
    for name, (w, _, _), outs4 in zip(names, small, small_res):
        res[name] = [r.reshape(-1)[:w.size].reshape(w.shape) for r in outs4]

    out0, = _exchange_wait(sent0a, True, small_res[0][0], "grads_wait0a")
    in0, grp0 = _exchange_wait(sent0b, True, small_res[0][0], "grads_wait0b")
    def both_layers(first, second, w, m, v, name):
        view = (w.shape[0], -1, w.shape[-1])
        landed = [first.reshape((N_DEV,) + w.reshape(view).shape[1:]), second.reshape((N_DEV,) + w.reshape(view).shape[1:])]
        return [r.reshape(w.shape) for r in _adamw_layers(w.reshape(view), m.reshape(view), v.reshape(view), landed, name)]

    res["pool_w_in"] = both_layers(in0, in3, pool_w_in, m_pool_w_in, v_pool_w_in, "adamw_pool_in")
    res["pool_w_grp"] = both_layers(grp0, grp3, pool_w_grp, m_pool_w_grp, v_pool_w_grp, "adamw_pool_grp")
    res["pool_w_out"] = both_layers(out0, out3, pool_w_out, m_pool_w_out, v_pool_w_out, "adamw_pool_out")

    order = ["c_ctx", "norm_g", "ada_w", "ada_b", "pool_w_in", "pool_w_grp", "pool_scale", "pool_w_out", "na_w_in",
             "na_rpb", "na_w_out", "conv_w_in", "conv_dw", "conv_db", "conv_w_out", "final_g"]
    outs = [loss, grad_x]
    for j in range(4):
        outs += [res[n][j] for n in order]
    return tuple(outs)
```

```python
import math

import numpy as np
import jax
import jax.numpy as jnp
from jax import lax
from jax.experimental import pallas as pl
from jax.experimental.pallas import tpu as pltpu

F32 = jnp.float32
BF16 = jnp.bfloat16
N_DEV = 8
LANES = 128
RMS_EPS = 1e-6
GRID_W = 64
WIN_ROWS = 8
WIN_COLS = 16
HEAD_DIM = 64
POOL_WINDOWS = (2, 4, 8, 16)
HALO = 8
CHUNK = 256
MASKED = -1e30
ADAM_LR = 0.001
ADAM_B1 = 0.9
ADAM_B2 = 0.999
ADAM_EPS = 1e-08
ADAM_WD = 0.01
ADAM_STEP = 10
VMEM_LIMIT = 56 * 1024 * 1024
ADAM_TILE_ELEMS = 256 * 1024
MESH = pl.DeviceIdType.MESH
ANY = pl.BlockSpec(memory_space=pl.ANY)
HBM = pl.BlockSpec(memory_space=pltpu.HBM)
SEM = pl.BlockSpec(memory_space=pltpu.SEMAPHORE)
EFFECT = pltpu.SideEffectType.DATAFLOW_SIDE_EFFECTING


def _pc(body, *, name, **kw):
    return pl.pallas_call(body, name=name, **kw)


def _params(*sem):
    return pltpu.CompilerParams(dimension_semantics=sem if sem else None, vmem_limit_bytes=VMEM_LIMIT)


def _dot(a, b, ca=1, cb=0, precision=None):
    return lax.dot_general(a, b, (((ca,), (cb,)), ((), ())), preferred_element_type=F32, precision=precision)


def _tile(n, pref, unit=LANES):
    best = None
    for t in range(unit, min(n, pref) + 1, unit):
        if n % t == 0:
            best = t
    return best if best is not None else n


def _sigmoid(x):
    return 1.0 / (1.0 + jnp.exp(-x))


def _silu(x):
    return x * _sigmoid(x)


def _dsilu(x):
    s = _sigmoid(x)
    return s * (1.0 + x * (1.0 - s))


def _my_place():
    return lax.axis_index("x"), lax.axis_index("y"), lax.axis_index("c")


def _flip(v, f):
    return 1 - v if f else v


def _gather_small(block, name, per_dest=False):
    rows, cols = block.shape[-2:]

    def body(x_ref, out_ref, send_sems, recv_sems):
        x, y, c = _my_place()
        me = 4 * x + 2 * y + c
        out_ref[me] = x_ref[me] if per_dest else x_ref[...]
        copies = []
        for k in range(1, N_DEV):
            peer = (_flip(x, k & 4), _flip(y, k & 2), _flip(c, k & 1))
            dest = 4 * peer[0] + 2 * peer[1] + peer[2]
            cp = pltpu.make_async_remote_copy(
                src_ref=x_ref.at[dest] if per_dest else x_ref, dst_ref=out_ref.at[me],
                send_sem=send_sems.at[k - 1], recv_sem=recv_sems.at[k - 1], device_id=peer, device_id_type=MESH)
            cp.start()
            copies.append(cp)
        for cp in copies:
            cp.wait()

    return _pc(
        body, name=name,
        out_shape=jax.ShapeDtypeStruct((N_DEV, rows, cols), block.dtype),
        in_specs=[pl.BlockSpec(memory_space=pltpu.VMEM)],
        out_specs=pl.BlockSpec(memory_space=pltpu.VMEM),
        scratch_shapes=[pltpu.SemaphoreType.DMA((N_DEV - 1,)), pltpu.SemaphoreType.DMA((N_DEV - 1,))],
    )(block)


ALL_PEERS = tuple(range(N_DEV))
CHIP_PEERS = (0, 1, 2, 4, 6)
OTHER_CHIPS = (2, 4, 6)


def _flag(per_dest, t):
    return per_dest[t] if isinstance(per_dest, (list, tuple)) else per_dest


def _peer(k):
    x, y, c = _my_place()
    peer = (_flip(x, k & 4), _flip(y, k & 2), _flip(c, k & 1))
    return peer, 4 * peer[0] + 2 * peer[1] + peer[2]


def _peer_lists(peers, nt):
    return list(peers) if isinstance(peers, list) else [peers] * nt


def _exchange_copies(srcs, lands, send_sems, recv_sems, per_dest, peers=ALL_PEERS):
    x, y, c = _my_place()
    me = 4 * x + 2 * y + c
    copies = []
    for t, (src, land, ks) in enumerate(zip(srcs, lands, _peer_lists(peers, len(srcs)))):
        for k in ks:
            peer, dest = _peer(k)
            s = len(copies)
            copies.append(pltpu.make_async_remote_copy(
                src_ref=src.at[dest] if _flag(per_dest, t) else src, dst_ref=land.at[me],
                send_sem=send_sems[s], recv_sem=recv_sems[s], device_id=peer, device_id_type=MESH))
    return copies


def _forward_copies(lands, send_sems, recv_sems):
    sibling, _ = _peer(1)
    copies = []
    for t, land in enumerate(lands):
        for n, k in enumerate(OTHER_CHIPS):
            _, slot = _peer(k)
            s = t * len(OTHER_CHIPS) + n
            copies.append(pltpu.make_async_remote_copy(
                src_ref=land.at[slot], dst_ref=land.at[slot], send_sem=send_sems[s], recv_sem=recv_sems[s],
                device_id=sibling, device_id_type=MESH))
    return copies


def _forward_start(lands, after, name):
    nt = len(lands)
    ns = nt * len(OTHER_CHIPS)

    def body(*refs):
        ins, outs = refs[:nt + 1], refs[nt + 1:]
        for cp in _forward_copies(ins[:nt], outs[:ns], outs[ns:2 * ns]):
            cp.start()
        outs[-1][...] = jnp.zeros_like(outs[-1])

    res = _pc(
        body, name=name,
        out_shape=(*[pltpu.SemaphoreType.DMA(())] * (2 * ns), *[pltpu.HBM(a.shape, a.dtype) for a in lands],
                   jax.ShapeDtypeStruct((8, LANES), F32)),
        in_specs=[HBM] * nt + [ANY],
        out_specs=(*[SEM] * (2 * ns), *[HBM] * nt, pl.BlockSpec(memory_space=pltpu.VMEM)),
        input_output_aliases={i: 2 * ns + i for i in range(nt)},
        compiler_params=pltpu.CompilerParams(has_side_effects=EFFECT),
    )(*lands, after)
    return list(res[:ns]), list(res[ns:2 * ns]), list(res[2 * ns:2 * ns + nt]), res[-1]


def _forward_wait(state, after, name):
    send_sems, recv_sems, lands, _ = state
    nt, ns = len(lands), len(send_sems)

    def body(*refs):
        sems = refs[nt:nt + 2 * ns]
        for cp in _forward_copies(refs[:nt], sems[:ns], sems[ns:]):
            cp.wait_send()
            cp.wait_recv()

    res = _pc(
        body, name=name,
        out_shape=tuple(pltpu.HBM(a.shape, a.dtype) for a in lands),
        in_specs=[HBM] * nt + [SEM] * (2 * ns) + [ANY],
        out_specs=tuple([HBM] * nt),
        input_output_aliases={i: i for i in range(nt)},
        compiler_params=pltpu.CompilerParams(has_side_effects=EFFECT),
    )(*lands, *send_sems, *recv_sems, after)
    return list(res)


def _exchange_start(srcs, per_dest, after, name, peers=ALL_PEERS):
    nt = len(srcs)
    peers = _peer_lists(peers, nt)
    ns = sum(len(ks) for ks in peers)
    lands = [lax.empty((N_DEV,) + (s.shape[1:] if _flag(per_dest, t) else s.shape), s.dtype) for t, s in enumerate(srcs)]

    def body(*refs):
        ins, outs = refs[:2 * nt + 1], refs[2 * nt + 1:]
        for cp in _exchange_copies(ins[:nt], ins[nt:2 * nt], outs[:ns], outs[ns:2 * ns], per_dest, peers):
            cp.start()
        outs[-1][...] = jnp.zeros_like(outs[-1])

    hbm = [pltpu.with_memory_space_constraint(a, pltpu.HBM) for a in list(srcs) + lands]
    res = _pc(
        body, name=name,
        out_shape=(*[pltpu.SemaphoreType.DMA(())] * (2 * ns), *[pltpu.HBM(a.shape, a.dtype) for a in hbm],
                   jax.ShapeDtypeStruct((8, LANES), F32)),
        in_specs=[HBM] * (2 * nt) + [ANY],
        out_specs=(*[SEM] * (2 * ns), *[HBM] * (2 * nt), pl.BlockSpec(memory_space=pltpu.VMEM)),
        input_output_aliases={i: 2 * ns + i for i in range(2 * nt)},
        compiler_params=pltpu.CompilerParams(has_side_effects=EFFECT),
    )(*hbm, after)
    sems, rest = res[:2 * ns], res[2 * ns:]
    return list(sems[:ns]), list(sems[ns:]), list(rest[:nt]), list(rest[nt:2 * nt]), peers, rest[-1]


def _exchange_wait(state, per_dest, after, name, which=None):
    send_sems, recv_sems, srcs, lands, peers, _ = state
    which = list(range(len(srcs))) if which is None else which
    per_dest = [_flag(per_dest, t) for t in which]
    after = list(after) if isinstance(after, (list, tuple)) else [after]
    first = np.cumsum([0] + [len(ks) for ks in peers])
    pick = [first[t] + n for t in which for n in range(len(peers[t]))]
    peers = [peers[t] for t in which]
    send_sems, recv_sems = [send_sems[s] for s in pick], [recv_sems[s] for s in pick]
    srcs, lands = [srcs[t] for t in which], [lands[t] for t in which]
    nt = len(srcs)
    ns = len(send_sems)

    def body(*refs):
        sems = refs[2 * nt:2 * nt + 2 * ns]
        for cp in _exchange_copies(refs[:nt], refs[nt:2 * nt], sems[:ns], sems[ns:], per_dest, peers):
            cp.wait_send()
            cp.wait_recv()

    thru = list(srcs) + list(lands)
    res = _pc(
        body, name=name,
        out_shape=tuple(pltpu.HBM(a.shape, a.dtype) for a in thru),
        in_specs=[HBM] * (2 * nt) + [SEM] * (2 * ns) + [ANY] * len(after),
        out_specs=tuple([HBM] * (2 * nt)),
        input_output_aliases={i: i for i in range(2 * nt)},
        compiler_params=pltpu.CompilerParams(has_side_effects=EFFECT),
    )(*thru, *send_sems, *recv_sems, *after)
    return list(res[nt:])


def _mod_fwd(cond, ada_w, bias):
    depth, d, nb = ada_w.shape

    def body(c_ref, w_ref, b_ref, o_ref):
        s = _silu(c_ref[...]).astype(BF16)
        o_ref[...] = _dot(s, w_ref[...].astype(BF16)) + b_ref[...]

    return _pc(
        body, name="mod_fwd", grid=(depth,),
        in_specs=[pl.BlockSpec((16, d), lambda i: (0, 0)), pl.BlockSpec((None, d, nb), lambda i: (i, 0, 0)),
                  pl.BlockSpec((None, 1, nb), lambda i: (i, 0, 0))],
        out_specs=pl.BlockSpec((None, 16, nb), lambda i: (i, 0, 0)),
        out_shape=jax.ShapeDtypeStruct((depth, 16, nb), F32),
        compiler_params=_params("parallel"),
    )(cond, ada_w, bias.reshape(depth, 1, nb))


def _mod_bwd(cond, ada_w, dm_all, dm_mine):
    depth, d, nb = ada_w.shape
    d3 = dm_all.shape[-1]

    def body(c_ref, w_ref, all_ref, call_ref, mine_ref, cmine_ref, gw_ref, gb_ref, part_ref, ds_ref):
        i = pl.program_id(0)
        cond_v = c_ref[...]
        s = _silu(cond_v).astype(BF16)
        has_ctx = jnp.where(i < 2, 1.0, 0.0)
        tot_all = jnp.sum(call_ref[...], axis=0, keepdims=True) * has_ctx
        tot_mine = jnp.broadcast_to(jnp.sum(cmine_ref[...], axis=0, keepdims=True) * has_ctx, (8, nb)).astype(BF16)
        gb_ref[...] = jnp.sum(all_ref[...], axis=0, keepdims=True) + tot_all
        gw_ref[...] = _dot(s[0:8], mine_ref[...].astype(BF16), 0, 0) + _dot(s[8:16], tot_mine, 0, 0)
        part = _dot(tot_mine, w_ref[...].astype(BF16), 1, 1)

        @pl.when(i == 0)
        def _():
            part_ref[...] = jnp.zeros_like(part_ref)
            ds_ref[...] = _dsilu(cond_v)

        part_ref[...] += part

    def rows(width, which):
        return pl.BlockSpec((None, N_DEV, width), which)

    layer = lambda i: (i, 0, 0)
    ctx_layer = lambda i: (jnp.minimum(i, 1) + 4, 0, 0)
    return _pc(
        body, name="mod_bwd", grid=(depth,),
        in_specs=[pl.BlockSpec((16, d), lambda i: (0, 0)), pl.BlockSpec((None, d, nb), layer),
                  rows(d3, layer), rows(d3, ctx_layer), rows(nb, layer), rows(nb, ctx_layer)],
        out_specs=[pl.BlockSpec((None, d, nb), layer), pl.BlockSpec((None, 1, d3), layer),
                   pl.BlockSpec((8, d), lambda i: (0, 0)), pl.BlockSpec((16, d), lambda i: (0, 0))],
        out_shape=[jax.ShapeDtypeStruct((depth, d, nb), F32), jax.ShapeDtypeStruct((depth, 1, d3), F32),
                   jax.ShapeDtypeStruct((8, d), F32), jax.ShapeDtypeStruct((16, d), F32)],
        compiler_params=_params("arbitrary"),
    )(cond, ada_w, dm_all, dm_all, dm_mine, dm_mine)


def _norm_fwd(xs, g, mod, tr, seg_tiles, name):
    t, d = xs.shape

    def body(x_ref, g_ref, mod_ref, h_ref):
        x = x_ref[...]
        r = lax.rsqrt(jnp.mean(x * x, axis=-1, keepdims=True) + RMS_EPS)
        y = (x * r) * g_ref[...]
        h_ref[...] = (y * (1.0 + mod_ref[1:2, :]) + mod_ref[0:1, :]).astype(BF16)

    return _pc(
        body, name=name, grid=(t // tr,),
        in_specs=[pl.BlockSpec((tr, d), lambda i: (i, 0)), pl.BlockSpec((1, d), lambda i: (0, 0)),
                  pl.BlockSpec((None, 8, d), lambda i: (i // seg_tiles, 0, 0))],
        out_specs=pl.BlockSpec((tr, d), lambda i: (i, 0)),
        out_shape=jax.ShapeDtypeStruct((t, d), BF16),
        compiler_params=_params("parallel"),
    )(xs, g, mod)


def _resid_grad(dx, i, seg_tiles, yx_ref, gate_ref, dyx_ref, gsum_ref):
    dyx_ref[...] = (dx * gate_ref[2:3, :]).astype(BF16)

    @pl.when(i % seg_tiles == 0)
    def _():
        gsum_ref[...] = jnp.zeros_like(gsum_ref)

    gsum_ref[0:1, :] += jnp.sum(dx * yx_ref[...], axis=0, keepdims=True)


def _norm_bwd(xs, dh, dres, g, mod, tr, seg_tiles, name, res_tiles=None, out_tiles=None, below=None):
    t, d = xs.shape
    n_tiles = t // tr
    res_tiles = n_tiles if res_tiles is None else res_tiles
    out_tiles = n_tiles if out_tiles is None else out_tiles

    def body(x_ref, dh_ref, dres_ref, g_ref, mod_ref, *rest):
        i = pl.program_id(0)
        x = x_ref[...]
        r = lax.rsqrt(jnp.mean(x * x, axis=-1, keepdims=True) + RMS_EPS)
        xn = x * r
        dhv = dh_ref[...]
        gain = g_ref[...]
        one_scale = 1.0 + mod_ref[1:2, :]
        dxn = dhv * (gain * one_scale)
        dx = r * (dxn - xn * jnp.mean(dxn * xn, axis=-1, keepdims=True))
        if res_tiles == n_tiles:
            dx = dx + dres_ref[...]
        else:
            dx = dx + jnp.where(i < res_tiles, dres_ref[...], 0.0)
        if below is None:
            dx_ref, sum_ref = rest
        else:
            yx_ref, gate_ref, dx_ref, sum_ref, dyx_ref, gsum_ref = rest
            _resid_grad(dx, i, seg_tiles, yx_ref, gate_ref, dyx_ref, gsum_ref)
        if out_tiles == n_tiles:
            dx_ref[...] = dx
        else:
            @pl.when(i < out_tiles)
            def _():
                dx_ref[...] = dx

        @pl.when(i % seg_tiles == 0)
        def _():
            sum_ref[...] = jnp.zeros_like(sum_ref)

        sum_ref[0:1, :] += jnp.sum(dhv, axis=0, keepdims=True)
        sum_ref[1:2, :] += jnp.sum(dhv * (xn * gain), axis=0, keepdims=True)
        sum_ref[2:3, :] += jnp.sum(dhv * one_scale * xn, axis=0, keepdims=True)

    row = pl.BlockSpec((tr, d), lambda i: (i, 0))
    seg = pl.BlockSpec((None, 8, d), lambda i: (i // seg_tiles, 0, 0))
    in_specs = [row, row, pl.BlockSpec((tr, d), lambda i: (jnp.minimum(i, res_tiles - 1), 0)),
                pl.BlockSpec((1, d), lambda i: (0, 0)), seg]
    out_specs = [pl.BlockSpec((tr, d), lambda i: (jnp.minimum(i, out_tiles - 1), 0)), seg]
    out_shape = [jax.ShapeDtypeStruct((out_tiles * tr, d), F32), jax.ShapeDtypeStruct((mod.shape[0], 8, d), F32)]
    args = [xs, dh, dres, g, mod]
    if below is not None:
        in_specs += [row, seg]
        out_specs += [row, seg]
        out_shape += [jax.ShapeDtypeStruct((t, d), BF16), jax.ShapeDtypeStruct((below[1].shape[0], 8, d), F32)]
        args += list(below)
    return _pc(
        body, name=name, grid=(n_tiles,), in_specs=in_specs, out_specs=out_specs, out_shape=out_shape,
        compiler_params=_params("arbitrary"),
    )(*args)


def _loss_head(xs, target, g, yx, mod, tr):
    t, d = xs.shape

    def body(x_ref, t_ref, g_ref, yx_ref, gate_ref, loss_ref, dx_ref, dg_ref, dyx_ref, gsum_ref):
        i = pl.program_id(0)
        x = x_ref[...]
        r = lax.rsqrt(jnp.mean(x * x, axis=-1, keepdims=True) + RMS_EPS)
        xn = x * r
        gain = g_ref[...]
        err = xn * gain - t_ref[...]
        dy = err * (1.0 / d)
        dxn = dy * gain
        dx = r * (dxn - xn * jnp.mean(dxn * xn, axis=-1, keepdims=True))
        dx_ref[...] = dx
        _resid_grad(dx, i, t // tr, yx_ref, gate_ref, dyx_ref, gsum_ref)

        @pl.when(i == 0)
        def _():
            loss_ref[...] = jnp.zeros_like(loss_ref)
            dg_ref[...] = jnp.zeros_like(dg_ref)

        loss_ref[...] += 0.5 * jnp.sum(jnp.mean(err * err, axis=-1, keepdims=True))
        dg_ref[0:1, :] += jnp.sum(dy * xn, axis=0, keepdims=True)

    row = pl.BlockSpec((tr, d), lambda i: (i, 0))
    seg = pl.BlockSpec((None, 8, d), lambda i: (0, 0, 0))
    return _pc(
        body, name="loss_head", grid=(t // tr,),
        in_specs=[row, row, pl.BlockSpec((1, d), lambda i: (0, 0)), row, seg],
        out_specs=[pl.BlockSpec((8, LANES), lambda i: (0, 0)), row, pl.BlockSpec((8, d), lambda i: (0, 0)), row, seg],
        out_shape=[jax.ShapeDtypeStruct((8, LANES), F32), jax.ShapeDtypeStruct((t, d), F32),
                   jax.ShapeDtypeStruct((8, d), F32), jax.ShapeDtypeStruct((t, d), BF16),
                   jax.ShapeDtypeStruct((1, 8, d), F32)],
        compiler_params=_params("arbitrary"),
    )(xs, target, g, yx, mod)


def _proj_in(h, w, layer, width, name, blocks=None, dtype=F32):
    t, d = h.shape
    n8 = w.shape[-1]
    first, count = blocks if blocks is not None else (0, N_DEV)
    per_part = width // n8
    assert first % per_part == 0 and count % per_part == 0
    tm = _tile(t, 1152)

    def body(a_ref, b_ref, o_ref):
        a = a_ref[...]
        for s in range(per_part):
            o_ref[:, s * n8:(s + 1) * n8] = _dot(a, b_ref[s]).astype(dtype)

    return _pc(
        body, name=name, grid=(t // tm, count // per_part),
        in_specs=[pl.BlockSpec((tm, d), lambda i, j: (i, 0)),
                  pl.BlockSpec((per_part, None, d, n8), lambda i, j: (first // per_part + j, layer, 0, 0))],
        out_specs=pl.BlockSpec((None, tm, width), lambda i, j: (j, i, 0)),
        out_shape=jax.ShapeDtypeStruct((count // per_part, t, width), dtype),
        compiler_params=_params("parallel", "parallel"),
    )(h, w)


def _proj_in_split(h, w, layer, width, name, narrow):
    t, d = h.shape
    n8 = w.shape[-1]
    per_part = width // n8
    parts = N_DEV // per_part
    tm = _tile(t, 1152)

    def body(a_ref, b_ref, lo_ref, hi_ref):
        j = pl.program_id(1)
        a = a_ref[...]
        cols = [_dot(a, b_ref[s]) for s in range(per_part)]

        @pl.when(j < narrow)
        def _():
            for s in range(per_part):
                lo_ref[:, s * n8:(s + 1) * n8] = cols[s].astype(BF16)

        @pl.when(j >= narrow)
        def _():
            for s in range(per_part):
                hi_ref[:, s * n8:(s + 1) * n8] = cols[s]

    return _pc(
        body, name=name, grid=(t // tm, parts),
        in_specs=[pl.BlockSpec((tm, d), lambda i, j: (i, 0)),
                  pl.BlockSpec((per_part, None, d, n8), lambda i, j: (j, layer, 0, 0))],
        out_specs=[pl.BlockSpec((None, tm, width), lambda i, j: (jnp.minimum(j, narrow - 1), i, 0)),
                   pl.BlockSpec((None, tm, width), lambda i, j: (jnp.maximum(j - narrow, 0), i, 0))],
        out_shape=[jax.ShapeDtypeStruct((narrow, t, width), BF16), jax.ShapeDtypeStruct((parts - narrow, t, width), F32)],
        compiler_params=_params("parallel", "arbitrary"),
    )(h, w)


def _proj_out(z, w, res, mod, tm, seg_tiles, name, nxt=None):
    t, k = z.shape
    d = w.shape[1]

    def body(z_ref, w_ref, res_ref, mod_ref, *rest):
        yx = _dot(z_ref[...], w_ref[...])
        x = res_ref[...] + mod_ref[2:3, :] * yx
        if nxt is None:
            yx_ref, x_ref = rest
        else:
            g_ref, nmod_ref, yx_ref, x_ref, h_ref = rest
            r = lax.rsqrt(jnp.mean(x * x, axis=-1, keepdims=True) + RMS_EPS)
            h_ref[...] = (((x * r) * g_ref[...]) * (1.0 + nmod_ref[1:2, :]) + nmod_ref[0:1, :]).astype(BF16)
        yx_ref[...] = yx
        x_ref[...] = x

    tile = pl.BlockSpec((tm, d), lambda i: (i, 0))
    seg = pl.BlockSpec((None, 8, d), lambda i: (i // seg_tiles, 0, 0))
    in_specs = [pl.BlockSpec((tm, k), lambda i: (i, 0)), pl.BlockSpec((k, d), lambda i: (0, 0)), tile, seg]
    out_specs = [tile, tile]
    out_shape = [jax.ShapeDtypeStruct((t, d), F32), jax.ShapeDtypeStruct((t, d), F32)]
    args = [z, w, res, mod]
    if nxt is not None:
        in_specs += [pl.BlockSpec((1, d), lambda i: (0, 0)), seg]
        out_specs.append(tile)
        out_shape.append(jax.ShapeDtypeStruct((t, d), BF16))
        args += list(nxt)
    return _pc(
        body, name=name, grid=(t // tm,), in_specs=in_specs, out_specs=out_specs, out_shape=out_shape,
        compiler_params=_params("parallel"),
    )(*args)


def _proj_out_bwd(dyx, z, w, name):
    t, d = dyx.shape
    width = w.shape[0]
    tm = _tile(t, 768)
    nk = t // tm

    def body(a_ref, z_ref, w_ref, dz_ref, dw_ref, acc_ref):
        k = pl.program_id(0)

        @pl.when(k == 0)
        def _():
            acc_ref[...] = jnp.zeros_like(acc_ref)

        a = a_ref[...]
        dz_ref[...] = _dot(a, w_ref[...], 1, 1)
        acc_ref[...] += _dot(z_ref[...].T, a)

        @pl.when(k == nk - 1)
        def _():
            dw_ref[...] = acc_ref[...].astype(BF16)

    return _pc(
        body, name=name, grid=(nk,),
        in_specs=[pl.BlockSpec((tm, d), lambda k: (k, 0)), pl.BlockSpec((tm, width), lambda k: (k, 0)),
                  pl.BlockSpec((width, d), lambda k: (0, 0))],
        out_specs=[pl.BlockSpec((tm, width), lambda k: (k, 0)), pl.BlockSpec((width, d), lambda k: (0, 0))],
        out_shape=[jax.ShapeDtypeStruct((t, width), F32), jax.ShapeDtypeStruct((width, d), BF16)],
        scratch_shapes=[pltpu.VMEM((width, d), F32)],
        compiler_params=_params("arbitrary"),
    )(dyx, z, w)


def _proj_in_dh(dpre, w, layer, name, after=None):
    parts, t, width = dpre.shape
    d, n8 = w.shape[-2:]
    per_part = width // n8
    tm, tn = _tile(t, 768), _tile(d, 512)

    def body(a_ref, w_ref, *rest):
        o_ref = rest[-1]
        acc = None
        for p in range(parts):
            for s in range(per_part):
                term = _dot(a_ref[p, :, s * n8:(s + 1) * n8], w_ref[p * per_part + s], 1, 1)
                acc = term if acc is None else acc + term
        o_ref[...] = acc

    extra = [] if after is None else [after]
    return _pc(
        body, name=name, grid=(t // tm, d // tn),
        in_specs=[pl.BlockSpec((parts, tm, width), lambda i, j: (0, i, 0)),
                  pl.BlockSpec((N_DEV, None, tn, n8), lambda i, j: (0, layer, j, 0))] + [ANY] * len(extra),
        out_specs=pl.BlockSpec((tm, tn), lambda i, j: (i, j)),
        out_shape=jax.ShapeDtypeStruct((t, d), F32),
        compiler_params=_params("parallel", "parallel"),
    )(dpre, w, *extra)


def _transposed(a_ref):
    return a_ref[...].T


def _grad_w_in(h, dpre, n8, name):
    t, d = h.shape
    parts, _, width = dpre.shape
    per_part = width // n8
    tm, tk = _tile(d, 512), _tile(t, 1152)
    nk = t // tk

    def body(a_ref, b_ref, o_ref, acc_ref):
        k = pl.program_id(1)

        @pl.when(k == 0)
        def _():
            acc_ref[...] = jnp.zeros_like(acc_ref)

        at = _transposed(a_ref)
        for p in range(parts):
            r = _dot(at, b_ref[p])
            for s in range(per_part):
                acc_ref[p * per_part + s] += r[:, s * n8:(s + 1) * n8]

        @pl.when(k == nk - 1)
        def _():
            o_ref[...] = acc_ref[...].astype(BF16)

    return _pc(
        body, name=name, grid=(d // tm, nk),
        in_specs=[pl.BlockSpec((tk, tm), lambda i, k: (k, i)), pl.BlockSpec((parts, tk, width), lambda i, k: (0, k, 0))],
        out_specs=pl.BlockSpec((parts * per_part, tm, n8), lambda i, k: (0, i, 0)),
        out_shape=jax.ShapeDtypeStruct((parts * per_part, d, n8), BF16),
        scratch_shapes=[pltpu.VMEM((parts * per_part, tm, n8), F32)],
        compiler_params=_params("parallel", "arbitrary"),
    )(h, dpre)


def _shift(v, k):
    n = v.shape[0]
    return pltpu.roll(v, k % n, 0)


def _window_sum(v, win):
    s = v + _shift(v, 1)
    step = 1
    while 2 * step < win:
        s = _shift(s, step) + _shift(s, -step)
        step *= 2
    return s


def _window_count(base, seg_len, win, shape):
    t = base + lax.broadcasted_iota(jnp.int32, shape, 0)
    hi = jnp.minimum(t + win // 2, seg_len)
    lo = jnp.maximum(t - win // 2, 0)
    return (hi - lo).astype(F32)


def _pad_offsets(segs):
    return [HALO * (s + 1) + st for s, (st, _) in enumerate(segs)]


def _zero_halos(pad_ref, segs):
    zeros = jnp.zeros((HALO, pad_ref.shape[1]), pad_ref.dtype)
    for off, (_, ln) in zip(_pad_offsets(segs), segs):
        pad_ref[off - HALO:off, :] = zeros
        pad_ref[off + ln:off + ln + HALO, :] = zeros


def _for_chunks(segs, fn):
    offs = _pad_offsets(segs)
    for s, (st, ln) in enumerate(segs):
        def step(ci, carry, s=s, st=st, ln=ln):
            fn(s, st, ln, offs[s], pl.multiple_of(ci * CHUNK, CHUNK))
            return carry
        lax.fori_loop(0, ln // CHUNK, step, 0)


def _pool_fwd(pre, w_grp, scale, segs, name):
    _, t, width = pre.shape
    grp = width // len(POOL_WINDOWS)
    padded = t + HALO * (len(segs) + 1)

    def group(win, pre_ref, w_ref, sc_ref, z_ref, diff_ref, pad_ref):
        _zero_halos(pad_ref, segs)

        def fill(s, st, ln, off, b):
            pad_ref[pl.ds(off + b, CHUNK), :] = pre_ref[0, pl.ds(st + b, CHUNK), :]

        _for_chunks(segs, fill)

        def mix(s, st, ln, off, b):
            ext = pad_ref[pl.ds(off - HALO + b, CHUNK + 2 * HALO), :]
            total = _window_sum(ext, win)[HALO:HALO + CHUNK]
            u = pre_ref[0, pl.ds(st + b, CHUNK), :]
            diff = (total / _window_count(b, ln, win, u.shape) - u).astype(BF16)
            mixed = _dot(diff, w_ref[...])
            gate = _silu(pre_ref[1, pl.ds(st + b, CHUNK), :])
            z_ref[pl.ds(st + b, CHUNK), :] = (mixed * sc_ref[...] * gate).astype(BF16)
            diff_ref[pl.ds(st + b, CHUNK), :] = diff

        _for_chunks(segs, mix)

    def body(pre_ref, w_ref, sc_ref, z_ref, diff_ref, pad_ref):
        gi = pl.program_id(0)
        for widx, win in enumerate(POOL_WINDOWS):
            @pl.when(gi == widx)
            def _(win=win):
                group(win, pre_ref, w_ref, sc_ref, z_ref, diff_ref, pad_ref)

    col = pl.BlockSpec((t, grp), lambda g: (0, g))
    return _pc(
        body, name=name, grid=(len(POOL_WINDOWS),),
        in_specs=[pl.BlockSpec((2, t, grp), lambda g: (0, 0, g)), pl.BlockSpec((None, grp, grp), lambda g: (g, 0, 0)),
                  pl.BlockSpec((1, grp), lambda g: (0, g))],
        out_specs=[col, col],
        out_shape=[jax.ShapeDtypeStruct((t, width), BF16), jax.ShapeDtypeStruct((t, width), BF16)],
        scratch_shapes=[pltpu.VMEM((padded, grp), F32)],
        compiler_params=_params("parallel"),
    )(pre, w_grp, scale)


def _pool_bwd(dz, diff, pre, w_grp, scale, segs, name):
    _, t, width = pre.shape
    grp = width // len(POOL_WINDOWS)
    padded = t + HALO * (len(segs) + 1)

    def group(win, dz_ref, diff_ref, pre_ref, w_ref, sc_ref, dpre_ref, dw_ref, dsc_ref, pad_ref, dd_ref):
        _zero_halos(pad_ref, segs)
        dw_ref[...] = jnp.zeros_like(dw_ref)
        dsc_ref[...] = jnp.zeros_like(dsc_ref)

        def first(s, st, ln, off, b):
            rows = pl.ds(st + b, CHUNK)
            diff_v = diff_ref[rows, :]
            mixed = _dot(diff_v, w_ref[...])
            g = pre_ref[1, rows, :]
            sg = _silu(g)
            dzv = dz_ref[rows, :]
            dmixed = (dzv * sc_ref[...] * sg).astype(BF16)
            dsc_ref[...] += jnp.sum(dzv * mixed * sg, axis=0, keepdims=True)
            dpre_ref[1, rows, :] = (dzv * mixed * sc_ref[...] * _dsilu(g)).astype(BF16)
            ddiff = _dot(dmixed, w_ref[...], 1, 1)
            dw_ref[...] += _dot(diff_v, dmixed, 0, 0)
            dd_ref[rows, :] = ddiff
            pad_ref[pl.ds(off + b, CHUNK), :] = ddiff / _window_count(b, ln, win, ddiff.shape)

        _for_chunks(segs, first)

        def second(s, st, ln, off, b):
            rows = pl.ds(st + b, CHUNK)
            ext = pad_ref[pl.ds(off - HALO + b, CHUNK + 2 * HALO), :]
            total = _shift(_window_sum(ext, win), -1)[HALO:HALO + CHUNK]
            dpre_ref[0, rows, :] = (total - dd_ref[rows, :]).astype(BF16)

        _for_chunks(segs, second)

    def body(dz_ref, diff_ref, pre_ref, w_ref, sc_ref, dpre_ref, dw_ref, dsc_ref, pad_ref, dd_ref):
        gi = pl.program_id(0)
        for widx, win in enumerate(POOL_WINDOWS):
            @pl.when(gi == widx)
            def _(win=win):
                group(win, dz_ref, diff_ref, pre_ref, w_ref, sc_ref, dpre_ref, dw_ref, dsc_ref, pad_ref, dd_ref)

    col = pl.BlockSpec((t, grp), lambda g: (0, g))
    both = pl.BlockSpec((2, t, grp), lambda g: (0, 0, g))
    wspec = pl.BlockSpec((None, grp, grp), lambda g: (g, 0, 0))
    sspec = pl.BlockSpec((1, grp), lambda g: (0, g))
    return _pc(
        body, name=name, grid=(len(POOL_WINDOWS),),
        in_specs=[col, col, both, wspec, sspec],
        out_specs=[both, wspec, sspec],
        out_shape=[jax.ShapeDtypeStruct((2, t, width), BF16), jax.ShapeDtypeStruct((len(POOL_WINDOWS), grp, grp), F32),
                   jax.ShapeDtypeStruct((1, width), F32)],
        scratch_shapes=[pltpu.VMEM((padded, grp), F32), pltpu.VMEM((t, grp), F32)],
        compiler_params=_params("parallel"),
    )(dz, diff, pre, w_grp, scale)


def _conv_fwd(pre, dw, db, name):
    _, t, width = pre.shape
    cb = LANES
    segs = [(0, t)]

    def body(pre_ref, dw_ref, db_ref, z_ref, pad_ref):
        _zero_halos(pad_ref, segs)

        def fill(s, st, ln, off, b):
            rows = pl.ds(b, CHUNK)
            pad_ref[pl.ds(off + b, CHUNK), :] = pre_ref[1, rows, :] * pre_ref[2, rows, :]

        _for_chunks(segs, fill)

        def mix(s, st, ln, off, b):
            rows = pl.ds(b, CHUNK)
            ext = pad_ref[pl.ds(off - HALO + b, CHUNK + 2 * HALO), :]
            conv = (dw_ref[0:1, :] * _shift(ext, 1) + dw_ref[1:2, :] * ext + dw_ref[2:3, :] * _shift(ext, -1))
            conv = conv[HALO:HALO + CHUNK] + db_ref[...]
            y = pre_ref[0, rows, :] * conv
            z_ref[rows, :] = (y * _silu(pre_ref[3, rows, :])).astype(BF16)

        _for_chunks(segs, mix)

    return _pc(
        body, name=name, grid=(width // cb,),
        in_specs=[pl.BlockSpec((4, t, cb), lambda j: (0, 0, j)), pl.BlockSpec((8, cb), lambda j: (0, j)),
                  pl.BlockSpec((1, cb), lambda j: (0, j))],
        out_specs=pl.BlockSpec((t, cb), lambda j: (0, j)),
        out_shape=jax.ShapeDtypeStruct((t, width), BF16),
        scratch_shapes=[pltpu.VMEM((t + 2 * HALO, cb), F32)],
        compiler_params=_params("parallel"),
    )(pre, dw, db)


def _conv_bwd(dz, pre, dw, db, name):
    _, t, width = pre.shape
    cb = LANES
    segs = [(0, t)]

    def body(dz_ref, pre_ref, dw_ref, db_ref, dpre_ref, ddw_ref, ddb_ref, pad_a, pad_c):
        _zero_halos(pad_a, segs)
        _zero_halos(pad_c, segs)
        ddw_ref[...] = jnp.zeros_like(ddw_ref)
        ddb_ref[...] = jnp.zeros_like(ddb_ref)

        def fill(s, st, ln, off, b):
            rows = pl.ds(b, CHUNK)
            pad_a[pl.ds(off + b, CHUNK), :] = pre_ref[1, rows, :] * pre_ref[2, rows, :]

        _for_chunks(segs, fill)

        def first(s, st, ln, off, b):
            rows = pl.ds(b, CHUNK)
            ext = pad_a[pl.ds(off - HALO + b, CHUNK + 2 * HALO), :]
            prev, nxt = _shift(ext, 1)[HALO:HALO + CHUNK], _shift(ext, -1)[HALO:HALO + CHUNK]
            here = ext[HALO:HALO + CHUNK]
            conv = dw_ref[0:1, :] * prev + dw_ref[1:2, :] * here + dw_ref[2:3, :] * nxt + db_ref[...]
            bg, g = pre_ref[0, rows, :], pre_ref[3, rows, :]
            dzv = dz_ref[rows, :]
            dy = dzv * _silu(g)
            dpre_ref[3, rows, :] = (dzv * (bg * conv) * _dsilu(g)).astype(BF16)
            dpre_ref[0, rows, :] = (dy * conv).astype(BF16)
            dconv = dy * bg
            pad_c[pl.ds(off + b, CHUNK), :] = dconv
            ddw_ref[0:1, :] += jnp.sum(dconv * prev, axis=0, keepdims=True)
            ddw_ref[1:2, :] += jnp.sum(dconv * here, axis=0, keepdims=True)
            ddw_ref[2:3, :] += jnp.sum(dconv * nxt, axis=0, keepdims=True)
            ddb_ref[0:1, :] += jnp.sum(dconv, axis=0, keepdims=True)

        _for_chunks(segs, first)

        def second(s, st, ln, off, b):
            rows = pl.ds(b, CHUNK)
            ext = pad_c[pl.ds(off - HALO + b, CHUNK + 2 * HALO), :]
            da = (dw_ref[0:1, :] * _shift(ext, -1) + dw_ref[1:2, :] * ext + dw_ref[2:3, :] * _shift(ext, 1))
            da = da[HALO:HALO + CHUNK]
            dpre_ref[1, rows, :] = (da * pre_ref[2, rows, :]).astype(BF16)
            dpre_ref[2, rows, :] = (da * pre_ref[1, rows, :]).astype(BF16)

        _for_chunks(segs, second)

    quad = pl.BlockSpec((4, t, cb), lambda j: (0, 0, j))
    rows8 = pl.BlockSpec((8, cb), lambda j: (0, j))
    return _pc(
        body, name=name, grid=(width // cb,),
        in_specs=[pl.BlockSpec((t, cb), lambda j: (0, j)), quad, rows8, pl.BlockSpec((1, cb), lambda j: (0, j))],
        out_specs=[quad, rows8, rows8],
        out_shape=[jax.ShapeDtypeStruct((4, t, width), BF16), jax.ShapeDtypeStruct((8, width), F32),
                   jax.ShapeDtypeStruct((8, width), F32)],
        scratch_shapes=[pltpu.VMEM((t + 2 * HALO, cb), F32), pltpu.VMEM((t + 2 * HALO, cb), F32)],
        compiler_params=_params("parallel"),
    )(dz, pre, dw, db)


PAIR_TILES = 2 * WIN_ROWS - 2


def _pair_geometry():
    lane = lax.broadcasted_iota(jnp.int32, (GRID_W, LANES), 1)
    qcol = lax.broadcasted_iota(jnp.int32, (GRID_W, LANES), 0)
    low = lane < GRID_W
    kcol = jnp.where(low, lane, lane - GRID_W)
    start = jnp.clip(qcol - WIN_COLS // 2, 0, GRID_W - WIN_COLS)
    inside = (kcol >= start) & (kcol < start + WIN_COLS)
    return low, inside


def _bias_tiles(rpb_ref, rows_ref, tiles_ref, inside):
    for h in range(2):
        rows = rpb_ref[h]
        rows_ref[h] = (pltpu.roll(rows, LANES - (WIN_COLS - 1), 1)
                       + pltpu.roll(pltpu.roll(rows, GRID_W - (WIN_COLS - 1), 1), 2 * WIN_ROWS - 1, 0))
        for t in range(PAIR_TILES):
            both = jnp.broadcast_to(rows_ref[h, t:t + 1, :], (GRID_W, LANES))
            tiles_ref[h, t] = jnp.where(inside, pltpu.roll(both, 0, 1, stride=1, stride_axis=0), MASKED)


def _bias_tiles_grad(dtiles_ref, drpb_ref):
    sub = 8
    qcol = lax.broadcasted_iota(jnp.int32, (PAIR_TILES * sub, LANES), 0) & (sub - 1)
    lane = lax.broadcasted_iota(jnp.int32, (1, LANES), 1)
    zero = jnp.zeros((1, LANES), F32)
    for h in range(2):
        v = dtiles_ref[h]
        rows = GRID_W
        while rows > sub:
            rows //= 2
            upper = pltpu.roll(v[:, rows:].reshape(PAIR_TILES * rows, LANES), LANES - rows, 1)
            v = v[:, :rows] + upper.reshape(PAIR_TILES, rows, LANES)
        v = v.reshape(PAIR_TILES * sub, LANES)
        for bit in range(3):
            v = jnp.where((qcol >> bit) & 1 == 1, pltpu.roll(v, LANES - (1 << bit), 1), v)
        v = pltpu.roll(v, WIN_COLS - 1, 1)
        sums = [jnp.sum(v[t * sub:(t + 1) * sub], axis=0, keepdims=True) for t in range(PAIR_TILES)]
        for r in range(2 * WIN_ROWS):
            here = sums[r] if r < PAIR_TILES else zero
            prev = pltpu.roll(sums[r - 1], GRID_W, 1) if 1 <= r <= PAIR_TILES else zero
            drpb_ref[h, r:r + 1, :] = jnp.where(lane < 2 * WIN_COLS - 1, here + prev, 0.0)


def _attn_rows(r, n_rows):
    first = jnp.clip(r - WIN_ROWS // 2, 0, n_rows - WIN_ROWS)
    return first, first - r + WIN_ROWS - 1


def _softmax(s_loc, s_ctx):
    m = jnp.maximum(jnp.max(s_loc, axis=-1, keepdims=True), jnp.max(s_ctx, axis=-1, keepdims=True))
    e_loc, e_ctx = jnp.exp(s_loc - m), jnp.exp(s_ctx - m)
    inv = 1.0 / (jnp.sum(e_loc, axis=-1, keepdims=True) + jnp.sum(e_ctx, axis=-1, keepdims=True))
    return e_loc * inv, e_ctx * inv


def _pair_bias(tiles_ref, j):
    return jnp.concatenate(
        [jnp.concatenate([tiles_ref[h, j + 2 * m] for m in range(WIN_ROWS // 2)], axis=1) for h in range(2)], axis=0)


ROWS_PER_STEP = 8
ROWS_PER_STEP_BWD = 4


def _by_head(tile, low):
    zero = jnp.zeros_like(tile)
    return jnp.concatenate([jnp.where(low, tile, zero), jnp.where(low, zero, tile)], axis=0)


def _merge_heads(stacked, low):
    return jnp.where(low, stacked[:GRID_W], stacked[GRID_W:])


def _attn_items(step, n_rows, q_ref, low, per_step):
    items = []
    for u in range(per_step):
        r = step * per_step + u
        first, j = _attn_rows(r, n_rows)
        rows = pl.ds(pl.multiple_of(r * GRID_W, GRID_W), GRID_W)
        keys = pl.ds(pl.multiple_of(first * GRID_W, GRID_W), WIN_ROWS * GRID_W)
        q = (q_ref[rows, :].astype(F32) * HEAD_DIM ** -0.5).astype(BF16)
        items.append((rows, keys, j, _by_head(q, low)))
    return items


def _attn_fwd(qkv, gate, rpb, seq):
    _, t, width = qkv.shape
    n_rows = seq // GRID_W
    n_ctx = t - seq
    heads = 2 * width // LANES

    def body(q_ref, k_ref, v_ref, g_ref, rpb_ref, z_ref, o_ref, tiles_ref, rows_ref):
        low, inside = _pair_geometry()
        _bias_tiles(rpb_ref, rows_ref, tiles_ref, inside)
        ctx = pl.ds(seq, n_ctx)

        def step(i, carry):
            items = _attn_items(i, n_rows, q_ref, low, ROWS_PER_STEP)
            k_ctx, v_ctx = k_ref[ctx, :], v_ref[ctx, :]
            scores = [(_dot(q, k_ref[keys, :], 1, 1) + _pair_bias(tiles_ref, j), _dot(q, k_ctx, 1, 1))
                      for _, keys, j, q in items]
            probs = [_softmax(s_loc, s_ctx) for s_loc, s_ctx in scores]
            outs = [_dot(p_loc.astype(BF16), v_ref[keys, :]) + _dot(p_ctx.astype(BF16), v_ctx)
                    for (_, keys, _, _), (p_loc, p_ctx) in zip(items, probs)]
            for (rows, _, _, _), out in zip(items, outs):
                o = _merge_heads(out, low)
                o_ref[rows, :] = o
                z_ref[rows, :] = (o * _silu(g_ref[rows, :])).astype(BF16)
            return carry

        lax.fori_loop(0, n_rows // ROWS_PER_STEP, step, 0)

    def part(p):
        return pl.BlockSpec((None, t, LANES), lambda h: (p, 0, h))

    out = pl.BlockSpec((seq, LANES), lambda h: (0, h))
    return _pc(
        body, name="attn_fwd", grid=(width // LANES,),
        in_specs=[part(0), part(1), part(2), part(0), pl.BlockSpec((2, 2 * WIN_ROWS, LANES), lambda h: (h, 0, 0))],
        out_specs=[out, out, pl.BlockSpec((2, PAIR_TILES, GRID_W, LANES), lambda h: (h, 0, 0, 0))],
        out_shape=[jax.ShapeDtypeStruct((seq, width), BF16), jax.ShapeDtypeStruct((seq, width), F32),
                   jax.ShapeDtypeStruct((heads, PAIR_TILES, GRID_W, LANES), F32)],
        scratch_shapes=[pltpu.VMEM((2, 2 * WIN_ROWS, LANES), F32)],
        compiler_params=_params("parallel"),
    )(qkv, qkv, qkv, gate, rpb)


def _attn_bwd(qkv, gate, o, dz, tiles, seq):
    _, t, width = qkv.shape
    n_rows = seq // GRID_W
    n_ctx = t - seq
    heads = 2 * width // LANES

    def body(q_ref, k_ref, v_ref, g_ref, o_ref, dz_ref, tiles_ref, dpre_ref, drpb_ref, dtiles_ref, dk_ref, dv_ref):
        low, _ = _pair_geometry()
        dtiles_ref[...] = jnp.zeros_like(dtiles_ref)
        dk_ref[...] = jnp.zeros_like(dk_ref)
        dv_ref[...] = jnp.zeros_like(dv_ref)
        ctx = pl.ds(seq, n_ctx)

        def step(i, carry):
            items = _attn_items(i, n_rows, q_ref, low, ROWS_PER_STEP_BWD)
            k_ctx, v_ctx = k_ref[ctx, :], v_ref[ctx, :]
            d_outs = []
            for rows, _, _, _ in items:
                g = g_ref[rows, :]
                dzv = dz_ref[rows, :]
                dpre_ref[3, rows, :] = (dzv * o_ref[rows, :] * _dsilu(g)).astype(BF16)
                d_outs.append(_by_head((dzv * _silu(g)).astype(BF16), low))
            scores = [(_dot(q, k_ref[keys, :], 1, 1) + _pair_bias(tiles_ref, j), _dot(q, k_ctx, 1, 1))
                      for _, keys, j, q in items]
            dprobs = [(_dot(d_o, v_ref[keys, :], 1, 1), _dot(d_o, v_ctx, 1, 1))
                      for (_, keys, _, _), d_o in zip(items, d_outs)]
            probs = [_softmax(s_loc, s_ctx) for s_loc, s_ctx in scores]
            dscores = []
            for (p_loc, p_ctx), (dp_loc, dp_ctx) in zip(probs, dprobs):
                delta = (jnp.sum(p_loc * dp_loc, axis=-1, keepdims=True)
                         + jnp.sum(p_ctx * dp_ctx, axis=-1, keepdims=True))
                dscores.append((p_loc * (dp_loc - delta), p_ctx * (dp_ctx - delta)))
            dqs = [_dot(ds_loc.astype(BF16), k_ref[keys, :]) + _dot(ds_ctx.astype(BF16), k_ctx)
                   for (_, keys, _, _), (ds_loc, ds_ctx) in zip(items, dscores)]
            for (rows, _, _, _), dq in zip(items, dqs):
                dpre_ref[0, rows, :] = (_merge_heads(dq, low) * HEAD_DIM ** -0.5).astype(BF16)
            for (_, keys, j, q), d_o, (p_loc, p_ctx), (ds_loc, ds_ctx) in zip(items, d_outs, probs, dscores):
                dk_ref[keys, :] += _dot(ds_loc.astype(BF16), q, 0, 0)
                dk_ref[ctx, :] += _dot(ds_ctx.astype(BF16), q, 0, 0)
                dv_ref[keys, :] += _dot(p_loc.astype(BF16), d_o, 0, 0)
                dv_ref[ctx, :] += _dot(p_ctx.astype(BF16), d_o, 0, 0)
                for h in range(2):
                    for m in range(WIN_ROWS // 2):
                        dtiles_ref[h, j + 2 * m] += ds_loc[h * GRID_W:(h + 1) * GRID_W, m * LANES:(m + 1) * LANES]
            return carry

        lax.fori_loop(0, n_rows // ROWS_PER_STEP_BWD, step, 0)
        dpre_ref[1] = dk_ref[...].astype(BF16)
        dpre_ref[2] = dv_ref[...].astype(BF16)
        dpre_ref[0, ctx, :] = jnp.zeros((n_ctx, LANES), BF16)
        dpre_ref[3, ctx, :] = jnp.zeros((n_ctx, LANES), BF16)
        _bias_tiles_grad(dtiles_ref, drpb_ref)

    def part(p):
        return pl.BlockSpec((None, t, LANES), lambda h: (p, 0, h))

    lat = pl.BlockSpec((seq, LANES), lambda h: (0, h))
    rspec = pl.BlockSpec((2, 2 * WIN_ROWS, LANES), lambda h: (h, 0, 0))
    tspec = pl.BlockSpec((2, PAIR_TILES, GRID_W, LANES), lambda h: (h, 0, 0, 0))
    return _pc(
        body, name="attn_bwd", grid=(width // LANES,),
        in_specs=[part(0), part(1), part(2), part(0), lat, lat, tspec],
        out_specs=[pl.BlockSpec((4, t, LANES), lambda h: (0, 0, h)), rspec],
        out_shape=[jax.ShapeDtypeStruct((4, t, width), BF16), jax.ShapeDtypeStruct((heads, 2 * WIN_ROWS, LANES), F32)],
        scratch_shapes=[pltpu.VMEM((2, PAIR_TILES, GRID_W, LANES), F32),
                        pltpu.VMEM((t, LANES), F32), pltpu.VMEM((t, LANES), F32)],
        compiler_params=_params("parallel"),
    )(qkv, qkv, qkv, gate, o, dz, tiles)


def _adam_update(w, m, v, g):
    m2 = ADAM_B1 * m + (1.0 - ADAM_B1) * g
    v2 = ADAM_B2 * v + (1.0 - ADAM_B2) * (g * g)
    m_hat = m2 / (1.0 - ADAM_B1 ** ADAM_STEP)
    v_hat = v2 / (1.0 - ADAM_B2 ** ADAM_STEP)
    return -ADAM_LR * (m_hat / (jnp.sqrt(v_hat) + ADAM_EPS) + ADAM_WD * w), m2, v2


def _adamw(w, m, v, parts, name, after=None):
    rows, cols = w.shape
    tr = _tile(rows, max(8, ADAM_TILE_ELEMS // cols), 8)
    n_parts = len(parts)

    def body(*refs):
        w_ref, m_ref, v_ref = refs[:3]
        part_refs = refs[3:3 + n_parts]
        g_ref, d_ref, nm_ref, nv_ref = refs[-4:]
        g = part_refs[0][...].astype(F32)
        for p in part_refs[1:]:
            g = g + p[...].astype(F32)
        g_ref[...] = g
        d_ref[...], nm_ref[...], nv_ref[...] = _adam_update(w_ref[...], m_ref[...], v_ref[...], g)

    tile = pl.BlockSpec((tr, cols), lambda i: (i, 0))
    in_specs, args = [tile, tile, tile], [w, m, v]
    for p in parts:
        if isinstance(p, tuple):
            arr, k = p
            in_specs.append(pl.BlockSpec((None, tr, cols), lambda i, k=k: (k, i, 0)))
            args.append(arr)
        else:
            in_specs.append(tile)
            args.append(p)
    if after is not None:
        in_specs.append(ANY)
        args.append(after)
    shape = jax.ShapeDtypeStruct((rows, cols), F32)
    return _pc(
        body, name=name, grid=(rows // tr,), in_specs=in_specs, out_specs=[tile] * 4, out_shape=[shape] * 4,
        compiler_params=_params("parallel"),
    )(*args)


def _adamw_layers(w, m, v, landed, name):
    n_layers, rows, cols = w.shape
    tr = _tile(rows, max(8, ADAM_TILE_ELEMS // cols), 8)

    def body(*refs):
        w_ref, m_ref, v_ref = refs[:3]
        part_refs = refs[3:3 + n_layers * N_DEV]
        g_ref, d_ref, nm_ref, nv_ref = refs[3 + n_layers * N_DEV:]
        layer = pl.program_id(0)
        g = None
        for l in range(n_layers):
            s = part_refs[l * N_DEV][...].astype(F32)
            for p in part_refs[l * N_DEV + 1:(l + 1) * N_DEV]:
                s = s + p[...].astype(F32)
            g = s if g is None else jnp.where(layer == l, s, g)
        g_ref[...] = g
        d_ref[...], nm_ref[...], nv_ref[...] = _adam_update(w_ref[...], m_ref[...], v_ref[...], g)

    tile = pl.BlockSpec((None, tr, cols), lambda l, i: (l, i, 0))
    in_specs, args = [tile, tile, tile], [w, m, v]
    for l, arr in enumerate(landed):
        for k in range(N_DEV):
            in_specs.append(pl.BlockSpec((None, tr, cols), lambda ll, i, l=l, k=k: (k, jnp.where(ll == l, i, 0), 0)))
            args.append(arr)
    shape = jax.ShapeDtypeStruct(w.shape, F32)
    return _pc(
        body, name=name, grid=(n_layers, rows // tr), in_specs=in_specs, out_specs=[tile] * 4, out_shape=[shape] * 4,
        compiler_params=_params("arbitrary", "arbitrary"),
    )(*args)


def _adamw_small(states, grads):
    sources, makers = grads
    n, ns = len(states), len(sources)

    def body(*refs):
        src = refs[:ns]
        ins = refs[ns:ns + 3 * n]
        outs = refs[ns + 3 * n:]
        for k in range(n):
            w_ref, m_ref, v_ref = ins[3 * k:3 * k + 3]
            g = makers[k](*src)
            outs[4 * k][...] = g
            outs[4 * k + 1][...], outs[4 * k + 2][...], outs[4 * k + 3][...] = _adam_update(
                w_ref[...], m_ref[...], v_ref[...], g)

    flat = [a for s in states for a in s]
    vmem = pl.BlockSpec(memory_space=pltpu.VMEM)
    res = _pc(
        body, name="adamw_small",
        in_specs=[vmem] * (ns + 3 * n), out_specs=[vmem] * (4 * n),
        out_shape=[jax.ShapeDtypeStruct(s[0].shape, F32) for s in states for _ in range(4)],
        compiler_params=pltpu.CompilerParams(vmem_limit_bytes=VMEM_LIMIT),
    )(*sources, *flat)
    return [res[4 * k:4 * k + 4] for k in range(n)]


def _rows128(a):
    flat = a.reshape(-1)
    pad = (-flat.shape[0]) % LANES
    if pad:
        flat = jnp.concatenate([flat, jnp.zeros((pad,), flat.dtype)])
    return flat.reshape(-1, LANES)


def _pad_rows(a, mult=8):
    pad = (-a.shape[0]) % mult
    if pad:
        a = jnp.concatenate([a, jnp.zeros((pad,) + a.shape[1:], a.dtype)], axis=0)
    return a


def kernel(x, c, ctx, c_ctx, norm_g, ada_w, ada_b, pool_w_in, pool_w_grp, pool_scale, pool_w_out, na_w_in, na_rpb, na_w_out, conv_w_in, conv_dw, conv_db, conv_w_out, final_g, loss_target, m_c_ctx, m_norm_g, m_ada_w, m_ada_b, m_pool_w_in, m_pool_w_grp, m_pool_scale, m_pool_w_out, m_na_w_in, m_na_rpb, m_na_w_out, m_conv_w_in, m_conv_dw, m_conv_db, m_conv_w_out, m_final_g, v_c_ctx, v_norm_g, v_ada_w, v_ada_b, v_pool_w_in, v_pool_w_grp, v_pool_scale, v_pool_w_out, v_na_w_in, v_na_rpb, v_na_w_out, v_conv_w_in, v_conv_dw, v_conv_db, v_conv_w_out, v_final_g):
    xi, yi, ci = _my_place()
    me = 4 * xi + 2 * yi + ci
    seq, d = x.shape[1], x.shape[2]
    n_ctx = ctx.shape[1]
    width = d
    depth = norm_g.shape[0]
    nb = ada_w.shape[2]
    shard = width // N_DEV
    d_rows = d // LANES
    assert seq % CHUNK == 0 and n_ctx % CHUNK == 0 and (seq // GRID_W) % ROWS_PER_STEP == 0 and seq >= WIN_ROWS * GRID_W
    tr = math.gcd(math.gcd(seq, n_ctx), 256)
    x_tiles = seq // tr
    tr_lat = math.gcd(seq, 512)
    lat_tiles = seq // tr_lat

    n_pool = pool_scale.shape[0]
    n_grp = pool_w_grp.shape[1]
    grp = width // n_grp

    small_in = _pad_rows(jnp.concatenate([_rows128(c), pool_scale, conv_dw[0], conv_db], axis=0))
    got = _gather_small(small_in, "gather_inputs")
    r0 = d_rows
    c_all = got[:, :r0].reshape(N_DEV, d)
    scale_full = got[:, r0:r0 + n_pool].transpose(1, 0, 2).reshape(n_pool, width)
    r1 = r0 + n_pool
    taps_full = _pad_rows(got[:, r1:r1 + 3].transpose(1, 0, 2).reshape(3, width))
    bias_full = got[:, r1 + 3:r1 + 4].transpose(1, 0, 2).reshape(1, width)

    cond = jnp.concatenate([c_all, c_ctx[None], jnp.zeros((7, d), F32)], axis=0)
    bias_mine = lax.dynamic_slice(ada_b, (0, me * nb), (depth, nb))
    mod_mine = _mod_fwd(cond, ada_w, bias_mine)
    by_example = jnp.stack([mod_mine[:, :N_DEV].transpose(1, 0, 2),
                            jnp.broadcast_to(mod_mine[:, N_DEV][None], (N_DEV, depth, nb))], axis=2)
    mod_all = _gather_small(by_example.reshape(N_DEV, -1, LANES), "gather_mod", per_dest=True)
    mod_all = mod_all.reshape(N_DEV, depth, 2, nb).transpose(1, 2, 0, 3).reshape(depth, 2, 3, d)
    mod_all = jnp.pad(mod_all, ((0, 0), (0, 0), (0, 5), (0, 0)))
    mods = [mod_all[i] if i < 2 else mod_all[i, :1] for i in range(depth)]

    layer_weights = [[pool_w_in[0], pool_w_grp[0], pool_w_out[0]], [na_w_in[0], na_w_out[0]],
                     [conv_w_in[0], conv_w_out[0]], [pool_w_in[1], pool_w_grp[1], pool_w_out[1]]]
    slot = {(i, t): n for n, (i, t) in enumerate((i, t) for i, ws in enumerate(layer_weights) for t in range(len(ws)))}
    two_level = [(0, 0), (1, 0)]
    weights_sent = _exchange_start(
        [w.astype(BF16) for ws in layer_weights for w in ws], False, mod_all, "weights_start",
        peers=[CHIP_PEERS if key in two_level else ALL_PEERS for key in slot])
    token = weights_sent[-1]

    def landed_weight(i, t, after):
        return _exchange_wait(weights_sent, False, after, f"weights_wait{i}_{t}", which=[slot[i, t]])[0]

    def handed_on(i, after):
        half = [landed_weight(i, 0, after)]
        rest = _forward_start(half, after, f"weights_forward{i}")
        return _forward_wait(rest, rest[-1], f"weights_forward_wait{i}")[0][:, None]

    def as_in(w):
        return w[:, None]

    def as_grp(w):
        return w.transpose(1, 0, 2, 3).reshape(n_grp, grp, grp)

    def as_out(w):
        return w.reshape(width, d)

    both = [(0, seq), (seq, n_ctx)]
    latent = [(0, seq)]

    def grp_slots(g):
        return g.reshape(n_grp, N_DEV, grp // N_DEV, grp).transpose(1, 0, 2, 3).reshape(N_DEV, -1, grp).astype(BF16)

    def send_grads(i, grads):
        return _exchange_start(grads, True, jnp.zeros((8, LANES), F32), f"grads_start{i}")

    xs0 = jnp.concatenate([x[0], ctx[0]], axis=0)
    h0 = _norm_fwd(xs0, norm_g[0:1] + token[0, 0], mods[0], tr, x_tiles, "norm_fwd0")
    pool_in_w0 = handed_on(0, h0)
    pre0 = _proj_in(h0, pool_in_w0, 0, width, "proj_in0")
    pool_grp_w0 = as_grp(landed_weight(0, 1, pre0))
    z0, diff0 = _pool_fwd(pre0, pool_grp_w0, scale_full[0:1], both, "pool_fwd0")
    pool_out_w0 = as_out(landed_weight(0, 2, z0))
    yx0, xs1, h1 = _proj_out(z0, pool_out_w0, xs0, mods[0], tr, x_tiles, "proj_out0", nxt=(norm_g[1:2], mods[1]))

    na_in_w = handed_on(1, h1)
    qkv1, gpre1 = _proj_in_split(h1, na_in_w, 0, width, "proj_in1", 3)
    rpb_rows = jnp.pad(na_rpb[0], ((0, 0), (0, 2 * WIN_ROWS - na_rpb.shape[2]), (0, LANES - na_rpb.shape[3])))
    z1, o1, bias_tiles = _attn_fwd(qkv1, gpre1, rpb_rows, seq)
    na_out_w = as_out(landed_weight(1, 1, z1))
    yx1, x2, h2 = _proj_out(z1, na_out_w, xs1, mods[1], tr_lat, lat_tiles, "proj_out1", nxt=(norm_g[2:3], mods[2]))

    conv_in_w = as_in(landed_weight(2, 0, h2))
    pre2 = _proj_in(h2, conv_in_w, 0, width, "proj_in2")
    z2 = _conv_fwd(pre2, taps_full, bias_full, "conv_fwd")
    conv_out_w = as_out(landed_weight(2, 1, z2))
    yx2, x3, h3 = _proj_out(z2, conv_out_w, x2, mods[2], tr_lat, lat_tiles, "proj_out2", nxt=(norm_g[3:4], mods[3]))

    pool_in_w3 = as_in(landed_weight(3, 0, h3))
    pre3 = _proj_in(h3, pool_in_w3, 0, width, "proj_in3")
    pool_grp_w3 = as_grp(landed_weight(3, 1, pre3))
    z3, diff3 = _pool_fwd(pre3, pool_grp_w3, scale_full[1:2], latent, "pool_fwd3")
    pool_out_w3 = as_out(landed_weight(3, 2, z3))
    yx3, x4 = _proj_out(z3, pool_out_w3, x3, mods[3], tr_lat, lat_tiles, "proj_out3")

    loss_part, dx4, d_final, dyx3, gate3 = _loss_head(x4, loss_target[0], final_g[None], yx3, mods[3], tr_lat)

    dz3, g_pool_out1 = _proj_out_bwd(dyx3, z3, pool_out_w3, "proj_out_bwd3")
    dpre3, g_grp1, g_scale1 = _pool_bwd(dz3, diff3, pre3, pool_grp_w3, scale_full[1:2], latent, "pool_bwd3")
    dh3 = _proj_in_dh(dpre3, pool_in_w3, 0, "proj_in_dh3")
    g_pool_in1 = _grad_w_in(h3, dpre3, pool_w_in.shape[2], "grad_w_in3")
    sent3 = send_grads(3, [g_pool_in1, grp_slots(g_grp1), g_pool_out1.reshape(N_DEV, shard, d)])
    dx3, norm3, dyx2, gate2 = _norm_bwd(x3, dh3, dx4, norm_g[3:4] + sent3[-1][0, 0], mods[3], tr_lat, lat_tiles, "norm_bwd3",
                                        below=(yx2, mods[2]))

    dz2, g_conv_out = _proj_out_bwd(dyx2, z2, conv_out_w, "proj_out_bwd2")
    dpre2, g_taps, g_cbias = _conv_bwd(dz2, pre2, taps_full, bias_full, "conv_bwd")
    dh2 = _proj_in_dh(dpre2, conv_in_w, 0, "proj_in_dh2")
    g_conv_in = _grad_w_in(h2, dpre2, conv_w_in.shape[2], "grad_w_in2")
    sent2 = send_grads(2, [g_conv_in, g_conv_out.reshape(N_DEV, shard, d)])
    dx2, norm2, dyx1, gate1 = _norm_bwd(x2, dh2, dx3, norm_g[2:3] + sent2[-1][0, 0], mods[2], tr_lat, lat_tiles, "norm_bwd2",
                                        below=(yx1, mods[1][:1]))

    dz1, g_na_out = _proj_out_bwd(dyx1, z1, na_out_w, "proj_out_bwd1")
    dpre1, g_rpb = _attn_bwd(qkv1, gpre1, o1, dz1, bias_tiles, seq)
    g_rpb = g_rpb[:, :na_rpb.shape[2], :na_rpb.shape[3]]
    dh1 = _proj_in_dh(dpre1, na_in_w, 0, "proj_in_dh1")
    g_na_in = _grad_w_in(h1, dpre1, na_w_in.shape[2], "grad_w_in1")
    sent1 = send_grads(1, [g_na_in, g_na_out.reshape(N_DEV, shard, d)])
    dxs1, norm1, dyx0, gate0 = _norm_bwd(xs1, dh1, dx2, norm_g[1:2] + sent1[-1][0, 0], mods[1], tr, x_tiles, "norm_bwd1",
                                         res_tiles=x_tiles, below=(yx0, mods[0]))

    dz0, g_pool_out0 = _proj_out_bwd(dyx0, z0, pool_out_w0, "proj_out_bwd0")
    sent0a = _exchange_start([g_pool_out0.reshape(N_DEV, shard, d)], True, jnp.zeros((8, LANES), F32), "grads_start0a")
    dpre0, g_grp0, g_scale0 = _pool_bwd(dz0, diff0, pre0, pool_grp_w0, scale_full[0:1], both, "pool_bwd0")
    g_pool_in0 = _grad_w_in(h0, dpre0, pool_w_in.shape[2], "grad_w_in0")
    dh0 = _proj_in_dh(dpre0, pool_in_w0, 0, "proj_in_dh0", after=sent0a[-1])
    dx0, norm0 = _norm_bwd(xs0, dh0, dxs1, norm_g[0:1], mods[0], tr, x_tiles, "norm_bwd0", out_tiles=x_tiles)
    grad_x = dx0[None]

    norms, gates = [norm0, norm1, norm2, norm3], [gate0, gate1, gate2, gate3]
    zero_d = jnp.zeros((d,), F32)
    dm_rows = [jnp.concatenate([norms[i][0, 0], norms[i][0, 1], gates[i][0, 0]]) for i in range(depth)]
    dm_rows.append(jnp.concatenate([norm0[1, 0], norm0[1, 1], gate0[1, 0]]))
    dm_rows.append(jnp.concatenate([norm1[1, 0], norm1[1, 1], zero_d]))
    dm_local = jnp.stack(dm_rows + [jnp.zeros((3 * d,), F32)] * 2)
    g_norm_part = jnp.stack([norm0[0, 2] + norm0[1, 2], norm1[0, 2] + norm1[1, 2], norm2[0, 2], norm3[0, 2]])
    pieces = [_rows128(dm_local), _rows128(g_norm_part), _rows128(d_final[0]), _pad_rows(_rows128(g_rpb)), loss_part]
    marks = np.cumsum([0] + [p.shape[0] for p in pieces])
    by_owner = [a.reshape(-1, N_DEV, shard).transpose(1, 0, 2) for a in (g_scale0, g_scale1, g_taps[:3], g_cbias[0:1])]
    by_owner = jnp.concatenate(by_owner + [jnp.zeros((N_DEV, 8 - n_pool - 4, shard), F32)], axis=1)
    small_sent = _exchange_start([jnp.concatenate(pieces, axis=0), by_owner], [False, True], jnp.zeros((8, LANES), F32),
                                 "small_grads_start")
    sent0b = _exchange_start([g_pool_in0, grp_slots(g_grp0)], True, small_sent[-1], "grads_start0b")

    def big(parts, w, m, v, name):
        shape = w.shape
        view = (-1, shape[-1])
        parts = [(parts.reshape((N_DEV,) + w.reshape(view).shape), k) for k in range(N_DEV)]
        return [r.reshape(shape) for r in _adamw(w.reshape(view), m.reshape(view), v.reshape(view), parts, name)]

    in3, grp3, out3 = _exchange_wait(sent3, True, sent0b[-1], "grads_wait3")
    in2, out2 = _exchange_wait(sent2, True, sent0b[-1], "grads_wait2")
    in1, out1 = _exchange_wait(sent1, True, sent0b[-1], "grads_wait1")
    res = {}
    res["na_w_in"] = [r[None] for r in big(in1, na_w_in[0], m_na_w_in[0], v_na_w_in[0], "adamw_na_in")]
    res["na_w_out"] = [r[None] for r in big(out1, na_w_out[0], m_na_w_out[0], v_na_w_out[0], "adamw_na_out")]
    res["conv_w_in"] = [r[None] for r in big(in2, conv_w_in[0], m_conv_w_in[0], v_conv_w_in[0], "adamw_conv_in")]
    res["conv_w_out"] = [r[None] for r in big(out2, conv_w_out[0], m_conv_w_out[0], v_conv_w_out[0], "adamw_conv_out")]

    done = [res[n][0] for n in ("na_w_in", "na_w_out", "conv_w_in", "conv_w_out")]
    small_out, owned = _exchange_wait(small_sent, [False, True], done, "small_grads_wait")
    loss = jnp.sum(small_out[:, marks[4], 0])
    dm_all = small_out[:, :marks[1]].reshape(N_DEV, 8, 3 * d).transpose(1, 0, 2)
    dm_mine = lax.dynamic_slice(dm_all, (0, 0, me * nb), (8, N_DEV, nb))
    g_ada_w, g_ada_b, cctx_part, dsilu_cond = _mod_bwd(cond, ada_w, dm_all, dm_mine)
    cctx_sent = _exchange_start([_rows128(cctx_part[0])], False, jnp.zeros((8, LANES), F32), "cctx_start")
    res["ada_w"] = [r.reshape(ada_w.shape) for r in _adamw(
        ada_w.reshape(-1, nb), m_ada_w.reshape(-1, nb), v_ada_w.reshape(-1, nb),
        [g_ada_w.reshape(-1, nb)], "adamw_ada_w", after=cctx_sent[-1])]
    cctx_all, = _exchange_wait(cctx_sent, False, res["ada_w"][0], "cctx_wait")

    def summed(ref, lo, hi):
        g = ref[0, lo:hi, :]
        for k in range(1, N_DEV):
            g = g + ref[k, lo:hi, :]
        return g

    makers = [
        lambda so, ow, cc, ab, ds: summed(cc, 0, d_rows) * ds[...],
        lambda so, ow, cc, ab, ds: summed(so, marks[1], marks[2]),
        lambda so, ow, cc, ab, ds: ab[...],
        lambda so, ow, cc, ab, ds: summed(so, marks[2], marks[3]),
        lambda so, ow, cc, ab, ds: summed(so, marks[3], marks[4]),
        lambda so, ow, cc, ab, ds: summed(ow, 0, n_pool),
        lambda so, ow, cc, ab, ds: summed(ow, n_pool, n_pool + 3),
        lambda so, ow, cc, ab, ds: summed(ow, n_pool + 3, n_pool + 4),
    ]
    rpb_rows128 = lambda a: _pad_rows(_rows128(a))
    views = [_rows128] * 4 + [rpb_rows128] + [lambda a: a.reshape(-1, LANES)] * 3
    small = [(c_ctx, m_c_ctx, v_c_ctx), (norm_g, m_norm_g, v_norm_g), (ada_b, m_ada_b, v_ada_b),
             (final_g, m_final_g, v_final_g), (na_rpb, m_na_rpb, v_na_rpb), (pool_scale, m_pool_scale, v_pool_scale),
             (conv_dw, m_conv_dw, v_conv_dw), (conv_db, m_conv_db, v_conv_db)]
    states = [tuple(view(a) for a in triple) for view, triple in zip(views, small)]
    sources = (small_out, owned, cctx_all, _rows128(g_ada_b), _rows128(dsilu_cond[8]))
    small_res = _adamw_small(states, (sources, makers))
    names = ["c_ctx", "norm_g", "ada_b", "final_g", "na_rpb", "pool_scale", "conv_dw", "conv_db"]
    for name, (w, _, _), outs4 in zip(names, small, small_res):
        res[name] = [r.reshape(-1)[:w.size].reshape(w.shape) for r in outs4]

    out0, = _exchange_wait(sent0a, True, small_res[0][0], "grads_wait0a")
    in0, grp0 = _exchange_wait(sent0b, True, small_res[0][0], "grads_wait0b")
    def both_layers(first, second, w, m, v, name):
        view = (w.shape[0], -1, w.shape[-1])
        landed = [first.reshape((N_DEV,) + w.reshape(view).shape[1:]), second.reshape((N_DEV,) + w.reshape(view).shape[1:])]
        return [r.reshape(w.shape) for r in _adamw_layers(w.reshape(view), m.reshape(view), v.reshape(view), landed, name)]

    res["pool_w_in"] = both_layers(in0, in3, pool_w_in, m_pool_w_in, v_pool_w_in, "adamw_pool_in")
    res["pool_w_grp"] = both_layers(grp0, grp3, pool_w_grp, m_pool_w_grp, v_pool_w_grp, "adamw_pool_grp")
    res["pool_w_out"] = both_layers(out0, out3, pool_w_out, m_pool_w_out, v_pool_w_out, "adamw_pool_out")

    order = ["c_ctx", "norm_g", "ada_w", "ada_b", "pool_w_in", "pool_w_grp", "pool_scale", "pool_w_out", "na_w_in",
             "na_rpb", "na_w_out", "conv_w_in", "conv_dw", "conv_db", "conv_w_out", "final_g"]
    outs = [loss, grad_x]
    for j in range(4):
        outs += [res[n][j] for n in order]
    return tuple(outs)
```
